```python
import jax, jax.numpy as jnp
from jax import lax
import numpy as np

D_MODEL = 1024
BATCH = 8
SEQ = 8192
DEPTH = 1

N_Q_HEADS = 16
N_KV_HEADS = 2
HEAD_DIM = 64
Q_PER_KV = N_Q_HEADS // N_KV_HEADS
WINDOW = 128
BLOCK = 128
ATTN_WIDTH = N_Q_HEADS * HEAD_DIM
KV_WIDTH = N_KV_HEADS * HEAD_DIM
POOL_WINDOWS = (2, 4, 8, 16)
N_POOL_GROUPS = len(POOL_WINDOWS)
POOL_WIDTH = 512
POOL_GROUP = POOL_WIDTH // N_POOL_GROUPS
D_FF = 2816
NORM_EPS = 1e-6
IN_SPLITS = tuple(int(s) for s in np.cumsum([ATTN_WIDTH, KV_WIDTH, KV_WIDTH, POOL_WIDTH, D_MODEL]))
IN_WIDTH = ATTN_WIDTH + 2 * KV_WIDTH + POOL_WIDTH + 2 * D_MODEL

kernel_name = "hybrid_swa_sink_alibi_pool_macaron"


def alibi_slopes():
    h = np.arange(1, N_Q_HEADS + 1, dtype=np.float32)
    return jnp.asarray(2.0 ** (-8.0 * h / N_Q_HEADS), dtype=jnp.float32).reshape(N_KV_HEADS, Q_PER_KV)


def rmsnorm(x, g):
    xf = x.astype(jnp.float32)
    y = xf * lax.rsqrt(jnp.mean(xf * xf, axis=-1, keepdims=True) + NORM_EPS)
    return (y * g.astype(jnp.float32)).astype(x.dtype)


def swiglu(x, w_up, w_down):
    a, b = jnp.split(x @ w_up, 2, axis=-1)
    return (jax.nn.silu(a) * b) @ w_down


def sliding_window_attention(q, k, v, sinks):
    B, S, _ = q.shape
    nb = S // BLOCK
    q = q.reshape(B, nb, BLOCK, N_KV_HEADS, Q_PER_KV, HEAD_DIM)
    k = k.reshape(B, S, N_KV_HEADS, HEAD_DIM)
    v = v.reshape(B, S, N_KV_HEADS, HEAD_DIM)
    pad = jnp.zeros((B, BLOCK, N_KV_HEADS, HEAD_DIM), k.dtype)

    def band(t):
        cur = t.reshape(B, nb, BLOCK, N_KV_HEADS, HEAD_DIM)
        prev = jnp.concatenate([pad, t[:, :S - BLOCK]], axis=1).reshape(B, nb, BLOCK, N_KV_HEADS, HEAD_DIM)
        return jnp.concatenate([prev, cur], axis=2)

    kb, vb = band(k), band(v)
    scale = HEAD_DIM ** -0.5
    scores = jnp.einsum('bnqhgd,bnkhd->bnhgqk', q, kb, preferred_element_type=jnp.float32) * scale
    qi = jnp.arange(BLOCK)[:, None] + BLOCK
    kj = jnp.arange(2 * BLOCK)[None, :]
    dist = (qi - kj)
    blk = jnp.arange(nb)[:, None, None]
    valid = (dist >= 0)[None] & (dist < WINDOW)[None] & (blk * BLOCK - BLOCK + kj[None] >= 0)
    slopes = alibi_slopes()[:, :, None, None]
    scores = scores - slopes * dist.astype(jnp.float32)
    scores = jnp.where(valid[None, :, None, None], scores, -jnp.inf)
    sink = sinks.astype(jnp.float32).reshape(N_KV_HEADS, Q_PER_KV)[:, :, None, None]
    m = jnp.maximum(jnp.max(scores, axis=-1, keepdims=True), sink)
    p = jnp.exp(scores - m)
    probs = p / (jnp.sum(p, axis=-1, keepdims=True) + jnp.exp(sink - m))
    out = jnp.einsum('bnhgqk,bnkhd->bnqhgd', probs.astype(vb.dtype), vb)
    return out.reshape(B, S, ATTN_WIDTH)


def multiscale_pool(z, w_mix, scale):
    B, S, _ = z.shape
    zf = z.astype(jnp.float32)
    c = jnp.concatenate([jnp.zeros((B, 1, POOL_WIDTH), jnp.float32), jnp.cumsum(zf, axis=1)], axis=1)
    t = jnp.arange(S)
    outs = []
    for gi, w in enumerate(POOL_WINDOWS):
        cg = c[:, :, gi * POOL_GROUP:(gi + 1) * POOL_GROUP]
        prev = jnp.concatenate([jnp.zeros((B, w - 1, POOL_GROUP), jnp.float32), cg[:, :S - w + 1]], axis=1)
        cnt = jnp.minimum(t + 1, w).astype(jnp.float32)[None, :, None]
        outs.append((cg[:, 1:] - prev) / cnt)
    pooled = (jnp.concatenate(outs, axis=-1) - zf).astype(z.dtype)
    pooled = pooled.reshape(B, S, N_POOL_GROUPS, POOL_GROUP)
    mixed = jnp.einsum('bsgc,gcd->bsgd', pooled, w_mix).reshape(B, S, POOL_WIDTH)
    return mixed * scale


def _fwd_setup_inputs(seed: int = 0) -> dict:
    key = jax.random.key(seed)
    ks = jax.random.split(key, 20)
    f32 = jnp.float32

    def nrm(k, shape, fan_in):
        return jax.random.normal(k, shape, f32) * (fan_in ** -0.5)

    def gain(k, shape):
        return 1.0 + 0.02 * jax.random.normal(k, shape, f32)

    L = DEPTH
    return {
        "x": jax.random.normal(ks[0], (BATCH, SEQ, D_MODEL), f32),
        "ffn1_norm": gain(ks[1], (L, D_MODEL)),
        "ffn1_w_up": nrm(ks[2], (L, D_MODEL, 2 * D_FF), D_MODEL),
        "ffn1_w_down": nrm(ks[3], (L, D_FF, D_MODEL), D_FF),
        "mix_norm": gain(ks[4], (L, D_MODEL)),
        "w_in": nrm(ks[5], (L, D_MODEL, IN_WIDTH), D_MODEL),
        "sinks": jax.random.normal(ks[6], (L, N_Q_HEADS), f32),
        "w_attn_up": nrm(ks[7], (L, ATTN_WIDTH, D_MODEL), ATTN_WIDTH),
        "pool_w_mix": nrm(ks[8], (L, N_POOL_GROUPS, POOL_GROUP, POOL_GROUP), POOL_GROUP),
        "pool_scale": gain(ks[9], (L, POOL_WIDTH)),
        "w_pool_up": nrm(ks[10], (L, POOL_WIDTH, D_MODEL), POOL_WIDTH),
        "w_out": nrm(ks[11], (L, D_MODEL, D_MODEL), D_MODEL),
        "ffn2_norm": gain(ks[12], (L, D_MODEL)),
        "ffn2_w_up": nrm(ks[13], (L, D_MODEL, 2 * D_FF), D_MODEL),
        "ffn2_w_down": nrm(ks[14], (L, D_FF, D_MODEL), D_FF),
        "final_norm": gain(ks[15], (D_MODEL,)),
    }


def _fwd_reference(x, ffn1_norm, ffn1_w_up, ffn1_w_down, mix_norm, w_in, sinks, w_attn_up,
              pool_w_mix, pool_scale, w_pool_up, w_out, ffn2_norm, ffn2_w_up, ffn2_w_down,
              final_norm):
    h = x
    for l in range(DEPTH):
        h = h + 0.5 * swiglu(rmsnorm(h, ffn1_norm[l]), ffn1_w_up[l], ffn1_w_down[l])
        u = rmsnorm(h, mix_norm[l])
        q, k, v, z, g_attn, g_pool = jnp.split(u @ w_in[l], IN_SPLITS, axis=-1)
        a = sliding_window_attention(q, k, v, sinks[l]) @ w_attn_up[l]
        p = multiscale_pool(z, pool_w_mix[l], pool_scale[l]) @ w_pool_up[l]
        merged = jax.nn.sigmoid(g_attn) * a + jax.nn.sigmoid(g_pool) * p
        h = h + merged @ w_out[l]
        h = h + 0.5 * swiglu(rmsnorm(h, ffn2_norm[l]), ffn2_w_up[l], ffn2_w_down[l])
    return rmsnorm(h, final_norm)


import jax as _jax
import jax.numpy as _jnp

TWIN_FORMAT = 'train_step'
FWD_PARAMS = ['x', 'ffn1_norm', 'ffn1_w_up', 'ffn1_w_down', 'mix_norm', 'w_in', 'sinks', 'w_attn_up', 'pool_w_mix', 'pool_scale', 'w_pool_up', 'w_out', 'ffn2_norm', 'ffn2_w_up', 'ffn2_w_down', 'final_norm']
TWIN_WEIGHTS = ['ffn1_norm', 'ffn1_w_up', 'ffn1_w_down', 'mix_norm', 'w_in', 'sinks', 'w_attn_up', 'pool_w_mix', 'pool_scale', 'w_pool_up', 'w_out', 'ffn2_norm', 'ffn2_w_up', 'ffn2_w_down', 'final_norm']
TWIN_DIFF_INPUT = 'x'
TWIN_INPUTS = ['x', 'ffn1_norm', 'ffn1_w_up', 'ffn1_w_down', 'mix_norm', 'w_in', 'sinks', 'w_attn_up', 'pool_w_mix', 'pool_scale', 'w_pool_up', 'w_out', 'ffn2_norm', 'ffn2_w_up', 'ffn2_w_down', 'final_norm', 'loss_target', 'm_ffn1_norm', 'm_ffn1_w_up', 'm_ffn1_w_down', 'm_mix_norm', 'm_w_in', 'm_sinks', 'm_w_attn_up', 'm_pool_w_mix', 'm_pool_scale', 'm_w_pool_up', 'm_w_out', 'm_ffn2_norm', 'm_ffn2_w_up', 'm_ffn2_w_down', 'm_final_norm', 'v_ffn1_norm', 'v_ffn1_w_up', 'v_ffn1_w_down', 'v_mix_norm', 'v_w_in', 'v_sinks', 'v_w_attn_up', 'v_pool_w_mix', 'v_pool_scale', 'v_w_pool_up', 'v_w_out', 'v_ffn2_norm', 'v_ffn2_w_up', 'v_ffn2_w_down', 'v_final_norm']
TWIN_OUTPUTS = ['loss', 'grad_x', 'grad_ffn1_norm', 'grad_ffn1_w_up', 'grad_ffn1_w_down', 'grad_mix_norm', 'grad_w_in', 'grad_sinks', 'grad_w_attn_up', 'grad_pool_w_mix', 'grad_pool_scale', 'grad_w_pool_up', 'grad_w_out', 'grad_ffn2_norm', 'grad_ffn2_w_up', 'grad_ffn2_w_down', 'grad_final_norm', 'delta_ffn1_norm', 'delta_ffn1_w_up', 'delta_ffn1_w_down', 'delta_mix_norm', 'delta_w_in', 'delta_sinks', 'delta_w_attn_up', 'delta_pool_w_mix', 'delta_pool_scale', 'delta_w_pool_up', 'delta_w_out', 'delta_ffn2_norm', 'delta_ffn2_w_up', 'delta_ffn2_w_down', 'delta_final_norm', 'new_m_ffn1_norm', 'new_m_ffn1_w_up', 'new_m_ffn1_w_down', 'new_m_mix_norm', 'new_m_w_in', 'new_m_sinks', 'new_m_w_attn_up', 'new_m_pool_w_mix', 'new_m_pool_scale', 'new_m_w_pool_up', 'new_m_w_out', 'new_m_ffn2_norm', 'new_m_ffn2_w_up', 'new_m_ffn2_w_down', 'new_m_final_norm', 'new_v_ffn1_norm', 'new_v_ffn1_w_up', 'new_v_ffn1_w_down', 'new_v_mix_norm', 'new_v_w_in', 'new_v_sinks', 'new_v_w_attn_up', 'new_v_pool_w_mix', 'new_v_pool_scale', 'new_v_w_pool_up', 'new_v_w_out', 'new_v_ffn2_norm', 'new_v_ffn2_w_up', 'new_v_ffn2_w_down', 'new_v_final_norm']
TWIN_LEAF_KINDS = {'loss': 'loss', 'grad_x': 'grad_x', 'grad_ffn1_norm': 'grad_w', 'grad_ffn1_w_up': 'grad_w', 'grad_ffn1_w_down': 'grad_w', 'grad_mix_norm': 'grad_w', 'grad_w_in': 'grad_w', 'grad_sinks': 'grad_w', 'grad_w_attn_up': 'grad_w', 'grad_pool_w_mix': 'grad_w', 'grad_pool_scale': 'grad_w', 'grad_w_pool_up': 'grad_w', 'grad_w_out': 'grad_w', 'grad_ffn2_norm': 'grad_w', 'grad_ffn2_w_up': 'grad_w', 'grad_ffn2_w_down': 'grad_w', 'grad_final_norm': 'grad_w', 'delta_ffn1_norm': 'delta_w', 'delta_ffn1_w_up': 'delta_w', 'delta_ffn1_w_down': 'delta_w', 'delta_mix_norm': 'delta_w', 'delta_w_in': 'delta_w', 'delta_sinks': 'delta_w', 'delta_w_attn_up': 'delta_w', 'delta_pool_w_mix': 'delta_w', 'delta_pool_scale': 'delta_w', 'delta_w_pool_up': 'delta_w', 'delta_w_out': 'delta_w', 'delta_ffn2_norm': 'delta_w', 'delta_ffn2_w_up': 'delta_w', 'delta_ffn2_w_down': 'delta_w', 'delta_final_norm': 'delta_w', 'new_m_ffn1_norm': 'new_m', 'new_m_ffn1_w_up': 'new_m', 'new_m_ffn1_w_down': 'new_m', 'new_m_mix_norm': 'new_m', 'new_m_w_in': 'new_m', 'new_m_sinks': 'new_m', 'new_m_w_attn_up': 'new_m', 'new_m_pool_w_mix': 'new_m', 'new_m_pool_scale': 'new_m', 'new_m_w_pool_up': 'new_m', 'new_m_w_out': 'new_m', 'new_m_ffn2_norm': 'new_m', 'new_m_ffn2_w_up': 'new_m', 'new_m_ffn2_w_down': 'new_m', 'new_m_final_norm': 'new_m', 'new_v_ffn1_norm': 'new_v', 'new_v_ffn1_w_up': 'new_v', 'new_v_ffn1_w_down': 'new_v', 'new_v_mix_norm': 'new_v', 'new_v_w_in': 'new_v', 'new_v_sinks': 'new_v', 'new_v_w_attn_up': 'new_v', 'new_v_pool_w_mix': 'new_v', 'new_v_pool_scale': 'new_v', 'new_v_w_pool_up': 'new_v', 'new_v_w_out': 'new_v', 'new_v_ffn2_norm': 'new_v', 'new_v_ffn2_w_up': 'new_v', 'new_v_ffn2_w_down': 'new_v', 'new_v_final_norm': 'new_v'}


def _forward(args):
    return _fwd_reference(*[args[k] for k in FWD_PARAMS])


def _output_shape():
    def fwd():
        inp = _fwd_setup_inputs(0)
        return _fwd_reference(*[inp[k] for k in FWD_PARAMS])
    out = _jax.eval_shape(fwd)
    return out.shape, out.dtype

N_MICROBATCH = 1
ADAM_LR = 0.001
ADAM_B1 = 0.9
ADAM_B2 = 0.999
ADAM_EPS = 1e-08
ADAM_WD = 0.01
ADAM_STEP = 10
PER_EXAMPLE_BATCH_AXIS = {'x': 0, 'loss_target': 0}
SHARED_INPUTS = []
_WEIGHT_DTYPES = {'ffn1_norm': _jnp.float32, 'ffn1_w_up': _jnp.float32, 'ffn1_w_down': _jnp.float32, 'mix_norm': _jnp.float32, 'w_in': _jnp.float32, 'sinks': _jnp.float32, 'w_attn_up': _jnp.float32, 'pool_w_mix': _jnp.float32, 'pool_scale': _jnp.float32, 'w_pool_up': _jnp.float32, 'w_out': _jnp.float32, 'ffn2_norm': _jnp.float32, 'ffn2_w_up': _jnp.float32, 'ffn2_w_down': _jnp.float32, 'final_norm': _jnp.float32}
MOMENT_SCALE = {'ffn1_norm': 1.158416e-01, 'ffn1_w_up': 4.648178e-02, 'ffn1_w_down': 7.570325e-02, 'mix_norm': 1.271610e-01, 'w_in': 6.518703e-02, 'sinks': 5.216051e-02, 'w_attn_up': 3.621386e-02, 'pool_w_mix': 1.529441e-01, 'pool_scale': 1.537049e-01, 'w_pool_up': 1.078795e-01, 'w_out': 1.127262e-01, 'ffn2_norm': 9.376869e-02, 'ffn2_w_up': 3.845426e-02, 'ffn2_w_down': 6.298106e-02, 'final_norm': 6.400827e+01}


def _to_microbatches(a, axis):
    t = _jnp.moveaxis(a, axis, 0)
    t = t.reshape((N_MICROBATCH, t.shape[0] // N_MICROBATCH) + t.shape[1:])
    return _jnp.moveaxis(t, 1, axis + 1)


def setup_inputs(seed: int = 0) -> dict:
    inp = _fwd_setup_inputs(seed)
    key = _jax.random.fold_in(_jax.random.key(seed), 7919)
    shape, _ = _output_shape()
    out = dict(inp)
    out["loss_target"] = _jax.random.normal(_jax.random.fold_in(key, 0), shape, _jnp.float32)
    for i, name in enumerate(TWIN_WEIGHTS):
        w = inp[name].astype(_jnp.float32)
        if MOMENT_SCALE is None:
            s = _jnp.sqrt(_jnp.mean(_jnp.square(w)) + 1e-30)
        else:
            s = MOMENT_SCALE[name]
        km, kv = _jax.random.split(_jax.random.fold_in(key, i + 1))
        out[name] = w
        out["m_" + name] = s * _jax.random.normal(km, w.shape, _jnp.float32)
        out["v_" + name] = (s * s) * _jax.random.uniform(kv, w.shape, _jnp.float32, 0.5, 1.5)
    if N_MICROBATCH > 1:
        for name, axis in PER_EXAMPLE_BATCH_AXIS.items():
            out[name] = _to_microbatches(out[name], axis)
    return {'x': out['x'], 'ffn1_norm': out['ffn1_norm'], 'ffn1_w_up': out['ffn1_w_up'], 'ffn1_w_down': out['ffn1_w_down'], 'mix_norm': out['mix_norm'], 'w_in': out['w_in'], 'sinks': out['sinks'], 'w_attn_up': out['w_attn_up'], 'pool_w_mix': out['pool_w_mix'], 'pool_scale': out['pool_scale'], 'w_pool_up': out['w_pool_up'], 'w_out': out['w_out'], 'ffn2_norm': out['ffn2_norm'], 'ffn2_w_up': out['ffn2_w_up'], 'ffn2_w_down': out['ffn2_w_down'], 'final_norm': out['final_norm'], 'loss_target': out['loss_target'], 'm_ffn1_norm': out['m_ffn1_norm'], 'm_ffn1_w_up': out['m_ffn1_w_up'], 'm_ffn1_w_down': out['m_ffn1_w_down'], 'm_mix_norm': out['m_mix_norm'], 'm_w_in': out['m_w_in'], 'm_sinks': out['m_sinks'], 'm_w_attn_up': out['m_w_attn_up'], 'm_pool_w_mix': out['m_pool_w_mix'], 'm_pool_scale': out['m_pool_scale'], 'm_w_pool_up': out['m_w_pool_up'], 'm_w_out': out['m_w_out'], 'm_ffn2_norm': out['m_ffn2_norm'], 'm_ffn2_w_up': out['m_ffn2_w_up'], 'm_ffn2_w_down': out['m_ffn2_w_down'], 'm_final_norm': out['m_final_norm'], 'v_ffn1_norm': out['v_ffn1_norm'], 'v_ffn1_w_up': out['v_ffn1_w_up'], 'v_ffn1_w_down': out['v_ffn1_w_down'], 'v_mix_norm': out['v_mix_norm'], 'v_w_in': out['v_w_in'], 'v_sinks': out['v_sinks'], 'v_w_attn_up': out['v_w_attn_up'], 'v_pool_w_mix': out['v_pool_w_mix'], 'v_pool_scale': out['v_pool_scale'], 'v_w_pool_up': out['v_w_pool_up'], 'v_w_out': out['v_w_out'], 'v_ffn2_norm': out['v_ffn2_norm'], 'v_ffn2_w_up': out['v_ffn2_w_up'], 'v_ffn2_w_down': out['v_ffn2_w_down'], 'v_final_norm': out['v_final_norm']}


def _loss(weights, diff, rest, loss_target):
    with _jax.named_scope("forward"):
        args = {**rest, TWIN_DIFF_INPUT: diff, **{k: w.astype(_WEIGHT_DTYPES[k]) for k, w in weights.items()}}
        y = _forward(args)
    with _jax.named_scope("loss_head"):
        err = _jnp.square(y.astype(_jnp.float32) - loss_target)
        return 0.5 * _jnp.sum(_jnp.mean(err, axis=-1)) if err.ndim else 0.5 * err


def _adamw(w, g, m, v):
    m = ADAM_B1 * m + (1.0 - ADAM_B1) * g
    v = ADAM_B2 * v + (1.0 - ADAM_B2) * _jnp.square(g)
    m_hat = m / (1.0 - ADAM_B1 ** ADAM_STEP)
    v_hat = v / (1.0 - ADAM_B2 ** ADAM_STEP)
    delta = -ADAM_LR * (m_hat / (_jnp.sqrt(v_hat) + ADAM_EPS) + ADAM_WD * w)
    return delta, m, v


def reference(x, ffn1_norm, ffn1_w_up, ffn1_w_down, mix_norm, w_in, sinks, w_attn_up, pool_w_mix, pool_scale, w_pool_up, w_out, ffn2_norm, ffn2_w_up, ffn2_w_down, final_norm, loss_target, m_ffn1_norm, m_ffn1_w_up, m_ffn1_w_down, m_mix_norm, m_w_in, m_sinks, m_w_attn_up, m_pool_w_mix, m_pool_scale, m_w_pool_up, m_w_out, m_ffn2_norm, m_ffn2_w_up, m_ffn2_w_down, m_final_norm, v_ffn1_norm, v_ffn1_w_up, v_ffn1_w_down, v_mix_norm, v_w_in, v_sinks, v_w_attn_up, v_pool_w_mix, v_pool_scale, v_w_pool_up, v_w_out, v_ffn2_norm, v_ffn2_w_up, v_ffn2_w_down, v_final_norm):
    given = dict(x=x, ffn1_norm=ffn1_norm, ffn1_w_up=ffn1_w_up, ffn1_w_down=ffn1_w_down, mix_norm=mix_norm, w_in=w_in, sinks=sinks, w_attn_up=w_attn_up, pool_w_mix=pool_w_mix, pool_scale=pool_scale, w_pool_up=w_pool_up, w_out=w_out, ffn2_norm=ffn2_norm, ffn2_w_up=ffn2_w_up, ffn2_w_down=ffn2_w_down, final_norm=final_norm, loss_target=loss_target, m_ffn1_norm=m_ffn1_norm, m_ffn1_w_up=m_ffn1_w_up, m_ffn1_w_down=m_ffn1_w_down, m_mix_norm=m_mix_norm, m_w_in=m_w_in, m_sinks=m_sinks, m_w_attn_up=m_w_attn_up, m_pool_w_mix=m_pool_w_mix, m_pool_scale=m_pool_scale, m_w_pool_up=m_w_pool_up, m_w_out=m_w_out, m_ffn2_norm=m_ffn2_norm, m_ffn2_w_up=m_ffn2_w_up, m_ffn2_w_down=m_ffn2_w_down, m_final_norm=m_final_norm, v_ffn1_norm=v_ffn1_norm, v_ffn1_w_up=v_ffn1_w_up, v_ffn1_w_down=v_ffn1_w_down, v_mix_norm=v_mix_norm, v_w_in=v_w_in, v_sinks=v_sinks, v_w_attn_up=v_w_attn_up, v_pool_w_mix=v_pool_w_mix, v_pool_scale=v_pool_scale, v_w_pool_up=v_w_pool_up, v_w_out=v_w_out, v_ffn2_norm=v_ffn2_norm, v_ffn2_w_up=v_ffn2_w_up, v_ffn2_w_down=v_ffn2_w_down, v_final_norm=v_final_norm)
    weights = {n: given[n] for n in TWIN_WEIGHTS}
    shared = {n: given[n] for n in SHARED_INPUTS}
    per_example = {n: given[n] for n in ['x']}
    grad_fn = _jax.value_and_grad(_loss, argnums=(0, 1))

    def one_microbatch(ex, loss_target):
        ex = dict(ex)
        diff = ex.pop(TWIN_DIFF_INPUT)
        return grad_fn(weights, diff, {**shared, **ex}, loss_target)

    if N_MICROBATCH == 1:
        loss, (grad_w, grad_x) = one_microbatch(per_example, given["loss_target"])
    else:
        def body(carry, xs):
            loss_sum, grad_sum = carry
            l_k, (gw_k, gx_k) = one_microbatch(xs[0], xs[1])
            with _jax.named_scope("update"):
                return (loss_sum + l_k, _jax.tree.map(_jnp.add, grad_sum, gw_k)), gx_k

        init = (_jnp.zeros((), _jnp.float32), _jax.tree.map(_jnp.zeros_like, weights))
        (loss, grad_w), grad_x = _jax.lax.scan(body, init, (per_example, given["loss_target"]))
    with _jax.named_scope("update"):
        delta_w, new_m, new_v = {}, {}, {}
        for n in TWIN_WEIGHTS:
            delta_w[n], new_m[n], new_v[n] = _adamw(weights[n], grad_w[n], given["m_" + n], given["v_" + n])
    return (loss, grad_x, *[grad_w[n] for n in TWIN_WEIGHTS], *[delta_w[n] for n in TWIN_WEIGHTS],
            *[new_m[n] for n in TWIN_WEIGHTS], *[new_v[n] for n in TWIN_WEIGHTS])
```

```python
import functools

import numpy as np
import jax
import jax.numpy as jnp
from jax import lax
from jax.experimental import pallas as pl
from jax.experimental.pallas import tpu as pltpu

F32 = jnp.float32
BF16 = jnp.bfloat16
SDS = jax.ShapeDtypeStruct
MESH = pl.DeviceIdType.MESH

D = 1024
FF = 2816
NQ = 16
NKV = 2
HD = 64
GQ = NQ // NKV
AW = NQ * HD
KVW = NKV * HD
BLK = 128
PW = 512
PG = 128
POOL_WINDOWS = (2, 4, 8, 16)
HALO = 16
INW = AW + 2 * KVW + PW + 2 * D
C_KV = AW
C_Z = AW + 2 * KVW
C_G = C_Z + PW
EPS = 1e-6
FF_CHUNK = 256
FF_CHUNKS = tuple((c, FF_CHUNK) for c in range(0, FF, FF_CHUNK))
SLOPES = tuple(float(2.0 ** (-8.0 * h / NQ)) for h in range(1, NQ + 1))
SCALE = HD ** -0.5

LR, B1, B2, ADAM_EPS, WD, STEP = 0.001, 0.9, 0.999, 1e-08, 0.01, 10

VMEM_LIMIT = 56 * 1024 * 1024
PACK_W = 1024
N_CHIPS = 4

NT = (((1,), (1,)), ((), ()))
TN = (((0,), (0,)), ((), ()))


def _cp(sem=None, vmem=VMEM_LIMIT):
    return pltpu.CompilerParams(dimension_semantics=sem, vmem_limit_bytes=vmem)


def _const_spec(shape):
    nd = len(shape)
    return pl.BlockSpec(shape, lambda *_: (0,) * nd, pipeline_mode=pl.Buffered(1))


def _rstd(x):
    return lax.rsqrt(jnp.mean(x * x, axis=-1, keepdims=True) + EPS)


def _rms_bwd(dn, xhat, rstd, g):
    dxhat = dn * g
    return rstd * (dxhat - xhat * jnp.mean(dxhat * xhat, axis=-1, keepdims=True))


def _dot(a, b):
    return jnp.dot(a, b, preferred_element_type=F32)


def _dot_nt(a, b):
    return lax.dot_general(a, b, NT, preferred_element_type=F32)


def _dot_tn(a, b):
    return lax.dot_general(a, b, TN, preferred_element_type=F32)


def ffn_fwd(h, g, wup, wdn, name):
    T = h.shape[0]
    TM = 512

    def body(h_ref, g_ref, wup_ref, wdn_ref, out_ref, ab_ref, hid_ref):
        x = h_ref[...]
        n = (x * _rstd(x) * g_ref[...]).astype(BF16)
        for c0, w in FF_CHUNKS:
            a = _dot(n, wup_ref[:, c0:c0 + w])
            b = _dot(n, wup_ref[:, FF + c0:FF + c0 + w])
            ab_ref[:, c0:c0 + w] = a.astype(BF16)
            ab_ref[:, FF + c0:FF + c0 + w] = b.astype(BF16)
            hid_ref[:, c0:c0 + w] = (a * jax.nn.sigmoid(a) * b).astype(BF16)
        out_ref[...] = x + 0.5 * _dot(hid_ref[...], wdn_ref[...])

    return pl.pallas_call(
        body, name=name, grid=(T // TM,),
        in_specs=[pl.BlockSpec((TM, D), lambda i: (i, 0)), _const_spec((1, D)),
                  _const_spec((D, 2 * FF)), _const_spec((FF, D))],
        out_specs=[pl.BlockSpec((TM, D), lambda i: (i, 0)), pl.BlockSpec((TM, 2 * FF), lambda i: (i, 0))],
        out_shape=[SDS((T, D), F32), SDS((T, 2 * FF), BF16)],
        scratch_shapes=[pltpu.VMEM((TM, FF), BF16)],
        compiler_params=_cp(("arbitrary",)),
    )(h, g, wup, wdn)


def ffn_bwd_x(dh, h_in, ab, g, wup, wdn, name):
    T = dh.shape[0]
    TM = 256

    def body(dh_ref, h_ref, ab_ref, g_ref, wup_ref, wdn_ref, dhin_ref, dup_ref, hid_ref, n_ref, dg_ref):
        x = h_ref[...]
        g = g_ref[...]
        rstd = _rstd(x)
        xhat = x * rstd
        n_ref[...] = (xhat * g).astype(BF16)
        dh = dh_ref[...]
        dhh = (0.5 * dh).astype(BF16)
        for c0, w in FF_CHUNKS:
            dhid = _dot_nt(dhh, wdn_ref[c0:c0 + w, :])
            a = ab_ref[:, c0:c0 + w].astype(F32)
            b = ab_ref[:, FF + c0:FF + c0 + w].astype(F32)
            sig = jax.nn.sigmoid(a)
            s = a * sig
            hid_ref[:, c0:c0 + w] = (s * b).astype(BF16)
            dup_ref[:, c0:c0 + w] = (dhid * b * (sig * (1.0 + a * (1.0 - sig)))).astype(BF16)
            dup_ref[:, FF + c0:FF + c0 + w] = (dhid * s).astype(BF16)
        dn = _dot_nt(dup_ref[...], wup_ref[...])
        dhin_ref[...] = dh + _rms_bwd(dn, xhat, rstd, g)

        @pl.when(pl.program_id(0) == 0)
        def _():
            dg_ref[...] = jnp.zeros_like(dg_ref)

        dg_ref[...] += jnp.sum(dn * xhat, axis=0, keepdims=True)

    tile = lambda w: pl.BlockSpec((TM, w), lambda i: (i, 0))
    return pl.pallas_call(
        body, name=name, grid=(T // TM,),
        in_specs=[tile(D), tile(D), tile(2 * FF), _const_spec((1, D)), _const_spec((D, 2 * FF)), _const_spec((FF, D))],
        out_specs=[tile(D), tile(2 * FF), tile(FF), tile(D), pl.BlockSpec((1, D), lambda i: (0, 0))],
        out_shape=[SDS((T, D), F32), SDS((T, 2 * FF), BF16), SDS((T, FF), BF16), SDS((T, D), BF16), SDS((1, D), F32)],
        compiler_params=_cp(("arbitrary",)),
    )(dh, h_in, ab, g, wup, wdn)


def matmul_tn(a, b, name, *, tm, tn, tt=1024, b_scale=None):
    T, M = a.shape
    N = b.shape[1]
    assert M % tm == 0 and N % tn == 0 and T % tt == 0

    def body(a_ref, b_ref, o_ref):
        @pl.when(pl.program_id(2) == 0)
        def _():
            o_ref[...] = jnp.zeros_like(o_ref)

        bv = b_ref[...]
        if b_scale is not None:
            bv = bv * b_scale
        o_ref[...] += _dot_tn(a_ref[...].astype(BF16), bv.astype(BF16))

    return pl.pallas_call(
        body, name=name, grid=(M // tm, N // tn, T // tt),
        in_specs=[pl.BlockSpec((tt, tm), lambda i, j, t: (t, i)), pl.BlockSpec((tt, tn), lambda i, j, t: (t, j))],
        out_specs=pl.BlockSpec((tm, tn), lambda i, j, t: (i, j)),
        out_shape=SDS((M, N), F32),
        compiler_params=_cp(("parallel", "parallel", "arbitrary")),
    )(a, b)


def mix_in_fwd(h1, g, win):
    T = h1.shape[0]
    TM = 512

    def body(h_ref, g_ref, w_ref, u_ref, q_ref, kv_ref, z_ref, gt_ref):
        x = h_ref[...]
        u = (x * _rstd(x) * g_ref[...]).astype(BF16)
        u_ref[...] = u
        for c in range(0, AW, 256):
            q_ref[:, c:c + 256] = _dot(u, w_ref[:, c:c + 256]).astype(BF16)
        kv_ref[...] = _dot(u, w_ref[:, C_KV:C_Z]).astype(BF16)
        for c in range(0, PW, 256):
            z_ref[:, c:c + 256] = _dot(u, w_ref[:, C_Z + c:C_Z + c + 256])
        for c in range(0, 2 * D, 256):
            gt_ref[:, c:c + 256] = _dot(u, w_ref[:, C_G + c:C_G + c + 256]).astype(BF16)

    tile = lambda w: pl.BlockSpec((TM, w), lambda i: (i, 0))
    return pl.pallas_call(
        body, name="mix_in_fwd", grid=(T // TM,),
        in_specs=[tile(D), _const_spec((1, D)), _const_spec((D, INW))],
        out_specs=[tile(D), tile(AW), tile(2 * KVW), tile(PW), tile(2 * D)],
        out_shape=[SDS((T, D), BF16), SDS((T, AW), BF16), SDS((T, 2 * KVW), BF16), SDS((T, PW), F32),
                   SDS((T, 2 * D), BF16)],
        compiler_params=_cp(("arbitrary",)),
    )(h1, g, win)


def mix_in_bwd(dq, dkv, dz, dgt, dh2, h1, g, win):
    T = h1.shape[0]
    TM = 512

    def body(dq_ref, dkv_ref, dz_ref, dgt_ref, dh2_ref, h_ref, g_ref, w_ref, dh1_ref, dg_ref):
        du = _dot_nt(dq_ref[...], w_ref[:, 0:AW])
        du += _dot_nt(dkv_ref[...], w_ref[:, C_KV:C_Z])
        du += _dot_nt(dz_ref[...], w_ref[:, C_Z:C_G])
        du += _dot_nt(dgt_ref[...], w_ref[:, C_G:INW])
        x = h_ref[...]
        g = g_ref[...]
        rstd = _rstd(x)
        xhat = x * rstd
        dh1_ref[...] = dh2_ref[...] + _rms_bwd(du, xhat, rstd, g)

        @pl.when(pl.program_id(0) == 0)
        def _():
            dg_ref[...] = jnp.zeros_like(dg_ref)

        dg_ref[...] += jnp.sum(du * xhat, axis=0, keepdims=True)

    tile = lambda w: pl.BlockSpec((TM, w), lambda i: (i, 0))
    return pl.pallas_call(
        body, name="mix_in_bwd", grid=(T // TM,),
        in_specs=[tile(AW), tile(2 * KVW), tile(PW), tile(2 * D), tile(D), tile(D), _const_spec((1, D)),
                  _const_spec((D, INW))],
        out_specs=[tile(D), pl.BlockSpec((1, D), lambda i: (0, 0))],
        out_shape=[SDS((T, D), F32), SDS((1, D), F32)],
        compiler_params=_cp(("arbitrary",)),
    )(dq, dkv, dz, dgt, dh2, h1, g, win)


def _band_scores(qh, kc, kp, slope, first):
    row = lax.broadcasted_iota(jnp.int32, (BLK, BLK), 0)
    col = lax.broadcasted_iota(jnp.int32, (BLK, BLK), 1)
    dist = (row - col).astype(F32)
    sc = _dot_nt(qh, kc) * SCALE - slope * dist
    sp = _dot_nt(qh, kp) * SCALE - slope * (dist + float(BLK))
    sc = jnp.where(col <= row, sc, -jnp.inf)
    sp = jnp.where(jnp.logical_and(col > row, jnp.logical_not(first)), sp, -jnp.inf)
    return sc, sp


def attn_fwd(q, kv, sinks):
    T = q.shape[0]
    nb = T // BLK

    def body(sink_ref, q_ref, kvc_ref, kvp_ref, att_ref, lse_ref):
        first = pl.program_id(0) == 0
        for hq in range(NQ):
            kvh = hq // GQ
            qh = q_ref[:, hq * HD:(hq + 1) * HD]
            kc = kvc_ref[:, kvh * HD:(kvh + 1) * HD]
            kp = kvp_ref[:, kvh * HD:(kvh + 1) * HD]
            vc = kvc_ref[:, KVW + kvh * HD:KVW + (kvh + 1) * HD]
            vp = kvp_ref[:, KVW + kvh * HD:KVW + (kvh + 1) * HD]
            sink = sink_ref[0, hq]
            sc, sp = _band_scores(qh, kc, kp, SLOPES[hq], first)
            m = jnp.maximum(jnp.maximum(jnp.max(sc, axis=-1, keepdims=True), jnp.max(sp, axis=-1, keepdims=True)), sink)
            pc = jnp.exp(sc - m)
            pp = jnp.exp(sp - m)
            l = jnp.sum(pc, axis=-1, keepdims=True) + jnp.sum(pp, axis=-1, keepdims=True) + jnp.exp(sink - m)
            inv = 1.0 / l
            out = _dot((pc * inv).astype(BF16), vc) + _dot((pp * inv).astype(BF16), vp)
            att_ref[:, hq * HD:(hq + 1) * HD] = out.astype(BF16)
            lse_ref[:, hq:hq + 1] = m + jnp.log(l)

    return pl.pallas_call(
        body, name="attn_fwd", grid=(nb,),
        in_specs=[pl.BlockSpec(memory_space=pltpu.SMEM),
                  pl.BlockSpec((BLK, AW), lambda i: (i, 0)),
                  pl.BlockSpec((BLK, 2 * KVW), lambda i: (i, 0)),
                  pl.BlockSpec((BLK, 2 * KVW), lambda i: (jnp.maximum(i - 1, 0), 0))],
        out_specs=[pl.BlockSpec((BLK, AW), lambda i: (i, 0)), pl.BlockSpec((BLK, NQ), lambda i: (i, 0))],
        out_shape=[SDS((T, AW), BF16), SDS((T, NQ), F32)],
        compiler_params=_cp(("arbitrary",)),
    )(sinks, q, kv, kv)


def attn_bwd(q, kv, datt, lse, sinks):
    T = q.shape[0]
    nb = T // BLK

    def body(sink_ref, q_ref, kvc_ref, kvp_ref, do_ref, lse_ref, dq_ref, dkv_ref, dsink_ref, carry_ref, prev_ref):
        i = pl.program_id(0)

        @pl.when(i == 0)
        def _():
            dsink_ref[...] = jnp.zeros_like(dsink_ref)
            carry_ref[...] = jnp.zeros_like(carry_ref)

        @pl.when(i < nb)
        def _():
            first = i == 0
            for kvh in range(NKV):
                kc = kvc_ref[:, kvh * HD:(kvh + 1) * HD]
                kp = kvp_ref[:, kvh * HD:(kvh + 1) * HD]
                vc = kvc_ref[:, KVW + kvh * HD:KVW + (kvh + 1) * HD]
                vp = kvp_ref[:, KVW + kvh * HD:KVW + (kvh + 1) * HD]
                dkc = jnp.zeros((BLK, HD), F32)
                dkp = jnp.zeros((BLK, HD), F32)
                dvc = jnp.zeros((BLK, HD), F32)
                dvp = jnp.zeros((BLK, HD), F32)
                for gq in range(GQ):
                    hq = kvh * GQ + gq
                    qh = q_ref[:, hq * HD:(hq + 1) * HD]
                    doh = do_ref[:, hq * HD:(hq + 1) * HD]
                    lse_h = lse_ref[:, hq:hq + 1]
                    sink = sink_ref[0, hq]
                    sc, sp = _band_scores(qh, kc, kp, SLOPES[hq], first)
                    pc = jnp.exp(sc - lse_h)
                    pp = jnp.exp(sp - lse_h)
                    dpc = _dot_nt(doh, vc)
                    dpp = _dot_nt(doh, vp)
                    delta = jnp.sum(pc * dpc, axis=-1, keepdims=True) + jnp.sum(pp * dpp, axis=-1, keepdims=True)
                    dsc = (pc * (dpc - delta)).astype(BF16)
                    dsp = (pp * (dpp - delta)).astype(BF16)
                    dsink_ref[:, hq:hq + 1] += -jnp.sum(jnp.exp(sink - lse_h) * delta, axis=0, keepdims=True)
                    dq_ref[:, hq * HD:(hq + 1) * HD] = ((_dot(dsc, kc) + _dot(dsp, kp)) * SCALE).astype(BF16)
                    dkc += _dot_tn(dsc, qh)
                    dkp += _dot_tn(dsp, qh)
                    dvc += _dot_tn(pc.astype(BF16), doh)
                    dvp += _dot_tn(pp.astype(BF16), doh)
                prev_ref[:, kvh * HD:(kvh + 1) * HD] = carry_ref[:, kvh * HD:(kvh + 1) * HD] + dkp * SCALE
                prev_ref[:, KVW + kvh * HD:KVW + (kvh + 1) * HD] = carry_ref[:, KVW + kvh * HD:KVW + (kvh + 1) * HD] + dvp
                carry_ref[:, kvh * HD:(kvh + 1) * HD] = dkc * SCALE
                carry_ref[:, KVW + kvh * HD:KVW + (kvh + 1) * HD] = dvc
            dkv_ref[...] = prev_ref[...].astype(BF16)

        @pl.when(i == nb)
        def _():
            dkv_ref[...] = carry_ref[...].astype(BF16)

    cur = lambda i: (jnp.minimum(i, nb - 1), 0)
    prev = lambda i: (jnp.maximum(jnp.minimum(i, nb - 1) - 1, 0), 0)
    return pl.pallas_call(
        body, name="attn_bwd", grid=(nb + 1,),
        in_specs=[pl.BlockSpec(memory_space=pltpu.SMEM),
                  pl.BlockSpec((BLK, AW), cur), pl.BlockSpec((BLK, 2 * KVW), cur), pl.BlockSpec((BLK, 2 * KVW), prev),
                  pl.BlockSpec((BLK, AW), cur), pl.BlockSpec((BLK, NQ), cur)],
        out_specs=[pl.BlockSpec((BLK, AW), cur),
                   pl.BlockSpec((BLK, 2 * KVW), lambda i: (jnp.maximum(i - 1, 0), 0)),
                   pl.BlockSpec((1, NQ), lambda i: (0, 0))],
        out_shape=[SDS((T, AW), BF16), SDS((T, 2 * KVW), BF16), SDS((1, NQ), F32)],
        scratch_shapes=[pltpu.VMEM((BLK, 2 * KVW), F32), pltpu.VMEM((BLK, 2 * KVW), F32)],
        compiler_params=_cp(("arbitrary",)),
    )(sinks, q, kv, kv, datt, lse)


def _inv_counts(t0, rows):
    t = (t0 + lax.broadcasted_iota(jnp.int32, (rows, 1), 0) + 1).astype(F32)
    return [1.0 / jnp.minimum(t, float(w)) for w in POOL_WINDOWS]


def pool_fwd(z, wmix, scale):
    T = z.shape[0]
    TM = 512
    L = TM + HALO

    def body(z_ref, halo_ref, wmix_ref, scale_ref, pooled_ref, mixs_ref):
        i = pl.program_id(0)
        halo = jnp.where(i > 0, halo_ref[...], 0.0)
        zt = z_ref[...]
        e = jnp.concatenate([halo, zt], axis=0)
        sums = []
        s = e
        for k in (1, 2, 4, 8):
            s = s + pltpu.roll(s, k, 0)
            sums.append(s)
        inv = _inv_counts(i * TM, TM)
        for gi in range(len(POOL_WINDOWS)):
            cols = slice(gi * PG, (gi + 1) * PG)
            pooled = (sums[gi][HALO:, cols] * inv[gi] - zt[:, cols]).astype(BF16)
            pooled_ref[:, cols] = pooled
            mixs_ref[:, cols] = (_dot(pooled, wmix_ref[gi]) * scale_ref[:, cols]).astype(BF16)

    return pl.pallas_call(
        body, name="pool_fwd", grid=(T // TM,),
        in_specs=[pl.BlockSpec((TM, PW), lambda i: (i, 0)),
                  pl.BlockSpec((HALO, PW), lambda i: (jnp.maximum(i * (TM // HALO) - 1, 0), 0)),
                  _const_spec((len(POOL_WINDOWS), PG, PG)), _const_spec((1, PW))],
        out_specs=[pl.BlockSpec((TM, PW), lambda i: (i, 0)), pl.BlockSpec((TM, PW), lambda i: (i, 0))],
        out_shape=[SDS((T, PW), BF16), SDS((T, PW), BF16)],
        compiler_params=_cp(("arbitrary",)),
    )(z, z, wmix, scale)


def pool_bwd(dmixs, pooled, wmix, scale):
    T = dmixs.shape[0]
    TM = 512
    L = TM + HALO
    nt = T // TM

    def body(dm_ref, halo_ref, pooled_ref, wmix_ref, scale_ref, dz_ref, dwmix_ref, dscale_ref):
        i = pl.program_id(0)

        @pl.when(i == 0)
        def _():
            dwmix_ref[...] = jnp.zeros_like(dwmix_ref)
            dscale_ref[...] = jnp.zeros_like(dscale_ref)

        halo = jnp.where(i < nt - 1, halo_ref[...], 0.0)
        dm = dm_ref[...]
        e = jnp.concatenate([dm, halo], axis=0)
        inv = _inv_counts(i * TM, L)
        for gi in range(len(POOL_WINDOWS)):
            cols = slice(gi * PG, (gi + 1) * PG)
            w = wmix_ref[gi]
            dmixed = (e[:, cols] * scale_ref[:, cols]).astype(BF16)
            dpooled = _dot_nt(dmixed, w)
            pooled = pooled_ref[:, cols]
            mixed = _dot(pooled, w)
            dscale_ref[:, cols] += jnp.sum(dm[:, cols] * mixed, axis=0, keepdims=True)
            dwmix_ref[gi] += _dot_tn(pooled, dmixed[:TM])
            s = dpooled * inv[gi]
            k = 1
            while k < POOL_WINDOWS[gi]:
                s = s + pltpu.roll(s, L - k, 0)
                k *= 2
            dz_ref[:, cols] = (s[:TM] - dpooled[:TM]).astype(BF16)

    return pl.pallas_call(
        body, name="pool_bwd", grid=(nt,),
        in_specs=[pl.BlockSpec((TM, PW), lambda i: (i, 0)),
                  pl.BlockSpec((HALO, PW), lambda i: (jnp.minimum((i + 1) * (TM // HALO), T // HALO - 1), 0)),
                  pl.BlockSpec((TM, PW), lambda i: (i, 0)),
                  _const_spec((len(POOL_WINDOWS), PG, PG)), _const_spec((1, PW))],
        out_specs=[pl.BlockSpec((TM, PW), lambda i: (i, 0)),
                   pl.BlockSpec((len(POOL_WINDOWS), PG, PG), lambda i: (0, 0, 0)),
                   pl.BlockSpec((1, PW), lambda i: (0, 0))],
        out_shape=[SDS((T, PW), BF16), SDS((len(POOL_WINDOWS), PG, PG), F32), SDS((1, PW), F32)],
        compiler_params=_cp(("arbitrary",)),
    )(dmixs, dmixs, pooled, wmix, scale)


def merge_fwd(att, mixs, gt, h1, wattn, wpool, wout):
    T = h1.shape[0]
    TM = 512

    def body(att_ref, mixs_ref, gt_ref, h_ref, wa_ref, wp_ref, wo_ref, h2_ref, mg_ref):
        a = _dot(att_ref[...], wa_ref[...])
        p = _dot(mixs_ref[...], wp_ref[...])
        merged = (jax.nn.sigmoid(gt_ref[:, 0:D].astype(F32)) * a + jax.nn.sigmoid(gt_ref[:, D:2 * D].astype(F32)) * p)
        mg = merged.astype(BF16)
        mg_ref[...] = mg
        h2_ref[...] = h_ref[...] + _dot(mg, wo_ref[...])

    tile = lambda w: pl.BlockSpec((TM, w), lambda i: (i, 0))
    return pl.pallas_call(
        body, name="merge_fwd", grid=(T // TM,),
        in_specs=[tile(AW), tile(PW), tile(2 * D), tile(D), _const_spec((AW, D)), _const_spec((PW, D)),
                  _const_spec((D, D))],
        out_specs=[tile(D), tile(D)],
        out_shape=[SDS((T, D), F32), SDS((T, D), BF16)],
        compiler_params=_cp(("arbitrary",)),
    )(att, mixs, gt, h1, wattn, wpool, wout)


def merge_bwd(dh2, att, mixs, gt, wattn, wpool, wout):
    T = dh2.shape[0]
    TM = 512

    def body(dh2_ref, att_ref, mixs_ref, gt_ref, wa_ref, wp_ref, wo_ref, datt_ref, dmixs_ref, dgt_ref, da_ref, dp_ref):
        dm = _dot_nt(dh2_ref[...].astype(BF16), wo_ref[...])
        a = _dot(att_ref[...], wa_ref[...])
        p = _dot(mixs_ref[...], wp_ref[...])
        sa = jax.nn.sigmoid(gt_ref[:, 0:D].astype(F32))
        sp = jax.nn.sigmoid(gt_ref[:, D:2 * D].astype(F32))
        da = (dm * sa).astype(BF16)
        dp = (dm * sp).astype(BF16)
        da_ref[...] = da
        dp_ref[...] = dp
        dgt_ref[:, 0:D] = (dm * a * sa * (1.0 - sa)).astype(BF16)
        dgt_ref[:, D:2 * D] = (dm * p * sp * (1.0 - sp)).astype(BF16)
        datt_ref[...] = _dot_nt(da, wa_ref[...]).astype(BF16)
        dmixs_ref[...] = _dot_nt(dp, wp_ref[...])

    tile = lambda w: pl.BlockSpec((TM, w), lambda i: (i, 0))
    return pl.pallas_call(
        body, name="merge_bwd", grid=(T // TM,),
        in_specs=[tile(D), tile(AW), tile(PW), tile(2 * D), _const_spec((AW, D)), _const_spec((PW, D)),
                  _const_spec((D, D))],
        out_specs=[tile(AW), tile(PW), tile(2 * D), tile(D), tile(D)],
        out_shape=[SDS((T, AW), BF16), SDS((T, PW), F32), SDS((T, 2 * D), BF16), SDS((T, D), BF16),
                   SDS((T, D), BF16)],
        compiler_params=_cp(("arbitrary",)),
    )(dh2, att, mixs, gt, wattn, wpool, wout)


def loss_head(h3, target, g):
    T = h3.shape[0]
    TM = 512

    def body(h_ref, t_ref, g_ref, dh_ref, loss_ref, dg_ref):
        @pl.when(pl.program_id(0) == 0)
        def _():
            loss_ref[...] = jnp.zeros_like(loss_ref)
            dg_ref[...] = jnp.zeros_like(dg_ref)

        x = h_ref[...]
        g = g_ref[...]
        rstd = _rstd(x)
        xhat = x * rstd
        err = xhat * g - t_ref[...]
        loss_ref[...] += 0.5 * jnp.sum(jnp.mean(err * err, axis=-1, keepdims=True), axis=0, keepdims=True)
        dy = err * (1.0 / D)
        dg_ref[...] += jnp.sum(dy * xhat, axis=0, keepdims=True)
        dh_ref[...] = _rms_bwd(dy, xhat, rstd, g)

    tile = pl.BlockSpec((TM, D), lambda i: (i, 0))
    return pl.pallas_call(
        body, name="loss_head", grid=(T // TM,),
        in_specs=[tile, tile, _const_spec((1, D))],
        out_specs=[tile, pl.BlockSpec((1, 1), lambda i: (0, 0)), pl.BlockSpec((1, D), lambda i: (0, 0))],
        out_shape=[SDS((T, D), F32), SDS((1, 1), F32), SDS((1, D), F32)],
        compiler_params=_cp(("arbitrary",)),
    )(h3, target, g)


def adamw(w, g, m, v, name):
    R, C = w.shape
    tile_bytes = 2 * 1024 * 1024
    tr = R
    if R * C * 4 > tile_bytes:
        tr = next(cand for cand in (512, 256, 128, 64, 32, 16, 8) if R % cand == 0 and cand * C * 4 <= tile_bytes)
    c1 = 1.0 - B1 ** STEP
    c2 = 1.0 - B2 ** STEP

    def body(w_ref, g_ref, m_ref, v_ref, d_ref, nm_ref, nv_ref):
        gv = g_ref[...]
        nm = B1 * m_ref[...] + (1.0 - B1) * gv
        nv = B2 * v_ref[...] + (1.0 - B2) * (gv * gv)
        nm_ref[...] = nm
        nv_ref[...] = nv
        d_ref[...] = -LR * ((nm / c1) / (jnp.sqrt(nv / c2) + ADAM_EPS) + WD * w_ref[...])

    spec = pl.BlockSpec((tr, C), lambda i: (i, 0))
    return pl.pallas_call(
        body, name=name, grid=(R // tr,),
        in_specs=[spec] * 4, out_specs=[spec] * 3, out_shape=[SDS((R, C), F32)] * 3,
        compiler_params=_cp(("parallel",)),
    )(w, g, m, v)


def local_fwd_bwd(x, target, W):
    h1, ab1 = ffn_fwd(x, W["ffn1_norm"], W["ffn1_w_up"], W["ffn1_w_down"], "ffn1_fwd")
    u, q, kv, z, gt = mix_in_fwd(h1, W["mix_norm"], W["w_in"])
    att, lse = attn_fwd(q, kv, W["sinks"])
    pooled, mixs = pool_fwd(z, W["pool_w_mix"], W["pool_scale"])
    h2, merged = merge_fwd(att, mixs, gt, h1, W["w_attn_up"], W["w_pool_up"], W["w_out"])
    h3, ab2 = ffn_fwd(h2, W["ffn2_norm"], W["ffn2_w_up"], W["ffn2_w_down"], "ffn2_fwd")
    dh3, loss, g_final = loss_head(h3, target, W["final_norm"])

    G = {"final_norm": g_final}
    dh2, dup2, hid2, n2, G["ffn2_norm"] = ffn_bwd_x(dh3, h2, ab2, W["ffn2_norm"], W["ffn2_w_up"], W["ffn2_w_down"],
                                                    "ffn2_bwd_x")
    G["ffn2_w_up"] = matmul_tn(n2, dup2, "ffn2_dw_up", tm=D, tn=1408)
    G["ffn2_w_down"] = matmul_tn(hid2, dh3, "ffn2_dw_down", tm=1408, tn=D, b_scale=0.5)

    datt, dmixs, dgt, da, dp = merge_bwd(dh2, att, mixs, gt, W["w_attn_up"], W["w_pool_up"], W["w_out"])
    G["w_out"] = matmul_tn(merged, dh2, "dw_out", tm=D, tn=D)
    G["w_attn_up"] = matmul_tn(att, da, "dw_attn_up", tm=AW, tn=D)
    G["w_pool_up"] = matmul_tn(mixs, dp, "dw_pool_up", tm=PW, tn=D)
    dz, G["pool_w_mix"], G["pool_scale"] = pool_bwd(dmixs, pooled, W["pool_w_mix"], W["pool_scale"])
    dq, dkv, G["sinks"] = attn_bwd(q, kv, datt, lse, W["sinks"])
    dh1, G["mix_norm"] = mix_in_bwd(dq, dkv, dz, dgt, dh2, h1, W["mix_norm"], W["w_in"])
    G["w_in"] = jnp.concatenate([
        matmul_tn(u, dq, "dw_in_q", tm=D, tn=AW),
        matmul_tn(u, dkv, "dw_in_kv", tm=D, tn=2 * KVW),
        matmul_tn(u, dz, "dw_in_z", tm=D, tn=PW),
        matmul_tn(u, dgt, "dw_in_g", tm=D, tn=D),
    ], axis=1)

    dx, dup1, hid1, n1, G["ffn1_norm"] = ffn_bwd_x(dh1, x, ab1, W["ffn1_norm"], W["ffn1_w_up"], W["ffn1_w_down"],
                                                   "ffn1_bwd_x")
    G["ffn1_w_up"] = matmul_tn(n1, dup1, "ffn1_dw_up", tm=D, tn=1408)
    G["ffn1_w_down"] = matmul_tn(hid1, dh1, "ffn1_dw_down", tm=1408, tn=D, b_scale=0.5)
    return loss, dx, G


HBM_SPEC = pl.BlockSpec(memory_space=pltpu.HBM)


def _me():
    return lax.axis_index("x"), lax.axis_index("y"), lax.axis_index("c")


def _peer_chip(x, y, k):
    return x ^ (k >> 1), y ^ (k & 1)


def all_gather_packed(shard):
    R, Wd = shard.shape
    H = R // 2

    def body(src_ref, out_ref, send_sems, recv_sems, local_sem):
        x, y, c = _me()
        chip = 2 * x + y
        mine = pl.ds(c * H, H)
        theirs = pl.ds((1 - c) * H, H)
        local = pltpu.make_async_copy(src_ref, out_ref.at[chip], local_sem)
        local.start()
        sends = []
        for k in (1, 2, 3):
            px, py = _peer_chip(x, y, k)
            cp = pltpu.make_async_remote_copy(src_ref.at[mine], out_ref.at[chip, mine], send_sems.at[k - 1],
                                              recv_sems.at[k - 1], device_id=(px, py, c), device_id_type=MESH)
            cp.start()
            sends.append(cp)
        for k in (1, 2, 3):
            px, py = _peer_chip(x, y, k)
            slot = out_ref.at[2 * px + py, mine]
            pltpu.make_async_remote_copy(slot, slot, send_sems.at[k - 1], recv_sems.at[k - 1],
                                         device_id=(px, py, c), device_id_type=MESH).wait_recv()
            fwd = pltpu.make_async_remote_copy(slot, slot, send_sems.at[2 + k], recv_sems.at[2 + k],
                                               device_id=(x, y, 1 - c), device_id_type=MESH)
            fwd.start()
            sends.append(fwd)
        for k in (1, 2, 3):
            px, py = _peer_chip(x, y, k)
            slot = out_ref.at[2 * px + py, theirs]
            pltpu.make_async_remote_copy(slot, slot, send_sems.at[2 + k], recv_sems.at[2 + k],
                                         device_id=(x, y, 1 - c), device_id_type=MESH).wait_recv()
        for cp in sends:
            cp.wait_send()
        local.wait()

    return pl.pallas_call(
        body, name="all_gather_packed", in_specs=[HBM_SPEC], out_specs=HBM_SPEC,
        out_shape=SDS((N_CHIPS, R, Wd), shard.dtype),
        scratch_shapes=[pltpu.SemaphoreType.DMA((6,)), pltpu.SemaphoreType.DMA((6,)), pltpu.SemaphoreType.DMA],
    )(shard)


def swap_halves(p):
    _, _, H, Wd = p.shape

    def body(p_ref, got_ref, send_sem, recv_sem):
        x, y, c = _me()
        cp = pltpu.make_async_remote_copy(p_ref.at[:, 1 - c], got_ref, send_sem, recv_sem,
                                          device_id=(x, y, 1 - c), device_id_type=MESH)
        cp.start()
        cp.wait()

    return pl.pallas_call(
        body, name="swap_halves", in_specs=[HBM_SPEC], out_specs=HBM_SPEC,
        out_shape=SDS((N_CHIPS, H, Wd), p.dtype),
        scratch_shapes=[pltpu.SemaphoreType.DMA, pltpu.SemaphoreType.DMA],
    )(p)


def add_halves(p, got, c_arr):
    _, _, H, Wd = p.shape
    tr = 416
    assert H % tr == 0

    def body(c_ref, p_ref, g_ref, o_ref):
        o_ref[...] = (p_ref[...] + g_ref[...]).astype(BF16)

    return pl.pallas_call(
        body, name="add_halves",
        grid_spec=pltpu.PrefetchScalarGridSpec(
            num_scalar_prefetch=1, grid=(N_CHIPS, H // tr),
            in_specs=[pl.BlockSpec((None, None, tr, Wd), lambda j, r, c_ref: (j, c_ref[0], r, 0)),
                      pl.BlockSpec((None, tr, Wd), lambda j, r, c_ref: (j, r, 0))],
            out_specs=pl.BlockSpec((None, tr, Wd), lambda j, r, c_ref: (j, r, 0))),
        out_shape=SDS((N_CHIPS, H, Wd), BF16),
        compiler_params=_cp(("parallel", "parallel")),
    )(c_arr, p, got)


def chip_all_to_all(q):
    _, H, Wd = q.shape

    def body(q_ref, out_ref, send_sems, recv_sems, local_sem):
        x, y, c = _me()
        chip = 2 * x + y
        local = pltpu.make_async_copy(q_ref.at[chip], out_ref.at[chip], local_sem)
        local.start()
        sends = []
        for k in (1, 2, 3):
            px, py = _peer_chip(x, y, k)
            cp = pltpu.make_async_remote_copy(q_ref.at[2 * px + py], out_ref.at[chip], send_sems.at[k - 1],
                                              recv_sems.at[k - 1], device_id=(px, py, c), device_id_type=MESH)
            cp.start()
            sends.append(cp)
        for k in (1, 2, 3):
            px, py = _peer_chip(x, y, k)
            slot = out_ref.at[2 * px + py]
            pltpu.make_async_remote_copy(slot, slot, send_sems.at[k - 1], recv_sems.at[k - 1],
                                         device_id=(px, py, c), device_id_type=MESH).wait_recv()
        for cp in sends:
            cp.wait_send()
        local.wait()

    return pl.pallas_call(
        body, name="chip_all_to_all", in_specs=[HBM_SPEC], out_specs=HBM_SPEC,
        out_shape=SDS(q.shape, q.dtype),
        scratch_shapes=[pltpu.SemaphoreType.DMA((3,)), pltpu.SemaphoreType.DMA((3,)), pltpu.SemaphoreType.DMA],
    )(q)


def sum_slots(r):
    _, H, Wd = r.shape
    tr = 416
    assert H % tr == 0

    def body(r_ref, o_ref):
        acc = r_ref[0].astype(F32)
        for j in range(1, N_CHIPS):
            acc = acc + r_ref[j].astype(F32)
        o_ref[...] = acc

    return pl.pallas_call(
        body, name="sum_slots", grid=(H // tr,),
        in_specs=[pl.BlockSpec((N_CHIPS, tr, Wd), lambda r: (0, r, 0))],
        out_specs=pl.BlockSpec((tr, Wd), lambda r: (r, 0)),
        out_shape=SDS((H, Wd), F32),
        compiler_params=_cp(("parallel",)),
    )(r)


def join_halves(half):
    H, Wd = half.shape

    def body(h_ref, out_ref, send_sem, recv_sem, local_sem):
        x, y, c = _me()
        local = pltpu.make_async_copy(h_ref, out_ref.at[c], local_sem)
        local.start()
        cp = pltpu.make_async_remote_copy(h_ref, out_ref.at[c], send_sem, recv_sem,
                                          device_id=(x, y, 1 - c), device_id_type=MESH)
        cp.start()
        got = out_ref.at[1 - c]
        pltpu.make_async_remote_copy(got, got, send_sem, recv_sem, device_id=(x, y, 1 - c),
                                     device_id_type=MESH).wait_recv()
        cp.wait_send()
        local.wait()

    return pl.pallas_call(
        body, name="join_halves", in_specs=[HBM_SPEC], out_specs=HBM_SPEC,
        out_shape=SDS((2, H, Wd), half.dtype),
        scratch_shapes=[pltpu.SemaphoreType.DMA, pltpu.SemaphoreType.DMA, pltpu.SemaphoreType.DMA],
    )(half)


N_DEV = 8


def all_reduce_small(s):
    R, Wd = s.shape

    def body(s_ref, out_ref, slots_ref, send_sems, recv_sems):
        x, y, c = _me()
        me = 4 * x + 2 * y + c
        slots_ref[me] = s_ref[...]
        sends = []
        for k in range(1, N_DEV):
            peer = (x ^ (k >> 2), y ^ ((k >> 1) & 1), c ^ (k & 1))
            cp = pltpu.make_async_remote_copy(s_ref, slots_ref.at[me], send_sems.at[k - 1], recv_sems.at[k - 1],
                                              device_id=peer, device_id_type=MESH)
            cp.start()
            sends.append(cp)
        for k in range(1, N_DEV):
            peer = (x ^ (k >> 2), y ^ ((k >> 1) & 1), c ^ (k & 1))
            slot = slots_ref.at[4 * peer[0] + 2 * peer[1] + peer[2]]
            pltpu.make_async_remote_copy(s_ref, slot, send_sems.at[k - 1], recv_sems.at[k - 1],
                                         device_id=peer, device_id_type=MESH).wait_recv()
        for cp in sends:
            cp.wait_send()
        acc = slots_ref[0]
        for d in range(1, N_DEV):
            acc = acc + slots_ref[d]
        out_ref[...] = acc

    vmem = pl.BlockSpec(memory_space=pltpu.VMEM)
    return pl.pallas_call(
        body, name="all_reduce_small", in_specs=[vmem], out_specs=vmem, out_shape=SDS((R, Wd), F32),
        scratch_shapes=[pltpu.VMEM((N_DEV, R, Wd), F32), pltpu.SemaphoreType.DMA((N_DEV - 1,)),
                        pltpu.SemaphoreType.DMA((N_DEV - 1,))],
    )(s)


BIG = (("ffn1_w_up", D, 2 * FF, 1), ("ffn1_w_down", FF, D, 0), ("w_in", D, INW, 1), ("w_attn_up", AW, D, 0),
       ("w_pool_up", PW, D, 1), ("w_out", D, D, 0), ("ffn2_w_up", D, 2 * FF, 1), ("ffn2_w_down", FF, D, 0))
SMALL = ("ffn1_norm", "mix_norm", "ffn2_norm", "final_norm", "pool_scale", "sinks", "pool_w_mix")
SMALL_W = 128


def _shard_shape(k, n, axis):
    return (k // N_CHIPS, n) if axis == 0 else (k, n // N_CHIPS)


def pack_shards(shards):
    return jnp.concatenate([shards[name].reshape(-1, PACK_W) for name, *_ in BIG], axis=0)


def unpack_shards(packed):
    out, r0 = {}, 0
    for name, k, n, axis in BIG:
        ks, ns = _shard_shape(k, n, axis)
        rows = ks * ns // PACK_W
        out[name] = packed[r0:r0 + rows].reshape(ks, ns)
        r0 += rows
    return out


def unpack_gathered(g):
    out, r0 = {}, 0
    for name, k, n, axis in BIG:
        ks, ns = _shard_shape(k, n, axis)
        rows = ks * ns // PACK_W
        piece = g[:, r0:r0 + rows].reshape(N_CHIPS, ks, ns)
        out[name] = piece.reshape(k, n) if axis == 0 else piece.transpose(1, 0, 2).reshape(k, n)
        r0 += rows
    return out


def pack_full(full):
    parts = []
    for name, k, n, axis in BIG:
        ks, ns = _shard_shape(k, n, axis)
        a = full[name]
        a = a.reshape(N_CHIPS, ks, ns) if axis == 0 else a.reshape(k, N_CHIPS, ns).transpose(1, 0, 2)
        parts.append(a.reshape(N_CHIPS, -1, PACK_W))
    return jnp.concatenate(parts, axis=1)


def pack_small(d):
    parts = []
    for name in SMALL:
        a = d[name].reshape(-1)
        pad = (-a.shape[0]) % SMALL_W
        parts.append(jnp.pad(a, (0, pad)).reshape(-1, SMALL_W))
    a = jnp.concatenate(parts, axis=0)
    return jnp.pad(a, ((0, (-a.shape[0]) % 8), (0, 0)))


def unpack_small(a, like):
    out, r0 = {}, 0
    for name in SMALL:
        size = int(np.prod(like[name].shape))
        rows = -(-size // SMALL_W)
        out[name] = a[r0:r0 + rows].reshape(-1)[:size].reshape(like[name].shape)
        r0 += rows
    return out


WEIGHTS = ("ffn1_norm", "ffn1_w_up", "ffn1_w_down", "mix_norm", "w_in", "sinks", "w_attn_up", "pool_w_mix",
           "pool_scale", "w_pool_up", "w_out", "ffn2_norm", "ffn2_w_up", "ffn2_w_down", "final_norm")


def kernel(x, ffn1_norm, ffn1_w_up, ffn1_w_down, mix_norm, w_in, sinks, w_attn_up, pool_w_mix, pool_scale, w_pool_up, w_out, ffn2_norm, ffn2_w_up, ffn2_w_down, final_norm, loss_target, m_ffn1_norm, m_ffn1_w_up, m_ffn1_w_down, m_mix_norm, m_w_in, m_sinks, m_w_attn_up, m_pool_w_mix, m_pool_scale, m_w_pool_up, m_w_out, m_ffn2_norm, m_ffn2_w_up, m_ffn2_w_down, m_final_norm, v_ffn1_norm, v_ffn1_w_up, v_ffn1_w_down, v_mix_norm, v_w_in, v_sinks, v_w_attn_up, v_pool_w_mix, v_pool_scale, v_w_pool_up, v_w_out, v_ffn2_norm, v_ffn2_w_up, v_ffn2_w_down, v_final_norm):
    given = dict(locals())
    w = {n: given[n] for n in WEIGHTS}
    m = {n: given["m_" + n] for n in WEIGHTS}
    v = {n: given["v_" + n] for n in WEIGHTS}
    big_names = [name for name, *_ in BIG]

    shards = {n: w[n][0] for n in big_names}
    gathered = all_gather_packed(pack_shards({n: shards[n].astype(BF16) for n in big_names}))
    W = unpack_gathered(gathered)
    for n in ("ffn1_norm", "mix_norm", "ffn2_norm", "final_norm", "pool_scale", "sinks"):
        W[n] = w[n].reshape(1, -1)
    W["pool_w_mix"] = w["pool_w_mix"][0].astype(BF16)

    loss, dx, G = local_fwd_bwd(x[0], loss_target[0], W)
    loss = lax.psum(loss[0, 0], ("x", "y", "c"))

    c_arr = lax.axis_index("c").astype(jnp.int32).reshape(1)
    p = pack_full(G)
    rows = p.shape[1]
    p = p.reshape(N_CHIPS, 2, rows // 2, PACK_W)
    q = add_halves(p, swap_halves(p), c_arr)
    reduced = join_halves(sum_slots(chip_all_to_all(q))).reshape(rows, PACK_W)
    grads = unpack_shards(reduced)

    small_like = {n: w[n] for n in SMALL}
    grads.update(unpack_small(all_reduce_small(pack_small({n: G[n] for n in SMALL})), small_like))

    delta, new_m, new_v = {}, {}, {}
    for n in big_names:
        delta[n], new_m[n], new_v[n] = adamw(shards[n], grads[n], m[n][0], v[n][0], "adamw_" + n)
    ds, ms, vs = adamw(pack_small(w), pack_small(grads), pack_small(m), pack_small(v), "adamw_small")
    delta.update(unpack_small(ds, small_like))
    new_m.update(unpack_small(ms, small_like))
    new_v.update(unpack_small(vs, small_like))

    def shaped(d, n):
        return d[n].reshape(w[n].shape)

    return (loss, dx[None], *[shaped(grads, n) for n in WEIGHTS], *[shaped(delta, n) for n in WEIGHTS],
            *[shaped(new_m, n) for n in WEIGHTS], *[shaped(new_v, n) for n in WEIGHTS])
```

```python
import functools

import numpy as np
import jax
import jax.numpy as jnp
from jax import lax
from jax.experimental import pallas as pl
from jax.experimental.pallas import tpu as pltpu

F32 = jnp.float32
BF16 = jnp.bfloat16
SDS = jax.ShapeDtypeStruct
MESH = pl.DeviceIdType.MESH

D = 1024
FF = 2816
NQ = 16
NKV = 2
HD = 64
GQ = NQ // NKV
AW = NQ * HD
KVW = NKV * HD
BLK = 128
PW = 512
PG = 128
POOL_WINDOWS = (2, 4, 8, 16)
HALO = 16
INW = AW + 2 * KVW + PW + 2 * D
C_KV = AW
C_Z = AW + 2 * KVW
C_G = C_Z + PW
EPS = 1e-6
FF_CHUNK = 256
FF_CHUNKS = tuple((c, FF_CHUNK) for c in range(0, FF, FF_CHUNK))
SLOPES = tuple(float(2.0 ** (-8.0 * h / NQ)) for h in range(1, NQ + 1))
SCALE = HD ** -0.5

LR, B1, B2, ADAM_EPS, WD, STEP = 0.001, 0.9, 0.999, 1e-08, 0.01, 10

VMEM_LIMIT = 56 * 1024 * 1024
PACK_W = 1024
N_CHIPS = 4

NT = (((1,), (1,)), ((), ()))
TN = (((0,), (0,)), ((), ()))


def _cp(sem=None, vmem=VMEM_LIMIT):
    return pltpu.CompilerParams(dimension_semantics=sem, vmem_limit_bytes=vmem)


def _const_spec(shape):
    nd = len(shape)
    return pl.BlockSpec(shape, lambda *_: (0,) * nd, pipeline_mode=pl.Buffered(1))


def _rstd(x):
    return lax.rsqrt(jnp.mean(x * x, axis=-1, keepdims=True) + EPS)


def _rms_bwd(dn, xhat, rstd, g):
    dxhat = dn * g
    return rstd * (dxhat - xhat * jnp.mean(dxhat * xhat, axis=-1, keepdims=True))


def _dot(a, b):
    return jnp.dot(a, b, preferred_element_type=F32)


def _dot_nt(a, b):
    return lax.dot_general(a, b, NT, preferred_element_type=F32)


def _dot_tn(a, b):
    return lax.dot_general(a, b, TN, preferred_element_type=F32)


def ffn_fwd(h, g, wup, wdn, name):
    T = h.shape[0]
    TM = 512

    def body(h_ref, g_ref, wup_ref, wdn_ref, out_ref, ab_ref, hid_ref):
        x = h_ref[...]
        n = (x * _rstd(x) * g_ref[...]).astype(BF16)
        for c0, w in FF_CHUNKS:
            a = _dot(n, wup_ref[:, c0:c0 + w])
            b = _dot(n, wup_ref[:, FF + c0:FF + c0 + w])
            ab_ref[:, c0:c0 + w] = a.astype(BF16)
            ab_ref[:, FF + c0:FF + c0 + w] = b.astype(BF16)
            hid_ref[:, c0:c0 + w] = (a * jax.nn.sigmoid(a) * b).astype(BF16)
        out_ref[...] = x + 0.5 * _dot(hid_ref[...], wdn_ref[...])

    return pl.pallas_call(
        body, name=name, grid=(T // TM,),
        in_specs=[pl.BlockSpec((TM, D), lambda i: (i, 0)), _const_spec((1, D)),
                  _const_spec((D, 2 * FF)), _const_spec((FF, D))],
        out_specs=[pl.BlockSpec((TM, D), lambda i: (i, 0)), pl.BlockSpec((TM, 2 * FF), lambda i: (i, 0))],
        out_shape=[SDS((T, D), F32), SDS((T, 2 * FF), BF16)],
        scratch_shapes=[pltpu.VMEM((TM, FF), BF16)],
        compiler_params=_cp(("arbitrary",)),
    )(h, g, wup, wdn)


def ffn_bwd_x(dh, h_in, ab, g, wup, wdn, name):
    T = dh.shape[0]
    TM = 256

    def body(dh_ref, h_ref, ab_ref, g_ref, wup_ref, wdn_ref, dhin_ref, dup_ref, hid_ref, n_ref, dg_ref):
        x = h_ref[...]
        g = g_ref[...]
        rstd = _rstd(x)
        xhat = x * rstd
        n_ref[...] = (xhat * g).astype(BF16)
        dh = dh_ref[...]
        dhh = (0.5 * dh).astype(BF16)
        for c0, w in FF_CHUNKS:
            dhid = _dot_nt(dhh, wdn_ref[c0:c0 + w, :])
            a = ab_ref[:, c0:c0 + w].astype(F32)
            b = ab_ref[:, FF + c0:FF + c0 + w].astype(F32)
            sig = jax.nn.sigmoid(a)
            s = a * sig
            hid_ref[:, c0:c0 + w] = (s * b).astype(BF16)
            dup_ref[:, c0:c0 + w] = (dhid * b * (sig * (1.0 + a * (1.0 - sig)))).astype(BF16)
            dup_ref[:, FF + c0:FF + c0 + w] = (dhid * s).astype(BF16)
        dn = _dot_nt(dup_ref[...], wup_ref[...])
        dhin_ref[...] = dh + _rms_bwd(dn, xhat, rstd, g)

        @pl.when(pl.program_id(0) == 0)
        def _():
            dg_ref[...] = jnp.zeros_like(dg_ref)

        dg_ref[...] += jnp.sum(dn * xhat, axis=0, keepdims=True)

    tile = lambda w: pl.BlockSpec((TM, w), lambda i: (i, 0))
    return pl.pallas_call(
        body, name=name, grid=(T // TM,),
        in_specs=[tile(D), tile(D), tile(2 * FF), _const_spec((1, D)), _const_spec((D, 2 * FF)), _const_spec((FF, D))],
        out_specs=[tile(D), tile(2 * FF), tile(FF), tile(D), pl.BlockSpec((1, D), lambda i: (0, 0))],
        out_shape=[SDS((T, D), F32), SDS((T, 2 * FF), BF16), SDS((T, FF), BF16), SDS((T, D), BF16), SDS((1, D), F32)],
        compiler_params=_cp(("arbitrary",)),
    )(dh, h_in, ab, g, wup, wdn)


def matmul_tn(a, b, name, *, tm, tn, tt=1024, b_scale=None):
    T, M = a.shape
    N = b.shape[1]
    assert M % tm == 0 and N % tn == 0 and T % tt == 0

    def body(a_ref, b_ref, o_ref):
        @pl.when(pl.program_id(2) == 0)
        def _():
            o_ref[...] = jnp.zeros_like(o_ref)

        bv = b_ref[...]
        if b_scale is not None:
            bv = bv * b_scale
        o_ref[...] += _dot_tn(a_ref[...].astype(BF16), bv.astype(BF16))

    return pl.pallas_call(
        body, name=name, grid=(M // tm, N // tn, T // tt),
        in_specs=[pl.BlockSpec((tt, tm), lambda i, j, t: (t, i)), pl.BlockSpec((tt, tn), lambda i, j, t: (t, j))],
        out_specs=pl.BlockSpec((tm, tn), lambda i, j, t: (i, j)),
        out_shape=SDS((M, N), F32),
        compiler_params=_cp(("parallel", "parallel", "arbitrary")),
    )(a, b)


def mix_in_fwd(h1, g, win):
    T = h1.shape[0]
    TM = 512

    def body(h_ref, g_ref, w_ref, u_ref, q_ref, kv_ref, z_ref, gt_ref):
        x = h_ref[...]
        u = (x * _rstd(x) * g_ref[...]).astype(BF16)
        u_ref[...] = u
        for c in range(0, AW, 256):
            q_ref[:, c:c + 256] = _dot(u, w_ref[:, c:c + 256]).astype(BF16)
        kv_ref[...] = _dot(u, w_ref[:, C_KV:C_Z]).astype(BF16)
        for c in range(0, PW, 256):
            z_ref[:, c:c + 256] = _dot(u, w_ref[:, C_Z + c:C_Z + c + 256])
        for c in range(0, 2 * D, 256):
            gt_ref[:, c:c + 256] = _dot(u, w_ref[:, C_G + c:C_G + c + 256]).astype(BF16)

    tile = lambda w: pl.BlockSpec((TM, w), lambda i: (i, 0))
    return pl.pallas_call(
        body, name="mix_in_fwd", grid=(T // TM,),
        in_specs=[tile(D), _const_spec((1, D)), _const_spec((D, INW))],
        out_specs=[tile(D), tile(AW), tile(2 * KVW), tile(PW), tile(2 * D)],
        out_shape=[SDS((T, D), BF16), SDS((T, AW), BF16), SDS((T, 2 * KVW), BF16), SDS((T, PW), F32),
                   SDS((T, 2 * D), BF16)],
        compiler_params=_cp(("arbitrary",)),
    )(h1, g, win)


def mix_in_bwd(dq, dkv, dz, dgt, dh2, h1, g, win):
    T = h1.shape[0]
    TM = 512

    def body(dq_ref, dkv_ref, dz_ref, dgt_ref, dh2_ref, h_ref, g_ref, w_ref, dh1_ref, dg_ref):
        du = _dot_nt(dq_ref[...], w_ref[:, 0:AW])
        du += _dot_nt(dkv_ref[...], w_ref[:, C_KV:C_Z])
        du += _dot_nt(dz_ref[...], w_ref[:, C_Z:C_G])
        du += _dot_nt(dgt_ref[...], w_ref[:, C_G:INW])
        x = h_ref[...]
        g = g_ref[...]
        rstd = _rstd(x)
        xhat = x * rstd
        dh1_ref[...] = dh2_ref[...] + _rms_bwd(du, xhat, rstd, g)

        @pl.when(pl.program_id(0) == 0)
        def _():
            dg_ref[...] = jnp.zeros_like(dg_ref)

        dg_ref[...] += jnp.sum(du * xhat, axis=0, keepdims=True)

    tile = lambda w: pl.BlockSpec((TM, w), lambda i: (i, 0))
    return pl.pallas_call(
        body, name="mix_in_bwd", grid=(T // TM,),
        in_specs=[tile(AW), tile(2 * KVW), tile(PW), tile(2 * D), tile(D), tile(D), _const_spec((1, D)),
                  _const_spec((D, INW))],
        out_specs=[tile(D), pl.BlockSpec((1, D), lambda i: (0, 0))],
        out_shape=[SDS((T, D), F32), SDS((1, D), F32)],
        compiler_params=_cp(("arbitrary",)),
    )(dq, dkv, dz, dgt, dh2, h1, g, win)


PAIR = 2 * HD
NPAIR = GQ // 2


def _lo_lanes():
    return lax.broadcasted_iota(jnp.int32, (BLK, PAIR), 1) < HD


def _stack_heads(ref, kvh, scale=None):
    lo = _lo_lanes()
    parts = []
    for pr in range(NPAIR):
        t = ref[:, (kvh * NPAIR + pr) * PAIR:(kvh * NPAIR + pr + 1) * PAIR]
        if scale is not None:
            t = t * scale
        zero = jnp.zeros_like(t)
        parts += [jnp.where(lo, t, zero), jnp.where(lo, zero, t)]
    return jnp.concatenate(parts, axis=0)


def _kv_tiles(kvc_ref, kvp_ref, tile, kvh):
    lo = _lo_lanes()
    dup, left, right = [], [], []
    for ref in (kvp_ref, kvc_ref):
        t = ref[:, tile * PAIR:(tile + 1) * PAIR]
        r = pltpu.roll(t.astype(F32), HD, 1).astype(BF16)
        zero = jnp.zeros_like(t)
        a, b = (t, r) if kvh == 0 else (r, t)
        dup.append(jnp.where(lo, a, b))
        left.append(jnp.where(lo, a, zero))
        right.append(jnp.where(lo, zero, b))
    cat = lambda xs: jnp.concatenate(xs, axis=0)
    return cat(dup), cat(left), cat(right)


def _band_consts(first):
    row = lax.broadcasted_iota(jnp.int32, (BLK, BLK), 0)
    col = lax.broadcasted_iota(jnp.int32, (BLK, BLK), 1)
    upper = col > row
    dist = jnp.where(upper, row - col + BLK, row - col).astype(F32)
    pen = jnp.where(jnp.logical_and(upper, first), -jnp.inf, 0.0)
    return upper, dist, pen


def _split_band(upper, t):
    zero = jnp.zeros_like(t)
    return jnp.concatenate([jnp.where(upper, t, zero), jnp.where(upper, zero, t)], axis=1)


def attn_fwd(q, kv, sinks):
    T = q.shape[0]
    nb = T // BLK

    def body(sink_ref, q_ref, kvc_ref, kvp_ref, att_ref, lse_ref):
        upper, dist, pen = _band_consts(pl.program_id(0) == 0)
        for kvh in range(NKV):
            kdup, _, _ = _kv_tiles(kvc_ref, kvp_ref, 0, kvh)
            _, vleft, vright = _kv_tiles(kvc_ref, kvp_ref, 1, kvh)
            s_all = _dot_nt(_stack_heads(q_ref, kvh, SCALE), kdup)
            for pr in range(NPAIR):
                out = None
                for side, vpad in ((0, vleft), (1, vright)):
                    g = 2 * pr + side
                    hq = kvh * GQ + g
                    sink = sink_ref[0, hq]
                    rows = slice(g * BLK, (g + 1) * BLK)
                    s = jnp.where(upper, s_all[rows, 0:BLK], s_all[rows, BLK:2 * BLK]) - SLOPES[hq] * dist + pen
                    m = jnp.maximum(jnp.max(s, axis=-1, keepdims=True), sink)
                    p = jnp.exp(s - m)
                    l = jnp.sum(p, axis=-1, keepdims=True) + jnp.exp(sink - m)
                    lse_ref[:, hq:hq + 1] = m + jnp.log(l)
                    o = _dot(_split_band(upper, (p * (1.0 / l)).astype(BF16)), vpad)
                    out = o if out is None else out + o
                col0 = (kvh * NPAIR + pr) * PAIR
                att_ref[:, col0:col0 + PAIR] = out.astype(BF16)

    return pl.pallas_call(
        body, name="attn_fwd", grid=(nb,),
        in_specs=[pl.BlockSpec(memory_space=pltpu.SMEM),
                  pl.BlockSpec((BLK, AW), lambda i: (i, 0)),
                  pl.BlockSpec((BLK, 2 * KVW), lambda i: (i, 0)),
                  pl.BlockSpec((BLK, 2 * KVW), lambda i: (jnp.maximum(i - 1, 0), 0))],
        out_specs=[pl.BlockSpec((BLK, AW), lambda i: (i, 0)), pl.BlockSpec((BLK, NQ), lambda i: (i, 0))],
        out_shape=[SDS((T, AW), BF16), SDS((T, NQ), F32)],
        compiler_params=_cp(("arbitrary",)),
    )(sinks, q, kv, kv)


def attn_bwd(q, kv, datt, lse, sinks):
    T = q.shape[0]
    nb = T // BLK

    def body(sink_ref, q_ref, kvc_ref, kvp_ref, do_ref, lse_ref, dq_ref, dkv_ref, dsink_ref, carry_ref):
        i = pl.program_id(0)

        @pl.when(i == 0)
        def _():
            dsink_ref[...] = jnp.zeros_like(dsink_ref)
            carry_ref[...] = jnp.zeros_like(carry_ref)

        @pl.when(i < nb)
        def _():
            upper, dist, pen = _band_consts(i == 0)
            lo = _lo_lanes()
            dk_dup, dv_dup = [], []
            for kvh in range(NKV):
                kdup, kleft, kright = _kv_tiles(kvc_ref, kvp_ref, 0, kvh)
                vdup, _, _ = _kv_tiles(kvc_ref, kvp_ref, 1, kvh)
                qs = _stack_heads(q_ref, kvh, SCALE)
                dos = _stack_heads(do_ref, kvh)
                s_all = _dot_nt(qs, kdup)
                dp_all = _dot_nt(dos, vdup)
                ds_parts, p_parts = [], []
                for pr in range(NPAIR):
                    dq = None
                    for side, kpad in ((0, kleft), (1, kright)):
                        g = 2 * pr + side
                        hq = kvh * GQ + g
                        lse_h = lse_ref[:, hq:hq + 1]
                        rows = slice(g * BLK, (g + 1) * BLK)
                        s = jnp.where(upper, s_all[rows, 0:BLK], s_all[rows, BLK:2 * BLK]) - SLOPES[hq] * dist + pen
                        p = jnp.exp(s - lse_h)
                        dp = jnp.where(upper, dp_all[rows, 0:BLK], dp_all[rows, BLK:2 * BLK])
                        delta = jnp.sum(p * dp, axis=-1, keepdims=True)
                        dsink_ref[:, hq:hq + 1] += -jnp.sum(jnp.exp(sink_ref[0, hq] - lse_h) * delta, axis=0,
                                                            keepdims=True)
                        ds = _split_band(upper, (p * (dp - delta)).astype(BF16))
                        ds_parts.append(ds)
                        p_parts.append(_split_band(upper, p.astype(BF16)))
                        d = _dot(ds, kpad)
                        dq = d if dq is None else dq + d
                    col0 = (kvh * NPAIR + pr) * PAIR
                    dq_ref[:, col0:col0 + PAIR] = (dq * SCALE).astype(BF16)
                dkw = _dot_tn(jnp.concatenate(ds_parts, axis=0), qs)
                dvw = _dot_tn(jnp.concatenate(p_parts, axis=0), dos)
                dk_dup.append(dkw + pltpu.roll(dkw, HD, 1))
                dv_dup.append(dvw + pltpu.roll(dvw, HD, 1))
            dk = jnp.where(jnp.concatenate([lo, lo], axis=0), dk_dup[0], dk_dup[1])
            dv = jnp.where(jnp.concatenate([lo, lo], axis=0), dv_dup[0], dv_dup[1])
            dkv_ref[:, 0:PAIR] = (carry_ref[:, 0:PAIR] + dk[0:BLK]).astype(BF16)
            dkv_ref[:, PAIR:2 * PAIR] = (carry_ref[:, PAIR:2 * PAIR] + dv[0:BLK]).astype(BF16)
            carry_ref[:, 0:PAIR] = dk[BLK:2 * BLK]
            carry_ref[:, PAIR:2 * PAIR] = dv[BLK:2 * BLK]

        @pl.when(i == nb)
        def _():
            dkv_ref[...] = carry_ref[...].astype(BF16)

    cur = lambda i: (jnp.minimum(i, nb - 1), 0)
    prev = lambda i: (jnp.maximum(jnp.minimum(i, nb - 1) - 1, 0), 0)
    return pl.pallas_call(
        body, name="attn_bwd", grid=(nb + 1,),
        in_specs=[pl.BlockSpec(memory_space=pltpu.SMEM),
                  pl.BlockSpec((BLK, AW), cur), pl.BlockSpec((BLK, 2 * KVW), cur), pl.BlockSpec((BLK, 2 * KVW), prev),
                  pl.BlockSpec((BLK, AW), cur), pl.BlockSpec((BLK, NQ), cur)],
        out_specs=[pl.BlockSpec((BLK, AW), cur),
                   pl.BlockSpec((BLK, 2 * KVW), lambda i: (jnp.maximum(i - 1, 0), 0)),
                   pl.BlockSpec((1, NQ), lambda i: (0, 0))],
        out_shape=[SDS((T, AW), BF16), SDS((T, 2 * KVW), BF16), SDS((1, NQ), F32)],
        scratch_shapes=[pltpu.VMEM((BLK, 2 * KVW), F32)],
        compiler_params=_cp(("arbitrary",)),
    )(sinks, q, kv, kv, datt, lse)


def _inv_counts(t0, rows):
    t = (t0 + lax.broadcasted_iota(jnp.int32, (rows, 1), 0) + 1).astype(F32)
    return [1.0 / jnp.minimum(t, float(w)) for w in POOL_WINDOWS]


def pool_fwd(z, wmix, scale):
    T = z.shape[0]
    TM = 512
    L = TM + HALO

    def body(z_ref, halo_ref, wmix_ref, scale_ref, pooled_ref, mixs_ref):
        i = pl.program_id(0)
        halo = jnp.where(i > 0, halo_ref[...], 0.0)
        zt = z_ref[...]
        e = jnp.concatenate([halo, zt], axis=0)
        sums = []
        s = e
        for k in (1, 2, 4, 8):
            s = s + pltpu.roll(s, k, 0)
            sums.append(s)
        inv = _inv_counts(i * TM, TM)
        for gi in range(len(POOL_WINDOWS)):
            cols = slice(gi * PG, (gi + 1) * PG)
            pooled = (sums[gi][HALO:, cols] * inv[gi] - zt[:, cols]).astype(BF16)
            pooled_ref[:, cols] = pooled
            mixs_ref[:, cols] = (_dot(pooled, wmix_ref[gi]) * scale_ref[:, cols]).astype(BF16)

    return pl.pallas_call(
        body, name="pool_fwd", grid=(T // TM,),
        in_specs=[pl.BlockSpec((TM, PW), lambda i: (i, 0)),
                  pl.BlockSpec((HALO, PW), lambda i: (jnp.maximum(i * (TM // HALO) - 1, 0), 0)),
                  _const_spec((len(POOL_WINDOWS), PG, PG)), _const_spec((1, PW))],
        out_specs=[pl.BlockSpec((TM, PW), lambda i: (i, 0)), pl.BlockSpec((TM, PW), lambda i: (i, 0))],
        out_shape=[SDS((T, PW), BF16), SDS((T, PW), BF16)],
        compiler_params=_cp(("arbitrary",)),
    )(z, z, wmix, scale)


def pool_bwd(dmixs, pooled, wmix, scale):
    T = dmixs.shape[0]
    TM = 512
    L = TM + HALO
    nt = T // TM

    def body(dm_ref, halo_ref, pooled_ref, wmix_ref, scale_ref, dz_ref, dwmix_ref, dscale_ref):
        i = pl.program_id(0)

        @pl.when(i == 0)
        def _():
            dwmix_ref[...] = jnp.zeros_like(dwmix_ref)
            dscale_ref[...] = jnp.zeros_like(dscale_ref)

        halo = jnp.where(i < nt - 1, halo_ref[...], 0.0)
        dm = dm_ref[...]
        e = jnp.concatenate([dm, halo], axis=0)
        inv = _inv_counts(i * TM, L)
        for gi in range(len(POOL_WINDOWS)):
            cols = slice(gi * PG, (gi + 1) * PG)
            w = wmix_ref[gi]
            dmixed = (e[:, cols] * scale_ref[:, cols]).astype(BF16)
            dpooled = _dot_nt(dmixed, w)
            pooled = pooled_ref[:, cols]
            mixed = _dot(pooled, w)
            dscale_ref[:, cols] += jnp.sum(dm[:, cols] * mixed, axis=0, keepdims=True)
            dwmix_ref[gi] += _dot_tn(pooled, dmixed[:TM])
            s = dpooled * inv[gi]
            k = 1
            while k < POOL_WINDOWS[gi]:
                s = s + pltpu.roll(s, L - k, 0)
                k *= 2
            dz_ref[:, cols] = (s[:TM] - dpooled[:TM]).astype(BF16)

    return pl.pallas_call(
        body, name="pool_bwd", grid=(nt,),
        in_specs=[pl.BlockSpec((TM, PW), lambda i: (i, 0)),
                  pl.BlockSpec((HALO, PW), lambda i: (jnp.minimum((i + 1) * (TM // HALO), T // HALO - 1), 0)),
                  pl.BlockSpec((TM, PW), lambda i: (i, 0)),
                  _const_spec((len(POOL_WINDOWS), PG, PG)), _const_spec((1, PW))],
        out_specs=[pl.BlockSpec((TM, PW), lambda i: (i, 0)),
                   pl.BlockSpec((len(POOL_WINDOWS), PG, PG), lambda i: (0, 0, 0)),
                   pl.BlockSpec((1, PW), lambda i: (0, 0))],
        out_shape=[SDS((T, PW), BF16), SDS((len(POOL_WINDOWS), PG, PG), F32), SDS((1, PW), F32)],
        compiler_params=_cp(("arbitrary",)),
    )(dmixs, dmixs, pooled, wmix, scale)


def merge_fwd(att, mixs, gt, h1, wattn, wpool, wout):
    T = h1.shape[0]
    TM = 512

    def body(att_ref, mixs_ref, gt_ref, h_ref, wa_ref, wp_ref, wo_ref, h2_ref, mg_ref):
        a = _dot(att_ref[...], wa_ref[...])
        p = _dot(mixs_ref[...], wp_ref[...])
        merged = (jax.nn.sigmoid(gt_ref[:, 0:D].astype(F32)) * a + jax.nn.sigmoid(gt_ref[:, D:2 * D].astype(F32)) * p)
        mg = merged.astype(BF16)
        mg_ref[...] = mg
        h2_ref[...] = h_ref[...] + _dot(mg, wo_ref[...])

    tile = lambda w: pl.BlockSpec((TM, w), lambda i: (i, 0))
    return pl.pallas_call(
        body, name="merge_fwd", grid=(T // TM,),
        in_specs=[tile(AW), tile(PW), tile(2 * D), tile(D), _const_spec((AW, D)), _const_spec((PW, D)),
                  _const_spec((D, D))],
        out_specs=[tile(D), tile(D)],
        out_shape=[SDS((T, D), F32), SDS((T, D), BF16)],
        compiler_params=_cp(("arbitrary",)),
    )(att, mixs, gt, h1, wattn, wpool, wout)


def merge_bwd(dh2, att, mixs, gt, wattn, wpool, wout):
    T = dh2.shape[0]
    TM = 512

    def body(dh2_ref, att_ref, mixs_ref, gt_ref, wa_ref, wp_ref, wo_ref, datt_ref, dmixs_ref, dgt_ref, da_ref, dp_ref):
        dm = _dot_nt(dh2_ref[...].astype(BF16), wo_ref[...])
        a = _dot(att_ref[...], wa_ref[...])
        p = _dot(mixs_ref[...], wp_ref[...])
        sa = jax.nn.sigmoid(gt_ref[:, 0:D].astype(F32))
        sp = jax.nn.sigmoid(gt_ref[:, D:2 * D].astype(F32))
        da = (dm * sa).astype(BF16)
        dp = (dm * sp).astype(BF16)
        da_ref[...] = da
        dp_ref[...] = dp
        dgt_ref[:, 0:D] = (dm * a * sa * (1.0 - sa)).astype(BF16)
        dgt_ref[:, D:2 * D] = (dm * p * sp * (1.0 - sp)).astype(BF16)
        datt_ref[...] = _dot_nt(da, wa_ref[...]).astype(BF16)
        dmixs_ref[...] = _dot_nt(dp, wp_ref[...])

    tile = lambda w: pl.BlockSpec((TM, w), lambda i: (i, 0))
    return pl.pallas_call(
        body, name="merge_bwd", grid=(T // TM,),
        in_specs=[tile(D), tile(AW), tile(PW), tile(2 * D), _const_spec((AW, D)), _const_spec((PW, D)),
                  _const_spec((D, D))],
        out_specs=[tile(AW), tile(PW), tile(2 * D), tile(D), tile(D)],
        out_shape=[SDS((T, AW), BF16), SDS((T, PW), F32), SDS((T, 2 * D), BF16), SDS((T, D), BF16),
                   SDS((T, D), BF16)],
        compiler_params=_cp(("arbitrary",)),
    )(dh2, att, mixs, gt, wattn, wpool, wout)


def loss_head(h3, target, g):
    T = h3.shape[0]
    TM = 512

    def body(h_ref, t_ref, g_ref, dh_ref, loss_ref, dg_ref):
        @pl.when(pl.program_id(0) == 0)
        def _():
            loss_ref[...] = jnp.zeros_like(loss_ref)
            dg_ref[...] = jnp.zeros_like(dg_ref)

        x = h_ref[...]
        g = g_ref[...]
        rstd = _rstd(x)
        xhat = x * rstd
        err = xhat * g - t_ref[...]
        loss_ref[...] += 0.5 * jnp.sum(jnp.mean(err * err, axis=-1, keepdims=True), axis=0, keepdims=True)
        dy = err * (1.0 / D)
        dg_ref[...] += jnp.sum(dy * xhat, axis=0, keepdims=True)
        dh_ref[...] = _rms_bwd(dy, xhat, rstd, g)

    tile = pl.BlockSpec((TM, D), lambda i: (i, 0))
    return pl.pallas_call(
        body, name="loss_head", grid=(T // TM,),
        in_specs=[tile, tile, _const_spec((1, D))],
        out_specs=[tile, pl.BlockSpec((1, 1), lambda i: (0, 0)), pl.BlockSpec((1, D), lambda i: (0, 0))],
        out_shape=[SDS((T, D), F32), SDS((1, 1), F32), SDS((1, D), F32)],
        compiler_params=_cp(("arbitrary",)),
    )(h3, target, g)


def adamw(w, g, m, v, name):
    R, C = w.shape
    tile_bytes = 2 * 1024 * 1024
    tr = R
    if R * C * 4 > tile_bytes:
        tr = next(cand for cand in (512, 256, 128, 64, 32, 16, 8) if R % cand == 0 and cand * C * 4 <= tile_bytes)
    c1 = 1.0 - B1 ** STEP
    c2 = 1.0 - B2 ** STEP

    def body(w_ref, g_ref, m_ref, v_ref, d_ref, nm_ref, nv_ref):
        gv = g_ref[...]
        nm = B1 * m_ref[...] + (1.0 - B1) * gv
        nv = B2 * v_ref[...] + (1.0 - B2) * (gv * gv)
        nm_ref[...] = nm
        nv_ref[...] = nv
        d_ref[...] = -LR * ((nm / c1) / (jnp.sqrt(nv / c2) + ADAM_EPS) + WD * w_ref[...])

    spec = pl.BlockSpec((tr, C), lambda i: (i, 0))
    return pl.pallas_call(
        body, name=name, grid=(R // tr,),
        in_specs=[spec] * 4, out_specs=[spec] * 3, out_shape=[SDS((R, C), F32)] * 3,
        compiler_params=_cp(("parallel",)),
    )(w, g, m, v)


def local_fwd_bwd(x, target, W):
    h1, ab1 = ffn_fwd(x, W["ffn1_norm"], W["ffn1_w_up"], W["ffn1_w_down"], "ffn1_fwd")
    u, q, kv, z, gt = mix_in_fwd(h1, W["mix_norm"], W["w_in"])
    att, lse = attn_fwd(q, kv, W["sinks"])
    pooled, mixs = pool_fwd(z, W["pool_w_mix"], W["pool_scale"])
    h2, merged = merge_fwd(att, mixs, gt, h1, W["w_attn_up"], W["w_pool_up"], W["w_out"])
    h3, ab2 = ffn_fwd(h2, W["ffn2_norm"], W["ffn2_w_up"], W["ffn2_w_down"], "ffn2_fwd")
    dh3, loss, g_final = loss_head(h3, target, W["final_norm"])

    G = {"final_norm": g_final}
    dh2, dup2, hid2, n2, G["ffn2_norm"] = ffn_bwd_x(dh3, h2, ab2, W["ffn2_norm"], W["ffn2_w_up"], W["ffn2_w_down"],
                                                    "ffn2_bwd_x")
    G["ffn2_w_up"] = matmul_tn(n2, dup2, "ffn2_dw_up", tm=D, tn=1408)
    G["ffn2_w_down"] = matmul_tn(hid2, dh3, "ffn2_dw_down", tm=1408, tn=D, b_scale=0.5)

    datt, dmixs, dgt, da, dp = merge_bwd(dh2, att, mixs, gt, W["w_attn_up"], W["w_pool_up"], W["w_out"])
    G["w_out"] = matmul_tn(merged, dh2, "dw_out", tm=D, tn=D)
    G["w_attn_up"] = matmul_tn(att, da, "dw_attn_up", tm=AW, tn=D)
    G["w_pool_up"] = matmul_tn(mixs, dp, "dw_pool_up", tm=PW, tn=D)
    dz, G["pool_w_mix"], G["pool_scale"] = pool_bwd(dmixs, pooled, W["pool_w_mix"], W["pool_scale"])
    dq, dkv, G["sinks"] = attn_bwd(q, kv, datt, lse, W["sinks"])
    dh1, G["mix_norm"] = mix_in_bwd(dq, dkv, dz, dgt, dh2, h1, W["mix_norm"], W["w_in"])
    G["w_in"] = jnp.concatenate([
        matmul_tn(u, dq, "dw_in_q", tm=D, tn=AW),
        matmul_tn(u, dkv, "dw_in_kv", tm=D, tn=2 * KVW),
        matmul_tn(u, dz, "dw_in_z", tm=D, tn=PW),
        matmul_tn(u, dgt, "dw_in_g", tm=D, tn=D),
    ], axis=1)

    dx, dup1, hid1, n1, G["ffn1_norm"] = ffn_bwd_x(dh1, x, ab1, W["ffn1_norm"], W["ffn1_w_up"], W["ffn1_w_down"],
                                                   "ffn1_bwd_x")
    G["ffn1_w_up"] = matmul_tn(n1, dup1, "ffn1_dw_up", tm=D, tn=1408)
    G["ffn1_w_down"] = matmul_tn(hid1, dh1, "ffn1_dw_down", tm=1408, tn=D, b_scale=0.5)
    return loss, dx, G


HBM_SPEC = pl.BlockSpec(memory_space=pltpu.HBM)


def _me():
    return lax.axis_index("x"), lax.axis_index("y"), lax.axis_index("c")


def _peer_chip(x, y, k):
    return x ^ (k >> 1), y ^ (k & 1)


def all_gather_packed(shard):
    R, Wd = shard.shape
    H = R // 2

    def body(src_ref, out_ref, send_sems, recv_sems, local_sem):
        x, y, c = _me()
        chip = 2 * x + y
        mine = pl.ds(c * H, H)
        theirs = pl.ds((1 - c) * H, H)
        local = pltpu.make_async_copy(src_ref, out_ref.at[chip], local_sem)
        local.start()
        sends = []
        for k in (1, 2, 3):
            px, py = _peer_chip(x, y, k)
            cp = pltpu.make_async_remote_copy(src_ref.at[mine], out_ref.at[chip, mine], send_sems.at[k - 1],
                                              recv_sems.at[k - 1], device_id=(px, py, c), device_id_type=MESH)
            cp.start()
            sends.append(cp)
        for k in (1, 2, 3):
            px, py = _peer_chip(x, y, k)
            slot = out_ref.at[2 * px + py, mine]
            pltpu.make_async_remote_copy(slot, slot, send_sems.at[k - 1], recv_sems.at[k - 1],
                                         device_id=(px, py, c), device_id_type=MESH).wait_recv()
            fwd = pltpu.make_async_remote_copy(slot, slot, send_sems.at[2 + k], recv_sems.at[2 + k],
                                               device_id=(x, y, 1 - c), device_id_type=MESH)
            fwd.start()
            sends.append(fwd)
        for k in (1, 2, 3):
            px, py = _peer_chip(x, y, k)
            slot = out_ref.at[2 * px + py, theirs]
            pltpu.make_async_remote_copy(slot, slot, send_sems.at[2 + k], recv_sems.at[2 + k],
                                         device_id=(x, y, 1 - c), device_id_type=MESH).wait_recv()
        for cp in sends:
            cp.wait_send()
        local.wait()

    return pl.pallas_call(
        body, name="all_gather_packed", in_specs=[HBM_SPEC], out_specs=HBM_SPEC,
        out_shape=SDS((N_CHIPS, R, Wd), shard.dtype),
        scratch_shapes=[pltpu.SemaphoreType.DMA((6,)), pltpu.SemaphoreType.DMA((6,)), pltpu.SemaphoreType.DMA],
    )(shard)


def swap_halves(p):
    _, _, H, Wd = p.shape

    def body(p_ref, got_ref, send_sem, recv_sem):
        x, y, c = _me()
        cp = pltpu.make_async_remote_copy(p_ref.at[:, 1 - c], got_ref, send_sem, recv_sem,
                                          device_id=(x, y, 1 - c), device_id_type=MESH)
        cp.start()
        cp.wait()

    return pl.pallas_call(
        body, name="swap_halves", in_specs=[HBM_SPEC], out_specs=HBM_SPEC,
        out_shape=SDS((N_CHIPS, H, Wd), p.dtype),
        scratch_shapes=[pltpu.SemaphoreType.DMA, pltpu.SemaphoreType.DMA],
    )(p)


def add_halves(p, got, c_arr):
    _, _, H, Wd = p.shape
    tr = 416
    assert H % tr == 0

    def body(c_ref, p_ref, g_ref, o_ref):
        o_ref[...] = (p_ref[...] + g_ref[...]).astype(BF16)

    return pl.pallas_call(
        body, name="add_halves",
        grid_spec=pltpu.PrefetchScalarGridSpec(
            num_scalar_prefetch=1, grid=(N_CHIPS, H // tr),
            in_specs=[pl.BlockSpec((None, None, tr, Wd), lambda j, r, c_ref: (j, c_ref[0], r, 0)),
                      pl.BlockSpec((None, tr, Wd), lambda j, r, c_ref: (j, r, 0))],
            out_specs=pl.BlockSpec((None, tr, Wd), lambda j, r, c_ref: (j, r, 0))),
        out_shape=SDS((N_CHIPS, H, Wd), BF16),
        compiler_params=_cp(("parallel", "parallel")),
    )(c_arr, p, got)


def chip_all_to_all(q):
    _, H, Wd = q.shape

    def body(q_ref, out_ref, send_sems, recv_sems, local_sem):
        x, y, c = _me()
        chip = 2 * x + y
        local = pltpu.make_async_copy(q_ref.at[chip], out_ref.at[chip], local_sem)
        local.start()
        sends = []
        for k in (1, 2, 3):
            px, py = _peer_chip(x, y, k)
            cp = pltpu.make_async_remote_copy(q_ref.at[2 * px + py], out_ref.at[chip], send_sems.at[k - 1],
                                              recv_sems.at[k - 1], device_id=(px, py, c), device_id_type=MESH)
            cp.start()
            sends.append(cp)
        for k in (1, 2, 3):
            px, py = _peer_chip(x, y, k)
            slot = out_ref.at[2 * px + py]
            pltpu.make_async_remote_copy(slot, slot, send_sems.at[k - 1], recv_sems.at[k - 1],
                                         device_id=(px, py, c), device_id_type=MESH).wait_recv()
        for cp in sends:
            cp.wait_send()
        local.wait()

    return pl.pallas_call(
        body, name="chip_all_to_all", in_specs=[HBM_SPEC], out_specs=HBM_SPEC,
        out_shape=SDS(q.shape, q.dtype),
        scratch_shapes=[pltpu.SemaphoreType.DMA((3,)), pltpu.SemaphoreType.DMA((3,)), pltpu.SemaphoreType.DMA],
    )(q)


def sum_slots(r):
    _, H, Wd = r.shape
    tr = 416
    assert H % tr == 0

    def body(r_ref, o_ref):
        acc = r_ref[0].astype(F32)
        for j in range(1, N_CHIPS):
            acc = acc + r_ref[j].astype(F32)
        o_ref[...] = acc

    return pl.pallas_call(
        body, name="sum_slots", grid=(H // tr,),
        in_specs=[pl.BlockSpec((N_CHIPS, tr, Wd), lambda r: (0, r, 0))],
        out_specs=pl.BlockSpec((tr, Wd), lambda r: (r, 0)),
        out_shape=SDS((H, Wd), F32),
        compiler_params=_cp(("parallel",)),
    )(r)


def join_halves(half):
    H, Wd = half.shape

    def body(h_ref, out_ref, send_sem, recv_sem, local_sem):
        x, y, c = _me()
        local = pltpu.make_async_copy(h_ref, out_ref.at[c], local_sem)
        local.start()
        cp = pltpu.make_async_remote_copy(h_ref, out_ref.at[c], send_sem, recv_sem,
                                          device_id=(x, y, 1 - c), device_id_type=MESH)
        cp.start()
        got = out_ref.at[1 - c]
        pltpu.make_async_remote_copy(got, got, send_sem, recv_sem, device_id=(x, y, 1 - c),
                                     device_id_type=MESH).wait_recv()
        cp.wait_send()
        local.wait()

    return pl.pallas_call(
        body, name="join_halves", in_specs=[HBM_SPEC], out_specs=HBM_SPEC,
        out_shape=SDS((2, H, Wd), half.dtype),
        scratch_shapes=[pltpu.SemaphoreType.DMA, pltpu.SemaphoreType.DMA, pltpu.SemaphoreType.DMA],
    )(half)


N_DEV = 8


def all_reduce_small(s):
    R, Wd = s.shape

    def body(s_ref, out_ref, slots_ref, send_sems, recv_sems):
        x, y, c = _me()
        me = 4 * x + 2 * y + c
        slots_ref[me] = s_ref[...]
        sends = []
        for k in range(1, N_DEV):
            peer = (x ^ (k >> 2), y ^ ((k >> 1) & 1), c ^ (k & 1))
            cp = pltpu.make_async_remote_copy(s_ref, slots_ref.at[me], send_sems.at[k - 1], recv_sems.at[k - 1],
                                              device_id=peer, device_id_type=MESH)
            cp.start()
            sends.append(cp)
        for k in range(1, N_DEV):
            peer = (x ^ (k >> 2), y ^ ((k >> 1) & 1), c ^ (k & 1))
            slot = slots_ref.at[4 * peer[0] + 2 * peer[1] + peer[2]]
            pltpu.make_async_remote_copy(s_ref, slot, send_sems.at[k - 1], recv_sems.at[k - 1],
                                         device_id=peer, device_id_type=MESH).wait_recv()
        for cp in sends:
            cp.wait_send()
        acc = slots_ref[0]
        for d in range(1, N_DEV):
            acc = acc + slots_ref[d]
        out_ref[...] = acc

    vmem = pl.BlockSpec(memory_space=pltpu.VMEM)
    return pl.pallas_call(
        body, name="all_reduce_small", in_specs=[vmem], out_specs=vmem, out_shape=SDS((R, Wd), F32),
        scratch_shapes=[pltpu.VMEM((N_DEV, R, Wd), F32), pltpu.SemaphoreType.DMA((N_DEV - 1,)),
                        pltpu.SemaphoreType.DMA((N_DEV - 1,))],
    )(s)


BIG = (("ffn1_w_up", D, 2 * FF, 1), ("ffn1_w_down", FF, D, 0), ("w_in", D, INW, 1), ("w_attn_up", AW, D, 0),
       ("w_pool_up", PW, D, 1), ("w_out", D, D, 0), ("ffn2_w_up", D, 2 * FF, 1), ("ffn2_w_down", FF, D, 0))
SMALL = ("ffn1_norm", "mix_norm", "ffn2_norm", "final_norm", "pool_scale", "sinks", "pool_w_mix")
SMALL_W = 128


def _shard_shape(k, n, axis):
    return (k // N_CHIPS, n) if axis == 0 else (k, n // N_CHIPS)


def pack_shards(shards):
    return jnp.concatenate([shards[name].reshape(-1, PACK_W) for name, *_ in BIG], axis=0)


def unpack_shards(packed):
    out, r0 = {}, 0
    for name, k, n, axis in BIG:
        ks, ns = _shard_shape(k, n, axis)
        rows = ks * ns // PACK_W
        out[name] = packed[r0:r0 + rows].reshape(ks, ns)
        r0 += rows
    return out


def unpack_gathered(g):
    out, r0 = {}, 0
    for name, k, n, axis in BIG:
        ks, ns = _shard_shape(k, n, axis)
        rows = ks * ns // PACK_W
        piece = g[:, r0:r0 + rows].reshape(N_CHIPS, ks, ns)
        out[name] = piece.reshape(k, n) if axis == 0 else piece.transpose(1, 0, 2).reshape(k, n)
        r0 += rows
    return out


def pack_full(full):
    parts = []
    for name, k, n, axis in BIG:
        ks, ns = _shard_shape(k, n, axis)
        a = full[name]
        a = a.reshape(N_CHIPS, ks, ns) if axis == 0 else a.reshape(k, N_CHIPS, ns).transpose(1, 0, 2)
        parts.append(a.reshape(N_CHIPS, -1, PACK_W))
    return jnp.concatenate(parts, axis=1)


def pack_small(d):
    parts = []
    for name in SMALL:
        a = d[name].reshape(-1)
        pad = (-a.shape[0]) % SMALL_W
        parts.append(jnp.pad(a, (0, pad)).reshape(-1, SMALL_W))
    a = jnp.concatenate(parts, axis=0)
    return jnp.pad(a, ((0, (-a.shape[0]) % 8), (0, 0)))


def unpack_small(a, like):
    out, r0 = {}, 0
    for name in SMALL:
        size = int(np.prod(like[name].shape))
        rows = -(-size // SMALL_W)
        out[name] = a[r0:r0 + rows].reshape(-1)[:size].reshape(like[name].shape)
        r0 += rows
    return out


WEIGHTS = ("ffn1_norm", "ffn1_w_up", "ffn1_w_down", "mix_norm", "w_in", "sinks", "w_attn_up", "pool_w_mix",
           "pool_scale", "w_pool_up", "w_out", "ffn2_norm", "ffn2_w_up", "ffn2_w_down", "final_norm")


def kernel(x, ffn1_norm, ffn1_w_up, ffn1_w_down, mix_norm, w_in, sinks, w_attn_up, pool_w_mix, pool_scale, w_pool_up, w_out, ffn2_norm, ffn2_w_up, ffn2_w_down, final_norm, loss_target, m_ffn1_norm, m_ffn1_w_up, m_ffn1_w_down, m_mix_norm, m_w_in, m_sinks, m_w_attn_up, m_pool_w_mix, m_pool_scale, m_w_pool_up, m_w_out, m_ffn2_norm, m_ffn2_w_up, m_ffn2_w_down, m_final_norm, v_ffn1_norm, v_ffn1_w_up, v_ffn1_w_down, v_mix_norm, v_w_in, v_sinks, v_w_attn_up, v_pool_w_mix, v_pool_scale, v_w_pool_up, v_w_out, v_ffn2_norm, v_ffn2_w_up, v_ffn2_w_down, v_final_norm):
    given = dict(locals())
    w = {n: given[n] for n in WEIGHTS}
    m = {n: given["m_" + n] for n in WEIGHTS}
    v = {n: given["v_" + n] for n in WEIGHTS}
    big_names = [name for name, *_ in BIG]

    shards = {n: w[n][0] for n in big_names}
    gathered = all_gather_packed(pack_shards({n: shards[n].astype(BF16) for n in big_names}))
    W = unpack_gathered(gathered)
    for n in ("ffn1_norm", "mix_norm", "ffn2_norm", "final_norm", "pool_scale", "sinks"):
        W[n] = w[n].reshape(1, -1)
    W["pool_w_mix"] = w["pool_w_mix"][0].astype(BF16)

    loss, dx, G = local_fwd_bwd(x[0], loss_target[0], W)
    loss = lax.psum(loss[0, 0], ("x", "y", "c"))

    c_arr = lax.axis_index("c").astype(jnp.int32).reshape(1)
    p = pack_full(G)
    rows = p.shape[1]
    p = p.reshape(N_CHIPS, 2, rows // 2, PACK_W)
    q = add_halves(p, swap_halves(p), c_arr)
    reduced = join_halves(sum_slots(chip_all_to_all(q))).reshape(rows, PACK_W)
    grads = unpack_shards(reduced)

    small_like = {n: w[n] for n in SMALL}
    grads.update(unpack_small(all_reduce_small(pack_small({n: G[n] for n in SMALL})), small_like))

    delta, new_m, new_v = {}, {}, {}
    for n in big_names:
        delta[n], new_m[n], new_v[n] = adamw(shards[n], grads[n], m[n][0], v[n][0], "adamw_" + n)
    ds, ms, vs = adamw(pack_small(w), pack_small(grads), pack_small(m), pack_small(v), "adamw_small")
    delta.update(unpack_small(ds, small_like))
    new_m.update(unpack_small(ms, small_like))
    new_v.update(unpack_small(vs, small_like))

    def shaped(d, n):
        return d[n].reshape(w[n].shape)

    return (loss, dx[None], *[shaped(grads, n) for n in WEIGHTS], *[shaped(delta, n) for n in WEIGHTS],
            *[shaped(new_m, n) for n in WEIGHTS], *[shaped(new_v, n) for n in WEIGHTS])
```

```python
import functools

import numpy as np
import jax
import jax.numpy as jnp
from jax import lax
from jax.experimental import pallas as pl
from jax.experimental.pallas import tpu as pltpu

F32 = jnp.float32
BF16 = jnp.bfloat16
SDS = jax.ShapeDtypeStruct
MESH = pl.DeviceIdType.MESH

D = 1024
FF = 2816
NQ = 16
NKV = 2
HD = 64
GQ = NQ // NKV
AW = NQ * HD
KVW = NKV * HD
BLK = 128
PW = 512
PG = 128
POOL_WINDOWS = (2, 4, 8, 16)
HALO = 16
INW = AW + 2 * KVW + PW + 2 * D
C_KV = AW
C_Z = AW + 2 * KVW
C_G = C_Z + PW
EPS = 1e-6
FF_CHUNK = 256
FF_CHUNKS = tuple((c, FF_CHUNK) for c in range(0, FF, FF_CHUNK))
SLOPES = tuple(float(2.0 ** (-8.0 * h / NQ)) for h in range(1, NQ + 1))
SCALE = HD ** -0.5

LR, B1, B2, ADAM_EPS, WD, STEP = 0.001, 0.9, 0.999, 1e-08, 0.01, 10

VMEM_LIMIT = 56 * 1024 * 1024
PACK_W = 1024
N_CHIPS = 4

NT = (((1,), (1,)), ((), ()))
TN = (((0,), (0,)), ((), ()))


def _cp(sem=None, vmem=VMEM_LIMIT):
    return pltpu.CompilerParams(dimension_semantics=sem, vmem_limit_bytes=vmem)


def _const_spec(shape):
    nd = len(shape)
    return pl.BlockSpec(shape, lambda *_: (0,) * nd, pipeline_mode=pl.Buffered(1))


def _rstd(x):
    return lax.rsqrt(jnp.mean(x * x, axis=-1, keepdims=True) + EPS)


def _rms_bwd(dn, xhat, rstd, g):
    dxhat = dn * g
    return rstd * (dxhat - xhat * jnp.mean(dxhat * xhat, axis=-1, keepdims=True))


def _dot(a, b):
    return jnp.dot(a, b, preferred_element_type=F32)


def _dot_nt(a, b):
    return lax.dot_general(a, b, NT, preferred_element_type=F32)


def _dot_tn(a, b):
    return lax.dot_general(a, b, TN, preferred_element_type=F32)


def ffn_fwd(h, g, wup, wdn, name):
    T = h.shape[0]
    TM = 512

    def body(h_ref, g_ref, wup_ref, wdn_ref, out_ref, ab_ref, hid_ref):
        x = h_ref[...]
        n = (x * _rstd(x) * g_ref[...]).astype(BF16)
        for c0, w in FF_CHUNKS:
            a = _dot(n, wup_ref[:, c0:c0 + w])
            b = _dot(n, wup_ref[:, FF + c0:FF + c0 + w])
            ab_ref[:, c0:c0 + w] = a.astype(BF16)
            ab_ref[:, FF + c0:FF + c0 + w] = b.astype(BF16)
            hid_ref[:, c0:c0 + w] = (a * jax.nn.sigmoid(a) * b).astype(BF16)
        out_ref[...] = x + 0.5 * _dot(hid_ref[...], wdn_ref[...])

    return pl.pallas_call(
        body, name=name, grid=(T // TM,),
        in_specs=[pl.BlockSpec((TM, D), lambda i: (i, 0)), _const_spec((1, D)),
                  _const_spec((D, 2 * FF)), _const_spec((FF, D))],
        out_specs=[pl.BlockSpec((TM, D), lambda i: (i, 0)), pl.BlockSpec((TM, 2 * FF), lambda i: (i, 0)),
                   pl.BlockSpec((TM, FF), lambda i: (i, 0))],
        out_shape=[SDS((T, D), F32), SDS((T, 2 * FF), BF16), SDS((T, FF), BF16)],
        compiler_params=_cp(("arbitrary",)),
    )(h, g, wup, wdn)


def ffn_bwd_x(dh, h_in, ab, g, wup, wdn, name):
    T = dh.shape[0]
    TM = 256

    def body(dh_ref, h_ref, ab_ref, g_ref, wup_ref, wdn_ref, dhin_ref, dup_ref, n_ref, dg_ref):
        x = h_ref[...]
        g = g_ref[...]
        rstd = _rstd(x)
        xhat = x * rstd
        n_ref[...] = (xhat * g).astype(BF16)
        dh = dh_ref[...]
        dhh = (0.5 * dh).astype(BF16)
        for c0, w in FF_CHUNKS:
            dhid = _dot_nt(dhh, wdn_ref[c0:c0 + w, :])
            a = ab_ref[:, c0:c0 + w].astype(F32)
            b = ab_ref[:, FF + c0:FF + c0 + w].astype(F32)
            sig = jax.nn.sigmoid(a)
            s = a * sig
            dup_ref[:, c0:c0 + w] = (dhid * b * (sig * (1.0 + a * (1.0 - sig)))).astype(BF16)
            dup_ref[:, FF + c0:FF + c0 + w] = (dhid * s).astype(BF16)
        dn = _dot_nt(dup_ref[...], wup_ref[...])
        dhin_ref[...] = dh + _rms_bwd(dn, xhat, rstd, g)

        @pl.when(pl.program_id(0) == 0)
        def _():
            dg_ref[...] = jnp.zeros_like(dg_ref)

        dg_ref[...] += jnp.sum(dn * xhat, axis=0, keepdims=True)

    tile = lambda w: pl.BlockSpec((TM, w), lambda i: (i, 0))
    return pl.pallas_call(
        body, name=name, grid=(T // TM,),
        in_specs=[tile(D), tile(D), tile(2 * FF), _const_spec((1, D)), _const_spec((D, 2 * FF)), _const_spec((FF, D))],
        out_specs=[tile(D), tile(2 * FF), tile(D), pl.BlockSpec((1, D), lambda i: (0, 0))],
        out_shape=[SDS((T, D), F32), SDS((T, 2 * FF), BF16), SDS((T, D), BF16), SDS((1, D), F32)],
        compiler_params=_cp(("arbitrary",)),
    )(dh, h_in, ab, g, wup, wdn)


def matmul_tn(a, b, name, *, tm, tn, tt=1024, b_scale=None):
    T, M = a.shape
    N = b.shape[1]
    assert M % tm == 0 and N % tn == 0 and T % tt == 0

    def body(a_ref, b_ref, o_ref):
        @pl.when(pl.program_id(2) == 0)
        def _():
            o_ref[...] = jnp.zeros_like(o_ref)

        bv = b_ref[...]
        if b_scale is not None:
            bv = bv * b_scale
        o_ref[...] += _dot_tn(a_ref[...].astype(BF16), bv.astype(BF16))

    return pl.pallas_call(
        body, name=name, grid=(M // tm, N // tn, T // tt),
        in_specs=[pl.BlockSpec((tt, tm), lambda i, j, t: (t, i)), pl.BlockSpec((tt, tn), lambda i, j, t: (t, j))],
        out_specs=pl.BlockSpec((tm, tn), lambda i, j, t: (i, j)),
        out_shape=SDS((M, N), F32),
        compiler_params=_cp(("parallel", "parallel", "arbitrary")),
    )(a, b)


def mix_in_fwd(h1, g, win):
    T = h1.shape[0]
    TM = 512

    def body(h_ref, g_ref, w_ref, u_ref, q_ref, kv_ref, z_ref, gt_ref):
        x = h_ref[...]
        u = (x * _rstd(x) * g_ref[...]).astype(BF16)
        u_ref[...] = u
        for c in range(0, AW, 256):
            q_ref[:, c:c + 256] = _dot(u, w_ref[:, c:c + 256]).astype(BF16)
        kv_ref[...] = _dot(u, w_ref[:, C_KV:C_Z]).astype(BF16)
        for c in range(0, PW, 256):
            z_ref[:, c:c + 256] = _dot(u, w_ref[:, C_Z + c:C_Z + c + 256])
        for c in range(0, 2 * D, 256):
            gt_ref[:, c:c + 256] = _dot(u, w_ref[:, C_G + c:C_G + c + 256]).astype(BF16)

    tile = lambda w: pl.BlockSpec((TM, w), lambda i: (i, 0))
    return pl.pallas_call(
        body, name="mix_in_fwd", grid=(T // TM,),
        in_specs=[tile(D), _const_spec((1, D)), _const_spec((D, INW))],
        out_specs=[tile(D), tile(AW), tile(2 * KVW), tile(PW), tile(2 * D)],
        out_shape=[SDS((T, D), BF16), SDS((T, AW), BF16), SDS((T, 2 * KVW), BF16), SDS((T, PW), F32),
                   SDS((T, 2 * D), BF16)],
        compiler_params=_cp(("arbitrary",)),
    )(h1, g, win)


def mix_in_bwd(dq, dkv, dz, dgt, dh2, h1, g, win):
    T = h1.shape[0]
    TM = 512

    def body(dq_ref, dkv_ref, dz_ref, dgt_ref, dh2_ref, h_ref, g_ref, w_ref, dh1_ref, dg_ref):
        du = _dot_nt(dq_ref[...], w_ref[:, 0:AW])
        du += _dot_nt(dkv_ref[...], w_ref[:, C_KV:C_Z])
        du += _dot_nt(dz_ref[...], w_ref[:, C_Z:C_G])
        du += _dot_nt(dgt_ref[...], w_ref[:, C_G:INW])
        x = h_ref[...]
        g = g_ref[...]
        rstd = _rstd(x)
        xhat = x * rstd
        dh1_ref[...] = dh2_ref[...] + _rms_bwd(du, xhat, rstd, g)

        @pl.when(pl.program_id(0) == 0)
        def _():
            dg_ref[...] = jnp.zeros_like(dg_ref)

        dg_ref[...] += jnp.sum(du * xhat, axis=0, keepdims=True)

    tile = lambda w: pl.BlockSpec((TM, w), lambda i: (i, 0))
    return pl.pallas_call(
        body, name="mix_in_bwd", grid=(T // TM,),
        in_specs=[tile(AW), tile(2 * KVW), tile(PW), tile(2 * D), tile(D), tile(D), _const_spec((1, D)),
                  _const_spec((D, INW))],
        out_specs=[tile(D), pl.BlockSpec((1, D), lambda i: (0, 0))],
        out_shape=[SDS((T, D), F32), SDS((1, D), F32)],
        compiler_params=_cp(("arbitrary",)),
    )(dq, dkv, dz, dgt, dh2, h1, g, win)


PAIR = 2 * HD
NPAIR = GQ // 2


def _lo_lanes():
    return lax.broadcasted_iota(jnp.int32, (BLK, PAIR), 1) < HD


def _stack_heads(ref, kvh, scale=None):
    lo = _lo_lanes()
    parts = []
    for pr in range(NPAIR):
        t = ref[:, (kvh * NPAIR + pr) * PAIR:(kvh * NPAIR + pr + 1) * PAIR]
        if scale is not None:
            t = t * scale
        zero = jnp.zeros_like(t)
        parts += [jnp.where(lo, t, zero), jnp.where(lo, zero, t)]
    return jnp.concatenate(parts, axis=0)


def _kv_tiles(kvc_ref, kvp_ref, tile, kvh):
    lo = _lo_lanes()
    dup, left, right = [], [], []
    for ref in (kvp_ref, kvc_ref):
        t = ref[:, tile * PAIR:(tile + 1) * PAIR]
        r = pltpu.roll(t.astype(F32), HD, 1).astype(BF16)
        zero = jnp.zeros_like(t)
        a, b = (t, r) if kvh == 0 else (r, t)
        dup.append(jnp.where(lo, a, b))
        left.append(jnp.where(lo, a, zero))
        right.append(jnp.where(lo, zero, b))
    cat = lambda xs: jnp.concatenate(xs, axis=0)
    return cat(dup), cat(left), cat(right)


def _band_consts(first):
    row = lax.broadcasted_iota(jnp.int32, (BLK, BLK), 0)
    col = lax.broadcasted_iota(jnp.int32, (BLK, BLK), 1)
    upper = col > row
    dist = jnp.where(upper, row - col + BLK, row - col).astype(F32)
    pen = jnp.where(jnp.logical_and(upper, first), -jnp.inf, 0.0)
    return upper, dist, pen


def _split_band(upper, t):
    zero = jnp.zeros_like(t)
    return jnp.concatenate([jnp.where(upper, t, zero), jnp.where(upper, zero, t)], axis=1)


def attn_fwd(q, kv, sinks):
    T = q.shape[0]
    nb = T // BLK

    def body(sink_ref, q_ref, kvc_ref, kvp_ref, att_ref, lse_ref):
        upper, dist, pen = _band_consts(pl.program_id(0) == 0)
        for kvh in range(NKV):
            kdup, _, _ = _kv_tiles(kvc_ref, kvp_ref, 0, kvh)
            _, vleft, vright = _kv_tiles(kvc_ref, kvp_ref, 1, kvh)
            s_all = _dot_nt(_stack_heads(q_ref, kvh, SCALE), kdup)
            for pr in range(NPAIR):
                out = None
                for side, vpad in ((0, vleft), (1, vright)):
                    g = 2 * pr + side
                    hq = kvh * GQ + g
                    sink = sink_ref[0, hq]
                    rows = slice(g * BLK, (g + 1) * BLK)
                    s = jnp.where(upper, s_all[rows, 0:BLK], s_all[rows, BLK:2 * BLK]) - SLOPES[hq] * dist + pen
                    m = jnp.maximum(jnp.max(s, axis=-1, keepdims=True), sink)
                    p = jnp.exp(s - m)
                    l = jnp.sum(p, axis=-1, keepdims=True) + jnp.exp(sink - m)
                    lse_ref[:, hq:hq + 1] = m + jnp.log(l)
                    o = _dot(_split_band(upper, (p * (1.0 / l)).astype(BF16)), vpad)
                    out = o if out is None else out + o
                col0 = (kvh * NPAIR + pr) * PAIR
                att_ref[:, col0:col0 + PAIR] = out.astype(BF16)

    return pl.pallas_call(
        body, name="attn_fwd", grid=(nb,),
        in_specs=[pl.BlockSpec(memory_space=pltpu.SMEM),
                  pl.BlockSpec((BLK, AW), lambda i: (i, 0)),
                  pl.BlockSpec((BLK, 2 * KVW), lambda i: (i, 0)),
                  pl.BlockSpec((BLK, 2 * KVW), lambda i: (jnp.maximum(i - 1, 0), 0))],
        out_specs=[pl.BlockSpec((BLK, AW), lambda i: (i, 0)), pl.BlockSpec((BLK, NQ), lambda i: (i, 0))],
        out_shape=[SDS((T, AW), BF16), SDS((T, NQ), F32)],
        compiler_params=_cp(("arbitrary",)),
    )(sinks, q, kv, kv)


def attn_bwd(q, kv, datt, lse, sinks):
    T = q.shape[0]
    nb = T // BLK

    def body(sink_ref, q_ref, kvc_ref, kvp_ref, do_ref, lse_ref, dq_ref, dkv_ref, dsink_ref, carry_ref):
        i = pl.program_id(0)

        @pl.when(i == 0)
        def _():
            dsink_ref[...] = jnp.zeros_like(dsink_ref)
            carry_ref[...] = jnp.zeros_like(carry_ref)

        @pl.when(i < nb)
        def _():
            upper, dist, pen = _band_consts(i == 0)
            lo = _lo_lanes()
            dk_dup, dv_dup = [], []
            for kvh in range(NKV):
                kdup, kleft, kright = _kv_tiles(kvc_ref, kvp_ref, 0, kvh)
                vdup, _, _ = _kv_tiles(kvc_ref, kvp_ref, 1, kvh)
                qs = _stack_heads(q_ref, kvh, SCALE)
                dos = _stack_heads(do_ref, kvh)
                s_all = _dot_nt(qs, kdup)
                dp_all = _dot_nt(dos, vdup)
                ds_parts, p_parts = [], []
                for pr in range(NPAIR):
                    dq = None
                    for side, kpad in ((0, kleft), (1, kright)):
                        g = 2 * pr + side
                        hq = kvh * GQ + g
                        lse_h = lse_ref[:, hq:hq + 1]
                        rows = slice(g * BLK, (g + 1) * BLK)
                        s = jnp.where(upper, s_all[rows, 0:BLK], s_all[rows, BLK:2 * BLK]) - SLOPES[hq] * dist + pen
                        p = jnp.exp(s - lse_h)
                        dp = jnp.where(upper, dp_all[rows, 0:BLK], dp_all[rows, BLK:2 * BLK])
                        delta = jnp.sum(p * dp, axis=-1, keepdims=True)
                        dsink_ref[:, hq:hq + 1] += -jnp.sum(jnp.exp(sink_ref[0, hq] - lse_h) * delta, axis=0,
                                                            keepdims=True)
                        ds = _split_band(upper, (p * (dp - delta)).astype(BF16))
                        ds_parts.append(ds)
                        p_parts.append(_split_band(upper, p.astype(BF16)))
                        d = _dot(ds, kpad)
                        dq = d if dq is None else dq + d
                    col0 = (kvh * NPAIR + pr) * PAIR
                    dq_ref[:, col0:col0 + PAIR] = (dq * SCALE).astype(BF16)
                dkw = _dot_tn(jnp.concatenate(ds_parts, axis=0), qs)
                dvw = _dot_tn(jnp.concatenate(p_parts, axis=0), dos)
                dk_dup.append(dkw + pltpu.roll(dkw, HD, 1))
                dv_dup.append(dvw + pltpu.roll(dvw, HD, 1))
            dk = jnp.where(jnp.concatenate([lo, lo], axis=0), dk_dup[0], dk_dup[1])
            dv = jnp.where(jnp.concatenate([lo, lo], axis=0), dv_dup[0], dv_dup[1])
            dkv_ref[:, 0:PAIR] = (carry_ref[:, 0:PAIR] + dk[0:BLK]).astype(BF16)
            dkv_ref[:, PAIR:2 * PAIR] = (carry_ref[:, PAIR:2 * PAIR] + dv[0:BLK]).astype(BF16)
            carry_ref[:, 0:PAIR] = dk[BLK:2 * BLK]
            carry_ref[:, PAIR:2 * PAIR] = dv[BLK:2 * BLK]

        @pl.when(i == nb)
        def _():
            dkv_ref[...] = carry_ref[...].astype(BF16)

    cur = lambda i: (jnp.minimum(i, nb - 1), 0)
    prev = lambda i: (jnp.maximum(jnp.minimum(i, nb - 1) - 1, 0), 0)
    return pl.pallas_call(
        body, name="attn_bwd", grid=(nb + 1,),
        in_specs=[pl.BlockSpec(memory_space=pltpu.SMEM),
                  pl.BlockSpec((BLK, AW), cur), pl.BlockSpec((BLK, 2 * KVW), cur), pl.BlockSpec((BLK, 2 * KVW), prev),
                  pl.BlockSpec((BLK, AW), cur), pl.BlockSpec((BLK, NQ), cur)],
        out_specs=[pl.BlockSpec((BLK, AW), cur),
                   pl.BlockSpec((BLK, 2 * KVW), lambda i: (jnp.maximum(i - 1, 0), 0)),
                   pl.BlockSpec((1, NQ), lambda i: (0, 0))],
        out_shape=[SDS((T, AW), BF16), SDS((T, 2 * KVW), BF16), SDS((1, NQ), F32)],
        scratch_shapes=[pltpu.VMEM((BLK, 2 * KVW), F32)],
        compiler_params=_cp(("arbitrary",)),
    )(sinks, q, kv, kv, datt, lse)


def _inv_counts(t0, rows):
    t = (t0 + lax.broadcasted_iota(jnp.int32, (rows, 1), 0) + 1).astype(F32)
    return [1.0 / jnp.minimum(t, float(w)) for w in POOL_WINDOWS]


def pool_fwd(z, wmix, scale):
    T = z.shape[0]
    TM = 512
    L = TM + HALO

    def body(z_ref, halo_ref, wmix_ref, scale_ref, pooled_ref, mixs_ref):
        i = pl.program_id(0)
        halo = jnp.where(i > 0, halo_ref[...], 0.0)
        zt = z_ref[...]
        e = jnp.concatenate([halo, zt], axis=0)
        sums = []
        s = e
        for k in (1, 2, 4, 8):
            s = s + pltpu.roll(s, k, 0)
            sums.append(s)
        inv = _inv_counts(i * TM, TM)
        for gi in range(len(POOL_WINDOWS)):
            cols = slice(gi * PG, (gi + 1) * PG)
            pooled = (sums[gi][HALO:, cols] * inv[gi] - zt[:, cols]).astype(BF16)
            pooled_ref[:, cols] = pooled
            mixs_ref[:, cols] = (_dot(pooled, wmix_ref[gi]) * scale_ref[:, cols]).astype(BF16)

    return pl.pallas_call(
        body, name="pool_fwd", grid=(T // TM,),
        in_specs=[pl.BlockSpec((TM, PW), lambda i: (i, 0)),
                  pl.BlockSpec((HALO, PW), lambda i: (jnp.maximum(i * (TM // HALO) - 1, 0), 0)),
                  _const_spec((len(POOL_WINDOWS), PG, PG)), _const_spec((1, PW))],
        out_specs=[pl.BlockSpec((TM, PW), lambda i: (i, 0)), pl.BlockSpec((TM, PW), lambda i: (i, 0))],
        out_shape=[SDS((T, PW), BF16), SDS((T, PW), BF16)],
        compiler_params=_cp(("arbitrary",)),
    )(z, z, wmix, scale)


def pool_bwd(dmixs, pooled, wmix, scale):
    T = dmixs.shape[0]
    TM = 512
    L = TM + HALO
    nt = T // TM

    def body(dm_ref, halo_ref, pooled_ref, wmix_ref, scale_ref, dz_ref, dwmix_ref, dscale_ref):
        i = pl.program_id(0)

        @pl.when(i == 0)
        def _():
            dwmix_ref[...] = jnp.zeros_like(dwmix_ref)
            dscale_ref[...] = jnp.zeros_like(dscale_ref)

        halo = jnp.where(i < nt - 1, halo_ref[...], 0.0)
        dm = dm_ref[...]
        e = jnp.concatenate([dm, halo], axis=0)
        inv = _inv_counts(i * TM, L)
        for gi in range(len(POOL_WINDOWS)):
            cols = slice(gi * PG, (gi + 1) * PG)
            w = wmix_ref[gi]
            dmixed = (e[:, cols] * scale_ref[:, cols]).astype(BF16)
            dpooled = _dot_nt(dmixed, w)
            pooled = pooled_ref[:, cols]
            mixed = _dot(pooled, w)
            dscale_ref[:, cols] += jnp.sum(dm[:, cols] * mixed, axis=0, keepdims=True)
            dwmix_ref[gi] += _dot_tn(pooled, dmixed[:TM])
            s = dpooled * inv[gi]
            k = 1
            while k < POOL_WINDOWS[gi]:
                s = s + pltpu.roll(s, L - k, 0)
                k *= 2
            dz_ref[:, cols] = (s[:TM] - dpooled[:TM]).astype(BF16)

    return pl.pallas_call(
        body, name="pool_bwd", grid=(nt,),
        in_specs=[pl.BlockSpec((TM, PW), lambda i: (i, 0)),
                  pl.BlockSpec((HALO, PW), lambda i: (jnp.minimum((i + 1) * (TM // HALO), T // HALO - 1), 0)),
                  pl.BlockSpec((TM, PW), lambda i: (i, 0)),
                  _const_spec((len(POOL_WINDOWS), PG, PG)), _const_spec((1, PW))],
        out_specs=[pl.BlockSpec((TM, PW), lambda i: (i, 0)),
                   pl.BlockSpec((len(POOL_WINDOWS), PG, PG), lambda i: (0, 0, 0)),
                   pl.BlockSpec((1, PW), lambda i: (0, 0))],
        out_shape=[SDS((T, PW), BF16), SDS((len(POOL_WINDOWS), PG, PG), F32), SDS((1, PW), F32)],
        compiler_params=_cp(("arbitrary",)),
    )(dmixs, dmixs, pooled, wmix, scale)


def merge_fwd(att, mixs, gt, h1, wattn, wpool, wout):
    T = h1.shape[0]
    TM = 512

    def body(att_ref, mixs_ref, gt_ref, h_ref, wa_ref, wp_ref, wo_ref, h2_ref, mg_ref):
        a = _dot(att_ref[...], wa_ref[...])
        p = _dot(mixs_ref[...], wp_ref[...])
        merged = (jax.nn.sigmoid(gt_ref[:, 0:D].astype(F32)) * a + jax.nn.sigmoid(gt_ref[:, D:2 * D].astype(F32)) * p)
        mg = merged.astype(BF16)
        mg_ref[...] = mg
        h2_ref[...] = h_ref[...] + _dot(mg, wo_ref[...])

    tile = lambda w: pl.BlockSpec((TM, w), lambda i: (i, 0))
    return pl.pallas_call(
        body, name="merge_fwd", grid=(T // TM,),
        in_specs=[tile(AW), tile(PW), tile(2 * D), tile(D), _const_spec((AW, D)), _const_spec((PW, D)),
                  _const_spec((D, D))],
        out_specs=[tile(D), tile(D)],
        out_shape=[SDS((T, D), F32), SDS((T, D), BF16)],
        compiler_params=_cp(("arbitrary",)),
    )(att, mixs, gt, h1, wattn, wpool, wout)


def merge_bwd(dh2, att, mixs, gt, wattn, wpool, wout):
    T = dh2.shape[0]
    TM = 512

    def body(dh2_ref, att_ref, mixs_ref, gt_ref, wa_ref, wp_ref, wo_ref, datt_ref, dmixs_ref, dgt_ref, da_ref, dp_ref):
        dm = _dot_nt(dh2_ref[...].astype(BF16), wo_ref[...])
        a = _dot(att_ref[...], wa_ref[...])
        p = _dot(mixs_ref[...], wp_ref[...])
        sa = jax.nn.sigmoid(gt_ref[:, 0:D].astype(F32))
        sp = jax.nn.sigmoid(gt_ref[:, D:2 * D].astype(F32))
        da = (dm * sa).astype(BF16)
        dp = (dm * sp).astype(BF16)
        da_ref[...] = da
        dp_ref[...] = dp
        dgt_ref[:, 0:D] = (dm * a * sa * (1.0 - sa)).astype(BF16)
        dgt_ref[:, D:2 * D] = (dm * p * sp * (1.0 - sp)).astype(BF16)
        datt_ref[...] = _dot_nt(da, wa_ref[...]).astype(BF16)
        dmixs_ref[...] = _dot_nt(dp, wp_ref[...])

    tile = lambda w: pl.BlockSpec((TM, w), lambda i: (i, 0))
    return pl.pallas_call(
        body, name="merge_bwd", grid=(T // TM,),
        in_specs=[tile(D), tile(AW), tile(PW), tile(2 * D), _const_spec((AW, D)), _const_spec((PW, D)),
                  _const_spec((D, D))],
        out_specs=[tile(AW), tile(PW), tile(2 * D), tile(D), tile(D)],
        out_shape=[SDS((T, AW), BF16), SDS((T, PW), F32), SDS((T, 2 * D), BF16), SDS((T, D), BF16),
                   SDS((T, D), BF16)],
        compiler_params=_cp(("arbitrary",)),
    )(dh2, att, mixs, gt, wattn, wpool, wout)


def loss_head(h3, target, g):
    T = h3.shape[0]
    TM = 512

    def body(h_ref, t_ref, g_ref, dh_ref, loss_ref, dg_ref):
        @pl.when(pl.program_id(0) == 0)
        def _():
            loss_ref[...] = jnp.zeros_like(loss_ref)
            dg_ref[...] = jnp.zeros_like(dg_ref)

        x = h_ref[...]
        g = g_ref[...]
        rstd = _rstd(x)
        xhat = x * rstd
        err = xhat * g - t_ref[...]
        loss_ref[...] += 0.5 * jnp.sum(jnp.mean(err * err, axis=-1, keepdims=True), axis=0, keepdims=True)
        dy = err * (1.0 / D)
        dg_ref[...] += jnp.sum(dy * xhat, axis=0, keepdims=True)
        dh_ref[...] = _rms_bwd(dy, xhat, rstd, g)

    tile = pl.BlockSpec((TM, D), lambda i: (i, 0))
    return pl.pallas_call(
        body, name="loss_head", grid=(T // TM,),
        in_specs=[tile, tile, _const_spec((1, D))],
        out_specs=[tile, pl.BlockSpec((1, 1), lambda i: (0, 0)), pl.BlockSpec((1, D), lambda i: (0, 0))],
        out_shape=[SDS((T, D), F32), SDS((1, 1), F32), SDS((1, D), F32)],
        compiler_params=_cp(("arbitrary",)),
    )(h3, target, g)


def adamw(w, g, m, v, name):
    R, C = w.shape
    tile_bytes = 2 * 1024 * 1024
    tr = R
    if R * C * 4 > tile_bytes:
        tr = next(cand for cand in (512, 256, 128, 64, 32, 16, 8) if R % cand == 0 and cand * C * 4 <= tile_bytes)
    c1 = 1.0 - B1 ** STEP
    c2 = 1.0 - B2 ** STEP

    def body(w_ref, g_ref, m_ref, v_ref, d_ref, nm_ref, nv_ref):
        gv = g_ref[...]
        nm = B1 * m_ref[...] + (1.0 - B1) * gv
        nv = B2 * v_ref[...] + (1.0 - B2) * (gv * gv)
        nm_ref[...] = nm
        nv_ref[...] = nv
        d_ref[...] = -LR * ((nm / c1) / (jnp.sqrt(nv / c2) + ADAM_EPS) + WD * w_ref[...])

    spec = pl.BlockSpec((tr, C), lambda i: (i, 0))
    return pl.pallas_call(
        body, name=name, grid=(R // tr,),
        in_specs=[spec] * 4, out_specs=[spec] * 3, out_shape=[SDS((R, C), F32)] * 3,
        compiler_params=_cp(("parallel",)),
    )(w, g, m, v)


GROUP_FFN1, GROUP_MIX, GROUP_FFN2 = 0, 1, 2


def local_fwd_bwd(x, target, S, get_w, put_g):
    W = get_w(GROUP_FFN1)
    h1, ab1, hid1 = ffn_fwd(x, S["ffn1_norm"], W["ffn1_w_up"], W["ffn1_w_down"], "ffn1_fwd")
    W = get_w(GROUP_MIX)
    u, q, kv, z, gt = mix_in_fwd(h1, S["mix_norm"], W["w_in"])
    att, lse = attn_fwd(q, kv, S["sinks"])
    pooled, mixs = pool_fwd(z, S["pool_w_mix"], S["pool_scale"])
    h2, merged = merge_fwd(att, mixs, gt, h1, W["w_attn_up"], W["w_pool_up"], W["w_out"])
    W = get_w(GROUP_FFN2)
    h3, ab2, hid2 = ffn_fwd(h2, S["ffn2_norm"], W["ffn2_w_up"], W["ffn2_w_down"], "ffn2_fwd")
    dh3, loss, g_final = loss_head(h3, target, S["final_norm"])

    G = {"final_norm": g_final}
    dh2, dup2, n2, G["ffn2_norm"] = ffn_bwd_x(dh3, h2, ab2, S["ffn2_norm"], W["ffn2_w_up"], W["ffn2_w_down"],
                                              "ffn2_bwd_x")
    put_g(GROUP_FFN2, {"ffn2_w_down": matmul_tn(hid2, dh3, "ffn2_dw_down", tm=1408, tn=D, b_scale=0.5),
                       "ffn2_w_up": matmul_tn(n2, dup2, "ffn2_dw_up", tm=D, tn=1408)})

    W = get_w(GROUP_MIX)
    datt, dmixs, dgt, da, dp = merge_bwd(dh2, att, mixs, gt, W["w_attn_up"], W["w_pool_up"], W["w_out"])
    g_mix = {"w_out": matmul_tn(merged, dh2, "dw_out", tm=D, tn=D),
             "w_attn_up": matmul_tn(att, da, "dw_attn_up", tm=AW, tn=D),
             "w_pool_up": matmul_tn(mixs, dp, "dw_pool_up", tm=PW, tn=D)}
    dz, G["pool_w_mix"], G["pool_scale"] = pool_bwd(dmixs, pooled, S["pool_w_mix"], S["pool_scale"])
    dq, dkv, G["sinks"] = attn_bwd(q, kv, datt, lse, S["sinks"])
    dh1, G["mix_norm"] = mix_in_bwd(dq, dkv, dz, dgt, dh2, h1, S["mix_norm"], W["w_in"])
    g_mix["w_in"] = jnp.concatenate([
        matmul_tn(u, dq, "dw_in_q", tm=D, tn=AW),
        matmul_tn(u, dkv, "dw_in_kv", tm=D, tn=2 * KVW),
        matmul_tn(u, dz, "dw_in_z", tm=D, tn=PW),
        matmul_tn(u, dgt, "dw_in_g", tm=D, tn=D),
    ], axis=1)
    put_g(GROUP_MIX, g_mix)

    W = get_w(GROUP_FFN1)
    g_dn1 = matmul_tn(hid1, dh1, "ffn1_dw_down", tm=1408, tn=D, b_scale=0.5)
    dx, dup1, n1, G["ffn1_norm"] = ffn_bwd_x(dh1, x, ab1, S["ffn1_norm"], W["ffn1_w_up"], W["ffn1_w_down"],
                                             "ffn1_bwd_x")
    put_g(GROUP_FFN1, {"ffn1_w_down": g_dn1, "ffn1_w_up": matmul_tn(n1, dup1, "ffn1_dw_up", tm=D, tn=1408)})
    return loss, dx, G


HBM_SPEC = pl.BlockSpec(memory_space=pltpu.HBM)


def _me():
    return lax.axis_index("x"), lax.axis_index("y"), lax.axis_index("c")


def _peer_chip(x, y, k):
    return x ^ (k >> 1), y ^ (k & 1)


def _piece_half(ref, rowlike, j, h):
    if rowlike:
        return ref.at[j, h]
    ns = ref.shape[-1] // N_CHIPS
    return ref.at[h, :, pl.ds(pl.multiple_of(j * ns, 128), ns)]


def _piece(ref, rowlike, j):
    if rowlike:
        return ref.at[j]
    ns = ref.shape[-1] // N_CHIPS
    return ref.at[:, :, pl.ds(pl.multiple_of(j * ns, 128), ns)]


def _full_shape(shard_view, rowlike):
    _, kh, ns = shard_view.shape
    return (N_CHIPS, 2, kh, ns) if rowlike else (2, kh, N_CHIPS * ns)


def _remote(src, dst, send_sem, recv_sem, dev):
    return pltpu.make_async_remote_copy(src, dst, send_sem, recv_sem, device_id=dev, device_id_type=MESH)


def gather_ici(shards, rowlikes, name):
    n = len(shards)

    def body(*refs):
        s_refs, f_refs = refs[:n], refs[n:2 * n]
        send_sems, recv_sems = refs[2 * n:]
        x, y, c = _me()
        chip = 2 * x + y
        sends = []
        for a in range(n):
            for k in (1, 2, 3):
                px, py = _peer_chip(x, y, k)
                cp = _remote(s_refs[a].at[c], _piece_half(f_refs[a], rowlikes[a], chip, c),
                             send_sems.at[a, k - 1], recv_sems.at[a, k - 1], (px, py, c))
                cp.start()
                sends.append(cp)
        for a in range(n):
            for k in (1, 2, 3):
                px, py = _peer_chip(x, y, k)
                slot = _piece_half(f_refs[a], rowlikes[a], 2 * px + py, c)
                _remote(slot, slot, send_sems.at[a, k - 1], recv_sems.at[a, k - 1], (px, py, c)).wait_recv()
        for cp in sends:
            cp.wait_send()

    return pl.pallas_call(
        body, name=name, in_specs=[HBM_SPEC] * n, out_specs=[HBM_SPEC] * n,
        out_shape=[SDS(_full_shape(s, r), s.dtype) for s, r in zip(shards, rowlikes)],
        scratch_shapes=[pltpu.SemaphoreType.DMA((n, 3)), pltpu.SemaphoreType.DMA((n, 3))],
    )(*shards)


def gather_finish(shards, fulls, rowlikes, name):
    n = len(shards)

    def body(*refs):
        s_refs, f_refs = refs[:n], refs[2 * n:3 * n]
        send_sems, recv_sems, local_sems = refs[3 * n:]
        x, y, c = _me()
        chip = 2 * x + y
        sib = (x, y, 1 - c)
        pending = []
        for a in range(n):
            local = pltpu.make_async_copy(s_refs[a], _piece(f_refs[a], rowlikes[a], chip), local_sems.at[a])
            local.start()
            pending.append(local)
        sends = []
        for a in range(n):
            for k in (1, 2, 3):
                px, py = _peer_chip(x, y, k)
                slot = _piece_half(f_refs[a], rowlikes[a], 2 * px + py, c)
                cp = _remote(slot, slot, send_sems.at[a, k - 1], recv_sems.at[a, k - 1], sib)
                cp.start()
                sends.append(cp)
        for a in range(n):
            for k in (1, 2, 3):
                px, py = _peer_chip(x, y, k)
                slot = _piece_half(f_refs[a], rowlikes[a], 2 * px + py, 1 - c)
                _remote(slot, slot, send_sems.at[a, k - 1], recv_sems.at[a, k - 1], sib).wait_recv()
        for cp in sends:
            cp.wait_send()
        for local in pending:
            local.wait()

    return pl.pallas_call(
        body, name=name, in_specs=[HBM_SPEC] * (2 * n), out_specs=[HBM_SPEC] * n,
        out_shape=[SDS(f.shape, f.dtype) for f in fulls],
        input_output_aliases={n + a: a for a in range(n)},
        scratch_shapes=[pltpu.SemaphoreType.DMA((n, 3)), pltpu.SemaphoreType.DMA((n, 3)),
                        pltpu.SemaphoreType.DMA((n,))],
    )(*shards, *fulls)


def swap_halves(gviews, rowlikes, name):
    n = len(gviews)

    def half_shape(g, rowlike):
        return (N_CHIPS,) + g.shape[2:] if rowlike else g.shape[1:]

    def body(*refs):
        g_refs, got_refs = refs[:n], refs[n:2 * n]
        send_sems, recv_sems = refs[2 * n:]
        x, y, c = _me()
        copies = []
        for a in range(n):
            src = g_refs[a].at[:, 1 - c] if rowlikes[a] else g_refs[a].at[1 - c]
            cp = _remote(src, got_refs[a], send_sems.at[a], recv_sems.at[a], (x, y, 1 - c))
            cp.start()
            copies.append(cp)
        for cp in copies:
            cp.wait()

    return pl.pallas_call(
        body, name=name, in_specs=[HBM_SPEC] * n, out_specs=[HBM_SPEC] * n,
        out_shape=[SDS(half_shape(g, r), g.dtype) for g, r in zip(gviews, rowlikes)],
        scratch_shapes=[pltpu.SemaphoreType.DMA((n,)), pltpu.SemaphoreType.DMA((n,))],
    )(*gviews)


ROW_TILES = 2


def _piece_specs(rowlike, kh, ns, piece, half):
    tr = kh // ROW_TILES
    if rowlike:
        return (pl.BlockSpec((None, None, tr, ns), lambda *g: (piece(*g), half(*g), g[-2], 0)),
                pl.BlockSpec((None, tr, ns), lambda *g: (piece(*g), g[-2], 0)))
    return (pl.BlockSpec((None, tr, ns), lambda *g: (half(*g), g[-2], piece(*g))),
            pl.BlockSpec((tr, ns), lambda *g: (g[-2], piece(*g))))


def add_halves(gview, got, rowlike, place, name):
    kh, ns = (gview.shape[2], gview.shape[3]) if rowlike else (gview.shape[1], gview.shape[2] // N_CHIPS)

    def body(place_ref, g_ref, got_ref, o_ref):
        o_ref[...] = (g_ref[...] + got_ref[...]).astype(BF16)

    g_spec, h_spec = _piece_specs(rowlike, kh, ns, lambda j, r, p: j, lambda j, r, p: p[0])
    return pl.pallas_call(
        body, name=name,
        grid_spec=pltpu.PrefetchScalarGridSpec(num_scalar_prefetch=1, grid=(N_CHIPS, ROW_TILES),
                                               in_specs=[g_spec, h_spec], out_specs=h_spec),
        out_shape=SDS(got.shape, BF16),
        compiler_params=_cp(("parallel", "parallel")),
    )(place, gview, got)


def chip_all_to_all(qs, rowlikes, name):
    n = len(qs)

    def piece(ref, rowlike, j):
        if rowlike:
            return ref.at[j]
        ns = ref.shape[-1] // N_CHIPS
        return ref.at[:, pl.ds(pl.multiple_of(j * ns, 128), ns)]

    def slot_shape(q, rowlike):
        return (3,) + (q.shape[1:] if rowlike else (q.shape[0], q.shape[1] // N_CHIPS))

    def body(*refs):
        q_refs, r_refs = refs[:n], refs[n:2 * n]
        send_sems, recv_sems = refs[2 * n:]
        x, y, c = _me()
        sends = []
        for a in range(n):
            for k in (1, 2, 3):
                px, py = _peer_chip(x, y, k)
                cp = _remote(piece(q_refs[a], rowlikes[a], 2 * px + py), r_refs[a].at[k - 1],
                             send_sems.at[a, k - 1], recv_sems.at[a, k - 1], (px, py, c))
                cp.start()
                sends.append(cp)
        for a in range(n):
            for k in (1, 2, 3):
                px, py = _peer_chip(x, y, k)
                slot = r_refs[a].at[k - 1]
                _remote(slot, slot, send_sems.at[a, k - 1], recv_sems.at[a, k - 1], (px, py, c)).wait_recv()
        for cp in sends:
            cp.wait_send()

    return pl.pallas_call(
        body, name=name, in_specs=[HBM_SPEC] * n, out_specs=[HBM_SPEC] * n,
        out_shape=[SDS(slot_shape(q, r), q.dtype) for q, r in zip(qs, rowlikes)],
        scratch_shapes=[pltpu.SemaphoreType.DMA((n, 3)), pltpu.SemaphoreType.DMA((n, 3))],
    )(*qs)


def sum_pieces(gview, got, recv, rowlike, place, name):
    kh, ns = recv.shape[1], recv.shape[2]
    tr = kh // ROW_TILES

    def body(place_ref, g_ref, got_ref, r_ref, o_ref):
        acc = g_ref[...] + got_ref[...]
        for k in range(3):
            acc = acc + r_ref[k].astype(F32)
        o_ref[...] = acc

    g_spec, h_spec = _piece_specs(rowlike, kh, ns, lambda z, r, p: p[1], lambda z, r, p: p[0])
    return pl.pallas_call(
        body, name=name,
        grid_spec=pltpu.PrefetchScalarGridSpec(
            num_scalar_prefetch=1, grid=(1, ROW_TILES),
            in_specs=[g_spec, h_spec, pl.BlockSpec((3, tr, ns), lambda z, r, p: (0, r, 0))],
            out_specs=pl.BlockSpec((tr, ns), lambda z, r, p: (r, 0))),
        out_shape=SDS((kh, ns), F32),
        compiler_params=_cp(("parallel", "parallel")),
    )(place, gview, got, recv)


def join_halves(halves, name):
    n = len(halves)

    def body(*refs):
        h_refs, o_refs = refs[:n], refs[n:2 * n]
        send_sems, recv_sems, local_sems = refs[2 * n:]
        x, y, c = _me()
        sib = (x, y, 1 - c)
        pending = []
        for a in range(n):
            local = pltpu.make_async_copy(h_refs[a], o_refs[a].at[c], local_sems.at[a])
            local.start()
            cp = _remote(h_refs[a], o_refs[a].at[c], send_sems.at[a], recv_sems.at[a], sib)
            cp.start()
            pending.append((local, cp))
        for a in range(n):
            got = o_refs[a].at[1 - c]
            _remote(got, got, send_sems.at[a], recv_sems.at[a], sib).wait_recv()
        for local, cp in pending:
            cp.wait_send()
            local.wait()

    return pl.pallas_call(
        body, name=name, in_specs=[HBM_SPEC] * n, out_specs=[HBM_SPEC] * n,
        out_shape=[SDS((2,) + h.shape, h.dtype) for h in halves],
        scratch_shapes=[pltpu.SemaphoreType.DMA((n,)), pltpu.SemaphoreType.DMA((n,)), pltpu.SemaphoreType.DMA((n,))],
    )(*halves)


N_DEV = 8


def all_reduce_small(s):
    R, Wd = s.shape

    def body(s_ref, out_ref, slots_ref, send_sems, recv_sems):
        x, y, c = _me()
        me = 4 * x + 2 * y + c
        slots_ref[me] = s_ref[...]
        sends = []
        for k in range(1, N_DEV):
            peer = (x ^ (k >> 2), y ^ ((k >> 1) & 1), c ^ (k & 1))
            cp = pltpu.make_async_remote_copy(s_ref, slots_ref.at[me], send_sems.at[k - 1], recv_sems.at[k - 1],
                                              device_id=peer, device_id_type=MESH)
            cp.start()
            sends.append(cp)
        for k in range(1, N_DEV):
            peer = (x ^ (k >> 2), y ^ ((k >> 1) & 1), c ^ (k & 1))
            slot = slots_ref.at[4 * peer[0] + 2 * peer[1] + peer[2]]
            pltpu.make_async_remote_copy(s_ref, slot, send_sems.at[k - 1], recv_sems.at[k - 1],
                                         device_id=peer, device_id_type=MESH).wait_recv()
        for cp in sends:
            cp.wait_send()
        acc = slots_ref[0]
        for d in range(1, N_DEV):
            acc = acc + slots_ref[d]
        out_ref[...] = acc

    vmem = pl.BlockSpec(memory_space=pltpu.VMEM)
    return pl.pallas_call(
        body, name="all_reduce_small", in_specs=[vmem], out_specs=vmem, out_shape=SDS((R, Wd), F32),
        scratch_shapes=[pltpu.VMEM((N_DEV, R, Wd), F32), pltpu.SemaphoreType.DMA((N_DEV - 1,)),
                        pltpu.SemaphoreType.DMA((N_DEV - 1,))],
    )(s)


BIG = {"ffn1_w_up": (D, 2 * FF, "col"), "ffn1_w_down": (FF, D, "row"), "w_in": (D, INW, "cm"),
       "w_attn_up": (AW, D, "row"), "w_pool_up": (PW, D, "col"), "w_out": (D, D, "row"),
       "ffn2_w_up": (D, 2 * FF, "col"), "ffn2_w_down": (FF, D, "row")}
GROUPS = (("ffn1_w_up", "ffn1_w_down"), ("w_in", "w_attn_up", "w_pool_up", "w_out"), ("ffn2_w_up", "ffn2_w_down"))
SMALL = ("ffn1_norm", "mix_norm", "ffn2_norm", "final_norm", "pool_scale", "sinks", "pool_w_mix")
SMALL_W = 128


def _rowlike(name):
    return BIG[name][2] != "col"


def _half_dims(name):
    k, n, kind = BIG[name]
    return (k // N_CHIPS // 2, n) if kind == "row" else (k // 2, n // N_CHIPS)


def shard_view(name, shard):
    return shard.reshape((2,) + _half_dims(name))


def full_from_view(name, fv):
    k, n, kind = BIG[name]
    if kind == "cm":
        return fv.reshape(N_CHIPS, k, n // N_CHIPS).transpose(1, 0, 2).reshape(k, n)
    return fv.reshape(k, n)


def grad_view(name, g):
    k, n, kind = BIG[name]
    kh, ns = _half_dims(name)
    if kind == "cm":
        return g.reshape(k, N_CHIPS, ns).transpose(1, 0, 2).reshape(N_CHIPS, 2, kh, ns)
    return g.reshape(_full_shape(jax.ShapeDtypeStruct((2, kh, ns), g.dtype), kind == "row"))


def pack_small(d):
    parts = []
    for name in SMALL:
        a = d[name].reshape(-1)
        pad = (-a.shape[0]) % SMALL_W
        parts.append(jnp.pad(a, (0, pad)).reshape(-1, SMALL_W))
    a = jnp.concatenate(parts, axis=0)
    return jnp.pad(a, ((0, (-a.shape[0]) % 8), (0, 0)))


def unpack_small(a, like):
    out, r0 = {}, 0
    for name in SMALL:
        size = int(np.prod(like[name].shape))
        rows = -(-size // SMALL_W)
        out[name] = a[r0:r0 + rows].reshape(-1)[:size].reshape(like[name].shape)
        r0 += rows
    return out


WEIGHTS = ("ffn1_norm", "ffn1_w_up", "ffn1_w_down", "mix_norm", "w_in", "sinks", "w_attn_up", "pool_w_mix",
           "pool_scale", "w_pool_up", "w_out", "ffn2_norm", "ffn2_w_up", "ffn2_w_down", "final_norm")


def kernel(x, ffn1_norm, ffn1_w_up, ffn1_w_down, mix_norm, w_in, sinks, w_attn_up, pool_w_mix, pool_scale, w_pool_up, w_out, ffn2_norm, ffn2_w_up, ffn2_w_down, final_norm, loss_target, m_ffn1_norm, m_ffn1_w_up, m_ffn1_w_down, m_mix_norm, m_w_in, m_sinks, m_w_attn_up, m_pool_w_mix, m_pool_scale, m_w_pool_up, m_w_out, m_ffn2_norm, m_ffn2_w_up, m_ffn2_w_down, m_final_norm, v_ffn1_norm, v_ffn1_w_up, v_ffn1_w_down, v_mix_norm, v_w_in, v_sinks, v_w_attn_up, v_pool_w_mix, v_pool_scale, v_w_pool_up, v_w_out, v_ffn2_norm, v_ffn2_w_up, v_ffn2_w_down, v_final_norm):
    given = dict(locals())
    w = {n: given[n] for n in WEIGHTS}
    m = {n: given["m_" + n] for n in WEIGHTS}
    v = {n: given["v_" + n] for n in WEIGHTS}
    cx, cy, cc = _me()
    place = jnp.stack([cc, 2 * cx + cy]).astype(jnp.int32)

    shards = {n: w[n][0] for n in BIG}
    sviews = {n: shard_view(n, shards[n].astype(BF16)) for n in BIG}
    gathered = {}

    def get_w(group):
        if group not in gathered:
            names = GROUPS[group]
            rl = [_rowlike(n) for n in names]
            sv = [sviews[n] for n in names]
            fulls = gather_finish(sv, gather_ici(sv, rl, f"gather_ici_{group}"), rl, f"gather_finish_{group}")
            gathered[group] = {n: full_from_view(n, f) for n, f in zip(names, fulls)}
        return gathered[group]

    grads, delta, new_m, new_v = {}, {}, {}, {}

    def put_g(group, g):
        names = GROUPS[group]
        rl = [_rowlike(n) for n in names]
        gv = [grad_view(n, g[n]) for n in names]
        gots = swap_halves(gv, rl, f"swap_halves_{group}")
        qs = [add_halves(a, b, r, place, "add_halves_" + n) for a, b, r, n in zip(gv, gots, rl, names)]
        recvs = chip_all_to_all(qs, rl, f"chip_all_to_all_{group}")
        halves = [sum_pieces(a, b, r, k, place, "sum_pieces_" + n) for a, b, r, k, n in zip(gv, gots, recvs, rl, names)]
        for n, o in zip(names, join_halves(halves, f"join_halves_{group}")):
            grads[n] = o.reshape(shards[n].shape)
            delta[n], new_m[n], new_v[n] = adamw(shards[n], grads[n], m[n][0], v[n][0], "adamw_" + n)

    S = {n: w[n].reshape(1, -1) for n in ("ffn1_norm", "mix_norm", "ffn2_norm", "final_norm", "pool_scale", "sinks")}
    S["pool_w_mix"] = w["pool_w_mix"][0].astype(BF16)
    loss, dx, G = local_fwd_bwd(x[0], loss_target[0], S, get_w, put_g)
    loss = lax.psum(loss[0, 0], ("x", "y", "c"))

    small_like = {n: w[n] for n in SMALL}
    grads.update(unpack_small(all_reduce_small(pack_small({n: G[n] for n in SMALL})), small_like))

    ds, ms, vs = adamw(pack_small(w), pack_small(grads), pack_small(m), pack_small(v), "adamw_small")
    delta.update(unpack_small(ds, small_like))
    new_m.update(unpack_small(ms, small_like))
    new_v.update(unpack_small(vs, small_like))

    def shaped(d, n):
        return d[n].reshape(w[n].shape)

    return (loss, dx[None], *[shaped(grads, n) for n in WEIGHTS], *[shaped(delta, n) for n in WEIGHTS],
            *[shaped(new_m, n) for n in WEIGHTS], *[shaped(new_v, n) for n in WEIGHTS])
```

```python
import numpy as np
import jax
import jax.numpy as jnp
from jax import lax
from jax.experimental import pallas as pl
from jax.experimental.pallas import tpu as pltpu

F32 = jnp.float32
BF16 = jnp.bfloat16
SDS = jax.ShapeDtypeStruct
MESH = pl.DeviceIdType.MESH

D = 1024
FF = 2816
NQ = 16
NKV = 2
HD = 64
GQ = NQ // NKV
AW = NQ * HD
KVW = NKV * HD
BLK = 128
PW = 512
PG = 128
POOL_WINDOWS = (2, 4, 8, 16)
HALO = 16
INW = AW + 2 * KVW + PW + 2 * D
C_KV = AW
C_Z = AW + 2 * KVW
C_G = C_Z + PW
EPS = 1e-6
FF_CHUNK = 256
FF_CHUNKS = tuple((c, FF_CHUNK) for c in range(0, FF, FF_CHUNK))
SLOPES = tuple(float(2.0 ** (-8.0 * h / NQ)) for h in range(1, NQ + 1))
SCALE = HD ** -0.5

LR, B1, B2, ADAM_EPS, WD, STEP = 0.001, 0.9, 0.999, 1e-08, 0.01, 10

VMEM_LIMIT = 56 * 1024 * 1024
N_CHIPS = 4

NT = (((1,), (1,)), ((), ()))
TN = (((0,), (0,)), ((), ()))


def _cp(sem=None, vmem=VMEM_LIMIT):
    return pltpu.CompilerParams(dimension_semantics=sem, vmem_limit_bytes=vmem)


def _const_spec(shape):
    nd = len(shape)
    return pl.BlockSpec(shape, lambda *_: (0,) * nd, pipeline_mode=pl.Buffered(1))


def _rstd(x):
    return lax.rsqrt(jnp.mean(x * x, axis=-1, keepdims=True) + EPS)


def _rms_bwd(dn, xhat, rstd, g):
    dxhat = dn * g
    return rstd * (dxhat - xhat * jnp.mean(dxhat * xhat, axis=-1, keepdims=True))


def _dot(a, b):
    return jnp.dot(a, b, preferred_element_type=F32)


def _dot_nt(a, b):
    return lax.dot_general(a, b, NT, preferred_element_type=F32)


def _dot_tn(a, b):
    return lax.dot_general(a, b, TN, preferred_element_type=F32)


def ffn_fwd(h, g, wup, wdn, name):
    T = h.shape[0]
    TM = 512

    def body(h_ref, g_ref, wup_ref, wdn_ref, out_ref, ab_ref, hid_ref):
        x = h_ref[...]
        n = (x * _rstd(x) * g_ref[...]).astype(BF16)
        for c0, w in FF_CHUNKS:
            a = _dot(n, wup_ref[:, c0:c0 + w])
            b = _dot(n, wup_ref[:, FF + c0:FF + c0 + w])
            ab_ref[:, c0:c0 + w] = a.astype(BF16)
            ab_ref[:, FF + c0:FF + c0 + w] = b.astype(BF16)
            hid_ref[:, c0:c0 + w] = (a * jax.nn.sigmoid(a) * b).astype(BF16)
        out_ref[...] = x + 0.5 * _dot(hid_ref[...], wdn_ref[...])

    return pl.pallas_call(
        body, name=name, grid=(T // TM,),
        in_specs=[pl.BlockSpec((TM, D), lambda i: (i, 0)), _const_spec((1, D)),
                  _const_spec((D, 2 * FF)), _const_spec((FF, D))],
        out_specs=[pl.BlockSpec((TM, D), lambda i: (i, 0)), pl.BlockSpec((TM, 2 * FF), lambda i: (i, 0)),
                   pl.BlockSpec((TM, FF), lambda i: (i, 0))],
        out_shape=[SDS((T, D), F32), SDS((T, 2 * FF), BF16), SDS((T, FF), BF16)],
        compiler_params=_cp(("arbitrary",)),
    )(h, g, wup, wdn)


def ffn_bwd_x(dh, h_in, ab, g, wup, wdn, name):
    T = dh.shape[0]
    TM = 256

    def body(dh_ref, h_ref, ab_ref, g_ref, wup_ref, wdn_ref, dhin_ref, dup_ref, n_ref, dg_ref):
        x = h_ref[...]
        g = g_ref[...]
        rstd = _rstd(x)
        xhat = x * rstd
        n_ref[...] = (xhat * g).astype(BF16)
        dh = dh_ref[...]
        dhh = (0.5 * dh).astype(BF16)
        for c0, w in FF_CHUNKS:
            dhid = _dot_nt(dhh, wdn_ref[c0:c0 + w, :])
            a = ab_ref[:, c0:c0 + w].astype(F32)
            b = ab_ref[:, FF + c0:FF + c0 + w].astype(F32)
            sig = jax.nn.sigmoid(a)
            s = a * sig
            dup_ref[:, c0:c0 + w] = (dhid * b * (sig * (1.0 + a * (1.0 - sig)))).astype(BF16)
            dup_ref[:, FF + c0:FF + c0 + w] = (dhid * s).astype(BF16)
        dn = _dot_nt(dup_ref[...], wup_ref[...])
        dhin_ref[...] = dh + _rms_bwd(dn, xhat, rstd, g)

        @pl.when(pl.program_id(0) == 0)
        def _():
            dg_ref[...] = jnp.zeros_like(dg_ref)

        dg_ref[...] += jnp.sum(dn * xhat, axis=0, keepdims=True)

    tile = lambda w: pl.BlockSpec((TM, w), lambda i: (i, 0))
    return pl.pallas_call(
        body, name=name, grid=(T // TM,),
        in_specs=[tile(D), tile(D), tile(2 * FF), _const_spec((1, D)), _const_spec((D, 2 * FF)), _const_spec((FF, D))],
        out_specs=[tile(D), tile(2 * FF), tile(D), pl.BlockSpec((1, D), lambda i: (0, 0))],
        out_shape=[SDS((T, D), F32), SDS((T, 2 * FF), BF16), SDS((T, D), BF16), SDS((1, D), F32)],
        compiler_params=_cp(("arbitrary",)),
    )(dh, h_in, ab, g, wup, wdn)


def matmul_tn(a, b, name, *, tm, tn, tt=1024, b_scale=None):
    T, M = a.shape
    N = b.shape[1]
    assert M % tm == 0 and N % tn == 0 and T % tt == 0

    def body(a_ref, b_ref, o_ref):
        @pl.when(pl.program_id(2) == 0)
        def _():
            o_ref[...] = jnp.zeros_like(o_ref)

        bv = b_ref[...]
        if b_scale is not None:
            bv = bv * b_scale
        o_ref[...] += _dot_tn(a_ref[...].astype(BF16), bv.astype(BF16))

    return pl.pallas_call(
        body, name=name, grid=(M // tm, N // tn, T // tt),
        in_specs=[pl.BlockSpec((tt, tm), lambda i, j, t: (t, i)), pl.BlockSpec((tt, tn), lambda i, j, t: (t, j))],
        out_specs=pl.BlockSpec((tm, tn), lambda i, j, t: (i, j)),
        out_shape=SDS((M, N), F32),
        compiler_params=_cp(("parallel", "parallel", "arbitrary")),
    )(a, b)


def mix_in_fwd(h1, g, win):
    T = h1.shape[0]
    TM = 512

    def body(h_ref, g_ref, w_ref, u_ref, q_ref, kv_ref, z_ref, gt_ref):
        x = h_ref[...]
        u = (x * _rstd(x) * g_ref[...]).astype(BF16)
        u_ref[...] = u
        for c in range(0, AW, 256):
            q_ref[:, c:c + 256] = _dot(u, w_ref[:, c:c + 256]).astype(BF16)
        kv_ref[...] = _dot(u, w_ref[:, C_KV:C_Z]).astype(BF16)
        for c in range(0, PW, 256):
            z_ref[:, c:c + 256] = _dot(u, w_ref[:, C_Z + c:C_Z + c + 256])
        for c in range(0, 2 * D, 256):
            gt_ref[:, c:c + 256] = _dot(u, w_ref[:, C_G + c:C_G + c + 256]).astype(BF16)

    tile = lambda w: pl.BlockSpec((TM, w), lambda i: (i, 0))
    return pl.pallas_call(
        body, name="mix_in_fwd", grid=(T // TM,),
        in_specs=[tile(D), _const_spec((1, D)), _const_spec((D, INW))],
        out_specs=[tile(D), tile(AW), tile(2 * KVW), tile(PW), tile(2 * D)],
        out_shape=[SDS((T, D), BF16), SDS((T, AW), BF16), SDS((T, 2 * KVW), BF16), SDS((T, PW), F32),
                   SDS((T, 2 * D), BF16)],
        compiler_params=_cp(("arbitrary",)),
    )(h1, g, win)


def mix_in_bwd(dq, dkv, dz, dgt, dh2, h1, g, win):
    T = h1.shape[0]
    TM = 512

    def body(dq_ref, dkv_ref, dz_ref, dgt_ref, dh2_ref, h_ref, g_ref, w_ref, dh1_ref, dg_ref):
        du = _dot_nt(dq_ref[...], w_ref[:, 0:AW])
        du += _dot_nt(dkv_ref[...], w_ref[:, C_KV:C_Z])
        du += _dot_nt(dz_ref[...], w_ref[:, C_Z:C_G])
        du += _dot_nt(dgt_ref[...], w_ref[:, C_G:INW])
        x = h_ref[...]
        g = g_ref[...]
        rstd = _rstd(x)
        xhat = x * rstd
        dh1_ref[...] = dh2_ref[...] + _rms_bwd(du, xhat, rstd, g)

        @pl.when(pl.program_id(0) == 0)
        def _():
            dg_ref[...] = jnp.zeros_like(dg_ref)

        dg_ref[...] += jnp.sum(du * xhat, axis=0, keepdims=True)

    tile = lambda w: pl.BlockSpec((TM, w), lambda i: (i, 0))
    return pl.pallas_call(
        body, name="mix_in_bwd", grid=(T // TM,),
        in_specs=[tile(AW), tile(2 * KVW), tile(PW), tile(2 * D), tile(D), tile(D), _const_spec((1, D)),
                  _const_spec((D, INW))],
        out_specs=[tile(D), pl.BlockSpec((1, D), lambda i: (0, 0))],
        out_shape=[SDS((T, D), F32), SDS((1, D), F32)],
        compiler_params=_cp(("arbitrary",)),
    )(dq, dkv, dz, dgt, dh2, h1, g, win)


PAIR = 2 * HD
NPAIR = GQ // 2


def _lo_lanes():
    return lax.broadcasted_iota(jnp.int32, (BLK, PAIR), 1) < HD


def _stack_heads(ref, kvh, scale=None):
    lo = _lo_lanes()
    parts = []
    for pr in range(NPAIR):
        t = ref[:, (kvh * NPAIR + pr) * PAIR:(kvh * NPAIR + pr + 1) * PAIR]
        if scale is not None:
            t = t * scale
        zero = jnp.zeros_like(t)
        parts += [jnp.where(lo, t, zero), jnp.where(lo, zero, t)]
    return jnp.concatenate(parts, axis=0)


def _kv_tiles(kvc_ref, kvp_ref, tile, kvh):
    lo = _lo_lanes()
    dup, left, right = [], [], []
    for ref in (kvp_ref, kvc_ref):
        t = ref[:, tile * PAIR:(tile + 1) * PAIR]
        r = pltpu.roll(t.astype(F32), HD, 1).astype(BF16)
        zero = jnp.zeros_like(t)
        a, b = (t, r) if kvh == 0 else (r, t)
        dup.append(jnp.where(lo, a, b))
        left.append(jnp.where(lo, a, zero))
        right.append(jnp.where(lo, zero, b))
    cat = lambda xs: jnp.concatenate(xs, axis=0)
    return cat(dup), cat(left), cat(right)


def _band_consts(first):
    row = lax.broadcasted_iota(jnp.int32, (BLK, BLK), 0)
    col = lax.broadcasted_iota(jnp.int32, (BLK, BLK), 1)
    upper = col > row
    dist = jnp.where(upper, row - col + BLK, row - col).astype(F32)
    pen = jnp.where(jnp.logical_and(upper, first), -jnp.inf, 0.0)
    return upper, dist, pen


def _split_band(upper, t):
    zero = jnp.zeros_like(t)
    return jnp.concatenate([jnp.where(upper, t, zero), jnp.where(upper, zero, t)], axis=1)


def attn_fwd(q, kv, sinks):
    T = q.shape[0]
    nb = T // BLK

    def body(sink_ref, q_ref, kvc_ref, kvp_ref, att_ref, lse_ref):
        upper, dist, pen = _band_consts(pl.program_id(0) == 0)
        for kvh in range(NKV):
            kdup, _, _ = _kv_tiles(kvc_ref, kvp_ref, 0, kvh)
            _, vleft, vright = _kv_tiles(kvc_ref, kvp_ref, 1, kvh)
            s_all = _dot_nt(_stack_heads(q_ref, kvh, SCALE), kdup)
            for pr in range(NPAIR):
                out = None
                for side, vpad in ((0, vleft), (1, vright)):
                    g = 2 * pr + side
                    hq = kvh * GQ + g
                    sink = sink_ref[0, hq]
                    rows = slice(g * BLK, (g + 1) * BLK)
                    s = jnp.where(upper, s_all[rows, 0:BLK], s_all[rows, BLK:2 * BLK]) - SLOPES[hq] * dist + pen
                    m = jnp.maximum(jnp.max(s, axis=-1, keepdims=True), sink)
                    p = jnp.exp(s - m)
                    l = jnp.sum(p, axis=-1, keepdims=True) + jnp.exp(sink - m)
                    lse_ref[:, hq:hq + 1] = m + jnp.log(l)
                    o = _dot(_split_band(upper, (p * (1.0 / l)).astype(BF16)), vpad)
                    out = o if out is None else out + o
                col0 = (kvh * NPAIR + pr) * PAIR
                att_ref[:, col0:col0 + PAIR] = out.astype(BF16)

    return pl.pallas_call(
        body, name="attn_fwd", grid=(nb,),
        in_specs=[pl.BlockSpec(memory_space=pltpu.SMEM),
                  pl.BlockSpec((BLK, AW), lambda i: (i, 0)),
                  pl.BlockSpec((BLK, 2 * KVW), lambda i: (i, 0)),
                  pl.BlockSpec((BLK, 2 * KVW), lambda i: (jnp.maximum(i - 1, 0), 0))],
        out_specs=[pl.BlockSpec((BLK, AW), lambda i: (i, 0)), pl.BlockSpec((BLK, NQ), lambda i: (i, 0))],
        out_shape=[SDS((T, AW), BF16), SDS((T, NQ), F32)],
        compiler_params=_cp(("arbitrary",)),
    )(sinks, q, kv, kv)


def attn_bwd(q, kv, datt, lse, sinks):
    T = q.shape[0]
    nb = T // BLK

    def body(sink_ref, q_ref, kvc_ref, kvp_ref, do_ref, lse_ref, dq_ref, dkv_ref, dsink_ref, carry_ref):
        i = pl.program_id(0)

        @pl.when(i == 0)
        def _():
            dsink_ref[...] = jnp.zeros_like(dsink_ref)
            carry_ref[...] = jnp.zeros_like(carry_ref)

        @pl.when(i < nb)
        def _():
            upper, dist, pen = _band_consts(i == 0)
            lo = _lo_lanes()
            dk_dup, dv_dup = [], []
            for kvh in range(NKV):
                kdup, kleft, kright = _kv_tiles(kvc_ref, kvp_ref, 0, kvh)
                vdup, _, _ = _kv_tiles(kvc_ref, kvp_ref, 1, kvh)
                qs = _stack_heads(q_ref, kvh, SCALE)
                dos = _stack_heads(do_ref, kvh)
                s_all = _dot_nt(qs, kdup)
                dp_all = _dot_nt(dos, vdup)
                ds_parts, p_parts = [], []
                for pr in range(NPAIR):
                    dq = None
                    for side, kpad in ((0, kleft), (1, kright)):
                        g = 2 * pr + side
                        hq = kvh * GQ + g
                        lse_h = lse_ref[:, hq:hq + 1]
                        rows = slice(g * BLK, (g + 1) * BLK)
                        s = jnp.where(upper, s_all[rows, 0:BLK], s_all[rows, BLK:2 * BLK]) - SLOPES[hq] * dist + pen
                        p = jnp.exp(s - lse_h)
                        dp = jnp.where(upper, dp_all[rows, 0:BLK], dp_all[rows, BLK:2 * BLK])
                        delta = jnp.sum(p * dp, axis=-1, keepdims=True)
                        dsink_ref[:, hq:hq + 1] += -jnp.sum(jnp.exp(sink_ref[0, hq] - lse_h) * delta, axis=0,
                                                            keepdims=True)
                        ds = _split_band(upper, (p * (dp - delta)).astype(BF16))
                        ds_parts.append(ds)
                        p_parts.append(_split_band(upper, p.astype(BF16)))
                        d = _dot(ds, kpad)
                        dq = d if dq is None else dq + d
                    col0 = (kvh * NPAIR + pr) * PAIR
                    dq_ref[:, col0:col0 + PAIR] = (dq * SCALE).astype(BF16)
                dkw = _dot_tn(jnp.concatenate(ds_parts, axis=0), qs)
                dvw = _dot_tn(jnp.concatenate(p_parts, axis=0), dos)
                dk_dup.append(dkw + pltpu.roll(dkw, HD, 1))
                dv_dup.append(dvw + pltpu.roll(dvw, HD, 1))
            dk = jnp.where(jnp.concatenate([lo, lo], axis=0), dk_dup[0], dk_dup[1])
            dv = jnp.where(jnp.concatenate([lo, lo], axis=0), dv_dup[0], dv_dup[1])
            dkv_ref[:, 0:PAIR] = (carry_ref[:, 0:PAIR] + dk[0:BLK]).astype(BF16)
            dkv_ref[:, PAIR:2 * PAIR] = (carry_ref[:, PAIR:2 * PAIR] + dv[0:BLK]).astype(BF16)
            carry_ref[:, 0:PAIR] = dk[BLK:2 * BLK]
            carry_ref[:, PAIR:2 * PAIR] = dv[BLK:2 * BLK]

        @pl.when(i == nb)
        def _():
            dkv_ref[...] = carry_ref[...].astype(BF16)

    cur = lambda i: (jnp.minimum(i, nb - 1), 0)
    prev = lambda i: (jnp.maximum(jnp.minimum(i, nb - 1) - 1, 0), 0)
    return pl.pallas_call(
        body, name="attn_bwd", grid=(nb + 1,),
        in_specs=[pl.BlockSpec(memory_space=pltpu.SMEM),
                  pl.BlockSpec((BLK, AW), cur), pl.BlockSpec((BLK, 2 * KVW), cur), pl.BlockSpec((BLK, 2 * KVW), prev),
                  pl.BlockSpec((BLK, AW), cur), pl.BlockSpec((BLK, NQ), cur)],
        out_specs=[pl.BlockSpec((BLK, AW), cur),
                   pl.BlockSpec((BLK, 2 * KVW), lambda i: (jnp.maximum(i - 1, 0), 0)),
                   pl.BlockSpec((1, NQ), lambda i: (0, 0))],
        out_shape=[SDS((T, AW), BF16), SDS((T, 2 * KVW), BF16), SDS((1, NQ), F32)],
        scratch_shapes=[pltpu.VMEM((BLK, 2 * KVW), F32)],
        compiler_params=_cp(("arbitrary",)),
    )(sinks, q, kv, kv, datt, lse)


def _inv_counts(t0, rows):
    t = (t0 + lax.broadcasted_iota(jnp.int32, (rows, 1), 0) + 1).astype(F32)
    return [1.0 / jnp.minimum(t, float(w)) for w in POOL_WINDOWS]


def pool_fwd(z, wmix, scale):
    T = z.shape[0]
    TM = 512
    L = TM + HALO

    def body(z_ref, halo_ref, wmix_ref, scale_ref, pooled_ref, mixs_ref):
        i = pl.program_id(0)
        halo = jnp.where(i > 0, halo_ref[...], 0.0)
        zt = z_ref[...]
        e = jnp.concatenate([halo, zt], axis=0)
        sums = []
        s = e
        for k in (1, 2, 4, 8):
            s = s + pltpu.roll(s, k, 0)
            sums.append(s)
        inv = _inv_counts(i * TM, TM)
        for gi in range(len(POOL_WINDOWS)):
            cols = slice(gi * PG, (gi + 1) * PG)
            pooled = (sums[gi][HALO:, cols] * inv[gi] - zt[:, cols]).astype(BF16)
            pooled_ref[:, cols] = pooled
            mixs_ref[:, cols] = (_dot(pooled, wmix_ref[gi]) * scale_ref[:, cols]).astype(BF16)

    return pl.pallas_call(
        body, name="pool_fwd", grid=(T // TM,),
        in_specs=[pl.BlockSpec((TM, PW), lambda i: (i, 0)),
                  pl.BlockSpec((HALO, PW), lambda i: (jnp.maximum(i * (TM // HALO) - 1, 0), 0)),
                  _const_spec((len(POOL_WINDOWS), PG, PG)), _const_spec((1, PW))],
        out_specs=[pl.BlockSpec((TM, PW), lambda i: (i, 0)), pl.BlockSpec((TM, PW), lambda i: (i, 0))],
        out_shape=[SDS((T, PW), BF16), SDS((T, PW), BF16)],
        compiler_params=_cp(("arbitrary",)),
    )(z, z, wmix, scale)


def pool_bwd(dmixs, pooled, wmix, scale):
    T = dmixs.shape[0]
    TM = 512
    L = TM + HALO
    nt = T // TM

    def body(dm_ref, halo_ref, pooled_ref, wmix_ref, scale_ref, dz_ref, dwmix_ref, dscale_ref):
        i = pl.program_id(0)

        @pl.when(i == 0)
        def _():
            dwmix_ref[...] = jnp.zeros_like(dwmix_ref)
            dscale_ref[...] = jnp.zeros_like(dscale_ref)

        halo = jnp.where(i < nt - 1, halo_ref[...], 0.0)
        dm = dm_ref[...]
        e = jnp.concatenate([dm, halo], axis=0)
        inv = _inv_counts(i * TM, L)
        for gi in range(len(POOL_WINDOWS)):
            cols = slice(gi * PG, (gi + 1) * PG)
            w = wmix_ref[gi]
            dmixed = (e[:, cols] * scale_ref[:, cols]).astype(BF16)
            dpooled = _dot_nt(dmixed, w)
            pooled = pooled_ref[:, cols]
            mixed = _dot(pooled, w)
            dscale_ref[:, cols] += jnp.sum(dm[:, cols] * mixed, axis=0, keepdims=True)
            dwmix_ref[gi] += _dot_tn(pooled, dmixed[:TM])
            s = dpooled * inv[gi]
            k = 1
            while k < POOL_WINDOWS[gi]:
                s = s + pltpu.roll(s, L - k, 0)
                k *= 2
            dz_ref[:, cols] = (s[:TM] - dpooled[:TM]).astype(BF16)

    return pl.pallas_call(
        body, name="pool_bwd", grid=(nt,),
        in_specs=[pl.BlockSpec((TM, PW), lambda i: (i, 0)),
                  pl.BlockSpec((HALO, PW), lambda i: (jnp.minimum((i + 1) * (TM // HALO), T // HALO - 1), 0)),
                  pl.BlockSpec((TM, PW), lambda i: (i, 0)),
                  _const_spec((len(POOL_WINDOWS), PG, PG)), _const_spec((1, PW))],
        out_specs=[pl.BlockSpec((TM, PW), lambda i: (i, 0)),
                   pl.BlockSpec((len(POOL_WINDOWS), PG, PG), lambda i: (0, 0, 0)),
                   pl.BlockSpec((1, PW), lambda i: (0, 0))],
        out_shape=[SDS((T, PW), BF16), SDS((len(POOL_WINDOWS), PG, PG), F32), SDS((1, PW), F32)],
        compiler_params=_cp(("arbitrary",)),
    )(dmixs, dmixs, pooled, wmix, scale)


def merge_fwd(att, mixs, gt, h1, wattn, wpool, wout):
    T = h1.shape[0]
    TM = 512

    def body(att_ref, mixs_ref, gt_ref, h_ref, wa_ref, wp_ref, wo_ref, h2_ref, mg_ref):
        a = _dot(att_ref[...], wa_ref[...])
        p = _dot(mixs_ref[...], wp_ref[...])
        merged = (jax.nn.sigmoid(gt_ref[:, 0:D].astype(F32)) * a + jax.nn.sigmoid(gt_ref[:, D:2 * D].astype(F32)) * p)
        mg = merged.astype(BF16)
        mg_ref[...] = mg
        h2_ref[...] = h_ref[...] + _dot(mg, wo_ref[...])

    tile = lambda w: pl.BlockSpec((TM, w), lambda i: (i, 0))
    return pl.pallas_call(
        body, name="merge_fwd", grid=(T // TM,),
        in_specs=[tile(AW), tile(PW), tile(2 * D), tile(D), _const_spec((AW, D)), _const_spec((PW, D)),
                  _const_spec((D, D))],
        out_specs=[tile(D), tile(D)],
        out_shape=[SDS((T, D), F32), SDS((T, D), BF16)],
        compiler_params=_cp(("arbitrary",)),
    )(att, mixs, gt, h1, wattn, wpool, wout)


def merge_bwd(dh2, att, mixs, gt, wattn, wpool, wout):
    T = dh2.shape[0]
    TM = 512

    def body(dh2_ref, att_ref, mixs_ref, gt_ref, wa_ref, wp_ref, wo_ref, datt_ref, dmixs_ref, dgt_ref, da_ref, dp_ref):
        dm = _dot_nt(dh2_ref[...].astype(BF16), wo_ref[...])
        a = _dot(att_ref[...], wa_ref[...])
        p = _dot(mixs_ref[...], wp_ref[...])
        sa = jax.nn.sigmoid(gt_ref[:, 0:D].astype(F32))
        sp = jax.nn.sigmoid(gt_ref[:, D:2 * D].astype(F32))
        da = (dm * sa).astype(BF16)
        dp = (dm * sp).astype(BF16)
        da_ref[...] = da
        dp_ref[...] = dp
        dgt_ref[:, 0:D] = (dm * a * sa * (1.0 - sa)).astype(BF16)
        dgt_ref[:, D:2 * D] = (dm * p * sp * (1.0 - sp)).astype(BF16)
        datt_ref[...] = _dot_nt(da, wa_ref[...]).astype(BF16)
        dmixs_ref[...] = _dot_nt(dp, wp_ref[...])

    tile = lambda w: pl.BlockSpec((TM, w), lambda i: (i, 0))
    return pl.pallas_call(
        body, name="merge_bwd", grid=(T // TM,),
        in_specs=[tile(D), tile(AW), tile(PW), tile(2 * D), _const_spec((AW, D)), _const_spec((PW, D)),
                  _const_spec((D, D))],
        out_specs=[tile(AW), tile(PW), tile(2 * D), tile(D), tile(D)],
        out_shape=[SDS((T, AW), BF16), SDS((T, PW), F32), SDS((T, 2 * D), BF16), SDS((T, D), BF16),
                   SDS((T, D), BF16)],
        compiler_params=_cp(("arbitrary",)),
    )(dh2, att, mixs, gt, wattn, wpool, wout)


def loss_head(h3, target, g):
    T = h3.shape[0]
    TM = 512

    def body(h_ref, t_ref, g_ref, dh_ref, loss_ref, dg_ref):
        @pl.when(pl.program_id(0) == 0)
        def _():
            loss_ref[...] = jnp.zeros_like(loss_ref)
            dg_ref[...] = jnp.zeros_like(dg_ref)

        x = h_ref[...]
        g = g_ref[...]
        rstd = _rstd(x)
        xhat = x * rstd
        err = xhat * g - t_ref[...]
        loss_ref[...] += 0.5 * jnp.sum(jnp.mean(err * err, axis=-1, keepdims=True), axis=0, keepdims=True)
        dy = err * (1.0 / D)
        dg_ref[...] += jnp.sum(dy * xhat, axis=0, keepdims=True)
        dh_ref[...] = _rms_bwd(dy, xhat, rstd, g)

    tile = pl.BlockSpec((TM, D), lambda i: (i, 0))
    return pl.pallas_call(
        body, name="loss_head", grid=(T // TM,),
        in_specs=[tile, tile, _const_spec((1, D))],
        out_specs=[tile, pl.BlockSpec((1, 1), lambda i: (0, 0)), pl.BlockSpec((1, D), lambda i: (0, 0))],
        out_shape=[SDS((T, D), F32), SDS((1, 1), F32), SDS((1, D), F32)],
        compiler_params=_cp(("arbitrary",)),
    )(h3, target, g)


def adamw(w, g, m, v, name):
    R, C = w.shape
    tile_bytes = 2 * 1024 * 1024
    tr = R
    if R * C * 4 > tile_bytes:
        tr = next(cand for cand in (512, 256, 128, 64, 32, 16, 8) if R % cand == 0 and cand * C * 4 <= tile_bytes)
    c1 = 1.0 - B1 ** STEP
    c2 = 1.0 - B2 ** STEP

    def body(w_ref, g_ref, m_ref, v_ref, d_ref, nm_ref, nv_ref):
        gv = g_ref[...]
        nm = B1 * m_ref[...] + (1.0 - B1) * gv
        nv = B2 * v_ref[...] + (1.0 - B2) * (gv * gv)
        nm_ref[...] = nm
        nv_ref[...] = nv
        d_ref[...] = -LR * ((nm / c1) / (jnp.sqrt(nv / c2) + ADAM_EPS) + WD * w_ref[...])

    spec = pl.BlockSpec((tr, C), lambda i: (i, 0))
    return pl.pallas_call(
        body, name=name, grid=(R // tr,),
        in_specs=[spec] * 4, out_specs=[spec] * 3, out_shape=[SDS((R, C), F32)] * 3,
        compiler_params=_cp(("parallel",)),
    )(w, g, m, v)


GROUP_FFN1, GROUP_MIX, GROUP_FFN2 = 0, 1, 2


def local_fwd_bwd(x, target, S, get_w, put_g):
    W = get_w(GROUP_FFN1)
    h1, ab1, hid1 = ffn_fwd(x, S["ffn1_norm"], W["ffn1_w_up"], W["ffn1_w_down"], "ffn1_fwd")
    W = get_w(GROUP_MIX)
    u, q, kv, z, gt = mix_in_fwd(h1, S["mix_norm"], W["w_in"])
    att, lse = attn_fwd(q, kv, S["sinks"])
    pooled, mixs = pool_fwd(z, S["pool_w_mix"], S["pool_scale"])
    h2, merged = merge_fwd(att, mixs, gt, h1, W["w_attn_up"], W["w_pool_up"], W["w_out"])
    W = get_w(GROUP_FFN2)
    h3, ab2, hid2 = ffn_fwd(h2, S["ffn2_norm"], W["ffn2_w_up"], W["ffn2_w_down"], "ffn2_fwd")
    dh3, loss, g_final = loss_head(h3, target, S["final_norm"])

    G = {"final_norm": g_final}
    dh2, dup2, n2, G["ffn2_norm"] = ffn_bwd_x(dh3, h2, ab2, S["ffn2_norm"], W["ffn2_w_up"], W["ffn2_w_down"],
                                              "ffn2_bwd_x")
    put_g(GROUP_FFN2, {"ffn2_w_down": matmul_tn(hid2, dh3, "ffn2_dw_down", tm=1408, tn=D, b_scale=0.5),
                       "ffn2_w_up": matmul_tn(n2, dup2, "ffn2_dw_up", tm=D, tn=1408)})

    W = get_w(GROUP_MIX)
    datt, dmixs, dgt, da, dp = merge_bwd(dh2, att, mixs, gt, W["w_attn_up"], W["w_pool_up"], W["w_out"])
    g_mix = {"w_out": matmul_tn(merged, dh2, "dw_out", tm=D, tn=D),
             "w_attn_up": matmul_tn(att, da, "dw_attn_up", tm=AW, tn=D),
             "w_pool_up": matmul_tn(mixs, dp, "dw_pool_up", tm=PW, tn=D)}
    dz, G["pool_w_mix"], G["pool_scale"] = pool_bwd(dmixs, pooled, S["pool_w_mix"], S["pool_scale"])
    dq, dkv, G["sinks"] = attn_bwd(q, kv, datt, lse, S["sinks"])
    dh1, G["mix_norm"] = mix_in_bwd(dq, dkv, dz, dgt, dh2, h1, S["mix_norm"], W["w_in"])
    g_mix["w_in"] = jnp.concatenate([
        matmul_tn(u, dq, "dw_in_q", tm=D, tn=AW),
        matmul_tn(u, dkv, "dw_in_kv", tm=D, tn=2 * KVW),
        matmul_tn(u, dz, "dw_in_z", tm=D, tn=PW),
        matmul_tn(u, dgt, "dw_in_g", tm=D, tn=D),
    ], axis=1)
    put_g(GROUP_MIX, g_mix)

    W = get_w(GROUP_FFN1)
    g_dn1 = matmul_tn(hid1, dh1, "ffn1_dw_down", tm=1408, tn=D, b_scale=0.5)
    dx, dup1, n1, G["ffn1_norm"] = ffn_bwd_x(dh1, x, ab1, S["ffn1_norm"], W["ffn1_w_up"], W["ffn1_w_down"],
                                             "ffn1_bwd_x")
    put_g(GROUP_FFN1, {"ffn1_w_down": g_dn1, "ffn1_w_up": matmul_tn(n1, dup1, "ffn1_dw_up", tm=D, tn=1408)})
    return loss, dx, G


HBM_SPEC = pl.BlockSpec(memory_space=pltpu.HBM)


def _me():
    return lax.axis_index("x"), lax.axis_index("y"), lax.axis_index("c")


def _peer_chip(x, y, k):
    return x ^ (k >> 1), y ^ (k & 1)


def _piece_half(ref, rowlike, j, h):
    if rowlike:
        return ref.at[j, h]
    ns = ref.shape[-1] // N_CHIPS
    return ref.at[h, :, pl.ds(pl.multiple_of(j * ns, 128), ns)]


def _piece(ref, rowlike, j):
    if rowlike:
        return ref.at[j]
    ns = ref.shape[-1] // N_CHIPS
    return ref.at[:, :, pl.ds(pl.multiple_of(j * ns, 128), ns)]


def _full_shape(shard_view, rowlike):
    _, kh, ns = shard_view.shape
    return (N_CHIPS, 2, kh, ns) if rowlike else (2, kh, N_CHIPS * ns)


def _remote(src, dst, send_sem, recv_sem, dev):
    return pltpu.make_async_remote_copy(src, dst, send_sem, recv_sem, device_id=dev, device_id_type=MESH)


def gather_ici(shards, rowlikes, name):
    n = len(shards)

    def body(*refs):
        s_refs, f_refs = refs[:n], refs[n:2 * n]
        send_sems, recv_sems = refs[2 * n:]
        x, y, c = _me()
        chip = 2 * x + y
        sends = []
        for a in range(n):
            for k in (1, 2, 3):
                px, py = _peer_chip(x, y, k)
                cp = _remote(s_refs[a].at[c], _piece_half(f_refs[a], rowlikes[a], chip, c),
                             send_sems.at[a, k - 1], recv_sems.at[a, k - 1], (px, py, c))
                cp.start()
                sends.append(cp)
        for a in range(n):
            for k in (1, 2, 3):
                px, py = _peer_chip(x, y, k)
                slot = _piece_half(f_refs[a], rowlikes[a], 2 * px + py, c)
                _remote(slot, slot, send_sems.at[a, k - 1], recv_sems.at[a, k - 1], (px, py, c)).wait_recv()
        for cp in sends:
            cp.wait_send()

    return pl.pallas_call(
        body, name=name, in_specs=[HBM_SPEC] * n, out_specs=[HBM_SPEC] * n,
        out_shape=[SDS(_full_shape(s, r), s.dtype) for s, r in zip(shards, rowlikes)],
        scratch_shapes=[pltpu.SemaphoreType.DMA((n, 3)), pltpu.SemaphoreType.DMA((n, 3))],
    )(*shards)


def gather_finish(shards, fulls, rowlikes, name):
    n = len(shards)

    def body(*refs):
        s_refs, f_refs = refs[:n], refs[2 * n:3 * n]
        send_sems, recv_sems = refs[3 * n:]
        x, y, c = _me()
        chip = 2 * x + y
        sib = (x, y, 1 - c)
        sends = []
        for a in range(n):
            own = _piece(f_refs[a], rowlikes[a], chip)
            cp = _remote(s_refs[a], own, send_sems.at[a, 0], recv_sems.at[a, 0], sib)
            cp.start()
            sends.append(cp)
            for k in (1, 2, 3):
                px, py = _peer_chip(x, y, k)
                slot = _piece_half(f_refs[a], rowlikes[a], 2 * px + py, c)
                cp = _remote(slot, slot, send_sems.at[a, k], recv_sems.at[a, k], sib)
                cp.start()
                sends.append(cp)
        for a in range(n):
            own = _piece(f_refs[a], rowlikes[a], chip)
            _remote(own, own, send_sems.at[a, 0], recv_sems.at[a, 0], sib).wait_recv()
            for k in (1, 2, 3):
                px, py = _peer_chip(x, y, k)
                slot = _piece_half(f_refs[a], rowlikes[a], 2 * px + py, 1 - c)
                _remote(slot, slot, send_sems.at[a, k], recv_sems.at[a, k], sib).wait_recv()
        for cp in sends:
            cp.wait_send()

    return pl.pallas_call(
        body, name=name, in_specs=[HBM_SPEC] * (2 * n), out_specs=[HBM_SPEC] * n,
        out_shape=[SDS(f.shape, f.dtype) for f in fulls],
        input_output_aliases={n + a: a for a in range(n)},
        scratch_shapes=[pltpu.SemaphoreType.DMA((n, 4)), pltpu.SemaphoreType.DMA((n, 4))],
    )(*shards, *fulls)


def swap_halves(gviews, rowlikes, name):
    n = len(gviews)

    def half_shape(g, rowlike):
        return (N_CHIPS,) + g.shape[2:] if rowlike else g.shape[1:]

    def body(*refs):
        g_refs, got_refs = refs[:n], refs[n:2 * n]
        send_sems, recv_sems = refs[2 * n:]
        x, y, c = _me()
        copies = []
        for a in range(n):
            src = g_refs[a].at[:, 1 - c] if rowlikes[a] else g_refs[a].at[1 - c]
            cp = _remote(src, got_refs[a], send_sems.at[a], recv_sems.at[a], (x, y, 1 - c))
            cp.start()
            copies.append(cp)
        for cp in copies:
            cp.wait()

    return pl.pallas_call(
        body, name=name, in_specs=[HBM_SPEC] * n, out_specs=[HBM_SPEC] * n,
        out_shape=[SDS(half_shape(g, r), g.dtype) for g, r in zip(gviews, rowlikes)],
        scratch_shapes=[pltpu.SemaphoreType.DMA((n,)), pltpu.SemaphoreType.DMA((n,))],
    )(*gviews)


ROW_TILES = 2


def _piece_specs(rowlike, kh, ns, piece, half):
    tr = kh // ROW_TILES
    if rowlike:
        return (pl.BlockSpec((None, None, tr, ns), lambda *g: (piece(*g), half(*g), g[-2], 0)),
                pl.BlockSpec((None, tr, ns), lambda *g: (piece(*g), g[-2], 0)))
    return (pl.BlockSpec((None, tr, ns), lambda *g: (half(*g), g[-2], piece(*g))),
            pl.BlockSpec((tr, ns), lambda *g: (g[-2], piece(*g))))


def add_halves(gview, got, rowlike, place, name):
    kh, ns = (gview.shape[2], gview.shape[3]) if rowlike else (gview.shape[1], gview.shape[2] // N_CHIPS)

    def body(place_ref, g_ref, got_ref, o_ref):
        o_ref[...] = (g_ref[...] + got_ref[...]).astype(BF16)

    g_spec, h_spec = _piece_specs(rowlike, kh, ns, lambda j, r, p: j, lambda j, r, p: p[0])
    return pl.pallas_call(
        body, name=name,
        grid_spec=pltpu.PrefetchScalarGridSpec(num_scalar_prefetch=1, grid=(N_CHIPS, ROW_TILES),
                                               in_specs=[g_spec, h_spec], out_specs=h_spec),
        out_shape=SDS(got.shape, BF16),
        compiler_params=_cp(("parallel", "parallel")),
    )(place, gview, got)


def chip_all_to_all(qs, rowlikes, name):
    n = len(qs)

    def piece(ref, rowlike, j):
        if rowlike:
            return ref.at[j]
        ns = ref.shape[-1] // N_CHIPS
        return ref.at[:, pl.ds(pl.multiple_of(j * ns, 128), ns)]

    def slot_shape(q, rowlike):
        return (3,) + (q.shape[1:] if rowlike else (q.shape[0], q.shape[1] // N_CHIPS))

    def body(*refs):
        q_refs, r_refs = refs[:n], refs[n:2 * n]
        send_sems, recv_sems = refs[2 * n:]
        x, y, c = _me()
        sends = []
        for a in range(n):
            for k in (1, 2, 3):
                px, py = _peer_chip(x, y, k)
                cp = _remote(piece(q_refs[a], rowlikes[a], 2 * px + py), r_refs[a].at[k - 1],
                             send_sems.at[a, k - 1], recv_sems.at[a, k - 1], (px, py, c))
                cp.start()
                sends.append(cp)
        for a in range(n):
            for k in (1, 2, 3):
                px, py = _peer_chip(x, y, k)
                slot = r_refs[a].at[k - 1]
                _remote(slot, slot, send_sems.at[a, k - 1], recv_sems.at[a, k - 1], (px, py, c)).wait_recv()
        for cp in sends:
            cp.wait_send()

    return pl.pallas_call(
        body, name=name, in_specs=[HBM_SPEC] * n, out_specs=[HBM_SPEC] * n,
        out_shape=[SDS(slot_shape(q, r), q.dtype) for q, r in zip(qs, rowlikes)],
        scratch_shapes=[pltpu.SemaphoreType.DMA((n, 3)), pltpu.SemaphoreType.DMA((n, 3))],
    )(*qs)


def sum_pieces(gview, got, recv, rowlike, place, name):
    kh, ns = recv.shape[1], recv.shape[2]
    tr = kh // ROW_TILES

    def body(place_ref, g_ref, got_ref, r_ref, o_ref):
        acc = g_ref[...] + got_ref[...]
        for k in range(3):
            acc = acc + r_ref[k].astype(F32)
        o_ref[...] = acc

    g_spec, h_spec = _piece_specs(rowlike, kh, ns, lambda z, r, p: p[1], lambda z, r, p: p[0])
    return pl.pallas_call(
        body, name=name,
        grid_spec=pltpu.PrefetchScalarGridSpec(
            num_scalar_prefetch=1, grid=(1, ROW_TILES),
            in_specs=[g_spec, h_spec, pl.BlockSpec((3, tr, ns), lambda z, r, p: (0, r, 0))],
            out_specs=pl.BlockSpec((None, tr, ns), lambda z, r, p: (p[0], r, 0))),
        out_shape=SDS((2, kh, ns), F32),
        compiler_params=_cp(("parallel", "parallel")),
    )(place, gview, got, recv)


def join_halves(halves, name):
    n = len(halves)

    def body(*refs):
        o_refs = refs[n:2 * n]
        send_sems, recv_sems = refs[2 * n:]
        x, y, c = _me()
        sib = (x, y, 1 - c)
        sends = []
        for a in range(n):
            cp = _remote(o_refs[a].at[c], o_refs[a].at[c], send_sems.at[a], recv_sems.at[a], sib)
            cp.start()
            sends.append(cp)
        for a in range(n):
            got = o_refs[a].at[1 - c]
            _remote(got, got, send_sems.at[a], recv_sems.at[a], sib).wait_recv()
        for cp in sends:
            cp.wait_send()

    return pl.pallas_call(
        body, name=name, in_specs=[HBM_SPEC] * n, out_specs=[HBM_SPEC] * n,
        out_shape=[SDS(h.shape, h.dtype) for h in halves],
        input_output_aliases={a: a for a in range(n)},
        scratch_shapes=[pltpu.SemaphoreType.DMA((n,)), pltpu.SemaphoreType.DMA((n,))],
    )(*halves)


N_DEV = 8


def all_reduce_small(s):
    R, Wd = s.shape

    def body(s_ref, out_ref, slots_ref, send_sems, recv_sems):
        x, y, c = _me()
        me = 4 * x + 2 * y + c
        slots_ref[me] = s_ref[...]
        sends = []
        for k in range(1, N_DEV):
            peer = (x ^ (k >> 2), y ^ ((k >> 1) & 1), c ^ (k & 1))
            cp = pltpu.make_async_remote_copy(s_ref, slots_ref.at[me], send_sems.at[k - 1], recv_sems.at[k - 1],
                                              device_id=peer, device_id_type=MESH)
            cp.start()
            sends.append(cp)
        for k in range(1, N_DEV):
            peer = (x ^ (k >> 2), y ^ ((k >> 1) & 1), c ^ (k & 1))
            slot = slots_ref.at[4 * peer[0] + 2 * peer[1] + peer[2]]
            pltpu.make_async_remote_copy(s_ref, slot, send_sems.at[k - 1], recv_sems.at[k - 1],
                                         device_id=peer, device_id_type=MESH).wait_recv()
        for cp in sends:
            cp.wait_send()
        acc = slots_ref[0]
        for d in range(1, N_DEV):
            acc = acc + slots_ref[d]
        out_ref[...] = acc

    vmem = pl.BlockSpec(memory_space=pltpu.VMEM)
    return pl.pallas_call(
        body, name="all_reduce_small", in_specs=[vmem], out_specs=vmem, out_shape=SDS((R, Wd), F32),
        scratch_shapes=[pltpu.VMEM((N_DEV, R, Wd), F32), pltpu.SemaphoreType.DMA((N_DEV - 1,)),
                        pltpu.SemaphoreType.DMA((N_DEV - 1,))],
    )(s)


BIG = {"ffn1_w_up": (D, 2 * FF, "col"), "ffn1_w_down": (FF, D, "row"), "w_in": (D, INW, "cm"),
       "w_attn_up": (AW, D, "row"), "w_pool_up": (PW, D, "col"), "w_out": (D, D, "row"),
       "ffn2_w_up": (D, 2 * FF, "col"), "ffn2_w_down": (FF, D, "row")}
GROUPS = (("ffn1_w_up", "ffn1_w_down"), ("w_in", "w_attn_up", "w_pool_up", "w_out"), ("ffn2_w_up", "ffn2_w_down"))
SMALL = ("ffn1_norm", "mix_norm", "ffn2_norm", "final_norm", "pool_scale", "sinks", "pool_w_mix")
SMALL_W = 128


def _rowlike(name):
    return BIG[name][2] != "col"


def _half_dims(name):
    k, n, kind = BIG[name]
    return (k // N_CHIPS // 2, n) if kind == "row" else (k // 2, n // N_CHIPS)


def shard_view(name, shard):
    return shard.reshape((2,) + _half_dims(name))


def full_from_view(name, fv):
    k, n, kind = BIG[name]
    if kind == "cm":
        return fv.reshape(N_CHIPS, k, n // N_CHIPS).transpose(1, 0, 2).reshape(k, n)
    return fv.reshape(k, n)


def grad_view(name, g):
    k, n, kind = BIG[name]
    kh, ns = _half_dims(name)
    if kind == "cm":
        return g.reshape(k, N_CHIPS, ns).transpose(1, 0, 2).reshape(N_CHIPS, 2, kh, ns)
    return g.reshape(_full_shape(jax.ShapeDtypeStruct((2, kh, ns), g.dtype), kind == "row"))


def pack_small(d):
    parts = []
    for name in SMALL:
        a = d[name].reshape(-1)
        pad = (-a.shape[0]) % SMALL_W
        parts.append(jnp.pad(a, (0, pad)).reshape(-1, SMALL_W))
    a = jnp.concatenate(parts, axis=0)
    return jnp.pad(a, ((0, (-a.shape[0]) % 8), (0, 0)))


def unpack_small(a, like):
    out, r0 = {}, 0
    for name in SMALL:
        size = int(np.prod(like[name].shape))
        rows = -(-size // SMALL_W)
        out[name] = a[r0:r0 + rows].reshape(-1)[:size].reshape(like[name].shape)
        r0 += rows
    return out


WEIGHTS = ("ffn1_norm", "ffn1_w_up", "ffn1_w_down", "mix_norm", "w_in", "sinks", "w_attn_up", "pool_w_mix",
           "pool_scale", "w_pool_up", "w_out", "ffn2_norm", "ffn2_w_up", "ffn2_w_down", "final_norm")


def kernel(x, ffn1_norm, ffn1_w_up, ffn1_w_down, mix_norm, w_in, sinks, w_attn_up, pool_w_mix, pool_scale, w_pool_up, w_out, ffn2_norm, ffn2_w_up, ffn2_w_down, final_norm, loss_target, m_ffn1_norm, m_ffn1_w_up, m_ffn1_w_down, m_mix_norm, m_w_in, m_sinks, m_w_attn_up, m_pool_w_mix, m_pool_scale, m_w_pool_up, m_w_out, m_ffn2_norm, m_ffn2_w_up, m_ffn2_w_down, m_final_norm, v_ffn1_norm, v_ffn1_w_up, v_ffn1_w_down, v_mix_norm, v_w_in, v_sinks, v_w_attn_up, v_pool_w_mix, v_pool_scale, v_w_pool_up, v_w_out, v_ffn2_norm, v_ffn2_w_up, v_ffn2_w_down, v_final_norm):
    given = dict(locals())
    w = {n: given[n] for n in WEIGHTS}
    m = {n: given["m_" + n] for n in WEIGHTS}
    v = {n: given["v_" + n] for n in WEIGHTS}
    cx, cy, cc = _me()
    place = jnp.stack([cc, 2 * cx + cy]).astype(jnp.int32)

    shards = {n: w[n][0] for n in BIG}
    sviews = {n: shard_view(n, shards[n].astype(BF16)) for n in BIG}
    gathered = {}

    def get_w(group):
        if group not in gathered:
            names = GROUPS[group]
            rl = [_rowlike(n) for n in names]
            sv = [sviews[n] for n in names]
            fulls = gather_finish(sv, gather_ici(sv, rl, f"gather_ici_{group}"), rl, f"gather_finish_{group}")
            gathered[group] = {n: full_from_view(n, f) for n, f in zip(names, fulls)}
        return gathered[group]

    grads, delta, new_m, new_v = {}, {}, {}, {}

    def put_g(group, g):
        names = GROUPS[group]
        rl = [_rowlike(n) for n in names]
        gv = [grad_view(n, g[n]) for n in names]
        gots = swap_halves(gv, rl, f"swap_halves_{group}")
        qs = [add_halves(a, b, r, place, "add_halves_" + n) for a, b, r, n in zip(gv, gots, rl, names)]
        recvs = chip_all_to_all(qs, rl, f"chip_all_to_all_{group}")
        halves = [sum_pieces(a, b, r, k, place, "sum_pieces_" + n) for a, b, r, k, n in zip(gv, gots, recvs, rl, names)]
        for n, o in zip(names, join_halves(halves, f"join_halves_{group}")):
            grads[n] = o.reshape(shards[n].shape)
            delta[n], new_m[n], new_v[n] = adamw(shards[n], grads[n], m[n][0], v[n][0], "adamw_" + n)

    S = {n: w[n].reshape(1, -1) for n in ("ffn1_norm", "mix_norm", "ffn2_norm", "final_norm", "pool_scale", "sinks")}
    S["pool_w_mix"] = w["pool_w_mix"][0].astype(BF16)
    loss, dx, G = local_fwd_bwd(x[0], loss_target[0], S, get_w, put_g)
    loss = lax.psum(loss[0, 0], ("x", "y", "c"))

    small_like = {n: w[n] for n in SMALL}
    grads.update(unpack_small(all_reduce_small(pack_small({n: G[n] for n in SMALL})), small_like))

    ds, ms, vs = adamw(pack_small(w), pack_small(grads), pack_small(m), pack_small(v), "adamw_small")
    delta.update(unpack_small(ds, small_like))
    new_m.update(unpack_small(ms, small_like))
    new_v.update(unpack_small(vs, small_like))

    def shaped(d, n):
        return d[n].reshape(w[n].shape)

    return (loss, dx[None], *[shaped(grads, n) for n in WEIGHTS], *[shaped(delta, n) for n in WEIGHTS],
            *[shaped(new_m, n) for n in WEIGHTS], *[shaped(new_v, n) for n in WEIGHTS])
```

```python
import numpy as np
import jax
import jax.numpy as jnp
from jax import lax
from jax.experimental import pallas as pl
from jax.experimental.pallas import tpu as pltpu

F32 = jnp.float32
BF16 = jnp.bfloat16
SDS = jax.ShapeDtypeStruct
MESH = pl.DeviceIdType.MESH

D = 1024
FF = 2816
NQ = 16
NKV = 2
HD = 64
GQ = NQ // NKV
AW = NQ * HD
KVW = NKV * HD
BLK = 128
PW = 512
PG = 128
POOL_WINDOWS = (2, 4, 8, 16)
HALO = 16
INW = AW + 2 * KVW + PW + 2 * D
C_KV = AW
C_Z = AW + 2 * KVW
C_G = C_Z + PW
EPS = 1e-6
FF_CHUNK = 256
FF_CHUNKS = tuple((c, FF_CHUNK) for c in range(0, FF, FF_CHUNK))
SLOPES = tuple(float(2.0 ** (-8.0 * h / NQ)) for h in range(1, NQ + 1))
SCALE = HD ** -0.5

LR, B1, B2, ADAM_EPS, WD, STEP = 0.001, 0.9, 0.999, 1e-08, 0.01, 10

VMEM_LIMIT = 56 * 1024 * 1024
N_CHIPS = 4

NT = (((1,), (1,)), ((), ()))
TN = (((0,), (0,)), ((), ()))


def _cp(sem=None, vmem=VMEM_LIMIT):
    return pltpu.CompilerParams(dimension_semantics=sem, vmem_limit_bytes=vmem)


def _const_spec(shape):
    nd = len(shape)
    return pl.BlockSpec(shape, lambda *_: (0,) * nd, pipeline_mode=pl.Buffered(1))


def _rstd(x):
    return lax.rsqrt(jnp.mean(x * x, axis=-1, keepdims=True) + EPS)


def _rms_bwd(dn, xhat, rstd, g):
    dxhat = dn * g
    return rstd * (dxhat - xhat * jnp.mean(dxhat * xhat, axis=-1, keepdims=True))


def _dot(a, b):
    return jnp.dot(a, b, preferred_element_type=F32)


def _dot_nt(a, b):
    return lax.dot_general(a, b, NT, preferred_element_type=F32)


def _dot_tn(a, b):
    return lax.dot_general(a, b, TN, preferred_element_type=F32)


def ffn_fwd(h, g, wup, wdn, name):
    T = h.shape[0]
    TM = 512

    def body(h_ref, g_ref, wup_ref, wdn_ref, out_ref, ab_ref, hid_ref):
        x = h_ref[...]
        n = (x * _rstd(x) * g_ref[...]).astype(BF16)
        for c0, w in FF_CHUNKS:
            a = _dot(n, wup_ref[:, c0:c0 + w])
            b = _dot(n, wup_ref[:, FF + c0:FF + c0 + w])
            ab_ref[:, c0:c0 + w] = a.astype(BF16)
            ab_ref[:, FF + c0:FF + c0 + w] = b.astype(BF16)
            hid_ref[:, c0:c0 + w] = (a * jax.nn.sigmoid(a) * b).astype(BF16)
        out_ref[...] = x + 0.5 * _dot(hid_ref[...], wdn_ref[...])

    return pl.pallas_call(
        body, name=name, grid=(T // TM,),
        in_specs=[pl.BlockSpec((TM, D), lambda i: (i, 0)), _const_spec((1, D)),
                  _const_spec((D, 2 * FF)), _const_spec((FF, D))],
        out_specs=[pl.BlockSpec((TM, D), lambda i: (i, 0)), pl.BlockSpec((TM, 2 * FF), lambda i: (i, 0)),
                   pl.BlockSpec((TM, FF), lambda i: (i, 0))],
        out_shape=[SDS((T, D), F32), SDS((T, 2 * FF), BF16), SDS((T, FF), BF16)],
        compiler_params=_cp(("arbitrary",)),
    )(h, g, wup, wdn)


def ffn_bwd_x(dh, h_in, ab, g, wup, wdn, name):
    T = dh.shape[0]
    TM = 256

    def body(dh_ref, h_ref, ab_ref, g_ref, wup_ref, wdn_ref, dhin_ref, dup_ref, n_ref, dg_ref):
        x = h_ref[...]
        g = g_ref[...]
        rstd = _rstd(x)
        xhat = x * rstd
        n_ref[...] = (xhat * g).astype(BF16)
        dh = dh_ref[...]
        dhh = (0.5 * dh).astype(BF16)
        for c0, w in FF_CHUNKS:
            dhid = _dot_nt(dhh, wdn_ref[c0:c0 + w, :])
            a = ab_ref[:, c0:c0 + w].astype(F32)
            b = ab_ref[:, FF + c0:FF + c0 + w].astype(F32)
            sig = jax.nn.sigmoid(a)
            s = a * sig
            dup_ref[:, c0:c0 + w] = (dhid * b * (sig * (1.0 + a * (1.0 - sig)))).astype(BF16)
            dup_ref[:, FF + c0:FF + c0 + w] = (dhid * s).astype(BF16)
        dn = _dot_nt(dup_ref[...], wup_ref[...])
        dhin_ref[...] = dh + _rms_bwd(dn, xhat, rstd, g)

        @pl.when(pl.program_id(0) == 0)
        def _():
            dg_ref[...] = jnp.zeros_like(dg_ref)

        dg_ref[...] += jnp.sum(dn * xhat, axis=0, keepdims=True)

    tile = lambda w: pl.BlockSpec((TM, w), lambda i: (i, 0))
    return pl.pallas_call(
        body, name=name, grid=(T // TM,),
        in_specs=[tile(D), tile(D), tile(2 * FF), _const_spec((1, D)), _const_spec((D, 2 * FF)), _const_spec((FF, D))],
        out_specs=[tile(D), tile(2 * FF), tile(D), pl.BlockSpec((1, D), lambda i: (0, 0))],
        out_shape=[SDS((T, D), F32), SDS((T, 2 * FF), BF16), SDS((T, D), BF16), SDS((1, D), F32)],
        compiler_params=_cp(("arbitrary",)),
    )(dh, h_in, ab, g, wup, wdn)


def matmul_tn(a, b, name, *, tm, tn, tt=1024, b_scale=None):
    T, M = a.shape
    N = b.shape[1]
    assert M % tm == 0 and N % tn == 0 and T % tt == 0

    def body(a_ref, b_ref, o_ref):
        @pl.when(pl.program_id(2) == 0)
        def _():
            o_ref[...] = jnp.zeros_like(o_ref)

        bv = b_ref[...]
        if b_scale is not None:
            bv = bv * b_scale
        o_ref[...] += _dot_tn(a_ref[...].astype(BF16), bv.astype(BF16))

    return pl.pallas_call(
        body, name=name, grid=(M // tm, N // tn, T // tt),
        in_specs=[pl.BlockSpec((tt, tm), lambda i, j, t: (t, i)), pl.BlockSpec((tt, tn), lambda i, j, t: (t, j))],
        out_specs=pl.BlockSpec((tm, tn), lambda i, j, t: (i, j)),
        out_shape=SDS((M, N), F32),
        compiler_params=_cp(("parallel", "parallel", "arbitrary")),
    )(a, b)


def mix_in_fwd(h1, g, win):
    T = h1.shape[0]
    TM = 512

    def body(h_ref, g_ref, w_ref, u_ref, q_ref, kv_ref, z_ref, gt_ref):
        x = h_ref[...]
        u = (x * _rstd(x) * g_ref[...]).astype(BF16)
        u_ref[...] = u
        for c in range(0, AW, 256):
            q_ref[:, c:c + 256] = _dot(u, w_ref[:, c:c + 256]).astype(BF16)
        kv_ref[...] = _dot(u, w_ref[:, C_KV:C_Z]).astype(BF16)
        for c in range(0, PW, 256):
            z_ref[:, c:c + 256] = _dot(u, w_ref[:, C_Z + c:C_Z + c + 256])
        for c in range(0, 2 * D, 256):
            gt_ref[:, c:c + 256] = _dot(u, w_ref[:, C_G + c:C_G + c + 256]).astype(BF16)

    tile = lambda w: pl.BlockSpec((TM, w), lambda i: (i, 0))
    return pl.pallas_call(
        body, name="mix_in_fwd", grid=(T // TM,),
        in_specs=[tile(D), _const_spec((1, D)), _const_spec((D, INW))],
        out_specs=[tile(D), tile(AW), tile(2 * KVW), tile(PW), tile(2 * D)],
        out_shape=[SDS((T, D), BF16), SDS((T, AW), BF16), SDS((T, 2 * KVW), BF16), SDS((T, PW), F32),
                   SDS((T, 2 * D), BF16)],
        compiler_params=_cp(("arbitrary",)),
    )(h1, g, win)


def mix_in_bwd(dq, dkv, dz, dgt, dh2, h1, g, win):
    T = h1.shape[0]
    TM = 512

    def body(dq_ref, dkv_ref, dz_ref, dgt_ref, dh2_ref, h_ref, g_ref, w_ref, dh1_ref, dg_ref):
        du = _dot_nt(dq_ref[...], w_ref[:, 0:AW])
        du += _dot_nt(dkv_ref[...], w_ref[:, C_KV:C_Z])
        du += _dot_nt(dz_ref[...], w_ref[:, C_Z:C_G])
        du += _dot_nt(dgt_ref[...], w_ref[:, C_G:INW])
        x = h_ref[...]
        g = g_ref[...]
        rstd = _rstd(x)
        xhat = x * rstd
        dh1_ref[...] = dh2_ref[...] + _rms_bwd(du, xhat, rstd, g)

        @pl.when(pl.program_id(0) == 0)
        def _():
            dg_ref[...] = jnp.zeros_like(dg_ref)

        dg_ref[...] += jnp.sum(du * xhat, axis=0, keepdims=True)

    tile = lambda w: pl.BlockSpec((TM, w), lambda i: (i, 0))
    return pl.pallas_call(
        body, name="mix_in_bwd", grid=(T // TM,),
        in_specs=[tile(AW), tile(2 * KVW), tile(PW), tile(2 * D), tile(D), tile(D), _const_spec((1, D)),
                  _const_spec((D, INW))],
        out_specs=[tile(D), pl.BlockSpec((1, D), lambda i: (0, 0))],
        out_shape=[SDS((T, D), F32), SDS((1, D), F32)],
        compiler_params=_cp(("arbitrary",)),
    )(dq, dkv, dz, dgt, dh2, h1, g, win)


PAIR = 2 * HD
NPAIR = GQ // 2


def _lo_lanes():
    return lax.broadcasted_iota(jnp.int32, (BLK, PAIR), 1) < HD


def _stack_heads(ref, kvh, scale=None):
    lo = _lo_lanes()
    parts = []
    for pr in range(NPAIR):
        t = ref[:, (kvh * NPAIR + pr) * PAIR:(kvh * NPAIR + pr + 1) * PAIR]
        if scale is not None:
            t = t * scale
        zero = jnp.zeros_like(t)
        parts += [jnp.where(lo, t, zero), jnp.where(lo, zero, t)]
    return jnp.concatenate(parts, axis=0)


def _kv_tiles(kvc_ref, kvp_ref, tile, kvh):
    lo = _lo_lanes()
    dup, left, right = [], [], []
    for ref in (kvp_ref, kvc_ref):
        t = ref[:, tile * PAIR:(tile + 1) * PAIR]
        r = pltpu.roll(t.astype(F32), HD, 1).astype(BF16)
        zero = jnp.zeros_like(t)
        a, b = (t, r) if kvh == 0 else (r, t)
        dup.append(jnp.where(lo, a, b))
        left.append(jnp.where(lo, a, zero))
        right.append(jnp.where(lo, zero, b))
    cat = lambda xs: jnp.concatenate(xs, axis=0)
    return cat(dup), cat(left), cat(right)


def _band_consts(first):
    row = lax.broadcasted_iota(jnp.int32, (BLK, BLK), 0)
    col = lax.broadcasted_iota(jnp.int32, (BLK, BLK), 1)
    upper = col > row
    dist = jnp.where(upper, row - col + BLK, row - col).astype(F32)
    pen = jnp.where(jnp.logical_and(upper, first), -jnp.inf, 0.0)
    return upper, dist, pen


def _split_band(upper, t):
    zero = jnp.zeros_like(t)
    return jnp.concatenate([jnp.where(upper, t, zero), jnp.where(upper, zero, t)], axis=1)


def attn_fwd(q, kv, sinks):
    T = q.shape[0]
    nb = T // BLK

    def body(sink_ref, q_ref, kvc_ref, kvp_ref, att_ref, lse_ref):
        upper, dist, pen = _band_consts(pl.program_id(0) == 0)
        for kvh in range(NKV):
            kdup, _, _ = _kv_tiles(kvc_ref, kvp_ref, 0, kvh)
            _, vleft, vright = _kv_tiles(kvc_ref, kvp_ref, 1, kvh)
            s_all = _dot_nt(_stack_heads(q_ref, kvh, SCALE), kdup)
            for pr in range(NPAIR):
                out = None
                for side, vpad in ((0, vleft), (1, vright)):
                    g = 2 * pr + side
                    hq = kvh * GQ + g
                    sink = sink_ref[0, hq]
                    rows = slice(g * BLK, (g + 1) * BLK)
                    s = jnp.where(upper, s_all[rows, 0:BLK], s_all[rows, BLK:2 * BLK]) - SLOPES[hq] * dist + pen
                    m = jnp.maximum(jnp.max(s, axis=-1, keepdims=True), sink)
                    p = jnp.exp(s - m)
                    l = jnp.sum(p, axis=-1, keepdims=True) + jnp.exp(sink - m)
                    lse_ref[:, hq:hq + 1] = m + jnp.log(l)
                    o = _dot(_split_band(upper, (p * (1.0 / l)).astype(BF16)), vpad)
                    out = o if out is None else out + o
                col0 = (kvh * NPAIR + pr) * PAIR
                att_ref[:, col0:col0 + PAIR] = out.astype(BF16)

    return pl.pallas_call(
        body, name="attn_fwd", grid=(nb,),
        in_specs=[pl.BlockSpec(memory_space=pltpu.SMEM),
                  pl.BlockSpec((BLK, AW), lambda i: (i, 0)),
                  pl.BlockSpec((BLK, 2 * KVW), lambda i: (i, 0)),
                  pl.BlockSpec((BLK, 2 * KVW), lambda i: (jnp.maximum(i - 1, 0), 0))],
        out_specs=[pl.BlockSpec((BLK, AW), lambda i: (i, 0)), pl.BlockSpec((BLK, NQ), lambda i: (i, 0))],
        out_shape=[SDS((T, AW), BF16), SDS((T, NQ), F32)],
        compiler_params=_cp(("arbitrary",)),
    )(sinks, q, kv, kv)


def attn_bwd(q, kv, datt, lse, sinks):
    T = q.shape[0]
    nb = T // BLK

    def body(sink_ref, q_ref, kvc_ref, kvp_ref, do_ref, lse_ref, dq_ref, dkv_ref, dsink_ref, carry_ref):
        i = pl.program_id(0)

        @pl.when(i == 0)
        def _():
            dsink_ref[...] = jnp.zeros_like(dsink_ref)
            carry_ref[...] = jnp.zeros_like(carry_ref)

        @pl.when(i < nb)
        def _():
            upper, dist, pen = _band_consts(i == 0)
            lo = _lo_lanes()
            dk_dup, dv_dup = [], []
            for kvh in range(NKV):
                kdup, kleft, kright = _kv_tiles(kvc_ref, kvp_ref, 0, kvh)
                vdup, _, _ = _kv_tiles(kvc_ref, kvp_ref, 1, kvh)
                qs = _stack_heads(q_ref, kvh, SCALE)
                dos = _stack_heads(do_ref, kvh)
                s_all = _dot_nt(qs, kdup)
                dp_all = _dot_nt(dos, vdup)
                ds_parts, p_parts = [], []
                for pr in range(NPAIR):
                    dq = None
                    for side, kpad in ((0, kleft), (1, kright)):
                        g = 2 * pr + side
                        hq = kvh * GQ + g
                        lse_h = lse_ref[:, hq:hq + 1]
                        rows = slice(g * BLK, (g + 1) * BLK)
                        s = jnp.where(upper, s_all[rows, 0:BLK], s_all[rows, BLK:2 * BLK]) - SLOPES[hq] * dist + pen
                        p = jnp.exp(s - lse_h)
                        dp = jnp.where(upper, dp_all[rows, 0:BLK], dp_all[rows, BLK:2 * BLK])
                        delta = jnp.sum(p * dp, axis=-1, keepdims=True)
                        dsink_ref[:, hq:hq + 1] += -jnp.sum(jnp.exp(sink_ref[0, hq] - lse_h) * delta, axis=0,
                                                            keepdims=True)
                        ds = _split_band(upper, (p * (dp - delta)).astype(BF16))
                        ds_parts.append(ds)
                        p_parts.append(_split_band(upper, p.astype(BF16)))
                        d = _dot(ds, kpad)
                        dq = d if dq is None else dq + d
                    col0 = (kvh * NPAIR + pr) * PAIR
                    dq_ref[:, col0:col0 + PAIR] = (dq * SCALE).astype(BF16)
                dkw = _dot_tn(jnp.concatenate(ds_parts, axis=0), qs)
                dvw = _dot_tn(jnp.concatenate(p_parts, axis=0), dos)
                dk_dup.append(dkw + pltpu.roll(dkw, HD, 1))
                dv_dup.append(dvw + pltpu.roll(dvw, HD, 1))
            dk = jnp.where(jnp.concatenate([lo, lo], axis=0), dk_dup[0], dk_dup[1])
            dv = jnp.where(jnp.concatenate([lo, lo], axis=0), dv_dup[0], dv_dup[1])
            dkv_ref[:, 0:PAIR] = (carry_ref[:, 0:PAIR] + dk[0:BLK]).astype(BF16)
            dkv_ref[:, PAIR:2 * PAIR] = (carry_ref[:, PAIR:2 * PAIR] + dv[0:BLK]).astype(BF16)
            carry_ref[:, 0:PAIR] = dk[BLK:2 * BLK]
            carry_ref[:, PAIR:2 * PAIR] = dv[BLK:2 * BLK]

        @pl.when(i == nb)
        def _():
            dkv_ref[...] = carry_ref[...].astype(BF16)

    cur = lambda i: (jnp.minimum(i, nb - 1), 0)
    prev = lambda i: (jnp.maximum(jnp.minimum(i, nb - 1) - 1, 0), 0)
    return pl.pallas_call(
        body, name="attn_bwd", grid=(nb + 1,),
        in_specs=[pl.BlockSpec(memory_space=pltpu.SMEM),
                  pl.BlockSpec((BLK, AW), cur), pl.BlockSpec((BLK, 2 * KVW), cur), pl.BlockSpec((BLK, 2 * KVW), prev),
                  pl.BlockSpec((BLK, AW), cur), pl.BlockSpec((BLK, NQ), cur)],
        out_specs=[pl.BlockSpec((BLK, AW), cur),
                   pl.BlockSpec((BLK, 2 * KVW), lambda i: (jnp.maximum(i - 1, 0), 0)),
                   pl.BlockSpec((1, NQ), lambda i: (0, 0))],
        out_shape=[SDS((T, AW), BF16), SDS((T, 2 * KVW), BF16), SDS((1, NQ), F32)],
        scratch_shapes=[pltpu.VMEM((BLK, 2 * KVW), F32)],
        compiler_params=_cp(("arbitrary",)),
    )(sinks, q, kv, kv, datt, lse)


def _inv_counts(t0, rows):
    t = (t0 + lax.broadcasted_iota(jnp.int32, (rows, 1), 0) + 1).astype(F32)
    return [1.0 / jnp.minimum(t, float(w)) for w in POOL_WINDOWS]


def pool_fwd(z, wmix, scale):
    T = z.shape[0]
    TM = 512
    L = TM + HALO

    def body(z_ref, halo_ref, wmix_ref, scale_ref, pooled_ref, mixs_ref):
        i = pl.program_id(0)
        halo = jnp.where(i > 0, halo_ref[...], 0.0)
        zt = z_ref[...]
        e = jnp.concatenate([halo, zt], axis=0)
        sums = []
        s = e
        for k in (1, 2, 4, 8):
            s = s + pltpu.roll(s, k, 0)
            sums.append(s)
        inv = _inv_counts(i * TM, TM)
        for gi in range(len(POOL_WINDOWS)):
            cols = slice(gi * PG, (gi + 1) * PG)
            pooled = (sums[gi][HALO:, cols] * inv[gi] - zt[:, cols]).astype(BF16)
            pooled_ref[:, cols] = pooled
            mixs_ref[:, cols] = (_dot(pooled, wmix_ref[gi]) * scale_ref[:, cols]).astype(BF16)

    return pl.pallas_call(
        body, name="pool_fwd", grid=(T // TM,),
        in_specs=[pl.BlockSpec((TM, PW), lambda i: (i, 0)),
                  pl.BlockSpec((HALO, PW), lambda i: (jnp.maximum(i * (TM // HALO) - 1, 0), 0)),
                  _const_spec((len(POOL_WINDOWS), PG, PG)), _const_spec((1, PW))],
        out_specs=[pl.BlockSpec((TM, PW), lambda i: (i, 0)), pl.BlockSpec((TM, PW), lambda i: (i, 0))],
        out_shape=[SDS((T, PW), BF16), SDS((T, PW), BF16)],
        compiler_params=_cp(("arbitrary",)),
    )(z, z, wmix, scale)


def pool_bwd(dmixs, pooled, wmix, scale):
    T = dmixs.shape[0]
    TM = 512
    L = TM + HALO
    nt = T // TM

    def body(dm_ref, halo_ref, pooled_ref, wmix_ref, scale_ref, dz_ref, dwmix_ref, dscale_ref):
        i = pl.program_id(0)

        @pl.when(i == 0)
        def _():
            dwmix_ref[...] = jnp.zeros_like(dwmix_ref)
            dscale_ref[...] = jnp.zeros_like(dscale_ref)

        halo = jnp.where(i < nt - 1, halo_ref[...], 0.0)
        dm = dm_ref[...]
        e = jnp.concatenate([dm, halo], axis=0)
        inv = _inv_counts(i * TM, L)
        for gi in range(len(POOL_WINDOWS)):
            cols = slice(gi * PG, (gi + 1) * PG)
            w = wmix_ref[gi]
            dmixed = (e[:, cols] * scale_ref[:, cols]).astype(BF16)
            dpooled = _dot_nt(dmixed, w)
            pooled = pooled_ref[:, cols]
            mixed = _dot(pooled, w)
            dscale_ref[:, cols] += jnp.sum(dm[:, cols] * mixed, axis=0, keepdims=True)
            dwmix_ref[gi] += _dot_tn(pooled, dmixed[:TM])
            s = dpooled * inv[gi]
            k = 1
            while k < POOL_WINDOWS[gi]:
                s = s + pltpu.roll(s, L - k, 0)
                k *= 2
            dz_ref[:, cols] = (s[:TM] - dpooled[:TM]).astype(BF16)

    return pl.pallas_call(
        body, name="pool_bwd", grid=(nt,),
        in_specs=[pl.BlockSpec((TM, PW), lambda i: (i, 0)),
                  pl.BlockSpec((HALO, PW), lambda i: (jnp.minimum((i + 1) * (TM // HALO), T // HALO - 1), 0)),
                  pl.BlockSpec((TM, PW), lambda i: (i, 0)),
                  _const_spec((len(POOL_WINDOWS), PG, PG)), _const_spec((1, PW))],
        out_specs=[pl.BlockSpec((TM, PW), lambda i: (i, 0)),
                   pl.BlockSpec((len(POOL_WINDOWS), PG, PG), lambda i: (0, 0, 0)),
                   pl.BlockSpec((1, PW), lambda i: (0, 0))],
        out_shape=[SDS((T, PW), BF16), SDS((len(POOL_WINDOWS), PG, PG), F32), SDS((1, PW), F32)],
        compiler_params=_cp(("arbitrary",)),
    )(dmixs, dmixs, pooled, wmix, scale)


def merge_fwd(att, mixs, gt, h1, wattn, wpool, wout):
    T = h1.shape[0]
    TM = 512

    def body(att_ref, mixs_ref, gt_ref, h_ref, wa_ref, wp_ref, wo_ref, h2_ref, mg_ref):
        a = _dot(att_ref[...], wa_ref[...])
        p = _dot(mixs_ref[...], wp_ref[...])
        merged = (jax.nn.sigmoid(gt_ref[:, 0:D].astype(F32)) * a + jax.nn.sigmoid(gt_ref[:, D:2 * D].astype(F32)) * p)
        mg = merged.astype(BF16)
        mg_ref[...] = mg
        h2_ref[...] = h_ref[...] + _dot(mg, wo_ref[...])

    tile = lambda w: pl.BlockSpec((TM, w), lambda i: (i, 0))
    return pl.pallas_call(
        body, name="merge_fwd", grid=(T // TM,),
        in_specs=[tile(AW), tile(PW), tile(2 * D), tile(D), _const_spec((AW, D)), _const_spec((PW, D)),
                  _const_spec((D, D))],
        out_specs=[tile(D), tile(D)],
        out_shape=[SDS((T, D), F32), SDS((T, D), BF16)],
        compiler_params=_cp(("arbitrary",)),
    )(att, mixs, gt, h1, wattn, wpool, wout)


def merge_bwd(dh2, att, mixs, gt, wattn, wpool, wout):
    T = dh2.shape[0]
    TM = 512

    def body(dh2_ref, att_ref, mixs_ref, gt_ref, wa_ref, wp_ref, wo_ref, datt_ref, dmixs_ref, dgt_ref, da_ref, dp_ref):
        dm = _dot_nt(dh2_ref[...].astype(BF16), wo_ref[...])
        a = _dot(att_ref[...], wa_ref[...])
        p = _dot(mixs_ref[...], wp_ref[...])
        sa = jax.nn.sigmoid(gt_ref[:, 0:D].astype(F32))
        sp = jax.nn.sigmoid(gt_ref[:, D:2 * D].astype(F32))
        da = (dm * sa).astype(BF16)
        dp = (dm * sp).astype(BF16)
        da_ref[...] = da
        dp_ref[...] = dp
        dgt_ref[:, 0:D] = (dm * a * sa * (1.0 - sa)).astype(BF16)
        dgt_ref[:, D:2 * D] = (dm * p * sp * (1.0 - sp)).astype(BF16)
        datt_ref[...] = _dot_nt(da, wa_ref[...]).astype(BF16)
        dmixs_ref[...] = _dot_nt(dp, wp_ref[...])

    tile = lambda w: pl.BlockSpec((TM, w), lambda i: (i, 0))
    return pl.pallas_call(
        body, name="merge_bwd", grid=(T // TM,),
        in_specs=[tile(D), tile(AW), tile(PW), tile(2 * D), _const_spec((AW, D)), _const_spec((PW, D)),
                  _const_spec((D, D))],
        out_specs=[tile(AW), tile(PW), tile(2 * D), tile(D), tile(D)],
        out_shape=[SDS((T, AW), BF16), SDS((T, PW), F32), SDS((T, 2 * D), BF16), SDS((T, D), BF16),
                   SDS((T, D), BF16)],
        compiler_params=_cp(("arbitrary",)),
    )(dh2, att, mixs, gt, wattn, wpool, wout)


def loss_head(h3, target, g):
    T = h3.shape[0]
    TM = 512

    def body(h_ref, t_ref, g_ref, dh_ref, loss_ref, dg_ref):
        @pl.when(pl.program_id(0) == 0)
        def _():
            loss_ref[...] = jnp.zeros_like(loss_ref)
            dg_ref[...] = jnp.zeros_like(dg_ref)

        x = h_ref[...]
        g = g_ref[...]
        rstd = _rstd(x)
        xhat = x * rstd
        err = xhat * g - t_ref[...]
        loss_ref[...] += 0.5 * jnp.sum(jnp.mean(err * err, axis=-1, keepdims=True), axis=0, keepdims=True)
        dy = err * (1.0 / D)
        dg_ref[...] += jnp.sum(dy * xhat, axis=0, keepdims=True)
        dh_ref[...] = _rms_bwd(dy, xhat, rstd, g)

    tile = pl.BlockSpec((TM, D), lambda i: (i, 0))
    return pl.pallas_call(
        body, name="loss_head", grid=(T // TM,),
        in_specs=[tile, tile, _const_spec((1, D))],
        out_specs=[tile, pl.BlockSpec((1, 1), lambda i: (0, 0)), pl.BlockSpec((1, D), lambda i: (0, 0))],
        out_shape=[SDS((T, D), F32), SDS((1, 1), F32), SDS((1, D), F32)],
        compiler_params=_cp(("arbitrary",)),
    )(h3, target, g)


def adamw(w, g, m, v, name):
    R, C = w.shape
    tile_bytes = 2 * 1024 * 1024
    tr = R
    if R * C * 4 > tile_bytes:
        tr = next(cand for cand in (512, 256, 128, 64, 32, 16, 8) if R % cand == 0 and cand * C * 4 <= tile_bytes)
    c1 = 1.0 - B1 ** STEP
    c2 = 1.0 - B2 ** STEP

    def body(w_ref, g_ref, m_ref, v_ref, d_ref, nm_ref, nv_ref):
        gv = g_ref[...]
        nm = B1 * m_ref[...] + (1.0 - B1) * gv
        nv = B2 * v_ref[...] + (1.0 - B2) * (gv * gv)
        nm_ref[...] = nm
        nv_ref[...] = nv
        d_ref[...] = -LR * ((nm / c1) / (jnp.sqrt(nv / c2) + ADAM_EPS) + WD * w_ref[...])

    spec = pl.BlockSpec((tr, C), lambda i: (i, 0))
    return pl.pallas_call(
        body, name=name, grid=(R // tr,),
        in_specs=[spec] * 4, out_specs=[spec] * 3, out_shape=[SDS((R, C), F32)] * 3,
        compiler_params=_cp(("parallel",)),
    )(w, g, m, v)


GROUP_FFN1, GROUP_MIX, GROUP_FFN2 = 0, 1, 2


def _behind(small, token):
    return small if token is None else small + token[0:1, 0:1]


def local_fwd_bwd(x, target, S, comm):
    W = comm.weights(GROUP_FFN1, None)
    h1, ab1, hid1 = ffn_fwd(x, _behind(S["ffn1_norm"], comm.started()), W["ffn1_w_up"], W["ffn1_w_down"], "ffn1_fwd")
    W = comm.weights(GROUP_MIX, h1)
    u, q, kv, z, gt = mix_in_fwd(h1, S["mix_norm"], W["w_in"])
    att, lse = attn_fwd(q, kv, S["sinks"])
    pooled, mixs = pool_fwd(z, S["pool_w_mix"], S["pool_scale"])
    h2, merged = merge_fwd(att, mixs, gt, h1, W["w_attn_up"], W["w_pool_up"], W["w_out"])
    W = comm.weights(GROUP_FFN2, h2)
    h3, ab2, hid2 = ffn_fwd(h2, S["ffn2_norm"], W["ffn2_w_up"], W["ffn2_w_down"], "ffn2_fwd")
    dh3, loss, g_final = loss_head(h3, target, S["final_norm"])

    G = {"final_norm": g_final}
    dh2, dup2, n2, G["ffn2_norm"] = ffn_bwd_x(dh3, h2, ab2, S["ffn2_norm"], W["ffn2_w_up"], W["ffn2_w_down"],
                                              "ffn2_bwd_x")
    token = comm.grads(GROUP_FFN2, {"ffn2_w_down": matmul_tn(hid2, dh3, "ffn2_dw_down", tm=1408, tn=D, b_scale=0.5),
                                    "ffn2_w_up": matmul_tn(n2, dup2, "ffn2_dw_up", tm=D, tn=1408)})

    W = comm.weights(GROUP_MIX, None)
    datt, dmixs, dgt, da, dp = merge_bwd(dh2, att, mixs, gt, W["w_attn_up"], W["w_pool_up"], W["w_out"])
    g_mix = {"w_out": matmul_tn(merged, dh2, "dw_out", tm=D, tn=D),
             "w_attn_up": matmul_tn(att, da, "dw_attn_up", tm=AW, tn=D),
             "w_pool_up": matmul_tn(mixs, dp, "dw_pool_up", tm=PW, tn=D)}
    dz, G["pool_w_mix"], G["pool_scale"] = pool_bwd(dmixs, pooled, S["pool_w_mix"], _behind(S["pool_scale"], token))
    dq, dkv, G["sinks"] = attn_bwd(q, kv, datt, lse, _behind(S["sinks"], token))
    dh1, G["mix_norm"] = mix_in_bwd(dq, dkv, dz, dgt, dh2, h1, S["mix_norm"], W["w_in"])
    g_mix["w_in"] = jnp.concatenate([
        matmul_tn(u, dq, "dw_in_q", tm=D, tn=AW),
        matmul_tn(u, dkv, "dw_in_kv", tm=D, tn=2 * KVW),
        matmul_tn(u, dz, "dw_in_z", tm=D, tn=PW),
        matmul_tn(u, dgt, "dw_in_g", tm=D, tn=D),
    ], axis=1)
    comm.finish(GROUP_FFN2, dh1)
    token = comm.grads(GROUP_MIX, g_mix)

    W = comm.weights(GROUP_FFN1, None)
    g_dn1 = matmul_tn(hid1, dh1, "ffn1_dw_down", tm=1408, tn=D, b_scale=0.5)
    dx, dup1, n1, G["ffn1_norm"] = ffn_bwd_x(dh1, x, ab1, _behind(S["ffn1_norm"], token), W["ffn1_w_up"],
                                             W["ffn1_w_down"], "ffn1_bwd_x")
    g_up1 = matmul_tn(n1, dup1, "ffn1_dw_up", tm=D, tn=1408)
    comm.finish(GROUP_MIX, g_up1)
    token = comm.grads(GROUP_FFN1, {"ffn1_w_down": g_dn1, "ffn1_w_up": g_up1})
    comm.finish(GROUP_FFN1, token)
    return loss, dx, G


HBM_SPEC = pl.BlockSpec(memory_space=pltpu.HBM)


def _me():
    return lax.axis_index("x"), lax.axis_index("y"), lax.axis_index("c")


def _peer_chip(x, y, k):
    return x ^ (k >> 1), y ^ (k & 1)


def _piece_half(ref, rowlike, j, h):
    if rowlike:
        return ref.at[j, h]
    ns = ref.shape[-1] // N_CHIPS
    return ref.at[h, :, pl.ds(pl.multiple_of(j * ns, 128), ns)]


def _piece(ref, rowlike, j):
    if rowlike:
        return ref.at[j]
    ns = ref.shape[-1] // N_CHIPS
    return ref.at[:, :, pl.ds(pl.multiple_of(j * ns, 128), ns)]


def _full_shape(shard_view, rowlike):
    _, kh, ns = shard_view.shape
    return (N_CHIPS, 2, kh, ns) if rowlike else (2, kh, N_CHIPS * ns)


def _remote(src, dst, send_sem, recv_sem, dev):
    return pltpu.make_async_remote_copy(src, dst, send_sem, recv_sem, device_id=dev, device_id_type=MESH)


SEM_SPEC = pl.BlockSpec(memory_space=pltpu.SEMAPHORE)
SPLIT_PARAMS = pltpu.CompilerParams(has_side_effects=pltpu.SideEffectType.DATAFLOW_SIDE_EFFECTING)


def _gather_plan(rowlikes):
    def plan(s_refs, f_refs, a, k, x, y, c):
        px, py = _peer_chip(x, y, k)
        return (s_refs[a].at[c], _piece_half(f_refs[a], rowlikes[a], 2 * x + y, c),
                _piece_half(f_refs[a], rowlikes[a], 2 * px + py, c), (px, py, c))
    return plan


def _all_to_all_plan(rowlikes):
    def plan(q_refs, r_refs, a, k, x, y, c):
        px, py = _peer_chip(x, y, k)
        if rowlikes[a]:
            src = q_refs[a].at[2 * px + py]
        else:
            ns = q_refs[a].shape[-1] // N_CHIPS
            src = q_refs[a].at[:, pl.ds(pl.multiple_of((2 * px + py) * ns, 128), ns)]
        return src, r_refs[a].at[k - 1], r_refs[a].at[k - 1], (px, py, c)
    return plan


def exchange_start(srcs, land_shapes, plan, name):
    n = len(srcs)
    lands = [lax.empty(shape, s.dtype) for shape, s in zip(land_shapes, srcs)]

    def body(*refs):
        s_refs, l_refs = refs[:n], refs[n:2 * n]
        send_sems, recv_sems = refs[2 * n], refs[2 * n + 1]
        token = refs[-1]
        x, y, c = _me()
        for a in range(n):
            for k in (1, 2, 3):
                src, dst, _, peer = plan(s_refs, l_refs, a, k, x, y, c)
                _remote(src, dst, send_sems.at[3 * a + k - 1], recv_sems.at[3 * a + k - 1], peer).start()
        token[...] = jnp.zeros_like(token)

    outs = pl.pallas_call(
        body, name=name, in_specs=[HBM_SPEC] * (2 * n),
        out_specs=[SEM_SPEC, SEM_SPEC] + [HBM_SPEC] * (2 * n) + [pl.BlockSpec(memory_space=pltpu.VMEM)],
        out_shape=[pltpu.SemaphoreType.DMA((3 * n,)), pltpu.SemaphoreType.DMA((3 * n,))]
        + [pltpu.HBM(a.shape, a.dtype) for a in (*srcs, *lands)] + [SDS((8, 128), F32)],
        input_output_aliases={i: 2 + i for i in range(2 * n)},
        compiler_params=SPLIT_PARAMS,
    )(*[pltpu.with_memory_space_constraint(a, pltpu.HBM) for a in (*srcs, *lands)])
    return {"sems": outs[:2], "srcs": outs[2:2 + n], "lands": outs[2 + n:2 + 2 * n], "token": outs[-1]}


def exchange_wait(state, plan, after, name):
    n = len(state["srcs"])

    def body(*refs):
        s_refs, l_refs = refs[:n], refs[n:2 * n]
        send_sems, recv_sems = refs[2 * n], refs[2 * n + 1]
        x, y, c = _me()
        for a in range(n):
            for k in (1, 2, 3):
                src, _, landing, peer = plan(s_refs, l_refs, a, k, x, y, c)
                cp = _remote(src, landing, send_sems.at[3 * a + k - 1], recv_sems.at[3 * a + k - 1], peer)
                cp.wait_send()
                cp.wait_recv()

    bufs = (*state["srcs"], *state["lands"])
    outs = pl.pallas_call(
        body, name=name,
        in_specs=[HBM_SPEC] * (2 * n) + [SEM_SPEC, SEM_SPEC, pl.BlockSpec(memory_space=pl.ANY)],
        out_specs=[HBM_SPEC] * (2 * n),
        out_shape=[pltpu.HBM(a.shape, a.dtype) for a in bufs],
        input_output_aliases={i: i for i in range(2 * n)},
        compiler_params=SPLIT_PARAMS,
    )(*bufs, *state["sems"], after)
    return outs[n:]


def gather_finish(shards, fulls, rowlikes, name):
    n = len(shards)

    def body(*refs):
        s_refs, f_refs = refs[:n], refs[2 * n:3 * n]
        send_sems, recv_sems = refs[3 * n:]
        x, y, c = _me()
        chip = 2 * x + y
        sib = (x, y, 1 - c)
        sends = []
        for a in range(n):
            own = _piece(f_refs[a], rowlikes[a], chip)
            cp = _remote(s_refs[a], own, send_sems.at[a, 0], recv_sems.at[a, 0], sib)
            cp.start()
            sends.append(cp)
            for k in (1, 2, 3):
                px, py = _peer_chip(x, y, k)
                slot = _piece_half(f_refs[a], rowlikes[a], 2 * px + py, c)
                cp = _remote(slot, slot, send_sems.at[a, k], recv_sems.at[a, k], sib)
                cp.start()
                sends.append(cp)
        for a in range(n):
            own = _piece(f_refs[a], rowlikes[a], chip)
            _remote(own, own, send_sems.at[a, 0], recv_sems.at[a, 0], sib).wait_recv()
            for k in (1, 2, 3):
                px, py = _peer_chip(x, y, k)
                slot = _piece_half(f_refs[a], rowlikes[a], 2 * px + py, 1 - c)
                _remote(slot, slot, send_sems.at[a, k], recv_sems.at[a, k], sib).wait_recv()
        for cp in sends:
            cp.wait_send()

    return pl.pallas_call(
        body, name=name, in_specs=[HBM_SPEC] * (2 * n), out_specs=[HBM_SPEC] * n,
        out_shape=[SDS(f.shape, f.dtype) for f in fulls],
        input_output_aliases={n + a: a for a in range(n)},
        scratch_shapes=[pltpu.SemaphoreType.DMA((n, 4)), pltpu.SemaphoreType.DMA((n, 4))],
    )(*shards, *fulls)


def swap_halves(gviews, rowlikes, name):
    n = len(gviews)

    def half_shape(g, rowlike):
        return (N_CHIPS,) + g.shape[2:] if rowlike else g.shape[1:]

    def body(*refs):
        g_refs, got_refs = refs[:n], refs[n:2 * n]
        send_sems, recv_sems = refs[2 * n:]
        x, y, c = _me()
        copies = []
        for a in range(n):
            src = g_refs[a].at[:, 1 - c] if rowlikes[a] else g_refs[a].at[1 - c]
            cp = _remote(src, got_refs[a], send_sems.at[a], recv_sems.at[a], (x, y, 1 - c))
            cp.start()
            copies.append(cp)
        for cp in copies:
            cp.wait()

    return pl.pallas_call(
        body, name=name, in_specs=[HBM_SPEC] * n, out_specs=[HBM_SPEC] * n,
        out_shape=[SDS(half_shape(g, r), g.dtype) for g, r in zip(gviews, rowlikes)],
        scratch_shapes=[pltpu.SemaphoreType.DMA((n,)), pltpu.SemaphoreType.DMA((n,))],
    )(*gviews)


ROW_TILES = 2


def _piece_specs(rowlike, kh, ns, piece, half):
    tr = kh // ROW_TILES
    if rowlike:
        return (pl.BlockSpec((None, None, tr, ns), lambda *g: (piece(*g), half(*g), g[-2], 0)),
                pl.BlockSpec((None, tr, ns), lambda *g: (piece(*g), g[-2], 0)))
    return (pl.BlockSpec((None, tr, ns), lambda *g: (half(*g), g[-2], piece(*g))),
            pl.BlockSpec((tr, ns), lambda *g: (g[-2], piece(*g))))


def add_halves(gview, got, rowlike, place, name):
    kh, ns = (gview.shape[2], gview.shape[3]) if rowlike else (gview.shape[1], gview.shape[2] // N_CHIPS)

    def body(place_ref, g_ref, got_ref, o_ref):
        o_ref[...] = (g_ref[...] + got_ref[...]).astype(BF16)

    g_spec, h_spec = _piece_specs(rowlike, kh, ns, lambda j, r, p: j, lambda j, r, p: p[0])
    return pl.pallas_call(
        body, name=name,
        grid_spec=pltpu.PrefetchScalarGridSpec(num_scalar_prefetch=1, grid=(N_CHIPS, ROW_TILES),
                                               in_specs=[g_spec, h_spec], out_specs=h_spec),
        out_shape=SDS(got.shape, BF16),
        compiler_params=_cp(("parallel", "parallel")),
    )(place, gview, got)


def _slot_shape(q, rowlike):
    return (3,) + (q.shape[1:] if rowlike else (q.shape[0], q.shape[1] // N_CHIPS))


def sum_pieces(gview, got, recv, rowlike, place, name):
    kh, ns = recv.shape[1], recv.shape[2]
    tr = kh // ROW_TILES

    def body(place_ref, g_ref, got_ref, r_ref, o_ref):
        acc = g_ref[...] + got_ref[...]
        for k in range(3):
            acc = acc + r_ref[k].astype(F32)
        o_ref[...] = acc

    g_spec, h_spec = _piece_specs(rowlike, kh, ns, lambda z, r, p: p[1], lambda z, r, p: p[0])
    return pl.pallas_call(
        body, name=name,
        grid_spec=pltpu.PrefetchScalarGridSpec(
            num_scalar_prefetch=1, grid=(1, ROW_TILES),
            in_specs=[g_spec, h_spec, pl.BlockSpec((3, tr, ns), lambda z, r, p: (0, r, 0))],
            out_specs=pl.BlockSpec((None, tr, ns), lambda z, r, p: (p[0], r, 0))),
        out_shape=SDS((2, kh, ns), F32),
        compiler_params=_cp(("parallel", "parallel")),
    )(place, gview, got, recv)


def join_halves(halves, name):
    n = len(halves)

    def body(*refs):
        o_refs = refs[n:2 * n]
        send_sems, recv_sems = refs[2 * n:]
        x, y, c = _me()
        sib = (x, y, 1 - c)
        sends = []
        for a in range(n):
            cp = _remote(o_refs[a].at[c], o_refs[a].at[c], send_sems.at[a], recv_sems.at[a], sib)
            cp.start()
            sends.append(cp)
        for a in range(n):
            got = o_refs[a].at[1 - c]
            _remote(got, got, send_sems.at[a], recv_sems.at[a], sib).wait_recv()
        for cp in sends:
            cp.wait_send()

    return pl.pallas_call(
        body, name=name, in_specs=[HBM_SPEC] * n, out_specs=[HBM_SPEC] * n,
        out_shape=[SDS(h.shape, h.dtype) for h in halves],
        input_output_aliases={a: a for a in range(n)},
        scratch_shapes=[pltpu.SemaphoreType.DMA((n,)), pltpu.SemaphoreType.DMA((n,))],
    )(*halves)


N_DEV = 8


def all_reduce_small(s):
    R, Wd = s.shape

    def body(s_ref, out_ref, slots_ref, send_sems, recv_sems):
        x, y, c = _me()
        me = 4 * x + 2 * y + c
        slots_ref[me] = s_ref[...]
        sends = []
        for k in range(1, N_DEV):
            peer = (x ^ (k >> 2), y ^ ((k >> 1) & 1), c ^ (k & 1))
            cp = pltpu.make_async_remote_copy(s_ref, slots_ref.at[me], send_sems.at[k - 1], recv_sems.at[k - 1],
                                              device_id=peer, device_id_type=MESH)
            cp.start()
            sends.append(cp)
        for k in range(1, N_DEV):
            peer = (x ^ (k >> 2), y ^ ((k >> 1) & 1), c ^ (k & 1))
            slot = slots_ref.at[4 * peer[0] + 2 * peer[1] + peer[2]]
            pltpu.make_async_remote_copy(s_ref, slot, send_sems.at[k - 1], recv_sems.at[k - 1],
                                         device_id=peer, device_id_type=MESH).wait_recv()
        for cp in sends:
            cp.wait_send()
        acc = slots_ref[0]
        for d in range(1, N_DEV):
            acc = acc + slots_ref[d]
        out_ref[...] = acc

    vmem = pl.BlockSpec(memory_space=pltpu.VMEM)
    return pl.pallas_call(
        body, name="all_reduce_small", in_specs=[vmem], out_specs=vmem, out_shape=SDS((R, Wd), F32),
        scratch_shapes=[pltpu.VMEM((N_DEV, R, Wd), F32), pltpu.SemaphoreType.DMA((N_DEV - 1,)),
                        pltpu.SemaphoreType.DMA((N_DEV - 1,))],
    )(s)


BIG = {"ffn1_w_up": (D, 2 * FF, "col"), "ffn1_w_down": (FF, D, "row"), "w_in": (D, INW, "cm"),
       "w_attn_up": (AW, D, "row"), "w_pool_up": (PW, D, "col"), "w_out": (D, D, "row"),
       "ffn2_w_up": (D, 2 * FF, "col"), "ffn2_w_down": (FF, D, "row")}
GROUPS = (("ffn1_w_up", "ffn1_w_down"), ("w_in", "w_attn_up", "w_pool_up", "w_out"), ("ffn2_w_up", "ffn2_w_down"))
SMALL = ("ffn1_norm", "mix_norm", "ffn2_norm", "final_norm", "pool_scale", "sinks", "pool_w_mix")
SMALL_W = 128


def _rowlike(name):
    return BIG[name][2] != "col"


def _half_dims(name):
    k, n, kind = BIG[name]
    return (k // N_CHIPS // 2, n) if kind == "row" else (k // 2, n // N_CHIPS)


def shard_view(name, shard):
    return shard.reshape((2,) + _half_dims(name))


def full_from_view(name, fv):
    k, n, kind = BIG[name]
    if kind == "cm":
        return fv.reshape(N_CHIPS, k, n // N_CHIPS).transpose(1, 0, 2).reshape(k, n)
    return fv.reshape(k, n)


def grad_view(name, g):
    k, n, kind = BIG[name]
    kh, ns = _half_dims(name)
    if kind == "cm":
        return g.reshape(k, N_CHIPS, ns).transpose(1, 0, 2).reshape(N_CHIPS, 2, kh, ns)
    return g.reshape(_full_shape(jax.ShapeDtypeStruct((2, kh, ns), g.dtype), kind == "row"))


def pack_small(d):
    parts = []
    for name in SMALL:
        a = d[name].reshape(-1)
        pad = (-a.shape[0]) % SMALL_W
        parts.append(jnp.pad(a, (0, pad)).reshape(-1, SMALL_W))
    a = jnp.concatenate(parts, axis=0)
    return jnp.pad(a, ((0, (-a.shape[0]) % 8), (0, 0)))


def unpack_small(a, like):
    out, r0 = {}, 0
    for name in SMALL:
        size = int(np.prod(like[name].shape))
        rows = -(-size // SMALL_W)
        out[name] = a[r0:r0 + rows].reshape(-1)[:size].reshape(like[name].shape)
        r0 += rows
    return out


WEIGHTS = ("ffn1_norm", "ffn1_w_up", "ffn1_w_down", "mix_norm", "w_in", "sinks", "w_attn_up", "pool_w_mix",
           "pool_scale", "w_pool_up", "w_out", "ffn2_norm", "ffn2_w_up", "ffn2_w_down", "final_norm")


def kernel(x, ffn1_norm, ffn1_w_up, ffn1_w_down, mix_norm, w_in, sinks, w_attn_up, pool_w_mix, pool_scale, w_pool_up, w_out, ffn2_norm, ffn2_w_up, ffn2_w_down, final_norm, loss_target, m_ffn1_norm, m_ffn1_w_up, m_ffn1_w_down, m_mix_norm, m_w_in, m_sinks, m_w_attn_up, m_pool_w_mix, m_pool_scale, m_w_pool_up, m_w_out, m_ffn2_norm, m_ffn2_w_up, m_ffn2_w_down, m_final_norm, v_ffn1_norm, v_ffn1_w_up, v_ffn1_w_down, v_mix_norm, v_w_in, v_sinks, v_w_attn_up, v_pool_w_mix, v_pool_scale, v_w_pool_up, v_w_out, v_ffn2_norm, v_ffn2_w_up, v_ffn2_w_down, v_final_norm):
    given = dict(locals())
    w = {n: given[n] for n in WEIGHTS}
    m = {n: given["m_" + n] for n in WEIGHTS}
    v = {n: given["v_" + n] for n in WEIGHTS}
    cx, cy, cc = _me()
    place = jnp.stack([cc, 2 * cx + cy]).astype(jnp.int32)

    shards = {n: w[n][0] for n in BIG}
    sviews = {n: shard_view(n, shards[n].astype(BF16)) for n in BIG}
    grads, delta, new_m, new_v = {}, {}, {}, {}
    rowlikes = [[_rowlike(n) for n in names] for names in GROUPS]

    class Exchanges:
        def __init__(self):
            self.gathers, self.fulls, self.reductions = {}, {}, {}

        def weights(self, group, after):
            if not self.gathers:
                for gi, names in enumerate(GROUPS):
                    sv = [sviews[n] for n in names]
                    self.gathers[gi] = exchange_start(sv, [_full_shape(s, r) for s, r in zip(sv, rowlikes[gi])],
                                                      _gather_plan(rowlikes[gi]), f"gather_start_{gi}")
            if group not in self.fulls:
                names, rl, state = GROUPS[group], rowlikes[group], self.gathers[group]
                fulls = exchange_wait(state, _gather_plan(rl), state["token"] if after is None else after,
                                      f"gather_wait_{group}")
                fulls = gather_finish([sviews[n] for n in names], fulls, rl, f"gather_finish_{group}")
                self.fulls[group] = {n: full_from_view(n, f) for n, f in zip(names, fulls)}
            return self.fulls[group]

        def started(self):
            return self.gathers[GROUP_MIX]["token"] + self.gathers[GROUP_FFN2]["token"]

        def grads(self, group, g):
            names, rl = GROUPS[group], rowlikes[group]
            gv = [grad_view(n, g[n]) for n in names]
            gots = swap_halves(gv, rl, f"swap_halves_{group}")
            qs = [add_halves(a, b, r, place, "add_halves_" + n) for a, b, r, n in zip(gv, gots, rl, names)]
            state = exchange_start(qs, [_slot_shape(q, r) for q, r in zip(qs, rl)], _all_to_all_plan(rl),
                                   f"all_to_all_start_{group}")
            self.reductions[group] = (gv, gots, state)
            return state["token"]

        def finish(self, group, after):
            names, rl = GROUPS[group], rowlikes[group]
            gv, gots, state = self.reductions.pop(group)
            recvs = exchange_wait(state, _all_to_all_plan(rl), after, f"all_to_all_wait_{group}")
            halves = [sum_pieces(a, b, r, k, place, "sum_pieces_" + n)
                      for a, b, r, k, n in zip(gv, gots, recvs, rl, names)]
            for n, o in zip(names, join_halves(halves, f"join_halves_{group}")):
                grads[n] = o.reshape(shards[n].shape)
                delta[n], new_m[n], new_v[n] = adamw(shards[n], grads[n], m[n][0], v[n][0], "adamw_" + n)

    S = {n: w[n].reshape(1, -1) for n in ("ffn1_norm", "mix_norm", "ffn2_norm", "final_norm", "pool_scale", "sinks")}
    S["pool_w_mix"] = w["pool_w_mix"][0].astype(BF16)
    loss, dx, G = local_fwd_bwd(x[0], loss_target[0], S, Exchanges())
    loss = lax.psum(loss[0, 0], ("x", "y", "c"))

    small_like = {n: w[n] for n in SMALL}
    grads.update(unpack_small(all_reduce_small(pack_small({n: G[n] for n in SMALL})), small_like))

    ds, ms, vs = adamw(pack_small(w), pack_small(grads), pack_small(m), pack_small(v), "adamw_small")
    delta.update(unpack_small(ds, small_like))
    new_m.update(unpack_small(ms, small_like))
    new_v.update(unpack_small(vs, small_like))

    def shaped(d, n):
        return d[n].reshape(w[n].shape)

    return (loss, dx[None], *[shaped(grads, n) for n in WEIGHTS], *[shaped(delta, n) for n in WEIGHTS],
            *[shaped(new_m, n) for n in WEIGHTS], *[shaped(new_v, n) for n in WEIGHTS])
```

```python
import numpy as np
import jax
import jax.numpy as jnp
from jax import lax
from jax.experimental import pallas as pl
from jax.experimental.pallas import tpu as pltpu

F32 = jnp.float32
BF16 = jnp.bfloat16
SDS = jax.ShapeDtypeStruct
MESH = pl.DeviceIdType.MESH

D = 1024
FF = 2816
NQ = 16
NKV = 2
HD = 64
GQ = NQ // NKV
AW = NQ * HD
KVW = NKV * HD
BLK = 128
PW = 512
PG = 128
POOL_WINDOWS = (2, 4, 8, 16)
HALO = 16
INW = AW + 2 * KVW + PW + 2 * D
C_KV = AW
C_Z = AW + 2 * KVW
C_G = C_Z + PW
EPS = 1e-6
FF_CHUNK = 256
FF_CHUNKS = tuple((c, FF_CHUNK) for c in range(0, FF, FF_CHUNK))
SLOPES = tuple(float(2.0 ** (-8.0 * h / NQ)) for h in range(1, NQ + 1))
SCALE = HD ** -0.5

LR, B1, B2, ADAM_EPS, WD, STEP = 0.001, 0.9, 0.999, 1e-08, 0.01, 10

VMEM_LIMIT = 56 * 1024 * 1024
N_CHIPS = 4

NT = (((1,), (1,)), ((), ()))
TN = (((0,), (0,)), ((), ()))


def _cp(sem=None, vmem=VMEM_LIMIT):
    return pltpu.CompilerParams(dimension_semantics=sem, vmem_limit_bytes=vmem)


def _const_spec(shape):
    nd = len(shape)
    return pl.BlockSpec(shape, lambda *_: (0,) * nd, pipeline_mode=pl.Buffered(1))


def _rstd(x):
    return lax.rsqrt(jnp.mean(x * x, axis=-1, keepdims=True) + EPS)


def _rms_bwd(dn, xhat, rstd, g):
    dxhat = dn * g
    return rstd * (dxhat - xhat * jnp.mean(dxhat * xhat, axis=-1, keepdims=True))


def _dot(a, b):
    return jnp.dot(a, b, preferred_element_type=F32)


def _dot_nt(a, b):
    return lax.dot_general(a, b, NT, preferred_element_type=F32)


def _dot_tn(a, b):
    return lax.dot_general(a, b, TN, preferred_element_type=F32)


def ffn_fwd(h, g, wup, wdn, name):
    T = h.shape[0]
    TM = 512

    def body(h_ref, g_ref, wup_ref, wdn_ref, out_ref, ab_ref, hid_ref):
        x = h_ref[...]
        n = (x * _rstd(x) * g_ref[...]).astype(BF16)
        for c0, w in FF_CHUNKS:
            a = _dot(n, wup_ref[:, c0:c0 + w])
            b = _dot(n, wup_ref[:, FF + c0:FF + c0 + w])
            ab_ref[:, c0:c0 + w] = a.astype(BF16)
            ab_ref[:, FF + c0:FF + c0 + w] = b.astype(BF16)
            hid_ref[:, c0:c0 + w] = (a * jax.nn.sigmoid(a) * b).astype(BF16)
        out_ref[...] = x + 0.5 * _dot(hid_ref[...], wdn_ref[...])

    return pl.pallas_call(
        body, name=name, grid=(T // TM,),
        in_specs=[pl.BlockSpec((TM, D), lambda i: (i, 0)), _const_spec((1, D)),
                  _const_spec((D, 2 * FF)), _const_spec((FF, D))],
        out_specs=[pl.BlockSpec((TM, D), lambda i: (i, 0)), pl.BlockSpec((TM, 2 * FF), lambda i: (i, 0)),
                   pl.BlockSpec((TM, FF), lambda i: (i, 0))],
        out_shape=[SDS((T, D), F32), SDS((T, 2 * FF), BF16), SDS((T, FF), BF16)],
        compiler_params=_cp(("arbitrary",)),
    )(h, g, wup, wdn)


def ffn_bwd_x(dh, h_in, ab, g, wup, wdn, name):
    T = dh.shape[0]
    TM = 256

    def body(dh_ref, h_ref, ab_ref, g_ref, wup_ref, wdn_ref, dhin_ref, dup_ref, n_ref, dg_ref):
        x = h_ref[...]
        g = g_ref[...]
        rstd = _rstd(x)
        xhat = x * rstd
        n_ref[...] = (xhat * g).astype(BF16)
        dh = dh_ref[...]
        dhh = (0.5 * dh).astype(BF16)
        for c0, w in FF_CHUNKS:
            dhid = _dot_nt(dhh, wdn_ref[c0:c0 + w, :])
            a = ab_ref[:, c0:c0 + w].astype(F32)
            b = ab_ref[:, FF + c0:FF + c0 + w].astype(F32)
            sig = jax.nn.sigmoid(a)
            s = a * sig
            dup_ref[:, c0:c0 + w] = (dhid * b * (sig * (1.0 + a * (1.0 - sig)))).astype(BF16)
            dup_ref[:, FF + c0:FF + c0 + w] = (dhid * s).astype(BF16)
        dn = _dot_nt(dup_ref[...], wup_ref[...])
        dhin_ref[...] = dh + _rms_bwd(dn, xhat, rstd, g)

        @pl.when(pl.program_id(0) == 0)
        def _():
            dg_ref[...] = jnp.zeros_like(dg_ref)

        dg_ref[...] += jnp.sum(dn * xhat, axis=0, keepdims=True)

    tile = lambda w: pl.BlockSpec((TM, w), lambda i: (i, 0))
    return pl.pallas_call(
        body, name=name, grid=(T // TM,),
        in_specs=[tile(D), tile(D), tile(2 * FF), _const_spec((1, D)), _const_spec((D, 2 * FF)), _const_spec((FF, D))],
        out_specs=[tile(D), tile(2 * FF), tile(D), pl.BlockSpec((1, D), lambda i: (0, 0))],
        out_shape=[SDS((T, D), F32), SDS((T, 2 * FF), BF16), SDS((T, D), BF16), SDS((1, D), F32)],
        compiler_params=_cp(("arbitrary",)),
    )(dh, h_in, ab, g, wup, wdn)


def matmul_tn(a, b, name, *, tm, tn, tt=1024, b_scale=None):
    T, M = a.shape
    N = b.shape[1]
    assert M % tm == 0 and N % tn == 0 and T % tt == 0

    def body(a_ref, b_ref, o_ref):
        @pl.when(pl.program_id(2) == 0)
        def _():
            o_ref[...] = jnp.zeros_like(o_ref)

        bv = b_ref[...]
        if b_scale is not None:
            bv = bv * b_scale
        o_ref[...] += _dot_tn(a_ref[...].astype(BF16), bv.astype(BF16))

    return pl.pallas_call(
        body, name=name, grid=(M // tm, N // tn, T // tt),
        in_specs=[pl.BlockSpec((tt, tm), lambda i, j, t: (t, i)), pl.BlockSpec((tt, tn), lambda i, j, t: (t, j))],
        out_specs=pl.BlockSpec((tm, tn), lambda i, j, t: (i, j)),
        out_shape=SDS((M, N), F32),
        compiler_params=_cp(("parallel", "parallel", "arbitrary")),
    )(a, b)


def mix_in_fwd(h1, g, win):
    T = h1.shape[0]
    TM = 512

    def body(h_ref, g_ref, w_ref, u_ref, q_ref, kv_ref, z_ref, gt_ref):
        x = h_ref[...]
        u = (x * _rstd(x) * g_ref[...]).astype(BF16)
        u_ref[...] = u
        for c in range(0, AW, 256):
            q_ref[:, c:c + 256] = _dot(u, w_ref[:, c:c + 256]).astype(BF16)
        kv_ref[...] = _dot(u, w_ref[:, C_KV:C_Z]).astype(BF16)
        for c in range(0, PW, 256):
            z_ref[:, c:c + 256] = _dot(u, w_ref[:, C_Z + c:C_Z + c + 256])
        for c in range(0, 2 * D, 256):
            gt_ref[:, c:c + 256] = _dot(u, w_ref[:, C_G + c:C_G + c + 256]).astype(BF16)

    tile = lambda w: pl.BlockSpec((TM, w), lambda i: (i, 0))
    return pl.pallas_call(
        body, name="mix_in_fwd", grid=(T // TM,),
        in_specs=[tile(D), _const_spec((1, D)), _const_spec((D, INW))],
        out_specs=[tile(D), tile(AW), tile(2 * KVW), tile(PW), tile(2 * D)],
        out_shape=[SDS((T, D), BF16), SDS((T, AW), BF16), SDS((T, 2 * KVW), BF16), SDS((T, PW), F32),
                   SDS((T, 2 * D), BF16)],
        compiler_params=_cp(("arbitrary",)),
    )(h1, g, win)


def mix_in_bwd(dq, dkv, dz, dgt, dh2, h1, g, win):
    T = h1.shape[0]
    TM = 512

    def body(dq_ref, dkv_ref, dz_ref, dgt_ref, dh2_ref, h_ref, g_ref, w_ref, dh1_ref, dg_ref):
        du = _dot_nt(dq_ref[...], w_ref[:, 0:AW])
        du += _dot_nt(dkv_ref[...], w_ref[:, C_KV:C_Z])
        du += _dot_nt(dz_ref[...], w_ref[:, C_Z:C_G])
        du += _dot_nt(dgt_ref[...], w_ref[:, C_G:INW])
        x = h_ref[...]
        g = g_ref[...]
        rstd = _rstd(x)
        xhat = x * rstd
        dh1_ref[...] = dh2_ref[...] + _rms_bwd(du, xhat, rstd, g)

        @pl.when(pl.program_id(0) == 0)
        def _():
            dg_ref[...] = jnp.zeros_like(dg_ref)

        dg_ref[...] += jnp.sum(du * xhat, axis=0, keepdims=True)

    tile = lambda w: pl.BlockSpec((TM, w), lambda i: (i, 0))
    return pl.pallas_call(
        body, name="mix_in_bwd", grid=(T // TM,),
        in_specs=[tile(AW), tile(2 * KVW), tile(PW), tile(2 * D), tile(D), tile(D), _const_spec((1, D)),
                  _const_spec((D, INW))],
        out_specs=[tile(D), pl.BlockSpec((1, D), lambda i: (0, 0))],
        out_shape=[SDS((T, D), F32), SDS((1, D), F32)],
        compiler_params=_cp(("arbitrary",)),
    )(dq, dkv, dz, dgt, dh2, h1, g, win)


PAIR = 2 * HD
NPAIR = GQ // 2


def _lo_lanes():
    return lax.broadcasted_iota(jnp.int32, (BLK, PAIR), 1) < HD


def _stack_heads(ref, kvh, scale=None):
    lo = _lo_lanes()
    parts = []
    for pr in range(NPAIR):
        t = ref[:, (kvh * NPAIR + pr) * PAIR:(kvh * NPAIR + pr + 1) * PAIR]
        if scale is not None:
            t = t * scale
        zero = jnp.zeros_like(t)
        parts += [jnp.where(lo, t, zero), jnp.where(lo, zero, t)]
    return jnp.concatenate(parts, axis=0)


def _kv_tiles(kvc_ref, kvp_ref, tile, kvh):
    lo = _lo_lanes()
    dup, left, right = [], [], []
    for ref in (kvp_ref, kvc_ref):
        t = ref[:, tile * PAIR:(tile + 1) * PAIR]
        r = pltpu.roll(t.astype(F32), HD, 1).astype(BF16)
        zero = jnp.zeros_like(t)
        a, b = (t, r) if kvh == 0 else (r, t)
        dup.append(jnp.where(lo, a, b))
        left.append(jnp.where(lo, a, zero))
        right.append(jnp.where(lo, zero, b))
    cat = lambda xs: jnp.concatenate(xs, axis=0)
    return cat(dup), cat(left), cat(right)


def _band_consts(first):
    row = lax.broadcasted_iota(jnp.int32, (BLK, BLK), 0)
    col = lax.broadcasted_iota(jnp.int32, (BLK, BLK), 1)
    upper = col > row
    dist = jnp.where(upper, row - col + BLK, row - col).astype(F32)
    pen = jnp.where(jnp.logical_and(upper, first), -jnp.inf, 0.0)
    return upper, dist, pen


def _split_band(upper, t):
    zero = jnp.zeros_like(t)
    return jnp.concatenate([jnp.where(upper, t, zero), jnp.where(upper, zero, t)], axis=1)


def attn_fwd(q, kv, sinks):
    T = q.shape[0]
    nb = T // BLK

    def body(sink_ref, q_ref, kvc_ref, kvp_ref, att_ref, lse_ref):
        upper, dist, pen = _band_consts(pl.program_id(0) == 0)
        for kvh in range(NKV):
            kdup, _, _ = _kv_tiles(kvc_ref, kvp_ref, 0, kvh)
            _, vleft, vright = _kv_tiles(kvc_ref, kvp_ref, 1, kvh)
            s_all = _dot_nt(_stack_heads(q_ref, kvh, SCALE), kdup)
            for pr in range(NPAIR):
                out = None
                for side, vpad in ((0, vleft), (1, vright)):
                    g = 2 * pr + side
                    hq = kvh * GQ + g
                    sink = sink_ref[0, hq]
                    rows = slice(g * BLK, (g + 1) * BLK)
                    s = jnp.where(upper, s_all[rows, 0:BLK], s_all[rows, BLK:2 * BLK]) - SLOPES[hq] * dist + pen
                    m = jnp.maximum(jnp.max(s, axis=-1, keepdims=True), sink)
                    p = jnp.exp(s - m)
                    l = jnp.sum(p, axis=-1, keepdims=True) + jnp.exp(sink - m)
                    lse_ref[:, hq:hq + 1] = m + jnp.log(l)
                    o = _dot(_split_band(upper, (p * (1.0 / l)).astype(BF16)), vpad)
                    out = o if out is None else out + o
                col0 = (kvh * NPAIR + pr) * PAIR
                att_ref[:, col0:col0 + PAIR] = out.astype(BF16)

    return pl.pallas_call(
        body, name="attn_fwd", grid=(nb,),
        in_specs=[pl.BlockSpec(memory_space=pltpu.SMEM),
                  pl.BlockSpec((BLK, AW), lambda i: (i, 0)),
                  pl.BlockSpec((BLK, 2 * KVW), lambda i: (i, 0)),
                  pl.BlockSpec((BLK, 2 * KVW), lambda i: (jnp.maximum(i - 1, 0), 0))],
        out_specs=[pl.BlockSpec((BLK, AW), lambda i: (i, 0)), pl.BlockSpec((BLK, NQ), lambda i: (i, 0))],
        out_shape=[SDS((T, AW), BF16), SDS((T, NQ), F32)],
        compiler_params=_cp(("arbitrary",)),
    )(sinks, q, kv, kv)


def attn_bwd(q, kv, datt, lse, sinks):
    T = q.shape[0]
    nb = T // BLK

    def body(sink_ref, q_ref, kvc_ref, kvp_ref, do_ref, lse_ref, dq_ref, dkv_ref, dsink_ref, carry_ref):
        i = pl.program_id(0)

        @pl.when(i == 0)
        def _():
            dsink_ref[...] = jnp.zeros_like(dsink_ref)
            carry_ref[...] = jnp.zeros_like(carry_ref)

        @pl.when(i < nb)
        def _():
            upper, dist, pen = _band_consts(i == 0)
            lo = _lo_lanes()
            dk_dup, dv_dup = [], []
            for kvh in range(NKV):
                kdup, kleft, kright = _kv_tiles(kvc_ref, kvp_ref, 0, kvh)
                vdup, _, _ = _kv_tiles(kvc_ref, kvp_ref, 1, kvh)
                qs = _stack_heads(q_ref, kvh, SCALE)
                dos = _stack_heads(do_ref, kvh)
                s_all = _dot_nt(qs, kdup)
                dp_all = _dot_nt(dos, vdup)
                ds_parts, p_parts = [], []
                for pr in range(NPAIR):
                    dq = None
                    for side, kpad in ((0, kleft), (1, kright)):
                        g = 2 * pr + side
                        hq = kvh * GQ + g
                        lse_h = lse_ref[:, hq:hq + 1]
                        rows = slice(g * BLK, (g + 1) * BLK)
                        s = jnp.where(upper, s_all[rows, 0:BLK], s_all[rows, BLK:2 * BLK]) - SLOPES[hq] * dist + pen
                        p = jnp.exp(s - lse_h)
                        dp = jnp.where(upper, dp_all[rows, 0:BLK], dp_all[rows, BLK:2 * BLK])
                        delta = jnp.sum(p * dp, axis=-1, keepdims=True)
                        dsink_ref[:, hq:hq + 1] += -jnp.sum(jnp.exp(sink_ref[0, hq] - lse_h) * delta, axis=0,
                                                            keepdims=True)
                        ds = _split_band(upper, (p * (dp - delta)).astype(BF16))
                        ds_parts.append(ds)
                        p_parts.append(_split_band(upper, p.astype(BF16)))
                        d = _dot(ds, kpad)
                        dq = d if dq is None else dq + d
                    col0 = (kvh * NPAIR + pr) * PAIR
                    dq_ref[:, col0:col0 + PAIR] = (dq * SCALE).astype(BF16)
                dkw = _dot_tn(jnp.concatenate(ds_parts, axis=0), qs)
                dvw = _dot_tn(jnp.concatenate(p_parts, axis=0), dos)
                dk_dup.append(dkw + pltpu.roll(dkw, HD, 1))
                dv_dup.append(dvw + pltpu.roll(dvw, HD, 1))
            dk = jnp.where(jnp.concatenate([lo, lo], axis=0), dk_dup[0], dk_dup[1])
            dv = jnp.where(jnp.concatenate([lo, lo], axis=0), dv_dup[0], dv_dup[1])
            dkv_ref[:, 0:PAIR] = (carry_ref[:, 0:PAIR] + dk[0:BLK]).astype(BF16)
            dkv_ref[:, PAIR:2 * PAIR] = (carry_ref[:, PAIR:2 * PAIR] + dv[0:BLK]).astype(BF16)
            carry_ref[:, 0:PAIR] = dk[BLK:2 * BLK]
            carry_ref[:, PAIR:2 * PAIR] = dv[BLK:2 * BLK]

        @pl.when(i == nb)
        def _():
            dkv_ref[...] = carry_ref[...].astype(BF16)

    cur = lambda i: (jnp.minimum(i, nb - 1), 0)
    prev = lambda i: (jnp.maximum(jnp.minimum(i, nb - 1) - 1, 0), 0)
    return pl.pallas_call(
        body, name="attn_bwd", grid=(nb + 1,),
        in_specs=[pl.BlockSpec(memory_space=pltpu.SMEM),
                  pl.BlockSpec((BLK, AW), cur), pl.BlockSpec((BLK, 2 * KVW), cur), pl.BlockSpec((BLK, 2 * KVW), prev),
                  pl.BlockSpec((BLK, AW), cur), pl.BlockSpec((BLK, NQ), cur)],
        out_specs=[pl.BlockSpec((BLK, AW), cur),
                   pl.BlockSpec((BLK, 2 * KVW), lambda i: (jnp.maximum(i - 1, 0), 0)),
                   pl.BlockSpec((1, NQ), lambda i: (0, 0))],
        out_shape=[SDS((T, AW), BF16), SDS((T, 2 * KVW), BF16), SDS((1, NQ), F32)],
        scratch_shapes=[pltpu.VMEM((BLK, 2 * KVW), F32)],
        compiler_params=_cp(("arbitrary",)),
    )(sinks, q, kv, kv, datt, lse)


def _inv_counts(t0, rows):
    t = (t0 + lax.broadcasted_iota(jnp.int32, (rows, 1), 0) + 1).astype(F32)
    return [1.0 / jnp.minimum(t, float(w)) for w in POOL_WINDOWS]


def pool_fwd(z, wmix, scale):
    T = z.shape[0]
    TM = 512
    L = TM + HALO

    def body(z_ref, halo_ref, wmix_ref, scale_ref, pooled_ref, mixs_ref):
        i = pl.program_id(0)
        halo = jnp.where(i > 0, halo_ref[...], 0.0)
        zt = z_ref[...]
        e = jnp.concatenate([halo, zt], axis=0)
        sums = []
        s = e
        for k in (1, 2, 4, 8):
            s = s + pltpu.roll(s, k, 0)
            sums.append(s)
        inv = _inv_counts(i * TM, TM)
        for gi in range(len(POOL_WINDOWS)):
            cols = slice(gi * PG, (gi + 1) * PG)
            pooled = (sums[gi][HALO:, cols] * inv[gi] - zt[:, cols]).astype(BF16)
            pooled_ref[:, cols] = pooled
            mixs_ref[:, cols] = (_dot(pooled, wmix_ref[gi]) * scale_ref[:, cols]).astype(BF16)

    return pl.pallas_call(
        body, name="pool_fwd", grid=(T // TM,),
        in_specs=[pl.BlockSpec((TM, PW), lambda i: (i, 0)),
                  pl.BlockSpec((HALO, PW), lambda i: (jnp.maximum(i * (TM // HALO) - 1, 0), 0)),
                  _const_spec((len(POOL_WINDOWS), PG, PG)), _const_spec((1, PW))],
        out_specs=[pl.BlockSpec((TM, PW), lambda i: (i, 0)), pl.BlockSpec((TM, PW), lambda i: (i, 0))],
        out_shape=[SDS((T, PW), BF16), SDS((T, PW), BF16)],
        compiler_params=_cp(("arbitrary",)),
    )(z, z, wmix, scale)


def pool_bwd(dmixs, pooled, wmix, scale):
    T = dmixs.shape[0]
    TM = 512
    L = TM + HALO
    nt = T // TM

    def body(dm_ref, halo_ref, pooled_ref, wmix_ref, scale_ref, dz_ref, dwmix_ref, dscale_ref):
        i = pl.program_id(0)

        @pl.when(i == 0)
        def _():
            dwmix_ref[...] = jnp.zeros_like(dwmix_ref)
            dscale_ref[...] = jnp.zeros_like(dscale_ref)

        halo = jnp.where(i < nt - 1, halo_ref[...], 0.0)
        dm = dm_ref[...]
        e = jnp.concatenate([dm, halo], axis=0)
        inv = _inv_counts(i * TM, L)
        for gi in range(len(POOL_WINDOWS)):
            cols = slice(gi * PG, (gi + 1) * PG)
            w = wmix_ref[gi]
            dmixed = (e[:, cols] * scale_ref[:, cols]).astype(BF16)
            dpooled = _dot_nt(dmixed, w)
            pooled = pooled_ref[:, cols]
            mixed = _dot(pooled, w)
            dscale_ref[:, cols] += jnp.sum(dm[:, cols] * mixed, axis=0, keepdims=True)
            dwmix_ref[gi] += _dot_tn(pooled, dmixed[:TM])
            s = dpooled * inv[gi]
            k = 1
            while k < POOL_WINDOWS[gi]:
                s = s + pltpu.roll(s, L - k, 0)
                k *= 2
            dz_ref[:, cols] = (s[:TM] - dpooled[:TM]).astype(BF16)

    return pl.pallas_call(
        body, name="pool_bwd", grid=(nt,),
        in_specs=[pl.BlockSpec((TM, PW), lambda i: (i, 0)),
                  pl.BlockSpec((HALO, PW), lambda i: (jnp.minimum((i + 1) * (TM // HALO), T // HALO - 1), 0)),
                  pl.BlockSpec((TM, PW), lambda i: (i, 0)),
                  _const_spec((len(POOL_WINDOWS), PG, PG)), _const_spec((1, PW))],
        out_specs=[pl.BlockSpec((TM, PW), lambda i: (i, 0)),
                   pl.BlockSpec((len(POOL_WINDOWS), PG, PG), lambda i: (0, 0, 0)),
                   pl.BlockSpec((1, PW), lambda i: (0, 0))],
        out_shape=[SDS((T, PW), BF16), SDS((len(POOL_WINDOWS), PG, PG), F32), SDS((1, PW), F32)],
        compiler_params=_cp(("arbitrary",)),
    )(dmixs, dmixs, pooled, wmix, scale)


def merge_fwd(att, mixs, gt, h1, wattn, wpool, wout):
    T = h1.shape[0]
    TM = 512

    def body(att_ref, mixs_ref, gt_ref, h_ref, wa_ref, wp_ref, wo_ref, h2_ref, mg_ref):
        a = _dot(att_ref[...], wa_ref[...])
        p = _dot(mixs_ref[...], wp_ref[...])
        merged = (jax.nn.sigmoid(gt_ref[:, 0:D].astype(F32)) * a + jax.nn.sigmoid(gt_ref[:, D:2 * D].astype(F32)) * p)
        mg = merged.astype(BF16)
        mg_ref[...] = mg
        h2_ref[...] = h_ref[...] + _dot(mg, wo_ref[...])

    tile = lambda w: pl.BlockSpec((TM, w), lambda i: (i, 0))
    return pl.pallas_call(
        body, name="merge_fwd", grid=(T // TM,),
        in_specs=[tile(AW), tile(PW), tile(2 * D), tile(D), _const_spec((AW, D)), _const_spec((PW, D)),
                  _const_spec((D, D))],
        out_specs=[tile(D), tile(D)],
        out_shape=[SDS((T, D), F32), SDS((T, D), BF16)],
        compiler_params=_cp(("arbitrary",)),
    )(att, mixs, gt, h1, wattn, wpool, wout)


def merge_bwd(dh2, att, mixs, gt, wattn, wpool, wout):
    T = dh2.shape[0]
    TM = 512

    def body(dh2_ref, att_ref, mixs_ref, gt_ref, wa_ref, wp_ref, wo_ref, datt_ref, dmixs_ref, dgt_ref, da_ref, dp_ref):
        dm = _dot_nt(dh2_ref[...].astype(BF16), wo_ref[...])
        a = _dot(att_ref[...], wa_ref[...])
        p = _dot(mixs_ref[...], wp_ref[...])
        sa = jax.nn.sigmoid(gt_ref[:, 0:D].astype(F32))
        sp = jax.nn.sigmoid(gt_ref[:, D:2 * D].astype(F32))
        da = (dm * sa).astype(BF16)
        dp = (dm * sp).astype(BF16)
        da_ref[...] = da
        dp_ref[...] = dp
        dgt_ref[:, 0:D] = (dm * a * sa * (1.0 - sa)).astype(BF16)
        dgt_ref[:, D:2 * D] = (dm * p * sp * (1.0 - sp)).astype(BF16)
        datt_ref[...] = _dot_nt(da, wa_ref[...]).astype(BF16)
        dmixs_ref[...] = _dot_nt(dp, wp_ref[...])

    tile = lambda w: pl.BlockSpec((TM, w), lambda i: (i, 0))
    return pl.pallas_call(
        body, name="merge_bwd", grid=(T // TM,),
        in_specs=[tile(D), tile(AW), tile(PW), tile(2 * D), _const_spec((AW, D)), _const_spec((PW, D)),
                  _const_spec((D, D))],
        out_specs=[tile(AW), tile(PW), tile(2 * D), tile(D), tile(D)],
        out_shape=[SDS((T, AW), BF16), SDS((T, PW), F32), SDS((T, 2 * D), BF16), SDS((T, D), BF16),
                   SDS((T, D), BF16)],
        compiler_params=_cp(("arbitrary",)),
    )(dh2, att, mixs, gt, wattn, wpool, wout)


def loss_head(h3, target, g):
    T = h3.shape[0]
    TM = 512

    def body(h_ref, t_ref, g_ref, dh_ref, loss_ref, dg_ref):
        @pl.when(pl.program_id(0) == 0)
        def _():
            loss_ref[...] = jnp.zeros_like(loss_ref)
            dg_ref[...] = jnp.zeros_like(dg_ref)

        x = h_ref[...]
        g = g_ref[...]
        rstd = _rstd(x)
        xhat = x * rstd
        err = xhat * g - t_ref[...]
        loss_ref[...] += 0.5 * jnp.sum(jnp.mean(err * err, axis=-1, keepdims=True), axis=0, keepdims=True)
        dy = err * (1.0 / D)
        dg_ref[...] += jnp.sum(dy * xhat, axis=0, keepdims=True)
        dh_ref[...] = _rms_bwd(dy, xhat, rstd, g)

    tile = pl.BlockSpec((TM, D), lambda i: (i, 0))
    return pl.pallas_call(
        body, name="loss_head", grid=(T // TM,),
        in_specs=[tile, tile, _const_spec((1, D))],
        out_specs=[tile, pl.BlockSpec((1, 1), lambda i: (0, 0)), pl.BlockSpec((1, D), lambda i: (0, 0))],
        out_shape=[SDS((T, D), F32), SDS((1, 1), F32), SDS((1, D), F32)],
        compiler_params=_cp(("arbitrary",)),
    )(h3, target, g)


def adamw(w, g, m, v, name):
    R, C = w.shape
    tile_bytes = 2 * 1024 * 1024
    tr = R
    if R * C * 4 > tile_bytes:
        tr = next(cand for cand in (512, 256, 128, 64, 32, 16, 8) if R % cand == 0 and cand * C * 4 <= tile_bytes)
    c1 = 1.0 - B1 ** STEP
    c2 = 1.0 - B2 ** STEP

    def body(w_ref, g_ref, m_ref, v_ref, d_ref, nm_ref, nv_ref):
        gv = g_ref[...]
        nm = B1 * m_ref[...] + (1.0 - B1) * gv
        nv = B2 * v_ref[...] + (1.0 - B2) * (gv * gv)
        nm_ref[...] = nm
        nv_ref[...] = nv
        d_ref[...] = -LR * ((nm / c1) / (jnp.sqrt(nv / c2) + ADAM_EPS) + WD * w_ref[...])

    spec = pl.BlockSpec((tr, C), lambda i: (i, 0))
    return pl.pallas_call(
        body, name=name, grid=(R // tr,),
        in_specs=[spec] * 4, out_specs=[spec] * 3, out_shape=[SDS((R, C), F32)] * 3,
        compiler_params=_cp(("parallel",)),
    )(w, g, m, v)


GROUP_FFN1, GROUP_MIX, GROUP_FFN2 = 0, 1, 2


def _behind(small, token):
    return small if token is None else small + token[0:1, 0:1]


def local_fwd_bwd(x, target, S, comm):
    W = comm.weights(GROUP_FFN1, None)
    h1, ab1, hid1 = ffn_fwd(x, _behind(S["ffn1_norm"], comm.started()), W["ffn1_w_up"], W["ffn1_w_down"], "ffn1_fwd")
    W = comm.weights(GROUP_MIX, h1)
    u, q, kv, z, gt = mix_in_fwd(h1, _behind(S["mix_norm"], comm.started()), W["w_in"])
    att, lse = attn_fwd(q, kv, S["sinks"])
    pooled, mixs = pool_fwd(z, S["pool_w_mix"], S["pool_scale"])
    h2, merged = merge_fwd(att, mixs, gt, h1, W["w_attn_up"], W["w_pool_up"], W["w_out"])
    W = comm.weights(GROUP_FFN2, h2)
    h3, ab2, hid2 = ffn_fwd(h2, S["ffn2_norm"], W["ffn2_w_up"], W["ffn2_w_down"], "ffn2_fwd")
    dh3, loss, g_final = loss_head(h3, target, S["final_norm"])

    G = {"final_norm": g_final}
    dh2, dup2, n2, G["ffn2_norm"] = ffn_bwd_x(dh3, h2, ab2, S["ffn2_norm"], W["ffn2_w_up"], W["ffn2_w_down"],
                                              "ffn2_bwd_x")
    token = comm.grads({"ffn2_w_down": matmul_tn(hid2, dh3, "ffn2_dw_down", tm=1408, tn=D, b_scale=0.5),
                        "ffn2_w_up": matmul_tn(n2, dup2, "ffn2_dw_up", tm=D, tn=1408)})

    W = comm.weights(GROUP_MIX, None)
    datt, dmixs, dgt, da, dp = merge_bwd(dh2, att, mixs, gt, W["w_attn_up"], W["w_pool_up"], W["w_out"])
    g_mix = {"w_out": matmul_tn(merged, dh2, "dw_out", tm=D, tn=D),
             "w_attn_up": matmul_tn(att, da, "dw_attn_up", tm=AW, tn=D),
             "w_pool_up": matmul_tn(mixs, dp, "dw_pool_up", tm=PW, tn=D)}
    dz, G["pool_w_mix"], G["pool_scale"] = pool_bwd(dmixs, pooled, S["pool_w_mix"], _behind(S["pool_scale"], token))
    dq, dkv, G["sinks"] = attn_bwd(q, kv, datt, lse, _behind(S["sinks"], token))
    dh1, G["mix_norm"] = mix_in_bwd(dq, dkv, dz, dgt, dh2, h1, S["mix_norm"], W["w_in"])
    g_mix["w_in"] = jnp.concatenate([
        matmul_tn(u, dq, "dw_in_q", tm=D, tn=AW),
        matmul_tn(u, dkv, "dw_in_kv", tm=D, tn=2 * KVW),
        matmul_tn(u, dz, "dw_in_z", tm=D, tn=PW),
        matmul_tn(u, dgt, "dw_in_g", tm=D, tn=D),
    ], axis=1)
    comm.finish(("ffn2_w_down", "ffn2_w_up"), dh1)
    token = comm.grads(g_mix)

    W = comm.weights(GROUP_FFN1, None)
    token2 = comm.grads({"ffn1_w_down": matmul_tn(hid1, dh1, "ffn1_dw_down", tm=1408, tn=D, b_scale=0.5)})
    g1 = _behind(_behind(S["ffn1_norm"], token), token2)
    dx, dup1, n1, G["ffn1_norm"] = ffn_bwd_x(dh1, x, ab1, g1, W["ffn1_w_up"], W["ffn1_w_down"], "ffn1_bwd_x")
    g_up1 = matmul_tn(n1, dup1, "ffn1_dw_up", tm=D, tn=1408)
    comm.finish(tuple(g_mix), dx)
    comm.finish(("ffn1_w_down",), g_up1)
    token = comm.grads({"ffn1_w_up": g_up1})
    comm.finish(("ffn1_w_up",), token)
    return loss, dx, G


HBM_SPEC = pl.BlockSpec(memory_space=pltpu.HBM)


def _me():
    return lax.axis_index("x"), lax.axis_index("y"), lax.axis_index("c")


def _peer_chip(x, y, k):
    return x ^ (k >> 1), y ^ (k & 1)


def _piece_half(ref, rowlike, j, h):
    if rowlike:
        return ref.at[j, h]
    ns = ref.shape[-1] // N_CHIPS
    return ref.at[h, :, pl.ds(pl.multiple_of(j * ns, 128), ns)]


def _piece(ref, rowlike, j):
    if rowlike:
        return ref.at[j]
    ns = ref.shape[-1] // N_CHIPS
    return ref.at[:, :, pl.ds(pl.multiple_of(j * ns, 128), ns)]


def _full_shape(shard_view, rowlike):
    _, kh, ns = shard_view.shape
    return (N_CHIPS, 2, kh, ns) if rowlike else (2, kh, N_CHIPS * ns)


def _remote(src, dst, send_sem, recv_sem, dev):
    return pltpu.make_async_remote_copy(src, dst, send_sem, recv_sem, device_id=dev, device_id_type=MESH)


SEM_SPEC = pl.BlockSpec(memory_space=pltpu.SEMAPHORE)
SPLIT_PARAMS = pltpu.CompilerParams(has_side_effects=pltpu.SideEffectType.DATAFLOW_SIDE_EFFECTING)


def _gather_plan(rowlikes):
    def plan(s_refs, f_refs, a, k, x, y, c):
        px, py = _peer_chip(x, y, k)
        return (s_refs[a].at[c], _piece_half(f_refs[a], rowlikes[a], 2 * x + y, c),
                _piece_half(f_refs[a], rowlikes[a], 2 * px + py, c), (px, py, c))
    return plan


def _all_to_all_plan(rowlikes):
    def plan(q_refs, r_refs, a, k, x, y, c):
        px, py = _peer_chip(x, y, k)
        if rowlikes[a]:
            src = q_refs[a].at[2 * px + py]
        else:
            ns = q_refs[a].shape[-1] // N_CHIPS
            src = q_refs[a].at[:, pl.ds(pl.multiple_of((2 * px + py) * ns, 128), ns)]
        return src, r_refs[a].at[k - 1], r_refs[a].at[k - 1], (px, py, c)
    return plan


def exchange_start(srcs, land_shapes, plan, after, name):
    n = len(srcs)
    lands = [lax.empty(shape, s.dtype) for shape, s in zip(land_shapes, srcs)]

    def body(*refs):
        s_refs, l_refs = refs[:n], refs[n:2 * n]
        send_sems, recv_sems = refs[2 * n + 1], refs[2 * n + 2]
        token = refs[-1]
        x, y, c = _me()
        for a in range(n):
            for k in (1, 2, 3):
                src, dst, _, peer = plan(s_refs, l_refs, a, k, x, y, c)
                _remote(src, dst, send_sems.at[3 * a + k - 1], recv_sems.at[3 * a + k - 1], peer).start()
        token[...] = jnp.zeros_like(token)

    outs = pl.pallas_call(
        body, name=name, in_specs=[HBM_SPEC] * (2 * n) + [pl.BlockSpec(memory_space=pl.ANY)],
        out_specs=[SEM_SPEC, SEM_SPEC] + [HBM_SPEC] * (2 * n) + [pl.BlockSpec(memory_space=pltpu.VMEM)],
        out_shape=[pltpu.SemaphoreType.DMA((3 * n,)), pltpu.SemaphoreType.DMA((3 * n,))]
        + [pltpu.HBM(a.shape, a.dtype) for a in (*srcs, *lands)] + [SDS((8, 128), F32)],
        input_output_aliases={i: 2 + i for i in range(2 * n)},
        compiler_params=SPLIT_PARAMS,
    )(*[pltpu.with_memory_space_constraint(a, pltpu.HBM) for a in (*srcs, *lands)], after)
    return {"sems": outs[:2], "srcs": outs[2:2 + n], "lands": outs[2 + n:2 + 2 * n], "token": outs[-1]}


def exchange_wait(state, plan, after, name):
    n = len(state["srcs"])

    def body(*refs):
        s_refs, l_refs = refs[:n], refs[n:2 * n]
        send_sems, recv_sems = refs[2 * n], refs[2 * n + 1]
        x, y, c = _me()
        for a in range(n):
            for k in (1, 2, 3):
                src, _, landing, peer = plan(s_refs, l_refs, a, k, x, y, c)
                cp = _remote(src, landing, send_sems.at[3 * a + k - 1], recv_sems.at[3 * a + k - 1], peer)
                cp.wait_send()
                cp.wait_recv()

    bufs = (*state["srcs"], *state["lands"])
    outs = pl.pallas_call(
        body, name=name,
        in_specs=[HBM_SPEC] * (2 * n) + [SEM_SPEC, SEM_SPEC, pl.BlockSpec(memory_space=pl.ANY)],
        out_specs=[HBM_SPEC] * (2 * n),
        out_shape=[pltpu.HBM(a.shape, a.dtype) for a in bufs],
        input_output_aliases={i: i for i in range(2 * n)},
        compiler_params=SPLIT_PARAMS,
    )(*bufs, *state["sems"], after)
    return outs[n:]


def gather_finish(shards, fulls, rowlikes, name):
    n = len(shards)

    def body(*refs):
        s_refs, f_refs = refs[:n], refs[2 * n:3 * n]
        send_sems, recv_sems = refs[3 * n:]
        x, y, c = _me()
        chip = 2 * x + y
        sib = (x, y, 1 - c)
        sends = []
        for a in range(n):
            own = _piece(f_refs[a], rowlikes[a], chip)
            cp = _remote(s_refs[a], own, send_sems.at[a, 0], recv_sems.at[a, 0], sib)
            cp.start()
            sends.append(cp)
            for k in (1, 2, 3):
                px, py = _peer_chip(x, y, k)
                slot = _piece_half(f_refs[a], rowlikes[a], 2 * px + py, c)
                cp = _remote(slot, slot, send_sems.at[a, k], recv_sems.at[a, k], sib)
                cp.start()
                sends.append(cp)
        for a in range(n):
            own = _piece(f_refs[a], rowlikes[a], chip)
            _remote(own, own, send_sems.at[a, 0], recv_sems.at[a, 0], sib).wait_recv()
            for k in (1, 2, 3):
                px, py = _peer_chip(x, y, k)
                slot = _piece_half(f_refs[a], rowlikes[a], 2 * px + py, 1 - c)
                _remote(slot, slot, send_sems.at[a, k], recv_sems.at[a, k], sib).wait_recv()
        for cp in sends:
            cp.wait_send()

    return pl.pallas_call(
        body, name=name, in_specs=[HBM_SPEC] * (2 * n), out_specs=[HBM_SPEC] * n,
        out_shape=[SDS(f.shape, f.dtype) for f in fulls],
        input_output_aliases={n + a: a for a in range(n)},
        scratch_shapes=[pltpu.SemaphoreType.DMA((n, 4)), pltpu.SemaphoreType.DMA((n, 4))],
    )(*shards, *fulls)


def swap_halves(gviews, rowlikes, name):
    n = len(gviews)

    def half_shape(g, rowlike):
        return (N_CHIPS,) + g.shape[2:] if rowlike else g.shape[1:]

    def body(*refs):
        g_refs, got_refs = refs[:n], refs[n:2 * n]
        send_sems, recv_sems = refs[2 * n:]
        x, y, c = _me()
        copies = []
        for a in range(n):
            src = g_refs[a].at[:, 1 - c] if rowlikes[a] else g_refs[a].at[1 - c]
            cp = _remote(src, got_refs[a], send_sems.at[a], recv_sems.at[a], (x, y, 1 - c))
            cp.start()
            copies.append(cp)
        for cp in copies:
            cp.wait()

    return pl.pallas_call(
        body, name=name, in_specs=[HBM_SPEC] * n, out_specs=[HBM_SPEC] * n,
        out_shape=[SDS(half_shape(g, r), g.dtype) for g, r in zip(gviews, rowlikes)],
        scratch_shapes=[pltpu.SemaphoreType.DMA((n,)), pltpu.SemaphoreType.DMA((n,))],
    )(*gviews)


ROW_TILES = 2


def _piece_specs(rowlike, kh, ns, piece, half):
    tr = kh // ROW_TILES
    if rowlike:
        return (pl.BlockSpec((None, None, tr, ns), lambda *g: (piece(*g), half(*g), g[-2], 0)),
                pl.BlockSpec((None, tr, ns), lambda *g: (piece(*g), g[-2], 0)))
    return (pl.BlockSpec((None, tr, ns), lambda *g: (half(*g), g[-2], piece(*g))),
            pl.BlockSpec((tr, ns), lambda *g: (g[-2], piece(*g))))


def add_halves(gview, got, rowlike, place, name):
    kh, ns = (gview.shape[2], gview.shape[3]) if rowlike else (gview.shape[1], gview.shape[2] // N_CHIPS)

    def body(place_ref, g_ref, got_ref, o_ref):
        o_ref[...] = (g_ref[...] + got_ref[...]).astype(BF16)

    g_spec, h_spec = _piece_specs(rowlike, kh, ns, lambda j, r, p: j, lambda j, r, p: p[0])
    return pl.pallas_call(
        body, name=name,
        grid_spec=pltpu.PrefetchScalarGridSpec(num_scalar_prefetch=1, grid=(N_CHIPS, ROW_TILES),
                                               in_specs=[g_spec, h_spec], out_specs=h_spec),
        out_shape=SDS(got.shape, BF16),
        compiler_params=_cp(("parallel", "parallel")),
    )(place, gview, got)


def _slot_shape(q, rowlike):
    return (3,) + (q.shape[1:] if rowlike else (q.shape[0], q.shape[1] // N_CHIPS))


def sum_pieces(gview, got, recv, rowlike, place, name):
    kh, ns = recv.shape[1], recv.shape[2]
    tr = kh // ROW_TILES

    def body(place_ref, g_ref, got_ref, r_ref, o_ref):
        acc = g_ref[...] + got_ref[...]
        for k in range(3):
            acc = acc + r_ref[k].astype(F32)
        o_ref[...] = acc

    g_spec, h_spec = _piece_specs(rowlike, kh, ns, lambda z, r, p: p[1], lambda z, r, p: p[0])
    return pl.pallas_call(
        body, name=name,
        grid_spec=pltpu.PrefetchScalarGridSpec(
            num_scalar_prefetch=1, grid=(1, ROW_TILES),
            in_specs=[g_spec, h_spec, pl.BlockSpec((3, tr, ns), lambda z, r, p: (0, r, 0))],
            out_specs=pl.BlockSpec((None, tr, ns), lambda z, r, p: (p[0], r, 0))),
        out_shape=SDS((2, kh, ns), F32),
        compiler_params=_cp(("parallel", "parallel")),
    )(place, gview, got, recv)


def join_halves(halves, name):
    n = len(halves)

    def body(*refs):
        o_refs = refs[n:2 * n]
        send_sems, recv_sems = refs[2 * n:]
        x, y, c = _me()
        sib = (x, y, 1 - c)
        sends = []
        for a in range(n):
            cp = _remote(o_refs[a].at[c], o_refs[a].at[c], send_sems.at[a], recv_sems.at[a], sib)
            cp.start()
            sends.append(cp)
        for a in range(n):
            got = o_refs[a].at[1 - c]
            _remote(got, got, send_sems.at[a], recv_sems.at[a], sib).wait_recv()
        for cp in sends:
            cp.wait_send()

    return pl.pallas_call(
        body, name=name, in_specs=[HBM_SPEC] * n, out_specs=[HBM_SPEC] * n,
        out_shape=[SDS(h.shape, h.dtype) for h in halves],
        input_output_aliases={a: a for a in range(n)},
        scratch_shapes=[pltpu.SemaphoreType.DMA((n,)), pltpu.SemaphoreType.DMA((n,))],
    )(*halves)


N_DEV = 8


def all_reduce_small(s):
    R, Wd = s.shape

    def body(s_ref, out_ref, slots_ref, send_sems, recv_sems):
        x, y, c = _me()
        me = 4 * x + 2 * y + c
        slots_ref[me] = s_ref[...]
        sends = []
        for k in range(1, N_DEV):
            peer = (x ^ (k >> 2), y ^ ((k >> 1) & 1), c ^ (k & 1))
            cp = pltpu.make_async_remote_copy(s_ref, slots_ref.at[me], send_sems.at[k - 1], recv_sems.at[k - 1],
                                              device_id=peer, device_id_type=MESH)
            cp.start()
            sends.append(cp)
        for k in range(1, N_DEV):
            peer = (x ^ (k >> 2), y ^ ((k >> 1) & 1), c ^ (k & 1))
            slot = slots_ref.at[4 * peer[0] + 2 * peer[1] + peer[2]]
            pltpu.make_async_remote_copy(s_ref, slot, send_sems.at[k - 1], recv_sems.at[k - 1],
                                         device_id=peer, device_id_type=MESH).wait_recv()
        for cp in sends:
            cp.wait_send()
        acc = slots_ref[0]
        for d in range(1, N_DEV):
            acc = acc + slots_ref[d]
        out_ref[...] = acc

    vmem = pl.BlockSpec(memory_space=pltpu.VMEM)
    return pl.pallas_call(
        body, name="all_reduce_small", in_specs=[vmem], out_specs=vmem, out_shape=SDS((R, Wd), F32),
        scratch_shapes=[pltpu.VMEM((N_DEV, R, Wd), F32), pltpu.SemaphoreType.DMA((N_DEV - 1,)),
                        pltpu.SemaphoreType.DMA((N_DEV - 1,))],
    )(s)


BIG = {"ffn1_w_up": (D, 2 * FF, "col"), "ffn1_w_down": (FF, D, "row"), "w_in": (D, INW, "cm"),
       "w_attn_up": (AW, D, "row"), "w_pool_up": (PW, D, "col"), "w_out": (D, D, "row"),
       "ffn2_w_up": (D, 2 * FF, "col"), "ffn2_w_down": (FF, D, "row")}
GROUPS = (("ffn1_w_up", "ffn1_w_down"), ("w_in", "w_attn_up", "w_pool_up", "w_out"), ("ffn2_w_up", "ffn2_w_down"))
SMALL = ("ffn1_norm", "mix_norm", "ffn2_norm", "final_norm", "pool_scale", "sinks", "pool_w_mix")
SMALL_W = 128


def _rowlike(name):
    return BIG[name][2] != "col"


def _half_dims(name):
    k, n, kind = BIG[name]
    return (k // N_CHIPS // 2, n) if kind == "row" else (k // 2, n // N_CHIPS)


def shard_view(name, shard):
    return shard.reshape((2,) + _half_dims(name))


def full_from_view(name, fv):
    k, n, kind = BIG[name]
    if kind == "cm":
        return fv.reshape(N_CHIPS, k, n // N_CHIPS).transpose(1, 0, 2).reshape(k, n)
    return fv.reshape(k, n)


def grad_view(name, g):
    k, n, kind = BIG[name]
    kh, ns = _half_dims(name)
    if kind == "cm":
        return g.reshape(k, N_CHIPS, ns).transpose(1, 0, 2).reshape(N_CHIPS, 2, kh, ns)
    return g.reshape(_full_shape(jax.ShapeDtypeStruct((2, kh, ns), g.dtype), kind == "row"))


def pack_small(d):
    parts = []
    for name in SMALL:
        a = d[name].reshape(-1)
        pad = (-a.shape[0]) % SMALL_W
        parts.append(jnp.pad(a, (0, pad)).reshape(-1, SMALL_W))
    a = jnp.concatenate(parts, axis=0)
    return jnp.pad(a, ((0, (-a.shape[0]) % 8), (0, 0)))


def unpack_small(a, like):
    out, r0 = {}, 0
    for name in SMALL:
        size = int(np.prod(like[name].shape))
        rows = -(-size // SMALL_W)
        out[name] = a[r0:r0 + rows].reshape(-1)[:size].reshape(like[name].shape)
        r0 += rows
    return out


WEIGHTS = ("ffn1_norm", "ffn1_w_up", "ffn1_w_down", "mix_norm", "w_in", "sinks", "w_attn_up", "pool_w_mix",
           "pool_scale", "w_pool_up", "w_out", "ffn2_norm", "ffn2_w_up", "ffn2_w_down", "final_norm")


def kernel(x, ffn1_norm, ffn1_w_up, ffn1_w_down, mix_norm, w_in, sinks, w_attn_up, pool_w_mix, pool_scale, w_pool_up, w_out, ffn2_norm, ffn2_w_up, ffn2_w_down, final_norm, loss_target, m_ffn1_norm, m_ffn1_w_up, m_ffn1_w_down, m_mix_norm, m_w_in, m_sinks, m_w_attn_up, m_pool_w_mix, m_pool_scale, m_w_pool_up, m_w_out, m_ffn2_norm, m_ffn2_w_up, m_ffn2_w_down, m_final_norm, v_ffn1_norm, v_ffn1_w_up, v_ffn1_w_down, v_mix_norm, v_w_in, v_sinks, v_w_attn_up, v_pool_w_mix, v_pool_scale, v_w_pool_up, v_w_out, v_ffn2_norm, v_ffn2_w_up, v_ffn2_w_down, v_final_norm):
    given = dict(locals())
    w = {n: given[n] for n in WEIGHTS}
    m = {n: given["m_" + n] for n in WEIGHTS}
    v = {n: given["v_" + n] for n in WEIGHTS}
    cx, cy, cc = _me()
    place = jnp.stack([cc, 2 * cx + cy]).astype(jnp.int32)

    shards = {n: w[n][0] for n in BIG}
    sviews = {n: shard_view(n, shards[n].astype(BF16)) for n in BIG}
    grads, delta, new_m, new_v = {}, {}, {}, {}
    rowlikes = [[_rowlike(n) for n in names] for names in GROUPS]

    class Exchanges:
        def __init__(self):
            self.gathers, self.fulls, self.reductions = {}, {}, {}

        def _start_gather(self, group, after):
            sv = [sviews[n] for n in GROUPS[group]]
            rl = rowlikes[group]
            self.gathers[group] = exchange_start(sv, [_full_shape(s, r) for s, r in zip(sv, rl)], _gather_plan(rl),
                                                 after, f"gather_start_{group}")

        def weights(self, group, after):
            if not self.gathers:
                self._start_gather(0, sviews[GROUPS[0][0]])
            if group not in self.fulls:
                names, rl, state = GROUPS[group], rowlikes[group], self.gathers[group]
                fulls = exchange_wait(state, _gather_plan(rl), state["token"] if after is None else after,
                                      f"gather_wait_{group}")
                fulls = gather_finish([sviews[n] for n in names], fulls, rl, f"gather_finish_{group}")
                self.fulls[group] = {n: full_from_view(n, f) for n, f in zip(names, fulls)}
                if group + 1 < len(GROUPS):
                    self._start_gather(group + 1, fulls[0])
            return self.fulls[group]

        def started(self):
            return self.gathers[max(self.gathers)]["token"]

        def grads(self, g):
            names = tuple(g)
            rl = [_rowlike(n) for n in names]
            gv = [grad_view(n, g[n]) for n in names]
            gots = swap_halves(gv, rl, "swap_halves_" + names[0])
            qs = [add_halves(a, b, r, place, "add_halves_" + n) for a, b, r, n in zip(gv, gots, rl, names)]
            state = exchange_start(qs, [_slot_shape(q, r) for q, r in zip(qs, rl)], _all_to_all_plan(rl), qs[0],
                                   "all_to_all_start_" + names[0])
            self.reductions[names] = (gv, gots, state)
            return state["token"]

        def finish(self, names, after):
            rl = [_rowlike(n) for n in names]
            gv, gots, state = self.reductions.pop(names)
            recvs = exchange_wait(state, _all_to_all_plan(rl), after, "all_to_all_wait_" + names[0])
            halves = [sum_pieces(a, b, r, k, place, "sum_pieces_" + n)
                      for a, b, r, k, n in zip(gv, gots, recvs, rl, names)]
            for n, o in zip(names, join_halves(halves, "join_halves_" + names[0])):
                grads[n] = o.reshape(shards[n].shape)
                delta[n], new_m[n], new_v[n] = adamw(shards[n], grads[n], m[n][0], v[n][0], "adamw_" + n)

    S = {n: w[n].reshape(1, -1) for n in ("ffn1_norm", "mix_norm", "ffn2_norm", "final_norm", "pool_scale", "sinks")}
    S["pool_w_mix"] = w["pool_w_mix"][0].astype(BF16)
    loss, dx, G = local_fwd_bwd(x[0], loss_target[0], S, Exchanges())
    loss = lax.psum(loss[0, 0], ("x", "y", "c"))

    small_like = {n: w[n] for n in SMALL}
    grads.update(unpack_small(all_reduce_small(pack_small({n: G[n] for n in SMALL})), small_like))

    ds, ms, vs = adamw(pack_small(w), pack_small(grads), pack_small(m), pack_small(v), "adamw_small")
    delta.update(unpack_small(ds, small_like))
    new_m.update(unpack_small(ms, small_like))
    new_v.update(unpack_small(vs, small_like))

    def shaped(d, n):
        return d[n].reshape(w[n].shape)

    return (loss, dx[None], *[shaped(grads, n) for n in WEIGHTS], *[shaped(delta, n) for n in WEIGHTS],
            *[shaped(new_m, n) for n in WEIGHTS], *[shaped(new_v, n) for n in WEIGHTS])
```

```python
import numpy as np
import jax
import jax.numpy as jnp
from jax import lax
from jax.experimental import pallas as pl
from jax.experimental.pallas import tpu as pltpu

F32 = jnp.float32
BF16 = jnp.bfloat16
SDS = jax.ShapeDtypeStruct
MESH = pl.DeviceIdType.MESH

D = 1024
FF = 2816
NQ = 16
NKV = 2
HD = 64
GQ = NQ // NKV
AW = NQ * HD
KVW = NKV * HD
BLK = 128
PW = 512
PG = 128
POOL_WINDOWS = (2, 4, 8, 16)
HALO = 16
INW = AW + 2 * KVW + PW + 2 * D
C_KV = AW
C_Z = AW + 2 * KVW
C_G = C_Z + PW
EPS = 1e-6
FF_CHUNK = 256
FF_CHUNKS = tuple((c, FF_CHUNK) for c in range(0, FF, FF_CHUNK))
SLOPES = tuple(float(2.0 ** (-8.0 * h / NQ)) for h in range(1, NQ + 1))
SCALE = HD ** -0.5

LR, B1, B2, ADAM_EPS, WD, STEP = 0.001, 0.9, 0.999, 1e-08, 0.01, 10

VMEM_LIMIT = 56 * 1024 * 1024
N_CHIPS = 4

NT = (((1,), (1,)), ((), ()))
TN = (((0,), (0,)), ((), ()))


def _cp(sem=None, vmem=VMEM_LIMIT):
    return pltpu.CompilerParams(dimension_semantics=sem, vmem_limit_bytes=vmem)


def _const_spec(shape):
    nd = len(shape)
    return pl.BlockSpec(shape, lambda *_: (0,) * nd, pipeline_mode=pl.Buffered(1))


def _rstd(x):
    return lax.rsqrt(jnp.mean(x * x, axis=-1, keepdims=True) + EPS)


def _rms_bwd(dn, xhat, rstd, g):
    dxhat = dn * g
    return rstd * (dxhat - xhat * jnp.mean(dxhat * xhat, axis=-1, keepdims=True))


def _dot(a, b):
    return jnp.dot(a, b, preferred_element_type=F32)


def _dot_nt(a, b):
    return lax.dot_general(a, b, NT, preferred_element_type=F32)


def _dot_tn(a, b):
    return lax.dot_general(a, b, TN, preferred_element_type=F32)


def ffn_fwd(h, g, wup, wdn, name):
    T = h.shape[0]
    TM = 512

    def body(h_ref, g_ref, wup_ref, wdn_ref, out_ref, ab_ref, hid_ref):
        x = h_ref[...]
        n = (x * _rstd(x) * g_ref[...]).astype(BF16)
        for c0, w in FF_CHUNKS:
            a = _dot(n, wup_ref[:, c0:c0 + w])
            b = _dot(n, wup_ref[:, FF + c0:FF + c0 + w])
            ab_ref[:, c0:c0 + w] = a.astype(BF16)
            ab_ref[:, FF + c0:FF + c0 + w] = b.astype(BF16)
            hid_ref[:, c0:c0 + w] = (a * jax.nn.sigmoid(a) * b).astype(BF16)
        out_ref[...] = x + 0.5 * _dot(hid_ref[...], wdn_ref[...])

    return pl.pallas_call(
        body, name=name, grid=(T // TM,),
        in_specs=[pl.BlockSpec((TM, D), lambda i: (i, 0)), _const_spec((1, D)),
                  _const_spec((D, 2 * FF)), _const_spec((FF, D))],
        out_specs=[pl.BlockSpec((TM, D), lambda i: (i, 0)), pl.BlockSpec((TM, 2 * FF), lambda i: (i, 0)),
                   pl.BlockSpec((TM, FF), lambda i: (i, 0))],
        out_shape=[SDS((T, D), F32), SDS((T, 2 * FF), BF16), SDS((T, FF), BF16)],
        compiler_params=_cp(("arbitrary",)),
    )(h, g, wup, wdn)


def ffn_bwd_x(dh, h_in, ab, g, wup, wdn, name):
    T = dh.shape[0]
    TM = 256

    def body(dh_ref, h_ref, ab_ref, g_ref, wup_ref, wdn_ref, dhin_ref, dup_ref, n_ref, dg_ref):
        x = h_ref[...]
        g = g_ref[...]
        rstd = _rstd(x)
        xhat = x * rstd
        n_ref[...] = (xhat * g).astype(BF16)
        dh = dh_ref[...]
        dhh = (0.5 * dh).astype(BF16)
        for c0, w in FF_CHUNKS:
            dhid = _dot_nt(dhh, wdn_ref[c0:c0 + w, :])
            a = ab_ref[:, c0:c0 + w].astype(F32)
            b = ab_ref[:, FF + c0:FF + c0 + w].astype(F32)
            sig = jax.nn.sigmoid(a)
            s = a * sig
            dup_ref[:, c0:c0 + w] = (dhid * b * (sig * (1.0 + a * (1.0 - sig)))).astype(BF16)
            dup_ref[:, FF + c0:FF + c0 + w] = (dhid * s).astype(BF16)
        dn = _dot_nt(dup_ref[...], wup_ref[...])
        dhin_ref[...] = dh + _rms_bwd(dn, xhat, rstd, g)

        @pl.when(pl.program_id(0) == 0)
        def _():
            dg_ref[...] = jnp.zeros_like(dg_ref)

        dg_ref[...] += jnp.sum(dn * xhat, axis=0, keepdims=True)

    tile = lambda w: pl.BlockSpec((TM, w), lambda i: (i, 0))
    return pl.pallas_call(
        body, name=name, grid=(T // TM,),
        in_specs=[tile(D), tile(D), tile(2 * FF), _const_spec((1, D)), _const_spec((D, 2 * FF)), _const_spec((FF, D))],
        out_specs=[tile(D), tile(2 * FF), tile(D), pl.BlockSpec((1, D), lambda i: (0, 0))],
        out_shape=[SDS((T, D), F32), SDS((T, 2 * FF), BF16), SDS((T, D), BF16), SDS((1, D), F32)],
        compiler_params=_cp(("arbitrary",)),
    )(dh, h_in, ab, g, wup, wdn)


def matmul_tn(a, b, name, *, tm, tn, tt=1024, b_scale=None):
    T, M = a.shape
    N = b.shape[1]
    tt = min(tt, T)
    assert M % tm == 0 and N % tn == 0 and T % tt == 0

    def body(a_ref, b_ref, o_ref):
        @pl.when(pl.program_id(2) == 0)
        def _():
            o_ref[...] = jnp.zeros_like(o_ref)

        bv = b_ref[...]
        if b_scale is not None:
            bv = bv * b_scale
        o_ref[...] += _dot_tn(a_ref[...].astype(BF16), bv.astype(BF16))

    return pl.pallas_call(
        body, name=name, grid=(M // tm, N // tn, T // tt),
        in_specs=[pl.BlockSpec((tt, tm), lambda i, j, t: (t, i)), pl.BlockSpec((tt, tn), lambda i, j, t: (t, j))],
        out_specs=pl.BlockSpec((tm, tn), lambda i, j, t: (i, j)),
        out_shape=SDS((M, N), F32),
        compiler_params=_cp(("parallel", "parallel", "arbitrary")),
    )(a, b)


def mix_in_fwd(h1, g, win):
    T = h1.shape[0]
    TM = 512

    def body(h_ref, g_ref, w_ref, u_ref, q_ref, kv_ref, z_ref, gt_ref):
        x = h_ref[...]
        u = (x * _rstd(x) * g_ref[...]).astype(BF16)
        u_ref[...] = u
        for c in range(0, AW, 256):
            q_ref[:, c:c + 256] = _dot(u, w_ref[:, c:c + 256]).astype(BF16)
        kv_ref[...] = _dot(u, w_ref[:, C_KV:C_Z]).astype(BF16)
        for c in range(0, PW, 256):
            z_ref[:, c:c + 256] = _dot(u, w_ref[:, C_Z + c:C_Z + c + 256])
        for c in range(0, 2 * D, 256):
            gt_ref[:, c:c + 256] = _dot(u, w_ref[:, C_G + c:C_G + c + 256]).astype(BF16)

    tile = lambda w: pl.BlockSpec((TM, w), lambda i: (i, 0))
    return pl.pallas_call(
        body, name="mix_in_fwd", grid=(T // TM,),
        in_specs=[tile(D), _const_spec((1, D)), _const_spec((D, INW))],
        out_specs=[tile(D), tile(AW), tile(2 * KVW), tile(PW), tile(2 * D)],
        out_shape=[SDS((T, D), BF16), SDS((T, AW), BF16), SDS((T, 2 * KVW), BF16), SDS((T, PW), F32),
                   SDS((T, 2 * D), BF16)],
        compiler_params=_cp(("arbitrary",)),
    )(h1, g, win)


def mix_in_bwd(dq, dkv, dz, dgt, dh2, h1, g, win):
    T = h1.shape[0]
    TM = 512

    def body(dq_ref, dkv_ref, dz_ref, dgt_ref, dh2_ref, h_ref, g_ref, w_ref, dh1_ref, dg_ref):
        du = _dot_nt(dq_ref[...], w_ref[:, 0:AW])
        du += _dot_nt(dkv_ref[...], w_ref[:, C_KV:C_Z])
        du += _dot_nt(dz_ref[...], w_ref[:, C_Z:C_G])
        du += _dot_nt(dgt_ref[...], w_ref[:, C_G:INW])
        x = h_ref[...]
        g = g_ref[...]
        rstd = _rstd(x)
        xhat = x * rstd
        dh1_ref[...] = dh2_ref[...] + _rms_bwd(du, xhat, rstd, g)

        @pl.when(pl.program_id(0) == 0)
        def _():
            dg_ref[...] = jnp.zeros_like(dg_ref)

        dg_ref[...] += jnp.sum(du * xhat, axis=0, keepdims=True)

    tile = lambda w: pl.BlockSpec((TM, w), lambda i: (i, 0))
    return pl.pallas_call(
        body, name="mix_in_bwd", grid=(T // TM,),
        in_specs=[tile(AW), tile(2 * KVW), tile(PW), tile(2 * D), tile(D), tile(D), _const_spec((1, D)),
                  _const_spec((D, INW))],
        out_specs=[tile(D), pl.BlockSpec((1, D), lambda i: (0, 0))],
        out_shape=[SDS((T, D), F32), SDS((1, D), F32)],
        compiler_params=_cp(("arbitrary",)),
    )(dq, dkv, dz, dgt, dh2, h1, g, win)


PAIR = 2 * HD
NPAIR = GQ // 2


def _lo_lanes():
    return lax.broadcasted_iota(jnp.int32, (BLK, PAIR), 1) < HD


def _stack_heads(ref, kvh, scale=None):
    lo = _lo_lanes()
    parts = []
    for pr in range(NPAIR):
        t = ref[:, (kvh * NPAIR + pr) * PAIR:(kvh * NPAIR + pr + 1) * PAIR]
        if scale is not None:
            t = t * scale
        zero = jnp.zeros_like(t)
        parts += [jnp.where(lo, t, zero), jnp.where(lo, zero, t)]
    return jnp.concatenate(parts, axis=0)


def _kv_tiles(kvc_ref, kvp_ref, tile, kvh):
    lo = _lo_lanes()
    dup, left, right = [], [], []
    for ref in (kvp_ref, kvc_ref):
        t = ref[:, tile * PAIR:(tile + 1) * PAIR]
        r = pltpu.roll(t.astype(F32), HD, 1).astype(BF16)
        zero = jnp.zeros_like(t)
        a, b = (t, r) if kvh == 0 else (r, t)
        dup.append(jnp.where(lo, a, b))
        left.append(jnp.where(lo, a, zero))
        right.append(jnp.where(lo, zero, b))
    cat = lambda xs: jnp.concatenate(xs, axis=0)
    return cat(dup), cat(left), cat(right)


def _band_consts(first):
    row = lax.broadcasted_iota(jnp.int32, (BLK, BLK), 0)
    col = lax.broadcasted_iota(jnp.int32, (BLK, BLK), 1)
    upper = col > row
    dist = jnp.where(upper, row - col + BLK, row - col).astype(F32)
    pen = jnp.where(jnp.logical_and(upper, first), -jnp.inf, 0.0)
    return upper, dist, pen


def _split_band(upper, t):
    zero = jnp.zeros_like(t)
    return jnp.concatenate([jnp.where(upper, t, zero), jnp.where(upper, zero, t)], axis=1)


def attn_fwd(q, kv, sinks):
    T = q.shape[0]
    nb = T // BLK

    def body(sink_ref, q_ref, kvc_ref, kvp_ref, att_ref, lse_ref):
        upper, dist, pen = _band_consts(pl.program_id(0) == 0)
        for kvh in range(NKV):
            kdup, _, _ = _kv_tiles(kvc_ref, kvp_ref, 0, kvh)
            _, vleft, vright = _kv_tiles(kvc_ref, kvp_ref, 1, kvh)
            s_all = _dot_nt(_stack_heads(q_ref, kvh, SCALE), kdup)
            for pr in range(NPAIR):
                out = None
                for side, vpad in ((0, vleft), (1, vright)):
                    g = 2 * pr + side
                    hq = kvh * GQ + g
                    sink = sink_ref[0, hq]
                    rows = slice(g * BLK, (g + 1) * BLK)
                    s = jnp.where(upper, s_all[rows, 0:BLK], s_all[rows, BLK:2 * BLK]) - SLOPES[hq] * dist + pen
                    m = jnp.maximum(jnp.max(s, axis=-1, keepdims=True), sink)
                    p = jnp.exp(s - m)
                    l = jnp.sum(p, axis=-1, keepdims=True) + jnp.exp(sink - m)
                    lse_ref[:, hq:hq + 1] = m + jnp.log(l)
                    o = _dot(_split_band(upper, (p * (1.0 / l)).astype(BF16)), vpad)
                    out = o if out is None else out + o
                col0 = (kvh * NPAIR + pr) * PAIR
                att_ref[:, col0:col0 + PAIR] = out.astype(BF16)

    return pl.pallas_call(
        body, name="attn_fwd", grid=(nb,),
        in_specs=[pl.BlockSpec(memory_space=pltpu.SMEM),
                  pl.BlockSpec((BLK, AW), lambda i: (i, 0)),
                  pl.BlockSpec((BLK, 2 * KVW), lambda i: (i, 0)),
                  pl.BlockSpec((BLK, 2 * KVW), lambda i: (jnp.maximum(i - 1, 0), 0))],
        out_specs=[pl.BlockSpec((BLK, AW), lambda i: (i, 0)), pl.BlockSpec((BLK, NQ), lambda i: (i, 0))],
        out_shape=[SDS((T, AW), BF16), SDS((T, NQ), F32)],
        compiler_params=_cp(("arbitrary",)),
    )(sinks, q, kv, kv)


def attn_bwd(q, kv, datt, lse, sinks):
    T = q.shape[0]
    nb = T // BLK

    def body(sink_ref, q_ref, kvc_ref, kvp_ref, do_ref, lse_ref, dq_ref, dkv_ref, dsink_ref, carry_ref):
        i = pl.program_id(0)

        @pl.when(i == 0)
        def _():
            dsink_ref[...] = jnp.zeros_like(dsink_ref)
            carry_ref[...] = jnp.zeros_like(carry_ref)

        @pl.when(i < nb)
        def _():
            upper, dist, pen = _band_consts(i == 0)
            lo = _lo_lanes()
            dk_dup, dv_dup = [], []
            for kvh in range(NKV):
                kdup, kleft, kright = _kv_tiles(kvc_ref, kvp_ref, 0, kvh)
                vdup, _, _ = _kv_tiles(kvc_ref, kvp_ref, 1, kvh)
                qs = _stack_heads(q_ref, kvh, SCALE)
                dos = _stack_heads(do_ref, kvh)
                s_all = _dot_nt(qs, kdup)
                dp_all = _dot_nt(dos, vdup)
                ds_parts, p_parts = [], []
                for pr in range(NPAIR):
                    dq = None
                    for side, kpad in ((0, kleft), (1, kright)):
                        g = 2 * pr + side
                        hq = kvh * GQ + g
                        lse_h = lse_ref[:, hq:hq + 1]
                        rows = slice(g * BLK, (g + 1) * BLK)
                        s = jnp.where(upper, s_all[rows, 0:BLK], s_all[rows, BLK:2 * BLK]) - SLOPES[hq] * dist + pen
                        p = jnp.exp(s - lse_h)
                        dp = jnp.where(upper, dp_all[rows, 0:BLK], dp_all[rows, BLK:2 * BLK])
                        delta = jnp.sum(p * dp, axis=-1, keepdims=True)
                        dsink_ref[:, hq:hq + 1] += -jnp.sum(jnp.exp(sink_ref[0, hq] - lse_h) * delta, axis=0,
                                                            keepdims=True)
                        ds = _split_band(upper, (p * (dp - delta)).astype(BF16))
                        ds_parts.append(ds)
                        p_parts.append(_split_band(upper, p.astype(BF16)))
                        d = _dot(ds, kpad)
                        dq = d if dq is None else dq + d
                    col0 = (kvh * NPAIR + pr) * PAIR
                    dq_ref[:, col0:col0 + PAIR] = (dq * SCALE).astype(BF16)
                dkw = _dot_tn(jnp.concatenate(ds_parts, axis=0), qs)
                dvw = _dot_tn(jnp.concatenate(p_parts, axis=0), dos)
                dk_dup.append(dkw + pltpu.roll(dkw, HD, 1))
                dv_dup.append(dvw + pltpu.roll(dvw, HD, 1))
            dk = jnp.where(jnp.concatenate([lo, lo], axis=0), dk_dup[0], dk_dup[1])
            dv = jnp.where(jnp.concatenate([lo, lo], axis=0), dv_dup[0], dv_dup[1])
            dkv_ref[:, 0:PAIR] = (carry_ref[:, 0:PAIR] + dk[0:BLK]).astype(BF16)
            dkv_ref[:, PAIR:2 * PAIR] = (carry_ref[:, PAIR:2 * PAIR] + dv[0:BLK]).astype(BF16)
            carry_ref[:, 0:PAIR] = dk[BLK:2 * BLK]
            carry_ref[:, PAIR:2 * PAIR] = dv[BLK:2 * BLK]

        @pl.when(i == nb)
        def _():
            dkv_ref[...] = carry_ref[...].astype(BF16)

    cur = lambda i: (jnp.minimum(i, nb - 1), 0)
    prev = lambda i: (jnp.maximum(jnp.minimum(i, nb - 1) - 1, 0), 0)
    return pl.pallas_call(
        body, name="attn_bwd", grid=(nb + 1,),
        in_specs=[pl.BlockSpec(memory_space=pltpu.SMEM),
                  pl.BlockSpec((BLK, AW), cur), pl.BlockSpec((BLK, 2 * KVW), cur), pl.BlockSpec((BLK, 2 * KVW), prev),
                  pl.BlockSpec((BLK, AW), cur), pl.BlockSpec((BLK, NQ), cur)],
        out_specs=[pl.BlockSpec((BLK, AW), cur),
                   pl.BlockSpec((BLK, 2 * KVW), lambda i: (jnp.maximum(i - 1, 0), 0)),
                   pl.BlockSpec((1, NQ), lambda i: (0, 0))],
        out_shape=[SDS((T, AW), BF16), SDS((T, 2 * KVW), BF16), SDS((1, NQ), F32)],
        scratch_shapes=[pltpu.VMEM((BLK, 2 * KVW), F32)],
        compiler_params=_cp(("arbitrary",)),
    )(sinks, q, kv, kv, datt, lse)


def _inv_counts(t0, rows):
    t = (t0 + lax.broadcasted_iota(jnp.int32, (rows, 1), 0) + 1).astype(F32)
    return [1.0 / jnp.minimum(t, float(w)) for w in POOL_WINDOWS]


def pool_fwd(z, wmix, scale):
    T = z.shape[0]
    TM = 512
    L = TM + HALO

    def body(z_ref, halo_ref, wmix_ref, scale_ref, pooled_ref, mixs_ref):
        i = pl.program_id(0)
        halo = jnp.where(i > 0, halo_ref[...], 0.0)
        zt = z_ref[...]
        e = jnp.concatenate([halo, zt], axis=0)
        sums = []
        s = e
        for k in (1, 2, 4, 8):
            s = s + pltpu.roll(s, k, 0)
            sums.append(s)
        inv = _inv_counts(i * TM, TM)
        for gi in range(len(POOL_WINDOWS)):
            cols = slice(gi * PG, (gi + 1) * PG)
            pooled = (sums[gi][HALO:, cols] * inv[gi] - zt[:, cols]).astype(BF16)
            pooled_ref[:, cols] = pooled
            mixs_ref[:, cols] = (_dot(pooled, wmix_ref[gi]) * scale_ref[:, cols]).astype(BF16)

    return pl.pallas_call(
        body, name="pool_fwd", grid=(T // TM,),
        in_specs=[pl.BlockSpec((TM, PW), lambda i: (i, 0)),
                  pl.BlockSpec((HALO, PW), lambda i: (jnp.maximum(i * (TM // HALO) - 1, 0), 0)),
                  _const_spec((len(POOL_WINDOWS), PG, PG)), _const_spec((1, PW))],
        out_specs=[pl.BlockSpec((TM, PW), lambda i: (i, 0)), pl.BlockSpec((TM, PW), lambda i: (i, 0))],
        out_shape=[SDS((T, PW), BF16), SDS((T, PW), BF16)],
        compiler_params=_cp(("arbitrary",)),
    )(z, z, wmix, scale)


def pool_bwd(dmixs, pooled, wmix, scale):
    T = dmixs.shape[0]
    TM = 512
    L = TM + HALO
    nt = T // TM

    def body(dm_ref, halo_ref, pooled_ref, wmix_ref, scale_ref, dz_ref, dwmix_ref, dscale_ref):
        i = pl.program_id(0)

        @pl.when(i == 0)
        def _():
            dwmix_ref[...] = jnp.zeros_like(dwmix_ref)
            dscale_ref[...] = jnp.zeros_like(dscale_ref)

        halo = jnp.where(i < nt - 1, halo_ref[...], 0.0)
        dm = dm_ref[...]
        e = jnp.concatenate([dm, halo], axis=0)
        inv = _inv_counts(i * TM, L)
        for gi in range(len(POOL_WINDOWS)):
            cols = slice(gi * PG, (gi + 1) * PG)
            w = wmix_ref[gi]
            dmixed = (e[:, cols] * scale_ref[:, cols]).astype(BF16)
            dpooled = _dot_nt(dmixed, w)
            pooled = pooled_ref[:, cols]
            mixed = _dot(pooled, w)
            dscale_ref[:, cols] += jnp.sum(dm[:, cols] * mixed, axis=0, keepdims=True)
            dwmix_ref[gi] += _dot_tn(pooled, dmixed[:TM])
            s = dpooled * inv[gi]
            k = 1
            while k < POOL_WINDOWS[gi]:
                s = s + pltpu.roll(s, L - k, 0)
                k *= 2
            dz_ref[:, cols] = (s[:TM] - dpooled[:TM]).astype(BF16)

    return pl.pallas_call(
        body, name="pool_bwd", grid=(nt,),
        in_specs=[pl.BlockSpec((TM, PW), lambda i: (i, 0)),
                  pl.BlockSpec((HALO, PW), lambda i: (jnp.minimum((i + 1) * (TM // HALO), T // HALO - 1), 0)),
                  pl.BlockSpec((TM, PW), lambda i: (i, 0)),
                  _const_spec((len(POOL_WINDOWS), PG, PG)), _const_spec((1, PW))],
        out_specs=[pl.BlockSpec((TM, PW), lambda i: (i, 0)),
                   pl.BlockSpec((len(POOL_WINDOWS), PG, PG), lambda i: (0, 0, 0)),
                   pl.BlockSpec((1, PW), lambda i: (0, 0))],
        out_shape=[SDS((T, PW), BF16), SDS((len(POOL_WINDOWS), PG, PG), F32), SDS((1, PW), F32)],
        compiler_params=_cp(("arbitrary",)),
    )(dmixs, dmixs, pooled, wmix, scale)


def merge_fwd(att, mixs, gt, h1, wattn, wpool, wout):
    T = h1.shape[0]
    TM = 512

    def body(att_ref, mixs_ref, gt_ref, h_ref, wa_ref, wp_ref, wo_ref, h2_ref, mg_ref):
        a = _dot(att_ref[...], wa_ref[...])
        p = _dot(mixs_ref[...], wp_ref[...])
        merged = (jax.nn.sigmoid(gt_ref[:, 0:D].astype(F32)) * a + jax.nn.sigmoid(gt_ref[:, D:2 * D].astype(F32)) * p)
        mg = merged.astype(BF16)
        mg_ref[...] = mg
        h2_ref[...] = h_ref[...] + _dot(mg, wo_ref[...])

    tile = lambda w: pl.BlockSpec((TM, w), lambda i: (i, 0))
    return pl.pallas_call(
        body, name="merge_fwd", grid=(T // TM,),
        in_specs=[tile(AW), tile(PW), tile(2 * D), tile(D), _const_spec((AW, D)), _const_spec((PW, D)),
                  _const_spec((D, D))],
        out_specs=[tile(D), tile(D)],
        out_shape=[SDS((T, D), F32), SDS((T, D), BF16)],
        compiler_params=_cp(("arbitrary",)),
    )(att, mixs, gt, h1, wattn, wpool, wout)


def merge_bwd(dh2, att, mixs, gt, wattn, wpool, wout):
    T = dh2.shape[0]
    TM = 512

    def body(dh2_ref, att_ref, mixs_ref, gt_ref, wa_ref, wp_ref, wo_ref, datt_ref, dmixs_ref, dgt_ref, da_ref, dp_ref):
        dm = _dot_nt(dh2_ref[...].astype(BF16), wo_ref[...])
        a = _dot(att_ref[...], wa_ref[...])
        p = _dot(mixs_ref[...], wp_ref[...])
        sa = jax.nn.sigmoid(gt_ref[:, 0:D].astype(F32))
        sp = jax.nn.sigmoid(gt_ref[:, D:2 * D].astype(F32))
        da = (dm * sa).astype(BF16)
        dp = (dm * sp).astype(BF16)
        da_ref[...] = da
        dp_ref[...] = dp
        dgt_ref[:, 0:D] = (dm * a * sa * (1.0 - sa)).astype(BF16)
        dgt_ref[:, D:2 * D] = (dm * p * sp * (1.0 - sp)).astype(BF16)
        datt_ref[...] = _dot_nt(da, wa_ref[...]).astype(BF16)
        dmixs_ref[...] = _dot_nt(dp, wp_ref[...])

    tile = lambda w: pl.BlockSpec((TM, w), lambda i: (i, 0))
    return pl.pallas_call(
        body, name="merge_bwd", grid=(T // TM,),
        in_specs=[tile(D), tile(AW), tile(PW), tile(2 * D), _const_spec((AW, D)), _const_spec((PW, D)),
                  _const_spec((D, D))],
        out_specs=[tile(AW), tile(PW), tile(2 * D), tile(D), tile(D)],
        out_shape=[SDS((T, AW), BF16), SDS((T, PW), F32), SDS((T, 2 * D), BF16), SDS((T, D), BF16),
                   SDS((T, D), BF16)],
        compiler_params=_cp(("arbitrary",)),
    )(dh2, att, mixs, gt, wattn, wpool, wout)


def loss_head(h3, target, g):
    T = h3.shape[0]
    TM = 512

    def body(h_ref, t_ref, g_ref, dh_ref, loss_ref, dg_ref):
        @pl.when(pl.program_id(0) == 0)
        def _():
            loss_ref[...] = jnp.zeros_like(loss_ref)
            dg_ref[...] = jnp.zeros_like(dg_ref)

        x = h_ref[...]
        g = g_ref[...]
        rstd = _rstd(x)
        xhat = x * rstd
        err = xhat * g - t_ref[...]
        loss_ref[...] += 0.5 * jnp.sum(jnp.mean(err * err, axis=-1, keepdims=True), axis=0, keepdims=True)
        dy = err * (1.0 / D)
        dg_ref[...] += jnp.sum(dy * xhat, axis=0, keepdims=True)
        dh_ref[...] = _rms_bwd(dy, xhat, rstd, g)

    tile = pl.BlockSpec((TM, D), lambda i: (i, 0))
    return pl.pallas_call(
        body, name="loss_head", grid=(T // TM,),
        in_specs=[tile, tile, _const_spec((1, D))],
        out_specs=[tile, pl.BlockSpec((1, 1), lambda i: (0, 0)), pl.BlockSpec((1, D), lambda i: (0, 0))],
        out_shape=[SDS((T, D), F32), SDS((1, 1), F32), SDS((1, D), F32)],
        compiler_params=_cp(("arbitrary",)),
    )(h3, target, g)


def adamw(w, g, m, v, name):
    R, C = w.shape
    tile_bytes = 2 * 1024 * 1024
    tr = R
    if R * C * 4 > tile_bytes:
        tr = next(cand for cand in (512, 256, 128, 64, 32, 16, 8) if R % cand == 0 and cand * C * 4 <= tile_bytes)
    c1 = 1.0 - B1 ** STEP
    c2 = 1.0 - B2 ** STEP

    def body(w_ref, g_ref, m_ref, v_ref, go_ref, d_ref, nm_ref, nv_ref):
        gv = g_ref[...]
        go_ref[...] = gv
        nm = B1 * m_ref[...] + (1.0 - B1) * gv
        nv = B2 * v_ref[...] + (1.0 - B2) * (gv * gv)
        nm_ref[...] = nm
        nv_ref[...] = nv
        d_ref[...] = -LR * ((nm / c1) / (jnp.sqrt(nv / c2) + ADAM_EPS) + WD * w_ref[...])

    spec = pl.BlockSpec((tr, C), lambda i: (i, 0))
    return pl.pallas_call(
        body, name=name, grid=(R // tr,),
        in_specs=[spec] * 4, out_specs=[spec] * 4, out_shape=[SDS((R, C), F32)] * 4,
        compiler_params=_cp(("parallel",)),
    )(w, g, m, v)


GROUP_FFN1, GROUP_MIX, GROUP_FFN2 = 0, 1, 2


def _behind(small, token):
    return small if token is None else small + token[0:1, 0:1]


def local_fwd_bwd(x, target, S, comm):
    W = comm.weights(GROUP_FFN1, None)
    h1, ab1, hid1 = ffn_fwd(x, _behind(S["ffn1_norm"], comm.started()), W["ffn1_w_up"], W["ffn1_w_down"], "ffn1_fwd")
    W = comm.weights(GROUP_MIX, h1)
    u, q, kv, z, gt = mix_in_fwd(h1, _behind(S["mix_norm"], comm.started()), W["w_in"])
    att, lse = attn_fwd(q, kv, S["sinks"])
    pooled, mixs = pool_fwd(z, S["pool_w_mix"], S["pool_scale"])
    h2, merged = merge_fwd(att, mixs, gt, h1, W["w_attn_up"], W["w_pool_up"], W["w_out"])
    W = comm.weights(GROUP_FFN2, h2)
    h3, ab2, hid2 = ffn_fwd(h2, S["ffn2_norm"], W["ffn2_w_up"], W["ffn2_w_down"], "ffn2_fwd")
    dh3, loss, g_final = loss_head(h3, target, S["final_norm"])

    G = {"final_norm": g_final}
    dh2, dup2, n2, G["ffn2_norm"] = ffn_bwd_x(dh3, h2, ab2, S["ffn2_norm"], W["ffn2_w_up"], W["ffn2_w_down"],
                                              "ffn2_bwd_x")
    ffn2 = ("ffn2_w_down", "ffn2_w_up")
    comm.grads({"ffn2_w_down": matmul_tn(hid2, dh3, "ffn2_dw_down", tm=1408, tn=512, tt=4096, b_scale=0.5),
                "ffn2_w_up": matmul_tn(n2, dup2, "ffn2_dw_up", tm=D, tn=512, tt=4096)})

    W = comm.weights(GROUP_MIX, None)
    datt, dmixs, dgt, da, dp = merge_bwd(dh2, att, mixs, gt, W["w_attn_up"], W["w_pool_up"], W["w_out"])
    token = comm.advance(ffn2, datt)
    g_mix = {"w_out": matmul_tn(merged, dh2, "dw_out", tm=D, tn=D),
             "w_attn_up": matmul_tn(att, da, "dw_attn_up", tm=AW, tn=D),
             "w_pool_up": matmul_tn(mixs, dp, "dw_pool_up", tm=PW, tn=D)}
    dz, G["pool_w_mix"], G["pool_scale"] = pool_bwd(dmixs, pooled, S["pool_w_mix"], _behind(S["pool_scale"], token))
    dq, dkv, G["sinks"] = attn_bwd(q, kv, datt, lse, _behind(S["sinks"], token))
    dh1, G["mix_norm"] = mix_in_bwd(dq, dkv, dz, dgt, dh2, h1, S["mix_norm"], W["w_in"])
    g_mix["w_in"] = jnp.concatenate([
        matmul_tn(u, dq, "dw_in_q", tm=D, tn=AW),
        matmul_tn(u, dkv, "dw_in_kv", tm=D, tn=2 * KVW),
        matmul_tn(u, dz, "dw_in_z", tm=D, tn=PW),
        matmul_tn(u, dgt, "dw_in_g", tm=D, tn=512, tt=4096),
    ], axis=1)
    mix = tuple(g_mix)
    comm.finish(ffn2, dh1)
    comm.grads(g_mix)

    W = comm.weights(GROUP_FFN1, None)
    g_dn1 = matmul_tn(hid1, dh1, "ffn1_dw_down", tm=1408, tn=512, tt=4096, b_scale=0.5)
    token = comm.advance(mix, g_dn1)
    token2 = comm.grads({"ffn1_w_down": g_dn1})
    g1 = _behind(_behind(S["ffn1_norm"], token), token2)
    dx, dup1, n1, G["ffn1_norm"] = ffn_bwd_x(dh1, x, ab1, g1, W["ffn1_w_up"], W["ffn1_w_down"], "ffn1_bwd_x")
    comm.advance(("ffn1_w_down",), dx)
    comm.small_start(G, dx)
    g_up1 = matmul_tn(n1, dup1, "ffn1_dw_up", tm=D, tn=512, tt=4096)
    token = comm.grads({"ffn1_w_up": g_up1})
    done = comm.finish(mix, token)
    token = comm.advance(("ffn1_w_up",), done)
    done = comm.finish(("ffn1_w_down",), token)
    done = comm.small_finish(done)
    comm.finish(("ffn1_w_up",), done)
    return loss, dx


HBM_SPEC = pl.BlockSpec(memory_space=pltpu.HBM)


def _me():
    return lax.axis_index("x"), lax.axis_index("y"), lax.axis_index("c")


def _peer_chip(x, y, k):
    return x ^ (k >> 1), y ^ (k & 1)


def _piece_half(ref, rowlike, j, h):
    if rowlike:
        return ref.at[j, h]
    ns = ref.shape[-1] // N_CHIPS
    return ref.at[h, :, pl.ds(pl.multiple_of(j * ns, 128), ns)]


def _piece(ref, rowlike, j):
    if rowlike:
        return ref.at[j]
    ns = ref.shape[-1] // N_CHIPS
    return ref.at[:, :, pl.ds(pl.multiple_of(j * ns, 128), ns)]


def _full_shape(shard_view, rowlike):
    _, kh, ns = shard_view.shape
    return (N_CHIPS, 2, kh, ns) if rowlike else (2, kh, N_CHIPS * ns)


def _remote(src, dst, send_sem, recv_sem, dev):
    return pltpu.make_async_remote_copy(src, dst, send_sem, recv_sem, device_id=dev, device_id_type=MESH)


SEM_SPEC = pl.BlockSpec(memory_space=pltpu.SEMAPHORE)
SPLIT_PARAMS = pltpu.CompilerParams(has_side_effects=pltpu.SideEffectType.DATAFLOW_SIDE_EFFECTING)


def _gather_plan(rowlikes):
    def plan(s_refs, f_refs, a, k, x, y, c):
        px, py = _peer_chip(x, y, k)
        return (s_refs[a].at[c], _piece_half(f_refs[a], rowlikes[a], 2 * x + y, c),
                _piece_half(f_refs[a], rowlikes[a], 2 * px + py, c), (px, py, c))
    return plan


def _all_to_all_plan(rowlikes):
    def plan(q_refs, r_refs, a, k, x, y, c):
        px, py = _peer_chip(x, y, k)
        if rowlikes[a]:
            src = q_refs[a].at[2 * px + py]
        else:
            ns = q_refs[a].shape[-1] // N_CHIPS
            src = q_refs[a].at[:, pl.ds(pl.multiple_of((2 * px + py) * ns, 128), ns)]
        return src, r_refs[a].at[k - 1], r_refs[a].at[k - 1], (px, py, c)
    return plan


def _swap_plan(rowlikes):
    def plan(g_refs, got_refs, a, k, x, y, c):
        src = g_refs[a].at[:, 1 - c] if rowlikes[a] else g_refs[a].at[1 - c]
        return src, got_refs[a], got_refs[a], (x, y, 1 - c)
    return plan


def _everyone_plan(s_refs, slot_refs, a, k, x, y, c):
    px, py, pc = x ^ (k >> 2), y ^ ((k >> 1) & 1), c ^ (k & 1)
    return s_refs[a], slot_refs[a].at[4 * x + 2 * y + c], slot_refs[a].at[4 * px + 2 * py + pc], (px, py, pc)


CHIPS, SIBLING, EVERYONE = (1, 2, 3), (1,), tuple(range(1, 8))


def exchange_start(srcs, land_shapes, plan, after, name, peers=CHIPS):
    n = len(srcs)
    lands = [lax.empty(shape, s.dtype) for shape, s in zip(land_shapes, srcs)]

    def body(*refs):
        s_refs, l_refs = refs[:n], refs[n:2 * n]
        send_sems, recv_sems = refs[2 * n + 1], refs[2 * n + 2]
        token = refs[-1]
        x, y, c = _me()
        for a in range(n):
            for i, k in enumerate(peers):
                src, dst, _, peer = plan(s_refs, l_refs, a, k, x, y, c)
                sem = len(peers) * a + i
                _remote(src, dst, send_sems.at[sem], recv_sems.at[sem], peer).start()
        token[...] = jnp.zeros_like(token)

    n_sems = len(peers) * n
    outs = pl.pallas_call(
        body, name=name, in_specs=[HBM_SPEC] * (2 * n) + [pl.BlockSpec(memory_space=pl.ANY)],
        out_specs=[SEM_SPEC, SEM_SPEC] + [HBM_SPEC] * (2 * n) + [pl.BlockSpec(memory_space=pltpu.VMEM)],
        out_shape=[pltpu.SemaphoreType.DMA((n_sems,)), pltpu.SemaphoreType.DMA((n_sems,))]
        + [pltpu.HBM(a.shape, a.dtype) for a in (*srcs, *lands)] + [SDS((8, 128), F32)],
        input_output_aliases={i: 2 + i for i in range(2 * n)},
        compiler_params=SPLIT_PARAMS,
    )(*[pltpu.with_memory_space_constraint(a, pltpu.HBM) for a in (*srcs, *lands)], after)
    return {"sems": outs[:2], "srcs": outs[2:2 + n], "lands": outs[2 + n:2 + 2 * n], "token": outs[-1]}


def exchange_wait(state, plan, after, name, peers=CHIPS):
    n = len(state["srcs"])

    def body(*refs):
        s_refs, l_refs = refs[:n], refs[n:2 * n]
        send_sems, recv_sems = refs[2 * n], refs[2 * n + 1]
        x, y, c = _me()
        for a in range(n):
            for i, k in enumerate(peers):
                src, _, landing, peer = plan(s_refs, l_refs, a, k, x, y, c)
                sem = len(peers) * a + i
                cp = _remote(src, landing, send_sems.at[sem], recv_sems.at[sem], peer)
                cp.wait_send()
                cp.wait_recv()

    bufs = (*state["srcs"], *state["lands"])
    outs = pl.pallas_call(
        body, name=name,
        in_specs=[HBM_SPEC] * (2 * n) + [SEM_SPEC, SEM_SPEC, pl.BlockSpec(memory_space=pl.ANY)],
        out_specs=[HBM_SPEC] * (2 * n),
        out_shape=[pltpu.HBM(a.shape, a.dtype) for a in bufs],
        input_output_aliases={i: i for i in range(2 * n)},
        compiler_params=SPLIT_PARAMS,
    )(*bufs, *state["sems"], after)
    return outs[:n], outs[n:]


def gather_finish(shards, fulls, rowlikes, name):
    n = len(shards)

    def body(*refs):
        s_refs, f_refs = refs[:n], refs[2 * n:3 * n]
        send_sems, recv_sems = refs[3 * n:]
        x, y, c = _me()
        chip = 2 * x + y
        sib = (x, y, 1 - c)
        sends = []
        for a in range(n):
            own = _piece(f_refs[a], rowlikes[a], chip)
            cp = _remote(s_refs[a], own, send_sems.at[a, 0], recv_sems.at[a, 0], sib)
            cp.start()
            sends.append(cp)
            for k in (1, 2, 3):
                px, py = _peer_chip(x, y, k)
                slot = _piece_half(f_refs[a], rowlikes[a], 2 * px + py, c)
                cp = _remote(slot, slot, send_sems.at[a, k], recv_sems.at[a, k], sib)
                cp.start()
                sends.append(cp)
        for a in range(n):
            own = _piece(f_refs[a], rowlikes[a], chip)
            _remote(own, own, send_sems.at[a, 0], recv_sems.at[a, 0], sib).wait_recv()
            for k in (1, 2, 3):
                px, py = _peer_chip(x, y, k)
                slot = _piece_half(f_refs[a], rowlikes[a], 2 * px + py, 1 - c)
                _remote(slot, slot, send_sems.at[a, k], recv_sems.at[a, k], sib).wait_recv()
        for cp in sends:
            cp.wait_send()

    return pl.pallas_call(
        body, name=name, in_specs=[HBM_SPEC] * (2 * n), out_specs=[HBM_SPEC] * n,
        out_shape=[SDS(f.shape, f.dtype) for f in fulls],
        input_output_aliases={n + a: a for a in range(n)},
        scratch_shapes=[pltpu.SemaphoreType.DMA((n, 4)), pltpu.SemaphoreType.DMA((n, 4))],
    )(*shards, *fulls)


def _half_buffer_shape(gview, rowlike):
    return (N_CHIPS,) + gview.shape[2:] if rowlike else gview.shape[1:]


ROW_TILES = 2


def _piece_specs(rowlike, kh, ns, piece, half):
    tr = kh // ROW_TILES
    if rowlike:
        return (pl.BlockSpec((None, None, tr, ns), lambda *g: (piece(*g), half(*g), g[-2], 0)),
                pl.BlockSpec((None, tr, ns), lambda *g: (piece(*g), g[-2], 0)))
    return (pl.BlockSpec((None, tr, ns), lambda *g: (half(*g), g[-2], piece(*g))),
            pl.BlockSpec((tr, ns), lambda *g: (g[-2], piece(*g))))


def add_halves(gview, got, rowlike, place, name):
    kh, ns = (gview.shape[2], gview.shape[3]) if rowlike else (gview.shape[1], gview.shape[2] // N_CHIPS)

    def body(place_ref, g_ref, got_ref, o_ref):
        o_ref[...] = (g_ref[...] + got_ref[...]).astype(BF16)

    g_spec, h_spec = _piece_specs(rowlike, kh, ns, lambda j, r, p: j, lambda j, r, p: p[0])
    return pl.pallas_call(
        body, name=name,
        grid_spec=pltpu.PrefetchScalarGridSpec(num_scalar_prefetch=1, grid=(N_CHIPS, ROW_TILES),
                                               in_specs=[g_spec, h_spec], out_specs=h_spec),
        out_shape=SDS(got.shape, BF16),
        compiler_params=_cp(("parallel", "parallel")),
    )(place, gview, got)


def _slot_shape(q, rowlike):
    return (3,) + (q.shape[1:] if rowlike else (q.shape[0], q.shape[1] // N_CHIPS))


def sum_pieces(gview, got, recv, rowlike, place, name):
    kh, ns = recv.shape[1], recv.shape[2]
    tr = kh // ROW_TILES

    def body(place_ref, g_ref, got_ref, r_ref, o_ref):
        acc = g_ref[...] + got_ref[...]
        for k in range(3):
            acc = acc + r_ref[k].astype(F32)
        o_ref[...] = acc

    g_spec, h_spec = _piece_specs(rowlike, kh, ns, lambda z, r, p: p[1], lambda z, r, p: p[0])
    return pl.pallas_call(
        body, name=name,
        grid_spec=pltpu.PrefetchScalarGridSpec(
            num_scalar_prefetch=1, grid=(1, ROW_TILES),
            in_specs=[g_spec, h_spec, pl.BlockSpec((3, tr, ns), lambda z, r, p: (0, r, 0))],
            out_specs=pl.BlockSpec((None, tr, ns), lambda z, r, p: (p[0], r, 0))),
        out_shape=SDS((2, kh, ns), F32),
        compiler_params=_cp(("parallel", "parallel")),
    )(place, gview, got, recv)


def join_halves(halves, name):
    n = len(halves)

    def body(*refs):
        o_refs = refs[n:2 * n]
        send_sems, recv_sems = refs[2 * n:]
        x, y, c = _me()
        sib = (x, y, 1 - c)
        sends = []
        for a in range(n):
            cp = _remote(o_refs[a].at[c], o_refs[a].at[c], send_sems.at[a], recv_sems.at[a], sib)
            cp.start()
            sends.append(cp)
        for a in range(n):
            got = o_refs[a].at[1 - c]
            _remote(got, got, send_sems.at[a], recv_sems.at[a], sib).wait_recv()
        for cp in sends:
            cp.wait_send()

    return pl.pallas_call(
        body, name=name, in_specs=[HBM_SPEC] * n, out_specs=[HBM_SPEC] * n,
        out_shape=[SDS(h.shape, h.dtype) for h in halves],
        input_output_aliases={a: a for a in range(n)},
        scratch_shapes=[pltpu.SemaphoreType.DMA((n,)), pltpu.SemaphoreType.DMA((n,))],
    )(*halves)


N_DEV = 8


def sum_devices(s, slots, me):
    R, Wd = s.shape

    def body(me_ref, s_ref, slots_ref, out_ref):
        acc = None
        for d in range(N_DEV):
            mine = me_ref[0] == d
            term = jnp.where(mine, s_ref[...], slots_ref[jnp.where(mine, d ^ 1, d)])
            acc = term if acc is None else acc + term
        out_ref[...] = acc

    vmem = pl.BlockSpec(memory_space=pltpu.VMEM)
    return pl.pallas_call(
        body, name="sum_devices", in_specs=[pl.BlockSpec(memory_space=pltpu.SMEM), vmem, vmem], out_specs=vmem,
        out_shape=SDS((R, Wd), F32),
    )(me, s, slots)


BIG = {"ffn1_w_up": (D, 2 * FF, "col"), "ffn1_w_down": (FF, D, "row"), "w_in": (D, INW, "cm"),
       "w_attn_up": (AW, D, "row"), "w_pool_up": (PW, D, "col"), "w_out": (D, D, "row"),
       "ffn2_w_up": (D, 2 * FF, "col"), "ffn2_w_down": (FF, D, "row")}
GROUPS = (("ffn1_w_up", "ffn1_w_down"), ("w_in", "w_attn_up", "w_pool_up", "w_out"), ("ffn2_w_up", "ffn2_w_down"))
SMALL = ("ffn1_norm", "mix_norm", "ffn2_norm", "final_norm", "pool_scale", "sinks", "pool_w_mix")
SMALL_W = 128


def _rowlike(name):
    return BIG[name][2] != "col"


def _half_dims(name):
    k, n, kind = BIG[name]
    return (k // N_CHIPS // 2, n) if kind == "row" else (k // 2, n // N_CHIPS)


def shard_view(name, shard):
    return shard.reshape((2,) + _half_dims(name))


def full_from_view(name, fv):
    k, n, kind = BIG[name]
    if kind == "cm":
        return fv.reshape(N_CHIPS, k, n // N_CHIPS).transpose(1, 0, 2).reshape(k, n)
    return fv.reshape(k, n)


def grad_view(name, g):
    k, n, kind = BIG[name]
    kh, ns = _half_dims(name)
    if kind == "cm":
        return g.reshape(k, N_CHIPS, ns).transpose(1, 0, 2).reshape(N_CHIPS, 2, kh, ns)
    return g.reshape(_full_shape(jax.ShapeDtypeStruct((2, kh, ns), g.dtype), kind == "row"))


def pack_small(d):
    parts = []
    for name in SMALL:
        a = d[name].reshape(-1)
        pad = (-a.shape[0]) % SMALL_W
        parts.append(jnp.pad(a, (0, pad)).reshape(-1, SMALL_W))
    a = jnp.concatenate(parts, axis=0)
    return jnp.pad(a, ((0, (-a.shape[0]) % 8), (0, 0)))


def unpack_small(a, like):
    out, r0 = {}, 0
    for name in SMALL:
        size = int(np.prod(like[name].shape))
        rows = -(-size // SMALL_W)
        out[name] = a[r0:r0 + rows].reshape(-1)[:size].reshape(like[name].shape)
        r0 += rows
    return out


WEIGHTS = ("ffn1_norm", "ffn1_w_up", "ffn1_w_down", "mix_norm", "w_in", "sinks", "w_attn_up", "pool_w_mix",
           "pool_scale", "w_pool_up", "w_out", "ffn2_norm", "ffn2_w_up", "ffn2_w_down", "final_norm")


def kernel(x, ffn1_norm, ffn1_w_up, ffn1_w_down, mix_norm, w_in, sinks, w_attn_up, pool_w_mix, pool_scale, w_pool_up, w_out, ffn2_norm, ffn2_w_up, ffn2_w_down, final_norm, loss_target, m_ffn1_norm, m_ffn1_w_up, m_ffn1_w_down, m_mix_norm, m_w_in, m_sinks, m_w_attn_up, m_pool_w_mix, m_pool_scale, m_w_pool_up, m_w_out, m_ffn2_norm, m_ffn2_w_up, m_ffn2_w_down, m_final_norm, v_ffn1_norm, v_ffn1_w_up, v_ffn1_w_down, v_mix_norm, v_w_in, v_sinks, v_w_attn_up, v_pool_w_mix, v_pool_scale, v_w_pool_up, v_w_out, v_ffn2_norm, v_ffn2_w_up, v_ffn2_w_down, v_final_norm):
    given = dict(locals())
    w = {n: given[n] for n in WEIGHTS}
    m = {n: given["m_" + n] for n in WEIGHTS}
    v = {n: given["v_" + n] for n in WEIGHTS}
    cx, cy, cc = _me()
    place = jnp.stack([cc, 2 * cx + cy]).astype(jnp.int32)

    shards = {n: w[n][0] for n in BIG}
    sviews = {n: shard_view(n, shards[n].astype(BF16)) for n in BIG}
    grads, delta, new_m, new_v = {}, {}, {}, {}
    rowlikes = [[_rowlike(n) for n in names] for names in GROUPS]

    class Exchanges:
        def __init__(self):
            self.gathers, self.fulls, self.reductions, self.small = {}, {}, {}, None

        def _start_gather(self, group, after):
            sv = [sviews[n] for n in GROUPS[group]]
            rl = rowlikes[group]
            self.gathers[group] = exchange_start(sv, [_full_shape(s, r) for s, r in zip(sv, rl)], _gather_plan(rl),
                                                 after, f"gather_start_{group}")

        def weights(self, group, after):
            if not self.gathers:
                self._start_gather(0, sviews[GROUPS[0][0]])
            if group not in self.fulls:
                names, rl, state = GROUPS[group], rowlikes[group], self.gathers[group]
                sv, fulls = exchange_wait(state, _gather_plan(rl), state["token"] if after is None else after,
                                          f"gather_wait_{group}")
                fulls = gather_finish(sv, fulls, rl, f"gather_finish_{group}")
                self.fulls[group] = {n: full_from_view(n, f) for n, f in zip(names, fulls)}
                if group + 1 < len(GROUPS):
                    self._start_gather(group + 1, fulls[0])
            return self.fulls[group]

        def started(self):
            return self.gathers[max(self.gathers)]["token"]

        def grads(self, g):
            names = tuple(g)
            rl = [_rowlike(n) for n in names]
            gv = [grad_view(n, g[n]) for n in names]
            state = exchange_start(gv, [_half_buffer_shape(a, r) for a, r in zip(gv, rl)], _swap_plan(rl), gv[0],
                                   "swap_start_" + names[0], SIBLING)
            self.reductions[names] = state
            return state["token"]

        def advance(self, names, after):
            rl = [_rowlike(n) for n in names]
            gv, gots = exchange_wait(self.reductions[names], _swap_plan(rl), after, "swap_wait_" + names[0], SIBLING)
            qs = [add_halves(a, b, r, place, "add_halves_" + n) for a, b, r, n in zip(gv, gots, rl, names)]
            state = exchange_start(qs, [_slot_shape(q, r) for q, r in zip(qs, rl)], _all_to_all_plan(rl), qs[0],
                                   "all_to_all_start_" + names[0])
            self.reductions[names] = (gv, gots, state)
            return state["token"]

        def finish(self, names, after):
            rl = [_rowlike(n) for n in names]
            gv, gots, state = self.reductions.pop(names)
            _, recvs = exchange_wait(state, _all_to_all_plan(rl), after, "all_to_all_wait_" + names[0])
            halves = [sum_pieces(a, b, r, k, place, "sum_pieces_" + n)
                      for a, b, r, k, n in zip(gv, gots, recvs, rl, names)]
            for n, o in zip(names, join_halves(halves, "join_halves_" + names[0])):
                grads[n], delta[n], new_m[n], new_v[n] = adamw(shards[n], o.reshape(shards[n].shape), m[n][0], v[n][0],
                                                               "adamw_" + n)
            return new_v[names[-1]]

        def small_start(self, G, after):
            packed = pack_small({n: G[n] for n in SMALL})
            self.small = exchange_start([packed], [(N_DEV,) + packed.shape], _everyone_plan, after,
                                        "small_start", EVERYONE)

        def small_finish(self, after):
            (packed,), (slots,) = exchange_wait(self.small, _everyone_plan, after, "small_wait", EVERYONE)
            total = sum_devices(packed, slots, (4 * cx + 2 * cy + cc).astype(jnp.int32).reshape(1))
            g, ds, ms, vs = adamw(pack_small(w), total, pack_small(m), pack_small(v), "adamw_small")
            for d, packed_d in ((grads, g), (delta, ds), (new_m, ms), (new_v, vs)):
                d.update(unpack_small(packed_d, small_like))
            return vs

    small_like = {n: w[n] for n in SMALL}
    S = {n: w[n].reshape(1, -1) for n in ("ffn1_norm", "mix_norm", "ffn2_norm", "final_norm", "pool_scale", "sinks")}
    S["pool_w_mix"] = w["pool_w_mix"][0].astype(BF16)
    loss, dx = local_fwd_bwd(x[0], loss_target[0], S, Exchanges())
    loss = lax.psum(loss[0, 0], ("x", "y", "c"))

    def shaped(d, n):
        return d[n].reshape(w[n].shape)

    return (loss, dx[None], *[shaped(grads, n) for n in WEIGHTS], *[shaped(delta, n) for n in WEIGHTS],
            *[shaped(new_m, n) for n in WEIGHTS], *[shaped(new_v, n) for n in WEIGHTS])
```

```python
import numpy as np
import jax
import jax.numpy as jnp
from jax import lax
from jax.experimental import pallas as pl
from jax.experimental.pallas import tpu as pltpu

F32 = jnp.float32
BF16 = jnp.bfloat16
SDS = jax.ShapeDtypeStruct
MESH = pl.DeviceIdType.MESH

D = 1024
FF = 2816
NQ = 16
NKV = 2
HD = 64
GQ = NQ // NKV
AW = NQ * HD
KVW = NKV * HD
BLK = 128
PW = 512
PG = 128
POOL_WINDOWS = (2, 4, 8, 16)
HALO = 16
INW = AW + 2 * KVW + PW + 2 * D
C_KV = AW
C_Z = AW + 2 * KVW
C_G = C_Z + PW
EPS = 1e-6
FF_CHUNK = 256
FF_CHUNKS = tuple((c, FF_CHUNK) for c in range(0, FF, FF_CHUNK))
SLOPES = tuple(float(2.0 ** (-8.0 * h / NQ)) for h in range(1, NQ + 1))
SCALE = HD ** -0.5

LR, B1, B2, ADAM_EPS, WD, STEP = 0.001, 0.9, 0.999, 1e-08, 0.01, 10

VMEM_LIMIT = 56 * 1024 * 1024
N_CHIPS = 4

NT = (((1,), (1,)), ((), ()))
TN = (((0,), (0,)), ((), ()))


def _cp(sem=None, vmem=VMEM_LIMIT):
    return pltpu.CompilerParams(dimension_semantics=sem, vmem_limit_bytes=vmem)


def _const_spec(shape):
    nd = len(shape)
    return pl.BlockSpec(shape, lambda *_: (0,) * nd, pipeline_mode=pl.Buffered(1))


def _rstd(x):
    return lax.rsqrt(jnp.mean(x * x, axis=-1, keepdims=True) + EPS)


def _rms_bwd(dn, xhat, rstd, g):
    dxhat = dn * g
    return rstd * (dxhat - xhat * jnp.mean(dxhat * xhat, axis=-1, keepdims=True))


def _dot(a, b):
    return jnp.dot(a, b, preferred_element_type=F32)


def _dot_nt(a, b):
    return lax.dot_general(a, b, NT, preferred_element_type=F32)


def _dot_tn(a, b):
    return lax.dot_general(a, b, TN, preferred_element_type=F32)


def ffn_fwd(h, g, wup, wdn, name):
    T = h.shape[0]
    TM = 512

    def body(h_ref, g_ref, wup_ref, wdn_ref, out_ref, ab_ref, hid_ref):
        x = h_ref[...]
        n = (x * _rstd(x) * g_ref[...]).astype(BF16)
        for c0, w in FF_CHUNKS:
            a = _dot(n, wup_ref[:, c0:c0 + w])
            b = _dot(n, wup_ref[:, FF + c0:FF + c0 + w])
            ab_ref[:, c0:c0 + w] = a.astype(BF16)
            ab_ref[:, FF + c0:FF + c0 + w] = b.astype(BF16)
            hid_ref[:, c0:c0 + w] = (a * jax.nn.sigmoid(a) * b).astype(BF16)
        out_ref[...] = x + 0.5 * _dot(hid_ref[...], wdn_ref[...])

    return pl.pallas_call(
        body, name=name, grid=(T // TM,),
        in_specs=[pl.BlockSpec((TM, D), lambda i: (i, 0)), _const_spec((1, D)),
                  _const_spec((D, 2 * FF)), _const_spec((FF, D))],
        out_specs=[pl.BlockSpec((TM, D), lambda i: (i, 0)), pl.BlockSpec((TM, 2 * FF), lambda i: (i, 0)),
                   pl.BlockSpec((TM, FF), lambda i: (i, 0))],
        out_shape=[SDS((T, D), F32), SDS((T, 2 * FF), BF16), SDS((T, FF), BF16)],
        compiler_params=_cp(("arbitrary",)),
    )(h, g, wup, wdn)


def ffn_bwd_x(dh, h_in, ab, g, wup, wdn, name):
    T = dh.shape[0]
    TM = 256

    def body(dh_ref, h_ref, ab_ref, g_ref, wup_ref, wdn_ref, dhin_ref, dup_ref, n_ref, dg_ref):
        x = h_ref[...]
        g = g_ref[...]
        rstd = _rstd(x)
        xhat = x * rstd
        n_ref[...] = (xhat * g).astype(BF16)
        dh = dh_ref[...]
        dhh = (0.5 * dh).astype(BF16)
        for c0, w in FF_CHUNKS:
            dhid = _dot_nt(dhh, wdn_ref[c0:c0 + w, :])
            a = ab_ref[:, c0:c0 + w].astype(F32)
            b = ab_ref[:, FF + c0:FF + c0 + w].astype(F32)
            sig = jax.nn.sigmoid(a)
            s = a * sig
            dup_ref[:, c0:c0 + w] = (dhid * b * (sig * (1.0 + a * (1.0 - sig)))).astype(BF16)
            dup_ref[:, FF + c0:FF + c0 + w] = (dhid * s).astype(BF16)
        dn = _dot_nt(dup_ref[...], wup_ref[...])
        dhin_ref[...] = dh + _rms_bwd(dn, xhat, rstd, g)

        @pl.when(pl.program_id(0) == 0)
        def _():
            dg_ref[...] = jnp.zeros_like(dg_ref)

        dg_ref[...] += jnp.sum(dn * xhat, axis=0, keepdims=True)

    tile = lambda w: pl.BlockSpec((TM, w), lambda i: (i, 0))
    return pl.pallas_call(
        body, name=name, grid=(T // TM,),
        in_specs=[tile(D), tile(D), tile(2 * FF), _const_spec((1, D)), _const_spec((D, 2 * FF)), _const_spec((FF, D))],
        out_specs=[tile(D), tile(2 * FF), tile(D), pl.BlockSpec((1, D), lambda i: (0, 0))],
        out_shape=[SDS((T, D), F32), SDS((T, 2 * FF), BF16), SDS((T, D), BF16), SDS((1, D), F32)],
        compiler_params=_cp(("arbitrary",)),
    )(dh, h_in, ab, g, wup, wdn)


TOKEN_SPEC = pl.BlockSpec((8, 128), lambda *_: (0, 0))


def _token_operand(token):
    return ([], []) if token is None else ([TOKEN_SPEC], [token])


def matmul_tn(a, b, name, *, tm, tn, tt=1024, b_scale=None, after=None):
    T, M = a.shape
    N = b.shape[1]
    tt = min(tt, T)
    assert M % tm == 0 and N % tn == 0 and T % tt == 0
    token_spec, token_arg = _token_operand(after)

    def body(a_ref, b_ref, *rest):
        o_ref = rest[-1]

        @pl.when(pl.program_id(2) == 0)
        def _():
            o_ref[...] = jnp.zeros_like(o_ref)

        bv = b_ref[...]
        if b_scale is not None:
            bv = bv * b_scale
        o_ref[...] += _dot_tn(a_ref[...].astype(BF16), bv.astype(BF16))

    return pl.pallas_call(
        body, name=name, grid=(M // tm, N // tn, T // tt),
        in_specs=[pl.BlockSpec((tt, tm), lambda i, j, t: (t, i)), pl.BlockSpec((tt, tn), lambda i, j, t: (t, j))]
        + token_spec,
        out_specs=pl.BlockSpec((tm, tn), lambda i, j, t: (i, j)),
        out_shape=SDS((M, N), F32),
        compiler_params=_cp(("parallel", "parallel", "arbitrary")),
    )(a, b, *token_arg)


def mix_in_fwd(h1, g, win):
    T = h1.shape[0]
    TM = 512

    def body(h_ref, g_ref, w_ref, u_ref, q_ref, kv_ref, z_ref, gt_ref):
        x = h_ref[...]
        u = (x * _rstd(x) * g_ref[...]).astype(BF16)
        u_ref[...] = u
        for c in range(0, AW, 256):
            q_ref[:, c:c + 256] = _dot(u, w_ref[:, c:c + 256]).astype(BF16)
        kv_ref[...] = _dot(u, w_ref[:, C_KV:C_Z]).astype(BF16)
        for c in range(0, PW, 256):
            z_ref[:, c:c + 256] = _dot(u, w_ref[:, C_Z + c:C_Z + c + 256])
        for c in range(0, 2 * D, 256):
            gt_ref[:, c:c + 256] = _dot(u, w_ref[:, C_G + c:C_G + c + 256]).astype(BF16)

    tile = lambda w: pl.BlockSpec((TM, w), lambda i: (i, 0))
    return pl.pallas_call(
        body, name="mix_in_fwd", grid=(T // TM,),
        in_specs=[tile(D), _const_spec((1, D)), _const_spec((D, INW))],
        out_specs=[tile(D), tile(AW), tile(2 * KVW), tile(PW), tile(2 * D)],
        out_shape=[SDS((T, D), BF16), SDS((T, AW), BF16), SDS((T, 2 * KVW), BF16), SDS((T, PW), F32),
                   SDS((T, 2 * D), BF16)],
        compiler_params=_cp(("arbitrary",)),
    )(h1, g, win)


def mix_in_bwd(dq, dkv, dz, dgt, dh2, h1, g, win):
    T = h1.shape[0]
    TM = 512

    def body(dq_ref, dkv_ref, dz_ref, dgt_ref, dh2_ref, h_ref, g_ref, w_ref, dh1_ref, dg_ref):
        du = _dot_nt(dq_ref[...], w_ref[:, 0:AW])
        du += _dot_nt(dkv_ref[...], w_ref[:, C_KV:C_Z])
        du += _dot_nt(dz_ref[...], w_ref[:, C_Z:C_G])
        du += _dot_nt(dgt_ref[...], w_ref[:, C_G:INW])
        x = h_ref[...]
        g = g_ref[...]
        rstd = _rstd(x)
        xhat = x * rstd
        dh1_ref[...] = dh2_ref[...] + _rms_bwd(du, xhat, rstd, g)

        @pl.when(pl.program_id(0) == 0)
        def _():
            dg_ref[...] = jnp.zeros_like(dg_ref)

        dg_ref[...] += jnp.sum(du * xhat, axis=0, keepdims=True)

    tile = lambda w: pl.BlockSpec((TM, w), lambda i: (i, 0))
    return pl.pallas_call(
        body, name="mix_in_bwd", grid=(T // TM,),
        in_specs=[tile(AW), tile(2 * KVW), tile(PW), tile(2 * D), tile(D), tile(D), _const_spec((1, D)),
                  _const_spec((D, INW))],
        out_specs=[tile(D), pl.BlockSpec((1, D), lambda i: (0, 0))],
        out_shape=[SDS((T, D), F32), SDS((1, D), F32)],
        compiler_params=_cp(("arbitrary",)),
    )(dq, dkv, dz, dgt, dh2, h1, g, win)


PAIR = 2 * HD
NPAIR = GQ // 2


def _lo_lanes():
    return lax.broadcasted_iota(jnp.int32, (BLK, PAIR), 1) < HD


def _stack_heads(ref, kvh, scale=None):
    lo = _lo_lanes()
    parts = []
    for pr in range(NPAIR):
        t = ref[:, (kvh * NPAIR + pr) * PAIR:(kvh * NPAIR + pr + 1) * PAIR]
        if scale is not None:
            t = t * scale
        zero = jnp.zeros_like(t)
        parts += [jnp.where(lo, t, zero), jnp.where(lo, zero, t)]
    return jnp.concatenate(parts, axis=0)


def _kv_tiles(kvc_ref, kvp_ref, tile, kvh):
    lo = _lo_lanes()
    dup, left, right = [], [], []
    for ref in (kvp_ref, kvc_ref):
        t = ref[:, tile * PAIR:(tile + 1) * PAIR]
        r = pltpu.roll(t.astype(F32), HD, 1).astype(BF16)
        zero = jnp.zeros_like(t)
        a, b = (t, r) if kvh == 0 else (r, t)
        dup.append(jnp.where(lo, a, b))
        left.append(jnp.where(lo, a, zero))
        right.append(jnp.where(lo, zero, b))
    cat = lambda xs: jnp.concatenate(xs, axis=0)
    return cat(dup), cat(left), cat(right)


def _band_consts(first):
    row = lax.broadcasted_iota(jnp.int32, (BLK, BLK), 0)
    col = lax.broadcasted_iota(jnp.int32, (BLK, BLK), 1)
    upper = col > row
    dist = jnp.where(upper, row - col + BLK, row - col).astype(F32)
    pen = jnp.where(jnp.logical_and(upper, first), -jnp.inf, 0.0)
    return upper, dist, pen


def _split_band(upper, t):
    zero = jnp.zeros_like(t)
    return jnp.concatenate([jnp.where(upper, t, zero), jnp.where(upper, zero, t)], axis=1)


def attn_fwd(q, kv, sinks):
    T = q.shape[0]
    nb = T // BLK

    def body(sink_ref, q_ref, kvc_ref, kvp_ref, att_ref, lse_ref):
        upper, dist, pen = _band_consts(pl.program_id(0) == 0)
        for kvh in range(NKV):
            kdup, _, _ = _kv_tiles(kvc_ref, kvp_ref, 0, kvh)
            _, vleft, vright = _kv_tiles(kvc_ref, kvp_ref, 1, kvh)
            s_all = _dot_nt(_stack_heads(q_ref, kvh, SCALE), kdup)
            for pr in range(NPAIR):
                out = None
                for side, vpad in ((0, vleft), (1, vright)):
                    g = 2 * pr + side
                    hq = kvh * GQ + g
                    sink = sink_ref[0, hq]
                    rows = slice(g * BLK, (g + 1) * BLK)
                    s = jnp.where(upper, s_all[rows, 0:BLK], s_all[rows, BLK:2 * BLK]) - SLOPES[hq] * dist + pen
                    m = jnp.maximum(jnp.max(s, axis=-1, keepdims=True), sink)
                    p = jnp.exp(s - m)
                    l = jnp.sum(p, axis=-1, keepdims=True) + jnp.exp(sink - m)
                    lse_ref[:, hq:hq + 1] = m + jnp.log(l)
                    o = _dot(_split_band(upper, (p * (1.0 / l)).astype(BF16)), vpad)
                    out = o if out is None else out + o
                col0 = (kvh * NPAIR + pr) * PAIR
                att_ref[:, col0:col0 + PAIR] = out.astype(BF16)

    return pl.pallas_call(
        body, name="attn_fwd", grid=(nb,),
        in_specs=[pl.BlockSpec(memory_space=pltpu.SMEM),
                  pl.BlockSpec((BLK, AW), lambda i: (i, 0)),
                  pl.BlockSpec((BLK, 2 * KVW), lambda i: (i, 0)),
                  pl.BlockSpec((BLK, 2 * KVW), lambda i: (jnp.maximum(i - 1, 0), 0))],
        out_specs=[pl.BlockSpec((BLK, AW), lambda i: (i, 0)), pl.BlockSpec((BLK, NQ), lambda i: (i, 0))],
        out_shape=[SDS((T, AW), BF16), SDS((T, NQ), F32)],
        compiler_params=_cp(("arbitrary",)),
    )(sinks, q, kv, kv)


def attn_bwd(q, kv, datt, lse, sinks):
    T = q.shape[0]
    nb = T // BLK

    def body(sink_ref, q_ref, kvc_ref, kvp_ref, do_ref, lse_ref, dq_ref, dkv_ref, dsink_ref, carry_ref):
        i = pl.program_id(0)

        @pl.when(i == 0)
        def _():
            dsink_ref[...] = jnp.zeros_like(dsink_ref)
            carry_ref[...] = jnp.zeros_like(carry_ref)

        @pl.when(i < nb)
        def _():
            upper, dist, pen = _band_consts(i == 0)
            lo = _lo_lanes()
            dk_dup, dv_dup = [], []
            for kvh in range(NKV):
                kdup, kleft, kright = _kv_tiles(kvc_ref, kvp_ref, 0, kvh)
                vdup, _, _ = _kv_tiles(kvc_ref, kvp_ref, 1, kvh)
                qs = _stack_heads(q_ref, kvh, SCALE)
                dos = _stack_heads(do_ref, kvh)
                s_all = _dot_nt(qs, kdup)
                dp_all = _dot_nt(dos, vdup)
                ds_parts, p_parts = [], []
                for pr in range(NPAIR):
                    dq = None
                    for side, kpad in ((0, kleft), (1, kright)):
                        g = 2 * pr + side
                        hq = kvh * GQ + g
                        lse_h = lse_ref[:, hq:hq + 1]
                        rows = slice(g * BLK, (g + 1) * BLK)
                        s = jnp.where(upper, s_all[rows, 0:BLK], s_all[rows, BLK:2 * BLK]) - SLOPES[hq] * dist + pen
                        p = jnp.exp(s - lse_h)
                        dp = jnp.where(upper, dp_all[rows, 0:BLK], dp_all[rows, BLK:2 * BLK])
                        delta = jnp.sum(p * dp, axis=-1, keepdims=True)
                        dsink_ref[:, hq:hq + 1] += -jnp.sum(jnp.exp(sink_ref[0, hq] - lse_h) * delta, axis=0,
                                                            keepdims=True)
                        ds = _split_band(upper, (p * (dp - delta)).astype(BF16))
                        ds_parts.append(ds)
                        p_parts.append(_split_band(upper, p.astype(BF16)))
                        d = _dot(ds, kpad)
                        dq = d if dq is None else dq + d
                    col0 = (kvh * NPAIR + pr) * PAIR
                    dq_ref[:, col0:col0 + PAIR] = (dq * SCALE).astype(BF16)
                dkw = _dot_tn(jnp.concatenate(ds_parts, axis=0), qs)
                dvw = _dot_tn(jnp.concatenate(p_parts, axis=0), dos)
                dk_dup.append(dkw + pltpu.roll(dkw, HD, 1))
                dv_dup.append(dvw + pltpu.roll(dvw, HD, 1))
            dk = jnp.where(jnp.concatenate([lo, lo], axis=0), dk_dup[0], dk_dup[1])
            dv = jnp.where(jnp.concatenate([lo, lo], axis=0), dv_dup[0], dv_dup[1])
            dkv_ref[:, 0:PAIR] = (carry_ref[:, 0:PAIR] + dk[0:BLK]).astype(BF16)
            dkv_ref[:, PAIR:2 * PAIR] = (carry_ref[:, PAIR:2 * PAIR] + dv[0:BLK]).astype(BF16)
            carry_ref[:, 0:PAIR] = dk[BLK:2 * BLK]
            carry_ref[:, PAIR:2 * PAIR] = dv[BLK:2 * BLK]

        @pl.when(i == nb)
        def _():
            dkv_ref[...] = carry_ref[...].astype(BF16)

    cur = lambda i: (jnp.minimum(i, nb - 1), 0)
    prev = lambda i: (jnp.maximum(jnp.minimum(i, nb - 1) - 1, 0), 0)
    return pl.pallas_call(
        body, name="attn_bwd", grid=(nb + 1,),
        in_specs=[pl.BlockSpec(memory_space=pltpu.SMEM),
                  pl.BlockSpec((BLK, AW), cur), pl.BlockSpec((BLK, 2 * KVW), cur), pl.BlockSpec((BLK, 2 * KVW), prev),
                  pl.BlockSpec((BLK, AW), cur), pl.BlockSpec((BLK, NQ), cur)],
        out_specs=[pl.BlockSpec((BLK, AW), cur),
                   pl.BlockSpec((BLK, 2 * KVW), lambda i: (jnp.maximum(i - 1, 0), 0)),
                   pl.BlockSpec((1, NQ), lambda i: (0, 0))],
        out_shape=[SDS((T, AW), BF16), SDS((T, 2 * KVW), BF16), SDS((1, NQ), F32)],
        scratch_shapes=[pltpu.VMEM((BLK, 2 * KVW), F32)],
        compiler_params=_cp(("arbitrary",)),
    )(sinks, q, kv, kv, datt, lse)


def _inv_counts(t0, rows):
    t = (t0 + lax.broadcasted_iota(jnp.int32, (rows, 1), 0) + 1).astype(F32)
    return [1.0 / jnp.minimum(t, float(w)) for w in POOL_WINDOWS]


def pool_fwd(z, wmix, scale):
    T = z.shape[0]
    TM = 512
    L = TM + HALO

    def body(z_ref, halo_ref, wmix_ref, scale_ref, pooled_ref, mixs_ref):
        i = pl.program_id(0)
        halo = jnp.where(i > 0, halo_ref[...], 0.0)
        zt = z_ref[...]
        e = jnp.concatenate([halo, zt], axis=0)
        sums = []
        s = e
        for k in (1, 2, 4, 8):
            s = s + pltpu.roll(s, k, 0)
            sums.append(s)
        inv = _inv_counts(i * TM, TM)
        for gi in range(len(POOL_WINDOWS)):
            cols = slice(gi * PG, (gi + 1) * PG)
            pooled = (sums[gi][HALO:, cols] * inv[gi] - zt[:, cols]).astype(BF16)
            pooled_ref[:, cols] = pooled
            mixs_ref[:, cols] = (_dot(pooled, wmix_ref[gi]) * scale_ref[:, cols]).astype(BF16)

    return pl.pallas_call(
        body, name="pool_fwd", grid=(T // TM,),
        in_specs=[pl.BlockSpec((TM, PW), lambda i: (i, 0)),
                  pl.BlockSpec((HALO, PW), lambda i: (jnp.maximum(i * (TM // HALO) - 1, 0), 0)),
                  _const_spec((len(POOL_WINDOWS), PG, PG)), _const_spec((1, PW))],
        out_specs=[pl.BlockSpec((TM, PW), lambda i: (i, 0)), pl.BlockSpec((TM, PW), lambda i: (i, 0))],
        out_shape=[SDS((T, PW), BF16), SDS((T, PW), BF16)],
        compiler_params=_cp(("arbitrary",)),
    )(z, z, wmix, scale)


def pool_bwd(dmixs, pooled, wmix, scale):
    T = dmixs.shape[0]
    TM = 512
    L = TM + HALO
    nt = T // TM

    def body(dm_ref, halo_ref, pooled_ref, wmix_ref, scale_ref, dz_ref, dwmix_ref, dscale_ref):
        i = pl.program_id(0)

        @pl.when(i == 0)
        def _():
            dwmix_ref[...] = jnp.zeros_like(dwmix_ref)
            dscale_ref[...] = jnp.zeros_like(dscale_ref)

        halo = jnp.where(i < nt - 1, halo_ref[...], 0.0)
        dm = dm_ref[...]
        e = jnp.concatenate([dm, halo], axis=0)
        inv = _inv_counts(i * TM, L)
        for gi in range(len(POOL_WINDOWS)):
            cols = slice(gi * PG, (gi + 1) * PG)
            w = wmix_ref[gi]
            dmixed = (e[:, cols] * scale_ref[:, cols]).astype(BF16)
            dpooled = _dot_nt(dmixed, w)
            pooled = pooled_ref[:, cols]
            mixed = _dot(pooled, w)
            dscale_ref[:, cols] += jnp.sum(dm[:, cols] * mixed, axis=0, keepdims=True)
            dwmix_ref[gi] += _dot_tn(pooled, dmixed[:TM])
            s = dpooled * inv[gi]
            k = 1
            while k < POOL_WINDOWS[gi]:
                s = s + pltpu.roll(s, L - k, 0)
                k *= 2
            dz_ref[:, cols] = (s[:TM] - dpooled[:TM]).astype(BF16)

    return pl.pallas_call(
        body, name="pool_bwd", grid=(nt,),
        in_specs=[pl.BlockSpec((TM, PW), lambda i: (i, 0)),
                  pl.BlockSpec((HALO, PW), lambda i: (jnp.minimum((i + 1) * (TM // HALO), T // HALO - 1), 0)),
                  pl.BlockSpec((TM, PW), lambda i: (i, 0)),
                  _const_spec((len(POOL_WINDOWS), PG, PG)), _const_spec((1, PW))],
        out_specs=[pl.BlockSpec((TM, PW), lambda i: (i, 0)),
                   pl.BlockSpec((len(POOL_WINDOWS), PG, PG), lambda i: (0, 0, 0)),
                   pl.BlockSpec((1, PW), lambda i: (0, 0))],
        out_shape=[SDS((T, PW), BF16), SDS((len(POOL_WINDOWS), PG, PG), F32), SDS((1, PW), F32)],
        compiler_params=_cp(("arbitrary",)),
    )(dmixs, dmixs, pooled, wmix, scale)


def merge_fwd(att, mixs, gt, h1, wattn, wpool, wout):
    T = h1.shape[0]
    TM = 512

    def body(att_ref, mixs_ref, gt_ref, h_ref, wa_ref, wp_ref, wo_ref, h2_ref, mg_ref):
        a = _dot(att_ref[...], wa_ref[...])
        p = _dot(mixs_ref[...], wp_ref[...])
        merged = (jax.nn.sigmoid(gt_ref[:, 0:D].astype(F32)) * a + jax.nn.sigmoid(gt_ref[:, D:2 * D].astype(F32)) * p)
        mg = merged.astype(BF16)
        mg_ref[...] = mg
        h2_ref[...] = h_ref[...] + _dot(mg, wo_ref[...])

    tile = lambda w: pl.BlockSpec((TM, w), lambda i: (i, 0))
    return pl.pallas_call(
        body, name="merge_fwd", grid=(T // TM,),
        in_specs=[tile(AW), tile(PW), tile(2 * D), tile(D), _const_spec((AW, D)), _const_spec((PW, D)),
                  _const_spec((D, D))],
        out_specs=[tile(D), tile(D)],
        out_shape=[SDS((T, D), F32), SDS((T, D), BF16)],
        compiler_params=_cp(("arbitrary",)),
    )(att, mixs, gt, h1, wattn, wpool, wout)


def merge_bwd(dh2, att, mixs, gt, wattn, wpool, wout, after=None):
    T = dh2.shape[0]
    TM = 512
    token_spec, token_arg = _token_operand(after)

    def body(dh2_ref, att_ref, mixs_ref, gt_ref, wa_ref, wp_ref, wo_ref, *rest):
        datt_ref, dmixs_ref, dgt_ref, da_ref, dp_ref = rest[-5:]
        dm = _dot_nt(dh2_ref[...].astype(BF16), wo_ref[...])
        a = _dot(att_ref[...], wa_ref[...])
        p = _dot(mixs_ref[...], wp_ref[...])
        sa = jax.nn.sigmoid(gt_ref[:, 0:D].astype(F32))
        sp = jax.nn.sigmoid(gt_ref[:, D:2 * D].astype(F32))
        da = (dm * sa).astype(BF16)
        dp = (dm * sp).astype(BF16)
        da_ref[...] = da
        dp_ref[...] = dp
        dgt_ref[:, 0:D] = (dm * a * sa * (1.0 - sa)).astype(BF16)
        dgt_ref[:, D:2 * D] = (dm * p * sp * (1.0 - sp)).astype(BF16)
        datt_ref[...] = _dot_nt(da, wa_ref[...]).astype(BF16)
        dmixs_ref[...] = _dot_nt(dp, wp_ref[...])

    tile = lambda w: pl.BlockSpec((TM, w), lambda i: (i, 0))
    return pl.pallas_call(
        body, name="merge_bwd", grid=(T // TM,),
        in_specs=[tile(D), tile(AW), tile(PW), tile(2 * D), _const_spec((AW, D)), _const_spec((PW, D)),
                  _const_spec((D, D))] + token_spec,
        out_specs=[tile(AW), tile(PW), tile(2 * D), tile(D), tile(D)],
        out_shape=[SDS((T, AW), BF16), SDS((T, PW), F32), SDS((T, 2 * D), BF16), SDS((T, D), BF16),
                   SDS((T, D), BF16)],
        compiler_params=_cp(("arbitrary",)),
    )(dh2, att, mixs, gt, wattn, wpool, wout, *token_arg)


def loss_head(h3, target, g):
    T = h3.shape[0]
    TM = 512

    def body(h_ref, t_ref, g_ref, dh_ref, loss_ref, dg_ref):
        @pl.when(pl.program_id(0) == 0)
        def _():
            loss_ref[...] = jnp.zeros_like(loss_ref)
            dg_ref[...] = jnp.zeros_like(dg_ref)

        x = h_ref[...]
        g = g_ref[...]
        rstd = _rstd(x)
        xhat = x * rstd
        err = xhat * g - t_ref[...]
        loss_ref[...] += 0.5 * jnp.sum(jnp.mean(err * err, axis=-1, keepdims=True), axis=0, keepdims=True)
        dy = err * (1.0 / D)
        dg_ref[...] += jnp.sum(dy * xhat, axis=0, keepdims=True)
        dh_ref[...] = _rms_bwd(dy, xhat, rstd, g)

    tile = pl.BlockSpec((TM, D), lambda i: (i, 0))
    return pl.pallas_call(
        body, name="loss_head", grid=(T // TM,),
        in_specs=[tile, tile, _const_spec((1, D))],
        out_specs=[tile, pl.BlockSpec((1, 1), lambda i: (0, 0)), pl.BlockSpec((1, D), lambda i: (0, 0))],
        out_shape=[SDS((T, D), F32), SDS((1, 1), F32), SDS((1, D), F32)],
        compiler_params=_cp(("arbitrary",)),
    )(h3, target, g)


def adamw(w, g, m, v, name):
    R, C = w.shape
    tile_bytes = 2 * 1024 * 1024
    tr = R
    if R * C * 4 > tile_bytes:
        tr = next(cand for cand in (512, 256, 128, 64, 32, 16, 8) if R % cand == 0 and cand * C * 4 <= tile_bytes)
    c1 = 1.0 - B1 ** STEP
    c2 = 1.0 - B2 ** STEP

    def body(w_ref, g_ref, m_ref, v_ref, go_ref, d_ref, nm_ref, nv_ref):
        gv = g_ref[...]
        go_ref[...] = gv
        nm = B1 * m_ref[...] + (1.0 - B1) * gv
        nv = B2 * v_ref[...] + (1.0 - B2) * (gv * gv)
        nm_ref[...] = nm
        nv_ref[...] = nv
        d_ref[...] = -LR * ((nm / c1) / (jnp.sqrt(nv / c2) + ADAM_EPS) + WD * w_ref[...])

    spec = pl.BlockSpec((tr, C), lambda i: (i, 0))
    return pl.pallas_call(
        body, name=name, grid=(R // tr,),
        in_specs=[spec] * 4, out_specs=[spec] * 4, out_shape=[SDS((R, C), F32)] * 4,
        compiler_params=_cp(("parallel",)),
    )(w, g, m, v)


GROUP_FFN1, GROUP_MIX, GROUP_FFN2 = 0, 1, 2


def _behind(small, token):
    return small if token is None else small + token[0:1, 0:1]


def local_fwd_bwd(x, target, S, comm):
    W = comm.weights(GROUP_FFN1, None)
    h1, ab1, hid1 = ffn_fwd(x, _behind(S["ffn1_norm"], comm.started()), W["ffn1_w_up"], W["ffn1_w_down"], "ffn1_fwd")
    W = comm.weights(GROUP_MIX, h1)
    u, q, kv, z, gt = mix_in_fwd(h1, _behind(S["mix_norm"], comm.started()), W["w_in"])
    att, lse = attn_fwd(q, kv, S["sinks"])
    pooled, mixs = pool_fwd(z, S["pool_w_mix"], S["pool_scale"])
    h2, merged = merge_fwd(att, mixs, gt, h1, W["w_attn_up"], W["w_pool_up"], W["w_out"])
    W = comm.weights(GROUP_FFN2, h2)
    h3, ab2, hid2 = ffn_fwd(h2, S["ffn2_norm"], W["ffn2_w_up"], W["ffn2_w_down"], "ffn2_fwd")
    dh3, loss, g_final = loss_head(h3, target, S["final_norm"])

    G = {"final_norm": g_final}
    dh2, dup2, n2, G["ffn2_norm"] = ffn_bwd_x(dh3, h2, ab2, S["ffn2_norm"], W["ffn2_w_up"], W["ffn2_w_down"],
                                              "ffn2_bwd_x")
    ffn2 = ("ffn2_w_down", "ffn2_w_up")
    token = comm.grads({"ffn2_w_down": matmul_tn(hid2, dh3, "ffn2_dw_down", tm=1408, tn=512, tt=4096, b_scale=0.5),
                        "ffn2_w_up": matmul_tn(n2, dup2, "ffn2_dw_up", tm=D, tn=512, tt=4096)})

    W = comm.weights(GROUP_MIX, None)
    datt, dmixs, dgt, da, dp = merge_bwd(dh2, att, mixs, gt, W["w_attn_up"], W["w_pool_up"], W["w_out"], after=token)
    token = comm.advance(ffn2, datt)
    g_mix = {"w_out": matmul_tn(merged, dh2, "dw_out", tm=D, tn=D, after=token),
             "w_attn_up": matmul_tn(att, da, "dw_attn_up", tm=AW, tn=D),
             "w_pool_up": matmul_tn(mixs, dp, "dw_pool_up", tm=PW, tn=D)}
    dz, G["pool_w_mix"], G["pool_scale"] = pool_bwd(dmixs, pooled, S["pool_w_mix"], _behind(S["pool_scale"], token))
    dq, dkv, G["sinks"] = attn_bwd(q, kv, datt, lse, _behind(S["sinks"], token))
    dh1, G["mix_norm"] = mix_in_bwd(dq, dkv, dz, dgt, dh2, h1, S["mix_norm"], W["w_in"])
    g_mix["w_in"] = jnp.concatenate([
        matmul_tn(u, dq, "dw_in_q", tm=D, tn=AW),
        matmul_tn(u, dkv, "dw_in_kv", tm=D, tn=2 * KVW),
        matmul_tn(u, dz, "dw_in_z", tm=D, tn=PW),
        matmul_tn(u, dgt, "dw_in_g", tm=D, tn=512, tt=4096),
    ], axis=1)
    mix = tuple(g_mix)
    comm.finish(ffn2, dh1)
    token = comm.grads(g_mix)

    W = comm.weights(GROUP_FFN1, None)
    g_dn1 = matmul_tn(hid1, dh1, "ffn1_dw_down", tm=1408, tn=512, tt=4096, b_scale=0.5, after=token)
    token = comm.advance(mix, g_dn1)
    token2 = comm.grads({"ffn1_w_down": g_dn1})
    g1 = _behind(_behind(S["ffn1_norm"], token), token2)
    dx, dup1, n1, G["ffn1_norm"] = ffn_bwd_x(dh1, x, ab1, g1, W["ffn1_w_up"], W["ffn1_w_down"], "ffn1_bwd_x")
    token = comm.advance(("ffn1_w_down",), dx)
    token2 = comm.small_start(G, dx)
    g_up1 = matmul_tn(n1, dup1, "ffn1_dw_up", tm=D, tn=512, tt=4096,
                      after=None if token is None else token + token2)
    token = comm.grads({"ffn1_w_up": g_up1})
    done = comm.finish(mix, token)
    token = comm.advance(("ffn1_w_up",), done)
    done = comm.finish(("ffn1_w_down",), token)
    done = comm.small_finish(done)
    comm.finish(("ffn1_w_up",), done)
    return loss, dx


HBM_SPEC = pl.BlockSpec(memory_space=pltpu.HBM)


def _me():
    return lax.axis_index("x"), lax.axis_index("y"), lax.axis_index("c")


def _peer_chip(x, y, k):
    return x ^ (k >> 1), y ^ (k & 1)


def _piece_half(ref, rowlike, j, h):
    if rowlike:
        return ref.at[j, h]
    ns = ref.shape[-1] // N_CHIPS
    return ref.at[h, :, pl.ds(pl.multiple_of(j * ns, 128), ns)]


def _piece(ref, rowlike, j):
    if rowlike:
        return ref.at[j]
    ns = ref.shape[-1] // N_CHIPS
    return ref.at[:, :, pl.ds(pl.multiple_of(j * ns, 128), ns)]


def _full_shape(shard_view, rowlike):
    _, kh, ns = shard_view.shape
    return (N_CHIPS, 2, kh, ns) if rowlike else (2, kh, N_CHIPS * ns)


def _remote(src, dst, send_sem, recv_sem, dev):
    return pltpu.make_async_remote_copy(src, dst, send_sem, recv_sem, device_id=dev, device_id_type=MESH)


SEM_SPEC = pl.BlockSpec(memory_space=pltpu.SEMAPHORE)
SPLIT_PARAMS = pltpu.CompilerParams(has_side_effects=pltpu.SideEffectType.DATAFLOW_SIDE_EFFECTING)


def _gather_plan(rowlikes):
    def plan(s_refs, f_refs, a, k, x, y, c):
        px, py = _peer_chip(x, y, k)
        return (s_refs[a].at[c], _piece_half(f_refs[a], rowlikes[a], 2 * x + y, c),
                _piece_half(f_refs[a], rowlikes[a], 2 * px + py, c), (px, py, c))
    return plan


def _all_to_all_plan(rowlikes):
    def plan(q_refs, r_refs, a, k, x, y, c):
        px, py = _peer_chip(x, y, k)
        if rowlikes[a]:
            src = q_refs[a].at[2 * px + py]
        else:
            ns = q_refs[a].shape[-1] // N_CHIPS
            src = q_refs[a].at[:, pl.ds(pl.multiple_of((2 * px + py) * ns, 128), ns)]
        return src, r_refs[a].at[k - 1], r_refs[a].at[k - 1], (px, py, c)
    return plan


def _swap_plan(rowlikes):
    def plan(g_refs, got_refs, a, k, x, y, c):
        src = g_refs[a].at[:, 1 - c] if rowlikes[a] else g_refs[a].at[1 - c]
        return src, got_refs[a], got_refs[a], (x, y, 1 - c)
    return plan


def _everyone_plan(s_refs, slot_refs, a, k, x, y, c):
    px, py, pc = x ^ (k >> 2), y ^ ((k >> 1) & 1), c ^ (k & 1)
    return s_refs[a], slot_refs[a].at[4 * x + 2 * y + c], slot_refs[a].at[4 * px + 2 * py + pc], (px, py, pc)


CHIPS, SIBLING, EVERYONE = (1, 2, 3), (1,), tuple(range(1, 8))


def exchange_start(srcs, land_shapes, plan, after, name, peers=CHIPS):
    n = len(srcs)
    lands = [lax.empty(shape, s.dtype) for shape, s in zip(land_shapes, srcs)]

    behind = [] if after is None else [after]

    def body(*refs):
        s_refs, l_refs = refs[:n], refs[n:2 * n]
        send_sems, recv_sems = refs[2 * n + len(behind)], refs[2 * n + len(behind) + 1]
        token = refs[-1]
        x, y, c = _me()
        for a in range(n):
            for i, k in enumerate(peers):
                src, dst, _, peer = plan(s_refs, l_refs, a, k, x, y, c)
                sem = len(peers) * a + i
                _remote(src, dst, send_sems.at[sem], recv_sems.at[sem], peer).start()
        token[...] = jnp.zeros_like(token)

    n_sems = len(peers) * n
    outs = pl.pallas_call(
        body, name=name, in_specs=[HBM_SPEC] * (2 * n) + [pl.BlockSpec(memory_space=pl.ANY)] * len(behind),
        out_specs=[SEM_SPEC, SEM_SPEC] + [HBM_SPEC] * (2 * n) + [pl.BlockSpec(memory_space=pltpu.VMEM)],
        out_shape=[pltpu.SemaphoreType.DMA((n_sems,)), pltpu.SemaphoreType.DMA((n_sems,))]
        + [pltpu.HBM(a.shape, a.dtype) for a in (*srcs, *lands)] + [SDS((8, 128), F32)],
        input_output_aliases={i: 2 + i for i in range(2 * n)},
        compiler_params=SPLIT_PARAMS,
    )(*[pltpu.with_memory_space_constraint(a, pltpu.HBM) for a in (*srcs, *lands)], *behind)
    return {"sems": outs[:2], "srcs": outs[2:2 + n], "lands": outs[2 + n:2 + 2 * n], "token": outs[-1]}


def exchange_wait(state, plan, after, name, peers=CHIPS):
    n = len(state["srcs"])

    def body(*refs):
        s_refs, l_refs = refs[:n], refs[n:2 * n]
        send_sems, recv_sems = refs[2 * n], refs[2 * n + 1]
        x, y, c = _me()
        for a in range(n):
            for i, k in enumerate(peers):
                src, _, landing, peer = plan(s_refs, l_refs, a, k, x, y, c)
                sem = len(peers) * a + i
                cp = _remote(src, landing, send_sems.at[sem], recv_sems.at[sem], peer)
                cp.wait_send()
                cp.wait_recv()

    bufs = (*state["srcs"], *state["lands"])
    outs = pl.pallas_call(
        body, name=name,
        in_specs=[HBM_SPEC] * (2 * n) + [SEM_SPEC, SEM_SPEC, pl.BlockSpec(memory_space=pl.ANY)],
        out_specs=[HBM_SPEC] * (2 * n),
        out_shape=[pltpu.HBM(a.shape, a.dtype) for a in bufs],
        input_output_aliases={i: i for i in range(2 * n)},
        compiler_params=SPLIT_PARAMS,
    )(*bufs, *state["sems"], after)
    return outs[:n], outs[n:]


def gather_finish(shards, fulls, rowlikes, name):
    n = len(shards)

    def body(*refs):
        s_refs, f_refs = refs[:n], refs[2 * n:3 * n]
        send_sems, recv_sems = refs[3 * n:]
        x, y, c = _me()
        chip = 2 * x + y
        sib = (x, y, 1 - c)
        sends = []
        for a in range(n):
            own = _piece(f_refs[a], rowlikes[a], chip)
            cp = _remote(s_refs[a], own, send_sems.at[a, 0], recv_sems.at[a, 0], sib)
            cp.start()
            sends.append(cp)
            for k in (1, 2, 3):
                px, py = _peer_chip(x, y, k)
                slot = _piece_half(f_refs[a], rowlikes[a], 2 * px + py, c)
                cp = _remote(slot, slot, send_sems.at[a, k], recv_sems.at[a, k], sib)
                cp.start()
                sends.append(cp)
        for a in range(n):
            own = _piece(f_refs[a], rowlikes[a], chip)
            _remote(own, own, send_sems.at[a, 0], recv_sems.at[a, 0], sib).wait_recv()
            for k in (1, 2, 3):
                px, py = _peer_chip(x, y, k)
                slot = _piece_half(f_refs[a], rowlikes[a], 2 * px + py, 1 - c)
                _remote(slot, slot, send_sems.at[a, k], recv_sems.at[a, k], sib).wait_recv()
        for cp in sends:
            cp.wait_send()

    return pl.pallas_call(
        body, name=name, in_specs=[HBM_SPEC] * (2 * n), out_specs=[HBM_SPEC] * n,
        out_shape=[SDS(f.shape, f.dtype) for f in fulls],
        input_output_aliases={n + a: a for a in range(n)},
        scratch_shapes=[pltpu.SemaphoreType.DMA((n, 4)), pltpu.SemaphoreType.DMA((n, 4))],
    )(*shards, *fulls)


def _half_buffer_shape(gview, rowlike):
    return (N_CHIPS,) + gview.shape[2:] if rowlike else gview.shape[1:]


ROW_TILES = 2


def _piece_specs(rowlike, kh, ns, piece, half):
    tr = kh // ROW_TILES
    if rowlike:
        return (pl.BlockSpec((None, None, tr, ns), lambda *g: (piece(*g), half(*g), g[-2], 0)),
                pl.BlockSpec((None, tr, ns), lambda *g: (piece(*g), g[-2], 0)))
    return (pl.BlockSpec((None, tr, ns), lambda *g: (half(*g), g[-2], piece(*g))),
            pl.BlockSpec((tr, ns), lambda *g: (g[-2], piece(*g))))


def add_halves(gview, got, rowlike, place, name):
    kh, ns = (gview.shape[2], gview.shape[3]) if rowlike else (gview.shape[1], gview.shape[2] // N_CHIPS)

    def body(place_ref, g_ref, got_ref, o_ref):
        o_ref[...] = (g_ref[...] + got_ref[...]).astype(BF16)

    g_spec, h_spec = _piece_specs(rowlike, kh, ns, lambda j, r, p: j, lambda j, r, p: p[0])
    return pl.pallas_call(
        body, name=name,
        grid_spec=pltpu.PrefetchScalarGridSpec(num_scalar_prefetch=1, grid=(N_CHIPS, ROW_TILES),
                                               in_specs=[g_spec, h_spec], out_specs=h_spec),
        out_shape=SDS(got.shape, BF16),
        compiler_params=_cp(("parallel", "parallel")),
    )(place, gview, got)


def _slot_shape(q, rowlike):
    return (3,) + (q.shape[1:] if rowlike else (q.shape[0], q.shape[1] // N_CHIPS))


def sum_pieces(gview, got, recv, rowlike, place, name):
    kh, ns = recv.shape[1], recv.shape[2]
    tr = kh // ROW_TILES

    def body(place_ref, g_ref, got_ref, r_ref, o_ref):
        acc = g_ref[...] + got_ref[...]
        for k in range(3):
            acc = acc + r_ref[k].astype(F32)
        o_ref[...] = acc

    g_spec, h_spec = _piece_specs(rowlike, kh, ns, lambda z, r, p: p[1], lambda z, r, p: p[0])
    return pl.pallas_call(
        body, name=name,
        grid_spec=pltpu.PrefetchScalarGridSpec(
            num_scalar_prefetch=1, grid=(1, ROW_TILES),
            in_specs=[g_spec, h_spec, pl.BlockSpec((3, tr, ns), lambda z, r, p: (0, r, 0))],
            out_specs=pl.BlockSpec((None, tr, ns), lambda z, r, p: (p[0], r, 0))),
        out_shape=SDS((2, kh, ns), F32),
        compiler_params=_cp(("parallel", "parallel")),
    )(place, gview, got, recv)


def join_halves(halves, name):
    n = len(halves)

    def body(*refs):
        o_refs = refs[n:2 * n]
        send_sems, recv_sems = refs[2 * n:]
        x, y, c = _me()
        sib = (x, y, 1 - c)
        sends = []
        for a in range(n):
            cp = _remote(o_refs[a].at[c], o_refs[a].at[c], send_sems.at[a], recv_sems.at[a], sib)
            cp.start()
            sends.append(cp)
        for a in range(n):
            got = o_refs[a].at[1 - c]
            _remote(got, got, send_sems.at[a], recv_sems.at[a], sib).wait_recv()
        for cp in sends:
            cp.wait_send()

    return pl.pallas_call(
        body, name=name, in_specs=[HBM_SPEC] * n, out_specs=[HBM_SPEC] * n,
        out_shape=[SDS(h.shape, h.dtype) for h in halves],
        input_output_aliases={a: a for a in range(n)},
        scratch_shapes=[pltpu.SemaphoreType.DMA((n,)), pltpu.SemaphoreType.DMA((n,))],
    )(*halves)


N_DEV = 8


def sum_devices(s, slots, me):
    R, Wd = s.shape

    def body(me_ref, s_ref, slots_ref, out_ref):
        acc = None
        for d in range(N_DEV):
            mine = me_ref[0] == d
            term = jnp.where(mine, s_ref[...], slots_ref[jnp.where(mine, d ^ 1, d)])
            acc = term if acc is None else acc + term
        out_ref[...] = acc

    vmem = pl.BlockSpec(memory_space=pltpu.VMEM)
    return pl.pallas_call(
        body, name="sum_devices", in_specs=[pl.BlockSpec(memory_space=pltpu.SMEM), vmem, vmem], out_specs=vmem,
        out_shape=SDS((R, Wd), F32),
    )(me, s, slots)


BIG = {"ffn1_w_up": (D, 2 * FF, "col"), "ffn1_w_down": (FF, D, "row"), "w_in": (D, INW, "cm"),
       "w_attn_up": (AW, D, "row"), "w_pool_up": (PW, D, "col"), "w_out": (D, D, "row"),
       "ffn2_w_up": (D, 2 * FF, "col"), "ffn2_w_down": (FF, D, "row")}
GROUPS = (("ffn1_w_up", "ffn1_w_down"), ("w_in", "w_attn_up", "w_pool_up", "w_out"), ("ffn2_w_up", "ffn2_w_down"))
SMALL = ("ffn1_norm", "mix_norm", "ffn2_norm", "final_norm", "pool_scale", "sinks", "pool_w_mix")
SMALL_W = 128


def _rowlike(name):
    return BIG[name][2] != "col"


def _half_dims(name):
    k, n, kind = BIG[name]
    return (k // N_CHIPS // 2, n) if kind == "row" else (k // 2, n // N_CHIPS)


def shard_view(name, shard):
    return shard.reshape((2,) + _half_dims(name))


def full_from_view(name, fv):
    k, n, kind = BIG[name]
    if kind == "cm":
        return fv.reshape(N_CHIPS, k, n // N_CHIPS).transpose(1, 0, 2).reshape(k, n)
    return fv.reshape(k, n)


def grad_view(name, g):
    k, n, kind = BIG[name]
    kh, ns = _half_dims(name)
    if kind == "cm":
        return g.reshape(k, N_CHIPS, ns).transpose(1, 0, 2).reshape(N_CHIPS, 2, kh, ns)
    return g.reshape(_full_shape(jax.ShapeDtypeStruct((2, kh, ns), g.dtype), kind == "row"))


def pack_small(d):
    parts = []
    for name in SMALL:
        a = d[name].reshape(-1)
        pad = (-a.shape[0]) % SMALL_W
        parts.append(jnp.pad(a, (0, pad)).reshape(-1, SMALL_W))
    a = jnp.concatenate(parts, axis=0)
    return jnp.pad(a, ((0, (-a.shape[0]) % 8), (0, 0)))


def unpack_small(a, like):
    out, r0 = {}, 0
    for name in SMALL:
        size = int(np.prod(like[name].shape))
        rows = -(-size // SMALL_W)
        out[name] = a[r0:r0 + rows].reshape(-1)[:size].reshape(like[name].shape)
        r0 += rows
    return out


WEIGHTS = ("ffn1_norm", "ffn1_w_up", "ffn1_w_down", "mix_norm", "w_in", "sinks", "w_attn_up", "pool_w_mix",
           "pool_scale", "w_pool_up", "w_out", "ffn2_norm", "ffn2_w_up", "ffn2_w_down", "final_norm")


def kernel(x, ffn1_norm, ffn1_w_up, ffn1_w_down, mix_norm, w_in, sinks, w_attn_up, pool_w_mix, pool_scale, w_pool_up, w_out, ffn2_norm, ffn2_w_up, ffn2_w_down, final_norm, loss_target, m_ffn1_norm, m_ffn1_w_up, m_ffn1_w_down, m_mix_norm, m_w_in, m_sinks, m_w_attn_up, m_pool_w_mix, m_pool_scale, m_w_pool_up, m_w_out, m_ffn2_norm, m_ffn2_w_up, m_ffn2_w_down, m_final_norm, v_ffn1_norm, v_ffn1_w_up, v_ffn1_w_down, v_mix_norm, v_w_in, v_sinks, v_w_attn_up, v_pool_w_mix, v_pool_scale, v_w_pool_up, v_w_out, v_ffn2_norm, v_ffn2_w_up, v_ffn2_w_down, v_final_norm):
    given = dict(locals())
    w = {n: given[n] for n in WEIGHTS}
    m = {n: given["m_" + n] for n in WEIGHTS}
    v = {n: given["v_" + n] for n in WEIGHTS}
    cx, cy, cc = _me()
    place = jnp.stack([cc, 2 * cx + cy]).astype(jnp.int32)

    shards = {n: w[n][0] for n in BIG}
    sviews = {n: shard_view(n, shards[n].astype(BF16)) for n in BIG}
    grads, delta, new_m, new_v = {}, {}, {}, {}
    rowlikes = [[_rowlike(n) for n in names] for names in GROUPS]

    class Exchanges:
        def __init__(self):
            self.gathers, self.fulls, self.reductions, self.small = {}, {}, {}, None

        def _start_gather(self, group, after):
            sv = [sviews[n] for n in GROUPS[group]]
            rl = rowlikes[group]
            self.gathers[group] = exchange_start(sv, [_full_shape(s, r) for s, r in zip(sv, rl)], _gather_plan(rl),
                                                 after, f"gather_start_{group}")

        def weights(self, group, after):
            if not self.gathers:
                self._start_gather(0, None)
            if group not in self.fulls:
                names, rl, state = GROUPS[group], rowlikes[group], self.gathers[group]
                sv, fulls = exchange_wait(state, _gather_plan(rl), state["token"] if after is None else after,
                                          f"gather_wait_{group}")
                fulls = gather_finish(sv, fulls, rl, f"gather_finish_{group}")
                self.fulls[group] = {n: full_from_view(n, f) for n, f in zip(names, fulls)}
                if group + 1 < len(GROUPS):
                    self._start_gather(group + 1, fulls[0])
            return self.fulls[group]

        def started(self):
            return self.gathers[max(self.gathers)]["token"]

        def grads(self, g):
            names = tuple(g)
            rl = [_rowlike(n) for n in names]
            gv = [grad_view(n, g[n]) for n in names]
            state = exchange_start(gv, [_half_buffer_shape(a, r) for a, r in zip(gv, rl)], _swap_plan(rl), None,
                                   "swap_start_" + names[0], SIBLING)
            self.reductions[names] = state
            return state["token"]

        def advance(self, names, after):
            rl = [_rowlike(n) for n in names]
            gv, gots = exchange_wait(self.reductions[names], _swap_plan(rl), after, "swap_wait_" + names[0], SIBLING)
            qs = [add_halves(a, b, r, place, "add_halves_" + n) for a, b, r, n in zip(gv, gots, rl, names)]
            state = exchange_start(qs, [_slot_shape(q, r) for q, r in zip(qs, rl)], _all_to_all_plan(rl), None,
                                   "all_to_all_start_" + names[0])
            self.reductions[names] = (gv, gots, state)
            return state["token"]

        def finish(self, names, after):
            rl = [_rowlike(n) for n in names]
            gv, gots, state = self.reductions.pop(names)
            _, recvs = exchange_wait(state, _all_to_all_plan(rl), after, "all_to_all_wait_" + names[0])
            halves = [sum_pieces(a, b, r, k, place, "sum_pieces_" + n)
                      for a, b, r, k, n in zip(gv, gots, recvs, rl, names)]
            for n, o in zip(names, join_halves(halves, "join_halves_" + names[0])):
                grads[n], delta[n], new_m[n], new_v[n] = adamw(shards[n], o.reshape(shards[n].shape), m[n][0], v[n][0],
                                                               "adamw_" + n)
            return new_v[names[-1]]

        def small_start(self, G, after):
            packed = pack_small({n: G[n] for n in SMALL})
            self.small = exchange_start([packed], [(N_DEV,) + packed.shape], _everyone_plan, after,
                                        "small_start", EVERYONE)
            return self.small["token"]

        def small_finish(self, after):
            (packed,), (slots,) = exchange_wait(self.small, _everyone_plan, after, "small_wait", EVERYONE)
            total = sum_devices(packed, slots, (4 * cx + 2 * cy + cc).astype(jnp.int32).reshape(1))
            g, ds, ms, vs = adamw(pack_small(w), total, pack_small(m), pack_small(v), "adamw_small")
            for d, packed_d in ((grads, g), (delta, ds), (new_m, ms), (new_v, vs)):
                d.update(unpack_small(packed_d, small_like))
            return vs

    small_like = {n: w[n] for n in SMALL}
    S = {n: w[n].reshape(1, -1) for n in ("ffn1_norm", "mix_norm", "ffn2_norm", "final_norm", "pool_scale", "sinks")}
    S["pool_w_mix"] = w["pool_w_mix"][0].astype(BF16)
    loss, dx = local_fwd_bwd(x[0], loss_target[0], S, Exchanges())
    loss = lax.psum(loss[0, 0], ("x", "y", "c"))

    def shaped(d, n):
        return d[n].reshape(w[n].shape)

    return (loss, dx[None], *[shaped(grads, n) for n in WEIGHTS], *[shaped(delta, n) for n in WEIGHTS],
            *[shaped(new_m, n) for n in WEIGHTS], *[shaped(new_v, n) for n in WEIGHTS])
```

```python
import numpy as np
import jax
import jax.numpy as jnp
from jax import lax
from jax.experimental import pallas as pl
from jax.experimental.pallas import tpu as pltpu

F32 = jnp.float32
BF16 = jnp.bfloat16
SDS = jax.ShapeDtypeStruct
MESH = pl.DeviceIdType.MESH

D = 1024
FF = 2816
NQ = 16
NKV = 2
HD = 64
GQ = NQ // NKV
AW = NQ * HD
KVW = NKV * HD
BLK = 128
PW = 512
PG = 128
POOL_WINDOWS = (2, 4, 8, 16)
HALO = 16
INW = AW + 2 * KVW + PW + 2 * D
C_KV = AW
C_Z = AW + 2 * KVW
C_G = C_Z + PW
EPS = 1e-6
FF_CHUNK = 256
FF_CHUNKS = tuple((c, FF_CHUNK) for c in range(0, FF, FF_CHUNK))
SLOPES = tuple(float(2.0 ** (-8.0 * h / NQ)) for h in range(1, NQ + 1))
SCALE = HD ** -0.5

LR, B1, B2, ADAM_EPS, WD, STEP = 0.001, 0.9, 0.999, 1e-08, 0.01, 10

VMEM_LIMIT = 56 * 1024 * 1024
N_CHIPS = 4

NT = (((1,), (1,)), ((), ()))
TN = (((0,), (0,)), ((), ()))


def _cp(sem=None, vmem=VMEM_LIMIT):
    return pltpu.CompilerParams(dimension_semantics=sem, vmem_limit_bytes=vmem)


def _const_spec(shape):
    nd = len(shape)
    return pl.BlockSpec(shape, lambda *_: (0,) * nd, pipeline_mode=pl.Buffered(1))


def _rstd(x):
    return lax.rsqrt(jnp.mean(x * x, axis=-1, keepdims=True) + EPS)


def _rms_bwd(dn, xhat, rstd, g):
    dxhat = dn * g
    return rstd * (dxhat - xhat * jnp.mean(dxhat * xhat, axis=-1, keepdims=True))


def _dot(a, b):
    return jnp.dot(a, b, preferred_element_type=F32)


def _dot_nt(a, b):
    return lax.dot_general(a, b, NT, preferred_element_type=F32)


def _dot_tn(a, b):
    return lax.dot_general(a, b, TN, preferred_element_type=F32)


def ffn_fwd(h, g, wup, wdn, name):
    T = h.shape[0]
    TM = 512

    def body(h_ref, g_ref, wup_ref, wdn_ref, out_ref, ab_ref, hid_ref):
        x = h_ref[...]
        n = (x * _rstd(x) * g_ref[...]).astype(BF16)
        for c0, w in FF_CHUNKS:
            a = _dot(n, wup_ref[:, c0:c0 + w])
            b = _dot(n, wup_ref[:, FF + c0:FF + c0 + w])
            ab_ref[:, c0:c0 + w] = a.astype(BF16)
            ab_ref[:, FF + c0:FF + c0 + w] = b.astype(BF16)
            hid_ref[:, c0:c0 + w] = (a * jax.nn.sigmoid(a) * b).astype(BF16)
        out_ref[...] = x + 0.5 * _dot(hid_ref[...], wdn_ref[...])

    return pl.pallas_call(
        body, name=name, grid=(T // TM,),
        in_specs=[pl.BlockSpec((TM, D), lambda i: (i, 0)), _const_spec((1, D)),
                  _const_spec((D, 2 * FF)), _const_spec((FF, D))],
        out_specs=[pl.BlockSpec((TM, D), lambda i: (i, 0)), pl.BlockSpec((TM, 2 * FF), lambda i: (i, 0)),
                   pl.BlockSpec((TM, FF), lambda i: (i, 0))],
        out_shape=[SDS((T, D), F32), SDS((T, 2 * FF), BF16), SDS((T, FF), BF16)],
        compiler_params=_cp(("arbitrary",)),
    )(h, g, wup, wdn)


def ffn_bwd_x(dh, h_in, ab, g, wup, wdn, name):
    T = dh.shape[0]
    TM = 256

    def body(dh_ref, h_ref, ab_ref, g_ref, wup_ref, wdn_ref, dhin_ref, dup_ref, n_ref, dg_ref):
        x = h_ref[...]
        g = g_ref[...]
        rstd = _rstd(x)
        xhat = x * rstd
        n_ref[...] = (xhat * g).astype(BF16)
        dh = dh_ref[...]
        dhh = (0.5 * dh).astype(BF16)
        for c0, w in FF_CHUNKS:
            dhid = _dot_nt(dhh, wdn_ref[c0:c0 + w, :])
            a = ab_ref[:, c0:c0 + w].astype(F32)
            b = ab_ref[:, FF + c0:FF + c0 + w].astype(F32)
            sig = jax.nn.sigmoid(a)
            s = a * sig
            dup_ref[:, c0:c0 + w] = (dhid * b * (sig * (1.0 + a * (1.0 - sig)))).astype(BF16)
            dup_ref[:, FF + c0:FF + c0 + w] = (dhid * s).astype(BF16)
        dn = _dot_nt(dup_ref[...], wup_ref[...])
        dhin_ref[...] = dh + _rms_bwd(dn, xhat, rstd, g)

        @pl.when(pl.program_id(0) == 0)
        def _():
            dg_ref[...] = jnp.zeros_like(dg_ref)

        dg_ref[...] += jnp.sum(dn * xhat, axis=0, keepdims=True)

    tile = lambda w: pl.BlockSpec((TM, w), lambda i: (i, 0))
    return pl.pallas_call(
        body, name=name, grid=(T // TM,),
        in_specs=[tile(D), tile(D), tile(2 * FF), _const_spec((1, D)), _const_spec((D, 2 * FF)), _const_spec((FF, D))],
        out_specs=[tile(D), tile(2 * FF), tile(D), pl.BlockSpec((1, D), lambda i: (0, 0))],
        out_shape=[SDS((T, D), F32), SDS((T, 2 * FF), BF16), SDS((T, D), BF16), SDS((1, D), F32)],
        compiler_params=_cp(("arbitrary",)),
    )(dh, h_in, ab, g, wup, wdn)


TOKEN_SPEC = pl.BlockSpec((8, 128), lambda *_: (0, 0))


def _token_operand(token):
    return ([], []) if token is None else ([TOKEN_SPEC], [token])


def matmul_tn(a, b, name, *, tm, tn, tt=1024, b_scale=None, after=None):
    T, M = a.shape
    N = b.shape[1]
    tt = min(tt, T)
    assert M % tm == 0 and N % tn == 0 and T % tt == 0
    token_spec, token_arg = _token_operand(after)

    def body(a_ref, b_ref, *rest):
        o_ref = rest[-1]

        @pl.when(pl.program_id(2) == 0)
        def _():
            o_ref[...] = jnp.zeros_like(o_ref)

        bv = b_ref[...]
        if b_scale is not None:
            bv = bv * b_scale
        o_ref[...] += _dot_tn(a_ref[...].astype(BF16), bv.astype(BF16))

    return pl.pallas_call(
        body, name=name, grid=(M // tm, N // tn, T // tt),
        in_specs=[pl.BlockSpec((tt, tm), lambda i, j, t: (t, i)), pl.BlockSpec((tt, tn), lambda i, j, t: (t, j))]
        + token_spec,
        out_specs=pl.BlockSpec((tm, tn), lambda i, j, t: (i, j)),
        out_shape=SDS((M, N), F32),
        compiler_params=_cp(("parallel", "parallel", "arbitrary")),
    )(a, b, *token_arg)


def mix_in_fwd(h1, g, win_t):
    T = h1.shape[0]
    TM = 512

    def body(h_ref, g_ref, w_ref, u_ref, q_ref, kv_ref, z_ref, gt_ref):
        x = h_ref[...]
        u = (x * _rstd(x) * g_ref[...]).astype(BF16)
        u_ref[...] = u
        for c in range(0, AW, 256):
            q_ref[:, c:c + 256] = _dot_nt(u, w_ref[c:c + 256, :]).astype(BF16)
        kv_ref[...] = _dot_nt(u, w_ref[C_KV:C_Z, :]).astype(BF16)
        for c in range(0, PW, 256):
            z_ref[:, c:c + 256] = _dot_nt(u, w_ref[C_Z + c:C_Z + c + 256, :])
        for c in range(0, 2 * D, 256):
            gt_ref[:, c:c + 256] = _dot_nt(u, w_ref[C_G + c:C_G + c + 256, :]).astype(BF16)

    tile = lambda w: pl.BlockSpec((TM, w), lambda i: (i, 0))
    return pl.pallas_call(
        body, name="mix_in_fwd", grid=(T // TM,),
        in_specs=[tile(D), _const_spec((1, D)), _const_spec((INW, D))],
        out_specs=[tile(D), tile(AW), tile(2 * KVW), tile(PW), tile(2 * D)],
        out_shape=[SDS((T, D), BF16), SDS((T, AW), BF16), SDS((T, 2 * KVW), BF16), SDS((T, PW), F32),
                   SDS((T, 2 * D), BF16)],
        compiler_params=_cp(("arbitrary",)),
    )(h1, g, win_t)


def mix_in_bwd(dq, dkv, dz, dgt, dh2, h1, g, win_t):
    T = h1.shape[0]
    TM = 512

    def body(dq_ref, dkv_ref, dz_ref, dgt_ref, dh2_ref, h_ref, g_ref, w_ref, dh1_ref, dg_ref):
        du = _dot(dq_ref[...], w_ref[0:AW, :])
        du += _dot(dkv_ref[...], w_ref[C_KV:C_Z, :])
        du += _dot(dz_ref[...], w_ref[C_Z:C_G, :])
        du += _dot(dgt_ref[...], w_ref[C_G:INW, :])
        x = h_ref[...]
        g = g_ref[...]
        rstd = _rstd(x)
        xhat = x * rstd
        dh1_ref[...] = dh2_ref[...] + _rms_bwd(du, xhat, rstd, g)

        @pl.when(pl.program_id(0) == 0)
        def _():
            dg_ref[...] = jnp.zeros_like(dg_ref)

        dg_ref[...] += jnp.sum(du * xhat, axis=0, keepdims=True)

    tile = lambda w: pl.BlockSpec((TM, w), lambda i: (i, 0))
    return pl.pallas_call(
        body, name="mix_in_bwd", grid=(T // TM,),
        in_specs=[tile(AW), tile(2 * KVW), tile(PW), tile(2 * D), tile(D), tile(D), _const_spec((1, D)),
                  _const_spec((INW, D))],
        out_specs=[tile(D), pl.BlockSpec((1, D), lambda i: (0, 0))],
        out_shape=[SDS((T, D), F32), SDS((1, D), F32)],
        compiler_params=_cp(("arbitrary",)),
    )(dq, dkv, dz, dgt, dh2, h1, g, win_t)


PAIR = 2 * HD
NPAIR = GQ // 2


def _lo_lanes():
    return lax.broadcasted_iota(jnp.int32, (BLK, PAIR), 1) < HD


def _stack_heads(ref, kvh, scale=None):
    lo = _lo_lanes()
    parts = []
    for pr in range(NPAIR):
        t = ref[:, (kvh * NPAIR + pr) * PAIR:(kvh * NPAIR + pr + 1) * PAIR]
        if scale is not None:
            t = t * scale
        zero = jnp.zeros_like(t)
        parts += [jnp.where(lo, t, zero), jnp.where(lo, zero, t)]
    return jnp.concatenate(parts, axis=0)


def _kv_tiles(kvc_ref, kvp_ref, tile, kvh):
    lo = _lo_lanes()
    dup, left, right = [], [], []
    for ref in (kvp_ref, kvc_ref):
        t = ref[:, tile * PAIR:(tile + 1) * PAIR]
        r = pltpu.roll(t.astype(F32), HD, 1).astype(BF16)
        zero = jnp.zeros_like(t)
        a, b = (t, r) if kvh == 0 else (r, t)
        dup.append(jnp.where(lo, a, b))
        left.append(jnp.where(lo, a, zero))
        right.append(jnp.where(lo, zero, b))
    cat = lambda xs: jnp.concatenate(xs, axis=0)
    return cat(dup), cat(left), cat(right)


def _band_consts(first):
    row = lax.broadcasted_iota(jnp.int32, (BLK, BLK), 0)
    col = lax.broadcasted_iota(jnp.int32, (BLK, BLK), 1)
    upper = col > row
    dist = jnp.where(upper, row - col + BLK, row - col).astype(F32)
    pen = jnp.where(jnp.logical_and(upper, first), -jnp.inf, 0.0)
    return upper, dist, pen


def _split_band(upper, t):
    zero = jnp.zeros_like(t)
    return jnp.concatenate([jnp.where(upper, t, zero), jnp.where(upper, zero, t)], axis=1)


def attn_fwd(q, kv, sinks):
    T = q.shape[0]
    nb = T // BLK

    def body(sink_ref, q_ref, kvc_ref, kvp_ref, att_ref, lse_ref):
        upper, dist, pen = _band_consts(pl.program_id(0) == 0)
        for kvh in range(NKV):
            kdup, _, _ = _kv_tiles(kvc_ref, kvp_ref, 0, kvh)
            _, vleft, vright = _kv_tiles(kvc_ref, kvp_ref, 1, kvh)
            s_all = _dot_nt(_stack_heads(q_ref, kvh, SCALE), kdup)
            for pr in range(NPAIR):
                out = None
                for side, vpad in ((0, vleft), (1, vright)):
                    g = 2 * pr + side
                    hq = kvh * GQ + g
                    sink = sink_ref[0, hq]
                    rows = slice(g * BLK, (g + 1) * BLK)
                    s = jnp.where(upper, s_all[rows, 0:BLK], s_all[rows, BLK:2 * BLK]) - SLOPES[hq] * dist + pen
                    m = jnp.maximum(jnp.max(s, axis=-1, keepdims=True), sink)
                    p = jnp.exp(s - m)
                    l = jnp.sum(p, axis=-1, keepdims=True) + jnp.exp(sink - m)
                    lse_ref[:, hq:hq + 1] = m + jnp.log(l)
                    o = _dot(_split_band(upper, (p * (1.0 / l)).astype(BF16)), vpad)
                    out = o if out is None else out + o
                col0 = (kvh * NPAIR + pr) * PAIR
                att_ref[:, col0:col0 + PAIR] = out.astype(BF16)

    return pl.pallas_call(
        body, name="attn_fwd", grid=(nb,),
        in_specs=[pl.BlockSpec(memory_space=pltpu.SMEM),
                  pl.BlockSpec((BLK, AW), lambda i: (i, 0)),
                  pl.BlockSpec((BLK, 2 * KVW), lambda i: (i, 0)),
                  pl.BlockSpec((BLK, 2 * KVW), lambda i: (jnp.maximum(i - 1, 0), 0))],
        out_specs=[pl.BlockSpec((BLK, AW), lambda i: (i, 0)), pl.BlockSpec((BLK, NQ), lambda i: (i, 0))],
        out_shape=[SDS((T, AW), BF16), SDS((T, NQ), F32)],
        compiler_params=_cp(("arbitrary",)),
    )(sinks, q, kv, kv)


def attn_bwd(q, kv, datt, lse, sinks):
    T = q.shape[0]
    nb = T // BLK

    def body(sink_ref, q_ref, kvc_ref, kvp_ref, do_ref, lse_ref, dq_ref, dkv_ref, dsink_ref, carry_ref):
        i = pl.program_id(0)

        @pl.when(i == 0)
        def _():
            dsink_ref[...] = jnp.zeros_like(dsink_ref)
            carry_ref[...] = jnp.zeros_like(carry_ref)

        @pl.when(i < nb)
        def _():
            upper, dist, pen = _band_consts(i == 0)
            lo = _lo_lanes()
            dk_dup, dv_dup = [], []
            for kvh in range(NKV):
                kdup, kleft, kright = _kv_tiles(kvc_ref, kvp_ref, 0, kvh)
                vdup, _, _ = _kv_tiles(kvc_ref, kvp_ref, 1, kvh)
                qs = _stack_heads(q_ref, kvh, SCALE)
                dos = _stack_heads(do_ref, kvh)
                s_all = _dot_nt(qs, kdup)
                dp_all = _dot_nt(dos, vdup)
                ds_parts, p_parts = [], []
                for pr in range(NPAIR):
                    dq = None
                    for side, kpad in ((0, kleft), (1, kright)):
                        g = 2 * pr + side
                        hq = kvh * GQ + g
                        lse_h = lse_ref[:, hq:hq + 1]
                        rows = slice(g * BLK, (g + 1) * BLK)
                        s = jnp.where(upper, s_all[rows, 0:BLK], s_all[rows, BLK:2 * BLK]) - SLOPES[hq] * dist + pen
                        p = jnp.exp(s - lse_h)
                        dp = jnp.where(upper, dp_all[rows, 0:BLK], dp_all[rows, BLK:2 * BLK])
                        delta = jnp.sum(p * dp, axis=-1, keepdims=True)
                        dsink_ref[:, hq:hq + 1] += -jnp.sum(jnp.exp(sink_ref[0, hq] - lse_h) * delta, axis=0,
                                                            keepdims=True)
                        ds = _split_band(upper, (p * (dp - delta)).astype(BF16))
                        ds_parts.append(ds)
                        p_parts.append(_split_band(upper, p.astype(BF16)))
                        d = _dot(ds, kpad)
                        dq = d if dq is None else dq + d
                    col0 = (kvh * NPAIR + pr) * PAIR
                    dq_ref[:, col0:col0 + PAIR] = (dq * SCALE).astype(BF16)
                dkw = _dot_tn(jnp.concatenate(ds_parts, axis=0), qs)
                dvw = _dot_tn(jnp.concatenate(p_parts, axis=0), dos)
                dk_dup.append(dkw + pltpu.roll(dkw, HD, 1))
                dv_dup.append(dvw + pltpu.roll(dvw, HD, 1))
            dk = jnp.where(jnp.concatenate([lo, lo], axis=0), dk_dup[0], dk_dup[1])
            dv = jnp.where(jnp.concatenate([lo, lo], axis=0), dv_dup[0], dv_dup[1])
            dkv_ref[:, 0:PAIR] = (carry_ref[:, 0:PAIR] + dk[0:BLK]).astype(BF16)
            dkv_ref[:, PAIR:2 * PAIR] = (carry_ref[:, PAIR:2 * PAIR] + dv[0:BLK]).astype(BF16)
            carry_ref[:, 0:PAIR] = dk[BLK:2 * BLK]
            carry_ref[:, PAIR:2 * PAIR] = dv[BLK:2 * BLK]

        @pl.when(i == nb)
        def _():
            dkv_ref[...] = carry_ref[...].astype(BF16)

    cur = lambda i: (jnp.minimum(i, nb - 1), 0)
    prev = lambda i: (jnp.maximum(jnp.minimum(i, nb - 1) - 1, 0), 0)
    return pl.pallas_call(
        body, name="attn_bwd", grid=(nb + 1,),
        in_specs=[pl.BlockSpec(memory_space=pltpu.SMEM),
                  pl.BlockSpec((BLK, AW), cur), pl.BlockSpec((BLK, 2 * KVW), cur), pl.BlockSpec((BLK, 2 * KVW), prev),
                  pl.BlockSpec((BLK, AW), cur), pl.BlockSpec((BLK, NQ), cur)],
        out_specs=[pl.BlockSpec((BLK, AW), cur),
                   pl.BlockSpec((BLK, 2 * KVW), lambda i: (jnp.maximum(i - 1, 0), 0)),
                   pl.BlockSpec((1, NQ), lambda i: (0, 0))],
        out_shape=[SDS((T, AW), BF16), SDS((T, 2 * KVW), BF16), SDS((1, NQ), F32)],
        scratch_shapes=[pltpu.VMEM((BLK, 2 * KVW), F32)],
        compiler_params=_cp(("arbitrary",)),
    )(sinks, q, kv, kv, datt, lse)


def _inv_counts(t0, rows):
    t = (t0 + lax.broadcasted_iota(jnp.int32, (rows, 1), 0) + 1).astype(F32)
    return [1.0 / jnp.minimum(t, float(w)) for w in POOL_WINDOWS]


def pool_fwd(z, wmix, scale):
    T = z.shape[0]
    TM = 512
    L = TM + HALO

    def body(z_ref, halo_ref, wmix_ref, scale_ref, pooled_ref, mixs_ref):
        i = pl.program_id(0)
        halo = jnp.where(i > 0, halo_ref[...], 0.0)
        zt = z_ref[...]
        e = jnp.concatenate([halo, zt], axis=0)
        sums = []
        s = e
        for k in (1, 2, 4, 8):
            s = s + pltpu.roll(s, k, 0)
            sums.append(s)
        inv = _inv_counts(i * TM, TM)
        for gi in range(len(POOL_WINDOWS)):
            cols = slice(gi * PG, (gi + 1) * PG)
            pooled = (sums[gi][HALO:, cols] * inv[gi] - zt[:, cols]).astype(BF16)
            pooled_ref[:, cols] = pooled
            mixs_ref[:, cols] = (_dot(pooled, wmix_ref[gi]) * scale_ref[:, cols]).astype(BF16)

    return pl.pallas_call(
        body, name="pool_fwd", grid=(T // TM,),
        in_specs=[pl.BlockSpec((TM, PW), lambda i: (i, 0)),
                  pl.BlockSpec((HALO, PW), lambda i: (jnp.maximum(i * (TM // HALO) - 1, 0), 0)),
                  _const_spec((len(POOL_WINDOWS), PG, PG)), _const_spec((1, PW))],
        out_specs=[pl.BlockSpec((TM, PW), lambda i: (i, 0)), pl.BlockSpec((TM, PW), lambda i: (i, 0))],
        out_shape=[SDS((T, PW), BF16), SDS((T, PW), BF16)],
        compiler_params=_cp(("arbitrary",)),
    )(z, z, wmix, scale)


def pool_bwd(dmixs, pooled, wmix, scale):
    T = dmixs.shape[0]
    TM = 512
    L = TM + HALO
    nt = T // TM

    def body(dm_ref, halo_ref, pooled_ref, wmix_ref, scale_ref, dz_ref, dwmix_ref, dscale_ref):
        i = pl.program_id(0)

        @pl.when(i == 0)
        def _():
            dwmix_ref[...] = jnp.zeros_like(dwmix_ref)
            dscale_ref[...] = jnp.zeros_like(dscale_ref)

        halo = jnp.where(i < nt - 1, halo_ref[...], 0.0)
        dm = dm_ref[...]
        e = jnp.concatenate([dm, halo], axis=0)
        inv = _inv_counts(i * TM, L)
        for gi in range(len(POOL_WINDOWS)):
            cols = slice(gi * PG, (gi + 1) * PG)
            w = wmix_ref[gi]
            dmixed = (e[:, cols] * scale_ref[:, cols]).astype(BF16)
            dpooled = _dot_nt(dmixed, w)
            pooled = pooled_ref[:, cols]
            mixed = _dot(pooled, w)
            dscale_ref[:, cols] += jnp.sum(dm[:, cols] * mixed, axis=0, keepdims=True)
            dwmix_ref[gi] += _dot_tn(pooled, dmixed[:TM])
            s = dpooled * inv[gi]
            k = 1
            while k < POOL_WINDOWS[gi]:
                s = s + pltpu.roll(s, L - k, 0)
                k *= 2
            dz_ref[:, cols] = (s[:TM] - dpooled[:TM]).astype(BF16)

    return pl.pallas_call(
        body, name="pool_bwd", grid=(nt,),
        in_specs=[pl.BlockSpec((TM, PW), lambda i: (i, 0)),
                  pl.BlockSpec((HALO, PW), lambda i: (jnp.minimum((i + 1) * (TM // HALO), T // HALO - 1), 0)),
                  pl.BlockSpec((TM, PW), lambda i: (i, 0)),
                  _const_spec((len(POOL_WINDOWS), PG, PG)), _const_spec((1, PW))],
        out_specs=[pl.BlockSpec((TM, PW), lambda i: (i, 0)),
                   pl.BlockSpec((len(POOL_WINDOWS), PG, PG), lambda i: (0, 0, 0)),
                   pl.BlockSpec((1, PW), lambda i: (0, 0))],
        out_shape=[SDS((T, PW), BF16), SDS((len(POOL_WINDOWS), PG, PG), F32), SDS((1, PW), F32)],
        compiler_params=_cp(("arbitrary",)),
    )(dmixs, dmixs, pooled, wmix, scale)


def merge_fwd(att, mixs, gt, h1, wattn, wpool, wout):
    T = h1.shape[0]
    TM = 512

    def body(att_ref, mixs_ref, gt_ref, h_ref, wa_ref, wp_ref, wo_ref, h2_ref, mg_ref):
        a = _dot(att_ref[...], wa_ref[...])
        p = _dot(mixs_ref[...], wp_ref[...])
        merged = (jax.nn.sigmoid(gt_ref[:, 0:D].astype(F32)) * a + jax.nn.sigmoid(gt_ref[:, D:2 * D].astype(F32)) * p)
        mg = merged.astype(BF16)
        mg_ref[...] = mg
        h2_ref[...] = h_ref[...] + _dot(mg, wo_ref[...])

    tile = lambda w: pl.BlockSpec((TM, w), lambda i: (i, 0))
    return pl.pallas_call(
        body, name="merge_fwd", grid=(T // TM,),
        in_specs=[tile(AW), tile(PW), tile(2 * D), tile(D), _const_spec((AW, D)), _const_spec((PW, D)),
                  _const_spec((D, D))],
        out_specs=[tile(D), tile(D)],
        out_shape=[SDS((T, D), F32), SDS((T, D), BF16)],
        compiler_params=_cp(("arbitrary",)),
    )(att, mixs, gt, h1, wattn, wpool, wout)


def merge_bwd(dh2, att, mixs, gt, wattn, wpool, wout, after=None):
    T = dh2.shape[0]
    TM = 512
    token_spec, token_arg = _token_operand(after)

    def body(dh2_ref, att_ref, mixs_ref, gt_ref, wa_ref, wp_ref, wo_ref, *rest):
        datt_ref, dmixs_ref, dgt_ref, da_ref, dp_ref = rest[-5:]
        dm = _dot_nt(dh2_ref[...].astype(BF16), wo_ref[...])
        a = _dot(att_ref[...], wa_ref[...])
        p = _dot(mixs_ref[...], wp_ref[...])
        sa = jax.nn.sigmoid(gt_ref[:, 0:D].astype(F32))
        sp = jax.nn.sigmoid(gt_ref[:, D:2 * D].astype(F32))
        da = (dm * sa).astype(BF16)
        dp = (dm * sp).astype(BF16)
        da_ref[...] = da
        dp_ref[...] = dp
        dgt_ref[:, 0:D] = (dm * a * sa * (1.0 - sa)).astype(BF16)
        dgt_ref[:, D:2 * D] = (dm * p * sp * (1.0 - sp)).astype(BF16)
        datt_ref[...] = _dot_nt(da, wa_ref[...]).astype(BF16)
        dmixs_ref[...] = _dot_nt(dp, wp_ref[...])

    tile = lambda w: pl.BlockSpec((TM, w), lambda i: (i, 0))
    return pl.pallas_call(
        body, name="merge_bwd", grid=(T // TM,),
        in_specs=[tile(D), tile(AW), tile(PW), tile(2 * D), _const_spec((AW, D)), _const_spec((PW, D)),
                  _const_spec((D, D))] + token_spec,
        out_specs=[tile(AW), tile(PW), tile(2 * D), tile(D), tile(D)],
        out_shape=[SDS((T, AW), BF16), SDS((T, PW), F32), SDS((T, 2 * D), BF16), SDS((T, D), BF16),
                   SDS((T, D), BF16)],
        compiler_params=_cp(("arbitrary",)),
    )(dh2, att, mixs, gt, wattn, wpool, wout, *token_arg)


def loss_head(h3, target, g):
    T = h3.shape[0]
    TM = 512

    def body(h_ref, t_ref, g_ref, dh_ref, loss_ref, dg_ref):
        @pl.when(pl.program_id(0) == 0)
        def _():
            loss_ref[...] = jnp.zeros_like(loss_ref)
            dg_ref[...] = jnp.zeros_like(dg_ref)

        x = h_ref[...]
        g = g_ref[...]
        rstd = _rstd(x)
        xhat = x * rstd
        err = xhat * g - t_ref[...]
        loss_ref[...] += 0.5 * jnp.sum(jnp.mean(err * err, axis=-1, keepdims=True), axis=0, keepdims=True)
        dy = err * (1.0 / D)
        dg_ref[...] += jnp.sum(dy * xhat, axis=0, keepdims=True)
        dh_ref[...] = _rms_bwd(dy, xhat, rstd, g)

    tile = pl.BlockSpec((TM, D), lambda i: (i, 0))
    return pl.pallas_call(
        body, name="loss_head", grid=(T // TM,),
        in_specs=[tile, tile, _const_spec((1, D))],
        out_specs=[tile, pl.BlockSpec((1, 1), lambda i: (0, 0)), pl.BlockSpec((1, D), lambda i: (0, 0))],
        out_shape=[SDS((T, D), F32), SDS((1, 1), F32), SDS((1, D), F32)],
        compiler_params=_cp(("arbitrary",)),
    )(h3, target, g)


def adamw(w, g, m, v, name):
    R, C = w.shape
    tile_bytes = 2 * 1024 * 1024
    tr = R
    if R * C * 4 > tile_bytes:
        tr = next(cand for cand in (512, 256, 128, 64, 32, 16, 8) if R % cand == 0 and cand * C * 4 <= tile_bytes)
    c1 = 1.0 - B1 ** STEP
    c2 = 1.0 - B2 ** STEP

    def body(w_ref, g_ref, m_ref, v_ref, go_ref, d_ref, nm_ref, nv_ref):
        gv = g_ref[...]
        go_ref[...] = gv
        nm = B1 * m_ref[...] + (1.0 - B1) * gv
        nv = B2 * v_ref[...] + (1.0 - B2) * (gv * gv)
        nm_ref[...] = nm
        nv_ref[...] = nv
        d_ref[...] = -LR * ((nm / c1) / (jnp.sqrt(nv / c2) + ADAM_EPS) + WD * w_ref[...])

    spec = pl.BlockSpec((tr, C), lambda i: (i, 0))
    return pl.pallas_call(
        body, name=name, grid=(R // tr,),
        in_specs=[spec] * 4, out_specs=[spec] * 4, out_shape=[SDS((R, C), F32)] * 4,
        compiler_params=_cp(("parallel",)),
    )(w, g, m, v)


GROUP_FFN1, GROUP_MIX, GROUP_FFN2 = 0, 1, 2


def _behind(small, token):
    return small if token is None else small + token[0:1, 0:1]


def local_fwd_bwd(x, target, S, comm):
    W = comm.weights(GROUP_FFN1, None)
    h1, ab1, hid1 = ffn_fwd(x, _behind(S["ffn1_norm"], comm.started()), W["ffn1_w_up"], W["ffn1_w_down"], "ffn1_fwd")
    W = comm.weights(GROUP_MIX, h1)
    u, q, kv, z, gt = mix_in_fwd(h1, _behind(S["mix_norm"], comm.started()), W["w_in"])
    att, lse = attn_fwd(q, kv, S["sinks"])
    pooled, mixs = pool_fwd(z, S["pool_w_mix"], S["pool_scale"])
    h2, merged = merge_fwd(att, mixs, gt, h1, W["w_attn_up"], W["w_pool_up"], W["w_out"])
    W = comm.weights(GROUP_FFN2, h2)
    h3, ab2, hid2 = ffn_fwd(h2, S["ffn2_norm"], W["ffn2_w_up"], W["ffn2_w_down"], "ffn2_fwd")
    dh3, loss, g_final = loss_head(h3, target, S["final_norm"])

    G = {"final_norm": g_final}
    dh2, dup2, n2, G["ffn2_norm"] = ffn_bwd_x(dh3, h2, ab2, S["ffn2_norm"], W["ffn2_w_up"], W["ffn2_w_down"],
                                              "ffn2_bwd_x")
    ffn2 = ("ffn2_w_down", "ffn2_w_up")
    token = comm.grads({"ffn2_w_down": matmul_tn(hid2, dh3, "ffn2_dw_down", tm=1408, tn=512, tt=4096, b_scale=0.5),
                        "ffn2_w_up": matmul_tn(n2, dup2, "ffn2_dw_up", tm=D, tn=512, tt=4096)})

    W = comm.weights(GROUP_MIX, None)
    datt, dmixs, dgt, da, dp = merge_bwd(dh2, att, mixs, gt, W["w_attn_up"], W["w_pool_up"], W["w_out"], after=token)
    token = comm.advance(ffn2, datt)
    g_mix = {"w_out": matmul_tn(merged, dh2, "dw_out", tm=D, tn=D, after=token),
             "w_attn_up": matmul_tn(att, da, "dw_attn_up", tm=AW, tn=D),
             "w_pool_up": matmul_tn(mixs, dp, "dw_pool_up", tm=PW, tn=D)}
    dz, G["pool_w_mix"], G["pool_scale"] = pool_bwd(dmixs, pooled, S["pool_w_mix"], _behind(S["pool_scale"], token))
    dq, dkv, G["sinks"] = attn_bwd(q, kv, datt, lse, _behind(S["sinks"], token))
    dh1, G["mix_norm"] = mix_in_bwd(dq, dkv, dz, dgt, dh2, h1, S["mix_norm"], W["w_in"])
    g_mix["w_in"] = jnp.concatenate([
        matmul_tn(dq, u, "dw_in_q", tm=AW, tn=D),
        matmul_tn(dkv, u, "dw_in_kv", tm=2 * KVW, tn=D),
        matmul_tn(dz, u, "dw_in_z", tm=PW, tn=D),
        matmul_tn(dgt, u, "dw_in_g", tm=D, tn=512, tt=4096),
    ], axis=0)
    mix = tuple(g_mix)
    comm.finish(ffn2, dh1)
    token = comm.grads(g_mix)

    W = comm.weights(GROUP_FFN1, None)
    g_dn1 = matmul_tn(hid1, dh1, "ffn1_dw_down", tm=1408, tn=512, tt=4096, b_scale=0.5, after=token)
    token = comm.advance(mix, g_dn1)
    token2 = comm.grads({"ffn1_w_down": g_dn1})
    g1 = _behind(_behind(S["ffn1_norm"], token), token2)
    dx, dup1, n1, G["ffn1_norm"] = ffn_bwd_x(dh1, x, ab1, g1, W["ffn1_w_up"], W["ffn1_w_down"], "ffn1_bwd_x")
    token = comm.advance(("ffn1_w_down",), dx)
    token2 = comm.small_start(G, dx)
    g_up1 = matmul_tn(n1, dup1, "ffn1_dw_up", tm=D, tn=512, tt=4096,
                      after=None if token is None else token + token2)
    token = comm.grads({"ffn1_w_up": g_up1})
    done = comm.finish(mix, token)
    token = comm.advance(("ffn1_w_up",), done)
    done = comm.finish(("ffn1_w_down",), token)
    done = comm.small_finish(done)
    comm.finish(("ffn1_w_up",), done)
    return loss, dx


HBM_SPEC = pl.BlockSpec(memory_space=pltpu.HBM)


def _me():
    return lax.axis_index("x"), lax.axis_index("y"), lax.axis_index("c")


def _peer_chip(x, y, k):
    return x ^ (k >> 1), y ^ (k & 1)


def _piece_half(ref, rowlike, j, h):
    if rowlike:
        return ref.at[j, h]
    ns = ref.shape[-1] // N_CHIPS
    return ref.at[h, :, pl.ds(pl.multiple_of(j * ns, 128), ns)]


def _piece(ref, rowlike, j):
    if rowlike:
        return ref.at[j]
    ns = ref.shape[-1] // N_CHIPS
    return ref.at[:, :, pl.ds(pl.multiple_of(j * ns, 128), ns)]


def _full_shape(shard_view, rowlike):
    _, kh, ns = shard_view.shape
    return (N_CHIPS, 2, kh, ns) if rowlike else (2, kh, N_CHIPS * ns)


def _remote(src, dst, send_sem, recv_sem, dev):
    return pltpu.make_async_remote_copy(src, dst, send_sem, recv_sem, device_id=dev, device_id_type=MESH)


SEM_SPEC = pl.BlockSpec(memory_space=pltpu.SEMAPHORE)
SPLIT_PARAMS = pltpu.CompilerParams(has_side_effects=pltpu.SideEffectType.DATAFLOW_SIDE_EFFECTING)


def _gather_plan(rowlikes):
    def plan(s_refs, f_refs, a, k, x, y, c):
        px, py = _peer_chip(x, y, k)
        return (s_refs[a].at[c], _piece_half(f_refs[a], rowlikes[a], 2 * x + y, c),
                _piece_half(f_refs[a], rowlikes[a], 2 * px + py, c), (px, py, c))
    return plan


def _all_to_all_plan(rowlikes):
    def plan(q_refs, r_refs, a, k, x, y, c):
        px, py = _peer_chip(x, y, k)
        if rowlikes[a]:
            src = q_refs[a].at[2 * px + py]
        else:
            ns = q_refs[a].shape[-1] // N_CHIPS
            src = q_refs[a].at[:, pl.ds(pl.multiple_of((2 * px + py) * ns, 128), ns)]
        return src, r_refs[a].at[k - 1], r_refs[a].at[k - 1], (px, py, c)
    return plan


def _swap_plan(rowlikes):
    def plan(g_refs, got_refs, a, k, x, y, c):
        src = g_refs[a].at[:, 1 - c] if rowlikes[a] else g_refs[a].at[1 - c]
        return src, got_refs[a], got_refs[a], (x, y, 1 - c)
    return plan


def _everyone_plan(s_refs, slot_refs, a, k, x, y, c):
    px, py, pc = x ^ (k >> 2), y ^ ((k >> 1) & 1), c ^ (k & 1)
    return s_refs[a], slot_refs[a].at[4 * x + 2 * y + c], slot_refs[a].at[4 * px + 2 * py + pc], (px, py, pc)


CHIPS, SIBLING, EVERYONE = (1, 2, 3), (1,), tuple(range(1, 8))


def exchange_start(srcs, land_shapes, plan, after, name, peers=CHIPS):
    n = len(srcs)
    lands = [lax.empty(shape, s.dtype) for shape, s in zip(land_shapes, srcs)]

    behind = [] if after is None else [after]

    def body(*refs):
        s_refs, l_refs = refs[:n], refs[n:2 * n]
        send_sems, recv_sems = refs[2 * n + len(behind)], refs[2 * n + len(behind) + 1]
        token = refs[-1]
        x, y, c = _me()
        for a in range(n):
            for i, k in enumerate(peers):
                src, dst, _, peer = plan(s_refs, l_refs, a, k, x, y, c)
                sem = len(peers) * a + i
                _remote(src, dst, send_sems.at[sem], recv_sems.at[sem], peer).start()
        token[...] = jnp.zeros_like(token)

    n_sems = len(peers) * n
    outs = pl.pallas_call(
        body, name=name, in_specs=[HBM_SPEC] * (2 * n) + [pl.BlockSpec(memory_space=pl.ANY)] * len(behind),
        out_specs=[SEM_SPEC, SEM_SPEC] + [HBM_SPEC] * (2 * n) + [pl.BlockSpec(memory_space=pltpu.VMEM)],
        out_shape=[pltpu.SemaphoreType.DMA((n_sems,)), pltpu.SemaphoreType.DMA((n_sems,))]
        + [pltpu.HBM(a.shape, a.dtype) for a in (*srcs, *lands)] + [SDS((8, 128), F32)],
        input_output_aliases={i: 2 + i for i in range(2 * n)},
        compiler_params=SPLIT_PARAMS,
    )(*[pltpu.with_memory_space_constraint(a, pltpu.HBM) for a in (*srcs, *lands)], *behind)
    return {"sems": outs[:2], "srcs": outs[2:2 + n], "lands": outs[2 + n:2 + 2 * n], "token": outs[-1]}


def exchange_wait(state, plan, after, name, peers=CHIPS):
    n = len(state["srcs"])

    def body(*refs):
        s_refs, l_refs = refs[:n], refs[n:2 * n]
        send_sems, recv_sems = refs[2 * n], refs[2 * n + 1]
        x, y, c = _me()
        for a in range(n):
            for i, k in enumerate(peers):
                src, _, landing, peer = plan(s_refs, l_refs, a, k, x, y, c)
                sem = len(peers) * a + i
                cp = _remote(src, landing, send_sems.at[sem], recv_sems.at[sem], peer)
                cp.wait_send()
                cp.wait_recv()

    bufs = (*state["srcs"], *state["lands"])
    outs = pl.pallas_call(
        body, name=name,
        in_specs=[HBM_SPEC] * (2 * n) + [SEM_SPEC, SEM_SPEC, pl.BlockSpec(memory_space=pl.ANY)],
        out_specs=[HBM_SPEC] * (2 * n),
        out_shape=[pltpu.HBM(a.shape, a.dtype) for a in bufs],
        input_output_aliases={i: i for i in range(2 * n)},
        compiler_params=SPLIT_PARAMS,
    )(*bufs, *state["sems"], after)
    return outs[:n], outs[n:]


def gather_finish(shards, fulls, rowlikes, name):
    n = len(shards)

    def body(*refs):
        s_refs, f_refs = refs[:n], refs[2 * n:3 * n]
        send_sems, recv_sems = refs[3 * n:]
        x, y, c = _me()
        chip = 2 * x + y
        sib = (x, y, 1 - c)
        sends = []
        for a in range(n):
            own = _piece(f_refs[a], rowlikes[a], chip)
            cp = _remote(s_refs[a], own, send_sems.at[a, 0], recv_sems.at[a, 0], sib)
            cp.start()
            sends.append(cp)
            for k in (1, 2, 3):
                px, py = _peer_chip(x, y, k)
                slot = _piece_half(f_refs[a], rowlikes[a], 2 * px + py, c)
                cp = _remote(slot, slot, send_sems.at[a, k], recv_sems.at[a, k], sib)
                cp.start()
                sends.append(cp)
        for a in range(n):
            own = _piece(f_refs[a], rowlikes[a], chip)
            _remote(own, own, send_sems.at[a, 0], recv_sems.at[a, 0], sib).wait_recv()
            for k in (1, 2, 3):
                px, py = _peer_chip(x, y, k)
                slot = _piece_half(f_refs[a], rowlikes[a], 2 * px + py, 1 - c)
                _remote(slot, slot, send_sems.at[a, k], recv_sems.at[a, k], sib).wait_recv()
        for cp in sends:
            cp.wait_send()

    return pl.pallas_call(
        body, name=name, in_specs=[HBM_SPEC] * (2 * n), out_specs=[HBM_SPEC] * n,
        out_shape=[SDS(f.shape, f.dtype) for f in fulls],
        input_output_aliases={n + a: a for a in range(n)},
        scratch_shapes=[pltpu.SemaphoreType.DMA((n, 4)), pltpu.SemaphoreType.DMA((n, 4))],
    )(*shards, *fulls)


def _half_buffer_shape(gview, rowlike):
    return (N_CHIPS,) + gview.shape[2:] if rowlike else gview.shape[1:]


ROW_TILES = 2


def _piece_specs(rowlike, kh, ns, piece, half):
    tr = kh // ROW_TILES
    if rowlike:
        return (pl.BlockSpec((None, None, tr, ns), lambda *g: (piece(*g), half(*g), g[-2], 0)),
                pl.BlockSpec((None, tr, ns), lambda *g: (piece(*g), g[-2], 0)))
    return (pl.BlockSpec((None, tr, ns), lambda *g: (half(*g), g[-2], piece(*g))),
            pl.BlockSpec((tr, ns), lambda *g: (g[-2], piece(*g))))


def add_halves(gview, got, rowlike, place, name):
    kh, ns = (gview.shape[2], gview.shape[3]) if rowlike else (gview.shape[1], gview.shape[2] // N_CHIPS)

    def body(place_ref, g_ref, got_ref, o_ref):
        o_ref[...] = (g_ref[...] + got_ref[...]).astype(BF16)

    g_spec, h_spec = _piece_specs(rowlike, kh, ns, lambda j, r, p: j, lambda j, r, p: p[0])
    return pl.pallas_call(
        body, name=name,
        grid_spec=pltpu.PrefetchScalarGridSpec(num_scalar_prefetch=1, grid=(N_CHIPS, ROW_TILES),
                                               in_specs=[g_spec, h_spec], out_specs=h_spec),
        out_shape=SDS(got.shape, BF16),
        compiler_params=_cp(("parallel", "parallel")),
    )(place, gview, got)


def _slot_shape(q, rowlike):
    return (3,) + (q.shape[1:] if rowlike else (q.shape[0], q.shape[1] // N_CHIPS))


def sum_pieces(gview, got, recv, rowlike, place, name):
    kh, ns = recv.shape[1], recv.shape[2]
    tr = kh // ROW_TILES

    def body(place_ref, g_ref, got_ref, r_ref, o_ref):
        acc = g_ref[...] + got_ref[...]
        for k in range(3):
            acc = acc + r_ref[k].astype(F32)
        o_ref[...] = acc

    g_spec, h_spec = _piece_specs(rowlike, kh, ns, lambda z, r, p: p[1], lambda z, r, p: p[0])
    return pl.pallas_call(
        body, name=name,
        grid_spec=pltpu.PrefetchScalarGridSpec(
            num_scalar_prefetch=1, grid=(1, ROW_TILES),
            in_specs=[g_spec, h_spec, pl.BlockSpec((3, tr, ns), lambda z, r, p: (0, r, 0))],
            out_specs=pl.BlockSpec((None, tr, ns), lambda z, r, p: (p[0], r, 0))),
        out_shape=SDS((2, kh, ns), F32),
        compiler_params=_cp(("parallel", "parallel")),
    )(place, gview, got, recv)


def join_halves(halves, name):
    n = len(halves)

    def body(*refs):
        o_refs = refs[n:2 * n]
        send_sems, recv_sems = refs[2 * n:]
        x, y, c = _me()
        sib = (x, y, 1 - c)
        sends = []
        for a in range(n):
            cp = _remote(o_refs[a].at[c], o_refs[a].at[c], send_sems.at[a], recv_sems.at[a], sib)
            cp.start()
            sends.append(cp)
        for a in range(n):
            got = o_refs[a].at[1 - c]
            _remote(got, got, send_sems.at[a], recv_sems.at[a], sib).wait_recv()
        for cp in sends:
            cp.wait_send()

    return pl.pallas_call(
        body, name=name, in_specs=[HBM_SPEC] * n, out_specs=[HBM_SPEC] * n,
        out_shape=[SDS(h.shape, h.dtype) for h in halves],
        input_output_aliases={a: a for a in range(n)},
        scratch_shapes=[pltpu.SemaphoreType.DMA((n,)), pltpu.SemaphoreType.DMA((n,))],
    )(*halves)


N_DEV = 8


def sum_devices(s, slots, me):
    R, Wd = s.shape

    def body(me_ref, s_ref, slots_ref, out_ref):
        acc = None
        for d in range(N_DEV):
            mine = me_ref[0] == d
            term = jnp.where(mine, s_ref[...], slots_ref[jnp.where(mine, d ^ 1, d)])
            acc = term if acc is None else acc + term
        out_ref[...] = acc

    vmem = pl.BlockSpec(memory_space=pltpu.VMEM)
    return pl.pallas_call(
        body, name="sum_devices", in_specs=[pl.BlockSpec(memory_space=pltpu.SMEM), vmem, vmem], out_specs=vmem,
        out_shape=SDS((R, Wd), F32),
    )(me, s, slots)


TRANSPOSED = ("w_in",)
BIG = {"ffn1_w_up": (D, 2 * FF, "col"), "ffn1_w_down": (FF, D, "row"), "w_in": (INW, D, "row"),
       "w_attn_up": (AW, D, "row"), "w_pool_up": (PW, D, "col"), "w_out": (D, D, "row"),
       "ffn2_w_up": (D, 2 * FF, "col"), "ffn2_w_down": (FF, D, "row")}
GROUPS = (("ffn1_w_up", "ffn1_w_down"), ("w_in", "w_attn_up", "w_pool_up", "w_out"), ("ffn2_w_up", "ffn2_w_down"))
SMALL = ("ffn1_norm", "mix_norm", "ffn2_norm", "final_norm", "pool_scale", "sinks", "pool_w_mix")
SMALL_W = 128


def _rowlike(name):
    return BIG[name][2] == "row"


def _half_dims(name):
    k, n, kind = BIG[name]
    return (k // N_CHIPS // 2, n) if kind == "row" else (k // 2, n // N_CHIPS)


def shard_view(name, shard):
    return shard.reshape((2,) + _half_dims(name))


def full_from_view(name, fv):
    k, n, _ = BIG[name]
    return fv.reshape(k, n)


def grad_view(name, g):
    kh, ns = _half_dims(name)
    return g.reshape(_full_shape(jax.ShapeDtypeStruct((2, kh, ns), g.dtype), _rowlike(name)))


def pack_small(d):
    parts = []
    for name in SMALL:
        a = d[name].reshape(-1)
        pad = (-a.shape[0]) % SMALL_W
        parts.append(jnp.pad(a, (0, pad)).reshape(-1, SMALL_W))
    a = jnp.concatenate(parts, axis=0)
    return jnp.pad(a, ((0, (-a.shape[0]) % 8), (0, 0)))


def unpack_small(a, like):
    out, r0 = {}, 0
    for name in SMALL:
        size = int(np.prod(like[name].shape))
        rows = -(-size // SMALL_W)
        out[name] = a[r0:r0 + rows].reshape(-1)[:size].reshape(like[name].shape)
        r0 += rows
    return out


WEIGHTS = ("ffn1_norm", "ffn1_w_up", "ffn1_w_down", "mix_norm", "w_in", "sinks", "w_attn_up", "pool_w_mix",
           "pool_scale", "w_pool_up", "w_out", "ffn2_norm", "ffn2_w_up", "ffn2_w_down", "final_norm")


def kernel(x, ffn1_norm, ffn1_w_up, ffn1_w_down, mix_norm, w_in, sinks, w_attn_up, pool_w_mix, pool_scale, w_pool_up, w_out, ffn2_norm, ffn2_w_up, ffn2_w_down, final_norm, loss_target, m_ffn1_norm, m_ffn1_w_up, m_ffn1_w_down, m_mix_norm, m_w_in, m_sinks, m_w_attn_up, m_pool_w_mix, m_pool_scale, m_w_pool_up, m_w_out, m_ffn2_norm, m_ffn2_w_up, m_ffn2_w_down, m_final_norm, v_ffn1_norm, v_ffn1_w_up, v_ffn1_w_down, v_mix_norm, v_w_in, v_sinks, v_w_attn_up, v_pool_w_mix, v_pool_scale, v_w_pool_up, v_w_out, v_ffn2_norm, v_ffn2_w_up, v_ffn2_w_down, v_final_norm):
    given = dict(locals())
    w = {n: given[n] for n in WEIGHTS}
    m = {n: given["m_" + n] for n in WEIGHTS}
    v = {n: given["v_" + n] for n in WEIGHTS}
    cx, cy, cc = _me()
    place = jnp.stack([cc, 2 * cx + cy]).astype(jnp.int32)

    def local2d(d, n):
        return d[n][0].T if n in TRANSPOSED else d[n][0]

    shards = {n: local2d(w, n) for n in BIG}
    sviews = {n: shard_view(n, shards[n].astype(BF16)) for n in BIG}
    grads, delta, new_m, new_v = {}, {}, {}, {}
    rowlikes = [[_rowlike(n) for n in names] for names in GROUPS]

    class Exchanges:
        def __init__(self):
            self.gathers, self.fulls, self.reductions, self.small = {}, {}, {}, None

        def _start_gather(self, group, after):
            sv = [sviews[n] for n in GROUPS[group]]
            rl = rowlikes[group]
            self.gathers[group] = exchange_start(sv, [_full_shape(s, r) for s, r in zip(sv, rl)], _gather_plan(rl),
                                                 after, f"gather_start_{group}")

        def weights(self, group, after):
            if not self.gathers:
                self._start_gather(0, None)
            if group not in self.fulls:
                names, rl, state = GROUPS[group], rowlikes[group], self.gathers[group]
                sv, fulls = exchange_wait(state, _gather_plan(rl), state["token"] if after is None else after,
                                          f"gather_wait_{group}")
                fulls = gather_finish(sv, fulls, rl, f"gather_finish_{group}")
                self.fulls[group] = {n: full_from_view(n, f) for n, f in zip(names, fulls)}
                if group + 1 < len(GROUPS):
                    self._start_gather(group + 1, fulls[0])
            return self.fulls[group]

        def started(self):
            return self.gathers[max(self.gathers)]["token"]

        def grads(self, g):
            names = tuple(g)
            rl = [_rowlike(n) for n in names]
            gv = [grad_view(n, g[n]) for n in names]
            state = exchange_start(gv, [_half_buffer_shape(a, r) for a, r in zip(gv, rl)], _swap_plan(rl), None,
                                   "swap_start_" + names[0], SIBLING)
            self.reductions[names] = state
            return state["token"]

        def advance(self, names, after):
            rl = [_rowlike(n) for n in names]
            gv, gots = exchange_wait(self.reductions[names], _swap_plan(rl), after, "swap_wait_" + names[0], SIBLING)
            qs = [add_halves(a, b, r, place, "add_halves_" + n) for a, b, r, n in zip(gv, gots, rl, names)]
            state = exchange_start(qs, [_slot_shape(q, r) for q, r in zip(qs, rl)], _all_to_all_plan(rl), None,
                                   "all_to_all_start_" + names[0])
            self.reductions[names] = (gv, gots, state)
            return state["token"]

        def finish(self, names, after):
            rl = [_rowlike(n) for n in names]
            gv, gots, state = self.reductions.pop(names)
            _, recvs = exchange_wait(state, _all_to_all_plan(rl), after, "all_to_all_wait_" + names[0])
            halves = [sum_pieces(a, b, r, k, place, "sum_pieces_" + n)
                      for a, b, r, k, n in zip(gv, gots, recvs, rl, names)]
            for n, o in zip(names, join_halves(halves, "join_halves_" + names[0])):
                grads[n], delta[n], new_m[n], new_v[n] = adamw(shards[n], o.reshape(shards[n].shape), local2d(m, n),
                                                               local2d(v, n), "adamw_" + n)
            return new_v[names[-1]]

        def small_start(self, G, after):
            packed = pack_small({n: G[n] for n in SMALL})
            self.small = exchange_start([packed], [(N_DEV,) + packed.shape], _everyone_plan, after,
                                        "small_start", EVERYONE)
            return self.small["token"]

        def small_finish(self, after):
            (packed,), (slots,) = exchange_wait(self.small, _everyone_plan, after, "small_wait", EVERYONE)
            total = sum_devices(packed, slots, (4 * cx + 2 * cy + cc).astype(jnp.int32).reshape(1))
            g, ds, ms, vs = adamw(pack_small(w), total, pack_small(m), pack_small(v), "adamw_small")
            for d, packed_d in ((grads, g), (delta, ds), (new_m, ms), (new_v, vs)):
                d.update(unpack_small(packed_d, small_like))
            return vs

    small_like = {n: w[n] for n in SMALL}
    S = {n: w[n].reshape(1, -1) for n in ("ffn1_norm", "mix_norm", "ffn2_norm", "final_norm", "pool_scale", "sinks")}
    S["pool_w_mix"] = w["pool_w_mix"][0].astype(BF16)
    loss, dx = local_fwd_bwd(x[0], loss_target[0], S, Exchanges())
    loss = lax.psum(loss[0, 0], ("x", "y", "c"))

    def shaped(d, n):
        return (d[n].T if n in TRANSPOSED else d[n]).reshape(w[n].shape)

    return (loss, dx[None], *[shaped(grads, n) for n in WEIGHTS], *[shaped(delta, n) for n in WEIGHTS],
            *[shaped(new_m, n) for n in WEIGHTS], *[shaped(new_v, n) for n in WEIGHTS])
```

```python
import numpy as np
import jax
import jax.numpy as jnp
from jax import lax
from jax.experimental import pallas as pl
from jax.experimental.pallas import tpu as pltpu

F32 = jnp.float32
BF16 = jnp.bfloat16
SDS = jax.ShapeDtypeStruct
MESH = pl.DeviceIdType.MESH

D = 1024
FF = 2816
NQ = 16
NKV = 2
HD = 64
GQ = NQ // NKV
AW = NQ * HD
KVW = NKV * HD
BLK = 128
PW = 512
PG = 128
POOL_WINDOWS = (2, 4, 8, 16)
HALO = 16
INW = AW + 2 * KVW + PW + 2 * D
C_KV = AW
C_Z = AW + 2 * KVW
C_G = C_Z + PW
EPS = 1e-6
FF_CHUNK = 256
FF_CHUNKS = tuple((c, FF_CHUNK) for c in range(0, FF, FF_CHUNK))
SLOPES = tuple(float(2.0 ** (-8.0 * h / NQ)) for h in range(1, NQ + 1))
SCALE = HD ** -0.5

LR, B1, B2, ADAM_EPS, WD, STEP = 0.001, 0.9, 0.999, 1e-08, 0.01, 10

VMEM_LIMIT = 56 * 1024 * 1024
N_CHIPS = 4

NT = (((1,), (1,)), ((), ()))
TN = (((0,), (0,)), ((), ()))


def _cp(sem=None, vmem=VMEM_LIMIT):
    return pltpu.CompilerParams(dimension_semantics=sem, vmem_limit_bytes=vmem)


def _const_spec(shape):
    nd = len(shape)
    return pl.BlockSpec(shape, lambda *_: (0,) * nd, pipeline_mode=pl.Buffered(1))


def _rstd(x):
    return lax.rsqrt(jnp.mean(x * x, axis=-1, keepdims=True) + EPS)


def _rms_bwd(dn, xhat, rstd, g):
    dxhat = dn * g
    return rstd * (dxhat - xhat * jnp.mean(dxhat * xhat, axis=-1, keepdims=True))


def _dot(a, b):
    return jnp.dot(a, b, preferred_element_type=F32)


def _dot_nt(a, b):
    return lax.dot_general(a, b, NT, preferred_element_type=F32)


def _dot_tn(a, b):
    return lax.dot_general(a, b, TN, preferred_element_type=F32)


def ffn_fwd(h, g, wup, wdn, name, head=None):
    T = h.shape[0]
    TM = 512
    tile = lambda w: pl.BlockSpec((TM, w), lambda i: (i, 0))
    acc_spec = lambda w: pl.BlockSpec((1, w), lambda i: (0, 0))

    def body(h_ref, g_ref, wup_ref, wdn_ref, *rest):
        out_ref, ab_ref, hid_ref = rest[-3:] if head is None else rest[2:5]
        x = h_ref[...]
        n = (x * _rstd(x) * g_ref[...]).astype(BF16)
        for c0, w in FF_CHUNKS:
            a = _dot(n, wup_ref[:, c0:c0 + w])
            b = _dot(n, wup_ref[:, FF + c0:FF + c0 + w])
            sig = jax.nn.sigmoid(a)
            s = a * sig
            ab_ref[:, c0:c0 + w] = (b * (sig * (1.0 + a * (1.0 - sig)))).astype(BF16)
            ab_ref[:, FF + c0:FF + c0 + w] = s.astype(BF16)
            hid_ref[:, c0:c0 + w] = (s * b).astype(BF16)
        out = x + 0.5 * _dot(hid_ref[...], wdn_ref[...])
        if head is None:
            out_ref[...] = out
        else:
            t_ref, gf_ref, loss_ref, dgf_ref = rest[0], rest[1], rest[5], rest[6]
            out_ref[...] = _loss_head(out, t_ref[...], gf_ref[...], loss_ref, dgf_ref, pl.program_id(0) == 0)

    head_in, head_specs, head_out_specs, head_out_shape = [], [], [], []
    if head is not None:
        head_in, head_specs = list(head), [tile(D), _const_spec((1, D))]
        head_out_specs, head_out_shape = [acc_spec(1), acc_spec(D)], [SDS((1, 1), F32), SDS((1, D), F32)]
    return pl.pallas_call(
        body, name=name, grid=(T // TM,),
        in_specs=[tile(D), _const_spec((1, D)), _const_spec((D, 2 * FF)), _const_spec((FF, D))] + head_specs,
        out_specs=[tile(D), tile(2 * FF), tile(FF)] + head_out_specs,
        out_shape=[SDS((T, D), F32), SDS((T, 2 * FF), BF16), SDS((T, FF), BF16)] + head_out_shape,
        compiler_params=_cp(("arbitrary",)),
    )(h, g, wup, wdn, *head_in)


def _loss_head(x, target, g, loss_ref, dg_ref, first):
    @pl.when(first)
    def _():
        loss_ref[...] = jnp.zeros_like(loss_ref)
        dg_ref[...] = jnp.zeros_like(dg_ref)

    rstd = _rstd(x)
    xhat = x * rstd
    err = xhat * g - target
    loss_ref[...] += 0.5 * jnp.sum(jnp.mean(err * err, axis=-1, keepdims=True), axis=0, keepdims=True)
    dy = err * (1.0 / D)
    dg_ref[...] += jnp.sum(dy * xhat, axis=0, keepdims=True)
    return _rms_bwd(dy, xhat, rstd, g)


def ffn_bwd_x(dh, h_in, ab, g, wup, wdn, name):
    T = dh.shape[0]
    TM = 256

    def body(dh_ref, h_ref, ab_ref, g_ref, wup_ref, wdn_ref, dhin_ref, dup_ref, n_ref, dg_ref):
        x = h_ref[...]
        g = g_ref[...]
        rstd = _rstd(x)
        xhat = x * rstd
        n_ref[...] = (xhat * g).astype(BF16)
        dh = dh_ref[...]
        dhh = (0.5 * dh).astype(BF16)
        for c0, w in FF_CHUNKS:
            dhid = _dot_nt(dhh, wdn_ref[c0:c0 + w, :])
            dup_ref[:, c0:c0 + w] = (dhid * ab_ref[:, c0:c0 + w].astype(F32)).astype(BF16)
            dup_ref[:, FF + c0:FF + c0 + w] = (dhid * ab_ref[:, FF + c0:FF + c0 + w].astype(F32)).astype(BF16)
        dn = _dot_nt(dup_ref[...], wup_ref[...])
        dhin_ref[...] = dh + _rms_bwd(dn, xhat, rstd, g)

        @pl.when(pl.program_id(0) == 0)
        def _():
            dg_ref[...] = jnp.zeros_like(dg_ref)

        dg_ref[...] += jnp.sum(dn * xhat, axis=0, keepdims=True)

    tile = lambda w: pl.BlockSpec((TM, w), lambda i: (i, 0))
    return pl.pallas_call(
        body, name=name, grid=(T // TM,),
        in_specs=[tile(D), tile(D), tile(2 * FF), _const_spec((1, D)), _const_spec((D, 2 * FF)), _const_spec((FF, D))],
        out_specs=[tile(D), tile(2 * FF), tile(D), pl.BlockSpec((1, D), lambda i: (0, 0))],
        out_shape=[SDS((T, D), F32), SDS((T, 2 * FF), BF16), SDS((T, D), BF16), SDS((1, D), F32)],
        compiler_params=_cp(("arbitrary",)),
    )(dh, h_in, ab, g, wup, wdn)


TOKEN_SPEC = pl.BlockSpec((8, 128), lambda *_: (0, 0))


def _token_operand(token):
    return ([], []) if token is None else ([TOKEN_SPEC], [token])


def matmul_tn(a, b, name, *, tm, tn, tt=1024, b_scale=None, after=None):
    T, M = a.shape
    N = b.shape[1]
    tt = min(tt, T)
    assert M % tm == 0 and N % tn == 0 and T % tt == 0
    nt = T // tt
    token_spec, token_arg = _token_operand(after)

    def body(a_ref, b_ref, *rest):
        o_ref, acc_ref = rest[-2:]
        t = pl.program_id(2)

        @pl.when(t == 0)
        def _():
            acc_ref[...] = jnp.zeros_like(acc_ref)

        bv = b_ref[...]
        if b_scale is not None:
            bv = bv * b_scale
        acc_ref[...] += _dot_tn(a_ref[...].astype(BF16), bv.astype(BF16))

        @pl.when(t == nt - 1)
        def _():
            o_ref[...] = acc_ref[...].astype(BF16)

    return pl.pallas_call(
        body, name=name, grid=(M // tm, N // tn, nt),
        in_specs=[pl.BlockSpec((tt, tm), lambda i, j, t: (t, i)), pl.BlockSpec((tt, tn), lambda i, j, t: (t, j))]
        + token_spec,
        out_specs=pl.BlockSpec((tm, tn), lambda i, j, t: (i, j)),
        out_shape=SDS((M, N), BF16),
        scratch_shapes=[pltpu.VMEM((tm, tn), F32)],
        compiler_params=_cp(("parallel", "parallel", "arbitrary")),
    )(a, b, *token_arg)


def mix_in_fwd(h1, g, win_t):
    T = h1.shape[0]
    TM = 512

    def body(h_ref, g_ref, w_ref, u_ref, q_ref, kv_ref, z_ref, gt_ref):
        x = h_ref[...]
        u = (x * _rstd(x) * g_ref[...]).astype(BF16)
        u_ref[...] = u
        for c in range(0, AW, 256):
            q_ref[:, c:c + 256] = _dot_nt(u, w_ref[c:c + 256, :]).astype(BF16)
        kv_ref[...] = _dot_nt(u, w_ref[C_KV:C_Z, :]).astype(BF16)
        for c in range(0, PW, 256):
            z_ref[:, c:c + 256] = _dot_nt(u, w_ref[C_Z + c:C_Z + c + 256, :])
        for c in range(0, 2 * D, 256):
            gt_ref[:, c:c + 256] = _dot_nt(u, w_ref[C_G + c:C_G + c + 256, :]).astype(BF16)

    tile = lambda w: pl.BlockSpec((TM, w), lambda i: (i, 0))
    return pl.pallas_call(
        body, name="mix_in_fwd", grid=(T // TM,),
        in_specs=[tile(D), _const_spec((1, D)), _const_spec((INW, D))],
        out_specs=[tile(D), tile(AW), tile(2 * KVW), tile(PW), tile(2 * D)],
        out_shape=[SDS((T, D), BF16), SDS((T, AW), BF16), SDS((T, 2 * KVW), BF16), SDS((T, PW), F32),
                   SDS((T, 2 * D), BF16)],
        compiler_params=_cp(("arbitrary",)),
    )(h1, g, win_t)


def mix_in_bwd(dq, dkv, dz, dgt, dh2, h1, g, win_t):
    T = h1.shape[0]
    TM = 512

    def body(dq_ref, dkv_ref, dz_ref, dgt_ref, dh2_ref, h_ref, g_ref, w_ref, dh1_ref, dg_ref):
        du = _dot(dq_ref[...], w_ref[0:AW, :])
        du += _dot(dkv_ref[...], w_ref[C_KV:C_Z, :])
        du += _dot(dz_ref[...], w_ref[C_Z:C_G, :])
        du += _dot(dgt_ref[...], w_ref[C_G:INW, :])
        x = h_ref[...]
        g = g_ref[...]
        rstd = _rstd(x)
        xhat = x * rstd
        dh1_ref[...] = dh2_ref[...] + _rms_bwd(du, xhat, rstd, g)

        @pl.when(pl.program_id(0) == 0)
        def _():
            dg_ref[...] = jnp.zeros_like(dg_ref)

        dg_ref[...] += jnp.sum(du * xhat, axis=0, keepdims=True)

    tile = lambda w: pl.BlockSpec((TM, w), lambda i: (i, 0))
    return pl.pallas_call(
        body, name="mix_in_bwd", grid=(T // TM,),
        in_specs=[tile(AW), tile(2 * KVW), tile(PW), tile(2 * D), tile(D), tile(D), _const_spec((1, D)),
                  _const_spec((INW, D))],
        out_specs=[tile(D), pl.BlockSpec((1, D), lambda i: (0, 0))],
        out_shape=[SDS((T, D), F32), SDS((1, D), F32)],
        compiler_params=_cp(("arbitrary",)),
    )(dq, dkv, dz, dgt, dh2, h1, g, win_t)


PAIR = 2 * HD
NPAIR = GQ // 2


def _lo_lanes():
    return lax.broadcasted_iota(jnp.int32, (BLK, PAIR), 1) < HD


def _stack_heads(ref, kvh, scale=None):
    lo = _lo_lanes()
    parts = []
    for pr in range(NPAIR):
        t = ref[:, (kvh * NPAIR + pr) * PAIR:(kvh * NPAIR + pr + 1) * PAIR]
        if scale is not None:
            t = t * scale
        zero = jnp.zeros_like(t)
        parts += [jnp.where(lo, t, zero), jnp.where(lo, zero, t)]
    return jnp.concatenate(parts, axis=0)


def _kv_tiles(kvc_ref, kvp_ref, tile, kvh):
    lo = _lo_lanes()
    dup, left, right = [], [], []
    for ref in (kvp_ref, kvc_ref):
        t = ref[:, tile * PAIR:(tile + 1) * PAIR]
        r = pltpu.roll(t.astype(F32), HD, 1).astype(BF16)
        zero = jnp.zeros_like(t)
        a, b = (t, r) if kvh == 0 else (r, t)
        dup.append(jnp.where(lo, a, b))
        left.append(jnp.where(lo, a, zero))
        right.append(jnp.where(lo, zero, b))
    cat = lambda xs: jnp.concatenate(xs, axis=0)
    return cat(dup), cat(left), cat(right)


def _band_consts(first):
    row = lax.broadcasted_iota(jnp.int32, (BLK, BLK), 0)
    col = lax.broadcasted_iota(jnp.int32, (BLK, BLK), 1)
    upper = col > row
    dist = jnp.where(upper, row - col + BLK, row - col).astype(F32)
    pen = jnp.where(jnp.logical_and(upper, first), -jnp.inf, 0.0)
    return upper, dist, pen


def _split_band(upper, t):
    zero = jnp.zeros_like(t)
    return jnp.concatenate([jnp.where(upper, t, zero), jnp.where(upper, zero, t)], axis=1)


def attn_fwd(q, kv, sinks):
    T = q.shape[0]
    nb = T // BLK

    def body(sink_ref, q_ref, kvc_ref, kvp_ref, att_ref, lse_ref):
        upper, dist, pen = _band_consts(pl.program_id(0) == 0)
        for kvh in range(NKV):
            kdup, _, _ = _kv_tiles(kvc_ref, kvp_ref, 0, kvh)
            _, vleft, vright = _kv_tiles(kvc_ref, kvp_ref, 1, kvh)
            s_all = _dot_nt(_stack_heads(q_ref, kvh, SCALE), kdup)
            for pr in range(NPAIR):
                out = None
                for side, vpad in ((0, vleft), (1, vright)):
                    g = 2 * pr + side
                    hq = kvh * GQ + g
                    sink = sink_ref[0, hq]
                    rows = slice(g * BLK, (g + 1) * BLK)
                    s = jnp.where(upper, s_all[rows, 0:BLK], s_all[rows, BLK:2 * BLK]) - SLOPES[hq] * dist + pen
                    m = jnp.maximum(jnp.max(s, axis=-1, keepdims=True), sink)
                    p = jnp.exp(s - m)
                    l = jnp.sum(p, axis=-1, keepdims=True) + jnp.exp(sink - m)
                    lse_ref[:, hq:hq + 1] = m + jnp.log(l)
                    o = _dot(_split_band(upper, (p * (1.0 / l)).astype(BF16)), vpad)
                    out = o if out is None else out + o
                col0 = (kvh * NPAIR + pr) * PAIR
                att_ref[:, col0:col0 + PAIR] = out.astype(BF16)

    return pl.pallas_call(
        body, name="attn_fwd", grid=(nb,),
        in_specs=[pl.BlockSpec(memory_space=pltpu.SMEM),
                  pl.BlockSpec((BLK, AW), lambda i: (i, 0)),
                  pl.BlockSpec((BLK, 2 * KVW), lambda i: (i, 0)),
                  pl.BlockSpec((BLK, 2 * KVW), lambda i: (jnp.maximum(i - 1, 0), 0))],
        out_specs=[pl.BlockSpec((BLK, AW), lambda i: (i, 0)), pl.BlockSpec((BLK, NQ), lambda i: (i, 0))],
        out_shape=[SDS((T, AW), BF16), SDS((T, NQ), F32)],
        compiler_params=_cp(("arbitrary",)),
    )(sinks, q, kv, kv)


def attn_bwd(q, kv, datt, lse, sinks):
    T = q.shape[0]
    nb = T // BLK

    def body(sink_ref, q_ref, kvc_ref, kvp_ref, do_ref, lse_ref, dq_ref, dkv_ref, dsink_ref, carry_ref):
        i = pl.program_id(0)

        @pl.when(i == 0)
        def _():
            dsink_ref[...] = jnp.zeros_like(dsink_ref)
            carry_ref[...] = jnp.zeros_like(carry_ref)

        @pl.when(i < nb)
        def _():
            upper, dist, pen = _band_consts(i == 0)
            lo = _lo_lanes()
            dk_dup, dv_dup = [], []
            for kvh in range(NKV):
                kdup, kleft, kright = _kv_tiles(kvc_ref, kvp_ref, 0, kvh)
                vdup, _, _ = _kv_tiles(kvc_ref, kvp_ref, 1, kvh)
                qs = _stack_heads(q_ref, kvh, SCALE)
                dos = _stack_heads(do_ref, kvh)
                s_all = _dot_nt(qs, kdup)
                dp_all = _dot_nt(dos, vdup)
                ds_parts, p_parts = [], []
                for pr in range(NPAIR):
                    dq = None
                    for side, kpad in ((0, kleft), (1, kright)):
                        g = 2 * pr + side
                        hq = kvh * GQ + g
                        lse_h = lse_ref[:, hq:hq + 1]
                        rows = slice(g * BLK, (g + 1) * BLK)
                        s = jnp.where(upper, s_all[rows, 0:BLK], s_all[rows, BLK:2 * BLK]) - SLOPES[hq] * dist + pen
                        p = jnp.exp(s - lse_h)
                        dp = jnp.where(upper, dp_all[rows, 0:BLK], dp_all[rows, BLK:2 * BLK])
                        delta = jnp.sum(p * dp, axis=-1, keepdims=True)
                        dsink_ref[:, hq:hq + 1] += -jnp.sum(jnp.exp(sink_ref[0, hq] - lse_h) * delta, axis=0,
                                                            keepdims=True)
                        ds = _split_band(upper, (p * (dp - delta)).astype(BF16))
                        ds_parts.append(ds)
                        p_parts.append(_split_band(upper, p.astype(BF16)))
                        d = _dot(ds, kpad)
                        dq = d if dq is None else dq + d
                    col0 = (kvh * NPAIR + pr) * PAIR
                    dq_ref[:, col0:col0 + PAIR] = (dq * SCALE).astype(BF16)
                dkw = _dot_tn(jnp.concatenate(ds_parts, axis=0), qs)
                dvw = _dot_tn(jnp.concatenate(p_parts, axis=0), dos)
                dk_dup.append(dkw + pltpu.roll(dkw, HD, 1))
                dv_dup.append(dvw + pltpu.roll(dvw, HD, 1))
            dk = jnp.where(jnp.concatenate([lo, lo], axis=0), dk_dup[0], dk_dup[1])
            dv = jnp.where(jnp.concatenate([lo, lo], axis=0), dv_dup[0], dv_dup[1])
            dkv_ref[:, 0:PAIR] = (carry_ref[:, 0:PAIR] + dk[0:BLK]).astype(BF16)
            dkv_ref[:, PAIR:2 * PAIR] = (carry_ref[:, PAIR:2 * PAIR] + dv[0:BLK]).astype(BF16)
            carry_ref[:, 0:PAIR] = dk[BLK:2 * BLK]
            carry_ref[:, PAIR:2 * PAIR] = dv[BLK:2 * BLK]

        @pl.when(i == nb)
        def _():
            dkv_ref[...] = carry_ref[...].astype(BF16)

    cur = lambda i: (jnp.minimum(i, nb - 1), 0)
    prev = lambda i: (jnp.maximum(jnp.minimum(i, nb - 1) - 1, 0), 0)
    return pl.pallas_call(
        body, name="attn_bwd", grid=(nb + 1,),
        in_specs=[pl.BlockSpec(memory_space=pltpu.SMEM),
                  pl.BlockSpec((BLK, AW), cur), pl.BlockSpec((BLK, 2 * KVW), cur), pl.BlockSpec((BLK, 2 * KVW), prev),
                  pl.BlockSpec((BLK, AW), cur), pl.BlockSpec((BLK, NQ), cur)],
        out_specs=[pl.BlockSpec((BLK, AW), cur),
                   pl.BlockSpec((BLK, 2 * KVW), lambda i: (jnp.maximum(i - 1, 0), 0)),
                   pl.BlockSpec((1, NQ), lambda i: (0, 0))],
        out_shape=[SDS((T, AW), BF16), SDS((T, 2 * KVW), BF16), SDS((1, NQ), F32)],
        scratch_shapes=[pltpu.VMEM((BLK, 2 * KVW), F32)],
        compiler_params=_cp(("arbitrary",)),
    )(sinks, q, kv, kv, datt, lse)


def _inv_counts(t0, rows):
    t = (t0 + lax.broadcasted_iota(jnp.int32, (rows, 1), 0) + 1).astype(F32)
    return [1.0 / jnp.minimum(t, float(w)) for w in POOL_WINDOWS]


def pool_fwd(z, wmix, scale):
    T = z.shape[0]
    TM = 512
    L = TM + HALO

    def body(z_ref, halo_ref, wmix_ref, scale_ref, pooled_ref, mixs_ref):
        i = pl.program_id(0)
        halo = jnp.where(i > 0, halo_ref[...], 0.0)
        zt = z_ref[...]
        e = jnp.concatenate([halo, zt], axis=0)
        sums = []
        s = e
        for k in (1, 2, 4, 8):
            s = s + pltpu.roll(s, k, 0)
            sums.append(s)
        inv = _inv_counts(i * TM, TM)
        for gi in range(len(POOL_WINDOWS)):
            cols = slice(gi * PG, (gi + 1) * PG)
            pooled = (sums[gi][HALO:, cols] * inv[gi] - zt[:, cols]).astype(BF16)
            pooled_ref[:, cols] = pooled
            mixs_ref[:, cols] = (_dot(pooled, wmix_ref[gi]) * scale_ref[:, cols]).astype(BF16)

    return pl.pallas_call(
        body, name="pool_fwd", grid=(T // TM,),
        in_specs=[pl.BlockSpec((TM, PW), lambda i: (i, 0)),
                  pl.BlockSpec((HALO, PW), lambda i: (jnp.maximum(i * (TM // HALO) - 1, 0), 0)),
                  _const_spec((len(POOL_WINDOWS), PG, PG)), _const_spec((1, PW))],
        out_specs=[pl.BlockSpec((TM, PW), lambda i: (i, 0)), pl.BlockSpec((TM, PW), lambda i: (i, 0))],
        out_shape=[SDS((T, PW), BF16), SDS((T, PW), BF16)],
        compiler_params=_cp(("arbitrary",)),
    )(z, z, wmix, scale)


def pool_bwd(dmixs, pooled, wmix, scale):
    T = dmixs.shape[0]
    TM = 512
    L = TM + HALO
    nt = T // TM

    def body(dm_ref, halo_ref, pooled_ref, wmix_ref, scale_ref, dz_ref, dwmix_ref, dscale_ref):
        i = pl.program_id(0)

        @pl.when(i == 0)
        def _():
            dwmix_ref[...] = jnp.zeros_like(dwmix_ref)
            dscale_ref[...] = jnp.zeros_like(dscale_ref)

        halo = jnp.where(i < nt - 1, halo_ref[...], 0.0)
        dm = dm_ref[...]
        e = jnp.concatenate([dm, halo], axis=0)
        inv = _inv_counts(i * TM, L)
        for gi in range(len(POOL_WINDOWS)):
            cols = slice(gi * PG, (gi + 1) * PG)
            w = wmix_ref[gi]
            dmixed = (e[:, cols] * scale_ref[:, cols]).astype(BF16)
            dpooled = _dot_nt(dmixed, w)
            pooled = pooled_ref[:, cols]
            mixed = _dot(pooled, w)
            dscale_ref[:, cols] += jnp.sum(dm[:, cols] * mixed, axis=0, keepdims=True)
            dwmix_ref[gi] += _dot_tn(pooled, dmixed[:TM])
            s = dpooled * inv[gi]
            k = 1
            while k < POOL_WINDOWS[gi]:
                s = s + pltpu.roll(s, L - k, 0)
                k *= 2
            dz_ref[:, cols] = (s[:TM] - dpooled[:TM]).astype(BF16)

    return pl.pallas_call(
        body, name="pool_bwd", grid=(nt,),
        in_specs=[pl.BlockSpec((TM, PW), lambda i: (i, 0)),
                  pl.BlockSpec((HALO, PW), lambda i: (jnp.minimum((i + 1) * (TM // HALO), T // HALO - 1), 0)),
                  pl.BlockSpec((TM, PW), lambda i: (i, 0)),
                  _const_spec((len(POOL_WINDOWS), PG, PG)), _const_spec((1, PW))],
        out_specs=[pl.BlockSpec((TM, PW), lambda i: (i, 0)),
                   pl.BlockSpec((len(POOL_WINDOWS), PG, PG), lambda i: (0, 0, 0)),
                   pl.BlockSpec((1, PW), lambda i: (0, 0))],
        out_shape=[SDS((T, PW), BF16), SDS((len(POOL_WINDOWS), PG, PG), F32), SDS((1, PW), F32)],
        compiler_params=_cp(("arbitrary",)),
    )(dmixs, dmixs, pooled, wmix, scale)


def merge_fwd(att, mixs, gt, h1, wattn, wpool, wout):
    T = h1.shape[0]
    TM = 512

    def body(att_ref, mixs_ref, gt_ref, h_ref, wa_ref, wp_ref, wo_ref, h2_ref, mg_ref, gf_ref):
        a = _dot(att_ref[...], wa_ref[...])
        p = _dot(mixs_ref[...], wp_ref[...])
        sa = jax.nn.sigmoid(gt_ref[:, 0:D].astype(F32))
        sp = jax.nn.sigmoid(gt_ref[:, D:2 * D].astype(F32))
        gf_ref[:, 0:D] = (a * sa * (1.0 - sa)).astype(BF16)
        gf_ref[:, D:2 * D] = (p * sp * (1.0 - sp)).astype(BF16)
        mg = (sa * a + sp * p).astype(BF16)
        mg_ref[...] = mg
        h2_ref[...] = h_ref[...] + _dot(mg, wo_ref[...])

    tile = lambda w: pl.BlockSpec((TM, w), lambda i: (i, 0))
    return pl.pallas_call(
        body, name="merge_fwd", grid=(T // TM,),
        in_specs=[tile(AW), tile(PW), tile(2 * D), tile(D), _const_spec((AW, D)), _const_spec((PW, D)),
                  _const_spec((D, D))],
        out_specs=[tile(D), tile(D), tile(2 * D)],
        out_shape=[SDS((T, D), F32), SDS((T, D), BF16), SDS((T, 2 * D), BF16)],
        compiler_params=_cp(("arbitrary",)),
    )(att, mixs, gt, h1, wattn, wpool, wout)


def merge_bwd(dh2, gt, gf, wattn, wpool, wout, after=None):
    T = dh2.shape[0]
    TM = 512
    token_spec, token_arg = _token_operand(after)

    def body(dh2_ref, gt_ref, gf_ref, wa_ref, wp_ref, wo_ref, *rest):
        datt_ref, dmixs_ref, dgt_ref, da_ref, dp_ref = rest[-5:]
        dm = _dot_nt(dh2_ref[...].astype(BF16), wo_ref[...])
        da = (dm * jax.nn.sigmoid(gt_ref[:, 0:D].astype(F32))).astype(BF16)
        dp = (dm * jax.nn.sigmoid(gt_ref[:, D:2 * D].astype(F32))).astype(BF16)
        da_ref[...] = da
        dp_ref[...] = dp
        dgt_ref[:, 0:D] = (dm * gf_ref[:, 0:D].astype(F32)).astype(BF16)
        dgt_ref[:, D:2 * D] = (dm * gf_ref[:, D:2 * D].astype(F32)).astype(BF16)
        datt_ref[...] = _dot_nt(da, wa_ref[...]).astype(BF16)
        dmixs_ref[...] = _dot_nt(dp, wp_ref[...])

    tile = lambda w: pl.BlockSpec((TM, w), lambda i: (i, 0))
    return pl.pallas_call(
        body, name="merge_bwd", grid=(T // TM,),
        in_specs=[tile(D), tile(2 * D), tile(2 * D), _const_spec((AW, D)), _const_spec((PW, D)),
                  _const_spec((D, D))] + token_spec,
        out_specs=[tile(AW), tile(PW), tile(2 * D), tile(D), tile(D)],
        out_shape=[SDS((T, AW), BF16), SDS((T, PW), F32), SDS((T, 2 * D), BF16), SDS((T, D), BF16),
                   SDS((T, D), BF16)],
        compiler_params=_cp(("arbitrary",)),
    )(dh2, gt, gf, wattn, wpool, wout, *token_arg)


def adamw(w, g, m, v, name):
    R, C = w.shape
    tile_bytes = 2 * 1024 * 1024
    tr = R
    if R * C * 4 > tile_bytes:
        tr = next(cand for cand in (512, 256, 128, 64, 32, 16, 8) if R % cand == 0 and cand * C * 4 <= tile_bytes)
    c1 = 1.0 - B1 ** STEP
    c2 = 1.0 - B2 ** STEP

    def body(w_ref, g_ref, m_ref, v_ref, go_ref, d_ref, nm_ref, nv_ref):
        gv = g_ref[...]
        go_ref[...] = gv
        nm = B1 * m_ref[...] + (1.0 - B1) * gv
        nv = B2 * v_ref[...] + (1.0 - B2) * (gv * gv)
        nm_ref[...] = nm
        nv_ref[...] = nv
        d_ref[...] = -LR * ((nm / c1) / (jnp.sqrt(nv / c2) + ADAM_EPS) + WD * w_ref[...])

    spec = pl.BlockSpec((tr, C), lambda i: (i, 0))
    return pl.pallas_call(
        body, name=name, grid=(R // tr,),
        in_specs=[spec] * 4, out_specs=[spec] * 4, out_shape=[SDS((R, C), F32)] * 4,
        compiler_params=_cp(("parallel",)),
    )(w, g, m, v)


GROUP_FFN1, GROUP_MIX, GROUP_FFN2 = 0, 1, 2


def _behind(small, token):
    return small if token is None else small + token[0:1, 0:1]


def local_fwd_bwd(x, target, S, comm):
    W = comm.weights(GROUP_FFN1, None)
    h1, ab1, hid1 = ffn_fwd(x, _behind(S["ffn1_norm"], comm.started()), W["ffn1_w_up"], W["ffn1_w_down"], "ffn1_fwd")
    W = comm.weights(GROUP_MIX, h1)
    u, q, kv, z, gt = mix_in_fwd(h1, _behind(S["mix_norm"], comm.started()), W["w_in"])
    att, lse = attn_fwd(q, kv, S["sinks"])
    pooled, mixs = pool_fwd(z, S["pool_w_mix"], S["pool_scale"])
    h2, merged, gf = merge_fwd(att, mixs, gt, h1, W["w_attn_up"], W["w_pool_up"], W["w_out"])
    W = comm.weights(GROUP_FFN2, h2)
    dh3, ab2, hid2, loss, g_final = ffn_fwd(h2, S["ffn2_norm"], W["ffn2_w_up"], W["ffn2_w_down"], "ffn2_fwd",
                                            head=(target, S["final_norm"]))

    G = {"final_norm": g_final}
    dh2, dup2, n2, G["ffn2_norm"] = ffn_bwd_x(dh3, h2, ab2, S["ffn2_norm"], W["ffn2_w_up"], W["ffn2_w_down"],
                                              "ffn2_bwd_x")
    ffn2 = ("ffn2_w_down", "ffn2_w_up")
    token = comm.grads({"ffn2_w_down": matmul_tn(hid2, dh3, "ffn2_dw_down", tm=1408, tn=512, tt=4096, b_scale=0.5),
                        "ffn2_w_up": matmul_tn(n2, dup2, "ffn2_dw_up", tm=D, tn=512, tt=4096)})

    W = comm.weights(GROUP_MIX, None)
    datt, dmixs, dgt, da, dp = merge_bwd(dh2, gt, gf, W["w_attn_up"], W["w_pool_up"], W["w_out"], after=token)
    token = comm.advance(ffn2, datt)
    g_mix = {"w_out": matmul_tn(merged, dh2, "dw_out", tm=D, tn=D, after=token),
             "w_attn_up": matmul_tn(att, da, "dw_attn_up", tm=AW, tn=D),
             "w_pool_up": matmul_tn(mixs, dp, "dw_pool_up", tm=PW, tn=D)}
    dz, G["pool_w_mix"], G["pool_scale"] = pool_bwd(dmixs, pooled, S["pool_w_mix"], _behind(S["pool_scale"], token))
    dq, dkv, G["sinks"] = attn_bwd(q, kv, datt, lse, _behind(S["sinks"], token))
    dh1, G["mix_norm"] = mix_in_bwd(dq, dkv, dz, dgt, dh2, h1, S["mix_norm"], W["w_in"])
    g_mix["w_in"] = jnp.concatenate([
        matmul_tn(dq, u, "dw_in_q", tm=AW, tn=D),
        matmul_tn(dkv, u, "dw_in_kv", tm=2 * KVW, tn=D),
        matmul_tn(dz, u, "dw_in_z", tm=PW, tn=D),
        matmul_tn(dgt, u, "dw_in_g", tm=D, tn=512, tt=4096),
    ], axis=0)
    mix = tuple(g_mix)
    token = comm.grads(g_mix)

    W = comm.weights(GROUP_FFN1, None)
    g_dn1 = matmul_tn(hid1, dh1, "ffn1_dw_down", tm=1408, tn=512, tt=4096, b_scale=0.5, after=token)
    token = comm.advance(mix, g_dn1)
    token2 = comm.grads({"ffn1_w_down": g_dn1})
    g1 = _behind(_behind(S["ffn1_norm"], token), token2)
    dx, dup1, n1, G["ffn1_norm"] = ffn_bwd_x(dh1, x, ab1, g1, W["ffn1_w_up"], W["ffn1_w_down"], "ffn1_bwd_x")
    token = comm.advance(("ffn1_w_down",), dx)
    token2 = comm.small_start(G, dx)
    g_up1 = matmul_tn(n1, dup1, "ffn1_dw_up", tm=D, tn=512, tt=4096,
                      after=None if token is None else token + token2)
    token = comm.grads({"ffn1_w_up": g_up1})
    done = comm.finish(mix, token)
    token = comm.advance(("ffn1_w_up",), done)
    done = comm.finish(("ffn1_w_down",), token)
    done = comm.small_finish(done)
    done = comm.finish(ffn2, done)
    comm.finish(("ffn1_w_up",), done)
    return loss, dx


HBM_SPEC = pl.BlockSpec(memory_space=pltpu.HBM)


def _me():
    return lax.axis_index("x"), lax.axis_index("y"), lax.axis_index("c")


def _peer_chip(x, y, k):
    return x ^ (k >> 1), y ^ (k & 1)


def _piece_half(ref, rowlike, j, h):
    if rowlike:
        return ref.at[j, h]
    ns = ref.shape[-1] // N_CHIPS
    return ref.at[h, :, pl.ds(pl.multiple_of(j * ns, 128), ns)]


def _piece(ref, rowlike, j):
    if rowlike:
        return ref.at[j]
    ns = ref.shape[-1] // N_CHIPS
    return ref.at[:, :, pl.ds(pl.multiple_of(j * ns, 128), ns)]


def _full_shape(shard_view, rowlike):
    _, kh, ns = shard_view.shape
    return (N_CHIPS, 2, kh, ns) if rowlike else (2, kh, N_CHIPS * ns)


def _remote(src, dst, send_sem, recv_sem, dev):
    return pltpu.make_async_remote_copy(src, dst, send_sem, recv_sem, device_id=dev, device_id_type=MESH)


SEM_SPEC = pl.BlockSpec(memory_space=pltpu.SEMAPHORE)
SPLIT_PARAMS = pltpu.CompilerParams(has_side_effects=pltpu.SideEffectType.DATAFLOW_SIDE_EFFECTING)


def _gather_plan(rowlikes):
    def plan(s_refs, f_refs, a, k, x, y, c):
        px, py = _peer_chip(x, y, k)
        return (s_refs[a].at[c], _piece_half(f_refs[a], rowlikes[a], 2 * x + y, c),
                _piece_half(f_refs[a], rowlikes[a], 2 * px + py, c), (px, py, c))
    return plan


def _all_to_all_plan(rowlikes):
    def plan(q_refs, r_refs, a, k, x, y, c):
        px, py = _peer_chip(x, y, k)
        if rowlikes[a]:
            src = q_refs[a].at[2 * px + py]
        else:
            ns = q_refs[a].shape[-1] // N_CHIPS
            src = q_refs[a].at[:, pl.ds(pl.multiple_of((2 * px + py) * ns, 128), ns)]
        return src, r_refs[a].at[k - 1], r_refs[a].at[k - 1], (px, py, c)
    return plan


def _swap_plan(rowlikes):
    def plan(g_refs, got_refs, a, k, x, y, c):
        src = g_refs[a].at[:, 1 - c] if rowlikes[a] else g_refs[a].at[1 - c]
        return src, got_refs[a], got_refs[a], (x, y, 1 - c)
    return plan


def _everyone_plan(s_refs, slot_refs, a, k, x, y, c):
    px, py, pc = x ^ (k >> 2), y ^ ((k >> 1) & 1), c ^ (k & 1)
    return s_refs[a], slot_refs[a].at[4 * x + 2 * y + c], slot_refs[a].at[4 * px + 2 * py + pc], (px, py, pc)


CHIPS, SIBLING, EVERYONE = (1, 2, 3), (1,), tuple(range(1, 8))


def _as_list(after):
    return [] if after is None else list(after) if isinstance(after, (list, tuple)) else [after]


def exchange_start(srcs, land_shapes, plan, after, name, peers=CHIPS):
    n = len(srcs)
    lands = [lax.empty(shape, s.dtype) for shape, s in zip(land_shapes, srcs)]

    behind = _as_list(after)

    def body(*refs):
        s_refs, l_refs = refs[:n], refs[n:2 * n]
        send_sems, recv_sems = refs[2 * n + len(behind)], refs[2 * n + len(behind) + 1]
        token = refs[-1]
        x, y, c = _me()
        for a in range(n):
            for i, k in enumerate(peers):
                src, dst, _, peer = plan(s_refs, l_refs, a, k, x, y, c)
                sem = len(peers) * a + i
                _remote(src, dst, send_sems.at[sem], recv_sems.at[sem], peer).start()
        token[...] = jnp.zeros_like(token)

    n_sems = len(peers) * n
    outs = pl.pallas_call(
        body, name=name, in_specs=[HBM_SPEC] * (2 * n) + [pl.BlockSpec(memory_space=pl.ANY)] * len(behind),
        out_specs=[SEM_SPEC, SEM_SPEC] + [HBM_SPEC] * (2 * n) + [pl.BlockSpec(memory_space=pltpu.VMEM)],
        out_shape=[pltpu.SemaphoreType.DMA((n_sems,)), pltpu.SemaphoreType.DMA((n_sems,))]
        + [pltpu.HBM(a.shape, a.dtype) for a in (*srcs, *lands)] + [SDS((8, 128), F32)],
        input_output_aliases={i: 2 + i for i in range(2 * n)},
        compiler_params=SPLIT_PARAMS,
    )(*[pltpu.with_memory_space_constraint(a, pltpu.HBM) for a in (*srcs, *lands)], *behind)
    return {"sems": outs[:2], "srcs": outs[2:2 + n], "lands": outs[2 + n:2 + 2 * n], "token": outs[-1]}


def exchange_wait(state, plan, after, name, peers=CHIPS):
    n = len(state["srcs"])
    behind = _as_list(after)

    def body(*refs):
        s_refs, l_refs = refs[:n], refs[n:2 * n]
        send_sems, recv_sems = refs[2 * n], refs[2 * n + 1]
        x, y, c = _me()
        for a in range(n):
            for i, k in enumerate(peers):
                src, _, landing, peer = plan(s_refs, l_refs, a, k, x, y, c)
                sem = len(peers) * a + i
                cp = _remote(src, landing, send_sems.at[sem], recv_sems.at[sem], peer)
                cp.wait_send()
                cp.wait_recv()

    bufs = (*state["srcs"], *state["lands"])
    outs = pl.pallas_call(
        body, name=name,
        in_specs=[HBM_SPEC] * (2 * n) + [SEM_SPEC, SEM_SPEC] + [pl.BlockSpec(memory_space=pl.ANY)] * len(behind),
        out_specs=[HBM_SPEC] * (2 * n),
        out_shape=[pltpu.HBM(a.shape, a.dtype) for a in bufs],
        input_output_aliases={i: i for i in range(2 * n)},
        compiler_params=SPLIT_PARAMS,
    )(*bufs, *state["sems"], *behind)
    return outs[:n], outs[n:]


def gather_finish(shards, fulls, rowlikes, name):
    n = len(shards)

    def body(*refs):
        s_refs, f_refs = refs[:n], refs[2 * n:3 * n]
        send_sems, recv_sems = refs[3 * n:]
        x, y, c = _me()
        chip = 2 * x + y
        sib = (x, y, 1 - c)
        sends = []
        for a in range(n):
            own = _piece(f_refs[a], rowlikes[a], chip)
            cp = _remote(s_refs[a], own, send_sems.at[a, 0], recv_sems.at[a, 0], sib)
            cp.start()
            sends.append(cp)
            for k in (1, 2, 3):
                px, py = _peer_chip(x, y, k)
                slot = _piece_half(f_refs[a], rowlikes[a], 2 * px + py, c)
                cp = _remote(slot, slot, send_sems.at[a, k], recv_sems.at[a, k], sib)
                cp.start()
                sends.append(cp)
        for a in range(n):
            own = _piece(f_refs[a], rowlikes[a], chip)
            _remote(own, own, send_sems.at[a, 0], recv_sems.at[a, 0], sib).wait_recv()
            for k in (1, 2, 3):
                px, py = _peer_chip(x, y, k)
                slot = _piece_half(f_refs[a], rowlikes[a], 2 * px + py, 1 - c)
                _remote(slot, slot, send_sems.at[a, k], recv_sems.at[a, k], sib).wait_recv()
        for cp in sends:
            cp.wait_send()

    return pl.pallas_call(
        body, name=name, in_specs=[HBM_SPEC] * (2 * n), out_specs=[HBM_SPEC] * n,
        out_shape=[SDS(f.shape, f.dtype) for f in fulls],
        input_output_aliases={n + a: a for a in range(n)},
        scratch_shapes=[pltpu.SemaphoreType.DMA((n, 4)), pltpu.SemaphoreType.DMA((n, 4))],
    )(*shards, *fulls)


def _half_buffer_shape(gview, rowlike):
    return (N_CHIPS,) + gview.shape[2:] if rowlike else gview.shape[1:]


ROW_TILES = 2


def _piece_specs(rowlike, kh, ns, piece, half):
    tr = kh // ROW_TILES
    if rowlike:
        return (pl.BlockSpec((None, None, tr, ns), lambda *g: (piece(*g), half(*g), g[-2], 0)),
                pl.BlockSpec((None, tr, ns), lambda *g: (piece(*g), g[-2], 0)))
    return (pl.BlockSpec((None, tr, ns), lambda *g: (half(*g), g[-2], piece(*g))),
            pl.BlockSpec((tr, ns), lambda *g: (g[-2], piece(*g))))


def add_halves(gview, got, rowlike, place, name):
    kh, ns = (gview.shape[2], gview.shape[3]) if rowlike else (gview.shape[1], gview.shape[2] // N_CHIPS)

    def body(place_ref, g_ref, got_ref, o_ref):
        o_ref[...] = (g_ref[...].astype(F32) + got_ref[...].astype(F32)).astype(BF16)

    g_spec, h_spec = _piece_specs(rowlike, kh, ns, lambda j, r, p: j, lambda j, r, p: p[0])
    return pl.pallas_call(
        body, name=name,
        grid_spec=pltpu.PrefetchScalarGridSpec(num_scalar_prefetch=1, grid=(N_CHIPS, ROW_TILES),
                                               in_specs=[g_spec, h_spec], out_specs=h_spec),
        out_shape=SDS(got.shape, BF16),
        compiler_params=_cp(("parallel", "parallel")),
    )(place, gview, got)


def _slot_shape(q, rowlike):
    return (3,) + (q.shape[1:] if rowlike else (q.shape[0], q.shape[1] // N_CHIPS))


def sum_pieces(q, recv, rowlike, place, name):
    kh, ns = recv.shape[1], recv.shape[2]
    tr = kh // ROW_TILES

    def body(place_ref, q_ref, r_ref, o_ref):
        acc = q_ref[...].astype(F32)
        for k in range(3):
            acc = acc + r_ref[k].astype(F32)
        o_ref[...] = acc

    _, h_spec = _piece_specs(rowlike, kh, ns, lambda z, r, p: p[1], lambda z, r, p: p[0])
    return pl.pallas_call(
        body, name=name,
        grid_spec=pltpu.PrefetchScalarGridSpec(
            num_scalar_prefetch=1, grid=(1, ROW_TILES),
            in_specs=[h_spec, pl.BlockSpec((3, tr, ns), lambda z, r, p: (0, r, 0))],
            out_specs=pl.BlockSpec((None, tr, ns), lambda z, r, p: (p[0], r, 0))),
        out_shape=SDS((2, kh, ns), F32),
        compiler_params=_cp(("parallel", "parallel")),
    )(place, q, recv)


def join_halves(halves, name):
    n = len(halves)

    def body(*refs):
        o_refs = refs[n:2 * n]
        send_sems, recv_sems = refs[2 * n:]
        x, y, c = _me()
        sib = (x, y, 1 - c)
        sends = []
        for a in range(n):
            cp = _remote(o_refs[a].at[c], o_refs[a].at[c], send_sems.at[a], recv_sems.at[a], sib)
            cp.start()
            sends.append(cp)
        for a in range(n):
            got = o_refs[a].at[1 - c]
            _remote(got, got, send_sems.at[a], recv_sems.at[a], sib).wait_recv()
        for cp in sends:
            cp.wait_send()

    return pl.pallas_call(
        body, name=name, in_specs=[HBM_SPEC] * n, out_specs=[HBM_SPEC] * n,
        out_shape=[SDS(h.shape, h.dtype) for h in halves],
        input_output_aliases={a: a for a in range(n)},
        scratch_shapes=[pltpu.SemaphoreType.DMA((n,)), pltpu.SemaphoreType.DMA((n,))],
    )(*halves)


N_DEV = 8


def sum_devices(s, slots, me):
    R, Wd = s.shape

    def body(me_ref, s_ref, slots_ref, out_ref):
        acc = None
        for d in range(N_DEV):
            mine = me_ref[0] == d
            term = jnp.where(mine, s_ref[...], slots_ref[jnp.where(mine, d ^ 1, d)])
            acc = term if acc is None else acc + term
        out_ref[...] = acc

    vmem = pl.BlockSpec(memory_space=pltpu.VMEM)
    return pl.pallas_call(
        body, name="sum_devices", in_specs=[pl.BlockSpec(memory_space=pltpu.SMEM), vmem, vmem], out_specs=vmem,
        out_shape=SDS((R, Wd), F32),
    )(me, s, slots)


TRANSPOSED = ("w_in",)
BIG = {"ffn1_w_up": (D, 2 * FF, "col"), "ffn1_w_down": (FF, D, "row"), "w_in": (INW, D, "row"),
       "w_attn_up": (AW, D, "row"), "w_pool_up": (PW, D, "col"), "w_out": (D, D, "row"),
       "ffn2_w_up": (D, 2 * FF, "col"), "ffn2_w_down": (FF, D, "row")}
GROUPS = (("ffn1_w_up", "ffn1_w_down"), ("w_in", "w_attn_up", "w_pool_up", "w_out"), ("ffn2_w_up", "ffn2_w_down"))
SMALL = ("ffn1_norm", "mix_norm", "ffn2_norm", "final_norm", "pool_scale", "sinks", "pool_w_mix")
SMALL_W = 128


def _rowlike(name):
    return BIG[name][2] == "row"


def _half_dims(name):
    k, n, kind = BIG[name]
    return (k // N_CHIPS // 2, n) if kind == "row" else (k // 2, n // N_CHIPS)


def shard_view(name, shard):
    return shard.reshape((2,) + _half_dims(name))


def full_from_view(name, fv):
    k, n, _ = BIG[name]
    return fv.reshape(k, n)


def grad_view(name, g):
    kh, ns = _half_dims(name)
    return g.reshape(_full_shape(jax.ShapeDtypeStruct((2, kh, ns), g.dtype), _rowlike(name)))


def pack_small(d):
    parts = []
    for name in SMALL:
        a = d[name].reshape(-1)
        pad = (-a.shape[0]) % SMALL_W
        parts.append(jnp.pad(a, (0, pad)).reshape(-1, SMALL_W))
    a = jnp.concatenate(parts, axis=0)
    return jnp.pad(a, ((0, (-a.shape[0]) % 8), (0, 0)))


def unpack_small(a, like):
    out, r0 = {}, 0
    for name in SMALL:
        size = int(np.prod(like[name].shape))
        rows = -(-size // SMALL_W)
        out[name] = a[r0:r0 + rows].reshape(-1)[:size].reshape(like[name].shape)
        r0 += rows
    return out


WEIGHTS = ("ffn1_norm", "ffn1_w_up", "ffn1_w_down", "mix_norm", "w_in", "sinks", "w_attn_up", "pool_w_mix",
           "pool_scale", "w_pool_up", "w_out", "ffn2_norm", "ffn2_w_up", "ffn2_w_down", "final_norm")


def kernel(x, ffn1_norm, ffn1_w_up, ffn1_w_down, mix_norm, w_in, sinks, w_attn_up, pool_w_mix, pool_scale, w_pool_up, w_out, ffn2_norm, ffn2_w_up, ffn2_w_down, final_norm, loss_target, m_ffn1_norm, m_ffn1_w_up, m_ffn1_w_down, m_mix_norm, m_w_in, m_sinks, m_w_attn_up, m_pool_w_mix, m_pool_scale, m_w_pool_up, m_w_out, m_ffn2_norm, m_ffn2_w_up, m_ffn2_w_down, m_final_norm, v_ffn1_norm, v_ffn1_w_up, v_ffn1_w_down, v_mix_norm, v_w_in, v_sinks, v_w_attn_up, v_pool_w_mix, v_pool_scale, v_w_pool_up, v_w_out, v_ffn2_norm, v_ffn2_w_up, v_ffn2_w_down, v_final_norm):
    given = dict(locals())
    w = {n: given[n] for n in WEIGHTS}
    m = {n: given["m_" + n] for n in WEIGHTS}
    v = {n: given["v_" + n] for n in WEIGHTS}
    cx, cy, cc = _me()
    place = jnp.stack([cc, 2 * cx + cy]).astype(jnp.int32)

    def local2d(d, n):
        return d[n][0].T if n in TRANSPOSED else d[n][0]

    shards = {n: local2d(w, n) for n in BIG}
    sviews = {n: shard_view(n, shards[n].astype(BF16)) for n in BIG}
    grads, delta, new_m, new_v = {}, {}, {}, {}
    rowlikes = [[_rowlike(n) for n in names] for names in GROUPS]

    class Exchanges:
        def __init__(self):
            self.gathers, self.fulls, self.reductions, self.small = {}, {}, {}, None

        def _start_gather(self, group, after):
            sv = [sviews[n] for n in GROUPS[group]]
            rl = rowlikes[group]
            self.gathers[group] = exchange_start(sv, [_full_shape(s, r) for s, r in zip(sv, rl)], _gather_plan(rl),
                                                 after, f"gather_start_{group}")

        def weights(self, group, after):
            if not self.gathers:
                self._start_gather(0, None)
            if group not in self.fulls:
                names, rl, state = GROUPS[group], rowlikes[group], self.gathers[group]
                sv, fulls = exchange_wait(state, _gather_plan(rl), state["token"] if after is None else after,
                                          f"gather_wait_{group}")
                fulls = gather_finish(sv, fulls, rl, f"gather_finish_{group}")
                self.fulls[group] = {n: full_from_view(n, f) for n, f in zip(names, fulls)}
                if group + 1 < len(GROUPS):
                    self._start_gather(group + 1, fulls[0])
            return self.fulls[group]

        def started(self):
            return self.gathers[max(self.gathers)]["token"]

        def grads(self, g):
            names = tuple(g)
            rl = [_rowlike(n) for n in names]
            gv = [grad_view(n, g[n]) for n in names]
            state = exchange_start(gv, [_half_buffer_shape(a, r) for a, r in zip(gv, rl)], _swap_plan(rl), None,
                                   "swap_start_" + names[0], SIBLING)
            self.reductions[names] = state
            return state["token"]

        def advance(self, names, after):
            rl = [_rowlike(n) for n in names]
            gv, gots = exchange_wait(self.reductions[names], _swap_plan(rl), after, "swap_wait_" + names[0], SIBLING)
            qs = [add_halves(a, b, r, place, "add_halves_" + n) for a, b, r, n in zip(gv, gots, rl, names)]
            state = exchange_start(qs, [_slot_shape(q, r) for q, r in zip(qs, rl)], _all_to_all_plan(rl), None,
                                   "all_to_all_start_" + names[0])
            self.reductions[names] = state
            return state["token"]

        def finish(self, names, after):
            rl = [_rowlike(n) for n in names]
            qs, recvs = exchange_wait(self.reductions.pop(names), _all_to_all_plan(rl), after,
                                      "all_to_all_wait_" + names[0])
            halves = [sum_pieces(q, r, k, place, "sum_pieces_" + n) for q, r, k, n in zip(qs, recvs, rl, names)]
            for n, o in zip(names, join_halves(halves, "join_halves_" + names[0])):
                grads[n], delta[n], new_m[n], new_v[n] = adamw(shards[n], o.reshape(shards[n].shape), local2d(m, n),
                                                               local2d(v, n), "adamw_" + n)
            return [new_v[n] for n in names]

        def small_start(self, G, after):
            packed = pack_small({n: G[n] for n in SMALL})
            self.small = exchange_start([packed], [(N_DEV,) + packed.shape], _everyone_plan, after,
                                        "small_start", EVERYONE)
            return self.small["token"]

        def small_finish(self, after):
            (packed,), (slots,) = exchange_wait(self.small, _everyone_plan, after, "small_wait", EVERYONE)
            total = sum_devices(packed, slots, (4 * cx + 2 * cy + cc).astype(jnp.int32).reshape(1))
            g, ds, ms, vs = adamw(pack_small(w), total, pack_small(m), pack_small(v), "adamw_small")
            for d, packed_d in ((grads, g), (delta, ds), (new_m, ms), (new_v, vs)):
                d.update(unpack_small(packed_d, small_like))
            return vs

    small_like = {n: w[n] for n in SMALL}
    S = {n: w[n].reshape(1, -1) for n in ("ffn1_norm", "mix_norm", "ffn2_norm", "final_norm", "pool_scale", "sinks")}
    S["pool_w_mix"] = w["pool_w_mix"][0].astype(BF16)
    loss, dx = local_fwd_bwd(x[0], loss_target[0], S, Exchanges())
    loss = lax.psum(loss[0, 0], ("x", "y", "c"))

    def shaped(d, n):
        return (d[n].T if n in TRANSPOSED else d[n]).reshape(w[n].shape)

    return (loss, dx[None], *[shaped(grads, n) for n in WEIGHTS], *[shaped(delta, n) for n in WEIGHTS],
            *[shaped(new_m, n) for n in WEIGHTS], *[shaped(new_v, n) for n in WEIGHTS])
```

```python
import numpy as np
import jax
import jax.numpy as jnp
from jax import lax
from jax.experimental import pallas as pl
from jax.experimental.pallas import tpu as pltpu

F32 = jnp.float32
BF16 = jnp.bfloat16
SDS = jax.ShapeDtypeStruct
MESH = pl.DeviceIdType.MESH

D = 1024
FF = 2816
NQ = 16
NKV = 2
HD = 64
GQ = NQ // NKV
AW = NQ * HD
KVW = NKV * HD
BLK = 128
PW = 512
PG = 128
POOL_WINDOWS = (2, 4, 8, 16)
HALO = 16
INW = AW + 2 * KVW + PW + 2 * D
C_KV = AW
C_Z = AW + 2 * KVW
C_G = C_Z + PW
EPS = 1e-6
FF_CHUNK = 256
FF_CHUNKS = tuple((c, FF_CHUNK) for c in range(0, FF, FF_CHUNK))
SLOPES = tuple(float(2.0 ** (-8.0 * h / NQ)) for h in range(1, NQ + 1))
SCALE = HD ** -0.5

LR, B1, B2, ADAM_EPS, WD, STEP = 0.001, 0.9, 0.999, 1e-08, 0.01, 10

VMEM_LIMIT = 56 * 1024 * 1024
N_CHIPS = 4

NT = (((1,), (1,)), ((), ()))
TN = (((0,), (0,)), ((), ()))


def _cp(sem=None, vmem=VMEM_LIMIT):
    return pltpu.CompilerParams(dimension_semantics=sem, vmem_limit_bytes=vmem)


def _const_spec(shape):
    nd = len(shape)
    return pl.BlockSpec(shape, lambda *_: (0,) * nd, pipeline_mode=pl.Buffered(1))


def _rstd(x):
    return lax.rsqrt(jnp.mean(x * x, axis=-1, keepdims=True) + EPS)


def _rms_bwd(dn, xhat, rstd, g):
    dxhat = dn * g
    return rstd * (dxhat - xhat * jnp.mean(dxhat * xhat, axis=-1, keepdims=True))


def _dot(a, b):
    return jnp.dot(a, b, preferred_element_type=F32)


def _dot_nt(a, b):
    return lax.dot_general(a, b, NT, preferred_element_type=F32)


def _dot_tn(a, b):
    return lax.dot_general(a, b, TN, preferred_element_type=F32)


def ffn_fwd(h, g, wup, wdn, name, head=None):
    T = h.shape[0]
    TM = 512
    tile = lambda w: pl.BlockSpec((TM, w), lambda i: (i, 0))
    acc_spec = lambda w: pl.BlockSpec((1, w), lambda i: (0, 0))

    def body(h_ref, g_ref, wup_ref, wdn_ref, *rest):
        out_ref, ab_ref, hid_ref = rest[-3:] if head is None else rest[2:5]
        x = h_ref[...]
        n = (x * _rstd(x) * g_ref[...]).astype(BF16)
        for c0, w in FF_CHUNKS:
            a = _dot(n, wup_ref[:, c0:c0 + w])
            b = _dot(n, wup_ref[:, FF + c0:FF + c0 + w])
            sig = jax.nn.sigmoid(a)
            s = a * sig
            ab_ref[:, c0:c0 + w] = (b * (sig * (1.0 + a * (1.0 - sig)))).astype(BF16)
            ab_ref[:, FF + c0:FF + c0 + w] = s.astype(BF16)
            hid_ref[:, c0:c0 + w] = (s * b).astype(BF16)
        out = x + 0.5 * _dot(hid_ref[...], wdn_ref[...])
        if head is None:
            out_ref[...] = out
        else:
            t_ref, gf_ref, loss_ref, dgf_ref = rest[0], rest[1], rest[5], rest[6]
            out_ref[...] = _loss_head(out, t_ref[...], gf_ref[...], loss_ref, dgf_ref, pl.program_id(0) == 0)

    head_in, head_specs, head_out_specs, head_out_shape = [], [], [], []
    if head is not None:
        head_in, head_specs = list(head), [tile(D), _const_spec((1, D))]
        head_out_specs, head_out_shape = [acc_spec(1), acc_spec(D)], [SDS((1, 1), F32), SDS((1, D), F32)]
    return pl.pallas_call(
        body, name=name, grid=(T // TM,),
        in_specs=[tile(D), _const_spec((1, D)), _const_spec((D, 2 * FF)), _const_spec((FF, D))] + head_specs,
        out_specs=[tile(D), tile(2 * FF), tile(FF)] + head_out_specs,
        out_shape=[SDS((T, D), F32), SDS((T, 2 * FF), BF16), SDS((T, FF), BF16)] + head_out_shape,
        compiler_params=_cp(("arbitrary",)),
    )(h, g, wup, wdn, *head_in)


def _loss_head(x, target, g, loss_ref, dg_ref, first):
    @pl.when(first)
    def _():
        loss_ref[...] = jnp.zeros_like(loss_ref)
        dg_ref[...] = jnp.zeros_like(dg_ref)

    rstd = _rstd(x)
    xhat = x * rstd
    err = xhat * g - target
    loss_ref[...] += 0.5 * jnp.sum(jnp.mean(err * err, axis=-1, keepdims=True), axis=0, keepdims=True)
    dy = err * (1.0 / D)
    dg_ref[...] += jnp.sum(dy * xhat, axis=0, keepdims=True)
    return _rms_bwd(dy, xhat, rstd, g)


def ffn_bwd_x(dh, h_in, ab, g, wup, wdn, name):
    T = dh.shape[0]
    TM = 256

    def body(dh_ref, h_ref, ab_ref, g_ref, wup_ref, wdn_ref, dhin_ref, dup_ref, n_ref, dg_ref):
        x = h_ref[...]
        g = g_ref[...]
        rstd = _rstd(x)
        xhat = x * rstd
        n_ref[...] = (xhat * g).astype(BF16)
        dh = dh_ref[...]
        dhh = (0.5 * dh).astype(BF16)
        for c0, w in FF_CHUNKS:
            dhid = _dot_nt(dhh, wdn_ref[c0:c0 + w, :])
            dup_ref[:, c0:c0 + w] = (dhid * ab_ref[:, c0:c0 + w].astype(F32)).astype(BF16)
            dup_ref[:, FF + c0:FF + c0 + w] = (dhid * ab_ref[:, FF + c0:FF + c0 + w].astype(F32)).astype(BF16)
        dn = _dot_nt(dup_ref[...], wup_ref[...])
        dhin_ref[...] = dh + _rms_bwd(dn, xhat, rstd, g)

        @pl.when(pl.program_id(0) == 0)
        def _():
            dg_ref[...] = jnp.zeros_like(dg_ref)

        dg_ref[...] += jnp.sum(dn * xhat, axis=0, keepdims=True)

    tile = lambda w: pl.BlockSpec((TM, w), lambda i: (i, 0))
    return pl.pallas_call(
        body, name=name, grid=(T // TM,),
        in_specs=[tile(D), tile(D), tile(2 * FF), _const_spec((1, D)), _const_spec((D, 2 * FF)), _const_spec((FF, D))],
        out_specs=[tile(D), tile(2 * FF), tile(D), pl.BlockSpec((1, D), lambda i: (0, 0))],
        out_shape=[SDS((T, D), F32), SDS((T, 2 * FF), BF16), SDS((T, D), BF16), SDS((1, D), F32)],
        compiler_params=_cp(("arbitrary",)),
    )(dh, h_in, ab, g, wup, wdn)


TOKEN_SPEC = pl.BlockSpec((8, 128), lambda *_: (0, 0))


def _token_operand(token):
    return ([], []) if token is None else ([TOKEN_SPEC], [token])


def matmul_tn(a, b, name, *, tm, tn, tt=1024, b_scale=None, after=None):
    T, M = a.shape
    N = b.shape[1]
    tt = min(tt, T)
    assert M % tm == 0 and N % tn == 0 and T % tt == 0
    nt = T // tt
    token_spec, token_arg = _token_operand(after)

    def body(a_ref, b_ref, *rest):
        o_ref, acc_ref = rest[-2:]
        t = pl.program_id(2)

        @pl.when(t == 0)
        def _():
            acc_ref[...] = jnp.zeros_like(acc_ref)

        bv = b_ref[...]
        if b_scale is not None:
            bv = bv * b_scale
        acc_ref[...] += _dot_tn(a_ref[...].astype(BF16), bv.astype(BF16))

        @pl.when(t == nt - 1)
        def _():
            o_ref[...] = acc_ref[...].astype(BF16)

    return pl.pallas_call(
        body, name=name, grid=(M // tm, N // tn, nt),
        in_specs=[pl.BlockSpec((tt, tm), lambda i, j, t: (t, i)), pl.BlockSpec((tt, tn), lambda i, j, t: (t, j))]
        + token_spec,
        out_specs=pl.BlockSpec((tm, tn), lambda i, j, t: (i, j)),
        out_shape=SDS((M, N), BF16),
        scratch_shapes=[pltpu.VMEM((tm, tn), F32)],
        compiler_params=_cp(("parallel", "parallel", "arbitrary")),
    )(a, b, *token_arg)


def mix_in_fwd(h1, g, win_t):
    T = h1.shape[0]
    TM = 512

    def body(h_ref, g_ref, w_ref, u_ref, q_ref, kv_ref, z_ref, gt_ref):
        x = h_ref[...]
        u = (x * _rstd(x) * g_ref[...]).astype(BF16)
        u_ref[...] = u
        for c in range(0, AW, 256):
            q_ref[:, c:c + 256] = _dot_nt(u, w_ref[c:c + 256, :]).astype(BF16)
        kv_ref[...] = _dot_nt(u, w_ref[C_KV:C_Z, :]).astype(BF16)
        for c in range(0, PW, 256):
            z_ref[:, c:c + 256] = _dot_nt(u, w_ref[C_Z + c:C_Z + c + 256, :])
        for c in range(0, 2 * D, 256):
            gt_ref[:, c:c + 256] = _dot_nt(u, w_ref[C_G + c:C_G + c + 256, :]).astype(BF16)

    tile = lambda w: pl.BlockSpec((TM, w), lambda i: (i, 0))
    return pl.pallas_call(
        body, name="mix_in_fwd", grid=(T // TM,),
        in_specs=[tile(D), _const_spec((1, D)), _const_spec((INW, D))],
        out_specs=[tile(D), tile(AW), tile(2 * KVW), tile(PW), tile(2 * D)],
        out_shape=[SDS((T, D), BF16), SDS((T, AW), BF16), SDS((T, 2 * KVW), BF16), SDS((T, PW), F32),
                   SDS((T, 2 * D), BF16)],
        compiler_params=_cp(("arbitrary",)),
    )(h1, g, win_t)


def mix_in_bwd(dq, dkv, dz, dgt, dh2, h1, g, win_t):
    T = h1.shape[0]
    TM = 512

    def body(dq_ref, dkv_ref, dz_ref, dgt_ref, dh2_ref, h_ref, g_ref, w_ref, dh1_ref, dg_ref):
        du = _dot(dq_ref[...], w_ref[0:AW, :])
        du += _dot(dkv_ref[...], w_ref[C_KV:C_Z, :])
        du += _dot(dz_ref[...], w_ref[C_Z:C_G, :])
        du += _dot(dgt_ref[...], w_ref[C_G:INW, :])
        x = h_ref[...]
        g = g_ref[...]
        rstd = _rstd(x)
        xhat = x * rstd
        dh1_ref[...] = dh2_ref[...] + _rms_bwd(du, xhat, rstd, g)

        @pl.when(pl.program_id(0) == 0)
        def _():
            dg_ref[...] = jnp.zeros_like(dg_ref)

        dg_ref[...] += jnp.sum(du * xhat, axis=0, keepdims=True)

    tile = lambda w: pl.BlockSpec((TM, w), lambda i: (i, 0))
    return pl.pallas_call(
        body, name="mix_in_bwd", grid=(T // TM,),
        in_specs=[tile(AW), tile(2 * KVW), tile(PW), tile(2 * D), tile(D), tile(D), _const_spec((1, D)),
                  _const_spec((INW, D))],
        out_specs=[tile(D), pl.BlockSpec((1, D), lambda i: (0, 0))],
        out_shape=[SDS((T, D), F32), SDS((1, D), F32)],
        compiler_params=_cp(("arbitrary",)),
    )(dq, dkv, dz, dgt, dh2, h1, g, win_t)


PAIR = 2 * HD
NPAIR = GQ // 2


def _lo_lanes():
    return lax.broadcasted_iota(jnp.int32, (BLK, PAIR), 1) < HD


def _stack_heads(ref, kvh, scale=None):
    lo = _lo_lanes()
    parts = []
    for pr in range(NPAIR):
        t = ref[:, (kvh * NPAIR + pr) * PAIR:(kvh * NPAIR + pr + 1) * PAIR]
        if scale is not None:
            t = t * scale
        zero = jnp.zeros_like(t)
        parts += [jnp.where(lo, t, zero), jnp.where(lo, zero, t)]
    return jnp.concatenate(parts, axis=0)


def _kv_tiles(kvc_ref, kvp_ref, tile, kvh):
    lo = _lo_lanes()
    dup, left, right = [], [], []
    for ref in (kvp_ref, kvc_ref):
        t = ref[:, tile * PAIR:(tile + 1) * PAIR]
        r = pltpu.roll(t.astype(F32), HD, 1).astype(BF16)
        zero = jnp.zeros_like(t)
        a, b = (t, r) if kvh == 0 else (r, t)
        dup.append(jnp.where(lo, a, b))
        left.append(jnp.where(lo, a, zero))
        right.append(jnp.where(lo, zero, b))
    cat = lambda xs: jnp.concatenate(xs, axis=0)
    return cat(dup), cat(left), cat(right)


def _band_consts(first):
    row = lax.broadcasted_iota(jnp.int32, (BLK, BLK), 0)
    col = lax.broadcasted_iota(jnp.int32, (BLK, BLK), 1)
    upper = col > row
    dist = jnp.where(upper, row - col + BLK, row - col).astype(F32)
    pen = jnp.where(jnp.logical_and(upper, first), -jnp.inf, 0.0)
    return upper, dist, pen


def _split_band(upper, t):
    zero = jnp.zeros_like(t)
    return jnp.concatenate([jnp.where(upper, t, zero), jnp.where(upper, zero, t)], axis=1)


def attn_fwd(q, kv, sinks):
    T = q.shape[0]
    nb = T // BLK

    def body(sink_ref, q_ref, kvc_ref, kvp_ref, att_ref, lse_ref):
        upper, dist, pen = _band_consts(pl.program_id(0) == 0)
        for kvh in range(NKV):
            kdup, _, _ = _kv_tiles(kvc_ref, kvp_ref, 0, kvh)
            _, vleft, vright = _kv_tiles(kvc_ref, kvp_ref, 1, kvh)
            s_all = _dot_nt(_stack_heads(q_ref, kvh, SCALE), kdup)
            for pr in range(NPAIR):
                out = None
                for side, vpad in ((0, vleft), (1, vright)):
                    g = 2 * pr + side
                    hq = kvh * GQ + g
                    sink = sink_ref[0, hq]
                    rows = slice(g * BLK, (g + 1) * BLK)
                    s = jnp.where(upper, s_all[rows, 0:BLK], s_all[rows, BLK:2 * BLK]) - SLOPES[hq] * dist + pen
                    m = jnp.maximum(jnp.max(s, axis=-1, keepdims=True), sink)
                    p = jnp.exp(s - m)
                    l = jnp.sum(p, axis=-1, keepdims=True) + jnp.exp(sink - m)
                    lse_ref[:, hq:hq + 1] = m + jnp.log(l)
                    o = _dot(_split_band(upper, (p * (1.0 / l)).astype(BF16)), vpad)
                    out = o if out is None else out + o
                col0 = (kvh * NPAIR + pr) * PAIR
                att_ref[:, col0:col0 + PAIR] = out.astype(BF16)

    return pl.pallas_call(
        body, name="attn_fwd", grid=(nb,),
        in_specs=[pl.BlockSpec(memory_space=pltpu.SMEM),
                  pl.BlockSpec((BLK, AW), lambda i: (i, 0)),
                  pl.BlockSpec((BLK, 2 * KVW), lambda i: (i, 0)),
                  pl.BlockSpec((BLK, 2 * KVW), lambda i: (jnp.maximum(i - 1, 0), 0))],
        out_specs=[pl.BlockSpec((BLK, AW), lambda i: (i, 0)), pl.BlockSpec((BLK, NQ), lambda i: (i, 0))],
        out_shape=[SDS((T, AW), BF16), SDS((T, NQ), F32)],
        compiler_params=_cp(("arbitrary",)),
    )(sinks, q, kv, kv)


def attn_bwd(q, kv, datt, lse, sinks):
    T = q.shape[0]
    nb = T // BLK

    def body(sink_ref, q_ref, kvc_ref, kvp_ref, do_ref, lse_ref, dq_ref, dkv_ref, dsink_ref, carry_ref):
        i = pl.program_id(0)

        @pl.when(i == 0)
        def _():
            dsink_ref[...] = jnp.zeros_like(dsink_ref)
            carry_ref[...] = jnp.zeros_like(carry_ref)

        @pl.when(i < nb)
        def _():
            upper, dist, pen = _band_consts(i == 0)
            lo = _lo_lanes()
            dk_dup, dv_dup = [], []
            for kvh in range(NKV):
                kdup, kleft, kright = _kv_tiles(kvc_ref, kvp_ref, 0, kvh)
                vdup, _, _ = _kv_tiles(kvc_ref, kvp_ref, 1, kvh)
                qs = _stack_heads(q_ref, kvh, SCALE)
                dos = _stack_heads(do_ref, kvh)
                s_all = _dot_nt(qs, kdup)
                dp_all = _dot_nt(dos, vdup)
                ds_parts, p_parts = [], []
                for pr in range(NPAIR):
                    dq = None
                    for side, kpad in ((0, kleft), (1, kright)):
                        g = 2 * pr + side
                        hq = kvh * GQ + g
                        lse_h = lse_ref[:, hq:hq + 1]
                        rows = slice(g * BLK, (g + 1) * BLK)
                        s = jnp.where(upper, s_all[rows, 0:BLK], s_all[rows, BLK:2 * BLK]) - SLOPES[hq] * dist + pen
                        p = jnp.exp(s - lse_h)
                        dp = jnp.where(upper, dp_all[rows, 0:BLK], dp_all[rows, BLK:2 * BLK])
                        delta = jnp.sum(p * dp, axis=-1, keepdims=True)
                        dsink_ref[:, hq:hq + 1] += -jnp.sum(jnp.exp(sink_ref[0, hq] - lse_h) * delta, axis=0,
                                                            keepdims=True)
                        ds = _split_band(upper, (p * (dp - delta)).astype(BF16))
                        ds_parts.append(ds)
                        p_parts.append(_split_band(upper, p.astype(BF16)))
                        d = _dot(ds, kpad)
                        dq = d if dq is None else dq + d
                    col0 = (kvh * NPAIR + pr) * PAIR
                    dq_ref[:, col0:col0 + PAIR] = (dq * SCALE).astype(BF16)
                dkw = _dot_tn(qs, jnp.concatenate(ds_parts, axis=0)).T
                dvw = _dot_tn(dos, jnp.concatenate(p_parts, axis=0)).T
                dk_dup.append(dkw + pltpu.roll(dkw, HD, 1))
                dv_dup.append(dvw + pltpu.roll(dvw, HD, 1))
            dk = jnp.where(jnp.concatenate([lo, lo], axis=0), dk_dup[0], dk_dup[1])
            dv = jnp.where(jnp.concatenate([lo, lo], axis=0), dv_dup[0], dv_dup[1])
            dkv_ref[:, 0:PAIR] = (carry_ref[:, 0:PAIR] + dk[0:BLK]).astype(BF16)
            dkv_ref[:, PAIR:2 * PAIR] = (carry_ref[:, PAIR:2 * PAIR] + dv[0:BLK]).astype(BF16)
            carry_ref[:, 0:PAIR] = dk[BLK:2 * BLK]
            carry_ref[:, PAIR:2 * PAIR] = dv[BLK:2 * BLK]

        @pl.when(i == nb)
        def _():
            dkv_ref[...] = carry_ref[...].astype(BF16)

    cur = lambda i: (jnp.minimum(i, nb - 1), 0)
    prev = lambda i: (jnp.maximum(jnp.minimum(i, nb - 1) - 1, 0), 0)
    return pl.pallas_call(
        body, name="attn_bwd", grid=(nb + 1,),
        in_specs=[pl.BlockSpec(memory_space=pltpu.SMEM),
                  pl.BlockSpec((BLK, AW), cur), pl.BlockSpec((BLK, 2 * KVW), cur), pl.BlockSpec((BLK, 2 * KVW), prev),
                  pl.BlockSpec((BLK, AW), cur), pl.BlockSpec((BLK, NQ), cur)],
        out_specs=[pl.BlockSpec((BLK, AW), cur),
                   pl.BlockSpec((BLK, 2 * KVW), lambda i: (jnp.maximum(i - 1, 0), 0)),
                   pl.BlockSpec((1, NQ), lambda i: (0, 0))],
        out_shape=[SDS((T, AW), BF16), SDS((T, 2 * KVW), BF16), SDS((1, NQ), F32)],
        scratch_shapes=[pltpu.VMEM((BLK, 2 * KVW), F32)],
        compiler_params=_cp(("arbitrary",)),
    )(sinks, q, kv, kv, datt, lse)


def _inv_counts(t0, rows):
    t = (t0 + lax.broadcasted_iota(jnp.int32, (rows, 1), 0) + 1).astype(F32)
    return [1.0 / jnp.minimum(t, float(w)) for w in POOL_WINDOWS]


def pool_fwd(z, wmix, scale):
    T = z.shape[0]
    TM = 512
    L = TM + HALO

    def body(z_ref, halo_ref, wmix_ref, scale_ref, pooled_ref, mixs_ref):
        i = pl.program_id(0)
        halo = jnp.where(i > 0, halo_ref[...], 0.0)
        zt = z_ref[...]
        e = jnp.concatenate([halo, zt], axis=0)
        sums = []
        s = e
        for k in (1, 2, 4, 8):
            s = s + pltpu.roll(s, k, 0)
            sums.append(s)
        inv = _inv_counts(i * TM, TM)
        for gi in range(len(POOL_WINDOWS)):
            cols = slice(gi * PG, (gi + 1) * PG)
            pooled = (sums[gi][HALO:, cols] * inv[gi] - zt[:, cols]).astype(BF16)
            pooled_ref[:, cols] = pooled
            mixs_ref[:, cols] = (_dot(pooled, wmix_ref[gi]) * scale_ref[:, cols]).astype(BF16)

    return pl.pallas_call(
        body, name="pool_fwd", grid=(T // TM,),
        in_specs=[pl.BlockSpec((TM, PW), lambda i: (i, 0)),
                  pl.BlockSpec((HALO, PW), lambda i: (jnp.maximum(i * (TM // HALO) - 1, 0), 0)),
                  _const_spec((len(POOL_WINDOWS), PG, PG)), _const_spec((1, PW))],
        out_specs=[pl.BlockSpec((TM, PW), lambda i: (i, 0)), pl.BlockSpec((TM, PW), lambda i: (i, 0))],
        out_shape=[SDS((T, PW), BF16), SDS((T, PW), BF16)],
        compiler_params=_cp(("arbitrary",)),
    )(z, z, wmix, scale)


def pool_bwd(dmixs, pooled, wmix, scale):
    T = dmixs.shape[0]
    TM = 512
    L = TM + HALO
    nt = T // TM

    def body(dm_ref, halo_ref, pooled_ref, wmix_ref, scale_ref, dz_ref, dwmix_ref, dscale_ref):
        i = pl.program_id(0)

        @pl.when(i == 0)
        def _():
            dwmix_ref[...] = jnp.zeros_like(dwmix_ref)
            dscale_ref[...] = jnp.zeros_like(dscale_ref)

        halo = jnp.where(i < nt - 1, halo_ref[...], 0.0)
        dm = dm_ref[...]
        e = jnp.concatenate([dm, halo], axis=0)
        inv = _inv_counts(i * TM, L)
        for gi in range(len(POOL_WINDOWS)):
            cols = slice(gi * PG, (gi + 1) * PG)
            w = wmix_ref[gi]
            dmixed = (e[:, cols] * scale_ref[:, cols]).astype(BF16)
            dpooled = _dot_nt(dmixed, w)
            pooled = pooled_ref[:, cols]
            mixed = _dot(pooled, w)
            dscale_ref[:, cols] += jnp.sum(dm[:, cols] * mixed, axis=0, keepdims=True)
            dwmix_ref[gi] += _dot_tn(pooled, dmixed[:TM])
            s = dpooled * inv[gi]
            k = 1
            while k < POOL_WINDOWS[gi]:
                s = s + pltpu.roll(s, L - k, 0)
                k *= 2
            dz_ref[:, cols] = (s[:TM] - dpooled[:TM]).astype(BF16)

    return pl.pallas_call(
        body, name="pool_bwd", grid=(nt,),
        in_specs=[pl.BlockSpec((TM, PW), lambda i: (i, 0)),
                  pl.BlockSpec((HALO, PW), lambda i: (jnp.minimum((i + 1) * (TM // HALO), T // HALO - 1), 0)),
                  pl.BlockSpec((TM, PW), lambda i: (i, 0)),
                  _const_spec((len(POOL_WINDOWS), PG, PG)), _const_spec((1, PW))],
        out_specs=[pl.BlockSpec((TM, PW), lambda i: (i, 0)),
                   pl.BlockSpec((len(POOL_WINDOWS), PG, PG), lambda i: (0, 0, 0)),
                   pl.BlockSpec((1, PW), lambda i: (0, 0))],
        out_shape=[SDS((T, PW), BF16), SDS((len(POOL_WINDOWS), PG, PG), F32), SDS((1, PW), F32)],
        compiler_params=_cp(("arbitrary",)),
    )(dmixs, dmixs, pooled, wmix, scale)


def merge_fwd(att, mixs, gt, h1, wattn, wpool, wout):
    T = h1.shape[0]
    TM = 512

    def body(att_ref, mixs_ref, gt_ref, h_ref, wa_ref, wp_ref, wo_ref, h2_ref, mg_ref, gf_ref):
        a = _dot(att_ref[...], wa_ref[...])
        p = _dot(mixs_ref[...], wp_ref[...])
        sa = jax.nn.sigmoid(gt_ref[:, 0:D].astype(F32))
        sp = jax.nn.sigmoid(gt_ref[:, D:2 * D].astype(F32))
        gf_ref[:, 0:D] = (a * sa * (1.0 - sa)).astype(BF16)
        gf_ref[:, D:2 * D] = (p * sp * (1.0 - sp)).astype(BF16)
        mg = (sa * a + sp * p).astype(BF16)
        mg_ref[...] = mg
        h2_ref[...] = h_ref[...] + _dot(mg, wo_ref[...])

    tile = lambda w: pl.BlockSpec((TM, w), lambda i: (i, 0))
    return pl.pallas_call(
        body, name="merge_fwd", grid=(T // TM,),
        in_specs=[tile(AW), tile(PW), tile(2 * D), tile(D), _const_spec((AW, D)), _const_spec((PW, D)),
                  _const_spec((D, D))],
        out_specs=[tile(D), tile(D), tile(2 * D)],
        out_shape=[SDS((T, D), F32), SDS((T, D), BF16), SDS((T, 2 * D), BF16)],
        compiler_params=_cp(("arbitrary",)),
    )(att, mixs, gt, h1, wattn, wpool, wout)


def merge_bwd(dh2, gt, gf, wattn, wpool, wout, after=None):
    T = dh2.shape[0]
    TM = 512
    token_spec, token_arg = _token_operand(after)

    def body(dh2_ref, gt_ref, gf_ref, wa_ref, wp_ref, wo_ref, *rest):
        datt_ref, dmixs_ref, dgt_ref, da_ref, dp_ref = rest[-5:]
        dm = _dot_nt(dh2_ref[...].astype(BF16), wo_ref[...])
        da = (dm * jax.nn.sigmoid(gt_ref[:, 0:D].astype(F32))).astype(BF16)
        dp = (dm * jax.nn.sigmoid(gt_ref[:, D:2 * D].astype(F32))).astype(BF16)
        da_ref[...] = da
        dp_ref[...] = dp
        dgt_ref[:, 0:D] = (dm * gf_ref[:, 0:D].astype(F32)).astype(BF16)
        dgt_ref[:, D:2 * D] = (dm * gf_ref[:, D:2 * D].astype(F32)).astype(BF16)
        datt_ref[...] = _dot_nt(da, wa_ref[...]).astype(BF16)
        dmixs_ref[...] = _dot_nt(dp, wp_ref[...])

    tile = lambda w: pl.BlockSpec((TM, w), lambda i: (i, 0))
    return pl.pallas_call(
        body, name="merge_bwd", grid=(T // TM,),
        in_specs=[tile(D), tile(2 * D), tile(2 * D), _const_spec((AW, D)), _const_spec((PW, D)),
                  _const_spec((D, D))] + token_spec,
        out_specs=[tile(AW), tile(PW), tile(2 * D), tile(D), tile(D)],
        out_shape=[SDS((T, AW), BF16), SDS((T, PW), F32), SDS((T, 2 * D), BF16), SDS((T, D), BF16),
                   SDS((T, D), BF16)],
        compiler_params=_cp(("arbitrary",)),
    )(dh2, gt, gf, wattn, wpool, wout, *token_arg)


def adamw(w, g, m, v, name):
    R, C = w.shape
    tile_bytes = 2 * 1024 * 1024
    tr = R
    if R * C * 4 > tile_bytes:
        tr = next(cand for cand in (512, 256, 128, 64, 32, 16, 8) if R % cand == 0 and cand * C * 4 <= tile_bytes)
    c1 = 1.0 - B1 ** STEP
    c2 = 1.0 - B2 ** STEP

    def body(w_ref, g_ref, m_ref, v_ref, go_ref, d_ref, nm_ref, nv_ref):
        gv = g_ref[...]
        go_ref[...] = gv
        nm = B1 * m_ref[...] + (1.0 - B1) * gv
        nv = B2 * v_ref[...] + (1.0 - B2) * (gv * gv)
        nm_ref[...] = nm
        nv_ref[...] = nv
        d_ref[...] = -LR * ((nm / c1) / (jnp.sqrt(nv / c2) + ADAM_EPS) + WD * w_ref[...])

    spec = pl.BlockSpec((tr, C), lambda i: (i, 0))
    return pl.pallas_call(
        body, name=name, grid=(R // tr,),
        in_specs=[spec] * 4, out_specs=[spec] * 4, out_shape=[SDS((R, C), F32)] * 4,
        compiler_params=_cp(("parallel",)),
    )(w, g, m, v)


GROUP_FFN1, GROUP_MIX, GROUP_FFN2 = 0, 1, 2


def _behind(small, token):
    return small if token is None else small + token[0:1, 0:1]


def local_fwd_bwd(x, target, S, comm):
    W = comm.weights(GROUP_FFN1, None)
    h1, ab1, hid1 = ffn_fwd(x, _behind(S["ffn1_norm"], comm.started()), W["ffn1_w_up"], W["ffn1_w_down"], "ffn1_fwd")
    W = comm.weights(GROUP_MIX, h1)
    u, q, kv, z, gt = mix_in_fwd(h1, _behind(S["mix_norm"], comm.started()), W["w_in"])
    att, lse = attn_fwd(q, kv, S["sinks"])
    pooled, mixs = pool_fwd(z, S["pool_w_mix"], S["pool_scale"])
    h2, merged, gf = merge_fwd(att, mixs, gt, h1, W["w_attn_up"], W["w_pool_up"], W["w_out"])
    W = comm.weights(GROUP_FFN2, h2)
    dh3, ab2, hid2, loss, g_final = ffn_fwd(h2, S["ffn2_norm"], W["ffn2_w_up"], W["ffn2_w_down"], "ffn2_fwd",
                                            head=(target, S["final_norm"]))

    G = {"final_norm": g_final}
    dh2, dup2, n2, G["ffn2_norm"] = ffn_bwd_x(dh3, h2, ab2, S["ffn2_norm"], W["ffn2_w_up"], W["ffn2_w_down"],
                                              "ffn2_bwd_x")
    ffn2 = ("ffn2_w_down", "ffn2_w_up")
    token = comm.grads({"ffn2_w_down": matmul_tn(hid2, dh3, "ffn2_dw_down", tm=1408, tn=512, tt=4096, b_scale=0.5),
                        "ffn2_w_up": matmul_tn(n2, dup2, "ffn2_dw_up", tm=D, tn=512, tt=4096)})

    W = comm.weights(GROUP_MIX, None)
    datt, dmixs, dgt, da, dp = merge_bwd(dh2, gt, gf, W["w_attn_up"], W["w_pool_up"], W["w_out"], after=token)
    token = comm.advance(ffn2, datt)
    g_mix = {"w_out": matmul_tn(merged, dh2, "dw_out", tm=D, tn=D, after=token),
             "w_attn_up": matmul_tn(att, da, "dw_attn_up", tm=AW, tn=D),
             "w_pool_up": matmul_tn(mixs, dp, "dw_pool_up", tm=PW, tn=D)}
    dz, G["pool_w_mix"], G["pool_scale"] = pool_bwd(dmixs, pooled, S["pool_w_mix"], _behind(S["pool_scale"], token))
    dq, dkv, G["sinks"] = attn_bwd(q, kv, datt, lse, _behind(S["sinks"], token))
    dh1, G["mix_norm"] = mix_in_bwd(dq, dkv, dz, dgt, dh2, h1, S["mix_norm"], W["w_in"])
    g_mix["w_in"] = jnp.concatenate([
        matmul_tn(dq, u, "dw_in_q", tm=AW, tn=D),
        matmul_tn(dkv, u, "dw_in_kv", tm=2 * KVW, tn=D),
        matmul_tn(dz, u, "dw_in_z", tm=PW, tn=D),
        matmul_tn(dgt, u, "dw_in_g", tm=D, tn=512, tt=4096),
    ], axis=0)
    mix = tuple(g_mix)
    token = comm.grads(g_mix)

    W = comm.weights(GROUP_FFN1, None)
    g_dn1 = matmul_tn(hid1, dh1, "ffn1_dw_down", tm=1408, tn=512, tt=4096, b_scale=0.5, after=token)
    token = comm.advance(mix, g_dn1)
    token2 = comm.grads({"ffn1_w_down": g_dn1})
    g1 = _behind(_behind(S["ffn1_norm"], token), token2)
    dx, dup1, n1, G["ffn1_norm"] = ffn_bwd_x(dh1, x, ab1, g1, W["ffn1_w_up"], W["ffn1_w_down"], "ffn1_bwd_x")
    token = comm.advance(("ffn1_w_down",), dx)
    token2 = comm.small_start(G, loss, dx)
    g_up1 = matmul_tn(n1, dup1, "ffn1_dw_up", tm=D, tn=512, tt=4096,
                      after=None if token is None else token + token2)
    token = comm.grads({"ffn1_w_up": g_up1})
    done = comm.finish(mix, token)
    token = comm.advance(("ffn1_w_up",), done)
    done = comm.finish(("ffn1_w_down",), token)
    done = comm.small_finish(done)
    done = comm.finish(ffn2, done)
    comm.finish(("ffn1_w_up",), done)
    return loss, dx


HBM_SPEC = pl.BlockSpec(memory_space=pltpu.HBM)


def _me():
    return lax.axis_index("x"), lax.axis_index("y"), lax.axis_index("c")


def _peer_chip(x, y, k):
    return x ^ (k >> 1), y ^ (k & 1)


def _piece_half(ref, rowlike, j, h):
    if rowlike:
        return ref.at[j, h]
    ns = ref.shape[-1] // N_CHIPS
    return ref.at[h, :, pl.ds(pl.multiple_of(j * ns, 128), ns)]


def _piece(ref, rowlike, j):
    if rowlike:
        return ref.at[j]
    ns = ref.shape[-1] // N_CHIPS
    return ref.at[:, :, pl.ds(pl.multiple_of(j * ns, 128), ns)]


def _full_shape(shard_view, rowlike):
    _, kh, ns = shard_view.shape
    return (N_CHIPS, 2, kh, ns) if rowlike else (2, kh, N_CHIPS * ns)


def _remote(src, dst, send_sem, recv_sem, dev):
    return pltpu.make_async_remote_copy(src, dst, send_sem, recv_sem, device_id=dev, device_id_type=MESH)


SEM_SPEC = pl.BlockSpec(memory_space=pltpu.SEMAPHORE)
SPLIT_PARAMS = pltpu.CompilerParams(has_side_effects=pltpu.SideEffectType.DATAFLOW_SIDE_EFFECTING)


def _gather_plan(rowlikes):
    def plan(s_refs, f_refs, a, k, x, y, c):
        px, py = _peer_chip(x, y, k)
        return (s_refs[a].at[c], _piece_half(f_refs[a], rowlikes[a], 2 * x + y, c),
                _piece_half(f_refs[a], rowlikes[a], 2 * px + py, c), (px, py, c))
    return plan


def _all_to_all_plan(rowlikes):
    def plan(q_refs, r_refs, a, k, x, y, c):
        px, py = _peer_chip(x, y, k)
        if rowlikes[a]:
            src = q_refs[a].at[2 * px + py]
        else:
            ns = q_refs[a].shape[-1] // N_CHIPS
            src = q_refs[a].at[:, pl.ds(pl.multiple_of((2 * px + py) * ns, 128), ns)]
        return src, r_refs[a].at[k - 1], r_refs[a].at[k - 1], (px, py, c)
    return plan


def _swap_plan(rowlikes):
    def plan(g_refs, got_refs, a, k, x, y, c):
        src = g_refs[a].at[:, 1 - c] if rowlikes[a] else g_refs[a].at[1 - c]
        return src, got_refs[a], got_refs[a], (x, y, 1 - c)
    return plan


def _everyone_plan(s_refs, slot_refs, a, k, x, y, c):
    px, py, pc = x ^ (k >> 2), y ^ ((k >> 1) & 1), c ^ (k & 1)
    return s_refs[a], slot_refs[a].at[4 * x + 2 * y + c], slot_refs[a].at[4 * px + 2 * py + pc], (px, py, pc)


CHIPS, SIBLING, EVERYONE = (1, 2, 3), (1,), tuple(range(1, 8))


def _as_list(after):
    return [] if after is None else list(after) if isinstance(after, (list, tuple)) else [after]


def exchange_start(srcs, land_shapes, plan, after, name, peers=CHIPS):
    n = len(srcs)
    lands = [lax.empty(shape, s.dtype) for shape, s in zip(land_shapes, srcs)]

    behind = _as_list(after)

    def body(*refs):
        s_refs, l_refs = refs[:n], refs[n:2 * n]
        send_sems, recv_sems = refs[2 * n + len(behind)], refs[2 * n + len(behind) + 1]
        token = refs[-1]
        x, y, c = _me()
        for a in range(n):
            for i, k in enumerate(peers):
                src, dst, _, peer = plan(s_refs, l_refs, a, k, x, y, c)
                sem = len(peers) * a + i
                _remote(src, dst, send_sems.at[sem], recv_sems.at[sem], peer).start()
        token[...] = jnp.zeros_like(token)

    n_sems = len(peers) * n
    outs = pl.pallas_call(
        body, name=name, in_specs=[HBM_SPEC] * (2 * n) + [pl.BlockSpec(memory_space=pl.ANY)] * len(behind),
        out_specs=[SEM_SPEC, SEM_SPEC] + [HBM_SPEC] * (2 * n) + [pl.BlockSpec(memory_space=pltpu.VMEM)],
        out_shape=[pltpu.SemaphoreType.DMA((n_sems,)), pltpu.SemaphoreType.DMA((n_sems,))]
        + [pltpu.HBM(a.shape, a.dtype) for a in (*srcs, *lands)] + [SDS((8, 128), F32)],
        input_output_aliases={i: 2 + i for i in range(2 * n)},
        compiler_params=SPLIT_PARAMS,
    )(*[pltpu.with_memory_space_constraint(a, pltpu.HBM) for a in (*srcs, *lands)], *behind)
    return {"sems": outs[:2], "srcs": outs[2:2 + n], "lands": outs[2 + n:2 + 2 * n], "token": outs[-1]}


def exchange_wait(state, plan, after, name, peers=CHIPS):
    n = len(state["srcs"])
    behind = _as_list(after)

    def body(*refs):
        s_refs, l_refs = refs[:n], refs[n:2 * n]
        send_sems, recv_sems = refs[2 * n], refs[2 * n + 1]
        x, y, c = _me()
        for a in range(n):
            for i, k in enumerate(peers):
                src, _, landing, peer = plan(s_refs, l_refs, a, k, x, y, c)
                sem = len(peers) * a + i
                cp = _remote(src, landing, send_sems.at[sem], recv_sems.at[sem], peer)
                cp.wait_send()
                cp.wait_recv()

    bufs = (*state["srcs"], *state["lands"])
    outs = pl.pallas_call(
        body, name=name,
        in_specs=[HBM_SPEC] * (2 * n) + [SEM_SPEC, SEM_SPEC] + [pl.BlockSpec(memory_space=pl.ANY)] * len(behind),
        out_specs=[HBM_SPEC] * (2 * n),
        out_shape=[pltpu.HBM(a.shape, a.dtype) for a in bufs],
        input_output_aliases={i: i for i in range(2 * n)},
        compiler_params=SPLIT_PARAMS,
    )(*bufs, *state["sems"], *behind)
    return outs[:n], outs[n:]


def gather_finish(shards, fulls, rowlikes, name):
    n = len(shards)

    def body(*refs):
        s_refs, f_refs = refs[:n], refs[2 * n:3 * n]
        send_sems, recv_sems = refs[3 * n:]
        x, y, c = _me()
        chip = 2 * x + y
        sib = (x, y, 1 - c)
        sends = []
        for a in range(n):
            own = _piece(f_refs[a], rowlikes[a], chip)
            cp = _remote(s_refs[a], own, send_sems.at[a, 0], recv_sems.at[a, 0], sib)
            cp.start()
            sends.append(cp)
            for k in (1, 2, 3):
                px, py = _peer_chip(x, y, k)
                slot = _piece_half(f_refs[a], rowlikes[a], 2 * px + py, c)
                cp = _remote(slot, slot, send_sems.at[a, k], recv_sems.at[a, k], sib)
                cp.start()
                sends.append(cp)
        for a in range(n):
            own = _piece(f_refs[a], rowlikes[a], chip)
            _remote(own, own, send_sems.at[a, 0], recv_sems.at[a, 0], sib).wait_recv()
            for k in (1, 2, 3):
                px, py = _peer_chip(x, y, k)
                slot = _piece_half(f_refs[a], rowlikes[a], 2 * px + py, 1 - c)
                _remote(slot, slot, send_sems.at[a, k], recv_sems.at[a, k], sib).wait_recv()
        for cp in sends:
            cp.wait_send()

    return pl.pallas_call(
        body, name=name, in_specs=[HBM_SPEC] * (2 * n), out_specs=[HBM_SPEC] * n,
        out_shape=[SDS(f.shape, f.dtype) for f in fulls],
        input_output_aliases={n + a: a for a in range(n)},
        scratch_shapes=[pltpu.SemaphoreType.DMA((n, 4)), pltpu.SemaphoreType.DMA((n, 4))],
    )(*shards, *fulls)


def _half_buffer_shape(gview, rowlike):
    return (N_CHIPS,) + gview.shape[2:] if rowlike else gview.shape[1:]


ROW_TILES = 2


def _piece_specs(rowlike, kh, ns, piece, half):
    tr = kh // ROW_TILES
    if rowlike:
        return (pl.BlockSpec((None, None, tr, ns), lambda *g: (piece(*g), half(*g), g[-2], 0)),
                pl.BlockSpec((None, tr, ns), lambda *g: (piece(*g), g[-2], 0)))
    return (pl.BlockSpec((None, tr, ns), lambda *g: (half(*g), g[-2], piece(*g))),
            pl.BlockSpec((tr, ns), lambda *g: (g[-2], piece(*g))))


def add_halves(gview, got, rowlike, place, name):
    kh, ns = (gview.shape[2], gview.shape[3]) if rowlike else (gview.shape[1], gview.shape[2] // N_CHIPS)

    def body(place_ref, g_ref, got_ref, o_ref):
        o_ref[...] = (g_ref[...].astype(F32) + got_ref[...].astype(F32)).astype(BF16)

    g_spec, h_spec = _piece_specs(rowlike, kh, ns, lambda j, r, p: j, lambda j, r, p: p[0])
    return pl.pallas_call(
        body, name=name,
        grid_spec=pltpu.PrefetchScalarGridSpec(num_scalar_prefetch=1, grid=(N_CHIPS, ROW_TILES),
                                               in_specs=[g_spec, h_spec], out_specs=h_spec),
        out_shape=SDS(got.shape, BF16),
        compiler_params=_cp(("parallel", "parallel")),
    )(place, gview, got)


def _slot_shape(q, rowlike):
    return (3,) + (q.shape[1:] if rowlike else (q.shape[0], q.shape[1] // N_CHIPS))


def sum_pieces(q, recv, rowlike, place, name):
    kh, ns = recv.shape[1], recv.shape[2]
    tr = kh // ROW_TILES

    def body(place_ref, q_ref, r_ref, o_ref):
        acc = q_ref[...].astype(F32)
        for k in range(3):
            acc = acc + r_ref[k].astype(F32)
        o_ref[...] = acc

    _, h_spec = _piece_specs(rowlike, kh, ns, lambda z, r, p: p[1], lambda z, r, p: p[0])
    return pl.pallas_call(
        body, name=name,
        grid_spec=pltpu.PrefetchScalarGridSpec(
            num_scalar_prefetch=1, grid=(1, ROW_TILES),
            in_specs=[h_spec, pl.BlockSpec((3, tr, ns), lambda z, r, p: (0, r, 0))],
            out_specs=pl.BlockSpec((None, tr, ns), lambda z, r, p: (p[0], r, 0))),
        out_shape=SDS((2, kh, ns), F32),
        compiler_params=_cp(("parallel", "parallel")),
    )(place, q, recv)


def join_halves(halves, name):
    n = len(halves)

    def body(*refs):
        o_refs = refs[n:2 * n]
        send_sems, recv_sems = refs[2 * n:]
        x, y, c = _me()
        sib = (x, y, 1 - c)
        sends = []
        for a in range(n):
            cp = _remote(o_refs[a].at[c], o_refs[a].at[c], send_sems.at[a], recv_sems.at[a], sib)
            cp.start()
            sends.append(cp)
        for a in range(n):
            got = o_refs[a].at[1 - c]
            _remote(got, got, send_sems.at[a], recv_sems.at[a], sib).wait_recv()
        for cp in sends:
            cp.wait_send()

    return pl.pallas_call(
        body, name=name, in_specs=[HBM_SPEC] * n, out_specs=[HBM_SPEC] * n,
        out_shape=[SDS(h.shape, h.dtype) for h in halves],
        input_output_aliases={a: a for a in range(n)},
        scratch_shapes=[pltpu.SemaphoreType.DMA((n,)), pltpu.SemaphoreType.DMA((n,))],
    )(*halves)


N_DEV = 8


def sum_devices(s, slots, me):
    R, Wd = s.shape

    def body(me_ref, s_ref, slots_ref, out_ref):
        acc = None
        for d in range(N_DEV):
            mine = me_ref[0] == d
            term = jnp.where(mine, s_ref[...], slots_ref[jnp.where(mine, d ^ 1, d)])
            acc = term if acc is None else acc + term
        out_ref[...] = acc

    vmem = pl.BlockSpec(memory_space=pltpu.VMEM)
    return pl.pallas_call(
        body, name="sum_devices", in_specs=[pl.BlockSpec(memory_space=pltpu.SMEM), vmem, vmem], out_specs=vmem,
        out_shape=SDS((R, Wd), F32),
    )(me, s, slots)


TRANSPOSED = ("w_in",)
BIG = {"ffn1_w_up": (D, 2 * FF, "col"), "ffn1_w_down": (FF, D, "row"), "w_in": (INW, D, "row"),
       "w_attn_up": (AW, D, "row"), "w_pool_up": (PW, D, "col"), "w_out": (D, D, "row"),
       "ffn2_w_up": (D, 2 * FF, "col"), "ffn2_w_down": (FF, D, "row")}
GROUPS = (("ffn1_w_up", "ffn1_w_down"), ("w_in", "w_attn_up", "w_pool_up", "w_out"), ("ffn2_w_up", "ffn2_w_down"))
SMALL = ("ffn1_norm", "mix_norm", "ffn2_norm", "final_norm", "pool_scale", "sinks", "pool_w_mix")
SMALL_W = 128


def _rowlike(name):
    return BIG[name][2] == "row"


def _half_dims(name):
    k, n, kind = BIG[name]
    return (k // N_CHIPS // 2, n) if kind == "row" else (k // 2, n // N_CHIPS)


def shard_view(name, shard):
    return shard.reshape((2,) + _half_dims(name))


def full_from_view(name, fv):
    k, n, _ = BIG[name]
    return fv.reshape(k, n)


def grad_view(name, g):
    kh, ns = _half_dims(name)
    return g.reshape(_full_shape(jax.ShapeDtypeStruct((2, kh, ns), g.dtype), _rowlike(name)))


def pack_small(d):
    parts = []
    for name in SMALL:
        a = d[name].reshape(-1)
        pad = (-a.shape[0]) % SMALL_W
        parts.append(jnp.pad(a, (0, pad)).reshape(-1, SMALL_W))
    a = jnp.concatenate(parts, axis=0)
    return jnp.pad(a, ((0, (-a.shape[0]) % 8), (0, 0)))


def unpack_small(a, like):
    out, r0 = {}, 0
    for name in SMALL:
        size = int(np.prod(like[name].shape))
        rows = -(-size // SMALL_W)
        out[name] = a[r0:r0 + rows].reshape(-1)[:size].reshape(like[name].shape)
        r0 += rows
    return out


WEIGHTS = ("ffn1_norm", "ffn1_w_up", "ffn1_w_down", "mix_norm", "w_in", "sinks", "w_attn_up", "pool_w_mix",
           "pool_scale", "w_pool_up", "w_out", "ffn2_norm", "ffn2_w_up", "ffn2_w_down", "final_norm")


def kernel(x, ffn1_norm, ffn1_w_up, ffn1_w_down, mix_norm, w_in, sinks, w_attn_up, pool_w_mix, pool_scale, w_pool_up, w_out, ffn2_norm, ffn2_w_up, ffn2_w_down, final_norm, loss_target, m_ffn1_norm, m_ffn1_w_up, m_ffn1_w_down, m_mix_norm, m_w_in, m_sinks, m_w_attn_up, m_pool_w_mix, m_pool_scale, m_w_pool_up, m_w_out, m_ffn2_norm, m_ffn2_w_up, m_ffn2_w_down, m_final_norm, v_ffn1_norm, v_ffn1_w_up, v_ffn1_w_down, v_mix_norm, v_w_in, v_sinks, v_w_attn_up, v_pool_w_mix, v_pool_scale, v_w_pool_up, v_w_out, v_ffn2_norm, v_ffn2_w_up, v_ffn2_w_down, v_final_norm):
    given = dict(locals())
    w = {n: given[n] for n in WEIGHTS}
    m = {n: given["m_" + n] for n in WEIGHTS}
    v = {n: given["v_" + n] for n in WEIGHTS}
    cx, cy, cc = _me()
    place = jnp.stack([cc, 2 * cx + cy]).astype(jnp.int32)

    def local2d(d, n):
        return d[n][0].T if n in TRANSPOSED else d[n][0]

    shards = {n: local2d(w, n) for n in BIG}
    sviews = {n: shard_view(n, shards[n].astype(BF16)) for n in BIG}
    grads, delta, new_m, new_v = {}, {}, {}, {}
    rowlikes = [[_rowlike(n) for n in names] for names in GROUPS]

    class Exchanges:
        def __init__(self):
            self.gathers, self.fulls, self.reductions, self.small = {}, {}, {}, None

        def _start_gather(self, group, after):
            sv = [sviews[n] for n in GROUPS[group]]
            rl = rowlikes[group]
            self.gathers[group] = exchange_start(sv, [_full_shape(s, r) for s, r in zip(sv, rl)], _gather_plan(rl),
                                                 after, f"gather_start_{group}")

        def weights(self, group, after):
            if not self.gathers:
                self._start_gather(0, None)
            if group not in self.fulls:
                names, rl, state = GROUPS[group], rowlikes[group], self.gathers[group]
                sv, fulls = exchange_wait(state, _gather_plan(rl), state["token"] if after is None else after,
                                          f"gather_wait_{group}")
                fulls = gather_finish(sv, fulls, rl, f"gather_finish_{group}")
                self.fulls[group] = {n: full_from_view(n, f) for n, f in zip(names, fulls)}
                if group + 1 < len(GROUPS):
                    self._start_gather(group + 1, fulls[0])
            return self.fulls[group]

        def started(self):
            return self.gathers[max(self.gathers)]["token"]

        def grads(self, g):
            names = tuple(g)
            rl = [_rowlike(n) for n in names]
            gv = [grad_view(n, g[n]) for n in names]
            state = exchange_start(gv, [_half_buffer_shape(a, r) for a, r in zip(gv, rl)], _swap_plan(rl), None,
                                   "swap_start_" + names[0], SIBLING)
            self.reductions[names] = state
            return state["token"]

        def advance(self, names, after):
            rl = [_rowlike(n) for n in names]
            gv, gots = exchange_wait(self.reductions[names], _swap_plan(rl), after, "swap_wait_" + names[0], SIBLING)
            qs = [add_halves(a, b, r, place, "add_halves_" + n) for a, b, r, n in zip(gv, gots, rl, names)]
            state = exchange_start(qs, [_slot_shape(q, r) for q, r in zip(qs, rl)], _all_to_all_plan(rl), None,
                                   "all_to_all_start_" + names[0])
            self.reductions[names] = state
            return state["token"]

        def finish(self, names, after):
            rl = [_rowlike(n) for n in names]
            qs, recvs = exchange_wait(self.reductions.pop(names), _all_to_all_plan(rl), after,
                                      "all_to_all_wait_" + names[0])
            halves = [sum_pieces(q, r, k, place, "sum_pieces_" + n) for q, r, k, n in zip(qs, recvs, rl, names)]
            for n, o in zip(names, join_halves(halves, "join_halves_" + names[0])):
                grads[n], delta[n], new_m[n], new_v[n] = adamw(shards[n], o.reshape(shards[n].shape), local2d(m, n),
                                                               local2d(v, n), "adamw_" + n)
            return [new_v[n] for n in names]

        def small_start(self, G, loss, after):
            packed = pack_small({n: G[n] for n in SMALL})
            used_rows = sum(-(-int(np.prod(small_like[n].shape)) // SMALL_W) for n in SMALL)
            assert packed.shape[0] > used_rows
            packed = packed.at[-1, 0].set(loss[0, 0])
            self.small = exchange_start([packed], [(N_DEV,) + packed.shape], _everyone_plan, after,
                                        "small_start", EVERYONE)
            return self.small["token"]

        def small_finish(self, after):
            (packed,), (slots,) = exchange_wait(self.small, _everyone_plan, after, "small_wait", EVERYONE)
            total = sum_devices(packed, slots, (4 * cx + 2 * cy + cc).astype(jnp.int32).reshape(1))
            self.loss = total[-1, 0]
            g, ds, ms, vs = adamw(pack_small(w), total, pack_small(m), pack_small(v), "adamw_small")
            for d, packed_d in ((grads, g), (delta, ds), (new_m, ms), (new_v, vs)):
                d.update(unpack_small(packed_d, small_like))
            return vs

    small_like = {n: w[n] for n in SMALL}
    S = {n: w[n].reshape(1, -1) for n in ("ffn1_norm", "mix_norm", "ffn2_norm", "final_norm", "pool_scale", "sinks")}
    S["pool_w_mix"] = w["pool_w_mix"][0].astype(BF16)
    exchanges = Exchanges()
    _, dx = local_fwd_bwd(x[0], loss_target[0], S, exchanges)
    loss = exchanges.loss

    def shaped(d, n):
        return (d[n].T if n in TRANSPOSED else d[n]).reshape(w[n].shape)

    return (loss, dx[None], *[shaped(grads, n) for n in WEIGHTS], *[shaped(delta, n) for n in WEIGHTS],
            *[shaped(new_m, n) for n in WEIGHTS], *[shaped(new_v, n) for n in WEIGHTS])
```

```python
import numpy as np
import jax
import jax.numpy as jnp
from jax import lax
from jax.experimental import pallas as pl
from jax.experimental.pallas import tpu as pltpu

F32 = jnp.float32
BF16 = jnp.bfloat16
SDS = jax.ShapeDtypeStruct
MESH = pl.DeviceIdType.MESH

D = 1024
FF = 2816
NQ = 16
NKV = 2
HD = 64
GQ = NQ // NKV
AW = NQ * HD
KVW = NKV * HD
BLK = 128
PW = 512
PG = 128
POOL_WINDOWS = (2, 4, 8, 16)
HALO = 16
INW = AW + 2 * KVW + PW + 2 * D
C_KV = AW
C_Z = AW + 2 * KVW
C_G = C_Z + PW
EPS = 1e-6
FF_CHUNK = 256
FF_CHUNKS = tuple((c, FF_CHUNK) for c in range(0, FF, FF_CHUNK))
SLOPES = tuple(float(2.0 ** (-8.0 * h / NQ)) for h in range(1, NQ + 1))
SCALE = HD ** -0.5

LR, B1, B2, ADAM_EPS, WD, STEP = 0.001, 0.9, 0.999, 1e-08, 0.01, 10

VMEM_LIMIT = 56 * 1024 * 1024
N_CHIPS = 4

NT = (((1,), (1,)), ((), ()))
TN = (((0,), (0,)), ((), ()))


def _cp(sem=None, vmem=VMEM_LIMIT):
    return pltpu.CompilerParams(dimension_semantics=sem, vmem_limit_bytes=vmem)


def _const_spec(shape):
    nd = len(shape)
    return pl.BlockSpec(shape, lambda *_: (0,) * nd, pipeline_mode=pl.Buffered(1))


def _rstd(x):
    return lax.rsqrt(jnp.mean(x * x, axis=-1, keepdims=True) + EPS)


def _rms_bwd(dn, xhat, rstd, g):
    dxhat = dn * g
    return rstd * (dxhat - xhat * jnp.mean(dxhat * xhat, axis=-1, keepdims=True))


def _dot(a, b):
    return jnp.dot(a, b, preferred_element_type=F32)


def _dot_nt(a, b):
    return lax.dot_general(a, b, NT, preferred_element_type=F32)


def _dot_tn(a, b):
    return lax.dot_general(a, b, TN, preferred_element_type=F32)


def ffn_fwd(h, g, wup, wdn, name, head=None):
    T = h.shape[0]
    TM = 512
    tile = lambda w: pl.BlockSpec((TM, w), lambda i: (i, 0))
    acc_spec = lambda w: pl.BlockSpec((1, w), lambda i: (0, 0))

    def body(h_ref, g_ref, wup_ref, wdn_ref, *rest):
        out_ref, ab_ref, hid_ref = rest[-3:] if head is None else rest[2:5]
        x = h_ref[...]
        n = (x * _rstd(x) * g_ref[...]).astype(BF16)
        for c0, w in FF_CHUNKS:
            a = _dot(n, wup_ref[:, c0:c0 + w])
            b = _dot(n, wup_ref[:, FF + c0:FF + c0 + w])
            sig = jax.nn.sigmoid(a)
            s = a * sig
            ab_ref[:, c0:c0 + w] = (b * (sig * (1.0 + a * (1.0 - sig)))).astype(BF16)
            ab_ref[:, FF + c0:FF + c0 + w] = s.astype(BF16)
            hid_ref[:, c0:c0 + w] = (s * b).astype(BF16)
        out = x + 0.5 * _dot(hid_ref[...], wdn_ref[...])
        if head is None:
            out_ref[...] = out
        else:
            t_ref, gf_ref, loss_ref, dgf_ref = rest[0], rest[1], rest[5], rest[6]
            out_ref[...] = _loss_head(out, t_ref[...], gf_ref[...], loss_ref, dgf_ref, pl.program_id(0) == 0)

    head_in, head_specs, head_out_specs, head_out_shape = [], [], [], []
    if head is not None:
        head_in, head_specs = list(head), [tile(D), _const_spec((1, D))]
        head_out_specs, head_out_shape = [acc_spec(1), acc_spec(D)], [SDS((1, 1), F32), SDS((1, D), F32)]
    return pl.pallas_call(
        body, name=name, grid=(T // TM,),
        in_specs=[tile(D), _const_spec((1, D)), _const_spec((D, 2 * FF)), _const_spec((FF, D))] + head_specs,
        out_specs=[tile(D), tile(2 * FF), tile(FF)] + head_out_specs,
        out_shape=[SDS((T, D), F32), SDS((T, 2 * FF), BF16), SDS((T, FF), BF16)] + head_out_shape,
        compiler_params=_cp(("arbitrary",)),
    )(h, g, wup, wdn, *head_in)


def _loss_head(x, target, g, loss_ref, dg_ref, first):
    @pl.when(first)
    def _():
        loss_ref[...] = jnp.zeros_like(loss_ref)
        dg_ref[...] = jnp.zeros_like(dg_ref)

    rstd = _rstd(x)
    xhat = x * rstd
    err = xhat * g - target
    loss_ref[...] += 0.5 * jnp.sum(jnp.mean(err * err, axis=-1, keepdims=True), axis=0, keepdims=True)
    dy = err * (1.0 / D)
    dg_ref[...] += jnp.sum(dy * xhat, axis=0, keepdims=True)
    return _rms_bwd(dy, xhat, rstd, g)


def ffn_bwd_x(dh, h_in, ab, g, wup, wdn, name):
    T = dh.shape[0]
    TM = 256

    def body(dh_ref, h_ref, ab_ref, g_ref, wup_ref, wdn_ref, dhin_ref, dup_ref, n_ref, dg_ref):
        x = h_ref[...]
        g = g_ref[...]
        rstd = _rstd(x)
        xhat = x * rstd
        n_ref[...] = (xhat * g).astype(BF16)
        dh = dh_ref[...]
        dhh = (0.5 * dh).astype(BF16)
        for c0, w in FF_CHUNKS:
            dhid = _dot_nt(dhh, wdn_ref[c0:c0 + w, :])
            dup_ref[:, c0:c0 + w] = (dhid * ab_ref[:, c0:c0 + w].astype(F32)).astype(BF16)
            dup_ref[:, FF + c0:FF + c0 + w] = (dhid * ab_ref[:, FF + c0:FF + c0 + w].astype(F32)).astype(BF16)
        dn = _dot_nt(dup_ref[...], wup_ref[...])
        dhin_ref[...] = dh + _rms_bwd(dn, xhat, rstd, g)

        @pl.when(pl.program_id(0) == 0)
        def _():
            dg_ref[...] = jnp.zeros_like(dg_ref)

        dg_ref[...] += jnp.sum(dn * xhat, axis=0, keepdims=True)

    tile = lambda w: pl.BlockSpec((TM, w), lambda i: (i, 0))
    return pl.pallas_call(
        body, name=name, grid=(T // TM,),
        in_specs=[tile(D), tile(D), tile(2 * FF), _const_spec((1, D)), _const_spec((D, 2 * FF)), _const_spec((FF, D))],
        out_specs=[tile(D), tile(2 * FF), tile(D), pl.BlockSpec((1, D), lambda i: (0, 0))],
        out_shape=[SDS((T, D), F32), SDS((T, 2 * FF), BF16), SDS((T, D), BF16), SDS((1, D), F32)],
        compiler_params=_cp(("arbitrary",)),
    )(dh, h_in, ab, g, wup, wdn)


TOKEN_SPEC = pl.BlockSpec((8, 128), lambda *_: (0, 0))


def _token_operand(token):
    return ([], []) if token is None else ([TOKEN_SPEC], [token])


def matmul_tn(a, b, name, *, tm, tn, tt=1024, b_scale=None, after=None):
    T, M = a.shape
    N = b.shape[1]
    tt = min(tt, T)
    assert M % tm == 0 and N % tn == 0 and T % tt == 0
    nt = T // tt
    token_spec, token_arg = _token_operand(after)

    def body(a_ref, b_ref, *rest):
        o_ref, acc_ref = rest[-2:]
        t = pl.program_id(2)

        @pl.when(t == 0)
        def _():
            acc_ref[...] = jnp.zeros_like(acc_ref)

        bv = b_ref[...]
        if b_scale is not None:
            bv = bv * b_scale
        acc_ref[...] += _dot_tn(a_ref[...].astype(BF16), bv.astype(BF16))

        @pl.when(t == nt - 1)
        def _():
            o_ref[...] = acc_ref[...].astype(BF16)

    return pl.pallas_call(
        body, name=name, grid=(M // tm, N // tn, nt),
        in_specs=[pl.BlockSpec((tt, tm), lambda i, j, t: (t, i)), pl.BlockSpec((tt, tn), lambda i, j, t: (t, j))]
        + token_spec,
        out_specs=pl.BlockSpec((tm, tn), lambda i, j, t: (i, j)),
        out_shape=SDS((M, N), BF16),
        scratch_shapes=[pltpu.VMEM((tm, tn), F32)],
        compiler_params=_cp(("parallel", "parallel", "arbitrary")),
    )(a, b, *token_arg)


def mix_in_fwd(h1, g, win_t):
    T = h1.shape[0]
    TM = 512

    def body(h_ref, g_ref, w_ref, u_ref, q_ref, kv_ref, z_ref, gt_ref):
        x = h_ref[...]
        u = (x * _rstd(x) * g_ref[...]).astype(BF16)
        u_ref[...] = u
        for c in range(0, AW, 256):
            q_ref[:, c:c + 256] = _dot_nt(u, w_ref[c:c + 256, :]).astype(BF16)
        kv_ref[...] = _dot_nt(u, w_ref[C_KV:C_Z, :]).astype(BF16)
        for c in range(0, PW, 256):
            z_ref[:, c:c + 256] = _dot_nt(u, w_ref[C_Z + c:C_Z + c + 256, :])
        for c in range(0, 2 * D, 256):
            gt_ref[:, c:c + 256] = _dot_nt(u, w_ref[C_G + c:C_G + c + 256, :]).astype(BF16)

    tile = lambda w: pl.BlockSpec((TM, w), lambda i: (i, 0))
    return pl.pallas_call(
        body, name="mix_in_fwd", grid=(T // TM,),
        in_specs=[tile(D), _const_spec((1, D)), _const_spec((INW, D))],
        out_specs=[tile(D), tile(AW), tile(2 * KVW), tile(PW), tile(2 * D)],
        out_shape=[SDS((T, D), BF16), SDS((T, AW), BF16), SDS((T, 2 * KVW), BF16), SDS((T, PW), F32),
                   SDS((T, 2 * D), BF16)],
        compiler_params=_cp(("arbitrary",)),
    )(h1, g, win_t)


def mix_in_bwd(dq, dkv, dz, dgt, dh2, h1, g, win_t):
    T = h1.shape[0]
    TM = 512

    def body(dq_ref, dkv_ref, dz_ref, dgt_ref, dh2_ref, h_ref, g_ref, w_ref, dh1_ref, dg_ref):
        du = _dot(dq_ref[...], w_ref[0:AW, :])
        du += _dot(dkv_ref[...], w_ref[C_KV:C_Z, :])
        du += _dot(dz_ref[...], w_ref[C_Z:C_G, :])
        du += _dot(dgt_ref[...], w_ref[C_G:INW, :])
        x = h_ref[...]
        g = g_ref[...]
        rstd = _rstd(x)
        xhat = x * rstd
        dh1_ref[...] = dh2_ref[...] + _rms_bwd(du, xhat, rstd, g)

        @pl.when(pl.program_id(0) == 0)
        def _():
            dg_ref[...] = jnp.zeros_like(dg_ref)

        dg_ref[...] += jnp.sum(du * xhat, axis=0, keepdims=True)

    tile = lambda w: pl.BlockSpec((TM, w), lambda i: (i, 0))
    return pl.pallas_call(
        body, name="mix_in_bwd", grid=(T // TM,),
        in_specs=[tile(AW), tile(2 * KVW), tile(PW), tile(2 * D), tile(D), tile(D), _const_spec((1, D)),
                  _const_spec((INW, D))],
        out_specs=[tile(D), pl.BlockSpec((1, D), lambda i: (0, 0))],
        out_shape=[SDS((T, D), F32), SDS((1, D), F32)],
        compiler_params=_cp(("arbitrary",)),
    )(dq, dkv, dz, dgt, dh2, h1, g, win_t)


PAIR = 2 * HD
NPAIR = GQ // 2


def _lo_lanes():
    return lax.broadcasted_iota(jnp.int32, (BLK, PAIR), 1) < HD


def _stack_heads(ref, kvh, scale=None):
    lo = _lo_lanes()
    parts = []
    for pr in range(NPAIR):
        t = ref[:, (kvh * NPAIR + pr) * PAIR:(kvh * NPAIR + pr + 1) * PAIR]
        if scale is not None:
            t = t * scale
        zero = jnp.zeros_like(t)
        parts += [jnp.where(lo, t, zero), jnp.where(lo, zero, t)]
    return jnp.concatenate(parts, axis=0)


def _kv_tiles(kvc_ref, kvp_ref, tile, kvh):
    lo = _lo_lanes()
    dup, left, right = [], [], []
    for ref in (kvp_ref, kvc_ref):
        t = ref[:, tile * PAIR:(tile + 1) * PAIR]
        r = pltpu.roll(t.astype(F32), HD, 1).astype(BF16)
        zero = jnp.zeros_like(t)
        a, b = (t, r) if kvh == 0 else (r, t)
        dup.append(jnp.where(lo, a, b))
        left.append(jnp.where(lo, a, zero))
        right.append(jnp.where(lo, zero, b))
    cat = lambda xs: jnp.concatenate(xs, axis=0)
    return cat(dup), cat(left), cat(right)


def _band_consts(first):
    row = lax.broadcasted_iota(jnp.int32, (BLK, BLK), 0)
    col = lax.broadcasted_iota(jnp.int32, (BLK, BLK), 1)
    upper = col > row
    dist = jnp.where(upper, row - col + BLK, row - col).astype(F32)
    pen = jnp.where(jnp.logical_and(upper, first), -jnp.inf, 0.0)
    return upper, dist, pen


def _split_band(upper, t):
    zero = jnp.zeros_like(t)
    return jnp.concatenate([jnp.where(upper, t, zero), jnp.where(upper, zero, t)], axis=1)


def attn_fwd(q, kv, sinks):
    T = q.shape[0]
    nb = T // BLK

    def body(sink_ref, q_ref, kvc_ref, kvp_ref, att_ref, lse_ref):
        upper, dist, pen = _band_consts(pl.program_id(0) == 0)
        scores, values = [], []
        for kvh in range(NKV):
            kdup, _, _ = _kv_tiles(kvc_ref, kvp_ref, 0, kvh)
            values.append(_kv_tiles(kvc_ref, kvp_ref, 1, kvh)[1:])
            scores.append(_dot_nt(_stack_heads(q_ref, kvh, SCALE), kdup))
        for kvh in range(NKV):
            s_all = scores[kvh]
            vleft, vright = values[kvh]
            for pr in range(NPAIR):
                outs, inv = [], []
                for side, vpad in ((0, vleft), (1, vright)):
                    g = 2 * pr + side
                    hq = kvh * GQ + g
                    sink = sink_ref[0, hq]
                    rows = slice(g * BLK, (g + 1) * BLK)
                    s = jnp.where(upper, s_all[rows, 0:BLK], s_all[rows, BLK:2 * BLK]) - SLOPES[hq] * dist + pen
                    m = jnp.maximum(jnp.max(s, axis=-1, keepdims=True), sink)
                    p = jnp.exp(s - m)
                    l = jnp.sum(p, axis=-1, keepdims=True) + jnp.exp(sink - m)
                    lse_ref[:, hq:hq + 1] = m + jnp.log(l)
                    outs.append(_dot(_split_band(upper, p.astype(BF16)), vpad))
                    inv.append(1.0 / l)
                col0 = (kvh * NPAIR + pr) * PAIR
                att_ref[:, col0:col0 + PAIR] = ((outs[0] + outs[1]) * jnp.where(_lo_lanes(), inv[0], inv[1])).astype(BF16)

    return pl.pallas_call(
        body, name="attn_fwd", grid=(nb,),
        in_specs=[pl.BlockSpec(memory_space=pltpu.SMEM),
                  pl.BlockSpec((BLK, AW), lambda i: (i, 0)),
                  pl.BlockSpec((BLK, 2 * KVW), lambda i: (i, 0)),
                  pl.BlockSpec((BLK, 2 * KVW), lambda i: (jnp.maximum(i - 1, 0), 0))],
        out_specs=[pl.BlockSpec((BLK, AW), lambda i: (i, 0)), pl.BlockSpec((BLK, NQ), lambda i: (i, 0))],
        out_shape=[SDS((T, AW), BF16), SDS((T, NQ), F32)],
        compiler_params=_cp(("arbitrary",)),
    )(sinks, q, kv, kv)


def attn_bwd(q, kv, datt, att, lse, sinks):
    T = q.shape[0]
    nb = T // BLK

    def body(sink_ref, q_ref, kvc_ref, kvp_ref, do_ref, out_ref, lse_ref, dq_ref, dkv_ref, dsink_ref, carry_ref):
        i = pl.program_id(0)

        @pl.when(i == 0)
        def _():
            dsink_ref[...] = jnp.zeros_like(dsink_ref)
            carry_ref[...] = jnp.zeros_like(carry_ref)

        @pl.when(i < nb)
        def _():
            upper, dist, pen = _band_consts(i == 0)
            lo = _lo_lanes()
            dk_dup, dv_dup = [], []
            staged = []
            for kvh in range(NKV):
                kdup, kleft, kright = _kv_tiles(kvc_ref, kvp_ref, 0, kvh)
                vdup, _, _ = _kv_tiles(kvc_ref, kvp_ref, 1, kvh)
                qs = _stack_heads(q_ref, kvh, SCALE)
                dos = _stack_heads(do_ref, kvh)
                staged.append((kleft, kright, qs, dos, _dot_nt(qs, kdup), _dot_nt(dos, vdup)))
            deltas = []
            for pair in range(NQ // 2):
                cols = slice(pair * PAIR, (pair + 1) * PAIR)
                t = do_ref[:, cols].astype(F32) * out_ref[:, cols].astype(F32)
                deltas += [jnp.sum(jnp.where(lo, t, 0.0), axis=-1, keepdims=True),
                           jnp.sum(jnp.where(lo, 0.0, t), axis=-1, keepdims=True)]
            for kvh in range(NKV):
                kleft, kright, qs, dos, s_all, dp_all = staged[kvh]
                ds_parts, p_parts = [], []
                for pr in range(NPAIR):
                    dq = None
                    for side, kpad in ((0, kleft), (1, kright)):
                        g = 2 * pr + side
                        hq = kvh * GQ + g
                        lse_h = lse_ref[:, hq:hq + 1]
                        rows = slice(g * BLK, (g + 1) * BLK)
                        s = jnp.where(upper, s_all[rows, 0:BLK], s_all[rows, BLK:2 * BLK]) - SLOPES[hq] * dist + pen
                        p = jnp.exp(s - lse_h)
                        dp = jnp.where(upper, dp_all[rows, 0:BLK], dp_all[rows, BLK:2 * BLK])
                        delta = deltas[hq]
                        dsink_ref[:, hq:hq + 1] += -jnp.sum(jnp.exp(sink_ref[0, hq] - lse_h) * delta, axis=0,
                                                            keepdims=True)
                        ds = _split_band(upper, (p * (dp - delta)).astype(BF16))
                        ds_parts.append(ds)
                        p_parts.append(_split_band(upper, p.astype(BF16)))
                        d = _dot(ds, kpad)
                        dq = d if dq is None else dq + d
                    col0 = (kvh * NPAIR + pr) * PAIR
                    dq_ref[:, col0:col0 + PAIR] = (dq * SCALE).astype(BF16)
                dkw = _dot_tn(qs, jnp.concatenate(ds_parts, axis=0)).T
                dvw = _dot_tn(dos, jnp.concatenate(p_parts, axis=0)).T
                dk_dup.append(dkw + pltpu.roll(dkw, HD, 1))
                dv_dup.append(dvw + pltpu.roll(dvw, HD, 1))
            dk = jnp.where(jnp.concatenate([lo, lo], axis=0), dk_dup[0], dk_dup[1])
            dv = jnp.where(jnp.concatenate([lo, lo], axis=0), dv_dup[0], dv_dup[1])
            dkv_ref[:, 0:PAIR] = (carry_ref[:, 0:PAIR] + dk[0:BLK]).astype(BF16)
            dkv_ref[:, PAIR:2 * PAIR] = (carry_ref[:, PAIR:2 * PAIR] + dv[0:BLK]).astype(BF16)
            carry_ref[:, 0:PAIR] = dk[BLK:2 * BLK]
            carry_ref[:, PAIR:2 * PAIR] = dv[BLK:2 * BLK]

        @pl.when(i == nb)
        def _():
            dkv_ref[...] = carry_ref[...].astype(BF16)

    cur = lambda i: (jnp.minimum(i, nb - 1), 0)
    prev = lambda i: (jnp.maximum(jnp.minimum(i, nb - 1) - 1, 0), 0)
    return pl.pallas_call(
        body, name="attn_bwd", grid=(nb + 1,),
        in_specs=[pl.BlockSpec(memory_space=pltpu.SMEM),
                  pl.BlockSpec((BLK, AW), cur), pl.BlockSpec((BLK, 2 * KVW), cur), pl.BlockSpec((BLK, 2 * KVW), prev),
                  pl.BlockSpec((BLK, AW), cur), pl.BlockSpec((BLK, AW), cur), pl.BlockSpec((BLK, NQ), cur)],
        out_specs=[pl.BlockSpec((BLK, AW), cur),
                   pl.BlockSpec((BLK, 2 * KVW), lambda i: (jnp.maximum(i - 1, 0), 0)),
                   pl.BlockSpec((1, NQ), lambda i: (0, 0))],
        out_shape=[SDS((T, AW), BF16), SDS((T, 2 * KVW), BF16), SDS((1, NQ), F32)],
        scratch_shapes=[pltpu.VMEM((BLK, 2 * KVW), F32)],
        compiler_params=_cp(("arbitrary",)),
    )(sinks, q, kv, kv, datt, att, lse)


def _inv_counts(t0, rows):
    t = (t0 + lax.broadcasted_iota(jnp.int32, (rows, 1), 0) + 1).astype(F32)
    return [1.0 / jnp.minimum(t, float(w)) for w in POOL_WINDOWS]


def pool_fwd(z, wmix, scale):
    T = z.shape[0]
    TM = 512
    L = TM + HALO

    def body(z_ref, halo_ref, wmix_ref, scale_ref, pooled_ref, mixs_ref):
        i = pl.program_id(0)
        halo = jnp.where(i > 0, halo_ref[...], 0.0)
        zt = z_ref[...]
        e = jnp.concatenate([halo, zt], axis=0)
        sums = []
        s = e
        for k in (1, 2, 4, 8):
            s = s + pltpu.roll(s, k, 0)
            sums.append(s)
        inv = _inv_counts(i * TM, TM)
        for gi in range(len(POOL_WINDOWS)):
            cols = slice(gi * PG, (gi + 1) * PG)
            pooled = (sums[gi][HALO:, cols] * inv[gi] - zt[:, cols]).astype(BF16)
            pooled_ref[:, cols] = pooled
            mixs_ref[:, cols] = (_dot(pooled, wmix_ref[gi]) * scale_ref[:, cols]).astype(BF16)

    return pl.pallas_call(
        body, name="pool_fwd", grid=(T // TM,),
        in_specs=[pl.BlockSpec((TM, PW), lambda i: (i, 0)),
                  pl.BlockSpec((HALO, PW), lambda i: (jnp.maximum(i * (TM // HALO) - 1, 0), 0)),
                  _const_spec((len(POOL_WINDOWS), PG, PG)), _const_spec((1, PW))],
        out_specs=[pl.BlockSpec((TM, PW), lambda i: (i, 0)), pl.BlockSpec((TM, PW), lambda i: (i, 0))],
        out_shape=[SDS((T, PW), BF16), SDS((T, PW), BF16)],
        compiler_params=_cp(("arbitrary",)),
    )(z, z, wmix, scale)


def pool_bwd(dmixs, pooled, wmix, scale):
    T = dmixs.shape[0]
    TM = 512
    L = TM + HALO
    nt = T // TM

    def body(dm_ref, halo_ref, pooled_ref, wmix_ref, scale_ref, dz_ref, dwmix_ref, dscale_ref):
        i = pl.program_id(0)

        @pl.when(i == 0)
        def _():
            dwmix_ref[...] = jnp.zeros_like(dwmix_ref)
            dscale_ref[...] = jnp.zeros_like(dscale_ref)

        halo = jnp.where(i < nt - 1, halo_ref[...], 0.0)
        dm = dm_ref[...]
        e = jnp.concatenate([dm, halo], axis=0)
        inv = _inv_counts(i * TM, L)
        for gi in range(len(POOL_WINDOWS)):
            cols = slice(gi * PG, (gi + 1) * PG)
            w = wmix_ref[gi]
            dmixed = (e[:, cols] * scale_ref[:, cols]).astype(BF16)
            dpooled = _dot_nt(dmixed, w)
            pooled = pooled_ref[:, cols]
            mixed = _dot(pooled, w)
            dscale_ref[:, cols] += jnp.sum(dm[:, cols] * mixed, axis=0, keepdims=True)
            dwmix_ref[gi] += _dot_tn(pooled, dmixed[:TM])
            s = dpooled * inv[gi]
            k = 1
            while k < POOL_WINDOWS[gi]:
                s = s + pltpu.roll(s, L - k, 0)
                k *= 2
            dz_ref[:, cols] = (s[:TM] - dpooled[:TM]).astype(BF16)

    return pl.pallas_call(
        body, name="pool_bwd", grid=(nt,),
        in_specs=[pl.BlockSpec((TM, PW), lambda i: (i, 0)),
                  pl.BlockSpec((HALO, PW), lambda i: (jnp.minimum((i + 1) * (TM // HALO), T // HALO - 1), 0)),
                  pl.BlockSpec((TM, PW), lambda i: (i, 0)),
                  _const_spec((len(POOL_WINDOWS), PG, PG)), _const_spec((1, PW))],
        out_specs=[pl.BlockSpec((TM, PW), lambda i: (i, 0)),
                   pl.BlockSpec((len(POOL_WINDOWS), PG, PG), lambda i: (0, 0, 0)),
                   pl.BlockSpec((1, PW), lambda i: (0, 0))],
        out_shape=[SDS((T, PW), BF16), SDS((len(POOL_WINDOWS), PG, PG), F32), SDS((1, PW), F32)],
        compiler_params=_cp(("arbitrary",)),
    )(dmixs, dmixs, pooled, wmix, scale)


def merge_fwd(att, mixs, gt, h1, wattn, wpool, wout):
    T = h1.shape[0]
    TM = 512

    def body(att_ref, mixs_ref, gt_ref, h_ref, wa_ref, wp_ref, wo_ref, h2_ref, mg_ref, gf_ref):
        a = _dot(att_ref[...], wa_ref[...])
        p = _dot(mixs_ref[...], wp_ref[...])
        sa = jax.nn.sigmoid(gt_ref[:, 0:D].astype(F32))
        sp = jax.nn.sigmoid(gt_ref[:, D:2 * D].astype(F32))
        gf_ref[:, 0:D] = (a * sa * (1.0 - sa)).astype(BF16)
        gf_ref[:, D:2 * D] = (p * sp * (1.0 - sp)).astype(BF16)
        mg = (sa * a + sp * p).astype(BF16)
        mg_ref[...] = mg
        h2_ref[...] = h_ref[...] + _dot(mg, wo_ref[...])

    tile = lambda w: pl.BlockSpec((TM, w), lambda i: (i, 0))
    return pl.pallas_call(
        body, name="merge_fwd", grid=(T // TM,),
        in_specs=[tile(AW), tile(PW), tile(2 * D), tile(D), _const_spec((AW, D)), _const_spec((PW, D)),
                  _const_spec((D, D))],
        out_specs=[tile(D), tile(D), tile(2 * D)],
        out_shape=[SDS((T, D), F32), SDS((T, D), BF16), SDS((T, 2 * D), BF16)],
        compiler_params=_cp(("arbitrary",)),
    )(att, mixs, gt, h1, wattn, wpool, wout)


def merge_bwd(dh2, gt, gf, wattn, wpool, wout, after=None):
    T = dh2.shape[0]
    TM = 512
    token_spec, token_arg = _token_operand(after)

    def body(dh2_ref, gt_ref, gf_ref, wa_ref, wp_ref, wo_ref, *rest):
        datt_ref, dmixs_ref, dgt_ref, da_ref, dp_ref = rest[-5:]
        dm = _dot_nt(dh2_ref[...].astype(BF16), wo_ref[...])
        da = (dm * jax.nn.sigmoid(gt_ref[:, 0:D].astype(F32))).astype(BF16)
        dp = (dm * jax.nn.sigmoid(gt_ref[:, D:2 * D].astype(F32))).astype(BF16)
        da_ref[...] = da
        dp_ref[...] = dp
        dgt_ref[:, 0:D] = (dm * gf_ref[:, 0:D].astype(F32)).astype(BF16)
        dgt_ref[:, D:2 * D] = (dm * gf_ref[:, D:2 * D].astype(F32)).astype(BF16)
        datt_ref[...] = _dot_nt(da, wa_ref[...]).astype(BF16)
        dmixs_ref[...] = _dot_nt(dp, wp_ref[...])

    tile = lambda w: pl.BlockSpec((TM, w), lambda i: (i, 0))
    return pl.pallas_call(
        body, name="merge_bwd", grid=(T // TM,),
        in_specs=[tile(D), tile(2 * D), tile(2 * D), _const_spec((AW, D)), _const_spec((PW, D)),
                  _const_spec((D, D))] + token_spec,
        out_specs=[tile(AW), tile(PW), tile(2 * D), tile(D), tile(D)],
        out_shape=[SDS((T, AW), BF16), SDS((T, PW), F32), SDS((T, 2 * D), BF16), SDS((T, D), BF16),
                   SDS((T, D), BF16)],
        compiler_params=_cp(("arbitrary",)),
    )(dh2, gt, gf, wattn, wpool, wout, *token_arg)


def adamw(w, g, m, v, name):
    R, C = w.shape
    tile_bytes = 2 * 1024 * 1024
    tr = R
    if R * C * 4 > tile_bytes:
        tr = next(cand for cand in (512, 256, 128, 64, 32, 16, 8) if R % cand == 0 and cand * C * 4 <= tile_bytes)
    c1 = 1.0 - B1 ** STEP
    c2 = 1.0 - B2 ** STEP

    def body(w_ref, g_ref, m_ref, v_ref, go_ref, d_ref, nm_ref, nv_ref):
        gv = g_ref[...]
        go_ref[...] = gv
        nm = B1 * m_ref[...] + (1.0 - B1) * gv
        nv = B2 * v_ref[...] + (1.0 - B2) * (gv * gv)
        nm_ref[...] = nm
        nv_ref[...] = nv
        d_ref[...] = -LR * ((nm / c1) / (jnp.sqrt(nv / c2) + ADAM_EPS) + WD * w_ref[...])

    spec = pl.BlockSpec((tr, C), lambda i: (i, 0))
    return pl.pallas_call(
        body, name=name, grid=(R // tr,),
        in_specs=[spec] * 4, out_specs=[spec] * 4, out_shape=[SDS((R, C), F32)] * 4,
        compiler_params=_cp(("parallel",)),
    )(w, g, m, v)


GROUP_FFN1, GROUP_MIX, GROUP_FFN2 = 0, 1, 2


def _behind(small, token):
    return small if token is None else small + token[0:1, 0:1]


def local_fwd_bwd(x, target, S, comm):
    W = comm.weights(GROUP_FFN1, None)
    h1, ab1, hid1 = ffn_fwd(x, _behind(S["ffn1_norm"], comm.started()), W["ffn1_w_up"], W["ffn1_w_down"], "ffn1_fwd")
    W = comm.weights(GROUP_MIX, h1)
    u, q, kv, z, gt = mix_in_fwd(h1, _behind(S["mix_norm"], comm.started()), W["w_in"])
    att, lse = attn_fwd(q, kv, S["sinks"])
    pooled, mixs = pool_fwd(z, S["pool_w_mix"], S["pool_scale"])
    h2, merged, gf = merge_fwd(att, mixs, gt, h1, W["w_attn_up"], W["w_pool_up"], W["w_out"])
    W = comm.weights(GROUP_FFN2, h2)
    dh3, ab2, hid2, loss, g_final = ffn_fwd(h2, S["ffn2_norm"], W["ffn2_w_up"], W["ffn2_w_down"], "ffn2_fwd",
                                            head=(target, S["final_norm"]))

    G = {"final_norm": g_final}
    dh2, dup2, n2, G["ffn2_norm"] = ffn_bwd_x(dh3, h2, ab2, S["ffn2_norm"], W["ffn2_w_up"], W["ffn2_w_down"],
                                              "ffn2_bwd_x")
    ffn2 = ("ffn2_w_down", "ffn2_w_up")
    token = comm.grads({"ffn2_w_down": matmul_tn(hid2, dh3, "ffn2_dw_down", tm=1408, tn=512, tt=4096, b_scale=0.5),
                        "ffn2_w_up": matmul_tn(n2, dup2, "ffn2_dw_up", tm=D, tn=512, tt=4096)})

    W = comm.weights(GROUP_MIX, None)
    datt, dmixs, dgt, da, dp = merge_bwd(dh2, gt, gf, W["w_attn_up"], W["w_pool_up"], W["w_out"], after=token)
    token = comm.advance(ffn2, datt)
    g_mix = {"w_out": matmul_tn(merged, dh2, "dw_out", tm=D, tn=D, after=token),
             "w_attn_up": matmul_tn(att, da, "dw_attn_up", tm=AW, tn=D),
             "w_pool_up": matmul_tn(mixs, dp, "dw_pool_up", tm=PW, tn=D)}
    dz, G["pool_w_mix"], G["pool_scale"] = pool_bwd(dmixs, pooled, S["pool_w_mix"], _behind(S["pool_scale"], token))
    dq, dkv, G["sinks"] = attn_bwd(q, kv, datt, att, lse, _behind(S["sinks"], token))
    dh1, G["mix_norm"] = mix_in_bwd(dq, dkv, dz, dgt, dh2, h1, S["mix_norm"], W["w_in"])
    g_mix["w_in"] = jnp.concatenate([
        matmul_tn(dq, u, "dw_in_q", tm=AW, tn=D),
        matmul_tn(dkv, u, "dw_in_kv", tm=2 * KVW, tn=D),
        matmul_tn(dz, u, "dw_in_z", tm=PW, tn=D),
        matmul_tn(dgt, u, "dw_in_g", tm=D, tn=512, tt=4096),
    ], axis=0)
    mix = tuple(g_mix)
    token = comm.grads(g_mix)

    W = comm.weights(GROUP_FFN1, None)
    g_dn1 = matmul_tn(hid1, dh1, "ffn1_dw_down", tm=1408, tn=512, tt=4096, b_scale=0.5, after=token)
    token = comm.advance(mix, g_dn1)
    token2 = comm.grads({"ffn1_w_down": g_dn1})
    g1 = _behind(_behind(S["ffn1_norm"], token), token2)
    dx, dup1, n1, G["ffn1_norm"] = ffn_bwd_x(dh1, x, ab1, g1, W["ffn1_w_up"], W["ffn1_w_down"], "ffn1_bwd_x")
    token = comm.advance(("ffn1_w_down",), dx)
    token2 = comm.small_start(G, loss, dx)
    g_up1 = matmul_tn(n1, dup1, "ffn1_dw_up", tm=D, tn=512, tt=4096,
                      after=None if token is None else token + token2)
    token = comm.grads({"ffn1_w_up": g_up1})
    done = comm.finish(mix, token)
    token = comm.advance(("ffn1_w_up",), done)
    done = comm.finish(("ffn1_w_down",), token)
    done = comm.small_finish(done)
    done = comm.finish(ffn2, done)
    comm.finish(("ffn1_w_up",), done)
    return loss, dx


HBM_SPEC = pl.BlockSpec(memory_space=pltpu.HBM)


def _me():
    return lax.axis_index("x"), lax.axis_index("y"), lax.axis_index("c")


def _peer_chip(x, y, k):
    return x ^ (k >> 1), y ^ (k & 1)


def _piece_half(ref, rowlike, j, h):
    if rowlike:
        return ref.at[j, h]
    ns = ref.shape[-1] // N_CHIPS
    return ref.at[h, :, pl.ds(pl.multiple_of(j * ns, 128), ns)]


def _piece(ref, rowlike, j):
    if rowlike:
        return ref.at[j]
    ns = ref.shape[-1] // N_CHIPS
    return ref.at[:, :, pl.ds(pl.multiple_of(j * ns, 128), ns)]


def _full_shape(shard_view, rowlike):
    _, kh, ns = shard_view.shape
    return (N_CHIPS, 2, kh, ns) if rowlike else (2, kh, N_CHIPS * ns)


def _remote(src, dst, send_sem, recv_sem, dev):
    return pltpu.make_async_remote_copy(src, dst, send_sem, recv_sem, device_id=dev, device_id_type=MESH)


SEM_SPEC = pl.BlockSpec(memory_space=pltpu.SEMAPHORE)
SPLIT_PARAMS = pltpu.CompilerParams(has_side_effects=pltpu.SideEffectType.DATAFLOW_SIDE_EFFECTING)


def _gather_plan(rowlikes):
    def plan(s_refs, f_refs, a, k, x, y, c):
        px, py = _peer_chip(x, y, k)
        return (s_refs[a].at[c], _piece_half(f_refs[a], rowlikes[a], 2 * x + y, c),
                _piece_half(f_refs[a], rowlikes[a], 2 * px + py, c), (px, py, c))
    return plan


def _all_to_all_plan(rowlikes):
    def plan(q_refs, r_refs, a, k, x, y, c):
        px, py = _peer_chip(x, y, k)
        if rowlikes[a]:
            src = q_refs[a].at[2 * px + py]
        else:
            ns = q_refs[a].shape[-1] // N_CHIPS
            src = q_refs[a].at[:, pl.ds(pl.multiple_of((2 * px + py) * ns, 128), ns)]
        return src, r_refs[a].at[k - 1], r_refs[a].at[k - 1], (px, py, c)
    return plan


def _swap_plan(rowlikes):
    def plan(g_refs, got_refs, a, k, x, y, c):
        src = g_refs[a].at[:, 1 - c] if rowlikes[a] else g_refs[a].at[1 - c]
        return src, got_refs[a], got_refs[a], (x, y, 1 - c)
    return plan


def _everyone_plan(s_refs, slot_refs, a, k, x, y, c):
    px, py, pc = x ^ (k >> 2), y ^ ((k >> 1) & 1), c ^ (k & 1)
    return s_refs[a], slot_refs[a].at[4 * x + 2 * y + c], slot_refs[a].at[4 * px + 2 * py + pc], (px, py, pc)


CHIPS, SIBLING, EVERYONE = (1, 2, 3), (1,), tuple(range(1, 8))


def _as_list(after):
    return [] if after is None else list(after) if isinstance(after, (list, tuple)) else [after]


def exchange_start(srcs, land_shapes, plan, after, name, peers=CHIPS):
    n = len(srcs)
    lands = [lax.empty(shape, s.dtype) for shape, s in zip(land_shapes, srcs)]

    behind = _as_list(after)

    def body(*refs):
        s_refs, l_refs = refs[:n], refs[n:2 * n]
        send_sems, recv_sems = refs[2 * n + len(behind)], refs[2 * n + len(behind) + 1]
        token = refs[-1]
        x, y, c = _me()
        for a in range(n):
            for i, k in enumerate(peers):
                src, dst, _, peer = plan(s_refs, l_refs, a, k, x, y, c)
                sem = len(peers) * a + i
                _remote(src, dst, send_sems.at[sem], recv_sems.at[sem], peer).start()
        token[...] = jnp.zeros_like(token)

    n_sems = len(peers) * n
    outs = pl.pallas_call(
        body, name=name, in_specs=[HBM_SPEC] * (2 * n) + [pl.BlockSpec(memory_space=pl.ANY)] * len(behind),
        out_specs=[SEM_SPEC, SEM_SPEC] + [HBM_SPEC] * (2 * n) + [pl.BlockSpec(memory_space=pltpu.VMEM)],
        out_shape=[pltpu.SemaphoreType.DMA((n_sems,)), pltpu.SemaphoreType.DMA((n_sems,))]
        + [pltpu.HBM(a.shape, a.dtype) for a in (*srcs, *lands)] + [SDS((8, 128), F32)],
        input_output_aliases={i: 2 + i for i in range(2 * n)},
        compiler_params=SPLIT_PARAMS,
    )(*[pltpu.with_memory_space_constraint(a, pltpu.HBM) for a in (*srcs, *lands)], *behind)
    return {"sems": outs[:2], "srcs": outs[2:2 + n], "lands": outs[2 + n:2 + 2 * n], "token": outs[-1]}


def exchange_wait(state, plan, after, name, peers=CHIPS):
    n = len(state["srcs"])
    behind = _as_list(after)

    def body(*refs):
        s_refs, l_refs = refs[:n], refs[n:2 * n]
        send_sems, recv_sems = refs[2 * n], refs[2 * n + 1]
        x, y, c = _me()
        for a in range(n):
            for i, k in enumerate(peers):
                src, _, landing, peer = plan(s_refs, l_refs, a, k, x, y, c)
                sem = len(peers) * a + i
                cp = _remote(src, landing, send_sems.at[sem], recv_sems.at[sem], peer)
                cp.wait_send()
                cp.wait_recv()

    bufs = (*state["srcs"], *state["lands"])
    outs = pl.pallas_call(
        body, name=name,
        in_specs=[HBM_SPEC] * (2 * n) + [SEM_SPEC, SEM_SPEC] + [pl.BlockSpec(memory_space=pl.ANY)] * len(behind),
        out_specs=[HBM_SPEC] * (2 * n),
        out_shape=[pltpu.HBM(a.shape, a.dtype) for a in bufs],
        input_output_aliases={i: i for i in range(2 * n)},
        compiler_params=SPLIT_PARAMS,
    )(*bufs, *state["sems"], *behind)
    return outs[:n], outs[n:]


def gather_finish(shards, fulls, rowlikes, name):
    n = len(shards)

    def body(*refs):
        s_refs, f_refs = refs[:n], refs[2 * n:3 * n]
        send_sems, recv_sems = refs[3 * n:]
        x, y, c = _me()
        chip = 2 * x + y
        sib = (x, y, 1 - c)
        sends = []
        for a in range(n):
            own = _piece(f_refs[a], rowlikes[a], chip)
            cp = _remote(s_refs[a], own, send_sems.at[a, 0], recv_sems.at[a, 0], sib)
            cp.start()
            sends.append(cp)
            for k in (1, 2, 3):
                px, py = _peer_chip(x, y, k)
                slot = _piece_half(f_refs[a], rowlikes[a], 2 * px + py, c)
                cp = _remote(slot, slot, send_sems.at[a, k], recv_sems.at[a, k], sib)
                cp.start()
                sends.append(cp)
        for a in range(n):
            own = _piece(f_refs[a], rowlikes[a], chip)
            _remote(own, own, send_sems.at[a, 0], recv_sems.at[a, 0], sib).wait_recv()
            for k in (1, 2, 3):
                px, py = _peer_chip(x, y, k)
                slot = _piece_half(f_refs[a], rowlikes[a], 2 * px + py, 1 - c)
                _remote(slot, slot, send_sems.at[a, k], recv_sems.at[a, k], sib).wait_recv()
        for cp in sends:
            cp.wait_send()

    return pl.pallas_call(
        body, name=name, in_specs=[HBM_SPEC] * (2 * n), out_specs=[HBM_SPEC] * n,
        out_shape=[SDS(f.shape, f.dtype) for f in fulls],
        input_output_aliases={n + a: a for a in range(n)},
        scratch_shapes=[pltpu.SemaphoreType.DMA((n, 4)), pltpu.SemaphoreType.DMA((n, 4))],
    )(*shards, *fulls)


def _half_buffer_shape(gview, rowlike):
    return (N_CHIPS,) + gview.shape[2:] if rowlike else gview.shape[1:]


ROW_TILES = 2


def _piece_specs(rowlike, kh, ns, piece, half):
    tr = kh // ROW_TILES
    if rowlike:
        return (pl.BlockSpec((None, None, tr, ns), lambda *g: (piece(*g), half(*g), g[-2], 0)),
                pl.BlockSpec((None, tr, ns), lambda *g: (piece(*g), g[-2], 0)))
    return (pl.BlockSpec((None, tr, ns), lambda *g: (half(*g), g[-2], piece(*g))),
            pl.BlockSpec((tr, ns), lambda *g: (g[-2], piece(*g))))


def add_halves(gview, got, rowlike, place, name):
    kh, ns = (gview.shape[2], gview.shape[3]) if rowlike else (gview.shape[1], gview.shape[2] // N_CHIPS)

    def body(place_ref, g_ref, got_ref, o_ref):
        o_ref[...] = (g_ref[...].astype(F32) + got_ref[...].astype(F32)).astype(BF16)

    g_spec, h_spec = _piece_specs(rowlike, kh, ns, lambda j, r, p: j, lambda j, r, p: p[0])
    return pl.pallas_call(
        body, name=name,
        grid_spec=pltpu.PrefetchScalarGridSpec(num_scalar_prefetch=1, grid=(N_CHIPS, ROW_TILES),
                                               in_specs=[g_spec, h_spec], out_specs=h_spec),
        out_shape=SDS(got.shape, BF16),
        compiler_params=_cp(("parallel", "parallel")),
    )(place, gview, got)


def _slot_shape(q, rowlike):
    return (3,) + (q.shape[1:] if rowlike else (q.shape[0], q.shape[1] // N_CHIPS))


def sum_pieces(q, recv, rowlike, place, name):
    kh, ns = recv.shape[1], recv.shape[2]
    tr = kh // ROW_TILES

    def body(place_ref, q_ref, r_ref, o_ref):
        acc = q_ref[...].astype(F32)
        for k in range(3):
            acc = acc + r_ref[k].astype(F32)
        o_ref[...] = acc

    _, h_spec = _piece_specs(rowlike, kh, ns, lambda z, r, p: p[1], lambda z, r, p: p[0])
    return pl.pallas_call(
        body, name=name,
        grid_spec=pltpu.PrefetchScalarGridSpec(
            num_scalar_prefetch=1, grid=(1, ROW_TILES),
            in_specs=[h_spec, pl.BlockSpec((3, tr, ns), lambda z, r, p: (0, r, 0))],
            out_specs=pl.BlockSpec((None, tr, ns), lambda z, r, p: (p[0], r, 0))),
        out_shape=SDS((2, kh, ns), F32),
        compiler_params=_cp(("parallel", "parallel")),
    )(place, q, recv)


def join_halves(halves, name):
    n = len(halves)

    def body(*refs):
        o_refs = refs[n:2 * n]
        send_sems, recv_sems = refs[2 * n:]
        x, y, c = _me()
        sib = (x, y, 1 - c)
        sends = []
        for a in range(n):
            cp = _remote(o_refs[a].at[c], o_refs[a].at[c], send_sems.at[a], recv_sems.at[a], sib)
            cp.start()
            sends.append(cp)
        for a in range(n):
            got = o_refs[a].at[1 - c]
            _remote(got, got, send_sems.at[a], recv_sems.at[a], sib).wait_recv()
        for cp in sends:
            cp.wait_send()

    return pl.pallas_call(
        body, name=name, in_specs=[HBM_SPEC] * n, out_specs=[HBM_SPEC] * n,
        out_shape=[SDS(h.shape, h.dtype) for h in halves],
        input_output_aliases={a: a for a in range(n)},
        scratch_shapes=[pltpu.SemaphoreType.DMA((n,)), pltpu.SemaphoreType.DMA((n,))],
    )(*halves)


N_DEV = 8


def sum_devices(s, slots, me):
    R, Wd = s.shape

    def body(me_ref, s_ref, slots_ref, out_ref):
        acc = None
        for d in range(N_DEV):
            mine = me_ref[0] == d
            term = jnp.where(mine, s_ref[...], slots_ref[jnp.where(mine, d ^ 1, d)])
            acc = term if acc is None else acc + term
        out_ref[...] = acc

    vmem = pl.BlockSpec(memory_space=pltpu.VMEM)
    return pl.pallas_call(
        body, name="sum_devices", in_specs=[pl.BlockSpec(memory_space=pltpu.SMEM), vmem, vmem], out_specs=vmem,
        out_shape=SDS((R, Wd), F32),
    )(me, s, slots)


TRANSPOSED = ("w_in",)
BIG = {"ffn1_w_up": (D, 2 * FF, "col"), "ffn1_w_down": (FF, D, "row"), "w_in": (INW, D, "row"),
       "w_attn_up": (AW, D, "row"), "w_pool_up": (PW, D, "col"), "w_out": (D, D, "row"),
       "ffn2_w_up": (D, 2 * FF, "col"), "ffn2_w_down": (FF, D, "row")}
GROUPS = (("ffn1_w_up", "ffn1_w_down"), ("w_in", "w_attn_up", "w_pool_up", "w_out"), ("ffn2_w_up", "ffn2_w_down"))
SMALL = ("ffn1_norm", "mix_norm", "ffn2_norm", "final_norm", "pool_scale", "sinks", "pool_w_mix")
SMALL_W = 128


def _rowlike(name):
    return BIG[name][2] == "row"


def _half_dims(name):
    k, n, kind = BIG[name]
    return (k // N_CHIPS // 2, n) if kind == "row" else (k // 2, n // N_CHIPS)


def shard_view(name, shard):
    return shard.reshape((2,) + _half_dims(name))


def full_from_view(name, fv):
    k, n, _ = BIG[name]
    return fv.reshape(k, n)


def grad_view(name, g):
    kh, ns = _half_dims(name)
    return g.reshape(_full_shape(jax.ShapeDtypeStruct((2, kh, ns), g.dtype), _rowlike(name)))


def pack_small(d):
    parts = []
    for name in SMALL:
        a = d[name].reshape(-1)
        pad = (-a.shape[0]) % SMALL_W
        parts.append(jnp.pad(a, (0, pad)).reshape(-1, SMALL_W))
    a = jnp.concatenate(parts, axis=0)
    return jnp.pad(a, ((0, (-a.shape[0]) % 8), (0, 0)))


def unpack_small(a, like):
    out, r0 = {}, 0
    for name in SMALL:
        size = int(np.prod(like[name].shape))
        rows = -(-size // SMALL_W)
        out[name] = a[r0:r0 + rows].reshape(-1)[:size].reshape(like[name].shape)
        r0 += rows
    return out


WEIGHTS = ("ffn1_norm", "ffn1_w_up", "ffn1_w_down", "mix_norm", "w_in", "sinks", "w_attn_up", "pool_w_mix",
           "pool_scale", "w_pool_up", "w_out", "ffn2_norm", "ffn2_w_up", "ffn2_w_down", "final_norm")


def kernel(x, ffn1_norm, ffn1_w_up, ffn1_w_down, mix_norm, w_in, sinks, w_attn_up, pool_w_mix, pool_scale, w_pool_up, w_out, ffn2_norm, ffn2_w_up, ffn2_w_down, final_norm, loss_target, m_ffn1_norm, m_ffn1_w_up, m_ffn1_w_down, m_mix_norm, m_w_in, m_sinks, m_w_attn_up, m_pool_w_mix, m_pool_scale, m_w_pool_up, m_w_out, m_ffn2_norm, m_ffn2_w_up, m_ffn2_w_down, m_final_norm, v_ffn1_norm, v_ffn1_w_up, v_ffn1_w_down, v_mix_norm, v_w_in, v_sinks, v_w_attn_up, v_pool_w_mix, v_pool_scale, v_w_pool_up, v_w_out, v_ffn2_norm, v_ffn2_w_up, v_ffn2_w_down, v_final_norm):
    given = dict(locals())
    w = {n: given[n] for n in WEIGHTS}
    m = {n: given["m_" + n] for n in WEIGHTS}
    v = {n: given["v_" + n] for n in WEIGHTS}
    cx, cy, cc = _me()
    place = jnp.stack([cc, 2 * cx + cy]).astype(jnp.int32)

    def local2d(d, n):
        return d[n][0].T if n in TRANSPOSED else d[n][0]

    shards = {n: local2d(w, n) for n in BIG}
    sviews = {n: shard_view(n, shards[n].astype(BF16)) for n in BIG}
    grads, delta, new_m, new_v = {}, {}, {}, {}
    rowlikes = [[_rowlike(n) for n in names] for names in GROUPS]

    class Exchanges:
        def __init__(self):
            self.gathers, self.fulls, self.reductions, self.small = {}, {}, {}, None

        def _start_gather(self, group, after):
            sv = [sviews[n] for n in GROUPS[group]]
            rl = rowlikes[group]
            self.gathers[group] = exchange_start(sv, [_full_shape(s, r) for s, r in zip(sv, rl)], _gather_plan(rl),
                                                 after, f"gather_start_{group}")

        def weights(self, group, after):
            if not self.gathers:
                self._start_gather(0, None)
            if group not in self.fulls:
                names, rl, state = GROUPS[group], rowlikes[group], self.gathers[group]
                sv, fulls = exchange_wait(state, _gather_plan(rl), state["token"] if after is None else after,
                                          f"gather_wait_{group}")
                fulls = gather_finish(sv, fulls, rl, f"gather_finish_{group}")
                self.fulls[group] = {n: full_from_view(n, f) for n, f in zip(names, fulls)}
                if group + 1 < len(GROUPS):
                    self._start_gather(group + 1, fulls[0])
            return self.fulls[group]

        def started(self):
            return self.gathers[max(self.gathers)]["token"]

        def grads(self, g):
            names = tuple(g)
            rl = [_rowlike(n) for n in names]
            gv = [grad_view(n, g[n]) for n in names]
            state = exchange_start(gv, [_half_buffer_shape(a, r) for a, r in zip(gv, rl)], _swap_plan(rl), None,
                                   "swap_start_" + names[0], SIBLING)
            self.reductions[names] = state
            return state["token"]

        def advance(self, names, after):
            rl = [_rowlike(n) for n in names]
            gv, gots = exchange_wait(self.reductions[names], _swap_plan(rl), after, "swap_wait_" + names[0], SIBLING)
            qs = [add_halves(a, b, r, place, "add_halves_" + n) for a, b, r, n in zip(gv, gots, rl, names)]
            state = exchange_start(qs, [_slot_shape(q, r) for q, r in zip(qs, rl)], _all_to_all_plan(rl), None,
                                   "all_to_all_start_" + names[0])
            self.reductions[names] = state
            return state["token"]

        def finish(self, names, after):
            rl = [_rowlike(n) for n in names]
            qs, recvs = exchange_wait(self.reductions.pop(names), _all_to_all_plan(rl), after,
                                      "all_to_all_wait_" + names[0])
            halves = [sum_pieces(q, r, k, place, "sum_pieces_" + n) for q, r, k, n in zip(qs, recvs, rl, names)]
            for n, o in zip(names, join_halves(halves, "join_halves_" + names[0])):
                grads[n], delta[n], new_m[n], new_v[n] = adamw(shards[n], o.reshape(shards[n].shape), local2d(m, n),
                                                               local2d(v, n), "adamw_" + n)
            return [new_v[n] for n in names]

        def small_start(self, G, loss, after):
            packed = pack_small({n: G[n] for n in SMALL})
            used_rows = sum(-(-int(np.prod(small_like[n].shape)) // SMALL_W) for n in SMALL)
            assert packed.shape[0] > used_rows
            packed = packed.at[-1, 0].set(loss[0, 0])
            self.small = exchange_start([packed], [(N_DEV,) + packed.shape], _everyone_plan, after,
                                        "small_start", EVERYONE)
            return self.small["token"]

        def small_finish(self, after):
            (packed,), (slots,) = exchange_wait(self.small, _everyone_plan, after, "small_wait", EVERYONE)
            total = sum_devices(packed, slots, (4 * cx + 2 * cy + cc).astype(jnp.int32).reshape(1))
            self.loss = total[-1, 0]
            g, ds, ms, vs = adamw(pack_small(w), total, pack_small(m), pack_small(v), "adamw_small")
            for d, packed_d in ((grads, g), (delta, ds), (new_m, ms), (new_v, vs)):
                d.update(unpack_small(packed_d, small_like))
            return vs

    small_like = {n: w[n] for n in SMALL}
    S = {n: w[n].reshape(1, -1) for n in ("ffn1_norm", "mix_norm", "ffn2_norm", "final_norm", "pool_scale", "sinks")}
    S["pool_w_mix"] = w["pool_w_mix"][0].astype(BF16)
    exchanges = Exchanges()
    _, dx = local_fwd_bwd(x[0], loss_target[0], S, exchanges)
    loss = exchanges.loss

    def shaped(d, n):
        return (d[n].T if n in TRANSPOSED else d[n]).reshape(w[n].shape)

    return (loss, dx[None], *[shaped(grads, n) for n in WEIGHTS], *[shaped(delta, n) for n in WEIGHTS],
            *[shaped(new_m, n) for n in WEIGHTS], *[shaped(new_v, n) for n in WEIGHTS])
```

```python
import numpy as np
import jax
import jax.numpy as jnp
from jax import lax
from jax.experimental import pallas as pl
from jax.experimental.pallas import tpu as pltpu

F32 = jnp.float32
BF16 = jnp.bfloat16
SDS = jax.ShapeDtypeStruct
MESH = pl.DeviceIdType.MESH

D = 1024
FF = 2816
NQ = 16
NKV = 2
HD = 64
GQ = NQ // NKV
AW = NQ * HD
KVW = NKV * HD
BLK = 128
PW = 512
PG = 128
POOL_WINDOWS = (2, 4, 8, 16)
HALO = 16
INW = AW + 2 * KVW + PW + 2 * D
C_KV = AW
C_Z = AW + 2 * KVW
C_G = C_Z + PW
EPS = 1e-6
FF_CHUNK = 256
FF_CHUNKS = tuple((c, FF_CHUNK) for c in range(0, FF, FF_CHUNK))
SLOPES = tuple(float(2.0 ** (-8.0 * h / NQ)) for h in range(1, NQ + 1))
SCALE = HD ** -0.5

LR, B1, B2, ADAM_EPS, WD, STEP = 0.001, 0.9, 0.999, 1e-08, 0.01, 10

VMEM_LIMIT = 56 * 1024 * 1024
N_CHIPS = 4

NT = (((1,), (1,)), ((), ()))
TN = (((0,), (0,)), ((), ()))


def _cp(sem=None, vmem=VMEM_LIMIT):
    return pltpu.CompilerParams(dimension_semantics=sem, vmem_limit_bytes=vmem)


def _const_spec(shape):
    nd = len(shape)
    return pl.BlockSpec(shape, lambda *_: (0,) * nd, pipeline_mode=pl.Buffered(1))


def _rstd(x):
    return lax.rsqrt(jnp.mean(x * x, axis=-1, keepdims=True) + EPS)


def _rms_bwd(dn, xhat, rstd, g):
    dxhat = dn * g
    return rstd * (dxhat - xhat * jnp.mean(dxhat * xhat, axis=-1, keepdims=True))


def _dot(a, b):
    return jnp.dot(a, b, preferred_element_type=F32)


def _dot_nt(a, b):
    return lax.dot_general(a, b, NT, preferred_element_type=F32)


def _dot_tn(a, b):
    return lax.dot_general(a, b, TN, preferred_element_type=F32)


def ffn_fwd(h, g, wup, wdn, name, head=None):
    T = h.shape[0]
    TM = 512
    tile = lambda w: pl.BlockSpec((TM, w), lambda i: (i, 0))
    acc_spec = lambda w: pl.BlockSpec((1, w), lambda i: (0, 0))

    def body(h_ref, g_ref, wup_ref, wdn_ref, *rest):
        out_ref, ab_ref, hid_ref = rest[-3:] if head is None else rest[2:5]
        x = h_ref[...]
        n = (x * _rstd(x) * g_ref[...]).astype(BF16)
        for c0, w in FF_CHUNKS:
            a = _dot(n, wup_ref[:, c0:c0 + w])
            b = _dot(n, wup_ref[:, FF + c0:FF + c0 + w])
            sig = jax.nn.sigmoid(a)
            s = a * sig
            ab_ref[:, c0:c0 + w] = (b * (sig * (1.0 + a * (1.0 - sig)))).astype(BF16)
            ab_ref[:, FF + c0:FF + c0 + w] = s.astype(BF16)
            hid_ref[:, c0:c0 + w] = (s * b).astype(BF16)
        out = x + 0.5 * _dot(hid_ref[...], wdn_ref[...])
        if head is None:
            out_ref[...] = out
        else:
            t_ref, gf_ref, loss_ref, dgf_ref = rest[0], rest[1], rest[5], rest[6]
            out_ref[...] = _loss_head(out, t_ref[...], gf_ref[...], loss_ref, dgf_ref, pl.program_id(0) == 0)

    head_in, head_specs, head_out_specs, head_out_shape = [], [], [], []
    if head is not None:
        head_in, head_specs = list(head), [tile(D), _const_spec((1, D))]
        head_out_specs, head_out_shape = [acc_spec(1), acc_spec(D)], [SDS((1, 1), F32), SDS((1, D), F32)]
    return pl.pallas_call(
        body, name=name, grid=(T // TM,),
        in_specs=[tile(D), _const_spec((1, D)), _const_spec((D, 2 * FF)), _const_spec((FF, D))] + head_specs,
        out_specs=[tile(D), tile(2 * FF), tile(FF)] + head_out_specs,
        out_shape=[SDS((T, D), F32), SDS((T, 2 * FF), BF16), SDS((T, FF), BF16)] + head_out_shape,
        compiler_params=_cp(("arbitrary",)),
    )(h, g, wup, wdn, *head_in)


def _loss_head(x, target, g, loss_ref, dg_ref, first):
    @pl.when(first)
    def _():
        loss_ref[...] = jnp.zeros_like(loss_ref)
        dg_ref[...] = jnp.zeros_like(dg_ref)

    rstd = _rstd(x)
    xhat = x * rstd
    err = xhat * g - target
    loss_ref[...] += 0.5 * jnp.sum(jnp.mean(err * err, axis=-1, keepdims=True), axis=0, keepdims=True)
    dy = err * (1.0 / D)
    dg_ref[...] += jnp.sum(dy * xhat, axis=0, keepdims=True)
    return _rms_bwd(dy, xhat, rstd, g)


def ffn_bwd_x(dh, h_in, ab, g, wup, wdn, name):
    T = dh.shape[0]
    TM = 256

    def body(dh_ref, h_ref, ab_ref, g_ref, wup_ref, wdn_ref, dhin_ref, dup_ref, n_ref, dg_ref):
        x = h_ref[...]
        g = g_ref[...]
        rstd = _rstd(x)
        xhat = x * rstd
        n_ref[...] = (xhat * g).astype(BF16)
        dh = dh_ref[...]
        dhh = (0.5 * dh).astype(BF16)
        for c0, w in FF_CHUNKS:
            dhid = _dot_nt(dhh, wdn_ref[c0:c0 + w, :])
            dup_ref[:, c0:c0 + w] = (dhid * ab_ref[:, c0:c0 + w].astype(F32)).astype(BF16)
            dup_ref[:, FF + c0:FF + c0 + w] = (dhid * ab_ref[:, FF + c0:FF + c0 + w].astype(F32)).astype(BF16)
        dn = _dot_nt(dup_ref[...], wup_ref[...])
        dhin_ref[...] = dh + _rms_bwd(dn, xhat, rstd, g)

        @pl.when(pl.program_id(0) == 0)
        def _():
            dg_ref[...] = jnp.zeros_like(dg_ref)

        dg_ref[...] += jnp.sum(dn * xhat, axis=0, keepdims=True)

    tile = lambda w: pl.BlockSpec((TM, w), lambda i: (i, 0))
    return pl.pallas_call(
        body, name=name, grid=(T // TM,),
        in_specs=[tile(D), tile(D), tile(2 * FF), _const_spec((1, D)), _const_spec((D, 2 * FF)), _const_spec((FF, D))],
        out_specs=[tile(D), tile(2 * FF), tile(D), pl.BlockSpec((1, D), lambda i: (0, 0))],
        out_shape=[SDS((T, D), F32), SDS((T, 2 * FF), BF16), SDS((T, D), BF16), SDS((1, D), F32)],
        compiler_params=_cp(("arbitrary",)),
    )(dh, h_in, ab, g, wup, wdn)


TOKEN_SPEC = pl.BlockSpec((8, 128), lambda *_: (0, 0))


def _token_operand(token):
    return ([], []) if token is None else ([TOKEN_SPEC], [token])


def matmul_tn(a, b, name, *, tm, tn, tt=1024, b_scale=None, after=None):
    T, M = a.shape
    N = b.shape[1]
    tt = min(tt, T)
    assert M % tm == 0 and N % tn == 0 and T % tt == 0
    nt = T // tt
    token_spec, token_arg = _token_operand(after)

    def body(a_ref, b_ref, *rest):
        o_ref, acc_ref = rest[-2:]
        t = pl.program_id(2)

        @pl.when(t == 0)
        def _():
            acc_ref[...] = jnp.zeros_like(acc_ref)

        bv = b_ref[...]
        if b_scale is not None:
            bv = bv * b_scale
        acc_ref[...] += _dot_tn(a_ref[...].astype(BF16), bv.astype(BF16))

        @pl.when(t == nt - 1)
        def _():
            o_ref[...] = acc_ref[...].astype(BF16)

    return pl.pallas_call(
        body, name=name, grid=(M // tm, N // tn, nt),
        in_specs=[pl.BlockSpec((tt, tm), lambda i, j, t: (t, i)), pl.BlockSpec((tt, tn), lambda i, j, t: (t, j))]
        + token_spec,
        out_specs=pl.BlockSpec((tm, tn), lambda i, j, t: (i, j)),
        out_shape=SDS((M, N), BF16),
        scratch_shapes=[pltpu.VMEM((tm, tn), F32)],
        compiler_params=_cp(("parallel", "parallel", "arbitrary")),
    )(a, b, *token_arg)


def mix_in_fwd(h1, g, win_t):
    T = h1.shape[0]
    TM = 512

    def body(h_ref, g_ref, w_ref, u_ref, q_ref, kv_ref, z_ref, gt_ref):
        x = h_ref[...]
        u = (x * _rstd(x) * g_ref[...]).astype(BF16)
        u_ref[...] = u
        for c in range(0, AW, 256):
            q_ref[:, c:c + 256] = _dot_nt(u, w_ref[c:c + 256, :]).astype(BF16)
        kv_ref[...] = _dot_nt(u, w_ref[C_KV:C_Z, :]).astype(BF16)
        for c in range(0, PW, 256):
            z_ref[:, c:c + 256] = _dot_nt(u, w_ref[C_Z + c:C_Z + c + 256, :])
        for c in range(0, 2 * D, 256):
            gt_ref[:, c:c + 256] = _dot_nt(u, w_ref[C_G + c:C_G + c + 256, :]).astype(BF16)

    tile = lambda w: pl.BlockSpec((TM, w), lambda i: (i, 0))
    return pl.pallas_call(
        body, name="mix_in_fwd", grid=(T // TM,),
        in_specs=[tile(D), _const_spec((1, D)), _const_spec((INW, D))],
        out_specs=[tile(D), tile(AW), tile(2 * KVW), tile(PW), tile(2 * D)],
        out_shape=[SDS((T, D), BF16), SDS((T, AW), BF16), SDS((T, 2 * KVW), BF16), SDS((T, PW), F32),
                   SDS((T, 2 * D), BF16)],
        compiler_params=_cp(("arbitrary",)),
    )(h1, g, win_t)


def mix_in_bwd(dq, dkv, dz, dgt, dh2, h1, g, win_t):
    T = h1.shape[0]
    TM = min(1024, T)
    SUB = 256

    def body(dq_ref, dkv_ref, dz_ref, dgt_ref, dh2_ref, h_ref, g_ref, w_ref, dh1_ref, dg_ref):
        g = g_ref[...]
        dg = None
        for r0 in range(0, TM, SUB):
            rows = slice(r0, r0 + SUB)
            du = _dot(dq_ref[rows, :], w_ref[0:AW, :])
            du += _dot(dkv_ref[rows, :], w_ref[C_KV:C_Z, :])
            du += _dot(dz_ref[rows, :], w_ref[C_Z:C_G, :])
            du += _dot(dgt_ref[rows, :], w_ref[C_G:INW, :])
            x = h_ref[rows, :]
            rstd = _rstd(x)
            xhat = x * rstd
            dh1_ref[rows, :] = dh2_ref[rows, :] + _rms_bwd(du, xhat, rstd, g)
            part = jnp.sum(du * xhat, axis=0, keepdims=True)
            dg = part if dg is None else dg + part

        @pl.when(pl.program_id(0) == 0)
        def _():
            dg_ref[...] = jnp.zeros_like(dg_ref)

        dg_ref[...] += dg

    tile = lambda w: pl.BlockSpec((TM, w), lambda i: (i, 0))
    return pl.pallas_call(
        body, name="mix_in_bwd", grid=(T // TM,),
        in_specs=[tile(AW), tile(2 * KVW), tile(PW), tile(2 * D), tile(D), tile(D), _const_spec((1, D)),
                  _const_spec((INW, D))],
        out_specs=[tile(D), pl.BlockSpec((1, D), lambda i: (0, 0))],
        out_shape=[SDS((T, D), F32), SDS((1, D), F32)],
        compiler_params=_cp(("arbitrary",)),
    )(dq, dkv, dz, dgt, dh2, h1, g, win_t)


PAIR = 2 * HD
NPAIR = GQ // 2


def _lo_lanes():
    return lax.broadcasted_iota(jnp.int32, (BLK, PAIR), 1) < HD


def _stack_heads(ref, kvh, scale=None):
    lo = _lo_lanes()
    parts = []
    for pr in range(NPAIR):
        t = ref[:, (kvh * NPAIR + pr) * PAIR:(kvh * NPAIR + pr + 1) * PAIR]
        if scale is not None:
            t = t * scale
        zero = jnp.zeros_like(t)
        parts += [jnp.where(lo, t, zero), jnp.where(lo, zero, t)]
    return jnp.concatenate(parts, axis=0)


def _kv_tiles(kvc_ref, kvp_ref, tile, kvh):
    lo = _lo_lanes()
    dup, left, right = [], [], []
    for ref in (kvp_ref, kvc_ref):
        t = ref[:, tile * PAIR:(tile + 1) * PAIR]
        r = pltpu.roll(t.astype(F32), HD, 1).astype(BF16)
        zero = jnp.zeros_like(t)
        a, b = (t, r) if kvh == 0 else (r, t)
        dup.append(jnp.where(lo, a, b))
        left.append(jnp.where(lo, a, zero))
        right.append(jnp.where(lo, zero, b))
    cat = lambda xs: jnp.concatenate(xs, axis=0)
    return cat(dup), cat(left), cat(right)


def _band_consts(first):
    row = lax.broadcasted_iota(jnp.int32, (BLK, BLK), 0)
    col = lax.broadcasted_iota(jnp.int32, (BLK, BLK), 1)
    upper = col > row
    dist = jnp.where(upper, row - col + BLK, row - col).astype(F32)
    pen = jnp.where(jnp.logical_and(upper, first), -jnp.inf, 0.0)
    return upper, dist, pen


def _split_band(upper, t):
    zero = jnp.zeros_like(t)
    return jnp.concatenate([jnp.where(upper, t, zero), jnp.where(upper, zero, t)], axis=1)


def attn_fwd(q, kv, sinks):
    T = q.shape[0]
    nb = T // BLK

    def body(sink_ref, q_ref, kvc_ref, kvp_ref, att_ref, lse_ref):
        upper, dist, pen = _band_consts(pl.program_id(0) == 0)
        scores, values = [], []
        for kvh in range(NKV):
            kdup, _, _ = _kv_tiles(kvc_ref, kvp_ref, 0, kvh)
            values.append(_kv_tiles(kvc_ref, kvp_ref, 1, kvh)[1:])
            scores.append(_dot_nt(_stack_heads(q_ref, kvh, SCALE), kdup))
        for kvh in range(NKV):
            s_all = scores[kvh]
            vleft, vright = values[kvh]
            for pr in range(NPAIR):
                outs, inv = [], []
                for side, vpad in ((0, vleft), (1, vright)):
                    g = 2 * pr + side
                    hq = kvh * GQ + g
                    sink = sink_ref[0, hq]
                    rows = slice(g * BLK, (g + 1) * BLK)
                    s = jnp.where(upper, s_all[rows, 0:BLK], s_all[rows, BLK:2 * BLK]) - SLOPES[hq] * dist + pen
                    m = jnp.maximum(jnp.max(s, axis=-1, keepdims=True), sink)
                    p = jnp.exp(s - m)
                    l = jnp.sum(p, axis=-1, keepdims=True) + jnp.exp(sink - m)
                    lse_ref[:, hq:hq + 1] = m + jnp.log(l)
                    outs.append(_dot(_split_band(upper, p.astype(BF16)), vpad))
                    inv.append(1.0 / l)
                col0 = (kvh * NPAIR + pr) * PAIR
                att_ref[:, col0:col0 + PAIR] = ((outs[0] + outs[1]) * jnp.where(_lo_lanes(), inv[0], inv[1])).astype(BF16)

    return pl.pallas_call(
        body, name="attn_fwd", grid=(nb,),
        in_specs=[pl.BlockSpec(memory_space=pltpu.SMEM),
                  pl.BlockSpec((BLK, AW), lambda i: (i, 0)),
                  pl.BlockSpec((BLK, 2 * KVW), lambda i: (i, 0)),
                  pl.BlockSpec((BLK, 2 * KVW), lambda i: (jnp.maximum(i - 1, 0), 0))],
        out_specs=[pl.BlockSpec((BLK, AW), lambda i: (i, 0)), pl.BlockSpec((BLK, NQ), lambda i: (i, 0))],
        out_shape=[SDS((T, AW), BF16), SDS((T, NQ), F32)],
        compiler_params=_cp(("arbitrary",)),
    )(sinks, q, kv, kv)


def attn_bwd(q, kv, datt, att, lse, sinks):
    T = q.shape[0]
    nb = T // BLK

    def body(sink_ref, q_ref, kvc_ref, kvp_ref, do_ref, out_ref, lse_ref, dq_ref, dkv_ref, dsink_ref, carry_ref):
        i = pl.program_id(0)

        @pl.when(i == 0)
        def _():
            dsink_ref[...] = jnp.zeros_like(dsink_ref)
            carry_ref[...] = jnp.zeros_like(carry_ref)

        @pl.when(i < nb)
        def _():
            upper, dist, pen = _band_consts(i == 0)
            lo = _lo_lanes()
            dk_dup, dv_dup = [], []
            staged = []
            for kvh in range(NKV):
                kdup, kleft, kright = _kv_tiles(kvc_ref, kvp_ref, 0, kvh)
                vdup, _, _ = _kv_tiles(kvc_ref, kvp_ref, 1, kvh)
                qs = _stack_heads(q_ref, kvh, SCALE)
                dos = _stack_heads(do_ref, kvh)
                staged.append((kleft, kright, qs, dos, _dot_nt(qs, kdup), _dot_nt(dos, vdup)))
            deltas = []
            for pair in range(NQ // 2):
                cols = slice(pair * PAIR, (pair + 1) * PAIR)
                t = do_ref[:, cols].astype(F32) * out_ref[:, cols].astype(F32)
                deltas += [jnp.sum(jnp.where(lo, t, 0.0), axis=-1, keepdims=True),
                           jnp.sum(jnp.where(lo, 0.0, t), axis=-1, keepdims=True)]
            for kvh in range(NKV):
                kleft, kright, qs, dos, s_all, dp_all = staged[kvh]
                ds_parts, p_parts = [], []
                for pr in range(NPAIR):
                    dq = None
                    for side, kpad in ((0, kleft), (1, kright)):
                        g = 2 * pr + side
                        hq = kvh * GQ + g
                        lse_h = lse_ref[:, hq:hq + 1]
                        rows = slice(g * BLK, (g + 1) * BLK)
                        s = jnp.where(upper, s_all[rows, 0:BLK], s_all[rows, BLK:2 * BLK]) - SLOPES[hq] * dist + pen
                        p = jnp.exp(s - lse_h)
                        dp = jnp.where(upper, dp_all[rows, 0:BLK], dp_all[rows, BLK:2 * BLK])
                        delta = deltas[hq]
                        dsink_ref[:, hq:hq + 1] += -jnp.sum(jnp.exp(sink_ref[0, hq] - lse_h) * delta, axis=0,
                                                            keepdims=True)
                        ds = _split_band(upper, (p * (dp - delta)).astype(BF16))
                        ds_parts.append(ds)
                        p_parts.append(_split_band(upper, p.astype(BF16)))
                        d = _dot(ds, kpad)
                        dq = d if dq is None else dq + d
                    col0 = (kvh * NPAIR + pr) * PAIR
                    dq_ref[:, col0:col0 + PAIR] = (dq * SCALE).astype(BF16)
                dkw = _dot_tn(qs, jnp.concatenate(ds_parts, axis=0)).T
                dvw = _dot_tn(dos, jnp.concatenate(p_parts, axis=0)).T
                dk_dup.append(dkw + pltpu.roll(dkw, HD, 1))
                dv_dup.append(dvw + pltpu.roll(dvw, HD, 1))
            dk = jnp.where(jnp.concatenate([lo, lo], axis=0), dk_dup[0], dk_dup[1])
            dv = jnp.where(jnp.concatenate([lo, lo], axis=0), dv_dup[0], dv_dup[1])
            dkv_ref[:, 0:PAIR] = (carry_ref[:, 0:PAIR] + dk[0:BLK]).astype(BF16)
            dkv_ref[:, PAIR:2 * PAIR] = (carry_ref[:, PAIR:2 * PAIR] + dv[0:BLK]).astype(BF16)
            carry_ref[:, 0:PAIR] = dk[BLK:2 * BLK]
            carry_ref[:, PAIR:2 * PAIR] = dv[BLK:2 * BLK]

        @pl.when(i == nb)
        def _():
            dkv_ref[...] = carry_ref[...].astype(BF16)

    cur = lambda i: (jnp.minimum(i, nb - 1), 0)
    prev = lambda i: (jnp.maximum(jnp.minimum(i, nb - 1) - 1, 0), 0)
    return pl.pallas_call(
        body, name="attn_bwd", grid=(nb + 1,),
        in_specs=[pl.BlockSpec(memory_space=pltpu.SMEM),
                  pl.BlockSpec((BLK, AW), cur), pl.BlockSpec((BLK, 2 * KVW), cur), pl.BlockSpec((BLK, 2 * KVW), prev),
                  pl.BlockSpec((BLK, AW), cur), pl.BlockSpec((BLK, AW), cur), pl.BlockSpec((BLK, NQ), cur)],
        out_specs=[pl.BlockSpec((BLK, AW), cur),
                   pl.BlockSpec((BLK, 2 * KVW), lambda i: (jnp.maximum(i - 1, 0), 0)),
                   pl.BlockSpec((1, NQ), lambda i: (0, 0))],
        out_shape=[SDS((T, AW), BF16), SDS((T, 2 * KVW), BF16), SDS((1, NQ), F32)],
        scratch_shapes=[pltpu.VMEM((BLK, 2 * KVW), F32)],
        compiler_params=_cp(("arbitrary",)),
    )(sinks, q, kv, kv, datt, att, lse)


def _inv_counts(t0, rows):
    t = (t0 + lax.broadcasted_iota(jnp.int32, (rows, 1), 0) + 1).astype(F32)
    return [1.0 / jnp.minimum(t, float(w)) for w in POOL_WINDOWS]


def pool_fwd(z, wmix, scale):
    T = z.shape[0]
    TM = 512
    L = TM + HALO

    def body(z_ref, halo_ref, wmix_ref, scale_ref, pooled_ref, mixs_ref):
        i = pl.program_id(0)
        halo = jnp.where(i > 0, halo_ref[...], 0.0)
        zt = z_ref[...]
        e = jnp.concatenate([halo, zt], axis=0)
        sums = []
        s = e
        for k in (1, 2, 4, 8):
            s = s + pltpu.roll(s, k, 0)
            sums.append(s)
        inv = _inv_counts(i * TM, TM)
        for gi in range(len(POOL_WINDOWS)):
            cols = slice(gi * PG, (gi + 1) * PG)
            pooled = (sums[gi][HALO:, cols] * inv[gi] - zt[:, cols]).astype(BF16)
            pooled_ref[:, cols] = pooled
            mixs_ref[:, cols] = (_dot(pooled, wmix_ref[gi]) * scale_ref[:, cols]).astype(BF16)

    return pl.pallas_call(
        body, name="pool_fwd", grid=(T // TM,),
        in_specs=[pl.BlockSpec((TM, PW), lambda i: (i, 0)),
                  pl.BlockSpec((HALO, PW), lambda i: (jnp.maximum(i * (TM // HALO) - 1, 0), 0)),
                  _const_spec((len(POOL_WINDOWS), PG, PG)), _const_spec((1, PW))],
        out_specs=[pl.BlockSpec((TM, PW), lambda i: (i, 0)), pl.BlockSpec((TM, PW), lambda i: (i, 0))],
        out_shape=[SDS((T, PW), BF16), SDS((T, PW), BF16)],
        compiler_params=_cp(("arbitrary",)),
    )(z, z, wmix, scale)


def pool_bwd(dmixs, pooled, wmix, scale):
    T = dmixs.shape[0]
    TM = 512
    L = TM + HALO
    nt = T // TM

    def body(dm_ref, halo_ref, pooled_ref, wmix_ref, scale_ref, dz_ref, dwmix_ref, dscale_ref):
        i = pl.program_id(0)

        @pl.when(i == 0)
        def _():
            dwmix_ref[...] = jnp.zeros_like(dwmix_ref)
            dscale_ref[...] = jnp.zeros_like(dscale_ref)

        halo = jnp.where(i < nt - 1, halo_ref[...], 0.0)
        dm = dm_ref[...]
        e = jnp.concatenate([dm, halo], axis=0)
        inv = _inv_counts(i * TM, L)
        for gi in range(len(POOL_WINDOWS)):
            cols = slice(gi * PG, (gi + 1) * PG)
            w = wmix_ref[gi]
            dmixed = (e[:, cols] * scale_ref[:, cols]).astype(BF16)
            dpooled = _dot_nt(dmixed, w)
            pooled = pooled_ref[:, cols]
            mixed = _dot(pooled, w)
            dscale_ref[:, cols] += jnp.sum(dm[:, cols] * mixed, axis=0, keepdims=True)
            dwmix_ref[gi] += _dot_tn(pooled, dmixed[:TM])
            s = dpooled * inv[gi]
            k = 1
            while k < POOL_WINDOWS[gi]:
                s = s + pltpu.roll(s, L - k, 0)
                k *= 2
            dz_ref[:, cols] = (s[:TM] - dpooled[:TM]).astype(BF16)

    return pl.pallas_call(
        body, name="pool_bwd", grid=(nt,),
        in_specs=[pl.BlockSpec((TM, PW), lambda i: (i, 0)),
                  pl.BlockSpec((HALO, PW), lambda i: (jnp.minimum((i + 1) * (TM // HALO), T // HALO - 1), 0)),
                  pl.BlockSpec((TM, PW), lambda i: (i, 0)),
                  _const_spec((len(POOL_WINDOWS), PG, PG)), _const_spec((1, PW))],
        out_specs=[pl.BlockSpec((TM, PW), lambda i: (i, 0)),
                   pl.BlockSpec((len(POOL_WINDOWS), PG, PG), lambda i: (0, 0, 0)),
                   pl.BlockSpec((1, PW), lambda i: (0, 0))],
        out_shape=[SDS((T, PW), BF16), SDS((len(POOL_WINDOWS), PG, PG), F32), SDS((1, PW), F32)],
        compiler_params=_cp(("arbitrary",)),
    )(dmixs, dmixs, pooled, wmix, scale)


def merge_fwd(att, mixs, gt, h1, wattn, wpool, wout):
    T = h1.shape[0]
    TM = 512

    def body(att_ref, mixs_ref, gt_ref, h_ref, wa_ref, wp_ref, wo_ref, h2_ref, mg_ref, gf_ref):
        a = _dot(att_ref[...], wa_ref[...])
        p = _dot(mixs_ref[...], wp_ref[...])
        sa = jax.nn.sigmoid(gt_ref[:, 0:D].astype(F32))
        sp = jax.nn.sigmoid(gt_ref[:, D:2 * D].astype(F32))
        gf_ref[:, 0:D] = (a * sa * (1.0 - sa)).astype(BF16)
        gf_ref[:, D:2 * D] = (p * sp * (1.0 - sp)).astype(BF16)
        mg = (sa * a + sp * p).astype(BF16)
        mg_ref[...] = mg
        h2_ref[...] = h_ref[...] + _dot(mg, wo_ref[...])

    tile = lambda w: pl.BlockSpec((TM, w), lambda i: (i, 0))
    return pl.pallas_call(
        body, name="merge_fwd", grid=(T // TM,),
        in_specs=[tile(AW), tile(PW), tile(2 * D), tile(D), _const_spec((AW, D)), _const_spec((PW, D)),
                  _const_spec((D, D))],
        out_specs=[tile(D), tile(D), tile(2 * D)],
        out_shape=[SDS((T, D), F32), SDS((T, D), BF16), SDS((T, 2 * D), BF16)],
        compiler_params=_cp(("arbitrary",)),
    )(att, mixs, gt, h1, wattn, wpool, wout)


def merge_bwd(dh2, gt, gf, wattn, wpool, wout, after=None):
    T = dh2.shape[0]
    TM = 512
    token_spec, token_arg = _token_operand(after)

    def body(dh2_ref, gt_ref, gf_ref, wa_ref, wp_ref, wo_ref, *rest):
        datt_ref, dmixs_ref, dgt_ref, da_ref, dp_ref = rest[-5:]
        dm = _dot_nt(dh2_ref[...].astype(BF16), wo_ref[...])
        da = (dm * jax.nn.sigmoid(gt_ref[:, 0:D].astype(F32))).astype(BF16)
        dp = (dm * jax.nn.sigmoid(gt_ref[:, D:2 * D].astype(F32))).astype(BF16)
        da_ref[...] = da
        dp_ref[...] = dp
        dgt_ref[:, 0:D] = (dm * gf_ref[:, 0:D].astype(F32)).astype(BF16)
        dgt_ref[:, D:2 * D] = (dm * gf_ref[:, D:2 * D].astype(F32)).astype(BF16)
        datt_ref[...] = _dot_nt(da, wa_ref[...]).astype(BF16)
        dmixs_ref[...] = _dot_nt(dp, wp_ref[...])

    tile = lambda w: pl.BlockSpec((TM, w), lambda i: (i, 0))
    return pl.pallas_call(
        body, name="merge_bwd", grid=(T // TM,),
        in_specs=[tile(D), tile(2 * D), tile(2 * D), _const_spec((AW, D)), _const_spec((PW, D)),
                  _const_spec((D, D))] + token_spec,
        out_specs=[tile(AW), tile(PW), tile(2 * D), tile(D), tile(D)],
        out_shape=[SDS((T, AW), BF16), SDS((T, PW), F32), SDS((T, 2 * D), BF16), SDS((T, D), BF16),
                   SDS((T, D), BF16)],
        compiler_params=_cp(("arbitrary",)),
    )(dh2, gt, gf, wattn, wpool, wout, *token_arg)


def adamw(w, g, m, v, name):
    R, C = w.shape
    tile_bytes = 2 * 1024 * 1024
    tr = R
    if R * C * 4 > tile_bytes:
        tr = next(cand for cand in (512, 256, 128, 64, 32, 16, 8) if R % cand == 0 and cand * C * 4 <= tile_bytes)
    c1 = 1.0 - B1 ** STEP
    c2 = 1.0 - B2 ** STEP

    def body(w_ref, g_ref, m_ref, v_ref, go_ref, d_ref, nm_ref, nv_ref):
        gv = g_ref[...]
        go_ref[...] = gv
        nm = B1 * m_ref[...] + (1.0 - B1) * gv
        nv = B2 * v_ref[...] + (1.0 - B2) * (gv * gv)
        nm_ref[...] = nm
        nv_ref[...] = nv
        d_ref[...] = -LR * ((nm / c1) / (jnp.sqrt(nv / c2) + ADAM_EPS) + WD * w_ref[...])

    spec = pl.BlockSpec((tr, C), lambda i: (i, 0))
    return pl.pallas_call(
        body, name=name, grid=(R // tr,),
        in_specs=[spec] * 4, out_specs=[spec] * 4, out_shape=[SDS((R, C), F32)] * 4,
        compiler_params=_cp(("parallel",)),
    )(w, g, m, v)


GROUP_FFN1, GROUP_MIX, GROUP_FFN2 = 0, 1, 2


def _behind(small, token):
    return small if token is None else small + token[0:1, 0:1]


def local_fwd_bwd(x, target, S, comm):
    W = comm.weights(GROUP_FFN1, None)
    h1, ab1, hid1 = ffn_fwd(x, _behind(S["ffn1_norm"], comm.started()), W["ffn1_w_up"], W["ffn1_w_down"], "ffn1_fwd")
    W = comm.weights(GROUP_MIX, h1)
    u, q, kv, z, gt = mix_in_fwd(h1, _behind(S["mix_norm"], comm.started()), W["w_in"])
    att, lse = attn_fwd(q, kv, S["sinks"])
    pooled, mixs = pool_fwd(z, S["pool_w_mix"], _behind(S["pool_scale"], comm.prefetch(GROUP_FFN2, att)))
    h2, merged, gf = merge_fwd(att, mixs, gt, h1, W["w_attn_up"], W["w_pool_up"], W["w_out"])
    W = comm.weights(GROUP_FFN2, h2)
    dh3, ab2, hid2, loss, g_final = ffn_fwd(h2, S["ffn2_norm"], W["ffn2_w_up"], W["ffn2_w_down"], "ffn2_fwd",
                                            head=(target, S["final_norm"]))

    G = {"final_norm": g_final}
    dh2, dup2, n2, G["ffn2_norm"] = ffn_bwd_x(dh3, h2, ab2, S["ffn2_norm"], W["ffn2_w_up"], W["ffn2_w_down"],
                                              "ffn2_bwd_x")
    ffn2 = ("ffn2_w_down", "ffn2_w_up")
    token = comm.grads({"ffn2_w_down": matmul_tn(hid2, dh3, "ffn2_dw_down", tm=1408, tn=512, tt=4096, b_scale=0.5),
                        "ffn2_w_up": matmul_tn(n2, dup2, "ffn2_dw_up", tm=D, tn=512, tt=4096)})

    W = comm.weights(GROUP_MIX, None)
    datt, dmixs, dgt, da, dp = merge_bwd(dh2, gt, gf, W["w_attn_up"], W["w_pool_up"], W["w_out"], after=token)
    token = comm.advance(ffn2, datt)
    g_mix = {"w_out": matmul_tn(merged, dh2, "dw_out", tm=D, tn=D, after=token),
             "w_attn_up": matmul_tn(att, da, "dw_attn_up", tm=AW, tn=D),
             "w_pool_up": matmul_tn(mixs, dp, "dw_pool_up", tm=PW, tn=D)}
    dz, G["pool_w_mix"], G["pool_scale"] = pool_bwd(dmixs, pooled, S["pool_w_mix"], _behind(S["pool_scale"], token))
    dq, dkv, G["sinks"] = attn_bwd(q, kv, datt, att, lse, _behind(S["sinks"], token))
    dh1, G["mix_norm"] = mix_in_bwd(dq, dkv, dz, dgt, dh2, h1, S["mix_norm"], W["w_in"])
    g_mix["w_in"] = jnp.concatenate([
        matmul_tn(dq, u, "dw_in_q", tm=AW, tn=D),
        matmul_tn(dkv, u, "dw_in_kv", tm=2 * KVW, tn=D),
        matmul_tn(dz, u, "dw_in_z", tm=PW, tn=D),
        matmul_tn(dgt, u, "dw_in_g", tm=D, tn=512, tt=4096),
    ], axis=0)
    mix = tuple(g_mix)
    token = comm.grads(g_mix)

    W = comm.weights(GROUP_FFN1, None)
    g_dn1 = matmul_tn(hid1, dh1, "ffn1_dw_down", tm=1408, tn=512, tt=4096, b_scale=0.5, after=token)
    token = comm.advance(mix, g_dn1)
    token2 = comm.grads({"ffn1_w_down": g_dn1})
    g1 = _behind(_behind(S["ffn1_norm"], token), token2)
    dx, dup1, n1, G["ffn1_norm"] = ffn_bwd_x(dh1, x, ab1, g1, W["ffn1_w_up"], W["ffn1_w_down"], "ffn1_bwd_x")
    token = comm.advance(("ffn1_w_down",), dx)
    token2 = comm.small_start(G, loss, dx)
    g_up1 = matmul_tn(n1, dup1, "ffn1_dw_up", tm=D, tn=512, tt=4096,
                      after=None if token is None else token + token2)
    token = comm.grads({"ffn1_w_up": g_up1})
    done = comm.finish(mix, token)
    token = comm.advance(("ffn1_w_up",), done)
    done = comm.finish(("ffn1_w_down",), token)
    done = comm.small_finish(done)
    done = comm.finish(ffn2, done)
    comm.finish(("ffn1_w_up",), done)
    return loss, dx


HBM_SPEC = pl.BlockSpec(memory_space=pltpu.HBM)


def _me():
    return lax.axis_index("x"), lax.axis_index("y"), lax.axis_index("c")


def _peer_chip(x, y, k):
    return x ^ (k >> 1), y ^ (k & 1)


def _piece_half(ref, rowlike, j, h):
    if rowlike:
        return ref.at[j, h]
    ns = ref.shape[-1] // N_CHIPS
    return ref.at[h, :, pl.ds(pl.multiple_of(j * ns, 128), ns)]


def _piece(ref, rowlike, j):
    if rowlike:
        return ref.at[j]
    ns = ref.shape[-1] // N_CHIPS
    return ref.at[:, :, pl.ds(pl.multiple_of(j * ns, 128), ns)]


def _full_shape(shard_view, rowlike):
    _, kh, ns = shard_view.shape
    return (N_CHIPS, 2, kh, ns) if rowlike else (2, kh, N_CHIPS * ns)


def _remote(src, dst, send_sem, recv_sem, dev):
    return pltpu.make_async_remote_copy(src, dst, send_sem, recv_sem, device_id=dev, device_id_type=MESH)


SEM_SPEC = pl.BlockSpec(memory_space=pltpu.SEMAPHORE)
SPLIT_PARAMS = pltpu.CompilerParams(has_side_effects=pltpu.SideEffectType.DATAFLOW_SIDE_EFFECTING)


def _gather_plan(rowlikes):
    def plan(s_refs, f_refs, a, k, x, y, c):
        px, py = _peer_chip(x, y, k)
        return (s_refs[a].at[c], _piece_half(f_refs[a], rowlikes[a], 2 * x + y, c),
                _piece_half(f_refs[a], rowlikes[a], 2 * px + py, c), (px, py, c))
    return plan


def _all_to_all_plan(rowlikes):
    def plan(q_refs, r_refs, a, k, x, y, c):
        px, py = _peer_chip(x, y, k)
        if rowlikes[a]:
            src = q_refs[a].at[2 * px + py]
        else:
            ns = q_refs[a].shape[-1] // N_CHIPS
            src = q_refs[a].at[:, pl.ds(pl.multiple_of((2 * px + py) * ns, 128), ns)]
        return src, r_refs[a].at[k - 1], r_refs[a].at[k - 1], (px, py, c)
    return plan


def _swap_plan(rowlikes):
    def plan(g_refs, got_refs, a, k, x, y, c):
        src = g_refs[a].at[:, 1 - c] if rowlikes[a] else g_refs[a].at[1 - c]
        return src, got_refs[a], got_refs[a], (x, y, 1 - c)
    return plan


def _everyone_plan(s_refs, slot_refs, a, k, x, y, c):
    px, py, pc = x ^ (k >> 2), y ^ ((k >> 1) & 1), c ^ (k & 1)
    return s_refs[a], slot_refs[a].at[4 * x + 2 * y + c], slot_refs[a].at[4 * px + 2 * py + pc], (px, py, pc)


def _forward_plan(rowlikes):
    def plan(s_refs, f_refs, a, k, x, y, c):
        sib = (x, y, 1 - c)
        if k == 0:
            own = _piece(f_refs[a], rowlikes[a], 2 * x + y)
            return s_refs[a], own, own, sib
        px, py = _peer_chip(x, y, k)
        mine = _piece_half(f_refs[a], rowlikes[a], 2 * px + py, c)
        return mine, mine, _piece_half(f_refs[a], rowlikes[a], 2 * px + py, 1 - c), sib
    return plan


CHIPS, SIBLING, EVERYONE, FORWARDS = (1, 2, 3), (1,), tuple(range(1, 8)), (0, 1, 2, 3)


def _as_list(after):
    return [] if after is None else list(after) if isinstance(after, (list, tuple)) else [after]


def exchange_start(srcs, land_shapes, plan, after, name, peers=CHIPS):
    n = len(srcs)
    lands = [l if hasattr(l, "dtype") else lax.empty(l, s.dtype) for l, s in zip(land_shapes, srcs)]

    behind = _as_list(after)

    def body(*refs):
        s_refs, l_refs = refs[:n], refs[n:2 * n]
        send_sems, recv_sems = refs[2 * n + len(behind)], refs[2 * n + len(behind) + 1]
        token = refs[-1]
        x, y, c = _me()
        for a in range(n):
            for i, k in enumerate(peers):
                src, dst, _, peer = plan(s_refs, l_refs, a, k, x, y, c)
                sem = len(peers) * a + i
                _remote(src, dst, send_sems.at[sem], recv_sems.at[sem], peer).start()
        token[...] = jnp.zeros_like(token)

    n_sems = len(peers) * n
    outs = pl.pallas_call(
        body, name=name, in_specs=[HBM_SPEC] * (2 * n) + [pl.BlockSpec(memory_space=pl.ANY)] * len(behind),
        out_specs=[SEM_SPEC, SEM_SPEC] + [HBM_SPEC] * (2 * n) + [pl.BlockSpec(memory_space=pltpu.VMEM)],
        out_shape=[pltpu.SemaphoreType.DMA((n_sems,)), pltpu.SemaphoreType.DMA((n_sems,))]
        + [pltpu.HBM(a.shape, a.dtype) for a in (*srcs, *lands)] + [SDS((8, 128), F32)],
        input_output_aliases={i: 2 + i for i in range(2 * n)},
        compiler_params=SPLIT_PARAMS,
    )(*[pltpu.with_memory_space_constraint(a, pltpu.HBM) for a in (*srcs, *lands)], *behind)
    return {"sems": outs[:2], "srcs": outs[2:2 + n], "lands": outs[2 + n:2 + 2 * n], "token": outs[-1]}


def exchange_wait(state, plan, after, name, peers=CHIPS):
    n = len(state["srcs"])
    behind = _as_list(after)

    def body(*refs):
        s_refs, l_refs = refs[:n], refs[n:2 * n]
        send_sems, recv_sems = refs[2 * n], refs[2 * n + 1]
        x, y, c = _me()
        for a in range(n):
            for i, k in enumerate(peers):
                src, _, landing, peer = plan(s_refs, l_refs, a, k, x, y, c)
                sem = len(peers) * a + i
                cp = _remote(src, landing, send_sems.at[sem], recv_sems.at[sem], peer)
                cp.wait_send()
                cp.wait_recv()

    bufs = (*state["srcs"], *state["lands"])
    outs = pl.pallas_call(
        body, name=name,
        in_specs=[HBM_SPEC] * (2 * n) + [SEM_SPEC, SEM_SPEC] + [pl.BlockSpec(memory_space=pl.ANY)] * len(behind),
        out_specs=[HBM_SPEC] * (2 * n),
        out_shape=[pltpu.HBM(a.shape, a.dtype) for a in bufs],
        input_output_aliases={i: i for i in range(2 * n)},
        compiler_params=SPLIT_PARAMS,
    )(*bufs, *state["sems"], *behind)
    return outs[:n], outs[n:]


def gather_finish(shards, fulls, rowlikes, name):
    n = len(shards)

    def body(*refs):
        s_refs, f_refs = refs[:n], refs[2 * n:3 * n]
        send_sems, recv_sems = refs[3 * n:]
        x, y, c = _me()
        chip = 2 * x + y
        sib = (x, y, 1 - c)
        sends = []
        for a in range(n):
            own = _piece(f_refs[a], rowlikes[a], chip)
            cp = _remote(s_refs[a], own, send_sems.at[a, 0], recv_sems.at[a, 0], sib)
            cp.start()
            sends.append(cp)
            for k in (1, 2, 3):
                px, py = _peer_chip(x, y, k)
                slot = _piece_half(f_refs[a], rowlikes[a], 2 * px + py, c)
                cp = _remote(slot, slot, send_sems.at[a, k], recv_sems.at[a, k], sib)
                cp.start()
                sends.append(cp)
        for a in range(n):
            own = _piece(f_refs[a], rowlikes[a], chip)
            _remote(own, own, send_sems.at[a, 0], recv_sems.at[a, 0], sib).wait_recv()
            for k in (1, 2, 3):
                px, py = _peer_chip(x, y, k)
                slot = _piece_half(f_refs[a], rowlikes[a], 2 * px + py, 1 - c)
                _remote(slot, slot, send_sems.at[a, k], recv_sems.at[a, k], sib).wait_recv()
        for cp in sends:
            cp.wait_send()

    return pl.pallas_call(
        body, name=name, in_specs=[HBM_SPEC] * (2 * n), out_specs=[HBM_SPEC] * n,
        out_shape=[SDS(f.shape, f.dtype) for f in fulls],
        input_output_aliases={n + a: a for a in range(n)},
        scratch_shapes=[pltpu.SemaphoreType.DMA((n, 4)), pltpu.SemaphoreType.DMA((n, 4))],
    )(*shards, *fulls)


def _half_buffer_shape(gview, rowlike):
    return (N_CHIPS,) + gview.shape[2:] if rowlike else gview.shape[1:]


ROW_TILES = 2


def _piece_specs(rowlike, kh, ns, piece, half):
    tr = kh // ROW_TILES
    if rowlike:
        return (pl.BlockSpec((None, None, tr, ns), lambda *g: (piece(*g), half(*g), g[-2], 0)),
                pl.BlockSpec((None, tr, ns), lambda *g: (piece(*g), g[-2], 0)))
    return (pl.BlockSpec((None, tr, ns), lambda *g: (half(*g), g[-2], piece(*g))),
            pl.BlockSpec((tr, ns), lambda *g: (g[-2], piece(*g))))


def add_halves(gview, got, rowlike, place, name):
    kh, ns = (gview.shape[2], gview.shape[3]) if rowlike else (gview.shape[1], gview.shape[2] // N_CHIPS)

    def body(place_ref, g_ref, got_ref, o_ref):
        o_ref[...] = (g_ref[...].astype(F32) + got_ref[...].astype(F32)).astype(BF16)

    g_spec, h_spec = _piece_specs(rowlike, kh, ns, lambda j, r, p: j, lambda j, r, p: p[0])
    return pl.pallas_call(
        body, name=name,
        grid_spec=pltpu.PrefetchScalarGridSpec(num_scalar_prefetch=1, grid=(N_CHIPS, ROW_TILES),
                                               in_specs=[g_spec, h_spec], out_specs=h_spec),
        out_shape=SDS(got.shape, BF16),
        compiler_params=_cp(("parallel", "parallel")),
    )(place, gview, got)


def _slot_shape(q, rowlike):
    return (3,) + (q.shape[1:] if rowlike else (q.shape[0], q.shape[1] // N_CHIPS))


def sum_pieces(q, recv, rowlike, place, name):
    kh, ns = recv.shape[1], recv.shape[2]
    tr = kh // ROW_TILES

    def body(place_ref, q_ref, r_ref, o_ref):
        acc = q_ref[...].astype(F32)
        for k in range(3):
            acc = acc + r_ref[k].astype(F32)
        o_ref[...] = acc

    _, h_spec = _piece_specs(rowlike, kh, ns, lambda z, r, p: p[1], lambda z, r, p: p[0])
    return pl.pallas_call(
        body, name=name,
        grid_spec=pltpu.PrefetchScalarGridSpec(
            num_scalar_prefetch=1, grid=(1, ROW_TILES),
            in_specs=[h_spec, pl.BlockSpec((3, tr, ns), lambda z, r, p: (0, r, 0))],
            out_specs=pl.BlockSpec((None, tr, ns), lambda z, r, p: (p[0], r, 0))),
        out_shape=SDS((2, kh, ns), F32),
        compiler_params=_cp(("parallel", "parallel")),
    )(place, q, recv)


def join_halves(halves, name):
    n = len(halves)

    def body(*refs):
        o_refs = refs[n:2 * n]
        send_sems, recv_sems = refs[2 * n:]
        x, y, c = _me()
        sib = (x, y, 1 - c)
        sends = []
        for a in range(n):
            cp = _remote(o_refs[a].at[c], o_refs[a].at[c], send_sems.at[a], recv_sems.at[a], sib)
            cp.start()
            sends.append(cp)
        for a in range(n):
            got = o_refs[a].at[1 - c]
            _remote(got, got, send_sems.at[a], recv_sems.at[a], sib).wait_recv()
        for cp in sends:
            cp.wait_send()

    return pl.pallas_call(
        body, name=name, in_specs=[HBM_SPEC] * n, out_specs=[HBM_SPEC] * n,
        out_shape=[SDS(h.shape, h.dtype) for h in halves],
        input_output_aliases={a: a for a in range(n)},
        scratch_shapes=[pltpu.SemaphoreType.DMA((n,)), pltpu.SemaphoreType.DMA((n,))],
    )(*halves)


N_DEV = 8


def sum_devices(s, slots, me):
    R, Wd = s.shape

    def body(me_ref, s_ref, slots_ref, out_ref):
        acc = None
        for d in range(N_DEV):
            mine = me_ref[0] == d
            term = jnp.where(mine, s_ref[...], slots_ref[jnp.where(mine, d ^ 1, d)])
            acc = term if acc is None else acc + term
        out_ref[...] = acc

    vmem = pl.BlockSpec(memory_space=pltpu.VMEM)
    return pl.pallas_call(
        body, name="sum_devices", in_specs=[pl.BlockSpec(memory_space=pltpu.SMEM), vmem, vmem], out_specs=vmem,
        out_shape=SDS((R, Wd), F32),
    )(me, s, slots)


TRANSPOSED = ("w_in",)
BIG = {"ffn1_w_up": (D, 2 * FF, "col"), "ffn1_w_down": (FF, D, "row"), "w_in": (INW, D, "row"),
       "w_attn_up": (AW, D, "row"), "w_pool_up": (PW, D, "col"), "w_out": (D, D, "row"),
       "ffn2_w_up": (D, 2 * FF, "col"), "ffn2_w_down": (FF, D, "row")}
GROUPS = (("ffn1_w_up", "ffn1_w_down"), ("w_in", "w_attn_up", "w_pool_up", "w_out"), ("ffn2_w_up", "ffn2_w_down"))
SMALL = ("ffn1_norm", "mix_norm", "ffn2_norm", "final_norm", "pool_scale", "sinks", "pool_w_mix")
SMALL_W = 128


def _rowlike(name):
    return BIG[name][2] == "row"


def _half_dims(name):
    k, n, kind = BIG[name]
    return (k // N_CHIPS // 2, n) if kind == "row" else (k // 2, n // N_CHIPS)


def shard_view(name, shard):
    return shard.reshape((2,) + _half_dims(name))


def full_from_view(name, fv):
    k, n, _ = BIG[name]
    return fv.reshape(k, n)


def grad_view(name, g):
    kh, ns = _half_dims(name)
    return g.reshape(_full_shape(jax.ShapeDtypeStruct((2, kh, ns), g.dtype), _rowlike(name)))


def pack_small(d):
    parts = []
    for name in SMALL:
        a = d[name].reshape(-1)
        pad = (-a.shape[0]) % SMALL_W
        parts.append(jnp.pad(a, (0, pad)).reshape(-1, SMALL_W))
    a = jnp.concatenate(parts, axis=0)
    return jnp.pad(a, ((0, (-a.shape[0]) % 8), (0, 0)))


def unpack_small(a, like):
    out, r0 = {}, 0
    for name in SMALL:
        size = int(np.prod(like[name].shape))
        rows = -(-size // SMALL_W)
        out[name] = a[r0:r0 + rows].reshape(-1)[:size].reshape(like[name].shape)
        r0 += rows
    return out


WEIGHTS = ("ffn1_norm", "ffn1_w_up", "ffn1_w_down", "mix_norm", "w_in", "sinks", "w_attn_up", "pool_w_mix",
           "pool_scale", "w_pool_up", "w_out", "ffn2_norm", "ffn2_w_up", "ffn2_w_down", "final_norm")


def kernel(x, ffn1_norm, ffn1_w_up, ffn1_w_down, mix_norm, w_in, sinks, w_attn_up, pool_w_mix, pool_scale, w_pool_up, w_out, ffn2_norm, ffn2_w_up, ffn2_w_down, final_norm, loss_target, m_ffn1_norm, m_ffn1_w_up, m_ffn1_w_down, m_mix_norm, m_w_in, m_sinks, m_w_attn_up, m_pool_w_mix, m_pool_scale, m_w_pool_up, m_w_out, m_ffn2_norm, m_ffn2_w_up, m_ffn2_w_down, m_final_norm, v_ffn1_norm, v_ffn1_w_up, v_ffn1_w_down, v_mix_norm, v_w_in, v_sinks, v_w_attn_up, v_pool_w_mix, v_pool_scale, v_w_pool_up, v_w_out, v_ffn2_norm, v_ffn2_w_up, v_ffn2_w_down, v_final_norm):
    given = dict(locals())
    w = {n: given[n] for n in WEIGHTS}
    m = {n: given["m_" + n] for n in WEIGHTS}
    v = {n: given["v_" + n] for n in WEIGHTS}
    cx, cy, cc = _me()
    place = jnp.stack([cc, 2 * cx + cy]).astype(jnp.int32)

    def local2d(d, n):
        return d[n][0].T if n in TRANSPOSED else d[n][0]

    shards = {n: local2d(w, n) for n in BIG}
    sviews = {n: shard_view(n, shards[n].astype(BF16)) for n in BIG}
    grads, delta, new_m, new_v = {}, {}, {}, {}
    rowlikes = [[_rowlike(n) for n in names] for names in GROUPS]

    class Exchanges:
        def __init__(self):
            self.gathers, self.forwards, self.fulls, self.reductions, self.small = {}, {}, {}, {}, None

        def _start_gather(self, group, after):
            sv = [sviews[n] for n in GROUPS[group]]
            rl = rowlikes[group]
            self.gathers[group] = exchange_start(sv, [_full_shape(s, r) for s, r in zip(sv, rl)], _gather_plan(rl),
                                                 after, f"gather_start_{group}")

        def weights(self, group, after):
            if not self.gathers:
                self._start_gather(0, None)
            if group not in self.fulls:
                names, rl, state = GROUPS[group], rowlikes[group], self.gathers[group]
                if group in self.forwards:
                    _, fulls = exchange_wait(self.forwards.pop(group), _forward_plan(rl), after,
                                             f"forward_wait_{group}", FORWARDS)
                else:
                    sv, fulls = exchange_wait(state, _gather_plan(rl), state["token"] if after is None else after,
                                              f"gather_wait_{group}")
                    fulls = gather_finish(sv, fulls, rl, f"gather_finish_{group}")
                self.fulls[group] = {n: full_from_view(n, f) for n, f in zip(names, fulls)}
                if group + 1 < len(GROUPS):
                    self._start_gather(group + 1, fulls[0])
            return self.fulls[group]

        def started(self):
            return self.gathers[max(self.gathers)]["token"]

        def prefetch(self, group, after):
            rl, state = rowlikes[group], self.gathers[group]
            sv, fulls = exchange_wait(state, _gather_plan(rl), after, f"gather_wait_{group}")
            self.forwards[group] = exchange_start(sv, fulls, _forward_plan(rl), None, f"forward_start_{group}", FORWARDS)
            return self.forwards[group]["token"]

        def grads(self, g):
            names = tuple(g)
            rl = [_rowlike(n) for n in names]
            gv = [grad_view(n, g[n]) for n in names]
            state = exchange_start(gv, [_half_buffer_shape(a, r) for a, r in zip(gv, rl)], _swap_plan(rl), None,
                                   "swap_start_" + names[0], SIBLING)
            self.reductions[names] = state
            return state["token"]

        def advance(self, names, after):
            rl = [_rowlike(n) for n in names]
            gv, gots = exchange_wait(self.reductions[names], _swap_plan(rl), after, "swap_wait_" + names[0], SIBLING)
            qs = [add_halves(a, b, r, place, "add_halves_" + n) for a, b, r, n in zip(gv, gots, rl, names)]
            state = exchange_start(qs, [_slot_shape(q, r) for q, r in zip(qs, rl)], _all_to_all_plan(rl), None,
                                   "all_to_all_start_" + names[0])
            self.reductions[names] = state
            return state["token"]

        def finish(self, names, after):
            rl = [_rowlike(n) for n in names]
            qs, recvs = exchange_wait(self.reductions.pop(names), _all_to_all_plan(rl), after,
                                      "all_to_all_wait_" + names[0])
            halves = [sum_pieces(q, r, k, place, "sum_pieces_" + n) for q, r, k, n in zip(qs, recvs, rl, names)]
            for n, o in zip(names, join_halves(halves, "join_halves_" + names[0])):
                grads[n], delta[n], new_m[n], new_v[n] = adamw(shards[n], o.reshape(shards[n].shape), local2d(m, n),
                                                               local2d(v, n), "adamw_" + n)
            return [new_v[n] for n in names]

        def small_start(self, G, loss, after):
            packed = pack_small({n: G[n] for n in SMALL})
            used_rows = sum(-(-int(np.prod(small_like[n].shape)) // SMALL_W) for n in SMALL)
            assert packed.shape[0] > used_rows
            packed = packed.at[-1, 0].set(loss[0, 0])
            self.small = exchange_start([packed], [(N_DEV,) + packed.shape], _everyone_plan, after,
                                        "small_start", EVERYONE)
            return self.small["token"]

        def small_finish(self, after):
            (packed,), (slots,) = exchange_wait(self.small, _everyone_plan, after, "small_wait", EVERYONE)
            total = sum_devices(packed, slots, (4 * cx + 2 * cy + cc).astype(jnp.int32).reshape(1))
            self.loss = total[-1, 0]
            g, ds, ms, vs = adamw(pack_small(w), total, pack_small(m), pack_small(v), "adamw_small")
            for d, packed_d in ((grads, g), (delta, ds), (new_m, ms), (new_v, vs)):
                d.update(unpack_small(packed_d, small_like))
            return vs

    small_like = {n: w[n] for n in SMALL}
    S = {n: w[n].reshape(1, -1) for n in ("ffn1_norm", "mix_norm", "ffn2_norm", "final_norm", "pool_scale", "sinks")}
    S["pool_w_mix"] = w["pool_w_mix"][0].astype(BF16)
    exchanges = Exchanges()
    _, dx = local_fwd_bwd(x[0], loss_target[0], S, exchanges)
    loss = exchanges.loss

    def shaped(d, n):
        return (d[n].T if n in TRANSPOSED else d[n]).reshape(w[n].shape)

    return (loss, dx[None], *[shaped(grads, n) for n in WEIGHTS], *[shaped(delta, n) for n in WEIGHTS],
            *[shaped(new_m, n) for n in WEIGHTS], *[shaped(new_v, n) for n in WEIGHTS])
```

```python
import numpy as np
import jax
import jax.numpy as jnp
from jax import lax
from jax.experimental import pallas as pl
from jax.experimental.pallas import tpu as pltpu

F32 = jnp.float32
BF16 = jnp.bfloat16
SDS = jax.ShapeDtypeStruct
MESH = pl.DeviceIdType.MESH

D = 1024
FF = 2816
NQ = 16
NKV = 2
HD = 64
GQ = NQ // NKV
AW = NQ * HD
KVW = NKV * HD
BLK = 128
PW = 512
PG = 128
POOL_WINDOWS = (2, 4, 8, 16)
HALO = 16
INW = AW + 2 * KVW + PW + 2 * D
C_KV = AW
C_Z = AW + 2 * KVW
C_G = C_Z + PW
EPS = 1e-6
FF_CHUNK = 256
FF_CHUNKS = tuple((c, FF_CHUNK) for c in range(0, FF, FF_CHUNK))
SLOPES = tuple(float(2.0 ** (-8.0 * h / NQ)) for h in range(1, NQ + 1))
SCALE = HD ** -0.5

LR, B1, B2, ADAM_EPS, WD, STEP = 0.001, 0.9, 0.999, 1e-08, 0.01, 10

VMEM_LIMIT = 56 * 1024 * 1024
N_CHIPS = 4

NT = (((1,), (1,)), ((), ()))
TN = (((0,), (0,)), ((), ()))


def _cp(sem=None, vmem=VMEM_LIMIT):
    return pltpu.CompilerParams(dimension_semantics=sem, vmem_limit_bytes=vmem)


def _const_spec(shape):
    nd = len(shape)
    return pl.BlockSpec(shape, lambda *_: (0,) * nd, pipeline_mode=pl.Buffered(1))


def _rstd(x):
    return lax.rsqrt(jnp.mean(x * x, axis=-1, keepdims=True) + EPS)


def _rms_bwd(dn, xhat, rstd, g):
    dxhat = dn * g
    return rstd * (dxhat - xhat * jnp.mean(dxhat * xhat, axis=-1, keepdims=True))


def _dot(a, b):
    return jnp.dot(a, b, preferred_element_type=F32)


def _dot_nt(a, b):
    return lax.dot_general(a, b, NT, preferred_element_type=F32)


def _dot_tn(a, b):
    return lax.dot_general(a, b, TN, preferred_element_type=F32)


def ffn_fwd(h, g, wup, wdn, name, head=None):
    T = h.shape[0]
    TM = 512
    tile = lambda w: pl.BlockSpec((TM, w), lambda i: (i, 0))
    acc_spec = lambda w: pl.BlockSpec((1, w), lambda i: (0, 0))

    def body(h_ref, g_ref, wup_ref, wdn_ref, *rest):
        out_ref, ab_ref, hid_ref = rest[-3:] if head is None else rest[2:5]
        x = h_ref[...]
        n = (x * _rstd(x) * g_ref[...]).astype(BF16)
        for c0, w in FF_CHUNKS:
            a = _dot(n, wup_ref[:, c0:c0 + w])
            b = _dot(n, wup_ref[:, FF + c0:FF + c0 + w])
            sig = jax.nn.sigmoid(a)
            s = a * sig
            ab_ref[:, c0:c0 + w] = (b * (sig * (1.0 + a * (1.0 - sig)))).astype(BF16)
            ab_ref[:, FF + c0:FF + c0 + w] = s.astype(BF16)
            hid_ref[:, c0:c0 + w] = (s * b).astype(BF16)
        out = x + 0.5 * _dot(hid_ref[...], wdn_ref[...])
        if head is None:
            out_ref[...] = out
        else:
            t_ref, gf_ref, loss_ref, dgf_ref = rest[0], rest[1], rest[5], rest[6]
            out_ref[...] = _loss_head(out, t_ref[...], gf_ref[...], loss_ref, dgf_ref, pl.program_id(0) == 0)

    head_in, head_specs, head_out_specs, head_out_shape = [], [], [], []
    if head is not None:
        head_in, head_specs = list(head), [tile(D), _const_spec((1, D))]
        head_out_specs, head_out_shape = [acc_spec(1), acc_spec(D)], [SDS((1, 1), F32), SDS((1, D), F32)]
    return pl.pallas_call(
        body, name=name, grid=(T // TM,),
        in_specs=[tile(D), _const_spec((1, D)), _const_spec((D, 2 * FF)), _const_spec((FF, D))] + head_specs,
        out_specs=[tile(D), tile(2 * FF), tile(FF)] + head_out_specs,
        out_shape=[SDS((T, D), F32), SDS((T, 2 * FF), BF16), SDS((T, FF), BF16)] + head_out_shape,
        compiler_params=_cp(("arbitrary",)),
    )(h, g, wup, wdn, *head_in)


def _loss_head(x, target, g, loss_ref, dg_ref, first):
    @pl.when(first)
    def _():
        loss_ref[...] = jnp.zeros_like(loss_ref)
        dg_ref[...] = jnp.zeros_like(dg_ref)

    rstd = _rstd(x)
    xhat = x * rstd
    err = xhat * g - target
    loss_ref[...] += 0.5 * jnp.sum(jnp.mean(err * err, axis=-1, keepdims=True), axis=0, keepdims=True)
    dy = err * (1.0 / D)
    dg_ref[...] += jnp.sum(dy * xhat, axis=0, keepdims=True)
    return _rms_bwd(dy, xhat, rstd, g)


def _adamw_tile(w_ref, g_ref, m_ref, v_ref, go_ref, d_ref, nm_ref, nv_ref):
    gv = g_ref[...]
    go_ref[...] = gv
    nm = B1 * m_ref[...] + (1.0 - B1) * gv
    nv = B2 * v_ref[...] + (1.0 - B2) * (gv * gv)
    nm_ref[...] = nm
    nv_ref[...] = nv
    d_ref[...] = -LR * ((nm / (1.0 - B1 ** STEP)) / (jnp.sqrt(nv / (1.0 - B2 ** STEP)) + ADAM_EPS) + WD * w_ref[...])


def _riders(updates, steps, step_of):
    in_specs, out_specs, out_shapes, operands, tiles = [], [], [], [], []
    for w, g, m, v in updates:
        R, C = w.shape
        n = max(d for d in range(1, steps + 1) if R % d == 0 and (R // d) % 8 == 0)
        spec = pl.BlockSpec((R // n, C), lambda *ids, n=n: (jnp.minimum(step_of(*ids), n - 1), 0))
        in_specs += [spec] * 4
        out_specs += [spec] * 4
        out_shapes += [SDS((R, C), F32)] * 4
        operands += [w, g, m, v]
        tiles.append(n)

    def run(step, in_refs, out_refs):
        for u, n in enumerate(tiles):
            @pl.when(step < n)
            def _(u=u):
                _adamw_tile(*in_refs[4 * u:4 * u + 4], *out_refs[4 * u:4 * u + 4])

    return in_specs, out_specs, out_shapes, operands, run


def ffn_bwd_x(dh, h_in, ab, g, wup, wdn, name, updates=()):
    T = dh.shape[0]
    TM = 256
    r_in, r_out, r_shapes, r_args, ride = _riders(updates, T // TM, lambda i: i)

    def body(dh_ref, h_ref, ab_ref, g_ref, wup_ref, wdn_ref, *rest):
        dhin_ref, dup_ref, n_ref, dg_ref = rest[len(r_in):len(r_in) + 4]
        ride(pl.program_id(0), rest[:len(r_in)], rest[len(r_in) + 4:])
        x = h_ref[...]
        g = g_ref[...]
        rstd = _rstd(x)
        xhat = x * rstd
        n_ref[...] = (xhat * g).astype(BF16)
        dh = dh_ref[...]
        dhh = (0.5 * dh).astype(BF16)
        for c0, w in FF_CHUNKS:
            dhid = _dot_nt(dhh, wdn_ref[c0:c0 + w, :])
            dup_ref[:, c0:c0 + w] = (dhid * ab_ref[:, c0:c0 + w].astype(F32)).astype(BF16)
            dup_ref[:, FF + c0:FF + c0 + w] = (dhid * ab_ref[:, FF + c0:FF + c0 + w].astype(F32)).astype(BF16)
        dn = _dot_nt(dup_ref[...], wup_ref[...])
        dhin_ref[...] = dh + _rms_bwd(dn, xhat, rstd, g)

        @pl.when(pl.program_id(0) == 0)
        def _():
            dg_ref[...] = jnp.zeros_like(dg_ref)

        dg_ref[...] += jnp.sum(dn * xhat, axis=0, keepdims=True)

    tile = lambda w: pl.BlockSpec((TM, w), lambda i: (i, 0))
    return pl.pallas_call(
        body, name=name, grid=(T // TM,),
        in_specs=[tile(D), tile(D), tile(2 * FF), _const_spec((1, D)), _const_spec((D, 2 * FF)), _const_spec((FF, D))]
        + r_in,
        out_specs=[tile(D), tile(2 * FF), tile(D), pl.BlockSpec((1, D), lambda i: (0, 0))] + r_out,
        out_shape=[SDS((T, D), F32), SDS((T, 2 * FF), BF16), SDS((T, D), BF16), SDS((1, D), F32)] + r_shapes,
        compiler_params=_cp(("arbitrary",)),
    )(dh, h_in, ab, g, wup, wdn, *r_args)


TOKEN_SPEC = pl.BlockSpec((8, 128), lambda *_: (0, 0))


def _token_operand(token):
    return ([], []) if token is None else ([TOKEN_SPEC], [token])


def matmul_tn(a, b, name, *, tm, tn, tt=1024, b_scale=None, after=None):
    T, M = a.shape
    N = b.shape[1]
    tt = min(tt, T)
    assert M % tm == 0 and N % tn == 0 and T % tt == 0
    nt = T // tt
    token_spec, token_arg = _token_operand(after)

    def body(a_ref, b_ref, *rest):
        o_ref, acc_ref = rest[-2:]
        t = pl.program_id(2)

        @pl.when(t == 0)
        def _():
            acc_ref[...] = jnp.zeros_like(acc_ref)

        bv = b_ref[...]
        if b_scale is not None:
            bv = bv * b_scale
        acc_ref[...] += _dot_tn(a_ref[...].astype(BF16), bv.astype(BF16))

        @pl.when(t == nt - 1)
        def _():
            o_ref[...] = acc_ref[...].astype(BF16)

    return pl.pallas_call(
        body, name=name, grid=(M // tm, N // tn, nt),
        in_specs=[pl.BlockSpec((tt, tm), lambda i, j, t: (t, i)), pl.BlockSpec((tt, tn), lambda i, j, t: (t, j))]
        + token_spec,
        out_specs=pl.BlockSpec((tm, tn), lambda i, j, t: (i, j)),
        out_shape=SDS((M, N), BF16),
        scratch_shapes=[pltpu.VMEM((tm, tn), F32)],
        compiler_params=_cp(("parallel", "parallel", "arbitrary")),
    )(a, b, *token_arg)


def mix_in_fwd(h1, g, win_t):
    T = h1.shape[0]
    TM = 512

    def body(h_ref, g_ref, w_ref, u_ref, q_ref, kv_ref, z_ref, gt_ref):
        x = h_ref[...]
        u = (x * _rstd(x) * g_ref[...]).astype(BF16)
        u_ref[...] = u
        for c in range(0, AW, 256):
            q_ref[:, c:c + 256] = _dot_nt(u, w_ref[c:c + 256, :]).astype(BF16)
        kv_ref[...] = _dot_nt(u, w_ref[C_KV:C_Z, :]).astype(BF16)
        for c in range(0, PW, 256):
            z_ref[:, c:c + 256] = _dot_nt(u, w_ref[C_Z + c:C_Z + c + 256, :])
        for c in range(0, 2 * D, 256):
            gt_ref[:, c:c + 256] = _dot_nt(u, w_ref[C_G + c:C_G + c + 256, :]).astype(BF16)

    tile = lambda w: pl.BlockSpec((TM, w), lambda i: (i, 0))
    return pl.pallas_call(
        body, name="mix_in_fwd", grid=(T // TM,),
        in_specs=[tile(D), _const_spec((1, D)), _const_spec((INW, D))],
        out_specs=[tile(D), tile(AW), tile(2 * KVW), tile(PW), tile(2 * D)],
        out_shape=[SDS((T, D), BF16), SDS((T, AW), BF16), SDS((T, 2 * KVW), BF16), SDS((T, PW), F32),
                   SDS((T, 2 * D), BF16)],
        compiler_params=_cp(("arbitrary",)),
    )(h1, g, win_t)


def mix_in_bwd(dq, dkv, dz, dgt, dh2, h1, g, win_t):
    T = h1.shape[0]
    TM = min(1024, T)
    SUB = 256

    def body(dq_ref, dkv_ref, dz_ref, dgt_ref, dh2_ref, h_ref, g_ref, w_ref, dh1_ref, dg_ref):
        g = g_ref[...]
        dg = None
        for r0 in range(0, TM, SUB):
            rows = slice(r0, r0 + SUB)
            du = _dot(dq_ref[rows, :], w_ref[0:AW, :])
            du += _dot(dkv_ref[rows, :], w_ref[C_KV:C_Z, :])
            du += _dot(dz_ref[rows, :], w_ref[C_Z:C_G, :])
            du += _dot(dgt_ref[rows, :], w_ref[C_G:INW, :])
            x = h_ref[rows, :]
            rstd = _rstd(x)
            xhat = x * rstd
            dh1_ref[rows, :] = dh2_ref[rows, :] + _rms_bwd(du, xhat, rstd, g)
            part = jnp.sum(du * xhat, axis=0, keepdims=True)
            dg = part if dg is None else dg + part

        @pl.when(pl.program_id(0) == 0)
        def _():
            dg_ref[...] = jnp.zeros_like(dg_ref)

        dg_ref[...] += dg

    tile = lambda w: pl.BlockSpec((TM, w), lambda i: (i, 0))
    return pl.pallas_call(
        body, name="mix_in_bwd", grid=(T // TM,),
        in_specs=[tile(AW), tile(2 * KVW), tile(PW), tile(2 * D), tile(D), tile(D), _const_spec((1, D)),
                  _const_spec((INW, D))],
        out_specs=[tile(D), pl.BlockSpec((1, D), lambda i: (0, 0))],
        out_shape=[SDS((T, D), F32), SDS((1, D), F32)],
        compiler_params=_cp(("arbitrary",)),
    )(dq, dkv, dz, dgt, dh2, h1, g, win_t)


PAIR = 2 * HD
NPAIR = GQ // 2


def _lo_lanes():
    return lax.broadcasted_iota(jnp.int32, (BLK, PAIR), 1) < HD


def _stack_heads(ref, kvh, scale=None):
    lo = _lo_lanes()
    parts = []
    for pr in range(NPAIR):
        t = ref[:, (kvh * NPAIR + pr) * PAIR:(kvh * NPAIR + pr + 1) * PAIR]
        if scale is not None:
            t = t * scale
        zero = jnp.zeros_like(t)
        parts += [jnp.where(lo, t, zero), jnp.where(lo, zero, t)]
    return jnp.concatenate(parts, axis=0)


def _kv_tiles(kvc_ref, kvp_ref, tile, kvh):
    lo = _lo_lanes()
    dup, left, right = [], [], []
    for ref in (kvp_ref, kvc_ref):
        t = ref[:, tile * PAIR:(tile + 1) * PAIR]
        r = pltpu.roll(t.astype(F32), HD, 1).astype(BF16)
        zero = jnp.zeros_like(t)
        a, b = (t, r) if kvh == 0 else (r, t)
        dup.append(jnp.where(lo, a, b))
        left.append(jnp.where(lo, a, zero))
        right.append(jnp.where(lo, zero, b))
    cat = lambda xs: jnp.concatenate(xs, axis=0)
    return cat(dup), cat(left), cat(right)


def _band_consts(first):
    row = lax.broadcasted_iota(jnp.int32, (BLK, BLK), 0)
    col = lax.broadcasted_iota(jnp.int32, (BLK, BLK), 1)
    upper = col > row
    dist = jnp.where(upper, row - col + BLK, row - col).astype(F32)
    pen = jnp.where(jnp.logical_and(upper, first), -jnp.inf, 0.0)
    return upper, dist, pen


def _split_band(upper, t):
    zero = jnp.zeros_like(t)
    return jnp.concatenate([jnp.where(upper, t, zero), jnp.where(upper, zero, t)], axis=1)


def attn_fwd(q, kv, sinks):
    T = q.shape[0]
    nb = T // BLK

    def body(sink_ref, q_ref, kvc_ref, kvp_ref, att_ref, lse_ref):
        upper, dist, pen = _band_consts(pl.program_id(0) == 0)
        scores, values = [], []
        for kvh in range(NKV):
            kdup, _, _ = _kv_tiles(kvc_ref, kvp_ref, 0, kvh)
            values.append(_kv_tiles(kvc_ref, kvp_ref, 1, kvh)[1:])
            scores.append(_dot_nt(_stack_heads(q_ref, kvh, SCALE), kdup))
        for kvh in range(NKV):
            s_all = scores[kvh]
            vleft, vright = values[kvh]
            for pr in range(NPAIR):
                outs, inv = [], []
                for side, vpad in ((0, vleft), (1, vright)):
                    g = 2 * pr + side
                    hq = kvh * GQ + g
                    sink = sink_ref[0, hq]
                    rows = slice(g * BLK, (g + 1) * BLK)
                    s = jnp.where(upper, s_all[rows, 0:BLK], s_all[rows, BLK:2 * BLK]) - SLOPES[hq] * dist + pen
                    m = jnp.maximum(jnp.max(s, axis=-1, keepdims=True), sink)
                    p = jnp.exp(s - m)
                    l = jnp.sum(p, axis=-1, keepdims=True) + jnp.exp(sink - m)
                    lse_ref[:, hq:hq + 1] = m + jnp.log(l)
                    outs.append(_dot(_split_band(upper, p.astype(BF16)), vpad))
                    inv.append(1.0 / l)
                col0 = (kvh * NPAIR + pr) * PAIR
                att_ref[:, col0:col0 + PAIR] = ((outs[0] + outs[1]) * jnp.where(_lo_lanes(), inv[0], inv[1])).astype(BF16)

    return pl.pallas_call(
        body, name="attn_fwd", grid=(nb,),
        in_specs=[pl.BlockSpec(memory_space=pltpu.SMEM),
                  pl.BlockSpec((BLK, AW), lambda i: (i, 0)),
                  pl.BlockSpec((BLK, 2 * KVW), lambda i: (i, 0)),
                  pl.BlockSpec((BLK, 2 * KVW), lambda i: (jnp.maximum(i - 1, 0), 0))],
        out_specs=[pl.BlockSpec((BLK, AW), lambda i: (i, 0)), pl.BlockSpec((BLK, NQ), lambda i: (i, 0))],
        out_shape=[SDS((T, AW), BF16), SDS((T, NQ), F32)],
        compiler_params=_cp(("arbitrary",)),
    )(sinks, q, kv, kv)


def attn_bwd(q, kv, datt, att, lse, sinks):
    T = q.shape[0]
    nb = T // BLK

    def body(sink_ref, q_ref, kvc_ref, kvp_ref, do_ref, out_ref, lse_ref, dq_ref, dkv_ref, dsink_ref, carry_ref):
        i = pl.program_id(0)

        @pl.when(i == 0)
        def _():
            dsink_ref[...] = jnp.zeros_like(dsink_ref)
            carry_ref[...] = jnp.zeros_like(carry_ref)

        @pl.when(i < nb)
        def _():
            upper, dist, pen = _band_consts(i == 0)
            lo = _lo_lanes()
            dk_dup, dv_dup = [], []
            staged = []
            for kvh in range(NKV):
                kdup, kleft, kright = _kv_tiles(kvc_ref, kvp_ref, 0, kvh)
                vdup, _, _ = _kv_tiles(kvc_ref, kvp_ref, 1, kvh)
                qs = _stack_heads(q_ref, kvh, SCALE)
                dos = _stack_heads(do_ref, kvh)
                staged.append((kleft, kright, qs, dos, _dot_nt(qs, kdup), _dot_nt(dos, vdup)))
            deltas = []
            for pair in range(NQ // 2):
                cols = slice(pair * PAIR, (pair + 1) * PAIR)
                t = do_ref[:, cols].astype(F32) * out_ref[:, cols].astype(F32)
                deltas += [jnp.sum(jnp.where(lo, t, 0.0), axis=-1, keepdims=True),
                           jnp.sum(jnp.where(lo, 0.0, t), axis=-1, keepdims=True)]
            for kvh in range(NKV):
                kleft, kright, qs, dos, s_all, dp_all = staged[kvh]
                ds_parts, p_parts = [], []
                for pr in range(NPAIR):
                    dq = None
                    for side, kpad in ((0, kleft), (1, kright)):
                        g = 2 * pr + side
                        hq = kvh * GQ + g
                        lse_h = lse_ref[:, hq:hq + 1]
                        rows = slice(g * BLK, (g + 1) * BLK)
                        s = jnp.where(upper, s_all[rows, 0:BLK], s_all[rows, BLK:2 * BLK]) - SLOPES[hq] * dist + pen
                        p = jnp.exp(s - lse_h)
                        dp = jnp.where(upper, dp_all[rows, 0:BLK], dp_all[rows, BLK:2 * BLK])
                        delta = deltas[hq]
                        dsink_ref[:, hq:hq + 1] += -jnp.sum(jnp.exp(sink_ref[0, hq] - lse_h) * delta, axis=0,
                                                            keepdims=True)
                        ds = _split_band(upper, (p * (dp - delta)).astype(BF16))
                        ds_parts.append(ds)
                        p_parts.append(_split_band(upper, p.astype(BF16)))
                        d = _dot(ds, kpad)
                        dq = d if dq is None else dq + d
                    col0 = (kvh * NPAIR + pr) * PAIR
                    dq_ref[:, col0:col0 + PAIR] = (dq * SCALE).astype(BF16)
                dkw = _dot_tn(qs, jnp.concatenate(ds_parts, axis=0)).T
                dvw = _dot_tn(dos, jnp.concatenate(p_parts, axis=0)).T
                dk_dup.append(dkw + pltpu.roll(dkw, HD, 1))
                dv_dup.append(dvw + pltpu.roll(dvw, HD, 1))
            dk = jnp.where(jnp.concatenate([lo, lo], axis=0), dk_dup[0], dk_dup[1])
            dv = jnp.where(jnp.concatenate([lo, lo], axis=0), dv_dup[0], dv_dup[1])
            dkv_ref[:, 0:PAIR] = (carry_ref[:, 0:PAIR] + dk[0:BLK]).astype(BF16)
            dkv_ref[:, PAIR:2 * PAIR] = (carry_ref[:, PAIR:2 * PAIR] + dv[0:BLK]).astype(BF16)
            carry_ref[:, 0:PAIR] = dk[BLK:2 * BLK]
            carry_ref[:, PAIR:2 * PAIR] = dv[BLK:2 * BLK]

        @pl.when(i == nb)
        def _():
            dkv_ref[...] = carry_ref[...].astype(BF16)

    cur = lambda i: (jnp.minimum(i, nb - 1), 0)
    prev = lambda i: (jnp.maximum(jnp.minimum(i, nb - 1) - 1, 0), 0)
    return pl.pallas_call(
        body, name="attn_bwd", grid=(nb + 1,),
        in_specs=[pl.BlockSpec(memory_space=pltpu.SMEM),
                  pl.BlockSpec((BLK, AW), cur), pl.BlockSpec((BLK, 2 * KVW), cur), pl.BlockSpec((BLK, 2 * KVW), prev),
                  pl.BlockSpec((BLK, AW), cur), pl.BlockSpec((BLK, AW), cur), pl.BlockSpec((BLK, NQ), cur)],
        out_specs=[pl.BlockSpec((BLK, AW), cur),
                   pl.BlockSpec((BLK, 2 * KVW), lambda i: (jnp.maximum(i - 1, 0), 0)),
                   pl.BlockSpec((1, NQ), lambda i: (0, 0))],
        out_shape=[SDS((T, AW), BF16), SDS((T, 2 * KVW), BF16), SDS((1, NQ), F32)],
        scratch_shapes=[pltpu.VMEM((BLK, 2 * KVW), F32)],
        compiler_params=_cp(("arbitrary",)),
    )(sinks, q, kv, kv, datt, att, lse)


def _inv_counts(t0, rows):
    t = (t0 + lax.broadcasted_iota(jnp.int32, (rows, 1), 0) + 1).astype(F32)
    return [1.0 / jnp.minimum(t, float(w)) for w in POOL_WINDOWS]


def pool_fwd(z, wmix, scale):
    T = z.shape[0]
    TM = 512
    L = TM + HALO

    def body(z_ref, halo_ref, wmix_ref, scale_ref, pooled_ref, mixs_ref):
        i = pl.program_id(0)
        halo = jnp.where(i > 0, halo_ref[...], 0.0)
        zt = z_ref[...]
        e = jnp.concatenate([halo, zt], axis=0)
        sums = []
        s = e
        for k in (1, 2, 4, 8):
            s = s + pltpu.roll(s, k, 0)
            sums.append(s)
        inv = _inv_counts(i * TM, TM)
        for gi in range(len(POOL_WINDOWS)):
            cols = slice(gi * PG, (gi + 1) * PG)
            pooled = (sums[gi][HALO:, cols] * inv[gi] - zt[:, cols]).astype(BF16)
            pooled_ref[:, cols] = pooled
            mixs_ref[:, cols] = (_dot(pooled, wmix_ref[gi]) * scale_ref[:, cols]).astype(BF16)

    return pl.pallas_call(
        body, name="pool_fwd", grid=(T // TM,),
        in_specs=[pl.BlockSpec((TM, PW), lambda i: (i, 0)),
                  pl.BlockSpec((HALO, PW), lambda i: (jnp.maximum(i * (TM // HALO) - 1, 0), 0)),
                  _const_spec((len(POOL_WINDOWS), PG, PG)), _const_spec((1, PW))],
        out_specs=[pl.BlockSpec((TM, PW), lambda i: (i, 0)), pl.BlockSpec((TM, PW), lambda i: (i, 0))],
        out_shape=[SDS((T, PW), BF16), SDS((T, PW), BF16)],
        compiler_params=_cp(("arbitrary",)),
    )(z, z, wmix, scale)


def pool_bwd(dmixs, pooled, wmix, scale):
    T = dmixs.shape[0]
    TM = 512
    L = TM + HALO
    nt = T // TM

    def body(dm_ref, halo_ref, pooled_ref, wmix_ref, scale_ref, dz_ref, dwmix_ref, dscale_ref):
        i = pl.program_id(0)

        @pl.when(i == 0)
        def _():
            dwmix_ref[...] = jnp.zeros_like(dwmix_ref)
            dscale_ref[...] = jnp.zeros_like(dscale_ref)

        halo = jnp.where(i < nt - 1, halo_ref[...], 0.0)
        dm = dm_ref[...]
        e = jnp.concatenate([dm, halo], axis=0)
        inv = _inv_counts(i * TM, L)
        for gi in range(len(POOL_WINDOWS)):
            cols = slice(gi * PG, (gi + 1) * PG)
            w = wmix_ref[gi]
            dmixed = (e[:, cols] * scale_ref[:, cols]).astype(BF16)
            dpooled = _dot_nt(dmixed, w)
            pooled = pooled_ref[:, cols]
            mixed = _dot(pooled, w)
            dscale_ref[:, cols] += jnp.sum(dm[:, cols] * mixed, axis=0, keepdims=True)
            dwmix_ref[gi] += _dot_tn(pooled, dmixed[:TM])
            s = dpooled * inv[gi]
            k = 1
            while k < POOL_WINDOWS[gi]:
                s = s + pltpu.roll(s, L - k, 0)
                k *= 2
            dz_ref[:, cols] = (s[:TM] - dpooled[:TM]).astype(BF16)

    return pl.pallas_call(
        body, name="pool_bwd", grid=(nt,),
        in_specs=[pl.BlockSpec((TM, PW), lambda i: (i, 0)),
                  pl.BlockSpec((HALO, PW), lambda i: (jnp.minimum((i + 1) * (TM // HALO), T // HALO - 1), 0)),
                  pl.BlockSpec((TM, PW), lambda i: (i, 0)),
                  _const_spec((len(POOL_WINDOWS), PG, PG)), _const_spec((1, PW))],
        out_specs=[pl.BlockSpec((TM, PW), lambda i: (i, 0)),
                   pl.BlockSpec((len(POOL_WINDOWS), PG, PG), lambda i: (0, 0, 0)),
                   pl.BlockSpec((1, PW), lambda i: (0, 0))],
        out_shape=[SDS((T, PW), BF16), SDS((len(POOL_WINDOWS), PG, PG), F32), SDS((1, PW), F32)],
        compiler_params=_cp(("arbitrary",)),
    )(dmixs, dmixs, pooled, wmix, scale)


def merge_fwd(att, mixs, gt, h1, wattn, wpool, wout):
    T = h1.shape[0]
    TM = 512

    def body(att_ref, mixs_ref, gt_ref, h_ref, wa_ref, wp_ref, wo_ref, h2_ref, mg_ref, gf_ref):
        a = _dot(att_ref[...], wa_ref[...])
        p = _dot(mixs_ref[...], wp_ref[...])
        sa = jax.nn.sigmoid(gt_ref[:, 0:D].astype(F32))
        sp = jax.nn.sigmoid(gt_ref[:, D:2 * D].astype(F32))
        gf_ref[:, 0:D] = (a * sa * (1.0 - sa)).astype(BF16)
        gf_ref[:, D:2 * D] = (p * sp * (1.0 - sp)).astype(BF16)
        mg = (sa * a + sp * p).astype(BF16)
        mg_ref[...] = mg
        h2_ref[...] = h_ref[...] + _dot(mg, wo_ref[...])

    tile = lambda w: pl.BlockSpec((TM, w), lambda i: (i, 0))
    return pl.pallas_call(
        body, name="merge_fwd", grid=(T // TM,),
        in_specs=[tile(AW), tile(PW), tile(2 * D), tile(D), _const_spec((AW, D)), _const_spec((PW, D)),
                  _const_spec((D, D))],
        out_specs=[tile(D), tile(D), tile(2 * D)],
        out_shape=[SDS((T, D), F32), SDS((T, D), BF16), SDS((T, 2 * D), BF16)],
        compiler_params=_cp(("arbitrary",)),
    )(att, mixs, gt, h1, wattn, wpool, wout)


def merge_bwd(dh2, gt, gf, wattn, wpool, wout, after=None):
    T = dh2.shape[0]
    TM = 512
    token_spec, token_arg = _token_operand(after)

    def body(dh2_ref, gt_ref, gf_ref, wa_ref, wp_ref, wo_ref, *rest):
        datt_ref, dmixs_ref, dgt_ref, da_ref, dp_ref = rest[-5:]
        dm = _dot_nt(dh2_ref[...].astype(BF16), wo_ref[...])
        da = (dm * jax.nn.sigmoid(gt_ref[:, 0:D].astype(F32))).astype(BF16)
        dp = (dm * jax.nn.sigmoid(gt_ref[:, D:2 * D].astype(F32))).astype(BF16)
        da_ref[...] = da
        dp_ref[...] = dp
        dgt_ref[:, 0:D] = (dm * gf_ref[:, 0:D].astype(F32)).astype(BF16)
        dgt_ref[:, D:2 * D] = (dm * gf_ref[:, D:2 * D].astype(F32)).astype(BF16)
        datt_ref[...] = _dot_nt(da, wa_ref[...]).astype(BF16)
        dmixs_ref[...] = _dot_nt(dp, wp_ref[...])

    tile = lambda w: pl.BlockSpec((TM, w), lambda i: (i, 0))
    return pl.pallas_call(
        body, name="merge_bwd", grid=(T // TM,),
        in_specs=[tile(D), tile(2 * D), tile(2 * D), _const_spec((AW, D)), _const_spec((PW, D)),
                  _const_spec((D, D))] + token_spec,
        out_specs=[tile(AW), tile(PW), tile(2 * D), tile(D), tile(D)],
        out_shape=[SDS((T, AW), BF16), SDS((T, PW), F32), SDS((T, 2 * D), BF16), SDS((T, D), BF16),
                   SDS((T, D), BF16)],
        compiler_params=_cp(("arbitrary",)),
    )(dh2, gt, gf, wattn, wpool, wout, *token_arg)


def adamw(w, g, m, v, name):
    R, C = w.shape
    tile_bytes = 2 * 1024 * 1024
    tr = R
    if R * C * 4 > tile_bytes:
        tr = next(cand for cand in (512, 256, 128, 64, 32, 16, 8) if R % cand == 0 and cand * C * 4 <= tile_bytes)

    def body(*refs):
        _adamw_tile(*refs)

    spec = pl.BlockSpec((tr, C), lambda i: (i, 0))
    return pl.pallas_call(
        body, name=name, grid=(R // tr,),
        in_specs=[spec] * 4, out_specs=[spec] * 4, out_shape=[SDS((R, C), F32)] * 4,
        compiler_params=_cp(("parallel",)),
    )(w, g, m, v)


GROUP_FFN1, GROUP_MIX, GROUP_FFN2 = 0, 1, 2


def _behind(small, token):
    return small if token is None else small + token[0:1, 0:1]


def local_fwd_bwd(x, target, S, comm):
    W = comm.weights(GROUP_FFN1, None)
    h1, ab1, hid1 = ffn_fwd(x, _behind(S["ffn1_norm"], comm.started()), W["ffn1_w_up"], W["ffn1_w_down"], "ffn1_fwd")
    W = comm.weights(GROUP_MIX, h1)
    u, q, kv, z, gt = mix_in_fwd(h1, _behind(S["mix_norm"], comm.started()), W["w_in"])
    att, lse = attn_fwd(q, kv, S["sinks"])
    pooled, mixs = pool_fwd(z, S["pool_w_mix"], _behind(S["pool_scale"], comm.prefetch(GROUP_FFN2, att)))
    h2, merged, gf = merge_fwd(att, mixs, gt, h1, W["w_attn_up"], W["w_pool_up"], W["w_out"])
    W = comm.weights(GROUP_FFN2, h2)
    dh3, ab2, hid2, loss, g_final = ffn_fwd(h2, S["ffn2_norm"], W["ffn2_w_up"], W["ffn2_w_down"], "ffn2_fwd",
                                            head=(target, S["final_norm"]))

    G = {"final_norm": g_final}
    dh2, dup2, n2, G["ffn2_norm"] = ffn_bwd_x(dh3, h2, ab2, S["ffn2_norm"], W["ffn2_w_up"], W["ffn2_w_down"],
                                              "ffn2_bwd_x")
    ffn2 = ("ffn2_w_down", "ffn2_w_up")
    token = comm.grads({"ffn2_w_down": matmul_tn(hid2, dh3, "ffn2_dw_down", tm=1408, tn=512, tt=4096, b_scale=0.5),
                        "ffn2_w_up": matmul_tn(n2, dup2, "ffn2_dw_up", tm=D, tn=512, tt=4096)})

    W = comm.weights(GROUP_MIX, None)
    datt, dmixs, dgt, da, dp = merge_bwd(dh2, gt, gf, W["w_attn_up"], W["w_pool_up"], W["w_out"], after=token)
    token = comm.advance(ffn2, datt)
    g_mix = {"w_out": matmul_tn(merged, dh2, "dw_out", tm=D, tn=D, after=token),
             "w_attn_up": matmul_tn(att, da, "dw_attn_up", tm=AW, tn=D),
             "w_pool_up": matmul_tn(mixs, dp, "dw_pool_up", tm=PW, tn=D)}
    dz, G["pool_w_mix"], G["pool_scale"] = pool_bwd(dmixs, pooled, S["pool_w_mix"], _behind(S["pool_scale"], token))
    dq, dkv, G["sinks"] = attn_bwd(q, kv, datt, att, lse, _behind(S["sinks"], token))
    dh1, G["mix_norm"] = mix_in_bwd(dq, dkv, dz, dgt, dh2, h1, S["mix_norm"], W["w_in"])
    g_mix["w_in"] = jnp.concatenate([
        matmul_tn(dq, u, "dw_in_q", tm=AW, tn=D),
        matmul_tn(dkv, u, "dw_in_kv", tm=2 * KVW, tn=D),
        matmul_tn(dz, u, "dw_in_z", tm=PW, tn=D),
        matmul_tn(dgt, u, "dw_in_g", tm=D, tn=512, tt=4096),
    ], axis=0)
    mix = tuple(g_mix)
    token = comm.grads(g_mix)

    W = comm.weights(GROUP_FFN1, None)
    g_dn1 = matmul_tn(hid1, dh1, "ffn1_dw_down", tm=1408, tn=512, tt=4096, b_scale=0.5, after=token)
    token = comm.advance(mix, g_dn1)
    token2 = comm.grads({"ffn1_w_down": g_dn1})
    g1 = _behind(_behind(S["ffn1_norm"], token), token2)
    dx, dup1, n1, G["ffn1_norm"], *outs = ffn_bwd_x(dh1, x, ab1, g1, W["ffn1_w_up"], W["ffn1_w_down"], "ffn1_bwd_x",
                                                   updates=comm.reduce(ffn2, token2))
    comm.updated(ffn2, outs)
    token = comm.advance(("ffn1_w_down",), dx)
    token2 = comm.small_start(G, loss, dx)
    g_up1 = matmul_tn(n1, dup1, "ffn1_dw_up", tm=D, tn=512, tt=4096,
                      after=None if token is None else token + token2)
    token = comm.grads({"ffn1_w_up": g_up1})
    done = comm.finish(("ffn1_w_down",), token)
    token = comm.advance(("ffn1_w_up",), done)
    done = comm.finish(mix, token)
    done = comm.small_finish(done)
    comm.finish(("ffn1_w_up",), done)
    return loss, dx


HBM_SPEC = pl.BlockSpec(memory_space=pltpu.HBM)


def _me():
    return lax.axis_index("x"), lax.axis_index("y"), lax.axis_index("c")


def _peer_chip(x, y, k):
    return x ^ (k >> 1), y ^ (k & 1)


def _piece_half(ref, rowlike, j, h):
    if rowlike:
        return ref.at[j, h]
    ns = ref.shape[-1] // N_CHIPS
    return ref.at[h, :, pl.ds(pl.multiple_of(j * ns, 128), ns)]


def _piece(ref, rowlike, j):
    if rowlike:
        return ref.at[j]
    ns = ref.shape[-1] // N_CHIPS
    return ref.at[:, :, pl.ds(pl.multiple_of(j * ns, 128), ns)]


def _full_shape(shard_view, rowlike):
    _, kh, ns = shard_view.shape
    return (N_CHIPS, 2, kh, ns) if rowlike else (2, kh, N_CHIPS * ns)


def _remote(src, dst, send_sem, recv_sem, dev):
    return pltpu.make_async_remote_copy(src, dst, send_sem, recv_sem, device_id=dev, device_id_type=MESH)


SEM_SPEC = pl.BlockSpec(memory_space=pltpu.SEMAPHORE)
SPLIT_PARAMS = pltpu.CompilerParams(has_side_effects=pltpu.SideEffectType.DATAFLOW_SIDE_EFFECTING)


def _gather_plan(rowlikes):
    def plan(s_refs, f_refs, a, k, x, y, c):
        px, py = _peer_chip(x, y, k)
        return (s_refs[a].at[c], _piece_half(f_refs[a], rowlikes[a], 2 * x + y, c),
                _piece_half(f_refs[a], rowlikes[a], 2 * px + py, c), (px, py, c))
    return plan


def _all_to_all_plan(rowlikes):
    def plan(q_refs, r_refs, a, k, x, y, c):
        px, py = _peer_chip(x, y, k)
        if rowlikes[a]:
            src = q_refs[a].at[2 * px + py]
        else:
            ns = q_refs[a].shape[-1] // N_CHIPS
            src = q_refs[a].at[:, pl.ds(pl.multiple_of((2 * px + py) * ns, 128), ns)]
        return src, r_refs[a].at[k - 1], r_refs[a].at[k - 1], (px, py, c)
    return plan


def _swap_plan(rowlikes):
    def plan(g_refs, got_refs, a, k, x, y, c):
        src = g_refs[a].at[:, 1 - c] if rowlikes[a] else g_refs[a].at[1 - c]
        return src, got_refs[a], got_refs[a], (x, y, 1 - c)
    return plan


def _everyone_plan(s_refs, slot_refs, a, k, x, y, c):
    px, py, pc = x ^ (k >> 2), y ^ ((k >> 1) & 1), c ^ (k & 1)
    return s_refs[a], slot_refs[a].at[4 * x + 2 * y + c], slot_refs[a].at[4 * px + 2 * py + pc], (px, py, pc)


def _forward_plan(rowlikes):
    def plan(s_refs, f_refs, a, k, x, y, c):
        sib = (x, y, 1 - c)
        if k == 0:
            own = _piece(f_refs[a], rowlikes[a], 2 * x + y)
            return s_refs[a], own, own, sib
        px, py = _peer_chip(x, y, k)
        mine = _piece_half(f_refs[a], rowlikes[a], 2 * px + py, c)
        return mine, mine, _piece_half(f_refs[a], rowlikes[a], 2 * px + py, 1 - c), sib
    return plan


CHIPS, SIBLING, EVERYONE, FORWARDS = (1, 2, 3), (1,), tuple(range(1, 8)), (0, 1, 2, 3)


def _as_list(after):
    return [] if after is None else list(after) if isinstance(after, (list, tuple)) else [after]


def exchange_start(srcs, land_shapes, plan, after, name, peers=CHIPS):
    n = len(srcs)
    lands = [l if hasattr(l, "dtype") else lax.empty(l, s.dtype) for l, s in zip(land_shapes, srcs)]

    behind = _as_list(after)

    def body(*refs):
        s_refs, l_refs = refs[:n], refs[n:2 * n]
        send_sems, recv_sems = refs[2 * n + len(behind)], refs[2 * n + len(behind) + 1]
        token = refs[-1]
        x, y, c = _me()
        for a in range(n):
            for i, k in enumerate(peers):
                src, dst, _, peer = plan(s_refs, l_refs, a, k, x, y, c)
                sem = len(peers) * a + i
                _remote(src, dst, send_sems.at[sem], recv_sems.at[sem], peer).start()
        token[...] = jnp.zeros_like(token)

    n_sems = len(peers) * n
    outs = pl.pallas_call(
        body, name=name, in_specs=[HBM_SPEC] * (2 * n) + [pl.BlockSpec(memory_space=pl.ANY)] * len(behind),
        out_specs=[SEM_SPEC, SEM_SPEC] + [HBM_SPEC] * (2 * n) + [pl.BlockSpec(memory_space=pltpu.VMEM)],
        out_shape=[pltpu.SemaphoreType.DMA((n_sems,)), pltpu.SemaphoreType.DMA((n_sems,))]
        + [pltpu.HBM(a.shape, a.dtype) for a in (*srcs, *lands)] + [SDS((8, 128), F32)],
        input_output_aliases={i: 2 + i for i in range(2 * n)},
        compiler_params=SPLIT_PARAMS,
    )(*[pltpu.with_memory_space_constraint(a, pltpu.HBM) for a in (*srcs, *lands)], *behind)
    return {"sems": outs[:2], "srcs": outs[2:2 + n], "lands": outs[2 + n:2 + 2 * n], "token": outs[-1]}


def exchange_wait(state, plan, after, name, peers=CHIPS):
    n = len(state["srcs"])
    behind = _as_list(after)

    def body(*refs):
        s_refs, l_refs = refs[:n], refs[n:2 * n]
        send_sems, recv_sems = refs[2 * n], refs[2 * n + 1]
        x, y, c = _me()
        for a in range(n):
            for i, k in enumerate(peers):
                src, _, landing, peer = plan(s_refs, l_refs, a, k, x, y, c)
                sem = len(peers) * a + i
                cp = _remote(src, landing, send_sems.at[sem], recv_sems.at[sem], peer)
                cp.wait_send()
                cp.wait_recv()

    bufs = (*state["srcs"], *state["lands"])
    outs = pl.pallas_call(
        body, name=name,
        in_specs=[HBM_SPEC] * (2 * n) + [SEM_SPEC, SEM_SPEC] + [pl.BlockSpec(memory_space=pl.ANY)] * len(behind),
        out_specs=[HBM_SPEC] * (2 * n),
        out_shape=[pltpu.HBM(a.shape, a.dtype) for a in bufs],
        input_output_aliases={i: i for i in range(2 * n)},
        compiler_params=SPLIT_PARAMS,
    )(*bufs, *state["sems"], *behind)
    return outs[:n], outs[n:]


def gather_finish(shards, fulls, rowlikes, name):
    n = len(shards)

    def body(*refs):
        s_refs, f_refs = refs[:n], refs[2 * n:3 * n]
        send_sems, recv_sems = refs[3 * n:]
        x, y, c = _me()
        chip = 2 * x + y
        sib = (x, y, 1 - c)
        sends = []
        for a in range(n):
            own = _piece(f_refs[a], rowlikes[a], chip)
            cp = _remote(s_refs[a], own, send_sems.at[a, 0], recv_sems.at[a, 0], sib)
            cp.start()
            sends.append(cp)
            for k in (1, 2, 3):
                px, py = _peer_chip(x, y, k)
                slot = _piece_half(f_refs[a], rowlikes[a], 2 * px + py, c)
                cp = _remote(slot, slot, send_sems.at[a, k], recv_sems.at[a, k], sib)
                cp.start()
                sends.append(cp)
        for a in range(n):
            own = _piece(f_refs[a], rowlikes[a], chip)
            _remote(own, own, send_sems.at[a, 0], recv_sems.at[a, 0], sib).wait_recv()
            for k in (1, 2, 3):
                px, py = _peer_chip(x, y, k)
                slot = _piece_half(f_refs[a], rowlikes[a], 2 * px + py, 1 - c)
                _remote(slot, slot, send_sems.at[a, k], recv_sems.at[a, k], sib).wait_recv()
        for cp in sends:
            cp.wait_send()

    return pl.pallas_call(
        body, name=name, in_specs=[HBM_SPEC] * (2 * n), out_specs=[HBM_SPEC] * n,
        out_shape=[SDS(f.shape, f.dtype) for f in fulls],
        input_output_aliases={n + a: a for a in range(n)},
        scratch_shapes=[pltpu.SemaphoreType.DMA((n, 4)), pltpu.SemaphoreType.DMA((n, 4))],
    )(*shards, *fulls)


def _half_buffer_shape(gview, rowlike):
    return (N_CHIPS,) + gview.shape[2:] if rowlike else gview.shape[1:]


ROW_TILES = 2


def _piece_specs(rowlike, kh, ns, piece, half):
    tr = kh // ROW_TILES
    if rowlike:
        return (pl.BlockSpec((None, None, tr, ns), lambda *g: (piece(*g), half(*g), g[-2], 0)),
                pl.BlockSpec((None, tr, ns), lambda *g: (piece(*g), g[-2], 0)))
    return (pl.BlockSpec((None, tr, ns), lambda *g: (half(*g), g[-2], piece(*g))),
            pl.BlockSpec((tr, ns), lambda *g: (g[-2], piece(*g))))


def add_halves(gview, got, rowlike, place, name):
    kh, ns = (gview.shape[2], gview.shape[3]) if rowlike else (gview.shape[1], gview.shape[2] // N_CHIPS)

    def body(place_ref, g_ref, got_ref, o_ref):
        o_ref[...] = (g_ref[...].astype(F32) + got_ref[...].astype(F32)).astype(BF16)

    g_spec, h_spec = _piece_specs(rowlike, kh, ns, lambda j, r, p: j, lambda j, r, p: p[0])
    return pl.pallas_call(
        body, name=name,
        grid_spec=pltpu.PrefetchScalarGridSpec(num_scalar_prefetch=1, grid=(N_CHIPS, ROW_TILES),
                                               in_specs=[g_spec, h_spec], out_specs=h_spec),
        out_shape=SDS(got.shape, BF16),
        compiler_params=_cp(("parallel", "parallel")),
    )(place, gview, got)


def _slot_shape(q, rowlike):
    return (3,) + (q.shape[1:] if rowlike else (q.shape[0], q.shape[1] // N_CHIPS))


def sum_pieces(q, recv, rowlike, place, name):
    kh, ns = recv.shape[1], recv.shape[2]
    tr = kh // ROW_TILES

    def body(place_ref, q_ref, r_ref, o_ref):
        acc = q_ref[...].astype(F32)
        for k in range(3):
            acc = acc + r_ref[k].astype(F32)
        o_ref[...] = acc

    _, h_spec = _piece_specs(rowlike, kh, ns, lambda z, r, p: p[1], lambda z, r, p: p[0])
    return pl.pallas_call(
        body, name=name,
        grid_spec=pltpu.PrefetchScalarGridSpec(
            num_scalar_prefetch=1, grid=(1, ROW_TILES),
            in_specs=[h_spec, pl.BlockSpec((3, tr, ns), lambda z, r, p: (0, r, 0))],
            out_specs=pl.BlockSpec((None, tr, ns), lambda z, r, p: (p[0], r, 0))),
        out_shape=SDS((2, kh, ns), F32),
        compiler_params=_cp(("parallel", "parallel")),
    )(place, q, recv)


def join_halves(halves, name):
    n = len(halves)

    def body(*refs):
        o_refs = refs[n:2 * n]
        send_sems, recv_sems = refs[2 * n:]
        x, y, c = _me()
        sib = (x, y, 1 - c)
        sends = []
        for a in range(n):
            cp = _remote(o_refs[a].at[c], o_refs[a].at[c], send_sems.at[a], recv_sems.at[a], sib)
            cp.start()
            sends.append(cp)
        for a in range(n):
            got = o_refs[a].at[1 - c]
            _remote(got, got, send_sems.at[a], recv_sems.at[a], sib).wait_recv()
        for cp in sends:
            cp.wait_send()

    return pl.pallas_call(
        body, name=name, in_specs=[HBM_SPEC] * n, out_specs=[HBM_SPEC] * n,
        out_shape=[SDS(h.shape, h.dtype) for h in halves],
        input_output_aliases={a: a for a in range(n)},
        scratch_shapes=[pltpu.SemaphoreType.DMA((n,)), pltpu.SemaphoreType.DMA((n,))],
    )(*halves)


N_DEV = 8


def sum_devices(s, slots, me):
    R, Wd = s.shape

    def body(me_ref, s_ref, slots_ref, out_ref):
        acc = None
        for d in range(N_DEV):
            mine = me_ref[0] == d
            term = jnp.where(mine, s_ref[...], slots_ref[jnp.where(mine, d ^ 1, d)])
            acc = term if acc is None else acc + term
        out_ref[...] = acc

    vmem = pl.BlockSpec(memory_space=pltpu.VMEM)
    return pl.pallas_call(
        body, name="sum_devices", in_specs=[pl.BlockSpec(memory_space=pltpu.SMEM), vmem, vmem], out_specs=vmem,
        out_shape=SDS((R, Wd), F32),
    )(me, s, slots)


TRANSPOSED = ("w_in",)
BIG = {"ffn1_w_up": (D, 2 * FF, "col"), "ffn1_w_down": (FF, D, "row"), "w_in": (INW, D, "row"),
       "w_attn_up": (AW, D, "row"), "w_pool_up": (PW, D, "col"), "w_out": (D, D, "row"),
       "ffn2_w_up": (D, 2 * FF, "col"), "ffn2_w_down": (FF, D, "row")}
GROUPS = (("ffn1_w_up", "ffn1_w_down"), ("w_in", "w_attn_up", "w_pool_up", "w_out"), ("ffn2_w_up", "ffn2_w_down"))
SMALL = ("ffn1_norm", "mix_norm", "ffn2_norm", "final_norm", "pool_scale", "sinks", "pool_w_mix")
SMALL_W = 128


def _rowlike(name):
    return BIG[name][2] == "row"


def _half_dims(name):
    k, n, kind = BIG[name]
    return (k // N_CHIPS // 2, n) if kind == "row" else (k // 2, n // N_CHIPS)


def shard_view(name, shard):
    return shard.reshape((2,) + _half_dims(name))


def full_from_view(name, fv):
    k, n, _ = BIG[name]
    return fv.reshape(k, n)


def grad_view(name, g):
    kh, ns = _half_dims(name)
    return g.reshape(_full_shape(jax.ShapeDtypeStruct((2, kh, ns), g.dtype), _rowlike(name)))


def pack_small(d):
    parts = []
    for name in SMALL:
        a = d[name].reshape(-1)
        pad = (-a.shape[0]) % SMALL_W
        parts.append(jnp.pad(a, (0, pad)).reshape(-1, SMALL_W))
    a = jnp.concatenate(parts, axis=0)
    return jnp.pad(a, ((0, (-a.shape[0]) % 8), (0, 0)))


def unpack_small(a, like):
    out, r0 = {}, 0
    for name in SMALL:
        size = int(np.prod(like[name].shape))
        rows = -(-size // SMALL_W)
        out[name] = a[r0:r0 + rows].reshape(-1)[:size].reshape(like[name].shape)
        r0 += rows
    return out


WEIGHTS = ("ffn1_norm", "ffn1_w_up", "ffn1_w_down", "mix_norm", "w_in", "sinks", "w_attn_up", "pool_w_mix",
           "pool_scale", "w_pool_up", "w_out", "ffn2_norm", "ffn2_w_up", "ffn2_w_down", "final_norm")


def kernel(x, ffn1_norm, ffn1_w_up, ffn1_w_down, mix_norm, w_in, sinks, w_attn_up, pool_w_mix, pool_scale, w_pool_up, w_out, ffn2_norm, ffn2_w_up, ffn2_w_down, final_norm, loss_target, m_ffn1_norm, m_ffn1_w_up, m_ffn1_w_down, m_mix_norm, m_w_in, m_sinks, m_w_attn_up, m_pool_w_mix, m_pool_scale, m_w_pool_up, m_w_out, m_ffn2_norm, m_ffn2_w_up, m_ffn2_w_down, m_final_norm, v_ffn1_norm, v_ffn1_w_up, v_ffn1_w_down, v_mix_norm, v_w_in, v_sinks, v_w_attn_up, v_pool_w_mix, v_pool_scale, v_w_pool_up, v_w_out, v_ffn2_norm, v_ffn2_w_up, v_ffn2_w_down, v_final_norm):
    given = dict(locals())
    w = {n: given[n] for n in WEIGHTS}
    m = {n: given["m_" + n] for n in WEIGHTS}
    v = {n: given["v_" + n] for n in WEIGHTS}
    cx, cy, cc = _me()
    place = jnp.stack([cc, 2 * cx + cy]).astype(jnp.int32)

    def local2d(d, n):
        return d[n][0].T if n in TRANSPOSED else d[n][0]

    shards = {n: local2d(w, n) for n in BIG}
    sviews = {n: shard_view(n, shards[n].astype(BF16)) for n in BIG}
    grads, delta, new_m, new_v = {}, {}, {}, {}
    rowlikes = [[_rowlike(n) for n in names] for names in GROUPS]

    class Exchanges:
        def __init__(self):
            self.gathers, self.forwards, self.fulls, self.reductions, self.small = {}, {}, {}, {}, None

        def _start_gather(self, group, after):
            sv = [sviews[n] for n in GROUPS[group]]
            rl = rowlikes[group]
            self.gathers[group] = exchange_start(sv, [_full_shape(s, r) for s, r in zip(sv, rl)], _gather_plan(rl),
                                                 after, f"gather_start_{group}")

        def weights(self, group, after):
            if not self.gathers:
                self._start_gather(0, None)
            if group not in self.fulls:
                names, rl, state = GROUPS[group], rowlikes[group], self.gathers[group]
                if group in self.forwards:
                    _, fulls = exchange_wait(self.forwards.pop(group), _forward_plan(rl), after,
                                             f"forward_wait_{group}", FORWARDS)
                else:
                    sv, fulls = exchange_wait(state, _gather_plan(rl), state["token"] if after is None else after,
                                              f"gather_wait_{group}")
                    fulls = gather_finish(sv, fulls, rl, f"gather_finish_{group}")
                self.fulls[group] = {n: full_from_view(n, f) for n, f in zip(names, fulls)}
                if group + 1 < len(GROUPS):
                    self._start_gather(group + 1, fulls[0])
            return self.fulls[group]

        def started(self):
            return self.gathers[max(self.gathers)]["token"]

        def prefetch(self, group, after):
            rl, state = rowlikes[group], self.gathers[group]
            sv, fulls = exchange_wait(state, _gather_plan(rl), after, f"gather_wait_{group}")
            self.forwards[group] = exchange_start(sv, fulls, _forward_plan(rl), None, f"forward_start_{group}", FORWARDS)
            return self.forwards[group]["token"]

        def grads(self, g):
            names = tuple(g)
            rl = [_rowlike(n) for n in names]
            gv = [grad_view(n, g[n]) for n in names]
            state = exchange_start(gv, [_half_buffer_shape(a, r) for a, r in zip(gv, rl)], _swap_plan(rl), None,
                                   "swap_start_" + names[0], SIBLING)
            self.reductions[names] = state
            return state["token"]

        def advance(self, names, after):
            rl = [_rowlike(n) for n in names]
            gv, gots = exchange_wait(self.reductions[names], _swap_plan(rl), after, "swap_wait_" + names[0], SIBLING)
            qs = [add_halves(a, b, r, place, "add_halves_" + n) for a, b, r, n in zip(gv, gots, rl, names)]
            state = exchange_start(qs, [_slot_shape(q, r) for q, r in zip(qs, rl)], _all_to_all_plan(rl), None,
                                   "all_to_all_start_" + names[0])
            self.reductions[names] = state
            return state["token"]

        def reduce(self, names, after):
            rl = [_rowlike(n) for n in names]
            qs, recvs = exchange_wait(self.reductions.pop(names), _all_to_all_plan(rl), after,
                                      "all_to_all_wait_" + names[0])
            halves = [sum_pieces(q, r, k, place, "sum_pieces_" + n) for q, r, k, n in zip(qs, recvs, rl, names)]
            return [(shards[n], o.reshape(shards[n].shape), local2d(m, n), local2d(v, n))
                    for n, o in zip(names, join_halves(halves, "join_halves_" + names[0]))]

        def updated(self, names, outs):
            for i, n in enumerate(names):
                grads[n], delta[n], new_m[n], new_v[n] = outs[4 * i:4 * i + 4]
            return [new_v[n] for n in names]

        def finish(self, names, after):
            updates = self.reduce(names, after)
            return self.updated(names, [o for n, u in zip(names, updates) for o in adamw(*u, "adamw_" + n)])

        def small_start(self, G, loss, after):
            packed = pack_small({n: G[n] for n in SMALL})
            used_rows = sum(-(-int(np.prod(small_like[n].shape)) // SMALL_W) for n in SMALL)
            assert packed.shape[0] > used_rows
            packed = packed.at[-1, 0].set(loss[0, 0])
            self.small = exchange_start([packed], [(N_DEV,) + packed.shape], _everyone_plan, after,
                                        "small_start", EVERYONE)
            return self.small["token"]

        def small_finish(self, after):
            (packed,), (slots,) = exchange_wait(self.small, _everyone_plan, after, "small_wait", EVERYONE)
            total = sum_devices(packed, slots, (4 * cx + 2 * cy + cc).astype(jnp.int32).reshape(1))
            self.loss = total[-1, 0]
            g, ds, ms, vs = adamw(pack_small(w), total, pack_small(m), pack_small(v), "adamw_small")
            for d, packed_d in ((grads, g), (delta, ds), (new_m, ms), (new_v, vs)):
                d.update(unpack_small(packed_d, small_like))
            return vs

    small_like = {n: w[n] for n in SMALL}
    S = {n: w[n].reshape(1, -1) for n in ("ffn1_norm", "mix_norm", "ffn2_norm", "final_norm", "pool_scale", "sinks")}
    S["pool_w_mix"] = w["pool_w_mix"][0].astype(BF16)
    exchanges = Exchanges()
    _, dx = local_fwd_bwd(x[0], loss_target[0], S, exchanges)
    loss = exchanges.loss

    def shaped(d, n):
        return (d[n].T if n in TRANSPOSED else d[n]).reshape(w[n].shape)

    return (loss, dx[None], *[shaped(grads, n) for n in WEIGHTS], *[shaped(delta, n) for n in WEIGHTS],
            *[shaped(new_m, n) for n in WEIGHTS], *[shaped(new_v, n) for n in WEIGHTS])
```

```python
import numpy as np
import jax
import jax.numpy as jnp
from jax import lax
from jax.experimental import pallas as pl
from jax.experimental.pallas import tpu as pltpu

F32 = jnp.float32
BF16 = jnp.bfloat16
SDS = jax.ShapeDtypeStruct
MESH = pl.DeviceIdType.MESH

D = 1024
FF = 2816
NQ = 16
NKV = 2
HD = 64
GQ = NQ // NKV
AW = NQ * HD
KVW = NKV * HD
BLK = 128
PW = 512
PG = 128
POOL_WINDOWS = (2, 4, 8, 16)
HALO = 16
INW = AW + 2 * KVW + PW + 2 * D
C_KV = AW
C_Z = AW + 2 * KVW
C_G = C_Z + PW
EPS = 1e-6
FF_CHUNK = 256
FF_CHUNKS = tuple((c, FF_CHUNK) for c in range(0, FF, FF_CHUNK))
SLOPES = tuple(float(2.0 ** (-8.0 * h / NQ)) for h in range(1, NQ + 1))
SCALE = HD ** -0.5

LR, B1, B2, ADAM_EPS, WD, STEP = 0.001, 0.9, 0.999, 1e-08, 0.01, 10

VMEM_LIMIT = 56 * 1024 * 1024
N_CHIPS = 4

NT = (((1,), (1,)), ((), ()))
TN = (((0,), (0,)), ((), ()))


def _cp(sem=None, vmem=VMEM_LIMIT):
    return pltpu.CompilerParams(dimension_semantics=sem, vmem_limit_bytes=vmem)


def _const_spec(shape):
    nd = len(shape)
    return pl.BlockSpec(shape, lambda *_: (0,) * nd, pipeline_mode=pl.Buffered(1))


def _rstd(x):
    return lax.rsqrt(jnp.mean(x * x, axis=-1, keepdims=True) + EPS)


def _rms_bwd(dn, xhat, rstd, g):
    dxhat = dn * g
    return rstd * (dxhat - xhat * jnp.mean(dxhat * xhat, axis=-1, keepdims=True))


def _dot(a, b):
    return jnp.dot(a, b, preferred_element_type=F32)


def _dot_nt(a, b):
    return lax.dot_general(a, b, NT, preferred_element_type=F32)


def _dot_tn(a, b):
    return lax.dot_general(a, b, TN, preferred_element_type=F32)


def ffn_fwd(h, g, wup, wdn, name, head=None):
    T = h.shape[0]
    TM = 512
    tile = lambda w: pl.BlockSpec((TM, w), lambda i: (i, 0))
    acc_spec = lambda w: pl.BlockSpec((1, w), lambda i: (0, 0))

    def body(h_ref, g_ref, wup_ref, wdn_ref, *rest):
        out_ref, ab_ref, hid_ref = rest[-3:] if head is None else rest[2:5]
        x = h_ref[...]
        n = (x * _rstd(x) * g_ref[...]).astype(BF16)
        for c0, w in FF_CHUNKS:
            a = _dot(n, wup_ref[:, c0:c0 + w])
            b = _dot(n, wup_ref[:, FF + c0:FF + c0 + w])
            sig = jax.nn.sigmoid(a)
            s = a * sig
            ab_ref[:, c0:c0 + w] = (b * (sig * (1.0 + a * (1.0 - sig)))).astype(BF16)
            ab_ref[:, FF + c0:FF + c0 + w] = s.astype(BF16)
            hid_ref[:, c0:c0 + w] = (s * b).astype(BF16)
        out = x + 0.5 * _dot(hid_ref[...], wdn_ref[...])
        if head is None:
            out_ref[...] = out
        else:
            t_ref, gf_ref, loss_ref, dgf_ref = rest[0], rest[1], rest[5], rest[6]
            out_ref[...] = _loss_head(out, t_ref[...], gf_ref[...], loss_ref, dgf_ref, pl.program_id(0) == 0)

    head_in, head_specs, head_out_specs, head_out_shape = [], [], [], []
    if head is not None:
        head_in, head_specs = list(head), [tile(D), _const_spec((1, D))]
        head_out_specs, head_out_shape = [acc_spec(1), acc_spec(D)], [SDS((1, 1), F32), SDS((1, D), F32)]
    return pl.pallas_call(
        body, name=name, grid=(T // TM,),
        in_specs=[tile(D), _const_spec((1, D)), _const_spec((D, 2 * FF)), _const_spec((FF, D))] + head_specs,
        out_specs=[tile(D), tile(2 * FF), tile(FF)] + head_out_specs,
        out_shape=[SDS((T, D), F32), SDS((T, 2 * FF), BF16), SDS((T, FF), BF16)] + head_out_shape,
        compiler_params=_cp(("arbitrary",)),
    )(h, g, wup, wdn, *head_in)


def _loss_head(x, target, g, loss_ref, dg_ref, first):
    @pl.when(first)
    def _():
        loss_ref[...] = jnp.zeros_like(loss_ref)
        dg_ref[...] = jnp.zeros_like(dg_ref)

    rstd = _rstd(x)
    xhat = x * rstd
    err = xhat * g - target
    loss_ref[...] += 0.5 * jnp.sum(jnp.mean(err * err, axis=-1, keepdims=True), axis=0, keepdims=True)
    dy = err * (1.0 / D)
    dg_ref[...] += jnp.sum(dy * xhat, axis=0, keepdims=True)
    return _rms_bwd(dy, xhat, rstd, g)


def _adamw_tile(w_ref, g_ref, m_ref, v_ref, go_ref, d_ref, nm_ref, nv_ref):
    gv = g_ref[...]
    go_ref[...] = gv
    nm = B1 * m_ref[...] + (1.0 - B1) * gv
    nv = B2 * v_ref[...] + (1.0 - B2) * (gv * gv)
    nm_ref[...] = nm
    nv_ref[...] = nv
    d_ref[...] = -LR * ((nm / (1.0 - B1 ** STEP)) / (jnp.sqrt(nv / (1.0 - B2 ** STEP)) + ADAM_EPS) + WD * w_ref[...])


def _riders(updates, steps, step_of):
    in_specs, out_specs, out_shapes, operands, tiles = [], [], [], [], []
    for w, g, m, v in updates:
        R, C = w.shape
        n = max(d for d in range(1, steps + 1) if R % d == 0 and (R // d) % 8 == 0)
        spec = pl.BlockSpec((R // n, C), lambda *ids, n=n: (jnp.minimum(step_of(*ids), n - 1), 0))
        in_specs += [spec] * 4
        out_specs += [spec] * 4
        out_shapes += [SDS((R, C), F32)] * 4
        operands += [w, g, m, v]
        tiles.append(n)

    def run(step, in_refs, out_refs):
        for u, n in enumerate(tiles):
            @pl.when(step < n)
            def _(u=u):
                _adamw_tile(*in_refs[4 * u:4 * u + 4], *out_refs[4 * u:4 * u + 4])

    return in_specs, out_specs, out_shapes, operands, run


def ffn_bwd_x(dh, h_in, ab, g, wup, wdn, name, updates=()):
    T = dh.shape[0]
    TM = 256
    r_in, r_out, r_shapes, r_args, ride = _riders(updates, T // TM, lambda i: i)

    def body(dh_ref, h_ref, ab_ref, g_ref, wup_ref, wdn_ref, *rest):
        dhin_ref, dup_ref, n_ref, dg_ref = rest[len(r_in):len(r_in) + 4]
        ride(pl.program_id(0), rest[:len(r_in)], rest[len(r_in) + 4:])
        x = h_ref[...]
        g = g_ref[...]
        rstd = _rstd(x)
        xhat = x * rstd
        n_ref[...] = (xhat * g).astype(BF16)
        dh = dh_ref[...]
        dhh = (0.5 * dh).astype(BF16)
        for c0, w in FF_CHUNKS:
            dhid = _dot_nt(dhh, wdn_ref[c0:c0 + w, :])
            dup_ref[:, c0:c0 + w] = (dhid * ab_ref[:, c0:c0 + w].astype(F32)).astype(BF16)
            dup_ref[:, FF + c0:FF + c0 + w] = (dhid * ab_ref[:, FF + c0:FF + c0 + w].astype(F32)).astype(BF16)
        dn = _dot_nt(dup_ref[...], wup_ref[...])
        dhin_ref[...] = dh + _rms_bwd(dn, xhat, rstd, g)

        @pl.when(pl.program_id(0) == 0)
        def _():
            dg_ref[...] = jnp.zeros_like(dg_ref)

        dg_ref[...] += jnp.sum(dn * xhat, axis=0, keepdims=True)

    tile = lambda w: pl.BlockSpec((TM, w), lambda i: (i, 0))
    return pl.pallas_call(
        body, name=name, grid=(T // TM,),
        in_specs=[tile(D), tile(D), tile(2 * FF), _const_spec((1, D)), _const_spec((D, 2 * FF)), _const_spec((FF, D))]
        + r_in,
        out_specs=[tile(D), tile(2 * FF), tile(D), pl.BlockSpec((1, D), lambda i: (0, 0))] + r_out,
        out_shape=[SDS((T, D), F32), SDS((T, 2 * FF), BF16), SDS((T, D), BF16), SDS((1, D), F32)] + r_shapes,
        compiler_params=_cp(("arbitrary",)),
    )(dh, h_in, ab, g, wup, wdn, *r_args)


TOKEN_SPEC = pl.BlockSpec((8, 128), lambda *_: (0, 0))


def _token_operand(token):
    return ([], []) if token is None else ([TOKEN_SPEC], [token])


def matmul_tn(a, b, name, *, tm, tn, tt=1024, b_scale=None, after=None):
    T, M = a.shape
    N = b.shape[1]
    tt = min(tt, T)
    assert M % tm == 0 and N % tn == 0 and T % tt == 0
    nt = T // tt
    token_spec, token_arg = _token_operand(after)

    def body(a_ref, b_ref, *rest):
        o_ref, acc_ref = rest[-2:]
        t = pl.program_id(2)

        @pl.when(t == 0)
        def _():
            acc_ref[...] = jnp.zeros_like(acc_ref)

        bv = b_ref[...]
        if b_scale is not None:
            bv = bv * b_scale
        acc_ref[...] += _dot_tn(a_ref[...].astype(BF16), bv.astype(BF16))

        @pl.when(t == nt - 1)
        def _():
            o_ref[...] = acc_ref[...].astype(BF16)

    return pl.pallas_call(
        body, name=name, grid=(M // tm, N // tn, nt),
        in_specs=[pl.BlockSpec((tt, tm), lambda i, j, t: (t, i)), pl.BlockSpec((tt, tn), lambda i, j, t: (t, j))]
        + token_spec,
        out_specs=pl.BlockSpec((tm, tn), lambda i, j, t: (i, j)),
        out_shape=SDS((M, N), BF16),
        scratch_shapes=[pltpu.VMEM((tm, tn), F32)],
        compiler_params=_cp(("parallel", "parallel", "arbitrary")),
    )(a, b, *token_arg)


def mix_in_fwd(h1, g, win_t):
    T = h1.shape[0]
    TM = 512

    def body(h_ref, g_ref, w_ref, u_ref, q_ref, kv_ref, z_ref, gt_ref):
        x = h_ref[...]
        u = (x * _rstd(x) * g_ref[...]).astype(BF16)
        u_ref[...] = u
        for c in range(0, AW, 256):
            q_ref[:, c:c + 256] = _dot_nt(u, w_ref[c:c + 256, :]).astype(BF16)
        kv_ref[...] = _dot_nt(u, w_ref[C_KV:C_Z, :]).astype(BF16)
        for c in range(0, PW, 256):
            z_ref[:, c:c + 256] = _dot_nt(u, w_ref[C_Z + c:C_Z + c + 256, :])
        for c in range(0, 2 * D, 256):
            gt_ref[:, c:c + 256] = _dot_nt(u, w_ref[C_G + c:C_G + c + 256, :]).astype(BF16)

    tile = lambda w: pl.BlockSpec((TM, w), lambda i: (i, 0))
    return pl.pallas_call(
        body, name="mix_in_fwd", grid=(T // TM,),
        in_specs=[tile(D), _const_spec((1, D)), _const_spec((INW, D))],
        out_specs=[tile(D), tile(AW), tile(2 * KVW), tile(PW), tile(2 * D)],
        out_shape=[SDS((T, D), BF16), SDS((T, AW), BF16), SDS((T, 2 * KVW), BF16), SDS((T, PW), F32),
                   SDS((T, 2 * D), BF16)],
        compiler_params=_cp(("arbitrary",)),
    )(h1, g, win_t)


def mix_in_bwd(dq, dkv, dz, dgt, dh2, h1, g, win_t):
    T = h1.shape[0]
    TM = min(1024, T)
    SUB = 256

    def body(dq_ref, dkv_ref, dz_ref, dgt_ref, dh2_ref, h_ref, g_ref, w_ref, dh1_ref, dg_ref):
        g = g_ref[...]
        dg = None
        for r0 in range(0, TM, SUB):
            rows = slice(r0, r0 + SUB)
            du = _dot(dq_ref[rows, :], w_ref[0:AW, :])
            du += _dot(dkv_ref[rows, :], w_ref[C_KV:C_Z, :])
            du += _dot(dz_ref[rows, :], w_ref[C_Z:C_G, :])
            du += _dot(dgt_ref[rows, :], w_ref[C_G:INW, :])
            x = h_ref[rows, :]
            rstd = _rstd(x)
            xhat = x * rstd
            dh1_ref[rows, :] = dh2_ref[rows, :] + _rms_bwd(du, xhat, rstd, g)
            part = jnp.sum(du * xhat, axis=0, keepdims=True)
            dg = part if dg is None else dg + part

        @pl.when(pl.program_id(0) == 0)
        def _():
            dg_ref[...] = jnp.zeros_like(dg_ref)

        dg_ref[...] += dg

    tile = lambda w: pl.BlockSpec((TM, w), lambda i: (i, 0))
    return pl.pallas_call(
        body, name="mix_in_bwd", grid=(T // TM,),
        in_specs=[tile(AW), tile(2 * KVW), tile(PW), tile(2 * D), tile(D), tile(D), _const_spec((1, D)),
                  _const_spec((INW, D))],
        out_specs=[tile(D), pl.BlockSpec((1, D), lambda i: (0, 0))],
        out_shape=[SDS((T, D), F32), SDS((1, D), F32)],
        compiler_params=_cp(("arbitrary",)),
    )(dq, dkv, dz, dgt, dh2, h1, g, win_t)


PAIR = 2 * HD
NPAIR = GQ // 2


def _lo_lanes():
    return lax.broadcasted_iota(jnp.int32, (BLK, PAIR), 1) < HD


def _stack_heads(ref, kvh, scale=None):
    lo = _lo_lanes()
    parts = []
    for pr in range(NPAIR):
        t = ref[:, (kvh * NPAIR + pr) * PAIR:(kvh * NPAIR + pr + 1) * PAIR]
        if scale is not None:
            t = t * scale
        zero = jnp.zeros_like(t)
        parts += [jnp.where(lo, t, zero), jnp.where(lo, zero, t)]
    return jnp.concatenate(parts, axis=0)


def _kv_tiles(kvc_ref, kvp_ref, tile, kvh):
    lo = _lo_lanes()
    dup, left, right = [], [], []
    for ref in (kvp_ref, kvc_ref):
        t = ref[:, tile * PAIR:(tile + 1) * PAIR]
        r = pltpu.roll(t.astype(F32), HD, 1).astype(BF16)
        zero = jnp.zeros_like(t)
        a, b = (t, r) if kvh == 0 else (r, t)
        dup.append(jnp.where(lo, a, b))
        left.append(jnp.where(lo, a, zero))
        right.append(jnp.where(lo, zero, b))
    cat = lambda xs: jnp.concatenate(xs, axis=0)
    return cat(dup), cat(left), cat(right)


def _band_consts(first):
    row = lax.broadcasted_iota(jnp.int32, (BLK, BLK), 0)
    col = lax.broadcasted_iota(jnp.int32, (BLK, BLK), 1)
    upper = col > row
    dist = jnp.where(upper, row - col + BLK, row - col).astype(F32)
    pen = jnp.where(jnp.logical_and(upper, first), -jnp.inf, 0.0)
    return upper, dist, pen


def _split_band(upper, t):
    zero = jnp.zeros_like(t)
    return jnp.concatenate([jnp.where(upper, t, zero), jnp.where(upper, zero, t)], axis=1)


def attn_fwd(q, kv, sinks):
    T = q.shape[0]
    nb = T // BLK

    def body(sink_ref, q_ref, kvc_ref, kvp_ref, att_ref, lse_ref):
        upper, dist, pen = _band_consts(pl.program_id(0) == 0)
        scores, values = [], []
        for kvh in range(NKV):
            kdup, _, _ = _kv_tiles(kvc_ref, kvp_ref, 0, kvh)
            values.append(_kv_tiles(kvc_ref, kvp_ref, 1, kvh)[1:])
            scores.append(_dot_nt(_stack_heads(q_ref, kvh, SCALE), kdup))
        for kvh in range(NKV):
            s_all = scores[kvh]
            vleft, vright = values[kvh]
            for pr in range(NPAIR):
                outs, inv = [], []
                for side, vpad in ((0, vleft), (1, vright)):
                    g = 2 * pr + side
                    hq = kvh * GQ + g
                    sink = sink_ref[0, hq]
                    rows = slice(g * BLK, (g + 1) * BLK)
                    s = jnp.where(upper, s_all[rows, 0:BLK], s_all[rows, BLK:2 * BLK]) - SLOPES[hq] * dist + pen
                    m = jnp.maximum(jnp.max(s, axis=-1, keepdims=True), sink)
                    p = jnp.exp(s - m)
                    l = jnp.sum(p, axis=-1, keepdims=True) + jnp.exp(sink - m)
                    lse_ref[:, hq:hq + 1] = m + jnp.log(l)
                    outs.append(_dot(_split_band(upper, p.astype(BF16)), vpad))
                    inv.append(1.0 / l)
                col0 = (kvh * NPAIR + pr) * PAIR
                att_ref[:, col0:col0 + PAIR] = ((outs[0] + outs[1]) * jnp.where(_lo_lanes(), inv[0], inv[1])).astype(BF16)

    return pl.pallas_call(
        body, name="attn_fwd", grid=(nb,),
        in_specs=[pl.BlockSpec(memory_space=pltpu.SMEM),
                  pl.BlockSpec((BLK, AW), lambda i: (i, 0)),
                  pl.BlockSpec((BLK, 2 * KVW), lambda i: (i, 0)),
                  pl.BlockSpec((BLK, 2 * KVW), lambda i: (jnp.maximum(i - 1, 0), 0))],
        out_specs=[pl.BlockSpec((BLK, AW), lambda i: (i, 0)), pl.BlockSpec((BLK, NQ), lambda i: (i, 0))],
        out_shape=[SDS((T, AW), BF16), SDS((T, NQ), F32)],
        compiler_params=_cp(("arbitrary",)),
    )(sinks, q, kv, kv)


def attn_bwd(q, kv, datt, att, lse, sinks):
    T = q.shape[0]
    nb = T // BLK

    def body(sink_ref, q_ref, kvc_ref, kvp_ref, do_ref, out_ref, lse_ref, dq_ref, dkv_ref, dsink_ref, carry_ref):
        i = pl.program_id(0)

        @pl.when(i == 0)
        def _():
            dsink_ref[...] = jnp.zeros_like(dsink_ref)
            carry_ref[...] = jnp.zeros_like(carry_ref)

        @pl.when(i < nb)
        def _():
            upper, dist, pen = _band_consts(i == 0)
            lo = _lo_lanes()
            dk_dup, dv_dup = [], []
            staged = []
            for kvh in range(NKV):
                kdup, kleft, kright = _kv_tiles(kvc_ref, kvp_ref, 0, kvh)
                vdup, _, _ = _kv_tiles(kvc_ref, kvp_ref, 1, kvh)
                qs = _stack_heads(q_ref, kvh, SCALE)
                dos = _stack_heads(do_ref, kvh)
                staged.append((kleft, kright, qs, dos, _dot_nt(qs, kdup), _dot_nt(dos, vdup)))
            deltas = []
            for pair in range(NQ // 2):
                cols = slice(pair * PAIR, (pair + 1) * PAIR)
                t = do_ref[:, cols].astype(F32) * out_ref[:, cols].astype(F32)
                deltas += [jnp.sum(jnp.where(lo, t, 0.0), axis=-1, keepdims=True),
                           jnp.sum(jnp.where(lo, 0.0, t), axis=-1, keepdims=True)]
            for kvh in range(NKV):
                kleft, kright, qs, dos, s_all, dp_all = staged[kvh]
                ds_parts, p_parts = [], []
                for pr in range(NPAIR):
                    dq = None
                    for side, kpad in ((0, kleft), (1, kright)):
                        g = 2 * pr + side
                        hq = kvh * GQ + g
                        lse_h = lse_ref[:, hq:hq + 1]
                        rows = slice(g * BLK, (g + 1) * BLK)
                        s = jnp.where(upper, s_all[rows, 0:BLK], s_all[rows, BLK:2 * BLK]) - SLOPES[hq] * dist + pen
                        p = jnp.exp(s - lse_h)
                        dp = jnp.where(upper, dp_all[rows, 0:BLK], dp_all[rows, BLK:2 * BLK])
                        delta = deltas[hq]
                        dsink_ref[:, hq:hq + 1] += -jnp.sum(jnp.exp(sink_ref[0, hq] - lse_h) * delta, axis=0,
                                                            keepdims=True)
                        ds = _split_band(upper, (p * (dp - delta)).astype(BF16))
                        ds_parts.append(ds)
                        p_parts.append(_split_band(upper, p.astype(BF16)))
                        d = _dot(ds, kpad)
                        dq = d if dq is None else dq + d
                    col0 = (kvh * NPAIR + pr) * PAIR
                    dq_ref[:, col0:col0 + PAIR] = (dq * SCALE).astype(BF16)
                dkw = _dot_tn(qs, jnp.concatenate(ds_parts, axis=0)).T
                dvw = _dot_tn(dos, jnp.concatenate(p_parts, axis=0)).T
                dk_dup.append(dkw + pltpu.roll(dkw, HD, 1))
                dv_dup.append(dvw + pltpu.roll(dvw, HD, 1))
            dk = jnp.where(jnp.concatenate([lo, lo], axis=0), dk_dup[0], dk_dup[1])
            dv = jnp.where(jnp.concatenate([lo, lo], axis=0), dv_dup[0], dv_dup[1])
            dkv_ref[:, 0:PAIR] = (carry_ref[:, 0:PAIR] + dk[0:BLK]).astype(BF16)
            dkv_ref[:, PAIR:2 * PAIR] = (carry_ref[:, PAIR:2 * PAIR] + dv[0:BLK]).astype(BF16)
            carry_ref[:, 0:PAIR] = dk[BLK:2 * BLK]
            carry_ref[:, PAIR:2 * PAIR] = dv[BLK:2 * BLK]

        @pl.when(i == nb)
        def _():
            dkv_ref[...] = carry_ref[...].astype(BF16)

    cur = lambda i: (jnp.minimum(i, nb - 1), 0)
    prev = lambda i: (jnp.maximum(jnp.minimum(i, nb - 1) - 1, 0), 0)
    return pl.pallas_call(
        body, name="attn_bwd", grid=(nb + 1,),
        in_specs=[pl.BlockSpec(memory_space=pltpu.SMEM),
                  pl.BlockSpec((BLK, AW), cur), pl.BlockSpec((BLK, 2 * KVW), cur), pl.BlockSpec((BLK, 2 * KVW), prev),
                  pl.BlockSpec((BLK, AW), cur), pl.BlockSpec((BLK, AW), cur), pl.BlockSpec((BLK, NQ), cur)],
        out_specs=[pl.BlockSpec((BLK, AW), cur),
                   pl.BlockSpec((BLK, 2 * KVW), lambda i: (jnp.maximum(i - 1, 0), 0)),
                   pl.BlockSpec((1, NQ), lambda i: (0, 0))],
        out_shape=[SDS((T, AW), BF16), SDS((T, 2 * KVW), BF16), SDS((1, NQ), F32)],
        scratch_shapes=[pltpu.VMEM((BLK, 2 * KVW), F32)],
        compiler_params=_cp(("arbitrary",)),
    )(sinks, q, kv, kv, datt, att, lse)


def _inv_counts(t0, rows):
    t = (t0 + lax.broadcasted_iota(jnp.int32, (rows, 1), 0) + 1).astype(F32)
    return [1.0 / jnp.minimum(t, float(w)) for w in POOL_WINDOWS]


def pool_fwd(z, wmix, scale):
    T = z.shape[0]
    TM = 512
    L = TM + HALO

    def body(z_ref, halo_ref, wmix_ref, scale_ref, pooled_ref, mixs_ref):
        i = pl.program_id(0)
        halo = jnp.where(i > 0, halo_ref[...], 0.0)
        zt = z_ref[...]
        e = jnp.concatenate([halo, zt], axis=0)
        sums = []
        s = e
        for k in (1, 2, 4, 8):
            s = s + pltpu.roll(s, k, 0)
            sums.append(s)
        inv = _inv_counts(i * TM, TM)
        for gi in range(len(POOL_WINDOWS)):
            cols = slice(gi * PG, (gi + 1) * PG)
            pooled = (sums[gi][HALO:, cols] * inv[gi] - zt[:, cols]).astype(BF16)
            pooled_ref[:, cols] = pooled
            mixs_ref[:, cols] = (_dot(pooled, wmix_ref[gi]) * scale_ref[:, cols]).astype(BF16)

    return pl.pallas_call(
        body, name="pool_fwd", grid=(T // TM,),
        in_specs=[pl.BlockSpec((TM, PW), lambda i: (i, 0)),
                  pl.BlockSpec((HALO, PW), lambda i: (jnp.maximum(i * (TM // HALO) - 1, 0), 0)),
                  _const_spec((len(POOL_WINDOWS), PG, PG)), _const_spec((1, PW))],
        out_specs=[pl.BlockSpec((TM, PW), lambda i: (i, 0)), pl.BlockSpec((TM, PW), lambda i: (i, 0))],
        out_shape=[SDS((T, PW), BF16), SDS((T, PW), BF16)],
        compiler_params=_cp(("arbitrary",)),
    )(z, z, wmix, scale)


def pool_bwd(dmixs, pooled, wmix, scale):
    T = dmixs.shape[0]
    TM = 512
    L = TM + HALO
    nt = T // TM

    def body(dm_ref, halo_ref, pooled_ref, wmix_ref, scale_ref, dz_ref, dwmix_ref, dscale_ref):
        i = pl.program_id(0)

        @pl.when(i == 0)
        def _():
            dwmix_ref[...] = jnp.zeros_like(dwmix_ref)
            dscale_ref[...] = jnp.zeros_like(dscale_ref)

        halo = jnp.where(i < nt - 1, halo_ref[...], 0.0)
        dm = dm_ref[...]
        e = jnp.concatenate([dm, halo], axis=0)
        inv = _inv_counts(i * TM, L)
        for gi in range(len(POOL_WINDOWS)):
            cols = slice(gi * PG, (gi + 1) * PG)
            w = wmix_ref[gi]
            dmixed = (e[:, cols] * scale_ref[:, cols]).astype(BF16)
            dpooled = _dot_nt(dmixed, w)
            pooled = pooled_ref[:, cols]
            mixed = _dot(pooled, w)
            dscale_ref[:, cols] += jnp.sum(dm[:, cols] * mixed, axis=0, keepdims=True)
            dwmix_ref[gi] += _dot_tn(pooled, dmixed[:TM])
            s = dpooled * inv[gi]
            k = 1
            while k < POOL_WINDOWS[gi]:
                s = s + pltpu.roll(s, L - k, 0)
                k *= 2
            dz_ref[:, cols] = (s[:TM] - dpooled[:TM]).astype(BF16)

    return pl.pallas_call(
        body, name="pool_bwd", grid=(nt,),
        in_specs=[pl.BlockSpec((TM, PW), lambda i: (i, 0)),
                  pl.BlockSpec((HALO, PW), lambda i: (jnp.minimum((i + 1) * (TM // HALO), T // HALO - 1), 0)),
                  pl.BlockSpec((TM, PW), lambda i: (i, 0)),
                  _const_spec((len(POOL_WINDOWS), PG, PG)), _const_spec((1, PW))],
        out_specs=[pl.BlockSpec((TM, PW), lambda i: (i, 0)),
                   pl.BlockSpec((len(POOL_WINDOWS), PG, PG), lambda i: (0, 0, 0)),
                   pl.BlockSpec((1, PW), lambda i: (0, 0))],
        out_shape=[SDS((T, PW), BF16), SDS((len(POOL_WINDOWS), PG, PG), F32), SDS((1, PW), F32)],
        compiler_params=_cp(("arbitrary",)),
    )(dmixs, dmixs, pooled, wmix, scale)


def merge_fwd(att, mixs, gt, h1, wattn, wpool, wout):
    T = h1.shape[0]
    TM = 512

    def body(att_ref, mixs_ref, gt_ref, h_ref, wa_ref, wp_ref, wo_ref, h2_ref, mg_ref, gf_ref):
        a = _dot(att_ref[...], wa_ref[...])
        p = _dot(mixs_ref[...], wp_ref[...])
        sa = jax.nn.sigmoid(gt_ref[:, 0:D].astype(F32))
        sp = jax.nn.sigmoid(gt_ref[:, D:2 * D].astype(F32))
        gf_ref[:, 0:D] = (a * sa * (1.0 - sa)).astype(BF16)
        gf_ref[:, D:2 * D] = (p * sp * (1.0 - sp)).astype(BF16)
        mg = (sa * a + sp * p).astype(BF16)
        mg_ref[...] = mg
        h2_ref[...] = h_ref[...] + _dot(mg, wo_ref[...])

    tile = lambda w: pl.BlockSpec((TM, w), lambda i: (i, 0))
    return pl.pallas_call(
        body, name="merge_fwd", grid=(T // TM,),
        in_specs=[tile(AW), tile(PW), tile(2 * D), tile(D), _const_spec((AW, D)), _const_spec((PW, D)),
                  _const_spec((D, D))],
        out_specs=[tile(D), tile(D), tile(2 * D)],
        out_shape=[SDS((T, D), F32), SDS((T, D), BF16), SDS((T, 2 * D), BF16)],
        compiler_params=_cp(("arbitrary",)),
    )(att, mixs, gt, h1, wattn, wpool, wout)


def merge_bwd(dh2, gt, gf, wattn, wpool, wout, after=None):
    T = dh2.shape[0]
    TM = 512
    token_spec, token_arg = _token_operand(after)

    def body(dh2_ref, gt_ref, gf_ref, wa_ref, wp_ref, wo_ref, *rest):
        datt_ref, dmixs_ref, dgt_ref, da_ref, dp_ref = rest[-5:]
        dm = _dot_nt(dh2_ref[...].astype(BF16), wo_ref[...])
        da = (dm * jax.nn.sigmoid(gt_ref[:, 0:D].astype(F32))).astype(BF16)
        dp = (dm * jax.nn.sigmoid(gt_ref[:, D:2 * D].astype(F32))).astype(BF16)
        da_ref[...] = da
        dp_ref[...] = dp
        dgt_ref[:, 0:D] = (dm * gf_ref[:, 0:D].astype(F32)).astype(BF16)
        dgt_ref[:, D:2 * D] = (dm * gf_ref[:, D:2 * D].astype(F32)).astype(BF16)
        datt_ref[...] = _dot_nt(da, wa_ref[...]).astype(BF16)
        dmixs_ref[...] = _dot_nt(dp, wp_ref[...])

    tile = lambda w: pl.BlockSpec((TM, w), lambda i: (i, 0))
    return pl.pallas_call(
        body, name="merge_bwd", grid=(T // TM,),
        in_specs=[tile(D), tile(2 * D), tile(2 * D), _const_spec((AW, D)), _const_spec((PW, D)),
                  _const_spec((D, D))] + token_spec,
        out_specs=[tile(AW), tile(PW), tile(2 * D), tile(D), tile(D)],
        out_shape=[SDS((T, AW), BF16), SDS((T, PW), F32), SDS((T, 2 * D), BF16), SDS((T, D), BF16),
                   SDS((T, D), BF16)],
        compiler_params=_cp(("arbitrary",)),
    )(dh2, gt, gf, wattn, wpool, wout, *token_arg)


def adamw(w, g, m, v, name):
    R, C = w.shape
    tile_bytes = 2 * 1024 * 1024
    tr = R
    if R * C * 4 > tile_bytes:
        tr = next(cand for cand in (512, 256, 128, 64, 32, 16, 8) if R % cand == 0 and cand * C * 4 <= tile_bytes)

    def body(*refs):
        _adamw_tile(*refs)

    spec = pl.BlockSpec((tr, C), lambda i: (i, 0))
    return pl.pallas_call(
        body, name=name, grid=(R // tr,),
        in_specs=[spec] * 4, out_specs=[spec] * 4, out_shape=[SDS((R, C), F32)] * 4,
        compiler_params=_cp(("parallel",)),
    )(w, g, m, v)


GROUP_FFN1, GROUP_MIX, GROUP_FFN2 = 0, 1, 2


def _behind(small, token):
    return small if token is None else small + token[0:1, 0:1]


def local_fwd_bwd(x, target, S, comm):
    W = comm.weights(GROUP_FFN1, None)
    h1, ab1, hid1 = ffn_fwd(x, _behind(S["ffn1_norm"], comm.started()), W["ffn1_w_up"], W["ffn1_w_down"], "ffn1_fwd")
    W = comm.weights(GROUP_MIX, h1)
    u, q, kv, z, gt = mix_in_fwd(h1, _behind(S["mix_norm"], comm.started()), W["w_in"])
    att, lse = attn_fwd(q, kv, S["sinks"])
    pooled, mixs = pool_fwd(z, S["pool_w_mix"], _behind(S["pool_scale"], comm.prefetch(GROUP_FFN2, att)))
    h2, merged, gf = merge_fwd(att, mixs, gt, h1, W["w_attn_up"], W["w_pool_up"], W["w_out"])
    W = comm.weights(GROUP_FFN2, h2)
    dh3, ab2, hid2, loss, g_final = ffn_fwd(h2, S["ffn2_norm"], W["ffn2_w_up"], W["ffn2_w_down"], "ffn2_fwd",
                                            head=(target, S["final_norm"]))

    G = {"final_norm": g_final}
    dh2, dup2, n2, G["ffn2_norm"] = ffn_bwd_x(dh3, h2, ab2, S["ffn2_norm"], W["ffn2_w_up"], W["ffn2_w_down"],
                                              "ffn2_bwd_x")
    ffn2 = ("ffn2_w_down", "ffn2_w_up")
    token = comm.grads({"ffn2_w_down": matmul_tn(hid2, dh3, "ffn2_dw_down", tm=1408, tn=512, tt=4096, b_scale=0.5),
                        "ffn2_w_up": matmul_tn(n2, dup2, "ffn2_dw_up", tm=D, tn=512, tt=4096)})

    W = comm.weights(GROUP_MIX, None)
    datt, dmixs, dgt, da, dp = merge_bwd(dh2, gt, gf, W["w_attn_up"], W["w_pool_up"], W["w_out"], after=token)
    token = comm.advance(ffn2, datt)
    g_mix = {"w_out": matmul_tn(merged, dh2, "dw_out", tm=D, tn=D, after=token),
             "w_attn_up": matmul_tn(att, da, "dw_attn_up", tm=AW, tn=D),
             "w_pool_up": matmul_tn(mixs, dp, "dw_pool_up", tm=PW, tn=D)}
    dz, G["pool_w_mix"], G["pool_scale"] = pool_bwd(dmixs, pooled, S["pool_w_mix"], _behind(S["pool_scale"], token))
    dq, dkv, G["sinks"] = attn_bwd(q, kv, datt, att, lse, _behind(S["sinks"], token))
    dh1, G["mix_norm"] = mix_in_bwd(dq, dkv, dz, dgt, dh2, h1, S["mix_norm"], W["w_in"])
    g_mix["w_in"] = jnp.concatenate([
        matmul_tn(dq, u, "dw_in_q", tm=AW, tn=D),
        matmul_tn(dkv, u, "dw_in_kv", tm=2 * KVW, tn=D),
        matmul_tn(dz, u, "dw_in_z", tm=PW, tn=D),
        matmul_tn(dgt, u, "dw_in_g", tm=D, tn=512, tt=4096),
    ], axis=0)
    mix = tuple(g_mix)
    token = comm.grads(g_mix)

    W = comm.weights(GROUP_FFN1, None)
    g_dn1 = matmul_tn(hid1, dh1, "ffn1_dw_down", tm=1408, tn=512, tt=4096, b_scale=0.5, after=token)
    token = comm.advance(mix, g_dn1)
    token2 = comm.grads({"ffn1_w_down": g_dn1})
    g1 = _behind(_behind(S["ffn1_norm"], token), token2)
    dx, dup1, n1, G["ffn1_norm"], *outs = ffn_bwd_x(dh1, x, ab1, g1, W["ffn1_w_up"], W["ffn1_w_down"], "ffn1_bwd_x",
                                                   updates=comm.reduce(ffn2, token2))
    comm.updated(ffn2, outs)
    token = comm.advance(("ffn1_w_down",), dx)
    token2 = comm.small_start(G, loss, dx)
    g_up1 = matmul_tn(n1, dup1, "ffn1_dw_up", tm=D, tn=512, tt=4096,
                      after=None if token is None else token + token2)
    token = comm.grads({"ffn1_w_up": g_up1})
    done = comm.finish(("ffn1_w_down",), token)
    token = comm.advance(("ffn1_w_up",), done)
    done = comm.finish(mix, token)
    done = comm.small_finish(done)
    comm.finish(("ffn1_w_up",), done)
    return loss, dx


HBM_SPEC = pl.BlockSpec(memory_space=pltpu.HBM)


def _me():
    return lax.axis_index("x"), lax.axis_index("y"), lax.axis_index("c")


def _peer_chip(x, y, k):
    return x ^ (k >> 1), y ^ (k & 1)


def _piece_half(ref, rowlike, j, h):
    if rowlike:
        return ref.at[j, h]
    ns = ref.shape[-1] // N_CHIPS
    return ref.at[h, :, pl.ds(pl.multiple_of(j * ns, 128), ns)]


def _piece(ref, rowlike, j):
    if rowlike:
        return ref.at[j]
    ns = ref.shape[-1] // N_CHIPS
    return ref.at[:, :, pl.ds(pl.multiple_of(j * ns, 128), ns)]


def _full_shape(shard_view, rowlike):
    _, kh, ns = shard_view.shape
    return (N_CHIPS, 2, kh, ns) if rowlike else (2, kh, N_CHIPS * ns)


def _remote(src, dst, send_sem, recv_sem, dev):
    return pltpu.make_async_remote_copy(src, dst, send_sem, recv_sem, device_id=dev, device_id_type=MESH)


SEM_SPEC = pl.BlockSpec(memory_space=pltpu.SEMAPHORE)
SPLIT_PARAMS = pltpu.CompilerParams(has_side_effects=pltpu.SideEffectType.DATAFLOW_SIDE_EFFECTING)


def _gather_plan(rowlikes):
    def plan(s_refs, f_refs, a, k, x, y, c):
        px, py = _peer_chip(x, y, k)
        return (s_refs[a].at[c], _piece_half(f_refs[a], rowlikes[a], 2 * x + y, c),
                _piece_half(f_refs[a], rowlikes[a], 2 * px + py, c), (px, py, c))
    return plan


def _all_to_all_plan(rowlikes):
    def plan(q_refs, r_refs, a, k, x, y, c):
        px, py = _peer_chip(x, y, k)
        if rowlikes[a]:
            src = q_refs[a].at[2 * px + py]
        else:
            ns = q_refs[a].shape[-1] // N_CHIPS
            src = q_refs[a].at[:, pl.ds(pl.multiple_of((2 * px + py) * ns, 128), ns)]
        return src, r_refs[a].at[k - 1], r_refs[a].at[k - 1], (px, py, c)
    return plan


def _swap_plan(rowlikes):
    def plan(g_refs, got_refs, a, k, x, y, c):
        src = g_refs[a].at[:, 1 - c] if rowlikes[a] else g_refs[a].at[1 - c]
        return src, got_refs[a], got_refs[a], (x, y, 1 - c)
    return plan


def _everyone_plan(s_refs, slot_refs, a, k, x, y, c):
    px, py, pc = x ^ (k >> 2), y ^ ((k >> 1) & 1), c ^ (k & 1)
    return s_refs[a], slot_refs[a].at[4 * x + 2 * y + c], slot_refs[a].at[4 * px + 2 * py + pc], (px, py, pc)


def _forward_plan(rowlikes):
    def plan(s_refs, f_refs, a, k, x, y, c):
        sib = (x, y, 1 - c)
        if k == 0:
            own = _piece(f_refs[a], rowlikes[a], 2 * x + y)
            return s_refs[a], own, own, sib
        px, py = _peer_chip(x, y, k)
        mine = _piece_half(f_refs[a], rowlikes[a], 2 * px + py, c)
        return mine, mine, _piece_half(f_refs[a], rowlikes[a], 2 * px + py, 1 - c), sib
    return plan


CHIPS, SIBLING, EVERYONE, FORWARDS = (1, 2, 3), (1,), tuple(range(1, 8)), (0, 1, 2, 3)


def _as_list(after):
    return [] if after is None else list(after) if isinstance(after, (list, tuple)) else [after]


def exchange_start(srcs, land_shapes, plan, after, name, peers=CHIPS):
    n = len(srcs)
    lands = [l if hasattr(l, "dtype") else lax.empty(l, s.dtype) for l, s in zip(land_shapes, srcs)]

    behind = _as_list(after)

    def body(*refs):
        s_refs, l_refs = refs[:n], refs[n:2 * n]
        send_sems, recv_sems = refs[2 * n + len(behind)], refs[2 * n + len(behind) + 1]
        token = refs[-1]
        x, y, c = _me()
        for a in range(n):
            for i, k in enumerate(peers):
                src, dst, _, peer = plan(s_refs, l_refs, a, k, x, y, c)
                sem = len(peers) * a + i
                _remote(src, dst, send_sems.at[sem], recv_sems.at[sem], peer).start()
        token[...] = jnp.zeros_like(token)

    n_sems = len(peers) * n
    outs = pl.pallas_call(
        body, name=name, in_specs=[HBM_SPEC] * (2 * n) + [pl.BlockSpec(memory_space=pl.ANY)] * len(behind),
        out_specs=[SEM_SPEC, SEM_SPEC] + [HBM_SPEC] * (2 * n) + [pl.BlockSpec(memory_space=pltpu.VMEM)],
        out_shape=[pltpu.SemaphoreType.DMA((n_sems,)), pltpu.SemaphoreType.DMA((n_sems,))]
        + [pltpu.HBM(a.shape, a.dtype) for a in (*srcs, *lands)] + [SDS((8, 128), F32)],
        input_output_aliases={i: 2 + i for i in range(2 * n)},
        compiler_params=SPLIT_PARAMS,
    )(*[pltpu.with_memory_space_constraint(a, pltpu.HBM) for a in (*srcs, *lands)], *behind)
    return {"sems": outs[:2], "srcs": outs[2:2 + n], "lands": outs[2 + n:2 + 2 * n], "token": outs[-1]}


def exchange_wait(state, plan, after, name, peers=CHIPS):
    n = len(state["srcs"])
    behind = _as_list(after)

    def body(*refs):
        s_refs, l_refs = refs[:n], refs[n:2 * n]
        send_sems, recv_sems = refs[2 * n], refs[2 * n + 1]
        x, y, c = _me()
        for a in range(n):
            for i, k in enumerate(peers):
                src, _, landing, peer = plan(s_refs, l_refs, a, k, x, y, c)
                sem = len(peers) * a + i
                cp = _remote(src, landing, send_sems.at[sem], recv_sems.at[sem], peer)
                cp.wait_send()
                cp.wait_recv()

    bufs = (*state["srcs"], *state["lands"])
    outs = pl.pallas_call(
        body, name=name,
        in_specs=[HBM_SPEC] * (2 * n) + [SEM_SPEC, SEM_SPEC] + [pl.BlockSpec(memory_space=pl.ANY)] * len(behind),
        out_specs=[HBM_SPEC] * (2 * n),
        out_shape=[pltpu.HBM(a.shape, a.dtype) for a in bufs],
        input_output_aliases={i: i for i in range(2 * n)},
        compiler_params=SPLIT_PARAMS,
    )(*bufs, *state["sems"], *behind)
    return outs[:n], outs[n:]


def gather_finish(shards, fulls, rowlikes, name):
    n = len(shards)

    def body(*refs):
        s_refs, f_refs = refs[:n], refs[2 * n:3 * n]
        send_sems, recv_sems = refs[3 * n:]
        x, y, c = _me()
        chip = 2 * x + y
        sib = (x, y, 1 - c)
        sends = []
        for a in range(n):
            own = _piece(f_refs[a], rowlikes[a], chip)
            cp = _remote(s_refs[a], own, send_sems.at[a, 0], recv_sems.at[a, 0], sib)
            cp.start()
            sends.append(cp)
            for k in (1, 2, 3):
                px, py = _peer_chip(x, y, k)
                slot = _piece_half(f_refs[a], rowlikes[a], 2 * px + py, c)
                cp = _remote(slot, slot, send_sems.at[a, k], recv_sems.at[a, k], sib)
                cp.start()
                sends.append(cp)
        for a in range(n):
            own = _piece(f_refs[a], rowlikes[a], chip)
            _remote(own, own, send_sems.at[a, 0], recv_sems.at[a, 0], sib).wait_recv()
            for k in (1, 2, 3):
                px, py = _peer_chip(x, y, k)
                slot = _piece_half(f_refs[a], rowlikes[a], 2 * px + py, 1 - c)
                _remote(slot, slot, send_sems.at[a, k], recv_sems.at[a, k], sib).wait_recv()
        for cp in sends:
            cp.wait_send()

    return pl.pallas_call(
        body, name=name, in_specs=[HBM_SPEC] * (2 * n), out_specs=[HBM_SPEC] * n,
        out_shape=[SDS(f.shape, f.dtype) for f in fulls],
        input_output_aliases={n + a: a for a in range(n)},
        scratch_shapes=[pltpu.SemaphoreType.DMA((n, 4)), pltpu.SemaphoreType.DMA((n, 4))],
    )(*shards, *fulls)


def _half_buffer_shape(gview, rowlike):
    return (N_CHIPS,) + gview.shape[2:] if rowlike else gview.shape[1:]


def _row_tiles(kh, ns):
    return 1 if kh * ns <= 256 * 1024 else 2


def _piece_specs(rowlike, kh, ns, piece, half):
    tr = kh // _row_tiles(kh, ns)
    if rowlike:
        return (pl.BlockSpec((None, None, tr, ns), lambda *g: (piece(*g), half(*g), g[-2], 0)),
                pl.BlockSpec((None, tr, ns), lambda *g: (piece(*g), g[-2], 0)))
    return (pl.BlockSpec((None, tr, ns), lambda *g: (half(*g), g[-2], piece(*g))),
            pl.BlockSpec((tr, ns), lambda *g: (g[-2], piece(*g))))


def add_halves(gview, got, rowlike, place, name):
    kh, ns = (gview.shape[2], gview.shape[3]) if rowlike else (gview.shape[1], gview.shape[2] // N_CHIPS)

    def body(place_ref, g_ref, got_ref, o_ref):
        o_ref[...] = (g_ref[...].astype(F32) + got_ref[...].astype(F32)).astype(BF16)

    g_spec, h_spec = _piece_specs(rowlike, kh, ns, lambda j, r, p: j, lambda j, r, p: p[0])
    return pl.pallas_call(
        body, name=name,
        grid_spec=pltpu.PrefetchScalarGridSpec(num_scalar_prefetch=1, grid=(N_CHIPS, _row_tiles(kh, ns)),
                                               in_specs=[g_spec, h_spec], out_specs=h_spec),
        out_shape=SDS(got.shape, BF16),
        compiler_params=_cp(("parallel", "parallel")),
    )(place, gview, got)


def _slot_shape(q, rowlike):
    return (3,) + (q.shape[1:] if rowlike else (q.shape[0], q.shape[1] // N_CHIPS))


def sum_pieces(q, recv, rowlike, place, name):
    kh, ns = recv.shape[1], recv.shape[2]
    tr = kh // _row_tiles(kh, ns)

    def body(place_ref, q_ref, r_ref, o_ref):
        acc = q_ref[...].astype(F32)
        for k in range(3):
            acc = acc + r_ref[k].astype(F32)
        o_ref[...] = acc

    _, h_spec = _piece_specs(rowlike, kh, ns, lambda z, r, p: p[1], lambda z, r, p: p[0])
    return pl.pallas_call(
        body, name=name,
        grid_spec=pltpu.PrefetchScalarGridSpec(
            num_scalar_prefetch=1, grid=(1, _row_tiles(kh, ns)),
            in_specs=[h_spec, pl.BlockSpec((3, tr, ns), lambda z, r, p: (0, r, 0))],
            out_specs=pl.BlockSpec((None, tr, ns), lambda z, r, p: (p[0], r, 0))),
        out_shape=SDS((2, kh, ns), F32),
        compiler_params=_cp(("parallel", "parallel")),
    )(place, q, recv)


def join_halves(halves, name):
    n = len(halves)

    def body(*refs):
        o_refs = refs[n:2 * n]
        send_sems, recv_sems = refs[2 * n:]
        x, y, c = _me()
        sib = (x, y, 1 - c)
        sends = []
        for a in range(n):
            cp = _remote(o_refs[a].at[c], o_refs[a].at[c], send_sems.at[a], recv_sems.at[a], sib)
            cp.start()
            sends.append(cp)
        for a in range(n):
            got = o_refs[a].at[1 - c]
            _remote(got, got, send_sems.at[a], recv_sems.at[a], sib).wait_recv()
        for cp in sends:
            cp.wait_send()

    return pl.pallas_call(
        body, name=name, in_specs=[HBM_SPEC] * n, out_specs=[HBM_SPEC] * n,
        out_shape=[SDS(h.shape, h.dtype) for h in halves],
        input_output_aliases={a: a for a in range(n)},
        scratch_shapes=[pltpu.SemaphoreType.DMA((n,)), pltpu.SemaphoreType.DMA((n,))],
    )(*halves)


N_DEV = 8


def sum_devices(s, slots, me):
    R, Wd = s.shape

    def body(me_ref, s_ref, slots_ref, out_ref):
        acc = None
        for d in range(N_DEV):
            mine = me_ref[0] == d
            term = jnp.where(mine, s_ref[...], slots_ref[jnp.where(mine, d ^ 1, d)])
            acc = term if acc is None else acc + term
        out_ref[...] = acc

    vmem = pl.BlockSpec(memory_space=pltpu.VMEM)
    return pl.pallas_call(
        body, name="sum_devices", in_specs=[pl.BlockSpec(memory_space=pltpu.SMEM), vmem, vmem], out_specs=vmem,
        out_shape=SDS((R, Wd), F32),
    )(me, s, slots)


TRANSPOSED = ("w_in",)
BIG = {"ffn1_w_up": (D, 2 * FF, "col"), "ffn1_w_down": (FF, D, "row"), "w_in": (INW, D, "row"),
       "w_attn_up": (AW, D, "row"), "w_pool_up": (PW, D, "col"), "w_out": (D, D, "row"),
       "ffn2_w_up": (D, 2 * FF, "col"), "ffn2_w_down": (FF, D, "row")}
GROUPS = (("ffn1_w_up", "ffn1_w_down"), ("w_in", "w_attn_up", "w_pool_up", "w_out"), ("ffn2_w_up", "ffn2_w_down"))
SMALL = ("ffn1_norm", "mix_norm", "ffn2_norm", "final_norm", "pool_scale", "sinks", "pool_w_mix")
SMALL_W = 128


def _rowlike(name):
    return BIG[name][2] == "row"


def _half_dims(name):
    k, n, kind = BIG[name]
    return (k // N_CHIPS // 2, n) if kind == "row" else (k // 2, n // N_CHIPS)


def shard_view(name, shard):
    return shard.reshape((2,) + _half_dims(name))


def full_from_view(name, fv):
    k, n, _ = BIG[name]
    return fv.reshape(k, n)


def grad_view(name, g):
    kh, ns = _half_dims(name)
    return g.reshape(_full_shape(jax.ShapeDtypeStruct((2, kh, ns), g.dtype), _rowlike(name)))


def pack_small(d):
    parts = []
    for name in SMALL:
        a = d[name].reshape(-1)
        pad = (-a.shape[0]) % SMALL_W
        parts.append(jnp.pad(a, (0, pad)).reshape(-1, SMALL_W))
    a = jnp.concatenate(parts, axis=0)
    return jnp.pad(a, ((0, (-a.shape[0]) % 8), (0, 0)))


def unpack_small(a, like):
    out, r0 = {}, 0
    for name in SMALL:
        size = int(np.prod(like[name].shape))
        rows = -(-size // SMALL_W)
        out[name] = a[r0:r0 + rows].reshape(-1)[:size].reshape(like[name].shape)
        r0 += rows
    return out


WEIGHTS = ("ffn1_norm", "ffn1_w_up", "ffn1_w_down", "mix_norm", "w_in", "sinks", "w_attn_up", "pool_w_mix",
           "pool_scale", "w_pool_up", "w_out", "ffn2_norm", "ffn2_w_up", "ffn2_w_down", "final_norm")


def kernel(x, ffn1_norm, ffn1_w_up, ffn1_w_down, mix_norm, w_in, sinks, w_attn_up, pool_w_mix, pool_scale, w_pool_up, w_out, ffn2_norm, ffn2_w_up, ffn2_w_down, final_norm, loss_target, m_ffn1_norm, m_ffn1_w_up, m_ffn1_w_down, m_mix_norm, m_w_in, m_sinks, m_w_attn_up, m_pool_w_mix, m_pool_scale, m_w_pool_up, m_w_out, m_ffn2_norm, m_ffn2_w_up, m_ffn2_w_down, m_final_norm, v_ffn1_norm, v_ffn1_w_up, v_ffn1_w_down, v_mix_norm, v_w_in, v_sinks, v_w_attn_up, v_pool_w_mix, v_pool_scale, v_w_pool_up, v_w_out, v_ffn2_norm, v_ffn2_w_up, v_ffn2_w_down, v_final_norm):
    given = dict(locals())
    w = {n: given[n] for n in WEIGHTS}
    m = {n: given["m_" + n] for n in WEIGHTS}
    v = {n: given["v_" + n] for n in WEIGHTS}
    cx, cy, cc = _me()
    place = jnp.stack([cc, 2 * cx + cy]).astype(jnp.int32)

    def local2d(d, n):
        return d[n][0].T if n in TRANSPOSED else d[n][0]

    shards = {n: local2d(w, n) for n in BIG}
    grads, delta, new_m, new_v = {}, {}, {}, {}
    rowlikes = [[_rowlike(n) for n in names] for names in GROUPS]

    class Exchanges:
        def __init__(self):
            self.gathers, self.forwards, self.fulls, self.reductions, self.small = {}, {}, {}, {}, None

        def _behind_first(self, a):
            return a + self.gathers[0]["token"][0, 0]

        def _start_gather(self, group, after):
            prepare = self._behind_first if group else (lambda a: a)
            sv = [shard_view(n, prepare(shards[n]).astype(BF16)) for n in GROUPS[group]]
            rl = rowlikes[group]
            self.gathers[group] = exchange_start(sv, [_full_shape(s, r) for s, r in zip(sv, rl)], _gather_plan(rl),
                                                 after, f"gather_start_{group}")

        def weights(self, group, after):
            if not self.gathers:
                self._start_gather(0, None)
                self.packed = [self._behind_first(pack_small(d)) for d in (w, m, v)]
                after = [self.gathers[0]["token"], *self.packed]
            if group not in self.fulls:
                names, rl, state = GROUPS[group], rowlikes[group], self.gathers[group]
                if group in self.forwards:
                    _, fulls = exchange_wait(self.forwards.pop(group), _forward_plan(rl), after,
                                             f"forward_wait_{group}", FORWARDS)
                else:
                    sv, fulls = exchange_wait(state, _gather_plan(rl), state["token"] if after is None else after,
                                              f"gather_wait_{group}")
                    fulls = gather_finish(sv, fulls, rl, f"gather_finish_{group}")
                self.fulls[group] = {n: full_from_view(n, f) for n, f in zip(names, fulls)}
                if group + 1 < len(GROUPS):
                    self._start_gather(group + 1, fulls[0])
            return self.fulls[group]

        def started(self):
            return self.gathers[max(self.gathers)]["token"]

        def prefetch(self, group, after):
            rl, state = rowlikes[group], self.gathers[group]
            sv, fulls = exchange_wait(state, _gather_plan(rl), after, f"gather_wait_{group}")
            self.forwards[group] = exchange_start(sv, fulls, _forward_plan(rl), None, f"forward_start_{group}", FORWARDS)
            return self.forwards[group]["token"]

        def grads(self, g):
            names = tuple(g)
            rl = [_rowlike(n) for n in names]
            gv = [grad_view(n, g[n]) for n in names]
            state = exchange_start(gv, [_half_buffer_shape(a, r) for a, r in zip(gv, rl)], _swap_plan(rl), None,
                                   "swap_start_" + names[0], SIBLING)
            self.reductions[names] = state
            return state["token"]

        def advance(self, names, after):
            rl = [_rowlike(n) for n in names]
            gv, gots = exchange_wait(self.reductions[names], _swap_plan(rl), after, "swap_wait_" + names[0], SIBLING)
            qs = [add_halves(a, b, r, place, "add_halves_" + n) for a, b, r, n in zip(gv, gots, rl, names)]
            state = exchange_start(qs, [_slot_shape(q, r) for q, r in zip(qs, rl)], _all_to_all_plan(rl), None,
                                   "all_to_all_start_" + names[0])
            self.reductions[names] = state
            return state["token"]

        def reduce(self, names, after):
            rl = [_rowlike(n) for n in names]
            qs, recvs = exchange_wait(self.reductions.pop(names), _all_to_all_plan(rl), after,
                                      "all_to_all_wait_" + names[0])
            halves = [sum_pieces(q, r, k, place, "sum_pieces_" + n) for q, r, k, n in zip(qs, recvs, rl, names)]
            return [(shards[n], o.reshape(shards[n].shape), local2d(m, n), local2d(v, n))
                    for n, o in zip(names, join_halves(halves, "join_halves_" + names[0]))]

        def updated(self, names, outs):
            for i, n in enumerate(names):
                grads[n], delta[n], new_m[n], new_v[n] = outs[4 * i:4 * i + 4]
            return [new_v[n] for n in names]

        def finish(self, names, after):
            updates = self.reduce(names, after)
            return self.updated(names, [o for n, u in zip(names, updates) for o in adamw(*u, "adamw_" + n)])

        def small_start(self, G, loss, after):
            packed = pack_small({n: G[n] for n in SMALL})
            used_rows = sum(-(-int(np.prod(small_like[n].shape)) // SMALL_W) for n in SMALL)
            assert packed.shape[0] > used_rows
            packed = packed.at[-1, 0].set(loss[0, 0])
            self.small = exchange_start([packed], [(N_DEV,) + packed.shape], _everyone_plan, after,
                                        "small_start", EVERYONE)
            return self.small["token"]

        def small_finish(self, after):
            (packed,), (slots,) = exchange_wait(self.small, _everyone_plan, after, "small_wait", EVERYONE)
            total = sum_devices(packed, slots, (4 * cx + 2 * cy + cc).astype(jnp.int32).reshape(1))
            self.loss = total[-1, 0]
            g, ds, ms, vs = adamw(self.packed[0], total, self.packed[1], self.packed[2], "adamw_small")
            for d, packed_d in ((grads, g), (delta, ds), (new_m, ms), (new_v, vs)):
                d.update(unpack_small(packed_d, small_like))
            return vs

    small_like = {n: w[n] for n in SMALL}
    S = {n: w[n].reshape(1, -1) for n in ("ffn1_norm", "mix_norm", "ffn2_norm", "final_norm", "pool_scale", "sinks")}
    S["pool_w_mix"] = w["pool_w_mix"][0].astype(BF16)
    exchanges = Exchanges()
    _, dx = local_fwd_bwd(x[0], loss_target[0], S, exchanges)
    loss = exchanges.loss

    def shaped(d, n):
        return (d[n].T if n in TRANSPOSED else d[n]).reshape(w[n].shape)

    return (loss, dx[None], *[shaped(grads, n) for n in WEIGHTS], *[shaped(delta, n) for n in WEIGHTS],
            *[shaped(new_m, n) for n in WEIGHTS], *[shaped(new_v, n) for n in WEIGHTS])
```

```python
import numpy as np
import jax
import jax.numpy as jnp
from jax import lax
from jax.experimental import pallas as pl
from jax.experimental.pallas import tpu as pltpu

F32 = jnp.float32
BF16 = jnp.bfloat16
SDS = jax.ShapeDtypeStruct
MESH = pl.DeviceIdType.MESH

D = 1024
FF = 2816
NQ = 16
NKV = 2
HD = 64
GQ = NQ // NKV
AW = NQ * HD
KVW = NKV * HD
BLK = 128
PW = 512
PG = 128
POOL_WINDOWS = (2, 4, 8, 16)
HALO = 16
INW = AW + 2 * KVW + PW + 2 * D
C_KV = AW
C_Z = AW + 2 * KVW
C_G = C_Z + PW
EPS = 1e-6
FF_CHUNK = 256
FF_CHUNKS = tuple((c, FF_CHUNK) for c in range(0, FF, FF_CHUNK))
SLOPES = tuple(float(2.0 ** (-8.0 * h / NQ)) for h in range(1, NQ + 1))
SCALE = HD ** -0.5

LR, B1, B2, ADAM_EPS, WD, STEP = 0.001, 0.9, 0.999, 1e-08, 0.01, 10

VMEM_LIMIT = 56 * 1024 * 1024
N_CHIPS = 4

NT = (((1,), (1,)), ((), ()))
TN = (((0,), (0,)), ((), ()))


def _cp(sem=None, vmem=VMEM_LIMIT):
    return pltpu.CompilerParams(dimension_semantics=sem, vmem_limit_bytes=vmem)


def _const_spec(shape):
    nd = len(shape)
    return pl.BlockSpec(shape, lambda *_: (0,) * nd, pipeline_mode=pl.Buffered(1))


def _rstd(x):
    return lax.rsqrt(jnp.mean(x * x, axis=-1, keepdims=True) + EPS)


def _rms_bwd(dn, xhat, rstd, g):
    dxhat = dn * g
    return rstd * (dxhat - xhat * jnp.mean(dxhat * xhat, axis=-1, keepdims=True))


def _dot(a, b):
    return jnp.dot(a, b, preferred_element_type=F32)


def _dot_nt(a, b):
    return lax.dot_general(a, b, NT, preferred_element_type=F32)


def _dot_tn(a, b):
    return lax.dot_general(a, b, TN, preferred_element_type=F32)


def ffn_fwd(h, g, wup, wdn, name, head=None):
    T = h.shape[0]
    TM = 512
    tile = lambda w: pl.BlockSpec((TM, w), lambda i: (i, 0))
    acc_spec = lambda w: pl.BlockSpec((1, w), lambda i: (0, 0))

    def body(h_ref, g_ref, wup_ref, wdn_ref, *rest):
        out_ref, ab_ref, hid_ref = rest[-3:] if head is None else rest[2:5]
        x = h_ref[...]
        n = (x * _rstd(x) * g_ref[...]).astype(BF16)
        for c0, w in FF_CHUNKS:
            a = _dot(n, wup_ref[:, c0:c0 + w])
            b = _dot(n, wup_ref[:, FF + c0:FF + c0 + w])
            sig = jax.nn.sigmoid(a)
            s = a * sig
            ab_ref[:, c0:c0 + w] = (b * (sig * (1.0 + a * (1.0 - sig)))).astype(BF16)
            ab_ref[:, FF + c0:FF + c0 + w] = s.astype(BF16)
            hid_ref[:, c0:c0 + w] = (s * b).astype(BF16)
        out = x + 0.5 * _dot(hid_ref[...], wdn_ref[...])
        if head is None:
            out_ref[...] = out
        else:
            t_ref, gf_ref, loss_ref, dgf_ref = rest[0], rest[1], rest[5], rest[6]
            out_ref[...] = _loss_head(out, t_ref[...], gf_ref[...], loss_ref, dgf_ref, pl.program_id(0) == 0)

    head_in, head_specs, head_out_specs, head_out_shape = [], [], [], []
    if head is not None:
        head_in, head_specs = list(head), [tile(D), _const_spec((1, D))]
        head_out_specs, head_out_shape = [acc_spec(1), acc_spec(D)], [SDS((1, 1), F32), SDS((1, D), F32)]
    return pl.pallas_call(
        body, name=name, grid=(T // TM,),
        in_specs=[tile(D), _const_spec((1, D)), _const_spec((D, 2 * FF)), _const_spec((FF, D))] + head_specs,
        out_specs=[tile(D), tile(2 * FF), tile(FF)] + head_out_specs,
        out_shape=[SDS((T, D), F32), SDS((T, 2 * FF), BF16), SDS((T, FF), BF16)] + head_out_shape,
        compiler_params=_cp(("arbitrary",)),
    )(h, g, wup, wdn, *head_in)


def _loss_head(x, target, g, loss_ref, dg_ref, first):
    @pl.when(first)
    def _():
        loss_ref[...] = jnp.zeros_like(loss_ref)
        dg_ref[...] = jnp.zeros_like(dg_ref)

    rstd = _rstd(x)
    xhat = x * rstd
    err = xhat * g - target
    loss_ref[...] += 0.5 * jnp.sum(jnp.mean(err * err, axis=-1, keepdims=True), axis=0, keepdims=True)
    dy = err * (1.0 / D)
    dg_ref[...] += jnp.sum(dy * xhat, axis=0, keepdims=True)
    return _rms_bwd(dy, xhat, rstd, g)


def _adamw_tile(w_ref, g_ref, m_ref, v_ref, go_ref, d_ref, nm_ref, nv_ref):
    gv = g_ref[...]
    go_ref[...] = gv
    nm = B1 * m_ref[...] + (1.0 - B1) * gv
    nv = B2 * v_ref[...] + (1.0 - B2) * (gv * gv)
    nm_ref[...] = nm
    nv_ref[...] = nv
    d_ref[...] = -LR * ((nm / (1.0 - B1 ** STEP)) / (jnp.sqrt(nv / (1.0 - B2 ** STEP)) + ADAM_EPS) + WD * w_ref[...])


def _riders(updates, steps, step_of):
    in_specs, out_specs, out_shapes, operands, tiles = [], [], [], [], []
    for w, g, m, v in updates:
        R, C = w.shape
        n = max(d for d in range(1, steps + 1) if R % d == 0 and (R // d) % 8 == 0)
        spec = pl.BlockSpec((R // n, C), lambda *ids, n=n: (jnp.minimum(step_of(*ids), n - 1), 0))
        in_specs += [spec] * 4
        out_specs += [spec] * 4
        out_shapes += [SDS((R, C), F32)] * 4
        operands += [w, g, m, v]
        tiles.append(n)

    def run(step, in_refs, out_refs):
        for u, n in enumerate(tiles):
            @pl.when(step < n)
            def _(u=u):
                _adamw_tile(*in_refs[4 * u:4 * u + 4], *out_refs[4 * u:4 * u + 4])

    return in_specs, out_specs, out_shapes, operands, run


def ffn_bwd_x(dh, h_in, ab, g, wup, wdn, name, updates=()):
    T = dh.shape[0]
    TM = 256
    r_in, r_out, r_shapes, r_args, ride = _riders(updates, T // TM, lambda i: i)

    def body(dh_ref, h_ref, ab_ref, g_ref, wup_ref, wdn_ref, *rest):
        dhin_ref, dup_ref, n_ref, dg_ref = rest[len(r_in):len(r_in) + 4]
        ride(pl.program_id(0), rest[:len(r_in)], rest[len(r_in) + 4:])
        x = h_ref[...]
        g = g_ref[...]
        rstd = _rstd(x)
        xhat = x * rstd
        n_ref[...] = (xhat * g).astype(BF16)
        dh = dh_ref[...]
        dhh = (0.5 * dh).astype(BF16)
        for c0, w in FF_CHUNKS:
            dhid = _dot_nt(dhh, wdn_ref[c0:c0 + w, :])
            dup_ref[:, c0:c0 + w] = (dhid * ab_ref[:, c0:c0 + w].astype(F32)).astype(BF16)
            dup_ref[:, FF + c0:FF + c0 + w] = (dhid * ab_ref[:, FF + c0:FF + c0 + w].astype(F32)).astype(BF16)
        dn = _dot_nt(dup_ref[...], wup_ref[...])
        dhin_ref[...] = dh + _rms_bwd(dn, xhat, rstd, g)

        @pl.when(pl.program_id(0) == 0)
        def _():
            dg_ref[...] = jnp.zeros_like(dg_ref)

        dg_ref[...] += jnp.sum(dn * xhat, axis=0, keepdims=True)

    tile = lambda w: pl.BlockSpec((TM, w), lambda i: (i, 0))
    return pl.pallas_call(
        body, name=name, grid=(T // TM,),
        in_specs=[tile(D), tile(D), tile(2 * FF), _const_spec((1, D)), _const_spec((D, 2 * FF)), _const_spec((FF, D))]
        + r_in,
        out_specs=[tile(D), tile(2 * FF), tile(D), pl.BlockSpec((1, D), lambda i: (0, 0))] + r_out,
        out_shape=[SDS((T, D), F32), SDS((T, 2 * FF), BF16), SDS((T, D), BF16), SDS((1, D), F32)] + r_shapes,
        compiler_params=_cp(("arbitrary",)),
    )(dh, h_in, ab, g, wup, wdn, *r_args)


TOKEN_SPEC = pl.BlockSpec((8, 128), lambda *_: (0, 0))


def _token_operand(token):
    return ([], []) if token is None else ([TOKEN_SPEC], [token])


def matmul_tn(a, b, name, *, tm, tn, tt=1024, b_scale=None, after=None):
    T, M = a.shape
    N = b.shape[1]
    tt = min(tt, T)
    assert M % tm == 0 and N % tn == 0 and T % tt == 0
    nt = T // tt
    token_spec, token_arg = _token_operand(after)

    def body(a_ref, b_ref, *rest):
        o_ref, acc_ref = rest[-2:]
        t = pl.program_id(2)

        @pl.when(t == 0)
        def _():
            acc_ref[...] = jnp.zeros_like(acc_ref)

        bv = b_ref[...]
        if b_scale is not None:
            bv = bv * b_scale
        acc_ref[...] += _dot_tn(a_ref[...].astype(BF16), bv.astype(BF16))

        @pl.when(t == nt - 1)
        def _():
            o_ref[...] = acc_ref[...].astype(BF16)

    return pl.pallas_call(
        body, name=name, grid=(M // tm, N // tn, nt),
        in_specs=[pl.BlockSpec((tt, tm), lambda i, j, t: (t, i)), pl.BlockSpec((tt, tn), lambda i, j, t: (t, j))]
        + token_spec,
        out_specs=pl.BlockSpec((tm, tn), lambda i, j, t: (i, j)),
        out_shape=SDS((M, N), BF16),
        scratch_shapes=[pltpu.VMEM((tm, tn), F32)],
        compiler_params=_cp(("parallel", "parallel", "arbitrary")),
    )(a, b, *token_arg)


def mix_in_fwd(h1, g, win_t):
    T = h1.shape[0]
    TM = 512

    def body(h_ref, g_ref, w_ref, u_ref, q_ref, kv_ref, z_ref, gt_ref):
        x = h_ref[...]
        u = (x * _rstd(x) * g_ref[...]).astype(BF16)
        u_ref[...] = u
        for c in range(0, AW, 256):
            q_ref[:, c:c + 256] = _dot_nt(u, w_ref[c:c + 256, :]).astype(BF16)
        kv_ref[...] = _dot_nt(u, w_ref[C_KV:C_Z, :]).astype(BF16)
        for c in range(0, PW, 256):
            z_ref[:, c:c + 256] = _dot_nt(u, w_ref[C_Z + c:C_Z + c + 256, :])
        for c in range(0, 2 * D, 256):
            gt_ref[:, c:c + 256] = _dot_nt(u, w_ref[C_G + c:C_G + c + 256, :]).astype(BF16)

    tile = lambda w: pl.BlockSpec((TM, w), lambda i: (i, 0))
    return pl.pallas_call(
        body, name="mix_in_fwd", grid=(T // TM,),
        in_specs=[tile(D), _const_spec((1, D)), _const_spec((INW, D))],
        out_specs=[tile(D), tile(AW), tile(2 * KVW), tile(PW), tile(2 * D)],
        out_shape=[SDS((T, D), BF16), SDS((T, AW), BF16), SDS((T, 2 * KVW), BF16), SDS((T, PW), F32),
                   SDS((T, 2 * D), BF16)],
        compiler_params=_cp(("arbitrary",)),
    )(h1, g, win_t)


def mix_in_bwd(dq, dkv, dz, dgt, dh2, h1, g, win_t):
    T = h1.shape[0]
    TM = min(1024, T)
    SUB = 256

    def body(dq_ref, dkv_ref, dz_ref, dgt_ref, dh2_ref, h_ref, g_ref, w_ref, dh1_ref, dg_ref):
        g = g_ref[...]
        dg = None
        for r0 in range(0, TM, SUB):
            rows = slice(r0, r0 + SUB)
            du = _dot(dq_ref[rows, :], w_ref[0:AW, :])
            du += _dot(dkv_ref[rows, :], w_ref[C_KV:C_Z, :])
            du += _dot(dz_ref[rows, :], w_ref[C_Z:C_G, :])
            du += _dot(dgt_ref[rows, :], w_ref[C_G:INW, :])
            x = h_ref[rows, :]
            rstd = _rstd(x)
            xhat = x * rstd
            dh1_ref[rows, :] = dh2_ref[rows, :] + _rms_bwd(du, xhat, rstd, g)
            part = jnp.sum(du * xhat, axis=0, keepdims=True)
            dg = part if dg is None else dg + part

        @pl.when(pl.program_id(0) == 0)
        def _():
            dg_ref[...] = jnp.zeros_like(dg_ref)

        dg_ref[...] += dg

    tile = lambda w: pl.BlockSpec((TM, w), lambda i: (i, 0))
    return pl.pallas_call(
        body, name="mix_in_bwd", grid=(T // TM,),
        in_specs=[tile(AW), tile(2 * KVW), tile(PW), tile(2 * D), tile(D), tile(D), _const_spec((1, D)),
                  _const_spec((INW, D))],
        out_specs=[tile(D), pl.BlockSpec((1, D), lambda i: (0, 0))],
        out_shape=[SDS((T, D), F32), SDS((1, D), F32)],
        compiler_params=_cp(("arbitrary",)),
    )(dq, dkv, dz, dgt, dh2, h1, g, win_t)


PAIR = 2 * HD
NPAIR = GQ // 2


def _lo_lanes():
    return lax.broadcasted_iota(jnp.int32, (BLK, PAIR), 1) < HD


def _stack_heads(ref, kvh, scale=None):
    lo = _lo_lanes()
    parts = []
    for pr in range(NPAIR):
        t = ref[:, (kvh * NPAIR + pr) * PAIR:(kvh * NPAIR + pr + 1) * PAIR]
        if scale is not None:
            t = t * scale
        zero = jnp.zeros_like(t)
        parts += [jnp.where(lo, t, zero), jnp.where(lo, zero, t)]
    return jnp.concatenate(parts, axis=0)


def _kv_tiles(kvc_ref, kvp_ref, tile, kvh):
    lo = _lo_lanes()
    dup, left, right = [], [], []
    for ref in (kvp_ref, kvc_ref):
        t = ref[:, tile * PAIR:(tile + 1) * PAIR]
        r = pltpu.roll(t.astype(F32), HD, 1).astype(BF16)
        zero = jnp.zeros_like(t)
        a, b = (t, r) if kvh == 0 else (r, t)
        dup.append(jnp.where(lo, a, b))
        left.append(jnp.where(lo, a, zero))
        right.append(jnp.where(lo, zero, b))
    cat = lambda xs: jnp.concatenate(xs, axis=0)
    return cat(dup), cat(left), cat(right)


def _band_consts(first):
    row = lax.broadcasted_iota(jnp.int32, (BLK, BLK), 0)
    col = lax.broadcasted_iota(jnp.int32, (BLK, BLK), 1)
    upper = col > row
    dist = jnp.where(upper, row - col + BLK, row - col).astype(F32)
    pen = jnp.where(jnp.logical_and(upper, first), -jnp.inf, 0.0)
    return upper, dist, pen


def _split_band(upper, t):
    zero = jnp.zeros_like(t)
    return jnp.concatenate([jnp.where(upper, t, zero), jnp.where(upper, zero, t)], axis=1)


def attn_fwd(q, kv, sinks):
    T = q.shape[0]
    nb = T // BLK

    def body(sink_ref, q_ref, kvc_ref, kvp_ref, att_ref, lse_ref):
        upper, dist, pen = _band_consts(pl.program_id(0) == 0)
        scores, values = [], []
        for kvh in range(NKV):
            kdup, _, _ = _kv_tiles(kvc_ref, kvp_ref, 0, kvh)
            values.append(_kv_tiles(kvc_ref, kvp_ref, 1, kvh)[1:])
            scores.append(_dot_nt(_stack_heads(q_ref, kvh, SCALE), kdup))
        for kvh in range(NKV):
            s_all = scores[kvh]
            vleft, vright = values[kvh]
            for pr in range(NPAIR):
                outs, inv = [], []
                for side, vpad in ((0, vleft), (1, vright)):
                    g = 2 * pr + side
                    hq = kvh * GQ + g
                    sink = sink_ref[0, hq]
                    rows = slice(g * BLK, (g + 1) * BLK)
                    s = jnp.where(upper, s_all[rows, 0:BLK], s_all[rows, BLK:2 * BLK]) - SLOPES[hq] * dist + pen
                    m = jnp.maximum(jnp.max(s, axis=-1, keepdims=True), sink)
                    p = jnp.exp(s - m)
                    l = jnp.sum(p, axis=-1, keepdims=True) + jnp.exp(sink - m)
                    lse_ref[:, hq:hq + 1] = m + jnp.log(l)
                    outs.append(_dot(_split_band(upper, p.astype(BF16)), vpad))
                    inv.append(1.0 / l)
                col0 = (kvh * NPAIR + pr) * PAIR
                att_ref[:, col0:col0 + PAIR] = ((outs[0] + outs[1]) * jnp.where(_lo_lanes(), inv[0], inv[1])).astype(BF16)

    return pl.pallas_call(
        body, name="attn_fwd", grid=(nb,),
        in_specs=[pl.BlockSpec(memory_space=pltpu.SMEM),
                  pl.BlockSpec((BLK, AW), lambda i: (i, 0)),
                  pl.BlockSpec((BLK, 2 * KVW), lambda i: (i, 0)),
                  pl.BlockSpec((BLK, 2 * KVW), lambda i: (jnp.maximum(i - 1, 0), 0))],
        out_specs=[pl.BlockSpec((BLK, AW), lambda i: (i, 0)), pl.BlockSpec((BLK, NQ), lambda i: (i, 0))],
        out_shape=[SDS((T, AW), BF16), SDS((T, NQ), F32)],
        compiler_params=_cp(("arbitrary",)),
    )(sinks, q, kv, kv)


def attn_bwd(q, kv, datt, att, lse, sinks):
    T = q.shape[0]
    nb = T // BLK

    def body(sink_ref, q_ref, kvc_ref, kvp_ref, do_ref, out_ref, lse_ref, dq_ref, dkv_ref, dsink_ref, carry_ref):
        i = pl.program_id(0)

        @pl.when(i == 0)
        def _():
            dsink_ref[...] = jnp.zeros_like(dsink_ref)
            carry_ref[...] = jnp.zeros_like(carry_ref)

        @pl.when(i < nb)
        def _():
            upper, dist, pen = _band_consts(i == 0)
            lo = _lo_lanes()
            dk_dup, dv_dup = [], []
            staged = []
            for kvh in range(NKV):
                kdup, kleft, kright = _kv_tiles(kvc_ref, kvp_ref, 0, kvh)
                vdup, _, _ = _kv_tiles(kvc_ref, kvp_ref, 1, kvh)
                qs = _stack_heads(q_ref, kvh, SCALE)
                dos = _stack_heads(do_ref, kvh)
                staged.append((kleft, kright, qs, dos, _dot_nt(qs, kdup), _dot_nt(dos, vdup)))
            deltas = []
            for pair in range(NQ // 2):
                cols = slice(pair * PAIR, (pair + 1) * PAIR)
                t = do_ref[:, cols].astype(F32) * out_ref[:, cols].astype(F32)
                deltas += [jnp.sum(jnp.where(lo, t, 0.0), axis=-1, keepdims=True),
                           jnp.sum(jnp.where(lo, 0.0, t), axis=-1, keepdims=True)]
            for kvh in range(NKV):
                kleft, kright, qs, dos, s_all, dp_all = staged[kvh]
                ds_parts, p_parts = [], []
                for pr in range(NPAIR):
                    dq = None
                    for side, kpad in ((0, kleft), (1, kright)):
                        g = 2 * pr + side
                        hq = kvh * GQ + g
                        lse_h = lse_ref[:, hq:hq + 1]
                        rows = slice(g * BLK, (g + 1) * BLK)
                        s = jnp.where(upper, s_all[rows, 0:BLK], s_all[rows, BLK:2 * BLK]) - SLOPES[hq] * dist + pen
                        p = jnp.exp(s - lse_h)
                        dp = jnp.where(upper, dp_all[rows, 0:BLK], dp_all[rows, BLK:2 * BLK])
                        delta = deltas[hq]
                        dsink_ref[:, hq:hq + 1] += -jnp.sum(jnp.exp(sink_ref[0, hq] - lse_h) * delta, axis=0,
                                                            keepdims=True)
                        ds = _split_band(upper, (p * (dp - delta)).astype(BF16))
                        ds_parts.append(ds)
                        p_parts.append(_split_band(upper, p.astype(BF16)))
                        d = _dot(ds, kpad)
                        dq = d if dq is None else dq + d
                    col0 = (kvh * NPAIR + pr) * PAIR
                    dq_ref[:, col0:col0 + PAIR] = (dq * SCALE).astype(BF16)
                dkw = _dot_tn(qs, jnp.concatenate(ds_parts, axis=0)).T
                dvw = _dot_tn(dos, jnp.concatenate(p_parts, axis=0)).T
                dk_dup.append(dkw + pltpu.roll(dkw, HD, 1))
                dv_dup.append(dvw + pltpu.roll(dvw, HD, 1))
            dk = jnp.where(jnp.concatenate([lo, lo], axis=0), dk_dup[0], dk_dup[1])
            dv = jnp.where(jnp.concatenate([lo, lo], axis=0), dv_dup[0], dv_dup[1])
            dkv_ref[:, 0:PAIR] = (carry_ref[:, 0:PAIR] + dk[0:BLK]).astype(BF16)
            dkv_ref[:, PAIR:2 * PAIR] = (carry_ref[:, PAIR:2 * PAIR] + dv[0:BLK]).astype(BF16)
            carry_ref[:, 0:PAIR] = dk[BLK:2 * BLK]
            carry_ref[:, PAIR:2 * PAIR] = dv[BLK:2 * BLK]

        @pl.when(i == nb)
        def _():
            dkv_ref[...] = carry_ref[...].astype(BF16)

    cur = lambda i: (jnp.minimum(i, nb - 1), 0)
    prev = lambda i: (jnp.maximum(jnp.minimum(i, nb - 1) - 1, 0), 0)
    return pl.pallas_call(
        body, name="attn_bwd", grid=(nb + 1,),
        in_specs=[pl.BlockSpec(memory_space=pltpu.SMEM),
                  pl.BlockSpec((BLK, AW), cur), pl.BlockSpec((BLK, 2 * KVW), cur), pl.BlockSpec((BLK, 2 * KVW), prev),
                  pl.BlockSpec((BLK, AW), cur), pl.BlockSpec((BLK, AW), cur), pl.BlockSpec((BLK, NQ), cur)],
        out_specs=[pl.BlockSpec((BLK, AW), cur),
                   pl.BlockSpec((BLK, 2 * KVW), lambda i: (jnp.maximum(i - 1, 0), 0)),
                   pl.BlockSpec((1, NQ), lambda i: (0, 0))],
        out_shape=[SDS((T, AW), BF16), SDS((T, 2 * KVW), BF16), SDS((1, NQ), F32)],
        scratch_shapes=[pltpu.VMEM((BLK, 2 * KVW), F32)],
        compiler_params=_cp(("arbitrary",)),
    )(sinks, q, kv, kv, datt, att, lse)


def _inv_counts(t0, rows):
    t = (t0 + lax.broadcasted_iota(jnp.int32, (rows, 1), 0) + 1).astype(F32)
    return [1.0 / jnp.minimum(t, float(w)) for w in POOL_WINDOWS]


def pool_fwd(z, wmix, scale):
    T = z.shape[0]
    TM = min(1024, T)
    L = TM + HALO

    def body(z_ref, halo_ref, wmix_ref, scale_ref, pooled_ref, mixs_ref):
        i = pl.program_id(0)
        halo = jnp.where(i > 0, halo_ref[...], 0.0)
        zt = z_ref[...]
        e = jnp.concatenate([halo, zt], axis=0)
        sums = []
        s = e
        for k in (1, 2, 4, 8):
            s = s + pltpu.roll(s, k, 0)
            sums.append(s)
        inv = _inv_counts(i * TM, TM)
        for gi in range(len(POOL_WINDOWS)):
            cols = slice(gi * PG, (gi + 1) * PG)
            pooled = (sums[gi][HALO:, cols] * inv[gi] - zt[:, cols]).astype(BF16)
            pooled_ref[:, cols] = pooled
            mixs_ref[:, cols] = (_dot(pooled, wmix_ref[gi]) * scale_ref[:, cols]).astype(BF16)

    return pl.pallas_call(
        body, name="pool_fwd", grid=(T // TM,),
        in_specs=[pl.BlockSpec((TM, PW), lambda i: (i, 0)),
                  pl.BlockSpec((HALO, PW), lambda i: (jnp.maximum(i * (TM // HALO) - 1, 0), 0)),
                  _const_spec((len(POOL_WINDOWS), PG, PG)), _const_spec((1, PW))],
        out_specs=[pl.BlockSpec((TM, PW), lambda i: (i, 0)), pl.BlockSpec((TM, PW), lambda i: (i, 0))],
        out_shape=[SDS((T, PW), BF16), SDS((T, PW), BF16)],
        compiler_params=_cp(("arbitrary",)),
    )(z, z, wmix, scale)


def pool_bwd(dmixs, pooled, wmix, scale):
    T = dmixs.shape[0]
    TM = min(1024, T)
    L = TM + HALO
    nt = T // TM

    def body(dm_ref, halo_ref, pooled_ref, wmix_ref, scale_ref, dz_ref, dwmix_ref, dscale_ref):
        i = pl.program_id(0)

        @pl.when(i == 0)
        def _():
            dwmix_ref[...] = jnp.zeros_like(dwmix_ref)
            dscale_ref[...] = jnp.zeros_like(dscale_ref)

        halo = jnp.where(i < nt - 1, halo_ref[...], 0.0)
        dm = dm_ref[...]
        e = jnp.concatenate([dm, halo], axis=0)
        inv = _inv_counts(i * TM, L)
        for gi in range(len(POOL_WINDOWS)):
            cols = slice(gi * PG, (gi + 1) * PG)
            w = wmix_ref[gi]
            dmixed = (e[:, cols] * scale_ref[:, cols]).astype(BF16)
            dpooled = _dot_nt(dmixed, w)
            pooled = pooled_ref[:, cols]
            mixed = _dot(pooled, w)
            dscale_ref[:, cols] += jnp.sum(dm[:, cols] * mixed, axis=0, keepdims=True)
            dwmix_ref[gi] += _dot_tn(pooled, dmixed[:TM])
            s = dpooled * inv[gi]
            k = 1
            while k < POOL_WINDOWS[gi]:
                s = s + pltpu.roll(s, L - k, 0)
                k *= 2
            dz_ref[:, cols] = (s[:TM] - dpooled[:TM]).astype(BF16)

    return pl.pallas_call(
        body, name="pool_bwd", grid=(nt,),
        in_specs=[pl.BlockSpec((TM, PW), lambda i: (i, 0)),
                  pl.BlockSpec((HALO, PW), lambda i: (jnp.minimum((i + 1) * (TM // HALO), T // HALO - 1), 0)),
                  pl.BlockSpec((TM, PW), lambda i: (i, 0)),
                  _const_spec((len(POOL_WINDOWS), PG, PG)), _const_spec((1, PW))],
        out_specs=[pl.BlockSpec((TM, PW), lambda i: (i, 0)),
                   pl.BlockSpec((len(POOL_WINDOWS), PG, PG), lambda i: (0, 0, 0)),
                   pl.BlockSpec((1, PW), lambda i: (0, 0))],
        out_shape=[SDS((T, PW), BF16), SDS((len(POOL_WINDOWS), PG, PG), F32), SDS((1, PW), F32)],
        compiler_params=_cp(("arbitrary",)),
    )(dmixs, dmixs, pooled, wmix, scale)


def merge_fwd(att, mixs, gt, h1, wattn, wpool, wout):
    T = h1.shape[0]
    TM = 512

    def body(att_ref, mixs_ref, gt_ref, h_ref, wa_ref, wp_ref, wo_ref, h2_ref, mg_ref, gf_ref):
        a = _dot(att_ref[...], wa_ref[...])
        p = _dot(mixs_ref[...], wp_ref[...])
        sa = jax.nn.sigmoid(gt_ref[:, 0:D].astype(F32))
        sp = jax.nn.sigmoid(gt_ref[:, D:2 * D].astype(F32))
        gf_ref[:, 0:D] = (a * sa * (1.0 - sa)).astype(BF16)
        gf_ref[:, D:2 * D] = (p * sp * (1.0 - sp)).astype(BF16)
        mg = (sa * a + sp * p).astype(BF16)
        mg_ref[...] = mg
        h2_ref[...] = h_ref[...] + _dot(mg, wo_ref[...])

    tile = lambda w: pl.BlockSpec((TM, w), lambda i: (i, 0))
    return pl.pallas_call(
        body, name="merge_fwd", grid=(T // TM,),
        in_specs=[tile(AW), tile(PW), tile(2 * D), tile(D), _const_spec((AW, D)), _const_spec((PW, D)),
                  _const_spec((D, D))],
        out_specs=[tile(D), tile(D), tile(2 * D)],
        out_shape=[SDS((T, D), F32), SDS((T, D), BF16), SDS((T, 2 * D), BF16)],
        compiler_params=_cp(("arbitrary",)),
    )(att, mixs, gt, h1, wattn, wpool, wout)


def merge_bwd(dh2, gt, gf, wattn, wpool, wout, after=None):
    T = dh2.shape[0]
    TM = 512
    token_spec, token_arg = _token_operand(after)

    def body(dh2_ref, gt_ref, gf_ref, wa_ref, wp_ref, wo_ref, *rest):
        datt_ref, dmixs_ref, dgt_ref, da_ref, dp_ref = rest[-5:]
        dm = _dot_nt(dh2_ref[...].astype(BF16), wo_ref[...])
        da = (dm * jax.nn.sigmoid(gt_ref[:, 0:D].astype(F32))).astype(BF16)
        dp = (dm * jax.nn.sigmoid(gt_ref[:, D:2 * D].astype(F32))).astype(BF16)
        da_ref[...] = da
        dp_ref[...] = dp
        dgt_ref[:, 0:D] = (dm * gf_ref[:, 0:D].astype(F32)).astype(BF16)
        dgt_ref[:, D:2 * D] = (dm * gf_ref[:, D:2 * D].astype(F32)).astype(BF16)
        datt_ref[...] = _dot_nt(da, wa_ref[...]).astype(BF16)
        dmixs_ref[...] = _dot_nt(dp, wp_ref[...])

    tile = lambda w: pl.BlockSpec((TM, w), lambda i: (i, 0))
    return pl.pallas_call(
        body, name="merge_bwd", grid=(T // TM,),
        in_specs=[tile(D), tile(2 * D), tile(2 * D), _const_spec((AW, D)), _const_spec((PW, D)),
                  _const_spec((D, D))] + token_spec,
        out_specs=[tile(AW), tile(PW), tile(2 * D), tile(D), tile(D)],
        out_shape=[SDS((T, AW), BF16), SDS((T, PW), F32), SDS((T, 2 * D), BF16), SDS((T, D), BF16),
                   SDS((T, D), BF16)],
        compiler_params=_cp(("arbitrary",)),
    )(dh2, gt, gf, wattn, wpool, wout, *token_arg)


def adamw(w, g, m, v, name):
    R, C = w.shape
    tile_bytes = 2 * 1024 * 1024
    tr = R
    if R * C * 4 > tile_bytes:
        tr = next(cand for cand in (512, 256, 128, 64, 32, 16, 8) if R % cand == 0 and cand * C * 4 <= tile_bytes)

    def body(*refs):
        _adamw_tile(*refs)

    spec = pl.BlockSpec((tr, C), lambda i: (i, 0))
    return pl.pallas_call(
        body, name=name, grid=(R // tr,),
        in_specs=[spec] * 4, out_specs=[spec] * 4, out_shape=[SDS((R, C), F32)] * 4,
        compiler_params=_cp(("parallel",)),
    )(w, g, m, v)


GROUP_FFN1, GROUP_MIX, GROUP_FFN2 = 0, 1, 2


def _behind(small, token):
    return small if token is None else small + token[0:1, 0:1]


def local_fwd_bwd(x, target, S, comm):
    W = comm.weights(GROUP_FFN1, None)
    h1, ab1, hid1 = ffn_fwd(x, _behind(S["ffn1_norm"], comm.started()), W["ffn1_w_up"], W["ffn1_w_down"], "ffn1_fwd")
    W = comm.weights(GROUP_MIX, h1)
    u, q, kv, z, gt = mix_in_fwd(h1, _behind(S["mix_norm"], comm.started()), W["w_in"])
    att, lse = attn_fwd(q, kv, S["sinks"])
    pooled, mixs = pool_fwd(z, S["pool_w_mix"], _behind(S["pool_scale"], comm.prefetch(GROUP_FFN2, att)))
    h2, merged, gf = merge_fwd(att, mixs, gt, h1, W["w_attn_up"], W["w_pool_up"], W["w_out"])
    W = comm.weights(GROUP_FFN2, h2)
    dh3, ab2, hid2, loss, g_final = ffn_fwd(h2, S["ffn2_norm"], W["ffn2_w_up"], W["ffn2_w_down"], "ffn2_fwd",
                                            head=(target, S["final_norm"]))

    G = {"final_norm": g_final}
    dh2, dup2, n2, G["ffn2_norm"] = ffn_bwd_x(dh3, h2, ab2, S["ffn2_norm"], W["ffn2_w_up"], W["ffn2_w_down"],
                                              "ffn2_bwd_x")
    ffn2 = ("ffn2_w_down", "ffn2_w_up")
    token = comm.grads({"ffn2_w_down": matmul_tn(hid2, dh3, "ffn2_dw_down", tm=1408, tn=512, tt=4096, b_scale=0.5),
                        "ffn2_w_up": matmul_tn(n2, dup2, "ffn2_dw_up", tm=D, tn=512, tt=4096)})

    W = comm.weights(GROUP_MIX, None)
    datt, dmixs, dgt, da, dp = merge_bwd(dh2, gt, gf, W["w_attn_up"], W["w_pool_up"], W["w_out"], after=token)
    token = comm.advance(ffn2, datt)
    g_mix = {"w_out": matmul_tn(merged, dh2, "dw_out", tm=D, tn=D, after=token),
             "w_attn_up": matmul_tn(att, da, "dw_attn_up", tm=AW, tn=D),
             "w_pool_up": matmul_tn(mixs, dp, "dw_pool_up", tm=PW, tn=D)}
    dz, G["pool_w_mix"], G["pool_scale"] = pool_bwd(dmixs, pooled, S["pool_w_mix"], _behind(S["pool_scale"], token))
    dq, dkv, G["sinks"] = attn_bwd(q, kv, datt, att, lse, _behind(S["sinks"], token))
    dh1, G["mix_norm"] = mix_in_bwd(dq, dkv, dz, dgt, dh2, h1, S["mix_norm"], W["w_in"])
    g_mix["w_in"] = jnp.concatenate([
        matmul_tn(dq, u, "dw_in_q", tm=AW, tn=D),
        matmul_tn(dkv, u, "dw_in_kv", tm=2 * KVW, tn=D),
        matmul_tn(dz, u, "dw_in_z", tm=PW, tn=D),
        matmul_tn(dgt, u, "dw_in_g", tm=D, tn=512, tt=4096),
    ], axis=0)
    mix = tuple(g_mix)
    token = comm.grads(g_mix)

    W = comm.weights(GROUP_FFN1, None)
    g_dn1 = matmul_tn(hid1, dh1, "ffn1_dw_down", tm=1408, tn=512, tt=4096, b_scale=0.5, after=token)
    token = comm.advance(mix, g_dn1)
    token2 = comm.grads({"ffn1_w_down": g_dn1})
    g1 = _behind(_behind(S["ffn1_norm"], token), token2)
    dx, dup1, n1, G["ffn1_norm"], *outs = ffn_bwd_x(dh1, x, ab1, g1, W["ffn1_w_up"], W["ffn1_w_down"], "ffn1_bwd_x",
                                                   updates=comm.reduce(ffn2, token2))
    comm.updated(ffn2, outs)
    token = comm.advance(("ffn1_w_down",), dx)
    token2 = comm.small_start(G, loss, dx)
    g_up1 = matmul_tn(n1, dup1, "ffn1_dw_up", tm=D, tn=512, tt=4096,
                      after=None if token is None else token + token2)
    token = comm.grads({"ffn1_w_up": g_up1})
    done = comm.finish(("ffn1_w_down",), token)
    token = comm.advance(("ffn1_w_up",), done)
    done = comm.finish(mix, token)
    done = comm.small_finish(done)
    comm.finish(("ffn1_w_up",), done)
    return loss, dx


HBM_SPEC = pl.BlockSpec(memory_space=pltpu.HBM)


def _me():
    return lax.axis_index("x"), lax.axis_index("y"), lax.axis_index("c")


def _peer_chip(x, y, k):
    return x ^ (k >> 1), y ^ (k & 1)


def _piece_half(ref, rowlike, j, h):
    if rowlike:
        return ref.at[j, h]
    ns = ref.shape[-1] // N_CHIPS
    return ref.at[h, :, pl.ds(pl.multiple_of(j * ns, 128), ns)]


def _piece(ref, rowlike, j):
    if rowlike:
        return ref.at[j]
    ns = ref.shape[-1] // N_CHIPS
    return ref.at[:, :, pl.ds(pl.multiple_of(j * ns, 128), ns)]


def _full_shape(shard_view, rowlike):
    _, kh, ns = shard_view.shape
    return (N_CHIPS, 2, kh, ns) if rowlike else (2, kh, N_CHIPS * ns)


def _remote(src, dst, send_sem, recv_sem, dev):
    return pltpu.make_async_remote_copy(src, dst, send_sem, recv_sem, device_id=dev, device_id_type=MESH)


SEM_SPEC = pl.BlockSpec(memory_space=pltpu.SEMAPHORE)
SPLIT_PARAMS = pltpu.CompilerParams(has_side_effects=pltpu.SideEffectType.DATAFLOW_SIDE_EFFECTING)


def _gather_plan(rowlikes):
    def plan(s_refs, f_refs, a, k, x, y, c):
        px, py = _peer_chip(x, y, k)
        return (s_refs[a].at[c], _piece_half(f_refs[a], rowlikes[a], 2 * x + y, c),
                _piece_half(f_refs[a], rowlikes[a], 2 * px + py, c), (px, py, c))
    return plan


def _all_to_all_plan(rowlikes):
    def plan(q_refs, r_refs, a, k, x, y, c):
        px, py = _peer_chip(x, y, k)
        if rowlikes[a]:
            src = q_refs[a].at[2 * px + py]
        else:
            ns = q_refs[a].shape[-1] // N_CHIPS
            src = q_refs[a].at[:, pl.ds(pl.multiple_of((2 * px + py) * ns, 128), ns)]
        return src, r_refs[a].at[k - 1], r_refs[a].at[k - 1], (px, py, c)
    return plan


def _swap_plan(rowlikes):
    def plan(g_refs, got_refs, a, k, x, y, c):
        src = g_refs[a].at[:, 1 - c] if rowlikes[a] else g_refs[a].at[1 - c]
        return src, got_refs[a], got_refs[a], (x, y, 1 - c)
    return plan


def _everyone_plan(s_refs, slot_refs, a, k, x, y, c):
    px, py, pc = x ^ (k >> 2), y ^ ((k >> 1) & 1), c ^ (k & 1)
    return s_refs[a], slot_refs[a].at[4 * x + 2 * y + c], slot_refs[a].at[4 * px + 2 * py + pc], (px, py, pc)


def _forward_plan(rowlikes):
    def plan(s_refs, f_refs, a, k, x, y, c):
        sib = (x, y, 1 - c)
        if k == 0:
            own = _piece(f_refs[a], rowlikes[a], 2 * x + y)
            return s_refs[a], own, own, sib
        px, py = _peer_chip(x, y, k)
        mine = _piece_half(f_refs[a], rowlikes[a], 2 * px + py, c)
        return mine, mine, _piece_half(f_refs[a], rowlikes[a], 2 * px + py, 1 - c), sib
    return plan


CHIPS, SIBLING, EVERYONE, FORWARDS = (1, 2, 3), (1,), tuple(range(1, 8)), (0, 1, 2, 3)


def _as_list(after):
    return [] if after is None else list(after) if isinstance(after, (list, tuple)) else [after]


def exchange_start(srcs, land_shapes, plan, after, name, peers=CHIPS):
    n = len(srcs)
    lands = [l if hasattr(l, "dtype") else lax.empty(l, s.dtype) for l, s in zip(land_shapes, srcs)]

    behind = _as_list(after)

    def body(*refs):
        s_refs, l_refs = refs[:n], refs[n:2 * n]
        send_sems, recv_sems = refs[2 * n + len(behind)], refs[2 * n + len(behind) + 1]
        token = refs[-1]
        x, y, c = _me()
        for a in range(n):
            for i, k in enumerate(peers):
                src, dst, _, peer = plan(s_refs, l_refs, a, k, x, y, c)
                sem = len(peers) * a + i
                _remote(src, dst, send_sems.at[sem], recv_sems.at[sem], peer).start()
        token[...] = jnp.zeros_like(token)

    n_sems = len(peers) * n
    outs = pl.pallas_call(
        body, name=name, in_specs=[HBM_SPEC] * (2 * n) + [pl.BlockSpec(memory_space=pl.ANY)] * len(behind),
        out_specs=[SEM_SPEC, SEM_SPEC] + [HBM_SPEC] * (2 * n) + [pl.BlockSpec(memory_space=pltpu.VMEM)],
        out_shape=[pltpu.SemaphoreType.DMA((n_sems,)), pltpu.SemaphoreType.DMA((n_sems,))]
        + [pltpu.HBM(a.shape, a.dtype) for a in (*srcs, *lands)] + [SDS((8, 128), F32)],
        input_output_aliases={i: 2 + i for i in range(2 * n)},
        compiler_params=SPLIT_PARAMS,
    )(*[pltpu.with_memory_space_constraint(a, pltpu.HBM) for a in (*srcs, *lands)], *behind)
    return {"sems": outs[:2], "srcs": outs[2:2 + n], "lands": outs[2 + n:2 + 2 * n], "token": outs[-1]}


def exchange_wait(state, plan, after, name, peers=CHIPS):
    n = len(state["srcs"])
    behind = _as_list(after)

    def body(*refs):
        s_refs, l_refs = refs[:n], refs[n:2 * n]
        send_sems, recv_sems = refs[2 * n], refs[2 * n + 1]
        x, y, c = _me()
        for a in range(n):
            for i, k in enumerate(peers):
                src, _, landing, peer = plan(s_refs, l_refs, a, k, x, y, c)
                sem = len(peers) * a + i
                cp = _remote(src, landing, send_sems.at[sem], recv_sems.at[sem], peer)
                cp.wait_send()
                cp.wait_recv()

    bufs = (*state["srcs"], *state["lands"])
    outs = pl.pallas_call(
        body, name=name,
        in_specs=[HBM_SPEC] * (2 * n) + [SEM_SPEC, SEM_SPEC] + [pl.BlockSpec(memory_space=pl.ANY)] * len(behind),
        out_specs=[HBM_SPEC] * (2 * n),
        out_shape=[pltpu.HBM(a.shape, a.dtype) for a in bufs],
        input_output_aliases={i: i for i in range(2 * n)},
        compiler_params=SPLIT_PARAMS,
    )(*bufs, *state["sems"], *behind)
    return outs[:n], outs[n:]


def gather_finish(shards, fulls, rowlikes, name):
    n = len(shards)

    def body(*refs):
        s_refs, f_refs = refs[:n], refs[2 * n:3 * n]
        send_sems, recv_sems = refs[3 * n:]
        x, y, c = _me()
        chip = 2 * x + y
        sib = (x, y, 1 - c)
        sends = []
        for a in range(n):
            own = _piece(f_refs[a], rowlikes[a], chip)
            cp = _remote(s_refs[a], own, send_sems.at[a, 0], recv_sems.at[a, 0], sib)
            cp.start()
            sends.append(cp)
            for k in (1, 2, 3):
                px, py = _peer_chip(x, y, k)
                slot = _piece_half(f_refs[a], rowlikes[a], 2 * px + py, c)
                cp = _remote(slot, slot, send_sems.at[a, k], recv_sems.at[a, k], sib)
                cp.start()
                sends.append(cp)
        for a in range(n):
            own = _piece(f_refs[a], rowlikes[a], chip)
            _remote(own, own, send_sems.at[a, 0], recv_sems.at[a, 0], sib).wait_recv()
            for k in (1, 2, 3):
                px, py = _peer_chip(x, y, k)
                slot = _piece_half(f_refs[a], rowlikes[a], 2 * px + py, 1 - c)
                _remote(slot, slot, send_sems.at[a, k], recv_sems.at[a, k], sib).wait_recv()
        for cp in sends:
            cp.wait_send()

    return pl.pallas_call(
        body, name=name, in_specs=[HBM_SPEC] * (2 * n), out_specs=[HBM_SPEC] * n,
        out_shape=[SDS(f.shape, f.dtype) for f in fulls],
        input_output_aliases={n + a: a for a in range(n)},
        scratch_shapes=[pltpu.SemaphoreType.DMA((n, 4)), pltpu.SemaphoreType.DMA((n, 4))],
    )(*shards, *fulls)


def _half_buffer_shape(gview, rowlike):
    return (N_CHIPS,) + gview.shape[2:] if rowlike else gview.shape[1:]


def _row_tiles(kh, ns):
    return 1 if kh * ns <= 256 * 1024 else 2


def _piece_specs(rowlike, kh, ns, piece, half):
    tr = kh // _row_tiles(kh, ns)
    if rowlike:
        return (pl.BlockSpec((None, None, tr, ns), lambda *g: (piece(*g), half(*g), g[-2], 0)),
                pl.BlockSpec((None, tr, ns), lambda *g: (piece(*g), g[-2], 0)))
    return (pl.BlockSpec((None, tr, ns), lambda *g: (half(*g), g[-2], piece(*g))),
            pl.BlockSpec((tr, ns), lambda *g: (g[-2], piece(*g))))


def add_halves(gview, got, rowlike, place, name):
    kh, ns = (gview.shape[2], gview.shape[3]) if rowlike else (gview.shape[1], gview.shape[2] // N_CHIPS)

    def body(place_ref, g_ref, got_ref, o_ref):
        o_ref[...] = (g_ref[...].astype(F32) + got_ref[...].astype(F32)).astype(BF16)

    g_spec, h_spec = _piece_specs(rowlike, kh, ns, lambda j, r, p: j, lambda j, r, p: p[0])
    return pl.pallas_call(
        body, name=name,
        grid_spec=pltpu.PrefetchScalarGridSpec(num_scalar_prefetch=1, grid=(N_CHIPS, _row_tiles(kh, ns)),
                                               in_specs=[g_spec, h_spec], out_specs=h_spec),
        out_shape=SDS(got.shape, BF16),
        compiler_params=_cp(("parallel", "parallel")),
    )(place, gview, got)


def _slot_shape(q, rowlike):
    return (3,) + (q.shape[1:] if rowlike else (q.shape[0], q.shape[1] // N_CHIPS))


def sum_pieces(q, recv, rowlike, place, name):
    kh, ns = recv.shape[1], recv.shape[2]
    tr = kh // _row_tiles(kh, ns)

    def body(place_ref, q_ref, r_ref, o_ref):
        acc = q_ref[...].astype(F32)
        for k in range(3):
            acc = acc + r_ref[k].astype(F32)
        o_ref[...] = acc

    _, h_spec = _piece_specs(rowlike, kh, ns, lambda z, r, p: p[1], lambda z, r, p: p[0])
    return pl.pallas_call(
        body, name=name,
        grid_spec=pltpu.PrefetchScalarGridSpec(
            num_scalar_prefetch=1, grid=(1, _row_tiles(kh, ns)),
            in_specs=[h_spec, pl.BlockSpec((3, tr, ns), lambda z, r, p: (0, r, 0))],
            out_specs=pl.BlockSpec((None, tr, ns), lambda z, r, p: (p[0], r, 0))),
        out_shape=SDS((2, kh, ns), F32),
        compiler_params=_cp(("parallel", "parallel")),
    )(place, q, recv)


def join_halves(halves, name):
    n = len(halves)

    def body(*refs):
        o_refs = refs[n:2 * n]
        send_sems, recv_sems = refs[2 * n:]
        x, y, c = _me()
        sib = (x, y, 1 - c)
        sends = []
        for a in range(n):
            cp = _remote(o_refs[a].at[c], o_refs[a].at[c], send_sems.at[a], recv_sems.at[a], sib)
            cp.start()
            sends.append(cp)
        for a in range(n):
            got = o_refs[a].at[1 - c]
            _remote(got, got, send_sems.at[a], recv_sems.at[a], sib).wait_recv()
        for cp in sends:
            cp.wait_send()

    return pl.pallas_call(
        body, name=name, in_specs=[HBM_SPEC] * n, out_specs=[HBM_SPEC] * n,
        out_shape=[SDS(h.shape, h.dtype) for h in halves],
        input_output_aliases={a: a for a in range(n)},
        scratch_shapes=[pltpu.SemaphoreType.DMA((n,)), pltpu.SemaphoreType.DMA((n,))],
    )(*halves)


N_DEV = 8


def sum_devices(s, slots, me):
    R, Wd = s.shape

    def body(me_ref, s_ref, slots_ref, out_ref):
        acc = None
        for d in range(N_DEV):
            mine = me_ref[0] == d
            term = jnp.where(mine, s_ref[...], slots_ref[jnp.where(mine, d ^ 1, d)])
            acc = term if acc is None else acc + term
        out_ref[...] = acc

    vmem = pl.BlockSpec(memory_space=pltpu.VMEM)
    return pl.pallas_call(
        body, name="sum_devices", in_specs=[pl.BlockSpec(memory_space=pltpu.SMEM), vmem, vmem], out_specs=vmem,
        out_shape=SDS((R, Wd), F32),
    )(me, s, slots)


TRANSPOSED = ("w_in",)
BIG = {"ffn1_w_up": (D, 2 * FF, "col"), "ffn1_w_down": (FF, D, "row"), "w_in": (INW, D, "row"),
       "w_attn_up": (AW, D, "row"), "w_pool_up": (PW, D, "col"), "w_out": (D, D, "row"),
       "ffn2_w_up": (D, 2 * FF, "col"), "ffn2_w_down": (FF, D, "row")}
GROUPS = (("ffn1_w_up", "ffn1_w_down"), ("w_in", "w_attn_up", "w_pool_up", "w_out"), ("ffn2_w_up", "ffn2_w_down"))
SMALL = ("ffn1_norm", "mix_norm", "ffn2_norm", "final_norm", "pool_scale", "sinks", "pool_w_mix")
SMALL_W = 128


def _rowlike(name):
    return BIG[name][2] == "row"


def _half_dims(name):
    k, n, kind = BIG[name]
    return (k // N_CHIPS // 2, n) if kind == "row" else (k // 2, n // N_CHIPS)


def shard_view(name, shard):
    return shard.reshape((2,) + _half_dims(name))


def full_from_view(name, fv):
    k, n, _ = BIG[name]
    return fv.reshape(k, n)


def grad_view(name, g):
    kh, ns = _half_dims(name)
    return g.reshape(_full_shape(jax.ShapeDtypeStruct((2, kh, ns), g.dtype), _rowlike(name)))


def pack_small(d):
    parts = []
    for name in SMALL:
        a = d[name].reshape(-1)
        pad = (-a.shape[0]) % SMALL_W
        parts.append(jnp.pad(a, (0, pad)).reshape(-1, SMALL_W))
    a = jnp.concatenate(parts, axis=0)
    return jnp.pad(a, ((0, (-a.shape[0]) % 8), (0, 0)))


def unpack_small(a, like):
    out, r0 = {}, 0
    for name in SMALL:
        size = int(np.prod(like[name].shape))
        rows = -(-size // SMALL_W)
        out[name] = a[r0:r0 + rows].reshape(-1)[:size].reshape(like[name].shape)
        r0 += rows
    return out


WEIGHTS = ("ffn1_norm", "ffn1_w_up", "ffn1_w_down", "mix_norm", "w_in", "sinks", "w_attn_up", "pool_w_mix",
           "pool_scale", "w_pool_up", "w_out", "ffn2_norm", "ffn2_w_up", "ffn2_w_down", "final_norm")


def kernel(x, ffn1_norm, ffn1_w_up, ffn1_w_down, mix_norm, w_in, sinks, w_attn_up, pool_w_mix, pool_scale, w_pool_up, w_out, ffn2_norm, ffn2_w_up, ffn2_w_down, final_norm, loss_target, m_ffn1_norm, m_ffn1_w_up, m_ffn1_w_down, m_mix_norm, m_w_in, m_sinks, m_w_attn_up, m_pool_w_mix, m_pool_scale, m_w_pool_up, m_w_out, m_ffn2_norm, m_ffn2_w_up, m_ffn2_w_down, m_final_norm, v_ffn1_norm, v_ffn1_w_up, v_ffn1_w_down, v_mix_norm, v_w_in, v_sinks, v_w_attn_up, v_pool_w_mix, v_pool_scale, v_w_pool_up, v_w_out, v_ffn2_norm, v_ffn2_w_up, v_ffn2_w_down, v_final_norm):
    given = dict(locals())
    w = {n: given[n] for n in WEIGHTS}
    m = {n: given["m_" + n] for n in WEIGHTS}
    v = {n: given["v_" + n] for n in WEIGHTS}
    cx, cy, cc = _me()
    place = jnp.stack([cc, 2 * cx + cy]).astype(jnp.int32)

    def local2d(d, n):
        return d[n][0].T if n in TRANSPOSED else d[n][0]

    shards = {n: local2d(w, n) for n in BIG}
    grads, delta, new_m, new_v = {}, {}, {}, {}
    rowlikes = [[_rowlike(n) for n in names] for names in GROUPS]

    class Exchanges:
        def __init__(self):
            self.gathers, self.forwards, self.fulls, self.reductions, self.small = {}, {}, {}, {}, None

        def _behind_first(self, a):
            return a + self.gathers[0]["token"][0, 0]

        def _start_gather(self, group, after):
            prepare = self._behind_first if group else (lambda a: a)
            sv = [shard_view(n, prepare(shards[n]).astype(BF16)) for n in GROUPS[group]]
            rl = rowlikes[group]
            self.gathers[group] = exchange_start(sv, [_full_shape(s, r) for s, r in zip(sv, rl)], _gather_plan(rl),
                                                 after, f"gather_start_{group}")

        def weights(self, group, after):
            if not self.gathers:
                self._start_gather(0, None)
                self.packed = [self._behind_first(pack_small(d)) for d in (w, m, v)]
                after = [self.gathers[0]["token"], *self.packed]
            if group not in self.fulls:
                names, rl, state = GROUPS[group], rowlikes[group], self.gathers[group]
                if group in self.forwards:
                    _, fulls = exchange_wait(self.forwards.pop(group), _forward_plan(rl), after,
                                             f"forward_wait_{group}", FORWARDS)
                else:
                    sv, fulls = exchange_wait(state, _gather_plan(rl), state["token"] if after is None else after,
                                              f"gather_wait_{group}")
                    fulls = gather_finish(sv, fulls, rl, f"gather_finish_{group}")
                self.fulls[group] = {n: full_from_view(n, f) for n, f in zip(names, fulls)}
                if group + 1 < len(GROUPS):
                    self._start_gather(group + 1, fulls[0])
            return self.fulls[group]

        def started(self):
            return self.gathers[max(self.gathers)]["token"]

        def prefetch(self, group, after):
            rl, state = rowlikes[group], self.gathers[group]
            sv, fulls = exchange_wait(state, _gather_plan(rl), after, f"gather_wait_{group}")
            self.forwards[group] = exchange_start(sv, fulls, _forward_plan(rl), None, f"forward_start_{group}", FORWARDS)
            return self.forwards[group]["token"]

        def grads(self, g):
            names = tuple(g)
            rl = [_rowlike(n) for n in names]
            gv = [grad_view(n, g[n]) for n in names]
            state = exchange_start(gv, [_half_buffer_shape(a, r) for a, r in zip(gv, rl)], _swap_plan(rl), None,
                                   "swap_start_" + names[0], SIBLING)
            self.reductions[names] = state
            return state["token"]

        def advance(self, names, after):
            rl = [_rowlike(n) for n in names]
            gv, gots = exchange_wait(self.reductions[names], _swap_plan(rl), after, "swap_wait_" + names[0], SIBLING)
            qs = [add_halves(a, b, r, place, "add_halves_" + n) for a, b, r, n in zip(gv, gots, rl, names)]
            state = exchange_start(qs, [_slot_shape(q, r) for q, r in zip(qs, rl)], _all_to_all_plan(rl), None,
                                   "all_to_all_start_" + names[0])
            self.reductions[names] = state
            return state["token"]

        def reduce(self, names, after):
            rl = [_rowlike(n) for n in names]
            qs, recvs = exchange_wait(self.reductions.pop(names), _all_to_all_plan(rl), after,
                                      "all_to_all_wait_" + names[0])
            halves = [sum_pieces(q, r, k, place, "sum_pieces_" + n) for q, r, k, n in zip(qs, recvs, rl, names)]
            return [(shards[n], o.reshape(shards[n].shape), local2d(m, n), local2d(v, n))
                    for n, o in zip(names, join_halves(halves, "join_halves_" + names[0]))]

        def updated(self, names, outs):
            for i, n in enumerate(names):
                grads[n], delta[n], new_m[n], new_v[n] = outs[4 * i:4 * i + 4]
            return [new_v[n] for n in names]

        def finish(self, names, after):
            updates = self.reduce(names, after)
            return self.updated(names, [o for n, u in zip(names, updates) for o in adamw(*u, "adamw_" + n)])

        def small_start(self, G, loss, after):
            packed = pack_small({n: G[n] for n in SMALL})
            used_rows = sum(-(-int(np.prod(small_like[n].shape)) // SMALL_W) for n in SMALL)
            assert packed.shape[0] > used_rows
            packed = packed.at[-1, 0].set(loss[0, 0])
            self.small = exchange_start([packed], [(N_DEV,) + packed.shape], _everyone_plan, after,
                                        "small_start", EVERYONE)
            return self.small["token"]

        def small_finish(self, after):
            (packed,), (slots,) = exchange_wait(self.small, _everyone_plan, after, "small_wait", EVERYONE)
            total = sum_devices(packed, slots, (4 * cx + 2 * cy + cc).astype(jnp.int32).reshape(1))
            self.loss = total[-1, 0]
            g, ds, ms, vs = adamw(self.packed[0], total, self.packed[1], self.packed[2], "adamw_small")
            for d, packed_d in ((grads, g), (delta, ds), (new_m, ms), (new_v, vs)):
                d.update(unpack_small(packed_d, small_like))
            return vs

    small_like = {n: w[n] for n in SMALL}
    S = {n: w[n].reshape(1, -1) for n in ("ffn1_norm", "mix_norm", "ffn2_norm", "final_norm", "pool_scale", "sinks")}
    S["pool_w_mix"] = w["pool_w_mix"][0].astype(BF16)
    exchanges = Exchanges()
    _, dx = local_fwd_bwd(x[0], loss_target[0], S, exchanges)
    loss = exchanges.loss

    def shaped(d, n):
        return (d[n].T if n in TRANSPOSED else d[n]).reshape(w[n].shape)

    return (loss, dx[None], *[shaped(grads, n) for n in WEIGHTS], *[shaped(delta, n) for n in WEIGHTS],
            *[shaped(new_m, n) for n in WEIGHTS], *[shaped(new_v, n) for n in WEIGHTS])
```

```python
import numpy as np
import jax
import jax.numpy as jnp
from jax import lax
from jax.experimental import pallas as pl
from jax.experimental.pallas import tpu as pltpu

F32 = jnp.float32
BF16 = jnp.bfloat16
SDS = jax.ShapeDtypeStruct
MESH = pl.DeviceIdType.MESH

D = 1024
FF = 2816
NQ = 16
NKV = 2
HD = 64
GQ = NQ // NKV
AW = NQ * HD
KVW = NKV * HD
BLK = 128
PW = 512
PG = 128
POOL_WINDOWS = (2, 4, 8, 16)
HALO = 16
INW = AW + 2 * KVW + PW + 2 * D
C_KV = AW
C_Z = AW + 2 * KVW
C_G = C_Z + PW
EPS = 1e-6
FF_CHUNK = 256
FF_CHUNKS = tuple((c, FF_CHUNK) for c in range(0, FF, FF_CHUNK))
SLOPES = tuple(float(2.0 ** (-8.0 * h / NQ)) for h in range(1, NQ + 1))
SCALE = HD ** -0.5

LR, B1, B2, ADAM_EPS, WD, STEP = 0.001, 0.9, 0.999, 1e-08, 0.01, 10

VMEM_LIMIT = 56 * 1024 * 1024
N_CHIPS = 4

NT = (((1,), (1,)), ((), ()))
TN = (((0,), (0,)), ((), ()))


def _cp(sem=None, vmem=VMEM_LIMIT):
    return pltpu.CompilerParams(dimension_semantics=sem, vmem_limit_bytes=vmem)


def _const_spec(shape):
    nd = len(shape)
    return pl.BlockSpec(shape, lambda *_: (0,) * nd, pipeline_mode=pl.Buffered(1))


def _rstd(x):
    return lax.rsqrt(jnp.mean(x * x, axis=-1, keepdims=True) + EPS)


def _rms_bwd(dn, xhat, rstd, g):
    dxhat = dn * g
    return rstd * (dxhat - xhat * jnp.mean(dxhat * xhat, axis=-1, keepdims=True))


def _dot(a, b):
    return jnp.dot(a, b, preferred_element_type=F32)


def _dot_nt(a, b):
    return lax.dot_general(a, b, NT, preferred_element_type=F32)


def _dot_tn(a, b):
    return lax.dot_general(a, b, TN, preferred_element_type=F32)


def ffn_fwd(h, g, wup, wdn, name, head=None):
    T = h.shape[0]
    TM = 512
    tile = lambda w: pl.BlockSpec((TM, w), lambda i: (i, 0))
    acc_spec = lambda w: pl.BlockSpec((1, w), lambda i: (0, 0))

    def body(h_ref, g_ref, wup_ref, wdn_ref, *rest):
        out_ref, ab_ref, hid_ref = rest[-3:] if head is None else rest[2:5]
        x = h_ref[...]
        n = (x * _rstd(x) * g_ref[...]).astype(BF16)
        for c0, w in FF_CHUNKS:
            a = _dot(n, wup_ref[:, c0:c0 + w])
            b = _dot(n, wup_ref[:, FF + c0:FF + c0 + w])
            sig = jax.nn.sigmoid(a)
            s = a * sig
            ab_ref[:, c0:c0 + w] = (b * (sig * (1.0 + a * (1.0 - sig)))).astype(BF16)
            ab_ref[:, FF + c0:FF + c0 + w] = s.astype(BF16)
            hid_ref[:, c0:c0 + w] = (s * b).astype(BF16)
        out = x + 0.5 * _dot(hid_ref[...], wdn_ref[...])
        if head is None:
            out_ref[...] = out
        else:
            t_ref, gf_ref, loss_ref, dgf_ref = rest[0], rest[1], rest[5], rest[6]
            out_ref[...] = _loss_head(out, t_ref[...], gf_ref[...], loss_ref, dgf_ref, pl.program_id(0) == 0)

    head_in, head_specs, head_out_specs, head_out_shape = [], [], [], []
    if head is not None:
        head_in, head_specs = list(head), [tile(D), _const_spec((1, D))]
        head_out_specs, head_out_shape = [acc_spec(1), acc_spec(D)], [SDS((1, 1), F32), SDS((1, D), F32)]
    return pl.pallas_call(
        body, name=name, grid=(T // TM,),
        in_specs=[tile(D), _const_spec((1, D)), _const_spec((D, 2 * FF)), _const_spec((FF, D))] + head_specs,
        out_specs=[tile(D), tile(2 * FF), tile(FF)] + head_out_specs,
        out_shape=[SDS((T, D), F32), SDS((T, 2 * FF), BF16), SDS((T, FF), BF16)] + head_out_shape,
        compiler_params=_cp(("arbitrary",)),
    )(h, g, wup, wdn, *head_in)


def _loss_head(x, target, g, loss_ref, dg_ref, first):
    @pl.when(first)
    def _():
        loss_ref[...] = jnp.zeros_like(loss_ref)
        dg_ref[...] = jnp.zeros_like(dg_ref)

    rstd = _rstd(x)
    xhat = x * rstd
    err = xhat * g - target
    loss_ref[...] += 0.5 * jnp.sum(jnp.mean(err * err, axis=-1, keepdims=True), axis=0, keepdims=True)
    dy = err * (1.0 / D)
    dg_ref[...] += jnp.sum(dy * xhat, axis=0, keepdims=True)
    return _rms_bwd(dy, xhat, rstd, g)


def _adamw_tile(w_ref, g_ref, m_ref, v_ref, go_ref, d_ref, nm_ref, nv_ref):
    gv = g_ref[...]
    go_ref[...] = gv
    nm = B1 * m_ref[...] + (1.0 - B1) * gv
    nv = B2 * v_ref[...] + (1.0 - B2) * (gv * gv)
    nm_ref[...] = nm
    nv_ref[...] = nv
    d_ref[...] = -LR * ((nm / (1.0 - B1 ** STEP)) / (jnp.sqrt(nv / (1.0 - B2 ** STEP)) + ADAM_EPS) + WD * w_ref[...])


def _riders(updates, steps, step_of):
    in_specs, out_specs, out_shapes, operands, tiles = [], [], [], [], []
    for w, g, m, v in updates:
        R, C = w.shape
        n = max(d for d in range(1, steps + 1) if R % d == 0 and (R // d) % 8 == 0)
        spec = pl.BlockSpec((R // n, C), lambda *ids, n=n: (jnp.minimum(step_of(*ids), n - 1), 0))
        in_specs += [spec] * 4
        out_specs += [spec] * 4
        out_shapes += [SDS((R, C), F32)] * 4
        operands += [w, g, m, v]
        tiles.append(n)

    def run(step, in_refs, out_refs):
        for u, n in enumerate(tiles):
            @pl.when(step < n)
            def _(u=u):
                _adamw_tile(*in_refs[4 * u:4 * u + 4], *out_refs[4 * u:4 * u + 4])

    return in_specs, out_specs, out_shapes, operands, run


def ffn_bwd_x(dh, h_in, ab, g, wup, wdn, name, updates=()):
    T = dh.shape[0]
    TM = 256
    r_in, r_out, r_shapes, r_args, ride = _riders(updates, T // TM, lambda i: i)

    def body(dh_ref, h_ref, ab_ref, g_ref, wup_ref, wdn_ref, *rest):
        dhin_ref, dup_ref, n_ref, dg_ref = rest[len(r_in):len(r_in) + 4]
        ride(pl.program_id(0), rest[:len(r_in)], rest[len(r_in) + 4:])
        x = h_ref[...]
        g = g_ref[...]
        rstd = _rstd(x)
        xhat = x * rstd
        n_ref[...] = (xhat * g).astype(BF16)
        dh = dh_ref[...]
        dhh = (0.5 * dh).astype(BF16)
        for c0, w in FF_CHUNKS:
            dhid = _dot_nt(dhh, wdn_ref[c0:c0 + w, :])
            dup_ref[:, c0:c0 + w] = (dhid * ab_ref[:, c0:c0 + w].astype(F32)).astype(BF16)
            dup_ref[:, FF + c0:FF + c0 + w] = (dhid * ab_ref[:, FF + c0:FF + c0 + w].astype(F32)).astype(BF16)
        dn = _dot_nt(dup_ref[...], wup_ref[...])
        dhin_ref[...] = dh + _rms_bwd(dn, xhat, rstd, g)

        @pl.when(pl.program_id(0) == 0)
        def _():
            dg_ref[...] = jnp.zeros_like(dg_ref)

        dg_ref[...] += jnp.sum(dn * xhat, axis=0, keepdims=True)

    tile = lambda w: pl.BlockSpec((TM, w), lambda i: (i, 0))
    return pl.pallas_call(
        body, name=name, grid=(T // TM,),
        in_specs=[tile(D), tile(D), tile(2 * FF), _const_spec((1, D)), _const_spec((D, 2 * FF)), _const_spec((FF, D))]
        + r_in,
        out_specs=[tile(D), tile(2 * FF), tile(D), pl.BlockSpec((1, D), lambda i: (0, 0))] + r_out,
        out_shape=[SDS((T, D), F32), SDS((T, 2 * FF), BF16), SDS((T, D), BF16), SDS((1, D), F32)] + r_shapes,
        compiler_params=_cp(("arbitrary",)),
    )(dh, h_in, ab, g, wup, wdn, *r_args)


TOKEN_SPEC = pl.BlockSpec((8, 128), lambda *_: (0, 0))


def _token_operand(token):
    return ([], []) if token is None else ([TOKEN_SPEC], [token])


def matmul_tn(a, b, name, *, tm, tn, tt=1024, b_scale=None, after=None):
    T, M = a.shape
    N = b.shape[1]
    tt = min(tt, T)
    assert M % tm == 0 and N % tn == 0 and T % tt == 0
    nt = T // tt
    token_spec, token_arg = _token_operand(after)

    def body(a_ref, b_ref, *rest):
        o_ref, acc_ref = rest[-2:]
        t = pl.program_id(2)

        @pl.when(t == 0)
        def _():
            acc_ref[...] = jnp.zeros_like(acc_ref)

        bv = b_ref[...]
        if b_scale is not None:
            bv = bv * b_scale
        acc_ref[...] += _dot_tn(a_ref[...].astype(BF16), bv.astype(BF16))

        @pl.when(t == nt - 1)
        def _():
            o_ref[...] = acc_ref[...].astype(BF16)

    return pl.pallas_call(
        body, name=name, grid=(M // tm, N // tn, nt),
        in_specs=[pl.BlockSpec((tt, tm), lambda i, j, t: (t, i)), pl.BlockSpec((tt, tn), lambda i, j, t: (t, j))]
        + token_spec,
        out_specs=pl.BlockSpec((tm, tn), lambda i, j, t: (i, j)),
        out_shape=SDS((M, N), BF16),
        scratch_shapes=[pltpu.VMEM((tm, tn), F32)],
        compiler_params=_cp(("parallel", "parallel", "arbitrary")),
    )(a, b, *token_arg)


def mix_in_fwd(h1, g, win_t):
    T = h1.shape[0]
    TM = 512

    def body(h_ref, g_ref, w_ref, u_ref, q_ref, kv_ref, z_ref, gt_ref):
        x = h_ref[...]
        u = (x * _rstd(x) * g_ref[...]).astype(BF16)
        u_ref[...] = u
        for c in range(0, AW, 256):
            q_ref[:, c:c + 256] = _dot_nt(u, w_ref[c:c + 256, :]).astype(BF16)
        kv_ref[...] = _dot_nt(u, w_ref[C_KV:C_Z, :]).astype(BF16)
        for c in range(0, PW, 256):
            z_ref[:, c:c + 256] = _dot_nt(u, w_ref[C_Z + c:C_Z + c + 256, :])
        for c in range(0, 2 * D, 256):
            gt_ref[:, c:c + 256] = _dot_nt(u, w_ref[C_G + c:C_G + c + 256, :]).astype(BF16)

    tile = lambda w: pl.BlockSpec((TM, w), lambda i: (i, 0))
    return pl.pallas_call(
        body, name="mix_in_fwd", grid=(T // TM,),
        in_specs=[tile(D), _const_spec((1, D)), _const_spec((INW, D))],
        out_specs=[tile(D), tile(AW), tile(2 * KVW), tile(PW), tile(2 * D)],
        out_shape=[SDS((T, D), BF16), SDS((T, AW), BF16), SDS((T, 2 * KVW), BF16), SDS((T, PW), F32),
                   SDS((T, 2 * D), BF16)],
        compiler_params=_cp(("arbitrary",)),
    )(h1, g, win_t)


def mix_in_bwd(dq, dkv, dz, dgt, dh2, h1, g, win_t):
    T = h1.shape[0]
    TM = min(1024, T)
    SUB = 256

    def body(dq_ref, dkv_ref, dz_ref, dgt_ref, dh2_ref, h_ref, g_ref, w_ref, dh1_ref, dg_ref):
        g = g_ref[...]
        dg = None
        for r0 in range(0, TM, SUB):
            rows = slice(r0, r0 + SUB)
            du = _dot(dq_ref[rows, :], w_ref[0:AW, :])
            du += _dot(dkv_ref[rows, :], w_ref[C_KV:C_Z, :])
            du += _dot(dz_ref[rows, :], w_ref[C_Z:C_G, :])
            du += _dot(dgt_ref[rows, :], w_ref[C_G:INW, :])
            x = h_ref[rows, :]
            rstd = _rstd(x)
            xhat = x * rstd
            dh1_ref[rows, :] = dh2_ref[rows, :] + _rms_bwd(du, xhat, rstd, g)
            part = jnp.sum(du * xhat, axis=0, keepdims=True)
            dg = part if dg is None else dg + part

        @pl.when(pl.program_id(0) == 0)
        def _():
            dg_ref[...] = jnp.zeros_like(dg_ref)

        dg_ref[...] += dg

    tile = lambda w: pl.BlockSpec((TM, w), lambda i: (i, 0))
    return pl.pallas_call(
        body, name="mix_in_bwd", grid=(T // TM,),
        in_specs=[tile(AW), tile(2 * KVW), tile(PW), tile(2 * D), tile(D), tile(D), _const_spec((1, D)),
                  _const_spec((INW, D))],
        out_specs=[tile(D), pl.BlockSpec((1, D), lambda i: (0, 0))],
        out_shape=[SDS((T, D), F32), SDS((1, D), F32)],
        compiler_params=_cp(("arbitrary",)),
    )(dq, dkv, dz, dgt, dh2, h1, g, win_t)


PAIR = 2 * HD
NPAIR = GQ // 2


def _lo_lanes():
    return lax.broadcasted_iota(jnp.int32, (BLK, PAIR), 1) < HD


def _stack_heads(ref, kvh, scale=None):
    lo = _lo_lanes()
    parts = []
    for pr in range(NPAIR):
        t = ref[:, (kvh * NPAIR + pr) * PAIR:(kvh * NPAIR + pr + 1) * PAIR]
        if scale is not None:
            t = t * scale
        zero = jnp.zeros_like(t)
        parts += [jnp.where(lo, t, zero), jnp.where(lo, zero, t)]
    return jnp.concatenate(parts, axis=0)


def _kv_tiles(kvc_ref, kvp_ref, tile, kvh):
    lo = _lo_lanes()
    dup, left, right = [], [], []
    for ref in (kvp_ref, kvc_ref):
        t = ref[:, tile * PAIR:(tile + 1) * PAIR]
        r = pltpu.roll(t.astype(F32), HD, 1).astype(BF16)
        zero = jnp.zeros_like(t)
        a, b = (t, r) if kvh == 0 else (r, t)
        dup.append(jnp.where(lo, a, b))
        left.append(jnp.where(lo, a, zero))
        right.append(jnp.where(lo, zero, b))
    cat = lambda xs: jnp.concatenate(xs, axis=0)
    return cat(dup), cat(left), cat(right)


def _band_consts(first):
    row = lax.broadcasted_iota(jnp.int32, (BLK, BLK), 0)
    col = lax.broadcasted_iota(jnp.int32, (BLK, BLK), 1)
    upper = col > row
    dist = jnp.where(upper, row - col + BLK, row - col).astype(F32)
    pen = jnp.where(jnp.logical_and(upper, first), -jnp.inf, 0.0)
    return upper, dist, pen


def _split_band(upper, t):
    zero = jnp.zeros_like(t)
    return jnp.concatenate([jnp.where(upper, t, zero), jnp.where(upper, zero, t)], axis=1)


def attn_fwd(q, kv, sinks):
    T = q.shape[0]
    nb = T // BLK

    def body(sink_ref, q_ref, kvc_ref, kvp_ref, att_ref, lse_ref):
        upper, dist, pen = _band_consts(pl.program_id(0) == 0)
        scores, values = [], []
        for kvh in range(NKV):
            kdup, _, _ = _kv_tiles(kvc_ref, kvp_ref, 0, kvh)
            values.append(_kv_tiles(kvc_ref, kvp_ref, 1, kvh)[1:])
            scores.append(_dot_nt(_stack_heads(q_ref, kvh, SCALE), kdup))
        for kvh in range(NKV):
            s_all = scores[kvh]
            vleft, vright = values[kvh]
            for pr in range(NPAIR):
                outs, inv = [], []
                for side, vpad in ((0, vleft), (1, vright)):
                    g = 2 * pr + side
                    hq = kvh * GQ + g
                    sink = sink_ref[0, hq]
                    rows = slice(g * BLK, (g + 1) * BLK)
                    s = jnp.where(upper, s_all[rows, 0:BLK], s_all[rows, BLK:2 * BLK]) - SLOPES[hq] * dist + pen
                    m = jnp.maximum(jnp.max(s, axis=-1, keepdims=True), sink)
                    p = jnp.exp(s - m)
                    l = jnp.sum(p, axis=-1, keepdims=True) + jnp.exp(sink - m)
                    lse_ref[:, hq:hq + 1] = m + jnp.log(l)
                    outs.append(_dot(_split_band(upper, p.astype(BF16)), vpad))
                    inv.append(1.0 / l)
                col0 = (kvh * NPAIR + pr) * PAIR
                att_ref[:, col0:col0 + PAIR] = ((outs[0] + outs[1]) * jnp.where(_lo_lanes(), inv[0], inv[1])).astype(BF16)

    return pl.pallas_call(
        body, name="attn_fwd", grid=(nb,),
        in_specs=[pl.BlockSpec(memory_space=pltpu.SMEM),
                  pl.BlockSpec((BLK, AW), lambda i: (i, 0)),
                  pl.BlockSpec((BLK, 2 * KVW), lambda i: (i, 0)),
                  pl.BlockSpec((BLK, 2 * KVW), lambda i: (jnp.maximum(i - 1, 0), 0))],
        out_specs=[pl.BlockSpec((BLK, AW), lambda i: (i, 0)), pl.BlockSpec((BLK, NQ), lambda i: (i, 0))],
        out_shape=[SDS((T, AW), BF16), SDS((T, NQ), F32)],
        compiler_params=_cp(("arbitrary",)),
    )(sinks, q, kv, kv)


def attn_bwd(q, kv, datt, att, lse, sinks):
    T = q.shape[0]
    nb = T // BLK

    def body(sink_ref, q_ref, kvc_ref, kvp_ref, do_ref, out_ref, lse_ref, dq_ref, dkv_ref, dsink_ref, carry_ref):
        i = pl.program_id(0)

        @pl.when(i == 0)
        def _():
            dsink_ref[...] = jnp.zeros_like(dsink_ref)
            carry_ref[...] = jnp.zeros_like(carry_ref)

        @pl.when(i < nb)
        def _():
            upper, dist, pen = _band_consts(i == 0)
            lo = _lo_lanes()
            dk_dup, dv_dup = [], []
            staged = []
            for kvh in range(NKV):
                kdup, kleft, kright = _kv_tiles(kvc_ref, kvp_ref, 0, kvh)
                vdup, _, _ = _kv_tiles(kvc_ref, kvp_ref, 1, kvh)
                qs = _stack_heads(q_ref, kvh, SCALE)
                dos = _stack_heads(do_ref, kvh)
                staged.append((kleft, kright, qs, dos, _dot_nt(qs, kdup), _dot_nt(dos, vdup)))
            deltas = []
            for pair in range(NQ // 2):
                cols = slice(pair * PAIR, (pair + 1) * PAIR)
                t = do_ref[:, cols].astype(F32) * out_ref[:, cols].astype(F32)
                deltas += [jnp.sum(jnp.where(lo, t, 0.0), axis=-1, keepdims=True),
                           jnp.sum(jnp.where(lo, 0.0, t), axis=-1, keepdims=True)]
            for kvh in range(NKV):
                kleft, kright, qs, dos, s_all, dp_all = staged[kvh]
                ds_parts, p_parts = [], []
                for pr in range(NPAIR):
                    dq = None
                    for side, kpad in ((0, kleft), (1, kright)):
                        g = 2 * pr + side
                        hq = kvh * GQ + g
                        lse_h = lse_ref[:, hq:hq + 1]
                        rows = slice(g * BLK, (g + 1) * BLK)
                        s = jnp.where(upper, s_all[rows, 0:BLK], s_all[rows, BLK:2 * BLK]) - SLOPES[hq] * dist + pen
                        p = jnp.exp(s - lse_h)
                        dp = jnp.where(upper, dp_all[rows, 0:BLK], dp_all[rows, BLK:2 * BLK])
                        delta = deltas[hq]
                        dsink_ref[:, hq:hq + 1] += -jnp.sum(jnp.exp(sink_ref[0, hq] - lse_h) * delta, axis=0,
                                                            keepdims=True)
                        ds = _split_band(upper, (p * (dp - delta)).astype(BF16))
                        ds_parts.append(ds)
                        p_parts.append(_split_band(upper, p.astype(BF16)))
                        d = _dot(ds, kpad)
                        dq = d if dq is None else dq + d
                    col0 = (kvh * NPAIR + pr) * PAIR
                    dq_ref[:, col0:col0 + PAIR] = (dq * SCALE).astype(BF16)
                dkw = _dot_tn(qs, jnp.concatenate(ds_parts, axis=0)).T
                dvw = _dot_tn(dos, jnp.concatenate(p_parts, axis=0)).T
                dk_dup.append(dkw + pltpu.roll(dkw, HD, 1))
                dv_dup.append(dvw + pltpu.roll(dvw, HD, 1))
            dk = jnp.where(jnp.concatenate([lo, lo], axis=0), dk_dup[0], dk_dup[1])
            dv = jnp.where(jnp.concatenate([lo, lo], axis=0), dv_dup[0], dv_dup[1])
            dkv_ref[:, 0:PAIR] = (carry_ref[:, 0:PAIR] + dk[0:BLK]).astype(BF16)
            dkv_ref[:, PAIR:2 * PAIR] = (carry_ref[:, PAIR:2 * PAIR] + dv[0:BLK]).astype(BF16)
            carry_ref[:, 0:PAIR] = dk[BLK:2 * BLK]
            carry_ref[:, PAIR:2 * PAIR] = dv[BLK:2 * BLK]

        @pl.when(i == nb)
        def _():
            dkv_ref[...] = carry_ref[...].astype(BF16)

    cur = lambda i: (jnp.minimum(i, nb - 1), 0)
    prev = lambda i: (jnp.maximum(jnp.minimum(i, nb - 1) - 1, 0), 0)
    return pl.pallas_call(
        body, name="attn_bwd", grid=(nb + 1,),
        in_specs=[pl.BlockSpec(memory_space=pltpu.SMEM),
                  pl.BlockSpec((BLK, AW), cur), pl.BlockSpec((BLK, 2 * KVW), cur), pl.BlockSpec((BLK, 2 * KVW), prev),
                  pl.BlockSpec((BLK, AW), cur), pl.BlockSpec((BLK, AW), cur), pl.BlockSpec((BLK, NQ), cur)],
        out_specs=[pl.BlockSpec((BLK, AW), cur),
                   pl.BlockSpec((BLK, 2 * KVW), lambda i: (jnp.maximum(i - 1, 0), 0)),
                   pl.BlockSpec((1, NQ), lambda i: (0, 0))],
        out_shape=[SDS((T, AW), BF16), SDS((T, 2 * KVW), BF16), SDS((1, NQ), F32)],
        scratch_shapes=[pltpu.VMEM((BLK, 2 * KVW), F32)],
        compiler_params=_cp(("arbitrary",)),
    )(sinks, q, kv, kv, datt, att, lse)


def _inv_counts(t0, rows):
    t = (t0 + lax.broadcasted_iota(jnp.int32, (rows, 1), 0) + 1).astype(F32)
    return [1.0 / jnp.minimum(t, float(w)) for w in POOL_WINDOWS]


def pool_fwd(z, wmix, scale):
    T = z.shape[0]
    TM = min(1024, T)
    L = TM + HALO

    def body(z_ref, halo_ref, wmix_ref, scale_ref, pooled_ref, mixs_ref):
        i = pl.program_id(0)
        halo = jnp.where(i > 0, halo_ref[...], 0.0)
        zt = z_ref[...]
        e = jnp.concatenate([halo, zt], axis=0)
        sums = []
        s = e
        for k in (1, 2, 4, 8):
            s = s + pltpu.roll(s, k, 0)
            sums.append(s)
        inv = _inv_counts(i * TM, TM)
        for gi in range(len(POOL_WINDOWS)):
            cols = slice(gi * PG, (gi + 1) * PG)
            pooled = (sums[gi][HALO:, cols] * inv[gi] - zt[:, cols]).astype(BF16)
            pooled_ref[:, cols] = pooled
            mixs_ref[:, cols] = (_dot(pooled, wmix_ref[gi]) * scale_ref[:, cols]).astype(BF16)

    return pl.pallas_call(
        body, name="pool_fwd", grid=(T // TM,),
        in_specs=[pl.BlockSpec((TM, PW), lambda i: (i, 0)),
                  pl.BlockSpec((HALO, PW), lambda i: (jnp.maximum(i * (TM // HALO) - 1, 0), 0)),
                  _const_spec((len(POOL_WINDOWS), PG, PG)), _const_spec((1, PW))],
        out_specs=[pl.BlockSpec((TM, PW), lambda i: (i, 0)), pl.BlockSpec((TM, PW), lambda i: (i, 0))],
        out_shape=[SDS((T, PW), BF16), SDS((T, PW), BF16)],
        compiler_params=_cp(("arbitrary",)),
    )(z, z, wmix, scale)


def pool_bwd(dmixs, pooled, wmix, scale):
    T = dmixs.shape[0]
    TM = min(1024, T)
    L = TM + HALO
    nt = T // TM

    def body(dm_ref, halo_ref, pooled_ref, wmix_ref, scale_ref, dz_ref, dwmix_ref, dscale_ref):
        i = pl.program_id(0)

        @pl.when(i == 0)
        def _():
            dwmix_ref[...] = jnp.zeros_like(dwmix_ref)
            dscale_ref[...] = jnp.zeros_like(dscale_ref)

        halo = jnp.where(i < nt - 1, halo_ref[...], 0.0)
        dm = dm_ref[...]
        e = jnp.concatenate([dm, halo], axis=0)
        inv = _inv_counts(i * TM, L)
        for gi in range(len(POOL_WINDOWS)):
            cols = slice(gi * PG, (gi + 1) * PG)
            w = wmix_ref[gi]
            dmixed = (e[:, cols] * scale_ref[:, cols]).astype(BF16)
            dpooled = _dot_nt(dmixed, w)
            pooled = pooled_ref[:, cols]
            mixed = _dot(pooled, w)
            dscale_ref[:, cols] += jnp.sum(dm[:, cols] * mixed, axis=0, keepdims=True)
            dwmix_ref[gi] += _dot_tn(pooled, dmixed[:TM])
            s = dpooled * inv[gi]
            k = 1
            while k < POOL_WINDOWS[gi]:
                s = s + pltpu.roll(s, L - k, 0)
                k *= 2
            dz_ref[:, cols] = (s[:TM] - dpooled[:TM]).astype(BF16)

    return pl.pallas_call(
        body, name="pool_bwd", grid=(nt,),
        in_specs=[pl.BlockSpec((TM, PW), lambda i: (i, 0)),
                  pl.BlockSpec((HALO, PW), lambda i: (jnp.minimum((i + 1) * (TM // HALO), T // HALO - 1), 0)),
                  pl.BlockSpec((TM, PW), lambda i: (i, 0)),
                  _const_spec((len(POOL_WINDOWS), PG, PG)), _const_spec((1, PW))],
        out_specs=[pl.BlockSpec((TM, PW), lambda i: (i, 0)),
                   pl.BlockSpec((len(POOL_WINDOWS), PG, PG), lambda i: (0, 0, 0)),
                   pl.BlockSpec((1, PW), lambda i: (0, 0))],
        out_shape=[SDS((T, PW), BF16), SDS((len(POOL_WINDOWS), PG, PG), F32), SDS((1, PW), F32)],
        compiler_params=_cp(("arbitrary",)),
    )(dmixs, dmixs, pooled, wmix, scale)


def merge_fwd(att, mixs, gt, h1, wattn, wpool, wout):
    T = h1.shape[0]
    TM = 512

    def body(att_ref, mixs_ref, gt_ref, h_ref, wa_ref, wp_ref, wo_ref, h2_ref, mg_ref, gf_ref):
        a = _dot(att_ref[...], wa_ref[...])
        p = _dot(mixs_ref[...], wp_ref[...])
        sa = jax.nn.sigmoid(gt_ref[:, 0:D].astype(F32))
        sp = jax.nn.sigmoid(gt_ref[:, D:2 * D].astype(F32))
        gf_ref[:, 0:D] = (a * sa * (1.0 - sa)).astype(BF16)
        gf_ref[:, D:2 * D] = (p * sp * (1.0 - sp)).astype(BF16)
        mg = (sa * a + sp * p).astype(BF16)
        mg_ref[...] = mg
        h2_ref[...] = h_ref[...] + _dot(mg, wo_ref[...])

    tile = lambda w: pl.BlockSpec((TM, w), lambda i: (i, 0))
    return pl.pallas_call(
        body, name="merge_fwd", grid=(T // TM,),
        in_specs=[tile(AW), tile(PW), tile(2 * D), tile(D), _const_spec((AW, D)), _const_spec((PW, D)),
                  _const_spec((D, D))],
        out_specs=[tile(D), tile(D), tile(2 * D)],
        out_shape=[SDS((T, D), F32), SDS((T, D), BF16), SDS((T, 2 * D), BF16)],
        compiler_params=_cp(("arbitrary",)),
    )(att, mixs, gt, h1, wattn, wpool, wout)


def merge_bwd(dh2, gt, gf, wattn, wpool, wout, after=None):
    T = dh2.shape[0]
    TM = 512
    token_spec, token_arg = _token_operand(after)

    def body(dh2_ref, gt_ref, gf_ref, wa_ref, wp_ref, wo_ref, *rest):
        datt_ref, dmixs_ref, dgt_ref, da_ref, dp_ref = rest[-5:]
        dm = _dot_nt(dh2_ref[...].astype(BF16), wo_ref[...])
        da = (dm * jax.nn.sigmoid(gt_ref[:, 0:D].astype(F32))).astype(BF16)
        dp = (dm * jax.nn.sigmoid(gt_ref[:, D:2 * D].astype(F32))).astype(BF16)
        da_ref[...] = da
        dp_ref[...] = dp
        dgt_ref[:, 0:D] = (dm * gf_ref[:, 0:D].astype(F32)).astype(BF16)
        dgt_ref[:, D:2 * D] = (dm * gf_ref[:, D:2 * D].astype(F32)).astype(BF16)
        datt_ref[...] = _dot_nt(da, wa_ref[...]).astype(BF16)
        dmixs_ref[...] = _dot_nt(dp, wp_ref[...])

    tile = lambda w: pl.BlockSpec((TM, w), lambda i: (i, 0))
    return pl.pallas_call(
        body, name="merge_bwd", grid=(T // TM,),
        in_specs=[tile(D), tile(2 * D), tile(2 * D), _const_spec((AW, D)), _const_spec((PW, D)),
                  _const_spec((D, D))] + token_spec,
        out_specs=[tile(AW), tile(PW), tile(2 * D), tile(D), tile(D)],
        out_shape=[SDS((T, AW), BF16), SDS((T, PW), F32), SDS((T, 2 * D), BF16), SDS((T, D), BF16),
                   SDS((T, D), BF16)],
        compiler_params=_cp(("arbitrary",)),
    )(dh2, gt, gf, wattn, wpool, wout, *token_arg)


def adamw(updates, place, name):
    tile_bytes = 512 * 1024
    jobs = []
    for w, g, m, v in updates:
        R, C = w.shape
        tr = R
        if R * C * 4 > tile_bytes:
            tr = next(cand for cand in (512, 256, 128, 64, 32, 16, 8) if R % cand == 0 and cand * C * 4 <= tile_bytes)
        blk = ((tr, C), lambda ls, p: (ls, 0))
        jobs.append((R // tr, [(a, *blk) for a in (w, g, m, v)], [((R, C), F32, *blk)] * 4,
                     lambda ins, outs: _adamw_tile(*ins, *outs)))
    return _run_jobs(jobs, place, name)


GROUP_FFN1, GROUP_MIX, GROUP_FFN2 = 0, 1, 2


def _behind(small, token):
    return small if token is None else small + token[0:1, 0:1]


def local_fwd_bwd(x, target, S, comm):
    W = comm.weights(GROUP_FFN1, None)
    h1, ab1, hid1 = ffn_fwd(x, _behind(S["ffn1_norm"], comm.started()), W["ffn1_w_up"], W["ffn1_w_down"], "ffn1_fwd")
    W = comm.weights(GROUP_MIX, h1)
    u, q, kv, z, gt = mix_in_fwd(h1, _behind(S["mix_norm"], comm.started()), W["w_in"])
    att, lse = attn_fwd(q, kv, S["sinks"])
    pooled, mixs = pool_fwd(z, S["pool_w_mix"], _behind(S["pool_scale"], comm.prefetch(GROUP_FFN2, att)))
    h2, merged, gf = merge_fwd(att, mixs, gt, h1, W["w_attn_up"], W["w_pool_up"], W["w_out"])
    W = comm.weights(GROUP_FFN2, h2)
    dh3, ab2, hid2, loss, g_final = ffn_fwd(h2, S["ffn2_norm"], W["ffn2_w_up"], W["ffn2_w_down"], "ffn2_fwd",
                                            head=(target, S["final_norm"]))

    G = {"final_norm": g_final}
    dh2, dup2, n2, G["ffn2_norm"] = ffn_bwd_x(dh3, h2, ab2, S["ffn2_norm"], W["ffn2_w_up"], W["ffn2_w_down"],
                                              "ffn2_bwd_x")
    ffn2 = ("ffn2_w_down", "ffn2_w_up")
    token = comm.grads({"ffn2_w_down": matmul_tn(hid2, dh3, "ffn2_dw_down", tm=1408, tn=512, tt=4096, b_scale=0.5),
                        "ffn2_w_up": matmul_tn(n2, dup2, "ffn2_dw_up", tm=D, tn=512, tt=4096)})

    W = comm.weights(GROUP_MIX, None)
    datt, dmixs, dgt, da, dp = merge_bwd(dh2, gt, gf, W["w_attn_up"], W["w_pool_up"], W["w_out"], after=token)
    token = comm.advance(ffn2, datt)
    g_mix = {"w_out": matmul_tn(merged, dh2, "dw_out", tm=D, tn=D, after=token),
             "w_attn_up": matmul_tn(att, da, "dw_attn_up", tm=AW, tn=D),
             "w_pool_up": matmul_tn(mixs, dp, "dw_pool_up", tm=PW, tn=D)}
    dz, G["pool_w_mix"], G["pool_scale"] = pool_bwd(dmixs, pooled, S["pool_w_mix"], _behind(S["pool_scale"], token))
    dq, dkv, G["sinks"] = attn_bwd(q, kv, datt, att, lse, _behind(S["sinks"], token))
    dh1, G["mix_norm"] = mix_in_bwd(dq, dkv, dz, dgt, dh2, h1, S["mix_norm"], W["w_in"])
    g_mix["w_in"] = jnp.concatenate([
        matmul_tn(dq, u, "dw_in_q", tm=AW, tn=D),
        matmul_tn(dkv, u, "dw_in_kv", tm=2 * KVW, tn=D),
        matmul_tn(dz, u, "dw_in_z", tm=PW, tn=D),
        matmul_tn(dgt, u, "dw_in_g", tm=D, tn=512, tt=4096),
    ], axis=0)
    mix = tuple(g_mix)
    token = comm.grads(g_mix)

    W = comm.weights(GROUP_FFN1, None)
    g_dn1 = matmul_tn(hid1, dh1, "ffn1_dw_down", tm=1408, tn=512, tt=4096, b_scale=0.5, after=token)
    token = comm.advance(mix, g_dn1)
    token2 = comm.grads({"ffn1_w_down": g_dn1})
    g1 = _behind(_behind(S["ffn1_norm"], token), token2)
    dx, dup1, n1, G["ffn1_norm"], *outs = ffn_bwd_x(dh1, x, ab1, g1, W["ffn1_w_up"], W["ffn1_w_down"], "ffn1_bwd_x",
                                                   updates=comm.reduce(ffn2, token2))
    comm.updated(ffn2, outs)
    token = comm.advance(("ffn1_w_down",), dx)
    token2 = comm.small_start(G, loss, dx)
    g_up1 = matmul_tn(n1, dup1, "ffn1_dw_up", tm=D, tn=512, tt=4096,
                      after=None if token is None else token + token2)
    token = comm.grads({"ffn1_w_up": g_up1})
    done = comm.finish(("ffn1_w_down",), token)
    token = comm.advance(("ffn1_w_up",), done)
    done = comm.finish(mix, token)
    done = comm.small_finish(done)
    comm.finish(("ffn1_w_up",), done)
    return loss, dx


HBM_SPEC = pl.BlockSpec(memory_space=pltpu.HBM)


def _me():
    return lax.axis_index("x"), lax.axis_index("y"), lax.axis_index("c")


def _peer_chip(x, y, k):
    return x ^ (k >> 1), y ^ (k & 1)


def _piece_half(ref, rowlike, j, h):
    if rowlike:
        return ref.at[j, h]
    ns = ref.shape[-1] // N_CHIPS
    return ref.at[h, :, pl.ds(pl.multiple_of(j * ns, 128), ns)]


def _piece(ref, rowlike, j):
    if rowlike:
        return ref.at[j]
    ns = ref.shape[-1] // N_CHIPS
    return ref.at[:, :, pl.ds(pl.multiple_of(j * ns, 128), ns)]


def _full_shape(shard_view, rowlike):
    _, kh, ns = shard_view.shape
    return (N_CHIPS, 2, kh, ns) if rowlike else (2, kh, N_CHIPS * ns)


def _remote(src, dst, send_sem, recv_sem, dev):
    return pltpu.make_async_remote_copy(src, dst, send_sem, recv_sem, device_id=dev, device_id_type=MESH)


SEM_SPEC = pl.BlockSpec(memory_space=pltpu.SEMAPHORE)
SPLIT_PARAMS = pltpu.CompilerParams(has_side_effects=pltpu.SideEffectType.DATAFLOW_SIDE_EFFECTING)


def _gather_plan(rowlikes):
    def plan(s_refs, f_refs, a, k, x, y, c):
        px, py = _peer_chip(x, y, k)
        return (s_refs[a].at[c], _piece_half(f_refs[a], rowlikes[a], 2 * x + y, c),
                _piece_half(f_refs[a], rowlikes[a], 2 * px + py, c), (px, py, c))
    return plan


def _all_to_all_plan(rowlikes):
    def plan(q_refs, r_refs, a, k, x, y, c):
        px, py = _peer_chip(x, y, k)
        if rowlikes[a]:
            src = q_refs[a].at[2 * px + py]
        else:
            ns = q_refs[a].shape[-1] // N_CHIPS
            src = q_refs[a].at[:, pl.ds(pl.multiple_of((2 * px + py) * ns, 128), ns)]
        return src, r_refs[a].at[k - 1], r_refs[a].at[k - 1], (px, py, c)
    return plan


def _swap_plan(rowlikes):
    def plan(g_refs, got_refs, a, k, x, y, c):
        src = g_refs[a].at[:, 1 - c] if rowlikes[a] else g_refs[a].at[1 - c]
        return src, got_refs[a], got_refs[a], (x, y, 1 - c)
    return plan


def _everyone_plan(s_refs, slot_refs, a, k, x, y, c):
    px, py, pc = x ^ (k >> 2), y ^ ((k >> 1) & 1), c ^ (k & 1)
    return s_refs[a], slot_refs[a].at[4 * x + 2 * y + c], slot_refs[a].at[4 * px + 2 * py + pc], (px, py, pc)


def _forward_plan(rowlikes):
    def plan(s_refs, f_refs, a, k, x, y, c):
        sib = (x, y, 1 - c)
        if k == 0:
            own = _piece(f_refs[a], rowlikes[a], 2 * x + y)
            return s_refs[a], own, own, sib
        px, py = _peer_chip(x, y, k)
        mine = _piece_half(f_refs[a], rowlikes[a], 2 * px + py, c)
        return mine, mine, _piece_half(f_refs[a], rowlikes[a], 2 * px + py, 1 - c), sib
    return plan


CHIPS, SIBLING, EVERYONE, FORWARDS = (1, 2, 3), (1,), tuple(range(1, 8)), (0, 1, 2, 3)


def _as_list(after):
    return [] if after is None else list(after) if isinstance(after, (list, tuple)) else [after]


def exchange_start(srcs, land_shapes, plan, after, name, peers=CHIPS):
    n = len(srcs)
    lands = [l if hasattr(l, "dtype") else lax.empty(l, s.dtype) for l, s in zip(land_shapes, srcs)]

    behind = _as_list(after)

    def body(*refs):
        s_refs, l_refs = refs[:n], refs[n:2 * n]
        send_sems, recv_sems = refs[2 * n + len(behind)], refs[2 * n + len(behind) + 1]
        token = refs[-1]
        x, y, c = _me()
        for a in range(n):
            for i, k in enumerate(peers):
                src, dst, _, peer = plan(s_refs, l_refs, a, k, x, y, c)
                sem = len(peers) * a + i
                _remote(src, dst, send_sems.at[sem], recv_sems.at[sem], peer).start()
        token[...] = jnp.zeros_like(token)

    n_sems = len(peers) * n
    outs = pl.pallas_call(
        body, name=name, in_specs=[HBM_SPEC] * (2 * n) + [pl.BlockSpec(memory_space=pl.ANY)] * len(behind),
        out_specs=[SEM_SPEC, SEM_SPEC] + [HBM_SPEC] * (2 * n) + [pl.BlockSpec(memory_space=pltpu.VMEM)],
        out_shape=[pltpu.SemaphoreType.DMA((n_sems,)), pltpu.SemaphoreType.DMA((n_sems,))]
        + [pltpu.HBM(a.shape, a.dtype) for a in (*srcs, *lands)] + [SDS((8, 128), F32)],
        input_output_aliases={i: 2 + i for i in range(2 * n)},
        compiler_params=SPLIT_PARAMS,
    )(*[pltpu.with_memory_space_constraint(a, pltpu.HBM) for a in (*srcs, *lands)], *behind)
    return {"sems": outs[:2], "srcs": outs[2:2 + n], "lands": outs[2 + n:2 + 2 * n], "token": outs[-1]}


def exchange_wait(state, plan, after, name, peers=CHIPS):
    n = len(state["srcs"])
    behind = _as_list(after)

    def body(*refs):
        s_refs, l_refs = refs[:n], refs[n:2 * n]
        send_sems, recv_sems = refs[2 * n], refs[2 * n + 1]
        x, y, c = _me()
        for a in range(n):
            for i, k in enumerate(peers):
                src, _, landing, peer = plan(s_refs, l_refs, a, k, x, y, c)
                sem = len(peers) * a + i
                cp = _remote(src, landing, send_sems.at[sem], recv_sems.at[sem], peer)
                cp.wait_send()
                cp.wait_recv()

    bufs = (*state["srcs"], *state["lands"])
    outs = pl.pallas_call(
        body, name=name,
        in_specs=[HBM_SPEC] * (2 * n) + [SEM_SPEC, SEM_SPEC] + [pl.BlockSpec(memory_space=pl.ANY)] * len(behind),
        out_specs=[HBM_SPEC] * (2 * n),
        out_shape=[pltpu.HBM(a.shape, a.dtype) for a in bufs],
        input_output_aliases={i: i for i in range(2 * n)},
        compiler_params=SPLIT_PARAMS,
    )(*bufs, *state["sems"], *behind)
    return outs[:n], outs[n:]


def gather_finish(shards, fulls, rowlikes, name):
    n = len(shards)

    def body(*refs):
        s_refs, f_refs = refs[:n], refs[2 * n:3 * n]
        send_sems, recv_sems = refs[3 * n:]
        x, y, c = _me()
        chip = 2 * x + y
        sib = (x, y, 1 - c)
        sends = []
        for a in range(n):
            own = _piece(f_refs[a], rowlikes[a], chip)
            cp = _remote(s_refs[a], own, send_sems.at[a, 0], recv_sems.at[a, 0], sib)
            cp.start()
            sends.append(cp)
            for k in (1, 2, 3):
                px, py = _peer_chip(x, y, k)
                slot = _piece_half(f_refs[a], rowlikes[a], 2 * px + py, c)
                cp = _remote(slot, slot, send_sems.at[a, k], recv_sems.at[a, k], sib)
                cp.start()
                sends.append(cp)
        for a in range(n):
            own = _piece(f_refs[a], rowlikes[a], chip)
            _remote(own, own, send_sems.at[a, 0], recv_sems.at[a, 0], sib).wait_recv()
            for k in (1, 2, 3):
                px, py = _peer_chip(x, y, k)
                slot = _piece_half(f_refs[a], rowlikes[a], 2 * px + py, 1 - c)
                _remote(slot, slot, send_sems.at[a, k], recv_sems.at[a, k], sib).wait_recv()
        for cp in sends:
            cp.wait_send()

    return pl.pallas_call(
        body, name=name, in_specs=[HBM_SPEC] * (2 * n), out_specs=[HBM_SPEC] * n,
        out_shape=[SDS(f.shape, f.dtype) for f in fulls],
        input_output_aliases={n + a: a for a in range(n)},
        scratch_shapes=[pltpu.SemaphoreType.DMA((n, 4)), pltpu.SemaphoreType.DMA((n, 4))],
    )(*shards, *fulls)


def _half_buffer_shape(gview, rowlike):
    return (N_CHIPS,) + gview.shape[2:] if rowlike else gview.shape[1:]


def _row_tiles(kh, ns):
    return 1 if kh * ns <= 256 * 1024 else 2


def _run_jobs(jobs, place, name):
    n_steps = max(job[0] for job in jobs)

    def spec(block, index, steps):
        return pl.BlockSpec(block, lambda s, p: index(jnp.minimum(s, steps - 1), p))

    in_specs = [spec(b, ix, steps) for steps, ins, _, _ in jobs for _, b, ix in ins]
    out_specs = [spec(b, ix, steps) for steps, _, outs, _ in jobs for _, _, b, ix in outs]

    def body(place_ref, *refs):
        s = pl.program_id(0)
        i, o = 0, len(in_specs)
        for steps, ins, outs, fn in jobs:
            in_refs, out_refs = refs[i:i + len(ins)], refs[o:o + len(outs)]
            i, o = i + len(ins), o + len(outs)

            @pl.when(s < steps)
            def _(fn=fn, in_refs=in_refs, out_refs=out_refs):
                fn(in_refs, out_refs)

    return pl.pallas_call(
        body, name=name,
        grid_spec=pltpu.PrefetchScalarGridSpec(num_scalar_prefetch=1, grid=(n_steps,), in_specs=in_specs,
                                               out_specs=out_specs),
        out_shape=[SDS(shape, dtype) for _, _, outs, _ in jobs for shape, dtype, _, _ in outs],
        compiler_params=_cp(("arbitrary",)),
    )(place, *[a for _, ins, _, _ in jobs for a, _, _ in ins])


def add_halves(gviews, gots, rowlikes, place, name):
    def add(ins, outs):
        outs[0][...] = (ins[0][...].astype(F32) + ins[1][...].astype(F32)).astype(BF16)

    jobs = []
    for g, got, rowlike in zip(gviews, gots, rowlikes):
        kh, ns = (g.shape[2], g.shape[3]) if rowlike else (g.shape[1], g.shape[2] // N_CHIPS)
        rt = _row_tiles(kh, ns)
        tr = kh // rt
        if rowlike:
            g_in = (g, (None, None, tr, ns), lambda ls, p, rt=rt: (ls // rt, p[0], ls % rt, 0))
            half = ((None, tr, ns), lambda ls, p, rt=rt: (ls // rt, ls % rt, 0))
        else:
            g_in = (g, (None, tr, ns), lambda ls, p, rt=rt: (p[0], ls % rt, ls // rt))
            half = ((tr, ns), lambda ls, p, rt=rt: (ls % rt, ls // rt))
        jobs.append((N_CHIPS * rt, [g_in, (got, *half)], [(got.shape, BF16, *half)], add))
    return _run_jobs(jobs, place, name)


def _slot_shape(q, rowlike):
    return (3,) + (q.shape[1:] if rowlike else (q.shape[0], q.shape[1] // N_CHIPS))


def sum_pieces(qs, recvs, rowlikes, place, name):
    def total(ins, outs):
        acc = ins[0][...].astype(F32)
        for k in range(3):
            acc = acc + ins[1][k].astype(F32)
        outs[0][...] = acc

    jobs = []
    for q, recv, rowlike in zip(qs, recvs, rowlikes):
        kh, ns = recv.shape[1], recv.shape[2]
        rt = _row_tiles(kh, ns)
        tr = kh // rt
        mine = ((None, tr, ns), lambda ls, p: (p[1], ls, 0)) if rowlike else ((tr, ns), lambda ls, p: (ls, p[1]))
        jobs.append((rt, [(q, *mine), (recv, (3, tr, ns), lambda ls, p: (0, ls, 0))],
                     [((2, kh, ns), F32, (None, tr, ns), lambda ls, p: (p[0], ls, 0))], total))
    return _run_jobs(jobs, place, name)


def join_halves(halves, name):
    n = len(halves)

    def body(*refs):
        o_refs = refs[n:2 * n]
        send_sems, recv_sems = refs[2 * n:]
        x, y, c = _me()
        sib = (x, y, 1 - c)
        sends = []
        for a in range(n):
            cp = _remote(o_refs[a].at[c], o_refs[a].at[c], send_sems.at[a], recv_sems.at[a], sib)
            cp.start()
            sends.append(cp)
        for a in range(n):
            got = o_refs[a].at[1 - c]
            _remote(got, got, send_sems.at[a], recv_sems.at[a], sib).wait_recv()
        for cp in sends:
            cp.wait_send()

    return pl.pallas_call(
        body, name=name, in_specs=[HBM_SPEC] * n, out_specs=[HBM_SPEC] * n,
        out_shape=[SDS(h.shape, h.dtype) for h in halves],
        input_output_aliases={a: a for a in range(n)},
        scratch_shapes=[pltpu.SemaphoreType.DMA((n,)), pltpu.SemaphoreType.DMA((n,))],
    )(*halves)


N_DEV = 8


def sum_devices(s, slots, me):
    R, Wd = s.shape

    def body(me_ref, s_ref, slots_ref, out_ref):
        acc = None
        for d in range(N_DEV):
            mine = me_ref[0] == d
            term = jnp.where(mine, s_ref[...], slots_ref[jnp.where(mine, d ^ 1, d)])
            acc = term if acc is None else acc + term
        out_ref[...] = acc

    vmem = pl.BlockSpec(memory_space=pltpu.VMEM)
    return pl.pallas_call(
        body, name="sum_devices", in_specs=[pl.BlockSpec(memory_space=pltpu.SMEM), vmem, vmem], out_specs=vmem,
        out_shape=SDS((R, Wd), F32),
    )(me, s, slots)


TRANSPOSED = ("w_in",)
BIG = {"ffn1_w_up": (D, 2 * FF, "col"), "ffn1_w_down": (FF, D, "row"), "w_in": (INW, D, "row"),
       "w_attn_up": (AW, D, "row"), "w_pool_up": (PW, D, "col"), "w_out": (D, D, "row"),
       "ffn2_w_up": (D, 2 * FF, "col"), "ffn2_w_down": (FF, D, "row")}
GROUPS = (("ffn1_w_up", "ffn1_w_down"), ("w_in", "w_attn_up", "w_pool_up", "w_out"), ("ffn2_w_up", "ffn2_w_down"))
SMALL = ("ffn1_norm", "mix_norm", "ffn2_norm", "final_norm", "pool_scale", "sinks", "pool_w_mix")
SMALL_W = 128


def _rowlike(name):
    return BIG[name][2] == "row"


def _half_dims(name):
    k, n, kind = BIG[name]
    return (k // N_CHIPS // 2, n) if kind == "row" else (k // 2, n // N_CHIPS)


def shard_view(name, shard):
    return shard.reshape((2,) + _half_dims(name))


def full_from_view(name, fv):
    k, n, _ = BIG[name]
    return fv.reshape(k, n)


def grad_view(name, g):
    kh, ns = _half_dims(name)
    return g.reshape(_full_shape(jax.ShapeDtypeStruct((2, kh, ns), g.dtype), _rowlike(name)))


def pack_small(d):
    parts = []
    for name in SMALL:
        a = d[name].reshape(-1)
        pad = (-a.shape[0]) % SMALL_W
        parts.append(jnp.pad(a, (0, pad)).reshape(-1, SMALL_W))
    a = jnp.concatenate(parts, axis=0)
    return jnp.pad(a, ((0, (-a.shape[0]) % 8), (0, 0)))


def unpack_small(a, like):
    out, r0 = {}, 0
    for name in SMALL:
        size = int(np.prod(like[name].shape))
        rows = -(-size // SMALL_W)
        out[name] = a[r0:r0 + rows].reshape(-1)[:size].reshape(like[name].shape)
        r0 += rows
    return out


WEIGHTS = ("ffn1_norm", "ffn1_w_up", "ffn1_w_down", "mix_norm", "w_in", "sinks", "w_attn_up", "pool_w_mix",
           "pool_scale", "w_pool_up", "w_out", "ffn2_norm", "ffn2_w_up", "ffn2_w_down", "final_norm")


def kernel(x, ffn1_norm, ffn1_w_up, ffn1_w_down, mix_norm, w_in, sinks, w_attn_up, pool_w_mix, pool_scale, w_pool_up, w_out, ffn2_norm, ffn2_w_up, ffn2_w_down, final_norm, loss_target, m_ffn1_norm, m_ffn1_w_up, m_ffn1_w_down, m_mix_norm, m_w_in, m_sinks, m_w_attn_up, m_pool_w_mix, m_pool_scale, m_w_pool_up, m_w_out, m_ffn2_norm, m_ffn2_w_up, m_ffn2_w_down, m_final_norm, v_ffn1_norm, v_ffn1_w_up, v_ffn1_w_down, v_mix_norm, v_w_in, v_sinks, v_w_attn_up, v_pool_w_mix, v_pool_scale, v_w_pool_up, v_w_out, v_ffn2_norm, v_ffn2_w_up, v_ffn2_w_down, v_final_norm):
    given = dict(locals())
    w = {n: given[n] for n in WEIGHTS}
    m = {n: given["m_" + n] for n in WEIGHTS}
    v = {n: given["v_" + n] for n in WEIGHTS}
    cx, cy, cc = _me()
    place = jnp.stack([cc, 2 * cx + cy]).astype(jnp.int32)

    def local2d(d, n):
        return d[n][0].T if n in TRANSPOSED else d[n][0]

    shards = {n: local2d(w, n) for n in BIG}
    grads, delta, new_m, new_v = {}, {}, {}, {}
    rowlikes = [[_rowlike(n) for n in names] for names in GROUPS]

    class Exchanges:
        def __init__(self):
            self.gathers, self.forwards, self.fulls, self.reductions, self.small = {}, {}, {}, {}, None

        def _behind_first(self, a):
            return a + self.gathers[0]["token"][0, 0]

        def _start_gather(self, group, after):
            prepare = self._behind_first if group else (lambda a: a)
            sv = [shard_view(n, prepare(shards[n]).astype(BF16)) for n in GROUPS[group]]
            rl = rowlikes[group]
            self.gathers[group] = exchange_start(sv, [_full_shape(s, r) for s, r in zip(sv, rl)], _gather_plan(rl),
                                                 after, f"gather_start_{group}")

        def weights(self, group, after):
            if not self.gathers:
                self._start_gather(0, None)
                self.packed = [self._behind_first(pack_small(d)) for d in (w, m, v)]
                after = [self.gathers[0]["token"], *self.packed]
            if group not in self.fulls:
                names, rl, state = GROUPS[group], rowlikes[group], self.gathers[group]
                if group in self.forwards:
                    _, fulls = exchange_wait(self.forwards.pop(group), _forward_plan(rl), after,
                                             f"forward_wait_{group}", FORWARDS)
                else:
                    sv, fulls = exchange_wait(state, _gather_plan(rl), state["token"] if after is None else after,
                                              f"gather_wait_{group}")
                    fulls = gather_finish(sv, fulls, rl, f"gather_finish_{group}")
                self.fulls[group] = {n: full_from_view(n, f) for n, f in zip(names, fulls)}
                if group + 1 < len(GROUPS):
                    self._start_gather(group + 1, fulls[0])
            return self.fulls[group]

        def started(self):
            return self.gathers[max(self.gathers)]["token"]

        def prefetch(self, group, after):
            rl, state = rowlikes[group], self.gathers[group]
            sv, fulls = exchange_wait(state, _gather_plan(rl), after, f"gather_wait_{group}")
            self.forwards[group] = exchange_start(sv, fulls, _forward_plan(rl), None, f"forward_start_{group}", FORWARDS)
            return self.forwards[group]["token"]

        def grads(self, g):
            names = tuple(g)
            rl = [_rowlike(n) for n in names]
            gv = [grad_view(n, g[n]) for n in names]
            state = exchange_start(gv, [_half_buffer_shape(a, r) for a, r in zip(gv, rl)], _swap_plan(rl), None,
                                   "swap_start_" + names[0], SIBLING)
            self.reductions[names] = state
            return state["token"]

        def advance(self, names, after):
            rl = [_rowlike(n) for n in names]
            gv, gots = exchange_wait(self.reductions[names], _swap_plan(rl), after, "swap_wait_" + names[0], SIBLING)
            qs = add_halves(gv, gots, rl, place, "add_halves_" + names[0])
            state = exchange_start(qs, [_slot_shape(q, r) for q, r in zip(qs, rl)], _all_to_all_plan(rl), None,
                                   "all_to_all_start_" + names[0])
            self.reductions[names] = state
            return state["token"]

        def reduce(self, names, after):
            rl = [_rowlike(n) for n in names]
            qs, recvs = exchange_wait(self.reductions.pop(names), _all_to_all_plan(rl), after,
                                      "all_to_all_wait_" + names[0])
            halves = sum_pieces(qs, recvs, rl, place, "sum_pieces_" + names[0])
            return [(shards[n], o.reshape(shards[n].shape), local2d(m, n), local2d(v, n))
                    for n, o in zip(names, join_halves(halves, "join_halves_" + names[0]))]

        def updated(self, names, outs):
            for i, n in enumerate(names):
                grads[n], delta[n], new_m[n], new_v[n] = outs[4 * i:4 * i + 4]
            return [new_v[n] for n in names]

        def finish(self, names, after):
            updates = self.reduce(names, after)
            return self.updated(names, adamw(updates, place, "adamw_" + names[0]))

        def small_start(self, G, loss, after):
            packed = pack_small({n: G[n] for n in SMALL})
            used_rows = sum(-(-int(np.prod(small_like[n].shape)) // SMALL_W) for n in SMALL)
            assert packed.shape[0] > used_rows
            packed = packed.at[-1, 0].set(loss[0, 0])
            self.small = exchange_start([packed], [(N_DEV,) + packed.shape], _everyone_plan, after,
                                        "small_start", EVERYONE)
            return self.small["token"]

        def small_finish(self, after):
            (packed,), (slots,) = exchange_wait(self.small, _everyone_plan, after, "small_wait", EVERYONE)
            total = sum_devices(packed, slots, (4 * cx + 2 * cy + cc).astype(jnp.int32).reshape(1))
            self.loss = total[-1, 0]
            g, ds, ms, vs = adamw([(self.packed[0], total, self.packed[1], self.packed[2])], place, "adamw_small")
            for d, packed_d in ((grads, g), (delta, ds), (new_m, ms), (new_v, vs)):
                d.update(unpack_small(packed_d, small_like))
            return vs

    small_like = {n: w[n] for n in SMALL}
    S = {n: w[n].reshape(1, -1) for n in ("ffn1_norm", "mix_norm", "ffn2_norm", "final_norm", "pool_scale", "sinks")}
    S["pool_w_mix"] = w["pool_w_mix"][0].astype(BF16)
    exchanges = Exchanges()
    _, dx = local_fwd_bwd(x[0], loss_target[0], S, exchanges)
    loss = exchanges.loss

    def shaped(d, n):
        return (d[n].T if n in TRANSPOSED else d[n]).reshape(w[n].shape)

    return (loss, dx[None], *[shaped(grads, n) for n in WEIGHTS], *[shaped(delta, n) for n in WEIGHTS],
            *[shaped(new_m, n) for n in WEIGHTS], *[shaped(new_v, n) for n in WEIGHTS])
```

```python
import numpy as np
import jax
import jax.numpy as jnp
from jax import lax
from jax.experimental import pallas as pl
from jax.experimental.pallas import tpu as pltpu

F32 = jnp.float32
BF16 = jnp.bfloat16
SDS = jax.ShapeDtypeStruct
MESH = pl.DeviceIdType.MESH

D = 1024
FF = 2816
NQ = 16
NKV = 2
HD = 64
GQ = NQ // NKV
AW = NQ * HD
KVW = NKV * HD
BLK = 128
PW = 512
PG = 128
POOL_WINDOWS = (2, 4, 8, 16)
HALO = 16
INW = AW + 2 * KVW + PW + 2 * D
C_KV = AW
C_Z = AW + 2 * KVW
C_G = C_Z + PW
EPS = 1e-6
FF_CHUNK = 256
FF_CHUNKS = tuple((c, FF_CHUNK) for c in range(0, FF, FF_CHUNK))
SLOPES = tuple(float(2.0 ** (-8.0 * h / NQ)) for h in range(1, NQ + 1))
SCALE = HD ** -0.5

LR, B1, B2, ADAM_EPS, WD, STEP = 0.001, 0.9, 0.999, 1e-08, 0.01, 10

VMEM_LIMIT = 56 * 1024 * 1024
N_CHIPS = 4

NT = (((1,), (1,)), ((), ()))
TN = (((0,), (0,)), ((), ()))


def _cp(sem=None, vmem=VMEM_LIMIT):
    return pltpu.CompilerParams(dimension_semantics=sem, vmem_limit_bytes=vmem)


def _const_spec(shape):
    nd = len(shape)
    return pl.BlockSpec(shape, lambda *_: (0,) * nd, pipeline_mode=pl.Buffered(1))


def _rstd(x):
    return lax.rsqrt(jnp.mean(x * x, axis=-1, keepdims=True) + EPS)


def _rms_bwd(dn, xhat, rstd, g):
    dxhat = dn * g
    return rstd * (dxhat - xhat * jnp.mean(dxhat * xhat, axis=-1, keepdims=True))


def _dot(a, b):
    return jnp.dot(a, b, preferred_element_type=F32)


def _dot_nt(a, b):
    return lax.dot_general(a, b, NT, preferred_element_type=F32)


def _dot_tn(a, b):
    return lax.dot_general(a, b, TN, preferred_element_type=F32)


def ffn_fwd(h, g, wup, wdn, name, head=None):
    T = h.shape[0]
    TM = 512
    tile = lambda w: pl.BlockSpec((TM, w), lambda i: (i, 0))
    acc_spec = lambda w: pl.BlockSpec((1, w), lambda i: (0, 0))

    def body(h_ref, g_ref, wup_ref, wdn_ref, *rest):
        out_ref, ab_ref, hid_ref = rest[-3:] if head is None else rest[2:5]
        x = h_ref[...]
        n = (x * _rstd(x) * g_ref[...]).astype(BF16)
        for c0, w in FF_CHUNKS:
            a = _dot(n, wup_ref[:, c0:c0 + w])
            b = _dot(n, wup_ref[:, FF + c0:FF + c0 + w])
            sig = jax.nn.sigmoid(a)
            s = a * sig
            ab_ref[:, c0:c0 + w] = (b * (sig * (1.0 + a * (1.0 - sig)))).astype(BF16)
            ab_ref[:, FF + c0:FF + c0 + w] = s.astype(BF16)
            hid_ref[:, c0:c0 + w] = (s * b).astype(BF16)
        out = x + 0.5 * _dot(hid_ref[...], wdn_ref[...])
        if head is None:
            out_ref[...] = out
        else:
            t_ref, gf_ref, loss_ref, dgf_ref = rest[0], rest[1], rest[5], rest[6]
            out_ref[...] = _loss_head(out, t_ref[...], gf_ref[...], loss_ref, dgf_ref, pl.program_id(0) == 0)

    head_in, head_specs, head_out_specs, head_out_shape = [], [], [], []
    if head is not None:
        head_in, head_specs = list(head), [tile(D), _const_spec((1, D))]
        head_out_specs, head_out_shape = [acc_spec(1), acc_spec(D)], [SDS((1, 1), F32), SDS((1, D), F32)]
    return pl.pallas_call(
        body, name=name, grid=(T // TM,),
        in_specs=[tile(D), _const_spec((1, D)), _const_spec((D, 2 * FF)), _const_spec((FF, D))] + head_specs,
        out_specs=[tile(D), tile(2 * FF), tile(FF)] + head_out_specs,
        out_shape=[SDS((T, D), F32), SDS((T, 2 * FF), BF16), SDS((T, FF), BF16)] + head_out_shape,
        compiler_params=_cp(("arbitrary",)),
    )(h, g, wup, wdn, *head_in)


def _loss_head(x, target, g, loss_ref, dg_ref, first):
    @pl.when(first)
    def _():
        loss_ref[...] = jnp.zeros_like(loss_ref)
        dg_ref[...] = jnp.zeros_like(dg_ref)

    rstd = _rstd(x)
    xhat = x * rstd
    err = xhat * g - target
    loss_ref[...] += 0.5 * jnp.sum(jnp.mean(err * err, axis=-1, keepdims=True), axis=0, keepdims=True)
    dy = err * (1.0 / D)
    dg_ref[...] += jnp.sum(dy * xhat, axis=0, keepdims=True)
    return _rms_bwd(dy, xhat, rstd, g)


def _adamw_tile(w_ref, g_ref, m_ref, v_ref, go_ref, d_ref, nm_ref, nv_ref):
    gv = g_ref[...]
    go_ref[...] = gv
    nm = B1 * m_ref[...] + (1.0 - B1) * gv
    nv = B2 * v_ref[...] + (1.0 - B2) * (gv * gv)
    nm_ref[...] = nm
    nv_ref[...] = nv
    d_ref[...] = -LR * ((nm / (1.0 - B1 ** STEP)) / (jnp.sqrt(nv / (1.0 - B2 ** STEP)) + ADAM_EPS) + WD * w_ref[...])


def _riders(updates, steps, step_of):
    in_specs, out_specs, out_shapes, operands, tiles = [], [], [], [], []
    for w, g, m, v in updates:
        R, C = w.shape
        n = max(d for d in range(1, steps + 1) if R % d == 0 and (R // d) % 8 == 0)
        spec = pl.BlockSpec((R // n, C), lambda *ids, n=n: (jnp.minimum(step_of(*ids), n - 1), 0))
        in_specs += [spec] * 4
        out_specs += [spec] * 4
        out_shapes += [SDS((R, C), F32)] * 4
        operands += [w, g, m, v]
        tiles.append(n)

    def run(step, in_refs, out_refs):
        for u, n in enumerate(tiles):
            @pl.when(step < n)
            def _(u=u):
                _adamw_tile(*in_refs[4 * u:4 * u + 4], *out_refs[4 * u:4 * u + 4])

    return in_specs, out_specs, out_shapes, operands, run


def ffn_bwd_x(dh, h_in, ab, g, wup, wdn, name, updates=()):
    T = dh.shape[0]
    TM = 256
    r_in, r_out, r_shapes, r_args, ride = _riders(updates, T // TM, lambda i: i)

    def body(dh_ref, h_ref, ab_ref, g_ref, wup_ref, wdn_ref, *rest):
        dhin_ref, dup_ref, n_ref, dg_ref = rest[len(r_in):len(r_in) + 4]
        ride(pl.program_id(0), rest[:len(r_in)], rest[len(r_in) + 4:])
        x = h_ref[...]
        g = g_ref[...]
        rstd = _rstd(x)
        xhat = x * rstd
        n_ref[...] = (xhat * g).astype(BF16)
        dh = dh_ref[...]
        dhh = (0.5 * dh).astype(BF16)
        for c0, w in FF_CHUNKS:
            dhid = _dot_nt(dhh, wdn_ref[c0:c0 + w, :])
            dup_ref[:, c0:c0 + w] = (dhid * ab_ref[:, c0:c0 + w].astype(F32)).astype(BF16)
            dup_ref[:, FF + c0:FF + c0 + w] = (dhid * ab_ref[:, FF + c0:FF + c0 + w].astype(F32)).astype(BF16)
        dn = _dot_nt(dup_ref[...], wup_ref[...])
        dhin_ref[...] = dh + _rms_bwd(dn, xhat, rstd, g)

        @pl.when(pl.program_id(0) == 0)
        def _():
            dg_ref[...] = jnp.zeros_like(dg_ref)

        dg_ref[...] += jnp.sum(dn * xhat, axis=0, keepdims=True)

    tile = lambda w: pl.BlockSpec((TM, w), lambda i: (i, 0))
    return pl.pallas_call(
        body, name=name, grid=(T // TM,),
        in_specs=[tile(D), tile(D), tile(2 * FF), _const_spec((1, D)), _const_spec((D, 2 * FF)), _const_spec((FF, D))]
        + r_in,
        out_specs=[tile(D), tile(2 * FF), tile(D), pl.BlockSpec((1, D), lambda i: (0, 0))] + r_out,
        out_shape=[SDS((T, D), F32), SDS((T, 2 * FF), BF16), SDS((T, D), BF16), SDS((1, D), F32)] + r_shapes,
        compiler_params=_cp(("arbitrary",)),
    )(dh, h_in, ab, g, wup, wdn, *r_args)


TOKEN_SPEC = pl.BlockSpec((8, 128), lambda *_: (0, 0))


def _token_operand(token):
    return ([], []) if token is None else ([TOKEN_SPEC], [token])


def matmul_tn(a, b, name, *, tm, tn, tt=1024, b_scale=None, after=None):
    T, M = a.shape
    N = b.shape[1]
    tt = min(tt, T)
    assert M % tm == 0 and N % tn == 0 and T % tt == 0
    nt = T // tt
    token_spec, token_arg = _token_operand(after)

    def body(a_ref, b_ref, *rest):
        o_ref, acc_ref = rest[-2:]
        t = pl.program_id(2)

        @pl.when(t == 0)
        def _():
            acc_ref[...] = jnp.zeros_like(acc_ref)

        bv = b_ref[...]
        if b_scale is not None:
            bv = bv * b_scale
        acc_ref[...] += _dot_tn(a_ref[...].astype(BF16), bv.astype(BF16))

        @pl.when(t == nt - 1)
        def _():
            o_ref[...] = acc_ref[...].astype(BF16)

    return pl.pallas_call(
        body, name=name, grid=(M // tm, N // tn, nt),
        in_specs=[pl.BlockSpec((tt, tm), lambda i, j, t: (t, i)), pl.BlockSpec((tt, tn), lambda i, j, t: (t, j))]
        + token_spec,
        out_specs=pl.BlockSpec((tm, tn), lambda i, j, t: (i, j)),
        out_shape=SDS((M, N), BF16),
        scratch_shapes=[pltpu.VMEM((tm, tn), F32)],
        compiler_params=_cp(("parallel", "parallel", "arbitrary")),
    )(a, b, *token_arg)


def mix_in_fwd(h1, g, win_t):
    T = h1.shape[0]
    TM = 512

    def body(h_ref, g_ref, w_ref, u_ref, q_ref, kv_ref, z_ref, gt_ref):
        x = h_ref[...]
        u = (x * _rstd(x) * g_ref[...]).astype(BF16)
        u_ref[...] = u
        for c in range(0, AW, 256):
            q_ref[:, c:c + 256] = _dot_nt(u, w_ref[c:c + 256, :]).astype(BF16)
        kv_ref[...] = _dot_nt(u, w_ref[C_KV:C_Z, :]).astype(BF16)
        for c in range(0, PW, 256):
            z_ref[:, c:c + 256] = _dot_nt(u, w_ref[C_Z + c:C_Z + c + 256, :])
        for c in range(0, 2 * D, 256):
            gt_ref[:, c:c + 256] = _dot_nt(u, w_ref[C_G + c:C_G + c + 256, :]).astype(BF16)

    tile = lambda w: pl.BlockSpec((TM, w), lambda i: (i, 0))
    return pl.pallas_call(
        body, name="mix_in_fwd", grid=(T // TM,),
        in_specs=[tile(D), _const_spec((1, D)), _const_spec((INW, D))],
        out_specs=[tile(D), tile(AW), tile(2 * KVW), tile(PW), tile(2 * D)],
        out_shape=[SDS((T, D), BF16), SDS((T, AW), BF16), SDS((T, 2 * KVW), BF16), SDS((T, PW), F32),
                   SDS((T, 2 * D), BF16)],
        compiler_params=_cp(("arbitrary",)),
    )(h1, g, win_t)


def mix_in_bwd(dq, dkv, dz, dgt, dh2, h1, g, win_t):
    T = h1.shape[0]
    TM = min(1024, T)
    SUB = 256

    def body(dq_ref, dkv_ref, dz_ref, dgt_ref, dh2_ref, h_ref, g_ref, w_ref, dh1_ref, dg_ref):
        g = g_ref[...]
        dg = None
        for r0 in range(0, TM, SUB):
            rows = slice(r0, r0 + SUB)
            du = _dot(dq_ref[rows, :], w_ref[0:AW, :])
            du += _dot(dkv_ref[rows, :], w_ref[C_KV:C_Z, :])
            du += _dot(dz_ref[rows, :], w_ref[C_Z:C_G, :])
            du += _dot(dgt_ref[rows, :], w_ref[C_G:INW, :])
            x = h_ref[rows, :]
            rstd = _rstd(x)
            xhat = x * rstd
            dh1_ref[rows, :] = dh2_ref[rows, :] + _rms_bwd(du, xhat, rstd, g)
            part = jnp.sum(du * xhat, axis=0, keepdims=True)
            dg = part if dg is None else dg + part

        @pl.when(pl.program_id(0) == 0)
        def _():
            dg_ref[...] = jnp.zeros_like(dg_ref)

        dg_ref[...] += dg

    tile = lambda w: pl.BlockSpec((TM, w), lambda i: (i, 0))
    return pl.pallas_call(
        body, name="mix_in_bwd", grid=(T // TM,),
        in_specs=[tile(AW), tile(2 * KVW), tile(PW), tile(2 * D), tile(D), tile(D), _const_spec((1, D)),
                  _const_spec((INW, D))],
        out_specs=[tile(D), pl.BlockSpec((1, D), lambda i: (0, 0))],
        out_shape=[SDS((T, D), F32), SDS((1, D), F32)],
        compiler_params=_cp(("arbitrary",)),
    )(dq, dkv, dz, dgt, dh2, h1, g, win_t)


PAIR = 2 * HD
NPAIR = GQ // 2


def _lo_lanes():
    return lax.broadcasted_iota(jnp.int32, (BLK, PAIR), 1) < HD


def _stack_heads(ref, kvh, scale=None):
    lo = _lo_lanes()
    parts = []
    for pr in range(NPAIR):
        t = ref[:, (kvh * NPAIR + pr) * PAIR:(kvh * NPAIR + pr + 1) * PAIR]
        if scale is not None:
            t = t * scale
        zero = jnp.zeros_like(t)
        parts += [jnp.where(lo, t, zero), jnp.where(lo, zero, t)]
    return jnp.concatenate(parts, axis=0)


def _kv_tiles(kvc_ref, kvp_ref, tile, kvh):
    lo = _lo_lanes()
    dup, left, right = [], [], []
    for ref in (kvp_ref, kvc_ref):
        t = ref[:, tile * PAIR:(tile + 1) * PAIR]
        r = pltpu.roll(t.astype(F32), HD, 1).astype(BF16)
        zero = jnp.zeros_like(t)
        a, b = (t, r) if kvh == 0 else (r, t)
        dup.append(jnp.where(lo, a, b))
        left.append(jnp.where(lo, a, zero))
        right.append(jnp.where(lo, zero, b))
    cat = lambda xs: jnp.concatenate(xs, axis=0)
    return cat(dup), cat(left), cat(right)


def _band_consts(first):
    row = lax.broadcasted_iota(jnp.int32, (BLK, BLK), 0)
    col = lax.broadcasted_iota(jnp.int32, (BLK, BLK), 1)
    upper = col > row
    dist = jnp.where(upper, row - col + BLK, row - col).astype(F32)
    pen = jnp.where(jnp.logical_and(upper, first), -jnp.inf, 0.0)
    return upper, dist, pen


def _split_band(upper, t):
    zero = jnp.zeros_like(t)
    return jnp.concatenate([jnp.where(upper, t, zero), jnp.where(upper, zero, t)], axis=1)


def attn_fwd(q, kv, sinks):
    T = q.shape[0]
    nb = T // BLK

    def body(sink_ref, q_ref, kvc_ref, kvp_ref, att_ref, lse_ref):
        upper, dist, pen = _band_consts(pl.program_id(0) == 0)
        scores, values = [], []
        for kvh in range(NKV):
            kdup, _, _ = _kv_tiles(kvc_ref, kvp_ref, 0, kvh)
            values.append(_kv_tiles(kvc_ref, kvp_ref, 1, kvh)[1:])
            scores.append(_dot_nt(_stack_heads(q_ref, kvh, SCALE), kdup))
        for kvh in range(NKV):
            s_all = scores[kvh]
            vleft, vright = values[kvh]
            for pr in range(NPAIR):
                outs, inv = [], []
                for side, vpad in ((0, vleft), (1, vright)):
                    g = 2 * pr + side
                    hq = kvh * GQ + g
                    sink = sink_ref[0, hq]
                    rows = slice(g * BLK, (g + 1) * BLK)
                    s = jnp.where(upper, s_all[rows, 0:BLK], s_all[rows, BLK:2 * BLK]) - SLOPES[hq] * dist + pen
                    m = jnp.maximum(jnp.max(s, axis=-1, keepdims=True), sink)
                    p = jnp.exp(s - m)
                    l = jnp.sum(p, axis=-1, keepdims=True) + jnp.exp(sink - m)
                    lse_ref[:, hq:hq + 1] = m + jnp.log(l)
                    outs.append(_dot(_split_band(upper, p.astype(BF16)), vpad))
                    inv.append(1.0 / l)
                col0 = (kvh * NPAIR + pr) * PAIR
                att_ref[:, col0:col0 + PAIR] = ((outs[0] + outs[1]) * jnp.where(_lo_lanes(), inv[0], inv[1])).astype(BF16)

    return pl.pallas_call(
        body, name="attn_fwd", grid=(nb,),
        in_specs=[pl.BlockSpec(memory_space=pltpu.SMEM),
                  pl.BlockSpec((BLK, AW), lambda i: (i, 0)),
                  pl.BlockSpec((BLK, 2 * KVW), lambda i: (i, 0)),
                  pl.BlockSpec((BLK, 2 * KVW), lambda i: (jnp.maximum(i - 1, 0), 0))],
        out_specs=[pl.BlockSpec((BLK, AW), lambda i: (i, 0)), pl.BlockSpec((BLK, NQ), lambda i: (i, 0))],
        out_shape=[SDS((T, AW), BF16), SDS((T, NQ), F32)],
        compiler_params=_cp(("arbitrary",)),
    )(sinks, q, kv, kv)


def attn_bwd(q, kv, datt, att, lse, sinks):
    T = q.shape[0]
    nb = T // BLK

    def body(sink_ref, q_ref, kvc_ref, kvp_ref, do_ref, out_ref, lse_ref, dq_ref, dkv_ref, dsink_ref, carry_ref):
        i = pl.program_id(0)

        @pl.when(i == 0)
        def _():
            dsink_ref[...] = jnp.zeros_like(dsink_ref)
            carry_ref[...] = jnp.zeros_like(carry_ref)

        @pl.when(i < nb)
        def _():
            upper, dist, pen = _band_consts(i == 0)
            lo = _lo_lanes()
            dk_dup, dv_dup = [], []
            staged = []
            for kvh in range(NKV):
                kdup, kleft, kright = _kv_tiles(kvc_ref, kvp_ref, 0, kvh)
                vdup, _, _ = _kv_tiles(kvc_ref, kvp_ref, 1, kvh)
                qs = _stack_heads(q_ref, kvh, SCALE)
                dos = _stack_heads(do_ref, kvh)
                staged.append((kleft, kright, qs, dos, _dot_nt(qs, kdup), _dot_nt(dos, vdup)))
            deltas = []
            for pair in range(NQ // 2):
                cols = slice(pair * PAIR, (pair + 1) * PAIR)
                t = do_ref[:, cols].astype(F32) * out_ref[:, cols].astype(F32)
                deltas += [jnp.sum(jnp.where(lo, t, 0.0), axis=-1, keepdims=True),
                           jnp.sum(jnp.where(lo, 0.0, t), axis=-1, keepdims=True)]
            for kvh in range(NKV):
                kleft, kright, qs, dos, s_all, dp_all = staged[kvh]
                ds_parts, p_parts = [], []
                for pr in range(NPAIR):
                    dq = None
                    for side, kpad in ((0, kleft), (1, kright)):
                        g = 2 * pr + side
                        hq = kvh * GQ + g
                        lse_h = lse_ref[:, hq:hq + 1]
                        rows = slice(g * BLK, (g + 1) * BLK)
                        s = jnp.where(upper, s_all[rows, 0:BLK], s_all[rows, BLK:2 * BLK]) - SLOPES[hq] * dist + pen
                        p = jnp.exp(s - lse_h)
                        dp = jnp.where(upper, dp_all[rows, 0:BLK], dp_all[rows, BLK:2 * BLK])
                        delta = deltas[hq]
                        dsink_ref[:, hq:hq + 1] += -jnp.sum(jnp.exp(sink_ref[0, hq] - lse_h) * delta, axis=0,
                                                            keepdims=True)
                        ds = _split_band(upper, (p * (dp - delta)).astype(BF16))
                        ds_parts.append(ds)
                        p_parts.append(_split_band(upper, p.astype(BF16)))
                        d = _dot(ds, kpad)
                        dq = d if dq is None else dq + d
                    col0 = (kvh * NPAIR + pr) * PAIR
                    dq_ref[:, col0:col0 + PAIR] = (dq * SCALE).astype(BF16)
                dkw = _dot_tn(qs, jnp.concatenate(ds_parts, axis=0)).T
                dvw = _dot_tn(dos, jnp.concatenate(p_parts, axis=0)).T
                dk_dup.append(dkw + pltpu.roll(dkw, HD, 1))
                dv_dup.append(dvw + pltpu.roll(dvw, HD, 1))
            dk = jnp.where(jnp.concatenate([lo, lo], axis=0), dk_dup[0], dk_dup[1])
            dv = jnp.where(jnp.concatenate([lo, lo], axis=0), dv_dup[0], dv_dup[1])
            dkv_ref[:, 0:PAIR] = (carry_ref[:, 0:PAIR] + dk[0:BLK]).astype(BF16)
            dkv_ref[:, PAIR:2 * PAIR] = (carry_ref[:, PAIR:2 * PAIR] + dv[0:BLK]).astype(BF16)
            carry_ref[:, 0:PAIR] = dk[BLK:2 * BLK]
            carry_ref[:, PAIR:2 * PAIR] = dv[BLK:2 * BLK]

        @pl.when(i == nb)
        def _():
            dkv_ref[...] = carry_ref[...].astype(BF16)

    cur = lambda i: (jnp.minimum(i, nb - 1), 0)
    prev = lambda i: (jnp.maximum(jnp.minimum(i, nb - 1) - 1, 0), 0)
    return pl.pallas_call(
        body, name="attn_bwd", grid=(nb + 1,),
        in_specs=[pl.BlockSpec(memory_space=pltpu.SMEM),
                  pl.BlockSpec((BLK, AW), cur), pl.BlockSpec((BLK, 2 * KVW), cur), pl.BlockSpec((BLK, 2 * KVW), prev),
                  pl.BlockSpec((BLK, AW), cur), pl.BlockSpec((BLK, AW), cur), pl.BlockSpec((BLK, NQ), cur)],
        out_specs=[pl.BlockSpec((BLK, AW), cur),
                   pl.BlockSpec((BLK, 2 * KVW), lambda i: (jnp.maximum(i - 1, 0), 0)),
                   pl.BlockSpec((1, NQ), lambda i: (0, 0))],
        out_shape=[SDS((T, AW), BF16), SDS((T, 2 * KVW), BF16), SDS((1, NQ), F32)],
        scratch_shapes=[pltpu.VMEM((BLK, 2 * KVW), F32)],
        compiler_params=_cp(("arbitrary",)),
    )(sinks, q, kv, kv, datt, att, lse)


def _inv_counts(t0, rows):
    t = (t0 + lax.broadcasted_iota(jnp.int32, (rows, 1), 0) + 1).astype(F32)
    return [1.0 / jnp.minimum(t, float(w)) for w in POOL_WINDOWS]


def pool_fwd(z, wmix, scale):
    T = z.shape[0]
    TM = min(1024, T)
    L = TM + HALO

    def body(z_ref, halo_ref, wmix_ref, scale_ref, pooled_ref, mixs_ref):
        i = pl.program_id(0)
        halo = jnp.where(i > 0, halo_ref[...], 0.0)
        zt = z_ref[...]
        e = jnp.concatenate([halo, zt], axis=0)
        sums = []
        s = e
        for k in (1, 2, 4, 8):
            s = s + pltpu.roll(s, k, 0)
            sums.append(s)
        inv = _inv_counts(i * TM, TM)
        for gi in range(len(POOL_WINDOWS)):
            cols = slice(gi * PG, (gi + 1) * PG)
            pooled = (sums[gi][HALO:, cols] * inv[gi] - zt[:, cols]).astype(BF16)
            pooled_ref[:, cols] = pooled
            mixs_ref[:, cols] = (_dot(pooled, wmix_ref[gi]) * scale_ref[:, cols]).astype(BF16)

    return pl.pallas_call(
        body, name="pool_fwd", grid=(T // TM,),
        in_specs=[pl.BlockSpec((TM, PW), lambda i: (i, 0)),
                  pl.BlockSpec((HALO, PW), lambda i: (jnp.maximum(i * (TM // HALO) - 1, 0), 0)),
                  _const_spec((len(POOL_WINDOWS), PG, PG)), _const_spec((1, PW))],
        out_specs=[pl.BlockSpec((TM, PW), lambda i: (i, 0)), pl.BlockSpec((TM, PW), lambda i: (i, 0))],
        out_shape=[SDS((T, PW), BF16), SDS((T, PW), BF16)],
        compiler_params=_cp(("arbitrary",)),
    )(z, z, wmix, scale)


def pool_bwd(dmixs, pooled, wmix, scale):
    T = dmixs.shape[0]
    TM = min(1024, T)
    L = TM + HALO
    nt = T // TM

    def body(dm_ref, halo_ref, pooled_ref, wmix_ref, scale_ref, dz_ref, dwmix_ref, dscale_ref):
        i = pl.program_id(0)

        @pl.when(i == 0)
        def _():
            dwmix_ref[...] = jnp.zeros_like(dwmix_ref)
            dscale_ref[...] = jnp.zeros_like(dscale_ref)

        halo = jnp.where(i < nt - 1, halo_ref[...], 0.0)
        dm = dm_ref[...]
        e = jnp.concatenate([dm, halo], axis=0)
        inv = _inv_counts(i * TM, L)
        for gi in range(len(POOL_WINDOWS)):
            cols = slice(gi * PG, (gi + 1) * PG)
            w = wmix_ref[gi]
            dmixed = (e[:, cols] * scale_ref[:, cols]).astype(BF16)
            dpooled = _dot_nt(dmixed, w)
            pooled = pooled_ref[:, cols]
            mixed = _dot(pooled, w)
            dscale_ref[:, cols] += jnp.sum(dm[:, cols] * mixed, axis=0, keepdims=True)
            dwmix_ref[gi] += _dot_tn(pooled, dmixed[:TM])
            s = dpooled * inv[gi]
            k = 1
            while k < POOL_WINDOWS[gi]:
                s = s + pltpu.roll(s, L - k, 0)
                k *= 2
            dz_ref[:, cols] = (s[:TM] - dpooled[:TM]).astype(BF16)

    return pl.pallas_call(
        body, name="pool_bwd", grid=(nt,),
        in_specs=[pl.BlockSpec((TM, PW), lambda i: (i, 0)),
                  pl.BlockSpec((HALO, PW), lambda i: (jnp.minimum((i + 1) * (TM // HALO), T // HALO - 1), 0)),
                  pl.BlockSpec((TM, PW), lambda i: (i, 0)),
                  _const_spec((len(POOL_WINDOWS), PG, PG)), _const_spec((1, PW))],
        out_specs=[pl.BlockSpec((TM, PW), lambda i: (i, 0)),
                   pl.BlockSpec((len(POOL_WINDOWS), PG, PG), lambda i: (0, 0, 0)),
                   pl.BlockSpec((1, PW), lambda i: (0, 0))],
        out_shape=[SDS((T, PW), BF16), SDS((len(POOL_WINDOWS), PG, PG), F32), SDS((1, PW), F32)],
        compiler_params=_cp(("arbitrary",)),
    )(dmixs, dmixs, pooled, wmix, scale)


def merge_fwd(att, mixs, gt, h1, wattn, wpool, wout):
    T = h1.shape[0]
    TM = 512

    def body(att_ref, mixs_ref, gt_ref, h_ref, wa_ref, wp_ref, wo_ref, h2_ref, mg_ref, gf_ref):
        a = _dot(att_ref[...], wa_ref[...])
        p = _dot(mixs_ref[...], wp_ref[...])
        sa = jax.nn.sigmoid(gt_ref[:, 0:D].astype(F32))
        sp = jax.nn.sigmoid(gt_ref[:, D:2 * D].astype(F32))
        gf_ref[:, 0:D] = (a * sa * (1.0 - sa)).astype(BF16)
        gf_ref[:, D:2 * D] = (p * sp * (1.0 - sp)).astype(BF16)
        mg = (sa * a + sp * p).astype(BF16)
        mg_ref[...] = mg
        h2_ref[...] = h_ref[...] + _dot(mg, wo_ref[...])

    tile = lambda w: pl.BlockSpec((TM, w), lambda i: (i, 0))
    return pl.pallas_call(
        body, name="merge_fwd", grid=(T // TM,),
        in_specs=[tile(AW), tile(PW), tile(2 * D), tile(D), _const_spec((AW, D)), _const_spec((PW, D)),
                  _const_spec((D, D))],
        out_specs=[tile(D), tile(D), tile(2 * D)],
        out_shape=[SDS((T, D), F32), SDS((T, D), BF16), SDS((T, 2 * D), BF16)],
        compiler_params=_cp(("arbitrary",)),
    )(att, mixs, gt, h1, wattn, wpool, wout)


def merge_bwd(dh2, gt, gf, wattn, wpool, wout, after=None):
    T = dh2.shape[0]
    TM = 512
    token_spec, token_arg = _token_operand(after)

    def body(dh2_ref, gt_ref, gf_ref, wa_ref, wp_ref, wo_ref, *rest):
        datt_ref, dmixs_ref, dgt_ref, da_ref, dp_ref = rest[-5:]
        dm = _dot_nt(dh2_ref[...].astype(BF16), wo_ref[...])
        da = (dm * jax.nn.sigmoid(gt_ref[:, 0:D].astype(F32))).astype(BF16)
        dp = (dm * jax.nn.sigmoid(gt_ref[:, D:2 * D].astype(F32))).astype(BF16)
        da_ref[...] = da
        dp_ref[...] = dp
        dgt_ref[:, 0:D] = (dm * gf_ref[:, 0:D].astype(F32)).astype(BF16)
        dgt_ref[:, D:2 * D] = (dm * gf_ref[:, D:2 * D].astype(F32)).astype(BF16)
        datt_ref[...] = _dot_nt(da, wa_ref[...]).astype(BF16)
        dmixs_ref[...] = _dot_nt(dp, wp_ref[...])

    tile = lambda w: pl.BlockSpec((TM, w), lambda i: (i, 0))
    return pl.pallas_call(
        body, name="merge_bwd", grid=(T // TM,),
        in_specs=[tile(D), tile(2 * D), tile(2 * D), _const_spec((AW, D)), _const_spec((PW, D)),
                  _const_spec((D, D))] + token_spec,
        out_specs=[tile(AW), tile(PW), tile(2 * D), tile(D), tile(D)],
        out_shape=[SDS((T, AW), BF16), SDS((T, PW), F32), SDS((T, 2 * D), BF16), SDS((T, D), BF16),
                   SDS((T, D), BF16)],
        compiler_params=_cp(("arbitrary",)),
    )(dh2, gt, gf, wattn, wpool, wout, *token_arg)


def adamw(updates, place, name):
    tile_bytes = 2 * 1024 * 1024 // len(updates)
    jobs = []
    for w, g, m, v in updates:
        R, C = w.shape
        tr = R
        if R * C * 4 > tile_bytes:
            tr = next(cand for cand in (512, 256, 128, 64, 32, 16, 8) if R % cand == 0 and cand * C * 4 <= tile_bytes)
        blk = ((tr, C), lambda ls, p: (ls, 0))
        jobs.append((R // tr, [(a, *blk) for a in (w, g, m, v)], [((R, C), F32, *blk)] * 4,
                     lambda ins, outs: _adamw_tile(*ins, *outs)))
    return _run_jobs(jobs, place, name)


GROUP_FFN1, GROUP_MIX, GROUP_FFN2 = 0, 1, 2


def _behind(small, token):
    return small if token is None else small + token[0:1, 0:1]


def local_fwd_bwd(x, target, S, comm):
    W = comm.weights(GROUP_FFN1, None)
    h1, ab1, hid1 = ffn_fwd(x, _behind(S["ffn1_norm"], comm.started()), W["ffn1_w_up"], W["ffn1_w_down"], "ffn1_fwd")
    W = comm.weights(GROUP_MIX, h1)
    u, q, kv, z, gt = mix_in_fwd(h1, _behind(S["mix_norm"], comm.started()), W["w_in"])
    att, lse = attn_fwd(q, kv, S["sinks"])
    pooled, mixs = pool_fwd(z, S["pool_w_mix"], _behind(S["pool_scale"], comm.prefetch(GROUP_FFN2, att)))
    h2, merged, gf = merge_fwd(att, mixs, gt, h1, W["w_attn_up"], W["w_pool_up"], W["w_out"])
    W = comm.weights(GROUP_FFN2, h2)
    dh3, ab2, hid2, loss, g_final = ffn_fwd(h2, S["ffn2_norm"], W["ffn2_w_up"], W["ffn2_w_down"], "ffn2_fwd",
                                            head=(target, S["final_norm"]))

    G = {"final_norm": g_final}
    dh2, dup2, n2, G["ffn2_norm"] = ffn_bwd_x(dh3, h2, ab2, S["ffn2_norm"], W["ffn2_w_up"], W["ffn2_w_down"],
                                              "ffn2_bwd_x")
    ffn2 = ("ffn2_w_down", "ffn2_w_up")
    token = comm.grads({"ffn2_w_down": matmul_tn(hid2, dh3, "ffn2_dw_down", tm=1408, tn=512, tt=4096, b_scale=0.5),
                        "ffn2_w_up": matmul_tn(n2, dup2, "ffn2_dw_up", tm=D, tn=512, tt=4096)})

    W = comm.weights(GROUP_MIX, None)
    datt, dmixs, dgt, da, dp = merge_bwd(dh2, gt, gf, W["w_attn_up"], W["w_pool_up"], W["w_out"], after=token)
    token = comm.advance(ffn2, datt)
    g_mix = {"w_out": matmul_tn(merged, dh2, "dw_out", tm=D, tn=D, after=token),
             "w_attn_up": matmul_tn(att, da, "dw_attn_up", tm=AW, tn=D),
             "w_pool_up": matmul_tn(mixs, dp, "dw_pool_up", tm=PW, tn=D)}
    dz, G["pool_w_mix"], G["pool_scale"] = pool_bwd(dmixs, pooled, S["pool_w_mix"], _behind(S["pool_scale"], token))
    dq, dkv, G["sinks"] = attn_bwd(q, kv, datt, att, lse, _behind(S["sinks"], token))
    dh1, G["mix_norm"] = mix_in_bwd(dq, dkv, dz, dgt, dh2, h1, S["mix_norm"], W["w_in"])
    g_mix["w_in"] = jnp.concatenate([
        matmul_tn(dq, u, "dw_in_q", tm=AW, tn=D),
        matmul_tn(dkv, u, "dw_in_kv", tm=2 * KVW, tn=D),
        matmul_tn(dz, u, "dw_in_z", tm=PW, tn=D),
        matmul_tn(dgt, u, "dw_in_g", tm=D, tn=512, tt=4096),
    ], axis=0)
    mix = tuple(g_mix)
    token = comm.grads(g_mix)

    W = comm.weights(GROUP_FFN1, None)
    g_dn1 = matmul_tn(hid1, dh1, "ffn1_dw_down", tm=1408, tn=512, tt=4096, b_scale=0.5, after=token)
    token = comm.advance(mix, g_dn1)
    token2 = comm.grads({"ffn1_w_down": g_dn1})
    g1 = _behind(_behind(S["ffn1_norm"], token), token2)
    dx, dup1, n1, G["ffn1_norm"], *outs = ffn_bwd_x(dh1, x, ab1, g1, W["ffn1_w_up"], W["ffn1_w_down"], "ffn1_bwd_x",
                                                   updates=comm.reduce(ffn2, token2))
    comm.updated(ffn2, outs)
    token = comm.advance(("ffn1_w_down",), dx)
    token2 = comm.small_start(G, loss, dx)
    g_up1 = matmul_tn(n1, dup1, "ffn1_dw_up", tm=D, tn=512, tt=4096,
                      after=None if token is None else token + token2)
    token = comm.grads({"ffn1_w_up": g_up1})
    done = comm.finish(("ffn1_w_down",), token)
    token = comm.advance(("ffn1_w_up",), done)
    done = comm.finish(mix, token)
    done = comm.small_finish(done)
    comm.finish(("ffn1_w_up",), done)
    return loss, dx


HBM_SPEC = pl.BlockSpec(memory_space=pltpu.HBM)


def _me():
    return lax.axis_index("x"), lax.axis_index("y"), lax.axis_index("c")


def _peer_chip(x, y, k):
    return x ^ (k >> 1), y ^ (k & 1)


def _piece_half(ref, rowlike, j, h):
    if rowlike:
        return ref.at[j, h]
    ns = ref.shape[-1] // N_CHIPS
    return ref.at[h, :, pl.ds(pl.multiple_of(j * ns, 128), ns)]


def _piece(ref, rowlike, j):
    if rowlike:
        return ref.at[j]
    ns = ref.shape[-1] // N_CHIPS
    return ref.at[:, :, pl.ds(pl.multiple_of(j * ns, 128), ns)]


def _full_shape(shard_view, rowlike):
    _, kh, ns = shard_view.shape
    return (N_CHIPS, 2, kh, ns) if rowlike else (2, kh, N_CHIPS * ns)


def _remote(src, dst, send_sem, recv_sem, dev):
    return pltpu.make_async_remote_copy(src, dst, send_sem, recv_sem, device_id=dev, device_id_type=MESH)


SEM_SPEC = pl.BlockSpec(memory_space=pltpu.SEMAPHORE)
SPLIT_PARAMS = pltpu.CompilerParams(has_side_effects=pltpu.SideEffectType.DATAFLOW_SIDE_EFFECTING)


def _gather_plan(rowlikes):
    def plan(s_refs, f_refs, a, k, x, y, c):
        px, py = _peer_chip(x, y, k)
        return (s_refs[a].at[c], _piece_half(f_refs[a], rowlikes[a], 2 * x + y, c),
                _piece_half(f_refs[a], rowlikes[a], 2 * px + py, c), (px, py, c))
    return plan


def _all_to_all_plan(rowlikes):
    def plan(q_refs, r_refs, a, k, x, y, c):
        px, py = _peer_chip(x, y, k)
        if rowlikes[a]:
            src = q_refs[a].at[2 * px + py]
        else:
            ns = q_refs[a].shape[-1] // N_CHIPS
            src = q_refs[a].at[:, pl.ds(pl.multiple_of((2 * px + py) * ns, 128), ns)]
        return src, r_refs[a].at[k - 1], r_refs[a].at[k - 1], (px, py, c)
    return plan


def _swap_plan(rowlikes):
    def plan(g_refs, got_refs, a, k, x, y, c):
        src = g_refs[a].at[:, 1 - c] if rowlikes[a] else g_refs[a].at[1 - c]
        return src, got_refs[a], got_refs[a], (x, y, 1 - c)
    return plan


def _everyone_plan(s_refs, slot_refs, a, k, x, y, c):
    px, py, pc = x ^ (k >> 2), y ^ ((k >> 1) & 1), c ^ (k & 1)
    return s_refs[a], slot_refs[a].at[4 * x + 2 * y + c], slot_refs[a].at[4 * px + 2 * py + pc], (px, py, pc)


def _forward_plan(rowlikes):
    def plan(s_refs, f_refs, a, k, x, y, c):
        sib = (x, y, 1 - c)
        if k == 0:
            own = _piece(f_refs[a], rowlikes[a], 2 * x + y)
            return s_refs[a], own, own, sib
        px, py = _peer_chip(x, y, k)
        mine = _piece_half(f_refs[a], rowlikes[a], 2 * px + py, c)
        return mine, mine, _piece_half(f_refs[a], rowlikes[a], 2 * px + py, 1 - c), sib
    return plan


CHIPS, SIBLING, EVERYONE, FORWARDS = (1, 2, 3), (1,), tuple(range(1, 8)), (0, 1, 2, 3)


def _as_list(after):
    return [] if after is None else list(after) if isinstance(after, (list, tuple)) else [after]


def exchange_start(srcs, land_shapes, plan, after, name, peers=CHIPS):
    n = len(srcs)
    lands = [l if hasattr(l, "dtype") else lax.empty(l, s.dtype) for l, s in zip(land_shapes, srcs)]

    behind = _as_list(after)

    def body(*refs):
        s_refs, l_refs = refs[:n], refs[n:2 * n]
        send_sems, recv_sems = refs[2 * n + len(behind)], refs[2 * n + len(behind) + 1]
        token = refs[-1]
        x, y, c = _me()
        for a in range(n):
            for i, k in enumerate(peers):
                src, dst, _, peer = plan(s_refs, l_refs, a, k, x, y, c)
                sem = len(peers) * a + i
                _remote(src, dst, send_sems.at[sem], recv_sems.at[sem], peer).start()
        token[...] = jnp.zeros_like(token)

    n_sems = len(peers) * n
    outs = pl.pallas_call(
        body, name=name, in_specs=[HBM_SPEC] * (2 * n) + [pl.BlockSpec(memory_space=pl.ANY)] * len(behind),
        out_specs=[SEM_SPEC, SEM_SPEC] + [HBM_SPEC] * (2 * n) + [pl.BlockSpec(memory_space=pltpu.VMEM)],
        out_shape=[pltpu.SemaphoreType.DMA((n_sems,)), pltpu.SemaphoreType.DMA((n_sems,))]
        + [pltpu.HBM(a.shape, a.dtype) for a in (*srcs, *lands)] + [SDS((8, 128), F32)],
        input_output_aliases={i: 2 + i for i in range(2 * n)},
        compiler_params=SPLIT_PARAMS,
    )(*[pltpu.with_memory_space_constraint(a, pltpu.HBM) for a in (*srcs, *lands)], *behind)
    return {"sems": outs[:2], "srcs": outs[2:2 + n], "lands": outs[2 + n:2 + 2 * n], "token": outs[-1]}


def exchange_wait(state, plan, after, name, peers=CHIPS):
    n = len(state["srcs"])
    behind = _as_list(after)

    def body(*refs):
        s_refs, l_refs = refs[:n], refs[n:2 * n]
        send_sems, recv_sems = refs[2 * n], refs[2 * n + 1]
        x, y, c = _me()
        for a in range(n):
            for i, k in enumerate(peers):
                src, _, landing, peer = plan(s_refs, l_refs, a, k, x, y, c)
                sem = len(peers) * a + i
                cp = _remote(src, landing, send_sems.at[sem], recv_sems.at[sem], peer)
                cp.wait_send()
                cp.wait_recv()

    bufs = (*state["srcs"], *state["lands"])
    outs = pl.pallas_call(
        body, name=name,
        in_specs=[HBM_SPEC] * (2 * n) + [SEM_SPEC, SEM_SPEC] + [pl.BlockSpec(memory_space=pl.ANY)] * len(behind),
        out_specs=[HBM_SPEC] * (2 * n),
        out_shape=[pltpu.HBM(a.shape, a.dtype) for a in bufs],
        input_output_aliases={i: i for i in range(2 * n)},
        compiler_params=SPLIT_PARAMS,
    )(*bufs, *state["sems"], *behind)
    return outs[:n], outs[n:]


def gather_finish(shards, fulls, rowlikes, name):
    n = len(shards)

    def body(*refs):
        s_refs, f_refs = refs[:n], refs[2 * n:3 * n]
        send_sems, recv_sems = refs[3 * n:]
        x, y, c = _me()
        chip = 2 * x + y
        sib = (x, y, 1 - c)
        sends = []
        for a in range(n):
            own = _piece(f_refs[a], rowlikes[a], chip)
            cp = _remote(s_refs[a], own, send_sems.at[a, 0], recv_sems.at[a, 0], sib)
            cp.start()
            sends.append(cp)
            for k in (1, 2, 3):
                px, py = _peer_chip(x, y, k)
                slot = _piece_half(f_refs[a], rowlikes[a], 2 * px + py, c)
                cp = _remote(slot, slot, send_sems.at[a, k], recv_sems.at[a, k], sib)
                cp.start()
                sends.append(cp)
        for a in range(n):
            own = _piece(f_refs[a], rowlikes[a], chip)
            _remote(own, own, send_sems.at[a, 0], recv_sems.at[a, 0], sib).wait_recv()
            for k in (1, 2, 3):
                px, py = _peer_chip(x, y, k)
                slot = _piece_half(f_refs[a], rowlikes[a], 2 * px + py, 1 - c)
                _remote(slot, slot, send_sems.at[a, k], recv_sems.at[a, k], sib).wait_recv()
        for cp in sends:
            cp.wait_send()

    return pl.pallas_call(
        body, name=name, in_specs=[HBM_SPEC] * (2 * n), out_specs=[HBM_SPEC] * n,
        out_shape=[SDS(f.shape, f.dtype) for f in fulls],
        input_output_aliases={n + a: a for a in range(n)},
        scratch_shapes=[pltpu.SemaphoreType.DMA((n, 4)), pltpu.SemaphoreType.DMA((n, 4))],
    )(*shards, *fulls)


def _half_buffer_shape(gview, rowlike):
    return (N_CHIPS,) + gview.shape[2:] if rowlike else gview.shape[1:]


def _row_tiles(kh, ns):
    return 1 if kh * ns <= 256 * 1024 else 2


def _run_jobs(jobs, place, name):
    n_steps = max(job[0] for job in jobs)

    def spec(block, index, steps):
        return pl.BlockSpec(block, lambda s, p: index(jnp.minimum(s, steps - 1), p))

    in_specs = [spec(b, ix, steps) for steps, ins, _, _ in jobs for _, b, ix in ins]
    out_specs = [spec(b, ix, steps) for steps, _, outs, _ in jobs for _, _, b, ix in outs]

    def body(place_ref, *refs):
        s = pl.program_id(0)
        i, o = 0, len(in_specs)
        for steps, ins, outs, fn in jobs:
            in_refs, out_refs = refs[i:i + len(ins)], refs[o:o + len(outs)]
            i, o = i + len(ins), o + len(outs)

            @pl.when(s < steps)
            def _(fn=fn, in_refs=in_refs, out_refs=out_refs):
                fn(in_refs, out_refs)

    return pl.pallas_call(
        body, name=name,
        grid_spec=pltpu.PrefetchScalarGridSpec(num_scalar_prefetch=1, grid=(n_steps,), in_specs=in_specs,
                                               out_specs=out_specs),
        out_shape=[SDS(shape, dtype) for _, _, outs, _ in jobs for shape, dtype, _, _ in outs],
        compiler_params=_cp(("arbitrary",)),
    )(place, *[a for _, ins, _, _ in jobs for a, _, _ in ins])


def add_halves(gviews, gots, rowlikes, place, name):
    def add(ins, outs):
        outs[0][...] = (ins[0][...].astype(F32) + ins[1][...].astype(F32)).astype(BF16)

    jobs = []
    for g, got, rowlike in zip(gviews, gots, rowlikes):
        kh, ns = (g.shape[2], g.shape[3]) if rowlike else (g.shape[1], g.shape[2] // N_CHIPS)
        rt = _row_tiles(kh, ns)
        tr = kh // rt
        if rowlike:
            g_in = (g, (None, None, tr, ns), lambda ls, p, rt=rt: (ls // rt, p[0], ls % rt, 0))
            half = ((None, tr, ns), lambda ls, p, rt=rt: (ls // rt, ls % rt, 0))
        else:
            g_in = (g, (None, tr, ns), lambda ls, p, rt=rt: (p[0], ls % rt, ls // rt))
            half = ((tr, ns), lambda ls, p, rt=rt: (ls % rt, ls // rt))
        jobs.append((N_CHIPS * rt, [g_in, (got, *half)], [(got.shape, BF16, *half)], add))
    return _run_jobs(jobs, place, name)


def _slot_shape(q, rowlike):
    return (3,) + (q.shape[1:] if rowlike else (q.shape[0], q.shape[1] // N_CHIPS))


def sum_pieces(qs, recvs, rowlikes, place, name):
    def total(ins, outs):
        acc = ins[0][...].astype(F32)
        for k in range(3):
            acc = acc + ins[1][k].astype(F32)
        outs[0][...] = acc

    jobs = []
    for q, recv, rowlike in zip(qs, recvs, rowlikes):
        kh, ns = recv.shape[1], recv.shape[2]
        rt = _row_tiles(kh, ns)
        tr = kh // rt
        mine = ((None, tr, ns), lambda ls, p: (p[1], ls, 0)) if rowlike else ((tr, ns), lambda ls, p: (ls, p[1]))
        jobs.append((rt, [(q, *mine), (recv, (3, tr, ns), lambda ls, p: (0, ls, 0))],
                     [((2, kh, ns), F32, (None, tr, ns), lambda ls, p: (p[0], ls, 0))], total))
    return _run_jobs(jobs, place, name)


def join_halves(halves, name):
    n = len(halves)

    def body(*refs):
        o_refs = refs[n:2 * n]
        send_sems, recv_sems = refs[2 * n:]
        x, y, c = _me()
        sib = (x, y, 1 - c)
        sends = []
        for a in range(n):
            cp = _remote(o_refs[a].at[c], o_refs[a].at[c], send_sems.at[a], recv_sems.at[a], sib)
            cp.start()
            sends.append(cp)
        for a in range(n):
            got = o_refs[a].at[1 - c]
            _remote(got, got, send_sems.at[a], recv_sems.at[a], sib).wait_recv()
        for cp in sends:
            cp.wait_send()

    return pl.pallas_call(
        body, name=name, in_specs=[HBM_SPEC] * n, out_specs=[HBM_SPEC] * n,
        out_shape=[SDS(h.shape, h.dtype) for h in halves],
        input_output_aliases={a: a for a in range(n)},
        scratch_shapes=[pltpu.SemaphoreType.DMA((n,)), pltpu.SemaphoreType.DMA((n,))],
    )(*halves)


N_DEV = 8


def sum_devices(s, slots, me):
    R, Wd = s.shape

    def body(me_ref, s_ref, slots_ref, out_ref):
        acc = None
        for d in range(N_DEV):
            mine = me_ref[0] == d
            term = jnp.where(mine, s_ref[...], slots_ref[jnp.where(mine, d ^ 1, d)])
            acc = term if acc is None else acc + term
        out_ref[...] = acc

    vmem = pl.BlockSpec(memory_space=pltpu.VMEM)
    return pl.pallas_call(
        body, name="sum_devices", in_specs=[pl.BlockSpec(memory_space=pltpu.SMEM), vmem, vmem], out_specs=vmem,
        out_shape=SDS((R, Wd), F32),
    )(me, s, slots)


TRANSPOSED = ("w_in",)
BIG = {"ffn1_w_up": (D, 2 * FF, "col"), "ffn1_w_down": (FF, D, "row"), "w_in": (INW, D, "row"),
       "w_attn_up": (AW, D, "row"), "w_pool_up": (PW, D, "col"), "w_out": (D, D, "row"),
       "ffn2_w_up": (D, 2 * FF, "col"), "ffn2_w_down": (FF, D, "row")}
GROUPS = (("ffn1_w_up", "ffn1_w_down"), ("w_in", "w_attn_up", "w_pool_up", "w_out"), ("ffn2_w_up", "ffn2_w_down"))
SMALL = ("ffn1_norm", "mix_norm", "ffn2_norm", "final_norm", "pool_scale", "sinks", "pool_w_mix")
SMALL_W = 128


def _rowlike(name):
    return BIG[name][2] == "row"


def _half_dims(name):
    k, n, kind = BIG[name]
    return (k // N_CHIPS // 2, n) if kind == "row" else (k // 2, n // N_CHIPS)


def shard_view(name, shard):
    return shard.reshape((2,) + _half_dims(name))


def full_from_view(name, fv):
    k, n, _ = BIG[name]
    return fv.reshape(k, n)


def grad_view(name, g):
    kh, ns = _half_dims(name)
    return g.reshape(_full_shape(jax.ShapeDtypeStruct((2, kh, ns), g.dtype), _rowlike(name)))


def pack_small(d):
    parts = []
    for name in SMALL:
        a = d[name].reshape(-1)
        pad = (-a.shape[0]) % SMALL_W
        parts.append(jnp.pad(a, (0, pad)).reshape(-1, SMALL_W))
    a = jnp.concatenate(parts, axis=0)
    return jnp.pad(a, ((0, (-a.shape[0]) % 8), (0, 0)))


def unpack_small(a, like):
    out, r0 = {}, 0
    for name in SMALL:
        size = int(np.prod(like[name].shape))
        rows = -(-size // SMALL_W)
        out[name] = a[r0:r0 + rows].reshape(-1)[:size].reshape(like[name].shape)
        r0 += rows
    return out


WEIGHTS = ("ffn1_norm", "ffn1_w_up", "ffn1_w_down", "mix_norm", "w_in", "sinks", "w_attn_up", "pool_w_mix",
           "pool_scale", "w_pool_up", "w_out", "ffn2_norm", "ffn2_w_up", "ffn2_w_down", "final_norm")


def kernel(x, ffn1_norm, ffn1_w_up, ffn1_w_down, mix_norm, w_in, sinks, w_attn_up, pool_w_mix, pool_scale, w_pool_up, w_out, ffn2_norm, ffn2_w_up, ffn2_w_down, final_norm, loss_target, m_ffn1_norm, m_ffn1_w_up, m_ffn1_w_down, m_mix_norm, m_w_in, m_sinks, m_w_attn_up, m_pool_w_mix, m_pool_scale, m_w_pool_up, m_w_out, m_ffn2_norm, m_ffn2_w_up, m_ffn2_w_down, m_final_norm, v_ffn1_norm, v_ffn1_w_up, v_ffn1_w_down, v_mix_norm, v_w_in, v_sinks, v_w_attn_up, v_pool_w_mix, v_pool_scale, v_w_pool_up, v_w_out, v_ffn2_norm, v_ffn2_w_up, v_ffn2_w_down, v_final_norm):
    given = dict(locals())
    w = {n: given[n] for n in WEIGHTS}
    m = {n: given["m_" + n] for n in WEIGHTS}
    v = {n: given["v_" + n] for n in WEIGHTS}
    cx, cy, cc = _me()
    place = jnp.stack([cc, 2 * cx + cy]).astype(jnp.int32)

    def local2d(d, n):
        return d[n][0].T if n in TRANSPOSED else d[n][0]

    shards = {n: local2d(w, n) for n in BIG}
    grads, delta, new_m, new_v = {}, {}, {}, {}
    rowlikes = [[_rowlike(n) for n in names] for names in GROUPS]

    class Exchanges:
        def __init__(self):
            self.gathers, self.forwards, self.fulls, self.reductions, self.small = {}, {}, {}, {}, None

        def _behind_first(self, a):
            return a + self.gathers[0]["token"][0, 0]

        def _start_gather(self, group, after):
            prepare = self._behind_first if group else (lambda a: a)
            sv = [shard_view(n, prepare(shards[n]).astype(BF16)) for n in GROUPS[group]]
            rl = rowlikes[group]
            self.gathers[group] = exchange_start(sv, [_full_shape(s, r) for s, r in zip(sv, rl)], _gather_plan(rl),
                                                 after, f"gather_start_{group}")

        def weights(self, group, after):
            if not self.gathers:
                self._start_gather(0, None)
                self.packed = [self._behind_first(pack_small(d)) for d in (w, m, v)]
                after = [self.gathers[0]["token"], *self.packed]
            if group not in self.fulls:
                names, rl, state = GROUPS[group], rowlikes[group], self.gathers[group]
                if group in self.forwards:
                    _, fulls = exchange_wait(self.forwards.pop(group), _forward_plan(rl), after,
                                             f"forward_wait_{group}", FORWARDS)
                else:
                    sv, fulls = exchange_wait(state, _gather_plan(rl), state["token"] if after is None else after,
                                              f"gather_wait_{group}")
                    fulls = gather_finish(sv, fulls, rl, f"gather_finish_{group}")
                self.fulls[group] = {n: full_from_view(n, f) for n, f in zip(names, fulls)}
                if group + 1 < len(GROUPS):
                    self._start_gather(group + 1, fulls[0])
            return self.fulls[group]

        def started(self):
            return self.gathers[max(self.gathers)]["token"]

        def prefetch(self, group, after):
            rl, state = rowlikes[group], self.gathers[group]
            sv, fulls = exchange_wait(state, _gather_plan(rl), after, f"gather_wait_{group}")
            self.forwards[group] = exchange_start(sv, fulls, _forward_plan(rl), None, f"forward_start_{group}", FORWARDS)
            return self.forwards[group]["token"]

        def grads(self, g):
            names = tuple(g)
            rl = [_rowlike(n) for n in names]
            gv = [grad_view(n, g[n]) for n in names]
            state = exchange_start(gv, [_half_buffer_shape(a, r) for a, r in zip(gv, rl)], _swap_plan(rl), None,
                                   "swap_start_" + names[0], SIBLING)
            self.reductions[names] = state
            return state["token"]

        def advance(self, names, after):
            rl = [_rowlike(n) for n in names]
            gv, gots = exchange_wait(self.reductions[names], _swap_plan(rl), after, "swap_wait_" + names[0], SIBLING)
            qs = add_halves(gv, gots, rl, place, "add_halves_" + names[0])
            state = exchange_start(qs, [_slot_shape(q, r) for q, r in zip(qs, rl)], _all_to_all_plan(rl), None,
                                   "all_to_all_start_" + names[0])
            self.reductions[names] = state
            return state["token"]

        def reduce(self, names, after):
            rl = [_rowlike(n) for n in names]
            qs, recvs = exchange_wait(self.reductions.pop(names), _all_to_all_plan(rl), after,
                                      "all_to_all_wait_" + names[0])
            halves = sum_pieces(qs, recvs, rl, place, "sum_pieces_" + names[0])
            return [(shards[n], o.reshape(shards[n].shape), local2d(m, n), local2d(v, n))
                    for n, o in zip(names, join_halves(halves, "join_halves_" + names[0]))]

        def updated(self, names, outs):
            for i, n in enumerate(names):
                grads[n], delta[n], new_m[n], new_v[n] = outs[4 * i:4 * i + 4]
            return [new_v[n] for n in names]

        def finish(self, names, after):
            updates = self.reduce(names, after)
            return self.updated(names, adamw(updates, place, "adamw_" + names[0]))

        def small_start(self, G, loss, after):
            packed = pack_small({n: G[n] for n in SMALL})
            used_rows = sum(-(-int(np.prod(small_like[n].shape)) // SMALL_W) for n in SMALL)
            assert packed.shape[0] > used_rows
            packed = packed.at[-1, 0].set(loss[0, 0])
            self.small = exchange_start([packed], [(N_DEV,) + packed.shape], _everyone_plan, after,
                                        "small_start", EVERYONE)
            return self.small["token"]

        def small_finish(self, after):
            (packed,), (slots,) = exchange_wait(self.small, _everyone_plan, after, "small_wait", EVERYONE)
            total = sum_devices(packed, slots, (4 * cx + 2 * cy + cc).astype(jnp.int32).reshape(1))
            self.loss = total[-1, 0]
            g, ds, ms, vs = adamw([(self.packed[0], total, self.packed[1], self.packed[2])], place, "adamw_small")
            for d, packed_d in ((grads, g), (delta, ds), (new_m, ms), (new_v, vs)):
                d.update(unpack_small(packed_d, small_like))
            return vs

    small_like = {n: w[n] for n in SMALL}
    S = {n: w[n].reshape(1, -1) for n in ("ffn1_norm", "mix_norm", "ffn2_norm", "final_norm", "pool_scale", "sinks")}
    S["pool_w_mix"] = w["pool_w_mix"][0].astype(BF16)
    exchanges = Exchanges()
    _, dx = local_fwd_bwd(x[0], loss_target[0], S, exchanges)
    loss = exchanges.loss

    def shaped(d, n):
        return (d[n].T if n in TRANSPOSED else d[n]).reshape(w[n].shape)

    return (loss, dx[None], *[shaped(grads, n) for n in WEIGHTS], *[shaped(delta, n) for n in WEIGHTS],
            *[shaped(new_m, n) for n in WEIGHTS], *[shaped(new_v, n) for n in WEIGHTS])
```

```python
import numpy as np
import jax
import jax.numpy as jnp
from jax import lax
from jax.experimental import pallas as pl
from jax.experimental.pallas import tpu as pltpu

F32 = jnp.float32
BF16 = jnp.bfloat16
SDS = jax.ShapeDtypeStruct
MESH = pl.DeviceIdType.MESH

D = 1024
FF = 2816
NQ = 16
NKV = 2
HD = 64
GQ = NQ // NKV
AW = NQ * HD
KVW = NKV * HD
BLK = 128
PW = 512
PG = 128
POOL_WINDOWS = (2, 4, 8, 16)
HALO = 16
INW = AW + 2 * KVW + PW + 2 * D
C_KV = AW
C_Z = AW + 2 * KVW
C_G = C_Z + PW
EPS = 1e-6
FF_CHUNK = 256
FF_CHUNKS = tuple((c, FF_CHUNK) for c in range(0, FF, FF_CHUNK))
SLOPES = tuple(float(2.0 ** (-8.0 * h / NQ)) for h in range(1, NQ + 1))
SCALE = HD ** -0.5

LR, B1, B2, ADAM_EPS, WD, STEP = 0.001, 0.9, 0.999, 1e-08, 0.01, 10

VMEM_LIMIT = 56 * 1024 * 1024
N_CHIPS = 4

NT = (((1,), (1,)), ((), ()))
TN = (((0,), (0,)), ((), ()))


def _cp(sem=None, vmem=VMEM_LIMIT):
    return pltpu.CompilerParams(dimension_semantics=sem, vmem_limit_bytes=vmem)


def _const_spec(shape):
    nd = len(shape)
    return pl.BlockSpec(shape, lambda *_: (0,) * nd, pipeline_mode=pl.Buffered(1))


def _rstd(x):
    return lax.rsqrt(jnp.mean(x * x, axis=-1, keepdims=True) + EPS)


def _rms_bwd(dn, xhat, rstd, g):
    dxhat = dn * g
    return rstd * (dxhat - xhat * jnp.mean(dxhat * xhat, axis=-1, keepdims=True))


def _dot(a, b):
    return jnp.dot(a, b, preferred_element_type=F32)


def _dot_nt(a, b):
    return lax.dot_general(a, b, NT, preferred_element_type=F32)


def _dot_tn(a, b):
    return lax.dot_general(a, b, TN, preferred_element_type=F32)


def ffn_fwd(h, g, wup, wdn, name, head=None):
    T = h.shape[0]
    TM = 512
    tile = lambda w: pl.BlockSpec((TM, w), lambda i: (i, 0))
    acc_spec = lambda w: pl.BlockSpec((1, w), lambda i: (0, 0))

    def body(h_ref, g_ref, wup_ref, wdn_ref, *rest):
        out_ref, ab_ref, hid_ref = rest[-3:] if head is None else rest[2:5]
        x = h_ref[...]
        n = (x * _rstd(x) * g_ref[...]).astype(BF16)
        for c0, w in FF_CHUNKS:
            a = _dot(n, wup_ref[:, c0:c0 + w])
            b = _dot(n, wup_ref[:, FF + c0:FF + c0 + w])
            sig = jax.nn.sigmoid(a)
            s = a * sig
            ab_ref[:, c0:c0 + w] = (b * (sig * (1.0 + a * (1.0 - sig)))).astype(BF16)
            ab_ref[:, FF + c0:FF + c0 + w] = s.astype(BF16)
            hid_ref[:, c0:c0 + w] = (s * b).astype(BF16)
        out = x + 0.5 * _dot(hid_ref[...], wdn_ref[...])
        if head is None:
            out_ref[...] = out
        else:
            t_ref, gf_ref, loss_ref, dgf_ref = rest[0], rest[1], rest[5], rest[6]
            out_ref[...] = _loss_head(out, t_ref[...], gf_ref[...], loss_ref, dgf_ref, pl.program_id(0) == 0)

    head_in, head_specs, head_out_specs, head_out_shape = [], [], [], []
    if head is not None:
        head_in, head_specs = list(head), [tile(D), _const_spec((1, D))]
        head_out_specs, head_out_shape = [acc_spec(1), acc_spec(D)], [SDS((1, 1), F32), SDS((1, D), F32)]
    return pl.pallas_call(
        body, name=name, grid=(T // TM,),
        in_specs=[tile(D), _const_spec((1, D)), _const_spec((D, 2 * FF)), _const_spec((FF, D))] + head_specs,
        out_specs=[tile(D), tile(2 * FF), tile(FF)] + head_out_specs,
        out_shape=[SDS((T, D), F32), SDS((T, 2 * FF), BF16), SDS((T, FF), BF16)] + head_out_shape,
        compiler_params=_cp(("arbitrary",)),
    )(h, g, wup, wdn, *head_in)


def _loss_head(x, target, g, loss_ref, dg_ref, first):
    @pl.when(first)
    def _():
        loss_ref[...] = jnp.zeros_like(loss_ref)
        dg_ref[...] = jnp.zeros_like(dg_ref)

    rstd = _rstd(x)
    xhat = x * rstd
    err = xhat * g - target
    loss_ref[...] += 0.5 * jnp.sum(jnp.mean(err * err, axis=-1, keepdims=True), axis=0, keepdims=True)
    dy = err * (1.0 / D)
    dg_ref[...] += jnp.sum(dy * xhat, axis=0, keepdims=True)
    return _rms_bwd(dy, xhat, rstd, g)


def _adamw_tile(w_ref, g_ref, m_ref, v_ref, go_ref, d_ref, nm_ref, nv_ref):
    gv = g_ref[...]
    go_ref[...] = gv
    nm = B1 * m_ref[...] + (1.0 - B1) * gv
    nv = B2 * v_ref[...] + (1.0 - B2) * (gv * gv)
    nm_ref[...] = nm
    nv_ref[...] = nv
    d_ref[...] = -LR * ((nm / (1.0 - B1 ** STEP)) / (jnp.sqrt(nv / (1.0 - B2 ** STEP)) + ADAM_EPS) + WD * w_ref[...])


def _riders(updates, steps, step_of):
    in_specs, out_specs, out_shapes, operands, tiles = [], [], [], [], []
    for w, g, m, v in updates:
        R, C = w.shape
        n = max(d for d in range(1, steps + 1) if R % d == 0 and (R // d) % 8 == 0)
        spec = pl.BlockSpec((R // n, C), lambda *ids, n=n: (jnp.minimum(step_of(*ids), n - 1), 0))
        in_specs += [spec] * 4
        out_specs += [spec] * 4
        out_shapes += [SDS((R, C), F32)] * 4
        operands += [w, g, m, v]
        tiles.append(n)

    def run(step, in_refs, out_refs):
        for u, n in enumerate(tiles):
            @pl.when(step < n)
            def _(u=u):
                _adamw_tile(*in_refs[4 * u:4 * u + 4], *out_refs[4 * u:4 * u + 4])

    return in_specs, out_specs, out_shapes, operands, run


def ffn_bwd_x(dh, h_in, ab, g, wup, wdn, name, updates=()):
    T = dh.shape[0]
    TM = 256
    r_in, r_out, r_shapes, r_args, ride = _riders(updates, T // TM, lambda i: i)

    def body(dh_ref, h_ref, ab_ref, g_ref, wup_ref, wdn_ref, *rest):
        dhin_ref, dup_ref, n_ref, dg_ref = rest[len(r_in):len(r_in) + 4]
        ride(pl.program_id(0), rest[:len(r_in)], rest[len(r_in) + 4:])
        x = h_ref[...]
        g = g_ref[...]
        rstd = _rstd(x)
        xhat = x * rstd
        n_ref[...] = (xhat * g).astype(BF16)
        dh = dh_ref[...]
        dhh = (0.5 * dh).astype(BF16)
        for c0, w in FF_CHUNKS:
            dhid = _dot_nt(dhh, wdn_ref[c0:c0 + w, :])
            dup_ref[:, c0:c0 + w] = (dhid * ab_ref[:, c0:c0 + w].astype(F32)).astype(BF16)
            dup_ref[:, FF + c0:FF + c0 + w] = (dhid * ab_ref[:, FF + c0:FF + c0 + w].astype(F32)).astype(BF16)
        dn = _dot_nt(dup_ref[...], wup_ref[...])
        dhin_ref[...] = dh + _rms_bwd(dn, xhat, rstd, g)

        @pl.when(pl.program_id(0) == 0)
        def _():
            dg_ref[...] = jnp.zeros_like(dg_ref)

        dg_ref[...] += jnp.sum(dn * xhat, axis=0, keepdims=True)

    tile = lambda w: pl.BlockSpec((TM, w), lambda i: (i, 0))
    return pl.pallas_call(
        body, name=name, grid=(T // TM,),
        in_specs=[tile(D), tile(D), tile(2 * FF), _const_spec((1, D)), _const_spec((D, 2 * FF)), _const_spec((FF, D))]
        + r_in,
        out_specs=[tile(D), tile(2 * FF), tile(D), pl.BlockSpec((1, D), lambda i: (0, 0))] + r_out,
        out_shape=[SDS((T, D), F32), SDS((T, 2 * FF), BF16), SDS((T, D), BF16), SDS((1, D), F32)] + r_shapes,
        compiler_params=_cp(("arbitrary",)),
    )(dh, h_in, ab, g, wup, wdn, *r_args)


TOKEN_SPEC = pl.BlockSpec((8, 128), lambda *_: (0, 0))


def _token_operand(token):
    return ([], []) if token is None else ([TOKEN_SPEC], [token])


def matmul_tn(a, b, name, *, tm, tn, tt=1024, b_scale=None, after=None):
    T, M = a.shape
    N = b.shape[1]
    tt = min(tt, T)
    assert M % tm == 0 and N % tn == 0 and T % tt == 0
    nt = T // tt
    token_spec, token_arg = _token_operand(after)

    def body(a_ref, b_ref, *rest):
        o_ref, acc_ref = rest[-2:]
        t = pl.program_id(2)

        @pl.when(t == 0)
        def _():
            acc_ref[...] = jnp.zeros_like(acc_ref)

        bv = b_ref[...]
        if b_scale is not None:
            bv = bv * b_scale
        acc_ref[...] += _dot_tn(a_ref[...].astype(BF16), bv.astype(BF16))

        @pl.when(t == nt - 1)
        def _():
            o_ref[...] = acc_ref[...].astype(BF16)

    return pl.pallas_call(
        body, name=name, grid=(M // tm, N // tn, nt),
        in_specs=[pl.BlockSpec((tt, tm), lambda i, j, t: (t, i)), pl.BlockSpec((tt, tn), lambda i, j, t: (t, j))]
        + token_spec,
        out_specs=pl.BlockSpec((tm, tn), lambda i, j, t: (i, j)),
        out_shape=SDS((M, N), BF16),
        scratch_shapes=[pltpu.VMEM((tm, tn), F32)],
        compiler_params=_cp(("parallel", "parallel", "arbitrary")),
    )(a, b, *token_arg)


def mix_in_fwd(h1, g, win_t):
    T = h1.shape[0]
    TM = 512

    def body(h_ref, g_ref, w_ref, u_ref, q_ref, kv_ref, z_ref, gt_ref):
        x = h_ref[...]
        u = (x * _rstd(x) * g_ref[...]).astype(BF16)
        u_ref[...] = u
        for c in range(0, AW, 256):
            q_ref[:, c:c + 256] = _dot_nt(u, w_ref[c:c + 256, :]).astype(BF16)
        kv_ref[...] = _dot_nt(u, w_ref[C_KV:C_Z, :]).astype(BF16)
        for c in range(0, PW, 256):
            z_ref[:, c:c + 256] = _dot_nt(u, w_ref[C_Z + c:C_Z + c + 256, :])
        for c in range(0, 2 * D, 256):
            gt_ref[:, c:c + 256] = _dot_nt(u, w_ref[C_G + c:C_G + c + 256, :]).astype(BF16)

    tile = lambda w: pl.BlockSpec((TM, w), lambda i: (i, 0))
    return pl.pallas_call(
        body, name="mix_in_fwd", grid=(T // TM,),
        in_specs=[tile(D), _const_spec((1, D)), _const_spec((INW, D))],
        out_specs=[tile(D), tile(AW), tile(2 * KVW), tile(PW), tile(2 * D)],
        out_shape=[SDS((T, D), BF16), SDS((T, AW), BF16), SDS((T, 2 * KVW), BF16), SDS((T, PW), F32),
                   SDS((T, 2 * D), BF16)],
        compiler_params=_cp(("arbitrary",)),
    )(h1, g, win_t)


def mix_in_bwd(dq, dkv, dz, dgt, dh2, h1, g, win_t):
    T = h1.shape[0]
    TM = min(1024, T)
    SUB = 256

    def body(dq_ref, dkv_ref, dz_ref, dgt_ref, dh2_ref, h_ref, g_ref, w_ref, dh1_ref, dg_ref):
        g = g_ref[...]
        dg = None
        for r0 in range(0, TM, SUB):
            rows = slice(r0, r0 + SUB)
            du = _dot(dq_ref[rows, :], w_ref[0:AW, :])
            du += _dot(dkv_ref[rows, :], w_ref[C_KV:C_Z, :])
            du += _dot(dz_ref[rows, :], w_ref[C_Z:C_G, :])
            du += _dot(dgt_ref[rows, :], w_ref[C_G:INW, :])
            x = h_ref[rows, :]
            rstd = _rstd(x)
            xhat = x * rstd
            dh1_ref[rows, :] = dh2_ref[rows, :] + _rms_bwd(du, xhat, rstd, g)
            part = jnp.sum(du * xhat, axis=0, keepdims=True)
            dg = part if dg is None else dg + part

        @pl.when(pl.program_id(0) == 0)
        def _():
            dg_ref[...] = jnp.zeros_like(dg_ref)

        dg_ref[...] += dg

    tile = lambda w: pl.BlockSpec((TM, w), lambda i: (i, 0))
    return pl.pallas_call(
        body, name="mix_in_bwd", grid=(T // TM,),
        in_specs=[tile(AW), tile(2 * KVW), tile(PW), tile(2 * D), tile(D), tile(D), _const_spec((1, D)),
                  _const_spec((INW, D))],
        out_specs=[tile(D), pl.BlockSpec((1, D), lambda i: (0, 0))],
        out_shape=[SDS((T, D), F32), SDS((1, D), F32)],
        compiler_params=_cp(("arbitrary",)),
    )(dq, dkv, dz, dgt, dh2, h1, g, win_t)


PAIR = 2 * HD
NPAIR = GQ // 2


def _lo_lanes():
    return lax.broadcasted_iota(jnp.int32, (BLK, PAIR), 1) < HD


def _stack_heads(ref, kvh, scale=None):
    lo = _lo_lanes()
    parts = []
    for pr in range(NPAIR):
        t = ref[:, (kvh * NPAIR + pr) * PAIR:(kvh * NPAIR + pr + 1) * PAIR]
        if scale is not None:
            t = t * scale
        zero = jnp.zeros_like(t)
        parts += [jnp.where(lo, t, zero), jnp.where(lo, zero, t)]
    return jnp.concatenate(parts, axis=0)


def _kv_tiles(kvc_ref, kvp_ref, tile, kvh):
    lo = _lo_lanes()
    dup, left, right = [], [], []
    for ref in (kvp_ref, kvc_ref):
        t = ref[:, tile * PAIR:(tile + 1) * PAIR]
        r = pltpu.roll(t.astype(F32), HD, 1).astype(BF16)
        zero = jnp.zeros_like(t)
        a, b = (t, r) if kvh == 0 else (r, t)
        dup.append(jnp.where(lo, a, b))
        left.append(jnp.where(lo, a, zero))
        right.append(jnp.where(lo, zero, b))
    cat = lambda xs: jnp.concatenate(xs, axis=0)
    return cat(dup), cat(left), cat(right)


def _band_consts(first):
    row = lax.broadcasted_iota(jnp.int32, (BLK, BLK), 0)
    col = lax.broadcasted_iota(jnp.int32, (BLK, BLK), 1)
    upper = col > row
    dist = jnp.where(upper, row - col + BLK, row - col).astype(F32)
    pen = jnp.where(jnp.logical_and(upper, first), -jnp.inf, 0.0)
    return upper, dist, pen


def _split_band(upper, t):
    zero = jnp.zeros_like(t)
    return jnp.concatenate([jnp.where(upper, t, zero), jnp.where(upper, zero, t)], axis=1)


def attn_fwd(q, kv, sinks):
    T = q.shape[0]
    nb = T // BLK

    def body(sink_ref, q_ref, kvc_ref, kvp_ref, att_ref, lse_ref):
        upper, dist, pen = _band_consts(pl.program_id(0) == 0)
        scores, values = [], []
        for kvh in range(NKV):
            kdup, _, _ = _kv_tiles(kvc_ref, kvp_ref, 0, kvh)
            values.append(_kv_tiles(kvc_ref, kvp_ref, 1, kvh)[1:])
            scores.append(_dot_nt(_stack_heads(q_ref, kvh, SCALE), kdup))
        for kvh in range(NKV):
            s_all = scores[kvh]
            vleft, vright = values[kvh]
            for pr in range(NPAIR):
                outs, inv = [], []
                for side, vpad in ((0, vleft), (1, vright)):
                    g = 2 * pr + side
                    hq = kvh * GQ + g
                    sink = sink_ref[0, hq]
                    rows = slice(g * BLK, (g + 1) * BLK)
                    s = jnp.where(upper, s_all[rows, 0:BLK], s_all[rows, BLK:2 * BLK]) - SLOPES[hq] * dist + pen
                    m = jnp.maximum(jnp.max(s, axis=-1, keepdims=True), sink)
                    p = jnp.exp(s - m)
                    l = jnp.sum(p, axis=-1, keepdims=True) + jnp.exp(sink - m)
                    lse_ref[:, hq:hq + 1] = m + jnp.log(l)
                    outs.append(_dot(_split_band(upper, p.astype(BF16)), vpad))
                    inv.append(1.0 / l)
                col0 = (kvh * NPAIR + pr) * PAIR
                att_ref[:, col0:col0 + PAIR] = ((outs[0] + outs[1]) * jnp.where(_lo_lanes(), inv[0], inv[1])).astype(BF16)

    return pl.pallas_call(
        body, name="attn_fwd", grid=(nb,),
        in_specs=[pl.BlockSpec(memory_space=pltpu.SMEM),
                  pl.BlockSpec((BLK, AW), lambda i: (i, 0)),
                  pl.BlockSpec((BLK, 2 * KVW), lambda i: (i, 0)),
                  pl.BlockSpec((BLK, 2 * KVW), lambda i: (jnp.maximum(i - 1, 0), 0))],
        out_specs=[pl.BlockSpec((BLK, AW), lambda i: (i, 0)), pl.BlockSpec((BLK, NQ), lambda i: (i, 0))],
        out_shape=[SDS((T, AW), BF16), SDS((T, NQ), F32)],
        compiler_params=_cp(("arbitrary",)),
    )(sinks, q, kv, kv)


def attn_bwd(q, kv, datt, att, lse, sinks):
    T = q.shape[0]
    nb = T // BLK

    def body(sink_ref, q_ref, kvc_ref, kvp_ref, do_ref, out_ref, lse_ref, dq_ref, dkv_ref, dsink_ref, carry_ref):
        i = pl.program_id(0)

        @pl.when(i == 0)
        def _():
            dsink_ref[...] = jnp.zeros_like(dsink_ref)
            carry_ref[...] = jnp.zeros_like(carry_ref)

        @pl.when(i < nb)
        def _():
            upper, dist, pen = _band_consts(i == 0)
            lo = _lo_lanes()
            dk_dup, dv_dup = [], []
            staged = []
            for kvh in range(NKV):
                kdup, kleft, kright = _kv_tiles(kvc_ref, kvp_ref, 0, kvh)
                vdup, _, _ = _kv_tiles(kvc_ref, kvp_ref, 1, kvh)
                qs = _stack_heads(q_ref, kvh, SCALE)
                dos = _stack_heads(do_ref, kvh)
                staged.append((kleft, kright, qs, dos, _dot_nt(qs, kdup), _dot_nt(dos, vdup)))
            deltas = []
            for pair in range(NQ // 2):
                cols = slice(pair * PAIR, (pair + 1) * PAIR)
                t = do_ref[:, cols].astype(F32) * out_ref[:, cols].astype(F32)
                deltas += [jnp.sum(jnp.where(lo, t, 0.0), axis=-1, keepdims=True),
                           jnp.sum(jnp.where(lo, 0.0, t), axis=-1, keepdims=True)]
            for kvh in range(NKV):
                kleft, kright, qs, dos, s_all, dp_all = staged[kvh]
                ds_parts, p_parts = [], []
                for pr in range(NPAIR):
                    dq = None
                    for side, kpad in ((0, kleft), (1, kright)):
                        g = 2 * pr + side
                        hq = kvh * GQ + g
                        lse_h = lse_ref[:, hq:hq + 1]
                        rows = slice(g * BLK, (g + 1) * BLK)
                        s = jnp.where(upper, s_all[rows, 0:BLK], s_all[rows, BLK:2 * BLK]) - SLOPES[hq] * dist + pen
                        p = jnp.exp(s - lse_h)
                        dp = jnp.where(upper, dp_all[rows, 0:BLK], dp_all[rows, BLK:2 * BLK])
                        delta = deltas[hq]
                        dsink_ref[:, hq:hq + 1] += -jnp.sum(jnp.exp(sink_ref[0, hq] - lse_h) * delta, axis=0,
                                                            keepdims=True)
                        ds = _split_band(upper, (p * (dp - delta)).astype(BF16))
                        ds_parts.append(ds)
                        p_parts.append(_split_band(upper, p.astype(BF16)))
                        d = _dot(ds, kpad)
                        dq = d if dq is None else dq + d
                    col0 = (kvh * NPAIR + pr) * PAIR
                    dq_ref[:, col0:col0 + PAIR] = (dq * SCALE).astype(BF16)
                dkw = _dot_tn(qs, jnp.concatenate(ds_parts, axis=0)).T
                dvw = _dot_tn(dos, jnp.concatenate(p_parts, axis=0)).T
                dk_dup.append(dkw + pltpu.roll(dkw, HD, 1))
                dv_dup.append(dvw + pltpu.roll(dvw, HD, 1))
            dk = jnp.where(jnp.concatenate([lo, lo], axis=0), dk_dup[0], dk_dup[1])
            dv = jnp.where(jnp.concatenate([lo, lo], axis=0), dv_dup[0], dv_dup[1])
            dkv_ref[:, 0:PAIR] = (carry_ref[:, 0:PAIR] + dk[0:BLK]).astype(BF16)
            dkv_ref[:, PAIR:2 * PAIR] = (carry_ref[:, PAIR:2 * PAIR] + dv[0:BLK]).astype(BF16)
            carry_ref[:, 0:PAIR] = dk[BLK:2 * BLK]
            carry_ref[:, PAIR:2 * PAIR] = dv[BLK:2 * BLK]

        @pl.when(i == nb)
        def _():
            dkv_ref[...] = carry_ref[...].astype(BF16)

    cur = lambda i: (jnp.minimum(i, nb - 1), 0)
    prev = lambda i: (jnp.maximum(jnp.minimum(i, nb - 1) - 1, 0), 0)
    return pl.pallas_call(
        body, name="attn_bwd", grid=(nb + 1,),
        in_specs=[pl.BlockSpec(memory_space=pltpu.SMEM),
                  pl.BlockSpec((BLK, AW), cur), pl.BlockSpec((BLK, 2 * KVW), cur), pl.BlockSpec((BLK, 2 * KVW), prev),
                  pl.BlockSpec((BLK, AW), cur), pl.BlockSpec((BLK, AW), cur), pl.BlockSpec((BLK, NQ), cur)],
        out_specs=[pl.BlockSpec((BLK, AW), cur),
                   pl.BlockSpec((BLK, 2 * KVW), lambda i: (jnp.maximum(i - 1, 0), 0)),
                   pl.BlockSpec((1, NQ), lambda i: (0, 0))],
        out_shape=[SDS((T, AW), BF16), SDS((T, 2 * KVW), BF16), SDS((1, NQ), F32)],
        scratch_shapes=[pltpu.VMEM((BLK, 2 * KVW), F32)],
        compiler_params=_cp(("arbitrary",)),
    )(sinks, q, kv, kv, datt, att, lse)


def _inv_counts(t0, rows):
    t = (t0 + lax.broadcasted_iota(jnp.int32, (rows, 1), 0) + 1).astype(F32)
    return [1.0 / jnp.minimum(t, float(w)) for w in POOL_WINDOWS]


def pool_fwd(z, wmix, scale):
    T = z.shape[0]
    TM = min(1024, T)
    L = TM + HALO

    def body(z_ref, halo_ref, wmix_ref, scale_ref, pooled_ref, mixs_ref):
        i = pl.program_id(0)
        halo = jnp.where(i > 0, halo_ref[...], 0.0)
        zt = z_ref[...]
        e = jnp.concatenate([halo, zt], axis=0)
        sums = []
        s = e
        for k in (1, 2, 4, 8):
            s = s + pltpu.roll(s, k, 0)
            sums.append(s)
        inv = _inv_counts(i * TM, TM)
        for gi in range(len(POOL_WINDOWS)):
            cols = slice(gi * PG, (gi + 1) * PG)
            pooled = (sums[gi][HALO:, cols] * inv[gi] - zt[:, cols]).astype(BF16)
            pooled_ref[:, cols] = pooled
            mixs_ref[:, cols] = (_dot(pooled, wmix_ref[gi]) * scale_ref[:, cols]).astype(BF16)

    return pl.pallas_call(
        body, name="pool_fwd", grid=(T // TM,),
        in_specs=[pl.BlockSpec((TM, PW), lambda i: (i, 0)),
                  pl.BlockSpec((HALO, PW), lambda i: (jnp.maximum(i * (TM // HALO) - 1, 0), 0)),
                  _const_spec((len(POOL_WINDOWS), PG, PG)), _const_spec((1, PW))],
        out_specs=[pl.BlockSpec((TM, PW), lambda i: (i, 0)), pl.BlockSpec((TM, PW), lambda i: (i, 0))],
        out_shape=[SDS((T, PW), BF16), SDS((T, PW), BF16)],
        compiler_params=_cp(("arbitrary",)),
    )(z, z, wmix, scale)


def pool_bwd(dmixs, pooled, wmix, scale):
    T = dmixs.shape[0]
    TM = min(1024, T)
    L = TM + HALO
    nt = T // TM

    def body(dm_ref, halo_ref, pooled_ref, wmix_ref, scale_ref, dz_ref, dwmix_ref, dscale_ref):
        i = pl.program_id(0)

        @pl.when(i == 0)
        def _():
            dwmix_ref[...] = jnp.zeros_like(dwmix_ref)
            dscale_ref[...] = jnp.zeros_like(dscale_ref)

        halo = jnp.where(i < nt - 1, halo_ref[...], 0.0)
        dm = dm_ref[...]
        e = jnp.concatenate([dm, halo], axis=0)
        inv = _inv_counts(i * TM, L)
        for gi in range(len(POOL_WINDOWS)):
            cols = slice(gi * PG, (gi + 1) * PG)
            w = wmix_ref[gi]
            dmixed = (e[:, cols] * scale_ref[:, cols]).astype(BF16)
            dpooled = _dot_nt(dmixed, w)
            pooled = pooled_ref[:, cols]
            mixed = _dot(pooled, w)
            dscale_ref[:, cols] += jnp.sum(dm[:, cols] * mixed, axis=0, keepdims=True)
            dwmix_ref[gi] += _dot_tn(pooled, dmixed[:TM])
            s = dpooled * inv[gi]
            k = 1
            while k < POOL_WINDOWS[gi]:
                s = s + pltpu.roll(s, L - k, 0)
                k *= 2
            dz_ref[:, cols] = (s[:TM] - dpooled[:TM]).astype(BF16)

    return pl.pallas_call(
        body, name="pool_bwd", grid=(nt,),
        in_specs=[pl.BlockSpec((TM, PW), lambda i: (i, 0)),
                  pl.BlockSpec((HALO, PW), lambda i: (jnp.minimum((i + 1) * (TM // HALO), T // HALO - 1), 0)),
                  pl.BlockSpec((TM, PW), lambda i: (i, 0)),
                  _const_spec((len(POOL_WINDOWS), PG, PG)), _const_spec((1, PW))],
        out_specs=[pl.BlockSpec((TM, PW), lambda i: (i, 0)),
                   pl.BlockSpec((len(POOL_WINDOWS), PG, PG), lambda i: (0, 0, 0)),
                   pl.BlockSpec((1, PW), lambda i: (0, 0))],
        out_shape=[SDS((T, PW), BF16), SDS((len(POOL_WINDOWS), PG, PG), F32), SDS((1, PW), F32)],
        compiler_params=_cp(("arbitrary",)),
    )(dmixs, dmixs, pooled, wmix, scale)


def merge_fwd(att, mixs, gt, h1, wattn, wpool, wout):
    T = h1.shape[0]
    TM = 512

    def body(att_ref, mixs_ref, gt_ref, h_ref, wa_ref, wp_ref, wo_ref, h2_ref, mg_ref, gf_ref):
        a = _dot(att_ref[...], wa_ref[...])
        p = _dot(mixs_ref[...], wp_ref[...])
        sa = jax.nn.sigmoid(gt_ref[:, 0:D].astype(F32))
        sp = jax.nn.sigmoid(gt_ref[:, D:2 * D].astype(F32))
        gf_ref[:, 0:D] = (a * sa * (1.0 - sa)).astype(BF16)
        gf_ref[:, D:2 * D] = (p * sp * (1.0 - sp)).astype(BF16)
        mg = (sa * a + sp * p).astype(BF16)
        mg_ref[...] = mg
        h2_ref[...] = h_ref[...] + _dot(mg, wo_ref[...])

    tile = lambda w: pl.BlockSpec((TM, w), lambda i: (i, 0))
    return pl.pallas_call(
        body, name="merge_fwd", grid=(T // TM,),
        in_specs=[tile(AW), tile(PW), tile(2 * D), tile(D), _const_spec((AW, D)), _const_spec((PW, D)),
                  _const_spec((D, D))],
        out_specs=[tile(D), tile(D), tile(2 * D)],
        out_shape=[SDS((T, D), F32), SDS((T, D), BF16), SDS((T, 2 * D), BF16)],
        compiler_params=_cp(("arbitrary",)),
    )(att, mixs, gt, h1, wattn, wpool, wout)


def merge_bwd(dh2, gt, gf, wattn, wpool, wout, after=None):
    T = dh2.shape[0]
    TM = 512
    token_spec, token_arg = _token_operand(after)

    def body(dh2_ref, gt_ref, gf_ref, wa_ref, wp_ref, wo_ref, *rest):
        datt_ref, dmixs_ref, dgt_ref, da_ref, dp_ref = rest[-5:]
        dm = _dot_nt(dh2_ref[...].astype(BF16), wo_ref[...])
        da = (dm * jax.nn.sigmoid(gt_ref[:, 0:D].astype(F32))).astype(BF16)
        dp = (dm * jax.nn.sigmoid(gt_ref[:, D:2 * D].astype(F32))).astype(BF16)
        da_ref[...] = da
        dp_ref[...] = dp
        dgt_ref[:, 0:D] = (dm * gf_ref[:, 0:D].astype(F32)).astype(BF16)
        dgt_ref[:, D:2 * D] = (dm * gf_ref[:, D:2 * D].astype(F32)).astype(BF16)
        datt_ref[...] = _dot_nt(da, wa_ref[...]).astype(BF16)
        dmixs_ref[...] = _dot_nt(dp, wp_ref[...])

    tile = lambda w: pl.BlockSpec((TM, w), lambda i: (i, 0))
    return pl.pallas_call(
        body, name="merge_bwd", grid=(T // TM,),
        in_specs=[tile(D), tile(2 * D), tile(2 * D), _const_spec((AW, D)), _const_spec((PW, D)),
                  _const_spec((D, D))] + token_spec,
        out_specs=[tile(AW), tile(PW), tile(2 * D), tile(D), tile(D)],
        out_shape=[SDS((T, AW), BF16), SDS((T, PW), F32), SDS((T, 2 * D), BF16), SDS((T, D), BF16),
                   SDS((T, D), BF16)],
        compiler_params=_cp(("arbitrary",)),
    )(dh2, gt, gf, wattn, wpool, wout, *token_arg)


def adamw(updates, place, name, after=None):
    tile_bytes = 2 * 1024 * 1024 // len(updates)
    jobs = []
    for w, g, m, v in updates:
        R, C = w.shape
        tr = R
        if R * C * 4 > tile_bytes:
            tr = next(cand for cand in (512, 256, 128, 64, 32, 16, 8) if R % cand == 0 and cand * C * 4 <= tile_bytes)
        blk = ((tr, C), lambda ls, p: (ls, 0))
        jobs.append((R // tr, [(a, *blk) for a in (w, g, m, v)], [((R, C), F32, *blk)] * 4,
                     lambda ins, outs: _adamw_tile(*ins, *outs)))
    return _run_jobs(jobs, place, name, after)


GROUP_FFN1, GROUP_MIX, GROUP_FFN2 = 0, 1, 2


def _behind(small, token):
    return small if token is None else small + token[0:1, 0:1]


def local_fwd_bwd(x, target, S, comm):
    W = comm.weights(GROUP_FFN1, None)
    h1, ab1, hid1 = ffn_fwd(x, _behind(S["ffn1_norm"], comm.started()), W["ffn1_w_up"], W["ffn1_w_down"], "ffn1_fwd")
    W = comm.weights(GROUP_MIX, h1)
    u, q, kv, z, gt = mix_in_fwd(h1, _behind(S["mix_norm"], comm.started()), W["w_in"])
    att, lse = attn_fwd(q, kv, S["sinks"])
    pooled, mixs = pool_fwd(z, S["pool_w_mix"], _behind(S["pool_scale"], comm.prefetch(GROUP_FFN2, att)))
    h2, merged, gf = merge_fwd(att, mixs, gt, h1, W["w_attn_up"], W["w_pool_up"], W["w_out"])
    W = comm.weights(GROUP_FFN2, h2)
    dh3, ab2, hid2, loss, g_final = ffn_fwd(h2, S["ffn2_norm"], W["ffn2_w_up"], W["ffn2_w_down"], "ffn2_fwd",
                                            head=(target, S["final_norm"]))

    G = {"final_norm": g_final}
    dh2, dup2, n2, G["ffn2_norm"] = ffn_bwd_x(dh3, h2, ab2, S["ffn2_norm"], W["ffn2_w_up"], W["ffn2_w_down"],
                                              "ffn2_bwd_x")
    ffn2 = ("ffn2_w_down", "ffn2_w_up")
    token = comm.grads({"ffn2_w_down": matmul_tn(hid2, dh3, "ffn2_dw_down", tm=1408, tn=512, tt=4096, b_scale=0.5),
                        "ffn2_w_up": matmul_tn(n2, dup2, "ffn2_dw_up", tm=D, tn=512, tt=4096)})

    W = comm.weights(GROUP_MIX, None)
    datt, dmixs, dgt, da, dp = merge_bwd(dh2, gt, gf, W["w_attn_up"], W["w_pool_up"], W["w_out"], after=token)
    token = comm.advance(ffn2, datt)
    g_mix = {"w_out": matmul_tn(merged, dh2, "dw_out", tm=D, tn=D, after=token),
             "w_attn_up": matmul_tn(att, da, "dw_attn_up", tm=AW, tn=D),
             "w_pool_up": matmul_tn(mixs, dp, "dw_pool_up", tm=PW, tn=D)}
    dz, G["pool_w_mix"], G["pool_scale"] = pool_bwd(dmixs, pooled, S["pool_w_mix"], _behind(S["pool_scale"], token))
    dq, dkv, G["sinks"] = attn_bwd(q, kv, datt, att, lse, _behind(S["sinks"], token))
    dh1, G["mix_norm"] = mix_in_bwd(dq, dkv, dz, dgt, dh2, h1, S["mix_norm"], W["w_in"])
    g_mix["w_in"] = jnp.concatenate([
        matmul_tn(dq, u, "dw_in_q", tm=AW, tn=D),
        matmul_tn(dkv, u, "dw_in_kv", tm=2 * KVW, tn=D),
        matmul_tn(dz, u, "dw_in_z", tm=PW, tn=D),
        matmul_tn(dgt, u, "dw_in_g", tm=D, tn=512, tt=4096),
    ], axis=0)
    mix = tuple(g_mix)
    token = comm.grads(g_mix)

    W = comm.weights(GROUP_FFN1, None)
    g_dn1 = matmul_tn(hid1, dh1, "ffn1_dw_down", tm=1408, tn=512, tt=4096, b_scale=0.5, after=token)
    token = comm.advance(mix, g_dn1)
    token2 = comm.grads({"ffn1_w_down": g_dn1})
    g1 = _behind(_behind(S["ffn1_norm"], token), token2)
    dx, dup1, n1, G["ffn1_norm"], *outs = ffn_bwd_x(dh1, x, ab1, g1, W["ffn1_w_up"], W["ffn1_w_down"], "ffn1_bwd_x",
                                                   updates=comm.reduce(ffn2, token2))
    comm.updated(ffn2, outs)
    token = comm.advance(("ffn1_w_down",), dx)
    token2 = comm.small_start(G, loss, dx)
    g_up1 = matmul_tn(n1, dup1, "ffn1_dw_up", tm=D, tn=512, tt=4096,
                      after=None if token is None else token + token2)
    token = comm.grads({"ffn1_w_up": g_up1})
    updates = comm.reduce(("ffn1_w_down",), token)
    token = comm.advance(("ffn1_w_up",), [u[1] for u in updates])
    done = comm.update(("ffn1_w_down",), updates, token)
    done = comm.finish(mix, done)
    done = comm.small_finish(done)
    comm.finish(("ffn1_w_up",), done)
    return loss, dx


HBM_SPEC = pl.BlockSpec(memory_space=pltpu.HBM)


def _me():
    return lax.axis_index("x"), lax.axis_index("y"), lax.axis_index("c")


def _peer_chip(x, y, k):
    return x ^ (k >> 1), y ^ (k & 1)


def _piece_half(ref, rowlike, j, h):
    if rowlike:
        return ref.at[j, h]
    ns = ref.shape[-1] // N_CHIPS
    return ref.at[h, :, pl.ds(pl.multiple_of(j * ns, 128), ns)]


def _piece(ref, rowlike, j):
    if rowlike:
        return ref.at[j]
    ns = ref.shape[-1] // N_CHIPS
    return ref.at[:, :, pl.ds(pl.multiple_of(j * ns, 128), ns)]


def _full_shape(shard_view, rowlike):
    _, kh, ns = shard_view.shape
    return (N_CHIPS, 2, kh, ns) if rowlike else (2, kh, N_CHIPS * ns)


def _remote(src, dst, send_sem, recv_sem, dev):
    return pltpu.make_async_remote_copy(src, dst, send_sem, recv_sem, device_id=dev, device_id_type=MESH)


SEM_SPEC = pl.BlockSpec(memory_space=pltpu.SEMAPHORE)
SPLIT_PARAMS = pltpu.CompilerParams(has_side_effects=pltpu.SideEffectType.DATAFLOW_SIDE_EFFECTING)


def _gather_plan(rowlikes):
    def plan(s_refs, f_refs, a, k, x, y, c):
        px, py = _peer_chip(x, y, k)
        return (s_refs[a].at[c], _piece_half(f_refs[a], rowlikes[a], 2 * x + y, c),
                _piece_half(f_refs[a], rowlikes[a], 2 * px + py, c), (px, py, c))
    return plan


def _all_to_all_plan(rowlikes):
    def plan(q_refs, r_refs, a, k, x, y, c):
        px, py = _peer_chip(x, y, k)
        if rowlikes[a]:
            src = q_refs[a].at[2 * px + py]
        else:
            ns = q_refs[a].shape[-1] // N_CHIPS
            src = q_refs[a].at[:, pl.ds(pl.multiple_of((2 * px + py) * ns, 128), ns)]
        return src, r_refs[a].at[k - 1], r_refs[a].at[k - 1], (px, py, c)
    return plan


def _swap_plan(rowlikes):
    def plan(g_refs, got_refs, a, k, x, y, c):
        src = g_refs[a].at[:, 1 - c] if rowlikes[a] else g_refs[a].at[1 - c]
        return src, got_refs[a], got_refs[a], (x, y, 1 - c)
    return plan


def _everyone_plan(s_refs, slot_refs, a, k, x, y, c):
    px, py, pc = x ^ (k >> 2), y ^ ((k >> 1) & 1), c ^ (k & 1)
    return s_refs[a], slot_refs[a].at[4 * x + 2 * y + c], slot_refs[a].at[4 * px + 2 * py + pc], (px, py, pc)


def _forward_plan(rowlikes):
    def plan(s_refs, f_refs, a, k, x, y, c):
        sib = (x, y, 1 - c)
        if k == 0:
            own = _piece(f_refs[a], rowlikes[a], 2 * x + y)
            return s_refs[a], own, own, sib
        px, py = _peer_chip(x, y, k)
        mine = _piece_half(f_refs[a], rowlikes[a], 2 * px + py, c)
        return mine, mine, _piece_half(f_refs[a], rowlikes[a], 2 * px + py, 1 - c), sib
    return plan


CHIPS, SIBLING, EVERYONE, FORWARDS = (1, 2, 3), (1,), tuple(range(1, 8)), (0, 1, 2, 3)


def _as_list(after):
    return [] if after is None else list(after) if isinstance(after, (list, tuple)) else [after]


def exchange_start(srcs, land_shapes, plan, after, name, peers=CHIPS):
    n = len(srcs)
    lands = [l if hasattr(l, "dtype") else lax.empty(l, s.dtype) for l, s in zip(land_shapes, srcs)]

    behind = _as_list(after)

    def body(*refs):
        s_refs, l_refs = refs[:n], refs[n:2 * n]
        send_sems, recv_sems = refs[2 * n + len(behind)], refs[2 * n + len(behind) + 1]
        token = refs[-1]
        x, y, c = _me()
        for a in range(n):
            for i, k in enumerate(peers):
                src, dst, _, peer = plan(s_refs, l_refs, a, k, x, y, c)
                sem = len(peers) * a + i
                _remote(src, dst, send_sems.at[sem], recv_sems.at[sem], peer).start()
        token[...] = jnp.zeros_like(token)

    n_sems = len(peers) * n
    outs = pl.pallas_call(
        body, name=name, in_specs=[HBM_SPEC] * (2 * n) + [pl.BlockSpec(memory_space=pl.ANY)] * len(behind),
        out_specs=[SEM_SPEC, SEM_SPEC] + [HBM_SPEC] * (2 * n) + [pl.BlockSpec(memory_space=pltpu.VMEM)],
        out_shape=[pltpu.SemaphoreType.DMA((n_sems,)), pltpu.SemaphoreType.DMA((n_sems,))]
        + [pltpu.HBM(a.shape, a.dtype) for a in (*srcs, *lands)] + [SDS((8, 128), F32)],
        input_output_aliases={i: 2 + i for i in range(2 * n)},
        compiler_params=SPLIT_PARAMS,
    )(*[pltpu.with_memory_space_constraint(a, pltpu.HBM) for a in (*srcs, *lands)], *behind)
    return {"sems": outs[:2], "srcs": outs[2:2 + n], "lands": outs[2 + n:2 + 2 * n], "token": outs[-1]}


def exchange_wait(state, plan, after, name, peers=CHIPS):
    n = len(state["srcs"])
    behind = _as_list(after)

    def body(*refs):
        s_refs, l_refs = refs[:n], refs[n:2 * n]
        send_sems, recv_sems = refs[2 * n], refs[2 * n + 1]
        x, y, c = _me()
        for a in range(n):
            for i, k in enumerate(peers):
                src, _, landing, peer = plan(s_refs, l_refs, a, k, x, y, c)
                sem = len(peers) * a + i
                cp = _remote(src, landing, send_sems.at[sem], recv_sems.at[sem], peer)
                cp.wait_send()
                cp.wait_recv()

    bufs = (*state["srcs"], *state["lands"])
    outs = pl.pallas_call(
        body, name=name,
        in_specs=[HBM_SPEC] * (2 * n) + [SEM_SPEC, SEM_SPEC] + [pl.BlockSpec(memory_space=pl.ANY)] * len(behind),
        out_specs=[HBM_SPEC] * (2 * n),
        out_shape=[pltpu.HBM(a.shape, a.dtype) for a in bufs],
        input_output_aliases={i: i for i in range(2 * n)},
        compiler_params=SPLIT_PARAMS,
    )(*bufs, *state["sems"], *behind)
    return outs[:n], outs[n:]


def gather_finish(shards, fulls, rowlikes, name):
    n = len(shards)

    def body(*refs):
        s_refs, f_refs = refs[:n], refs[2 * n:3 * n]
        send_sems, recv_sems = refs[3 * n:]
        x, y, c = _me()
        chip = 2 * x + y
        sib = (x, y, 1 - c)
        sends = []
        for a in range(n):
            own = _piece(f_refs[a], rowlikes[a], chip)
            cp = _remote(s_refs[a], own, send_sems.at[a, 0], recv_sems.at[a, 0], sib)
            cp.start()
            sends.append(cp)
            for k in (1, 2, 3):
                px, py = _peer_chip(x, y, k)
                slot = _piece_half(f_refs[a], rowlikes[a], 2 * px + py, c)
                cp = _remote(slot, slot, send_sems.at[a, k], recv_sems.at[a, k], sib)
                cp.start()
                sends.append(cp)
        for a in range(n):
            own = _piece(f_refs[a], rowlikes[a], chip)
            _remote(own, own, send_sems.at[a, 0], recv_sems.at[a, 0], sib).wait_recv()
            for k in (1, 2, 3):
                px, py = _peer_chip(x, y, k)
                slot = _piece_half(f_refs[a], rowlikes[a], 2 * px + py, 1 - c)
                _remote(slot, slot, send_sems.at[a, k], recv_sems.at[a, k], sib).wait_recv()
        for cp in sends:
            cp.wait_send()

    return pl.pallas_call(
        body, name=name, in_specs=[HBM_SPEC] * (2 * n), out_specs=[HBM_SPEC] * n,
        out_shape=[SDS(f.shape, f.dtype) for f in fulls],
        input_output_aliases={n + a: a for a in range(n)},
        scratch_shapes=[pltpu.SemaphoreType.DMA((n, 4)), pltpu.SemaphoreType.DMA((n, 4))],
    )(*shards, *fulls)


def _half_buffer_shape(gview, rowlike):
    return (N_CHIPS,) + gview.shape[2:] if rowlike else gview.shape[1:]


def _row_tiles(kh, ns):
    return 1 if kh * ns <= 256 * 1024 else 2


def _run_jobs(jobs, place, name, after=None):
    n_steps = max(job[0] for job in jobs)

    def spec(block, index, steps):
        return pl.BlockSpec(block, lambda s, p: index(jnp.minimum(s, steps - 1), p))

    in_specs = [spec(b, ix, steps) for steps, ins, _, _ in jobs for _, b, ix in ins]
    out_specs = [spec(b, ix, steps) for steps, _, outs, _ in jobs for _, _, b, ix in outs]
    token_spec, token_arg = _token_operand(after)
    n_in = len(in_specs) + len(token_spec)

    def body(place_ref, *refs):
        s = pl.program_id(0)
        i, o = 0, n_in
        for steps, ins, outs, fn in jobs:
            in_refs, out_refs = refs[i:i + len(ins)], refs[o:o + len(outs)]
            i, o = i + len(ins), o + len(outs)

            @pl.when(s < steps)
            def _(fn=fn, in_refs=in_refs, out_refs=out_refs):
                fn(in_refs, out_refs)

    return pl.pallas_call(
        body, name=name,
        grid_spec=pltpu.PrefetchScalarGridSpec(num_scalar_prefetch=1, grid=(n_steps,),
                                               in_specs=in_specs + token_spec, out_specs=out_specs),
        out_shape=[SDS(shape, dtype) for _, _, outs, _ in jobs for shape, dtype, _, _ in outs],
        compiler_params=_cp(("arbitrary",)),
    )(place, *[a for _, ins, _, _ in jobs for a, _, _ in ins], *token_arg)


def add_halves(gviews, gots, rowlikes, place, name):
    def add(ins, outs):
        outs[0][...] = (ins[0][...].astype(F32) + ins[1][...].astype(F32)).astype(BF16)

    jobs = []
    for g, got, rowlike in zip(gviews, gots, rowlikes):
        kh, ns = (g.shape[2], g.shape[3]) if rowlike else (g.shape[1], g.shape[2] // N_CHIPS)
        rt = _row_tiles(kh, ns)
        tr = kh // rt
        if rowlike:
            g_in = (g, (None, None, tr, ns), lambda ls, p, rt=rt: (ls // rt, p[0], ls % rt, 0))
            half = ((None, tr, ns), lambda ls, p, rt=rt: (ls // rt, ls % rt, 0))
        else:
            g_in = (g, (None, tr, ns), lambda ls, p, rt=rt: (p[0], ls % rt, ls // rt))
            half = ((tr, ns), lambda ls, p, rt=rt: (ls % rt, ls // rt))
        jobs.append((N_CHIPS * rt, [g_in, (got, *half)], [(got.shape, BF16, *half)], add))
    return _run_jobs(jobs, place, name)


def _slot_shape(q, rowlike):
    return (3,) + (q.shape[1:] if rowlike else (q.shape[0], q.shape[1] // N_CHIPS))


def sum_pieces(qs, recvs, rowlikes, place, name):
    def total(ins, outs):
        acc = ins[0][...].astype(F32)
        for k in range(3):
            acc = acc + ins[1][k].astype(F32)
        outs[0][...] = acc

    jobs = []
    for q, recv, rowlike in zip(qs, recvs, rowlikes):
        kh, ns = recv.shape[1], recv.shape[2]
        rt = _row_tiles(kh, ns)
        tr = kh // rt
        mine = ((None, tr, ns), lambda ls, p: (p[1], ls, 0)) if rowlike else ((tr, ns), lambda ls, p: (ls, p[1]))
        jobs.append((rt, [(q, *mine), (recv, (3, tr, ns), lambda ls, p: (0, ls, 0))],
                     [((2, kh, ns), F32, (None, tr, ns), lambda ls, p: (p[0], ls, 0))], total))
    return _run_jobs(jobs, place, name)


def join_halves(halves, name):
    n = len(halves)

    def body(*refs):
        o_refs = refs[n:2 * n]
        send_sems, recv_sems = refs[2 * n:]
        x, y, c = _me()
        sib = (x, y, 1 - c)
        sends = []
        for a in range(n):
            cp = _remote(o_refs[a].at[c], o_refs[a].at[c], send_sems.at[a], recv_sems.at[a], sib)
            cp.start()
            sends.append(cp)
        for a in range(n):
            got = o_refs[a].at[1 - c]
            _remote(got, got, send_sems.at[a], recv_sems.at[a], sib).wait_recv()
        for cp in sends:
            cp.wait_send()

    return pl.pallas_call(
        body, name=name, in_specs=[HBM_SPEC] * n, out_specs=[HBM_SPEC] * n,
        out_shape=[SDS(h.shape, h.dtype) for h in halves],
        input_output_aliases={a: a for a in range(n)},
        scratch_shapes=[pltpu.SemaphoreType.DMA((n,)), pltpu.SemaphoreType.DMA((n,))],
    )(*halves)


N_DEV = 8


def sum_devices(s, slots, me):
    R, Wd = s.shape

    def body(me_ref, s_ref, slots_ref, out_ref):
        acc = None
        for d in range(N_DEV):
            mine = me_ref[0] == d
            term = jnp.where(mine, s_ref[...], slots_ref[jnp.where(mine, d ^ 1, d)])
            acc = term if acc is None else acc + term
        out_ref[...] = acc

    vmem = pl.BlockSpec(memory_space=pltpu.VMEM)
    return pl.pallas_call(
        body, name="sum_devices", in_specs=[pl.BlockSpec(memory_space=pltpu.SMEM), vmem, vmem], out_specs=vmem,
        out_shape=SDS((R, Wd), F32),
    )(me, s, slots)


TRANSPOSED = ("w_in",)
BIG = {"ffn1_w_up": (D, 2 * FF, "col"), "ffn1_w_down": (FF, D, "row"), "w_in": (INW, D, "row"),
       "w_attn_up": (AW, D, "row"), "w_pool_up": (PW, D, "col"), "w_out": (D, D, "row"),
       "ffn2_w_up": (D, 2 * FF, "col"), "ffn2_w_down": (FF, D, "row")}
GROUPS = (("ffn1_w_up", "ffn1_w_down"), ("w_in", "w_attn_up", "w_pool_up", "w_out"), ("ffn2_w_up", "ffn2_w_down"))
SMALL = ("ffn1_norm", "mix_norm", "ffn2_norm", "final_norm", "pool_scale", "sinks", "pool_w_mix")
SMALL_W = 128


def _rowlike(name):
    return BIG[name][2] == "row"


def _half_dims(name):
    k, n, kind = BIG[name]
    return (k // N_CHIPS // 2, n) if kind == "row" else (k // 2, n // N_CHIPS)


def shard_view(name, shard):
    return shard.reshape((2,) + _half_dims(name))


def full_from_view(name, fv):
    k, n, _ = BIG[name]
    return fv.reshape(k, n)


def grad_view(name, g):
    kh, ns = _half_dims(name)
    return g.reshape(_full_shape(jax.ShapeDtypeStruct((2, kh, ns), g.dtype), _rowlike(name)))


def pack_small(d):
    parts = []
    for name in SMALL:
        a = d[name].reshape(-1)
        pad = (-a.shape[0]) % SMALL_W
        parts.append(jnp.pad(a, (0, pad)).reshape(-1, SMALL_W))
    a = jnp.concatenate(parts, axis=0)
    return jnp.pad(a, ((0, (-a.shape[0]) % 8), (0, 0)))


def unpack_small(a, like):
    out, r0 = {}, 0
    for name in SMALL:
        size = int(np.prod(like[name].shape))
        rows = -(-size // SMALL_W)
        out[name] = a[r0:r0 + rows].reshape(-1)[:size].reshape(like[name].shape)
        r0 += rows
    return out


WEIGHTS = ("ffn1_norm", "ffn1_w_up", "ffn1_w_down", "mix_norm", "w_in", "sinks", "w_attn_up", "pool_w_mix",
           "pool_scale", "w_pool_up", "w_out", "ffn2_norm", "ffn2_w_up", "ffn2_w_down", "final_norm")


def kernel(x, ffn1_norm, ffn1_w_up, ffn1_w_down, mix_norm, w_in, sinks, w_attn_up, pool_w_mix, pool_scale, w_pool_up, w_out, ffn2_norm, ffn2_w_up, ffn2_w_down, final_norm, loss_target, m_ffn1_norm, m_ffn1_w_up, m_ffn1_w_down, m_mix_norm, m_w_in, m_sinks, m_w_attn_up, m_pool_w_mix, m_pool_scale, m_w_pool_up, m_w_out, m_ffn2_norm, m_ffn2_w_up, m_ffn2_w_down, m_final_norm, v_ffn1_norm, v_ffn1_w_up, v_ffn1_w_down, v_mix_norm, v_w_in, v_sinks, v_w_attn_up, v_pool_w_mix, v_pool_scale, v_w_pool_up, v_w_out, v_ffn2_norm, v_ffn2_w_up, v_ffn2_w_down, v_final_norm):
    given = dict(locals())
    w = {n: given[n] for n in WEIGHTS}
    m = {n: given["m_" + n] for n in WEIGHTS}
    v = {n: given["v_" + n] for n in WEIGHTS}
    cx, cy, cc = _me()
    place = jnp.stack([cc, 2 * cx + cy]).astype(jnp.int32)

    def local2d(d, n):
        return d[n][0].T if n in TRANSPOSED else d[n][0]

    shards = {n: local2d(w, n) for n in BIG}
    grads, delta, new_m, new_v = {}, {}, {}, {}
    rowlikes = [[_rowlike(n) for n in names] for names in GROUPS]

    class Exchanges:
        def __init__(self):
            self.gathers, self.forwards, self.fulls, self.reductions, self.small = {}, {}, {}, {}, None

        def _behind_first(self, a):
            return a + self.gathers[0]["token"][0, 0]

        def _start_gather(self, group, after):
            prepare = self._behind_first if group else (lambda a: a)
            sv = [shard_view(n, prepare(shards[n]).astype(BF16)) for n in GROUPS[group]]
            rl = rowlikes[group]
            self.gathers[group] = exchange_start(sv, [_full_shape(s, r) for s, r in zip(sv, rl)], _gather_plan(rl),
                                                 after, f"gather_start_{group}")

        def weights(self, group, after):
            if not self.gathers:
                self._start_gather(0, None)
                self.packed = [self._behind_first(pack_small(d)) for d in (w, m, v)]
                after = [self.gathers[0]["token"], *self.packed]
            if group not in self.fulls:
                names, rl, state = GROUPS[group], rowlikes[group], self.gathers[group]
                if group in self.forwards:
                    _, fulls = exchange_wait(self.forwards.pop(group), _forward_plan(rl), after,
                                             f"forward_wait_{group}", FORWARDS)
                else:
                    sv, fulls = exchange_wait(state, _gather_plan(rl), state["token"] if after is None else after,
                                              f"gather_wait_{group}")
                    fulls = gather_finish(sv, fulls, rl, f"gather_finish_{group}")
                self.fulls[group] = {n: full_from_view(n, f) for n, f in zip(names, fulls)}
                if group + 1 < len(GROUPS):
                    self._start_gather(group + 1, fulls[0])
            return self.fulls[group]

        def started(self):
            return self.gathers[max(self.gathers)]["token"]

        def prefetch(self, group, after):
            rl, state = rowlikes[group], self.gathers[group]
            sv, fulls = exchange_wait(state, _gather_plan(rl), after, f"gather_wait_{group}")
            self.forwards[group] = exchange_start(sv, fulls, _forward_plan(rl), None, f"forward_start_{group}", FORWARDS)
            return self.forwards[group]["token"]

        def grads(self, g):
            names = tuple(g)
            rl = [_rowlike(n) for n in names]
            gv = [grad_view(n, g[n]) for n in names]
            state = exchange_start(gv, [_half_buffer_shape(a, r) for a, r in zip(gv, rl)], _swap_plan(rl), None,
                                   "swap_start_" + names[0], SIBLING)
            self.reductions[names] = state
            return state["token"]

        def advance(self, names, after):
            rl = [_rowlike(n) for n in names]
            gv, gots = exchange_wait(self.reductions[names], _swap_plan(rl), after, "swap_wait_" + names[0], SIBLING)
            qs = add_halves(gv, gots, rl, place, "add_halves_" + names[0])
            state = exchange_start(qs, [_slot_shape(q, r) for q, r in zip(qs, rl)], _all_to_all_plan(rl), None,
                                   "all_to_all_start_" + names[0])
            self.reductions[names] = state
            return state["token"]

        def reduce(self, names, after):
            rl = [_rowlike(n) for n in names]
            qs, recvs = exchange_wait(self.reductions.pop(names), _all_to_all_plan(rl), after,
                                      "all_to_all_wait_" + names[0])
            halves = sum_pieces(qs, recvs, rl, place, "sum_pieces_" + names[0])
            return [(shards[n], o.reshape(shards[n].shape), local2d(m, n), local2d(v, n))
                    for n, o in zip(names, join_halves(halves, "join_halves_" + names[0]))]

        def updated(self, names, outs):
            for i, n in enumerate(names):
                grads[n], delta[n], new_m[n], new_v[n] = outs[4 * i:4 * i + 4]
            return [new_v[n] for n in names]

        def update(self, names, updates, token=None):
            return self.updated(names, adamw(updates, place, "adamw_" + names[0], after=token))

        def finish(self, names, after):
            return self.update(names, self.reduce(names, after))

        def small_start(self, G, loss, after):
            packed = pack_small({n: G[n] for n in SMALL})
            used_rows = sum(-(-int(np.prod(small_like[n].shape)) // SMALL_W) for n in SMALL)
            assert packed.shape[0] > used_rows
            packed = packed.at[-1, 0].set(loss[0, 0])
            self.small = exchange_start([packed], [(N_DEV,) + packed.shape], _everyone_plan, after,
                                        "small_start", EVERYONE)
            return self.small["token"]

        def small_finish(self, after):
            (packed,), (slots,) = exchange_wait(self.small, _everyone_plan, after, "small_wait", EVERYONE)
            total = sum_devices(packed, slots, (4 * cx + 2 * cy + cc).astype(jnp.int32).reshape(1))
            self.loss = total[-1, 0]
            g, ds, ms, vs = adamw([(self.packed[0], total, self.packed[1], self.packed[2])], place, "adamw_small")
            for d, packed_d in ((grads, g), (delta, ds), (new_m, ms), (new_v, vs)):
                d.update(unpack_small(packed_d, small_like))
            return vs

    small_like = {n: w[n] for n in SMALL}
    S = {n: w[n].reshape(1, -1) for n in ("ffn1_norm", "mix_norm", "ffn2_norm", "final_norm", "pool_scale", "sinks")}
    S["pool_w_mix"] = w["pool_w_mix"][0].astype(BF16)
    exchanges = Exchanges()
    _, dx = local_fwd_bwd(x[0], loss_target[0], S, exchanges)
    loss = exchanges.loss

    def shaped(d, n):
        return (d[n].T if n in TRANSPOSED else d[n]).reshape(w[n].shape)

    return (loss, dx[None], *[shaped(grads, n) for n in WEIGHTS], *[shaped(delta, n) for n in WEIGHTS],
            *[shaped(new_m, n) for n in WEIGHTS], *[shaped(new_v, n) for n in WEIGHTS])
```

```python
import numpy as np
import jax
import jax.numpy as jnp
from jax import lax
from jax.experimental import pallas as pl
from jax.experimental.pallas import tpu as pltpu

F32 = jnp.float32
BF16 = jnp.bfloat16
SDS = jax.ShapeDtypeStruct
MESH = pl.DeviceIdType.MESH

D = 1024
FF = 2816
NQ = 16
NKV = 2
HD = 64
GQ = NQ // NKV
AW = NQ * HD
KVW = NKV * HD
BLK = 128
PW = 512
PG = 128
POOL_WINDOWS = (2, 4, 8, 16)
HALO = 16
INW = AW + 2 * KVW + PW + 2 * D
C_KV = AW
C_Z = AW + 2 * KVW
C_G = C_Z + PW
EPS = 1e-6
FF_CHUNK = 256
FF_CHUNKS = tuple((c, FF_CHUNK) for c in range(0, FF, FF_CHUNK))
SLOPES = tuple(float(2.0 ** (-8.0 * h / NQ)) for h in range(1, NQ + 1))
SCALE = HD ** -0.5

LR, B1, B2, ADAM_EPS, WD, STEP = 0.001, 0.9, 0.999, 1e-08, 0.01, 10

VMEM_LIMIT = 56 * 1024 * 1024
N_CHIPS = 4

NT = (((1,), (1,)), ((), ()))
TN = (((0,), (0,)), ((), ()))


def _cp(sem=None, vmem=VMEM_LIMIT):
    return pltpu.CompilerParams(dimension_semantics=sem, vmem_limit_bytes=vmem)


def _const_spec(shape):
    nd = len(shape)
    return pl.BlockSpec(shape, lambda *_: (0,) * nd, pipeline_mode=pl.Buffered(1))


def _rstd(x):
    return lax.rsqrt(jnp.mean(x * x, axis=-1, keepdims=True) + EPS)


def _rms_bwd(dn, xhat, rstd, g):
    dxhat = dn * g
    return rstd * (dxhat - xhat * jnp.mean(dxhat * xhat, axis=-1, keepdims=True))


def _dot(a, b):
    return jnp.dot(a, b, preferred_element_type=F32)


def _dot_nt(a, b):
    return lax.dot_general(a, b, NT, preferred_element_type=F32)


def _dot_tn(a, b):
    return lax.dot_general(a, b, TN, preferred_element_type=F32)


def ffn_fwd(h, g, wup, wdn, name, head=None):
    T = h.shape[0]
    TM = 512
    tile = lambda w: pl.BlockSpec((TM, w), lambda i: (i, 0))
    acc_spec = lambda w: pl.BlockSpec((1, w), lambda i: (0, 0))

    def body(h_ref, g_ref, wup_ref, wdn_ref, *rest):
        out_ref, ab_ref, hid_ref = rest[-3:] if head is None else rest[2:5]
        x = h_ref[...]
        n = (x * _rstd(x) * g_ref[...]).astype(BF16)
        for c0, w in FF_CHUNKS:
            a = _dot(n, wup_ref[:, c0:c0 + w])
            b = _dot(n, wup_ref[:, FF + c0:FF + c0 + w])
            sig = jax.nn.sigmoid(a)
            s = a * sig
            ab_ref[:, c0:c0 + w] = (b * (sig * (1.0 + a * (1.0 - sig)))).astype(BF16)
            ab_ref[:, FF + c0:FF + c0 + w] = s.astype(BF16)
            hid_ref[:, c0:c0 + w] = (s * b).astype(BF16)
        out = x + 0.5 * _dot(hid_ref[...], wdn_ref[...])
        if head is None:
            out_ref[...] = out
        else:
            t_ref, gf_ref, loss_ref, dgf_ref = rest[0], rest[1], rest[5], rest[6]
            out_ref[...] = _loss_head(out, t_ref[...], gf_ref[...], loss_ref, dgf_ref, pl.program_id(0) == 0)

    head_in, head_specs, head_out_specs, head_out_shape = [], [], [], []
    if head is not None:
        head_in, head_specs = list(head), [tile(D), _const_spec((1, D))]
        head_out_specs, head_out_shape = [acc_spec(1), acc_spec(D)], [SDS((1, 1), F32), SDS((1, D), F32)]
    return pl.pallas_call(
        body, name=name, grid=(T // TM,),
        in_specs=[tile(D), _const_spec((1, D)), _const_spec((D, 2 * FF)), _const_spec((FF, D))] + head_specs,
        out_specs=[tile(D), tile(2 * FF), tile(FF)] + head_out_specs,
        out_shape=[SDS((T, D), F32), SDS((T, 2 * FF), BF16), SDS((T, FF), BF16)] + head_out_shape,
        compiler_params=_cp(("arbitrary",)),
    )(h, g, wup, wdn, *head_in)


def _loss_head(x, target, g, loss_ref, dg_ref, first):
    @pl.when(first)
    def _():
        loss_ref[...] = jnp.zeros_like(loss_ref)
        dg_ref[...] = jnp.zeros_like(dg_ref)

    rstd = _rstd(x)
    xhat = x * rstd
    err = xhat * g - target
    loss_ref[...] += 0.5 * jnp.sum(jnp.mean(err * err, axis=-1, keepdims=True), axis=0, keepdims=True)
    dy = err * (1.0 / D)
    dg_ref[...] += jnp.sum(dy * xhat, axis=0, keepdims=True)
    return _rms_bwd(dy, xhat, rstd, g)


def _adamw_tile(w_ref, g_ref, m_ref, v_ref, go_ref, d_ref, nm_ref, nv_ref):
    gv = g_ref[...]
    go_ref[...] = gv
    nm = B1 * m_ref[...] + (1.0 - B1) * gv
    nv = B2 * v_ref[...] + (1.0 - B2) * (gv * gv)
    nm_ref[...] = nm
    nv_ref[...] = nv
    d_ref[...] = -LR * ((nm / (1.0 - B1 ** STEP)) / (jnp.sqrt(nv / (1.0 - B2 ** STEP)) + ADAM_EPS) + WD * w_ref[...])


def _riders(updates, steps, step_of):
    in_specs, out_specs, out_shapes, operands, tiles = [], [], [], [], []
    for w, g, m, v in updates:
        R, C = w.shape
        n = max(d for d in range(1, steps + 1) if R % d == 0 and (R // d) % 8 == 0)
        spec = pl.BlockSpec((R // n, C), lambda *ids, n=n: (jnp.minimum(step_of(*ids), n - 1), 0))
        in_specs += [spec] * 4
        out_specs += [spec] * 4
        out_shapes += [SDS((R, C), F32)] * 4
        operands += [w, g, m, v]
        tiles.append(n)

    def run(step, in_refs, out_refs):
        for u, n in enumerate(tiles):
            @pl.when(step < n)
            def _(u=u):
                _adamw_tile(*in_refs[4 * u:4 * u + 4], *out_refs[4 * u:4 * u + 4])

    return in_specs, out_specs, out_shapes, operands, run


def ffn_bwd_x(dh, h_in, ab, g, wup, wdn, name, updates=()):
    T = dh.shape[0]
    TM = 256 if updates else 512
    SUB = 256
    r_in, r_out, r_shapes, r_args, ride = _riders(updates, T // TM, lambda i: i)

    def body(dh_ref, h_ref, ab_ref, g_ref, wup_ref, wdn_ref, *rest):
        dhin_ref, dup_ref, n_ref, dg_ref = rest[len(r_in):len(r_in) + 4]
        ride(pl.program_id(0), rest[:len(r_in)], rest[len(r_in) + 4:])
        g = g_ref[...]
        dg = None
        for r0 in range(0, TM, SUB):
            rows = slice(r0, r0 + SUB)
            x = h_ref[rows, :]
            rstd = _rstd(x)
            xhat = x * rstd
            n_ref[rows, :] = (xhat * g).astype(BF16)
            dh = dh_ref[rows, :]
            dhh = (0.5 * dh).astype(BF16)
            for c0, w in FF_CHUNKS:
                dhid = _dot_nt(dhh, wdn_ref[c0:c0 + w, :])
                dup_ref[rows, c0:c0 + w] = (dhid * ab_ref[rows, c0:c0 + w].astype(F32)).astype(BF16)
                dup_ref[rows, FF + c0:FF + c0 + w] = (dhid * ab_ref[rows, FF + c0:FF + c0 + w].astype(F32)).astype(BF16)
            dn = _dot_nt(dup_ref[rows, :], wup_ref[...])
            dhin_ref[rows, :] = dh + _rms_bwd(dn, xhat, rstd, g)
            part = jnp.sum(dn * xhat, axis=0, keepdims=True)
            dg = part if dg is None else dg + part

        @pl.when(pl.program_id(0) == 0)
        def _():
            dg_ref[...] = jnp.zeros_like(dg_ref)

        dg_ref[...] += dg

    tile = lambda w: pl.BlockSpec((TM, w), lambda i: (i, 0))
    return pl.pallas_call(
        body, name=name, grid=(T // TM,),
        in_specs=[tile(D), tile(D), tile(2 * FF), _const_spec((1, D)), _const_spec((D, 2 * FF)), _const_spec((FF, D))]
        + r_in,
        out_specs=[tile(D), tile(2 * FF), tile(D), pl.BlockSpec((1, D), lambda i: (0, 0))] + r_out,
        out_shape=[SDS((T, D), F32), SDS((T, 2 * FF), BF16), SDS((T, D), BF16), SDS((1, D), F32)] + r_shapes,
        compiler_params=_cp(("arbitrary",)),
    )(dh, h_in, ab, g, wup, wdn, *r_args)


TOKEN_SPEC = pl.BlockSpec((8, 128), lambda *_: (0, 0))


def _token_operand(token):
    return ([], []) if token is None else ([TOKEN_SPEC], [token])


def matmul_tn(a, b, name, *, tm, tn, tt=1024, b_scale=None, after=None):
    T, M = a.shape
    N = b.shape[1]
    tt = min(tt, T)
    assert M % tm == 0 and N % tn == 0 and T % tt == 0
    nt = T // tt
    token_spec, token_arg = _token_operand(after)

    def body(a_ref, b_ref, *rest):
        o_ref, acc_ref = rest[-2:]
        t = pl.program_id(2)

        @pl.when(t == 0)
        def _():
            acc_ref[...] = jnp.zeros_like(acc_ref)

        bv = b_ref[...]
        if b_scale is not None:
            bv = bv * b_scale
        acc_ref[...] += _dot_tn(a_ref[...].astype(BF16), bv.astype(BF16))

        @pl.when(t == nt - 1)
        def _():
            o_ref[...] = acc_ref[...].astype(BF16)

    return pl.pallas_call(
        body, name=name, grid=(M // tm, N // tn, nt),
        in_specs=[pl.BlockSpec((tt, tm), lambda i, j, t: (t, i)), pl.BlockSpec((tt, tn), lambda i, j, t: (t, j))]
        + token_spec,
        out_specs=pl.BlockSpec((tm, tn), lambda i, j, t: (i, j)),
        out_shape=SDS((M, N), BF16),
        scratch_shapes=[pltpu.VMEM((tm, tn), F32)],
        compiler_params=_cp(("parallel", "parallel", "arbitrary")),
    )(a, b, *token_arg)


def mix_in_fwd(h1, g, win_t):
    T = h1.shape[0]
    TM = 512

    def body(h_ref, g_ref, w_ref, u_ref, q_ref, kv_ref, z_ref, gt_ref):
        x = h_ref[...]
        u = (x * _rstd(x) * g_ref[...]).astype(BF16)
        u_ref[...] = u
        for c in range(0, AW, 256):
            q_ref[:, c:c + 256] = _dot_nt(u, w_ref[c:c + 256, :]).astype(BF16)
        kv_ref[...] = _dot_nt(u, w_ref[C_KV:C_Z, :]).astype(BF16)
        for c in range(0, PW, 256):
            z_ref[:, c:c + 256] = _dot_nt(u, w_ref[C_Z + c:C_Z + c + 256, :])
        for c in range(0, 2 * D, 256):
            gt_ref[:, c:c + 256] = _dot_nt(u, w_ref[C_G + c:C_G + c + 256, :]).astype(BF16)

    tile = lambda w: pl.BlockSpec((TM, w), lambda i: (i, 0))
    return pl.pallas_call(
        body, name="mix_in_fwd", grid=(T // TM,),
        in_specs=[tile(D), _const_spec((1, D)), _const_spec((INW, D))],
        out_specs=[tile(D), tile(AW), tile(2 * KVW), tile(PW), tile(2 * D)],
        out_shape=[SDS((T, D), BF16), SDS((T, AW), BF16), SDS((T, 2 * KVW), BF16), SDS((T, PW), F32),
                   SDS((T, 2 * D), BF16)],
        compiler_params=_cp(("arbitrary",)),
    )(h1, g, win_t)


def mix_in_bwd(dq, dkv, dz, dgt, dh2, h1, g, win_t):
    T = h1.shape[0]
    TM = min(1024, T)
    SUB = 256

    def body(dq_ref, dkv_ref, dz_ref, dgt_ref, dh2_ref, h_ref, g_ref, w_ref, dh1_ref, dg_ref):
        g = g_ref[...]
        dg = None
        for r0 in range(0, TM, SUB):
            rows = slice(r0, r0 + SUB)
            du = _dot(dq_ref[rows, :], w_ref[0:AW, :])
            du += _dot(dkv_ref[rows, :], w_ref[C_KV:C_Z, :])
            du += _dot(dz_ref[rows, :], w_ref[C_Z:C_G, :])
            du += _dot(dgt_ref[rows, :], w_ref[C_G:INW, :])
            x = h_ref[rows, :]
            rstd = _rstd(x)
            xhat = x * rstd
            dh1_ref[rows, :] = dh2_ref[rows, :] + _rms_bwd(du, xhat, rstd, g)
            part = jnp.sum(du * xhat, axis=0, keepdims=True)
            dg = part if dg is None else dg + part

        @pl.when(pl.program_id(0) == 0)
        def _():
            dg_ref[...] = jnp.zeros_like(dg_ref)

        dg_ref[...] += dg

    tile = lambda w: pl.BlockSpec((TM, w), lambda i: (i, 0))
    return pl.pallas_call(
        body, name="mix_in_bwd", grid=(T // TM,),
        in_specs=[tile(AW), tile(2 * KVW), tile(PW), tile(2 * D), tile(D), tile(D), _const_spec((1, D)),
                  _const_spec((INW, D))],
        out_specs=[tile(D), pl.BlockSpec((1, D), lambda i: (0, 0))],
        out_shape=[SDS((T, D), F32), SDS((1, D), F32)],
        compiler_params=_cp(("arbitrary",)),
    )(dq, dkv, dz, dgt, dh2, h1, g, win_t)


PAIR = 2 * HD
NPAIR = GQ // 2


def _lo_lanes():
    return lax.broadcasted_iota(jnp.int32, (BLK, PAIR), 1) < HD


def _stack_heads(ref, kvh, scale=None):
    lo = _lo_lanes()
    parts = []
    for pr in range(NPAIR):
        t = ref[:, (kvh * NPAIR + pr) * PAIR:(kvh * NPAIR + pr + 1) * PAIR]
        if scale is not None:
            t = t * scale
        zero = jnp.zeros_like(t)
        parts += [jnp.where(lo, t, zero), jnp.where(lo, zero, t)]
    return jnp.concatenate(parts, axis=0)


def _kv_tiles(kvc_ref, kvp_ref, tile, kvh):
    lo = _lo_lanes()
    dup, left, right = [], [], []
    for ref in (kvp_ref, kvc_ref):
        t = ref[:, tile * PAIR:(tile + 1) * PAIR]
        r = pltpu.roll(t.astype(F32), HD, 1).astype(BF16)
        zero = jnp.zeros_like(t)
        a, b = (t, r) if kvh == 0 else (r, t)
        dup.append(jnp.where(lo, a, b))
        left.append(jnp.where(lo, a, zero))
        right.append(jnp.where(lo, zero, b))
    cat = lambda xs: jnp.concatenate(xs, axis=0)
    return cat(dup), cat(left), cat(right)


def _band_consts(first):
    row = lax.broadcasted_iota(jnp.int32, (BLK, BLK), 0)
    col = lax.broadcasted_iota(jnp.int32, (BLK, BLK), 1)
    upper = col > row
    dist = jnp.where(upper, row - col + BLK, row - col).astype(F32)
    pen = jnp.where(jnp.logical_and(upper, first), -jnp.inf, 0.0)
    return upper, dist, pen


def _split_band(upper, t):
    zero = jnp.zeros_like(t)
    return jnp.concatenate([jnp.where(upper, t, zero), jnp.where(upper, zero, t)], axis=1)


def attn_fwd(q, kv, sinks):
    T = q.shape[0]
    nb = T // BLK

    def body(sink_ref, q_ref, kvc_ref, kvp_ref, att_ref, lse_ref):
        upper, dist, pen = _band_consts(pl.program_id(0) == 0)
        scores, values = [], []
        for kvh in range(NKV):
            kdup, _, _ = _kv_tiles(kvc_ref, kvp_ref, 0, kvh)
            values.append(_kv_tiles(kvc_ref, kvp_ref, 1, kvh)[1:])
            scores.append(_dot_nt(_stack_heads(q_ref, kvh, SCALE), kdup))
        for kvh in range(NKV):
            s_all = scores[kvh]
            vleft, vright = values[kvh]
            for pr in range(NPAIR):
                outs, inv = [], []
                for side, vpad in ((0, vleft), (1, vright)):
                    g = 2 * pr + side
                    hq = kvh * GQ + g
                    sink = sink_ref[0, hq]
                    rows = slice(g * BLK, (g + 1) * BLK)
                    s = jnp.where(upper, s_all[rows, 0:BLK], s_all[rows, BLK:2 * BLK]) - SLOPES[hq] * dist + pen
                    m = jnp.maximum(jnp.max(s, axis=-1, keepdims=True), sink)
                    p = jnp.exp(s - m)
                    l = jnp.sum(p, axis=-1, keepdims=True) + jnp.exp(sink - m)
                    lse_ref[:, hq:hq + 1] = m + jnp.log(l)
                    outs.append(_dot(_split_band(upper, p.astype(BF16)), vpad))
                    inv.append(1.0 / l)
                col0 = (kvh * NPAIR + pr) * PAIR
                att_ref[:, col0:col0 + PAIR] = ((outs[0] + outs[1]) * jnp.where(_lo_lanes(), inv[0], inv[1])).astype(BF16)

    return pl.pallas_call(
        body, name="attn_fwd", grid=(nb,),
        in_specs=[pl.BlockSpec(memory_space=pltpu.SMEM),
                  pl.BlockSpec((BLK, AW), lambda i: (i, 0)),
                  pl.BlockSpec((BLK, 2 * KVW), lambda i: (i, 0)),
                  pl.BlockSpec((BLK, 2 * KVW), lambda i: (jnp.maximum(i - 1, 0), 0))],
        out_specs=[pl.BlockSpec((BLK, AW), lambda i: (i, 0)), pl.BlockSpec((BLK, NQ), lambda i: (i, 0))],
        out_shape=[SDS((T, AW), BF16), SDS((T, NQ), F32)],
        compiler_params=_cp(("arbitrary",)),
    )(sinks, q, kv, kv)


def attn_bwd(q, kv, datt, att, lse, sinks):
    T = q.shape[0]
    nb = T // BLK

    def body(sink_ref, q_ref, kvc_ref, kvp_ref, do_ref, out_ref, lse_ref, dq_ref, dkv_ref, dsink_ref, carry_ref):
        i = pl.program_id(0)

        @pl.when(i == 0)
        def _():
            dsink_ref[...] = jnp.zeros_like(dsink_ref)
            carry_ref[...] = jnp.zeros_like(carry_ref)

        @pl.when(i < nb)
        def _():
            upper, dist, pen = _band_consts(i == 0)
            lo = _lo_lanes()
            dk_dup, dv_dup = [], []
            staged = []
            for kvh in range(NKV):
                kdup, kleft, kright = _kv_tiles(kvc_ref, kvp_ref, 0, kvh)
                vdup, _, _ = _kv_tiles(kvc_ref, kvp_ref, 1, kvh)
                qs = _stack_heads(q_ref, kvh, SCALE)
                dos = _stack_heads(do_ref, kvh)
                staged.append((kleft, kright, qs, dos, _dot_nt(qs, kdup), _dot_nt(dos, vdup)))
            deltas = []
            for pair in range(NQ // 2):
                cols = slice(pair * PAIR, (pair + 1) * PAIR)
                t = do_ref[:, cols].astype(F32) * out_ref[:, cols].astype(F32)
                deltas += [jnp.sum(jnp.where(lo, t, 0.0), axis=-1, keepdims=True),
                           jnp.sum(jnp.where(lo, 0.0, t), axis=-1, keepdims=True)]
            for kvh in range(NKV):
                kleft, kright, qs, dos, s_all, dp_all = staged[kvh]
                ds_parts, p_parts = [], []
                for pr in range(NPAIR):
                    dq = None
                    for side, kpad in ((0, kleft), (1, kright)):
                        g = 2 * pr + side
                        hq = kvh * GQ + g
                        lse_h = lse_ref[:, hq:hq + 1]
                        rows = slice(g * BLK, (g + 1) * BLK)
                        s = jnp.where(upper, s_all[rows, 0:BLK], s_all[rows, BLK:2 * BLK]) - SLOPES[hq] * dist + pen
                        p = jnp.exp(s - lse_h)
                        dp = jnp.where(upper, dp_all[rows, 0:BLK], dp_all[rows, BLK:2 * BLK])
                        delta = deltas[hq]
                        dsink_ref[:, hq:hq + 1] += -jnp.sum(jnp.exp(sink_ref[0, hq] - lse_h) * delta, axis=0,
                                                            keepdims=True)
                        ds = _split_band(upper, (p * (dp - delta)).astype(BF16))
                        ds_parts.append(ds)
                        p_parts.append(_split_band(upper, p.astype(BF16)))
                        d = _dot(ds, kpad)
                        dq = d if dq is None else dq + d
                    col0 = (kvh * NPAIR + pr) * PAIR
                    dq_ref[:, col0:col0 + PAIR] = (dq * SCALE).astype(BF16)
                dkw = _dot_tn(qs, jnp.concatenate(ds_parts, axis=0)).T
                dvw = _dot_tn(dos, jnp.concatenate(p_parts, axis=0)).T
                dk_dup.append(dkw + pltpu.roll(dkw, HD, 1))
                dv_dup.append(dvw + pltpu.roll(dvw, HD, 1))
            dk = jnp.where(jnp.concatenate([lo, lo], axis=0), dk_dup[0], dk_dup[1])
            dv = jnp.where(jnp.concatenate([lo, lo], axis=0), dv_dup[0], dv_dup[1])
            dkv_ref[:, 0:PAIR] = (carry_ref[:, 0:PAIR] + dk[0:BLK]).astype(BF16)
            dkv_ref[:, PAIR:2 * PAIR] = (carry_ref[:, PAIR:2 * PAIR] + dv[0:BLK]).astype(BF16)
            carry_ref[:, 0:PAIR] = dk[BLK:2 * BLK]
            carry_ref[:, PAIR:2 * PAIR] = dv[BLK:2 * BLK]

        @pl.when(i == nb)
        def _():
            dkv_ref[...] = carry_ref[...].astype(BF16)

    cur = lambda i: (jnp.minimum(i, nb - 1), 0)
    prev = lambda i: (jnp.maximum(jnp.minimum(i, nb - 1) - 1, 0), 0)
    return pl.pallas_call(
        body, name="attn_bwd", grid=(nb + 1,),
        in_specs=[pl.BlockSpec(memory_space=pltpu.SMEM),
                  pl.BlockSpec((BLK, AW), cur), pl.BlockSpec((BLK, 2 * KVW), cur), pl.BlockSpec((BLK, 2 * KVW), prev),
                  pl.BlockSpec((BLK, AW), cur), pl.BlockSpec((BLK, AW), cur), pl.BlockSpec((BLK, NQ), cur)],
        out_specs=[pl.BlockSpec((BLK, AW), cur),
                   pl.BlockSpec((BLK, 2 * KVW), lambda i: (jnp.maximum(i - 1, 0), 0)),
                   pl.BlockSpec((1, NQ), lambda i: (0, 0))],
        out_shape=[SDS((T, AW), BF16), SDS((T, 2 * KVW), BF16), SDS((1, NQ), F32)],
        scratch_shapes=[pltpu.VMEM((BLK, 2 * KVW), F32)],
        compiler_params=_cp(("arbitrary",)),
    )(sinks, q, kv, kv, datt, att, lse)


def _inv_counts(t0, rows):
    t = (t0 + lax.broadcasted_iota(jnp.int32, (rows, 1), 0) + 1).astype(F32)
    return [1.0 / jnp.minimum(t, float(w)) for w in POOL_WINDOWS]


def pool_fwd(z, wmix, scale):
    T = z.shape[0]
    TM = min(1024, T)
    L = TM + HALO

    def body(z_ref, halo_ref, wmix_ref, scale_ref, pooled_ref, mixs_ref):
        i = pl.program_id(0)
        halo = jnp.where(i > 0, halo_ref[...], 0.0)
        zt = z_ref[...]
        e = jnp.concatenate([halo, zt], axis=0)
        sums = []
        s = e
        for k in (1, 2, 4, 8):
            s = s + pltpu.roll(s, k, 0)
            sums.append(s)
        inv = _inv_counts(i * TM, TM)
        for gi in range(len(POOL_WINDOWS)):
            cols = slice(gi * PG, (gi + 1) * PG)
            pooled = (sums[gi][HALO:, cols] * inv[gi] - zt[:, cols]).astype(BF16)
            pooled_ref[:, cols] = pooled
            mixs_ref[:, cols] = (_dot(pooled, wmix_ref[gi]) * scale_ref[:, cols]).astype(BF16)

    return pl.pallas_call(
        body, name="pool_fwd", grid=(T // TM,),
        in_specs=[pl.BlockSpec((TM, PW), lambda i: (i, 0)),
                  pl.BlockSpec((HALO, PW), lambda i: (jnp.maximum(i * (TM // HALO) - 1, 0), 0)),
                  _const_spec((len(POOL_WINDOWS), PG, PG)), _const_spec((1, PW))],
        out_specs=[pl.BlockSpec((TM, PW), lambda i: (i, 0)), pl.BlockSpec((TM, PW), lambda i: (i, 0))],
        out_shape=[SDS((T, PW), BF16), SDS((T, PW), BF16)],
        compiler_params=_cp(("arbitrary",)),
    )(z, z, wmix, scale)


def pool_bwd(dmixs, pooled, wmix, scale):
    T = dmixs.shape[0]
    TM = min(1024, T)
    L = TM + HALO
    nt = T // TM

    def body(dm_ref, halo_ref, pooled_ref, wmix_ref, scale_ref, dz_ref, dwmix_ref, dscale_ref):
        i = pl.program_id(0)

        @pl.when(i == 0)
        def _():
            dwmix_ref[...] = jnp.zeros_like(dwmix_ref)
            dscale_ref[...] = jnp.zeros_like(dscale_ref)

        halo = jnp.where(i < nt - 1, halo_ref[...], 0.0)
        dm = dm_ref[...]
        e = jnp.concatenate([dm, halo], axis=0)
        inv = _inv_counts(i * TM, L)
        for gi in range(len(POOL_WINDOWS)):
            cols = slice(gi * PG, (gi + 1) * PG)
            w = wmix_ref[gi]
            dmixed = (e[:, cols] * scale_ref[:, cols]).astype(BF16)
            dpooled = _dot_nt(dmixed, w)
            pooled = pooled_ref[:, cols]
            mixed = _dot(pooled, w)
            dscale_ref[:, cols] += jnp.sum(dm[:, cols] * mixed, axis=0, keepdims=True)
            dwmix_ref[gi] += _dot_tn(pooled, dmixed[:TM])
            s = dpooled * inv[gi]
            k = 1
            while k < POOL_WINDOWS[gi]:
                s = s + pltpu.roll(s, L - k, 0)
                k *= 2
            dz_ref[:, cols] = (s[:TM] - dpooled[:TM]).astype(BF16)

    return pl.pallas_call(
        body, name="pool_bwd", grid=(nt,),
        in_specs=[pl.BlockSpec((TM, PW), lambda i: (i, 0)),
                  pl.BlockSpec((HALO, PW), lambda i: (jnp.minimum((i + 1) * (TM // HALO), T // HALO - 1), 0)),
                  pl.BlockSpec((TM, PW), lambda i: (i, 0)),
                  _const_spec((len(POOL_WINDOWS), PG, PG)), _const_spec((1, PW))],
        out_specs=[pl.BlockSpec((TM, PW), lambda i: (i, 0)),
                   pl.BlockSpec((len(POOL_WINDOWS), PG, PG), lambda i: (0, 0, 0)),
                   pl.BlockSpec((1, PW), lambda i: (0, 0))],
        out_shape=[SDS((T, PW), BF16), SDS((len(POOL_WINDOWS), PG, PG), F32), SDS((1, PW), F32)],
        compiler_params=_cp(("arbitrary",)),
    )(dmixs, dmixs, pooled, wmix, scale)


def merge_fwd(att, mixs, gt, h1, wattn, wpool, wout):
    T = h1.shape[0]
    TM = 512

    def body(att_ref, mixs_ref, gt_ref, h_ref, wa_ref, wp_ref, wo_ref, h2_ref, mg_ref, gf_ref):
        a = _dot(att_ref[...], wa_ref[...])
        p = _dot(mixs_ref[...], wp_ref[...])
        sa = jax.nn.sigmoid(gt_ref[:, 0:D].astype(F32))
        sp = jax.nn.sigmoid(gt_ref[:, D:2 * D].astype(F32))
        gf_ref[:, 0:D] = (a * sa * (1.0 - sa)).astype(BF16)
        gf_ref[:, D:2 * D] = (p * sp * (1.0 - sp)).astype(BF16)
        mg = (sa * a + sp * p).astype(BF16)
        mg_ref[...] = mg
        h2_ref[...] = h_ref[...] + _dot(mg, wo_ref[...])

    tile = lambda w: pl.BlockSpec((TM, w), lambda i: (i, 0))
    return pl.pallas_call(
        body, name="merge_fwd", grid=(T // TM,),
        in_specs=[tile(AW), tile(PW), tile(2 * D), tile(D), _const_spec((AW, D)), _const_spec((PW, D)),
                  _const_spec((D, D))],
        out_specs=[tile(D), tile(D), tile(2 * D)],
        out_shape=[SDS((T, D), F32), SDS((T, D), BF16), SDS((T, 2 * D), BF16)],
        compiler_params=_cp(("arbitrary",)),
    )(att, mixs, gt, h1, wattn, wpool, wout)


def merge_bwd(dh2, gt, gf, wattn, wpool, wout, after=None):
    T = dh2.shape[0]
    TM = 512
    token_spec, token_arg = _token_operand(after)

    def body(dh2_ref, gt_ref, gf_ref, wa_ref, wp_ref, wo_ref, *rest):
        datt_ref, dmixs_ref, dgt_ref, da_ref, dp_ref = rest[-5:]
        dm = _dot_nt(dh2_ref[...].astype(BF16), wo_ref[...])
        da = (dm * jax.nn.sigmoid(gt_ref[:, 0:D].astype(F32))).astype(BF16)
        dp = (dm * jax.nn.sigmoid(gt_ref[:, D:2 * D].astype(F32))).astype(BF16)
        da_ref[...] = da
        dp_ref[...] = dp
        dgt_ref[:, 0:D] = (dm * gf_ref[:, 0:D].astype(F32)).astype(BF16)
        dgt_ref[:, D:2 * D] = (dm * gf_ref[:, D:2 * D].astype(F32)).astype(BF16)
        datt_ref[...] = _dot_nt(da, wa_ref[...]).astype(BF16)
        dmixs_ref[...] = _dot_nt(dp, wp_ref[...])

    tile = lambda w: pl.BlockSpec((TM, w), lambda i: (i, 0))
    return pl.pallas_call(
        body, name="merge_bwd", grid=(T // TM,),
        in_specs=[tile(D), tile(2 * D), tile(2 * D), _const_spec((AW, D)), _const_spec((PW, D)),
                  _const_spec((D, D))] + token_spec,
        out_specs=[tile(AW), tile(PW), tile(2 * D), tile(D), tile(D)],
        out_shape=[SDS((T, AW), BF16), SDS((T, PW), F32), SDS((T, 2 * D), BF16), SDS((T, D), BF16),
                   SDS((T, D), BF16)],
        compiler_params=_cp(("arbitrary",)),
    )(dh2, gt, gf, wattn, wpool, wout, *token_arg)


def adamw(updates, place, name, after=None):
    tile_bytes = 2 * 1024 * 1024 // len(updates)
    jobs = []
    for w, g, m, v in updates:
        R, C = w.shape
        tr = R
        if R * C * 4 > tile_bytes:
            tr = next(cand for cand in (512, 256, 128, 64, 32, 16, 8) if R % cand == 0 and cand * C * 4 <= tile_bytes)
        blk = ((tr, C), lambda ls, p: (ls, 0))
        jobs.append((R // tr, [(a, *blk) for a in (w, g, m, v)], [((R, C), F32, *blk)] * 4,
                     lambda ins, outs: _adamw_tile(*ins, *outs)))
    return _run_jobs(jobs, place, name, after)


GROUP_FFN1, GROUP_MIX, GROUP_FFN2 = 0, 1, 2


def _behind(small, token):
    return small if token is None else small + token[0:1, 0:1]


def local_fwd_bwd(x, target, S, comm):
    W = comm.weights(GROUP_FFN1, None)
    h1, ab1, hid1 = ffn_fwd(x, _behind(S["ffn1_norm"], comm.started()), W["ffn1_w_up"], W["ffn1_w_down"], "ffn1_fwd")
    W = comm.weights(GROUP_MIX, h1)
    u, q, kv, z, gt = mix_in_fwd(h1, _behind(S["mix_norm"], comm.started()), W["w_in"])
    att, lse = attn_fwd(q, kv, S["sinks"])
    pooled, mixs = pool_fwd(z, S["pool_w_mix"], _behind(S["pool_scale"], comm.prefetch(GROUP_FFN2, att)))
    h2, merged, gf = merge_fwd(att, mixs, gt, h1, W["w_attn_up"], W["w_pool_up"], W["w_out"])
    W = comm.weights(GROUP_FFN2, h2)
    dh3, ab2, hid2, loss, g_final = ffn_fwd(h2, S["ffn2_norm"], W["ffn2_w_up"], W["ffn2_w_down"], "ffn2_fwd",
                                            head=(target, S["final_norm"]))

    G = {"final_norm": g_final}
    dh2, dup2, n2, G["ffn2_norm"] = ffn_bwd_x(dh3, h2, ab2, S["ffn2_norm"], W["ffn2_w_up"], W["ffn2_w_down"],
                                              "ffn2_bwd_x")
    ffn2 = ("ffn2_w_down", "ffn2_w_up")
    token = comm.grads({"ffn2_w_down": matmul_tn(hid2, dh3, "ffn2_dw_down", tm=1408, tn=512, tt=4096, b_scale=0.5),
                        "ffn2_w_up": matmul_tn(n2, dup2, "ffn2_dw_up", tm=D, tn=512, tt=4096)})

    W = comm.weights(GROUP_MIX, None)
    datt, dmixs, dgt, da, dp = merge_bwd(dh2, gt, gf, W["w_attn_up"], W["w_pool_up"], W["w_out"], after=token)
    token = comm.advance(ffn2, datt)
    g_mix = {"w_out": matmul_tn(merged, dh2, "dw_out", tm=D, tn=D, after=token),
             "w_attn_up": matmul_tn(att, da, "dw_attn_up", tm=AW, tn=D),
             "w_pool_up": matmul_tn(mixs, dp, "dw_pool_up", tm=PW, tn=D)}
    dz, G["pool_w_mix"], G["pool_scale"] = pool_bwd(dmixs, pooled, S["pool_w_mix"], _behind(S["pool_scale"], token))
    dq, dkv, G["sinks"] = attn_bwd(q, kv, datt, att, lse, _behind(S["sinks"], token))
    dh1, G["mix_norm"] = mix_in_bwd(dq, dkv, dz, dgt, dh2, h1, S["mix_norm"], W["w_in"])
    g_mix["w_in"] = jnp.concatenate([
        matmul_tn(dq, u, "dw_in_q", tm=AW, tn=D),
        matmul_tn(dkv, u, "dw_in_kv", tm=2 * KVW, tn=D),
        matmul_tn(dz, u, "dw_in_z", tm=PW, tn=D),
        matmul_tn(dgt, u, "dw_in_g", tm=D, tn=512, tt=4096),
    ], axis=0)
    mix = tuple(g_mix)
    token = comm.grads(g_mix)

    W = comm.weights(GROUP_FFN1, None)
    g_dn1 = matmul_tn(hid1, dh1, "ffn1_dw_down", tm=1408, tn=512, tt=4096, b_scale=0.5, after=token)
    token = comm.advance(mix, g_dn1)
    token2 = comm.grads({"ffn1_w_down": g_dn1})
    g1 = _behind(_behind(S["ffn1_norm"], token), token2)
    dx, dup1, n1, G["ffn1_norm"], *outs = ffn_bwd_x(dh1, x, ab1, g1, W["ffn1_w_up"], W["ffn1_w_down"], "ffn1_bwd_x",
                                                   updates=comm.reduce(ffn2, token2))
    comm.updated(ffn2, outs)
    token = comm.advance(("ffn1_w_down",), dx)
    token2 = comm.small_start(G, loss, dx)
    g_up1 = matmul_tn(n1, dup1, "ffn1_dw_up", tm=D, tn=512, tt=4096,
                      after=None if token is None else token + token2)
    token = comm.grads({"ffn1_w_up": g_up1})
    updates = comm.reduce(("ffn1_w_down",), token)
    token = comm.advance(("ffn1_w_up",), [u[1] for u in updates])
    done = comm.update(("ffn1_w_down",), updates, token)
    done = comm.finish(mix, done)
    done = comm.small_finish(done)
    comm.finish(("ffn1_w_up",), done)
    return loss, dx


HBM_SPEC = pl.BlockSpec(memory_space=pltpu.HBM)


def _me():
    return lax.axis_index("x"), lax.axis_index("y"), lax.axis_index("c")


def _peer_chip(x, y, k):
    return x ^ (k >> 1), y ^ (k & 1)


def _piece_half(ref, rowlike, j, h):
    if rowlike:
        return ref.at[j, h]
    ns = ref.shape[-1] // N_CHIPS
    return ref.at[h, :, pl.ds(pl.multiple_of(j * ns, 128), ns)]


def _piece(ref, rowlike, j):
    if rowlike:
        return ref.at[j]
    ns = ref.shape[-1] // N_CHIPS
    return ref.at[:, :, pl.ds(pl.multiple_of(j * ns, 128), ns)]


def _full_shape(shard_view, rowlike):
    _, kh, ns = shard_view.shape
    return (N_CHIPS, 2, kh, ns) if rowlike else (2, kh, N_CHIPS * ns)


def _remote(src, dst, send_sem, recv_sem, dev):
    return pltpu.make_async_remote_copy(src, dst, send_sem, recv_sem, device_id=dev, device_id_type=MESH)


SEM_SPEC = pl.BlockSpec(memory_space=pltpu.SEMAPHORE)
SPLIT_PARAMS = pltpu.CompilerParams(has_side_effects=pltpu.SideEffectType.DATAFLOW_SIDE_EFFECTING)


def _gather_plan(rowlikes):
    def plan(s_refs, f_refs, a, k, x, y, c):
        px, py = _peer_chip(x, y, k)
        return (s_refs[a].at[c], _piece_half(f_refs[a], rowlikes[a], 2 * x + y, c),
                _piece_half(f_refs[a], rowlikes[a], 2 * px + py, c), (px, py, c))
    return plan


def _all_to_all_plan(rowlikes):
    def plan(q_refs, r_refs, a, k, x, y, c):
        px, py = _peer_chip(x, y, k)
        if rowlikes[a]:
            src = q_refs[a].at[2 * px + py]
        else:
            ns = q_refs[a].shape[-1] // N_CHIPS
            src = q_refs[a].at[:, pl.ds(pl.multiple_of((2 * px + py) * ns, 128), ns)]
        return src, r_refs[a].at[k - 1], r_refs[a].at[k - 1], (px, py, c)
    return plan


def _swap_plan(rowlikes):
    def plan(g_refs, got_refs, a, k, x, y, c):
        src = g_refs[a].at[:, 1 - c] if rowlikes[a] else g_refs[a].at[1 - c]
        return src, got_refs[a], got_refs[a], (x, y, 1 - c)
    return plan


def _everyone_plan(s_refs, slot_refs, a, k, x, y, c):
    px, py, pc = x ^ (k >> 2), y ^ ((k >> 1) & 1), c ^ (k & 1)
    return s_refs[a], slot_refs[a].at[4 * x + 2 * y + c], slot_refs[a].at[4 * px + 2 * py + pc], (px, py, pc)


def _forward_plan(rowlikes):
    def plan(s_refs, f_refs, a, k, x, y, c):
        sib = (x, y, 1 - c)
        if k == 0:
            own = _piece(f_refs[a], rowlikes[a], 2 * x + y)
            return s_refs[a], own, own, sib
        px, py = _peer_chip(x, y, k)
        mine = _piece_half(f_refs[a], rowlikes[a], 2 * px + py, c)
        return mine, mine, _piece_half(f_refs[a], rowlikes[a], 2 * px + py, 1 - c), sib
    return plan


CHIPS, SIBLING, EVERYONE, FORWARDS = (1, 2, 3), (1,), tuple(range(1, 8)), (0, 1, 2, 3)


def _as_list(after):
    return [] if after is None else list(after) if isinstance(after, (list, tuple)) else [after]


def exchange_start(srcs, land_shapes, plan, after, name, peers=CHIPS):
    n = len(srcs)
    lands = [l if hasattr(l, "dtype") else lax.empty(l, s.dtype) for l, s in zip(land_shapes, srcs)]

    behind = _as_list(after)

    def body(*refs):
        s_refs, l_refs = refs[:n], refs[n:2 * n]
        send_sems, recv_sems = refs[2 * n + len(behind)], refs[2 * n + len(behind) + 1]
        token = refs[-1]
        x, y, c = _me()
        for a in range(n):
            for i, k in enumerate(peers):
                src, dst, _, peer = plan(s_refs, l_refs, a, k, x, y, c)
                sem = len(peers) * a + i
                _remote(src, dst, send_sems.at[sem], recv_sems.at[sem], peer).start()
        token[...] = jnp.zeros_like(token)

    n_sems = len(peers) * n
    outs = pl.pallas_call(
        body, name=name, in_specs=[HBM_SPEC] * (2 * n) + [pl.BlockSpec(memory_space=pl.ANY)] * len(behind),
        out_specs=[SEM_SPEC, SEM_SPEC] + [HBM_SPEC] * (2 * n) + [pl.BlockSpec(memory_space=pltpu.VMEM)],
        out_shape=[pltpu.SemaphoreType.DMA((n_sems,)), pltpu.SemaphoreType.DMA((n_sems,))]
        + [pltpu.HBM(a.shape, a.dtype) for a in (*srcs, *lands)] + [SDS((8, 128), F32)],
        input_output_aliases={i: 2 + i for i in range(2 * n)},
        compiler_params=SPLIT_PARAMS,
    )(*[pltpu.with_memory_space_constraint(a, pltpu.HBM) for a in (*srcs, *lands)], *behind)
    return {"sems": outs[:2], "srcs": outs[2:2 + n], "lands": outs[2 + n:2 + 2 * n], "token": outs[-1]}


def exchange_wait(state, plan, after, name, peers=CHIPS):
    n = len(state["srcs"])
    behind = _as_list(after)

    def body(*refs):
        s_refs, l_refs = refs[:n], refs[n:2 * n]
        send_sems, recv_sems = refs[2 * n], refs[2 * n + 1]
        x, y, c = _me()
        for a in range(n):
            for i, k in enumerate(peers):
                src, _, landing, peer = plan(s_refs, l_refs, a, k, x, y, c)
                sem = len(peers) * a + i
                cp = _remote(src, landing, send_sems.at[sem], recv_sems.at[sem], peer)
                cp.wait_send()
                cp.wait_recv()

    bufs = (*state["srcs"], *state["lands"])
    outs = pl.pallas_call(
        body, name=name,
        in_specs=[HBM_SPEC] * (2 * n) + [SEM_SPEC, SEM_SPEC] + [pl.BlockSpec(memory_space=pl.ANY)] * len(behind),
        out_specs=[HBM_SPEC] * (2 * n),
        out_shape=[pltpu.HBM(a.shape, a.dtype) for a in bufs],
        input_output_aliases={i: i for i in range(2 * n)},
        compiler_params=SPLIT_PARAMS,
    )(*bufs, *state["sems"], *behind)
    return outs[:n], outs[n:]


def gather_finish(shards, fulls, rowlikes, name):
    n = len(shards)

    def body(*refs):
        s_refs, f_refs = refs[:n], refs[2 * n:3 * n]
        send_sems, recv_sems = refs[3 * n:]
        x, y, c = _me()
        chip = 2 * x + y
        sib = (x, y, 1 - c)
        sends = []
        for a in range(n):
            own = _piece(f_refs[a], rowlikes[a], chip)
            cp = _remote(s_refs[a], own, send_sems.at[a, 0], recv_sems.at[a, 0], sib)
            cp.start()
            sends.append(cp)
            for k in (1, 2, 3):
                px, py = _peer_chip(x, y, k)
                slot = _piece_half(f_refs[a], rowlikes[a], 2 * px + py, c)
                cp = _remote(slot, slot, send_sems.at[a, k], recv_sems.at[a, k], sib)
                cp.start()
                sends.append(cp)
        for a in range(n):
            own = _piece(f_refs[a], rowlikes[a], chip)
            _remote(own, own, send_sems.at[a, 0], recv_sems.at[a, 0], sib).wait_recv()
            for k in (1, 2, 3):
                px, py = _peer_chip(x, y, k)
                slot = _piece_half(f_refs[a], rowlikes[a], 2 * px + py, 1 - c)
                _remote(slot, slot, send_sems.at[a, k], recv_sems.at[a, k], sib).wait_recv()
        for cp in sends:
            cp.wait_send()

    return pl.pallas_call(
        body, name=name, in_specs=[HBM_SPEC] * (2 * n), out_specs=[HBM_SPEC] * n,
        out_shape=[SDS(f.shape, f.dtype) for f in fulls],
        input_output_aliases={n + a: a for a in range(n)},
        scratch_shapes=[pltpu.SemaphoreType.DMA((n, 4)), pltpu.SemaphoreType.DMA((n, 4))],
    )(*shards, *fulls)


def _half_buffer_shape(gview, rowlike):
    return (N_CHIPS,) + gview.shape[2:] if rowlike else gview.shape[1:]


def _row_tiles(kh, ns):
    return 1 if kh * ns <= 256 * 1024 else 2


def _run_jobs(jobs, place, name, after=None):
    n_steps = max(job[0] for job in jobs)

    def spec(block, index, steps):
        return pl.BlockSpec(block, lambda s, p: index(jnp.minimum(s, steps - 1), p))

    in_specs = [spec(b, ix, steps) for steps, ins, _, _ in jobs for _, b, ix in ins]
    out_specs = [spec(b, ix, steps) for steps, _, outs, _ in jobs for _, _, b, ix in outs]
    token_spec, token_arg = _token_operand(after)
    n_in = len(in_specs) + len(token_spec)

    def body(place_ref, *refs):
        s = pl.program_id(0)
        i, o = 0, n_in
        for steps, ins, outs, fn in jobs:
            in_refs, out_refs = refs[i:i + len(ins)], refs[o:o + len(outs)]
            i, o = i + len(ins), o + len(outs)

            @pl.when(s < steps)
            def _(fn=fn, in_refs=in_refs, out_refs=out_refs):
                fn(in_refs, out_refs)

    return pl.pallas_call(
        body, name=name,
        grid_spec=pltpu.PrefetchScalarGridSpec(num_scalar_prefetch=1, grid=(n_steps,),
                                               in_specs=in_specs + token_spec, out_specs=out_specs),
        out_shape=[SDS(shape, dtype) for _, _, outs, _ in jobs for shape, dtype, _, _ in outs],
        compiler_params=_cp(("arbitrary",)),
    )(place, *[a for _, ins, _, _ in jobs for a, _, _ in ins], *token_arg)


def add_halves(gviews, gots, rowlikes, place, name):
    def add(ins, outs):
        outs[0][...] = (ins[0][...].astype(F32) + ins[1][...].astype(F32)).astype(BF16)

    jobs = []
    for g, got, rowlike in zip(gviews, gots, rowlikes):
        kh, ns = (g.shape[2], g.shape[3]) if rowlike else (g.shape[1], g.shape[2] // N_CHIPS)
        rt = _row_tiles(kh, ns)
        tr = kh // rt
        if rowlike:
            g_in = (g, (None, None, tr, ns), lambda ls, p, rt=rt: (ls // rt, p[0], ls % rt, 0))
            half = ((None, tr, ns), lambda ls, p, rt=rt: (ls // rt, ls % rt, 0))
        else:
            g_in = (g, (None, tr, ns), lambda ls, p, rt=rt: (p[0], ls % rt, ls // rt))
            half = ((tr, ns), lambda ls, p, rt=rt: (ls % rt, ls // rt))
        jobs.append((N_CHIPS * rt, [g_in, (got, *half)], [(got.shape, BF16, *half)], add))
    return _run_jobs(jobs, place, name)


def _slot_shape(q, rowlike):
    return (3,) + (q.shape[1:] if rowlike else (q.shape[0], q.shape[1] // N_CHIPS))


def sum_pieces(qs, recvs, rowlikes, place, name):
    def total(ins, outs):
        acc = ins[0][...].astype(F32)
        for k in range(3):
            acc = acc + ins[1][k].astype(F32)
        outs[0][...] = acc

    jobs = []
    for q, recv, rowlike in zip(qs, recvs, rowlikes):
        kh, ns = recv.shape[1], recv.shape[2]
        rt = _row_tiles(kh, ns)
        tr = kh // rt
        mine = ((None, tr, ns), lambda ls, p: (p[1], ls, 0)) if rowlike else ((tr, ns), lambda ls, p: (ls, p[1]))
        jobs.append((rt, [(q, *mine), (recv, (3, tr, ns), lambda ls, p: (0, ls, 0))],
                     [((2, kh, ns), F32, (None, tr, ns), lambda ls, p: (p[0], ls, 0))], total))
    return _run_jobs(jobs, place, name)


def join_halves(halves, name):
    n = len(halves)

    def body(*refs):
        o_refs = refs[n:2 * n]
        send_sems, recv_sems = refs[2 * n:]
        x, y, c = _me()
        sib = (x, y, 1 - c)
        sends = []
        for a in range(n):
            cp = _remote(o_refs[a].at[c], o_refs[a].at[c], send_sems.at[a], recv_sems.at[a], sib)
            cp.start()
            sends.append(cp)
        for a in range(n):
            got = o_refs[a].at[1 - c]
            _remote(got, got, send_sems.at[a], recv_sems.at[a], sib).wait_recv()
        for cp in sends:
            cp.wait_send()

    return pl.pallas_call(
        body, name=name, in_specs=[HBM_SPEC] * n, out_specs=[HBM_SPEC] * n,
        out_shape=[SDS(h.shape, h.dtype) for h in halves],
        input_output_aliases={a: a for a in range(n)},
        scratch_shapes=[pltpu.SemaphoreType.DMA((n,)), pltpu.SemaphoreType.DMA((n,))],
    )(*halves)


N_DEV = 8


def sum_devices(s, slots, me):
    R, Wd = s.shape

    def body(me_ref, s_ref, slots_ref, out_ref):
        acc = None
        for d in range(N_DEV):
            mine = me_ref[0] == d
            term = jnp.where(mine, s_ref[...], slots_ref[jnp.where(mine, d ^ 1, d)])
            acc = term if acc is None else acc + term
        out_ref[...] = acc

    vmem = pl.BlockSpec(memory_space=pltpu.VMEM)
    return pl.pallas_call(
        body, name="sum_devices", in_specs=[pl.BlockSpec(memory_space=pltpu.SMEM), vmem, vmem], out_specs=vmem,
        out_shape=SDS((R, Wd), F32),
    )(me, s, slots)


TRANSPOSED = ("w_in",)
BIG = {"ffn1_w_up": (D, 2 * FF, "col"), "ffn1_w_down": (FF, D, "row"), "w_in": (INW, D, "row"),
       "w_attn_up": (AW, D, "row"), "w_pool_up": (PW, D, "col"), "w_out": (D, D, "row"),
       "ffn2_w_up": (D, 2 * FF, "col"), "ffn2_w_down": (FF, D, "row")}
GROUPS = (("ffn1_w_up", "ffn1_w_down"), ("w_in", "w_attn_up", "w_pool_up", "w_out"), ("ffn2_w_up", "ffn2_w_down"))
SMALL = ("ffn1_norm", "mix_norm", "ffn2_norm", "final_norm", "pool_scale", "sinks", "pool_w_mix")
SMALL_W = 128


def _rowlike(name):
    return BIG[name][2] == "row"


def _half_dims(name):
    k, n, kind = BIG[name]
    return (k // N_CHIPS // 2, n) if kind == "row" else (k // 2, n // N_CHIPS)


def shard_view(name, shard):
    return shard.reshape((2,) + _half_dims(name))


def full_from_view(name, fv):
    k, n, _ = BIG[name]
    return fv.reshape(k, n)


def grad_view(name, g):
    kh, ns = _half_dims(name)
    return g.reshape(_full_shape(jax.ShapeDtypeStruct((2, kh, ns), g.dtype), _rowlike(name)))


def pack_small(d):
    parts = []
    for name in SMALL:
        a = d[name].reshape(-1)
        pad = (-a.shape[0]) % SMALL_W
        parts.append(jnp.pad(a, (0, pad)).reshape(-1, SMALL_W))
    a = jnp.concatenate(parts, axis=0)
    return jnp.pad(a, ((0, (-a.shape[0]) % 8), (0, 0)))


def unpack_small(a, like):
    out, r0 = {}, 0
    for name in SMALL:
        size = int(np.prod(like[name].shape))
        rows = -(-size // SMALL_W)
        out[name] = a[r0:r0 + rows].reshape(-1)[:size].reshape(like[name].shape)
        r0 += rows
    return out


WEIGHTS = ("ffn1_norm", "ffn1_w_up", "ffn1_w_down", "mix_norm", "w_in", "sinks", "w_attn_up", "pool_w_mix",
           "pool_scale", "w_pool_up", "w_out", "ffn2_norm", "ffn2_w_up", "ffn2_w_down", "final_norm")


def kernel(x, ffn1_norm, ffn1_w_up, ffn1_w_down, mix_norm, w_in, sinks, w_attn_up, pool_w_mix, pool_scale, w_pool_up, w_out, ffn2_norm, ffn2_w_up, ffn2_w_down, final_norm, loss_target, m_ffn1_norm, m_ffn1_w_up, m_ffn1_w_down, m_mix_norm, m_w_in, m_sinks, m_w_attn_up, m_pool_w_mix, m_pool_scale, m_w_pool_up, m_w_out, m_ffn2_norm, m_ffn2_w_up, m_ffn2_w_down, m_final_norm, v_ffn1_norm, v_ffn1_w_up, v_ffn1_w_down, v_mix_norm, v_w_in, v_sinks, v_w_attn_up, v_pool_w_mix, v_pool_scale, v_w_pool_up, v_w_out, v_ffn2_norm, v_ffn2_w_up, v_ffn2_w_down, v_final_norm):
    given = dict(locals())
    w = {n: given[n] for n in WEIGHTS}
    m = {n: given["m_" + n] for n in WEIGHTS}
    v = {n: given["v_" + n] for n in WEIGHTS}
    cx, cy, cc = _me()
    place = jnp.stack([cc, 2 * cx + cy]).astype(jnp.int32)

    def local2d(d, n):
        return d[n][0].T if n in TRANSPOSED else d[n][0]

    shards = {n: local2d(w, n) for n in BIG}
    grads, delta, new_m, new_v = {}, {}, {}, {}
    rowlikes = [[_rowlike(n) for n in names] for names in GROUPS]

    class Exchanges:
        def __init__(self):
            self.gathers, self.forwards, self.fulls, self.reductions, self.small = {}, {}, {}, {}, None

        def _behind_first(self, a):
            return a + self.gathers[0]["token"][0, 0]

        def _start_gather(self, group, after):
            prepare = self._behind_first if group else (lambda a: a)
            sv = [shard_view(n, prepare(shards[n]).astype(BF16)) for n in GROUPS[group]]
            rl = rowlikes[group]
            self.gathers[group] = exchange_start(sv, [_full_shape(s, r) for s, r in zip(sv, rl)], _gather_plan(rl),
                                                 after, f"gather_start_{group}")

        def weights(self, group, after):
            if not self.gathers:
                self._start_gather(0, None)
                self.packed = [self._behind_first(pack_small(d)) for d in (w, m, v)]
                after = [self.gathers[0]["token"], *self.packed]
            if group not in self.fulls:
                names, rl, state = GROUPS[group], rowlikes[group], self.gathers[group]
                if group in self.forwards:
                    _, fulls = exchange_wait(self.forwards.pop(group), _forward_plan(rl), after,
                                             f"forward_wait_{group}", FORWARDS)
                else:
                    sv, fulls = exchange_wait(state, _gather_plan(rl), state["token"] if after is None else after,
                                              f"gather_wait_{group}")
                    fulls = gather_finish(sv, fulls, rl, f"gather_finish_{group}")
                self.fulls[group] = {n: full_from_view(n, f) for n, f in zip(names, fulls)}
                if group + 1 < len(GROUPS):
                    self._start_gather(group + 1, fulls[0])
            return self.fulls[group]

        def started(self):
            return self.gathers[max(self.gathers)]["token"]

        def prefetch(self, group, after):
            rl, state = rowlikes[group], self.gathers[group]
            sv, fulls = exchange_wait(state, _gather_plan(rl), after, f"gather_wait_{group}")
            self.forwards[group] = exchange_start(sv, fulls, _forward_plan(rl), None, f"forward_start_{group}", FORWARDS)
            return self.forwards[group]["token"]

        def grads(self, g):
            names = tuple(g)
            rl = [_rowlike(n) for n in names]
            gv = [grad_view(n, g[n]) for n in names]
            state = exchange_start(gv, [_half_buffer_shape(a, r) for a, r in zip(gv, rl)], _swap_plan(rl), None,
                                   "swap_start_" + names[0], SIBLING)
            self.reductions[names] = state
            return state["token"]

        def advance(self, names, after):
            rl = [_rowlike(n) for n in names]
            gv, gots = exchange_wait(self.reductions[names], _swap_plan(rl), after, "swap_wait_" + names[0], SIBLING)
            qs = add_halves(gv, gots, rl, place, "add_halves_" + names[0])
            state = exchange_start(qs, [_slot_shape(q, r) for q, r in zip(qs, rl)], _all_to_all_plan(rl), None,
                                   "all_to_all_start_" + names[0])
            self.reductions[names] = state
            return state["token"]

        def reduce(self, names, after):
            rl = [_rowlike(n) for n in names]
            qs, recvs = exchange_wait(self.reductions.pop(names), _all_to_all_plan(rl), after,
                                      "all_to_all_wait_" + names[0])
            halves = sum_pieces(qs, recvs, rl, place, "sum_pieces_" + names[0])
            return [(shards[n], o.reshape(shards[n].shape), local2d(m, n), local2d(v, n))
                    for n, o in zip(names, join_halves(halves, "join_halves_" + names[0]))]

        def updated(self, names, outs):
            for i, n in enumerate(names):
                grads[n], delta[n], new_m[n], new_v[n] = outs[4 * i:4 * i + 4]
            return [new_v[n] for n in names]

        def update(self, names, updates, token=None):
            return self.updated(names, adamw(updates, place, "adamw_" + names[0], after=token))

        def finish(self, names, after):
            return self.update(names, self.reduce(names, after))

        def small_start(self, G, loss, after):
            packed = pack_small({n: G[n] for n in SMALL})
            used_rows = sum(-(-int(np.prod(small_like[n].shape)) // SMALL_W) for n in SMALL)
            assert packed.shape[0] > used_rows
            packed = packed.at[-1, 0].set(loss[0, 0])
            self.small = exchange_start([packed], [(N_DEV,) + packed.shape], _everyone_plan, after,
                                        "small_start", EVERYONE)
            return self.small["token"]

        def small_finish(self, after):
            (packed,), (slots,) = exchange_wait(self.small, _everyone_plan, after, "small_wait", EVERYONE)
            total = sum_devices(packed, slots, (4 * cx + 2 * cy + cc).astype(jnp.int32).reshape(1))
            self.loss = total[-1, 0]
            g, ds, ms, vs = adamw([(self.packed[0], total, self.packed[1], self.packed[2])], place, "adamw_small")
            for d, packed_d in ((grads, g), (delta, ds), (new_m, ms), (new_v, vs)):
                d.update(unpack_small(packed_d, small_like))
            return vs

    small_like = {n: w[n] for n in SMALL}
    S = {n: w[n].reshape(1, -1) for n in ("ffn1_norm", "mix_norm", "ffn2_norm", "final_norm", "pool_scale", "sinks")}
    S["pool_w_mix"] = w["pool_w_mix"][0].astype(BF16)
    exchanges = Exchanges()
    _, dx = local_fwd_bwd(x[0], loss_target[0], S, exchanges)
    loss = exchanges.loss

    def shaped(d, n):
        return (d[n].T if n in TRANSPOSED else d[n]).reshape(w[n].shape)

    return (loss, dx[None], *[shaped(grads, n) for n in WEIGHTS], *[shaped(delta, n) for n in WEIGHTS],
            *[shaped(new_m, n) for n in WEIGHTS], *[shaped(new_v, n) for n in WEIGHTS])
```

```python
import numpy as np
import jax
import jax.numpy as jnp
from jax import lax
from jax.experimental import pallas as pl
from jax.experimental.pallas import tpu as pltpu

F32 = jnp.float32
BF16 = jnp.bfloat16
SDS = jax.ShapeDtypeStruct
MESH = pl.DeviceIdType.MESH

D = 1024
FF = 2816
NQ = 16
NKV = 2
HD = 64
GQ = NQ // NKV
AW = NQ * HD
KVW = NKV * HD
BLK = 128
PW = 512
PG = 128
POOL_WINDOWS = (2, 4, 8, 16)
HALO = 16
INW = AW + 2 * KVW + PW + 2 * D
C_KV = AW
C_Z = AW + 2 * KVW
C_G = C_Z + PW
EPS = 1e-6
FF_CHUNK = 256
FF_CHUNKS = tuple((c, FF_CHUNK) for c in range(0, FF, FF_CHUNK))
SLOPES = tuple(float(2.0 ** (-8.0 * h / NQ)) for h in range(1, NQ + 1))
SCALE = HD ** -0.5

LR, B1, B2, ADAM_EPS, WD, STEP = 0.001, 0.9, 0.999, 1e-08, 0.01, 10

VMEM_LIMIT = 56 * 1024 * 1024
N_CHIPS = 4

NT = (((1,), (1,)), ((), ()))
TN = (((0,), (0,)), ((), ()))


def _cp(sem=None, vmem=VMEM_LIMIT):
    return pltpu.CompilerParams(dimension_semantics=sem, vmem_limit_bytes=vmem)


def _const_spec(shape):
    nd = len(shape)
    return pl.BlockSpec(shape, lambda *_: (0,) * nd, pipeline_mode=pl.Buffered(1))


def _rstd(x):
    return lax.rsqrt(jnp.mean(x * x, axis=-1, keepdims=True) + EPS)


def _rms_bwd(dn, xhat, rstd, g):
    dxhat = dn * g
    return rstd * (dxhat - xhat * jnp.mean(dxhat * xhat, axis=-1, keepdims=True))


def _dot(a, b):
    return jnp.dot(a, b, preferred_element_type=F32)


def _dot_nt(a, b):
    return lax.dot_general(a, b, NT, preferred_element_type=F32)


def _dot_tn(a, b):
    return lax.dot_general(a, b, TN, preferred_element_type=F32)


def _swiglu_up(x, g, wup_ref, ab_ref, hid_ref):
    n = (x * _rstd(x) * g).astype(BF16)
    for c0, w in FF_CHUNKS:
        a = _dot(n, wup_ref[:, c0:c0 + w])
        b = _dot(n, wup_ref[:, FF + c0:FF + c0 + w])
        sig = jax.nn.sigmoid(a)
        s = a * sig
        ab_ref[:, c0:c0 + w] = (b * (sig * (1.0 + a * (1.0 - sig)))).astype(BF16)
        ab_ref[:, FF + c0:FF + c0 + w] = s.astype(BF16)
        hid_ref[:, c0:c0 + w] = (s * b).astype(BF16)


def ffn_up(h, g, wup, name):
    T = h.shape[0]
    TM = 512
    tile = lambda w: pl.BlockSpec((TM, w), lambda i: (i, 0))

    def body(h_ref, g_ref, wup_ref, ab_ref, hid_ref):
        _swiglu_up(h_ref[...], g_ref[...], wup_ref, ab_ref, hid_ref)

    return pl.pallas_call(
        body, name=name, grid=(T // TM,),
        in_specs=[tile(D), _const_spec((1, D)), _const_spec((D, 2 * FF))],
        out_specs=[tile(2 * FF), tile(FF)],
        out_shape=[SDS((T, 2 * FF), BF16), SDS((T, FF), BF16)],
        compiler_params=_cp(("arbitrary",)),
    )(h, g, wup)


def ffn_down(h, hid, wdn, name, after=None):
    T = h.shape[0]
    TM = min(1024, T)
    tile = lambda w: pl.BlockSpec((TM, w), lambda i: (i, 0))
    token_spec, token_arg = _token_operand(after)

    def body(h_ref, hid_ref, wdn_ref, *rest):
        rest[-1][...] = h_ref[...] + 0.5 * _dot(hid_ref[...], wdn_ref[...])

    return pl.pallas_call(
        body, name=name, grid=(T // TM,),
        in_specs=[tile(D), tile(FF), _const_spec((FF, D))] + token_spec,
        out_specs=tile(D), out_shape=SDS((T, D), F32),
        compiler_params=_cp(("arbitrary",)),
    )(h, hid, wdn, *token_arg)


def ffn_fwd(h, g, wup, wdn, name, head=None):
    T = h.shape[0]
    TM = 512
    tile = lambda w: pl.BlockSpec((TM, w), lambda i: (i, 0))
    acc_spec = lambda w: pl.BlockSpec((1, w), lambda i: (0, 0))

    def body(h_ref, g_ref, wup_ref, wdn_ref, *rest):
        out_ref, ab_ref, hid_ref = rest[-3:] if head is None else rest[2:5]
        x = h_ref[...]
        _swiglu_up(x, g_ref[...], wup_ref, ab_ref, hid_ref)
        out = x + 0.5 * _dot(hid_ref[...], wdn_ref[...])
        if head is None:
            out_ref[...] = out
        else:
            t_ref, gf_ref, loss_ref, dgf_ref = rest[0], rest[1], rest[5], rest[6]
            out_ref[...] = _loss_head(out, t_ref[...], gf_ref[...], loss_ref, dgf_ref, pl.program_id(0) == 0)

    head_in, head_specs, head_out_specs, head_out_shape = [], [], [], []
    if head is not None:
        head_in, head_specs = list(head), [tile(D), _const_spec((1, D))]
        head_out_specs, head_out_shape = [acc_spec(1), acc_spec(D)], [SDS((1, 1), F32), SDS((1, D), F32)]
    return pl.pallas_call(
        body, name=name, grid=(T // TM,),
        in_specs=[tile(D), _const_spec((1, D)), _const_spec((D, 2 * FF)), _const_spec((FF, D))] + head_specs,
        out_specs=[tile(D), tile(2 * FF), tile(FF)] + head_out_specs,
        out_shape=[SDS((T, D), F32), SDS((T, 2 * FF), BF16), SDS((T, FF), BF16)] + head_out_shape,
        compiler_params=_cp(("arbitrary",)),
    )(h, g, wup, wdn, *head_in)


def _loss_head(x, target, g, loss_ref, dg_ref, first):
    @pl.when(first)
    def _():
        loss_ref[...] = jnp.zeros_like(loss_ref)
        dg_ref[...] = jnp.zeros_like(dg_ref)

    rstd = _rstd(x)
    xhat = x * rstd
    err = xhat * g - target
    loss_ref[...] += 0.5 * jnp.sum(jnp.mean(err * err, axis=-1, keepdims=True), axis=0, keepdims=True)
    dy = err * (1.0 / D)
    dg_ref[...] += jnp.sum(dy * xhat, axis=0, keepdims=True)
    return _rms_bwd(dy, xhat, rstd, g)


def _adamw_tile(w_ref, g_ref, m_ref, v_ref, go_ref, d_ref, nm_ref, nv_ref):
    gv = g_ref[...]
    go_ref[...] = gv
    nm = B1 * m_ref[...] + (1.0 - B1) * gv
    nv = B2 * v_ref[...] + (1.0 - B2) * (gv * gv)
    nm_ref[...] = nm
    nv_ref[...] = nv
    d_ref[...] = -LR * ((nm / (1.0 - B1 ** STEP)) / (jnp.sqrt(nv / (1.0 - B2 ** STEP)) + ADAM_EPS) + WD * w_ref[...])


def _riders(updates, steps):
    in_specs, out_specs, out_shapes, operands, tiles = [], [], [], [], []
    for w, g, m, v in updates:
        R, C = w.shape
        n = max(d for d in range(1, steps + 1) if R % d == 0 and (R // d) % 8 == 0)
        spec = pl.BlockSpec((R // n, C), lambda i, n=n: (jnp.minimum(i, n - 1), 0))
        in_specs += [spec] * 4
        out_specs += [spec] * 4
        out_shapes += [SDS((R, C), F32)] * 4
        operands += [w, g, m, v]
        tiles.append(n)

    def run(step, in_refs, out_refs):
        for u, n in enumerate(tiles):
            @pl.when(step < n)
            def _(u=u):
                _adamw_tile(*in_refs[4 * u:4 * u + 4], *out_refs[4 * u:4 * u + 4])

    return in_specs, out_specs, out_shapes, operands, run


def ffn_bwd_x(dh, h_in, ab, g, wup, wdn, name, updates=()):
    T = dh.shape[0]
    TM = 256 if updates else 512
    SUB = 256
    r_in, r_out, r_shapes, r_args, ride = _riders(updates, T // TM)

    def body(dh_ref, h_ref, ab_ref, g_ref, wup_ref, wdn_ref, *rest):
        dhin_ref, dup_ref, n_ref, dg_ref = rest[len(r_in):len(r_in) + 4]
        ride(pl.program_id(0), rest[:len(r_in)], rest[len(r_in) + 4:])
        g = g_ref[...]
        dg = None
        for r0 in range(0, TM, SUB):
            rows = slice(r0, r0 + SUB)
            x = h_ref[rows, :]
            rstd = _rstd(x)
            xhat = x * rstd
            n_ref[rows, :] = (xhat * g).astype(BF16)
            dh = dh_ref[rows, :]
            dhh = (0.5 * dh).astype(BF16)
            for c0, w in FF_CHUNKS:
                dhid = _dot_nt(dhh, wdn_ref[c0:c0 + w, :])
                dup_ref[rows, c0:c0 + w] = (dhid * ab_ref[rows, c0:c0 + w].astype(F32)).astype(BF16)
                dup_ref[rows, FF + c0:FF + c0 + w] = (dhid * ab_ref[rows, FF + c0:FF + c0 + w].astype(F32)).astype(BF16)
            dn = _dot_nt(dup_ref[rows, :], wup_ref[...])
            dhin_ref[rows, :] = dh + _rms_bwd(dn, xhat, rstd, g)
            part = jnp.sum(dn * xhat, axis=0, keepdims=True)
            dg = part if dg is None else dg + part

        @pl.when(pl.program_id(0) == 0)
        def _():
            dg_ref[...] = jnp.zeros_like(dg_ref)

        dg_ref[...] += dg

    tile = lambda w: pl.BlockSpec((TM, w), lambda i: (i, 0))
    return pl.pallas_call(
        body, name=name, grid=(T // TM,),
        in_specs=[tile(D), tile(D), tile(2 * FF), _const_spec((1, D)), _const_spec((D, 2 * FF)), _const_spec((FF, D))]
        + r_in,
        out_specs=[tile(D), tile(2 * FF), tile(D), pl.BlockSpec((1, D), lambda i: (0, 0))] + r_out,
        out_shape=[SDS((T, D), F32), SDS((T, 2 * FF), BF16), SDS((T, D), BF16), SDS((1, D), F32)] + r_shapes,
        compiler_params=_cp(("arbitrary",)),
    )(dh, h_in, ab, g, wup, wdn, *r_args)


TOKEN_SPEC = pl.BlockSpec((8, 128), lambda *_: (0, 0))


def _token_operand(token):
    return ([], []) if token is None else ([TOKEN_SPEC], [token])


def matmul_tn(a, b, name, *, tm, tn, tt=1024, b_scale=None, after=None):
    T, M = a.shape
    N = b.shape[1]
    tt = min(tt, T)
    assert M % tm == 0 and N % tn == 0 and T % tt == 0
    nt = T // tt
    token_spec, token_arg = _token_operand(after)

    def body(a_ref, b_ref, *rest):
        o_ref, acc_ref = rest[-2:]
        t = pl.program_id(2)

        @pl.when(t == 0)
        def _():
            acc_ref[...] = jnp.zeros_like(acc_ref)

        bv = b_ref[...]
        if b_scale is not None:
            bv = bv * b_scale
        acc_ref[...] += _dot_tn(a_ref[...].astype(BF16), bv.astype(BF16))

        @pl.when(t == nt - 1)
        def _():
            o_ref[...] = acc_ref[...].astype(BF16)

    return pl.pallas_call(
        body, name=name, grid=(M // tm, N // tn, nt),
        in_specs=[pl.BlockSpec((tt, tm), lambda i, j, t: (t, i)), pl.BlockSpec((tt, tn), lambda i, j, t: (t, j))]
        + token_spec,
        out_specs=pl.BlockSpec((tm, tn), lambda i, j, t: (i, j)),
        out_shape=SDS((M, N), BF16),
        scratch_shapes=[pltpu.VMEM((tm, tn), F32)],
        compiler_params=_cp(("parallel", "parallel", "arbitrary")),
    )(a, b, *token_arg)


def mix_in_fwd(h1, g, win_t):
    T = h1.shape[0]
    TM = 512

    def body(h_ref, g_ref, w_ref, u_ref, q_ref, kv_ref, z_ref, gt_ref):
        x = h_ref[...]
        u = (x * _rstd(x) * g_ref[...]).astype(BF16)
        u_ref[...] = u
        for c in range(0, AW, 256):
            q_ref[:, c:c + 256] = _dot_nt(u, w_ref[c:c + 256, :]).astype(BF16)
        kv_ref[...] = _dot_nt(u, w_ref[C_KV:C_Z, :]).astype(BF16)
        for c in range(0, PW, 256):
            z_ref[:, c:c + 256] = _dot_nt(u, w_ref[C_Z + c:C_Z + c + 256, :])
        for c in range(0, 2 * D, 256):
            gt_ref[:, c:c + 256] = _dot_nt(u, w_ref[C_G + c:C_G + c + 256, :]).astype(BF16)

    tile = lambda w: pl.BlockSpec((TM, w), lambda i: (i, 0))
    return pl.pallas_call(
        body, name="mix_in_fwd", grid=(T // TM,),
        in_specs=[tile(D), _const_spec((1, D)), _const_spec((INW, D))],
        out_specs=[tile(D), tile(AW), tile(2 * KVW), tile(PW), tile(2 * D)],
        out_shape=[SDS((T, D), BF16), SDS((T, AW), BF16), SDS((T, 2 * KVW), BF16), SDS((T, PW), F32),
                   SDS((T, 2 * D), BF16)],
        compiler_params=_cp(("arbitrary",)),
    )(h1, g, win_t)


def mix_in_bwd(dq, dkv, dz, dgt, dh2, h1, g, win_t):
    T = h1.shape[0]
    TM = min(1024, T)
    SUB = 256

    def body(dq_ref, dkv_ref, dz_ref, dgt_ref, dh2_ref, h_ref, g_ref, w_ref, dh1_ref, dg_ref):
        g = g_ref[...]
        dg = None
        for r0 in range(0, TM, SUB):
            rows = slice(r0, r0 + SUB)
            du = _dot(dq_ref[rows, :], w_ref[0:AW, :])
            du += _dot(dkv_ref[rows, :], w_ref[C_KV:C_Z, :])
            du += _dot(dz_ref[rows, :], w_ref[C_Z:C_G, :])
            du += _dot(dgt_ref[rows, :], w_ref[C_G:INW, :])
            x = h_ref[rows, :]
            rstd = _rstd(x)
            xhat = x * rstd
            dh1_ref[rows, :] = dh2_ref[rows, :] + _rms_bwd(du, xhat, rstd, g)
            part = jnp.sum(du * xhat, axis=0, keepdims=True)
            dg = part if dg is None else dg + part

        @pl.when(pl.program_id(0) == 0)
        def _():
            dg_ref[...] = jnp.zeros_like(dg_ref)

        dg_ref[...] += dg

    tile = lambda w: pl.BlockSpec((TM, w), lambda i: (i, 0))
    return pl.pallas_call(
        body, name="mix_in_bwd", grid=(T // TM,),
        in_specs=[tile(AW), tile(2 * KVW), tile(PW), tile(2 * D), tile(D), tile(D), _const_spec((1, D)),
                  _const_spec((INW, D))],
        out_specs=[tile(D), pl.BlockSpec((1, D), lambda i: (0, 0))],
        out_shape=[SDS((T, D), F32), SDS((1, D), F32)],
        compiler_params=_cp(("arbitrary",)),
    )(dq, dkv, dz, dgt, dh2, h1, g, win_t)


PAIR = 2 * HD
NPAIR = GQ // 2


def _lo_lanes():
    return lax.broadcasted_iota(jnp.int32, (BLK, PAIR), 1) < HD


def _stack_heads(ref, kvh, scale=None):
    lo = _lo_lanes()
    parts = []
    for pr in range(NPAIR):
        t = ref[:, (kvh * NPAIR + pr) * PAIR:(kvh * NPAIR + pr + 1) * PAIR]
        if scale is not None:
            t = t * scale
        zero = jnp.zeros_like(t)
        parts += [jnp.where(lo, t, zero), jnp.where(lo, zero, t)]
    return jnp.concatenate(parts, axis=0)


def _kv_tiles(kvc_ref, kvp_ref, tile, kvh):
    lo = _lo_lanes()
    dup, left, right = [], [], []
    for ref in (kvp_ref, kvc_ref):
        t = ref[:, tile * PAIR:(tile + 1) * PAIR]
        r = pltpu.roll(t.astype(F32), HD, 1).astype(BF16)
        zero = jnp.zeros_like(t)
        a, b = (t, r) if kvh == 0 else (r, t)
        dup.append(jnp.where(lo, a, b))
        left.append(jnp.where(lo, a, zero))
        right.append(jnp.where(lo, zero, b))
    cat = lambda xs: jnp.concatenate(xs, axis=0)
    return cat(dup), cat(left), cat(right)


def _band_consts(first):
    row = lax.broadcasted_iota(jnp.int32, (BLK, BLK), 0)
    col = lax.broadcasted_iota(jnp.int32, (BLK, BLK), 1)
    upper = col > row
    dist = jnp.where(upper, row - col + BLK, row - col).astype(F32)
    pen = jnp.where(jnp.logical_and(upper, first), -jnp.inf, 0.0)
    return upper, dist, pen


def _split_band(upper, t):
    zero = jnp.zeros_like(t)
    return jnp.concatenate([jnp.where(upper, t, zero), jnp.where(upper, zero, t)], axis=1)


def attn_fwd(q, kv, sinks):
    T = q.shape[0]
    nb = T // BLK

    def body(sink_ref, q_ref, kvc_ref, kvp_ref, att_ref, lse_ref):
        upper, dist, pen = _band_consts(pl.program_id(0) == 0)
        scores, values = [], []
        for kvh in range(NKV):
            kdup, _, _ = _kv_tiles(kvc_ref, kvp_ref, 0, kvh)
            values.append(_kv_tiles(kvc_ref, kvp_ref, 1, kvh)[1:])
            scores.append(_dot_nt(_stack_heads(q_ref, kvh, SCALE), kdup))
        for kvh in range(NKV):
            s_all = scores[kvh]
            vleft, vright = values[kvh]
            for pr in range(NPAIR):
                outs, inv = [], []
                for side, vpad in ((0, vleft), (1, vright)):
                    g = 2 * pr + side
                    hq = kvh * GQ + g
                    sink = sink_ref[0, hq]
                    rows = slice(g * BLK, (g + 1) * BLK)
                    s = jnp.where(upper, s_all[rows, 0:BLK], s_all[rows, BLK:2 * BLK]) - SLOPES[hq] * dist + pen
                    m = jnp.maximum(jnp.max(s, axis=-1, keepdims=True), sink)
                    p = jnp.exp(s - m)
                    l = jnp.sum(p, axis=-1, keepdims=True) + jnp.exp(sink - m)
                    lse_ref[:, hq:hq + 1] = m + jnp.log(l)
                    outs.append(_dot(_split_band(upper, p.astype(BF16)), vpad))
                    inv.append(1.0 / l)
                col0 = (kvh * NPAIR + pr) * PAIR
                att_ref[:, col0:col0 + PAIR] = ((outs[0] + outs[1]) * jnp.where(_lo_lanes(), inv[0], inv[1])).astype(BF16)

    return pl.pallas_call(
        body, name="attn_fwd", grid=(nb,),
        in_specs=[pl.BlockSpec(memory_space=pltpu.SMEM),
                  pl.BlockSpec((BLK, AW), lambda i: (i, 0)),
                  pl.BlockSpec((BLK, 2 * KVW), lambda i: (i, 0)),
                  pl.BlockSpec((BLK, 2 * KVW), lambda i: (jnp.maximum(i - 1, 0), 0))],
        out_specs=[pl.BlockSpec((BLK, AW), lambda i: (i, 0)), pl.BlockSpec((BLK, NQ), lambda i: (i, 0))],
        out_shape=[SDS((T, AW), BF16), SDS((T, NQ), F32)],
        compiler_params=_cp(("arbitrary",)),
    )(sinks, q, kv, kv)


def attn_bwd(q, kv, datt, att, lse, sinks):
    T = q.shape[0]
    nb = T // BLK

    def body(sink_ref, q_ref, kvc_ref, kvp_ref, do_ref, out_ref, lse_ref, dq_ref, dkv_ref, dsink_ref, carry_ref):
        i = pl.program_id(0)

        @pl.when(i == 0)
        def _():
            dsink_ref[...] = jnp.zeros_like(dsink_ref)
            carry_ref[...] = jnp.zeros_like(carry_ref)

        @pl.when(i < nb)
        def _():
            upper, dist, pen = _band_consts(i == 0)
            lo = _lo_lanes()
            dk_dup, dv_dup = [], []
            staged = []
            for kvh in range(NKV):
                kdup, kleft, kright = _kv_tiles(kvc_ref, kvp_ref, 0, kvh)
                vdup, _, _ = _kv_tiles(kvc_ref, kvp_ref, 1, kvh)
                qs = _stack_heads(q_ref, kvh, SCALE)
                dos = _stack_heads(do_ref, kvh)
                staged.append((kleft, kright, qs, dos, _dot_nt(qs, kdup), _dot_nt(dos, vdup)))
            deltas = []
            for pair in range(NQ // 2):
                cols = slice(pair * PAIR, (pair + 1) * PAIR)
                t = do_ref[:, cols].astype(F32) * out_ref[:, cols].astype(F32)
                deltas += [jnp.sum(jnp.where(lo, t, 0.0), axis=-1, keepdims=True),
                           jnp.sum(jnp.where(lo, 0.0, t), axis=-1, keepdims=True)]
            for kvh in range(NKV):
                kleft, kright, qs, dos, s_all, dp_all = staged[kvh]
                ds_parts, p_parts = [], []
                for pr in range(NPAIR):
                    dq = None
                    for side, kpad in ((0, kleft), (1, kright)):
                        g = 2 * pr + side
                        hq = kvh * GQ + g
                        lse_h = lse_ref[:, hq:hq + 1]
                        rows = slice(g * BLK, (g + 1) * BLK)
                        s = jnp.where(upper, s_all[rows, 0:BLK], s_all[rows, BLK:2 * BLK]) - SLOPES[hq] * dist + pen
                        p = jnp.exp(s - lse_h)
                        dp = jnp.where(upper, dp_all[rows, 0:BLK], dp_all[rows, BLK:2 * BLK])
                        delta = deltas[hq]
                        dsink_ref[:, hq:hq + 1] += -jnp.sum(jnp.exp(sink_ref[0, hq] - lse_h) * delta, axis=0,
                                                            keepdims=True)
                        ds = _split_band(upper, (p * (dp - delta)).astype(BF16))
                        ds_parts.append(ds)
                        p_parts.append(_split_band(upper, p.astype(BF16)))
                        d = _dot(ds, kpad)
                        dq = d if dq is None else dq + d
                    col0 = (kvh * NPAIR + pr) * PAIR
                    dq_ref[:, col0:col0 + PAIR] = (dq * SCALE).astype(BF16)
                dkw = _dot_tn(qs, jnp.concatenate(ds_parts, axis=0)).T
                dvw = _dot_tn(dos, jnp.concatenate(p_parts, axis=0)).T
                dk_dup.append(dkw + pltpu.roll(dkw, HD, 1))
                dv_dup.append(dvw + pltpu.roll(dvw, HD, 1))
            dk = jnp.where(jnp.concatenate([lo, lo], axis=0), dk_dup[0], dk_dup[1])
            dv = jnp.where(jnp.concatenate([lo, lo], axis=0), dv_dup[0], dv_dup[1])
            dkv_ref[:, 0:PAIR] = (carry_ref[:, 0:PAIR] + dk[0:BLK]).astype(BF16)
            dkv_ref[:, PAIR:2 * PAIR] = (carry_ref[:, PAIR:2 * PAIR] + dv[0:BLK]).astype(BF16)
            carry_ref[:, 0:PAIR] = dk[BLK:2 * BLK]
            carry_ref[:, PAIR:2 * PAIR] = dv[BLK:2 * BLK]

        @pl.when(i == nb)
        def _():
            dkv_ref[...] = carry_ref[...].astype(BF16)

    cur = lambda i: (jnp.minimum(i, nb - 1), 0)
    prev = lambda i: (jnp.maximum(jnp.minimum(i, nb - 1) - 1, 0), 0)
    return pl.pallas_call(
        body, name="attn_bwd", grid=(nb + 1,),
        in_specs=[pl.BlockSpec(memory_space=pltpu.SMEM),
                  pl.BlockSpec((BLK, AW), cur), pl.BlockSpec((BLK, 2 * KVW), cur), pl.BlockSpec((BLK, 2 * KVW), prev),
                  pl.BlockSpec((BLK, AW), cur), pl.BlockSpec((BLK, AW), cur), pl.BlockSpec((BLK, NQ), cur)],
        out_specs=[pl.BlockSpec((BLK, AW), cur),
                   pl.BlockSpec((BLK, 2 * KVW), lambda i: (jnp.maximum(i - 1, 0), 0)),
                   pl.BlockSpec((1, NQ), lambda i: (0, 0))],
        out_shape=[SDS((T, AW), BF16), SDS((T, 2 * KVW), BF16), SDS((1, NQ), F32)],
        scratch_shapes=[pltpu.VMEM((BLK, 2 * KVW), F32)],
        compiler_params=_cp(("arbitrary",)),
    )(sinks, q, kv, kv, datt, att, lse)


def _inv_counts(t0, rows):
    t = (t0 + lax.broadcasted_iota(jnp.int32, (rows, 1), 0) + 1).astype(F32)
    return [1.0 / jnp.minimum(t, float(w)) for w in POOL_WINDOWS]


def pool_fwd(z, wmix, scale):
    T = z.shape[0]
    TM = min(1024, T)
    L = TM + HALO

    def body(z_ref, halo_ref, wmix_ref, scale_ref, pooled_ref, mixs_ref):
        i = pl.program_id(0)
        halo = jnp.where(i > 0, halo_ref[...], 0.0)
        zt = z_ref[...]
        e = jnp.concatenate([halo, zt], axis=0)
        sums = []
        s = e
        for k in (1, 2, 4, 8):
            s = s + pltpu.roll(s, k, 0)
            sums.append(s)
        inv = _inv_counts(i * TM, TM)
        for gi in range(len(POOL_WINDOWS)):
            cols = slice(gi * PG, (gi + 1) * PG)
            pooled = (sums[gi][HALO:, cols] * inv[gi] - zt[:, cols]).astype(BF16)
            pooled_ref[:, cols] = pooled
            mixs_ref[:, cols] = (_dot(pooled, wmix_ref[gi]) * scale_ref[:, cols]).astype(BF16)

    return pl.pallas_call(
        body, name="pool_fwd", grid=(T // TM,),
        in_specs=[pl.BlockSpec((TM, PW), lambda i: (i, 0)),
                  pl.BlockSpec((HALO, PW), lambda i: (jnp.maximum(i * (TM // HALO) - 1, 0), 0)),
                  _const_spec((len(POOL_WINDOWS), PG, PG)), _const_spec((1, PW))],
        out_specs=[pl.BlockSpec((TM, PW), lambda i: (i, 0)), pl.BlockSpec((TM, PW), lambda i: (i, 0))],
        out_shape=[SDS((T, PW), BF16), SDS((T, PW), BF16)],
        compiler_params=_cp(("arbitrary",)),
    )(z, z, wmix, scale)


def pool_bwd(dmixs, pooled, wmix, scale):
    T = dmixs.shape[0]
    TM = min(1024, T)
    L = TM + HALO
    nt = T // TM

    def body(dm_ref, halo_ref, pooled_ref, wmix_ref, scale_ref, dz_ref, dwmix_ref, dscale_ref):
        i = pl.program_id(0)

        @pl.when(i == 0)
        def _():
            dwmix_ref[...] = jnp.zeros_like(dwmix_ref)
            dscale_ref[...] = jnp.zeros_like(dscale_ref)

        halo = jnp.where(i < nt - 1, halo_ref[...], 0.0)
        dm = dm_ref[...]
        e = jnp.concatenate([dm, halo], axis=0)
        inv = _inv_counts(i * TM, L)
        for gi in range(len(POOL_WINDOWS)):
            cols = slice(gi * PG, (gi + 1) * PG)
            w = wmix_ref[gi]
            dmixed = (e[:, cols] * scale_ref[:, cols]).astype(BF16)
            dpooled = _dot_nt(dmixed, w)
            pooled = pooled_ref[:, cols]
            mixed = _dot(pooled, w)
            dscale_ref[:, cols] += jnp.sum(dm[:, cols] * mixed, axis=0, keepdims=True)
            dwmix_ref[gi] += _dot_tn(pooled, dmixed[:TM])
            s = dpooled * inv[gi]
            k = 1
            while k < POOL_WINDOWS[gi]:
                s = s + pltpu.roll(s, L - k, 0)
                k *= 2
            dz_ref[:, cols] = (s[:TM] - dpooled[:TM]).astype(BF16)

    return pl.pallas_call(
        body, name="pool_bwd", grid=(nt,),
        in_specs=[pl.BlockSpec((TM, PW), lambda i: (i, 0)),
                  pl.BlockSpec((HALO, PW), lambda i: (jnp.minimum((i + 1) * (TM // HALO), T // HALO - 1), 0)),
                  pl.BlockSpec((TM, PW), lambda i: (i, 0)),
                  _const_spec((len(POOL_WINDOWS), PG, PG)), _const_spec((1, PW))],
        out_specs=[pl.BlockSpec((TM, PW), lambda i: (i, 0)),
                   pl.BlockSpec((len(POOL_WINDOWS), PG, PG), lambda i: (0, 0, 0)),
                   pl.BlockSpec((1, PW), lambda i: (0, 0))],
        out_shape=[SDS((T, PW), BF16), SDS((len(POOL_WINDOWS), PG, PG), F32), SDS((1, PW), F32)],
        compiler_params=_cp(("arbitrary",)),
    )(dmixs, dmixs, pooled, wmix, scale)


def merge_fwd(att, mixs, gt, h1, wattn, wpool, wout):
    T = h1.shape[0]
    TM = 512

    def body(att_ref, mixs_ref, gt_ref, h_ref, wa_ref, wp_ref, wo_ref, h2_ref, mg_ref, gf_ref):
        a = _dot(att_ref[...], wa_ref[...])
        p = _dot(mixs_ref[...], wp_ref[...])
        sa = jax.nn.sigmoid(gt_ref[:, 0:D].astype(F32))
        sp = jax.nn.sigmoid(gt_ref[:, D:2 * D].astype(F32))
        gf_ref[:, 0:D] = (a * sa * (1.0 - sa)).astype(BF16)
        gf_ref[:, D:2 * D] = (p * sp * (1.0 - sp)).astype(BF16)
        mg = (sa * a + sp * p).astype(BF16)
        mg_ref[...] = mg
        h2_ref[...] = h_ref[...] + _dot(mg, wo_ref[...])

    tile = lambda w: pl.BlockSpec((TM, w), lambda i: (i, 0))
    return pl.pallas_call(
        body, name="merge_fwd", grid=(T // TM,),
        in_specs=[tile(AW), tile(PW), tile(2 * D), tile(D), _const_spec((AW, D)), _const_spec((PW, D)),
                  _const_spec((D, D))],
        out_specs=[tile(D), tile(D), tile(2 * D)],
        out_shape=[SDS((T, D), F32), SDS((T, D), BF16), SDS((T, 2 * D), BF16)],
        compiler_params=_cp(("arbitrary",)),
    )(att, mixs, gt, h1, wattn, wpool, wout)


def merge_bwd(dh2, gt, gf, wattn, wpool, wout, after=None):
    T = dh2.shape[0]
    TM = 512
    token_spec, token_arg = _token_operand(after)

    def body(dh2_ref, gt_ref, gf_ref, wa_ref, wp_ref, wo_ref, *rest):
        datt_ref, dmixs_ref, dgt_ref, da_ref, dp_ref = rest[-5:]
        dm = _dot_nt(dh2_ref[...].astype(BF16), wo_ref[...])
        da = (dm * jax.nn.sigmoid(gt_ref[:, 0:D].astype(F32))).astype(BF16)
        dp = (dm * jax.nn.sigmoid(gt_ref[:, D:2 * D].astype(F32))).astype(BF16)
        da_ref[...] = da
        dp_ref[...] = dp
        dgt_ref[:, 0:D] = (dm * gf_ref[:, 0:D].astype(F32)).astype(BF16)
        dgt_ref[:, D:2 * D] = (dm * gf_ref[:, D:2 * D].astype(F32)).astype(BF16)
        datt_ref[...] = _dot_nt(da, wa_ref[...]).astype(BF16)
        dmixs_ref[...] = _dot_nt(dp, wp_ref[...])

    tile = lambda w: pl.BlockSpec((TM, w), lambda i: (i, 0))
    return pl.pallas_call(
        body, name="merge_bwd", grid=(T // TM,),
        in_specs=[tile(D), tile(2 * D), tile(2 * D), _const_spec((AW, D)), _const_spec((PW, D)),
                  _const_spec((D, D))] + token_spec,
        out_specs=[tile(AW), tile(PW), tile(2 * D), tile(D), tile(D)],
        out_shape=[SDS((T, AW), BF16), SDS((T, PW), F32), SDS((T, 2 * D), BF16), SDS((T, D), BF16),
                   SDS((T, D), BF16)],
        compiler_params=_cp(("arbitrary",)),
    )(dh2, gt, gf, wattn, wpool, wout, *token_arg)


def adamw(updates, place, name, after=None):
    tile_bytes = 2 * 1024 * 1024 // len(updates)
    jobs = []
    for w, g, m, v in updates:
        R, C = w.shape
        tr = R
        if R * C * 4 > tile_bytes:
            tr = next(cand for cand in (512, 256, 128, 64, 32, 16, 8) if R % cand == 0 and cand * C * 4 <= tile_bytes)
        blk = ((tr, C), lambda ls, p: (ls, 0))
        jobs.append((R // tr, [(a, *blk) for a in (w, g, m, v)], [((R, C), F32, *blk)] * 4,
                     lambda ins, outs: _adamw_tile(*ins, *outs)))
    return _run_jobs(jobs, place, name, after)


GROUP_UP1, GROUP_MIX, GROUP_FFN2 = 0, 1, 2


def _behind(small, token):
    return small if token is None else small + token[0:1, 0:1]


def local_fwd_bwd(x, target, S, comm):
    w_up1 = comm.weights(GROUP_UP1, None)["ffn1_w_up"]
    ab1, hid1 = ffn_up(x, _behind(S["ffn1_norm"], comm.started()), w_up1, "ffn1_up")
    W = comm.weights(GROUP_MIX, hid1)
    h1 = ffn_down(x, hid1, W["ffn1_w_down"], "ffn1_down", after=comm.started())
    u, q, kv, z, gt = mix_in_fwd(h1, S["mix_norm"], W["w_in"])
    att, lse = attn_fwd(q, kv, S["sinks"])
    pooled, mixs = pool_fwd(z, S["pool_w_mix"], _behind(S["pool_scale"], comm.prefetch(GROUP_FFN2, att)))
    h2, merged, gf = merge_fwd(att, mixs, gt, h1, W["w_attn_up"], W["w_pool_up"], W["w_out"])
    W = comm.weights(GROUP_FFN2, h2)
    dh3, ab2, hid2, loss, g_final = ffn_fwd(h2, S["ffn2_norm"], W["ffn2_w_up"], W["ffn2_w_down"], "ffn2_fwd",
                                            head=(target, S["final_norm"]))

    G = {"final_norm": g_final}
    dh2, dup2, n2, G["ffn2_norm"] = ffn_bwd_x(dh3, h2, ab2, S["ffn2_norm"], W["ffn2_w_up"], W["ffn2_w_down"],
                                              "ffn2_bwd_x")
    ffn2 = ("ffn2_w_down", "ffn2_w_up")
    token = comm.grads({"ffn2_w_down": matmul_tn(hid2, dh3, "ffn2_dw_down", tm=1408, tn=512, tt=4096, b_scale=0.5),
                        "ffn2_w_up": matmul_tn(n2, dup2, "ffn2_dw_up", tm=D, tn=512, tt=4096)})

    W = comm.weights(GROUP_MIX, None)
    datt, dmixs, dgt, da, dp = merge_bwd(dh2, gt, gf, W["w_attn_up"], W["w_pool_up"], W["w_out"], after=token)
    token = comm.advance(ffn2, datt)
    g_mix = {"w_out": matmul_tn(merged, dh2, "dw_out", tm=D, tn=D, after=token),
             "w_attn_up": matmul_tn(att, da, "dw_attn_up", tm=AW, tn=D),
             "w_pool_up": matmul_tn(mixs, dp, "dw_pool_up", tm=PW, tn=D)}
    dz, G["pool_w_mix"], G["pool_scale"] = pool_bwd(dmixs, pooled, S["pool_w_mix"], _behind(S["pool_scale"], token))
    dq, dkv, G["sinks"] = attn_bwd(q, kv, datt, att, lse, _behind(S["sinks"], token))
    dh1, G["mix_norm"] = mix_in_bwd(dq, dkv, dz, dgt, dh2, h1, S["mix_norm"], W["w_in"])
    g_mix["w_in"] = jnp.concatenate([
        matmul_tn(dq, u, "dw_in_q", tm=AW, tn=D),
        matmul_tn(dkv, u, "dw_in_kv", tm=2 * KVW, tn=D),
        matmul_tn(dz, u, "dw_in_z", tm=PW, tn=D),
        matmul_tn(dgt, u, "dw_in_g", tm=D, tn=512, tt=4096),
    ], axis=0)
    mix = tuple(g_mix)
    token = comm.grads(g_mix)

    g_dn1 = matmul_tn(hid1, dh1, "ffn1_dw_down", tm=1408, tn=512, tt=4096, b_scale=0.5, after=token)
    token = comm.advance(mix, g_dn1)
    token2 = comm.grads({"ffn1_w_down": g_dn1})
    g1 = _behind(_behind(S["ffn1_norm"], token), token2)
    dx, dup1, n1, G["ffn1_norm"], *outs = ffn_bwd_x(dh1, x, ab1, g1, w_up1, W["ffn1_w_down"], "ffn1_bwd_x",
                                                   updates=comm.reduce(ffn2, token2))
    comm.updated(ffn2, outs)
    token = comm.advance(("ffn1_w_down",), dx)
    token2 = comm.small_start(G, loss, dx)
    g_up1 = matmul_tn(n1, dup1, "ffn1_dw_up", tm=D, tn=512, tt=4096,
                      after=None if token is None else token + token2)
    token = comm.grads({"ffn1_w_up": g_up1})
    updates = comm.reduce(("ffn1_w_down",), token)
    token = comm.advance(("ffn1_w_up",), [u[1] for u in updates])
    done = comm.update(("ffn1_w_down",), updates, token)
    done = comm.finish(mix, done)
    done = comm.small_finish(done)
    comm.finish(("ffn1_w_up",), done)
    return loss, dx


HBM_SPEC = pl.BlockSpec(memory_space=pltpu.HBM)


def _me():
    return lax.axis_index("x"), lax.axis_index("y"), lax.axis_index("c")


def _peer_chip(x, y, k):
    return x ^ (k >> 1), y ^ (k & 1)


def _piece_half(ref, rowlike, j, h):
    if rowlike:
        return ref.at[j, h]
    ns = ref.shape[-1] // N_CHIPS
    return ref.at[h, :, pl.ds(pl.multiple_of(j * ns, 128), ns)]


def _piece(ref, rowlike, j):
    if rowlike:
        return ref.at[j]
    ns = ref.shape[-1] // N_CHIPS
    return ref.at[:, :, pl.ds(pl.multiple_of(j * ns, 128), ns)]


def _full_shape(shard_view, rowlike):
    _, kh, ns = shard_view.shape
    return (N_CHIPS, 2, kh, ns) if rowlike else (2, kh, N_CHIPS * ns)


def _remote(src, dst, send_sem, recv_sem, dev):
    return pltpu.make_async_remote_copy(src, dst, send_sem, recv_sem, device_id=dev, device_id_type=MESH)


SEM_SPEC = pl.BlockSpec(memory_space=pltpu.SEMAPHORE)
SPLIT_PARAMS = pltpu.CompilerParams(has_side_effects=pltpu.SideEffectType.DATAFLOW_SIDE_EFFECTING)


def _gather_plan(rowlikes):
    def plan(s_refs, f_refs, a, k, x, y, c):
        px, py = _peer_chip(x, y, k)
        return (s_refs[a].at[c], _piece_half(f_refs[a], rowlikes[a], 2 * x + y, c),
                _piece_half(f_refs[a], rowlikes[a], 2 * px + py, c), (px, py, c))
    return plan


def _all_to_all_plan(rowlikes):
    def plan(q_refs, r_refs, a, k, x, y, c):
        px, py = _peer_chip(x, y, k)
        if rowlikes[a]:
            src = q_refs[a].at[2 * px + py]
        else:
            ns = q_refs[a].shape[-1] // N_CHIPS
            src = q_refs[a].at[:, pl.ds(pl.multiple_of((2 * px + py) * ns, 128), ns)]
        return src, r_refs[a].at[k - 1], r_refs[a].at[k - 1], (px, py, c)
    return plan


def _swap_plan(rowlikes):
    def plan(g_refs, got_refs, a, k, x, y, c):
        src = g_refs[a].at[:, 1 - c] if rowlikes[a] else g_refs[a].at[1 - c]
        return src, got_refs[a], got_refs[a], (x, y, 1 - c)
    return plan


def _everyone_plan(s_refs, slot_refs, a, k, x, y, c):
    px, py, pc = x ^ (k >> 2), y ^ ((k >> 1) & 1), c ^ (k & 1)
    return s_refs[a], slot_refs[a].at[4 * x + 2 * y + c], slot_refs[a].at[4 * px + 2 * py + pc], (px, py, pc)


def _forward_plan(rowlikes):
    def plan(s_refs, f_refs, a, k, x, y, c):
        sib = (x, y, 1 - c)
        if k == 0:
            own = _piece(f_refs[a], rowlikes[a], 2 * x + y)
            return s_refs[a], own, own, sib
        px, py = _peer_chip(x, y, k)
        mine = _piece_half(f_refs[a], rowlikes[a], 2 * px + py, c)
        return mine, mine, _piece_half(f_refs[a], rowlikes[a], 2 * px + py, 1 - c), sib
    return plan


CHIPS, SIBLING, EVERYONE, FORWARDS = (1, 2, 3), (1,), tuple(range(1, 8)), (0, 1, 2, 3)


def _as_list(after):
    return [] if after is None else list(after) if isinstance(after, (list, tuple)) else [after]


def exchange_start(srcs, land_shapes, plan, after, name, peers=CHIPS):
    n = len(srcs)
    lands = [l if hasattr(l, "dtype") else lax.empty(l, s.dtype) for l, s in zip(land_shapes, srcs)]

    behind = _as_list(after)

    def body(*refs):
        s_refs, l_refs = refs[:n], refs[n:2 * n]
        send_sems, recv_sems = refs[2 * n + len(behind)], refs[2 * n + len(behind) + 1]
        token = refs[-1]
        x, y, c = _me()
        for a in range(n):
            for i, k in enumerate(peers):
                src, dst, _, peer = plan(s_refs, l_refs, a, k, x, y, c)
                sem = len(peers) * a + i
                _remote(src, dst, send_sems.at[sem], recv_sems.at[sem], peer).start()
        token[...] = jnp.zeros_like(token)

    n_sems = len(peers) * n
    outs = pl.pallas_call(
        body, name=name, in_specs=[HBM_SPEC] * (2 * n) + [pl.BlockSpec(memory_space=pl.ANY)] * len(behind),
        out_specs=[SEM_SPEC, SEM_SPEC] + [HBM_SPEC] * (2 * n) + [pl.BlockSpec(memory_space=pltpu.VMEM)],
        out_shape=[pltpu.SemaphoreType.DMA((n_sems,)), pltpu.SemaphoreType.DMA((n_sems,))]
        + [pltpu.HBM(a.shape, a.dtype) for a in (*srcs, *lands)] + [SDS((8, 128), F32)],
        input_output_aliases={i: 2 + i for i in range(2 * n)},
        compiler_params=SPLIT_PARAMS,
    )(*[pltpu.with_memory_space_constraint(a, pltpu.HBM) for a in (*srcs, *lands)], *behind)
    return {"sems": outs[:2], "srcs": outs[2:2 + n], "lands": outs[2 + n:2 + 2 * n], "token": outs[-1]}


def exchange_wait(state, plan, after, name, peers=CHIPS):
    n = len(state["srcs"])
    behind = _as_list(after)

    def body(*refs):
        s_refs, l_refs = refs[:n], refs[n:2 * n]
        send_sems, recv_sems = refs[2 * n], refs[2 * n + 1]
        x, y, c = _me()
        for a in range(n):
            for i, k in enumerate(peers):
                src, _, landing, peer = plan(s_refs, l_refs, a, k, x, y, c)
                sem = len(peers) * a + i
                cp = _remote(src, landing, send_sems.at[sem], recv_sems.at[sem], peer)
                cp.wait_send()
                cp.wait_recv()

    bufs = (*state["srcs"], *state["lands"])
    outs = pl.pallas_call(
        body, name=name,
        in_specs=[HBM_SPEC] * (2 * n) + [SEM_SPEC, SEM_SPEC] + [pl.BlockSpec(memory_space=pl.ANY)] * len(behind),
        out_specs=[HBM_SPEC] * (2 * n),
        out_shape=[pltpu.HBM(a.shape, a.dtype) for a in bufs],
        input_output_aliases={i: i for i in range(2 * n)},
        compiler_params=SPLIT_PARAMS,
    )(*bufs, *state["sems"], *behind)
    return outs[:n], outs[n:]


def gather_finish(shards, fulls, rowlikes, name):
    n = len(shards)

    def body(*refs):
        s_refs, f_refs = refs[:n], refs[2 * n:3 * n]
        send_sems, recv_sems = refs[3 * n:]
        x, y, c = _me()
        chip = 2 * x + y
        sib = (x, y, 1 - c)
        sends = []
        for a in range(n):
            own = _piece(f_refs[a], rowlikes[a], chip)
            cp = _remote(s_refs[a], own, send_sems.at[a, 0], recv_sems.at[a, 0], sib)
            cp.start()
            sends.append(cp)
            for k in (1, 2, 3):
                px, py = _peer_chip(x, y, k)
                slot = _piece_half(f_refs[a], rowlikes[a], 2 * px + py, c)
                cp = _remote(slot, slot, send_sems.at[a, k], recv_sems.at[a, k], sib)
                cp.start()
                sends.append(cp)
        for a in range(n):
            own = _piece(f_refs[a], rowlikes[a], chip)
            _remote(own, own, send_sems.at[a, 0], recv_sems.at[a, 0], sib).wait_recv()
            for k in (1, 2, 3):
                px, py = _peer_chip(x, y, k)
                slot = _piece_half(f_refs[a], rowlikes[a], 2 * px + py, 1 - c)
                _remote(slot, slot, send_sems.at[a, k], recv_sems.at[a, k], sib).wait_recv()
        for cp in sends:
            cp.wait_send()

    return pl.pallas_call(
        body, name=name, in_specs=[HBM_SPEC] * (2 * n), out_specs=[HBM_SPEC] * n,
        out_shape=[SDS(f.shape, f.dtype) for f in fulls],
        input_output_aliases={n + a: a for a in range(n)},
        scratch_shapes=[pltpu.SemaphoreType.DMA((n, 4)), pltpu.SemaphoreType.DMA((n, 4))],
    )(*shards, *fulls)


def _half_buffer_shape(gview, rowlike):
    return (N_CHIPS,) + gview.shape[2:] if rowlike else gview.shape[1:]


def _row_tiles(kh, ns):
    return 1 if kh * ns <= 256 * 1024 else 2


def _run_jobs(jobs, place, name, after=None):
    n_steps = max(job[0] for job in jobs)

    def spec(block, index, steps):
        return pl.BlockSpec(block, lambda s, p: index(jnp.minimum(s, steps - 1), p))

    in_specs = [spec(b, ix, steps) for steps, ins, _, _ in jobs for _, b, ix in ins]
    out_specs = [spec(b, ix, steps) for steps, _, outs, _ in jobs for _, _, b, ix in outs]
    token_spec, token_arg = _token_operand(after)
    n_in = len(in_specs) + len(token_spec)

    def body(place_ref, *refs):
        s = pl.program_id(0)
        i, o = 0, n_in
        for steps, ins, outs, fn in jobs:
            in_refs, out_refs = refs[i:i + len(ins)], refs[o:o + len(outs)]
            i, o = i + len(ins), o + len(outs)

            @pl.when(s < steps)
            def _(fn=fn, in_refs=in_refs, out_refs=out_refs):
                fn(in_refs, out_refs)

    return pl.pallas_call(
        body, name=name,
        grid_spec=pltpu.PrefetchScalarGridSpec(num_scalar_prefetch=1, grid=(n_steps,),
                                               in_specs=in_specs + token_spec, out_specs=out_specs),
        out_shape=[SDS(shape, dtype) for _, _, outs, _ in jobs for shape, dtype, _, _ in outs],
        compiler_params=_cp(("arbitrary",)),
    )(place, *[a for _, ins, _, _ in jobs for a, _, _ in ins], *token_arg)


def add_halves(gviews, gots, rowlikes, place, name):
    def add(ins, outs):
        outs[0][...] = (ins[0][...].astype(F32) + ins[1][...].astype(F32)).astype(BF16)

    jobs = []
    for g, got, rowlike in zip(gviews, gots, rowlikes):
        kh, ns = (g.shape[2], g.shape[3]) if rowlike else (g.shape[1], g.shape[2] // N_CHIPS)
        rt = _row_tiles(kh, ns)
        tr = kh // rt
        if rowlike:
            g_in = (g, (None, None, tr, ns), lambda ls, p, rt=rt: (ls // rt, p[0], ls % rt, 0))
            half = ((None, tr, ns), lambda ls, p, rt=rt: (ls // rt, ls % rt, 0))
        else:
            g_in = (g, (None, tr, ns), lambda ls, p, rt=rt: (p[0], ls % rt, ls // rt))
            half = ((tr, ns), lambda ls, p, rt=rt: (ls % rt, ls // rt))
        jobs.append((N_CHIPS * rt, [g_in, (got, *half)], [(got.shape, BF16, *half)], add))
    return _run_jobs(jobs, place, name)


def _slot_shape(q, rowlike):
    return (3,) + (q.shape[1:] if rowlike else (q.shape[0], q.shape[1] // N_CHIPS))


def sum_pieces(qs, recvs, rowlikes, place, name):
    def total(ins, outs):
        acc = ins[0][...].astype(F32)
        for k in range(3):
            acc = acc + ins[1][k].astype(F32)
        outs[0][...] = acc

    jobs = []
    for q, recv, rowlike in zip(qs, recvs, rowlikes):
        kh, ns = recv.shape[1], recv.shape[2]
        rt = _row_tiles(kh, ns)
        tr = kh // rt
        mine = ((None, tr, ns), lambda ls, p: (p[1], ls, 0)) if rowlike else ((tr, ns), lambda ls, p: (ls, p[1]))
        jobs.append((rt, [(q, *mine), (recv, (3, tr, ns), lambda ls, p: (0, ls, 0))],
                     [((2, kh, ns), F32, (None, tr, ns), lambda ls, p: (p[0], ls, 0))], total))
    return _run_jobs(jobs, place, name)


def join_halves(halves, name):
    n = len(halves)

    def body(*refs):
        o_refs = refs[n:2 * n]
        send_sems, recv_sems = refs[2 * n:]
        x, y, c = _me()
        sib = (x, y, 1 - c)
        sends = []
        for a in range(n):
            cp = _remote(o_refs[a].at[c], o_refs[a].at[c], send_sems.at[a], recv_sems.at[a], sib)
            cp.start()
            sends.append(cp)
        for a in range(n):
            got = o_refs[a].at[1 - c]
            _remote(got, got, send_sems.at[a], recv_sems.at[a], sib).wait_recv()
        for cp in sends:
            cp.wait_send()

    return pl.pallas_call(
        body, name=name, in_specs=[HBM_SPEC] * n, out_specs=[HBM_SPEC] * n,
        out_shape=[SDS(h.shape, h.dtype) for h in halves],
        input_output_aliases={a: a for a in range(n)},
        scratch_shapes=[pltpu.SemaphoreType.DMA((n,)), pltpu.SemaphoreType.DMA((n,))],
    )(*halves)


N_DEV = 8


def sum_devices(s, slots, me):
    R, Wd = s.shape

    def body(me_ref, s_ref, slots_ref, out_ref):
        acc = None
        for d in range(N_DEV):
            mine = me_ref[0] == d
            term = jnp.where(mine, s_ref[...], slots_ref[jnp.where(mine, d ^ 1, d)])
            acc = term if acc is None else acc + term
        out_ref[...] = acc

    vmem = pl.BlockSpec(memory_space=pltpu.VMEM)
    return pl.pallas_call(
        body, name="sum_devices", in_specs=[pl.BlockSpec(memory_space=pltpu.SMEM), vmem, vmem], out_specs=vmem,
        out_shape=SDS((R, Wd), F32),
    )(me, s, slots)


TRANSPOSED = ("w_in",)
BIG = {"ffn1_w_up": (D, 2 * FF, "col"), "ffn1_w_down": (FF, D, "row"), "w_in": (INW, D, "row"),
       "w_attn_up": (AW, D, "row"), "w_pool_up": (PW, D, "col"), "w_out": (D, D, "row"),
       "ffn2_w_up": (D, 2 * FF, "col"), "ffn2_w_down": (FF, D, "row")}
GROUPS = (("ffn1_w_up",), ("ffn1_w_down", "w_in", "w_attn_up", "w_pool_up", "w_out"), ("ffn2_w_up", "ffn2_w_down"))
SMALL = ("ffn1_norm", "mix_norm", "ffn2_norm", "final_norm", "pool_scale", "sinks", "pool_w_mix")
SMALL_W = 128


def _rowlike(name):
    return BIG[name][2] == "row"


def _half_dims(name):
    k, n, kind = BIG[name]
    return (k // N_CHIPS // 2, n) if kind == "row" else (k // 2, n // N_CHIPS)


def shard_view(name, shard):
    return shard.reshape((2,) + _half_dims(name))


def full_from_view(name, fv):
    k, n, _ = BIG[name]
    return fv.reshape(k, n)


def grad_view(name, g):
    kh, ns = _half_dims(name)
    return g.reshape(_full_shape(jax.ShapeDtypeStruct((2, kh, ns), g.dtype), _rowlike(name)))


def pack_small(d):
    parts = []
    for name in SMALL:
        a = d[name].reshape(-1)
        pad = (-a.shape[0]) % SMALL_W
        parts.append(jnp.pad(a, (0, pad)).reshape(-1, SMALL_W))
    a = jnp.concatenate(parts, axis=0)
    return jnp.pad(a, ((0, (-a.shape[0]) % 8), (0, 0)))


def unpack_small(a, like):
    out, r0 = {}, 0
    for name in SMALL:
        size = int(np.prod(like[name].shape))
        rows = -(-size // SMALL_W)
        out[name] = a[r0:r0 + rows].reshape(-1)[:size].reshape(like[name].shape)
        r0 += rows
    return out


WEIGHTS = ("ffn1_norm", "ffn1_w_up", "ffn1_w_down", "mix_norm", "w_in", "sinks", "w_attn_up", "pool_w_mix",
           "pool_scale", "w_pool_up", "w_out", "ffn2_norm", "ffn2_w_up", "ffn2_w_down", "final_norm")


def kernel(x, ffn1_norm, ffn1_w_up, ffn1_w_down, mix_norm, w_in, sinks, w_attn_up, pool_w_mix, pool_scale, w_pool_up, w_out, ffn2_norm, ffn2_w_up, ffn2_w_down, final_norm, loss_target, m_ffn1_norm, m_ffn1_w_up, m_ffn1_w_down, m_mix_norm, m_w_in, m_sinks, m_w_attn_up, m_pool_w_mix, m_pool_scale, m_w_pool_up, m_w_out, m_ffn2_norm, m_ffn2_w_up, m_ffn2_w_down, m_final_norm, v_ffn1_norm, v_ffn1_w_up, v_ffn1_w_down, v_mix_norm, v_w_in, v_sinks, v_w_attn_up, v_pool_w_mix, v_pool_scale, v_w_pool_up, v_w_out, v_ffn2_norm, v_ffn2_w_up, v_ffn2_w_down, v_final_norm):
    given = dict(locals())
    w = {n: given[n] for n in WEIGHTS}
    m = {n: given["m_" + n] for n in WEIGHTS}
    v = {n: given["v_" + n] for n in WEIGHTS}
    cx, cy, cc = _me()
    place = jnp.stack([cc, 2 * cx + cy]).astype(jnp.int32)

    def local2d(d, n):
        return d[n][0].T if n in TRANSPOSED else d[n][0]

    shards = {n: local2d(w, n) for n in BIG}
    grads, delta, new_m, new_v = {}, {}, {}, {}
    rowlikes = [[_rowlike(n) for n in names] for names in GROUPS]

    class Exchanges:
        def __init__(self):
            self.gathers, self.forwards, self.fulls, self.reductions, self.small = {}, {}, {}, {}, None

        def _behind_first(self, a):
            return a + self.gathers[0]["token"][0, 0]

        def _start_gather(self, group, after):
            prepare = self._behind_first if group else (lambda a: a)
            sv = [shard_view(n, prepare(shards[n]).astype(BF16)) for n in GROUPS[group]]
            rl = rowlikes[group]
            self.gathers[group] = exchange_start(sv, [_full_shape(s, r) for s, r in zip(sv, rl)], _gather_plan(rl),
                                                 after, f"gather_start_{group}")

        def weights(self, group, after):
            if not self.gathers:
                self._start_gather(0, None)
                self.packed = [self._behind_first(pack_small(d)) for d in (w, m, v)]
                after = [self.gathers[0]["token"], *self.packed]
            if group not in self.fulls:
                names, rl, state = GROUPS[group], rowlikes[group], self.gathers[group]
                if group in self.forwards:
                    _, fulls = exchange_wait(self.forwards.pop(group), _forward_plan(rl), after,
                                             f"forward_wait_{group}", FORWARDS)
                else:
                    sv, fulls = exchange_wait(state, _gather_plan(rl), state["token"] if after is None else after,
                                              f"gather_wait_{group}")
                    fulls = gather_finish(sv, fulls, rl, f"gather_finish_{group}")
                self.fulls[group] = {n: full_from_view(n, f) for n, f in zip(names, fulls)}
                if group + 1 < len(GROUPS):
                    self._start_gather(group + 1, fulls[0])
            return self.fulls[group]

        def started(self):
            return self.gathers[max(self.gathers)]["token"]

        def prefetch(self, group, after):
            rl, state = rowlikes[group], self.gathers[group]
            sv, fulls = exchange_wait(state, _gather_plan(rl), after, f"gather_wait_{group}")
            self.forwards[group] = exchange_start(sv, fulls, _forward_plan(rl), None, f"forward_start_{group}", FORWARDS)
            return self.forwards[group]["token"]

        def grads(self, g):
            names = tuple(g)
            rl = [_rowlike(n) for n in names]
            gv = [grad_view(n, g[n]) for n in names]
            state = exchange_start(gv, [_half_buffer_shape(a, r) for a, r in zip(gv, rl)], _swap_plan(rl), None,
                                   "swap_start_" + names[0], SIBLING)
            self.reductions[names] = state
            return state["token"]

        def advance(self, names, after):
            rl = [_rowlike(n) for n in names]
            gv, gots = exchange_wait(self.reductions[names], _swap_plan(rl), after, "swap_wait_" + names[0], SIBLING)
            qs = add_halves(gv, gots, rl, place, "add_halves_" + names[0])
            state = exchange_start(qs, [_slot_shape(q, r) for q, r in zip(qs, rl)], _all_to_all_plan(rl), None,
                                   "all_to_all_start_" + names[0])
            self.reductions[names] = state
            return state["token"]

        def reduce(self, names, after):
            rl = [_rowlike(n) for n in names]
            qs, recvs = exchange_wait(self.reductions.pop(names), _all_to_all_plan(rl), after,
                                      "all_to_all_wait_" + names[0])
            halves = sum_pieces(qs, recvs, rl, place, "sum_pieces_" + names[0])
            return [(shards[n], o.reshape(shards[n].shape), local2d(m, n), local2d(v, n))
                    for n, o in zip(names, join_halves(halves, "join_halves_" + names[0]))]

        def updated(self, names, outs):
            for i, n in enumerate(names):
                grads[n], delta[n], new_m[n], new_v[n] = outs[4 * i:4 * i + 4]
            return [new_v[n] for n in names]

        def update(self, names, updates, token=None):
            return self.updated(names, adamw(updates, place, "adamw_" + names[0], after=token))

        def finish(self, names, after):
            return self.update(names, self.reduce(names, after))

        def small_start(self, G, loss, after):
            packed = pack_small({n: G[n] for n in SMALL})
            used_rows = sum(-(-int(np.prod(small_like[n].shape)) // SMALL_W) for n in SMALL)
            assert packed.shape[0] > used_rows
            packed = packed.at[-1, 0].set(loss[0, 0])
            self.small = exchange_start([packed], [(N_DEV,) + packed.shape], _everyone_plan, after,
                                        "small_start", EVERYONE)
            return self.small["token"]

        def small_finish(self, after):
            (packed,), (slots,) = exchange_wait(self.small, _everyone_plan, after, "small_wait", EVERYONE)
            total = sum_devices(packed, slots, (4 * cx + 2 * cy + cc).astype(jnp.int32).reshape(1))
            self.loss = total[-1, 0]
            g, ds, ms, vs = adamw([(self.packed[0], total, self.packed[1], self.packed[2])], place, "adamw_small")
            for d, packed_d in ((grads, g), (delta, ds), (new_m, ms), (new_v, vs)):
                d.update(unpack_small(packed_d, small_like))
            return vs

    small_like = {n: w[n] for n in SMALL}
    S = {n: w[n].reshape(1, -1) for n in ("ffn1_norm", "mix_norm", "ffn2_norm", "final_norm", "pool_scale", "sinks")}
    S["pool_w_mix"] = w["pool_w_mix"][0].astype(BF16)
    exchanges = Exchanges()
    _, dx = local_fwd_bwd(x[0], loss_target[0], S, exchanges)
    loss = exchanges.loss

    def shaped(d, n):
        return (d[n].T if n in TRANSPOSED else d[n]).reshape(w[n].shape)

    return (loss, dx[None], *[shaped(grads, n) for n in WEIGHTS], *[shaped(delta, n) for n in WEIGHTS],
            *[shaped(new_m, n) for n in WEIGHTS], *[shaped(new_v, n) for n in WEIGHTS])
```

```python
import numpy as np
import jax
import jax.numpy as jnp
from jax import lax
from jax.experimental import pallas as pl
from jax.experimental.pallas import tpu as pltpu

F32 = jnp.float32
BF16 = jnp.bfloat16
SDS = jax.ShapeDtypeStruct
MESH = pl.DeviceIdType.MESH

D = 1024
FF = 2816
NQ = 16
NKV = 2
HD = 64
GQ = NQ // NKV
AW = NQ * HD
KVW = NKV * HD
BLK = 128
PW = 512
PG = 128
POOL_WINDOWS = (2, 4, 8, 16)
HALO = 16
INW = AW + 2 * KVW + PW + 2 * D
C_KV = AW
C_Z = AW + 2 * KVW
C_G = C_Z + PW
EPS = 1e-6
FF_CHUNK = 256
FF_CHUNKS = tuple((c, FF_CHUNK) for c in range(0, FF, FF_CHUNK))
SLOPES = tuple(float(2.0 ** (-8.0 * h / NQ)) for h in range(1, NQ + 1))
SCALE = HD ** -0.5

LR, B1, B2, ADAM_EPS, WD, STEP = 0.001, 0.9, 0.999, 1e-08, 0.01, 10

VMEM_LIMIT = 56 * 1024 * 1024
N_CHIPS = 4

NT = (((1,), (1,)), ((), ()))
TN = (((0,), (0,)), ((), ()))


def _cp(sem=None, vmem=VMEM_LIMIT):
    return pltpu.CompilerParams(dimension_semantics=sem, vmem_limit_bytes=vmem)


def _const_spec(shape):
    nd = len(shape)
    return pl.BlockSpec(shape, lambda *_: (0,) * nd, pipeline_mode=pl.Buffered(1))


def _rstd(x):
    return lax.rsqrt(jnp.mean(x * x, axis=-1, keepdims=True) + EPS)


def _rms_bwd(dn, xhat, rstd, g):
    dxhat = dn * g
    return rstd * (dxhat - xhat * jnp.mean(dxhat * xhat, axis=-1, keepdims=True))


def _dot(a, b):
    return jnp.dot(a, b, preferred_element_type=F32)


def _dot_nt(a, b):
    return lax.dot_general(a, b, NT, preferred_element_type=F32)


def _dot_tn(a, b):
    return lax.dot_general(a, b, TN, preferred_element_type=F32)


def _swiglu_up(x, g, wup_ref, ab_ref, hid_ref):
    n = (x * _rstd(x) * g).astype(BF16)
    for c0, w in FF_CHUNKS:
        a = _dot(n, wup_ref[:, c0:c0 + w])
        b = _dot(n, wup_ref[:, FF + c0:FF + c0 + w])
        sig = jax.nn.sigmoid(a)
        s = a * sig
        ab_ref[:, c0:c0 + w] = (b * (sig * (1.0 + a * (1.0 - sig)))).astype(BF16)
        ab_ref[:, FF + c0:FF + c0 + w] = s.astype(BF16)
        hid_ref[:, c0:c0 + w] = (s * b).astype(BF16)


def ffn_up(h, g, wup, name):
    T = h.shape[0]
    TM = 512
    tile = lambda w: pl.BlockSpec((TM, w), lambda i: (i, 0))

    def body(h_ref, g_ref, wup_ref, ab_ref, hid_ref):
        _swiglu_up(h_ref[...], g_ref[...], wup_ref, ab_ref, hid_ref)

    return pl.pallas_call(
        body, name=name, grid=(T // TM,),
        in_specs=[tile(D), _const_spec((1, D)), _const_spec((D, 2 * FF))],
        out_specs=[tile(2 * FF), tile(FF)],
        out_shape=[SDS((T, 2 * FF), BF16), SDS((T, FF), BF16)],
        compiler_params=_cp(("arbitrary",)),
    )(h, g, wup)


def ffn_down(h, hid, wdn, name, after=None):
    T = h.shape[0]
    TM = min(1024, T)
    tile = lambda w: pl.BlockSpec((TM, w), lambda i: (i, 0))
    token_spec, token_arg = _token_operand(after)

    def body(h_ref, hid_ref, wdn_ref, *rest):
        rest[-1][...] = h_ref[...] + 0.5 * _dot(hid_ref[...], wdn_ref[...])

    return pl.pallas_call(
        body, name=name, grid=(T // TM,),
        in_specs=[tile(D), tile(FF), _const_spec((FF, D))] + token_spec,
        out_specs=tile(D), out_shape=SDS((T, D), F32),
        compiler_params=_cp(("arbitrary",)),
    )(h, hid, wdn, *token_arg)


def ffn_fwd(h, g, wup, wdn, name, head=None):
    T = h.shape[0]
    TM = 512
    tile = lambda w: pl.BlockSpec((TM, w), lambda i: (i, 0))
    acc_spec = lambda w: pl.BlockSpec((1, w), lambda i: (0, 0))

    def body(h_ref, g_ref, wup_ref, wdn_ref, *rest):
        out_ref, ab_ref, hid_ref = rest[-3:] if head is None else rest[2:5]
        x = h_ref[...]
        _swiglu_up(x, g_ref[...], wup_ref, ab_ref, hid_ref)
        out = x + 0.5 * _dot(hid_ref[...], wdn_ref[...])
        if head is None:
            out_ref[...] = out
        else:
            t_ref, gf_ref, loss_ref, dgf_ref = rest[0], rest[1], rest[5], rest[6]
            out_ref[...] = _loss_head(out, t_ref[...], gf_ref[...], loss_ref, dgf_ref, pl.program_id(0) == 0)

    head_in, head_specs, head_out_specs, head_out_shape = [], [], [], []
    if head is not None:
        head_in, head_specs = list(head), [tile(D), _const_spec((1, D))]
        head_out_specs, head_out_shape = [acc_spec(1), acc_spec(D)], [SDS((1, 1), F32), SDS((1, D), F32)]
    return pl.pallas_call(
        body, name=name, grid=(T // TM,),
        in_specs=[tile(D), _const_spec((1, D)), _const_spec((D, 2 * FF)), _const_spec((FF, D))] + head_specs,
        out_specs=[tile(D), tile(2 * FF), tile(FF)] + head_out_specs,
        out_shape=[SDS((T, D), F32), SDS((T, 2 * FF), BF16), SDS((T, FF), BF16)] + head_out_shape,
        compiler_params=_cp(("arbitrary",)),
    )(h, g, wup, wdn, *head_in)


def _loss_head(x, target, g, loss_ref, dg_ref, first):
    @pl.when(first)
    def _():
        loss_ref[...] = jnp.zeros_like(loss_ref)
        dg_ref[...] = jnp.zeros_like(dg_ref)

    rstd = _rstd(x)
    xhat = x * rstd
    err = xhat * g - target
    loss_ref[...] += 0.5 * jnp.sum(jnp.mean(err * err, axis=-1, keepdims=True), axis=0, keepdims=True)
    dy = err * (1.0 / D)
    dg_ref[...] += jnp.sum(dy * xhat, axis=0, keepdims=True)
    return _rms_bwd(dy, xhat, rstd, g)


def _adamw_tile(w_ref, g_ref, m_ref, v_ref, go_ref, d_ref, nm_ref, nv_ref):
    gv = g_ref[...]
    go_ref[...] = gv
    nm = B1 * m_ref[...] + (1.0 - B1) * gv
    nv = B2 * v_ref[...] + (1.0 - B2) * (gv * gv)
    nm_ref[...] = nm
    nv_ref[...] = nv
    d_ref[...] = -LR * ((nm / (1.0 - B1 ** STEP)) / (jnp.sqrt(nv / (1.0 - B2 ** STEP)) + ADAM_EPS) + WD * w_ref[...])


def _riders(updates, steps):
    in_specs, out_specs, out_shapes, operands, tiles = [], [], [], [], []
    for w, g, m, v in updates:
        R, C = w.shape
        n = max(d for d in range(1, steps + 1) if R % d == 0 and (R // d) % 8 == 0)
        spec = pl.BlockSpec((R // n, C), lambda i, n=n: (jnp.minimum(i, n - 1), 0))
        in_specs += [spec] * 4
        out_specs += [spec] * 4
        out_shapes += [SDS((R, C), F32)] * 4
        operands += [w, g, m, v]
        tiles.append(n)

    def run(step, in_refs, out_refs):
        for u, n in enumerate(tiles):
            @pl.when(step < n)
            def _(u=u):
                _adamw_tile(*in_refs[4 * u:4 * u + 4], *out_refs[4 * u:4 * u + 4])

    return in_specs, out_specs, out_shapes, operands, run


def ffn_bwd_x(dh, h_in, ab, g, wup, wdn, name, updates=()):
    T = dh.shape[0]
    TM = 256 if updates else 512
    SUB = 256
    r_in, r_out, r_shapes, r_args, ride = _riders(updates, T // TM)

    def body(dh_ref, h_ref, ab_ref, g_ref, wup_ref, wdn_ref, *rest):
        dhin_ref, dup_ref, n_ref, dg_ref = rest[len(r_in):len(r_in) + 4]
        ride(pl.program_id(0), rest[:len(r_in)], rest[len(r_in) + 4:])
        g = g_ref[...]
        dg = None
        for r0 in range(0, TM, SUB):
            rows = slice(r0, r0 + SUB)
            x = h_ref[rows, :]
            rstd = _rstd(x)
            xhat = x * rstd
            n_ref[rows, :] = (xhat * g).astype(BF16)
            dh = dh_ref[rows, :]
            dhh = (0.5 * dh).astype(BF16)
            for c0, w in FF_CHUNKS:
                dhid = _dot_nt(dhh, wdn_ref[c0:c0 + w, :])
                dup_ref[rows, c0:c0 + w] = (dhid * ab_ref[rows, c0:c0 + w].astype(F32)).astype(BF16)
                dup_ref[rows, FF + c0:FF + c0 + w] = (dhid * ab_ref[rows, FF + c0:FF + c0 + w].astype(F32)).astype(BF16)
            dn = _dot_nt(dup_ref[rows, :], wup_ref[...])
            dhin_ref[rows, :] = dh + _rms_bwd(dn, xhat, rstd, g)
            part = jnp.sum(dn * xhat, axis=0, keepdims=True)
            dg = part if dg is None else dg + part

        @pl.when(pl.program_id(0) == 0)
        def _():
            dg_ref[...] = jnp.zeros_like(dg_ref)

        dg_ref[...] += dg

    tile = lambda w: pl.BlockSpec((TM, w), lambda i: (i, 0))
    return pl.pallas_call(
        body, name=name, grid=(T // TM,),
        in_specs=[tile(D), tile(D), tile(2 * FF), _const_spec((1, D)), _const_spec((D, 2 * FF)), _const_spec((FF, D))]
        + r_in,
        out_specs=[tile(D), tile(2 * FF), tile(D), pl.BlockSpec((1, D), lambda i: (0, 0))] + r_out,
        out_shape=[SDS((T, D), F32), SDS((T, 2 * FF), BF16), SDS((T, D), BF16), SDS((1, D), F32)] + r_shapes,
        compiler_params=_cp(("arbitrary",)),
    )(dh, h_in, ab, g, wup, wdn, *r_args)


TOKEN_SPEC = pl.BlockSpec((8, 128), lambda *_: (0, 0))


def _token_operand(token):
    return ([], []) if token is None else ([TOKEN_SPEC], [token])


def matmul_tn(a, b, name, *, tm, tn, tt=1024, b_scale=None, after=None):
    T, M = a.shape
    N = b.shape[1]
    tt = min(tt, T)
    assert M % tm == 0 and N % tn == 0 and T % tt == 0
    nt = T // tt
    token_spec, token_arg = _token_operand(after)

    def body(a_ref, b_ref, *rest):
        o_ref, acc_ref = rest[-2:]
        t = pl.program_id(2)

        @pl.when(t == 0)
        def _():
            acc_ref[...] = jnp.zeros_like(acc_ref)

        bv = b_ref[...]
        if b_scale is not None:
            bv = bv * b_scale
        acc_ref[...] += _dot_tn(a_ref[...].astype(BF16), bv.astype(BF16))

        @pl.when(t == nt - 1)
        def _():
            o_ref[...] = acc_ref[...].astype(BF16)

    return pl.pallas_call(
        body, name=name, grid=(M // tm, N // tn, nt),
        in_specs=[pl.BlockSpec((tt, tm), lambda i, j, t: (t, i)), pl.BlockSpec((tt, tn), lambda i, j, t: (t, j))]
        + token_spec,
        out_specs=pl.BlockSpec((tm, tn), lambda i, j, t: (i, j)),
        out_shape=SDS((M, N), BF16),
        scratch_shapes=[pltpu.VMEM((tm, tn), F32)],
        compiler_params=_cp(("parallel", "parallel", "arbitrary")),
    )(a, b, *token_arg)


def mix_in_fwd(h1, g, win_t):
    T = h1.shape[0]
    TM = 512

    def body(h_ref, g_ref, w_ref, u_ref, q_ref, kv_ref, z_ref, gt_ref):
        x = h_ref[...]
        u = (x * _rstd(x) * g_ref[...]).astype(BF16)
        u_ref[...] = u
        for c in range(0, AW, 256):
            q_ref[:, c:c + 256] = _dot_nt(u, w_ref[c:c + 256, :]).astype(BF16)
        kv_ref[...] = _dot_nt(u, w_ref[C_KV:C_Z, :]).astype(BF16)
        for c in range(0, PW, 256):
            z_ref[:, c:c + 256] = _dot_nt(u, w_ref[C_Z + c:C_Z + c + 256, :])
        for c in range(0, 2 * D, 256):
            gt_ref[:, c:c + 256] = _dot_nt(u, w_ref[C_G + c:C_G + c + 256, :]).astype(BF16)

    tile = lambda w: pl.BlockSpec((TM, w), lambda i: (i, 0))
    return pl.pallas_call(
        body, name="mix_in_fwd", grid=(T // TM,),
        in_specs=[tile(D), _const_spec((1, D)), _const_spec((INW, D))],
        out_specs=[tile(D), tile(AW), tile(2 * KVW), tile(PW), tile(2 * D)],
        out_shape=[SDS((T, D), BF16), SDS((T, AW), BF16), SDS((T, 2 * KVW), BF16), SDS((T, PW), F32),
                   SDS((T, 2 * D), BF16)],
        compiler_params=_cp(("arbitrary",)),
    )(h1, g, win_t)


def mix_in_bwd(dq, dkv, dz, dgt, dh2, h1, g, win_t):
    T = h1.shape[0]
    TM = min(1024, T)
    SUB = 256

    def body(dq_ref, dkv_ref, dz_ref, dgt_ref, dh2_ref, h_ref, g_ref, w_ref, dh1_ref, dg_ref):
        g = g_ref[...]
        dg = None
        for r0 in range(0, TM, SUB):
            rows = slice(r0, r0 + SUB)
            du = _dot(dq_ref[rows, :], w_ref[0:AW, :])
            du += _dot(dkv_ref[rows, :], w_ref[C_KV:C_Z, :])
            du += _dot(dz_ref[rows, :], w_ref[C_Z:C_G, :])
            du += _dot(dgt_ref[rows, :], w_ref[C_G:INW, :])
            x = h_ref[rows, :]
            rstd = _rstd(x)
            xhat = x * rstd
            dh1_ref[rows, :] = dh2_ref[rows, :] + _rms_bwd(du, xhat, rstd, g)
            part = jnp.sum(du * xhat, axis=0, keepdims=True)
            dg = part if dg is None else dg + part

        @pl.when(pl.program_id(0) == 0)
        def _():
            dg_ref[...] = jnp.zeros_like(dg_ref)

        dg_ref[...] += dg

    tile = lambda w: pl.BlockSpec((TM, w), lambda i: (i, 0))
    return pl.pallas_call(
        body, name="mix_in_bwd", grid=(T // TM,),
        in_specs=[tile(AW), tile(2 * KVW), tile(PW), tile(2 * D), tile(D), tile(D), _const_spec((1, D)),
                  _const_spec((INW, D))],
        out_specs=[tile(D), pl.BlockSpec((1, D), lambda i: (0, 0))],
        out_shape=[SDS((T, D), F32), SDS((1, D), F32)],
        compiler_params=_cp(("arbitrary",)),
    )(dq, dkv, dz, dgt, dh2, h1, g, win_t)


PAIR = 2 * HD
NPAIR = GQ // 2


def _lo_lanes():
    return lax.broadcasted_iota(jnp.int32, (BLK, PAIR), 1) < HD


def _stack_heads(ref, kvh, scale=None):
    lo = _lo_lanes()
    parts = []
    for pr in range(NPAIR):
        t = ref[:, (kvh * NPAIR + pr) * PAIR:(kvh * NPAIR + pr + 1) * PAIR]
        if scale is not None:
            t = t * scale
        zero = jnp.zeros_like(t)
        parts += [jnp.where(lo, t, zero), jnp.where(lo, zero, t)]
    return jnp.concatenate(parts, axis=0)


def _kv_tiles(kvc_ref, kvp_ref, tile, kvh):
    lo = _lo_lanes()
    dup, left, right = [], [], []
    for ref in (kvp_ref, kvc_ref):
        t = ref[:, tile * PAIR:(tile + 1) * PAIR]
        r = pltpu.roll(t.astype(F32), HD, 1).astype(BF16)
        zero = jnp.zeros_like(t)
        a, b = (t, r) if kvh == 0 else (r, t)
        dup.append(jnp.where(lo, a, b))
        left.append(jnp.where(lo, a, zero))
        right.append(jnp.where(lo, zero, b))
    cat = lambda xs: jnp.concatenate(xs, axis=0)
    return cat(dup), cat(left), cat(right)


def _band_consts(first):
    row = lax.broadcasted_iota(jnp.int32, (BLK, BLK), 0)
    col = lax.broadcasted_iota(jnp.int32, (BLK, BLK), 1)
    upper = col > row
    dist = jnp.where(upper, row - col + BLK, row - col).astype(F32)
    pen = jnp.where(jnp.logical_and(upper, first), -jnp.inf, 0.0)
    return upper, dist, pen


def _split_band(upper, t):
    zero = jnp.zeros_like(t)
    return jnp.concatenate([jnp.where(upper, t, zero), jnp.where(upper, zero, t)], axis=1)


def attn_fwd(q, kv, sinks):
    T = q.shape[0]
    nb = T // BLK

    def body(sink_ref, q_ref, kvc_ref, kvp_ref, att_ref, lse_ref):
        upper, dist, pen = _band_consts(pl.program_id(0) == 0)
        scores, values = [], []
        for kvh in range(NKV):
            kdup, _, _ = _kv_tiles(kvc_ref, kvp_ref, 0, kvh)
            values.append(_kv_tiles(kvc_ref, kvp_ref, 1, kvh)[1:])
            scores.append(_dot_nt(_stack_heads(q_ref, kvh, SCALE), kdup))
        for kvh in range(NKV):
            s_all = scores[kvh]
            vleft, vright = values[kvh]
            for pr in range(NPAIR):
                outs, inv = [], []
                for side, vpad in ((0, vleft), (1, vright)):
                    g = 2 * pr + side
                    hq = kvh * GQ + g
                    sink = sink_ref[0, hq]
                    rows = slice(g * BLK, (g + 1) * BLK)
                    s = jnp.where(upper, s_all[rows, 0:BLK], s_all[rows, BLK:2 * BLK]) - SLOPES[hq] * dist + pen
                    m = jnp.maximum(jnp.max(s, axis=-1, keepdims=True), sink)
                    p = jnp.exp(s - m)
                    l = jnp.sum(p, axis=-1, keepdims=True) + jnp.exp(sink - m)
                    lse_ref[:, hq:hq + 1] = m + jnp.log(l)
                    outs.append(_dot(_split_band(upper, p.astype(BF16)), vpad))
                    inv.append(1.0 / l)
                col0 = (kvh * NPAIR + pr) * PAIR
                att_ref[:, col0:col0 + PAIR] = ((outs[0] + outs[1]) * jnp.where(_lo_lanes(), inv[0], inv[1])).astype(BF16)

    return pl.pallas_call(
        body, name="attn_fwd", grid=(nb,),
        in_specs=[pl.BlockSpec(memory_space=pltpu.SMEM),
                  pl.BlockSpec((BLK, AW), lambda i: (i, 0)),
                  pl.BlockSpec((BLK, 2 * KVW), lambda i: (i, 0)),
                  pl.BlockSpec((BLK, 2 * KVW), lambda i: (jnp.maximum(i - 1, 0), 0))],
        out_specs=[pl.BlockSpec((BLK, AW), lambda i: (i, 0)), pl.BlockSpec((BLK, NQ), lambda i: (i, 0))],
        out_shape=[SDS((T, AW), BF16), SDS((T, NQ), F32)],
        compiler_params=_cp(("arbitrary",)),
    )(sinks, q, kv, kv)


def attn_bwd(q, kv, datt, att, lse, sinks):
    T = q.shape[0]
    nb = T // BLK

    def body(sink_ref, q_ref, kvc_ref, kvp_ref, do_ref, out_ref, lse_ref, dq_ref, dkv_ref, dsink_ref, carry_ref):
        i = pl.program_id(0)

        @pl.when(i == 0)
        def _():
            dsink_ref[...] = jnp.zeros_like(dsink_ref)
            carry_ref[...] = jnp.zeros_like(carry_ref)

        @pl.when(i < nb)
        def _():
            upper, dist, pen = _band_consts(i == 0)
            lo = _lo_lanes()
            dk_dup, dv_dup = [], []
            staged = []
            for kvh in range(NKV):
                kdup, kleft, kright = _kv_tiles(kvc_ref, kvp_ref, 0, kvh)
                vdup, _, _ = _kv_tiles(kvc_ref, kvp_ref, 1, kvh)
                qs = _stack_heads(q_ref, kvh, SCALE)
                dos = _stack_heads(do_ref, kvh)
                staged.append((kleft, kright, qs, dos, _dot_nt(qs, kdup), _dot_nt(dos, vdup)))
            deltas = []
            for pair in range(NQ // 2):
                cols = slice(pair * PAIR, (pair + 1) * PAIR)
                t = do_ref[:, cols].astype(F32) * out_ref[:, cols].astype(F32)
                deltas += [jnp.sum(jnp.where(lo, t, 0.0), axis=-1, keepdims=True),
                           jnp.sum(jnp.where(lo, 0.0, t), axis=-1, keepdims=True)]
            for kvh in range(NKV):
                kleft, kright, qs, dos, s_all, dp_all = staged[kvh]
                ds_parts, p_parts = [], []
                for pr in range(NPAIR):
                    dq = None
                    for side, kpad in ((0, kleft), (1, kright)):
                        g = 2 * pr + side
                        hq = kvh * GQ + g
                        lse_h = lse_ref[:, hq:hq + 1]
                        rows = slice(g * BLK, (g + 1) * BLK)
                        s = jnp.where(upper, s_all[rows, 0:BLK], s_all[rows, BLK:2 * BLK]) - SLOPES[hq] * dist + pen
                        p = jnp.exp(s - lse_h)
                        dp = jnp.where(upper, dp_all[rows, 0:BLK], dp_all[rows, BLK:2 * BLK])
                        delta = deltas[hq]
                        dsink_ref[:, hq:hq + 1] += -jnp.sum(jnp.exp(sink_ref[0, hq] - lse_h) * delta, axis=0,
                                                            keepdims=True)
                        ds = _split_band(upper, (p * (dp - delta)).astype(BF16))
                        ds_parts.append(ds)
                        p_parts.append(_split_band(upper, p.astype(BF16)))
                        d = _dot(ds, kpad)
                        dq = d if dq is None else dq + d
                    col0 = (kvh * NPAIR + pr) * PAIR
                    dq_ref[:, col0:col0 + PAIR] = (dq * SCALE).astype(BF16)
                dkw = _dot_tn(qs, jnp.concatenate(ds_parts, axis=0)).T
                dvw = _dot_tn(dos, jnp.concatenate(p_parts, axis=0)).T
                dk_dup.append(dkw + pltpu.roll(dkw, HD, 1))
                dv_dup.append(dvw + pltpu.roll(dvw, HD, 1))
            dk = jnp.where(jnp.concatenate([lo, lo], axis=0), dk_dup[0], dk_dup[1])
            dv = jnp.where(jnp.concatenate([lo, lo], axis=0), dv_dup[0], dv_dup[1])
            dkv_ref[:, 0:PAIR] = (carry_ref[:, 0:PAIR] + dk[0:BLK]).astype(BF16)
            dkv_ref[:, PAIR:2 * PAIR] = (carry_ref[:, PAIR:2 * PAIR] + dv[0:BLK]).astype(BF16)
            carry_ref[:, 0:PAIR] = dk[BLK:2 * BLK]
            carry_ref[:, PAIR:2 * PAIR] = dv[BLK:2 * BLK]

        @pl.when(i == nb)
        def _():
            dkv_ref[...] = carry_ref[...].astype(BF16)

    cur = lambda i: (jnp.minimum(i, nb - 1), 0)
    prev = lambda i: (jnp.maximum(jnp.minimum(i, nb - 1) - 1, 0), 0)
    return pl.pallas_call(
        body, name="attn_bwd", grid=(nb + 1,),
        in_specs=[pl.BlockSpec(memory_space=pltpu.SMEM),
                  pl.BlockSpec((BLK, AW), cur), pl.BlockSpec((BLK, 2 * KVW), cur), pl.BlockSpec((BLK, 2 * KVW), prev),
                  pl.BlockSpec((BLK, AW), cur), pl.BlockSpec((BLK, AW), cur), pl.BlockSpec((BLK, NQ), cur)],
        out_specs=[pl.BlockSpec((BLK, AW), cur),
                   pl.BlockSpec((BLK, 2 * KVW), lambda i: (jnp.maximum(i - 1, 0), 0)),
                   pl.BlockSpec((1, NQ), lambda i: (0, 0))],
        out_shape=[SDS((T, AW), BF16), SDS((T, 2 * KVW), BF16), SDS((1, NQ), F32)],
        scratch_shapes=[pltpu.VMEM((BLK, 2 * KVW), F32)],
        compiler_params=_cp(("arbitrary",)),
    )(sinks, q, kv, kv, datt, att, lse)


def _inv_counts(t0, rows):
    t = (t0 + lax.broadcasted_iota(jnp.int32, (rows, 1), 0) + 1).astype(F32)
    return [1.0 / jnp.minimum(t, float(w)) for w in POOL_WINDOWS]


def pool_fwd(z, wmix, scale):
    T = z.shape[0]
    TM = min(1024, T)
    L = TM + HALO

    def body(z_ref, halo_ref, wmix_ref, scale_ref, pooled_ref, mixs_ref):
        i = pl.program_id(0)
        halo = jnp.where(i > 0, halo_ref[...], 0.0)
        zt = z_ref[...]
        e = jnp.concatenate([halo, zt], axis=0)
        sums = []
        s = e
        for k in (1, 2, 4, 8):
            s = s + pltpu.roll(s, k, 0)
            sums.append(s)
        inv = _inv_counts(i * TM, TM)
        for gi in range(len(POOL_WINDOWS)):
            cols = slice(gi * PG, (gi + 1) * PG)
            pooled = (sums[gi][HALO:, cols] * inv[gi] - zt[:, cols]).astype(BF16)
            pooled_ref[:, cols] = pooled
            mixs_ref[:, cols] = (_dot(pooled, wmix_ref[gi]) * scale_ref[:, cols]).astype(BF16)

    return pl.pallas_call(
        body, name="pool_fwd", grid=(T // TM,),
        in_specs=[pl.BlockSpec((TM, PW), lambda i: (i, 0)),
                  pl.BlockSpec((HALO, PW), lambda i: (jnp.maximum(i * (TM // HALO) - 1, 0), 0)),
                  _const_spec((len(POOL_WINDOWS), PG, PG)), _const_spec((1, PW))],
        out_specs=[pl.BlockSpec((TM, PW), lambda i: (i, 0)), pl.BlockSpec((TM, PW), lambda i: (i, 0))],
        out_shape=[SDS((T, PW), BF16), SDS((T, PW), BF16)],
        compiler_params=_cp(("arbitrary",)),
    )(z, z, wmix, scale)


def pool_bwd(dmixs, pooled, wmix, scale):
    T = dmixs.shape[0]
    TM = min(1024, T)
    L = TM + HALO
    nt = T // TM

    def body(dm_ref, halo_ref, pooled_ref, wmix_ref, scale_ref, dz_ref, dwmix_ref, dscale_ref):
        i = pl.program_id(0)

        @pl.when(i == 0)
        def _():
            dwmix_ref[...] = jnp.zeros_like(dwmix_ref)
            dscale_ref[...] = jnp.zeros_like(dscale_ref)

        halo = jnp.where(i < nt - 1, halo_ref[...], 0.0)
        dm = dm_ref[...]
        e = jnp.concatenate([dm, halo], axis=0)
        inv = _inv_counts(i * TM, L)
        for gi in range(len(POOL_WINDOWS)):
            cols = slice(gi * PG, (gi + 1) * PG)
            w = wmix_ref[gi]
            dmixed = (e[:, cols] * scale_ref[:, cols]).astype(BF16)
            dpooled = _dot_nt(dmixed, w)
            pooled = pooled_ref[:, cols]
            mixed = _dot(pooled, w)
            dscale_ref[:, cols] += jnp.sum(dm[:, cols] * mixed, axis=0, keepdims=True)
            dwmix_ref[gi] += _dot_tn(pooled, dmixed[:TM])
            s = dpooled * inv[gi]
            k = 1
            while k < POOL_WINDOWS[gi]:
                s = s + pltpu.roll(s, L - k, 0)
                k *= 2
            dz_ref[:, cols] = (s[:TM] - dpooled[:TM]).astype(BF16)

    return pl.pallas_call(
        body, name="pool_bwd", grid=(nt,),
        in_specs=[pl.BlockSpec((TM, PW), lambda i: (i, 0)),
                  pl.BlockSpec((HALO, PW), lambda i: (jnp.minimum((i + 1) * (TM // HALO), T // HALO - 1), 0)),
                  pl.BlockSpec((TM, PW), lambda i: (i, 0)),
                  _const_spec((len(POOL_WINDOWS), PG, PG)), _const_spec((1, PW))],
        out_specs=[pl.BlockSpec((TM, PW), lambda i: (i, 0)),
                   pl.BlockSpec((len(POOL_WINDOWS), PG, PG), lambda i: (0, 0, 0)),
                   pl.BlockSpec((1, PW), lambda i: (0, 0))],
        out_shape=[SDS((T, PW), BF16), SDS((len(POOL_WINDOWS), PG, PG), F32), SDS((1, PW), F32)],
        compiler_params=_cp(("arbitrary",)),
    )(dmixs, dmixs, pooled, wmix, scale)


def merge_fwd(att, mixs, gt, h1, wattn, wpool, wout):
    T = h1.shape[0]
    TM = 512

    def body(att_ref, mixs_ref, gt_ref, h_ref, wa_ref, wp_ref, wo_ref, h2_ref, mg_ref, gf_ref):
        a = _dot(att_ref[...], wa_ref[...])
        p = _dot(mixs_ref[...], wp_ref[...])
        sa = jax.nn.sigmoid(gt_ref[:, 0:D].astype(F32))
        sp = jax.nn.sigmoid(gt_ref[:, D:2 * D].astype(F32))
        gf_ref[:, 0:D] = (a * sa * (1.0 - sa)).astype(BF16)
        gf_ref[:, D:2 * D] = (p * sp * (1.0 - sp)).astype(BF16)
        mg = (sa * a + sp * p).astype(BF16)
        mg_ref[...] = mg
        h2_ref[...] = h_ref[...] + _dot(mg, wo_ref[...])

    tile = lambda w: pl.BlockSpec((TM, w), lambda i: (i, 0))
    return pl.pallas_call(
        body, name="merge_fwd", grid=(T // TM,),
        in_specs=[tile(AW), tile(PW), tile(2 * D), tile(D), _const_spec((AW, D)), _const_spec((PW, D)),
                  _const_spec((D, D))],
        out_specs=[tile(D), tile(D), tile(2 * D)],
        out_shape=[SDS((T, D), F32), SDS((T, D), BF16), SDS((T, 2 * D), BF16)],
        compiler_params=_cp(("arbitrary",)),
    )(att, mixs, gt, h1, wattn, wpool, wout)


def merge_bwd(dh2, gt, gf, wattn, wpool, wout, after=None):
    T = dh2.shape[0]
    TM = 512
    token_spec, token_arg = _token_operand(after)

    def body(dh2_ref, gt_ref, gf_ref, wa_ref, wp_ref, wo_ref, *rest):
        datt_ref, dmixs_ref, dgt_ref, da_ref, dp_ref = rest[-5:]
        dm = _dot_nt(dh2_ref[...].astype(BF16), wo_ref[...])
        da = (dm * jax.nn.sigmoid(gt_ref[:, 0:D].astype(F32))).astype(BF16)
        dp = (dm * jax.nn.sigmoid(gt_ref[:, D:2 * D].astype(F32))).astype(BF16)
        da_ref[...] = da
        dp_ref[...] = dp
        dgt_ref[:, 0:D] = (dm * gf_ref[:, 0:D].astype(F32)).astype(BF16)
        dgt_ref[:, D:2 * D] = (dm * gf_ref[:, D:2 * D].astype(F32)).astype(BF16)
        datt_ref[...] = _dot_nt(da, wa_ref[...]).astype(BF16)
        dmixs_ref[...] = _dot_nt(dp, wp_ref[...])

    tile = lambda w: pl.BlockSpec((TM, w), lambda i: (i, 0))
    return pl.pallas_call(
        body, name="merge_bwd", grid=(T // TM,),
        in_specs=[tile(D), tile(2 * D), tile(2 * D), _const_spec((AW, D)), _const_spec((PW, D)),
                  _const_spec((D, D))] + token_spec,
        out_specs=[tile(AW), tile(PW), tile(2 * D), tile(D), tile(D)],
        out_shape=[SDS((T, AW), BF16), SDS((T, PW), F32), SDS((T, 2 * D), BF16), SDS((T, D), BF16),
                   SDS((T, D), BF16)],
        compiler_params=_cp(("arbitrary",)),
    )(dh2, gt, gf, wattn, wpool, wout, *token_arg)


def adamw(updates, place, name, after=None):
    tile_bytes = 2 * 1024 * 1024 // len(updates)
    jobs = []
    for w, g, m, v in updates:
        R, C = w.shape
        tr = R
        if R * C * 4 > tile_bytes:
            tr = next(cand for cand in (512, 256, 128, 64, 32, 16, 8) if R % cand == 0 and cand * C * 4 <= tile_bytes)
        blk = ((tr, C), lambda ls, p: (ls, 0))
        jobs.append((R // tr, [(a, *blk) for a in (w, g, m, v)], [((R, C), F32, *blk)] * 4,
                     lambda ins, outs: _adamw_tile(*ins, *outs)))
    return _run_jobs(jobs, place, name, after)


GROUP_UP1, GROUP_MIX, GROUP_FFN2 = 0, 1, 2


def _behind(small, token):
    return small if token is None else small + token[0:1, 0:1]


def local_fwd_bwd(x, target, S, comm):
    w_up1 = comm.weights(GROUP_UP1, None)["ffn1_w_up"]
    ab1, hid1 = ffn_up(x, _behind(S["ffn1_norm"], comm.started()), w_up1, "ffn1_up")
    W = comm.weights(GROUP_MIX, hid1)
    h1 = ffn_down(x, hid1, W["ffn1_w_down"], "ffn1_down", after=comm.started())
    W = comm.late_weights(h1)
    u, q, kv, z, gt = mix_in_fwd(h1, S["mix_norm"], W["w_in"])
    att, lse = attn_fwd(q, kv, S["sinks"])
    pooled, mixs = pool_fwd(z, S["pool_w_mix"], _behind(S["pool_scale"], comm.prefetch(GROUP_FFN2, att)))
    h2, merged, gf = merge_fwd(att, mixs, gt, h1, W["w_attn_up"], W["w_pool_up"], W["w_out"])
    W = comm.weights(GROUP_FFN2, h2)
    dh3, ab2, hid2, loss, g_final = ffn_fwd(h2, S["ffn2_norm"], W["ffn2_w_up"], W["ffn2_w_down"], "ffn2_fwd",
                                            head=(target, S["final_norm"]))

    G = {"final_norm": g_final}
    dh2, dup2, n2, G["ffn2_norm"] = ffn_bwd_x(dh3, h2, ab2, S["ffn2_norm"], W["ffn2_w_up"], W["ffn2_w_down"],
                                              "ffn2_bwd_x")
    ffn2 = ("ffn2_w_down", "ffn2_w_up")
    token = comm.grads({"ffn2_w_down": matmul_tn(hid2, dh3, "ffn2_dw_down", tm=1408, tn=512, tt=4096, b_scale=0.5),
                        "ffn2_w_up": matmul_tn(n2, dup2, "ffn2_dw_up", tm=D, tn=512, tt=4096)})

    W = comm.weights(GROUP_MIX, None)
    datt, dmixs, dgt, da, dp = merge_bwd(dh2, gt, gf, W["w_attn_up"], W["w_pool_up"], W["w_out"], after=token)
    token = comm.advance(ffn2, datt)
    g_mix = {"w_out": matmul_tn(merged, dh2, "dw_out", tm=D, tn=D, after=token),
             "w_attn_up": matmul_tn(att, da, "dw_attn_up", tm=AW, tn=D),
             "w_pool_up": matmul_tn(mixs, dp, "dw_pool_up", tm=PW, tn=D)}
    dz, G["pool_w_mix"], G["pool_scale"] = pool_bwd(dmixs, pooled, S["pool_w_mix"], _behind(S["pool_scale"], token))
    dq, dkv, G["sinks"] = attn_bwd(q, kv, datt, att, lse, _behind(S["sinks"], token))
    dh1, G["mix_norm"] = mix_in_bwd(dq, dkv, dz, dgt, dh2, h1, S["mix_norm"], W["w_in"])
    g_mix["w_in"] = jnp.concatenate([
        matmul_tn(dq, u, "dw_in_q", tm=AW, tn=D),
        matmul_tn(dkv, u, "dw_in_kv", tm=2 * KVW, tn=D),
        matmul_tn(dz, u, "dw_in_z", tm=PW, tn=D),
        matmul_tn(dgt, u, "dw_in_g", tm=D, tn=512, tt=4096),
    ], axis=0)
    mix = tuple(g_mix)
    token = comm.grads(g_mix)

    g_dn1 = matmul_tn(hid1, dh1, "ffn1_dw_down", tm=1408, tn=512, tt=4096, b_scale=0.5, after=token)
    token = comm.advance(mix, g_dn1)
    token2 = comm.grads({"ffn1_w_down": g_dn1})
    g1 = _behind(_behind(S["ffn1_norm"], token), token2)
    dx, dup1, n1, G["ffn1_norm"], *outs = ffn_bwd_x(dh1, x, ab1, g1, w_up1, W["ffn1_w_down"], "ffn1_bwd_x",
                                                   updates=comm.reduce(ffn2, token2))
    comm.updated(ffn2, outs)
    token = comm.advance(("ffn1_w_down",), dx)
    token2 = comm.small_start(G, loss, dx)
    g_up1 = matmul_tn(n1, dup1, "ffn1_dw_up", tm=D, tn=512, tt=4096,
                      after=None if token is None else token + token2)
    token = comm.grads({"ffn1_w_up": g_up1})
    updates = comm.reduce(("ffn1_w_down",), token)
    token = comm.advance(("ffn1_w_up",), [u[1] for u in updates])
    done = comm.update(("ffn1_w_down",), updates, token)
    done = comm.finish(mix, done)
    done = comm.small_finish(done)
    comm.finish(("ffn1_w_up",), done)
    return loss, dx


HBM_SPEC = pl.BlockSpec(memory_space=pltpu.HBM)


def _me():
    return lax.axis_index("x"), lax.axis_index("y"), lax.axis_index("c")


def _peer_chip(x, y, k):
    return x ^ (k >> 1), y ^ (k & 1)


def _piece_half(ref, rowlike, j, h):
    if rowlike:
        return ref.at[j, h]
    ns = ref.shape[-1] // N_CHIPS
    return ref.at[h, :, pl.ds(pl.multiple_of(j * ns, 128), ns)]


def _piece(ref, rowlike, j):
    if rowlike:
        return ref.at[j]
    ns = ref.shape[-1] // N_CHIPS
    return ref.at[:, :, pl.ds(pl.multiple_of(j * ns, 128), ns)]


def _full_shape(shard_view, rowlike):
    _, kh, ns = shard_view.shape
    return (N_CHIPS, 2, kh, ns) if rowlike else (2, kh, N_CHIPS * ns)


def _remote(src, dst, send_sem, recv_sem, dev):
    return pltpu.make_async_remote_copy(src, dst, send_sem, recv_sem, device_id=dev, device_id_type=MESH)


SEM_SPEC = pl.BlockSpec(memory_space=pltpu.SEMAPHORE)
SPLIT_PARAMS = pltpu.CompilerParams(has_side_effects=pltpu.SideEffectType.DATAFLOW_SIDE_EFFECTING)


def _gather_plan(rowlikes):
    def plan(s_refs, f_refs, a, k, x, y, c):
        px, py = _peer_chip(x, y, k)
        return (s_refs[a].at[c], _piece_half(f_refs[a], rowlikes[a], 2 * x + y, c),
                _piece_half(f_refs[a], rowlikes[a], 2 * px + py, c), (px, py, c))
    return plan


def _all_to_all_plan(rowlikes):
    def plan(q_refs, r_refs, a, k, x, y, c):
        px, py = _peer_chip(x, y, k)
        if rowlikes[a]:
            src = q_refs[a].at[2 * px + py]
        else:
            ns = q_refs[a].shape[-1] // N_CHIPS
            src = q_refs[a].at[:, pl.ds(pl.multiple_of((2 * px + py) * ns, 128), ns)]
        return src, r_refs[a].at[k - 1], r_refs[a].at[k - 1], (px, py, c)
    return plan


def _swap_plan(rowlikes):
    def plan(g_refs, got_refs, a, k, x, y, c):
        src = g_refs[a].at[:, 1 - c] if rowlikes[a] else g_refs[a].at[1 - c]
        return src, got_refs[a], got_refs[a], (x, y, 1 - c)
    return plan


def _everyone_plan(s_refs, slot_refs, a, k, x, y, c):
    px, py, pc = x ^ (k >> 2), y ^ ((k >> 1) & 1), c ^ (k & 1)
    return s_refs[a], slot_refs[a].at[4 * x + 2 * y + c], slot_refs[a].at[4 * px + 2 * py + pc], (px, py, pc)


def _forward_plan(rowlikes):
    def plan(s_refs, f_refs, a, k, x, y, c):
        sib = (x, y, 1 - c)
        if k == 0:
            own = _piece(f_refs[a], rowlikes[a], 2 * x + y)
            return s_refs[a], own, own, sib
        px, py = _peer_chip(x, y, k)
        mine = _piece_half(f_refs[a], rowlikes[a], 2 * px + py, c)
        return mine, mine, _piece_half(f_refs[a], rowlikes[a], 2 * px + py, 1 - c), sib
    return plan


CHIPS, SIBLING, EVERYONE, FORWARDS = (1, 2, 3), (1,), tuple(range(1, 8)), (0, 1, 2, 3)


def _as_list(after):
    return [] if after is None else list(after) if isinstance(after, (list, tuple)) else [after]


def exchange_start(srcs, land_shapes, plan, after, name, peers=CHIPS):
    n = len(srcs)
    lands = [l if hasattr(l, "dtype") else lax.empty(l, s.dtype) for l, s in zip(land_shapes, srcs)]

    behind = _as_list(after)

    def body(*refs):
        s_refs, l_refs = refs[:n], refs[n:2 * n]
        send_sems, recv_sems = refs[2 * n + len(behind)], refs[2 * n + len(behind) + 1]
        token = refs[-1]
        x, y, c = _me()
        for a in range(n):
            for i, k in enumerate(peers):
                src, dst, _, peer = plan(s_refs, l_refs, a, k, x, y, c)
                sem = len(peers) * a + i
                _remote(src, dst, send_sems.at[sem], recv_sems.at[sem], peer).start()
        token[...] = jnp.zeros_like(token)

    n_sems = len(peers) * n
    outs = pl.pallas_call(
        body, name=name, in_specs=[HBM_SPEC] * (2 * n) + [pl.BlockSpec(memory_space=pl.ANY)] * len(behind),
        out_specs=[SEM_SPEC, SEM_SPEC] + [HBM_SPEC] * (2 * n) + [pl.BlockSpec(memory_space=pltpu.VMEM)],
        out_shape=[pltpu.SemaphoreType.DMA((n_sems,)), pltpu.SemaphoreType.DMA((n_sems,))]
        + [pltpu.HBM(a.shape, a.dtype) for a in (*srcs, *lands)] + [SDS((8, 128), F32)],
        input_output_aliases={i: 2 + i for i in range(2 * n)},
        compiler_params=SPLIT_PARAMS,
    )(*[pltpu.with_memory_space_constraint(a, pltpu.HBM) for a in (*srcs, *lands)], *behind)
    return {"sems": outs[:2], "srcs": outs[2:2 + n], "lands": outs[2 + n:2 + 2 * n], "token": outs[-1]}


def exchange_wait(state, plan, after, name, peers=CHIPS):
    n = len(state["srcs"])
    behind = _as_list(after)

    def body(*refs):
        s_refs, l_refs = refs[:n], refs[n:2 * n]
        send_sems, recv_sems = refs[2 * n], refs[2 * n + 1]
        x, y, c = _me()
        for a in range(n):
            for i, k in enumerate(peers):
                src, _, landing, peer = plan(s_refs, l_refs, a, k, x, y, c)
                sem = len(peers) * a + i
                cp = _remote(src, landing, send_sems.at[sem], recv_sems.at[sem], peer)
                cp.wait_send()
                cp.wait_recv()

    bufs = (*state["srcs"], *state["lands"])
    outs = pl.pallas_call(
        body, name=name,
        in_specs=[HBM_SPEC] * (2 * n) + [SEM_SPEC, SEM_SPEC] + [pl.BlockSpec(memory_space=pl.ANY)] * len(behind),
        out_specs=[HBM_SPEC] * (2 * n),
        out_shape=[pltpu.HBM(a.shape, a.dtype) for a in bufs],
        input_output_aliases={i: i for i in range(2 * n)},
        compiler_params=SPLIT_PARAMS,
    )(*bufs, *state["sems"], *behind)
    return outs[:n], outs[n:]


def gather_finish(shards, fulls, rowlikes, name):
    n = len(shards)

    def body(*refs):
        s_refs, f_refs = refs[:n], refs[2 * n:3 * n]
        send_sems, recv_sems = refs[3 * n:]
        x, y, c = _me()
        chip = 2 * x + y
        sib = (x, y, 1 - c)
        sends = []
        for a in range(n):
            own = _piece(f_refs[a], rowlikes[a], chip)
            cp = _remote(s_refs[a], own, send_sems.at[a, 0], recv_sems.at[a, 0], sib)
            cp.start()
            sends.append(cp)
            for k in (1, 2, 3):
                px, py = _peer_chip(x, y, k)
                slot = _piece_half(f_refs[a], rowlikes[a], 2 * px + py, c)
                cp = _remote(slot, slot, send_sems.at[a, k], recv_sems.at[a, k], sib)
                cp.start()
                sends.append(cp)
        for a in range(n):
            own = _piece(f_refs[a], rowlikes[a], chip)
            _remote(own, own, send_sems.at[a, 0], recv_sems.at[a, 0], sib).wait_recv()
            for k in (1, 2, 3):
                px, py = _peer_chip(x, y, k)
                slot = _piece_half(f_refs[a], rowlikes[a], 2 * px + py, 1 - c)
                _remote(slot, slot, send_sems.at[a, k], recv_sems.at[a, k], sib).wait_recv()
        for cp in sends:
            cp.wait_send()

    return pl.pallas_call(
        body, name=name, in_specs=[HBM_SPEC] * (2 * n), out_specs=[HBM_SPEC] * n,
        out_shape=[SDS(f.shape, f.dtype) for f in fulls],
        input_output_aliases={n + a: a for a in range(n)},
        scratch_shapes=[pltpu.SemaphoreType.DMA((n, 4)), pltpu.SemaphoreType.DMA((n, 4))],
    )(*shards, *fulls)


def _half_buffer_shape(gview, rowlike):
    return (N_CHIPS,) + gview.shape[2:] if rowlike else gview.shape[1:]


def _row_tiles(kh, ns):
    return 1 if kh * ns <= 256 * 1024 else 2


def _run_jobs(jobs, place, name, after=None):
    n_steps = max(job[0] for job in jobs)

    def spec(block, index, steps):
        return pl.BlockSpec(block, lambda s, p: index(jnp.minimum(s, steps - 1), p))

    in_specs = [spec(b, ix, steps) for steps, ins, _, _ in jobs for _, b, ix in ins]
    out_specs = [spec(b, ix, steps) for steps, _, outs, _ in jobs for _, _, b, ix in outs]
    token_spec, token_arg = _token_operand(after)
    n_in = len(in_specs) + len(token_spec)

    def body(place_ref, *refs):
        s = pl.program_id(0)
        i, o = 0, n_in
        for steps, ins, outs, fn in jobs:
            in_refs, out_refs = refs[i:i + len(ins)], refs[o:o + len(outs)]
            i, o = i + len(ins), o + len(outs)

            @pl.when(s < steps)
            def _(fn=fn, in_refs=in_refs, out_refs=out_refs):
                fn(in_refs, out_refs)

    return pl.pallas_call(
        body, name=name,
        grid_spec=pltpu.PrefetchScalarGridSpec(num_scalar_prefetch=1, grid=(n_steps,),
                                               in_specs=in_specs + token_spec, out_specs=out_specs),
        out_shape=[SDS(shape, dtype) for _, _, outs, _ in jobs for shape, dtype, _, _ in outs],
        compiler_params=_cp(("arbitrary",)),
    )(place, *[a for _, ins, _, _ in jobs for a, _, _ in ins], *token_arg)


def add_halves(gviews, gots, rowlikes, place, name):
    def add(ins, outs):
        outs[0][...] = (ins[0][...].astype(F32) + ins[1][...].astype(F32)).astype(BF16)

    jobs = []
    for g, got, rowlike in zip(gviews, gots, rowlikes):
        kh, ns = (g.shape[2], g.shape[3]) if rowlike else (g.shape[1], g.shape[2] // N_CHIPS)
        rt = _row_tiles(kh, ns)
        tr = kh // rt
        if rowlike:
            g_in = (g, (None, None, tr, ns), lambda ls, p, rt=rt: (ls // rt, p[0], ls % rt, 0))
            half = ((None, tr, ns), lambda ls, p, rt=rt: (ls // rt, ls % rt, 0))
        else:
            g_in = (g, (None, tr, ns), lambda ls, p, rt=rt: (p[0], ls % rt, ls // rt))
            half = ((tr, ns), lambda ls, p, rt=rt: (ls % rt, ls // rt))
        jobs.append((N_CHIPS * rt, [g_in, (got, *half)], [(got.shape, BF16, *half)], add))
    return _run_jobs(jobs, place, name)


def _slot_shape(q, rowlike):
    return (3,) + (q.shape[1:] if rowlike else (q.shape[0], q.shape[1] // N_CHIPS))


def sum_pieces(qs, recvs, rowlikes, place, name):
    def total(ins, outs):
        acc = ins[0][...].astype(F32)
        for k in range(3):
            acc = acc + ins[1][k].astype(F32)
        outs[0][...] = acc

    jobs = []
    for q, recv, rowlike in zip(qs, recvs, rowlikes):
        kh, ns = recv.shape[1], recv.shape[2]
        rt = _row_tiles(kh, ns)
        tr = kh // rt
        mine = ((None, tr, ns), lambda ls, p: (p[1], ls, 0)) if rowlike else ((tr, ns), lambda ls, p: (ls, p[1]))
        jobs.append((rt, [(q, *mine), (recv, (3, tr, ns), lambda ls, p: (0, ls, 0))],
                     [((2, kh, ns), F32, (None, tr, ns), lambda ls, p: (p[0], ls, 0))], total))
    return _run_jobs(jobs, place, name)


def join_halves(halves, name):
    n = len(halves)

    def body(*refs):
        o_refs = refs[n:2 * n]
        send_sems, recv_sems = refs[2 * n:]
        x, y, c = _me()
        sib = (x, y, 1 - c)
        sends = []
        for a in range(n):
            cp = _remote(o_refs[a].at[c], o_refs[a].at[c], send_sems.at[a], recv_sems.at[a], sib)
            cp.start()
            sends.append(cp)
        for a in range(n):
            got = o_refs[a].at[1 - c]
            _remote(got, got, send_sems.at[a], recv_sems.at[a], sib).wait_recv()
        for cp in sends:
            cp.wait_send()

    return pl.pallas_call(
        body, name=name, in_specs=[HBM_SPEC] * n, out_specs=[HBM_SPEC] * n,
        out_shape=[SDS(h.shape, h.dtype) for h in halves],
        input_output_aliases={a: a for a in range(n)},
        scratch_shapes=[pltpu.SemaphoreType.DMA((n,)), pltpu.SemaphoreType.DMA((n,))],
    )(*halves)


N_DEV = 8


def sum_devices(s, slots, me):
    R, Wd = s.shape

    def body(me_ref, s_ref, slots_ref, out_ref):
        acc = None
        for d in range(N_DEV):
            mine = me_ref[0] == d
            term = jnp.where(mine, s_ref[...], slots_ref[jnp.where(mine, d ^ 1, d)])
            acc = term if acc is None else acc + term
        out_ref[...] = acc

    vmem = pl.BlockSpec(memory_space=pltpu.VMEM)
    return pl.pallas_call(
        body, name="sum_devices", in_specs=[pl.BlockSpec(memory_space=pltpu.SMEM), vmem, vmem], out_specs=vmem,
        out_shape=SDS((R, Wd), F32),
    )(me, s, slots)


TRANSPOSED = ("w_in",)
BIG = {"ffn1_w_up": (D, 2 * FF, "col"), "ffn1_w_down": (FF, D, "row"), "w_in": (INW, D, "row"),
       "w_attn_up": (AW, D, "row"), "w_pool_up": (PW, D, "col"), "w_out": (D, D, "row"),
       "ffn2_w_up": (D, 2 * FF, "col"), "ffn2_w_down": (FF, D, "row")}
GROUPS = (("ffn1_w_up",), ("ffn1_w_down", "w_in", "w_attn_up", "w_pool_up", "w_out"), ("ffn2_w_up", "ffn2_w_down"))
SMALL = ("ffn1_norm", "mix_norm", "ffn2_norm", "final_norm", "pool_scale", "sinks", "pool_w_mix")
SMALL_W = 128


def _rowlike(name):
    return BIG[name][2] == "row"


def _half_dims(name):
    k, n, kind = BIG[name]
    return (k // N_CHIPS // 2, n) if kind == "row" else (k // 2, n // N_CHIPS)


def shard_view(name, shard):
    return shard.reshape((2,) + _half_dims(name))


def full_from_view(name, fv):
    k, n, _ = BIG[name]
    return fv.reshape(k, n)


def grad_view(name, g):
    kh, ns = _half_dims(name)
    return g.reshape(_full_shape(jax.ShapeDtypeStruct((2, kh, ns), g.dtype), _rowlike(name)))


def pack_small(d):
    parts = []
    for name in SMALL:
        a = d[name].reshape(-1)
        pad = (-a.shape[0]) % SMALL_W
        parts.append(jnp.pad(a, (0, pad)).reshape(-1, SMALL_W))
    a = jnp.concatenate(parts, axis=0)
    return jnp.pad(a, ((0, (-a.shape[0]) % 8), (0, 0)))


def unpack_small(a, like):
    out, r0 = {}, 0
    for name in SMALL:
        size = int(np.prod(like[name].shape))
        rows = -(-size // SMALL_W)
        out[name] = a[r0:r0 + rows].reshape(-1)[:size].reshape(like[name].shape)
        r0 += rows
    return out


WEIGHTS = ("ffn1_norm", "ffn1_w_up", "ffn1_w_down", "mix_norm", "w_in", "sinks", "w_attn_up", "pool_w_mix",
           "pool_scale", "w_pool_up", "w_out", "ffn2_norm", "ffn2_w_up", "ffn2_w_down", "final_norm")


def kernel(x, ffn1_norm, ffn1_w_up, ffn1_w_down, mix_norm, w_in, sinks, w_attn_up, pool_w_mix, pool_scale, w_pool_up, w_out, ffn2_norm, ffn2_w_up, ffn2_w_down, final_norm, loss_target, m_ffn1_norm, m_ffn1_w_up, m_ffn1_w_down, m_mix_norm, m_w_in, m_sinks, m_w_attn_up, m_pool_w_mix, m_pool_scale, m_w_pool_up, m_w_out, m_ffn2_norm, m_ffn2_w_up, m_ffn2_w_down, m_final_norm, v_ffn1_norm, v_ffn1_w_up, v_ffn1_w_down, v_mix_norm, v_w_in, v_sinks, v_w_attn_up, v_pool_w_mix, v_pool_scale, v_w_pool_up, v_w_out, v_ffn2_norm, v_ffn2_w_up, v_ffn2_w_down, v_final_norm):
    given = dict(locals())
    w = {n: given[n] for n in WEIGHTS}
    m = {n: given["m_" + n] for n in WEIGHTS}
    v = {n: given["v_" + n] for n in WEIGHTS}
    cx, cy, cc = _me()
    place = jnp.stack([cc, 2 * cx + cy]).astype(jnp.int32)

    def local2d(d, n):
        return d[n][0].T if n in TRANSPOSED else d[n][0]

    shards = {n: local2d(w, n) for n in BIG}
    grads, delta, new_m, new_v = {}, {}, {}, {}
    rowlikes = [[_rowlike(n) for n in names] for names in GROUPS]

    class Exchanges:
        def __init__(self):
            self.gathers, self.forwards, self.fulls, self.reductions, self.small, self.late = {}, {}, {}, {}, None, None

        def _behind_first(self, a):
            return a + self.gathers[0]["token"][0, 0]

        def _start_gather(self, group, after):
            prepare = self._behind_first if group else (lambda a: a)
            sv = [shard_view(n, prepare(shards[n]).astype(BF16)) for n in GROUPS[group]]
            rl = rowlikes[group]
            self.gathers[group] = exchange_start(sv, [_full_shape(s, r) for s, r in zip(sv, rl)], _gather_plan(rl),
                                                 after, f"gather_start_{group}")

        def weights(self, group, after):
            if not self.gathers:
                self._start_gather(0, None)
                self.packed = [self._behind_first(pack_small(d)) for d in (w, m, v)]
                after = [self.gathers[0]["token"], *self.packed]
            if group not in self.fulls:
                names, rl, state = GROUPS[group], rowlikes[group], self.gathers[group]
                if group in self.forwards:
                    _, fulls = exchange_wait(self.forwards.pop(group), _forward_plan(rl), after,
                                             f"forward_wait_{group}", FORWARDS)
                else:
                    sv, fulls = exchange_wait(state, _gather_plan(rl), state["token"] if after is None else after,
                                              f"gather_wait_{group}")
                    if group == GROUP_MIX:
                        self.late = exchange_start(sv[1:], fulls[1:], _forward_plan(rl[1:]), None, "forward_start_1",
                                                   FORWARDS)
                        names, sv, fulls, rl = names[:1], sv[:1], fulls[:1], rl[:1]
                    fulls = gather_finish(sv, fulls, rl, f"gather_finish_{group}")
                self.fulls[group] = {n: full_from_view(n, f) for n, f in zip(names, fulls)}
                if group + 1 < len(GROUPS):
                    self._start_gather(group + 1, fulls[0])
            return self.fulls[group]

        def late_weights(self, after):
            if self.late is not None:
                names, rl = GROUPS[GROUP_MIX][1:], rowlikes[GROUP_MIX][1:]
                _, fulls = exchange_wait(self.late, _forward_plan(rl), after, "forward_wait_1", FORWARDS)
                self.fulls[GROUP_MIX].update({n: full_from_view(n, f) for n, f in zip(names, fulls)})
                self.late = None
            return self.fulls[GROUP_MIX]

        def started(self):
            token = self.gathers[max(self.gathers)]["token"]
            return token if self.late is None else token + self.late["token"]

        def prefetch(self, group, after):
            rl, state = rowlikes[group], self.gathers[group]
            sv, fulls = exchange_wait(state, _gather_plan(rl), after, f"gather_wait_{group}")
            self.forwards[group] = exchange_start(sv, fulls, _forward_plan(rl), None, f"forward_start_{group}", FORWARDS)
            return self.forwards[group]["token"]

        def grads(self, g):
            names = tuple(g)
            rl = [_rowlike(n) for n in names]
            gv = [grad_view(n, g[n]) for n in names]
            state = exchange_start(gv, [_half_buffer_shape(a, r) for a, r in zip(gv, rl)], _swap_plan(rl), None,
                                   "swap_start_" + names[0], SIBLING)
            self.reductions[names] = state
            return state["token"]

        def advance(self, names, after):
            rl = [_rowlike(n) for n in names]
            gv, gots = exchange_wait(self.reductions[names], _swap_plan(rl), after, "swap_wait_" + names[0], SIBLING)
            qs = add_halves(gv, gots, rl, place, "add_halves_" + names[0])
            state = exchange_start(qs, [_slot_shape(q, r) for q, r in zip(qs, rl)], _all_to_all_plan(rl), None,
                                   "all_to_all_start_" + names[0])
            self.reductions[names] = state
            return state["token"]

        def reduce(self, names, after):
            rl = [_rowlike(n) for n in names]
            qs, recvs = exchange_wait(self.reductions.pop(names), _all_to_all_plan(rl), after,
                                      "all_to_all_wait_" + names[0])
            halves = sum_pieces(qs, recvs, rl, place, "sum_pieces_" + names[0])
            return [(shards[n], o.reshape(shards[n].shape), local2d(m, n), local2d(v, n))
                    for n, o in zip(names, join_halves(halves, "join_halves_" + names[0]))]

        def updated(self, names, outs):
            for i, n in enumerate(names):
                grads[n], delta[n], new_m[n], new_v[n] = outs[4 * i:4 * i + 4]
            return [new_v[n] for n in names]

        def update(self, names, updates, token=None):
            return self.updated(names, adamw(updates, place, "adamw_" + names[0], after=token))

        def finish(self, names, after):
            return self.update(names, self.reduce(names, after))

        def small_start(self, G, loss, after):
            packed = pack_small({n: G[n] for n in SMALL})
            used_rows = sum(-(-int(np.prod(small_like[n].shape)) // SMALL_W) for n in SMALL)
            assert packed.shape[0] > used_rows
            packed = packed.at[-1, 0].set(loss[0, 0])
            self.small = exchange_start([packed], [(N_DEV,) + packed.shape], _everyone_plan, after,
                                        "small_start", EVERYONE)
            return self.small["token"]

        def small_finish(self, after):
            (packed,), (slots,) = exchange_wait(self.small, _everyone_plan, after, "small_wait", EVERYONE)
            total = sum_devices(packed, slots, (4 * cx + 2 * cy + cc).astype(jnp.int32).reshape(1))
            self.loss = total[-1, 0]
            g, ds, ms, vs = adamw([(self.packed[0], total, self.packed[1], self.packed[2])], place, "adamw_small")
            for d, packed_d in ((grads, g), (delta, ds), (new_m, ms), (new_v, vs)):
                d.update(unpack_small(packed_d, small_like))
            return vs

    small_like = {n: w[n] for n in SMALL}
    S = {n: w[n].reshape(1, -1) for n in ("ffn1_norm", "mix_norm", "ffn2_norm", "final_norm", "pool_scale", "sinks")}
    S["pool_w_mix"] = w["pool_w_mix"][0].astype(BF16)
    exchanges = Exchanges()
    _, dx = local_fwd_bwd(x[0], loss_target[0], S, exchanges)
    loss = exchanges.loss

    def shaped(d, n):
        return (d[n].T if n in TRANSPOSED else d[n]).reshape(w[n].shape)

    return (loss, dx[None], *[shaped(grads, n) for n in WEIGHTS], *[shaped(delta, n) for n in WEIGHTS],
            *[shaped(new_m, n) for n in WEIGHTS], *[shaped(new_v, n) for n in WEIGHTS])
```

```python
import numpy as np
import jax
import jax.numpy as jnp
from jax import lax
from jax.experimental import pallas as pl
from jax.experimental.pallas import tpu as pltpu

F32 = jnp.float32
BF16 = jnp.bfloat16
SDS = jax.ShapeDtypeStruct
MESH = pl.DeviceIdType.MESH

D = 1024
FF = 2816
NQ = 16
NKV = 2
HD = 64
GQ = NQ // NKV
AW = NQ * HD
KVW = NKV * HD
BLK = 128
PW = 512
PG = 128
POOL_WINDOWS = (2, 4, 8, 16)
HALO = 16
INW = AW + 2 * KVW + PW + 2 * D
C_KV = AW
C_Z = AW + 2 * KVW
C_G = C_Z + PW
EPS = 1e-6
FF_CHUNK = 256
FF_CHUNKS = tuple((c, FF_CHUNK) for c in range(0, FF, FF_CHUNK))
SLOPES = tuple(float(2.0 ** (-8.0 * h / NQ)) for h in range(1, NQ + 1))
SCALE = HD ** -0.5

LR, B1, B2, ADAM_EPS, WD, STEP = 0.001, 0.9, 0.999, 1e-08, 0.01, 10

VMEM_LIMIT = 56 * 1024 * 1024
N_CHIPS = 4

NT = (((1,), (1,)), ((), ()))
TN = (((0,), (0,)), ((), ()))


def _cp(sem=None, vmem=VMEM_LIMIT):
    return pltpu.CompilerParams(dimension_semantics=sem, vmem_limit_bytes=vmem)


def _const_spec(shape):
    nd = len(shape)
    return pl.BlockSpec(shape, lambda *_: (0,) * nd, pipeline_mode=pl.Buffered(1))


def _rstd(x):
    return lax.rsqrt(jnp.mean(x * x, axis=-1, keepdims=True) + EPS)


def _rms_bwd(dn, xhat, rstd, g):
    dxhat = dn * g
    return rstd * (dxhat - xhat * jnp.mean(dxhat * xhat, axis=-1, keepdims=True))


def _sigmoid(x):
    return 0.5 * jnp.tanh(0.5 * x) + 0.5


def _dot(a, b):
    return jnp.dot(a, b, preferred_element_type=F32)


def _dot_nt(a, b):
    return lax.dot_general(a, b, NT, preferred_element_type=F32)


def _dot_tn(a, b):
    return lax.dot_general(a, b, TN, preferred_element_type=F32)


def _swiglu_up(x, g, wup_ref, ab_ref, hid_ref):
    n = (x * _rstd(x) * g).astype(BF16)
    for c0, w in FF_CHUNKS:
        a = _dot(n, wup_ref[:, c0:c0 + w])
        b = _dot(n, wup_ref[:, FF + c0:FF + c0 + w])
        sig = _sigmoid(a)
        s = a * sig
        ab_ref[:, c0:c0 + w] = (b * (sig + s * (1.0 - sig))).astype(BF16)
        ab_ref[:, FF + c0:FF + c0 + w] = s.astype(BF16)
        hid_ref[:, c0:c0 + w] = (s * b).astype(BF16)


def ffn_up(h, g, wup, name):
    T = h.shape[0]
    TM = 512
    tile = lambda w: pl.BlockSpec((TM, w), lambda i: (i, 0))

    def body(h_ref, g_ref, wup_ref, ab_ref, hid_ref):
        _swiglu_up(h_ref[...], g_ref[...], wup_ref, ab_ref, hid_ref)

    return pl.pallas_call(
        body, name=name, grid=(T // TM,),
        in_specs=[tile(D), _const_spec((1, D)), _const_spec((D, 2 * FF))],
        out_specs=[tile(2 * FF), tile(FF)],
        out_shape=[SDS((T, 2 * FF), BF16), SDS((T, FF), BF16)],
        compiler_params=_cp(("arbitrary",)),
    )(h, g, wup)


def ffn_down(h, hid, wdn, name, after=None):
    T = h.shape[0]
    TM = min(1024, T)
    tile = lambda w: pl.BlockSpec((TM, w), lambda i: (i, 0))
    token_spec, token_arg = _token_operand(after)

    def body(h_ref, hid_ref, wdn_ref, *rest):
        rest[-1][...] = h_ref[...] + 0.5 * _dot(hid_ref[...], wdn_ref[...])

    return pl.pallas_call(
        body, name=name, grid=(T // TM,),
        in_specs=[tile(D), tile(FF), _const_spec((FF, D))] + token_spec,
        out_specs=tile(D), out_shape=SDS((T, D), F32),
        compiler_params=_cp(("arbitrary",)),
    )(h, hid, wdn, *token_arg)


def ffn_fwd(h, g, wup, wdn, name, head=None):
    T = h.shape[0]
    TM = 512
    tile = lambda w: pl.BlockSpec((TM, w), lambda i: (i, 0))
    acc_spec = lambda w: pl.BlockSpec((1, w), lambda i: (0, 0))

    def body(h_ref, g_ref, wup_ref, wdn_ref, *rest):
        out_ref, ab_ref, hid_ref = rest[-3:] if head is None else rest[2:5]
        x = h_ref[...]
        _swiglu_up(x, g_ref[...], wup_ref, ab_ref, hid_ref)
        out = x + 0.5 * _dot(hid_ref[...], wdn_ref[...])
        if head is None:
            out_ref[...] = out
        else:
            t_ref, gf_ref, loss_ref, dgf_ref = rest[0], rest[1], rest[5], rest[6]
            out_ref[...] = _loss_head(out, t_ref[...], gf_ref[...], loss_ref, dgf_ref, pl.program_id(0) == 0)

    head_in, head_specs, head_out_specs, head_out_shape = [], [], [], []
    if head is not None:
        head_in, head_specs = list(head), [tile(D), _const_spec((1, D))]
        head_out_specs, head_out_shape = [acc_spec(1), acc_spec(D)], [SDS((1, 1), F32), SDS((1, D), F32)]
    return pl.pallas_call(
        body, name=name, grid=(T // TM,),
        in_specs=[tile(D), _const_spec((1, D)), _const_spec((D, 2 * FF)), _const_spec((FF, D))] + head_specs,
        out_specs=[tile(D), tile(2 * FF), tile(FF)] + head_out_specs,
        out_shape=[SDS((T, D), F32), SDS((T, 2 * FF), BF16), SDS((T, FF), BF16)] + head_out_shape,
        compiler_params=_cp(("arbitrary",)),
    )(h, g, wup, wdn, *head_in)


def _loss_head(x, target, g, loss_ref, dg_ref, first):
    @pl.when(first)
    def _():
        loss_ref[...] = jnp.zeros_like(loss_ref)
        dg_ref[...] = jnp.zeros_like(dg_ref)

    rstd = _rstd(x)
    xhat = x * rstd
    err = xhat * g - target
    loss_ref[...] += 0.5 * jnp.sum(jnp.mean(err * err, axis=-1, keepdims=True), axis=0, keepdims=True)
    dy = err * (1.0 / D)
    dg_ref[...] += jnp.sum(dy * xhat, axis=0, keepdims=True)
    return _rms_bwd(dy, xhat, rstd, g)


def _adamw_tile(w_ref, g_ref, m_ref, v_ref, go_ref, d_ref, nm_ref, nv_ref):
    gv = g_ref[...]
    go_ref[...] = gv
    nm = B1 * m_ref[...] + (1.0 - B1) * gv
    nv = B2 * v_ref[...] + (1.0 - B2) * (gv * gv)
    nm_ref[...] = nm
    nv_ref[...] = nv
    d_ref[...] = -LR * ((nm / (1.0 - B1 ** STEP)) / (jnp.sqrt(nv / (1.0 - B2 ** STEP)) + ADAM_EPS) + WD * w_ref[...])


def _riders(updates, steps):
    in_specs, out_specs, out_shapes, operands, tiles = [], [], [], [], []
    for w, g, m, v in updates:
        R, C = w.shape
        n = max(d for d in range(1, steps + 1) if R % d == 0 and (R // d) % 8 == 0)
        spec = pl.BlockSpec((R // n, C), lambda i, n=n: (jnp.minimum(i, n - 1), 0))
        in_specs += [spec] * 4
        out_specs += [spec] * 4
        out_shapes += [SDS((R, C), F32)] * 4
        operands += [w, g, m, v]
        tiles.append(n)

    def run(step, in_refs, out_refs):
        for u, n in enumerate(tiles):
            @pl.when(step < n)
            def _(u=u):
                _adamw_tile(*in_refs[4 * u:4 * u + 4], *out_refs[4 * u:4 * u + 4])

    return in_specs, out_specs, out_shapes, operands, run


def ffn_bwd_x(dh, h_in, ab, g, wup, wdn, name, updates=()):
    T = dh.shape[0]
    TM = 256 if updates else 512
    SUB = 256
    r_in, r_out, r_shapes, r_args, ride = _riders(updates, T // TM)

    def body(dh_ref, h_ref, ab_ref, g_ref, wup_ref, wdn_ref, *rest):
        dhin_ref, dup_ref, n_ref, dg_ref = rest[len(r_in):len(r_in) + 4]
        ride(pl.program_id(0), rest[:len(r_in)], rest[len(r_in) + 4:])
        g = g_ref[...]
        dg = None
        for r0 in range(0, TM, SUB):
            rows = slice(r0, r0 + SUB)
            x = h_ref[rows, :]
            rstd = _rstd(x)
            xhat = x * rstd
            n_ref[rows, :] = (xhat * g).astype(BF16)
            dh = dh_ref[rows, :]
            dhh = (0.5 * dh).astype(BF16)
            for c0, w in FF_CHUNKS:
                dhid = _dot_nt(dhh, wdn_ref[c0:c0 + w, :])
                dup_ref[rows, c0:c0 + w] = (dhid * ab_ref[rows, c0:c0 + w].astype(F32)).astype(BF16)
                dup_ref[rows, FF + c0:FF + c0 + w] = (dhid * ab_ref[rows, FF + c0:FF + c0 + w].astype(F32)).astype(BF16)
            dn = _dot_nt(dup_ref[rows, :], wup_ref[...])
            dhin_ref[rows, :] = dh + _rms_bwd(dn, xhat, rstd, g)
            part = jnp.sum(dn * xhat, axis=0, keepdims=True)
            dg = part if dg is None else dg + part

        @pl.when(pl.program_id(0) == 0)
        def _():
            dg_ref[...] = jnp.zeros_like(dg_ref)

        dg_ref[...] += dg

    tile = lambda w: pl.BlockSpec((TM, w), lambda i: (i, 0))
    return pl.pallas_call(
        body, name=name, grid=(T // TM,),
        in_specs=[tile(D), tile(D), tile(2 * FF), _const_spec((1, D)), _const_spec((D, 2 * FF)), _const_spec((FF, D))]
        + r_in,
        out_specs=[tile(D), tile(2 * FF), tile(D), pl.BlockSpec((1, D), lambda i: (0, 0))] + r_out,
        out_shape=[SDS((T, D), F32), SDS((T, 2 * FF), BF16), SDS((T, D), BF16), SDS((1, D), F32)] + r_shapes,
        compiler_params=_cp(("arbitrary",)),
    )(dh, h_in, ab, g, wup, wdn, *r_args)


TOKEN_SPEC = pl.BlockSpec((8, 128), lambda *_: (0, 0))


def _token_operand(token):
    return ([], []) if token is None else ([TOKEN_SPEC], [token])


def matmul_tn(a, b, name, *, tm, tn, tt=1024, b_scale=None, after=None):
    T, M = a.shape
    N = b.shape[1]
    tt = min(tt, T)
    assert M % tm == 0 and N % tn == 0 and T % tt == 0
    nt = T // tt
    token_spec, token_arg = _token_operand(after)

    def body(a_ref, b_ref, *rest):
        o_ref, acc_ref = rest[-2:]
        t = pl.program_id(2)

        @pl.when(t == 0)
        def _():
            acc_ref[...] = jnp.zeros_like(acc_ref)

        bv = b_ref[...]
        if b_scale is not None:
            bv = bv * b_scale
        acc_ref[...] += _dot_tn(a_ref[...].astype(BF16), bv.astype(BF16))

        @pl.when(t == nt - 1)
        def _():
            o_ref[...] = acc_ref[...].astype(BF16)

    return pl.pallas_call(
        body, name=name, grid=(M // tm, N // tn, nt),
        in_specs=[pl.BlockSpec((tt, tm), lambda i, j, t: (t, i)), pl.BlockSpec((tt, tn), lambda i, j, t: (t, j))]
        + token_spec,
        out_specs=pl.BlockSpec((tm, tn), lambda i, j, t: (i, j)),
        out_shape=SDS((M, N), BF16),
        scratch_shapes=[pltpu.VMEM((tm, tn), F32)],
        compiler_params=_cp(("parallel", "parallel", "arbitrary")),
    )(a, b, *token_arg)


def mix_in_fwd(h1, g, win_t):
    T = h1.shape[0]
    TM = 512

    def body(h_ref, g_ref, w_ref, u_ref, q_ref, kv_ref, z_ref, gt_ref):
        x = h_ref[...]
        u = (x * _rstd(x) * g_ref[...]).astype(BF16)
        u_ref[...] = u
        for c in range(0, AW, 256):
            q_ref[:, c:c + 256] = _dot_nt(u, w_ref[c:c + 256, :]).astype(BF16)
        kv_ref[...] = _dot_nt(u, w_ref[C_KV:C_Z, :]).astype(BF16)
        for c in range(0, PW, 256):
            z_ref[:, c:c + 256] = _dot_nt(u, w_ref[C_Z + c:C_Z + c + 256, :])
        for c in range(0, 2 * D, 256):
            gt_ref[:, c:c + 256] = _dot_nt(u, w_ref[C_G + c:C_G + c + 256, :]).astype(BF16)

    tile = lambda w: pl.BlockSpec((TM, w), lambda i: (i, 0))
    return pl.pallas_call(
        body, name="mix_in_fwd", grid=(T // TM,),
        in_specs=[tile(D), _const_spec((1, D)), _const_spec((INW, D))],
        out_specs=[tile(D), tile(AW), tile(2 * KVW), tile(PW), tile(2 * D)],
        out_shape=[SDS((T, D), BF16), SDS((T, AW), BF16), SDS((T, 2 * KVW), BF16), SDS((T, PW), F32),
                   SDS((T, 2 * D), BF16)],
        compiler_params=_cp(("arbitrary",)),
    )(h1, g, win_t)


def mix_in_bwd(dq, dkv, dz, dgt, dh2, h1, g, win_t):
    T = h1.shape[0]
    TM = min(1024, T)
    SUB = 256

    def body(dq_ref, dkv_ref, dz_ref, dgt_ref, dh2_ref, h_ref, g_ref, w_ref, dh1_ref, dg_ref):
        g = g_ref[...]
        dg = None
        for r0 in range(0, TM, SUB):
            rows = slice(r0, r0 + SUB)
            du = _dot(dq_ref[rows, :], w_ref[0:AW, :])
            du += _dot(dkv_ref[rows, :], w_ref[C_KV:C_Z, :])
            du += _dot(dz_ref[rows, :], w_ref[C_Z:C_G, :])
            du += _dot(dgt_ref[rows, :], w_ref[C_G:INW, :])
            x = h_ref[rows, :]
            rstd = _rstd(x)
            xhat = x * rstd
            dh1_ref[rows, :] = dh2_ref[rows, :] + _rms_bwd(du, xhat, rstd, g)
            part = jnp.sum(du * xhat, axis=0, keepdims=True)
            dg = part if dg is None else dg + part

        @pl.when(pl.program_id(0) == 0)
        def _():
            dg_ref[...] = jnp.zeros_like(dg_ref)

        dg_ref[...] += dg

    tile = lambda w: pl.BlockSpec((TM, w), lambda i: (i, 0))
    return pl.pallas_call(
        body, name="mix_in_bwd", grid=(T // TM,),
        in_specs=[tile(AW), tile(2 * KVW), tile(PW), tile(2 * D), tile(D), tile(D), _const_spec((1, D)),
                  _const_spec((INW, D))],
        out_specs=[tile(D), pl.BlockSpec((1, D), lambda i: (0, 0))],
        out_shape=[SDS((T, D), F32), SDS((1, D), F32)],
        compiler_params=_cp(("arbitrary",)),
    )(dq, dkv, dz, dgt, dh2, h1, g, win_t)


PAIR = 2 * HD
NPAIR = GQ // 2


def _lo_lanes():
    return lax.broadcasted_iota(jnp.int32, (BLK, PAIR), 1) < HD


def _stack_heads(ref, kvh, scale=None):
    lo = _lo_lanes()
    parts = []
    for pr in range(NPAIR):
        t = ref[:, (kvh * NPAIR + pr) * PAIR:(kvh * NPAIR + pr + 1) * PAIR]
        if scale is not None:
            t = t * scale
        zero = jnp.zeros_like(t)
        parts += [jnp.where(lo, t, zero), jnp.where(lo, zero, t)]
    return jnp.concatenate(parts, axis=0)


def _kv_tiles(kvc_ref, kvp_ref, tile, kvh):
    lo = _lo_lanes()
    dup, left, right = [], [], []
    for ref in (kvp_ref, kvc_ref):
        t = ref[:, tile * PAIR:(tile + 1) * PAIR]
        r = pltpu.roll(t.astype(F32), HD, 1).astype(BF16)
        zero = jnp.zeros_like(t)
        a, b = (t, r) if kvh == 0 else (r, t)
        dup.append(jnp.where(lo, a, b))
        left.append(jnp.where(lo, a, zero))
        right.append(jnp.where(lo, zero, b))
    cat = lambda xs: jnp.concatenate(xs, axis=0)
    return cat(dup), cat(left), cat(right)


def _band_consts(first):
    row = lax.broadcasted_iota(jnp.int32, (BLK, BLK), 0)
    col = lax.broadcasted_iota(jnp.int32, (BLK, BLK), 1)
    upper = col > row
    dist = jnp.where(upper, row - col + BLK, row - col).astype(F32)
    pen = jnp.where(jnp.logical_and(upper, first), -jnp.inf, 0.0)
    return upper, dist, pen


def _split_band(upper, t):
    zero = jnp.zeros_like(t)
    return jnp.concatenate([jnp.where(upper, t, zero), jnp.where(upper, zero, t)], axis=1)


def attn_fwd(q, kv, sinks):
    T = q.shape[0]
    nb = T // BLK

    def body(sink_ref, q_ref, kvc_ref, kvp_ref, att_ref, lse_ref):
        upper, dist, pen = _band_consts(pl.program_id(0) == 0)
        scores, values = [], []
        for kvh in range(NKV):
            kdup, _, _ = _kv_tiles(kvc_ref, kvp_ref, 0, kvh)
            values.append(_kv_tiles(kvc_ref, kvp_ref, 1, kvh)[1:])
            scores.append(_dot_nt(_stack_heads(q_ref, kvh, SCALE), kdup))
        for kvh in range(NKV):
            s_all = scores[kvh]
            vleft, vright = values[kvh]
            for pr in range(NPAIR):
                outs, inv = [], []
                for side, vpad in ((0, vleft), (1, vright)):
                    g = 2 * pr + side
                    hq = kvh * GQ + g
                    sink = sink_ref[0, hq]
                    rows = slice(g * BLK, (g + 1) * BLK)
                    s = jnp.where(upper, s_all[rows, 0:BLK], s_all[rows, BLK:2 * BLK]) - SLOPES[hq] * dist + pen
                    m = jnp.maximum(jnp.max(s, axis=-1, keepdims=True), sink)
                    p = jnp.exp(s - m)
                    l = jnp.sum(p, axis=-1, keepdims=True) + jnp.exp(sink - m)
                    lse_ref[:, hq:hq + 1] = m + jnp.log(l)
                    outs.append(_dot(_split_band(upper, p.astype(BF16)), vpad))
                    inv.append(1.0 / l)
                col0 = (kvh * NPAIR + pr) * PAIR
                att_ref[:, col0:col0 + PAIR] = ((outs[0] + outs[1]) * jnp.where(_lo_lanes(), inv[0], inv[1])).astype(BF16)

    return pl.pallas_call(
        body, name="attn_fwd", grid=(nb,),
        in_specs=[pl.BlockSpec(memory_space=pltpu.SMEM),
                  pl.BlockSpec((BLK, AW), lambda i: (i, 0)),
                  pl.BlockSpec((BLK, 2 * KVW), lambda i: (i, 0)),
                  pl.BlockSpec((BLK, 2 * KVW), lambda i: (jnp.maximum(i - 1, 0), 0))],
        out_specs=[pl.BlockSpec((BLK, AW), lambda i: (i, 0)), pl.BlockSpec((BLK, NQ), lambda i: (i, 0))],
        out_shape=[SDS((T, AW), BF16), SDS((T, NQ), F32)],
        compiler_params=_cp(("arbitrary",)),
    )(sinks, q, kv, kv)


def attn_bwd(q, kv, datt, att, lse, sinks):
    T = q.shape[0]
    nb = T // BLK

    def body(sink_ref, q_ref, kvc_ref, kvp_ref, do_ref, out_ref, lse_ref, dq_ref, dkv_ref, dsink_ref, carry_ref):
        i = pl.program_id(0)

        @pl.when(i == 0)
        def _():
            dsink_ref[...] = jnp.zeros_like(dsink_ref)
            carry_ref[...] = jnp.zeros_like(carry_ref)

        @pl.when(i < nb)
        def _():
            upper, dist, pen = _band_consts(i == 0)
            lo = _lo_lanes()
            dk_dup, dv_dup = [], []
            staged = []
            for kvh in range(NKV):
                kdup, kleft, kright = _kv_tiles(kvc_ref, kvp_ref, 0, kvh)
                vdup, _, _ = _kv_tiles(kvc_ref, kvp_ref, 1, kvh)
                qs = _stack_heads(q_ref, kvh, SCALE)
                dos = _stack_heads(do_ref, kvh)
                staged.append((kleft, kright, qs, dos, _dot_nt(qs, kdup), _dot_nt(dos, vdup)))
            deltas = []
            for pair in range(NQ // 2):
                cols = slice(pair * PAIR, (pair + 1) * PAIR)
                t = do_ref[:, cols].astype(F32) * out_ref[:, cols].astype(F32)
                deltas += [jnp.sum(jnp.where(lo, t, 0.0), axis=-1, keepdims=True),
                           jnp.sum(jnp.where(lo, 0.0, t), axis=-1, keepdims=True)]
            for kvh in range(NKV):
                kleft, kright, qs, dos, s_all, dp_all = staged[kvh]
                ds_parts, p_parts = [], []
                for pr in range(NPAIR):
                    dq = None
                    for side, kpad in ((0, kleft), (1, kright)):
                        g = 2 * pr + side
                        hq = kvh * GQ + g
                        lse_h = lse_ref[:, hq:hq + 1]
                        rows = slice(g * BLK, (g + 1) * BLK)
                        s = jnp.where(upper, s_all[rows, 0:BLK], s_all[rows, BLK:2 * BLK]) - SLOPES[hq] * dist + pen
                        p = jnp.exp(s - lse_h)
                        dp = jnp.where(upper, dp_all[rows, 0:BLK], dp_all[rows, BLK:2 * BLK])
                        delta = deltas[hq]
                        dsink_ref[:, hq:hq + 1] += -jnp.sum(jnp.exp(sink_ref[0, hq] - lse_h) * delta, axis=0,
                                                            keepdims=True)
                        ds = _split_band(upper, (p * (dp - delta)).astype(BF16))
                        ds_parts.append(ds)
                        p_parts.append(_split_band(upper, p.astype(BF16)))
                        d = _dot(ds, kpad)
                        dq = d if dq is None else dq + d
                    col0 = (kvh * NPAIR + pr) * PAIR
                    dq_ref[:, col0:col0 + PAIR] = (dq * SCALE).astype(BF16)
                dkw = _dot_tn(qs, jnp.concatenate(ds_parts, axis=0)).T
                dvw = _dot_tn(dos, jnp.concatenate(p_parts, axis=0)).T
                dk_dup.append(dkw + pltpu.roll(dkw, HD, 1))
                dv_dup.append(dvw + pltpu.roll(dvw, HD, 1))
            dk = jnp.where(jnp.concatenate([lo, lo], axis=0), dk_dup[0], dk_dup[1])
            dv = jnp.where(jnp.concatenate([lo, lo], axis=0), dv_dup[0], dv_dup[1])
            dkv_ref[:, 0:PAIR] = (carry_ref[:, 0:PAIR] + dk[0:BLK]).astype(BF16)
            dkv_ref[:, PAIR:2 * PAIR] = (carry_ref[:, PAIR:2 * PAIR] + dv[0:BLK]).astype(BF16)
            carry_ref[:, 0:PAIR] = dk[BLK:2 * BLK]
            carry_ref[:, PAIR:2 * PAIR] = dv[BLK:2 * BLK]

        @pl.when(i == nb)
        def _():
            dkv_ref[...] = carry_ref[...].astype(BF16)

    cur = lambda i: (jnp.minimum(i, nb - 1), 0)
    prev = lambda i: (jnp.maximum(jnp.minimum(i, nb - 1) - 1, 0), 0)
    return pl.pallas_call(
        body, name="attn_bwd", grid=(nb + 1,),
        in_specs=[pl.BlockSpec(memory_space=pltpu.SMEM),
                  pl.BlockSpec((BLK, AW), cur), pl.BlockSpec((BLK, 2 * KVW), cur), pl.BlockSpec((BLK, 2 * KVW), prev),
                  pl.BlockSpec((BLK, AW), cur), pl.BlockSpec((BLK, AW), cur), pl.BlockSpec((BLK, NQ), cur)],
        out_specs=[pl.BlockSpec((BLK, AW), cur),
                   pl.BlockSpec((BLK, 2 * KVW), lambda i: (jnp.maximum(i - 1, 0), 0)),
                   pl.BlockSpec((1, NQ), lambda i: (0, 0))],
        out_shape=[SDS((T, AW), BF16), SDS((T, 2 * KVW), BF16), SDS((1, NQ), F32)],
        scratch_shapes=[pltpu.VMEM((BLK, 2 * KVW), F32)],
        compiler_params=_cp(("arbitrary",)),
    )(sinks, q, kv, kv, datt, att, lse)


def _inv_counts(t0, rows):
    t = (t0 + lax.broadcasted_iota(jnp.int32, (rows, 1), 0) + 1).astype(F32)
    return [1.0 / jnp.minimum(t, float(w)) for w in POOL_WINDOWS]


def pool_fwd(z, wmix, scale):
    T = z.shape[0]
    TM = min(1024, T)
    L = TM + HALO

    def body(z_ref, halo_ref, wmix_ref, scale_ref, pooled_ref, mixs_ref):
        i = pl.program_id(0)
        halo = jnp.where(i > 0, halo_ref[...], 0.0)
        zt = z_ref[...]
        e = jnp.concatenate([halo, zt], axis=0)
        sums = []
        s = e
        for k in (1, 2, 4, 8):
            s = s + pltpu.roll(s, k, 0)
            sums.append(s)
        inv = _inv_counts(i * TM, TM)
        for gi in range(len(POOL_WINDOWS)):
            cols = slice(gi * PG, (gi + 1) * PG)
            pooled = (sums[gi][HALO:, cols] * inv[gi] - zt[:, cols]).astype(BF16)
            pooled_ref[:, cols] = pooled
            mixs_ref[:, cols] = (_dot(pooled, wmix_ref[gi]) * scale_ref[:, cols]).astype(BF16)

    return pl.pallas_call(
        body, name="pool_fwd", grid=(T // TM,),
        in_specs=[pl.BlockSpec((TM, PW), lambda i: (i, 0)),
                  pl.BlockSpec((HALO, PW), lambda i: (jnp.maximum(i * (TM // HALO) - 1, 0), 0)),
                  _const_spec((len(POOL_WINDOWS), PG, PG)), _const_spec((1, PW))],
        out_specs=[pl.BlockSpec((TM, PW), lambda i: (i, 0)), pl.BlockSpec((TM, PW), lambda i: (i, 0))],
        out_shape=[SDS((T, PW), BF16), SDS((T, PW), BF16)],
        compiler_params=_cp(("arbitrary",)),
    )(z, z, wmix, scale)


def pool_bwd(dmixs, pooled, wmix, scale):
    T = dmixs.shape[0]
    TM = min(1024, T)
    L = TM + HALO
    nt = T // TM

    def body(dm_ref, halo_ref, pooled_ref, wmix_ref, scale_ref, dz_ref, dwmix_ref, dscale_ref):
        i = pl.program_id(0)

        @pl.when(i == 0)
        def _():
            dwmix_ref[...] = jnp.zeros_like(dwmix_ref)
            dscale_ref[...] = jnp.zeros_like(dscale_ref)

        halo = jnp.where(i < nt - 1, halo_ref[...], 0.0)
        dm = dm_ref[...]
        e = jnp.concatenate([dm, halo], axis=0)
        inv = _inv_counts(i * TM, L)
        for gi in range(len(POOL_WINDOWS)):
            cols = slice(gi * PG, (gi + 1) * PG)
            w = wmix_ref[gi]
            dmixed = (e[:, cols] * scale_ref[:, cols]).astype(BF16)
            dpooled = _dot_nt(dmixed, w)
            pooled = pooled_ref[:, cols]
            mixed = _dot(pooled, w)
            dscale_ref[:, cols] += jnp.sum(dm[:, cols] * mixed, axis=0, keepdims=True)
            dwmix_ref[gi] += _dot_tn(pooled, dmixed[:TM])
            s = dpooled * inv[gi]
            k = 1
            while k < POOL_WINDOWS[gi]:
                s = s + pltpu.roll(s, L - k, 0)
                k *= 2
            dz_ref[:, cols] = (s[:TM] - dpooled[:TM]).astype(BF16)

    return pl.pallas_call(
        body, name="pool_bwd", grid=(nt,),
        in_specs=[pl.BlockSpec((TM, PW), lambda i: (i, 0)),
                  pl.BlockSpec((HALO, PW), lambda i: (jnp.minimum((i + 1) * (TM // HALO), T // HALO - 1), 0)),
                  pl.BlockSpec((TM, PW), lambda i: (i, 0)),
                  _const_spec((len(POOL_WINDOWS), PG, PG)), _const_spec((1, PW))],
        out_specs=[pl.BlockSpec((TM, PW), lambda i: (i, 0)),
                   pl.BlockSpec((len(POOL_WINDOWS), PG, PG), lambda i: (0, 0, 0)),
                   pl.BlockSpec((1, PW), lambda i: (0, 0))],
        out_shape=[SDS((T, PW), BF16), SDS((len(POOL_WINDOWS), PG, PG), F32), SDS((1, PW), F32)],
        compiler_params=_cp(("arbitrary",)),
    )(dmixs, dmixs, pooled, wmix, scale)


def merge_fwd(att, mixs, gt, h1, wattn, wpool, wout):
    T = h1.shape[0]
    TM = 512

    def body(att_ref, mixs_ref, gt_ref, h_ref, wa_ref, wp_ref, wo_ref, h2_ref, mg_ref, gf_ref):
        a = _dot(att_ref[...], wa_ref[...])
        p = _dot(mixs_ref[...], wp_ref[...])
        sa = _sigmoid(gt_ref[:, 0:D].astype(F32))
        sp = _sigmoid(gt_ref[:, D:2 * D].astype(F32))
        gf_ref[:, 0:D] = (a * sa * (1.0 - sa)).astype(BF16)
        gf_ref[:, D:2 * D] = (p * sp * (1.0 - sp)).astype(BF16)
        mg = (sa * a + sp * p).astype(BF16)
        mg_ref[...] = mg
        h2_ref[...] = h_ref[...] + _dot(mg, wo_ref[...])

    tile = lambda w: pl.BlockSpec((TM, w), lambda i: (i, 0))
    return pl.pallas_call(
        body, name="merge_fwd", grid=(T // TM,),
        in_specs=[tile(AW), tile(PW), tile(2 * D), tile(D), _const_spec((AW, D)), _const_spec((PW, D)),
                  _const_spec((D, D))],
        out_specs=[tile(D), tile(D), tile(2 * D)],
        out_shape=[SDS((T, D), F32), SDS((T, D), BF16), SDS((T, 2 * D), BF16)],
        compiler_params=_cp(("arbitrary",)),
    )(att, mixs, gt, h1, wattn, wpool, wout)


def merge_bwd(dh2, gt, gf, wattn, wpool, wout, after=None):
    T = dh2.shape[0]
    TM = 512
    token_spec, token_arg = _token_operand(after)

    def body(dh2_ref, gt_ref, gf_ref, wa_ref, wp_ref, wo_ref, *rest):
        datt_ref, dmixs_ref, dgt_ref, da_ref, dp_ref = rest[-5:]
        dm = _dot_nt(dh2_ref[...].astype(BF16), wo_ref[...])
        da = (dm * _sigmoid(gt_ref[:, 0:D].astype(F32))).astype(BF16)
        dp = (dm * _sigmoid(gt_ref[:, D:2 * D].astype(F32))).astype(BF16)
        da_ref[...] = da
        dp_ref[...] = dp
        dgt_ref[:, 0:D] = (dm * gf_ref[:, 0:D].astype(F32)).astype(BF16)
        dgt_ref[:, D:2 * D] = (dm * gf_ref[:, D:2 * D].astype(F32)).astype(BF16)
        datt_ref[...] = _dot_nt(da, wa_ref[...]).astype(BF16)
        dmixs_ref[...] = _dot_nt(dp, wp_ref[...])

    tile = lambda w: pl.BlockSpec((TM, w), lambda i: (i, 0))
    return pl.pallas_call(
        body, name="merge_bwd", grid=(T // TM,),
        in_specs=[tile(D), tile(2 * D), tile(2 * D), _const_spec((AW, D)), _const_spec((PW, D)),
                  _const_spec((D, D))] + token_spec,
        out_specs=[tile(AW), tile(PW), tile(2 * D), tile(D), tile(D)],
        out_shape=[SDS((T, AW), BF16), SDS((T, PW), F32), SDS((T, 2 * D), BF16), SDS((T, D), BF16),
                   SDS((T, D), BF16)],
        compiler_params=_cp(("arbitrary",)),
    )(dh2, gt, gf, wattn, wpool, wout, *token_arg)


def adamw(updates, place, name, after=None):
    tile_bytes = 2 * 1024 * 1024 // len(updates)
    jobs = []
    for w, g, m, v in updates:
        R, C = w.shape
        tr = R
        if R * C * 4 > tile_bytes:
            tr = next(cand for cand in (512, 256, 128, 64, 32, 16, 8) if R % cand == 0 and cand * C * 4 <= tile_bytes)
        blk = ((tr, C), lambda ls, p: (ls, 0))
        jobs.append((R // tr, [(a, *blk) for a in (w, g, m, v)], [((R, C), F32, *blk)] * 4,
                     lambda ins, outs: _adamw_tile(*ins, *outs)))
    return _run_jobs(jobs, place, name, after)


GROUP_UP1, GROUP_MIX, GROUP_FFN2 = 0, 1, 2


def _behind(small, token):
    return small if token is None else small + token[0:1, 0:1]


def local_fwd_bwd(x, target, S, comm):
    w_up1 = comm.weights(GROUP_UP1, None)["ffn1_w_up"]
    ab1, hid1 = ffn_up(x, _behind(S["ffn1_norm"], comm.started()), w_up1, "ffn1_up")
    W = comm.weights(GROUP_MIX, hid1)
    h1 = ffn_down(x, hid1, W["ffn1_w_down"], "ffn1_down", after=comm.started())
    u, q, kv, z, gt = mix_in_fwd(h1, S["mix_norm"], W["w_in"])
    att, lse = attn_fwd(q, kv, S["sinks"])
    pooled, mixs = pool_fwd(z, S["pool_w_mix"], _behind(S["pool_scale"], comm.prefetch(GROUP_FFN2, att)))
    h2, merged, gf = merge_fwd(att, mixs, gt, h1, W["w_attn_up"], W["w_pool_up"], W["w_out"])
    W = comm.weights(GROUP_FFN2, h2)
    dh3, ab2, hid2, loss, g_final = ffn_fwd(h2, S["ffn2_norm"], W["ffn2_w_up"], W["ffn2_w_down"], "ffn2_fwd",
                                            head=(target, S["final_norm"]))

    G = {"final_norm": g_final}
    dh2, dup2, n2, G["ffn2_norm"] = ffn_bwd_x(dh3, h2, ab2, S["ffn2_norm"], W["ffn2_w_up"], W["ffn2_w_down"],
                                              "ffn2_bwd_x")
    ffn2 = ("ffn2_w_down", "ffn2_w_up")
    token = comm.grads({"ffn2_w_down": matmul_tn(hid2, dh3, "ffn2_dw_down", tm=1408, tn=512, tt=4096, b_scale=0.5),
                        "ffn2_w_up": matmul_tn(n2, dup2, "ffn2_dw_up", tm=D, tn=512, tt=4096)})

    W = comm.weights(GROUP_MIX, None)
    datt, dmixs, dgt, da, dp = merge_bwd(dh2, gt, gf, W["w_attn_up"], W["w_pool_up"], W["w_out"], after=token)
    token = comm.advance(ffn2, datt)
    g_mix = {"w_out": matmul_tn(merged, dh2, "dw_out", tm=D, tn=D, after=token),
             "w_attn_up": matmul_tn(att, da, "dw_attn_up", tm=AW, tn=D),
             "w_pool_up": matmul_tn(mixs, dp, "dw_pool_up", tm=PW, tn=D)}
    dz, G["pool_w_mix"], G["pool_scale"] = pool_bwd(dmixs, pooled, S["pool_w_mix"], _behind(S["pool_scale"], token))
    dq, dkv, G["sinks"] = attn_bwd(q, kv, datt, att, lse, _behind(S["sinks"], token))
    dh1, G["mix_norm"] = mix_in_bwd(dq, dkv, dz, dgt, dh2, h1, S["mix_norm"], W["w_in"])
    g_mix["w_in"] = jnp.concatenate([
        matmul_tn(dq, u, "dw_in_q", tm=AW, tn=D),
        matmul_tn(dkv, u, "dw_in_kv", tm=2 * KVW, tn=D),
        matmul_tn(dz, u, "dw_in_z", tm=PW, tn=D),
        matmul_tn(dgt, u, "dw_in_g", tm=D, tn=512, tt=4096),
    ], axis=0)
    mix = tuple(g_mix)
    token = comm.grads(g_mix)

    g_dn1 = matmul_tn(hid1, dh1, "ffn1_dw_down", tm=1408, tn=512, tt=4096, b_scale=0.5, after=token)
    token = comm.advance(mix, g_dn1)
    token2 = comm.grads({"ffn1_w_down": g_dn1})
    g1 = _behind(_behind(S["ffn1_norm"], token), token2)
    dx, dup1, n1, G["ffn1_norm"], *outs = ffn_bwd_x(dh1, x, ab1, g1, w_up1, W["ffn1_w_down"], "ffn1_bwd_x",
                                                   updates=comm.reduce(ffn2, token2))
    comm.updated(ffn2, outs)
    token = comm.advance(("ffn1_w_down",), dx)
    token2 = comm.small_start(G, loss, dx)
    g_up1 = matmul_tn(n1, dup1, "ffn1_dw_up", tm=D, tn=512, tt=4096,
                      after=None if token is None else token + token2)
    token = comm.grads({"ffn1_w_up": g_up1})
    updates = comm.reduce(("ffn1_w_down",), token)
    token = comm.advance(("ffn1_w_up",), [u[1] for u in updates])
    done = comm.update(("ffn1_w_down",), updates, token)
    done = comm.finish(mix, done)
    done = comm.small_finish(done)
    comm.finish(("ffn1_w_up",), done)
    return loss, dx


HBM_SPEC = pl.BlockSpec(memory_space=pltpu.HBM)


def _me():
    return lax.axis_index("x"), lax.axis_index("y"), lax.axis_index("c")


def _peer_chip(x, y, k):
    return x ^ (k >> 1), y ^ (k & 1)


def _piece_half(ref, rowlike, j, h):
    if rowlike:
        return ref.at[j, h]
    ns = ref.shape[-1] // N_CHIPS
    return ref.at[h, :, pl.ds(pl.multiple_of(j * ns, 128), ns)]


def _piece(ref, rowlike, j):
    if rowlike:
        return ref.at[j]
    ns = ref.shape[-1] // N_CHIPS
    return ref.at[:, :, pl.ds(pl.multiple_of(j * ns, 128), ns)]


def _full_shape(shard_view, rowlike):
    _, kh, ns = shard_view.shape
    return (N_CHIPS, 2, kh, ns) if rowlike else (2, kh, N_CHIPS * ns)


def _remote(src, dst, send_sem, recv_sem, dev):
    return pltpu.make_async_remote_copy(src, dst, send_sem, recv_sem, device_id=dev, device_id_type=MESH)


SEM_SPEC = pl.BlockSpec(memory_space=pltpu.SEMAPHORE)
SPLIT_PARAMS = pltpu.CompilerParams(has_side_effects=pltpu.SideEffectType.DATAFLOW_SIDE_EFFECTING)


def _gather_plan(rowlikes):
    def plan(s_refs, f_refs, a, k, x, y, c):
        px, py = _peer_chip(x, y, k)
        return (s_refs[a].at[c], _piece_half(f_refs[a], rowlikes[a], 2 * x + y, c),
                _piece_half(f_refs[a], rowlikes[a], 2 * px + py, c), (px, py, c))
    return plan


def _all_to_all_plan(rowlikes):
    def plan(q_refs, r_refs, a, k, x, y, c):
        px, py = _peer_chip(x, y, k)
        if rowlikes[a]:
            src = q_refs[a].at[2 * px + py]
        else:
            ns = q_refs[a].shape[-1] // N_CHIPS
            src = q_refs[a].at[:, pl.ds(pl.multiple_of((2 * px + py) * ns, 128), ns)]
        return src, r_refs[a].at[k - 1], r_refs[a].at[k - 1], (px, py, c)
    return plan


def _swap_plan(rowlikes):
    def plan(g_refs, got_refs, a, k, x, y, c):
        src = g_refs[a].at[:, 1 - c] if rowlikes[a] else g_refs[a].at[1 - c]
        return src, got_refs[a], got_refs[a], (x, y, 1 - c)
    return plan


def _everyone_plan(s_refs, slot_refs, a, k, x, y, c):
    px, py, pc = x ^ (k >> 2), y ^ ((k >> 1) & 1), c ^ (k & 1)
    return s_refs[a], slot_refs[a].at[4 * x + 2 * y + c], slot_refs[a].at[4 * px + 2 * py + pc], (px, py, pc)


def _forward_plan(rowlikes):
    def plan(s_refs, f_refs, a, k, x, y, c):
        sib = (x, y, 1 - c)
        if k == 0:
            own = _piece(f_refs[a], rowlikes[a], 2 * x + y)
            return s_refs[a], own, own, sib
        px, py = _peer_chip(x, y, k)
        mine = _piece_half(f_refs[a], rowlikes[a], 2 * px + py, c)
        return mine, mine, _piece_half(f_refs[a], rowlikes[a], 2 * px + py, 1 - c), sib
    return plan


CHIPS, SIBLING, EVERYONE, FORWARDS = (1, 2, 3), (1,), tuple(range(1, 8)), (0, 1, 2, 3)


def _as_list(after):
    return [] if after is None else list(after) if isinstance(after, (list, tuple)) else [after]


def exchange_start(srcs, land_shapes, plan, after, name, peers=CHIPS):
    n = len(srcs)
    lands = [l if hasattr(l, "dtype") else lax.empty(l, s.dtype) for l, s in zip(land_shapes, srcs)]

    behind = _as_list(after)

    def body(*refs):
        s_refs, l_refs = refs[:n], refs[n:2 * n]
        send_sems, recv_sems = refs[2 * n + len(behind)], refs[2 * n + len(behind) + 1]
        token = refs[-1]
        x, y, c = _me()
        for a in range(n):
            for i, k in enumerate(peers):
                src, dst, _, peer = plan(s_refs, l_refs, a, k, x, y, c)
                sem = len(peers) * a + i
                _remote(src, dst, send_sems.at[sem], recv_sems.at[sem], peer).start()
        token[...] = jnp.zeros_like(token)

    n_sems = len(peers) * n
    outs = pl.pallas_call(
        body, name=name, in_specs=[HBM_SPEC] * (2 * n) + [pl.BlockSpec(memory_space=pl.ANY)] * len(behind),
        out_specs=[SEM_SPEC, SEM_SPEC] + [HBM_SPEC] * (2 * n) + [pl.BlockSpec(memory_space=pltpu.VMEM)],
        out_shape=[pltpu.SemaphoreType.DMA((n_sems,)), pltpu.SemaphoreType.DMA((n_sems,))]
        + [pltpu.HBM(a.shape, a.dtype) for a in (*srcs, *lands)] + [SDS((8, 128), F32)],
        input_output_aliases={i: 2 + i for i in range(2 * n)},
        compiler_params=SPLIT_PARAMS,
    )(*[pltpu.with_memory_space_constraint(a, pltpu.HBM) for a in (*srcs, *lands)], *behind)
    return {"sems": outs[:2], "srcs": outs[2:2 + n], "lands": outs[2 + n:2 + 2 * n], "token": outs[-1]}


def exchange_wait(state, plan, after, name, peers=CHIPS):
    n = len(state["srcs"])
    behind = _as_list(after)

    def body(*refs):
        s_refs, l_refs = refs[:n], refs[n:2 * n]
        send_sems, recv_sems = refs[2 * n], refs[2 * n + 1]
        x, y, c = _me()
        for a in range(n):
            for i, k in enumerate(peers):
                src, _, landing, peer = plan(s_refs, l_refs, a, k, x, y, c)
                sem = len(peers) * a + i
                cp = _remote(src, landing, send_sems.at[sem], recv_sems.at[sem], peer)
                cp.wait_send()
                cp.wait_recv()

    bufs = (*state["srcs"], *state["lands"])
    outs = pl.pallas_call(
        body, name=name,
        in_specs=[HBM_SPEC] * (2 * n) + [SEM_SPEC, SEM_SPEC] + [pl.BlockSpec(memory_space=pl.ANY)] * len(behind),
        out_specs=[HBM_SPEC] * (2 * n),
        out_shape=[pltpu.HBM(a.shape, a.dtype) for a in bufs],
        input_output_aliases={i: i for i in range(2 * n)},
        compiler_params=SPLIT_PARAMS,
    )(*bufs, *state["sems"], *behind)
    return outs[:n], outs[n:]


def gather_finish(shards, fulls, rowlikes, name):
    n = len(shards)

    def body(*refs):
        s_refs, f_refs = refs[:n], refs[2 * n:3 * n]
        send_sems, recv_sems = refs[3 * n:]
        x, y, c = _me()
        chip = 2 * x + y
        sib = (x, y, 1 - c)
        sends = []
        for a in range(n):
            own = _piece(f_refs[a], rowlikes[a], chip)
            cp = _remote(s_refs[a], own, send_sems.at[a, 0], recv_sems.at[a, 0], sib)
            cp.start()
            sends.append(cp)
            for k in (1, 2, 3):
                px, py = _peer_chip(x, y, k)
                slot = _piece_half(f_refs[a], rowlikes[a], 2 * px + py, c)
                cp = _remote(slot, slot, send_sems.at[a, k], recv_sems.at[a, k], sib)
                cp.start()
                sends.append(cp)
        for a in range(n):
            own = _piece(f_refs[a], rowlikes[a], chip)
            _remote(own, own, send_sems.at[a, 0], recv_sems.at[a, 0], sib).wait_recv()
            for k in (1, 2, 3):
                px, py = _peer_chip(x, y, k)
                slot = _piece_half(f_refs[a], rowlikes[a], 2 * px + py, 1 - c)
                _remote(slot, slot, send_sems.at[a, k], recv_sems.at[a, k], sib).wait_recv()
        for cp in sends:
            cp.wait_send()

    return pl.pallas_call(
        body, name=name, in_specs=[HBM_SPEC] * (2 * n), out_specs=[HBM_SPEC] * n,
        out_shape=[SDS(f.shape, f.dtype) for f in fulls],
        input_output_aliases={n + a: a for a in range(n)},
        scratch_shapes=[pltpu.SemaphoreType.DMA((n, 4)), pltpu.SemaphoreType.DMA((n, 4))],
    )(*shards, *fulls)


def _half_buffer_shape(gview, rowlike):
    return (N_CHIPS,) + gview.shape[2:] if rowlike else gview.shape[1:]


def _row_tiles(kh, ns):
    return 1 if kh * ns <= 256 * 1024 else 2


def _run_jobs(jobs, place, name, after=None):
    n_steps = max(job[0] for job in jobs)

    def spec(block, index, steps):
        return pl.BlockSpec(block, lambda s, p: index(jnp.minimum(s, steps - 1), p))

    in_specs = [spec(b, ix, steps) for steps, ins, _, _ in jobs for _, b, ix in ins]
    out_specs = [spec(b, ix, steps) for steps, _, outs, _ in jobs for _, _, b, ix in outs]
    token_spec, token_arg = _token_operand(after)
    n_in = len(in_specs) + len(token_spec)

    def body(place_ref, *refs):
        s = pl.program_id(0)
        i, o = 0, n_in
        for steps, ins, outs, fn in jobs:
            in_refs, out_refs = refs[i:i + len(ins)], refs[o:o + len(outs)]
            i, o = i + len(ins), o + len(outs)

            @pl.when(s < steps)
            def _(fn=fn, in_refs=in_refs, out_refs=out_refs):
                fn(in_refs, out_refs)

    return pl.pallas_call(
        body, name=name,
        grid_spec=pltpu.PrefetchScalarGridSpec(num_scalar_prefetch=1, grid=(n_steps,),
                                               in_specs=in_specs + token_spec, out_specs=out_specs),
        out_shape=[SDS(shape, dtype) for _, _, outs, _ in jobs for shape, dtype, _, _ in outs],
        compiler_params=_cp(("arbitrary",)),
    )(place, *[a for _, ins, _, _ in jobs for a, _, _ in ins], *token_arg)


def add_halves(gviews, gots, rowlikes, place, name):
    def add(ins, outs):
        outs[0][...] = (ins[0][...].astype(F32) + ins[1][...].astype(F32)).astype(BF16)

    jobs = []
    for g, got, rowlike in zip(gviews, gots, rowlikes):
        kh, ns = (g.shape[2], g.shape[3]) if rowlike else (g.shape[1], g.shape[2] // N_CHIPS)
        rt = _row_tiles(kh, ns)
        tr = kh // rt
        if rowlike:
            g_in = (g, (None, None, tr, ns), lambda ls, p, rt=rt: (ls // rt, p[0], ls % rt, 0))
            half = ((None, tr, ns), lambda ls, p, rt=rt: (ls // rt, ls % rt, 0))
        else:
            g_in = (g, (None, tr, ns), lambda ls, p, rt=rt: (p[0], ls % rt, ls // rt))
            half = ((tr, ns), lambda ls, p, rt=rt: (ls % rt, ls // rt))
        jobs.append((N_CHIPS * rt, [g_in, (got, *half)], [(got.shape, BF16, *half)], add))
    return _run_jobs(jobs, place, name)


def _slot_shape(q, rowlike):
    return (3,) + (q.shape[1:] if rowlike else (q.shape[0], q.shape[1] // N_CHIPS))


def sum_pieces(qs, recvs, rowlikes, place, name):
    def total(ins, outs):
        acc = ins[0][...].astype(F32)
        for k in range(3):
            acc = acc + ins[1][k].astype(F32)
        outs[0][...] = acc

    jobs = []
    for q, recv, rowlike in zip(qs, recvs, rowlikes):
        kh, ns = recv.shape[1], recv.shape[2]
        rt = _row_tiles(kh, ns)
        tr = kh // rt
        mine = ((None, tr, ns), lambda ls, p: (p[1], ls, 0)) if rowlike else ((tr, ns), lambda ls, p: (ls, p[1]))
        jobs.append((rt, [(q, *mine), (recv, (3, tr, ns), lambda ls, p: (0, ls, 0))],
                     [((2, kh, ns), F32, (None, tr, ns), lambda ls, p: (p[0], ls, 0))], total))
    return _run_jobs(jobs, place, name)


def join_halves(halves, name):
    n = len(halves)

    def body(*refs):
        o_refs = refs[n:2 * n]
        send_sems, recv_sems = refs[2 * n:]
        x, y, c = _me()
        sib = (x, y, 1 - c)
        sends = []
        for a in range(n):
            cp = _remote(o_refs[a].at[c], o_refs[a].at[c], send_sems.at[a], recv_sems.at[a], sib)
            cp.start()
            sends.append(cp)
        for a in range(n):
            got = o_refs[a].at[1 - c]
            _remote(got, got, send_sems.at[a], recv_sems.at[a], sib).wait_recv()
        for cp in sends:
            cp.wait_send()

    return pl.pallas_call(
        body, name=name, in_specs=[HBM_SPEC] * n, out_specs=[HBM_SPEC] * n,
        out_shape=[SDS(h.shape, h.dtype) for h in halves],
        input_output_aliases={a: a for a in range(n)},
        scratch_shapes=[pltpu.SemaphoreType.DMA((n,)), pltpu.SemaphoreType.DMA((n,))],
    )(*halves)


N_DEV = 8


def sum_devices(s, slots, me):
    R, Wd = s.shape

    def body(me_ref, s_ref, slots_ref, out_ref):
        acc = None
        for d in range(N_DEV):
            mine = me_ref[0] == d
            term = jnp.where(mine, s_ref[...], slots_ref[jnp.where(mine, d ^ 1, d)])
            acc = term if acc is None else acc + term
        out_ref[...] = acc

    vmem = pl.BlockSpec(memory_space=pltpu.VMEM)
    return pl.pallas_call(
        body, name="sum_devices", in_specs=[pl.BlockSpec(memory_space=pltpu.SMEM), vmem, vmem], out_specs=vmem,
        out_shape=SDS((R, Wd), F32),
    )(me, s, slots)


TRANSPOSED = ("w_in",)
BIG = {"ffn1_w_up": (D, 2 * FF, "col"), "ffn1_w_down": (FF, D, "row"), "w_in": (INW, D, "row"),
       "w_attn_up": (AW, D, "row"), "w_pool_up": (PW, D, "col"), "w_out": (D, D, "row"),
       "ffn2_w_up": (D, 2 * FF, "col"), "ffn2_w_down": (FF, D, "row")}
GROUPS = (("ffn1_w_up",), ("ffn1_w_down", "w_in", "w_attn_up", "w_pool_up", "w_out"), ("ffn2_w_up", "ffn2_w_down"))
SMALL = ("ffn1_norm", "mix_norm", "ffn2_norm", "final_norm", "pool_scale", "sinks", "pool_w_mix")
SMALL_W = 128


def _rowlike(name):
    return BIG[name][2] == "row"


def _half_dims(name):
    k, n, kind = BIG[name]
    return (k // N_CHIPS // 2, n) if kind == "row" else (k // 2, n // N_CHIPS)


def shard_view(name, shard):
    return shard.reshape((2,) + _half_dims(name))


def full_from_view(name, fv):
    k, n, _ = BIG[name]
    return fv.reshape(k, n)


def grad_view(name, g):
    kh, ns = _half_dims(name)
    return g.reshape(_full_shape(jax.ShapeDtypeStruct((2, kh, ns), g.dtype), _rowlike(name)))


def pack_small(d):
    parts = []
    for name in SMALL:
        a = d[name].reshape(-1)
        pad = (-a.shape[0]) % SMALL_W
        parts.append(jnp.pad(a, (0, pad)).reshape(-1, SMALL_W))
    a = jnp.concatenate(parts, axis=0)
    return jnp.pad(a, ((0, (-a.shape[0]) % 8), (0, 0)))


def unpack_small(a, like):
    out, r0 = {}, 0
    for name in SMALL:
        size = int(np.prod(like[name].shape))
        rows = -(-size // SMALL_W)
        out[name] = a[r0:r0 + rows].reshape(-1)[:size].reshape(like[name].shape)
        r0 += rows
    return out


WEIGHTS = ("ffn1_norm", "ffn1_w_up", "ffn1_w_down", "mix_norm", "w_in", "sinks", "w_attn_up", "pool_w_mix",
           "pool_scale", "w_pool_up", "w_out", "ffn2_norm", "ffn2_w_up", "ffn2_w_down", "final_norm")


def kernel(x, ffn1_norm, ffn1_w_up, ffn1_w_down, mix_norm, w_in, sinks, w_attn_up, pool_w_mix, pool_scale, w_pool_up, w_out, ffn2_norm, ffn2_w_up, ffn2_w_down, final_norm, loss_target, m_ffn1_norm, m_ffn1_w_up, m_ffn1_w_down, m_mix_norm, m_w_in, m_sinks, m_w_attn_up, m_pool_w_mix, m_pool_scale, m_w_pool_up, m_w_out, m_ffn2_norm, m_ffn2_w_up, m_ffn2_w_down, m_final_norm, v_ffn1_norm, v_ffn1_w_up, v_ffn1_w_down, v_mix_norm, v_w_in, v_sinks, v_w_attn_up, v_pool_w_mix, v_pool_scale, v_w_pool_up, v_w_out, v_ffn2_norm, v_ffn2_w_up, v_ffn2_w_down, v_final_norm):
    given = dict(locals())
    w = {n: given[n] for n in WEIGHTS}
    m = {n: given["m_" + n] for n in WEIGHTS}
    v = {n: given["v_" + n] for n in WEIGHTS}
    cx, cy, cc = _me()
    place = jnp.stack([cc, 2 * cx + cy]).astype(jnp.int32)

    def local2d(d, n):
        return d[n][0].T if n in TRANSPOSED else d[n][0]

    shards = {n: local2d(w, n) for n in BIG}
    grads, delta, new_m, new_v = {}, {}, {}, {}
    rowlikes = [[_rowlike(n) for n in names] for names in GROUPS]

    class Exchanges:
        def __init__(self):
            self.gathers, self.forwards, self.fulls, self.reductions, self.small = {}, {}, {}, {}, None

        def _behind_first(self, a):
            return a + self.gathers[0]["token"][0, 0]

        def _start_gather(self, group, after):
            prepare = self._behind_first if group else (lambda a: a)
            sv = [shard_view(n, prepare(shards[n]).astype(BF16)) for n in GROUPS[group]]
            rl = rowlikes[group]
            self.gathers[group] = exchange_start(sv, [_full_shape(s, r) for s, r in zip(sv, rl)], _gather_plan(rl),
                                                 after, f"gather_start_{group}")

        def weights(self, group, after):
            if not self.gathers:
                self._start_gather(0, None)
                self.packed = [self._behind_first(pack_small(d)) for d in (w, m, v)]
                after = [self.gathers[0]["token"], *self.packed]
            if group not in self.fulls:
                names, rl, state = GROUPS[group], rowlikes[group], self.gathers[group]
                if group in self.forwards:
                    _, fulls = exchange_wait(self.forwards.pop(group), _forward_plan(rl), after,
                                             f"forward_wait_{group}", FORWARDS)
                else:
                    sv, fulls = exchange_wait(state, _gather_plan(rl), state["token"] if after is None else after,
                                              f"gather_wait_{group}")
                    fulls = gather_finish(sv, fulls, rl, f"gather_finish_{group}")
                self.fulls[group] = {n: full_from_view(n, f) for n, f in zip(names, fulls)}
                if group + 1 < len(GROUPS):
                    self._start_gather(group + 1, fulls[0])
            return self.fulls[group]

        def started(self):
            return self.gathers[max(self.gathers)]["token"]

        def prefetch(self, group, after):
            rl, state = rowlikes[group], self.gathers[group]
            sv, fulls = exchange_wait(state, _gather_plan(rl), after, f"gather_wait_{group}")
            self.forwards[group] = exchange_start(sv, fulls, _forward_plan(rl), None, f"forward_start_{group}", FORWARDS)
            return self.forwards[group]["token"]

        def grads(self, g):
            names = tuple(g)
            rl = [_rowlike(n) for n in names]
            gv = [grad_view(n, g[n]) for n in names]
            state = exchange_start(gv, [_half_buffer_shape(a, r) for a, r in zip(gv, rl)], _swap_plan(rl), None,
                                   "swap_start_" + names[0], SIBLING)
            self.reductions[names] = state
            return state["token"]

        def advance(self, names, after):
            rl = [_rowlike(n) for n in names]
            gv, gots = exchange_wait(self.reductions[names], _swap_plan(rl), after, "swap_wait_" + names[0], SIBLING)
            qs = add_halves(gv, gots, rl, place, "add_halves_" + names[0])
            state = exchange_start(qs, [_slot_shape(q, r) for q, r in zip(qs, rl)], _all_to_all_plan(rl), None,
                                   "all_to_all_start_" + names[0])
            self.reductions[names] = state
            return state["token"]

        def reduce(self, names, after):
            rl = [_rowlike(n) for n in names]
            qs, recvs = exchange_wait(self.reductions.pop(names), _all_to_all_plan(rl), after,
                                      "all_to_all_wait_" + names[0])
            halves = sum_pieces(qs, recvs, rl, place, "sum_pieces_" + names[0])
            return [(shards[n], o.reshape(shards[n].shape), local2d(m, n), local2d(v, n))
                    for n, o in zip(names, join_halves(halves, "join_halves_" + names[0]))]

        def updated(self, names, outs):
            for i, n in enumerate(names):
                grads[n], delta[n], new_m[n], new_v[n] = outs[4 * i:4 * i + 4]
            return [new_v[n] for n in names]

        def update(self, names, updates, token=None):
            return self.updated(names, adamw(updates, place, "adamw_" + names[0], after=token))

        def finish(self, names, after):
            return self.update(names, self.reduce(names, after))

        def small_start(self, G, loss, after):
            packed = pack_small({n: G[n] for n in SMALL})
            used_rows = sum(-(-int(np.prod(small_like[n].shape)) // SMALL_W) for n in SMALL)
            assert packed.shape[0] > used_rows
            packed = packed.at[-1, 0].set(loss[0, 0])
            self.small = exchange_start([packed], [(N_DEV,) + packed.shape], _everyone_plan, after,
                                        "small_start", EVERYONE)
            return self.small["token"]

        def small_finish(self, after):
            (packed,), (slots,) = exchange_wait(self.small, _everyone_plan, after, "small_wait", EVERYONE)
            total = sum_devices(packed, slots, (4 * cx + 2 * cy + cc).astype(jnp.int32).reshape(1))
            self.loss = total[-1, 0]
            g, ds, ms, vs = adamw([(self.packed[0], total, self.packed[1], self.packed[2])], place, "adamw_small")
            for d, packed_d in ((grads, g), (delta, ds), (new_m, ms), (new_v, vs)):
                d.update(unpack_small(packed_d, small_like))
            return vs

    small_like = {n: w[n] for n in SMALL}
    S = {n: w[n].reshape(1, -1) for n in ("ffn1_norm", "mix_norm", "ffn2_norm", "final_norm", "pool_scale", "sinks")}
    S["pool_w_mix"] = w["pool_w_mix"][0].astype(BF16)
    exchanges = Exchanges()
    _, dx = local_fwd_bwd(x[0], loss_target[0], S, exchanges)
    loss = exchanges.loss

    def shaped(d, n):
        return (d[n].T if n in TRANSPOSED else d[n]).reshape(w[n].shape)

    return (loss, dx[None], *[shaped(grads, n) for n in WEIGHTS], *[shaped(delta, n) for n in WEIGHTS],
            *[shaped(new_m, n) for n in WEIGHTS], *[shaped(new_v, n) for n in WEIGHTS])
```

```python
import numpy as np
import jax
import jax.numpy as jnp
from jax import lax
from jax.experimental import pallas as pl
from jax.experimental.pallas import tpu as pltpu

F32 = jnp.float32
BF16 = jnp.bfloat16
SDS = jax.ShapeDtypeStruct
MESH = pl.DeviceIdType.MESH

D = 1024
FF = 2816
NQ = 16
NKV = 2
HD = 64
GQ = NQ // NKV
AW = NQ * HD
KVW = NKV * HD
BLK = 128
PW = 512
PG = 128
POOL_WINDOWS = (2, 4, 8, 16)
HALO = 16
INW = AW + 2 * KVW + PW + 2 * D
C_KV = AW
C_Z = AW + 2 * KVW
C_G = C_Z + PW
EPS = 1e-6
FF_CHUNK = 256
FF_CHUNKS = tuple((c, FF_CHUNK) for c in range(0, FF, FF_CHUNK))
SLOPES = tuple(float(2.0 ** (-8.0 * h / NQ)) for h in range(1, NQ + 1))
SCALE = HD ** -0.5

LR, B1, B2, ADAM_EPS, WD, STEP = 0.001, 0.9, 0.999, 1e-08, 0.01, 10

VMEM_LIMIT = 56 * 1024 * 1024
N_CHIPS = 4

NT = (((1,), (1,)), ((), ()))
TN = (((0,), (0,)), ((), ()))


def _cp(sem=None, vmem=VMEM_LIMIT):
    return pltpu.CompilerParams(dimension_semantics=sem, vmem_limit_bytes=vmem)


def _const_spec(shape):
    nd = len(shape)
    return pl.BlockSpec(shape, lambda *_: (0,) * nd, pipeline_mode=pl.Buffered(1))


def _rstd(x):
    return lax.rsqrt(jnp.mean(x * x, axis=-1, keepdims=True) + EPS)


def _rms_bwd(dn, xhat, rstd, g):
    dxhat = dn * g
    return rstd * (dxhat - xhat * jnp.mean(dxhat * xhat, axis=-1, keepdims=True))


def _sigmoid(x):
    return 0.5 * jnp.tanh(0.5 * x) + 0.5


def _dot(a, b):
    return jnp.dot(a, b, preferred_element_type=F32)


def _dot_nt(a, b):
    return lax.dot_general(a, b, NT, preferred_element_type=F32)


def _dot_tn(a, b):
    return lax.dot_general(a, b, TN, preferred_element_type=F32)


def _swiglu_up(x, g, wup_ref, ab_ref, hid_ref):
    n = (x * _rstd(x) * g).astype(BF16)
    for c0, w in FF_CHUNKS:
        a = _dot(n, wup_ref[:, c0:c0 + w])
        b = _dot(n, wup_ref[:, FF + c0:FF + c0 + w])
        sig = _sigmoid(a)
        s = a * sig
        ab_ref[:, c0:c0 + w] = (b * (sig + s * (1.0 - sig))).astype(BF16)
        ab_ref[:, FF + c0:FF + c0 + w] = s.astype(BF16)
        hid_ref[:, c0:c0 + w] = (s * b).astype(BF16)


def ffn_up(h, g, wup, name):
    T = h.shape[0]
    TM = 512
    tile = lambda w: pl.BlockSpec((TM, w), lambda i: (i, 0))

    def body(h_ref, g_ref, wup_ref, ab_ref, hid_ref):
        _swiglu_up(h_ref[...], g_ref[...], wup_ref, ab_ref, hid_ref)

    return pl.pallas_call(
        body, name=name, grid=(T // TM,),
        in_specs=[tile(D), _const_spec((1, D)), _const_spec((D, 2 * FF))],
        out_specs=[tile(2 * FF), tile(FF)],
        out_shape=[SDS((T, 2 * FF), BF16), SDS((T, FF), BF16)],
        compiler_params=_cp(("arbitrary",)),
    )(h, g, wup)


def ffn_down(h, hid, wdn, name, after=None):
    T = h.shape[0]
    TM = min(1024, T)
    tile = lambda w: pl.BlockSpec((TM, w), lambda i: (i, 0))
    token_spec, token_arg = _token_operand(after)

    def body(h_ref, hid_ref, wdn_ref, *rest):
        rest[-1][...] = h_ref[...] + 0.5 * _dot(hid_ref[...], wdn_ref[...])

    return pl.pallas_call(
        body, name=name, grid=(T // TM,),
        in_specs=[tile(D), tile(FF), _const_spec((FF, D))] + token_spec,
        out_specs=tile(D), out_shape=SDS((T, D), F32),
        compiler_params=_cp(("arbitrary",)),
    )(h, hid, wdn, *token_arg)


def ffn_fwd(h, g, wup, wdn, name, head=None):
    T = h.shape[0]
    TM = 512
    tile = lambda w: pl.BlockSpec((TM, w), lambda i: (i, 0))
    acc_spec = lambda w: pl.BlockSpec((1, w), lambda i: (0, 0))

    def body(h_ref, g_ref, wup_ref, wdn_ref, *rest):
        out_ref, ab_ref, hid_ref = rest[-3:] if head is None else rest[2:5]
        x = h_ref[...]
        _swiglu_up(x, g_ref[...], wup_ref, ab_ref, hid_ref)
        out = x + 0.5 * _dot(hid_ref[...], wdn_ref[...])
        if head is None:
            out_ref[...] = out
        else:
            t_ref, gf_ref, loss_ref, dgf_ref = rest[0], rest[1], rest[5], rest[6]
            out_ref[...] = _loss_head(out, t_ref[...], gf_ref[...], loss_ref, dgf_ref, pl.program_id(0) == 0)

    head_in, head_specs, head_out_specs, head_out_shape = [], [], [], []
    if head is not None:
        head_in, head_specs = list(head), [tile(D), _const_spec((1, D))]
        head_out_specs, head_out_shape = [acc_spec(1), acc_spec(D)], [SDS((1, 1), F32), SDS((1, D), F32)]
    return pl.pallas_call(
        body, name=name, grid=(T // TM,),
        in_specs=[tile(D), _const_spec((1, D)), _const_spec((D, 2 * FF)), _const_spec((FF, D))] + head_specs,
        out_specs=[tile(D), tile(2 * FF), tile(FF)] + head_out_specs,
        out_shape=[SDS((T, D), F32), SDS((T, 2 * FF), BF16), SDS((T, FF), BF16)] + head_out_shape,
        compiler_params=_cp(("arbitrary",)),
    )(h, g, wup, wdn, *head_in)


def _loss_head(x, target, g, loss_ref, dg_ref, first):
    @pl.when(first)
    def _():
        loss_ref[...] = jnp.zeros_like(loss_ref)
        dg_ref[...] = jnp.zeros_like(dg_ref)

    rstd = _rstd(x)
    xhat = x * rstd
    err = xhat * g - target
    loss_ref[...] += 0.5 * jnp.sum(jnp.mean(err * err, axis=-1, keepdims=True), axis=0, keepdims=True)
    dy = err * (1.0 / D)
    dg_ref[...] += jnp.sum(dy * xhat, axis=0, keepdims=True)
    return _rms_bwd(dy, xhat, rstd, g)


def _adamw_tile(w_ref, g_ref, m_ref, v_ref, go_ref, d_ref, nm_ref, nv_ref):
    gv = g_ref[...]
    go_ref[...] = gv
    nm = B1 * m_ref[...] + (1.0 - B1) * gv
    nv = B2 * v_ref[...] + (1.0 - B2) * (gv * gv)
    nm_ref[...] = nm
    nv_ref[...] = nv
    d_ref[...] = -LR * ((nm / (1.0 - B1 ** STEP)) / (jnp.sqrt(nv / (1.0 - B2 ** STEP)) + ADAM_EPS) + WD * w_ref[...])


def _riders(updates, steps):
    in_specs, out_specs, out_shapes, operands, tiles = [], [], [], [], []
    for w, g, m, v in updates:
        R, C = w.shape
        n = max(d for d in range(1, steps + 1) if R % d == 0 and (R // d) % 8 == 0)
        spec = pl.BlockSpec((R // n, C), lambda i, n=n: (jnp.minimum(i, n - 1), 0))
        in_specs += [spec] * 4
        out_specs += [spec] * 4
        out_shapes += [SDS((R, C), F32)] * 4
        operands += [w, g, m, v]
        tiles.append(n)

    def run(step, in_refs, out_refs):
        for u, n in enumerate(tiles):
            @pl.when(step < n)
            def _(u=u):
                _adamw_tile(*in_refs[4 * u:4 * u + 4], *out_refs[4 * u:4 * u + 4])

    return in_specs, out_specs, out_shapes, operands, run


def ffn_bwd_x(dh, h_in, ab, g, wup, wdn, name, updates=()):
    T = dh.shape[0]
    TM = 256 if updates else 512
    SUB = 256
    r_in, r_out, r_shapes, r_args, ride = _riders(updates, T // TM)

    def body(dh_ref, h_ref, ab_ref, g_ref, wup_ref, wdn_ref, *rest):
        dhin_ref, dup_ref, n_ref, dg_ref = rest[len(r_in):len(r_in) + 4]
        ride(pl.program_id(0), rest[:len(r_in)], rest[len(r_in) + 4:])
        g = g_ref[...]
        dg = None
        for r0 in range(0, TM, SUB):
            rows = slice(r0, r0 + SUB)
            x = h_ref[rows, :]
            rstd = _rstd(x)
            xhat = x * rstd
            n_ref[rows, :] = (xhat * g).astype(BF16)
            dh = dh_ref[rows, :]
            dhh = (0.5 * dh).astype(BF16)
            for c0, w in FF_CHUNKS:
                dhid = _dot_nt(dhh, wdn_ref[c0:c0 + w, :]).astype(BF16)
                dup_ref[rows, c0:c0 + w] = dhid * ab_ref[rows, c0:c0 + w]
                dup_ref[rows, FF + c0:FF + c0 + w] = dhid * ab_ref[rows, FF + c0:FF + c0 + w]
            dn = _dot_nt(dup_ref[rows, :], wup_ref[...])
            dhin_ref[rows, :] = dh + _rms_bwd(dn, xhat, rstd, g)
            part = jnp.sum(dn * xhat, axis=0, keepdims=True)
            dg = part if dg is None else dg + part

        @pl.when(pl.program_id(0) == 0)
        def _():
            dg_ref[...] = jnp.zeros_like(dg_ref)

        dg_ref[...] += dg

    tile = lambda w: pl.BlockSpec((TM, w), lambda i: (i, 0))
    return pl.pallas_call(
        body, name=name, grid=(T // TM,),
        in_specs=[tile(D), tile(D), tile(2 * FF), _const_spec((1, D)), _const_spec((D, 2 * FF)), _const_spec((FF, D))]
        + r_in,
        out_specs=[tile(D), tile(2 * FF), tile(D), pl.BlockSpec((1, D), lambda i: (0, 0))] + r_out,
        out_shape=[SDS((T, D), F32), SDS((T, 2 * FF), BF16), SDS((T, D), BF16), SDS((1, D), F32)] + r_shapes,
        compiler_params=_cp(("arbitrary",)),
    )(dh, h_in, ab, g, wup, wdn, *r_args)


TOKEN_SPEC = pl.BlockSpec((8, 128), lambda *_: (0, 0))


def _token_operand(token):
    return ([], []) if token is None else ([TOKEN_SPEC], [token])


def matmul_tn(a, b, name, *, tm, tn, tt=1024, b_scale=None, after=None):
    T, M = a.shape
    N = b.shape[1]
    tt = min(tt, T)
    assert M % tm == 0 and N % tn == 0 and T % tt == 0
    nt = T // tt
    token_spec, token_arg = _token_operand(after)

    def body(a_ref, b_ref, *rest):
        o_ref, acc_ref = rest[-2:]
        t = pl.program_id(2)

        @pl.when(t == 0)
        def _():
            acc_ref[...] = jnp.zeros_like(acc_ref)

        bv = b_ref[...]
        if b_scale is not None:
            bv = bv * b_scale
        acc_ref[...] += _dot_tn(a_ref[...].astype(BF16), bv.astype(BF16))

        @pl.when(t == nt - 1)
        def _():
            o_ref[...] = acc_ref[...].astype(BF16)

    return pl.pallas_call(
        body, name=name, grid=(M // tm, N // tn, nt),
        in_specs=[pl.BlockSpec((tt, tm), lambda i, j, t: (t, i)), pl.BlockSpec((tt, tn), lambda i, j, t: (t, j))]
        + token_spec,
        out_specs=pl.BlockSpec((tm, tn), lambda i, j, t: (i, j)),
        out_shape=SDS((M, N), BF16),
        scratch_shapes=[pltpu.VMEM((tm, tn), F32)],
        compiler_params=_cp(("parallel", "parallel", "arbitrary")),
    )(a, b, *token_arg)


def mix_in_fwd(h1, g, win_t):
    T = h1.shape[0]
    TM = 512

    def body(h_ref, g_ref, w_ref, u_ref, q_ref, kv_ref, z_ref, gt_ref):
        x = h_ref[...]
        u = (x * _rstd(x) * g_ref[...]).astype(BF16)
        u_ref[...] = u
        for c in range(0, AW, 256):
            q_ref[:, c:c + 256] = _dot_nt(u, w_ref[c:c + 256, :]).astype(BF16)
        kv_ref[...] = _dot_nt(u, w_ref[C_KV:C_Z, :]).astype(BF16)
        for c in range(0, PW, 256):
            z_ref[:, c:c + 256] = _dot_nt(u, w_ref[C_Z + c:C_Z + c + 256, :])
        for c in range(0, 2 * D, 256):
            gt_ref[:, c:c + 256] = _dot_nt(u, w_ref[C_G + c:C_G + c + 256, :]).astype(BF16)

    tile = lambda w: pl.BlockSpec((TM, w), lambda i: (i, 0))
    return pl.pallas_call(
        body, name="mix_in_fwd", grid=(T // TM,),
        in_specs=[tile(D), _const_spec((1, D)), _const_spec((INW, D))],
        out_specs=[tile(D), tile(AW), tile(2 * KVW), tile(PW), tile(2 * D)],
        out_shape=[SDS((T, D), BF16), SDS((T, AW), BF16), SDS((T, 2 * KVW), BF16), SDS((T, PW), F32),
                   SDS((T, 2 * D), BF16)],
        compiler_params=_cp(("arbitrary",)),
    )(h1, g, win_t)


def mix_in_bwd(dq, dkv, dz, dgt, dh2, h1, g, win_t):
    T = h1.shape[0]
    TM = min(1024, T)
    SUB = 256

    def body(dq_ref, dkv_ref, dz_ref, dgt_ref, dh2_ref, h_ref, g_ref, w_ref, dh1_ref, dg_ref):
        g = g_ref[...]
        dg = None
        for r0 in range(0, TM, SUB):
            rows = slice(r0, r0 + SUB)
            du = _dot(dq_ref[rows, :], w_ref[0:AW, :])
            du += _dot(dkv_ref[rows, :], w_ref[C_KV:C_Z, :])
            du += _dot(dz_ref[rows, :], w_ref[C_Z:C_G, :])
            du += _dot(dgt_ref[rows, :], w_ref[C_G:INW, :])
            x = h_ref[rows, :]
            rstd = _rstd(x)
            xhat = x * rstd
            dh1_ref[rows, :] = dh2_ref[rows, :] + _rms_bwd(du, xhat, rstd, g)
            part = jnp.sum(du * xhat, axis=0, keepdims=True)
            dg = part if dg is None else dg + part

        @pl.when(pl.program_id(0) == 0)
        def _():
            dg_ref[...] = jnp.zeros_like(dg_ref)

        dg_ref[...] += dg

    tile = lambda w: pl.BlockSpec((TM, w), lambda i: (i, 0))
    return pl.pallas_call(
        body, name="mix_in_bwd", grid=(T // TM,),
        in_specs=[tile(AW), tile(2 * KVW), tile(PW), tile(2 * D), tile(D), tile(D), _const_spec((1, D)),
                  _const_spec((INW, D))],
        out_specs=[tile(D), pl.BlockSpec((1, D), lambda i: (0, 0))],
        out_shape=[SDS((T, D), F32), SDS((1, D), F32)],
        compiler_params=_cp(("arbitrary",)),
    )(dq, dkv, dz, dgt, dh2, h1, g, win_t)


PAIR = 2 * HD
NPAIR = GQ // 2


def _lo_lanes():
    return lax.broadcasted_iota(jnp.int32, (BLK, PAIR), 1) < HD


def _stack_heads(ref, kvh, scale=None):
    lo = _lo_lanes()
    parts = []
    for pr in range(NPAIR):
        t = ref[:, (kvh * NPAIR + pr) * PAIR:(kvh * NPAIR + pr + 1) * PAIR]
        if scale is not None:
            t = t * scale
        zero = jnp.zeros_like(t)
        parts += [jnp.where(lo, t, zero), jnp.where(lo, zero, t)]
    return jnp.concatenate(parts, axis=0)


def _kv_tiles(kvc_ref, kvp_ref, tile, kvh):
    lo = _lo_lanes()
    dup, left, right = [], [], []
    for ref in (kvp_ref, kvc_ref):
        t = ref[:, tile * PAIR:(tile + 1) * PAIR]
        r = pltpu.roll(t.astype(F32), HD, 1).astype(BF16)
        zero = jnp.zeros_like(t)
        a, b = (t, r) if kvh == 0 else (r, t)
        dup.append(jnp.where(lo, a, b))
        left.append(jnp.where(lo, a, zero))
        right.append(jnp.where(lo, zero, b))
    cat = lambda xs: jnp.concatenate(xs, axis=0)
    return cat(dup), cat(left), cat(right)


def _band_consts(first):
    row = lax.broadcasted_iota(jnp.int32, (BLK, BLK), 0)
    col = lax.broadcasted_iota(jnp.int32, (BLK, BLK), 1)
    upper = col > row
    dist = jnp.where(upper, row - col + BLK, row - col).astype(F32)
    pen = jnp.where(jnp.logical_and(upper, first), -jnp.inf, 0.0)
    return upper, dist, pen


def _split_band(upper, t):
    zero = jnp.zeros_like(t)
    return jnp.concatenate([jnp.where(upper, t, zero), jnp.where(upper, zero, t)], axis=1)


def attn_fwd(q, kv, sinks):
    T = q.shape[0]
    nb = T // BLK

    def body(sink_ref, q_ref, kvc_ref, kvp_ref, att_ref, lse_ref):
        upper, dist, pen = _band_consts(pl.program_id(0) == 0)
        scores, values = [], []
        for kvh in range(NKV):
            kdup, _, _ = _kv_tiles(kvc_ref, kvp_ref, 0, kvh)
            values.append(_kv_tiles(kvc_ref, kvp_ref, 1, kvh)[1:])
            scores.append(_dot_nt(_stack_heads(q_ref, kvh, SCALE), kdup))
        for kvh in range(NKV):
            s_all = scores[kvh]
            vleft, vright = values[kvh]
            for pr in range(NPAIR):
                outs, inv = [], []
                for side, vpad in ((0, vleft), (1, vright)):
                    g = 2 * pr + side
                    hq = kvh * GQ + g
                    sink = sink_ref[0, hq]
                    rows = slice(g * BLK, (g + 1) * BLK)
                    s = jnp.where(upper, s_all[rows, 0:BLK], s_all[rows, BLK:2 * BLK]) - SLOPES[hq] * dist + pen
                    m = jnp.maximum(jnp.max(s, axis=-1, keepdims=True), sink)
                    p = jnp.exp(s - m)
                    l = jnp.sum(p, axis=-1, keepdims=True) + jnp.exp(sink - m)
                    lse_ref[:, hq:hq + 1] = m + jnp.log(l)
                    outs.append(_dot(_split_band(upper, p.astype(BF16)), vpad))
                    inv.append(1.0 / l)
                col0 = (kvh * NPAIR + pr) * PAIR
                att_ref[:, col0:col0 + PAIR] = ((outs[0] + outs[1]) * jnp.where(_lo_lanes(), inv[0], inv[1])).astype(BF16)

    return pl.pallas_call(
        body, name="attn_fwd", grid=(nb,),
        in_specs=[pl.BlockSpec(memory_space=pltpu.SMEM),
                  pl.BlockSpec((BLK, AW), lambda i: (i, 0)),
                  pl.BlockSpec((BLK, 2 * KVW), lambda i: (i, 0)),
                  pl.BlockSpec((BLK, 2 * KVW), lambda i: (jnp.maximum(i - 1, 0), 0))],
        out_specs=[pl.BlockSpec((BLK, AW), lambda i: (i, 0)), pl.BlockSpec((BLK, NQ), lambda i: (i, 0))],
        out_shape=[SDS((T, AW), BF16), SDS((T, NQ), F32)],
        compiler_params=_cp(("arbitrary",)),
    )(sinks, q, kv, kv)


def attn_bwd(q, kv, datt, att, lse, sinks):
    T = q.shape[0]
    nb = T // BLK

    def body(sink_ref, q_ref, kvc_ref, kvp_ref, do_ref, out_ref, lse_ref, dq_ref, dkv_ref, dsink_ref, carry_ref):
        i = pl.program_id(0)

        @pl.when(i == 0)
        def _():
            dsink_ref[...] = jnp.zeros_like(dsink_ref)
            carry_ref[...] = jnp.zeros_like(carry_ref)

        @pl.when(i < nb)
        def _():
            upper, dist, pen = _band_consts(i == 0)
            lo = _lo_lanes()
            dk_dup, dv_dup = [], []
            staged = []
            for kvh in range(NKV):
                kdup, kleft, kright = _kv_tiles(kvc_ref, kvp_ref, 0, kvh)
                vdup, _, _ = _kv_tiles(kvc_ref, kvp_ref, 1, kvh)
                qs = _stack_heads(q_ref, kvh, SCALE)
                dos = _stack_heads(do_ref, kvh)
                staged.append((kleft, kright, qs, dos, _dot_nt(qs, kdup), _dot_nt(dos, vdup)))
            deltas = []
            for pair in range(NQ // 2):
                cols = slice(pair * PAIR, (pair + 1) * PAIR)
                t = do_ref[:, cols].astype(F32) * out_ref[:, cols].astype(F32)
                deltas += [jnp.sum(jnp.where(lo, t, 0.0), axis=-1, keepdims=True),
                           jnp.sum(jnp.where(lo, 0.0, t), axis=-1, keepdims=True)]
            for kvh in range(NKV):
                kleft, kright, qs, dos, s_all, dp_all = staged[kvh]
                ds_parts, p_parts = [], []
                for pr in range(NPAIR):
                    dq = None
                    for side, kpad in ((0, kleft), (1, kright)):
                        g = 2 * pr + side
                        hq = kvh * GQ + g
                        lse_h = lse_ref[:, hq:hq + 1]
                        rows = slice(g * BLK, (g + 1) * BLK)
                        s = jnp.where(upper, s_all[rows, 0:BLK], s_all[rows, BLK:2 * BLK]) - SLOPES[hq] * dist + pen
                        p = jnp.exp(s - lse_h)
                        dp = jnp.where(upper, dp_all[rows, 0:BLK], dp_all[rows, BLK:2 * BLK])
                        delta = deltas[hq]
                        dsink_ref[:, hq:hq + 1] += -jnp.sum(jnp.exp(sink_ref[0, hq] - lse_h) * delta, axis=0,
                                                            keepdims=True)
                        ds = _split_band(upper, (p * (dp - delta)).astype(BF16))
                        ds_parts.append(ds)
                        p_parts.append(_split_band(upper, p.astype(BF16)))
                        d = _dot(ds, kpad)
                        dq = d if dq is None else dq + d
                    col0 = (kvh * NPAIR + pr) * PAIR
                    dq_ref[:, col0:col0 + PAIR] = (dq * SCALE).astype(BF16)
                dkw = _dot_tn(qs, jnp.concatenate(ds_parts, axis=0)).T
                dvw = _dot_tn(dos, jnp.concatenate(p_parts, axis=0)).T
                dk_dup.append(dkw + pltpu.roll(dkw, HD, 1))
                dv_dup.append(dvw + pltpu.roll(dvw, HD, 1))
            dk = jnp.where(jnp.concatenate([lo, lo], axis=0), dk_dup[0], dk_dup[1])
            dv = jnp.where(jnp.concatenate([lo, lo], axis=0), dv_dup[0], dv_dup[1])
            dkv_ref[:, 0:PAIR] = (carry_ref[:, 0:PAIR] + dk[0:BLK]).astype(BF16)
            dkv_ref[:, PAIR:2 * PAIR] = (carry_ref[:, PAIR:2 * PAIR] + dv[0:BLK]).astype(BF16)
            carry_ref[:, 0:PAIR] = dk[BLK:2 * BLK]
            carry_ref[:, PAIR:2 * PAIR] = dv[BLK:2 * BLK]

        @pl.when(i == nb)
        def _():
            dkv_ref[...] = carry_ref[...].astype(BF16)

    cur = lambda i: (jnp.minimum(i, nb - 1), 0)
    prev = lambda i: (jnp.maximum(jnp.minimum(i, nb - 1) - 1, 0), 0)
    return pl.pallas_call(
        body, name="attn_bwd", grid=(nb + 1,),
        in_specs=[pl.BlockSpec(memory_space=pltpu.SMEM),
                  pl.BlockSpec((BLK, AW), cur), pl.BlockSpec((BLK, 2 * KVW), cur), pl.BlockSpec((BLK, 2 * KVW), prev),
                  pl.BlockSpec((BLK, AW), cur), pl.BlockSpec((BLK, AW), cur), pl.BlockSpec((BLK, NQ), cur)],
        out_specs=[pl.BlockSpec((BLK, AW), cur),
                   pl.BlockSpec((BLK, 2 * KVW), lambda i: (jnp.maximum(i - 1, 0), 0)),
                   pl.BlockSpec((1, NQ), lambda i: (0, 0))],
        out_shape=[SDS((T, AW), BF16), SDS((T, 2 * KVW), BF16), SDS((1, NQ), F32)],
        scratch_shapes=[pltpu.VMEM((BLK, 2 * KVW), F32)],
        compiler_params=_cp(("arbitrary",)),
    )(sinks, q, kv, kv, datt, att, lse)


def _inv_counts(t0, rows):
    t = (t0 + lax.broadcasted_iota(jnp.int32, (rows, 1), 0) + 1).astype(F32)
    return [1.0 / jnp.minimum(t, float(w)) for w in POOL_WINDOWS]


def pool_fwd(z, wmix, scale):
    T = z.shape[0]
    TM = min(1024, T)
    L = TM + HALO

    def body(z_ref, halo_ref, wmix_ref, scale_ref, pooled_ref, mixs_ref):
        i = pl.program_id(0)
        halo = jnp.where(i > 0, halo_ref[...], 0.0)
        zt = z_ref[...]
        e = jnp.concatenate([halo, zt], axis=0)
        sums = []
        s = e
        for k in (1, 2, 4, 8):
            s = s + pltpu.roll(s, k, 0)
            sums.append(s)
        inv = _inv_counts(i * TM, TM)
        for gi in range(len(POOL_WINDOWS)):
            cols = slice(gi * PG, (gi + 1) * PG)
            pooled = (sums[gi][HALO:, cols] * inv[gi] - zt[:, cols]).astype(BF16)
            pooled_ref[:, cols] = pooled
            mixs_ref[:, cols] = (_dot(pooled, wmix_ref[gi]) * scale_ref[:, cols]).astype(BF16)

    return pl.pallas_call(
        body, name="pool_fwd", grid=(T // TM,),
        in_specs=[pl.BlockSpec((TM, PW), lambda i: (i, 0)),
                  pl.BlockSpec((HALO, PW), lambda i: (jnp.maximum(i * (TM // HALO) - 1, 0), 0)),
                  _const_spec((len(POOL_WINDOWS), PG, PG)), _const_spec((1, PW))],
        out_specs=[pl.BlockSpec((TM, PW), lambda i: (i, 0)), pl.BlockSpec((TM, PW), lambda i: (i, 0))],
        out_shape=[SDS((T, PW), BF16), SDS((T, PW), BF16)],
        compiler_params=_cp(("arbitrary",)),
    )(z, z, wmix, scale)


def pool_bwd(dmixs, pooled, wmix, scale):
    T = dmixs.shape[0]
    TM = min(1024, T)
    L = TM + HALO
    nt = T // TM

    def body(dm_ref, halo_ref, pooled_ref, wmix_ref, scale_ref, dz_ref, dwmix_ref, dscale_ref):
        i = pl.program_id(0)

        @pl.when(i == 0)
        def _():
            dwmix_ref[...] = jnp.zeros_like(dwmix_ref)
            dscale_ref[...] = jnp.zeros_like(dscale_ref)

        halo = jnp.where(i < nt - 1, halo_ref[...], 0.0)
        dm = dm_ref[...]
        e = jnp.concatenate([dm, halo], axis=0)
        inv = _inv_counts(i * TM, L)
        for gi in range(len(POOL_WINDOWS)):
            cols = slice(gi * PG, (gi + 1) * PG)
            w = wmix_ref[gi]
            dmixed = (e[:, cols] * scale_ref[:, cols]).astype(BF16)
            dpooled = _dot_nt(dmixed, w)
            pooled = pooled_ref[:, cols]
            mixed = _dot(pooled, w)
            dscale_ref[:, cols] += jnp.sum(dm[:, cols] * mixed, axis=0, keepdims=True)
            dwmix_ref[gi] += _dot_tn(pooled, dmixed[:TM])
            s = dpooled * inv[gi]
            k = 1
            while k < POOL_WINDOWS[gi]:
                s = s + pltpu.roll(s, L - k, 0)
                k *= 2
            dz_ref[:, cols] = (s[:TM] - dpooled[:TM]).astype(BF16)

    return pl.pallas_call(
        body, name="pool_bwd", grid=(nt,),
        in_specs=[pl.BlockSpec((TM, PW), lambda i: (i, 0)),
                  pl.BlockSpec((HALO, PW), lambda i: (jnp.minimum((i + 1) * (TM // HALO), T // HALO - 1), 0)),
                  pl.BlockSpec((TM, PW), lambda i: (i, 0)),
                  _const_spec((len(POOL_WINDOWS), PG, PG)), _const_spec((1, PW))],
        out_specs=[pl.BlockSpec((TM, PW), lambda i: (i, 0)),
                   pl.BlockSpec((len(POOL_WINDOWS), PG, PG), lambda i: (0, 0, 0)),
                   pl.BlockSpec((1, PW), lambda i: (0, 0))],
        out_shape=[SDS((T, PW), BF16), SDS((len(POOL_WINDOWS), PG, PG), F32), SDS((1, PW), F32)],
        compiler_params=_cp(("arbitrary",)),
    )(dmixs, dmixs, pooled, wmix, scale)


def merge_fwd(att, mixs, gt, h1, wattn, wpool, wout):
    T = h1.shape[0]
    TM = 512

    def body(att_ref, mixs_ref, gt_ref, h_ref, wa_ref, wp_ref, wo_ref, h2_ref, mg_ref, gf_ref):
        a = _dot(att_ref[...], wa_ref[...])
        p = _dot(mixs_ref[...], wp_ref[...])
        sa = _sigmoid(gt_ref[:, 0:D].astype(F32))
        sp = _sigmoid(gt_ref[:, D:2 * D].astype(F32))
        gf_ref[:, 0:D] = (a * sa * (1.0 - sa)).astype(BF16)
        gf_ref[:, D:2 * D] = (p * sp * (1.0 - sp)).astype(BF16)
        mg = (sa * a + sp * p).astype(BF16)
        mg_ref[...] = mg
        h2_ref[...] = h_ref[...] + _dot(mg, wo_ref[...])

    tile = lambda w: pl.BlockSpec((TM, w), lambda i: (i, 0))
    return pl.pallas_call(
        body, name="merge_fwd", grid=(T // TM,),
        in_specs=[tile(AW), tile(PW), tile(2 * D), tile(D), _const_spec((AW, D)), _const_spec((PW, D)),
                  _const_spec((D, D))],
        out_specs=[tile(D), tile(D), tile(2 * D)],
        out_shape=[SDS((T, D), F32), SDS((T, D), BF16), SDS((T, 2 * D), BF16)],
        compiler_params=_cp(("arbitrary",)),
    )(att, mixs, gt, h1, wattn, wpool, wout)


def merge_bwd(dh2, gt, gf, wattn, wpool, wout, after=None):
    T = dh2.shape[0]
    TM = 512
    token_spec, token_arg = _token_operand(after)

    def body(dh2_ref, gt_ref, gf_ref, wa_ref, wp_ref, wo_ref, *rest):
        datt_ref, dmixs_ref, dgt_ref, da_ref, dp_ref = rest[-5:]
        dm = _dot_nt(dh2_ref[...].astype(BF16), wo_ref[...])
        da = (dm * _sigmoid(gt_ref[:, 0:D].astype(F32))).astype(BF16)
        dp = (dm * _sigmoid(gt_ref[:, D:2 * D].astype(F32))).astype(BF16)
        da_ref[...] = da
        dp_ref[...] = dp
        dgt_ref[:, 0:D] = (dm * gf_ref[:, 0:D].astype(F32)).astype(BF16)
        dgt_ref[:, D:2 * D] = (dm * gf_ref[:, D:2 * D].astype(F32)).astype(BF16)
        datt_ref[...] = _dot_nt(da, wa_ref[...]).astype(BF16)
        dmixs_ref[...] = _dot_nt(dp, wp_ref[...])

    tile = lambda w: pl.BlockSpec((TM, w), lambda i: (i, 0))
    return pl.pallas_call(
        body, name="merge_bwd", grid=(T // TM,),
        in_specs=[tile(D), tile(2 * D), tile(2 * D), _const_spec((AW, D)), _const_spec((PW, D)),
                  _const_spec((D, D))] + token_spec,
        out_specs=[tile(AW), tile(PW), tile(2 * D), tile(D), tile(D)],
        out_shape=[SDS((T, AW), BF16), SDS((T, PW), F32), SDS((T, 2 * D), BF16), SDS((T, D), BF16),
                   SDS((T, D), BF16)],
        compiler_params=_cp(("arbitrary",)),
    )(dh2, gt, gf, wattn, wpool, wout, *token_arg)


def adamw(updates, place, name, after=None):
    tile_bytes = 2 * 1024 * 1024 // len(updates)
    jobs = []
    for w, g, m, v in updates:
        R, C = w.shape
        tr = R
        if R * C * 4 > tile_bytes:
            tr = next(cand for cand in (512, 256, 128, 64, 32, 16, 8) if R % cand == 0 and cand * C * 4 <= tile_bytes)
        blk = ((tr, C), lambda ls, p: (ls, 0))
        jobs.append((R // tr, [(a, *blk) for a in (w, g, m, v)], [((R, C), F32, *blk)] * 4,
                     lambda ins, outs: _adamw_tile(*ins, *outs)))
    return _run_jobs(jobs, place, name, after)


GROUP_UP1, GROUP_MIX, GROUP_FFN2 = 0, 1, 2


def _behind(small, token):
    return small if token is None else small + token[0:1, 0:1]


def local_fwd_bwd(x, target, S, comm):
    w_up1 = comm.weights(GROUP_UP1, None)["ffn1_w_up"]
    ab1, hid1 = ffn_up(x, _behind(S["ffn1_norm"], comm.started()), w_up1, "ffn1_up")
    W = comm.weights(GROUP_MIX, hid1)
    h1 = ffn_down(x, hid1, W["ffn1_w_down"], "ffn1_down", after=comm.started())
    u, q, kv, z, gt = mix_in_fwd(h1, S["mix_norm"], W["w_in"])
    att, lse = attn_fwd(q, kv, S["sinks"])
    pooled, mixs = pool_fwd(z, S["pool_w_mix"], _behind(S["pool_scale"], comm.prefetch(GROUP_FFN2, att)))
    h2, merged, gf = merge_fwd(att, mixs, gt, h1, W["w_attn_up"], W["w_pool_up"], W["w_out"])
    W = comm.weights(GROUP_FFN2, h2)
    dh3, ab2, hid2, loss, g_final = ffn_fwd(h2, S["ffn2_norm"], W["ffn2_w_up"], W["ffn2_w_down"], "ffn2_fwd",
                                            head=(target, S["final_norm"]))

    G = {"final_norm": g_final}
    dh2, dup2, n2, G["ffn2_norm"] = ffn_bwd_x(dh3, h2, ab2, S["ffn2_norm"], W["ffn2_w_up"], W["ffn2_w_down"],
                                              "ffn2_bwd_x")
    ffn2 = ("ffn2_w_down", "ffn2_w_up")
    token = comm.grads({"ffn2_w_down": matmul_tn(hid2, dh3, "ffn2_dw_down", tm=1408, tn=512, tt=4096, b_scale=0.5),
                        "ffn2_w_up": matmul_tn(n2, dup2, "ffn2_dw_up", tm=D, tn=512, tt=4096)})

    W = comm.weights(GROUP_MIX, None)
    datt, dmixs, dgt, da, dp = merge_bwd(dh2, gt, gf, W["w_attn_up"], W["w_pool_up"], W["w_out"], after=token)
    token = comm.advance(ffn2, datt)
    g_mix = {"w_out": matmul_tn(merged, dh2, "dw_out", tm=D, tn=D, after=token),
             "w_attn_up": matmul_tn(att, da, "dw_attn_up", tm=AW, tn=D),
             "w_pool_up": matmul_tn(mixs, dp, "dw_pool_up", tm=PW, tn=D)}
    dz, G["pool_w_mix"], G["pool_scale"] = pool_bwd(dmixs, pooled, S["pool_w_mix"], _behind(S["pool_scale"], token))
    dq, dkv, G["sinks"] = attn_bwd(q, kv, datt, att, lse, _behind(S["sinks"], token))
    dh1, G["mix_norm"] = mix_in_bwd(dq, dkv, dz, dgt, dh2, h1, S["mix_norm"], W["w_in"])
    g_mix["w_in"] = jnp.concatenate([
        matmul_tn(dq, u, "dw_in_q", tm=AW, tn=D),
        matmul_tn(dkv, u, "dw_in_kv", tm=2 * KVW, tn=D),
        matmul_tn(dz, u, "dw_in_z", tm=PW, tn=D),
        matmul_tn(dgt, u, "dw_in_g", tm=D, tn=512, tt=4096),
    ], axis=0)
    mix = tuple(g_mix)
    token = comm.grads(g_mix)

    g_dn1 = matmul_tn(hid1, dh1, "ffn1_dw_down", tm=1408, tn=512, tt=4096, b_scale=0.5, after=token)
    token = comm.advance(mix, g_dn1)
    token2 = comm.grads({"ffn1_w_down": g_dn1})
    g1 = _behind(_behind(S["ffn1_norm"], token), token2)
    dx, dup1, n1, G["ffn1_norm"], *outs = ffn_bwd_x(dh1, x, ab1, g1, w_up1, W["ffn1_w_down"], "ffn1_bwd_x",
                                                   updates=comm.reduce(ffn2, token2))
    comm.updated(ffn2, outs)
    token = comm.advance(("ffn1_w_down",), dx)
    token2 = comm.small_start(G, loss, dx)
    g_up1 = matmul_tn(n1, dup1, "ffn1_dw_up", tm=D, tn=512, tt=4096,
                      after=None if token is None else token + token2)
    token = comm.grads({"ffn1_w_up": g_up1})
    updates = comm.reduce(("ffn1_w_down",), token)
    token = comm.advance(("ffn1_w_up",), [u[1] for u in updates])
    done = comm.update(("ffn1_w_down",), updates, token)
    done = comm.finish(mix, done)
    done = comm.small_finish(done)
    comm.finish(("ffn1_w_up",), done)
    return loss, dx


HBM_SPEC = pl.BlockSpec(memory_space=pltpu.HBM)


def _me():
    return lax.axis_index("x"), lax.axis_index("y"), lax.axis_index("c")


def _peer_chip(x, y, k):
    return x ^ (k >> 1), y ^ (k & 1)


def _piece_half(ref, rowlike, j, h):
    if rowlike:
        return ref.at[j, h]
    ns = ref.shape[-1] // N_CHIPS
    return ref.at[h, :, pl.ds(pl.multiple_of(j * ns, 128), ns)]


def _piece(ref, rowlike, j):
    if rowlike:
        return ref.at[j]
    ns = ref.shape[-1] // N_CHIPS
    return ref.at[:, :, pl.ds(pl.multiple_of(j * ns, 128), ns)]


def _full_shape(shard_view, rowlike):
    _, kh, ns = shard_view.shape
    return (N_CHIPS, 2, kh, ns) if rowlike else (2, kh, N_CHIPS * ns)


def _remote(src, dst, send_sem, recv_sem, dev):
    return pltpu.make_async_remote_copy(src, dst, send_sem, recv_sem, device_id=dev, device_id_type=MESH)


SEM_SPEC = pl.BlockSpec(memory_space=pltpu.SEMAPHORE)
SPLIT_PARAMS = pltpu.CompilerParams(has_side_effects=pltpu.SideEffectType.DATAFLOW_SIDE_EFFECTING)


def _gather_plan(rowlikes):
    def plan(s_refs, f_refs, a, k, x, y, c):
        px, py = _peer_chip(x, y, k)
        return (s_refs[a].at[c], _piece_half(f_refs[a], rowlikes[a], 2 * x + y, c),
                _piece_half(f_refs[a], rowlikes[a], 2 * px + py, c), (px, py, c))
    return plan


def _all_to_all_plan(rowlikes):
    def plan(q_refs, r_refs, a, k, x, y, c):
        px, py = _peer_chip(x, y, k)
        if rowlikes[a]:
            src = q_refs[a].at[2 * px + py]
        else:
            ns = q_refs[a].shape[-1] // N_CHIPS
            src = q_refs[a].at[:, pl.ds(pl.multiple_of((2 * px + py) * ns, 128), ns)]
        return src, r_refs[a].at[k - 1], r_refs[a].at[k - 1], (px, py, c)
    return plan


def _swap_plan(rowlikes):
    def plan(g_refs, got_refs, a, k, x, y, c):
        src = g_refs[a].at[:, 1 - c] if rowlikes[a] else g_refs[a].at[1 - c]
        return src, got_refs[a], got_refs[a], (x, y, 1 - c)
    return plan


def _everyone_plan(s_refs, slot_refs, a, k, x, y, c):
    px, py, pc = x ^ (k >> 2), y ^ ((k >> 1) & 1), c ^ (k & 1)
    return s_refs[a], slot_refs[a].at[4 * x + 2 * y + c], slot_refs[a].at[4 * px + 2 * py + pc], (px, py, pc)


def _forward_plan(rowlikes):
    def plan(s_refs, f_refs, a, k, x, y, c):
        sib = (x, y, 1 - c)
        if k == 0:
            own = _piece(f_refs[a], rowlikes[a], 2 * x + y)
            return s_refs[a], own, own, sib
        px, py = _peer_chip(x, y, k)
        mine = _piece_half(f_refs[a], rowlikes[a], 2 * px + py, c)
        return mine, mine, _piece_half(f_refs[a], rowlikes[a], 2 * px + py, 1 - c), sib
    return plan


CHIPS, SIBLING, EVERYONE, FORWARDS = (1, 2, 3), (1,), tuple(range(1, 8)), (0, 1, 2, 3)


def _as_list(after):
    return [] if after is None else list(after) if isinstance(after, (list, tuple)) else [after]


def exchange_start(srcs, land_shapes, plan, after, name, peers=CHIPS):
    n = len(srcs)
    lands = [l if hasattr(l, "dtype") else lax.empty(l, s.dtype) for l, s in zip(land_shapes, srcs)]

    behind = _as_list(after)

    def body(*refs):
        s_refs, l_refs = refs[:n], refs[n:2 * n]
        send_sems, recv_sems = refs[2 * n + len(behind)], refs[2 * n + len(behind) + 1]
        token = refs[-1]
        x, y, c = _me()
        for a in range(n):
            for i, k in enumerate(peers):
                src, dst, _, peer = plan(s_refs, l_refs, a, k, x, y, c)
                sem = len(peers) * a + i
                _remote(src, dst, send_sems.at[sem], recv_sems.at[sem], peer).start()
        token[...] = jnp.zeros_like(token)

    n_sems = len(peers) * n
    outs = pl.pallas_call(
        body, name=name, in_specs=[HBM_SPEC] * (2 * n) + [pl.BlockSpec(memory_space=pl.ANY)] * len(behind),
        out_specs=[SEM_SPEC, SEM_SPEC] + [HBM_SPEC] * (2 * n) + [pl.BlockSpec(memory_space=pltpu.VMEM)],
        out_shape=[pltpu.SemaphoreType.DMA((n_sems,)), pltpu.SemaphoreType.DMA((n_sems,))]
        + [pltpu.HBM(a.shape, a.dtype) for a in (*srcs, *lands)] + [SDS((8, 128), F32)],
        input_output_aliases={i: 2 + i for i in range(2 * n)},
        compiler_params=SPLIT_PARAMS,
    )(*[pltpu.with_memory_space_constraint(a, pltpu.HBM) for a in (*srcs, *lands)], *behind)
    return {"sems": outs[:2], "srcs": outs[2:2 + n], "lands": outs[2 + n:2 + 2 * n], "token": outs[-1]}


def exchange_wait(state, plan, after, name, peers=CHIPS):
    n = len(state["srcs"])
    behind = _as_list(after)

    def body(*refs):
        s_refs, l_refs = refs[:n], refs[n:2 * n]
        send_sems, recv_sems = refs[2 * n], refs[2 * n + 1]
        x, y, c = _me()
        for a in range(n):
            for i, k in enumerate(peers):
                src, _, landing, peer = plan(s_refs, l_refs, a, k, x, y, c)
                sem = len(peers) * a + i
                cp = _remote(src, landing, send_sems.at[sem], recv_sems.at[sem], peer)
                cp.wait_send()
                cp.wait_recv()

    bufs = (*state["srcs"], *state["lands"])
    outs = pl.pallas_call(
        body, name=name,
        in_specs=[HBM_SPEC] * (2 * n) + [SEM_SPEC, SEM_SPEC] + [pl.BlockSpec(memory_space=pl.ANY)] * len(behind),
        out_specs=[HBM_SPEC] * (2 * n),
        out_shape=[pltpu.HBM(a.shape, a.dtype) for a in bufs],
        input_output_aliases={i: i for i in range(2 * n)},
        compiler_params=SPLIT_PARAMS,
    )(*bufs, *state["sems"], *behind)
    return outs[:n], outs[n:]


def gather_finish(shards, fulls, rowlikes, name):
    n = len(shards)

    def body(*refs):
        s_refs, f_refs = refs[:n], refs[2 * n:3 * n]
        send_sems, recv_sems = refs[3 * n:]
        x, y, c = _me()
        chip = 2 * x + y
        sib = (x, y, 1 - c)
        sends = []
        for a in range(n):
            own = _piece(f_refs[a], rowlikes[a], chip)
            cp = _remote(s_refs[a], own, send_sems.at[a, 0], recv_sems.at[a, 0], sib)
            cp.start()
            sends.append(cp)
            for k in (1, 2, 3):
                px, py = _peer_chip(x, y, k)
                slot = _piece_half(f_refs[a], rowlikes[a], 2 * px + py, c)
                cp = _remote(slot, slot, send_sems.at[a, k], recv_sems.at[a, k], sib)
                cp.start()
                sends.append(cp)
        for a in range(n):
            own = _piece(f_refs[a], rowlikes[a], chip)
            _remote(own, own, send_sems.at[a, 0], recv_sems.at[a, 0], sib).wait_recv()
            for k in (1, 2, 3):
                px, py = _peer_chip(x, y, k)
                slot = _piece_half(f_refs[a], rowlikes[a], 2 * px + py, 1 - c)
                _remote(slot, slot, send_sems.at[a, k], recv_sems.at[a, k], sib).wait_recv()
        for cp in sends:
            cp.wait_send()

    return pl.pallas_call(
        body, name=name, in_specs=[HBM_SPEC] * (2 * n), out_specs=[HBM_SPEC] * n,
        out_shape=[SDS(f.shape, f.dtype) for f in fulls],
        input_output_aliases={n + a: a for a in range(n)},
        scratch_shapes=[pltpu.SemaphoreType.DMA((n, 4)), pltpu.SemaphoreType.DMA((n, 4))],
    )(*shards, *fulls)


def _half_buffer_shape(gview, rowlike):
    return (N_CHIPS,) + gview.shape[2:] if rowlike else gview.shape[1:]


def _row_tiles(kh, ns):
    return 1 if kh * ns <= 256 * 1024 else 2


def _run_jobs(jobs, place, name, after=None):
    n_steps = max(job[0] for job in jobs)

    def spec(block, index, steps):
        return pl.BlockSpec(block, lambda s, p: index(jnp.minimum(s, steps - 1), p))

    in_specs = [spec(b, ix, steps) for steps, ins, _, _ in jobs for _, b, ix in ins]
    out_specs = [spec(b, ix, steps) for steps, _, outs, _ in jobs for _, _, b, ix in outs]
    token_spec, token_arg = _token_operand(after)
    n_in = len(in_specs) + len(token_spec)

    def body(place_ref, *refs):
        s = pl.program_id(0)
        i, o = 0, n_in
        for steps, ins, outs, fn in jobs:
            in_refs, out_refs = refs[i:i + len(ins)], refs[o:o + len(outs)]
            i, o = i + len(ins), o + len(outs)

            @pl.when(s < steps)
            def _(fn=fn, in_refs=in_refs, out_refs=out_refs):
                fn(in_refs, out_refs)

    return pl.pallas_call(
        body, name=name,
        grid_spec=pltpu.PrefetchScalarGridSpec(num_scalar_prefetch=1, grid=(n_steps,),
                                               in_specs=in_specs + token_spec, out_specs=out_specs),
        out_shape=[SDS(shape, dtype) for _, _, outs, _ in jobs for shape, dtype, _, _ in outs],
        compiler_params=_cp(("arbitrary",)),
    )(place, *[a for _, ins, _, _ in jobs for a, _, _ in ins], *token_arg)


def add_halves(gviews, gots, rowlikes, place, name):
    def add(ins, outs):
        outs[0][...] = (ins[0][...].astype(F32) + ins[1][...].astype(F32)).astype(BF16)

    jobs = []
    for g, got, rowlike in zip(gviews, gots, rowlikes):
        kh, ns = (g.shape[2], g.shape[3]) if rowlike else (g.shape[1], g.shape[2] // N_CHIPS)
        rt = _row_tiles(kh, ns)
        tr = kh // rt
        if rowlike:
            g_in = (g, (None, None, tr, ns), lambda ls, p, rt=rt: (ls // rt, p[0], ls % rt, 0))
            half = ((None, tr, ns), lambda ls, p, rt=rt: (ls // rt, ls % rt, 0))
        else:
            g_in = (g, (None, tr, ns), lambda ls, p, rt=rt: (p[0], ls % rt, ls // rt))
            half = ((tr, ns), lambda ls, p, rt=rt: (ls % rt, ls // rt))
        jobs.append((N_CHIPS * rt, [g_in, (got, *half)], [(got.shape, BF16, *half)], add))
    return _run_jobs(jobs, place, name)


def _slot_shape(q, rowlike):
    return (3,) + (q.shape[1:] if rowlike else (q.shape[0], q.shape[1] // N_CHIPS))


def sum_pieces(qs, recvs, rowlikes, place, name):
    def total(ins, outs):
        acc = ins[0][...].astype(F32)
        for k in range(3):
            acc = acc + ins[1][k].astype(F32)
        outs[0][...] = acc

    jobs = []
    for q, recv, rowlike in zip(qs, recvs, rowlikes):
        kh, ns = recv.shape[1], recv.shape[2]
        rt = _row_tiles(kh, ns)
        tr = kh // rt
        mine = ((None, tr, ns), lambda ls, p: (p[1], ls, 0)) if rowlike else ((tr, ns), lambda ls, p: (ls, p[1]))
        jobs.append((rt, [(q, *mine), (recv, (3, tr, ns), lambda ls, p: (0, ls, 0))],
                     [((2, kh, ns), F32, (None, tr, ns), lambda ls, p: (p[0], ls, 0))], total))
    return _run_jobs(jobs, place, name)


def join_halves(halves, name):
    n = len(halves)

    def body(*refs):
        o_refs = refs[n:2 * n]
        send_sems, recv_sems = refs[2 * n:]
        x, y, c = _me()
        sib = (x, y, 1 - c)
        sends = []
        for a in range(n):
            cp = _remote(o_refs[a].at[c], o_refs[a].at[c], send_sems.at[a], recv_sems.at[a], sib)
            cp.start()
            sends.append(cp)
        for a in range(n):
            got = o_refs[a].at[1 - c]
            _remote(got, got, send_sems.at[a], recv_sems.at[a], sib).wait_recv()
        for cp in sends:
            cp.wait_send()

    return pl.pallas_call(
        body, name=name, in_specs=[HBM_SPEC] * n, out_specs=[HBM_SPEC] * n,
        out_shape=[SDS(h.shape, h.dtype) for h in halves],
        input_output_aliases={a: a for a in range(n)},
        scratch_shapes=[pltpu.SemaphoreType.DMA((n,)), pltpu.SemaphoreType.DMA((n,))],
    )(*halves)


N_DEV = 8


def sum_devices(s, slots, me):
    R, Wd = s.shape

    def body(me_ref, s_ref, slots_ref, out_ref):
        acc = None
        for d in range(N_DEV):
            mine = me_ref[0] == d
            term = jnp.where(mine, s_ref[...], slots_ref[jnp.where(mine, d ^ 1, d)])
            acc = term if acc is None else acc + term
        out_ref[...] = acc

    vmem = pl.BlockSpec(memory_space=pltpu.VMEM)
    return pl.pallas_call(
        body, name="sum_devices", in_specs=[pl.BlockSpec(memory_space=pltpu.SMEM), vmem, vmem], out_specs=vmem,
        out_shape=SDS((R, Wd), F32),
    )(me, s, slots)


TRANSPOSED = ("w_in",)
BIG = {"ffn1_w_up": (D, 2 * FF, "col"), "ffn1_w_down": (FF, D, "row"), "w_in": (INW, D, "row"),
       "w_attn_up": (AW, D, "row"), "w_pool_up": (PW, D, "col"), "w_out": (D, D, "row"),
       "ffn2_w_up": (D, 2 * FF, "col"), "ffn2_w_down": (FF, D, "row")}
GROUPS = (("ffn1_w_up",), ("ffn1_w_down", "w_in", "w_attn_up", "w_pool_up", "w_out"), ("ffn2_w_up", "ffn2_w_down"))
SMALL = ("ffn1_norm", "mix_norm", "ffn2_norm", "final_norm", "pool_scale", "sinks", "pool_w_mix")
SMALL_W = 128


def _rowlike(name):
    return BIG[name][2] == "row"


def _half_dims(name):
    k, n, kind = BIG[name]
    return (k // N_CHIPS // 2, n) if kind == "row" else (k // 2, n // N_CHIPS)


def shard_view(name, shard):
    return shard.reshape((2,) + _half_dims(name))


def full_from_view(name, fv):
    k, n, _ = BIG[name]
    return fv.reshape(k, n)


def grad_view(name, g):
    kh, ns = _half_dims(name)
    return g.reshape(_full_shape(jax.ShapeDtypeStruct((2, kh, ns), g.dtype), _rowlike(name)))


def pack_small(d):
    parts = []
    for name in SMALL:
        a = d[name].reshape(-1)
        pad = (-a.shape[0]) % SMALL_W
        parts.append(jnp.pad(a, (0, pad)).reshape(-1, SMALL_W))
    a = jnp.concatenate(parts, axis=0)
    return jnp.pad(a, ((0, (-a.shape[0]) % 8), (0, 0)))


def unpack_small(a, like):
    out, r0 = {}, 0
    for name in SMALL:
        size = int(np.prod(like[name].shape))
        rows = -(-size // SMALL_W)
        out[name] = a[r0:r0 + rows].reshape(-1)[:size].reshape(like[name].shape)
        r0 += rows
    return out


WEIGHTS = ("ffn1_norm", "ffn1_w_up", "ffn1_w_down", "mix_norm", "w_in", "sinks", "w_attn_up", "pool_w_mix",
           "pool_scale", "w_pool_up", "w_out", "ffn2_norm", "ffn2_w_up", "ffn2_w_down", "final_norm")


def kernel(x, ffn1_norm, ffn1_w_up, ffn1_w_down, mix_norm, w_in, sinks, w_attn_up, pool_w_mix, pool_scale, w_pool_up, w_out, ffn2_norm, ffn2_w_up, ffn2_w_down, final_norm, loss_target, m_ffn1_norm, m_ffn1_w_up, m_ffn1_w_down, m_mix_norm, m_w_in, m_sinks, m_w_attn_up, m_pool_w_mix, m_pool_scale, m_w_pool_up, m_w_out, m_ffn2_norm, m_ffn2_w_up, m_ffn2_w_down, m_final_norm, v_ffn1_norm, v_ffn1_w_up, v_ffn1_w_down, v_mix_norm, v_w_in, v_sinks, v_w_attn_up, v_pool_w_mix, v_pool_scale, v_w_pool_up, v_w_out, v_ffn2_norm, v_ffn2_w_up, v_ffn2_w_down, v_final_norm):
    given = dict(locals())
    w = {n: given[n] for n in WEIGHTS}
    m = {n: given["m_" + n] for n in WEIGHTS}
    v = {n: given["v_" + n] for n in WEIGHTS}
    cx, cy, cc = _me()
    place = jnp.stack([cc, 2 * cx + cy]).astype(jnp.int32)

    def local2d(d, n):
        return d[n][0].T if n in TRANSPOSED else d[n][0]

    shards = {n: local2d(w, n) for n in BIG}
    grads, delta, new_m, new_v = {}, {}, {}, {}
    rowlikes = [[_rowlike(n) for n in names] for names in GROUPS]

    class Exchanges:
        def __init__(self):
            self.gathers, self.forwards, self.fulls, self.reductions, self.small = {}, {}, {}, {}, None

        def _behind_first(self, a):
            return a + self.gathers[0]["token"][0, 0]

        def _start_gather(self, group, after):
            prepare = self._behind_first if group else (lambda a: a)
            sv = [shard_view(n, prepare(shards[n]).astype(BF16)) for n in GROUPS[group]]
            rl = rowlikes[group]
            self.gathers[group] = exchange_start(sv, [_full_shape(s, r) for s, r in zip(sv, rl)], _gather_plan(rl),
                                                 after, f"gather_start_{group}")

        def weights(self, group, after):
            if not self.gathers:
                self._start_gather(0, None)
                self.packed = [self._behind_first(pack_small(d)) for d in (w, m, v)]
                after = [self.gathers[0]["token"], *self.packed]
            if group not in self.fulls:
                names, rl, state = GROUPS[group], rowlikes[group], self.gathers[group]
                if group in self.forwards:
                    _, fulls = exchange_wait(self.forwards.pop(group), _forward_plan(rl), after,
                                             f"forward_wait_{group}", FORWARDS)
                else:
                    sv, fulls = exchange_wait(state, _gather_plan(rl), state["token"] if after is None else after,
                                              f"gather_wait_{group}")
                    fulls = gather_finish(sv, fulls, rl, f"gather_finish_{group}")
                self.fulls[group] = {n: full_from_view(n, f) for n, f in zip(names, fulls)}
                if group + 1 < len(GROUPS):
                    self._start_gather(group + 1, fulls[0])
            return self.fulls[group]

        def started(self):
            return self.gathers[max(self.gathers)]["token"]

        def prefetch(self, group, after):
            rl, state = rowlikes[group], self.gathers[group]
            sv, fulls = exchange_wait(state, _gather_plan(rl), after, f"gather_wait_{group}")
            self.forwards[group] = exchange_start(sv, fulls, _forward_plan(rl), None, f"forward_start_{group}", FORWARDS)
            return self.forwards[group]["token"]

        def grads(self, g):
            names = tuple(g)
            rl = [_rowlike(n) for n in names]
            gv = [grad_view(n, g[n]) for n in names]
            state = exchange_start(gv, [_half_buffer_shape(a, r) for a, r in zip(gv, rl)], _swap_plan(rl), None,
                                   "swap_start_" + names[0], SIBLING)
            self.reductions[names] = state
            return state["token"]

        def advance(self, names, after):
            rl = [_rowlike(n) for n in names]
            gv, gots = exchange_wait(self.reductions[names], _swap_plan(rl), after, "swap_wait_" + names[0], SIBLING)
            qs = add_halves(gv, gots, rl, place, "add_halves_" + names[0])
            state = exchange_start(qs, [_slot_shape(q, r) for q, r in zip(qs, rl)], _all_to_all_plan(rl), None,
                                   "all_to_all_start_" + names[0])
            self.reductions[names] = state
            return state["token"]

        def reduce(self, names, after):
            rl = [_rowlike(n) for n in names]
            qs, recvs = exchange_wait(self.reductions.pop(names), _all_to_all_plan(rl), after,
                                      "all_to_all_wait_" + names[0])
            halves = sum_pieces(qs, recvs, rl, place, "sum_pieces_" + names[0])
            return [(shards[n], o.reshape(shards[n].shape), local2d(m, n), local2d(v, n))
                    for n, o in zip(names, join_halves(halves, "join_halves_" + names[0]))]

        def updated(self, names, outs):
            for i, n in enumerate(names):
                grads[n], delta[n], new_m[n], new_v[n] = outs[4 * i:4 * i + 4]
            return [new_v[n] for n in names]

        def update(self, names, updates, token=None):
            return self.updated(names, adamw(updates, place, "adamw_" + names[0], after=token))

        def finish(self, names, after):
            return self.update(names, self.reduce(names, after))

        def small_start(self, G, loss, after):
            packed = pack_small({n: G[n] for n in SMALL})
            used_rows = sum(-(-int(np.prod(small_like[n].shape)) // SMALL_W) for n in SMALL)
            assert packed.shape[0] > used_rows
            packed = packed.at[-1, 0].set(loss[0, 0])
            self.small = exchange_start([packed], [(N_DEV,) + packed.shape], _everyone_plan, after,
                                        "small_start", EVERYONE)
            return self.small["token"]

        def small_finish(self, after):
            (packed,), (slots,) = exchange_wait(self.small, _everyone_plan, after, "small_wait", EVERYONE)
            total = sum_devices(packed, slots, (4 * cx + 2 * cy + cc).astype(jnp.int32).reshape(1))
            self.loss = total[-1, 0]
            g, ds, ms, vs = adamw([(self.packed[0], total, self.packed[1], self.packed[2])], place, "adamw_small")
            for d, packed_d in ((grads, g), (delta, ds), (new_m, ms), (new_v, vs)):
                d.update(unpack_small(packed_d, small_like))
            return vs

    small_like = {n: w[n] for n in SMALL}
    S = {n: w[n].reshape(1, -1) for n in ("ffn1_norm", "mix_norm", "ffn2_norm", "final_norm", "pool_scale", "sinks")}
    S["pool_w_mix"] = w["pool_w_mix"][0].astype(BF16)
    exchanges = Exchanges()
    _, dx = local_fwd_bwd(x[0], loss_target[0], S, exchanges)
    loss = exchanges.loss

    def shaped(d, n):
        return (d[n].T if n in TRANSPOSED else d[n]).reshape(w[n].shape)

    return (loss, dx[None], *[shaped(grads, n) for n in WEIGHTS], *[shaped(delta, n) for n in WEIGHTS],
            *[shaped(new_m, n) for n in WEIGHTS], *[shaped(new_v, n) for n in WEIGHTS])
```

```python
import numpy as np
import jax
import jax.numpy as jnp
from jax import lax
from jax.experimental import pallas as pl
from jax.experimental.pallas import tpu as pltpu

F32 = jnp.float32
BF16 = jnp.bfloat16
SDS = jax.ShapeDtypeStruct
MESH = pl.DeviceIdType.MESH

D = 1024
FF = 2816
NQ = 16
NKV = 2
HD = 64
GQ = NQ // NKV
AW = NQ * HD
KVW = NKV * HD
BLK = 128
PW = 512
PG = 128
POOL_WINDOWS = (2, 4, 8, 16)
HALO = 16
INW = AW + 2 * KVW + PW + 2 * D
C_KV = AW
C_Z = AW + 2 * KVW
C_G = C_Z + PW
EPS = 1e-6
FF_CHUNK = 256
FF_CHUNKS = tuple((c, FF_CHUNK) for c in range(0, FF, FF_CHUNK))
SLOPES = tuple(float(2.0 ** (-8.0 * h / NQ)) for h in range(1, NQ + 1))
SCALE = HD ** -0.5

LR, B1, B2, ADAM_EPS, WD, STEP = 0.001, 0.9, 0.999, 1e-08, 0.01, 10

VMEM_LIMIT = 56 * 1024 * 1024
N_CHIPS = 4

NT = (((1,), (1,)), ((), ()))
TN = (((0,), (0,)), ((), ()))


def _cp(sem=None, vmem=VMEM_LIMIT):
    return pltpu.CompilerParams(dimension_semantics=sem, vmem_limit_bytes=vmem)


def _const_spec(shape):
    nd = len(shape)
    return pl.BlockSpec(shape, lambda *_: (0,) * nd, pipeline_mode=pl.Buffered(1))


def _rstd(x):
    return lax.rsqrt(jnp.mean(x * x, axis=-1, keepdims=True) + EPS)


def _rms_bwd(dn, xhat, rstd, g):
    dxhat = dn * g
    return rstd * (dxhat - xhat * jnp.mean(dxhat * xhat, axis=-1, keepdims=True))


def _sigmoid(x):
    return 0.5 * jnp.tanh(0.5 * x) + 0.5


def _dot(a, b):
    return jnp.dot(a, b, preferred_element_type=F32)


def _dot_nt(a, b):
    return lax.dot_general(a, b, NT, preferred_element_type=F32)


def _dot_tn(a, b):
    return lax.dot_general(a, b, TN, preferred_element_type=F32)


ANY_SPEC = pl.BlockSpec(memory_space=pl.ANY)
UP_SLABS = tuple((slice(None), slice(c, c + w)) for c0, w in FF_CHUNKS for c in (c0, FF + c0))
DOWN_SLABS = tuple((slice(c0, c0 + w), slice(None)) for c0, w in FF_CHUNKS)
WHOLE = ((slice(None), slice(None)),)


def _streamed(weights, slabs, sems, compute):
    loads = [pltpu.make_async_copy(hbm.at[s], vmem.at[s], sems.at[k])
             for k, (hbm, vmem, s) in enumerate((h, v, s) for (h, v), ss in zip(weights, slabs) for s in ss)]
    first = pl.program_id(0) == 0

    @pl.when(first)
    def _():
        for load in loads:
            load.start()
        compute(lambda *ks: [loads[k].wait() for k in ks])

    @pl.when(jnp.logical_not(first))
    def _():
        compute(lambda *ks: None)


def _swiglu_up(x, g, wup_ref, ab_ref, hid_ref, arrived):
    n = (x * _rstd(x) * g).astype(BF16)
    for k, (c0, w) in enumerate(FF_CHUNKS):
        arrived(2 * k, 2 * k + 1)
        a = _dot(n, wup_ref[:, c0:c0 + w])
        b = _dot(n, wup_ref[:, FF + c0:FF + c0 + w])
        sig = _sigmoid(a)
        s = a * sig
        ab_ref[:, c0:c0 + w] = (b * (sig + s * (1.0 - sig))).astype(BF16)
        ab_ref[:, FF + c0:FF + c0 + w] = s.astype(BF16)
        hid_ref[:, c0:c0 + w] = (s * b).astype(BF16)


def ffn_up(h, g, wup, name):
    T = h.shape[0]
    TM = 512
    tile = lambda w: pl.BlockSpec((TM, w), lambda i: (i, 0))

    def body(h_ref, g_ref, wup_hbm, ab_ref, hid_ref, wup_ref, sems):
        _streamed([(wup_hbm, wup_ref)], [UP_SLABS], sems,
                  lambda arrived: _swiglu_up(h_ref[...], g_ref[...], wup_ref, ab_ref, hid_ref, arrived))

    return pl.pallas_call(
        body, name=name, grid=(T // TM,),
        in_specs=[tile(D), _const_spec((1, D)), ANY_SPEC],
        out_specs=[tile(2 * FF), tile(FF)],
        out_shape=[SDS((T, 2 * FF), BF16), SDS((T, FF), BF16)],
        scratch_shapes=[pltpu.VMEM((D, 2 * FF), BF16), pltpu.SemaphoreType.DMA((len(UP_SLABS),))],
        compiler_params=_cp(("arbitrary",)),
    )(h, g, wup)


def ffn_down(h, hid, wdn, name, after=None):
    T = h.shape[0]
    TM = min(1024, T)
    tile = lambda w: pl.BlockSpec((TM, w), lambda i: (i, 0))
    token_spec, token_arg = _token_operand(after)

    def body(h_ref, hid_ref, wdn_ref, *rest):
        rest[-1][...] = h_ref[...] + 0.5 * _dot(hid_ref[...], wdn_ref[...])

    return pl.pallas_call(
        body, name=name, grid=(T // TM,),
        in_specs=[tile(D), tile(FF), _const_spec((FF, D))] + token_spec,
        out_specs=tile(D), out_shape=SDS((T, D), F32),
        compiler_params=_cp(("arbitrary",)),
    )(h, hid, wdn, *token_arg)


def ffn_fwd(h, g, wup, wdn, name, head=None):
    T = h.shape[0]
    TM = 512
    tile = lambda w: pl.BlockSpec((TM, w), lambda i: (i, 0))
    acc_spec = lambda w: pl.BlockSpec((1, w), lambda i: (0, 0))

    def body(h_ref, g_ref, wup_hbm, wdn_hbm, *rest):
        wup_ref, wdn_ref, sems = rest[-3:]
        out_ref, ab_ref, hid_ref = rest[-6:-3] if head is None else rest[2:5]

        def step(arrived):
            x = h_ref[...]
            _swiglu_up(x, g_ref[...], wup_ref, ab_ref, hid_ref, arrived)
            arrived(len(UP_SLABS))
            out = x + 0.5 * _dot(hid_ref[...], wdn_ref[...])
            if head is None:
                out_ref[...] = out
            else:
                t_ref, gf_ref, loss_ref, dgf_ref = rest[0], rest[1], rest[5], rest[6]
                out_ref[...] = _loss_head(out, t_ref[...], gf_ref[...], loss_ref, dgf_ref, pl.program_id(0) == 0)

        _streamed([(wup_hbm, wup_ref), (wdn_hbm, wdn_ref)], [UP_SLABS, WHOLE], sems, step)

    head_in, head_specs, head_out_specs, head_out_shape = [], [], [], []
    if head is not None:
        head_in, head_specs = list(head), [tile(D), _const_spec((1, D))]
        head_out_specs, head_out_shape = [acc_spec(1), acc_spec(D)], [SDS((1, 1), F32), SDS((1, D), F32)]
    return pl.pallas_call(
        body, name=name, grid=(T // TM,),
        in_specs=[tile(D), _const_spec((1, D)), ANY_SPEC, ANY_SPEC] + head_specs,
        out_specs=[tile(D), tile(2 * FF), tile(FF)] + head_out_specs,
        out_shape=[SDS((T, D), F32), SDS((T, 2 * FF), BF16), SDS((T, FF), BF16)] + head_out_shape,
        scratch_shapes=[pltpu.VMEM((D, 2 * FF), BF16), pltpu.VMEM((FF, D), BF16),
                        pltpu.SemaphoreType.DMA((len(UP_SLABS) + 1,))],
        compiler_params=_cp(("arbitrary",)),
    )(h, g, wup, wdn, *head_in)


def _loss_head(x, target, g, loss_ref, dg_ref, first):
    @pl.when(first)
    def _():
        loss_ref[...] = jnp.zeros_like(loss_ref)
        dg_ref[...] = jnp.zeros_like(dg_ref)

    rstd = _rstd(x)
    xhat = x * rstd
    err = xhat * g - target
    loss_ref[...] += 0.5 * jnp.sum(jnp.mean(err * err, axis=-1, keepdims=True), axis=0, keepdims=True)
    dy = err * (1.0 / D)
    dg_ref[...] += jnp.sum(dy * xhat, axis=0, keepdims=True)
    return _rms_bwd(dy, xhat, rstd, g)


def _adamw_tile(w_ref, g_ref, m_ref, v_ref, go_ref, d_ref, nm_ref, nv_ref):
    gv = g_ref[...]
    go_ref[...] = gv
    nm = B1 * m_ref[...] + (1.0 - B1) * gv
    nv = B2 * v_ref[...] + (1.0 - B2) * (gv * gv)
    nm_ref[...] = nm
    nv_ref[...] = nv
    d_ref[...] = -LR * ((nm / (1.0 - B1 ** STEP)) / (jnp.sqrt(nv / (1.0 - B2 ** STEP)) + ADAM_EPS) + WD * w_ref[...])


def _riders(updates, steps):
    in_specs, out_specs, out_shapes, operands, tiles = [], [], [], [], []
    for w, g, m, v in updates:
        R, C = w.shape
        n = max(d for d in range(1, steps + 1) if R % d == 0 and (R // d) % 8 == 0)
        spec = pl.BlockSpec((R // n, C), lambda i, n=n: (jnp.minimum(i, n - 1), 0))
        in_specs += [spec] * 4
        out_specs += [spec] * 4
        out_shapes += [SDS((R, C), F32)] * 4
        operands += [w, g, m, v]
        tiles.append(n)

    def run(step, in_refs, out_refs):
        for u, n in enumerate(tiles):
            @pl.when(step < n)
            def _(u=u):
                _adamw_tile(*in_refs[4 * u:4 * u + 4], *out_refs[4 * u:4 * u + 4])

    return in_specs, out_specs, out_shapes, operands, run


def ffn_bwd_x(dh, h_in, ab, g, wup, wdn, name, updates=()):
    T = dh.shape[0]
    TM = 256 if updates else 512
    SUB = 256
    r_in, r_out, r_shapes, r_args, ride = _riders(updates, T // TM)

    def body(dh_ref, h_ref, ab_ref, g_ref, wdn_hbm, wup_hbm, *rest):
        dhin_ref, dup_ref, n_ref, dg_ref = rest[len(r_in):len(r_in) + 4]
        wdn_ref, wup_ref, sems = rest[-3:]
        ride(pl.program_id(0), rest[:len(r_in)], rest[len(r_in) + 4:-3])

        def step(arrived):
            g = g_ref[...]
            dg = None
            for r0 in range(0, TM, SUB):
                rows = slice(r0, r0 + SUB)
                x = h_ref[rows, :]
                rstd = _rstd(x)
                xhat = x * rstd
                n_ref[rows, :] = (xhat * g).astype(BF16)
                dh = dh_ref[rows, :]
                dhh = (0.5 * dh).astype(BF16)
                for k, (c0, w) in enumerate(FF_CHUNKS):
                    if r0 == 0:
                        arrived(k)
                    dhid = _dot_nt(dhh, wdn_ref[c0:c0 + w, :]).astype(BF16)
                    dup_ref[rows, c0:c0 + w] = dhid * ab_ref[rows, c0:c0 + w]
                    dup_ref[rows, FF + c0:FF + c0 + w] = dhid * ab_ref[rows, FF + c0:FF + c0 + w]
                if r0 == 0:
                    arrived(len(DOWN_SLABS))
                dn = _dot_nt(dup_ref[rows, :], wup_ref[...])
                dhin_ref[rows, :] = dh + _rms_bwd(dn, xhat, rstd, g)
                part = jnp.sum(dn * xhat, axis=0, keepdims=True)
                dg = part if dg is None else dg + part

            @pl.when(pl.program_id(0) == 0)
            def _():
                dg_ref[...] = jnp.zeros_like(dg_ref)

            dg_ref[...] += dg

        _streamed([(wdn_hbm, wdn_ref), (wup_hbm, wup_ref)], [DOWN_SLABS, WHOLE], sems, step)

    tile = lambda w: pl.BlockSpec((TM, w), lambda i: (i, 0))
    return pl.pallas_call(
        body, name=name, grid=(T // TM,),
        in_specs=[tile(D), tile(D), tile(2 * FF), _const_spec((1, D)), ANY_SPEC, ANY_SPEC] + r_in,
        out_specs=[tile(D), tile(2 * FF), tile(D), pl.BlockSpec((1, D), lambda i: (0, 0))] + r_out,
        out_shape=[SDS((T, D), F32), SDS((T, 2 * FF), BF16), SDS((T, D), BF16), SDS((1, D), F32)] + r_shapes,
        scratch_shapes=[pltpu.VMEM((FF, D), BF16), pltpu.VMEM((D, 2 * FF), BF16),
                        pltpu.SemaphoreType.DMA((len(DOWN_SLABS) + 1,))],
        compiler_params=_cp(("arbitrary",)),
    )(dh, h_in, ab, g, wdn, wup, *r_args)


TOKEN_SPEC = pl.BlockSpec((8, 128), lambda *_: (0, 0))


def _token_operand(token):
    return ([], []) if token is None else ([TOKEN_SPEC], [token])


def matmul_tn(a, b, name, *, tm, tn, tt=1024, b_scale=None, after=None):
    T, M = a.shape
    N = b.shape[1]
    tt = min(tt, T)
    assert M % tm == 0 and N % tn == 0 and T % tt == 0
    nt = T // tt
    token_spec, token_arg = _token_operand(after)

    def body(a_ref, b_ref, *rest):
        o_ref, acc_ref = rest[-2:]
        t = pl.program_id(2)

        @pl.when(t == 0)
        def _():
            acc_ref[...] = jnp.zeros_like(acc_ref)

        bv = b_ref[...]
        if b_scale is not None:
            bv = bv * b_scale
        acc_ref[...] += _dot_tn(a_ref[...].astype(BF16), bv.astype(BF16))

        @pl.when(t == nt - 1)
        def _():
            o_ref[...] = acc_ref[...].astype(BF16)

    return pl.pallas_call(
        body, name=name, grid=(M // tm, N // tn, nt),
        in_specs=[pl.BlockSpec((tt, tm), lambda i, j, t: (t, i)), pl.BlockSpec((tt, tn), lambda i, j, t: (t, j))]
        + token_spec,
        out_specs=pl.BlockSpec((tm, tn), lambda i, j, t: (i, j)),
        out_shape=SDS((M, N), BF16),
        scratch_shapes=[pltpu.VMEM((tm, tn), F32)],
        compiler_params=_cp(("parallel", "parallel", "arbitrary")),
    )(a, b, *token_arg)


def mix_in_fwd(h1, g, win_t):
    T = h1.shape[0]
    TM = 512

    def body(h_ref, g_ref, w_ref, u_ref, q_ref, kv_ref, z_ref, gt_ref):
        x = h_ref[...]
        u = (x * _rstd(x) * g_ref[...]).astype(BF16)
        u_ref[...] = u
        for c in range(0, AW, 256):
            q_ref[:, c:c + 256] = _dot_nt(u, w_ref[c:c + 256, :]).astype(BF16)
        kv_ref[...] = _dot_nt(u, w_ref[C_KV:C_Z, :]).astype(BF16)
        for c in range(0, PW, 256):
            z_ref[:, c:c + 256] = _dot_nt(u, w_ref[C_Z + c:C_Z + c + 256, :])
        for c in range(0, 2 * D, 256):
            gt_ref[:, c:c + 256] = _dot_nt(u, w_ref[C_G + c:C_G + c + 256, :]).astype(BF16)

    tile = lambda w: pl.BlockSpec((TM, w), lambda i: (i, 0))
    return pl.pallas_call(
        body, name="mix_in_fwd", grid=(T // TM,),
        in_specs=[tile(D), _const_spec((1, D)), _const_spec((INW, D))],
        out_specs=[tile(D), tile(AW), tile(2 * KVW), tile(PW), tile(2 * D)],
        out_shape=[SDS((T, D), BF16), SDS((T, AW), BF16), SDS((T, 2 * KVW), BF16), SDS((T, PW), F32),
                   SDS((T, 2 * D), BF16)],
        compiler_params=_cp(("arbitrary",)),
    )(h1, g, win_t)


def mix_in_bwd(dq, dkv, dz, dgt, dh2, h1, g, win_t):
    T = h1.shape[0]
    TM = min(1024, T)
    SUB = 256

    def body(dq_ref, dkv_ref, dz_ref, dgt_ref, dh2_ref, h_ref, g_ref, w_ref, dh1_ref, dg_ref):
        g = g_ref[...]
        dg = None
        for r0 in range(0, TM, SUB):
            rows = slice(r0, r0 + SUB)
            du = _dot(dq_ref[rows, :], w_ref[0:AW, :])
            du += _dot(dkv_ref[rows, :], w_ref[C_KV:C_Z, :])
            du += _dot(dz_ref[rows, :], w_ref[C_Z:C_G, :])
            du += _dot(dgt_ref[rows, :], w_ref[C_G:INW, :])
            x = h_ref[rows, :]
            rstd = _rstd(x)
            xhat = x * rstd
            dh1_ref[rows, :] = dh2_ref[rows, :] + _rms_bwd(du, xhat, rstd, g)
            part = jnp.sum(du * xhat, axis=0, keepdims=True)
            dg = part if dg is None else dg + part

        @pl.when(pl.program_id(0) == 0)
        def _():
            dg_ref[...] = jnp.zeros_like(dg_ref)

        dg_ref[...] += dg

    tile = lambda w: pl.BlockSpec((TM, w), lambda i: (i, 0))
    return pl.pallas_call(
        body, name="mix_in_bwd", grid=(T // TM,),
        in_specs=[tile(AW), tile(2 * KVW), tile(PW), tile(2 * D), tile(D), tile(D), _const_spec((1, D)),
                  _const_spec((INW, D))],
        out_specs=[tile(D), pl.BlockSpec((1, D), lambda i: (0, 0))],
        out_shape=[SDS((T, D), F32), SDS((1, D), F32)],
        compiler_params=_cp(("arbitrary",)),
    )(dq, dkv, dz, dgt, dh2, h1, g, win_t)


PAIR = 2 * HD
NPAIR = GQ // 2


def _lo_lanes():
    return lax.broadcasted_iota(jnp.int32, (BLK, PAIR), 1) < HD


def _stack_heads(ref, kvh, scale=None):
    lo = _lo_lanes()
    parts = []
    for pr in range(NPAIR):
        t = ref[:, (kvh * NPAIR + pr) * PAIR:(kvh * NPAIR + pr + 1) * PAIR]
        if scale is not None:
            t = t * scale
        zero = jnp.zeros_like(t)
        parts += [jnp.where(lo, t, zero), jnp.where(lo, zero, t)]
    return jnp.concatenate(parts, axis=0)


def _kv_tiles(kvc_ref, kvp_ref, tile, kvh):
    lo = _lo_lanes()
    dup, left, right = [], [], []
    for ref in (kvp_ref, kvc_ref):
        t = ref[:, tile * PAIR:(tile + 1) * PAIR]
        r = pltpu.roll(t.astype(F32), HD, 1).astype(BF16)
        zero = jnp.zeros_like(t)
        a, b = (t, r) if kvh == 0 else (r, t)
        dup.append(jnp.where(lo, a, b))
        left.append(jnp.where(lo, a, zero))
        right.append(jnp.where(lo, zero, b))
    cat = lambda xs: jnp.concatenate(xs, axis=0)
    return cat(dup), cat(left), cat(right)


def _band_consts(first):
    row = lax.broadcasted_iota(jnp.int32, (BLK, BLK), 0)
    col = lax.broadcasted_iota(jnp.int32, (BLK, BLK), 1)
    upper = col > row
    dist = jnp.where(upper, row - col + BLK, row - col).astype(F32)
    pen = jnp.where(jnp.logical_and(upper, first), -jnp.inf, 0.0)
    return upper, dist, pen


def _split_band(upper, t):
    zero = jnp.zeros_like(t)
    return jnp.concatenate([jnp.where(upper, t, zero), jnp.where(upper, zero, t)], axis=1)


def attn_fwd(q, kv, sinks):
    T = q.shape[0]
    nb = T // BLK

    def body(sink_ref, q_ref, kvc_ref, kvp_ref, att_ref, lse_ref):
        upper, dist, pen = _band_consts(pl.program_id(0) == 0)
        scores, values = [], []
        for kvh in range(NKV):
            kdup, _, _ = _kv_tiles(kvc_ref, kvp_ref, 0, kvh)
            values.append(_kv_tiles(kvc_ref, kvp_ref, 1, kvh)[1:])
            scores.append(_dot_nt(_stack_heads(q_ref, kvh, SCALE), kdup))
        for kvh in range(NKV):
            s_all = scores[kvh]
            vleft, vright = values[kvh]
            for pr in range(NPAIR):
                outs, inv = [], []
                for side, vpad in ((0, vleft), (1, vright)):
                    g = 2 * pr + side
                    hq = kvh * GQ + g
                    sink = sink_ref[0, hq]
                    rows = slice(g * BLK, (g + 1) * BLK)
                    s = jnp.where(upper, s_all[rows, 0:BLK], s_all[rows, BLK:2 * BLK]) - SLOPES[hq] * dist + pen
                    m = jnp.maximum(jnp.max(s, axis=-1, keepdims=True), sink)
                    p = jnp.exp(s - m)
                    l = jnp.sum(p, axis=-1, keepdims=True) + jnp.exp(sink - m)
                    lse_ref[:, hq:hq + 1] = m + jnp.log(l)
                    outs.append(_dot(_split_band(upper, p.astype(BF16)), vpad))
                    inv.append(1.0 / l)
                col0 = (kvh * NPAIR + pr) * PAIR
                att_ref[:, col0:col0 + PAIR] = ((outs[0] + outs[1]) * jnp.where(_lo_lanes(), inv[0], inv[1])).astype(BF16)

    return pl.pallas_call(
        body, name="attn_fwd", grid=(nb,),
        in_specs=[pl.BlockSpec(memory_space=pltpu.SMEM),
                  pl.BlockSpec((BLK, AW), lambda i: (i, 0)),
                  pl.BlockSpec((BLK, 2 * KVW), lambda i: (i, 0)),
                  pl.BlockSpec((BLK, 2 * KVW), lambda i: (jnp.maximum(i - 1, 0), 0))],
        out_specs=[pl.BlockSpec((BLK, AW), lambda i: (i, 0)), pl.BlockSpec((BLK, NQ), lambda i: (i, 0))],
        out_shape=[SDS((T, AW), BF16), SDS((T, NQ), F32)],
        compiler_params=_cp(("arbitrary",)),
    )(sinks, q, kv, kv)


def attn_bwd(q, kv, datt, att, lse, sinks):
    T = q.shape[0]
    nb = T // BLK

    def body(sink_ref, q_ref, kvc_ref, kvp_ref, do_ref, out_ref, lse_ref, dq_ref, dkv_ref, dsink_ref, carry_ref):
        i = pl.program_id(0)

        @pl.when(i == 0)
        def _():
            dsink_ref[...] = jnp.zeros_like(dsink_ref)
            carry_ref[...] = jnp.zeros_like(carry_ref)

        @pl.when(i < nb)
        def _():
            upper, dist, pen = _band_consts(i == 0)
            lo = _lo_lanes()
            dk_dup, dv_dup = [], []
            staged = []
            for kvh in range(NKV):
                kdup, kleft, kright = _kv_tiles(kvc_ref, kvp_ref, 0, kvh)
                vdup, _, _ = _kv_tiles(kvc_ref, kvp_ref, 1, kvh)
                qs = _stack_heads(q_ref, kvh, SCALE)
                dos = _stack_heads(do_ref, kvh)
                staged.append((kleft, kright, qs, dos, _dot_nt(qs, kdup), _dot_nt(dos, vdup)))
            deltas = []
            for pair in range(NQ // 2):
                cols = slice(pair * PAIR, (pair + 1) * PAIR)
                t = do_ref[:, cols].astype(F32) * out_ref[:, cols].astype(F32)
                deltas += [jnp.sum(jnp.where(lo, t, 0.0), axis=-1, keepdims=True),
                           jnp.sum(jnp.where(lo, 0.0, t), axis=-1, keepdims=True)]
            for kvh in range(NKV):
                kleft, kright, qs, dos, s_all, dp_all = staged[kvh]
                ds_parts, p_parts = [], []
                for pr in range(NPAIR):
                    dq = None
                    for side, kpad in ((0, kleft), (1, kright)):
                        g = 2 * pr + side
                        hq = kvh * GQ + g
                        lse_h = lse_ref[:, hq:hq + 1]
                        rows = slice(g * BLK, (g + 1) * BLK)
                        s = jnp.where(upper, s_all[rows, 0:BLK], s_all[rows, BLK:2 * BLK]) - SLOPES[hq] * dist + pen
                        p = jnp.exp(s - lse_h)
                        dp = jnp.where(upper, dp_all[rows, 0:BLK], dp_all[rows, BLK:2 * BLK])
                        delta = deltas[hq]
                        dsink_ref[:, hq:hq + 1] += -jnp.sum(jnp.exp(sink_ref[0, hq] - lse_h) * delta, axis=0,
                                                            keepdims=True)
                        ds = _split_band(upper, (p * (dp - delta)).astype(BF16))
                        ds_parts.append(ds)
                        p_parts.append(_split_band(upper, p.astype(BF16)))
                        d = _dot(ds, kpad)
                        dq = d if dq is None else dq + d
                    col0 = (kvh * NPAIR + pr) * PAIR
                    dq_ref[:, col0:col0 + PAIR] = (dq * SCALE).astype(BF16)
                dkw = _dot_tn(qs, jnp.concatenate(ds_parts, axis=0)).T
                dvw = _dot_tn(dos, jnp.concatenate(p_parts, axis=0)).T
                dk_dup.append(dkw + pltpu.roll(dkw, HD, 1))
                dv_dup.append(dvw + pltpu.roll(dvw, HD, 1))
            dk = jnp.where(jnp.concatenate([lo, lo], axis=0), dk_dup[0], dk_dup[1])
            dv = jnp.where(jnp.concatenate([lo, lo], axis=0), dv_dup[0], dv_dup[1])
            dkv_ref[:, 0:PAIR] = (carry_ref[:, 0:PAIR] + dk[0:BLK]).astype(BF16)
            dkv_ref[:, PAIR:2 * PAIR] = (carry_ref[:, PAIR:2 * PAIR] + dv[0:BLK]).astype(BF16)
            carry_ref[:, 0:PAIR] = dk[BLK:2 * BLK]
            carry_ref[:, PAIR:2 * PAIR] = dv[BLK:2 * BLK]

        @pl.when(i == nb)
        def _():
            dkv_ref[...] = carry_ref[...].astype(BF16)

    cur = lambda i: (jnp.minimum(i, nb - 1), 0)
    prev = lambda i: (jnp.maximum(jnp.minimum(i, nb - 1) - 1, 0), 0)
    return pl.pallas_call(
        body, name="attn_bwd", grid=(nb + 1,),
        in_specs=[pl.BlockSpec(memory_space=pltpu.SMEM),
                  pl.BlockSpec((BLK, AW), cur), pl.BlockSpec((BLK, 2 * KVW), cur), pl.BlockSpec((BLK, 2 * KVW), prev),
                  pl.BlockSpec((BLK, AW), cur), pl.BlockSpec((BLK, AW), cur), pl.BlockSpec((BLK, NQ), cur)],
        out_specs=[pl.BlockSpec((BLK, AW), cur),
                   pl.BlockSpec((BLK, 2 * KVW), lambda i: (jnp.maximum(i - 1, 0), 0)),
                   pl.BlockSpec((1, NQ), lambda i: (0, 0))],
        out_shape=[SDS((T, AW), BF16), SDS((T, 2 * KVW), BF16), SDS((1, NQ), F32)],
        scratch_shapes=[pltpu.VMEM((BLK, 2 * KVW), F32)],
        compiler_params=_cp(("arbitrary",)),
    )(sinks, q, kv, kv, datt, att, lse)


def _inv_counts(t0, rows):
    t = (t0 + lax.broadcasted_iota(jnp.int32, (rows, 1), 0) + 1).astype(F32)
    return [1.0 / jnp.minimum(t, float(w)) for w in POOL_WINDOWS]


def pool_fwd(z, wmix, scale):
    T = z.shape[0]
    TM = min(1024, T)
    L = TM + HALO

    def body(z_ref, halo_ref, wmix_ref, scale_ref, pooled_ref, mixs_ref):
        i = pl.program_id(0)
        halo = jnp.where(i > 0, halo_ref[...], 0.0)
        zt = z_ref[...]
        e = jnp.concatenate([halo, zt], axis=0)
        sums = []
        s = e
        for k in (1, 2, 4, 8):
            s = s + pltpu.roll(s, k, 0)
            sums.append(s)
        inv = _inv_counts(i * TM, TM)
        for gi in range(len(POOL_WINDOWS)):
            cols = slice(gi * PG, (gi + 1) * PG)
            pooled = (sums[gi][HALO:, cols] * inv[gi] - zt[:, cols]).astype(BF16)
            pooled_ref[:, cols] = pooled
            mixs_ref[:, cols] = (_dot(pooled, wmix_ref[gi]) * scale_ref[:, cols]).astype(BF16)

    return pl.pallas_call(
        body, name="pool_fwd", grid=(T // TM,),
        in_specs=[pl.BlockSpec((TM, PW), lambda i: (i, 0)),
                  pl.BlockSpec((HALO, PW), lambda i: (jnp.maximum(i * (TM // HALO) - 1, 0), 0)),
                  _const_spec((len(POOL_WINDOWS), PG, PG)), _const_spec((1, PW))],
        out_specs=[pl.BlockSpec((TM, PW), lambda i: (i, 0)), pl.BlockSpec((TM, PW), lambda i: (i, 0))],
        out_shape=[SDS((T, PW), BF16), SDS((T, PW), BF16)],
        compiler_params=_cp(("arbitrary",)),
    )(z, z, wmix, scale)


def pool_bwd(dmixs, pooled, wmix, scale):
    T = dmixs.shape[0]
    TM = min(1024, T)
    L = TM + HALO
    nt = T // TM

    def body(dm_ref, halo_ref, pooled_ref, wmix_ref, scale_ref, dz_ref, dwmix_ref, dscale_ref):
        i = pl.program_id(0)

        @pl.when(i == 0)
        def _():
            dwmix_ref[...] = jnp.zeros_like(dwmix_ref)
            dscale_ref[...] = jnp.zeros_like(dscale_ref)

        halo = jnp.where(i < nt - 1, halo_ref[...], 0.0)
        dm = dm_ref[...]
        e = jnp.concatenate([dm, halo], axis=0)
        inv = _inv_counts(i * TM, L)
        for gi in range(len(POOL_WINDOWS)):
            cols = slice(gi * PG, (gi + 1) * PG)
            w = wmix_ref[gi]
            dmixed = (e[:, cols] * scale_ref[:, cols]).astype(BF16)
            dpooled = _dot_nt(dmixed, w)
            pooled = pooled_ref[:, cols]
            mixed = _dot(pooled, w)
            dscale_ref[:, cols] += jnp.sum(dm[:, cols] * mixed, axis=0, keepdims=True)
            dwmix_ref[gi] += _dot_tn(pooled, dmixed[:TM])
            s = dpooled * inv[gi]
            k = 1
            while k < POOL_WINDOWS[gi]:
                s = s + pltpu.roll(s, L - k, 0)
                k *= 2
            dz_ref[:, cols] = (s[:TM] - dpooled[:TM]).astype(BF16)

    return pl.pallas_call(
        body, name="pool_bwd", grid=(nt,),
        in_specs=[pl.BlockSpec((TM, PW), lambda i: (i, 0)),
                  pl.BlockSpec((HALO, PW), lambda i: (jnp.minimum((i + 1) * (TM // HALO), T // HALO - 1), 0)),
                  pl.BlockSpec((TM, PW), lambda i: (i, 0)),
                  _const_spec((len(POOL_WINDOWS), PG, PG)), _const_spec((1, PW))],
        out_specs=[pl.BlockSpec((TM, PW), lambda i: (i, 0)),
                   pl.BlockSpec((len(POOL_WINDOWS), PG, PG), lambda i: (0, 0, 0)),
                   pl.BlockSpec((1, PW), lambda i: (0, 0))],
        out_shape=[SDS((T, PW), BF16), SDS((len(POOL_WINDOWS), PG, PG), F32), SDS((1, PW), F32)],
        compiler_params=_cp(("arbitrary",)),
    )(dmixs, dmixs, pooled, wmix, scale)


def merge_fwd(att, mixs, gt, h1, wattn, wpool, wout):
    T = h1.shape[0]
    TM = 512

    def body(att_ref, mixs_ref, gt_ref, h_ref, wa_ref, wp_ref, wo_ref, h2_ref, mg_ref, gf_ref):
        a = _dot(att_ref[...], wa_ref[...])
        p = _dot(mixs_ref[...], wp_ref[...])
        sa = _sigmoid(gt_ref[:, 0:D].astype(F32))
        sp = _sigmoid(gt_ref[:, D:2 * D].astype(F32))
        gf_ref[:, 0:D] = (a * sa * (1.0 - sa)).astype(BF16)
        gf_ref[:, D:2 * D] = (p * sp * (1.0 - sp)).astype(BF16)
        mg = (sa * a + sp * p).astype(BF16)
        mg_ref[...] = mg
        h2_ref[...] = h_ref[...] + _dot(mg, wo_ref[...])

    tile = lambda w: pl.BlockSpec((TM, w), lambda i: (i, 0))
    return pl.pallas_call(
        body, name="merge_fwd", grid=(T // TM,),
        in_specs=[tile(AW), tile(PW), tile(2 * D), tile(D), _const_spec((AW, D)), _const_spec((PW, D)),
                  _const_spec((D, D))],
        out_specs=[tile(D), tile(D), tile(2 * D)],
        out_shape=[SDS((T, D), F32), SDS((T, D), BF16), SDS((T, 2 * D), BF16)],
        compiler_params=_cp(("arbitrary",)),
    )(att, mixs, gt, h1, wattn, wpool, wout)


def merge_bwd(dh2, gt, gf, wattn, wpool, wout, after=None):
    T = dh2.shape[0]
    TM = 512
    token_spec, token_arg = _token_operand(after)

    def body(dh2_ref, gt_ref, gf_ref, wa_ref, wp_ref, wo_ref, *rest):
        datt_ref, dmixs_ref, dgt_ref, da_ref, dp_ref = rest[-5:]
        dm = _dot_nt(dh2_ref[...].astype(BF16), wo_ref[...])
        da = (dm * _sigmoid(gt_ref[:, 0:D].astype(F32))).astype(BF16)
        dp = (dm * _sigmoid(gt_ref[:, D:2 * D].astype(F32))).astype(BF16)
        da_ref[...] = da
        dp_ref[...] = dp
        dgt_ref[:, 0:D] = (dm * gf_ref[:, 0:D].astype(F32)).astype(BF16)
        dgt_ref[:, D:2 * D] = (dm * gf_ref[:, D:2 * D].astype(F32)).astype(BF16)
        datt_ref[...] = _dot_nt(da, wa_ref[...]).astype(BF16)
        dmixs_ref[...] = _dot_nt(dp, wp_ref[...])

    tile = lambda w: pl.BlockSpec((TM, w), lambda i: (i, 0))
    return pl.pallas_call(
        body, name="merge_bwd", grid=(T // TM,),
        in_specs=[tile(D), tile(2 * D), tile(2 * D), _const_spec((AW, D)), _const_spec((PW, D)),
                  _const_spec((D, D))] + token_spec,
        out_specs=[tile(AW), tile(PW), tile(2 * D), tile(D), tile(D)],
        out_shape=[SDS((T, AW), BF16), SDS((T, PW), F32), SDS((T, 2 * D), BF16), SDS((T, D), BF16),
                   SDS((T, D), BF16)],
        compiler_params=_cp(("arbitrary",)),
    )(dh2, gt, gf, wattn, wpool, wout, *token_arg)


def adamw(updates, place, name, after=None):
    tile_bytes = 2 * 1024 * 1024 // len(updates)
    jobs = []
    for w, g, m, v in updates:
        R, C = w.shape
        tr = R
        if R * C * 4 > tile_bytes:
            tr = next(cand for cand in (512, 256, 128, 64, 32, 16, 8) if R % cand == 0 and cand * C * 4 <= tile_bytes)
        blk = ((tr, C), lambda ls, p: (ls, 0))
        jobs.append((R // tr, [(a, *blk) for a in (w, g, m, v)], [((R, C), F32, *blk)] * 4,
                     lambda ins, outs: _adamw_tile(*ins, *outs)))
    return _run_jobs(jobs, place, name, after)


GROUP_UP1, GROUP_MIX, GROUP_FFN2 = 0, 1, 2


def _behind(small, token):
    return small if token is None else small + token[0:1, 0:1]


def local_fwd_bwd(x, target, S, comm):
    w_up1 = comm.weights(GROUP_UP1, None)["ffn1_w_up"]
    ab1, hid1 = ffn_up(x, _behind(S["ffn1_norm"], comm.started()), w_up1, "ffn1_up")
    W = comm.weights(GROUP_MIX, hid1)
    h1 = ffn_down(x, hid1, W["ffn1_w_down"], "ffn1_down", after=comm.started())
    u, q, kv, z, gt = mix_in_fwd(h1, S["mix_norm"], W["w_in"])
    att, lse = attn_fwd(q, kv, S["sinks"])
    pooled, mixs = pool_fwd(z, S["pool_w_mix"], _behind(S["pool_scale"], comm.prefetch(GROUP_FFN2, att)))
    h2, merged, gf = merge_fwd(att, mixs, gt, h1, W["w_attn_up"], W["w_pool_up"], W["w_out"])
    W = comm.weights(GROUP_FFN2, h2)
    dh3, ab2, hid2, loss, g_final = ffn_fwd(h2, S["ffn2_norm"], W["ffn2_w_up"], W["ffn2_w_down"], "ffn2_fwd",
                                            head=(target, S["final_norm"]))

    G = {"final_norm": g_final}
    dh2, dup2, n2, G["ffn2_norm"] = ffn_bwd_x(dh3, h2, ab2, S["ffn2_norm"], W["ffn2_w_up"], W["ffn2_w_down"],
                                              "ffn2_bwd_x")
    ffn2 = ("ffn2_w_down", "ffn2_w_up")
    token = comm.grads({"ffn2_w_down": matmul_tn(hid2, dh3, "ffn2_dw_down", tm=1408, tn=512, tt=4096, b_scale=0.5),
                        "ffn2_w_up": matmul_tn(n2, dup2, "ffn2_dw_up", tm=D, tn=512, tt=4096)})

    W = comm.weights(GROUP_MIX, None)
    datt, dmixs, dgt, da, dp = merge_bwd(dh2, gt, gf, W["w_attn_up"], W["w_pool_up"], W["w_out"], after=token)
    token = comm.advance(ffn2, datt)
    g_mix = {"w_out": matmul_tn(merged, dh2, "dw_out", tm=D, tn=D, after=token),
             "w_attn_up": matmul_tn(att, da, "dw_attn_up", tm=AW, tn=D),
             "w_pool_up": matmul_tn(mixs, dp, "dw_pool_up", tm=PW, tn=D)}
    dz, G["pool_w_mix"], G["pool_scale"] = pool_bwd(dmixs, pooled, S["pool_w_mix"], _behind(S["pool_scale"], token))
    dq, dkv, G["sinks"] = attn_bwd(q, kv, datt, att, lse, _behind(S["sinks"], token))
    dh1, G["mix_norm"] = mix_in_bwd(dq, dkv, dz, dgt, dh2, h1, S["mix_norm"], W["w_in"])
    g_mix["w_in"] = jnp.concatenate([
        matmul_tn(dq, u, "dw_in_q", tm=AW, tn=D),
        matmul_tn(dkv, u, "dw_in_kv", tm=2 * KVW, tn=D),
        matmul_tn(dz, u, "dw_in_z", tm=PW, tn=D),
        matmul_tn(dgt, u, "dw_in_g", tm=D, tn=512, tt=4096),
    ], axis=0)
    mix = tuple(g_mix)
    token = comm.grads(g_mix)

    g_dn1 = matmul_tn(hid1, dh1, "ffn1_dw_down", tm=1408, tn=512, tt=4096, b_scale=0.5, after=token)
    token = comm.advance(mix, g_dn1)
    token2 = comm.grads({"ffn1_w_down": g_dn1})
    g1 = _behind(_behind(S["ffn1_norm"], token), token2)
    dx, dup1, n1, G["ffn1_norm"], *outs = ffn_bwd_x(dh1, x, ab1, g1, w_up1, W["ffn1_w_down"], "ffn1_bwd_x",
                                                   updates=comm.reduce(ffn2, token2))
    comm.updated(ffn2, outs)
    token = comm.advance(("ffn1_w_down",), dx)
    token2 = comm.small_start(G, loss, dx)
    g_up1 = matmul_tn(n1, dup1, "ffn1_dw_up", tm=D, tn=512, tt=4096,
                      after=None if token is None else token + token2)
    token = comm.grads({"ffn1_w_up": g_up1})
    updates = comm.reduce(("ffn1_w_down",), token)
    token = comm.advance(("ffn1_w_up",), [u[1] for u in updates])
    done = comm.update(("ffn1_w_down",), updates, token)
    done = comm.finish(mix, done)
    done = comm.small_finish(done)
    comm.finish(("ffn1_w_up",), done)
    return loss, dx


HBM_SPEC = pl.BlockSpec(memory_space=pltpu.HBM)


def _me():
    return lax.axis_index("x"), lax.axis_index("y"), lax.axis_index("c")


def _peer_chip(x, y, k):
    return x ^ (k >> 1), y ^ (k & 1)


def _piece_half(ref, rowlike, j, h):
    if rowlike:
        return ref.at[j, h]
    ns = ref.shape[-1] // N_CHIPS
    return ref.at[h, :, pl.ds(pl.multiple_of(j * ns, 128), ns)]


def _piece(ref, rowlike, j):
    if rowlike:
        return ref.at[j]
    ns = ref.shape[-1] // N_CHIPS
    return ref.at[:, :, pl.ds(pl.multiple_of(j * ns, 128), ns)]


def _full_shape(shard_view, rowlike):
    _, kh, ns = shard_view.shape
    return (N_CHIPS, 2, kh, ns) if rowlike else (2, kh, N_CHIPS * ns)


def _remote(src, dst, send_sem, recv_sem, dev):
    return pltpu.make_async_remote_copy(src, dst, send_sem, recv_sem, device_id=dev, device_id_type=MESH)


SEM_SPEC = pl.BlockSpec(memory_space=pltpu.SEMAPHORE)
SPLIT_PARAMS = pltpu.CompilerParams(has_side_effects=pltpu.SideEffectType.DATAFLOW_SIDE_EFFECTING)


def _gather_plan(rowlikes):
    def plan(s_refs, f_refs, a, k, x, y, c):
        px, py = _peer_chip(x, y, k)
        return (s_refs[a].at[c], _piece_half(f_refs[a], rowlikes[a], 2 * x + y, c),
                _piece_half(f_refs[a], rowlikes[a], 2 * px + py, c), (px, py, c))
    return plan


def _all_to_all_plan(rowlikes):
    def plan(q_refs, r_refs, a, k, x, y, c):
        px, py = _peer_chip(x, y, k)
        if rowlikes[a]:
            src = q_refs[a].at[2 * px + py]
        else:
            ns = q_refs[a].shape[-1] // N_CHIPS
            src = q_refs[a].at[:, pl.ds(pl.multiple_of((2 * px + py) * ns, 128), ns)]
        return src, r_refs[a].at[k - 1], r_refs[a].at[k - 1], (px, py, c)
    return plan


def _swap_plan(rowlikes):
    def plan(g_refs, got_refs, a, k, x, y, c):
        src = g_refs[a].at[:, 1 - c] if rowlikes[a] else g_refs[a].at[1 - c]
        return src, got_refs[a], got_refs[a], (x, y, 1 - c)
    return plan


def _everyone_plan(s_refs, slot_refs, a, k, x, y, c):
    px, py, pc = x ^ (k >> 2), y ^ ((k >> 1) & 1), c ^ (k & 1)
    return s_refs[a], slot_refs[a].at[4 * x + 2 * y + c], slot_refs[a].at[4 * px + 2 * py + pc], (px, py, pc)


def _forward_plan(rowlikes):
    def plan(s_refs, f_refs, a, k, x, y, c):
        sib = (x, y, 1 - c)
        if k == 0:
            own = _piece(f_refs[a], rowlikes[a], 2 * x + y)
            return s_refs[a], own, own, sib
        px, py = _peer_chip(x, y, k)
        mine = _piece_half(f_refs[a], rowlikes[a], 2 * px + py, c)
        return mine, mine, _piece_half(f_refs[a], rowlikes[a], 2 * px + py, 1 - c), sib
    return plan


CHIPS, SIBLING, EVERYONE, FORWARDS = (1, 2, 3), (1,), tuple(range(1, 8)), (0, 1, 2, 3)


def _as_list(after):
    return [] if after is None else list(after) if isinstance(after, (list, tuple)) else [after]


def exchange_start(srcs, land_shapes, plan, after, name, peers=CHIPS):
    n = len(srcs)
    lands = [l if hasattr(l, "dtype") else lax.empty(l, s.dtype) for l, s in zip(land_shapes, srcs)]

    behind = _as_list(after)

    def body(*refs):
        s_refs, l_refs = refs[:n], refs[n:2 * n]
        send_sems, recv_sems = refs[2 * n + len(behind)], refs[2 * n + len(behind) + 1]
        token = refs[-1]
        x, y, c = _me()
        for a in range(n):
            for i, k in enumerate(peers):
                src, dst, _, peer = plan(s_refs, l_refs, a, k, x, y, c)
                sem = len(peers) * a + i
                _remote(src, dst, send_sems.at[sem], recv_sems.at[sem], peer).start()
        token[...] = jnp.zeros_like(token)

    n_sems = len(peers) * n
    outs = pl.pallas_call(
        body, name=name, in_specs=[HBM_SPEC] * (2 * n) + [pl.BlockSpec(memory_space=pl.ANY)] * len(behind),
        out_specs=[SEM_SPEC, SEM_SPEC] + [HBM_SPEC] * (2 * n) + [pl.BlockSpec(memory_space=pltpu.VMEM)],
        out_shape=[pltpu.SemaphoreType.DMA((n_sems,)), pltpu.SemaphoreType.DMA((n_sems,))]
        + [pltpu.HBM(a.shape, a.dtype) for a in (*srcs, *lands)] + [SDS((8, 128), F32)],
        input_output_aliases={i: 2 + i for i in range(2 * n)},
        compiler_params=SPLIT_PARAMS,
    )(*[pltpu.with_memory_space_constraint(a, pltpu.HBM) for a in (*srcs, *lands)], *behind)
    return {"sems": outs[:2], "srcs": outs[2:2 + n], "lands": outs[2 + n:2 + 2 * n], "token": outs[-1]}


def exchange_wait(state, plan, after, name, peers=CHIPS):
    n = len(state["srcs"])
    behind = _as_list(after)

    def body(*refs):
        s_refs, l_refs = refs[:n], refs[n:2 * n]
        send_sems, recv_sems = refs[2 * n], refs[2 * n + 1]
        x, y, c = _me()
        for a in range(n):
            for i, k in enumerate(peers):
                src, _, landing, peer = plan(s_refs, l_refs, a, k, x, y, c)
                sem = len(peers) * a + i
                cp = _remote(src, landing, send_sems.at[sem], recv_sems.at[sem], peer)
                cp.wait_send()
                cp.wait_recv()

    bufs = (*state["srcs"], *state["lands"])
    outs = pl.pallas_call(
        body, name=name,
        in_specs=[HBM_SPEC] * (2 * n) + [SEM_SPEC, SEM_SPEC] + [pl.BlockSpec(memory_space=pl.ANY)] * len(behind),
        out_specs=[HBM_SPEC] * (2 * n),
        out_shape=[pltpu.HBM(a.shape, a.dtype) for a in bufs],
        input_output_aliases={i: i for i in range(2 * n)},
        compiler_params=SPLIT_PARAMS,
    )(*bufs, *state["sems"], *behind)
    return outs[:n], outs[n:]


def gather_finish(shards, fulls, rowlikes, name):
    n = len(shards)

    def body(*refs):
        s_refs, f_refs = refs[:n], refs[2 * n:3 * n]
        send_sems, recv_sems = refs[3 * n:]
        x, y, c = _me()
        chip = 2 * x + y
        sib = (x, y, 1 - c)
        sends = []
        for a in range(n):
            own = _piece(f_refs[a], rowlikes[a], chip)
            cp = _remote(s_refs[a], own, send_sems.at[a, 0], recv_sems.at[a, 0], sib)
            cp.start()
            sends.append(cp)
            for k in (1, 2, 3):
                px, py = _peer_chip(x, y, k)
                slot = _piece_half(f_refs[a], rowlikes[a], 2 * px + py, c)
                cp = _remote(slot, slot, send_sems.at[a, k], recv_sems.at[a, k], sib)
                cp.start()
                sends.append(cp)
        for a in range(n):
            own = _piece(f_refs[a], rowlikes[a], chip)
            _remote(own, own, send_sems.at[a, 0], recv_sems.at[a, 0], sib).wait_recv()
            for k in (1, 2, 3):
                px, py = _peer_chip(x, y, k)
                slot = _piece_half(f_refs[a], rowlikes[a], 2 * px + py, 1 - c)
                _remote(slot, slot, send_sems.at[a, k], recv_sems.at[a, k], sib).wait_recv()
        for cp in sends:
            cp.wait_send()

    return pl.pallas_call(
        body, name=name, in_specs=[HBM_SPEC] * (2 * n), out_specs=[HBM_SPEC] * n,
        out_shape=[SDS(f.shape, f.dtype) for f in fulls],
        input_output_aliases={n + a: a for a in range(n)},
        scratch_shapes=[pltpu.SemaphoreType.DMA((n, 4)), pltpu.SemaphoreType.DMA((n, 4))],
    )(*shards, *fulls)


def _half_buffer_shape(gview, rowlike):
    return (N_CHIPS,) + gview.shape[2:] if rowlike else gview.shape[1:]


def _row_tiles(kh, ns):
    return 1 if kh * ns <= 256 * 1024 else 2


def _run_jobs(jobs, place, name, after=None):
    n_steps = max(job[0] for job in jobs)

    def spec(block, index, steps):
        return pl.BlockSpec(block, lambda s, p: index(jnp.minimum(s, steps - 1), p))

    in_specs = [spec(b, ix, steps) for steps, ins, _, _ in jobs for _, b, ix in ins]
    out_specs = [spec(b, ix, steps) for steps, _, outs, _ in jobs for _, _, b, ix in outs]
    token_spec, token_arg = _token_operand(after)
    n_in = len(in_specs) + len(token_spec)

    def body(place_ref, *refs):
        s = pl.program_id(0)
        i, o = 0, n_in
        for steps, ins, outs, fn in jobs:
            in_refs, out_refs = refs[i:i + len(ins)], refs[o:o + len(outs)]
            i, o = i + len(ins), o + len(outs)

            @pl.when(s < steps)
            def _(fn=fn, in_refs=in_refs, out_refs=out_refs):
                fn(in_refs, out_refs)

    return pl.pallas_call(
        body, name=name,
        grid_spec=pltpu.PrefetchScalarGridSpec(num_scalar_prefetch=1, grid=(n_steps,),
                                               in_specs=in_specs + token_spec, out_specs=out_specs),
        out_shape=[SDS(shape, dtype) for _, _, outs, _ in jobs for shape, dtype, _, _ in outs],
        compiler_params=_cp(("arbitrary",)),
    )(place, *[a for _, ins, _, _ in jobs for a, _, _ in ins], *token_arg)


def add_halves(gviews, gots, rowlikes, place, name):
    def add(ins, outs):
        outs[0][...] = (ins[0][...].astype(F32) + ins[1][...].astype(F32)).astype(BF16)

    jobs = []
    for g, got, rowlike in zip(gviews, gots, rowlikes):
        kh, ns = (g.shape[2], g.shape[3]) if rowlike else (g.shape[1], g.shape[2] // N_CHIPS)
        rt = _row_tiles(kh, ns)
        tr = kh // rt
        if rowlike:
            g_in = (g, (None, None, tr, ns), lambda ls, p, rt=rt: (ls // rt, p[0], ls % rt, 0))
            half = ((None, tr, ns), lambda ls, p, rt=rt: (ls // rt, ls % rt, 0))
        else:
            g_in = (g, (None, tr, ns), lambda ls, p, rt=rt: (p[0], ls % rt, ls // rt))
            half = ((tr, ns), lambda ls, p, rt=rt: (ls % rt, ls // rt))
        jobs.append((N_CHIPS * rt, [g_in, (got, *half)], [(got.shape, BF16, *half)], add))
    return _run_jobs(jobs, place, name)


def _slot_shape(q, rowlike):
    return (3,) + (q.shape[1:] if rowlike else (q.shape[0], q.shape[1] // N_CHIPS))


def sum_pieces(qs, recvs, rowlikes, place, name):
    def total(ins, outs):
        acc = ins[0][...].astype(F32)
        for k in range(3):
            acc = acc + ins[1][k].astype(F32)
        outs[0][...] = acc

    jobs = []
    for q, recv, rowlike in zip(qs, recvs, rowlikes):
        kh, ns = recv.shape[1], recv.shape[2]
        rt = _row_tiles(kh, ns)
        tr = kh // rt
        mine = ((None, tr, ns), lambda ls, p: (p[1], ls, 0)) if rowlike else ((tr, ns), lambda ls, p: (ls, p[1]))
        jobs.append((rt, [(q, *mine), (recv, (3, tr, ns), lambda ls, p: (0, ls, 0))],
                     [((2, kh, ns), F32, (None, tr, ns), lambda ls, p: (p[0], ls, 0))], total))
    return _run_jobs(jobs, place, name)


def join_halves(halves, name):
    n = len(halves)

    def body(*refs):
        o_refs = refs[n:2 * n]
        send_sems, recv_sems = refs[2 * n:]
        x, y, c = _me()
        sib = (x, y, 1 - c)
        sends = []
        for a in range(n):
            cp = _remote(o_refs[a].at[c], o_refs[a].at[c], send_sems.at[a], recv_sems.at[a], sib)
            cp.start()
            sends.append(cp)
        for a in range(n):
            got = o_refs[a].at[1 - c]
            _remote(got, got, send_sems.at[a], recv_sems.at[a], sib).wait_recv()
        for cp in sends:
            cp.wait_send()

    return pl.pallas_call(
        body, name=name, in_specs=[HBM_SPEC] * n, out_specs=[HBM_SPEC] * n,
        out_shape=[SDS(h.shape, h.dtype) for h in halves],
        input_output_aliases={a: a for a in range(n)},
        scratch_shapes=[pltpu.SemaphoreType.DMA((n,)), pltpu.SemaphoreType.DMA((n,))],
    )(*halves)


N_DEV = 8


def sum_devices(s, slots, me):
    R, Wd = s.shape

    def body(me_ref, s_ref, slots_ref, out_ref):
        acc = None
        for d in range(N_DEV):
            mine = me_ref[0] == d
            term = jnp.where(mine, s_ref[...], slots_ref[jnp.where(mine, d ^ 1, d)])
            acc = term if acc is None else acc + term
        out_ref[...] = acc

    vmem = pl.BlockSpec(memory_space=pltpu.VMEM)
    return pl.pallas_call(
        body, name="sum_devices", in_specs=[pl.BlockSpec(memory_space=pltpu.SMEM), vmem, vmem], out_specs=vmem,
        out_shape=SDS((R, Wd), F32),
    )(me, s, slots)


TRANSPOSED = ("w_in",)
BIG = {"ffn1_w_up": (D, 2 * FF, "col"), "ffn1_w_down": (FF, D, "row"), "w_in": (INW, D, "row"),
       "w_attn_up": (AW, D, "row"), "w_pool_up": (PW, D, "col"), "w_out": (D, D, "row"),
       "ffn2_w_up": (D, 2 * FF, "col"), "ffn2_w_down": (FF, D, "row")}
GROUPS = (("ffn1_w_up",), ("ffn1_w_down", "w_in", "w_attn_up", "w_pool_up", "w_out"), ("ffn2_w_up", "ffn2_w_down"))
SMALL = ("ffn1_norm", "mix_norm", "ffn2_norm", "final_norm", "pool_scale", "sinks", "pool_w_mix")
SMALL_W = 128


def _rowlike(name):
    return BIG[name][2] == "row"


def _half_dims(name):
    k, n, kind = BIG[name]
    return (k // N_CHIPS // 2, n) if kind == "row" else (k // 2, n // N_CHIPS)


def shard_view(name, shard):
    return shard.reshape((2,) + _half_dims(name))


def full_from_view(name, fv):
    k, n, _ = BIG[name]
    return fv.reshape(k, n)


def grad_view(name, g):
    kh, ns = _half_dims(name)
    return g.reshape(_full_shape(jax.ShapeDtypeStruct((2, kh, ns), g.dtype), _rowlike(name)))


def pack_small(d):
    parts = []
    for name in SMALL:
        a = d[name].reshape(-1)
        pad = (-a.shape[0]) % SMALL_W
        parts.append(jnp.pad(a, (0, pad)).reshape(-1, SMALL_W))
    a = jnp.concatenate(parts, axis=0)
    return jnp.pad(a, ((0, (-a.shape[0]) % 8), (0, 0)))


def unpack_small(a, like):
    out, r0 = {}, 0
    for name in SMALL:
        size = int(np.prod(like[name].shape))
        rows = -(-size // SMALL_W)
        out[name] = a[r0:r0 + rows].reshape(-1)[:size].reshape(like[name].shape)
        r0 += rows
    return out


WEIGHTS = ("ffn1_norm", "ffn1_w_up", "ffn1_w_down", "mix_norm", "w_in", "sinks", "w_attn_up", "pool_w_mix",
           "pool_scale", "w_pool_up", "w_out", "ffn2_norm", "ffn2_w_up", "ffn2_w_down", "final_norm")


def kernel(x, ffn1_norm, ffn1_w_up, ffn1_w_down, mix_norm, w_in, sinks, w_attn_up, pool_w_mix, pool_scale, w_pool_up, w_out, ffn2_norm, ffn2_w_up, ffn2_w_down, final_norm, loss_target, m_ffn1_norm, m_ffn1_w_up, m_ffn1_w_down, m_mix_norm, m_w_in, m_sinks, m_w_attn_up, m_pool_w_mix, m_pool_scale, m_w_pool_up, m_w_out, m_ffn2_norm, m_ffn2_w_up, m_ffn2_w_down, m_final_norm, v_ffn1_norm, v_ffn1_w_up, v_ffn1_w_down, v_mix_norm, v_w_in, v_sinks, v_w_attn_up, v_pool_w_mix, v_pool_scale, v_w_pool_up, v_w_out, v_ffn2_norm, v_ffn2_w_up, v_ffn2_w_down, v_final_norm):
    given = dict(locals())
    w = {n: given[n] for n in WEIGHTS}
    m = {n: given["m_" + n] for n in WEIGHTS}
    v = {n: given["v_" + n] for n in WEIGHTS}
    cx, cy, cc = _me()
    place = jnp.stack([cc, 2 * cx + cy]).astype(jnp.int32)

    def local2d(d, n):
        return d[n][0].T if n in TRANSPOSED else d[n][0]

    shards = {n: local2d(w, n) for n in BIG}
    grads, delta, new_m, new_v = {}, {}, {}, {}
    rowlikes = [[_rowlike(n) for n in names] for names in GROUPS]

    class Exchanges:
        def __init__(self):
            self.gathers, self.forwards, self.fulls, self.reductions, self.small = {}, {}, {}, {}, None

        def _behind_first(self, a):
            return a + self.gathers[0]["token"][0, 0]

        def _start_gather(self, group, after):
            prepare = self._behind_first if group else (lambda a: a)
            sv = [shard_view(n, prepare(shards[n]).astype(BF16)) for n in GROUPS[group]]
            rl = rowlikes[group]
            self.gathers[group] = exchange_start(sv, [_full_shape(s, r) for s, r in zip(sv, rl)], _gather_plan(rl),
                                                 after, f"gather_start_{group}")

        def weights(self, group, after):
            if not self.gathers:
                self._start_gather(0, None)
                self.packed = [self._behind_first(pack_small(d)) for d in (w, m, v)]
                after = [self.gathers[0]["token"], *self.packed]
            if group not in self.fulls:
                names, rl, state = GROUPS[group], rowlikes[group], self.gathers[group]
                if group in self.forwards:
                    _, fulls = exchange_wait(self.forwards.pop(group), _forward_plan(rl), after,
                                             f"forward_wait_{group}", FORWARDS)
                else:
                    sv, fulls = exchange_wait(state, _gather_plan(rl), state["token"] if after is None else after,
                                              f"gather_wait_{group}")
                    fulls = gather_finish(sv, fulls, rl, f"gather_finish_{group}")
                self.fulls[group] = {n: full_from_view(n, f) for n, f in zip(names, fulls)}
                if group + 1 < len(GROUPS):
                    self._start_gather(group + 1, fulls[0])
            return self.fulls[group]

        def started(self):
            return self.gathers[max(self.gathers)]["token"]

        def prefetch(self, group, after):
            rl, state = rowlikes[group], self.gathers[group]
            sv, fulls = exchange_wait(state, _gather_plan(rl), after, f"gather_wait_{group}")
            self.forwards[group] = exchange_start(sv, fulls, _forward_plan(rl), None, f"forward_start_{group}", FORWARDS)
            return self.forwards[group]["token"]

        def grads(self, g):
            names = tuple(g)
            rl = [_rowlike(n) for n in names]
            gv = [grad_view(n, g[n]) for n in names]
            state = exchange_start(gv, [_half_buffer_shape(a, r) for a, r in zip(gv, rl)], _swap_plan(rl), None,
                                   "swap_start_" + names[0], SIBLING)
            self.reductions[names] = state
            return state["token"]

        def advance(self, names, after):
            rl = [_rowlike(n) for n in names]
            gv, gots = exchange_wait(self.reductions[names], _swap_plan(rl), after, "swap_wait_" + names[0], SIBLING)
            qs = add_halves(gv, gots, rl, place, "add_halves_" + names[0])
            state = exchange_start(qs, [_slot_shape(q, r) for q, r in zip(qs, rl)], _all_to_all_plan(rl), None,
                                   "all_to_all_start_" + names[0])
            self.reductions[names] = state
            return state["token"]

        def reduce(self, names, after):
            rl = [_rowlike(n) for n in names]
            qs, recvs = exchange_wait(self.reductions.pop(names), _all_to_all_plan(rl), after,
                                      "all_to_all_wait_" + names[0])
            halves = sum_pieces(qs, recvs, rl, place, "sum_pieces_" + names[0])
            return [(shards[n], o.reshape(shards[n].shape), local2d(m, n), local2d(v, n))
                    for n, o in zip(names, join_halves(halves, "join_halves_" + names[0]))]

        def updated(self, names, outs):
            for i, n in enumerate(names):
                grads[n], delta[n], new_m[n], new_v[n] = outs[4 * i:4 * i + 4]
            return [new_v[n] for n in names]

        def update(self, names, updates, token=None):
            return self.updated(names, adamw(updates, place, "adamw_" + names[0], after=token))

        def finish(self, names, after):
            return self.update(names, self.reduce(names, after))

        def small_start(self, G, loss, after):
            packed = pack_small({n: G[n] for n in SMALL})
            used_rows = sum(-(-int(np.prod(small_like[n].shape)) // SMALL_W) for n in SMALL)
            assert packed.shape[0] > used_rows
            packed = packed.at[-1, 0].set(loss[0, 0])
            self.small = exchange_start([packed], [(N_DEV,) + packed.shape], _everyone_plan, after,
                                        "small_start", EVERYONE)
            return self.small["token"]

        def small_finish(self, after):
            (packed,), (slots,) = exchange_wait(self.small, _everyone_plan, after, "small_wait", EVERYONE)
            total = sum_devices(packed, slots, (4 * cx + 2 * cy + cc).astype(jnp.int32).reshape(1))
            self.loss = total[-1, 0]
            g, ds, ms, vs = adamw([(self.packed[0], total, self.packed[1], self.packed[2])], place, "adamw_small")
            for d, packed_d in ((grads, g), (delta, ds), (new_m, ms), (new_v, vs)):
                d.update(unpack_small(packed_d, small_like))
            return vs

    small_like = {n: w[n] for n in SMALL}
    S = {n: w[n].reshape(1, -1) for n in ("ffn1_norm", "mix_norm", "ffn2_norm", "final_norm", "pool_scale", "sinks")}
    S["pool_w_mix"] = w["pool_w_mix"][0].astype(BF16)
    exchanges = Exchanges()
    _, dx = local_fwd_bwd(x[0], loss_target[0], S, exchanges)
    loss = exchanges.loss

    def shaped(d, n):
        return (d[n].T if n in TRANSPOSED else d[n]).reshape(w[n].shape)

    return (loss, dx[None], *[shaped(grads, n) for n in WEIGHTS], *[shaped(delta, n) for n in WEIGHTS],
            *[shaped(new_m, n) for n in WEIGHTS], *[shaped(new_v, n) for n in WEIGHTS])
```

```python
import numpy as np
import jax
import jax.numpy as jnp
from jax import lax
from jax.experimental import pallas as pl
from jax.experimental.pallas import tpu as pltpu

F32 = jnp.float32
BF16 = jnp.bfloat16
SDS = jax.ShapeDtypeStruct
MESH = pl.DeviceIdType.MESH

D = 1024
FF = 2816
NQ = 16
NKV = 2
HD = 64
GQ = NQ // NKV
AW = NQ * HD
KVW = NKV * HD
BLK = 128
PW = 512
PG = 128
POOL_WINDOWS = (2, 4, 8, 16)
HALO = 16
INW = AW + 2 * KVW + PW + 2 * D
C_KV = AW
C_Z = AW + 2 * KVW
C_G = C_Z + PW
EPS = 1e-6
FF_CHUNK = 256
FF_CHUNKS = tuple((c, FF_CHUNK) for c in range(0, FF, FF_CHUNK))
SLOPES = tuple(float(2.0 ** (-8.0 * h / NQ)) for h in range(1, NQ + 1))
SCALE = HD ** -0.5

LR, B1, B2, ADAM_EPS, WD, STEP = 0.001, 0.9, 0.999, 1e-08, 0.01, 10

VMEM_LIMIT = 56 * 1024 * 1024
N_CHIPS = 4

NT = (((1,), (1,)), ((), ()))
TN = (((0,), (0,)), ((), ()))


def _cp(sem=None, vmem=VMEM_LIMIT):
    return pltpu.CompilerParams(dimension_semantics=sem, vmem_limit_bytes=vmem)


def _const_spec(shape):
    nd = len(shape)
    return pl.BlockSpec(shape, lambda *_: (0,) * nd, pipeline_mode=pl.Buffered(1))


def _rstd(x):
    return lax.rsqrt(jnp.mean(x * x, axis=-1, keepdims=True) + EPS)


def _rms_bwd(dn, xhat, rstd, g):
    dxhat = dn * g
    return rstd * (dxhat - xhat * jnp.mean(dxhat * xhat, axis=-1, keepdims=True))


def _sigmoid(x):
    return 0.5 * jnp.tanh(0.5 * x) + 0.5


def _dot(a, b):
    return jnp.dot(a, b, preferred_element_type=F32)


def _dot_nt(a, b):
    return lax.dot_general(a, b, NT, preferred_element_type=F32)


def _dot_tn(a, b):
    return lax.dot_general(a, b, TN, preferred_element_type=F32)


def _swiglu_up(x, g, wup_ref, ab_ref, hid_ref):
    n = (x * _rstd(x) * g).astype(BF16)
    for c0, w in FF_CHUNKS:
        a = _dot(n, wup_ref[:, c0:c0 + w])
        b = _dot(n, wup_ref[:, FF + c0:FF + c0 + w])
        sig = _sigmoid(a)
        s = a * sig
        ab_ref[:, c0:c0 + w] = (b * (sig + s * (1.0 - sig))).astype(BF16)
        ab_ref[:, FF + c0:FF + c0 + w] = s.astype(BF16)
        hid_ref[:, c0:c0 + w] = (s * b).astype(BF16)


def ffn_up(h, g, wup, name, after=None):
    T = h.shape[0]
    TM = 512
    tile = lambda w: pl.BlockSpec((TM, w), lambda i: (i, 0))
    token_spec, token_arg = _token_operand(after)

    def body(h_ref, g_ref, wup_ref, *rest):
        _swiglu_up(h_ref[...], g_ref[...], wup_ref, *rest[-2:])

    return pl.pallas_call(
        body, name=name, grid=(T // TM,),
        in_specs=[tile(D), _const_spec((1, D)), _const_spec((D, 2 * FF))] + token_spec,
        out_specs=[tile(2 * FF), tile(FF)],
        out_shape=[SDS((T, 2 * FF), BF16), SDS((T, FF), BF16)],
        compiler_params=_cp(("arbitrary",)),
    )(h, g, wup, *token_arg)


def ffn_down(h, hid, wdn, name, after=None):
    T = h.shape[0]
    TM = min(1024, T)
    tile = lambda w: pl.BlockSpec((TM, w), lambda i: (i, 0))
    token_spec, token_arg = _token_operand(after)

    def body(h_ref, hid_ref, wdn_ref, *rest):
        rest[-1][...] = h_ref[...] + 0.5 * _dot(hid_ref[...], wdn_ref[...])

    return pl.pallas_call(
        body, name=name, grid=(T // TM,),
        in_specs=[tile(D), tile(FF), _const_spec((FF, D))] + token_spec,
        out_specs=tile(D), out_shape=SDS((T, D), F32),
        compiler_params=_cp(("arbitrary",)),
    )(h, hid, wdn, *token_arg)


def ffn_fwd(h, g, wup, wdn, name, head=None):
    T = h.shape[0]
    TM = 512
    tile = lambda w: pl.BlockSpec((TM, w), lambda i: (i, 0))
    acc_spec = lambda w: pl.BlockSpec((1, w), lambda i: (0, 0))

    def body(h_ref, g_ref, wup_ref, wdn_ref, *rest):
        out_ref, ab_ref, hid_ref = rest[-3:] if head is None else rest[2:5]
        x = h_ref[...]
        _swiglu_up(x, g_ref[...], wup_ref, ab_ref, hid_ref)
        out = x + 0.5 * _dot(hid_ref[...], wdn_ref[...])
        if head is None:
            out_ref[...] = out
        else:
            t_ref, gf_ref, loss_ref, dgf_ref = rest[0], rest[1], rest[5], rest[6]
            out_ref[...] = _loss_head(out, t_ref[...], gf_ref[...], loss_ref, dgf_ref, pl.program_id(0) == 0)

    head_in, head_specs, head_out_specs, head_out_shape = [], [], [], []
    if head is not None:
        head_in, head_specs = list(head), [tile(D), _const_spec((1, D))]
        head_out_specs, head_out_shape = [acc_spec(1), acc_spec(D)], [SDS((1, 1), F32), SDS((1, D), F32)]
    return pl.pallas_call(
        body, name=name, grid=(T // TM,),
        in_specs=[tile(D), _const_spec((1, D)), _const_spec((D, 2 * FF)), _const_spec((FF, D))] + head_specs,
        out_specs=[tile(D), tile(2 * FF), tile(FF)] + head_out_specs,
        out_shape=[SDS((T, D), F32), SDS((T, 2 * FF), BF16), SDS((T, FF), BF16)] + head_out_shape,
        compiler_params=_cp(("arbitrary",)),
    )(h, g, wup, wdn, *head_in)


def _loss_head(x, target, g, loss_ref, dg_ref, first):
    @pl.when(first)
    def _():
        loss_ref[...] = jnp.zeros_like(loss_ref)
        dg_ref[...] = jnp.zeros_like(dg_ref)

    rstd = _rstd(x)
    xhat = x * rstd
    err = xhat * g - target
    loss_ref[...] += 0.5 * jnp.sum(jnp.mean(err * err, axis=-1, keepdims=True), axis=0, keepdims=True)
    dy = err * (1.0 / D)
    dg_ref[...] += jnp.sum(dy * xhat, axis=0, keepdims=True)
    return _rms_bwd(dy, xhat, rstd, g)


def _adamw_tile(w_ref, g_ref, m_ref, v_ref, go_ref, d_ref, nm_ref, nv_ref):
    gv = g_ref[...]
    go_ref[...] = gv
    nm = B1 * m_ref[...] + (1.0 - B1) * gv
    nv = B2 * v_ref[...] + (1.0 - B2) * (gv * gv)
    nm_ref[...] = nm
    nv_ref[...] = nv
    d_ref[...] = -LR * ((nm / (1.0 - B1 ** STEP)) / (jnp.sqrt(nv / (1.0 - B2 ** STEP)) + ADAM_EPS) + WD * w_ref[...])


def _riders(updates, steps):
    in_specs, out_specs, out_shapes, operands, tiles = [], [], [], [], []
    for w, g, m, v in updates:
        R, C = w.shape
        n = max(d for d in range(1, steps + 1) if R % d == 0 and (R // d) % 8 == 0)
        spec = pl.BlockSpec((R // n, C), lambda i, n=n: (jnp.minimum(i, n - 1), 0))
        in_specs += [spec] * 4
        out_specs += [spec] * 4
        out_shapes += [SDS((R, C), F32)] * 4
        operands += [w, g, m, v]
        tiles.append(n)

    def run(step, in_refs, out_refs):
        for u, n in enumerate(tiles):
            @pl.when(step < n)
            def _(u=u):
                _adamw_tile(*in_refs[4 * u:4 * u + 4], *out_refs[4 * u:4 * u + 4])

    return in_specs, out_specs, out_shapes, operands, run


def ffn_bwd_x(dh, h_in, ab, g, wup, wdn, name, updates=(), after=None):
    T = dh.shape[0]
    TM = 256 if updates else 512
    SUB = 256
    r_in, r_out, r_shapes, r_args, ride = _riders(updates, T // TM)
    token_spec, token_arg = _token_operand(after)
    n_in = len(r_in) + len(token_spec)

    def body(dh_ref, h_ref, ab_ref, g_ref, wup_ref, wdn_ref, *rest):
        dhin_ref, dup_ref, n_ref, dg_ref = rest[n_in:n_in + 4]
        ride(pl.program_id(0), rest[:len(r_in)], rest[n_in + 4:])
        g = g_ref[...]
        dg = None
        for r0 in range(0, TM, SUB):
            rows = slice(r0, r0 + SUB)
            x = h_ref[rows, :]
            rstd = _rstd(x)
            xhat = x * rstd
            n_ref[rows, :] = (xhat * g).astype(BF16)
            dh = dh_ref[rows, :]
            dhh = (0.5 * dh).astype(BF16)
            for c0, w in FF_CHUNKS:
                dhid = _dot_nt(dhh, wdn_ref[c0:c0 + w, :]).astype(BF16)
                dup_ref[rows, c0:c0 + w] = dhid * ab_ref[rows, c0:c0 + w]
                dup_ref[rows, FF + c0:FF + c0 + w] = dhid * ab_ref[rows, FF + c0:FF + c0 + w]
            dn = _dot_nt(dup_ref[rows, :], wup_ref[...])
            dhin_ref[rows, :] = dh + _rms_bwd(dn, xhat, rstd, g)
            part = jnp.sum(dn * xhat, axis=0, keepdims=True)
            dg = part if dg is None else dg + part

        @pl.when(pl.program_id(0) == 0)
        def _():
            dg_ref[...] = jnp.zeros_like(dg_ref)

        dg_ref[...] += dg

    tile = lambda w: pl.BlockSpec((TM, w), lambda i: (i, 0))
    return pl.pallas_call(
        body, name=name, grid=(T // TM,),
        in_specs=[tile(D), tile(D), tile(2 * FF), _const_spec((1, D)), _const_spec((D, 2 * FF)), _const_spec((FF, D))]
        + r_in + token_spec,
        out_specs=[tile(D), tile(2 * FF), tile(D), pl.BlockSpec((1, D), lambda i: (0, 0))] + r_out,
        out_shape=[SDS((T, D), F32), SDS((T, 2 * FF), BF16), SDS((T, D), BF16), SDS((1, D), F32)] + r_shapes,
        compiler_params=_cp(("arbitrary",)),
    )(dh, h_in, ab, g, wup, wdn, *r_args, *token_arg)


TOKEN_SPEC = pl.BlockSpec((8, 128), lambda *_: (0, 0))


def _token_operand(after):
    tokens = [t for t in (after if isinstance(after, tuple) else (after,)) if t is not None]
    return [TOKEN_SPEC] * len(tokens), tokens


def matmul_tn(a, b, name, *, tm, tn, tt=1024, b_scale=None, after=None):
    T, M = a.shape
    N = b.shape[1]
    tt = min(tt, T)
    assert M % tm == 0 and N % tn == 0 and T % tt == 0
    nt = T // tt
    token_spec, token_arg = _token_operand(after)

    def body(a_ref, b_ref, *rest):
        o_ref, acc_ref = rest[-2:]
        t = pl.program_id(2)

        @pl.when(t == 0)
        def _():
            acc_ref[...] = jnp.zeros_like(acc_ref)

        bv = b_ref[...]
        if b_scale is not None:
            bv = bv * b_scale
        acc_ref[...] += _dot_tn(a_ref[...].astype(BF16), bv.astype(BF16))

        @pl.when(t == nt - 1)
        def _():
            o_ref[...] = acc_ref[...].astype(BF16)

    return pl.pallas_call(
        body, name=name, grid=(M // tm, N // tn, nt),
        in_specs=[pl.BlockSpec((tt, tm), lambda i, j, t: (t, i)), pl.BlockSpec((tt, tn), lambda i, j, t: (t, j))]
        + token_spec,
        out_specs=pl.BlockSpec((tm, tn), lambda i, j, t: (i, j)),
        out_shape=SDS((M, N), BF16),
        scratch_shapes=[pltpu.VMEM((tm, tn), F32)],
        compiler_params=_cp(("parallel", "parallel", "arbitrary")),
    )(a, b, *token_arg)


def mix_in_fwd(h1, g, win_t):
    T = h1.shape[0]
    TM = 512

    def body(h_ref, g_ref, w_ref, u_ref, q_ref, kv_ref, z_ref, gt_ref):
        x = h_ref[...]
        u = (x * _rstd(x) * g_ref[...]).astype(BF16)
        u_ref[...] = u
        for c in range(0, AW, 256):
            q_ref[:, c:c + 256] = _dot_nt(u, w_ref[c:c + 256, :]).astype(BF16)
        kv_ref[...] = _dot_nt(u, w_ref[C_KV:C_Z, :]).astype(BF16)
        for c in range(0, PW, 256):
            z_ref[:, c:c + 256] = _dot_nt(u, w_ref[C_Z + c:C_Z + c + 256, :])
        for c in range(0, 2 * D, 256):
            gt_ref[:, c:c + 256] = _dot_nt(u, w_ref[C_G + c:C_G + c + 256, :]).astype(BF16)

    tile = lambda w: pl.BlockSpec((TM, w), lambda i: (i, 0))
    return pl.pallas_call(
        body, name="mix_in_fwd", grid=(T // TM,),
        in_specs=[tile(D), _const_spec((1, D)), _const_spec((INW, D))],
        out_specs=[tile(D), tile(AW), tile(2 * KVW), tile(PW), tile(2 * D)],
        out_shape=[SDS((T, D), BF16), SDS((T, AW), BF16), SDS((T, 2 * KVW), BF16), SDS((T, PW), F32),
                   SDS((T, 2 * D), BF16)],
        compiler_params=_cp(("arbitrary",)),
    )(h1, g, win_t)


def mix_in_bwd(dq, dkv, dz, dgt, dh2, h1, g, win_t):
    T = h1.shape[0]
    TM = min(1024, T)
    SUB = 256

    def body(dq_ref, dkv_ref, dz_ref, dgt_ref, dh2_ref, h_ref, g_ref, w_ref, dh1_ref, dg_ref):
        g = g_ref[...]
        dg = None
        for r0 in range(0, TM, SUB):
            rows = slice(r0, r0 + SUB)
            du = _dot(dq_ref[rows, :], w_ref[0:AW, :])
            du += _dot(dkv_ref[rows, :], w_ref[C_KV:C_Z, :])
            du += _dot(dz_ref[rows, :], w_ref[C_Z:C_G, :])
            du += _dot(dgt_ref[rows, :], w_ref[C_G:INW, :])
            x = h_ref[rows, :]
            rstd = _rstd(x)
            xhat = x * rstd
            dh1_ref[rows, :] = dh2_ref[rows, :] + _rms_bwd(du, xhat, rstd, g)
            part = jnp.sum(du * xhat, axis=0, keepdims=True)
            dg = part if dg is None else dg + part

        @pl.when(pl.program_id(0) == 0)
        def _():
            dg_ref[...] = jnp.zeros_like(dg_ref)

        dg_ref[...] += dg

    tile = lambda w: pl.BlockSpec((TM, w), lambda i: (i, 0))
    return pl.pallas_call(
        body, name="mix_in_bwd", grid=(T // TM,),
        in_specs=[tile(AW), tile(2 * KVW), tile(PW), tile(2 * D), tile(D), tile(D), _const_spec((1, D)),
                  _const_spec((INW, D))],
        out_specs=[tile(D), pl.BlockSpec((1, D), lambda i: (0, 0))],
        out_shape=[SDS((T, D), F32), SDS((1, D), F32)],
        compiler_params=_cp(("arbitrary",)),
    )(dq, dkv, dz, dgt, dh2, h1, g, win_t)


PAIR = 2 * HD
NPAIR = GQ // 2


def _lo_lanes():
    return lax.broadcasted_iota(jnp.int32, (BLK, PAIR), 1) < HD


def _stack_heads(ref, kvh, scale=None):
    lo = _lo_lanes()
    parts = []
    for pr in range(NPAIR):
        t = ref[:, (kvh * NPAIR + pr) * PAIR:(kvh * NPAIR + pr + 1) * PAIR]
        if scale is not None:
            t = t * scale
        zero = jnp.zeros_like(t)
        parts += [jnp.where(lo, t, zero), jnp.where(lo, zero, t)]
    return jnp.concatenate(parts, axis=0)


def _kv_tiles(kvc_ref, kvp_ref, tile, kvh):
    lo = _lo_lanes()
    dup, left, right = [], [], []
    for ref in (kvp_ref, kvc_ref):
        t = ref[:, tile * PAIR:(tile + 1) * PAIR]
        r = pltpu.roll(t.astype(F32), HD, 1).astype(BF16)
        zero = jnp.zeros_like(t)
        a, b = (t, r) if kvh == 0 else (r, t)
        dup.append(jnp.where(lo, a, b))
        left.append(jnp.where(lo, a, zero))
        right.append(jnp.where(lo, zero, b))
    cat = lambda xs: jnp.concatenate(xs, axis=0)
    return cat(dup), cat(left), cat(right)


def _band_consts(first):
    row = lax.broadcasted_iota(jnp.int32, (BLK, BLK), 0)
    col = lax.broadcasted_iota(jnp.int32, (BLK, BLK), 1)
    upper = col > row
    dist = jnp.where(upper, row - col + BLK, row - col).astype(F32)
    pen = jnp.where(jnp.logical_and(upper, first), -jnp.inf, 0.0)
    return upper, dist, pen


def _split_band(upper, t):
    zero = jnp.zeros_like(t)
    return jnp.concatenate([jnp.where(upper, t, zero), jnp.where(upper, zero, t)], axis=1)


def attn_fwd(q, kv, sinks):
    T = q.shape[0]
    nb = T // BLK

    def body(sink_ref, q_ref, kvc_ref, kvp_ref, att_ref, lse_ref):
        upper, dist, pen = _band_consts(pl.program_id(0) == 0)
        scores, values = [], []
        for kvh in range(NKV):
            kdup, _, _ = _kv_tiles(kvc_ref, kvp_ref, 0, kvh)
            values.append(_kv_tiles(kvc_ref, kvp_ref, 1, kvh)[1:])
            scores.append(_dot_nt(_stack_heads(q_ref, kvh, SCALE), kdup))
        for kvh in range(NKV):
            s_all = scores[kvh]
            vleft, vright = values[kvh]
            for pr in range(NPAIR):
                outs, inv = [], []
                for side, vpad in ((0, vleft), (1, vright)):
                    g = 2 * pr + side
                    hq = kvh * GQ + g
                    sink = sink_ref[0, hq]
                    rows = slice(g * BLK, (g + 1) * BLK)
                    s = jnp.where(upper, s_all[rows, 0:BLK], s_all[rows, BLK:2 * BLK]) - SLOPES[hq] * dist + pen
                    m = jnp.maximum(jnp.max(s, axis=-1, keepdims=True), sink)
                    p = jnp.exp(s - m)
                    l = jnp.sum(p, axis=-1, keepdims=True) + jnp.exp(sink - m)
                    lse_ref[:, hq:hq + 1] = m + jnp.log(l)
                    outs.append(_dot(_split_band(upper, p.astype(BF16)), vpad))
                    inv.append(1.0 / l)
                col0 = (kvh * NPAIR + pr) * PAIR
                att_ref[:, col0:col0 + PAIR] = ((outs[0] + outs[1]) * jnp.where(_lo_lanes(), inv[0], inv[1])).astype(BF16)

    return pl.pallas_call(
        body, name="attn_fwd", grid=(nb,),
        in_specs=[pl.BlockSpec(memory_space=pltpu.SMEM),
                  pl.BlockSpec((BLK, AW), lambda i: (i, 0)),
                  pl.BlockSpec((BLK, 2 * KVW), lambda i: (i, 0)),
                  pl.BlockSpec((BLK, 2 * KVW), lambda i: (jnp.maximum(i - 1, 0), 0))],
        out_specs=[pl.BlockSpec((BLK, AW), lambda i: (i, 0)), pl.BlockSpec((BLK, NQ), lambda i: (i, 0))],
        out_shape=[SDS((T, AW), BF16), SDS((T, NQ), F32)],
        compiler_params=_cp(("arbitrary",)),
    )(sinks, q, kv, kv)


def attn_bwd(q, kv, datt, att, lse, sinks, after=None):
    T = q.shape[0]
    nb = T // BLK
    token_spec, token_arg = _token_operand(after)

    def body(sink_ref, q_ref, kvc_ref, kvp_ref, do_ref, out_ref, lse_ref, *rest):
        dq_ref, dkv_ref, dsink_ref, carry_ref = rest[-4:]
        i = pl.program_id(0)

        @pl.when(i == 0)
        def _():
            dsink_ref[...] = jnp.zeros_like(dsink_ref)
            carry_ref[...] = jnp.zeros_like(carry_ref)

        @pl.when(i < nb)
        def _():
            upper, dist, pen = _band_consts(i == 0)
            lo = _lo_lanes()
            dk_dup, dv_dup = [], []
            staged = []
            for kvh in range(NKV):
                kdup, kleft, kright = _kv_tiles(kvc_ref, kvp_ref, 0, kvh)
                vdup, _, _ = _kv_tiles(kvc_ref, kvp_ref, 1, kvh)
                qs = _stack_heads(q_ref, kvh, SCALE)
                dos = _stack_heads(do_ref, kvh)
                staged.append((kleft, kright, qs, dos, _dot_nt(qs, kdup), _dot_nt(dos, vdup)))
            deltas = []
            for pair in range(NQ // 2):
                cols = slice(pair * PAIR, (pair + 1) * PAIR)
                t = do_ref[:, cols].astype(F32) * out_ref[:, cols].astype(F32)
                deltas += [jnp.sum(jnp.where(lo, t, 0.0), axis=-1, keepdims=True),
                           jnp.sum(jnp.where(lo, 0.0, t), axis=-1, keepdims=True)]
            for kvh in range(NKV):
                kleft, kright, qs, dos, s_all, dp_all = staged[kvh]
                ds_parts, p_parts = [], []
                for pr in range(NPAIR):
                    dq = None
                    for side, kpad in ((0, kleft), (1, kright)):
                        g = 2 * pr + side
                        hq = kvh * GQ + g
                        lse_h = lse_ref[:, hq:hq + 1]
                        rows = slice(g * BLK, (g + 1) * BLK)
                        s = jnp.where(upper, s_all[rows, 0:BLK], s_all[rows, BLK:2 * BLK]) - SLOPES[hq] * dist + pen
                        p = jnp.exp(s - lse_h)
                        dp = jnp.where(upper, dp_all[rows, 0:BLK], dp_all[rows, BLK:2 * BLK])
                        delta = deltas[hq]
                        dsink_ref[:, hq:hq + 1] += -jnp.sum(jnp.exp(sink_ref[0, hq] - lse_h) * delta, axis=0,
                                                            keepdims=True)
                        ds = _split_band(upper, (p * (dp - delta)).astype(BF16))
                        ds_parts.append(ds)
                        p_parts.append(_split_band(upper, p.astype(BF16)))
                        d = _dot(ds, kpad)
                        dq = d if dq is None else dq + d
                    col0 = (kvh * NPAIR + pr) * PAIR
                    dq_ref[:, col0:col0 + PAIR] = (dq * SCALE).astype(BF16)
                dkw = _dot_tn(qs, jnp.concatenate(ds_parts, axis=0)).T
                dvw = _dot_tn(dos, jnp.concatenate(p_parts, axis=0)).T
                dk_dup.append(dkw + pltpu.roll(dkw, HD, 1))
                dv_dup.append(dvw + pltpu.roll(dvw, HD, 1))
            dk = jnp.where(jnp.concatenate([lo, lo], axis=0), dk_dup[0], dk_dup[1])
            dv = jnp.where(jnp.concatenate([lo, lo], axis=0), dv_dup[0], dv_dup[1])
            dkv_ref[:, 0:PAIR] = (carry_ref[:, 0:PAIR] + dk[0:BLK]).astype(BF16)
            dkv_ref[:, PAIR:2 * PAIR] = (carry_ref[:, PAIR:2 * PAIR] + dv[0:BLK]).astype(BF16)
            carry_ref[:, 0:PAIR] = dk[BLK:2 * BLK]
            carry_ref[:, PAIR:2 * PAIR] = dv[BLK:2 * BLK]

        @pl.when(i == nb)
        def _():
            dkv_ref[...] = carry_ref[...].astype(BF16)

    cur = lambda i: (jnp.minimum(i, nb - 1), 0)
    prev = lambda i: (jnp.maximum(jnp.minimum(i, nb - 1) - 1, 0), 0)
    return pl.pallas_call(
        body, name="attn_bwd", grid=(nb + 1,),
        in_specs=[pl.BlockSpec(memory_space=pltpu.SMEM),
                  pl.BlockSpec((BLK, AW), cur), pl.BlockSpec((BLK, 2 * KVW), cur), pl.BlockSpec((BLK, 2 * KVW), prev),
                  pl.BlockSpec((BLK, AW), cur), pl.BlockSpec((BLK, AW), cur), pl.BlockSpec((BLK, NQ), cur)] + token_spec,
        out_specs=[pl.BlockSpec((BLK, AW), cur),
                   pl.BlockSpec((BLK, 2 * KVW), lambda i: (jnp.maximum(i - 1, 0), 0)),
                   pl.BlockSpec((1, NQ), lambda i: (0, 0))],
        out_shape=[SDS((T, AW), BF16), SDS((T, 2 * KVW), BF16), SDS((1, NQ), F32)],
        scratch_shapes=[pltpu.VMEM((BLK, 2 * KVW), F32)],
        compiler_params=_cp(("arbitrary",)),
    )(sinks, q, kv, kv, datt, att, lse, *token_arg)


def _inv_counts(t0, rows):
    t = (t0 + lax.broadcasted_iota(jnp.int32, (rows, 1), 0) + 1).astype(F32)
    return [1.0 / jnp.minimum(t, float(w)) for w in POOL_WINDOWS]


def pool_fwd(z, wmix, scale, after=None):
    T = z.shape[0]
    TM = min(1024, T)
    L = TM + HALO
    token_spec, token_arg = _token_operand(after)

    def body(z_ref, halo_ref, wmix_ref, scale_ref, *rest):
        pooled_ref, mixs_ref = rest[-2:]
        i = pl.program_id(0)
        halo = jnp.where(i > 0, halo_ref[...], 0.0)
        zt = z_ref[...]
        e = jnp.concatenate([halo, zt], axis=0)
        sums = []
        s = e
        for k in (1, 2, 4, 8):
            s = s + pltpu.roll(s, k, 0)
            sums.append(s)
        inv = _inv_counts(i * TM, TM)
        for gi in range(len(POOL_WINDOWS)):
            cols = slice(gi * PG, (gi + 1) * PG)
            pooled = (sums[gi][HALO:, cols] * inv[gi] - zt[:, cols]).astype(BF16)
            pooled_ref[:, cols] = pooled
            mixs_ref[:, cols] = (_dot(pooled, wmix_ref[gi]) * scale_ref[:, cols]).astype(BF16)

    return pl.pallas_call(
        body, name="pool_fwd", grid=(T // TM,),
        in_specs=[pl.BlockSpec((TM, PW), lambda i: (i, 0)),
                  pl.BlockSpec((HALO, PW), lambda i: (jnp.maximum(i * (TM // HALO) - 1, 0), 0)),
                  _const_spec((len(POOL_WINDOWS), PG, PG)), _const_spec((1, PW))] + token_spec,
        out_specs=[pl.BlockSpec((TM, PW), lambda i: (i, 0)), pl.BlockSpec((TM, PW), lambda i: (i, 0))],
        out_shape=[SDS((T, PW), BF16), SDS((T, PW), BF16)],
        compiler_params=_cp(("arbitrary",)),
    )(z, z, wmix, scale, *token_arg)


def pool_bwd(dmixs, pooled, wmix, scale, after=None):
    T = dmixs.shape[0]
    TM = min(1024, T)
    L = TM + HALO
    nt = T // TM
    token_spec, token_arg = _token_operand(after)

    def body(dm_ref, halo_ref, pooled_ref, wmix_ref, scale_ref, *rest):
        dz_ref, dwmix_ref, dscale_ref = rest[-3:]
        i = pl.program_id(0)

        @pl.when(i == 0)
        def _():
            dwmix_ref[...] = jnp.zeros_like(dwmix_ref)
            dscale_ref[...] = jnp.zeros_like(dscale_ref)

        halo = jnp.where(i < nt - 1, halo_ref[...], 0.0)
        dm = dm_ref[...]
        e = jnp.concatenate([dm, halo], axis=0)
        inv = _inv_counts(i * TM, L)
        for gi in range(len(POOL_WINDOWS)):
            cols = slice(gi * PG, (gi + 1) * PG)
            w = wmix_ref[gi]
            dmixed = (e[:, cols] * scale_ref[:, cols]).astype(BF16)
            dpooled = _dot_nt(dmixed, w)
            pooled = pooled_ref[:, cols]
            mixed = _dot(pooled, w)
            dscale_ref[:, cols] += jnp.sum(dm[:, cols] * mixed, axis=0, keepdims=True)
            dwmix_ref[gi] += _dot_tn(pooled, dmixed[:TM])
            s = dpooled * inv[gi]
            k = 1
            while k < POOL_WINDOWS[gi]:
                s = s + pltpu.roll(s, L - k, 0)
                k *= 2
            dz_ref[:, cols] = (s[:TM] - dpooled[:TM]).astype(BF16)

    return pl.pallas_call(
        body, name="pool_bwd", grid=(nt,),
        in_specs=[pl.BlockSpec((TM, PW), lambda i: (i, 0)),
                  pl.BlockSpec((HALO, PW), lambda i: (jnp.minimum((i + 1) * (TM // HALO), T // HALO - 1), 0)),
                  pl.BlockSpec((TM, PW), lambda i: (i, 0)),
                  _const_spec((len(POOL_WINDOWS), PG, PG)), _const_spec((1, PW))] + token_spec,
        out_specs=[pl.BlockSpec((TM, PW), lambda i: (i, 0)),
                   pl.BlockSpec((len(POOL_WINDOWS), PG, PG), lambda i: (0, 0, 0)),
                   pl.BlockSpec((1, PW), lambda i: (0, 0))],
        out_shape=[SDS((T, PW), BF16), SDS((len(POOL_WINDOWS), PG, PG), F32), SDS((1, PW), F32)],
        compiler_params=_cp(("arbitrary",)),
    )(dmixs, dmixs, pooled, wmix, scale, *token_arg)


def merge_fwd(att, mixs, gt, h1, wattn, wpool, wout):
    T = h1.shape[0]
    TM = 512

    def body(att_ref, mixs_ref, gt_ref, h_ref, wa_ref, wp_ref, wo_ref, h2_ref, mg_ref, gf_ref):
        a = _dot(att_ref[...], wa_ref[...])
        p = _dot(mixs_ref[...], wp_ref[...])
        sa = _sigmoid(gt_ref[:, 0:D].astype(F32))
        sp = _sigmoid(gt_ref[:, D:2 * D].astype(F32))
        gf_ref[:, 0:D] = (a * sa * (1.0 - sa)).astype(BF16)
        gf_ref[:, D:2 * D] = (p * sp * (1.0 - sp)).astype(BF16)
        mg = (sa * a + sp * p).astype(BF16)
        mg_ref[...] = mg
        h2_ref[...] = h_ref[...] + _dot(mg, wo_ref[...])

    tile = lambda w: pl.BlockSpec((TM, w), lambda i: (i, 0))
    return pl.pallas_call(
        body, name="merge_fwd", grid=(T // TM,),
        in_specs=[tile(AW), tile(PW), tile(2 * D), tile(D), _const_spec((AW, D)), _const_spec((PW, D)),
                  _const_spec((D, D))],
        out_specs=[tile(D), tile(D), tile(2 * D)],
        out_shape=[SDS((T, D), F32), SDS((T, D), BF16), SDS((T, 2 * D), BF16)],
        compiler_params=_cp(("arbitrary",)),
    )(att, mixs, gt, h1, wattn, wpool, wout)


def merge_bwd(dh2, gt, gf, wattn, wpool, wout, after=None):
    T = dh2.shape[0]
    TM = 512
    token_spec, token_arg = _token_operand(after)

    def body(dh2_ref, gt_ref, gf_ref, wa_ref, wp_ref, wo_ref, *rest):
        datt_ref, dmixs_ref, dgt_ref, da_ref, dp_ref = rest[-5:]
        dm = _dot_nt(dh2_ref[...].astype(BF16), wo_ref[...])
        da = (dm * _sigmoid(gt_ref[:, 0:D].astype(F32))).astype(BF16)
        dp = (dm * _sigmoid(gt_ref[:, D:2 * D].astype(F32))).astype(BF16)
        da_ref[...] = da
        dp_ref[...] = dp
        dgt_ref[:, 0:D] = (dm * gf_ref[:, 0:D].astype(F32)).astype(BF16)
        dgt_ref[:, D:2 * D] = (dm * gf_ref[:, D:2 * D].astype(F32)).astype(BF16)
        datt_ref[...] = _dot_nt(da, wa_ref[...]).astype(BF16)
        dmixs_ref[...] = _dot_nt(dp, wp_ref[...])

    tile = lambda w: pl.BlockSpec((TM, w), lambda i: (i, 0))
    return pl.pallas_call(
        body, name="merge_bwd", grid=(T // TM,),
        in_specs=[tile(D), tile(2 * D), tile(2 * D), _const_spec((AW, D)), _const_spec((PW, D)),
                  _const_spec((D, D))] + token_spec,
        out_specs=[tile(AW), tile(PW), tile(2 * D), tile(D), tile(D)],
        out_shape=[SDS((T, AW), BF16), SDS((T, PW), F32), SDS((T, 2 * D), BF16), SDS((T, D), BF16),
                   SDS((T, D), BF16)],
        compiler_params=_cp(("arbitrary",)),
    )(dh2, gt, gf, wattn, wpool, wout, *token_arg)


def adamw(updates, place, name, after=None):
    tile_bytes = 2 * 1024 * 1024 // len(updates)
    jobs = []
    for w, g, m, v in updates:
        R, C = w.shape
        tr = R
        if R * C * 4 > tile_bytes:
            tr = next(cand for cand in (512, 256, 128, 64, 32, 16, 8) if R % cand == 0 and cand * C * 4 <= tile_bytes)
        blk = ((tr, C), lambda ls, p: (ls, 0))
        jobs.append((R // tr, [(a, *blk) for a in (w, g, m, v)], [((R, C), F32, *blk)] * 4,
                     lambda ins, outs: _adamw_tile(*ins, *outs)))
    return _run_jobs(jobs, place, name, after)


GROUP_UP1, GROUP_MIX, GROUP_FFN2 = 0, 1, 2


def local_fwd_bwd(x, target, S, comm):
    w_up1 = comm.weights(GROUP_UP1, None)["ffn1_w_up"]
    ab1, hid1 = ffn_up(x, S["ffn1_norm"], w_up1, "ffn1_up", after=comm.started())
    W = comm.weights(GROUP_MIX, hid1)
    h1 = ffn_down(x, hid1, W["ffn1_w_down"], "ffn1_down", after=comm.started())
    u, q, kv, z, gt = mix_in_fwd(h1, S["mix_norm"], W["w_in"])
    att, lse = attn_fwd(q, kv, S["sinks"])
    pooled, mixs = pool_fwd(z, S["pool_w_mix"], S["pool_scale"], after=comm.prefetch(GROUP_FFN2, att))
    h2, merged, gf = merge_fwd(att, mixs, gt, h1, W["w_attn_up"], W["w_pool_up"], W["w_out"])
    W = comm.weights(GROUP_FFN2, h2)
    dh3, ab2, hid2, loss, g_final = ffn_fwd(h2, S["ffn2_norm"], W["ffn2_w_up"], W["ffn2_w_down"], "ffn2_fwd",
                                            head=(target, S["final_norm"]))

    G = {"final_norm": g_final}
    dh2, dup2, n2, G["ffn2_norm"] = ffn_bwd_x(dh3, h2, ab2, S["ffn2_norm"], W["ffn2_w_up"], W["ffn2_w_down"],
                                              "ffn2_bwd_x")
    ffn2 = ("ffn2_w_down", "ffn2_w_up")
    token = comm.grads({"ffn2_w_down": matmul_tn(hid2, dh3, "ffn2_dw_down", tm=1408, tn=512, tt=4096, b_scale=0.5),
                        "ffn2_w_up": matmul_tn(n2, dup2, "ffn2_dw_up", tm=D, tn=512, tt=4096)})

    W = comm.weights(GROUP_MIX, None)
    datt, dmixs, dgt, da, dp = merge_bwd(dh2, gt, gf, W["w_attn_up"], W["w_pool_up"], W["w_out"], after=token)
    token = comm.advance(ffn2, datt)
    g_mix = {"w_out": matmul_tn(merged, dh2, "dw_out", tm=D, tn=D, after=token),
             "w_attn_up": matmul_tn(att, da, "dw_attn_up", tm=AW, tn=D),
             "w_pool_up": matmul_tn(mixs, dp, "dw_pool_up", tm=PW, tn=D)}
    dz, G["pool_w_mix"], G["pool_scale"] = pool_bwd(dmixs, pooled, S["pool_w_mix"], S["pool_scale"], after=token)
    dq, dkv, G["sinks"] = attn_bwd(q, kv, datt, att, lse, S["sinks"], after=token)
    dh1, G["mix_norm"] = mix_in_bwd(dq, dkv, dz, dgt, dh2, h1, S["mix_norm"], W["w_in"])
    g_mix["w_in"] = jnp.concatenate([
        matmul_tn(dq, u, "dw_in_q", tm=AW, tn=D),
        matmul_tn(dkv, u, "dw_in_kv", tm=2 * KVW, tn=D),
        matmul_tn(dz, u, "dw_in_z", tm=PW, tn=D),
        matmul_tn(dgt, u, "dw_in_g", tm=D, tn=512, tt=4096),
    ], axis=0)
    mix = tuple(g_mix)
    token = comm.grads(g_mix)

    g_dn1 = matmul_tn(hid1, dh1, "ffn1_dw_down", tm=1408, tn=512, tt=4096, b_scale=0.5, after=token)
    token = comm.advance(mix, g_dn1)
    token2 = comm.grads({"ffn1_w_down": g_dn1})
    dx, dup1, n1, G["ffn1_norm"], *outs = ffn_bwd_x(dh1, x, ab1, S["ffn1_norm"], w_up1, W["ffn1_w_down"], "ffn1_bwd_x",
                                                   updates=comm.reduce(ffn2, token2), after=(token, token2))
    comm.updated(ffn2, outs)
    token = comm.advance(("ffn1_w_down",), dx)
    token2 = comm.small_start(G, loss, dx)
    g_up1 = matmul_tn(n1, dup1, "ffn1_dw_up", tm=D, tn=512, tt=4096, after=(token, token2))
    token = comm.grads({"ffn1_w_up": g_up1})
    updates = comm.reduce(("ffn1_w_down",), token)
    token = comm.advance(("ffn1_w_up",), [u[1] for u in updates])
    done = comm.update(("ffn1_w_down",), updates, token)
    done = comm.finish(mix, done)
    done = comm.small_finish(done)
    comm.finish(("ffn1_w_up",), done)
    return loss, dx


HBM_SPEC = pl.BlockSpec(memory_space=pltpu.HBM)


def _me():
    return lax.axis_index("x"), lax.axis_index("y"), lax.axis_index("c")


def _peer_chip(x, y, k):
    return x ^ (k >> 1), y ^ (k & 1)


def _piece_half(ref, rowlike, j, h):
    if rowlike:
        return ref.at[j, h]
    ns = ref.shape[-1] // N_CHIPS
    return ref.at[h, :, pl.ds(pl.multiple_of(j * ns, 128), ns)]


def _piece(ref, rowlike, j):
    if rowlike:
        return ref.at[j]
    ns = ref.shape[-1] // N_CHIPS
    return ref.at[:, :, pl.ds(pl.multiple_of(j * ns, 128), ns)]


def _full_shape(shard_view, rowlike):
    _, kh, ns = shard_view.shape
    return (N_CHIPS, 2, kh, ns) if rowlike else (2, kh, N_CHIPS * ns)


def _remote(src, dst, send_sem, recv_sem, dev):
    return pltpu.make_async_remote_copy(src, dst, send_sem, recv_sem, device_id=dev, device_id_type=MESH)


SEM_SPEC = pl.BlockSpec(memory_space=pltpu.SEMAPHORE)
SPLIT_PARAMS = pltpu.CompilerParams(has_side_effects=pltpu.SideEffectType.DATAFLOW_SIDE_EFFECTING)


def _gather_plan(rowlikes):
    def plan(s_refs, f_refs, a, k, x, y, c):
        px, py = _peer_chip(x, y, k)
        return (s_refs[a].at[c], _piece_half(f_refs[a], rowlikes[a], 2 * x + y, c),
                _piece_half(f_refs[a], rowlikes[a], 2 * px + py, c), (px, py, c))
    return plan


def _all_to_all_plan(rowlikes):
    def plan(q_refs, r_refs, a, k, x, y, c):
        px, py = _peer_chip(x, y, k)
        if rowlikes[a]:
            src = q_refs[a].at[2 * px + py]
        else:
            ns = q_refs[a].shape[-1] // N_CHIPS
            src = q_refs[a].at[:, pl.ds(pl.multiple_of((2 * px + py) * ns, 128), ns)]
        return src, r_refs[a].at[k - 1], r_refs[a].at[k - 1], (px, py, c)
    return plan


def _swap_plan(rowlikes):
    def plan(g_refs, got_refs, a, k, x, y, c):
        src = g_refs[a].at[:, 1 - c] if rowlikes[a] else g_refs[a].at[1 - c]
        return src, got_refs[a], got_refs[a], (x, y, 1 - c)
    return plan


def _everyone_plan(s_refs, slot_refs, a, k, x, y, c):
    px, py, pc = x ^ (k >> 2), y ^ ((k >> 1) & 1), c ^ (k & 1)
    return s_refs[a], slot_refs[a].at[4 * x + 2 * y + c], slot_refs[a].at[4 * px + 2 * py + pc], (px, py, pc)


def _forward_plan(rowlikes):
    def plan(s_refs, f_refs, a, k, x, y, c):
        sib = (x, y, 1 - c)
        if k == 0:
            own = _piece(f_refs[a], rowlikes[a], 2 * x + y)
            return s_refs[a], own, own, sib
        px, py = _peer_chip(x, y, k)
        mine = _piece_half(f_refs[a], rowlikes[a], 2 * px + py, c)
        return mine, mine, _piece_half(f_refs[a], rowlikes[a], 2 * px + py, 1 - c), sib
    return plan


CHIPS, SIBLING, EVERYONE, FORWARDS = (1, 2, 3), (1,), tuple(range(1, 8)), (0, 1, 2, 3)


def _as_list(after):
    return [] if after is None else list(after) if isinstance(after, (list, tuple)) else [after]


def exchange_start(srcs, land_shapes, plan, after, name, peers=CHIPS):
    n = len(srcs)
    lands = [l if hasattr(l, "dtype") else lax.empty(l, s.dtype) for l, s in zip(land_shapes, srcs)]

    behind = _as_list(after)

    def body(*refs):
        s_refs, l_refs = refs[:n], refs[n:2 * n]
        send_sems, recv_sems = refs[2 * n + len(behind)], refs[2 * n + len(behind) + 1]
        token = refs[-1]
        x, y, c = _me()
        for a in range(n):
            for i, k in enumerate(peers):
                src, dst, _, peer = plan(s_refs, l_refs, a, k, x, y, c)
                sem = len(peers) * a + i
                _remote(src, dst, send_sems.at[sem], recv_sems.at[sem], peer).start()
        token[...] = jnp.zeros_like(token)

    n_sems = len(peers) * n
    outs = pl.pallas_call(
        body, name=name, in_specs=[HBM_SPEC] * (2 * n) + [pl.BlockSpec(memory_space=pl.ANY)] * len(behind),
        out_specs=[SEM_SPEC, SEM_SPEC] + [HBM_SPEC] * (2 * n) + [pl.BlockSpec(memory_space=pltpu.VMEM)],
        out_shape=[pltpu.SemaphoreType.DMA((n_sems,)), pltpu.SemaphoreType.DMA((n_sems,))]
        + [pltpu.HBM(a.shape, a.dtype) for a in (*srcs, *lands)] + [SDS((8, 128), F32)],
        input_output_aliases={i: 2 + i for i in range(2 * n)},
        compiler_params=SPLIT_PARAMS,
    )(*[pltpu.with_memory_space_constraint(a, pltpu.HBM) for a in (*srcs, *lands)], *behind)
    return {"sems": outs[:2], "srcs": outs[2:2 + n], "lands": outs[2 + n:2 + 2 * n], "token": outs[-1]}


def exchange_wait(state, plan, after, name, peers=CHIPS):
    n = len(state["srcs"])
    behind = _as_list(after)

    def body(*refs):
        s_refs, l_refs = refs[:n], refs[n:2 * n]
        send_sems, recv_sems = refs[2 * n], refs[2 * n + 1]
        x, y, c = _me()
        for a in range(n):
            for i, k in enumerate(peers):
                src, _, landing, peer = plan(s_refs, l_refs, a, k, x, y, c)
                sem = len(peers) * a + i
                cp = _remote(src, landing, send_sems.at[sem], recv_sems.at[sem], peer)
                cp.wait_send()
                cp.wait_recv()

    bufs = (*state["srcs"], *state["lands"])
    outs = pl.pallas_call(
        body, name=name,
        in_specs=[HBM_SPEC] * (2 * n) + [SEM_SPEC, SEM_SPEC] + [pl.BlockSpec(memory_space=pl.ANY)] * len(behind),
        out_specs=[HBM_SPEC] * (2 * n),
        out_shape=[pltpu.HBM(a.shape, a.dtype) for a in bufs],
        input_output_aliases={i: i for i in range(2 * n)},
        compiler_params=SPLIT_PARAMS,
    )(*bufs, *state["sems"], *behind)
    return outs[:n], outs[n:]


def gather_finish(shards, fulls, rowlikes, name):
    n = len(shards)

    def body(*refs):
        s_refs, f_refs = refs[:n], refs[2 * n:3 * n]
        send_sems, recv_sems = refs[3 * n:]
        x, y, c = _me()
        chip = 2 * x + y
        sib = (x, y, 1 - c)
        sends = []
        for a in range(n):
            own = _piece(f_refs[a], rowlikes[a], chip)
            cp = _remote(s_refs[a], own, send_sems.at[a, 0], recv_sems.at[a, 0], sib)
            cp.start()
            sends.append(cp)
            for k in (1, 2, 3):
                px, py = _peer_chip(x, y, k)
                slot = _piece_half(f_refs[a], rowlikes[a], 2 * px + py, c)
                cp = _remote(slot, slot, send_sems.at[a, k], recv_sems.at[a, k], sib)
                cp.start()
                sends.append(cp)
        for a in range(n):
            own = _piece(f_refs[a], rowlikes[a], chip)
            _remote(own, own, send_sems.at[a, 0], recv_sems.at[a, 0], sib).wait_recv()
            for k in (1, 2, 3):
                px, py = _peer_chip(x, y, k)
                slot = _piece_half(f_refs[a], rowlikes[a], 2 * px + py, 1 - c)
                _remote(slot, slot, send_sems.at[a, k], recv_sems.at[a, k], sib).wait_recv()
        for cp in sends:
            cp.wait_send()

    return pl.pallas_call(
        body, name=name, in_specs=[HBM_SPEC] * (2 * n), out_specs=[HBM_SPEC] * n,
        out_shape=[SDS(f.shape, f.dtype) for f in fulls],
        input_output_aliases={n + a: a for a in range(n)},
        scratch_shapes=[pltpu.SemaphoreType.DMA((n, 4)), pltpu.SemaphoreType.DMA((n, 4))],
    )(*shards, *fulls)


def _half_buffer_shape(gview, rowlike):
    return (N_CHIPS,) + gview.shape[2:] if rowlike else gview.shape[1:]


def _row_tiles(kh, ns):
    return 1 if kh * ns <= 256 * 1024 else 2


def _run_jobs(jobs, place, name, after=None):
    n_steps = max(job[0] for job in jobs)

    def spec(block, index, steps):
        return pl.BlockSpec(block, lambda s, p: index(jnp.minimum(s, steps - 1), p))

    in_specs = [spec(b, ix, steps) for steps, ins, _, _ in jobs for _, b, ix in ins]
    out_specs = [spec(b, ix, steps) for steps, _, outs, _ in jobs for _, _, b, ix in outs]
    token_spec, token_arg = _token_operand(after)
    n_in = len(in_specs) + len(token_spec)

    def body(place_ref, *refs):
        s = pl.program_id(0)
        i, o = 0, n_in
        for steps, ins, outs, fn in jobs:
            in_refs, out_refs = refs[i:i + len(ins)], refs[o:o + len(outs)]
            i, o = i + len(ins), o + len(outs)

            @pl.when(s < steps)
            def _(fn=fn, in_refs=in_refs, out_refs=out_refs):
                fn(in_refs, out_refs)

    return pl.pallas_call(
        body, name=name,
        grid_spec=pltpu.PrefetchScalarGridSpec(num_scalar_prefetch=1, grid=(n_steps,),
                                               in_specs=in_specs + token_spec, out_specs=out_specs),
        out_shape=[SDS(shape, dtype) for _, _, outs, _ in jobs for shape, dtype, _, _ in outs],
        compiler_params=_cp(("arbitrary",)),
    )(place, *[a for _, ins, _, _ in jobs for a, _, _ in ins], *token_arg)


def add_halves(gviews, gots, rowlikes, place, name):
    def add(ins, outs):
        outs[0][...] = (ins[0][...].astype(F32) + ins[1][...].astype(F32)).astype(BF16)

    jobs = []
    for g, got, rowlike in zip(gviews, gots, rowlikes):
        kh, ns = (g.shape[2], g.shape[3]) if rowlike else (g.shape[1], g.shape[2] // N_CHIPS)
        rt = _row_tiles(kh, ns)
        tr = kh // rt
        if rowlike:
            g_in = (g, (None, None, tr, ns), lambda ls, p, rt=rt: (ls // rt, p[0], ls % rt, 0))
            half = ((None, tr, ns), lambda ls, p, rt=rt: (ls // rt, ls % rt, 0))
        else:
            g_in = (g, (None, tr, ns), lambda ls, p, rt=rt: (p[0], ls % rt, ls // rt))
            half = ((tr, ns), lambda ls, p, rt=rt: (ls % rt, ls // rt))
        jobs.append((N_CHIPS * rt, [g_in, (got, *half)], [(got.shape, BF16, *half)], add))
    return _run_jobs(jobs, place, name)


def _slot_shape(q, rowlike):
    return (3,) + (q.shape[1:] if rowlike else (q.shape[0], q.shape[1] // N_CHIPS))


def sum_pieces(qs, recvs, rowlikes, place, name):
    def total(ins, outs):
        acc = ins[0][...].astype(F32)
        for k in range(3):
            acc = acc + ins[1][k].astype(F32)
        outs[0][...] = acc

    jobs = []
    for q, recv, rowlike in zip(qs, recvs, rowlikes):
        kh, ns = recv.shape[1], recv.shape[2]
        rt = _row_tiles(kh, ns)
        tr = kh // rt
        mine = ((None, tr, ns), lambda ls, p: (p[1], ls, 0)) if rowlike else ((tr, ns), lambda ls, p: (ls, p[1]))
        jobs.append((rt, [(q, *mine), (recv, (3, tr, ns), lambda ls, p: (0, ls, 0))],
                     [((2, kh, ns), F32, (None, tr, ns), lambda ls, p: (p[0], ls, 0))], total))
    return _run_jobs(jobs, place, name)


def join_halves(halves, name):
    n = len(halves)

    def body(*refs):
        o_refs = refs[n:2 * n]
        send_sems, recv_sems = refs[2 * n:]
        x, y, c = _me()
        sib = (x, y, 1 - c)
        sends = []
        for a in range(n):
            cp = _remote(o_refs[a].at[c], o_refs[a].at[c], send_sems.at[a], recv_sems.at[a], sib)
            cp.start()
            sends.append(cp)
        for a in range(n):
            got = o_refs[a].at[1 - c]
            _remote(got, got, send_sems.at[a], recv_sems.at[a], sib).wait_recv()
        for cp in sends:
            cp.wait_send()

    return pl.pallas_call(
        body, name=name, in_specs=[HBM_SPEC] * n, out_specs=[HBM_SPEC] * n,
        out_shape=[SDS(h.shape, h.dtype) for h in halves],
        input_output_aliases={a: a for a in range(n)},
        scratch_shapes=[pltpu.SemaphoreType.DMA((n,)), pltpu.SemaphoreType.DMA((n,))],
    )(*halves)


N_DEV = 8


def sum_devices(s, slots, me):
    R, Wd = s.shape

    def body(me_ref, s_ref, slots_ref, out_ref):
        acc = None
        for d in range(N_DEV):
            mine = me_ref[0] == d
            term = jnp.where(mine, s_ref[...], slots_ref[jnp.where(mine, d ^ 1, d)])
            acc = term if acc is None else acc + term
        out_ref[...] = acc

    vmem = pl.BlockSpec(memory_space=pltpu.VMEM)
    return pl.pallas_call(
        body, name="sum_devices", in_specs=[pl.BlockSpec(memory_space=pltpu.SMEM), vmem, vmem], out_specs=vmem,
        out_shape=SDS((R, Wd), F32),
    )(me, s, slots)


TRANSPOSED = ("w_in",)
BIG = {"ffn1_w_up": (D, 2 * FF, "col"), "ffn1_w_down": (FF, D, "row"), "w_in": (INW, D, "row"),
       "w_attn_up": (AW, D, "row"), "w_pool_up": (PW, D, "col"), "w_out": (D, D, "row"),
       "ffn2_w_up": (D, 2 * FF, "col"), "ffn2_w_down": (FF, D, "row")}
GROUPS = (("ffn1_w_up",), ("ffn1_w_down", "w_in", "w_attn_up", "w_pool_up", "w_out"), ("ffn2_w_up", "ffn2_w_down"))
SMALL = ("ffn1_norm", "mix_norm", "ffn2_norm", "final_norm", "pool_scale", "sinks", "pool_w_mix")
SMALL_W = 128


def _rowlike(name):
    return BIG[name][2] == "row"


def _half_dims(name):
    k, n, kind = BIG[name]
    return (k // N_CHIPS // 2, n) if kind == "row" else (k // 2, n // N_CHIPS)


def shard_view(name, shard):
    return shard.reshape((2,) + _half_dims(name))


def full_from_view(name, fv):
    k, n, _ = BIG[name]
    return fv.reshape(k, n)


def grad_view(name, g):
    kh, ns = _half_dims(name)
    return g.reshape(_full_shape(jax.ShapeDtypeStruct((2, kh, ns), g.dtype), _rowlike(name)))


def pack_small(d):
    parts = []
    for name in SMALL:
        a = d[name].reshape(-1)
        pad = (-a.shape[0]) % SMALL_W
        parts.append(jnp.pad(a, (0, pad)).reshape(-1, SMALL_W))
    a = jnp.concatenate(parts, axis=0)
    return jnp.pad(a, ((0, (-a.shape[0]) % 8), (0, 0)))


def unpack_small(a, like):
    out, r0 = {}, 0
    for name in SMALL:
        size = int(np.prod(like[name].shape))
        rows = -(-size // SMALL_W)
        out[name] = a[r0:r0 + rows].reshape(-1)[:size].reshape(like[name].shape)
        r0 += rows
    return out


WEIGHTS = ("ffn1_norm", "ffn1_w_up", "ffn1_w_down", "mix_norm", "w_in", "sinks", "w_attn_up", "pool_w_mix",
           "pool_scale", "w_pool_up", "w_out", "ffn2_norm", "ffn2_w_up", "ffn2_w_down", "final_norm")


def kernel(x, ffn1_norm, ffn1_w_up, ffn1_w_down, mix_norm, w_in, sinks, w_attn_up, pool_w_mix, pool_scale, w_pool_up, w_out, ffn2_norm, ffn2_w_up, ffn2_w_down, final_norm, loss_target, m_ffn1_norm, m_ffn1_w_up, m_ffn1_w_down, m_mix_norm, m_w_in, m_sinks, m_w_attn_up, m_pool_w_mix, m_pool_scale, m_w_pool_up, m_w_out, m_ffn2_norm, m_ffn2_w_up, m_ffn2_w_down, m_final_norm, v_ffn1_norm, v_ffn1_w_up, v_ffn1_w_down, v_mix_norm, v_w_in, v_sinks, v_w_attn_up, v_pool_w_mix, v_pool_scale, v_w_pool_up, v_w_out, v_ffn2_norm, v_ffn2_w_up, v_ffn2_w_down, v_final_norm):
    given = dict(locals())
    w = {n: given[n] for n in WEIGHTS}
    m = {n: given["m_" + n] for n in WEIGHTS}
    v = {n: given["v_" + n] for n in WEIGHTS}
    cx, cy, cc = _me()
    place = jnp.stack([cc, 2 * cx + cy]).astype(jnp.int32)

    def local2d(d, n):
        return d[n][0].T if n in TRANSPOSED else d[n][0]

    shards = {n: local2d(w, n) for n in BIG}
    grads, delta, new_m, new_v = {}, {}, {}, {}
    rowlikes = [[_rowlike(n) for n in names] for names in GROUPS]

    class Exchanges:
        def __init__(self):
            self.gathers, self.forwards, self.fulls, self.reductions, self.small = {}, {}, {}, {}, None

        def _behind_first(self, a):
            return a + self.gathers[0]["token"][0, 0]

        def _start_gather(self, group, after):
            prepare = self._behind_first if group else (lambda a: a)
            sv = [shard_view(n, prepare(shards[n]).astype(BF16)) for n in GROUPS[group]]
            rl = rowlikes[group]
            self.gathers[group] = exchange_start(sv, [_full_shape(s, r) for s, r in zip(sv, rl)], _gather_plan(rl),
                                                 after, f"gather_start_{group}")

        def weights(self, group, after):
            if not self.gathers:
                self._start_gather(0, None)
                self.packed = [self._behind_first(pack_small(d)) for d in (w, m, v)]
                after = [self.gathers[0]["token"], *self.packed]
            if group not in self.fulls:
                names, rl, state = GROUPS[group], rowlikes[group], self.gathers[group]
                if group in self.forwards:
                    _, fulls = exchange_wait(self.forwards.pop(group), _forward_plan(rl), after,
                                             f"forward_wait_{group}", FORWARDS)
                else:
                    sv, fulls = exchange_wait(state, _gather_plan(rl), state["token"] if after is None else after,
                                              f"gather_wait_{group}")
                    fulls = gather_finish(sv, fulls, rl, f"gather_finish_{group}")
                self.fulls[group] = {n: full_from_view(n, f) for n, f in zip(names, fulls)}
                if group + 1 < len(GROUPS):
                    self._start_gather(group + 1, fulls[0])
            return self.fulls[group]

        def started(self):
            return self.gathers[max(self.gathers)]["token"]

        def prefetch(self, group, after):
            rl, state = rowlikes[group], self.gathers[group]
            sv, fulls = exchange_wait(state, _gather_plan(rl), after, f"gather_wait_{group}")
            self.forwards[group] = exchange_start(sv, fulls, _forward_plan(rl), None, f"forward_start_{group}", FORWARDS)
            return self.forwards[group]["token"]

        def grads(self, g):
            names = tuple(g)
            rl = [_rowlike(n) for n in names]
            gv = [grad_view(n, g[n]) for n in names]
            state = exchange_start(gv, [_half_buffer_shape(a, r) for a, r in zip(gv, rl)], _swap_plan(rl), None,
                                   "swap_start_" + names[0], SIBLING)
            self.reductions[names] = state
            return state["token"]

        def advance(self, names, after):
            rl = [_rowlike(n) for n in names]
            gv, gots = exchange_wait(self.reductions[names], _swap_plan(rl), after, "swap_wait_" + names[0], SIBLING)
            qs = add_halves(gv, gots, rl, place, "add_halves_" + names[0])
            state = exchange_start(qs, [_slot_shape(q, r) for q, r in zip(qs, rl)], _all_to_all_plan(rl), None,
                                   "all_to_all_start_" + names[0])
            self.reductions[names] = state
            return state["token"]

        def reduce(self, names, after):
            rl = [_rowlike(n) for n in names]
            qs, recvs = exchange_wait(self.reductions.pop(names), _all_to_all_plan(rl), after,
                                      "all_to_all_wait_" + names[0])
            halves = sum_pieces(qs, recvs, rl, place, "sum_pieces_" + names[0])
            return [(shards[n], o.reshape(shards[n].shape), local2d(m, n), local2d(v, n))
                    for n, o in zip(names, join_halves(halves, "join_halves_" + names[0]))]

        def updated(self, names, outs):
            for i, n in enumerate(names):
                grads[n], delta[n], new_m[n], new_v[n] = outs[4 * i:4 * i + 4]
            return [new_v[n] for n in names]

        def update(self, names, updates, token=None):
            return self.updated(names, adamw(updates, place, "adamw_" + names[0], after=token))

        def finish(self, names, after):
            return self.update(names, self.reduce(names, after))

        def small_start(self, G, loss, after):
            packed = pack_small({n: G[n] for n in SMALL})
            used_rows = sum(-(-int(np.prod(small_like[n].shape)) // SMALL_W) for n in SMALL)
            assert packed.shape[0] > used_rows
            packed = packed.at[-1, 0].set(loss[0, 0])
            self.small = exchange_start([packed], [(N_DEV,) + packed.shape], _everyone_plan, after,
                                        "small_start", EVERYONE)
            return self.small["token"]

        def small_finish(self, after):
            (packed,), (slots,) = exchange_wait(self.small, _everyone_plan, after, "small_wait", EVERYONE)
            total = sum_devices(packed, slots, (4 * cx + 2 * cy + cc).astype(jnp.int32).reshape(1))
            self.loss = total[-1, 0]
            g, ds, ms, vs = adamw([(self.packed[0], total, self.packed[1], self.packed[2])], place, "adamw_small")
            for d, packed_d in ((grads, g), (delta, ds), (new_m, ms), (new_v, vs)):
                d.update(unpack_small(packed_d, small_like))
            return vs

    small_like = {n: w[n] for n in SMALL}
    S = {n: w[n].reshape(1, -1) for n in ("ffn1_norm", "mix_norm", "ffn2_norm", "final_norm", "pool_scale", "sinks")}
    S["pool_w_mix"] = w["pool_w_mix"][0].astype(BF16)
    exchanges = Exchanges()
    _, dx = local_fwd_bwd(x[0], loss_target[0], S, exchanges)
    loss = exchanges.loss

    def shaped(d, n):
        return (d[n].T if n in TRANSPOSED else d[n]).reshape(w[n].shape)

    return (loss, dx[None], *[shaped(grads, n) for n in WEIGHTS], *[shaped(delta, n) for n in WEIGHTS],
            *[shaped(new_m, n) for n in WEIGHTS], *[shaped(new_v, n) for n in WEIGHTS])
```

```python
import numpy as np
import jax
import jax.numpy as jnp
from jax import lax
from jax.experimental import pallas as pl
from jax.experimental.pallas import tpu as pltpu

F32 = jnp.float32
BF16 = jnp.bfloat16
SDS = jax.ShapeDtypeStruct
MESH = pl.DeviceIdType.MESH

D = 1024
FF = 2816
NQ = 16
NKV = 2
HD = 64
GQ = NQ // NKV
AW = NQ * HD
KVW = NKV * HD
BLK = 128
PW = 512
PG = 128
POOL_WINDOWS = (2, 4, 8, 16)
HALO = 16
INW = AW + 2 * KVW + PW + 2 * D
C_KV = AW
C_Z = AW + 2 * KVW
C_G = C_Z + PW
EPS = 1e-6
FF_CHUNK = 256
FF_CHUNKS = tuple((c, FF_CHUNK) for c in range(0, FF, FF_CHUNK))
SLOPES = tuple(float(2.0 ** (-8.0 * h / NQ)) for h in range(1, NQ + 1))
SCALE = HD ** -0.5

LR, B1, B2, ADAM_EPS, WD, STEP = 0.001, 0.9, 0.999, 1e-08, 0.01, 10

VMEM_LIMIT = 56 * 1024 * 1024
N_CHIPS = 4

NT = (((1,), (1,)), ((), ()))
TN = (((0,), (0,)), ((), ()))


def _cp(sem=None, vmem=VMEM_LIMIT):
    return pltpu.CompilerParams(dimension_semantics=sem, vmem_limit_bytes=vmem)


def _const_spec(shape):
    nd = len(shape)
    return pl.BlockSpec(shape, lambda *_: (0,) * nd, pipeline_mode=pl.Buffered(1))


def _rstd(x):
    return lax.rsqrt(jnp.mean(x * x, axis=-1, keepdims=True) + EPS)


def _rms_bwd(dn, xhat, rstd, g):
    dxhat = dn * g
    return rstd * (dxhat - xhat * jnp.mean(dxhat * xhat, axis=-1, keepdims=True))


def _sigmoid(x):
    return 0.5 * jnp.tanh(0.5 * x) + 0.5


def _dot(a, b):
    return jnp.dot(a, b, preferred_element_type=F32)


def _dot_nt(a, b):
    return lax.dot_general(a, b, NT, preferred_element_type=F32)


def _dot_tn(a, b):
    return lax.dot_general(a, b, TN, preferred_element_type=F32)


def _swiglu_up(x, g, wup_ref, ab_ref, hid_ref):
    n = (x * _rstd(x) * g).astype(BF16)
    for c0, w in FF_CHUNKS:
        a = _dot(n, wup_ref[:, c0:c0 + w])
        b = _dot(n, wup_ref[:, FF + c0:FF + c0 + w])
        sig = _sigmoid(a)
        s = a * sig
        ab_ref[:, c0:c0 + w] = (b * (sig + s * (1.0 - sig))).astype(BF16)
        ab_ref[:, FF + c0:FF + c0 + w] = s.astype(BF16)
        hid_ref[:, c0:c0 + w] = (s * b).astype(BF16)


def ffn_up(h, g, wup, name, after=None):
    T = h.shape[0]
    TM = 512
    tile = lambda w: pl.BlockSpec((TM, w), lambda i: (i, 0))
    token_spec, token_arg = _token_operand(after)

    def body(h_ref, g_ref, wup_ref, *rest):
        _swiglu_up(h_ref[...], g_ref[...], wup_ref, *rest[-2:])

    return pl.pallas_call(
        body, name=name, grid=(T // TM,),
        in_specs=[tile(D), _const_spec((1, D)), _const_spec((D, 2 * FF))] + token_spec,
        out_specs=[tile(2 * FF), tile(FF)],
        out_shape=[SDS((T, 2 * FF), BF16), SDS((T, FF), BF16)],
        compiler_params=_cp(("arbitrary",)),
    )(h, g, wup, *token_arg)


def ffn_down(h, hid, wdn, name, after=None):
    T = h.shape[0]
    TM = min(1024, T)
    tile = lambda w: pl.BlockSpec((TM, w), lambda i: (i, 0))
    token_spec, token_arg = _token_operand(after)

    def body(h_ref, hid_ref, wdn_ref, *rest):
        rest[-1][...] = h_ref[...] + 0.5 * _dot(hid_ref[...], wdn_ref[...])

    return pl.pallas_call(
        body, name=name, grid=(T // TM,),
        in_specs=[tile(D), tile(FF), _const_spec((FF, D))] + token_spec,
        out_specs=tile(D), out_shape=SDS((T, D), F32),
        compiler_params=_cp(("arbitrary",)),
    )(h, hid, wdn, *token_arg)


def ffn_fwd(h, g, wup, wdn, name, head=None):
    T = h.shape[0]
    TM = 512
    tile = lambda w: pl.BlockSpec((TM, w), lambda i: (i, 0))
    acc_spec = lambda w: pl.BlockSpec((1, w), lambda i: (0, 0))

    def body(h_ref, g_ref, wup_ref, wdn_ref, *rest):
        out_ref, ab_ref, hid_ref = rest[-3:] if head is None else rest[2:5]
        x = h_ref[...]
        _swiglu_up(x, g_ref[...], wup_ref, ab_ref, hid_ref)
        out = x + 0.5 * _dot(hid_ref[...], wdn_ref[...])
        if head is None:
            out_ref[...] = out
        else:
            t_ref, gf_ref, loss_ref, dgf_ref = rest[0], rest[1], rest[5], rest[6]
            out_ref[...] = _loss_head(out, t_ref[...], gf_ref[...], loss_ref, dgf_ref, pl.program_id(0) == 0)

    head_in, head_specs, head_out_specs, head_out_shape = [], [], [], []
    if head is not None:
        head_in, head_specs = list(head), [tile(D), _const_spec((1, D))]
        head_out_specs, head_out_shape = [acc_spec(1), acc_spec(D)], [SDS((1, 1), F32), SDS((1, D), F32)]
    return pl.pallas_call(
        body, name=name, grid=(T // TM,),
        in_specs=[tile(D), _const_spec((1, D)), _const_spec((D, 2 * FF)), _const_spec((FF, D))] + head_specs,
        out_specs=[tile(D), tile(2 * FF), tile(FF)] + head_out_specs,
        out_shape=[SDS((T, D), F32), SDS((T, 2 * FF), BF16), SDS((T, FF), BF16)] + head_out_shape,
        compiler_params=_cp(("arbitrary",)),
    )(h, g, wup, wdn, *head_in)


def _loss_head(x, target, g, loss_ref, dg_ref, first):
    @pl.when(first)
    def _():
        loss_ref[...] = jnp.zeros_like(loss_ref)
        dg_ref[...] = jnp.zeros_like(dg_ref)

    rstd = _rstd(x)
    xhat = x * rstd
    err = xhat * g - target
    loss_ref[...] += 0.5 * jnp.sum(jnp.mean(err * err, axis=-1, keepdims=True), axis=0, keepdims=True)
    dy = err * (1.0 / D)
    dg_ref[...] += jnp.sum(dy * xhat, axis=0, keepdims=True)
    return _rms_bwd(dy, xhat, rstd, g)


def _adamw_tile(w_ref, g_ref, m_ref, v_ref, go_ref, d_ref, nm_ref, nv_ref):
    gv = g_ref[...]
    go_ref[...] = gv
    nm = B1 * m_ref[...] + (1.0 - B1) * gv
    nv = B2 * v_ref[...] + (1.0 - B2) * (gv * gv)
    nm_ref[...] = nm
    nv_ref[...] = nv
    d_ref[...] = -LR * ((nm / (1.0 - B1 ** STEP)) / (jnp.sqrt(nv / (1.0 - B2 ** STEP)) + ADAM_EPS) + WD * w_ref[...])


def _riders(updates, steps):
    in_specs, out_specs, out_shapes, operands, tiles = [], [], [], [], []
    for w, g, m, v in updates:
        R, C = w.shape
        n = max(d for d in range(1, steps + 1) if R % d == 0 and (R // d) % 8 == 0)
        spec = pl.BlockSpec((R // n, C), lambda i, n=n: (jnp.minimum(i, n - 1), 0))
        in_specs += [spec] * 4
        out_specs += [spec] * 4
        out_shapes += [SDS((R, C), F32)] * 4
        operands += [w, g, m, v]
        tiles.append(n)

    def run(step, in_refs, out_refs):
        for u, n in enumerate(tiles):
            @pl.when(step < n)
            def _(u=u):
                _adamw_tile(*in_refs[4 * u:4 * u + 4], *out_refs[4 * u:4 * u + 4])

    return in_specs, out_specs, out_shapes, operands, run


def ffn_bwd_x(dh, h_in, ab, g, wup, wdn, name, updates=(), after=None):
    T = dh.shape[0]
    TM = 256 if updates else 512
    SUB = 256
    r_in, r_out, r_shapes, r_args, ride = _riders(updates, T // TM)
    token_spec, token_arg = _token_operand(after)
    n_in = len(r_in) + len(token_spec)

    def body(dh_ref, h_ref, ab_ref, g_ref, wup_ref, wdn_ref, *rest):
        dhin_ref, dup_ref, n_ref, dg_ref = rest[n_in:n_in + 4]
        ride(pl.program_id(0), rest[:len(r_in)], rest[n_in + 4:])
        g = g_ref[...]
        dg = None
        for r0 in range(0, TM, SUB):
            rows = slice(r0, r0 + SUB)
            x = h_ref[rows, :]
            rstd = _rstd(x)
            xhat = x * rstd
            n_ref[rows, :] = (xhat * g).astype(BF16)
            dh = dh_ref[rows, :]
            dhh = (0.5 * dh).astype(BF16)
            for c0, w in FF_CHUNKS:
                dhid = _dot_nt(dhh, wdn_ref[c0:c0 + w, :]).astype(BF16)
                dup_ref[rows, c0:c0 + w] = dhid * ab_ref[rows, c0:c0 + w]
                dup_ref[rows, FF + c0:FF + c0 + w] = dhid * ab_ref[rows, FF + c0:FF + c0 + w]
            dn = _dot_nt(dup_ref[rows, :], wup_ref[...])
            dhin_ref[rows, :] = dh + _rms_bwd(dn, xhat, rstd, g)
            part = jnp.sum(dn * xhat, axis=0, keepdims=True)
            dg = part if dg is None else dg + part

        @pl.when(pl.program_id(0) == 0)
        def _():
            dg_ref[...] = jnp.zeros_like(dg_ref)

        dg_ref[...] += dg

    tile = lambda w: pl.BlockSpec((TM, w), lambda i: (i, 0))
    return pl.pallas_call(
        body, name=name, grid=(T // TM,),
        in_specs=[tile(D), tile(D), tile(2 * FF), _const_spec((1, D)), _const_spec((D, 2 * FF)), _const_spec((FF, D))]
        + r_in + token_spec,
        out_specs=[tile(D), tile(2 * FF), tile(D), pl.BlockSpec((1, D), lambda i: (0, 0))] + r_out,
        out_shape=[SDS((T, D), F32), SDS((T, 2 * FF), BF16), SDS((T, D), BF16), SDS((1, D), F32)] + r_shapes,
        compiler_params=_cp(("arbitrary",)),
    )(dh, h_in, ab, g, wup, wdn, *r_args, *token_arg)


TOKEN_SPEC = pl.BlockSpec((8, 128), lambda *_: (0, 0))


def _token_operand(after):
    tokens = [t for t in (after if isinstance(after, tuple) else (after,)) if t is not None]
    return [TOKEN_SPEC] * len(tokens), tokens


def matmul_tn(a, b, name, *, tm, tn, tt=1024, b_scale=None, after=None):
    T, M = a.shape
    N = b.shape[1]
    tt = min(tt, T)
    assert M % tm == 0 and N % tn == 0 and T % tt == 0
    nt = T // tt
    token_spec, token_arg = _token_operand(after)

    def body(a_ref, b_ref, *rest):
        o_ref, acc_ref = rest[-2:]
        t = pl.program_id(2)

        @pl.when(t == 0)
        def _():
            acc_ref[...] = jnp.zeros_like(acc_ref)

        bv = b_ref[...]
        if b_scale is not None:
            bv = bv * b_scale
        acc_ref[...] += _dot_tn(a_ref[...].astype(BF16), bv.astype(BF16))

        @pl.when(t == nt - 1)
        def _():
            o_ref[...] = acc_ref[...].astype(BF16)

    return pl.pallas_call(
        body, name=name, grid=(M // tm, N // tn, nt),
        in_specs=[pl.BlockSpec((tt, tm), lambda i, j, t: (t, i)), pl.BlockSpec((tt, tn), lambda i, j, t: (t, j))]
        + token_spec,
        out_specs=pl.BlockSpec((tm, tn), lambda i, j, t: (i, j)),
        out_shape=SDS((M, N), BF16),
        scratch_shapes=[pltpu.VMEM((tm, tn), F32)],
        compiler_params=_cp(("parallel", "parallel", "arbitrary")),
    )(a, b, *token_arg)


def mix_in_fwd(h1, g, win_t):
    T = h1.shape[0]
    TM = 512

    def body(h_ref, g_ref, w_ref, u_ref, q_ref, kv_ref, z_ref, gt_ref):
        x = h_ref[...]
        u = (x * _rstd(x) * g_ref[...]).astype(BF16)
        u_ref[...] = u
        for c in range(0, AW, 256):
            q_ref[:, c:c + 256] = _dot_nt(u, w_ref[c:c + 256, :]).astype(BF16)
        kv_ref[...] = _dot_nt(u, w_ref[C_KV:C_Z, :]).astype(BF16)
        for c in range(0, PW, 256):
            z_ref[:, c:c + 256] = _dot_nt(u, w_ref[C_Z + c:C_Z + c + 256, :])
        for c in range(0, 2 * D, 256):
            gt_ref[:, c:c + 256] = _dot_nt(u, w_ref[C_G + c:C_G + c + 256, :]).astype(BF16)

    tile = lambda w: pl.BlockSpec((TM, w), lambda i: (i, 0))
    return pl.pallas_call(
        body, name="mix_in_fwd", grid=(T // TM,),
        in_specs=[tile(D), _const_spec((1, D)), _const_spec((INW, D))],
        out_specs=[tile(D), tile(AW), tile(2 * KVW), tile(PW), tile(2 * D)],
        out_shape=[SDS((T, D), BF16), SDS((T, AW), BF16), SDS((T, 2 * KVW), BF16), SDS((T, PW), F32),
                   SDS((T, 2 * D), BF16)],
        compiler_params=_cp(("arbitrary",)),
    )(h1, g, win_t)


def mix_in_bwd(dq, dkv, dz, dgt, dh2, h1, g, win_t):
    T = h1.shape[0]
    TM = min(1024, T)
    SUB = 256

    def body(dq_ref, dkv_ref, dz_ref, dgt_ref, dh2_ref, h_ref, g_ref, w_ref, dh1_ref, dg_ref):
        g = g_ref[...]
        dg = None
        for r0 in range(0, TM, SUB):
            rows = slice(r0, r0 + SUB)
            du = _dot(dq_ref[rows, :], w_ref[0:AW, :])
            du += _dot(dkv_ref[rows, :], w_ref[C_KV:C_Z, :])
            du += _dot(dz_ref[rows, :], w_ref[C_Z:C_G, :])
            du += _dot(dgt_ref[rows, :], w_ref[C_G:INW, :])
            x = h_ref[rows, :]
            rstd = _rstd(x)
            xhat = x * rstd
            dh1_ref[rows, :] = dh2_ref[rows, :] + _rms_bwd(du, xhat, rstd, g)
            part = jnp.sum(du * xhat, axis=0, keepdims=True)
            dg = part if dg is None else dg + part

        @pl.when(pl.program_id(0) == 0)
        def _():
            dg_ref[...] = jnp.zeros_like(dg_ref)

        dg_ref[...] += dg

    tile = lambda w: pl.BlockSpec((TM, w), lambda i: (i, 0))
    return pl.pallas_call(
        body, name="mix_in_bwd", grid=(T // TM,),
        in_specs=[tile(AW), tile(2 * KVW), tile(PW), tile(2 * D), tile(D), tile(D), _const_spec((1, D)),
                  _const_spec((INW, D))],
        out_specs=[tile(D), pl.BlockSpec((1, D), lambda i: (0, 0))],
        out_shape=[SDS((T, D), F32), SDS((1, D), F32)],
        compiler_params=_cp(("arbitrary",)),
    )(dq, dkv, dz, dgt, dh2, h1, g, win_t)


PAIR = 2 * HD
NPAIR = GQ // 2


def _lo_lanes():
    return lax.broadcasted_iota(jnp.int32, (BLK, PAIR), 1) < HD


def _stack_heads(ref, kvh, scale=None):
    lo = _lo_lanes()
    parts = []
    for pr in range(NPAIR):
        t = ref[:, (kvh * NPAIR + pr) * PAIR:(kvh * NPAIR + pr + 1) * PAIR]
        if scale is not None:
            t = t * scale
        zero = jnp.zeros_like(t)
        parts += [jnp.where(lo, t, zero), jnp.where(lo, zero, t)]
    return jnp.concatenate(parts, axis=0)


def _kv_tiles(kvc_ref, kvp_ref, tile, kvh):
    lo = _lo_lanes()
    dup, left, right = [], [], []
    for ref in (kvp_ref, kvc_ref):
        t = ref[:, tile * PAIR:(tile + 1) * PAIR]
        r = pltpu.roll(t.astype(F32), HD, 1).astype(BF16)
        zero = jnp.zeros_like(t)
        a, b = (t, r) if kvh == 0 else (r, t)
        dup.append(jnp.where(lo, a, b))
        left.append(jnp.where(lo, a, zero))
        right.append(jnp.where(lo, zero, b))
    cat = lambda xs: jnp.concatenate(xs, axis=0)
    return cat(dup), cat(left), cat(right)


def _band_consts(first):
    row = lax.broadcasted_iota(jnp.int32, (BLK, BLK), 0)
    col = lax.broadcasted_iota(jnp.int32, (BLK, BLK), 1)
    upper = col > row
    dist = jnp.where(upper, row - col + BLK, row - col).astype(F32)
    pen = jnp.where(jnp.logical_and(upper, first), -jnp.inf, 0.0)
    return upper, dist, pen


def _split_band(upper, t):
    zero = jnp.zeros_like(t)
    return jnp.concatenate([jnp.where(upper, t, zero), jnp.where(upper, zero, t)], axis=1)


def attn_fwd(q, kv, sinks):
    T = q.shape[0]
    nb = T // BLK

    def body(sink_ref, q_ref, kvc_ref, kvp_ref, att_ref, lse_ref):
        upper, dist, pen = _band_consts(pl.program_id(0) == 0)
        scores, values = [], []
        for kvh in range(NKV):
            kdup, _, _ = _kv_tiles(kvc_ref, kvp_ref, 0, kvh)
            values.append(_kv_tiles(kvc_ref, kvp_ref, 1, kvh)[1:])
            scores.append(_dot_nt(_stack_heads(q_ref, kvh, SCALE), kdup))
        for kvh in range(NKV):
            s_all = scores[kvh]
            vleft, vright = values[kvh]
            for pr in range(NPAIR):
                outs, inv = [], []
                for side, vpad in ((0, vleft), (1, vright)):
                    g = 2 * pr + side
                    hq = kvh * GQ + g
                    sink = sink_ref[0, hq]
                    rows = slice(g * BLK, (g + 1) * BLK)
                    s = jnp.where(upper, s_all[rows, 0:BLK], s_all[rows, BLK:2 * BLK]) - SLOPES[hq] * dist + pen
                    m = jnp.maximum(jnp.max(s, axis=-1, keepdims=True), sink)
                    p = jnp.exp(s - m)
                    l = jnp.sum(p, axis=-1, keepdims=True) + jnp.exp(sink - m)
                    lse_ref[:, hq:hq + 1] = m + jnp.log(l)
                    outs.append(_dot(_split_band(upper, p.astype(BF16)), vpad))
                    inv.append(1.0 / l)
                col0 = (kvh * NPAIR + pr) * PAIR
                att_ref[:, col0:col0 + PAIR] = ((outs[0] + outs[1]) * jnp.where(_lo_lanes(), inv[0], inv[1])).astype(BF16)

    return pl.pallas_call(
        body, name="attn_fwd", grid=(nb,),
        in_specs=[pl.BlockSpec(memory_space=pltpu.SMEM),
                  pl.BlockSpec((BLK, AW), lambda i: (i, 0)),
                  pl.BlockSpec((BLK, 2 * KVW), lambda i: (i, 0)),
                  pl.BlockSpec((BLK, 2 * KVW), lambda i: (jnp.maximum(i - 1, 0), 0))],
        out_specs=[pl.BlockSpec((BLK, AW), lambda i: (i, 0)), pl.BlockSpec((BLK, NQ), lambda i: (i, 0))],
        out_shape=[SDS((T, AW), BF16), SDS((T, NQ), F32)],
        compiler_params=_cp(("arbitrary",)),
    )(sinks, q, kv, kv)


def attn_bwd(q, kv, datt, att, lse, sinks, after=None):
    T = q.shape[0]
    nb = T // BLK
    token_spec, token_arg = _token_operand(after)

    def body(sink_ref, q_ref, kvc_ref, kvp_ref, do_ref, out_ref, lse_ref, *rest):
        dq_ref, dkv_ref, dsink_ref, carry_ref = rest[-4:]
        i = pl.program_id(0)

        @pl.when(i == 0)
        def _():
            dsink_ref[...] = jnp.zeros_like(dsink_ref)
            carry_ref[...] = jnp.zeros_like(carry_ref)

        @pl.when(i < nb)
        def _():
            upper, dist, pen = _band_consts(i == 0)
            lo = _lo_lanes()
            dk_dup, dv_dup = [], []
            staged = []
            for kvh in range(NKV):
                kdup, kleft, kright = _kv_tiles(kvc_ref, kvp_ref, 0, kvh)
                vdup, _, _ = _kv_tiles(kvc_ref, kvp_ref, 1, kvh)
                qs = _stack_heads(q_ref, kvh, SCALE)
                dos = _stack_heads(do_ref, kvh)
                staged.append((kleft, kright, qs, dos, _dot_nt(qs, kdup), _dot_nt(dos, vdup)))
            deltas = []
            for pair in range(NQ // 2):
                cols = slice(pair * PAIR, (pair + 1) * PAIR)
                t = do_ref[:, cols].astype(F32) * out_ref[:, cols].astype(F32)
                deltas += [jnp.sum(jnp.where(lo, t, 0.0), axis=-1, keepdims=True),
                           jnp.sum(jnp.where(lo, 0.0, t), axis=-1, keepdims=True)]
            for kvh in range(NKV):
                kleft, kright, qs, dos, s_all, dp_all = staged[kvh]
                ds_parts, p_parts = [], []
                for pr in range(NPAIR):
                    dq = None
                    for side, kpad in ((0, kleft), (1, kright)):
                        g = 2 * pr + side
                        hq = kvh * GQ + g
                        lse_h = lse_ref[:, hq:hq + 1]
                        rows = slice(g * BLK, (g + 1) * BLK)
                        s = jnp.where(upper, s_all[rows, 0:BLK], s_all[rows, BLK:2 * BLK]) - SLOPES[hq] * dist + pen
                        p = jnp.exp(s - lse_h)
                        dp = jnp.where(upper, dp_all[rows, 0:BLK], dp_all[rows, BLK:2 * BLK])
                        delta = deltas[hq]
                        dsink_ref[:, hq:hq + 1] += -jnp.sum(jnp.exp(sink_ref[0, hq] - lse_h) * delta, axis=0,
                                                            keepdims=True)
                        ds = _split_band(upper, (p * (dp - delta)).astype(BF16))
                        ds_parts.append(ds)
                        p_parts.append(_split_band(upper, p.astype(BF16)))
                        d = _dot(ds, kpad)
                        dq = d if dq is None else dq + d
                    col0 = (kvh * NPAIR + pr) * PAIR
                    dq_ref[:, col0:col0 + PAIR] = (dq * SCALE).astype(BF16)
                dkw = _dot_tn(qs, jnp.concatenate(ds_parts, axis=0)).T
                dvw = _dot_tn(dos, jnp.concatenate(p_parts, axis=0)).T
                dk_dup.append(dkw + pltpu.roll(dkw, HD, 1))
                dv_dup.append(dvw + pltpu.roll(dvw, HD, 1))
            dk = jnp.where(jnp.concatenate([lo, lo], axis=0), dk_dup[0], dk_dup[1])
            dv = jnp.where(jnp.concatenate([lo, lo], axis=0), dv_dup[0], dv_dup[1])
            dkv_ref[:, 0:PAIR] = (carry_ref[:, 0:PAIR] + dk[0:BLK]).astype(BF16)
            dkv_ref[:, PAIR:2 * PAIR] = (carry_ref[:, PAIR:2 * PAIR] + dv[0:BLK]).astype(BF16)
            carry_ref[:, 0:PAIR] = dk[BLK:2 * BLK]
            carry_ref[:, PAIR:2 * PAIR] = dv[BLK:2 * BLK]

        @pl.when(i == nb)
        def _():
            dkv_ref[...] = carry_ref[...].astype(BF16)

    cur = lambda i: (jnp.minimum(i, nb - 1), 0)
    prev = lambda i: (jnp.maximum(jnp.minimum(i, nb - 1) - 1, 0), 0)
    return pl.pallas_call(
        body, name="attn_bwd", grid=(nb + 1,),
        in_specs=[pl.BlockSpec(memory_space=pltpu.SMEM),
                  pl.BlockSpec((BLK, AW), cur), pl.BlockSpec((BLK, 2 * KVW), cur), pl.BlockSpec((BLK, 2 * KVW), prev),
                  pl.BlockSpec((BLK, AW), cur), pl.BlockSpec((BLK, AW), cur), pl.BlockSpec((BLK, NQ), cur)] + token_spec,
        out_specs=[pl.BlockSpec((BLK, AW), cur),
                   pl.BlockSpec((BLK, 2 * KVW), lambda i: (jnp.maximum(i - 1, 0), 0)),
                   pl.BlockSpec((1, NQ), lambda i: (0, 0))],
        out_shape=[SDS((T, AW), BF16), SDS((T, 2 * KVW), BF16), SDS((1, NQ), F32)],
        scratch_shapes=[pltpu.VMEM((BLK, 2 * KVW), F32)],
        compiler_params=_cp(("arbitrary",)),
    )(sinks, q, kv, kv, datt, att, lse, *token_arg)


def _inv_counts(t0, rows):
    t = (t0 + lax.broadcasted_iota(jnp.int32, (rows, 1), 0) + 1).astype(F32)
    return [1.0 / jnp.minimum(t, float(w)) for w in POOL_WINDOWS]


def pool_fwd(z, wmix, scale, after=None):
    T = z.shape[0]
    TM = min(1024, T)
    L = TM + HALO
    token_spec, token_arg = _token_operand(after)

    def body(z_ref, halo_ref, wmix_ref, scale_ref, *rest):
        pooled_ref, mixs_ref = rest[-2:]
        i = pl.program_id(0)
        halo = jnp.where(i > 0, halo_ref[...], 0.0)
        zt = z_ref[...]
        e = jnp.concatenate([halo, zt], axis=0)
        sums = []
        s = e
        for k in (1, 2, 4, 8):
            s = s + pltpu.roll(s, k, 0)
            sums.append(s)
        inv = _inv_counts(i * TM, TM)
        for gi in range(len(POOL_WINDOWS)):
            cols = slice(gi * PG, (gi + 1) * PG)
            pooled = (sums[gi][HALO:, cols] * inv[gi] - zt[:, cols]).astype(BF16)
            pooled_ref[:, cols] = pooled
            mixs_ref[:, cols] = (_dot(pooled, wmix_ref[gi]) * scale_ref[:, cols]).astype(BF16)

    return pl.pallas_call(
        body, name="pool_fwd", grid=(T // TM,),
        in_specs=[pl.BlockSpec((TM, PW), lambda i: (i, 0)),
                  pl.BlockSpec((HALO, PW), lambda i: (jnp.maximum(i * (TM // HALO) - 1, 0), 0)),
                  _const_spec((len(POOL_WINDOWS), PG, PG)), _const_spec((1, PW))] + token_spec,
        out_specs=[pl.BlockSpec((TM, PW), lambda i: (i, 0)), pl.BlockSpec((TM, PW), lambda i: (i, 0))],
        out_shape=[SDS((T, PW), BF16), SDS((T, PW), BF16)],
        compiler_params=_cp(("arbitrary",)),
    )(z, z, wmix, scale, *token_arg)


def pool_bwd(dmixs, pooled, wmix, scale, after=None):
    T = dmixs.shape[0]
    TM = min(1024, T)
    L = TM + HALO
    nt = T // TM
    token_spec, token_arg = _token_operand(after)

    def body(dm_ref, halo_ref, pooled_ref, wmix_ref, scale_ref, *rest):
        dz_ref, dwmix_ref, dscale_ref = rest[-3:]
        i = pl.program_id(0)

        @pl.when(i == 0)
        def _():
            dwmix_ref[...] = jnp.zeros_like(dwmix_ref)
            dscale_ref[...] = jnp.zeros_like(dscale_ref)

        halo = jnp.where(i < nt - 1, halo_ref[...], 0.0)
        dm = dm_ref[...]
        e = jnp.concatenate([dm, halo], axis=0)
        inv = _inv_counts(i * TM, L)
        for gi in range(len(POOL_WINDOWS)):
            cols = slice(gi * PG, (gi + 1) * PG)
            w = wmix_ref[gi]
            dmixed = (e[:, cols] * scale_ref[:, cols]).astype(BF16)
            dpooled = _dot_nt(dmixed, w)
            pooled = pooled_ref[:, cols]
            mixed = _dot(pooled, w)
            dscale_ref[:, cols] += jnp.sum(dm[:, cols] * mixed, axis=0, keepdims=True)
            dwmix_ref[gi] += _dot_tn(pooled, dmixed[:TM])
            s = dpooled * inv[gi]
            k = 1
            while k < POOL_WINDOWS[gi]:
                s = s + pltpu.roll(s, L - k, 0)
                k *= 2
            dz_ref[:, cols] = (s[:TM] - dpooled[:TM]).astype(BF16)

    return pl.pallas_call(
        body, name="pool_bwd", grid=(nt,),
        in_specs=[pl.BlockSpec((TM, PW), lambda i: (i, 0)),
                  pl.BlockSpec((HALO, PW), lambda i: (jnp.minimum((i + 1) * (TM // HALO), T // HALO - 1), 0)),
                  pl.BlockSpec((TM, PW), lambda i: (i, 0)),
                  _const_spec((len(POOL_WINDOWS), PG, PG)), _const_spec((1, PW))] + token_spec,
        out_specs=[pl.BlockSpec((TM, PW), lambda i: (i, 0)),
                   pl.BlockSpec((len(POOL_WINDOWS), PG, PG), lambda i: (0, 0, 0)),
                   pl.BlockSpec((1, PW), lambda i: (0, 0))],
        out_shape=[SDS((T, PW), BF16), SDS((len(POOL_WINDOWS), PG, PG), F32), SDS((1, PW), F32)],
        compiler_params=_cp(("arbitrary",)),
    )(dmixs, dmixs, pooled, wmix, scale, *token_arg)


def merge_fwd(att, mixs, gt, h1, wattn, wpool, wout):
    T = h1.shape[0]
    TM = 512

    def body(att_ref, mixs_ref, gt_ref, h_ref, wa_ref, wp_ref, wo_ref, h2_ref, mg_ref, gf_ref):
        a = _dot(att_ref[...], wa_ref[...])
        p = _dot(mixs_ref[...], wp_ref[...])
        sa = _sigmoid(gt_ref[:, 0:D].astype(F32))
        sp = _sigmoid(gt_ref[:, D:2 * D].astype(F32))
        gf_ref[:, 0:D] = (a * sa * (1.0 - sa)).astype(BF16)
        gf_ref[:, D:2 * D] = (p * sp * (1.0 - sp)).astype(BF16)
        mg = (sa * a + sp * p).astype(BF16)
        mg_ref[...] = mg
        h2_ref[...] = h_ref[...] + _dot(mg, wo_ref[...])

    tile = lambda w: pl.BlockSpec((TM, w), lambda i: (i, 0))
    return pl.pallas_call(
        body, name="merge_fwd", grid=(T // TM,),
        in_specs=[tile(AW), tile(PW), tile(2 * D), tile(D), _const_spec((AW, D)), _const_spec((PW, D)),
                  _const_spec((D, D))],
        out_specs=[tile(D), tile(D), tile(2 * D)],
        out_shape=[SDS((T, D), F32), SDS((T, D), BF16), SDS((T, 2 * D), BF16)],
        compiler_params=_cp(("arbitrary",)),
    )(att, mixs, gt, h1, wattn, wpool, wout)


def merge_bwd(dh2, gt, gf, wattn, wpool, wout, after=None):
    T = dh2.shape[0]
    TM = 512
    token_spec, token_arg = _token_operand(after)

    def body(dh2_ref, gt_ref, gf_ref, wa_ref, wp_ref, wo_ref, *rest):
        datt_ref, dmixs_ref, dgt_ref, da_ref, dp_ref = rest[-5:]
        dm = _dot_nt(dh2_ref[...].astype(BF16), wo_ref[...])
        da = (dm * _sigmoid(gt_ref[:, 0:D].astype(F32))).astype(BF16)
        dp = (dm * _sigmoid(gt_ref[:, D:2 * D].astype(F32))).astype(BF16)
        da_ref[...] = da
        dp_ref[...] = dp
        dgt_ref[:, 0:D] = (dm * gf_ref[:, 0:D].astype(F32)).astype(BF16)
        dgt_ref[:, D:2 * D] = (dm * gf_ref[:, D:2 * D].astype(F32)).astype(BF16)
        datt_ref[...] = _dot_nt(da, wa_ref[...]).astype(BF16)
        dmixs_ref[...] = _dot_nt(dp, wp_ref[...])

    tile = lambda w: pl.BlockSpec((TM, w), lambda i: (i, 0))
    return pl.pallas_call(
        body, name="merge_bwd", grid=(T // TM,),
        in_specs=[tile(D), tile(2 * D), tile(2 * D), _const_spec((AW, D)), _const_spec((PW, D)),
                  _const_spec((D, D))] + token_spec,
        out_specs=[tile(AW), tile(PW), tile(2 * D), tile(D), tile(D)],
        out_shape=[SDS((T, AW), BF16), SDS((T, PW), F32), SDS((T, 2 * D), BF16), SDS((T, D), BF16),
                   SDS((T, D), BF16)],
        compiler_params=_cp(("arbitrary",)),
    )(dh2, gt, gf, wattn, wpool, wout, *token_arg)


def adamw(updates, place, name, after=None):
    tile_bytes = 2 * 1024 * 1024 // len(updates)
    jobs = []
    for w, g, m, v in updates:
        R, C = w.shape
        tr = R
        if R * C * 4 > tile_bytes:
            tr = next(cand for cand in (512, 256, 128, 64, 32, 16, 8) if R % cand == 0 and cand * C * 4 <= tile_bytes)
        blk = ((tr, C), lambda ls, p: (ls, 0))
        jobs.append((R // tr, [(a, *blk) for a in (w, g, m, v)], [((R, C), F32, *blk)] * 4,
                     lambda ins, outs: _adamw_tile(*ins, *outs)))
    return _run_jobs(jobs, place, name, after)


GROUP_UP1, GROUP_MIX, GROUP_FFN2 = 0, 1, 2


def local_fwd_bwd(x, target, S, comm):
    w_up1 = comm.weights(GROUP_UP1, None)["ffn1_w_up"]
    ab1, hid1 = ffn_up(x, S["ffn1_norm"], w_up1, "ffn1_up", after=comm.started())
    W = comm.weights(GROUP_MIX, hid1)
    h1 = ffn_down(x, hid1, W["ffn1_w_down"], "ffn1_down", after=comm.started())
    u, q, kv, z, gt = mix_in_fwd(h1, S["mix_norm"], W["w_in"])
    att, lse = attn_fwd(q, kv, S["sinks"])
    pooled, mixs = pool_fwd(z, S["pool_w_mix"], S["pool_scale"], after=comm.prefetch(GROUP_FFN2, att))
    h2, merged, gf = merge_fwd(att, mixs, gt, h1, W["w_attn_up"], W["w_pool_up"], W["w_out"])
    W = comm.weights(GROUP_FFN2, h2)
    dh3, ab2, hid2, loss, g_final = ffn_fwd(h2, S["ffn2_norm"], W["ffn2_w_up"], W["ffn2_w_down"], "ffn2_fwd",
                                            head=(target, S["final_norm"]))

    G = {"final_norm": g_final}
    dh2, dup2, n2, G["ffn2_norm"] = ffn_bwd_x(dh3, h2, ab2, S["ffn2_norm"], W["ffn2_w_up"], W["ffn2_w_down"],
                                              "ffn2_bwd_x")
    ffn2 = ("ffn2_w_down", "ffn2_w_up")
    token = comm.grads({"ffn2_w_down": matmul_tn(hid2, dh3, "ffn2_dw_down", tm=1408, tn=512, tt=4096, b_scale=0.5),
                        "ffn2_w_up": matmul_tn(n2, dup2, "ffn2_dw_up", tm=D, tn=512, tt=4096)})

    W = comm.weights(GROUP_MIX, None)
    datt, dmixs, dgt, da, dp = merge_bwd(dh2, gt, gf, W["w_attn_up"], W["w_pool_up"], W["w_out"], after=token)
    token = comm.advance(ffn2, datt)
    g_mix = {"w_out": matmul_tn(merged, dh2, "dw_out", tm=D, tn=D, after=token),
             "w_attn_up": matmul_tn(att, da, "dw_attn_up", tm=AW, tn=D),
             "w_pool_up": matmul_tn(mixs, dp, "dw_pool_up", tm=PW, tn=D)}
    dz, G["pool_w_mix"], G["pool_scale"] = pool_bwd(dmixs, pooled, S["pool_w_mix"], S["pool_scale"], after=token)
    dq, dkv, G["sinks"] = attn_bwd(q, kv, datt, att, lse, S["sinks"], after=token)
    dh1, G["mix_norm"] = mix_in_bwd(dq, dkv, dz, dgt, dh2, h1, S["mix_norm"], W["w_in"])
    g_mix["w_in"] = jnp.concatenate([
        matmul_tn(dq, u, "dw_in_q", tm=AW, tn=D),
        matmul_tn(dkv, u, "dw_in_kv", tm=2 * KVW, tn=D),
        matmul_tn(dz, u, "dw_in_z", tm=PW, tn=D),
        matmul_tn(dgt, u, "dw_in_g", tm=D, tn=512, tt=4096),
    ], axis=0)
    mix = tuple(g_mix)
    token = comm.grads(g_mix)

    g_dn1 = matmul_tn(hid1, dh1, "ffn1_dw_down", tm=1408, tn=512, tt=4096, b_scale=0.5, after=token)
    token = comm.advance(mix, g_dn1)
    token2 = comm.grads({"ffn1_w_down": g_dn1})
    dx, dup1, n1, G["ffn1_norm"], *outs = ffn_bwd_x(dh1, x, ab1, S["ffn1_norm"], w_up1, W["ffn1_w_down"], "ffn1_bwd_x",
                                                   updates=comm.reduce(ffn2, token2), after=(token, token2))
    comm.updated(ffn2, outs)
    token = comm.advance(("ffn1_w_down",), dx)
    token2 = comm.small_start(G, loss, dx)
    g_up1 = matmul_tn(n1, dup1, "ffn1_dw_up", tm=D, tn=512, tt=4096, after=(token, token2))
    token = comm.grads({"ffn1_w_up": g_up1})
    updates = comm.reduce(("ffn1_w_down",), token)
    token = comm.advance(("ffn1_w_up",), [u[1] for u in updates])
    done = comm.update(("ffn1_w_down",), updates, token)
    done = comm.finish(mix, done)
    done = comm.small_finish(done)
    comm.finish(("ffn1_w_up",), done)
    return loss, dx


HBM_SPEC = pl.BlockSpec(memory_space=pltpu.HBM)


def _me():
    return lax.axis_index("x"), lax.axis_index("y"), lax.axis_index("c")


def _peer_chip(x, y, k):
    return x ^ (k >> 1), y ^ (k & 1)


def _piece_half(ref, rowlike, j, h):
    if rowlike:
        return ref.at[j, h]
    ns = ref.shape[-1] // N_CHIPS
    return ref.at[h, :, pl.ds(pl.multiple_of(j * ns, 128), ns)]


def _piece(ref, rowlike, j):
    if rowlike:
        return ref.at[j]
    ns = ref.shape[-1] // N_CHIPS
    return ref.at[:, :, pl.ds(pl.multiple_of(j * ns, 128), ns)]


def _full_shape(shard_view, rowlike):
    _, kh, ns = shard_view.shape
    return (N_CHIPS, 2, kh, ns) if rowlike else (2, kh, N_CHIPS * ns)


def _remote(src, dst, send_sem, recv_sem, dev):
    return pltpu.make_async_remote_copy(src, dst, send_sem, recv_sem, device_id=dev, device_id_type=MESH)


SEM_SPEC = pl.BlockSpec(memory_space=pltpu.SEMAPHORE)
SPLIT_PARAMS = pltpu.CompilerParams(has_side_effects=pltpu.SideEffectType.DATAFLOW_SIDE_EFFECTING)


def _gather_plan(rowlikes):
    def plan(s_refs, f_refs, a, k, x, y, c):
        px, py = _peer_chip(x, y, k)
        return (s_refs[a].at[c], _piece_half(f_refs[a], rowlikes[a], 2 * x + y, c),
                _piece_half(f_refs[a], rowlikes[a], 2 * px + py, c), (px, py, c))
    return plan


def _all_to_all_plan(rowlikes):
    def plan(q_refs, r_refs, a, k, x, y, c):
        px, py = _peer_chip(x, y, k)
        if rowlikes[a]:
            src = q_refs[a].at[2 * px + py]
        else:
            ns = q_refs[a].shape[-1] // N_CHIPS
            src = q_refs[a].at[:, pl.ds(pl.multiple_of((2 * px + py) * ns, 128), ns)]
        return src, r_refs[a].at[k - 1], r_refs[a].at[k - 1], (px, py, c)
    return plan


def _swap_plan(rowlikes):
    def plan(g_refs, got_refs, a, k, x, y, c):
        src = g_refs[a].at[:, 1 - c] if rowlikes[a] else g_refs[a].at[1 - c]
        return src, got_refs[a], got_refs[a], (x, y, 1 - c)
    return plan


def _everyone_plan(s_refs, slot_refs, a, k, x, y, c):
    px, py, pc = x ^ (k >> 2), y ^ ((k >> 1) & 1), c ^ (k & 1)
    return s_refs[a], slot_refs[a].at[4 * x + 2 * y + c], slot_refs[a].at[4 * px + 2 * py + pc], (px, py, pc)


def _forward_plan(rowlikes):
    def plan(s_refs, f_refs, a, k, x, y, c):
        sib = (x, y, 1 - c)
        if k == 0:
            own = _piece(f_refs[a], rowlikes[a], 2 * x + y)
            return s_refs[a], own, own, sib
        px, py = _peer_chip(x, y, k)
        mine = _piece_half(f_refs[a], rowlikes[a], 2 * px + py, c)
        return mine, mine, _piece_half(f_refs[a], rowlikes[a], 2 * px + py, 1 - c), sib
    return plan


CHIPS, SIBLING, EVERYONE, FORWARDS = (1, 2, 3), (1,), tuple(range(1, 8)), (0, 1, 2, 3)


def _as_list(after):
    return [] if after is None else list(after) if isinstance(after, (list, tuple)) else [after]


def exchange_start(srcs, land_shapes, plan, after, name, peers=CHIPS):
    n = len(srcs)
    lands = [l if hasattr(l, "dtype") else lax.empty(l, s.dtype) for l, s in zip(land_shapes, srcs)]

    behind = _as_list(after)

    def body(*refs):
        s_refs, l_refs = refs[:n], refs[n:2 * n]
        send_sems, recv_sems = refs[2 * n + len(behind)], refs[2 * n + len(behind) + 1]
        token = refs[-1]
        x, y, c = _me()
        for a in range(n):
            for i, k in enumerate(peers):
                src, dst, _, peer = plan(s_refs, l_refs, a, k, x, y, c)
                sem = len(peers) * a + i
                _remote(src, dst, send_sems.at[sem], recv_sems.at[sem], peer).start()
        token[...] = jnp.zeros_like(token)

    n_sems = len(peers) * n
    outs = pl.pallas_call(
        body, name=name, in_specs=[HBM_SPEC] * (2 * n) + [pl.BlockSpec(memory_space=pl.ANY)] * len(behind),
        out_specs=[SEM_SPEC, SEM_SPEC] + [HBM_SPEC] * (2 * n) + [pl.BlockSpec(memory_space=pltpu.VMEM)],
        out_shape=[pltpu.SemaphoreType.DMA((n_sems,)), pltpu.SemaphoreType.DMA((n_sems,))]
        + [pltpu.HBM(a.shape, a.dtype) for a in (*srcs, *lands)] + [SDS((8, 128), F32)],
        input_output_aliases={i: 2 + i for i in range(2 * n)},
        compiler_params=SPLIT_PARAMS,
    )(*[pltpu.with_memory_space_constraint(a, pltpu.HBM) for a in (*srcs, *lands)], *behind)
    return {"sems": outs[:2], "srcs": outs[2:2 + n], "lands": outs[2 + n:2 + 2 * n], "token": outs[-1]}


def exchange_wait(state, plan, after, name, peers=CHIPS):
    n = len(state["srcs"])
    behind = _as_list(after)

    def body(*refs):
        s_refs, l_refs = refs[:n], refs[n:2 * n]
        send_sems, recv_sems = refs[2 * n], refs[2 * n + 1]
        x, y, c = _me()
        for a in range(n):
            for i, k in enumerate(peers):
                src, _, landing, peer = plan(s_refs, l_refs, a, k, x, y, c)
                sem = len(peers) * a + i
                cp = _remote(src, landing, send_sems.at[sem], recv_sems.at[sem], peer)
                cp.wait_send()
                cp.wait_recv()

    bufs = (*state["srcs"], *state["lands"])
    outs = pl.pallas_call(
        body, name=name,
        in_specs=[HBM_SPEC] * (2 * n) + [SEM_SPEC, SEM_SPEC] + [pl.BlockSpec(memory_space=pl.ANY)] * len(behind),
        out_specs=[HBM_SPEC] * (2 * n),
        out_shape=[pltpu.HBM(a.shape, a.dtype) for a in bufs],
        input_output_aliases={i: i for i in range(2 * n)},
        compiler_params=SPLIT_PARAMS,
    )(*bufs, *state["sems"], *behind)
    return outs[:n], outs[n:]


def gather_finish(shards, fulls, rowlikes, name):
    n = len(shards)

    def body(*refs):
        s_refs, f_refs = refs[:n], refs[2 * n:3 * n]
        send_sems, recv_sems = refs[3 * n:]
        x, y, c = _me()
        chip = 2 * x + y
        sib = (x, y, 1 - c)
        sends = []
        for a in range(n):
            own = _piece(f_refs[a], rowlikes[a], chip)
            cp = _remote(s_refs[a], own, send_sems.at[a, 0], recv_sems.at[a, 0], sib)
            cp.start()
            sends.append(cp)
            for k in (1, 2, 3):
                px, py = _peer_chip(x, y, k)
                slot = _piece_half(f_refs[a], rowlikes[a], 2 * px + py, c)
                cp = _remote(slot, slot, send_sems.at[a, k], recv_sems.at[a, k], sib)
                cp.start()
                sends.append(cp)
        for a in range(n):
            own = _piece(f_refs[a], rowlikes[a], chip)
            _remote(own, own, send_sems.at[a, 0], recv_sems.at[a, 0], sib).wait_recv()
            for k in (1, 2, 3):
                px, py = _peer_chip(x, y, k)
                slot = _piece_half(f_refs[a], rowlikes[a], 2 * px + py, 1 - c)
                _remote(slot, slot, send_sems.at[a, k], recv_sems.at[a, k], sib).wait_recv()
        for cp in sends:
            cp.wait_send()

    return pl.pallas_call(
        body, name=name, in_specs=[HBM_SPEC] * (2 * n), out_specs=[HBM_SPEC] * n,
        out_shape=[SDS(f.shape, f.dtype) for f in fulls],
        input_output_aliases={n + a: a for a in range(n)},
        scratch_shapes=[pltpu.SemaphoreType.DMA((n, 4)), pltpu.SemaphoreType.DMA((n, 4))],
    )(*shards, *fulls)


def _half_buffer_shape(gview, rowlike):
    return (N_CHIPS,) + gview.shape[2:] if rowlike else gview.shape[1:]


def _row_tiles(kh, ns):
    return 1 if kh * ns <= 256 * 1024 else 2


def _run_jobs(jobs, place, name, after=None):
    n_steps = max(job[0] for job in jobs)

    def spec(block, index, steps):
        return pl.BlockSpec(block, lambda s, p: index(jnp.minimum(s, steps - 1), p))

    in_specs = [spec(b, ix, steps) for steps, ins, _, _ in jobs for _, b, ix in ins]
    out_specs = [spec(b, ix, steps) for steps, _, outs, _ in jobs for _, _, b, ix in outs]
    token_spec, token_arg = _token_operand(after)
    n_in = len(in_specs) + len(token_spec)

    def body(place_ref, *refs):
        s = pl.program_id(0)
        i, o = 0, n_in
        for steps, ins, outs, fn in jobs:
            in_refs, out_refs = refs[i:i + len(ins)], refs[o:o + len(outs)]
            i, o = i + len(ins), o + len(outs)

            @pl.when(s < steps)
            def _(fn=fn, in_refs=in_refs, out_refs=out_refs):
                fn(in_refs, out_refs)

    return pl.pallas_call(
        body, name=name,
        grid_spec=pltpu.PrefetchScalarGridSpec(num_scalar_prefetch=1, grid=(n_steps,),
                                               in_specs=in_specs + token_spec, out_specs=out_specs),
        out_shape=[SDS(shape, dtype) for _, _, outs, _ in jobs for shape, dtype, _, _ in outs],
        compiler_params=_cp(("arbitrary",)),
    )(place, *[a for _, ins, _, _ in jobs for a, _, _ in ins], *token_arg)


def add_halves(gviews, gots, rowlikes, place, name):
    def add(ins, outs):
        outs[0][...] = (ins[0][...].astype(F32) + ins[1][...].astype(F32)).astype(BF16)

    jobs = []
    for g, got, rowlike in zip(gviews, gots, rowlikes):
        kh, ns = (g.shape[2], g.shape[3]) if rowlike else (g.shape[1], g.shape[2] // N_CHIPS)
        rt = _row_tiles(kh, ns)
        tr = kh // rt
        if rowlike:
            g_in = (g, (None, None, tr, ns), lambda ls, p, rt=rt: (ls // rt, p[0], ls % rt, 0))
            half = ((None, tr, ns), lambda ls, p, rt=rt: (ls // rt, ls % rt, 0))
        else:
            g_in = (g, (None, tr, ns), lambda ls, p, rt=rt: (p[0], ls % rt, ls // rt))
            half = ((tr, ns), lambda ls, p, rt=rt: (ls % rt, ls // rt))
        jobs.append((N_CHIPS * rt, [g_in, (got, *half)], [(got.shape, BF16, *half)], add))
    return _run_jobs(jobs, place, name)


def _slot_shape(q, rowlike):
    return (3,) + (q.shape[1:] if rowlike else (q.shape[0], q.shape[1] // N_CHIPS))


def sum_pieces(qs, recvs, rowlikes, place, name):
    def total(ins, outs):
        acc = ins[0][...].astype(F32)
        for k in range(3):
            acc = acc + ins[1][k].astype(F32)
        outs[0][...] = acc

    jobs = []
    for q, recv, rowlike in zip(qs, recvs, rowlikes):
        kh, ns = recv.shape[1], recv.shape[2]
        rt = _row_tiles(kh, ns)
        tr = kh // rt
        mine = ((None, tr, ns), lambda ls, p: (p[1], ls, 0)) if rowlike else ((tr, ns), lambda ls, p: (ls, p[1]))
        jobs.append((rt, [(q, *mine), (recv, (3, tr, ns), lambda ls, p: (0, ls, 0))],
                     [((2, kh, ns), F32, (None, tr, ns), lambda ls, p: (p[0], ls, 0))], total))
    return _run_jobs(jobs, place, name)


def join_halves(halves, name):
    n = len(halves)

    def body(*refs):
        o_refs = refs[n:2 * n]
        send_sems, recv_sems = refs[2 * n:]
        x, y, c = _me()
        sib = (x, y, 1 - c)
        sends = []
        for a in range(n):
            cp = _remote(o_refs[a].at[c], o_refs[a].at[c], send_sems.at[a], recv_sems.at[a], sib)
            cp.start()
            sends.append(cp)
        for a in range(n):
            got = o_refs[a].at[1 - c]
            _remote(got, got, send_sems.at[a], recv_sems.at[a], sib).wait_recv()
        for cp in sends:
            cp.wait_send()

    return pl.pallas_call(
        body, name=name, in_specs=[HBM_SPEC] * n, out_specs=[HBM_SPEC] * n,
        out_shape=[SDS(h.shape, h.dtype) for h in halves],
        input_output_aliases={a: a for a in range(n)},
        scratch_shapes=[pltpu.SemaphoreType.DMA((n,)), pltpu.SemaphoreType.DMA((n,))],
    )(*halves)


N_DEV = 8


def sum_devices(s, slots, me):
    R, Wd = s.shape

    def body(me_ref, s_ref, slots_ref, out_ref):
        acc = None
        for d in range(N_DEV):
            mine = me_ref[0] == d
            term = jnp.where(mine, s_ref[...], slots_ref[jnp.where(mine, d ^ 1, d)])
            acc = term if acc is None else acc + term
        out_ref[...] = acc

    vmem = pl.BlockSpec(memory_space=pltpu.VMEM)
    return pl.pallas_call(
        body, name="sum_devices", in_specs=[pl.BlockSpec(memory_space=pltpu.SMEM), vmem, vmem], out_specs=vmem,
        out_shape=SDS((R, Wd), F32),
    )(me, s, slots)


TRANSPOSED = ("w_in",)
BIG = {"ffn1_w_up": (D, 2 * FF, "col"), "ffn1_w_down": (FF, D, "row"), "w_in": (INW, D, "row"),
       "w_attn_up": (AW, D, "row"), "w_pool_up": (PW, D, "col"), "w_out": (D, D, "row"),
       "ffn2_w_up": (D, 2 * FF, "col"), "ffn2_w_down": (FF, D, "row")}
GROUPS = (("ffn1_w_up",), ("ffn1_w_down", "w_in", "w_attn_up", "w_pool_up", "w_out"), ("ffn2_w_up", "ffn2_w_down"))
SMALL = ("ffn1_norm", "mix_norm", "ffn2_norm", "final_norm", "pool_scale", "sinks", "pool_w_mix")
SMALL_W = 128


def _rowlike(name):
    return BIG[name][2] == "row"


def _half_dims(name):
    k, n, kind = BIG[name]
    return (k // N_CHIPS // 2, n) if kind == "row" else (k // 2, n // N_CHIPS)


def shard_view(name, shard):
    return shard.reshape((2,) + _half_dims(name))


def full_from_view(name, fv):
    k, n, _ = BIG[name]
    return fv.reshape(k, n)


def grad_view(name, g):
    kh, ns = _half_dims(name)
    return g.reshape(_full_shape(jax.ShapeDtypeStruct((2, kh, ns), g.dtype), _rowlike(name)))


def pack_small(d):
    parts = []
    for name in SMALL:
        a = d[name].reshape(-1)
        pad = (-a.shape[0]) % SMALL_W
        parts.append(jnp.pad(a, (0, pad)).reshape(-1, SMALL_W))
    a = jnp.concatenate(parts, axis=0)
    return jnp.pad(a, ((0, (-a.shape[0]) % 8), (0, 0)))


def unpack_small(a, like):
    out, r0 = {}, 0
    for name in SMALL:
        size = int(np.prod(like[name].shape))
        rows = -(-size // SMALL_W)
        out[name] = a[r0:r0 + rows].reshape(-1)[:size].reshape(like[name].shape)
        r0 += rows
    return out


WEIGHTS = ("ffn1_norm", "ffn1_w_up", "ffn1_w_down", "mix_norm", "w_in", "sinks", "w_attn_up", "pool_w_mix",
           "pool_scale", "w_pool_up", "w_out", "ffn2_norm", "ffn2_w_up", "ffn2_w_down", "final_norm")


def kernel(x, ffn1_norm, ffn1_w_up, ffn1_w_down, mix_norm, w_in, sinks, w_attn_up, pool_w_mix, pool_scale, w_pool_up, w_out, ffn2_norm, ffn2_w_up, ffn2_w_down, final_norm, loss_target, m_ffn1_norm, m_ffn1_w_up, m_ffn1_w_down, m_mix_norm, m_w_in, m_sinks, m_w_attn_up, m_pool_w_mix, m_pool_scale, m_w_pool_up, m_w_out, m_ffn2_norm, m_ffn2_w_up, m_ffn2_w_down, m_final_norm, v_ffn1_norm, v_ffn1_w_up, v_ffn1_w_down, v_mix_norm, v_w_in, v_sinks, v_w_attn_up, v_pool_w_mix, v_pool_scale, v_w_pool_up, v_w_out, v_ffn2_norm, v_ffn2_w_up, v_ffn2_w_down, v_final_norm):
    given = dict(locals())
    w = {n: given[n] for n in WEIGHTS}
    m = {n: given["m_" + n] for n in WEIGHTS}
    v = {n: given["v_" + n] for n in WEIGHTS}
    cx, cy, cc = _me()
    place = jnp.stack([cc, 2 * cx + cy]).astype(jnp.int32)

    def local2d(d, n):
        return d[n][0].T if n in TRANSPOSED else d[n][0]

    shards = {n: local2d(w, n) for n in BIG}
    grads, delta, new_m, new_v = {}, {}, {}, {}
    rowlikes = [[_rowlike(n) for n in names] for names in GROUPS]

    class Exchanges:
        def __init__(self):
            self.gathers, self.forwards, self.fulls, self.reductions, self.small = {}, {}, {}, {}, None

        def _behind_first(self, a):
            return a + self.gathers[0]["token"][0, 0]

        def _start_gather(self, group, after):
            sv = [self.cast[n] if group else shard_view(n, shards[n].astype(BF16)) for n in GROUPS[group]]
            rl = rowlikes[group]
            self.gathers[group] = exchange_start(sv, [_full_shape(s, r) for s, r in zip(sv, rl)], _gather_plan(rl),
                                                 after, f"gather_start_{group}")

        def weights(self, group, after):
            if not self.gathers:
                self._start_gather(0, None)
                self.packed = [self._behind_first(pack_small(d)) for d in (w, m, v)]
                self.cast = {n: shard_view(n, self._behind_first(shards[n]).astype(BF16))
                             for names in GROUPS[1:] for n in names}
                after = [self.gathers[0]["token"], *self.packed, *self.cast.values()]
            if group not in self.fulls:
                names, rl, state = GROUPS[group], rowlikes[group], self.gathers[group]
                if group in self.forwards:
                    _, fulls = exchange_wait(self.forwards.pop(group), _forward_plan(rl), after,
                                             f"forward_wait_{group}", FORWARDS)
                else:
                    sv, fulls = exchange_wait(state, _gather_plan(rl), state["token"] if after is None else after,
                                              f"gather_wait_{group}")
                    fulls = gather_finish(sv, fulls, rl, f"gather_finish_{group}")
                self.fulls[group] = {n: full_from_view(n, f) for n, f in zip(names, fulls)}
                if group + 1 < len(GROUPS):
                    self._start_gather(group + 1, fulls[0])
            return self.fulls[group]

        def started(self):
            return self.gathers[max(self.gathers)]["token"]

        def prefetch(self, group, after):
            rl, state = rowlikes[group], self.gathers[group]
            sv, fulls = exchange_wait(state, _gather_plan(rl), after, f"gather_wait_{group}")
            self.forwards[group] = exchange_start(sv, fulls, _forward_plan(rl), None, f"forward_start_{group}", FORWARDS)
            return self.forwards[group]["token"]

        def grads(self, g):
            names = tuple(g)
            rl = [_rowlike(n) for n in names]
            gv = [grad_view(n, g[n]) for n in names]
            state = exchange_start(gv, [_half_buffer_shape(a, r) for a, r in zip(gv, rl)], _swap_plan(rl), None,
                                   "swap_start_" + names[0], SIBLING)
            self.reductions[names] = state
            return state["token"]

        def advance(self, names, after):
            rl = [_rowlike(n) for n in names]
            gv, gots = exchange_wait(self.reductions[names], _swap_plan(rl), after, "swap_wait_" + names[0], SIBLING)
            qs = add_halves(gv, gots, rl, place, "add_halves_" + names[0])
            state = exchange_start(qs, [_slot_shape(q, r) for q, r in zip(qs, rl)], _all_to_all_plan(rl), None,
                                   "all_to_all_start_" + names[0])
            self.reductions[names] = state
            return state["token"]

        def reduce(self, names, after):
            rl = [_rowlike(n) for n in names]
            qs, recvs = exchange_wait(self.reductions.pop(names), _all_to_all_plan(rl), after,
                                      "all_to_all_wait_" + names[0])
            halves = sum_pieces(qs, recvs, rl, place, "sum_pieces_" + names[0])
            return [(shards[n], o.reshape(shards[n].shape), local2d(m, n), local2d(v, n))
                    for n, o in zip(names, join_halves(halves, "join_halves_" + names[0]))]

        def updated(self, names, outs):
            for i, n in enumerate(names):
                grads[n], delta[n], new_m[n], new_v[n] = outs[4 * i:4 * i + 4]
            return [new_v[n] for n in names]

        def update(self, names, updates, token=None):
            return self.updated(names, adamw(updates, place, "adamw_" + names[0], after=token))

        def finish(self, names, after):
            return self.update(names, self.reduce(names, after))

        def small_start(self, G, loss, after):
            packed = pack_small({n: G[n] for n in SMALL})
            used_rows = sum(-(-int(np.prod(small_like[n].shape)) // SMALL_W) for n in SMALL)
            assert packed.shape[0] > used_rows
            packed = packed.at[-1, 0].set(loss[0, 0])
            self.small = exchange_start([packed], [(N_DEV,) + packed.shape], _everyone_plan, after,
                                        "small_start", EVERYONE)
            return self.small["token"]

        def small_finish(self, after):
            (packed,), (slots,) = exchange_wait(self.small, _everyone_plan, after, "small_wait", EVERYONE)
            total = sum_devices(packed, slots, (4 * cx + 2 * cy + cc).astype(jnp.int32).reshape(1))
            self.loss = total[-1, 0]
            g, ds, ms, vs = adamw([(self.packed[0], total, self.packed[1], self.packed[2])], place, "adamw_small")
            for d, packed_d in ((grads, g), (delta, ds), (new_m, ms), (new_v, vs)):
                d.update(unpack_small(packed_d, small_like))
            return vs

    small_like = {n: w[n] for n in SMALL}
    S = {n: w[n].reshape(1, -1) for n in ("ffn1_norm", "mix_norm", "ffn2_norm", "final_norm", "pool_scale", "sinks")}
    S["pool_w_mix"] = w["pool_w_mix"][0].astype(BF16)
    exchanges = Exchanges()
    _, dx = local_fwd_bwd(x[0], loss_target[0], S, exchanges)
    loss = exchanges.loss

    def shaped(d, n):
        return (d[n].T if n in TRANSPOSED else d[n]).reshape(w[n].shape)

    return (loss, dx[None], *[shaped(grads, n) for n in WEIGHTS], *[shaped(delta, n) for n in WEIGHTS],
            *[shaped(new_m, n) for n in WEIGHTS], *[shaped(new_v, n) for n in WEIGHTS])
```

```python
import numpy as np
import jax
import jax.numpy as jnp
from jax import lax
from jax.experimental import pallas as pl
from jax.experimental.pallas import tpu as pltpu

F32 = jnp.float32
BF16 = jnp.bfloat16
SDS = jax.ShapeDtypeStruct
MESH = pl.DeviceIdType.MESH

D = 1024
FF = 2816
NQ = 16
NKV = 2
HD = 64
GQ = NQ // NKV
AW = NQ * HD
KVW = NKV * HD
BLK = 128
PW = 512
PG = 128
POOL_WINDOWS = (2, 4, 8, 16)
HALO = 16
INW = AW + 2 * KVW + PW + 2 * D
C_KV = AW
C_Z = AW + 2 * KVW
C_G = C_Z + PW
EPS = 1e-6
FF_CHUNK = 256
FF_CHUNKS = tuple((c, FF_CHUNK) for c in range(0, FF, FF_CHUNK))
SLOPES = tuple(float(2.0 ** (-8.0 * h / NQ)) for h in range(1, NQ + 1))
SCALE = HD ** -0.5

LR, B1, B2, ADAM_EPS, WD, STEP = 0.001, 0.9, 0.999, 1e-08, 0.01, 10

VMEM_LIMIT = 56 * 1024 * 1024
N_CHIPS = 4

NT = (((1,), (1,)), ((), ()))
TN = (((0,), (0,)), ((), ()))


def _cp(sem=None, vmem=VMEM_LIMIT):
    return pltpu.CompilerParams(dimension_semantics=sem, vmem_limit_bytes=vmem)


def _const_spec(shape):
    nd = len(shape)
    return pl.BlockSpec(shape, lambda *_: (0,) * nd, pipeline_mode=pl.Buffered(1))


def _rstd(x):
    return lax.rsqrt(jnp.mean(x * x, axis=-1, keepdims=True) + EPS)


def _rms_bwd(dn, xhat, rstd, g):
    dxhat = dn * g
    return rstd * (dxhat - xhat * jnp.mean(dxhat * xhat, axis=-1, keepdims=True))


def _sigmoid(x):
    return 0.5 * jnp.tanh(0.5 * x) + 0.5


def _dot(a, b):
    return jnp.dot(a, b, preferred_element_type=F32)


def _dot_nt(a, b):
    return lax.dot_general(a, b, NT, preferred_element_type=F32)


def _dot_tn(a, b):
    return lax.dot_general(a, b, TN, preferred_element_type=F32)


def _swiglu_up(x, g, wup_ref, ab_ref, hid_ref):
    n = (x * _rstd(x) * g).astype(BF16)
    for c0, w in FF_CHUNKS:
        a = _dot(n, wup_ref[:, c0:c0 + w])
        b = _dot(n, wup_ref[:, FF + c0:FF + c0 + w])
        sig = _sigmoid(a)
        s = a * sig
        ab_ref[:, c0:c0 + w] = (b * (sig + s * (1.0 - sig))).astype(BF16)
        ab_ref[:, FF + c0:FF + c0 + w] = s.astype(BF16)
        hid_ref[:, c0:c0 + w] = (s * b).astype(BF16)


def ffn_up(h, g, wup, name, after=None):
    T = h.shape[0]
    TM = 512
    tile = lambda w: pl.BlockSpec((TM, w), lambda i: (i, 0))
    token_spec, token_arg = _token_operand(after)

    def body(h_ref, g_ref, wup_ref, *rest):
        _swiglu_up(h_ref[...], g_ref[...], wup_ref, *rest[-2:])

    return pl.pallas_call(
        body, name=name, grid=(T // TM,),
        in_specs=[tile(D), _const_spec((1, D)), _const_spec((D, 2 * FF))] + token_spec,
        out_specs=[tile(2 * FF), tile(FF)],
        out_shape=[SDS((T, 2 * FF), BF16), SDS((T, FF), BF16)],
        compiler_params=_cp(("arbitrary",)),
    )(h, g, wup, *token_arg)


def ffn_down(h, hid, wdn, name, after=None):
    T = h.shape[0]
    TM = min(1024, T)
    tile = lambda w: pl.BlockSpec((TM, w), lambda i: (i, 0))
    token_spec, token_arg = _token_operand(after)

    def body(h_ref, hid_ref, wdn_ref, *rest):
        rest[-1][...] = h_ref[...] + 0.5 * _dot(hid_ref[...], wdn_ref[...])

    return pl.pallas_call(
        body, name=name, grid=(T // TM,),
        in_specs=[tile(D), tile(FF), _const_spec((FF, D))] + token_spec,
        out_specs=tile(D), out_shape=SDS((T, D), F32),
        compiler_params=_cp(("arbitrary",)),
    )(h, hid, wdn, *token_arg)


def ffn_fwd(h, g, wup, wdn, name, head=None):
    T = h.shape[0]
    TM = 512
    tile = lambda w: pl.BlockSpec((TM, w), lambda i: (i, 0))
    acc_spec = lambda w: pl.BlockSpec((1, w), lambda i: (0, 0))

    def body(h_ref, g_ref, wup_ref, wdn_ref, *rest):
        out_ref, ab_ref, hid_ref = rest[-3:] if head is None else rest[2:5]
        x = h_ref[...]
        _swiglu_up(x, g_ref[...], wup_ref, ab_ref, hid_ref)
        out = x + 0.5 * _dot(hid_ref[...], wdn_ref[...])
        if head is None:
            out_ref[...] = out
        else:
            t_ref, gf_ref, loss_ref, dgf_ref = rest[0], rest[1], rest[5], rest[6]
            out_ref[...] = _loss_head(out, t_ref[...], gf_ref[...], loss_ref, dgf_ref, pl.program_id(0) == 0)

    head_in, head_specs, head_out_specs, head_out_shape = [], [], [], []
    if head is not None:
        head_in, head_specs = list(head), [tile(D), _const_spec((1, D))]
        head_out_specs, head_out_shape = [acc_spec(1), acc_spec(D)], [SDS((1, 1), F32), SDS((1, D), F32)]
    return pl.pallas_call(
        body, name=name, grid=(T // TM,),
        in_specs=[tile(D), _const_spec((1, D)), _const_spec((D, 2 * FF)), _const_spec((FF, D))] + head_specs,
        out_specs=[tile(D), tile(2 * FF), tile(FF)] + head_out_specs,
        out_shape=[SDS((T, D), F32), SDS((T, 2 * FF), BF16), SDS((T, FF), BF16)] + head_out_shape,
        compiler_params=_cp(("arbitrary",)),
    )(h, g, wup, wdn, *head_in)


def _loss_head(x, target, g, loss_ref, dg_ref, first):
    @pl.when(first)
    def _():
        loss_ref[...] = jnp.zeros_like(loss_ref)
        dg_ref[...] = jnp.zeros_like(dg_ref)

    rstd = _rstd(x)
    xhat = x * rstd
    err = xhat * g - target
    loss_ref[...] += 0.5 * jnp.sum(jnp.mean(err * err, axis=-1, keepdims=True), axis=0, keepdims=True)
    dy = err * (1.0 / D)
    dg_ref[...] += jnp.sum(dy * xhat, axis=0, keepdims=True)
    return _rms_bwd(dy, xhat, rstd, g)


def _adamw_tile(w_ref, g_ref, m_ref, v_ref, go_ref, d_ref, nm_ref, nv_ref):
    gv = g_ref[...]
    go_ref[...] = gv
    nm = B1 * m_ref[...] + (1.0 - B1) * gv
    nv = B2 * v_ref[...] + (1.0 - B2) * (gv * gv)
    nm_ref[...] = nm
    nv_ref[...] = nv
    d_ref[...] = -LR * ((nm / (1.0 - B1 ** STEP)) / (jnp.sqrt(nv / (1.0 - B2 ** STEP)) + ADAM_EPS) + WD * w_ref[...])


def _riders(updates, steps):
    in_specs, out_specs, out_shapes, operands, tiles = [], [], [], [], []
    for w, g, m, v in updates:
        R, C = w.shape
        n = max(d for d in range(1, steps + 1) if R % d == 0 and (R // d) % 8 == 0)
        spec = pl.BlockSpec((R // n, C), lambda i, n=n: (jnp.minimum(i, n - 1), 0))
        in_specs += [spec] * 4
        out_specs += [spec] * 4
        out_shapes += [SDS((R, C), F32)] * 4
        operands += [w, g, m, v]
        tiles.append(n)

    def run(step, in_refs, out_refs):
        for u, n in enumerate(tiles):
            @pl.when(step < n)
            def _(u=u):
                _adamw_tile(*in_refs[4 * u:4 * u + 4], *out_refs[4 * u:4 * u + 4])

    return in_specs, out_specs, out_shapes, operands, run


def ffn_bwd_x(dh, h_in, ab, g, wup, wdn, name, updates=(), after=None):
    T = dh.shape[0]
    TM = 256 if updates else 512
    SUB = 256
    r_in, r_out, r_shapes, r_args, ride = _riders(updates, T // TM)
    token_spec, token_arg = _token_operand(after)
    n_in = len(r_in) + len(token_spec)

    def body(dh_ref, h_ref, ab_ref, g_ref, wup_ref, wdn_ref, *rest):
        dhin_ref, dup_ref, n_ref, dg_ref = rest[n_in:n_in + 4]
        ride(pl.program_id(0), rest[:len(r_in)], rest[n_in + 4:])
        g = g_ref[...]
        dg = None
        for r0 in range(0, TM, SUB):
            rows = slice(r0, r0 + SUB)
            x = h_ref[rows, :]
            rstd = _rstd(x)
            xhat = x * rstd
            n_ref[rows, :] = (xhat * g).astype(BF16)
            dh = dh_ref[rows, :]
            dhh = (0.5 * dh).astype(BF16)
            for c0, w in FF_CHUNKS:
                dhid = _dot_nt(dhh, wdn_ref[c0:c0 + w, :]).astype(BF16)
                dup_ref[rows, c0:c0 + w] = dhid * ab_ref[rows, c0:c0 + w]
                dup_ref[rows, FF + c0:FF + c0 + w] = dhid * ab_ref[rows, FF + c0:FF + c0 + w]
            dn = _dot_nt(dup_ref[rows, :], wup_ref[...])
            dhin_ref[rows, :] = dh + _rms_bwd(dn, xhat, rstd, g)
            part = jnp.sum(dn * xhat, axis=0, keepdims=True)
            dg = part if dg is None else dg + part

        @pl.when(pl.program_id(0) == 0)
        def _():
            dg_ref[...] = jnp.zeros_like(dg_ref)

        dg_ref[...] += dg

    tile = lambda w: pl.BlockSpec((TM, w), lambda i: (i, 0))
    return pl.pallas_call(
        body, name=name, grid=(T // TM,),
        in_specs=[tile(D), tile(D), tile(2 * FF), _const_spec((1, D)), _const_spec((D, 2 * FF)), _const_spec((FF, D))]
        + r_in + token_spec,
        out_specs=[tile(D), tile(2 * FF), tile(D), pl.BlockSpec((1, D), lambda i: (0, 0))] + r_out,
        out_shape=[SDS((T, D), F32), SDS((T, 2 * FF), BF16), SDS((T, D), BF16), SDS((1, D), F32)] + r_shapes,
        compiler_params=_cp(("arbitrary",)),
    )(dh, h_in, ab, g, wup, wdn, *r_args, *token_arg)


TOKEN_SPEC = pl.BlockSpec((8, 128), lambda *_: (0, 0))


def _token_operand(after):
    tokens = [t for t in (after if isinstance(after, tuple) else (after,)) if t is not None]
    return [TOKEN_SPEC] * len(tokens), tokens


def matmul_tn(a, b, name, *, tm, tn, tt=2048, b_scale=None, after=None):
    T, M = a.shape
    N = b.shape[1]
    tt = min(tt, T)
    assert M % tm == 0 and N % tn == 0 and T % tt == 0
    nt = T // tt
    token_spec, token_arg = _token_operand(after)

    def body(a_ref, b_ref, *rest):
        o_ref, acc_ref = rest[-2:]
        t = pl.program_id(2)

        @pl.when(t == 0)
        def _():
            acc_ref[...] = jnp.zeros_like(acc_ref)

        bv = b_ref[...]
        if b_scale is not None:
            bv = bv * b_scale
        acc_ref[...] += _dot_tn(a_ref[...].astype(BF16), bv.astype(BF16))

        @pl.when(t == nt - 1)
        def _():
            o_ref[...] = acc_ref[...].astype(BF16)

    return pl.pallas_call(
        body, name=name, grid=(M // tm, N // tn, nt),
        in_specs=[pl.BlockSpec((tt, tm), lambda i, j, t: (t, i)), pl.BlockSpec((tt, tn), lambda i, j, t: (t, j))]
        + token_spec,
        out_specs=pl.BlockSpec((tm, tn), lambda i, j, t: (i, j)),
        out_shape=SDS((M, N), BF16),
        scratch_shapes=[pltpu.VMEM((tm, tn), F32)],
        compiler_params=_cp(("parallel", "parallel", "arbitrary")),
    )(a, b, *token_arg)


def mix_in_fwd(h1, g, win_t):
    T = h1.shape[0]
    TM = 512

    def body(h_ref, g_ref, w_ref, u_ref, q_ref, kv_ref, z_ref, gt_ref):
        x = h_ref[...]
        u = (x * _rstd(x) * g_ref[...]).astype(BF16)
        u_ref[...] = u
        for c in range(0, AW, 256):
            q_ref[:, c:c + 256] = _dot_nt(u, w_ref[c:c + 256, :]).astype(BF16)
        kv_ref[...] = _dot_nt(u, w_ref[C_KV:C_Z, :]).astype(BF16)
        for c in range(0, PW, 256):
            z_ref[:, c:c + 256] = _dot_nt(u, w_ref[C_Z + c:C_Z + c + 256, :])
        for c in range(0, 2 * D, 256):
            gt_ref[:, c:c + 256] = _dot_nt(u, w_ref[C_G + c:C_G + c + 256, :]).astype(BF16)

    tile = lambda w: pl.BlockSpec((TM, w), lambda i: (i, 0))
    return pl.pallas_call(
        body, name="mix_in_fwd", grid=(T // TM,),
        in_specs=[tile(D), _const_spec((1, D)), _const_spec((INW, D))],
        out_specs=[tile(D), tile(AW), tile(2 * KVW), tile(PW), tile(2 * D)],
        out_shape=[SDS((T, D), BF16), SDS((T, AW), BF16), SDS((T, 2 * KVW), BF16), SDS((T, PW), F32),
                   SDS((T, 2 * D), BF16)],
        compiler_params=_cp(("arbitrary",)),
    )(h1, g, win_t)


def mix_in_bwd(dq, dkv, dz, dgt, dh2, h1, g, win_t):
    T = h1.shape[0]
    TM = min(1024, T)
    SUB = 256

    def body(dq_ref, dkv_ref, dz_ref, dgt_ref, dh2_ref, h_ref, g_ref, w_ref, dh1_ref, dg_ref):
        g = g_ref[...]
        dg = None
        for r0 in range(0, TM, SUB):
            rows = slice(r0, r0 + SUB)
            du = _dot(dq_ref[rows, :], w_ref[0:AW, :])
            du += _dot(dkv_ref[rows, :], w_ref[C_KV:C_Z, :])
            du += _dot(dz_ref[rows, :], w_ref[C_Z:C_G, :])
            du += _dot(dgt_ref[rows, :], w_ref[C_G:INW, :])
            x = h_ref[rows, :]
            rstd = _rstd(x)
            xhat = x * rstd
            dh1_ref[rows, :] = dh2_ref[rows, :] + _rms_bwd(du, xhat, rstd, g)
            part = jnp.sum(du * xhat, axis=0, keepdims=True)
            dg = part if dg is None else dg + part

        @pl.when(pl.program_id(0) == 0)
        def _():
            dg_ref[...] = jnp.zeros_like(dg_ref)

        dg_ref[...] += dg

    tile = lambda w: pl.BlockSpec((TM, w), lambda i: (i, 0))
    return pl.pallas_call(
        body, name="mix_in_bwd", grid=(T // TM,),
        in_specs=[tile(AW), tile(2 * KVW), tile(PW), tile(2 * D), tile(D), tile(D), _const_spec((1, D)),
                  _const_spec((INW, D))],
        out_specs=[tile(D), pl.BlockSpec((1, D), lambda i: (0, 0))],
        out_shape=[SDS((T, D), F32), SDS((1, D), F32)],
        compiler_params=_cp(("arbitrary",)),
    )(dq, dkv, dz, dgt, dh2, h1, g, win_t)


PAIR = 2 * HD
NPAIR = GQ // 2


def _lo_lanes():
    return lax.broadcasted_iota(jnp.int32, (BLK, PAIR), 1) < HD


def _stack_heads(ref, kvh, scale=None):
    lo = _lo_lanes()
    parts = []
    for pr in range(NPAIR):
        t = ref[:, (kvh * NPAIR + pr) * PAIR:(kvh * NPAIR + pr + 1) * PAIR]
        if scale is not None:
            t = t * scale
        zero = jnp.zeros_like(t)
        parts += [jnp.where(lo, t, zero), jnp.where(lo, zero, t)]
    return jnp.concatenate(parts, axis=0)


def _kv_tiles(kvc_ref, kvp_ref, tile, kvh):
    lo = _lo_lanes()
    dup, left, right = [], [], []
    for ref in (kvp_ref, kvc_ref):
        t = ref[:, tile * PAIR:(tile + 1) * PAIR]
        r = pltpu.roll(t.astype(F32), HD, 1).astype(BF16)
        zero = jnp.zeros_like(t)
        a, b = (t, r) if kvh == 0 else (r, t)
        dup.append(jnp.where(lo, a, b))
        left.append(jnp.where(lo, a, zero))
        right.append(jnp.where(lo, zero, b))
    cat = lambda xs: jnp.concatenate(xs, axis=0)
    return cat(dup), cat(left), cat(right)


def _band_consts(first):
    row = lax.broadcasted_iota(jnp.int32, (BLK, BLK), 0)
    col = lax.broadcasted_iota(jnp.int32, (BLK, BLK), 1)
    upper = col > row
    dist = jnp.where(upper, row - col + BLK, row - col).astype(F32)
    pen = jnp.where(jnp.logical_and(upper, first), -jnp.inf, 0.0)
    return upper, dist, pen


def _split_band(upper, t):
    zero = jnp.zeros_like(t)
    return jnp.concatenate([jnp.where(upper, t, zero), jnp.where(upper, zero, t)], axis=1)


def attn_fwd(q, kv, sinks):
    T = q.shape[0]
    nb = T // BLK

    def body(sink_ref, q_ref, kvc_ref, kvp_ref, att_ref, lse_ref):
        upper, dist, pen = _band_consts(pl.program_id(0) == 0)
        scores, values = [], []
        for kvh in range(NKV):
            kdup, _, _ = _kv_tiles(kvc_ref, kvp_ref, 0, kvh)
            values.append(_kv_tiles(kvc_ref, kvp_ref, 1, kvh)[1:])
            scores.append(_dot_nt(_stack_heads(q_ref, kvh, SCALE), kdup))
        for kvh in range(NKV):
            s_all = scores[kvh]
            vleft, vright = values[kvh]
            for pr in range(NPAIR):
                outs, inv = [], []
                for side, vpad in ((0, vleft), (1, vright)):
                    g = 2 * pr + side
                    hq = kvh * GQ + g
                    sink = sink_ref[0, hq]
                    rows = slice(g * BLK, (g + 1) * BLK)
                    s = jnp.where(upper, s_all[rows, 0:BLK], s_all[rows, BLK:2 * BLK]) - SLOPES[hq] * dist + pen
                    m = jnp.maximum(jnp.max(s, axis=-1, keepdims=True), sink)
                    p = jnp.exp(s - m)
                    l = jnp.sum(p, axis=-1, keepdims=True) + jnp.exp(sink - m)
                    lse_ref[:, hq:hq + 1] = m + jnp.log(l)
                    outs.append(_dot(_split_band(upper, p.astype(BF16)), vpad))
                    inv.append(1.0 / l)
                col0 = (kvh * NPAIR + pr) * PAIR
                att_ref[:, col0:col0 + PAIR] = ((outs[0] + outs[1]) * jnp.where(_lo_lanes(), inv[0], inv[1])).astype(BF16)

    return pl.pallas_call(
        body, name="attn_fwd", grid=(nb,),
        in_specs=[pl.BlockSpec(memory_space=pltpu.SMEM),
                  pl.BlockSpec((BLK, AW), lambda i: (i, 0)),
                  pl.BlockSpec((BLK, 2 * KVW), lambda i: (i, 0)),
                  pl.BlockSpec((BLK, 2 * KVW), lambda i: (jnp.maximum(i - 1, 0), 0))],
        out_specs=[pl.BlockSpec((BLK, AW), lambda i: (i, 0)), pl.BlockSpec((BLK, NQ), lambda i: (i, 0))],
        out_shape=[SDS((T, AW), BF16), SDS((T, NQ), F32)],
        compiler_params=_cp(("arbitrary",)),
    )(sinks, q, kv, kv)


def attn_bwd(q, kv, datt, att, lse, sinks, after=None):
    T = q.shape[0]
    nb = T // BLK
    token_spec, token_arg = _token_operand(after)

    def body(sink_ref, q_ref, kvc_ref, kvp_ref, do_ref, out_ref, lse_ref, *rest):
        dq_ref, dkv_ref, dsink_ref, carry_ref = rest[-4:]
        i = pl.program_id(0)

        @pl.when(i == 0)
        def _():
            dsink_ref[...] = jnp.zeros_like(dsink_ref)
            carry_ref[...] = jnp.zeros_like(carry_ref)

        @pl.when(i < nb)
        def _():
            upper, dist, pen = _band_consts(i == 0)
            lo = _lo_lanes()
            dk_dup, dv_dup = [], []
            staged = []
            for kvh in range(NKV):
                kdup, kleft, kright = _kv_tiles(kvc_ref, kvp_ref, 0, kvh)
                vdup, _, _ = _kv_tiles(kvc_ref, kvp_ref, 1, kvh)
                qs = _stack_heads(q_ref, kvh, SCALE)
                dos = _stack_heads(do_ref, kvh)
                staged.append((kleft, kright, qs, dos, _dot_nt(qs, kdup), _dot_nt(dos, vdup)))
            deltas = []
            for pair in range(NQ // 2):
                cols = slice(pair * PAIR, (pair + 1) * PAIR)
                t = do_ref[:, cols].astype(F32) * out_ref[:, cols].astype(F32)
                deltas += [jnp.sum(jnp.where(lo, t, 0.0), axis=-1, keepdims=True),
                           jnp.sum(jnp.where(lo, 0.0, t), axis=-1, keepdims=True)]
            for kvh in range(NKV):
                kleft, kright, qs, dos, s_all, dp_all = staged[kvh]
                ds_parts, p_parts = [], []
                for pr in range(NPAIR):
                    dq = None
                    for side, kpad in ((0, kleft), (1, kright)):
                        g = 2 * pr + side
                        hq = kvh * GQ + g
                        lse_h = lse_ref[:, hq:hq + 1]
                        rows = slice(g * BLK, (g + 1) * BLK)
                        s = jnp.where(upper, s_all[rows, 0:BLK], s_all[rows, BLK:2 * BLK]) - SLOPES[hq] * dist + pen
                        p = jnp.exp(s - lse_h)
                        dp = jnp.where(upper, dp_all[rows, 0:BLK], dp_all[rows, BLK:2 * BLK])
                        delta = deltas[hq]
                        dsink_ref[:, hq:hq + 1] += -jnp.sum(jnp.exp(sink_ref[0, hq] - lse_h) * delta, axis=0,
                                                            keepdims=True)
                        ds = _split_band(upper, (p * (dp - delta)).astype(BF16))
                        ds_parts.append(ds)
                        p_parts.append(_split_band(upper, p.astype(BF16)))
                        d = _dot(ds, kpad)
                        dq = d if dq is None else dq + d
                    col0 = (kvh * NPAIR + pr) * PAIR
                    dq_ref[:, col0:col0 + PAIR] = (dq * SCALE).astype(BF16)
                dkw = _dot_tn(qs, jnp.concatenate(ds_parts, axis=0)).T
                dvw = _dot_tn(dos, jnp.concatenate(p_parts, axis=0)).T
                dk_dup.append(dkw + pltpu.roll(dkw, HD, 1))
                dv_dup.append(dvw + pltpu.roll(dvw, HD, 1))
            dk = jnp.where(jnp.concatenate([lo, lo], axis=0), dk_dup[0], dk_dup[1])
            dv = jnp.where(jnp.concatenate([lo, lo], axis=0), dv_dup[0], dv_dup[1])
            dkv_ref[:, 0:PAIR] = (carry_ref[:, 0:PAIR] + dk[0:BLK]).astype(BF16)
            dkv_ref[:, PAIR:2 * PAIR] = (carry_ref[:, PAIR:2 * PAIR] + dv[0:BLK]).astype(BF16)
            carry_ref[:, 0:PAIR] = dk[BLK:2 * BLK]
            carry_ref[:, PAIR:2 * PAIR] = dv[BLK:2 * BLK]

        @pl.when(i == nb)
        def _():
            dkv_ref[...] = carry_ref[...].astype(BF16)

    cur = lambda i: (jnp.minimum(i, nb - 1), 0)
    prev = lambda i: (jnp.maximum(jnp.minimum(i, nb - 1) - 1, 0), 0)
    return pl.pallas_call(
        body, name="attn_bwd", grid=(nb + 1,),
        in_specs=[pl.BlockSpec(memory_space=pltpu.SMEM),
                  pl.BlockSpec((BLK, AW), cur), pl.BlockSpec((BLK, 2 * KVW), cur), pl.BlockSpec((BLK, 2 * KVW), prev),
                  pl.BlockSpec((BLK, AW), cur), pl.BlockSpec((BLK, AW), cur), pl.BlockSpec((BLK, NQ), cur)] + token_spec,
        out_specs=[pl.BlockSpec((BLK, AW), cur),
                   pl.BlockSpec((BLK, 2 * KVW), lambda i: (jnp.maximum(i - 1, 0), 0)),
                   pl.BlockSpec((1, NQ), lambda i: (0, 0))],
        out_shape=[SDS((T, AW), BF16), SDS((T, 2 * KVW), BF16), SDS((1, NQ), F32)],
        scratch_shapes=[pltpu.VMEM((BLK, 2 * KVW), F32)],
        compiler_params=_cp(("arbitrary",)),
    )(sinks, q, kv, kv, datt, att, lse, *token_arg)


def _inv_counts(t0, rows):
    t = (t0 + lax.broadcasted_iota(jnp.int32, (rows, 1), 0) + 1).astype(F32)
    return [1.0 / jnp.minimum(t, float(w)) for w in POOL_WINDOWS]


def pool_fwd(z, wmix, scale, after=None):
    T = z.shape[0]
    TM = min(1024, T)
    L = TM + HALO
    token_spec, token_arg = _token_operand(after)

    def body(z_ref, halo_ref, wmix_ref, scale_ref, *rest):
        pooled_ref, mixs_ref = rest[-2:]
        i = pl.program_id(0)
        halo = jnp.where(i > 0, halo_ref[...], 0.0)
        zt = z_ref[...]
        e = jnp.concatenate([halo, zt], axis=0)
        sums = []
        s = e
        for k in (1, 2, 4, 8):
            s = s + pltpu.roll(s, k, 0)
            sums.append(s)
        inv = _inv_counts(i * TM, TM)
        for gi in range(len(POOL_WINDOWS)):
            cols = slice(gi * PG, (gi + 1) * PG)
            pooled = (sums[gi][HALO:, cols] * inv[gi] - zt[:, cols]).astype(BF16)
            pooled_ref[:, cols] = pooled
            mixs_ref[:, cols] = (_dot(pooled, wmix_ref[gi]) * scale_ref[:, cols]).astype(BF16)

    return pl.pallas_call(
        body, name="pool_fwd", grid=(T // TM,),
        in_specs=[pl.BlockSpec((TM, PW), lambda i: (i, 0)),
                  pl.BlockSpec((HALO, PW), lambda i: (jnp.maximum(i * (TM // HALO) - 1, 0), 0)),
                  _const_spec((len(POOL_WINDOWS), PG, PG)), _const_spec((1, PW))] + token_spec,
        out_specs=[pl.BlockSpec((TM, PW), lambda i: (i, 0)), pl.BlockSpec((TM, PW), lambda i: (i, 0))],
        out_shape=[SDS((T, PW), BF16), SDS((T, PW), BF16)],
        compiler_params=_cp(("arbitrary",)),
    )(z, z, wmix, scale, *token_arg)


def pool_bwd(dmixs, pooled, wmix, scale, after=None):
    T = dmixs.shape[0]
    TM = min(1024, T)
    L = TM + HALO
    nt = T // TM
    token_spec, token_arg = _token_operand(after)

    def body(dm_ref, halo_ref, pooled_ref, wmix_ref, scale_ref, *rest):
        dz_ref, dwmix_ref, dscale_ref = rest[-3:]
        i = pl.program_id(0)

        @pl.when(i == 0)
        def _():
            dwmix_ref[...] = jnp.zeros_like(dwmix_ref)
            dscale_ref[...] = jnp.zeros_like(dscale_ref)

        halo = jnp.where(i < nt - 1, halo_ref[...], 0.0)
        dm = dm_ref[...]
        e = jnp.concatenate([dm, halo], axis=0)
        inv = _inv_counts(i * TM, L)
        for gi in range(len(POOL_WINDOWS)):
            cols = slice(gi * PG, (gi + 1) * PG)
            w = wmix_ref[gi]
            dmixed = (e[:, cols] * scale_ref[:, cols]).astype(BF16)
            dpooled = _dot_nt(dmixed, w)
            pooled = pooled_ref[:, cols]
            mixed = _dot(pooled, w)
            dscale_ref[:, cols] += jnp.sum(dm[:, cols] * mixed, axis=0, keepdims=True)
            dwmix_ref[gi] += _dot_tn(pooled, dmixed[:TM])
            s = dpooled * inv[gi]
            k = 1
            while k < POOL_WINDOWS[gi]:
                s = s + pltpu.roll(s, L - k, 0)
                k *= 2
            dz_ref[:, cols] = (s[:TM] - dpooled[:TM]).astype(BF16)

    return pl.pallas_call(
        body, name="pool_bwd", grid=(nt,),
        in_specs=[pl.BlockSpec((TM, PW), lambda i: (i, 0)),
                  pl.BlockSpec((HALO, PW), lambda i: (jnp.minimum((i + 1) * (TM // HALO), T // HALO - 1), 0)),
                  pl.BlockSpec((TM, PW), lambda i: (i, 0)),
                  _const_spec((len(POOL_WINDOWS), PG, PG)), _const_spec((1, PW))] + token_spec,
        out_specs=[pl.BlockSpec((TM, PW), lambda i: (i, 0)),
                   pl.BlockSpec((len(POOL_WINDOWS), PG, PG), lambda i: (0, 0, 0)),
                   pl.BlockSpec((1, PW), lambda i: (0, 0))],
        out_shape=[SDS((T, PW), BF16), SDS((len(POOL_WINDOWS), PG, PG), F32), SDS((1, PW), F32)],
        compiler_params=_cp(("arbitrary",)),
    )(dmixs, dmixs, pooled, wmix, scale, *token_arg)


def merge_fwd(att, mixs, gt, h1, wattn, wpool, wout):
    T = h1.shape[0]
    TM = 512

    def body(att_ref, mixs_ref, gt_ref, h_ref, wa_ref, wp_ref, wo_ref, h2_ref, mg_ref, gf_ref):
        a = _dot(att_ref[...], wa_ref[...])
        p = _dot(mixs_ref[...], wp_ref[...])
        sa = _sigmoid(gt_ref[:, 0:D].astype(F32))
        sp = _sigmoid(gt_ref[:, D:2 * D].astype(F32))
        gf_ref[:, 0:D] = (a * sa * (1.0 - sa)).astype(BF16)
        gf_ref[:, D:2 * D] = (p * sp * (1.0 - sp)).astype(BF16)
        mg = (sa * a + sp * p).astype(BF16)
        mg_ref[...] = mg
        h2_ref[...] = h_ref[...] + _dot(mg, wo_ref[...])

    tile = lambda w: pl.BlockSpec((TM, w), lambda i: (i, 0))
    return pl.pallas_call(
        body, name="merge_fwd", grid=(T // TM,),
        in_specs=[tile(AW), tile(PW), tile(2 * D), tile(D), _const_spec((AW, D)), _const_spec((PW, D)),
                  _const_spec((D, D))],
        out_specs=[tile(D), tile(D), tile(2 * D)],
        out_shape=[SDS((T, D), F32), SDS((T, D), BF16), SDS((T, 2 * D), BF16)],
        compiler_params=_cp(("arbitrary",)),
    )(att, mixs, gt, h1, wattn, wpool, wout)


def merge_bwd(dh2, gt, gf, wattn, wpool, wout, after=None):
    T = dh2.shape[0]
    TM = 512
    token_spec, token_arg = _token_operand(after)

    def body(dh2_ref, gt_ref, gf_ref, wa_ref, wp_ref, wo_ref, *rest):
        datt_ref, dmixs_ref, dgt_ref, da_ref, dp_ref = rest[-5:]
        dm = _dot_nt(dh2_ref[...].astype(BF16), wo_ref[...])
        da = (dm * _sigmoid(gt_ref[:, 0:D].astype(F32))).astype(BF16)
        dp = (dm * _sigmoid(gt_ref[:, D:2 * D].astype(F32))).astype(BF16)
        da_ref[...] = da
        dp_ref[...] = dp
        dgt_ref[:, 0:D] = (dm * gf_ref[:, 0:D].astype(F32)).astype(BF16)
        dgt_ref[:, D:2 * D] = (dm * gf_ref[:, D:2 * D].astype(F32)).astype(BF16)
        datt_ref[...] = _dot_nt(da, wa_ref[...]).astype(BF16)
        dmixs_ref[...] = _dot_nt(dp, wp_ref[...])

    tile = lambda w: pl.BlockSpec((TM, w), lambda i: (i, 0))
    return pl.pallas_call(
        body, name="merge_bwd", grid=(T // TM,),
        in_specs=[tile(D), tile(2 * D), tile(2 * D), _const_spec((AW, D)), _const_spec((PW, D)),
                  _const_spec((D, D))] + token_spec,
        out_specs=[tile(AW), tile(PW), tile(2 * D), tile(D), tile(D)],
        out_shape=[SDS((T, AW), BF16), SDS((T, PW), F32), SDS((T, 2 * D), BF16), SDS((T, D), BF16),
                   SDS((T, D), BF16)],
        compiler_params=_cp(("arbitrary",)),
    )(dh2, gt, gf, wattn, wpool, wout, *token_arg)


def adamw(updates, place, name, after=None):
    tile_bytes = 2 * 1024 * 1024 // len(updates)
    jobs = []
    for w, g, m, v in updates:
        R, C = w.shape
        tr = R
        if R * C * 4 > tile_bytes:
            tr = next(cand for cand in (512, 256, 128, 64, 32, 16, 8) if R % cand == 0 and cand * C * 4 <= tile_bytes)
        blk = ((tr, C), lambda ls, p: (ls, 0))
        jobs.append((R // tr, [(a, *blk) for a in (w, g, m, v)], [((R, C), F32, *blk)] * 4,
                     lambda ins, outs: _adamw_tile(*ins, *outs)))
    return _run_jobs(jobs, place, name, after)


GROUP_UP1, GROUP_MIX, GROUP_FFN2 = 0, 1, 2


def local_fwd_bwd(x, target, S, comm):
    w_up1 = comm.weights(GROUP_UP1, None)["ffn1_w_up"]
    ab1, hid1 = ffn_up(x, S["ffn1_norm"], w_up1, "ffn1_up", after=comm.started())
    W = comm.weights(GROUP_MIX, hid1)
    h1 = ffn_down(x, hid1, W["ffn1_w_down"], "ffn1_down", after=comm.started())
    u, q, kv, z, gt = mix_in_fwd(h1, S["mix_norm"], W["w_in"])
    att, lse = attn_fwd(q, kv, S["sinks"])
    pooled, mixs = pool_fwd(z, S["pool_w_mix"], S["pool_scale"], after=comm.prefetch(GROUP_FFN2, att))
    h2, merged, gf = merge_fwd(att, mixs, gt, h1, W["w_attn_up"], W["w_pool_up"], W["w_out"])
    W = comm.weights(GROUP_FFN2, h2)
    dh3, ab2, hid2, loss, g_final = ffn_fwd(h2, S["ffn2_norm"], W["ffn2_w_up"], W["ffn2_w_down"], "ffn2_fwd",
                                            head=(target, S["final_norm"]))

    G = {"final_norm": g_final}
    dh2, dup2, n2, G["ffn2_norm"] = ffn_bwd_x(dh3, h2, ab2, S["ffn2_norm"], W["ffn2_w_up"], W["ffn2_w_down"],
                                              "ffn2_bwd_x")
    ffn2 = ("ffn2_w_down", "ffn2_w_up")
    token = comm.grads({"ffn2_w_down": matmul_tn(hid2, dh3, "ffn2_dw_down", tm=1408, tn=512, tt=4096, b_scale=0.5),
                        "ffn2_w_up": matmul_tn(n2, dup2, "ffn2_dw_up", tm=D, tn=512, tt=4096)})

    W = comm.weights(GROUP_MIX, None)
    datt, dmixs, dgt, da, dp = merge_bwd(dh2, gt, gf, W["w_attn_up"], W["w_pool_up"], W["w_out"], after=token)
    token = comm.advance(ffn2, datt)
    g_mix = {"w_out": matmul_tn(merged, dh2, "dw_out", tm=D, tn=D, after=token),
             "w_attn_up": matmul_tn(att, da, "dw_attn_up", tm=AW, tn=D),
             "w_pool_up": matmul_tn(mixs, dp, "dw_pool_up", tm=PW, tn=D)}
    dz, G["pool_w_mix"], G["pool_scale"] = pool_bwd(dmixs, pooled, S["pool_w_mix"], S["pool_scale"], after=token)
    dq, dkv, G["sinks"] = attn_bwd(q, kv, datt, att, lse, S["sinks"], after=token)
    dh1, G["mix_norm"] = mix_in_bwd(dq, dkv, dz, dgt, dh2, h1, S["mix_norm"], W["w_in"])
    g_mix["w_in"] = jnp.concatenate([
        matmul_tn(dq, u, "dw_in_q", tm=AW, tn=D),
        matmul_tn(dkv, u, "dw_in_kv", tm=2 * KVW, tn=D),
        matmul_tn(dz, u, "dw_in_z", tm=PW, tn=D),
        matmul_tn(dgt, u, "dw_in_g", tm=D, tn=512, tt=4096),
    ], axis=0)
    mix = tuple(g_mix)
    token = comm.grads(g_mix)

    g_dn1 = matmul_tn(hid1, dh1, "ffn1_dw_down", tm=1408, tn=512, tt=4096, b_scale=0.5, after=token)
    token = comm.advance(mix, g_dn1)
    token2 = comm.grads({"ffn1_w_down": g_dn1})
    dx, dup1, n1, G["ffn1_norm"], *outs = ffn_bwd_x(dh1, x, ab1, S["ffn1_norm"], w_up1, W["ffn1_w_down"], "ffn1_bwd_x",
                                                   updates=comm.reduce(ffn2, token2), after=(token, token2))
    comm.updated(ffn2, outs)
    token = comm.advance(("ffn1_w_down",), dx)
    token2 = comm.small_start(G, loss, dx)
    g_up1 = matmul_tn(n1, dup1, "ffn1_dw_up", tm=D, tn=512, tt=4096, after=(token, token2))
    token = comm.grads({"ffn1_w_up": g_up1})
    updates = comm.reduce(("ffn1_w_down",), token)
    token = comm.advance(("ffn1_w_up",), [u[1] for u in updates])
    done = comm.update(("ffn1_w_down",), updates, token)
    done = comm.finish(mix, done)
    done = comm.small_finish(done)
    comm.finish(("ffn1_w_up",), done)
    return loss, dx


HBM_SPEC = pl.BlockSpec(memory_space=pltpu.HBM)


def _me():
    return lax.axis_index("x"), lax.axis_index("y"), lax.axis_index("c")


def _peer_chip(x, y, k):
    return x ^ (k >> 1), y ^ (k & 1)


def _piece_half(ref, rowlike, j, h):
    if rowlike:
        return ref.at[j, h]
    ns = ref.shape[-1] // N_CHIPS
    return ref.at[h, :, pl.ds(pl.multiple_of(j * ns, 128), ns)]


def _piece(ref, rowlike, j):
    if rowlike:
        return ref.at[j]
    ns = ref.shape[-1] // N_CHIPS
    return ref.at[:, :, pl.ds(pl.multiple_of(j * ns, 128), ns)]


def _full_shape(shard_view, rowlike):
    _, kh, ns = shard_view.shape
    return (N_CHIPS, 2, kh, ns) if rowlike else (2, kh, N_CHIPS * ns)


def _remote(src, dst, send_sem, recv_sem, dev):
    return pltpu.make_async_remote_copy(src, dst, send_sem, recv_sem, device_id=dev, device_id_type=MESH)


SEM_SPEC = pl.BlockSpec(memory_space=pltpu.SEMAPHORE)
SPLIT_PARAMS = pltpu.CompilerParams(has_side_effects=pltpu.SideEffectType.DATAFLOW_SIDE_EFFECTING)


def _gather_plan(rowlikes):
    def plan(s_refs, f_refs, a, k, x, y, c):
        px, py = _peer_chip(x, y, k)
        return (s_refs[a].at[c], _piece_half(f_refs[a], rowlikes[a], 2 * x + y, c),
                _piece_half(f_refs[a], rowlikes[a], 2 * px + py, c), (px, py, c))
    return plan


def _all_to_all_plan(rowlikes):
    def plan(q_refs, r_refs, a, k, x, y, c):
        px, py = _peer_chip(x, y, k)
        if rowlikes[a]:
            src = q_refs[a].at[2 * px + py]
        else:
            ns = q_refs[a].shape[-1] // N_CHIPS
            src = q_refs[a].at[:, pl.ds(pl.multiple_of((2 * px + py) * ns, 128), ns)]
        return src, r_refs[a].at[k - 1], r_refs[a].at[k - 1], (px, py, c)
    return plan


def _swap_plan(rowlikes):
    def plan(g_refs, got_refs, a, k, x, y, c):
        src = g_refs[a].at[:, 1 - c] if rowlikes[a] else g_refs[a].at[1 - c]
        return src, got_refs[a], got_refs[a], (x, y, 1 - c)
    return plan


def _everyone_plan(s_refs, slot_refs, a, k, x, y, c):
    px, py, pc = x ^ (k >> 2), y ^ ((k >> 1) & 1), c ^ (k & 1)
    return s_refs[a], slot_refs[a].at[4 * x + 2 * y + c], slot_refs[a].at[4 * px + 2 * py + pc], (px, py, pc)


def _forward_plan(rowlikes):
    def plan(s_refs, f_refs, a, k, x, y, c):
        sib = (x, y, 1 - c)
        if k == 0:
            own = _piece(f_refs[a], rowlikes[a], 2 * x + y)
            return s_refs[a], own, own, sib
        px, py = _peer_chip(x, y, k)
        mine = _piece_half(f_refs[a], rowlikes[a], 2 * px + py, c)
        return mine, mine, _piece_half(f_refs[a], rowlikes[a], 2 * px + py, 1 - c), sib
    return plan


CHIPS, SIBLING, EVERYONE, FORWARDS = (1, 2, 3), (1,), tuple(range(1, 8)), (0, 1, 2, 3)


def _as_list(after):
    return [] if after is None else list(after) if isinstance(after, (list, tuple)) else [after]


def exchange_start(srcs, land_shapes, plan, after, name, peers=CHIPS):
    n = len(srcs)
    lands = [l if hasattr(l, "dtype") else lax.empty(l, s.dtype) for l, s in zip(land_shapes, srcs)]

    behind = _as_list(after)

    def body(*refs):
        s_refs, l_refs = refs[:n], refs[n:2 * n]
        send_sems, recv_sems = refs[2 * n + len(behind)], refs[2 * n + len(behind) + 1]
        token = refs[-1]
        x, y, c = _me()
        for a in range(n):
            for i, k in enumerate(peers):
                src, dst, _, peer = plan(s_refs, l_refs, a, k, x, y, c)
                sem = len(peers) * a + i
                _remote(src, dst, send_sems.at[sem], recv_sems.at[sem], peer).start()
        token[...] = jnp.zeros_like(token)

    n_sems = len(peers) * n
    outs = pl.pallas_call(
        body, name=name, in_specs=[HBM_SPEC] * (2 * n) + [pl.BlockSpec(memory_space=pl.ANY)] * len(behind),
        out_specs=[SEM_SPEC, SEM_SPEC] + [HBM_SPEC] * (2 * n) + [pl.BlockSpec(memory_space=pltpu.VMEM)],
        out_shape=[pltpu.SemaphoreType.DMA((n_sems,)), pltpu.SemaphoreType.DMA((n_sems,))]
        + [pltpu.HBM(a.shape, a.dtype) for a in (*srcs, *lands)] + [SDS((8, 128), F32)],
        input_output_aliases={i: 2 + i for i in range(2 * n)},
        compiler_params=SPLIT_PARAMS,
    )(*[pltpu.with_memory_space_constraint(a, pltpu.HBM) for a in (*srcs, *lands)], *behind)
    return {"sems": outs[:2], "srcs": outs[2:2 + n], "lands": outs[2 + n:2 + 2 * n], "token": outs[-1]}


def exchange_wait(state, plan, after, name, peers=CHIPS):
    n = len(state["srcs"])
    behind = _as_list(after)

    def body(*refs):
        s_refs, l_refs = refs[:n], refs[n:2 * n]
        send_sems, recv_sems = refs[2 * n], refs[2 * n + 1]
        x, y, c = _me()
        for a in range(n):
            for i, k in enumerate(peers):
                src, _, landing, peer = plan(s_refs, l_refs, a, k, x, y, c)
                sem = len(peers) * a + i
                cp = _remote(src, landing, send_sems.at[sem], recv_sems.at[sem], peer)
                cp.wait_send()
                cp.wait_recv()

    bufs = (*state["srcs"], *state["lands"])
    outs = pl.pallas_call(
        body, name=name,
        in_specs=[HBM_SPEC] * (2 * n) + [SEM_SPEC, SEM_SPEC] + [pl.BlockSpec(memory_space=pl.ANY)] * len(behind),
        out_specs=[HBM_SPEC] * (2 * n),
        out_shape=[pltpu.HBM(a.shape, a.dtype) for a in bufs],
        input_output_aliases={i: i for i in range(2 * n)},
        compiler_params=SPLIT_PARAMS,
    )(*bufs, *state["sems"], *behind)
    return outs[:n], outs[n:]


def gather_finish(shards, fulls, rowlikes, name):
    n = len(shards)

    def body(*refs):
        s_refs, f_refs = refs[:n], refs[2 * n:3 * n]
        send_sems, recv_sems = refs[3 * n:]
        x, y, c = _me()
        chip = 2 * x + y
        sib = (x, y, 1 - c)
        sends = []
        for a in range(n):
            own = _piece(f_refs[a], rowlikes[a], chip)
            cp = _remote(s_refs[a], own, send_sems.at[a, 0], recv_sems.at[a, 0], sib)
            cp.start()
            sends.append(cp)
            for k in (1, 2, 3):
                px, py = _peer_chip(x, y, k)
                slot = _piece_half(f_refs[a], rowlikes[a], 2 * px + py, c)
                cp = _remote(slot, slot, send_sems.at[a, k], recv_sems.at[a, k], sib)
                cp.start()
                sends.append(cp)
        for a in range(n):
            own = _piece(f_refs[a], rowlikes[a], chip)
            _remote(own, own, send_sems.at[a, 0], recv_sems.at[a, 0], sib).wait_recv()
            for k in (1, 2, 3):
                px, py = _peer_chip(x, y, k)
                slot = _piece_half(f_refs[a], rowlikes[a], 2 * px + py, 1 - c)
                _remote(slot, slot, send_sems.at[a, k], recv_sems.at[a, k], sib).wait_recv()
        for cp in sends:
            cp.wait_send()

    return pl.pallas_call(
        body, name=name, in_specs=[HBM_SPEC] * (2 * n), out_specs=[HBM_SPEC] * n,
        out_shape=[SDS(f.shape, f.dtype) for f in fulls],
        input_output_aliases={n + a: a for a in range(n)},
        scratch_shapes=[pltpu.SemaphoreType.DMA((n, 4)), pltpu.SemaphoreType.DMA((n, 4))],
    )(*shards, *fulls)


def _half_buffer_shape(gview, rowlike):
    return (N_CHIPS,) + gview.shape[2:] if rowlike else gview.shape[1:]


def _row_tiles(kh, ns):
    return 1 if kh * ns <= 256 * 1024 else 2


def _run_jobs(jobs, place, name, after=None):
    n_steps = max(job[0] for job in jobs)

    def spec(block, index, steps):
        return pl.BlockSpec(block, lambda s, p: index(jnp.minimum(s, steps - 1), p))

    in_specs = [spec(b, ix, steps) for steps, ins, _, _ in jobs for _, b, ix in ins]
    out_specs = [spec(b, ix, steps) for steps, _, outs, _ in jobs for _, _, b, ix in outs]
    token_spec, token_arg = _token_operand(after)
    n_in = len(in_specs) + len(token_spec)

    def body(place_ref, *refs):
        s = pl.program_id(0)
        i, o = 0, n_in
        for steps, ins, outs, fn in jobs:
            in_refs, out_refs = refs[i:i + len(ins)], refs[o:o + len(outs)]
            i, o = i + len(ins), o + len(outs)

            @pl.when(s < steps)
            def _(fn=fn, in_refs=in_refs, out_refs=out_refs):
                fn(in_refs, out_refs)

    return pl.pallas_call(
        body, name=name,
        grid_spec=pltpu.PrefetchScalarGridSpec(num_scalar_prefetch=1, grid=(n_steps,),
                                               in_specs=in_specs + token_spec, out_specs=out_specs),
        out_shape=[SDS(shape, dtype) for _, _, outs, _ in jobs for shape, dtype, _, _ in outs],
        compiler_params=_cp(("arbitrary",)),
    )(place, *[a for _, ins, _, _ in jobs for a, _, _ in ins], *token_arg)


def add_halves(gviews, gots, rowlikes, place, name):
    def add(ins, outs):
        outs[0][...] = (ins[0][...].astype(F32) + ins[1][...].astype(F32)).astype(BF16)

    jobs = []
    for g, got, rowlike in zip(gviews, gots, rowlikes):
        kh, ns = (g.shape[2], g.shape[3]) if rowlike else (g.shape[1], g.shape[2] // N_CHIPS)
        rt = _row_tiles(kh, ns)
        tr = kh // rt
        if rowlike:
            g_in = (g, (None, None, tr, ns), lambda ls, p, rt=rt: (ls // rt, p[0], ls % rt, 0))
            half = ((None, tr, ns), lambda ls, p, rt=rt: (ls // rt, ls % rt, 0))
        else:
            g_in = (g, (None, tr, ns), lambda ls, p, rt=rt: (p[0], ls % rt, ls // rt))
            half = ((tr, ns), lambda ls, p, rt=rt: (ls % rt, ls // rt))
        jobs.append((N_CHIPS * rt, [g_in, (got, *half)], [(got.shape, BF16, *half)], add))
    return _run_jobs(jobs, place, name)


def _slot_shape(q, rowlike):
    return (3,) + (q.shape[1:] if rowlike else (q.shape[0], q.shape[1] // N_CHIPS))


def sum_pieces(qs, recvs, rowlikes, place, name):
    def total(ins, outs):
        acc = ins[0][...].astype(F32)
        for k in range(3):
            acc = acc + ins[1][k].astype(F32)
        outs[0][...] = acc

    jobs = []
    for q, recv, rowlike in zip(qs, recvs, rowlikes):
        kh, ns = recv.shape[1], recv.shape[2]
        rt = _row_tiles(kh, ns)
        tr = kh // rt
        mine = ((None, tr, ns), lambda ls, p: (p[1], ls, 0)) if rowlike else ((tr, ns), lambda ls, p: (ls, p[1]))
        jobs.append((rt, [(q, *mine), (recv, (3, tr, ns), lambda ls, p: (0, ls, 0))],
                     [((2, kh, ns), F32, (None, tr, ns), lambda ls, p: (p[0], ls, 0))], total))
    return _run_jobs(jobs, place, name)


def join_halves(halves, name):
    n = len(halves)

    def body(*refs):
        o_refs = refs[n:2 * n]
        send_sems, recv_sems = refs[2 * n:]
        x, y, c = _me()
        sib = (x, y, 1 - c)
        sends = []
        for a in range(n):
            cp = _remote(o_refs[a].at[c], o_refs[a].at[c], send_sems.at[a], recv_sems.at[a], sib)
            cp.start()
            sends.append(cp)
        for a in range(n):
            got = o_refs[a].at[1 - c]
            _remote(got, got, send_sems.at[a], recv_sems.at[a], sib).wait_recv()
        for cp in sends:
            cp.wait_send()

    return pl.pallas_call(
        body, name=name, in_specs=[HBM_SPEC] * n, out_specs=[HBM_SPEC] * n,
        out_shape=[SDS(h.shape, h.dtype) for h in halves],
        input_output_aliases={a: a for a in range(n)},
        scratch_shapes=[pltpu.SemaphoreType.DMA((n,)), pltpu.SemaphoreType.DMA((n,))],
    )(*halves)


N_DEV = 8


def sum_devices(s, slots, me):
    R, Wd = s.shape

    def body(me_ref, s_ref, slots_ref, out_ref):
        acc = None
        for d in range(N_DEV):
            mine = me_ref[0] == d
            term = jnp.where(mine, s_ref[...], slots_ref[jnp.where(mine, d ^ 1, d)])
            acc = term if acc is None else acc + term
        out_ref[...] = acc

    vmem = pl.BlockSpec(memory_space=pltpu.VMEM)
    return pl.pallas_call(
        body, name="sum_devices", in_specs=[pl.BlockSpec(memory_space=pltpu.SMEM), vmem, vmem], out_specs=vmem,
        out_shape=SDS((R, Wd), F32),
    )(me, s, slots)


TRANSPOSED = ("w_in",)
BIG = {"ffn1_w_up": (D, 2 * FF, "col"), "ffn1_w_down": (FF, D, "row"), "w_in": (INW, D, "row"),
       "w_attn_up": (AW, D, "row"), "w_pool_up": (PW, D, "col"), "w_out": (D, D, "row"),
       "ffn2_w_up": (D, 2 * FF, "col"), "ffn2_w_down": (FF, D, "row")}
GROUPS = (("ffn1_w_up",), ("ffn1_w_down", "w_in", "w_attn_up", "w_pool_up", "w_out"), ("ffn2_w_up", "ffn2_w_down"))
SMALL = ("ffn1_norm", "mix_norm", "ffn2_norm", "final_norm", "pool_scale", "sinks", "pool_w_mix")
SMALL_W = 128


def _rowlike(name):
    return BIG[name][2] == "row"


def _half_dims(name):
    k, n, kind = BIG[name]
    return (k // N_CHIPS // 2, n) if kind == "row" else (k // 2, n // N_CHIPS)


def shard_view(name, shard):
    return shard.reshape((2,) + _half_dims(name))


def full_from_view(name, fv):
    k, n, _ = BIG[name]
    return fv.reshape(k, n)


def grad_view(name, g):
    kh, ns = _half_dims(name)
    return g.reshape(_full_shape(jax.ShapeDtypeStruct((2, kh, ns), g.dtype), _rowlike(name)))


def pack_small(d):
    parts = []
    for name in SMALL:
        a = d[name].reshape(-1)
        pad = (-a.shape[0]) % SMALL_W
        parts.append(jnp.pad(a, (0, pad)).reshape(-1, SMALL_W))
    a = jnp.concatenate(parts, axis=0)
    return jnp.pad(a, ((0, (-a.shape[0]) % 8), (0, 0)))


def unpack_small(a, like):
    out, r0 = {}, 0
    for name in SMALL:
        size = int(np.prod(like[name].shape))
        rows = -(-size // SMALL_W)
        out[name] = a[r0:r0 + rows].reshape(-1)[:size].reshape(like[name].shape)
        r0 += rows
    return out


WEIGHTS = ("ffn1_norm", "ffn1_w_up", "ffn1_w_down", "mix_norm", "w_in", "sinks", "w_attn_up", "pool_w_mix",
           "pool_scale", "w_pool_up", "w_out", "ffn2_norm", "ffn2_w_up", "ffn2_w_down", "final_norm")


def kernel(x, ffn1_norm, ffn1_w_up, ffn1_w_down, mix_norm, w_in, sinks, w_attn_up, pool_w_mix, pool_scale, w_pool_up, w_out, ffn2_norm, ffn2_w_up, ffn2_w_down, final_norm, loss_target, m_ffn1_norm, m_ffn1_w_up, m_ffn1_w_down, m_mix_norm, m_w_in, m_sinks, m_w_attn_up, m_pool_w_mix, m_pool_scale, m_w_pool_up, m_w_out, m_ffn2_norm, m_ffn2_w_up, m_ffn2_w_down, m_final_norm, v_ffn1_norm, v_ffn1_w_up, v_ffn1_w_down, v_mix_norm, v_w_in, v_sinks, v_w_attn_up, v_pool_w_mix, v_pool_scale, v_w_pool_up, v_w_out, v_ffn2_norm, v_ffn2_w_up, v_ffn2_w_down, v_final_norm):
    given = dict(locals())
    w = {n: given[n] for n in WEIGHTS}
    m = {n: given["m_" + n] for n in WEIGHTS}
    v = {n: given["v_" + n] for n in WEIGHTS}
    cx, cy, cc = _me()
    place = jnp.stack([cc, 2 * cx + cy]).astype(jnp.int32)

    def local2d(d, n):
        return d[n][0].T if n in TRANSPOSED else d[n][0]

    shards = {n: local2d(w, n) for n in BIG}
    grads, delta, new_m, new_v = {}, {}, {}, {}
    rowlikes = [[_rowlike(n) for n in names] for names in GROUPS]

    class Exchanges:
        def __init__(self):
            self.gathers, self.forwards, self.fulls, self.reductions, self.small = {}, {}, {}, {}, None

        def _behind_first(self, a):
            return a + self.gathers[0]["token"][0, 0]

        def _start_gather(self, group, after):
            sv = [self.cast[n] if group else shard_view(n, shards[n].astype(BF16)) for n in GROUPS[group]]
            rl = rowlikes[group]
            self.gathers[group] = exchange_start(sv, [_full_shape(s, r) for s, r in zip(sv, rl)], _gather_plan(rl),
                                                 after, f"gather_start_{group}")

        def weights(self, group, after):
            if not self.gathers:
                self._start_gather(0, None)
                self.packed = [self._behind_first(pack_small(d)) for d in (w, m, v)]
                self.cast = {n: shard_view(n, self._behind_first(shards[n]).astype(BF16))
                             for names in GROUPS[1:] for n in names}
                after = [self.gathers[0]["token"], *self.packed, *self.cast.values()]
            if group not in self.fulls:
                names, rl, state = GROUPS[group], rowlikes[group], self.gathers[group]
                if group in self.forwards:
                    _, fulls = exchange_wait(self.forwards.pop(group), _forward_plan(rl), after,
                                             f"forward_wait_{group}", FORWARDS)
                else:
                    sv, fulls = exchange_wait(state, _gather_plan(rl), state["token"] if after is None else after,
                                              f"gather_wait_{group}")
                    fulls = gather_finish(sv, fulls, rl, f"gather_finish_{group}")
                self.fulls[group] = {n: full_from_view(n, f) for n, f in zip(names, fulls)}
                if group + 1 < len(GROUPS):
                    self._start_gather(group + 1, fulls[0])
            return self.fulls[group]

        def started(self):
            return self.gathers[max(self.gathers)]["token"]

        def prefetch(self, group, after):
            rl, state = rowlikes[group], self.gathers[group]
            sv, fulls = exchange_wait(state, _gather_plan(rl), after, f"gather_wait_{group}")
            self.forwards[group] = exchange_start(sv, fulls, _forward_plan(rl), None, f"forward_start_{group}", FORWARDS)
            return self.forwards[group]["token"]

        def grads(self, g):
            names = tuple(g)
            rl = [_rowlike(n) for n in names]
            gv = [grad_view(n, g[n]) for n in names]
            state = exchange_start(gv, [_half_buffer_shape(a, r) for a, r in zip(gv, rl)], _swap_plan(rl), None,
                                   "swap_start_" + names[0], SIBLING)
            self.reductions[names] = state
            return state["token"]

        def advance(self, names, after):
            rl = [_rowlike(n) for n in names]
            gv, gots = exchange_wait(self.reductions[names], _swap_plan(rl), after, "swap_wait_" + names[0], SIBLING)
            qs = add_halves(gv, gots, rl, place, "add_halves_" + names[0])
            state = exchange_start(qs, [_slot_shape(q, r) for q, r in zip(qs, rl)], _all_to_all_plan(rl), None,
                                   "all_to_all_start_" + names[0])
            self.reductions[names] = state
            return state["token"]

        def reduce(self, names, after):
            rl = [_rowlike(n) for n in names]
            qs, recvs = exchange_wait(self.reductions.pop(names), _all_to_all_plan(rl), after,
                                      "all_to_all_wait_" + names[0])
            halves = sum_pieces(qs, recvs, rl, place, "sum_pieces_" + names[0])
            return [(shards[n], o.reshape(shards[n].shape), local2d(m, n), local2d(v, n))
                    for n, o in zip(names, join_halves(halves, "join_halves_" + names[0]))]

        def updated(self, names, outs):
            for i, n in enumerate(names):
                grads[n], delta[n], new_m[n], new_v[n] = outs[4 * i:4 * i + 4]
            return [new_v[n] for n in names]

        def update(self, names, updates, token=None):
            return self.updated(names, adamw(updates, place, "adamw_" + names[0], after=token))

        def finish(self, names, after):
            return self.update(names, self.reduce(names, after))

        def small_start(self, G, loss, after):
            packed = pack_small({n: G[n] for n in SMALL})
            used_rows = sum(-(-int(np.prod(small_like[n].shape)) // SMALL_W) for n in SMALL)
            assert packed.shape[0] > used_rows
            packed = packed.at[-1, 0].set(loss[0, 0])
            self.small = exchange_start([packed], [(N_DEV,) + packed.shape], _everyone_plan, after,
                                        "small_start", EVERYONE)
            return self.small["token"]

        def small_finish(self, after):
            (packed,), (slots,) = exchange_wait(self.small, _everyone_plan, after, "small_wait", EVERYONE)
            total = sum_devices(packed, slots, (4 * cx + 2 * cy + cc).astype(jnp.int32).reshape(1))
            self.loss = total[-1, 0]
            g, ds, ms, vs = adamw([(self.packed[0], total, self.packed[1], self.packed[2])], place, "adamw_small")
            for d, packed_d in ((grads, g), (delta, ds), (new_m, ms), (new_v, vs)):
                d.update(unpack_small(packed_d, small_like))
            return vs

    small_like = {n: w[n] for n in SMALL}
    S = {n: w[n].reshape(1, -1) for n in ("ffn1_norm", "mix_norm", "ffn2_norm", "final_norm", "pool_scale", "sinks")}
    S["pool_w_mix"] = w["pool_w_mix"][0].astype(BF16)
    exchanges = Exchanges()
    _, dx = local_fwd_bwd(x[0], loss_target[0], S, exchanges)
    loss = exchanges.loss

    def shaped(d, n):
        return (d[n].T if n in TRANSPOSED else d[n]).reshape(w[n].shape)

    return (loss, dx[None], *[shaped(grads, n) for n in WEIGHTS], *[shaped(delta, n) for n in WEIGHTS],
            *[shaped(new_m, n) for n in WEIGHTS], *[shaped(new_v, n) for n in WEIGHTS])
```

```python
import numpy as np
import jax
import jax.numpy as jnp
from jax import lax
from jax.experimental import pallas as pl
from jax.experimental.pallas import tpu as pltpu

F32 = jnp.float32
BF16 = jnp.bfloat16
SDS = jax.ShapeDtypeStruct
MESH = pl.DeviceIdType.MESH

D = 1024
FF = 2816
NQ = 16
NKV = 2
HD = 64
GQ = NQ // NKV
AW = NQ * HD
KVW = NKV * HD
BLK = 128
PW = 512
PG = 128
POOL_WINDOWS = (2, 4, 8, 16)
HALO = 16
INW = AW + 2 * KVW + PW + 2 * D
C_KV = AW
C_Z = AW + 2 * KVW
C_G = C_Z + PW
EPS = 1e-6
FF_CHUNK = 256
FF_CHUNKS = tuple((c, FF_CHUNK) for c in range(0, FF, FF_CHUNK))
SLOPES = tuple(float(2.0 ** (-8.0 * h / NQ)) for h in range(1, NQ + 1))
SCALE = HD ** -0.5

LR, B1, B2, ADAM_EPS, WD, STEP = 0.001, 0.9, 0.999, 1e-08, 0.01, 10

VMEM_LIMIT = 56 * 1024 * 1024
N_CHIPS = 4

NT = (((1,), (1,)), ((), ()))
TN = (((0,), (0,)), ((), ()))


def _cp(sem=None, vmem=VMEM_LIMIT):
    return pltpu.CompilerParams(dimension_semantics=sem, vmem_limit_bytes=vmem)


def _const_spec(shape):
    nd = len(shape)
    return pl.BlockSpec(shape, lambda *_: (0,) * nd, pipeline_mode=pl.Buffered(1))


def _rstd(x):
    return lax.rsqrt(jnp.mean(x * x, axis=-1, keepdims=True) + EPS)


def _rms_bwd(dn, xhat, rstd, g):
    dxhat = dn * g
    return rstd * (dxhat - xhat * jnp.mean(dxhat * xhat, axis=-1, keepdims=True))


def _sigmoid(x):
    return 0.5 * jnp.tanh(0.5 * x) + 0.5


def _dot(a, b):
    return jnp.dot(a, b, preferred_element_type=F32)


def _dot_nt(a, b):
    return lax.dot_general(a, b, NT, preferred_element_type=F32)


def _dot_tn(a, b):
    return lax.dot_general(a, b, TN, preferred_element_type=F32)


def _swiglu_up(x, g, wup_ref, ab_ref, hid_ref):
    n = (x * _rstd(x) * g).astype(BF16)
    for c0, w in FF_CHUNKS:
        a = _dot(n, wup_ref[:, c0:c0 + w])
        b = _dot(n, wup_ref[:, FF + c0:FF + c0 + w])
        sig = _sigmoid(a)
        s = a * sig
        ab_ref[:, c0:c0 + w] = (b * (sig + s * (1.0 - sig))).astype(BF16)
        ab_ref[:, FF + c0:FF + c0 + w] = s.astype(BF16)
        hid_ref[:, c0:c0 + w] = (s * b).astype(BF16)


def ffn_up(h, g, wup, name, after=None):
    T = h.shape[0]
    TM = 512
    tile = lambda w: pl.BlockSpec((TM, w), lambda i: (i, 0))
    token_spec, token_arg = _token_operand(after)

    def body(h_ref, g_ref, wup_ref, *rest):
        _swiglu_up(h_ref[...], g_ref[...], wup_ref, *rest[-2:])

    return pl.pallas_call(
        body, name=name, grid=(T // TM,),
        in_specs=[tile(D), _const_spec((1, D)), _const_spec((D, 2 * FF))] + token_spec,
        out_specs=[tile(2 * FF), tile(FF)],
        out_shape=[SDS((T, 2 * FF), BF16), SDS((T, FF), BF16)],
        compiler_params=_cp(("arbitrary",)),
    )(h, g, wup, *token_arg)


def ffn_down(h, hid, wdn, name, after=None):
    T = h.shape[0]
    TM = min(1024, T)
    tile = lambda w: pl.BlockSpec((TM, w), lambda i: (i, 0))
    token_spec, token_arg = _token_operand(after)

    def body(h_ref, hid_ref, wdn_ref, *rest):
        rest[-1][...] = h_ref[...] + 0.5 * _dot(hid_ref[...], wdn_ref[...])

    return pl.pallas_call(
        body, name=name, grid=(T // TM,),
        in_specs=[tile(D), tile(FF), _const_spec((FF, D))] + token_spec,
        out_specs=tile(D), out_shape=SDS((T, D), F32),
        compiler_params=_cp(("arbitrary",)),
    )(h, hid, wdn, *token_arg)


def ffn_fwd_loss(h, g, wup, wdn, target, g_final, name):
    T = h.shape[0]
    TM = 512
    tile = lambda w: pl.BlockSpec((TM, w), lambda i: (i, 0))
    acc_spec = lambda w: pl.BlockSpec((1, w), lambda i: (0, 0))

    def body(h_ref, g_ref, wup_ref, wdn_ref, t_ref, gf_ref, dout_ref, ab_ref, hid_ref, loss_ref, dgf_ref):
        x = h_ref[...]
        _swiglu_up(x, g_ref[...], wup_ref, ab_ref, hid_ref)
        out = x + 0.5 * _dot(hid_ref[...], wdn_ref[...])
        dout_ref[...] = _loss_head(out, t_ref[...], gf_ref[...], loss_ref, dgf_ref, pl.program_id(0) == 0)

    return pl.pallas_call(
        body, name=name, grid=(T // TM,),
        in_specs=[tile(D), _const_spec((1, D)), _const_spec((D, 2 * FF)), _const_spec((FF, D)), tile(D),
                  _const_spec((1, D))],
        out_specs=[tile(D), tile(2 * FF), tile(FF), acc_spec(1), acc_spec(D)],
        out_shape=[SDS((T, D), F32), SDS((T, 2 * FF), BF16), SDS((T, FF), BF16), SDS((1, 1), F32), SDS((1, D), F32)],
        compiler_params=_cp(("arbitrary",)),
    )(h, g, wup, wdn, target, g_final)


def _loss_head(x, target, g, loss_ref, dg_ref, first):
    @pl.when(first)
    def _():
        loss_ref[...] = jnp.zeros_like(loss_ref)
        dg_ref[...] = jnp.zeros_like(dg_ref)

    rstd = _rstd(x)
    xhat = x * rstd
    err = xhat * g - target
    loss_ref[...] += 0.5 * jnp.sum(jnp.mean(err * err, axis=-1, keepdims=True), axis=0, keepdims=True)
    dy = err * (1.0 / D)
    dg_ref[...] += jnp.sum(dy * xhat, axis=0, keepdims=True)
    return _rms_bwd(dy, xhat, rstd, g)


def _adamw_tile(w_ref, g_ref, m_ref, v_ref, go_ref, d_ref, nm_ref, nv_ref):
    gv = g_ref[...]
    go_ref[...] = gv
    nm = B1 * m_ref[...] + (1.0 - B1) * gv
    nv = B2 * v_ref[...] + (1.0 - B2) * (gv * gv)
    nm_ref[...] = nm
    nv_ref[...] = nv
    d_ref[...] = -LR * ((nm / (1.0 - B1 ** STEP)) / (jnp.sqrt(nv / (1.0 - B2 ** STEP)) + ADAM_EPS) + WD * w_ref[...])


def _riders(updates, steps):
    in_specs, out_specs, out_shapes, operands, tiles = [], [], [], [], []
    for w, g, m, v in updates:
        R, C = w.shape
        n = max(d for d in range(1, steps + 1) if R % d == 0 and (R // d) % 8 == 0)
        spec = pl.BlockSpec((R // n, C), lambda i, n=n: (jnp.minimum(i, n - 1), 0))
        in_specs += [spec] * 4
        out_specs += [spec] * 4
        out_shapes += [SDS((R, C), F32)] * 4
        operands += [w, g, m, v]
        tiles.append(n)

    def run(step, in_refs, out_refs):
        for u, n in enumerate(tiles):
            @pl.when(step < n)
            def _(u=u):
                _adamw_tile(*in_refs[4 * u:4 * u + 4], *out_refs[4 * u:4 * u + 4])

    return in_specs, out_specs, out_shapes, operands, run


def ffn_bwd_x(dh, h_in, ab, g, wup, wdn, name, updates=(), after=None):
    T = dh.shape[0]
    TM = 256 if updates else 512
    SUB = 256
    r_in, r_out, r_shapes, r_args, ride = _riders(updates, T // TM)
    token_spec, token_arg = _token_operand(after)
    n_in = len(r_in) + len(token_spec)

    def body(dh_ref, h_ref, ab_ref, g_ref, wup_ref, wdn_ref, *rest):
        dhin_ref, dup_ref, n_ref, dg_ref = rest[n_in:n_in + 4]
        ride(pl.program_id(0), rest[:len(r_in)], rest[n_in + 4:])
        g = g_ref[...]
        dg = None
        for r0 in range(0, TM, SUB):
            rows = slice(r0, r0 + SUB)
            x = h_ref[rows, :]
            rstd = _rstd(x)
            xhat = x * rstd
            n_ref[rows, :] = (xhat * g).astype(BF16)
            dh = dh_ref[rows, :]
            dhh = (0.5 * dh).astype(BF16)
            for c0, w in FF_CHUNKS:
                dhid = _dot_nt(dhh, wdn_ref[c0:c0 + w, :]).astype(BF16)
                dup_ref[rows, c0:c0 + w] = dhid * ab_ref[rows, c0:c0 + w]
                dup_ref[rows, FF + c0:FF + c0 + w] = dhid * ab_ref[rows, FF + c0:FF + c0 + w]
            dn = _dot_nt(dup_ref[rows, :], wup_ref[...])
            dhin_ref[rows, :] = dh + _rms_bwd(dn, xhat, rstd, g)
            part = jnp.sum(dn * xhat, axis=0, keepdims=True)
            dg = part if dg is None else dg + part

        @pl.when(pl.program_id(0) == 0)
        def _():
            dg_ref[...] = jnp.zeros_like(dg_ref)

        dg_ref[...] += dg

    tile = lambda w: pl.BlockSpec((TM, w), lambda i: (i, 0))
    return pl.pallas_call(
        body, name=name, grid=(T // TM,),
        in_specs=[tile(D), tile(D), tile(2 * FF), _const_spec((1, D)), _const_spec((D, 2 * FF)), _const_spec((FF, D))]
        + r_in + token_spec,
        out_specs=[tile(D), tile(2 * FF), tile(D), pl.BlockSpec((1, D), lambda i: (0, 0))] + r_out,
        out_shape=[SDS((T, D), F32), SDS((T, 2 * FF), BF16), SDS((T, D), BF16), SDS((1, D), F32)] + r_shapes,
        compiler_params=_cp(("arbitrary",)),
    )(dh, h_in, ab, g, wup, wdn, *r_args, *token_arg)


TOKEN_SPEC = pl.BlockSpec((8, 128), lambda *_: (0, 0))


def _token_operand(after):
    tokens = [t for t in (after if isinstance(after, tuple) else (after,)) if t is not None]
    return [TOKEN_SPEC] * len(tokens), tokens


def matmul_tn(a, b, name, *, tm, tn, tt=2048, b_scale=None, after=None):
    T, M = a.shape
    N = b.shape[1]
    tt = min(tt, T)
    assert M % tm == 0 and N % tn == 0 and T % tt == 0
    nt = T // tt
    token_spec, token_arg = _token_operand(after)

    def body(a_ref, b_ref, *rest):
        o_ref, acc_ref = rest[-2:]
        t = pl.program_id(2)

        @pl.when(t == 0)
        def _():
            acc_ref[...] = jnp.zeros_like(acc_ref)

        bv = b_ref[...]
        if b_scale is not None:
            bv = bv * b_scale
        acc_ref[...] += _dot_tn(a_ref[...].astype(BF16), bv.astype(BF16))

        @pl.when(t == nt - 1)
        def _():
            o_ref[...] = acc_ref[...].astype(BF16)

    return pl.pallas_call(
        body, name=name, grid=(M // tm, N // tn, nt),
        in_specs=[pl.BlockSpec((tt, tm), lambda i, j, t: (t, i)), pl.BlockSpec((tt, tn), lambda i, j, t: (t, j))]
        + token_spec,
        out_specs=pl.BlockSpec((tm, tn), lambda i, j, t: (i, j)),
        out_shape=SDS((M, N), BF16),
        scratch_shapes=[pltpu.VMEM((tm, tn), F32)],
        compiler_params=_cp(("parallel", "parallel", "arbitrary")),
    )(a, b, *token_arg)


def mix_in_fwd(h1, g, win_t):
    T = h1.shape[0]
    TM = 512

    def body(h_ref, g_ref, w_ref, u_ref, q_ref, kv_ref, z_ref, gt_ref):
        x = h_ref[...]
        u = (x * _rstd(x) * g_ref[...]).astype(BF16)
        u_ref[...] = u
        for c in range(0, AW, 256):
            q_ref[:, c:c + 256] = _dot_nt(u, w_ref[c:c + 256, :]).astype(BF16)
        kv_ref[...] = _dot_nt(u, w_ref[C_KV:C_Z, :]).astype(BF16)
        for c in range(0, PW, 256):
            z_ref[:, c:c + 256] = _dot_nt(u, w_ref[C_Z + c:C_Z + c + 256, :])
        for c in range(0, 2 * D, 256):
            gt_ref[:, c:c + 256] = _dot_nt(u, w_ref[C_G + c:C_G + c + 256, :]).astype(BF16)

    tile = lambda w: pl.BlockSpec((TM, w), lambda i: (i, 0))
    return pl.pallas_call(
        body, name="mix_in_fwd", grid=(T // TM,),
        in_specs=[tile(D), _const_spec((1, D)), _const_spec((INW, D))],
        out_specs=[tile(D), tile(AW), tile(2 * KVW), tile(PW), tile(2 * D)],
        out_shape=[SDS((T, D), BF16), SDS((T, AW), BF16), SDS((T, 2 * KVW), BF16), SDS((T, PW), F32),
                   SDS((T, 2 * D), BF16)],
        compiler_params=_cp(("arbitrary",)),
    )(h1, g, win_t)


def mix_in_bwd(dq, dkv, dz, dgt, dh2, h1, g, win_t):
    T = h1.shape[0]
    TM = min(1024, T)
    SUB = 256

    def body(dq_ref, dkv_ref, dz_ref, dgt_ref, dh2_ref, h_ref, g_ref, w_ref, dh1_ref, dg_ref):
        g = g_ref[...]
        dg = None
        for r0 in range(0, TM, SUB):
            rows = slice(r0, r0 + SUB)
            du = _dot(dq_ref[rows, :], w_ref[0:AW, :])
            du += _dot(dkv_ref[rows, :], w_ref[C_KV:C_Z, :])
            du += _dot(dz_ref[rows, :], w_ref[C_Z:C_G, :])
            du += _dot(dgt_ref[rows, :], w_ref[C_G:INW, :])
            x = h_ref[rows, :]
            rstd = _rstd(x)
            xhat = x * rstd
            dh1_ref[rows, :] = dh2_ref[rows, :] + _rms_bwd(du, xhat, rstd, g)
            part = jnp.sum(du * xhat, axis=0, keepdims=True)
            dg = part if dg is None else dg + part

        @pl.when(pl.program_id(0) == 0)
        def _():
            dg_ref[...] = jnp.zeros_like(dg_ref)

        dg_ref[...] += dg

    tile = lambda w: pl.BlockSpec((TM, w), lambda i: (i, 0))
    return pl.pallas_call(
        body, name="mix_in_bwd", grid=(T // TM,),
        in_specs=[tile(AW), tile(2 * KVW), tile(PW), tile(2 * D), tile(D), tile(D), _const_spec((1, D)),
                  _const_spec((INW, D))],
        out_specs=[tile(D), pl.BlockSpec((1, D), lambda i: (0, 0))],
        out_shape=[SDS((T, D), F32), SDS((1, D), F32)],
        compiler_params=_cp(("arbitrary",)),
    )(dq, dkv, dz, dgt, dh2, h1, g, win_t)


PAIR = 2 * HD
NPAIR = GQ // 2


def _lo_lanes():
    return lax.broadcasted_iota(jnp.int32, (BLK, PAIR), 1) < HD


def _stack_heads(ref, kvh, scale=None):
    lo = _lo_lanes()
    parts = []
    for pr in range(NPAIR):
        t = ref[:, (kvh * NPAIR + pr) * PAIR:(kvh * NPAIR + pr + 1) * PAIR]
        if scale is not None:
            t = t * scale
        zero = jnp.zeros_like(t)
        parts += [jnp.where(lo, t, zero), jnp.where(lo, zero, t)]
    return jnp.concatenate(parts, axis=0)


def _kv_tiles(kvc_ref, kvp_ref, tile, kvh):
    lo = _lo_lanes()
    dup, left, right = [], [], []
    for ref in (kvp_ref, kvc_ref):
        t = ref[:, tile * PAIR:(tile + 1) * PAIR]
        r = pltpu.roll(t.astype(F32), HD, 1).astype(BF16)
        zero = jnp.zeros_like(t)
        a, b = (t, r) if kvh == 0 else (r, t)
        dup.append(jnp.where(lo, a, b))
        left.append(jnp.where(lo, a, zero))
        right.append(jnp.where(lo, zero, b))
    cat = lambda xs: jnp.concatenate(xs, axis=0)
    return cat(dup), cat(left), cat(right)


def _band_consts(first):
    row = lax.broadcasted_iota(jnp.int32, (BLK, BLK), 0)
    col = lax.broadcasted_iota(jnp.int32, (BLK, BLK), 1)
    upper = col > row
    dist = jnp.where(upper, row - col + BLK, row - col).astype(F32)
    pen = jnp.where(jnp.logical_and(upper, first), -jnp.inf, 0.0)
    return upper, dist, pen


def _split_band(upper, t):
    zero = jnp.zeros_like(t)
    return jnp.concatenate([jnp.where(upper, t, zero), jnp.where(upper, zero, t)], axis=1)


def attn_fwd(q, kv, sinks):
    T = q.shape[0]
    nb = T // BLK

    def body(sink_ref, q_ref, kvc_ref, kvp_ref, att_ref, lse_ref):
        upper, dist, pen = _band_consts(pl.program_id(0) == 0)
        scores, values = [], []
        for kvh in range(NKV):
            kdup, _, _ = _kv_tiles(kvc_ref, kvp_ref, 0, kvh)
            values.append(_kv_tiles(kvc_ref, kvp_ref, 1, kvh)[1:])
            scores.append(_dot_nt(_stack_heads(q_ref, kvh, SCALE), kdup))
        for kvh in range(NKV):
            s_all = scores[kvh]
            vleft, vright = values[kvh]
            for pr in range(NPAIR):
                outs, inv = [], []
                for side, vpad in ((0, vleft), (1, vright)):
                    g = 2 * pr + side
                    hq = kvh * GQ + g
                    sink = sink_ref[0, hq]
                    rows = slice(g * BLK, (g + 1) * BLK)
                    s = jnp.where(upper, s_all[rows, 0:BLK], s_all[rows, BLK:2 * BLK]) - SLOPES[hq] * dist + pen
                    m = jnp.maximum(jnp.max(s, axis=-1, keepdims=True), sink)
                    p = jnp.exp(s - m)
                    l = jnp.sum(p, axis=-1, keepdims=True) + jnp.exp(sink - m)
                    lse_ref[:, hq:hq + 1] = m + jnp.log(l)
                    outs.append(_dot(_split_band(upper, p.astype(BF16)), vpad))
                    inv.append(1.0 / l)
                col0 = (kvh * NPAIR + pr) * PAIR
                att_ref[:, col0:col0 + PAIR] = ((outs[0] + outs[1]) * jnp.where(_lo_lanes(), inv[0], inv[1])).astype(BF16)

    return pl.pallas_call(
        body, name="attn_fwd", grid=(nb,),
        in_specs=[pl.BlockSpec(memory_space=pltpu.SMEM),
                  pl.BlockSpec((BLK, AW), lambda i: (i, 0)),
                  pl.BlockSpec((BLK, 2 * KVW), lambda i: (i, 0)),
                  pl.BlockSpec((BLK, 2 * KVW), lambda i: (jnp.maximum(i - 1, 0), 0))],
        out_specs=[pl.BlockSpec((BLK, AW), lambda i: (i, 0)), pl.BlockSpec((BLK, NQ), lambda i: (i, 0))],
        out_shape=[SDS((T, AW), BF16), SDS((T, NQ), F32)],
        compiler_params=_cp(("arbitrary",)),
    )(sinks, q, kv, kv)


def attn_bwd(q, kv, datt, att, lse, sinks, after=None):
    T = q.shape[0]
    nb = T // BLK
    token_spec, token_arg = _token_operand(after)

    def body(sink_ref, q_ref, kvc_ref, kvp_ref, do_ref, out_ref, lse_ref, *rest):
        dq_ref, dkv_ref, dsink_ref, carry_ref = rest[-4:]
        i = pl.program_id(0)

        @pl.when(i == 0)
        def _():
            dsink_ref[...] = jnp.zeros_like(dsink_ref)
            carry_ref[...] = jnp.zeros_like(carry_ref)

        @pl.when(i < nb)
        def _():
            upper, dist, pen = _band_consts(i == 0)
            lo = _lo_lanes()
            dk_dup, dv_dup = [], []
            staged = []
            for kvh in range(NKV):
                kdup, kleft, kright = _kv_tiles(kvc_ref, kvp_ref, 0, kvh)
                vdup, _, _ = _kv_tiles(kvc_ref, kvp_ref, 1, kvh)
                qs = _stack_heads(q_ref, kvh, SCALE)
                dos = _stack_heads(do_ref, kvh)
                staged.append((kleft, kright, qs, dos, _dot_nt(qs, kdup), _dot_nt(dos, vdup)))
            deltas = []
            for pair in range(NQ // 2):
                cols = slice(pair * PAIR, (pair + 1) * PAIR)
                t = do_ref[:, cols].astype(F32) * out_ref[:, cols].astype(F32)
                deltas += [jnp.sum(jnp.where(lo, t, 0.0), axis=-1, keepdims=True),
                           jnp.sum(jnp.where(lo, 0.0, t), axis=-1, keepdims=True)]
            for kvh in range(NKV):
                kleft, kright, qs, dos, s_all, dp_all = staged[kvh]
                ds_parts, p_parts = [], []
                for pr in range(NPAIR):
                    dq = None
                    for side, kpad in ((0, kleft), (1, kright)):
                        g = 2 * pr + side
                        hq = kvh * GQ + g
                        lse_h = lse_ref[:, hq:hq + 1]
                        rows = slice(g * BLK, (g + 1) * BLK)
                        s = jnp.where(upper, s_all[rows, 0:BLK], s_all[rows, BLK:2 * BLK]) - SLOPES[hq] * dist + pen
                        p = jnp.exp(s - lse_h)
                        dp = jnp.where(upper, dp_all[rows, 0:BLK], dp_all[rows, BLK:2 * BLK])
                        delta = deltas[hq]
                        dsink_ref[:, hq:hq + 1] += -jnp.sum(jnp.exp(sink_ref[0, hq] - lse_h) * delta, axis=0,
                                                            keepdims=True)
                        ds = _split_band(upper, (p * (dp - delta)).astype(BF16))
                        ds_parts.append(ds)
                        p_parts.append(_split_band(upper, p.astype(BF16)))
                        d = _dot(ds, kpad)
                        dq = d if dq is None else dq + d
                    col0 = (kvh * NPAIR + pr) * PAIR
                    dq_ref[:, col0:col0 + PAIR] = (dq * SCALE).astype(BF16)
                dkw = _dot_tn(qs, jnp.concatenate(ds_parts, axis=0)).T
                dvw = _dot_tn(dos, jnp.concatenate(p_parts, axis=0)).T
                dk_dup.append(dkw + pltpu.roll(dkw, HD, 1))
                dv_dup.append(dvw + pltpu.roll(dvw, HD, 1))
            dk = jnp.where(jnp.concatenate([lo, lo], axis=0), dk_dup[0], dk_dup[1])
            dv = jnp.where(jnp.concatenate([lo, lo], axis=0), dv_dup[0], dv_dup[1])
            dkv_ref[:, 0:PAIR] = (carry_ref[:, 0:PAIR] + dk[0:BLK]).astype(BF16)
            dkv_ref[:, PAIR:2 * PAIR] = (carry_ref[:, PAIR:2 * PAIR] + dv[0:BLK]).astype(BF16)
            carry_ref[:, 0:PAIR] = dk[BLK:2 * BLK]
            carry_ref[:, PAIR:2 * PAIR] = dv[BLK:2 * BLK]

        @pl.when(i == nb)
        def _():
            dkv_ref[...] = carry_ref[...].astype(BF16)

    cur = lambda i: (jnp.minimum(i, nb - 1), 0)
    prev = lambda i: (jnp.maximum(jnp.minimum(i, nb - 1) - 1, 0), 0)
    return pl.pallas_call(
        body, name="attn_bwd", grid=(nb + 1,),
        in_specs=[pl.BlockSpec(memory_space=pltpu.SMEM),
                  pl.BlockSpec((BLK, AW), cur), pl.BlockSpec((BLK, 2 * KVW), cur), pl.BlockSpec((BLK, 2 * KVW), prev),
                  pl.BlockSpec((BLK, AW), cur), pl.BlockSpec((BLK, AW), cur), pl.BlockSpec((BLK, NQ), cur)] + token_spec,
        out_specs=[pl.BlockSpec((BLK, AW), cur),
                   pl.BlockSpec((BLK, 2 * KVW), lambda i: (jnp.maximum(i - 1, 0), 0)),
                   pl.BlockSpec((1, NQ), lambda i: (0, 0))],
        out_shape=[SDS((T, AW), BF16), SDS((T, 2 * KVW), BF16), SDS((1, NQ), F32)],
        scratch_shapes=[pltpu.VMEM((BLK, 2 * KVW), F32)],
        compiler_params=_cp(("arbitrary",)),
    )(sinks, q, kv, kv, datt, att, lse, *token_arg)


def _inv_counts(t0, rows):
    t = (t0 + lax.broadcasted_iota(jnp.int32, (rows, 1), 0) + 1).astype(F32)
    return [1.0 / jnp.minimum(t, float(w)) for w in POOL_WINDOWS]


def pool_fwd(z, wmix, scale, after=None):
    T = z.shape[0]
    TM = min(1024, T)
    L = TM + HALO
    token_spec, token_arg = _token_operand(after)

    def body(z_ref, halo_ref, wmix_ref, scale_ref, *rest):
        pooled_ref, mixs_ref = rest[-2:]
        i = pl.program_id(0)
        halo = jnp.where(i > 0, halo_ref[...], 0.0)
        zt = z_ref[...]
        e = jnp.concatenate([halo, zt], axis=0)
        sums = []
        s = e
        for k in (1, 2, 4, 8):
            s = s + pltpu.roll(s, k, 0)
            sums.append(s)
        inv = _inv_counts(i * TM, TM)
        for gi in range(len(POOL_WINDOWS)):
            cols = slice(gi * PG, (gi + 1) * PG)
            pooled = (sums[gi][HALO:, cols] * inv[gi] - zt[:, cols]).astype(BF16)
            pooled_ref[:, cols] = pooled
            mixs_ref[:, cols] = (_dot(pooled, wmix_ref[gi]) * scale_ref[:, cols]).astype(BF16)

    return pl.pallas_call(
        body, name="pool_fwd", grid=(T // TM,),
        in_specs=[pl.BlockSpec((TM, PW), lambda i: (i, 0)),
                  pl.BlockSpec((HALO, PW), lambda i: (jnp.maximum(i * (TM // HALO) - 1, 0), 0)),
                  _const_spec((len(POOL_WINDOWS), PG, PG)), _const_spec((1, PW))] + token_spec,
        out_specs=[pl.BlockSpec((TM, PW), lambda i: (i, 0)), pl.BlockSpec((TM, PW), lambda i: (i, 0))],
        out_shape=[SDS((T, PW), BF16), SDS((T, PW), BF16)],
        compiler_params=_cp(("arbitrary",)),
    )(z, z, wmix, scale, *token_arg)


def pool_bwd(dmixs, pooled, wmix, scale, after=None):
    T = dmixs.shape[0]
    TM = min(1024, T)
    L = TM + HALO
    nt = T // TM
    token_spec, token_arg = _token_operand(after)

    def body(dm_ref, halo_ref, pooled_ref, wmix_ref, scale_ref, *rest):
        dz_ref, dwmix_ref, dscale_ref = rest[-3:]
        i = pl.program_id(0)

        @pl.when(i == 0)
        def _():
            dwmix_ref[...] = jnp.zeros_like(dwmix_ref)
            dscale_ref[...] = jnp.zeros_like(dscale_ref)

        halo = jnp.where(i < nt - 1, halo_ref[...], 0.0)
        dm = dm_ref[...]
        e = jnp.concatenate([dm, halo], axis=0)
        inv = _inv_counts(i * TM, L)
        for gi in range(len(POOL_WINDOWS)):
            cols = slice(gi * PG, (gi + 1) * PG)
            w = wmix_ref[gi]
            dmixed = (e[:, cols] * scale_ref[:, cols]).astype(BF16)
            dpooled = _dot_nt(dmixed, w)
            pooled = pooled_ref[:, cols]
            mixed = _dot(pooled, w)
            dscale_ref[:, cols] += jnp.sum(dm[:, cols] * mixed, axis=0, keepdims=True)
            dwmix_ref[gi] += _dot_tn(pooled, dmixed[:TM])
            s = dpooled * inv[gi]
            k = 1
            while k < POOL_WINDOWS[gi]:
                s = s + pltpu.roll(s, L - k, 0)
                k *= 2
            dz_ref[:, cols] = (s[:TM] - dpooled[:TM]).astype(BF16)

    return pl.pallas_call(
        body, name="pool_bwd", grid=(nt,),
        in_specs=[pl.BlockSpec((TM, PW), lambda i: (i, 0)),
                  pl.BlockSpec((HALO, PW), lambda i: (jnp.minimum((i + 1) * (TM // HALO), T // HALO - 1), 0)),
                  pl.BlockSpec((TM, PW), lambda i: (i, 0)),
                  _const_spec((len(POOL_WINDOWS), PG, PG)), _const_spec((1, PW))] + token_spec,
        out_specs=[pl.BlockSpec((TM, PW), lambda i: (i, 0)),
                   pl.BlockSpec((len(POOL_WINDOWS), PG, PG), lambda i: (0, 0, 0)),
                   pl.BlockSpec((1, PW), lambda i: (0, 0))],
        out_shape=[SDS((T, PW), BF16), SDS((len(POOL_WINDOWS), PG, PG), F32), SDS((1, PW), F32)],
        compiler_params=_cp(("arbitrary",)),
    )(dmixs, dmixs, pooled, wmix, scale, *token_arg)


def merge_fwd(att, mixs, gt, h1, wattn, wpool, wout):
    T = h1.shape[0]
    TM = 512

    def body(att_ref, mixs_ref, gt_ref, h_ref, wa_ref, wp_ref, wo_ref, h2_ref, mg_ref, gf_ref):
        a = _dot(att_ref[...], wa_ref[...])
        p = _dot(mixs_ref[...], wp_ref[...])
        sa = _sigmoid(gt_ref[:, 0:D].astype(F32))
        sp = _sigmoid(gt_ref[:, D:2 * D].astype(F32))
        gf_ref[:, 0:D] = (a * sa * (1.0 - sa)).astype(BF16)
        gf_ref[:, D:2 * D] = (p * sp * (1.0 - sp)).astype(BF16)
        mg = (sa * a + sp * p).astype(BF16)
        mg_ref[...] = mg
        h2_ref[...] = h_ref[...] + _dot(mg, wo_ref[...])

    tile = lambda w: pl.BlockSpec((TM, w), lambda i: (i, 0))
    return pl.pallas_call(
        body, name="merge_fwd", grid=(T // TM,),
        in_specs=[tile(AW), tile(PW), tile(2 * D), tile(D), _const_spec((AW, D)), _const_spec((PW, D)),
                  _const_spec((D, D))],
        out_specs=[tile(D), tile(D), tile(2 * D)],
        out_shape=[SDS((T, D), F32), SDS((T, D), BF16), SDS((T, 2 * D), BF16)],
        compiler_params=_cp(("arbitrary",)),
    )(att, mixs, gt, h1, wattn, wpool, wout)


def merge_bwd(dh2, gt, gf, wattn, wpool, wout, after=None):
    T = dh2.shape[0]
    TM = 512
    token_spec, token_arg = _token_operand(after)

    def body(dh2_ref, gt_ref, gf_ref, wa_ref, wp_ref, wo_ref, *rest):
        datt_ref, dmixs_ref, dgt_ref, da_ref, dp_ref = rest[-5:]
        dm = _dot_nt(dh2_ref[...].astype(BF16), wo_ref[...])
        da = (dm * _sigmoid(gt_ref[:, 0:D].astype(F32))).astype(BF16)
        dp = (dm * _sigmoid(gt_ref[:, D:2 * D].astype(F32))).astype(BF16)
        da_ref[...] = da
        dp_ref[...] = dp
        dgt_ref[:, 0:D] = (dm * gf_ref[:, 0:D].astype(F32)).astype(BF16)
        dgt_ref[:, D:2 * D] = (dm * gf_ref[:, D:2 * D].astype(F32)).astype(BF16)
        datt_ref[...] = _dot_nt(da, wa_ref[...]).astype(BF16)
        dmixs_ref[...] = _dot_nt(dp, wp_ref[...])

    tile = lambda w: pl.BlockSpec((TM, w), lambda i: (i, 0))
    return pl.pallas_call(
        body, name="merge_bwd", grid=(T // TM,),
        in_specs=[tile(D), tile(2 * D), tile(2 * D), _const_spec((AW, D)), _const_spec((PW, D)),
                  _const_spec((D, D))] + token_spec,
        out_specs=[tile(AW), tile(PW), tile(2 * D), tile(D), tile(D)],
        out_shape=[SDS((T, AW), BF16), SDS((T, PW), F32), SDS((T, 2 * D), BF16), SDS((T, D), BF16),
                   SDS((T, D), BF16)],
        compiler_params=_cp(("arbitrary",)),
    )(dh2, gt, gf, wattn, wpool, wout, *token_arg)


def adamw(updates, place, name, after=None):
    tile_bytes = 2 * 1024 * 1024 // len(updates)
    jobs = []
    for w, g, m, v in updates:
        R, C = w.shape
        tr = R
        if R * C * 4 > tile_bytes:
            tr = next(cand for cand in (512, 256, 128, 64, 32, 16, 8) if R % cand == 0 and cand * C * 4 <= tile_bytes)
        blk = ((tr, C), lambda ls, p: (ls, 0))
        jobs.append((R // tr, [(a, *blk) for a in (w, g, m, v)], [((R, C), F32, *blk)] * 4,
                     lambda ins, outs: _adamw_tile(*ins, *outs)))
    return _run_jobs(jobs, place, name, after)


GROUP_UP1, GROUP_MIX, GROUP_FFN2 = 0, 1, 2


def local_fwd_bwd(x, target, S, comm):
    w_up1 = comm.weights(GROUP_UP1, None)["ffn1_w_up"]
    ab1, hid1 = ffn_up(x, S["ffn1_norm"], w_up1, "ffn1_up", after=comm.started())
    W = comm.weights(GROUP_MIX, hid1)
    h1 = ffn_down(x, hid1, W["ffn1_w_down"], "ffn1_down", after=comm.started())
    u, q, kv, z, gt = mix_in_fwd(h1, S["mix_norm"], W["w_in"])
    att, lse = attn_fwd(q, kv, S["sinks"])
    pooled, mixs = pool_fwd(z, S["pool_w_mix"], S["pool_scale"], after=comm.prefetch(GROUP_FFN2, att))
    h2, merged, gf = merge_fwd(att, mixs, gt, h1, W["w_attn_up"], W["w_pool_up"], W["w_out"])
    W = comm.weights(GROUP_FFN2, h2)
    dh3, ab2, hid2, loss, g_final = ffn_fwd_loss(h2, S["ffn2_norm"], W["ffn2_w_up"], W["ffn2_w_down"], target,
                                                 S["final_norm"], "ffn2_fwd")

    G = {"final_norm": g_final}
    dh2, dup2, n2, G["ffn2_norm"] = ffn_bwd_x(dh3, h2, ab2, S["ffn2_norm"], W["ffn2_w_up"], W["ffn2_w_down"],
                                              "ffn2_bwd_x")
    ffn2 = ("ffn2_w_down", "ffn2_w_up")
    token = comm.grads({"ffn2_w_down": matmul_tn(hid2, dh3, "ffn2_dw_down", tm=1408, tn=512, tt=4096, b_scale=0.5),
                        "ffn2_w_up": matmul_tn(n2, dup2, "ffn2_dw_up", tm=D, tn=512, tt=4096)})

    W = comm.weights(GROUP_MIX, None)
    datt, dmixs, dgt, da, dp = merge_bwd(dh2, gt, gf, W["w_attn_up"], W["w_pool_up"], W["w_out"], after=token)
    token = comm.advance(ffn2, datt)
    g_mix = {"w_out": matmul_tn(merged, dh2, "dw_out", tm=D, tn=D, after=token),
             "w_attn_up": matmul_tn(att, da, "dw_attn_up", tm=AW, tn=D),
             "w_pool_up": matmul_tn(mixs, dp, "dw_pool_up", tm=PW, tn=D)}
    dz, G["pool_w_mix"], G["pool_scale"] = pool_bwd(dmixs, pooled, S["pool_w_mix"], S["pool_scale"], after=token)
    dq, dkv, G["sinks"] = attn_bwd(q, kv, datt, att, lse, S["sinks"], after=token)
    dh1, G["mix_norm"] = mix_in_bwd(dq, dkv, dz, dgt, dh2, h1, S["mix_norm"], W["w_in"])
    g_mix["w_in"] = jnp.concatenate([
        matmul_tn(dq, u, "dw_in_q", tm=AW, tn=D),
        matmul_tn(dkv, u, "dw_in_kv", tm=2 * KVW, tn=D),
        matmul_tn(dz, u, "dw_in_z", tm=PW, tn=D),
        matmul_tn(dgt, u, "dw_in_g", tm=D, tn=512, tt=4096),
    ], axis=0)
    mix = tuple(g_mix)
    token = comm.grads(g_mix)

    g_dn1 = matmul_tn(hid1, dh1, "ffn1_dw_down", tm=1408, tn=512, tt=4096, b_scale=0.5, after=token)
    token = comm.advance(mix, g_dn1)
    token2 = comm.grads({"ffn1_w_down": g_dn1})
    dx, dup1, n1, G["ffn1_norm"], *outs = ffn_bwd_x(dh1, x, ab1, S["ffn1_norm"], w_up1, W["ffn1_w_down"], "ffn1_bwd_x",
                                                   updates=comm.reduce(ffn2, token2), after=(token, token2))
    comm.updated(ffn2, outs)
    token = comm.advance(("ffn1_w_down",), dx)
    token2 = comm.small_start(G, loss, dx)
    g_up1 = matmul_tn(n1, dup1, "ffn1_dw_up", tm=D, tn=512, tt=4096, after=(token, token2))
    token = comm.grads({"ffn1_w_up": g_up1})
    updates = comm.reduce(("ffn1_w_down",), token)
    token = comm.advance(("ffn1_w_up",), [u[1] for u in updates])
    done = comm.update(("ffn1_w_down",), updates, token)
    done = comm.finish(mix, done)
    done = comm.small_finish(done)
    comm.finish(("ffn1_w_up",), done)
    return loss, dx


HBM_SPEC = pl.BlockSpec(memory_space=pltpu.HBM)


def _me():
    return lax.axis_index("x"), lax.axis_index("y"), lax.axis_index("c")


def _peer_chip(x, y, k):
    return x ^ (k >> 1), y ^ (k & 1)


def _piece_half(ref, rowlike, j, h):
    if rowlike:
        return ref.at[j, h]
    ns = ref.shape[-1] // N_CHIPS
    return ref.at[h, :, pl.ds(pl.multiple_of(j * ns, 128), ns)]


def _piece(ref, rowlike, j):
    if rowlike:
        return ref.at[j]
    ns = ref.shape[-1] // N_CHIPS
    return ref.at[:, :, pl.ds(pl.multiple_of(j * ns, 128), ns)]


def _full_shape(shard_view, rowlike):
    _, kh, ns = shard_view.shape
    return (N_CHIPS, 2, kh, ns) if rowlike else (2, kh, N_CHIPS * ns)


def _remote(src, dst, send_sem, recv_sem, dev):
    return pltpu.make_async_remote_copy(src, dst, send_sem, recv_sem, device_id=dev, device_id_type=MESH)


SEM_SPEC = pl.BlockSpec(memory_space=pltpu.SEMAPHORE)
SPLIT_PARAMS = pltpu.CompilerParams(has_side_effects=pltpu.SideEffectType.DATAFLOW_SIDE_EFFECTING)


def _gather_plan(rowlikes):
    def plan(s_refs, f_refs, a, k, x, y, c):
        if k == 0:
            own = _piece(f_refs[a], rowlikes[a], 2 * x + y)
            return s_refs[a], own, own, (x, y, 1 - c)
        px, py = _peer_chip(x, y, k)
        return (s_refs[a].at[c], _piece_half(f_refs[a], rowlikes[a], 2 * x + y, c),
                _piece_half(f_refs[a], rowlikes[a], 2 * px + py, c), (px, py, c))
    return plan


def _all_to_all_plan(rowlikes):
    def plan(q_refs, r_refs, a, k, x, y, c):
        px, py = _peer_chip(x, y, k)
        if rowlikes[a]:
            src = q_refs[a].at[2 * px + py]
        else:
            ns = q_refs[a].shape[-1] // N_CHIPS
            src = q_refs[a].at[:, pl.ds(pl.multiple_of((2 * px + py) * ns, 128), ns)]
        return src, r_refs[a].at[k - 1], r_refs[a].at[k - 1], (px, py, c)
    return plan


def _swap_plan(rowlikes):
    def plan(g_refs, got_refs, a, k, x, y, c):
        src = g_refs[a].at[:, 1 - c] if rowlikes[a] else g_refs[a].at[1 - c]
        return src, got_refs[a], got_refs[a], (x, y, 1 - c)
    return plan


def _everyone_plan(s_refs, slot_refs, a, k, x, y, c):
    px, py, pc = x ^ (k >> 2), y ^ ((k >> 1) & 1), c ^ (k & 1)
    return s_refs[a], slot_refs[a].at[4 * x + 2 * y + c], slot_refs[a].at[4 * px + 2 * py + pc], (px, py, pc)


def _forward_plan(rowlikes):
    def plan(s_refs, f_refs, a, k, x, y, c):
        px, py = _peer_chip(x, y, k)
        mine = _piece_half(f_refs[a], rowlikes[a], 2 * px + py, c)
        return mine, mine, _piece_half(f_refs[a], rowlikes[a], 2 * px + py, 1 - c), (x, y, 1 - c)
    return plan


CHIPS, SIBLING, EVERYONE, GATHERS = (1, 2, 3), (1,), tuple(range(1, 8)), (0, 1, 2, 3)


def _as_list(after):
    return [] if after is None else list(after) if isinstance(after, (list, tuple)) else [after]


def exchange_start(srcs, land_shapes, plan, after, name, peers=CHIPS):
    n = len(srcs)
    lands = [l if hasattr(l, "dtype") else lax.empty(l, s.dtype) for l, s in zip(land_shapes, srcs)]

    behind = _as_list(after)

    def body(*refs):
        s_refs, l_refs = refs[:n], refs[n:2 * n]
        send_sems, recv_sems = refs[2 * n + len(behind)], refs[2 * n + len(behind) + 1]
        token = refs[-1]
        x, y, c = _me()
        for a in range(n):
            for i, k in enumerate(peers):
                src, dst, _, peer = plan(s_refs, l_refs, a, k, x, y, c)
                sem = len(peers) * a + i
                _remote(src, dst, send_sems.at[sem], recv_sems.at[sem], peer).start()
        token[...] = jnp.zeros_like(token)

    n_sems = len(peers) * n
    outs = pl.pallas_call(
        body, name=name, in_specs=[HBM_SPEC] * (2 * n) + [pl.BlockSpec(memory_space=pl.ANY)] * len(behind),
        out_specs=[SEM_SPEC, SEM_SPEC] + [HBM_SPEC] * (2 * n) + [pl.BlockSpec(memory_space=pltpu.VMEM)],
        out_shape=[pltpu.SemaphoreType.DMA((n_sems,)), pltpu.SemaphoreType.DMA((n_sems,))]
        + [pltpu.HBM(a.shape, a.dtype) for a in (*srcs, *lands)] + [SDS((8, 128), F32)],
        input_output_aliases={i: 2 + i for i in range(2 * n)},
        compiler_params=SPLIT_PARAMS,
    )(*[pltpu.with_memory_space_constraint(a, pltpu.HBM) for a in (*srcs, *lands)], *behind)
    return {"sems": outs[:2], "srcs": outs[2:2 + n], "lands": outs[2 + n:2 + 2 * n], "token": outs[-1]}


def exchange_wait(state, plan, after, name, peers=CHIPS):
    n = len(state["srcs"])
    behind = _as_list(after)

    def body(*refs):
        s_refs, l_refs = refs[:n], refs[n:2 * n]
        send_sems, recv_sems = refs[2 * n], refs[2 * n + 1]
        x, y, c = _me()
        for a in range(n):
            for i, k in enumerate(peers):
                src, _, landing, peer = plan(s_refs, l_refs, a, k, x, y, c)
                sem = len(peers) * a + i
                cp = _remote(src, landing, send_sems.at[sem], recv_sems.at[sem], peer)
                cp.wait_send()
                cp.wait_recv()

    bufs = (*state["srcs"], *state["lands"])
    outs = pl.pallas_call(
        body, name=name,
        in_specs=[HBM_SPEC] * (2 * n) + [SEM_SPEC, SEM_SPEC] + [pl.BlockSpec(memory_space=pl.ANY)] * len(behind),
        out_specs=[HBM_SPEC] * (2 * n),
        out_shape=[pltpu.HBM(a.shape, a.dtype) for a in bufs],
        input_output_aliases={i: i for i in range(2 * n)},
        compiler_params=SPLIT_PARAMS,
    )(*bufs, *state["sems"], *behind)
    return outs[:n], outs[n:]


def gather_finish(fulls, rowlikes, name):
    n = len(fulls)

    def body(*refs):
        f_refs = refs[n:2 * n]
        send_sems, recv_sems = refs[2 * n:]
        x, y, c = _me()
        sib = (x, y, 1 - c)
        sends = []
        for a in range(n):
            for k in CHIPS:
                px, py = _peer_chip(x, y, k)
                slot = _piece_half(f_refs[a], rowlikes[a], 2 * px + py, c)
                cp = _remote(slot, slot, send_sems.at[a, k - 1], recv_sems.at[a, k - 1], sib)
                cp.start()
                sends.append(cp)
        for a in range(n):
            for k in CHIPS:
                px, py = _peer_chip(x, y, k)
                slot = _piece_half(f_refs[a], rowlikes[a], 2 * px + py, 1 - c)
                _remote(slot, slot, send_sems.at[a, k - 1], recv_sems.at[a, k - 1], sib).wait_recv()
        for cp in sends:
            cp.wait_send()

    return pl.pallas_call(
        body, name=name, in_specs=[HBM_SPEC] * n, out_specs=[HBM_SPEC] * n,
        out_shape=[SDS(f.shape, f.dtype) for f in fulls],
        input_output_aliases={a: a for a in range(n)},
        scratch_shapes=[pltpu.SemaphoreType.DMA((n, len(CHIPS))), pltpu.SemaphoreType.DMA((n, len(CHIPS)))],
    )(*fulls)


def _half_buffer_shape(gview, rowlike):
    return (N_CHIPS,) + gview.shape[2:] if rowlike else gview.shape[1:]


def _row_tiles(kh, ns):
    return 1 if kh * ns <= 256 * 1024 else 2


def _run_jobs(jobs, place, name, after=None):
    n_steps = max(job[0] for job in jobs)

    def spec(block, index, steps):
        return pl.BlockSpec(block, lambda s, p: index(jnp.minimum(s, steps - 1), p))

    in_specs = [spec(b, ix, steps) for steps, ins, _, _ in jobs for _, b, ix in ins]
    out_specs = [spec(b, ix, steps) for steps, _, outs, _ in jobs for _, _, b, ix in outs]
    token_spec, token_arg = _token_operand(after)
    n_in = len(in_specs) + len(token_spec)

    def body(place_ref, *refs):
        s = pl.program_id(0)
        i, o = 0, n_in
        for steps, ins, outs, fn in jobs:
            in_refs, out_refs = refs[i:i + len(ins)], refs[o:o + len(outs)]
            i, o = i + len(ins), o + len(outs)

            @pl.when(s < steps)
            def _(fn=fn, in_refs=in_refs, out_refs=out_refs):
                fn(in_refs, out_refs)

    return pl.pallas_call(
        body, name=name,
        grid_spec=pltpu.PrefetchScalarGridSpec(num_scalar_prefetch=1, grid=(n_steps,),
                                               in_specs=in_specs + token_spec, out_specs=out_specs),
        out_shape=[SDS(shape, dtype) for _, _, outs, _ in jobs for shape, dtype, _, _ in outs],
        compiler_params=_cp(("arbitrary",)),
    )(place, *[a for _, ins, _, _ in jobs for a, _, _ in ins], *token_arg)


def add_halves(gviews, gots, rowlikes, place, name):
    def add(ins, outs):
        outs[0][...] = (ins[0][...].astype(F32) + ins[1][...].astype(F32)).astype(BF16)

    jobs = []
    for g, got, rowlike in zip(gviews, gots, rowlikes):
        kh, ns = (g.shape[2], g.shape[3]) if rowlike else (g.shape[1], g.shape[2] // N_CHIPS)
        rt = _row_tiles(kh, ns)
        tr = kh // rt
        if rowlike:
            g_in = (g, (None, None, tr, ns), lambda ls, p, rt=rt: (ls // rt, p[0], ls % rt, 0))
            half = ((None, tr, ns), lambda ls, p, rt=rt: (ls // rt, ls % rt, 0))
        else:
            g_in = (g, (None, tr, ns), lambda ls, p, rt=rt: (p[0], ls % rt, ls // rt))
            half = ((tr, ns), lambda ls, p, rt=rt: (ls % rt, ls // rt))
        jobs.append((N_CHIPS * rt, [g_in, (got, *half)], [(got.shape, BF16, *half)], add))
    return _run_jobs(jobs, place, name)


def _slot_shape(q, rowlike):
    return (3,) + (q.shape[1:] if rowlike else (q.shape[0], q.shape[1] // N_CHIPS))


def sum_pieces(qs, recvs, rowlikes, place, name):
    def total(ins, outs):
        acc = ins[0][...].astype(F32)
        for k in range(3):
            acc = acc + ins[1][k].astype(F32)
        outs[0][...] = acc

    jobs = []
    for q, recv, rowlike in zip(qs, recvs, rowlikes):
        kh, ns = recv.shape[1], recv.shape[2]
        rt = _row_tiles(kh, ns)
        tr = kh // rt
        mine = ((None, tr, ns), lambda ls, p: (p[1], ls, 0)) if rowlike else ((tr, ns), lambda ls, p: (ls, p[1]))
        jobs.append((rt, [(q, *mine), (recv, (3, tr, ns), lambda ls, p: (0, ls, 0))],
                     [((2, kh, ns), F32, (None, tr, ns), lambda ls, p: (p[0], ls, 0))], total))
    return _run_jobs(jobs, place, name)


def join_halves(halves, name):
    n = len(halves)

    def body(*refs):
        o_refs = refs[n:2 * n]
        send_sems, recv_sems = refs[2 * n:]
        x, y, c = _me()
        sib = (x, y, 1 - c)
        sends = []
        for a in range(n):
            cp = _remote(o_refs[a].at[c], o_refs[a].at[c], send_sems.at[a], recv_sems.at[a], sib)
            cp.start()
            sends.append(cp)
        for a in range(n):
            got = o_refs[a].at[1 - c]
            _remote(got, got, send_sems.at[a], recv_sems.at[a], sib).wait_recv()
        for cp in sends:
            cp.wait_send()

    return pl.pallas_call(
        body, name=name, in_specs=[HBM_SPEC] * n, out_specs=[HBM_SPEC] * n,
        out_shape=[SDS(h.shape, h.dtype) for h in halves],
        input_output_aliases={a: a for a in range(n)},
        scratch_shapes=[pltpu.SemaphoreType.DMA((n,)), pltpu.SemaphoreType.DMA((n,))],
    )(*halves)


N_DEV = 8


def sum_devices(s, slots, me):
    R, Wd = s.shape

    def body(me_ref, s_ref, slots_ref, out_ref):
        acc = None
        for d in range(N_DEV):
            mine = me_ref[0] == d
            term = jnp.where(mine, s_ref[...], slots_ref[jnp.where(mine, d ^ 1, d)])
            acc = term if acc is None else acc + term
        out_ref[...] = acc

    vmem = pl.BlockSpec(memory_space=pltpu.VMEM)
    return pl.pallas_call(
        body, name="sum_devices", in_specs=[pl.BlockSpec(memory_space=pltpu.SMEM), vmem, vmem], out_specs=vmem,
        out_shape=SDS((R, Wd), F32),
    )(me, s, slots)


TRANSPOSED = ("w_in",)
BIG = {"ffn1_w_up": (D, 2 * FF, "col"), "ffn1_w_down": (FF, D, "row"), "w_in": (INW, D, "row"),
       "w_attn_up": (AW, D, "row"), "w_pool_up": (PW, D, "col"), "w_out": (D, D, "row"),
       "ffn2_w_up": (D, 2 * FF, "col"), "ffn2_w_down": (FF, D, "row")}
GROUPS = (("ffn1_w_up",), ("ffn1_w_down", "w_in", "w_attn_up", "w_pool_up", "w_out"), ("ffn2_w_up", "ffn2_w_down"))
SMALL = ("ffn1_norm", "mix_norm", "ffn2_norm", "final_norm", "pool_scale", "sinks", "pool_w_mix")
SMALL_W = 128


def _rowlike(name):
    return BIG[name][2] == "row"


def _half_dims(name):
    k, n, kind = BIG[name]
    return (k // N_CHIPS // 2, n) if kind == "row" else (k // 2, n // N_CHIPS)


def shard_view(name, shard):
    return shard.reshape((2,) + _half_dims(name))


def full_from_view(name, fv):
    k, n, _ = BIG[name]
    return fv.reshape(k, n)


def grad_view(name, g):
    kh, ns = _half_dims(name)
    return g.reshape(_full_shape(jax.ShapeDtypeStruct((2, kh, ns), g.dtype), _rowlike(name)))


def pack_small(d):
    parts = []
    for name in SMALL:
        a = d[name].reshape(-1)
        pad = (-a.shape[0]) % SMALL_W
        parts.append(jnp.pad(a, (0, pad)).reshape(-1, SMALL_W))
    a = jnp.concatenate(parts, axis=0)
    return jnp.pad(a, ((0, (-a.shape[0]) % 8), (0, 0)))


def unpack_small(a, like):
    out, r0 = {}, 0
    for name in SMALL:
        size = int(np.prod(like[name].shape))
        rows = -(-size // SMALL_W)
        out[name] = a[r0:r0 + rows].reshape(-1)[:size].reshape(like[name].shape)
        r0 += rows
    return out


WEIGHTS = ("ffn1_norm", "ffn1_w_up", "ffn1_w_down", "mix_norm", "w_in", "sinks", "w_attn_up", "pool_w_mix",
           "pool_scale", "w_pool_up", "w_out", "ffn2_norm", "ffn2_w_up", "ffn2_w_down", "final_norm")


def kernel(x, ffn1_norm, ffn1_w_up, ffn1_w_down, mix_norm, w_in, sinks, w_attn_up, pool_w_mix, pool_scale, w_pool_up, w_out, ffn2_norm, ffn2_w_up, ffn2_w_down, final_norm, loss_target, m_ffn1_norm, m_ffn1_w_up, m_ffn1_w_down, m_mix_norm, m_w_in, m_sinks, m_w_attn_up, m_pool_w_mix, m_pool_scale, m_w_pool_up, m_w_out, m_ffn2_norm, m_ffn2_w_up, m_ffn2_w_down, m_final_norm, v_ffn1_norm, v_ffn1_w_up, v_ffn1_w_down, v_mix_norm, v_w_in, v_sinks, v_w_attn_up, v_pool_w_mix, v_pool_scale, v_w_pool_up, v_w_out, v_ffn2_norm, v_ffn2_w_up, v_ffn2_w_down, v_final_norm):
    given = dict(locals())
    w = {n: given[n] for n in WEIGHTS}
    m = {n: given["m_" + n] for n in WEIGHTS}
    v = {n: given["v_" + n] for n in WEIGHTS}
    cx, cy, cc = _me()
    place = jnp.stack([cc, 2 * cx + cy]).astype(jnp.int32)

    def local2d(d, n):
        return d[n][0].T if n in TRANSPOSED else d[n][0]

    shards = {n: local2d(w, n) for n in BIG}
    grads, delta, new_m, new_v = {}, {}, {}, {}
    rowlikes = [[_rowlike(n) for n in names] for names in GROUPS]

    class Exchanges:
        def __init__(self):
            self.gathers, self.forwards, self.fulls, self.reductions, self.small = {}, {}, {}, {}, None

        def _behind_first(self, a):
            return a + self.gathers[0]["token"][0, 0]

        def _start_gather(self, group, after):
            sv = [self.cast[n] if group else shard_view(n, shards[n].astype(BF16)) for n in GROUPS[group]]
            rl = rowlikes[group]
            self.gathers[group] = exchange_start(sv, [_full_shape(s, r) for s, r in zip(sv, rl)], _gather_plan(rl),
                                                 after, f"gather_start_{group}", GATHERS)

        def weights(self, group, after):
            if not self.gathers:
                self._start_gather(0, None)
                self.packed = [self._behind_first(pack_small(d)) for d in (w, m, v)]
                self.cast = {n: shard_view(n, self._behind_first(shards[n]).astype(BF16))
                             for names in GROUPS[1:] for n in names}
                after = [self.gathers[0]["token"], *self.packed, *self.cast.values()]
            if group not in self.fulls:
                names, rl, state = GROUPS[group], rowlikes[group], self.gathers[group]
                if group in self.forwards:
                    _, fulls = exchange_wait(self.forwards.pop(group), _forward_plan(rl), after,
                                             f"forward_wait_{group}")
                else:
                    _, fulls = exchange_wait(state, _gather_plan(rl), state["token"] if after is None else after,
                                             f"gather_wait_{group}", GATHERS)
                    fulls = gather_finish(fulls, rl, f"gather_finish_{group}")
                self.fulls[group] = {n: full_from_view(n, f) for n, f in zip(names, fulls)}
                if group + 1 < len(GROUPS):
                    self._start_gather(group + 1, fulls[0])
            return self.fulls[group]

        def started(self):
            return self.gathers[max(self.gathers)]["token"]

        def prefetch(self, group, after):
            rl, state = rowlikes[group], self.gathers[group]
            sv, fulls = exchange_wait(state, _gather_plan(rl), after, f"gather_wait_{group}", GATHERS)
            self.forwards[group] = exchange_start(sv, fulls, _forward_plan(rl), None, f"forward_start_{group}")
            return self.forwards[group]["token"]

        def grads(self, g):
            names = tuple(g)
            rl = [_rowlike(n) for n in names]
            gv = [grad_view(n, g[n]) for n in names]
            state = exchange_start(gv, [_half_buffer_shape(a, r) for a, r in zip(gv, rl)], _swap_plan(rl), None,
                                   "swap_start_" + names[0], SIBLING)
            self.reductions[names] = state
            return state["token"]

        def advance(self, names, after):
            rl = [_rowlike(n) for n in names]
            gv, gots = exchange_wait(self.reductions[names], _swap_plan(rl), after, "swap_wait_" + names[0], SIBLING)
            qs = add_halves(gv, gots, rl, place, "add_halves_" + names[0])
            state = exchange_start(qs, [_slot_shape(q, r) for q, r in zip(qs, rl)], _all_to_all_plan(rl), None,
                                   "all_to_all_start_" + names[0])
            self.reductions[names] = state
            return state["token"]

        def reduce(self, names, after):
            rl = [_rowlike(n) for n in names]
            qs, recvs = exchange_wait(self.reductions.pop(names), _all_to_all_plan(rl), after,
                                      "all_to_all_wait_" + names[0])
            halves = sum_pieces(qs, recvs, rl, place, "sum_pieces_" + names[0])
            return [(shards[n], o.reshape(shards[n].shape), local2d(m, n), local2d(v, n))
                    for n, o in zip(names, join_halves(halves, "join_halves_" + names[0]))]

        def updated(self, names, outs):
            for i, n in enumerate(names):
                grads[n], delta[n], new_m[n], new_v[n] = outs[4 * i:4 * i + 4]
            return [new_v[n] for n in names]

        def update(self, names, updates, token=None):
            return self.updated(names, adamw(updates, place, "adamw_" + names[0], after=token))

        def finish(self, names, after):
            return self.update(names, self.reduce(names, after))

        def small_start(self, G, loss, after):
            packed = pack_small({n: G[n] for n in SMALL})
            used_rows = sum(-(-int(np.prod(small_like[n].shape)) // SMALL_W) for n in SMALL)
            assert packed.shape[0] > used_rows
            packed = packed.at[-1, 0].set(loss[0, 0])
            self.small = exchange_start([packed], [(N_DEV,) + packed.shape], _everyone_plan, after,
                                        "small_start", EVERYONE)
            return self.small["token"]

        def small_finish(self, after):
            (packed,), (slots,) = exchange_wait(self.small, _everyone_plan, after, "small_wait", EVERYONE)
            total = sum_devices(packed, slots, (4 * cx + 2 * cy + cc).astype(jnp.int32).reshape(1))
            self.loss = total[-1, 0]
            g, ds, ms, vs = adamw([(self.packed[0], total, self.packed[1], self.packed[2])], place, "adamw_small")
            for d, packed_d in ((grads, g), (delta, ds), (new_m, ms), (new_v, vs)):
                d.update(unpack_small(packed_d, small_like))
            return vs

    small_like = {n: w[n] for n in SMALL}
    S = {n: w[n].reshape(1, -1) for n in ("ffn1_norm", "mix_norm", "ffn2_norm", "final_norm", "pool_scale", "sinks")}
    S["pool_w_mix"] = w["pool_w_mix"][0].astype(BF16)
    exchanges = Exchanges()
    _, dx = local_fwd_bwd(x[0], loss_target[0], S, exchanges)
    loss = exchanges.loss

    def shaped(d, n):
        return (d[n].T if n in TRANSPOSED else d[n]).reshape(w[n].shape)

    return (loss, dx[None], *[shaped(grads, n) for n in WEIGHTS], *[shaped(delta, n) for n in WEIGHTS],
            *[shaped(new_m, n) for n in WEIGHTS], *[shaped(new_v, n) for n in WEIGHTS])
```

```python
import numpy as np
import jax
import jax.numpy as jnp
from jax import lax
from jax.experimental import pallas as pl
from jax.experimental.pallas import tpu as pltpu

F32 = jnp.float32
BF16 = jnp.bfloat16
SDS = jax.ShapeDtypeStruct
MESH = pl.DeviceIdType.MESH

D = 1024
FF = 2816
NQ = 16
NKV = 2
HD = 64
GQ = NQ // NKV
AW = NQ * HD
KVW = NKV * HD
BLK = 128
PW = 512
PG = 128
POOL_WINDOWS = (2, 4, 8, 16)
HALO = 16
INW = AW + 2 * KVW + PW + 2 * D
C_KV = AW
C_Z = AW + 2 * KVW
C_G = C_Z + PW
EPS = 1e-6
FF_CHUNK = 256
FF_CHUNKS = tuple((c, FF_CHUNK) for c in range(0, FF, FF_CHUNK))
SLOPES = tuple(float(2.0 ** (-8.0 * h / NQ)) for h in range(1, NQ + 1))
SCALE = HD ** -0.5

LR, B1, B2, ADAM_EPS, WD, STEP = 0.001, 0.9, 0.999, 1e-08, 0.01, 10

VMEM_LIMIT = 56 * 1024 * 1024
N_CHIPS = 4

NT = (((1,), (1,)), ((), ()))
TN = (((0,), (0,)), ((), ()))


def _cp(sem=None, vmem=VMEM_LIMIT):
    return pltpu.CompilerParams(dimension_semantics=sem, vmem_limit_bytes=vmem)


def _const_spec(shape):
    nd = len(shape)
    return pl.BlockSpec(shape, lambda *_: (0,) * nd, pipeline_mode=pl.Buffered(1))


def _rstd(x):
    return lax.rsqrt(jnp.mean(x * x, axis=-1, keepdims=True) + EPS)


def _rms_bwd(dn, xhat, rstd, g):
    dxhat = dn * g
    return rstd * (dxhat - xhat * jnp.mean(dxhat * xhat, axis=-1, keepdims=True))


def _sigmoid(x):
    return 0.5 * jnp.tanh(0.5 * x) + 0.5


def _dot(a, b):
    return jnp.dot(a, b, preferred_element_type=F32)


def _dot_nt(a, b):
    return lax.dot_general(a, b, NT, preferred_element_type=F32)


def _dot_tn(a, b):
    return lax.dot_general(a, b, TN, preferred_element_type=F32)


def _swiglu_up(x, g, wup_ref, ab_ref, hid_ref):
    n = (x * _rstd(x) * g).astype(BF16)
    for c0, w in FF_CHUNKS:
        a = _dot(n, wup_ref[:, c0:c0 + w])
        b = _dot(n, wup_ref[:, FF + c0:FF + c0 + w])
        sig = _sigmoid(a)
        s = a * sig
        ab_ref[:, c0:c0 + w] = (b * (sig + s * (1.0 - sig))).astype(BF16)
        ab_ref[:, FF + c0:FF + c0 + w] = s.astype(BF16)
        hid_ref[:, c0:c0 + w] = (s * b).astype(BF16)


def ffn_up(h, g, wup, name, after=None):
    T = h.shape[0]
    TM = 512
    tile = lambda w: pl.BlockSpec((TM, w), lambda i: (i, 0))
    token_spec, token_arg = _token_operand(after)

    def body(h_ref, g_ref, wup_ref, *rest):
        _swiglu_up(h_ref[...], g_ref[...], wup_ref, *rest[-2:])

    return pl.pallas_call(
        body, name=name, grid=(T // TM,),
        in_specs=[tile(D), _const_spec((1, D)), _const_spec((D, 2 * FF))] + token_spec,
        out_specs=[tile(2 * FF), tile(FF)],
        out_shape=[SDS((T, 2 * FF), BF16), SDS((T, FF), BF16)],
        compiler_params=_cp(("arbitrary",)),
    )(h, g, wup, *token_arg)


def ffn_down(h, hid, wdn, name, after=None):
    T = h.shape[0]
    TM = min(1024, T)
    tile = lambda w: pl.BlockSpec((TM, w), lambda i: (i, 0))
    token_spec, token_arg = _token_operand(after)

    def body(h_ref, hid_ref, wdn_ref, *rest):
        rest[-1][...] = h_ref[...] + 0.5 * _dot(hid_ref[...], wdn_ref[...])

    return pl.pallas_call(
        body, name=name, grid=(T // TM,),
        in_specs=[tile(D), tile(FF), _const_spec((FF, D))] + token_spec,
        out_specs=tile(D), out_shape=SDS((T, D), F32),
        compiler_params=_cp(("arbitrary",)),
    )(h, hid, wdn, *token_arg)


def ffn_fwd_loss(h, g, wup, wdn, target, g_final, name):
    T = h.shape[0]
    TM = 512
    tile = lambda w: pl.BlockSpec((TM, w), lambda i: (i, 0))
    acc_spec = lambda w: pl.BlockSpec((1, w), lambda i: (0, 0))

    def body(h_ref, g_ref, wup_ref, wdn_ref, t_ref, gf_ref, dout_ref, ab_ref, hid_ref, loss_ref, dgf_ref):
        x = h_ref[...]
        _swiglu_up(x, g_ref[...], wup_ref, ab_ref, hid_ref)
        out = x + 0.5 * _dot(hid_ref[...], wdn_ref[...])
        dout_ref[...] = _loss_head(out, t_ref[...], gf_ref[...], loss_ref, dgf_ref, pl.program_id(0) == 0)

    return pl.pallas_call(
        body, name=name, grid=(T // TM,),
        in_specs=[tile(D), _const_spec((1, D)), _const_spec((D, 2 * FF)), _const_spec((FF, D)), tile(D),
                  _const_spec((1, D))],
        out_specs=[tile(D), tile(2 * FF), tile(FF), acc_spec(1), acc_spec(D)],
        out_shape=[SDS((T, D), F32), SDS((T, 2 * FF), BF16), SDS((T, FF), BF16), SDS((1, 1), F32), SDS((1, D), F32)],
        compiler_params=_cp(("arbitrary",)),
    )(h, g, wup, wdn, target, g_final)


def _loss_head(x, target, g, loss_ref, dg_ref, first):
    @pl.when(first)
    def _():
        loss_ref[...] = jnp.zeros_like(loss_ref)
        dg_ref[...] = jnp.zeros_like(dg_ref)

    rstd = _rstd(x)
    xhat = x * rstd
    err = xhat * g - target
    loss_ref[...] += 0.5 * jnp.sum(jnp.mean(err * err, axis=-1, keepdims=True), axis=0, keepdims=True)
    dy = err * (1.0 / D)
    dg_ref[...] += jnp.sum(dy * xhat, axis=0, keepdims=True)
    return _rms_bwd(dy, xhat, rstd, g)


def _adamw_tile(w_ref, g_ref, m_ref, v_ref, go_ref, d_ref, nm_ref, nv_ref):
    gv = g_ref[...]
    go_ref[...] = gv
    nm = B1 * m_ref[...] + (1.0 - B1) * gv
    nv = B2 * v_ref[...] + (1.0 - B2) * (gv * gv)
    nm_ref[...] = nm
    nv_ref[...] = nv
    d_ref[...] = -LR * ((nm / (1.0 - B1 ** STEP)) / (jnp.sqrt(nv / (1.0 - B2 ** STEP)) + ADAM_EPS) + WD * w_ref[...])


def _riders(updates, steps):
    in_specs, out_specs, out_shapes, operands, tiles = [], [], [], [], []
    for w, g, m, v in updates:
        R, C = w.shape
        n = max(d for d in range(1, steps + 1) if R % d == 0 and (R // d) % 8 == 0)
        spec = pl.BlockSpec((R // n, C), lambda i, n=n: (jnp.minimum(i, n - 1), 0))
        in_specs += [spec] * 4
        out_specs += [spec] * 4
        out_shapes += [SDS((R, C), F32)] * 4
        operands += [w, g, m, v]
        tiles.append(n)

    def run(step, in_refs, out_refs):
        for u, n in enumerate(tiles):
            @pl.when(step < n)
            def _(u=u):
                _adamw_tile(*in_refs[4 * u:4 * u + 4], *out_refs[4 * u:4 * u + 4])

    return in_specs, out_specs, out_shapes, operands, run


def ffn_bwd_x(dh, h_in, ab, g, wup, wdn, name, updates=(), after=None):
    T = dh.shape[0]
    TM = 256 if updates else 512
    SUB = 256
    r_in, r_out, r_shapes, r_args, ride = _riders(updates, T // TM)
    token_spec, token_arg = _token_operand(after)
    n_in = len(r_in) + len(token_spec)

    def body(dh_ref, h_ref, ab_ref, g_ref, wup_ref, wdn_ref, *rest):
        dhin_ref, dup_ref, n_ref, dg_ref = rest[n_in:n_in + 4]
        ride(pl.program_id(0), rest[:len(r_in)], rest[n_in + 4:])
        g = g_ref[...]
        dg = None
        for r0 in range(0, TM, SUB):
            rows = slice(r0, r0 + SUB)
            x = h_ref[rows, :]
            rstd = _rstd(x)
            xhat = x * rstd
            n_ref[rows, :] = (xhat * g).astype(BF16)
            dh = dh_ref[rows, :]
            dhh = (0.5 * dh).astype(BF16)
            for c0, w in FF_CHUNKS:
                dhid = _dot_nt(dhh, wdn_ref[c0:c0 + w, :]).astype(BF16)
                dup_ref[rows, c0:c0 + w] = dhid * ab_ref[rows, c0:c0 + w]
                dup_ref[rows, FF + c0:FF + c0 + w] = dhid * ab_ref[rows, FF + c0:FF + c0 + w]
            dn = _dot_nt(dup_ref[rows, :], wup_ref[...])
            dhin_ref[rows, :] = dh + _rms_bwd(dn, xhat, rstd, g)
            part = jnp.sum(dn * xhat, axis=0, keepdims=True)
            dg = part if dg is None else dg + part

        @pl.when(pl.program_id(0) == 0)
        def _():
            dg_ref[...] = jnp.zeros_like(dg_ref)

        dg_ref[...] += dg

    tile = lambda w: pl.BlockSpec((TM, w), lambda i: (i, 0))
    return pl.pallas_call(
        body, name=name, grid=(T // TM,),
        in_specs=[tile(D), tile(D), tile(2 * FF), _const_spec((1, D)), _const_spec((D, 2 * FF)), _const_spec((FF, D))]
        + r_in + token_spec,
        out_specs=[tile(D), tile(2 * FF), tile(D), pl.BlockSpec((1, D), lambda i: (0, 0))] + r_out,
        out_shape=[SDS((T, D), F32), SDS((T, 2 * FF), BF16), SDS((T, D), BF16), SDS((1, D), F32)] + r_shapes,
        compiler_params=_cp(("arbitrary",)),
    )(dh, h_in, ab, g, wup, wdn, *r_args, *token_arg)


TOKEN_SPEC = pl.BlockSpec((8, 128), lambda *_: (0, 0))


def _token_operand(after):
    tokens = [t for t in (after if isinstance(after, tuple) else (after,)) if t is not None]
    return [TOKEN_SPEC] * len(tokens), tokens


def matmul_tn(a, b, name, *, tm, tn, tt=2048, b_scale=None, after=None):
    T, M = a.shape
    N = b.shape[1]
    tt = min(tt, T)
    assert M % tm == 0 and N % tn == 0 and T % tt == 0
    nt = T // tt
    token_spec, token_arg = _token_operand(after)

    def body(a_ref, b_ref, *rest):
        o_ref, acc_ref = rest[-2:]
        t = pl.program_id(2)

        @pl.when(t == 0)
        def _():
            acc_ref[...] = jnp.zeros_like(acc_ref)

        bv = b_ref[...]
        if b_scale is not None:
            bv = bv * b_scale
        acc_ref[...] += _dot_tn(a_ref[...].astype(BF16), bv.astype(BF16))

        @pl.when(t == nt - 1)
        def _():
            o_ref[...] = acc_ref[...].astype(BF16)

    return pl.pallas_call(
        body, name=name, grid=(M // tm, N // tn, nt),
        in_specs=[pl.BlockSpec((tt, tm), lambda i, j, t: (t, i)), pl.BlockSpec((tt, tn), lambda i, j, t: (t, j))]
        + token_spec,
        out_specs=pl.BlockSpec((tm, tn), lambda i, j, t: (i, j)),
        out_shape=SDS((M, N), BF16),
        scratch_shapes=[pltpu.VMEM((tm, tn), F32)],
        compiler_params=_cp(("parallel", "parallel", "arbitrary")),
    )(a, b, *token_arg)


def mix_in_fwd(h1, g, win_t):
    T = h1.shape[0]
    TM = 512

    def body(h_ref, g_ref, w_ref, u_ref, q_ref, kv_ref, z_ref, gt_ref):
        x = h_ref[...]
        u = (x * _rstd(x) * g_ref[...]).astype(BF16)
        u_ref[...] = u
        for c in range(0, AW, 256):
            q_ref[:, c:c + 256] = _dot_nt(u, w_ref[c:c + 256, :]).astype(BF16)
        kv_ref[...] = _dot_nt(u, w_ref[C_KV:C_Z, :]).astype(BF16)
        for c in range(0, PW, 256):
            z_ref[:, c:c + 256] = _dot_nt(u, w_ref[C_Z + c:C_Z + c + 256, :])
        for c in range(0, 2 * D, 256):
            gt_ref[:, c:c + 256] = _dot_nt(u, w_ref[C_G + c:C_G + c + 256, :]).astype(BF16)

    tile = lambda w: pl.BlockSpec((TM, w), lambda i: (i, 0))
    return pl.pallas_call(
        body, name="mix_in_fwd", grid=(T // TM,),
        in_specs=[tile(D), _const_spec((1, D)), _const_spec((INW, D))],
        out_specs=[tile(D), tile(AW), tile(2 * KVW), tile(PW), tile(2 * D)],
        out_shape=[SDS((T, D), BF16), SDS((T, AW), BF16), SDS((T, 2 * KVW), BF16), SDS((T, PW), F32),
                   SDS((T, 2 * D), BF16)],
        compiler_params=_cp(("arbitrary",)),
    )(h1, g, win_t)


def mix_in_bwd(dq, dkv, dz, dgt, dh2, h1, g, win_t):
    T = h1.shape[0]
    TM = min(1024, T)
    SUB = 256

    def body(dq_ref, dkv_ref, dz_ref, dgt_ref, dh2_ref, h_ref, g_ref, w_ref, dh1_ref, dg_ref):
        g = g_ref[...]
        dg = None
        for r0 in range(0, TM, SUB):
            rows = slice(r0, r0 + SUB)
            du = _dot(dq_ref[rows, :], w_ref[0:AW, :])
            du += _dot(dkv_ref[rows, :], w_ref[C_KV:C_Z, :])
            du += _dot(dz_ref[rows, :], w_ref[C_Z:C_G, :])
            du += _dot(dgt_ref[rows, :], w_ref[C_G:INW, :])
            x = h_ref[rows, :]
            rstd = _rstd(x)
            xhat = x * rstd
            dh1_ref[rows, :] = dh2_ref[rows, :] + _rms_bwd(du, xhat, rstd, g)
            part = jnp.sum(du * xhat, axis=0, keepdims=True)
            dg = part if dg is None else dg + part

        @pl.when(pl.program_id(0) == 0)
        def _():
            dg_ref[...] = jnp.zeros_like(dg_ref)

        dg_ref[...] += dg

    tile = lambda w: pl.BlockSpec((TM, w), lambda i: (i, 0))
    return pl.pallas_call(
        body, name="mix_in_bwd", grid=(T // TM,),
        in_specs=[tile(AW), tile(2 * KVW), tile(PW), tile(2 * D), tile(D), tile(D), _const_spec((1, D)),
                  _const_spec((INW, D))],
        out_specs=[tile(D), pl.BlockSpec((1, D), lambda i: (0, 0))],
        out_shape=[SDS((T, D), F32), SDS((1, D), F32)],
        compiler_params=_cp(("arbitrary",)),
    )(dq, dkv, dz, dgt, dh2, h1, g, win_t)


PAIR = 2 * HD
NPAIR = GQ // 2


def _lo_lanes():
    return lax.broadcasted_iota(jnp.int32, (BLK, PAIR), 1) < HD


def _stack_heads(ref, kvh, scale=None):
    lo = _lo_lanes()
    parts = []
    for pr in range(NPAIR):
        t = ref[:, (kvh * NPAIR + pr) * PAIR:(kvh * NPAIR + pr + 1) * PAIR]
        if scale is not None:
            t = t * scale
        zero = jnp.zeros_like(t)
        parts += [jnp.where(lo, t, zero), jnp.where(lo, zero, t)]
    return jnp.concatenate(parts, axis=0)


def _kv_tiles(kvc_ref, kvp_ref, tile, kvh):
    lo = _lo_lanes()
    dup, left, right = [], [], []
    for ref in (kvp_ref, kvc_ref):
        t = ref[:, tile * PAIR:(tile + 1) * PAIR]
        r = pltpu.roll(t.astype(F32), HD, 1).astype(BF16)
        zero = jnp.zeros_like(t)
        a, b = (t, r) if kvh == 0 else (r, t)
        dup.append(jnp.where(lo, a, b))
        left.append(jnp.where(lo, a, zero))
        right.append(jnp.where(lo, zero, b))
    cat = lambda xs: jnp.concatenate(xs, axis=0)
    return cat(dup), cat(left), cat(right)


def _band_consts(first):
    row = lax.broadcasted_iota(jnp.int32, (BLK, BLK), 0)
    col = lax.broadcasted_iota(jnp.int32, (BLK, BLK), 1)
    upper = col > row
    dist = jnp.where(upper, row - col + BLK, row - col).astype(F32)
    pen = jnp.where(jnp.logical_and(upper, first), -jnp.inf, 0.0)
    return upper, dist, pen


def _split_band(upper, t):
    zero = jnp.zeros_like(t)
    return jnp.concatenate([jnp.where(upper, t, zero), jnp.where(upper, zero, t)], axis=1)


def attn_fwd(q, kv, sinks):
    T = q.shape[0]
    nb = T // BLK

    def body(sink_ref, q_ref, kvc_ref, kvp_ref, att_ref, lse_ref):
        upper, dist, pen = _band_consts(pl.program_id(0) == 0)
        scores, values = [], []
        for kvh in range(NKV):
            kdup, _, _ = _kv_tiles(kvc_ref, kvp_ref, 0, kvh)
            values.append(_kv_tiles(kvc_ref, kvp_ref, 1, kvh)[1:])
            scores.append(_dot_nt(_stack_heads(q_ref, kvh, SCALE), kdup))
        for kvh in range(NKV):
            s_all = scores[kvh]
            vleft, vright = values[kvh]
            for pr in range(NPAIR):
                outs, inv = [], []
                for side, vpad in ((0, vleft), (1, vright)):
                    g = 2 * pr + side
                    hq = kvh * GQ + g
                    sink = sink_ref[0, hq]
                    rows = slice(g * BLK, (g + 1) * BLK)
                    s = jnp.where(upper, s_all[rows, 0:BLK], s_all[rows, BLK:2 * BLK]) - SLOPES[hq] * dist + pen
                    m = jnp.maximum(jnp.max(s, axis=-1, keepdims=True), sink)
                    p = jnp.exp(s - m)
                    l = jnp.sum(p, axis=-1, keepdims=True) + jnp.exp(sink - m)
                    lse_ref[:, hq:hq + 1] = m + jnp.log(l)
                    outs.append(_dot(_split_band(upper, p.astype(BF16)), vpad))
                    inv.append(1.0 / l)
                col0 = (kvh * NPAIR + pr) * PAIR
                att_ref[:, col0:col0 + PAIR] = ((outs[0] + outs[1]) * jnp.where(_lo_lanes(), inv[0], inv[1])).astype(BF16)

    return pl.pallas_call(
        body, name="attn_fwd", grid=(nb,),
        in_specs=[pl.BlockSpec(memory_space=pltpu.SMEM),
                  pl.BlockSpec((BLK, AW), lambda i: (i, 0)),
                  pl.BlockSpec((BLK, 2 * KVW), lambda i: (i, 0)),
                  pl.BlockSpec((BLK, 2 * KVW), lambda i: (jnp.maximum(i - 1, 0), 0))],
        out_specs=[pl.BlockSpec((BLK, AW), lambda i: (i, 0)), pl.BlockSpec((BLK, NQ), lambda i: (i, 0))],
        out_shape=[SDS((T, AW), BF16), SDS((T, NQ), F32)],
        compiler_params=_cp(("arbitrary",)),
    )(sinks, q, kv, kv)


def attn_bwd(q, kv, datt, att, lse, sinks, after=None):
    T = q.shape[0]
    nb = T // BLK
    token_spec, token_arg = _token_operand(after)

    def body(sink_ref, q_ref, kvc_ref, kvp_ref, do_ref, out_ref, lse_ref, *rest):
        dq_ref, dkv_ref, dsink_ref, carry_ref = rest[-4:]
        i = pl.program_id(0)

        @pl.when(i == 0)
        def _():
            dsink_ref[...] = jnp.zeros_like(dsink_ref)
            carry_ref[...] = jnp.zeros_like(carry_ref)

        @pl.when(i < nb)
        def _():
            upper, dist, pen = _band_consts(i == 0)
            lo = _lo_lanes()
            dk_dup, dv_dup = [], []
            staged = []
            for kvh in range(NKV):
                kdup, kleft, kright = _kv_tiles(kvc_ref, kvp_ref, 0, kvh)
                vdup, _, _ = _kv_tiles(kvc_ref, kvp_ref, 1, kvh)
                qs = _stack_heads(q_ref, kvh, SCALE)
                dos = _stack_heads(do_ref, kvh)
                staged.append((kleft, kright, qs, dos, _dot_nt(qs, kdup), _dot_nt(dos, vdup)))
            deltas = []
            for pair in range(NQ // 2):
                cols = slice(pair * PAIR, (pair + 1) * PAIR)
                t = do_ref[:, cols].astype(F32) * out_ref[:, cols].astype(F32)
                deltas += [jnp.sum(jnp.where(lo, t, 0.0), axis=-1, keepdims=True),
                           jnp.sum(jnp.where(lo, 0.0, t), axis=-1, keepdims=True)]
            for kvh in range(NKV):
                kleft, kright, qs, dos, s_all, dp_all = staged[kvh]
                ds_parts, p_parts = [], []
                for pr in range(NPAIR):
                    dq = None
                    for side, kpad in ((0, kleft), (1, kright)):
                        g = 2 * pr + side
                        hq = kvh * GQ + g
                        lse_h = lse_ref[:, hq:hq + 1]
                        rows = slice(g * BLK, (g + 1) * BLK)
                        s = jnp.where(upper, s_all[rows, 0:BLK], s_all[rows, BLK:2 * BLK]) - SLOPES[hq] * dist + pen
                        p = jnp.exp(s - lse_h)
                        dp = jnp.where(upper, dp_all[rows, 0:BLK], dp_all[rows, BLK:2 * BLK])
                        delta = deltas[hq]
                        dsink_ref[:, hq:hq + 1] += -jnp.sum(jnp.exp(sink_ref[0, hq] - lse_h) * delta, axis=0,
                                                            keepdims=True)
                        ds = _split_band(upper, (p * (dp - delta)).astype(BF16))
                        ds_parts.append(ds)
                        p_parts.append(_split_band(upper, p.astype(BF16)))
                        d = _dot(ds, kpad)
                        dq = d if dq is None else dq + d
                    col0 = (kvh * NPAIR + pr) * PAIR
                    dq_ref[:, col0:col0 + PAIR] = (dq * SCALE).astype(BF16)
                dkw = _dot_tn(qs, jnp.concatenate(ds_parts, axis=0)).T
                dvw = _dot_tn(dos, jnp.concatenate(p_parts, axis=0)).T
                dk_dup.append(dkw + pltpu.roll(dkw, HD, 1))
                dv_dup.append(dvw + pltpu.roll(dvw, HD, 1))
            dk = jnp.where(jnp.concatenate([lo, lo], axis=0), dk_dup[0], dk_dup[1])
            dv = jnp.where(jnp.concatenate([lo, lo], axis=0), dv_dup[0], dv_dup[1])
            dkv_ref[:, 0:PAIR] = (carry_ref[:, 0:PAIR] + dk[0:BLK]).astype(BF16)
            dkv_ref[:, PAIR:2 * PAIR] = (carry_ref[:, PAIR:2 * PAIR] + dv[0:BLK]).astype(BF16)
            carry_ref[:, 0:PAIR] = dk[BLK:2 * BLK]
            carry_ref[:, PAIR:2 * PAIR] = dv[BLK:2 * BLK]

        @pl.when(i == nb)
        def _():
            dkv_ref[...] = carry_ref[...].astype(BF16)

    cur = lambda i: (jnp.minimum(i, nb - 1), 0)
    prev = lambda i: (jnp.maximum(jnp.minimum(i, nb - 1) - 1, 0), 0)
    return pl.pallas_call(
        body, name="attn_bwd", grid=(nb + 1,),
        in_specs=[pl.BlockSpec(memory_space=pltpu.SMEM),
                  pl.BlockSpec((BLK, AW), cur), pl.BlockSpec((BLK, 2 * KVW), cur), pl.BlockSpec((BLK, 2 * KVW), prev),
                  pl.BlockSpec((BLK, AW), cur), pl.BlockSpec((BLK, AW), cur), pl.BlockSpec((BLK, NQ), cur)] + token_spec,
        out_specs=[pl.BlockSpec((BLK, AW), cur),
                   pl.BlockSpec((BLK, 2 * KVW), lambda i: (jnp.maximum(i - 1, 0), 0)),
                   pl.BlockSpec((1, NQ), lambda i: (0, 0))],
        out_shape=[SDS((T, AW), BF16), SDS((T, 2 * KVW), BF16), SDS((1, NQ), F32)],
        scratch_shapes=[pltpu.VMEM((BLK, 2 * KVW), F32)],
        compiler_params=_cp(("arbitrary",)),
    )(sinks, q, kv, kv, datt, att, lse, *token_arg)


def _inv_counts(t0, rows):
    t = (t0 + lax.broadcasted_iota(jnp.int32, (rows, 1), 0) + 1).astype(F32)
    return [1.0 / jnp.minimum(t, float(w)) for w in POOL_WINDOWS]


def pool_fwd(z, wmix, scale, after=None):
    T = z.shape[0]
    TM = min(1024, T)
    L = TM + HALO
    token_spec, token_arg = _token_operand(after)

    def body(z_ref, halo_ref, wmix_ref, scale_ref, *rest):
        pooled_ref, mixs_ref = rest[-2:]
        i = pl.program_id(0)
        halo = jnp.where(i > 0, halo_ref[...], 0.0)
        zt = z_ref[...]
        e = jnp.concatenate([halo, zt], axis=0)
        sums = []
        s = e
        for k in (1, 2, 4, 8):
            s = s + pltpu.roll(s, k, 0)
            sums.append(s)
        inv = _inv_counts(i * TM, TM)
        for gi in range(len(POOL_WINDOWS)):
            cols = slice(gi * PG, (gi + 1) * PG)
            pooled = (sums[gi][HALO:, cols] * inv[gi] - zt[:, cols]).astype(BF16)
            pooled_ref[:, cols] = pooled
            mixs_ref[:, cols] = (_dot(pooled, wmix_ref[gi]) * scale_ref[:, cols]).astype(BF16)

    return pl.pallas_call(
        body, name="pool_fwd", grid=(T // TM,),
        in_specs=[pl.BlockSpec((TM, PW), lambda i: (i, 0)),
                  pl.BlockSpec((HALO, PW), lambda i: (jnp.maximum(i * (TM // HALO) - 1, 0), 0)),
                  _const_spec((len(POOL_WINDOWS), PG, PG)), _const_spec((1, PW))] + token_spec,
        out_specs=[pl.BlockSpec((TM, PW), lambda i: (i, 0)), pl.BlockSpec((TM, PW), lambda i: (i, 0))],
        out_shape=[SDS((T, PW), BF16), SDS((T, PW), BF16)],
        compiler_params=_cp(("arbitrary",)),
    )(z, z, wmix, scale, *token_arg)


def pool_bwd(dmixs, pooled, wmix, scale, after=None):
    T = dmixs.shape[0]
    TM = min(1024, T)
    L = TM + HALO
    nt = T // TM
    token_spec, token_arg = _token_operand(after)

    def body(dm_ref, halo_ref, pooled_ref, wmix_ref, scale_ref, *rest):
        dz_ref, dwmix_ref, dscale_ref = rest[-3:]
        i = pl.program_id(0)

        @pl.when(i == 0)
        def _():
            dwmix_ref[...] = jnp.zeros_like(dwmix_ref)
            dscale_ref[...] = jnp.zeros_like(dscale_ref)

        halo = jnp.where(i < nt - 1, halo_ref[...], 0.0)
        dm = dm_ref[...]
        e = jnp.concatenate([dm, halo], axis=0)
        inv = _inv_counts(i * TM, L)
        for gi in range(len(POOL_WINDOWS)):
            cols = slice(gi * PG, (gi + 1) * PG)
            w = wmix_ref[gi]
            dmixed = (e[:, cols] * scale_ref[:, cols]).astype(BF16)
            dpooled = _dot_nt(dmixed, w)
            pooled = pooled_ref[:, cols]
            mixed = _dot(pooled, w)
            dscale_ref[:, cols] += jnp.sum(dm[:, cols] * mixed, axis=0, keepdims=True)
            dwmix_ref[gi] += _dot_tn(pooled, dmixed[:TM])
            s = dpooled * inv[gi]
            k = 1
            while k < POOL_WINDOWS[gi]:
                s = s + pltpu.roll(s, L - k, 0)
                k *= 2
            dz_ref[:, cols] = (s[:TM] - dpooled[:TM]).astype(BF16)

    return pl.pallas_call(
        body, name="pool_bwd", grid=(nt,),
        in_specs=[pl.BlockSpec((TM, PW), lambda i: (i, 0)),
                  pl.BlockSpec((HALO, PW), lambda i: (jnp.minimum((i + 1) * (TM // HALO), T // HALO - 1), 0)),
                  pl.BlockSpec((TM, PW), lambda i: (i, 0)),
                  _const_spec((len(POOL_WINDOWS), PG, PG)), _const_spec((1, PW))] + token_spec,
        out_specs=[pl.BlockSpec((TM, PW), lambda i: (i, 0)),
                   pl.BlockSpec((len(POOL_WINDOWS), PG, PG), lambda i: (0, 0, 0)),
                   pl.BlockSpec((1, PW), lambda i: (0, 0))],
        out_shape=[SDS((T, PW), BF16), SDS((len(POOL_WINDOWS), PG, PG), F32), SDS((1, PW), F32)],
        compiler_params=_cp(("arbitrary",)),
    )(dmixs, dmixs, pooled, wmix, scale, *token_arg)


def merge_fwd(att, mixs, gt, h1, wattn, wpool, wout):
    T = h1.shape[0]
    TM = 512

    def body(att_ref, mixs_ref, gt_ref, h_ref, wa_ref, wp_ref, wo_ref, h2_ref, mg_ref, gf_ref):
        a = _dot(att_ref[...], wa_ref[...])
        p = _dot(mixs_ref[...], wp_ref[...])
        sa = _sigmoid(gt_ref[:, 0:D].astype(F32))
        sp = _sigmoid(gt_ref[:, D:2 * D].astype(F32))
        gf_ref[:, 0:D] = (a * sa * (1.0 - sa)).astype(BF16)
        gf_ref[:, D:2 * D] = (p * sp * (1.0 - sp)).astype(BF16)
        mg = (sa * a + sp * p).astype(BF16)
        mg_ref[...] = mg
        h2_ref[...] = h_ref[...] + _dot(mg, wo_ref[...])

    tile = lambda w: pl.BlockSpec((TM, w), lambda i: (i, 0))
    return pl.pallas_call(
        body, name="merge_fwd", grid=(T // TM,),
        in_specs=[tile(AW), tile(PW), tile(2 * D), tile(D), _const_spec((AW, D)), _const_spec((PW, D)),
                  _const_spec((D, D))],
        out_specs=[tile(D), tile(D), tile(2 * D)],
        out_shape=[SDS((T, D), F32), SDS((T, D), BF16), SDS((T, 2 * D), BF16)],
        compiler_params=_cp(("arbitrary",)),
    )(att, mixs, gt, h1, wattn, wpool, wout)


def merge_bwd(dh2, gt, gf, wattn, wpool, wout, after=None):
    T = dh2.shape[0]
    TM = 512
    token_spec, token_arg = _token_operand(after)

    def body(dh2_ref, gt_ref, gf_ref, wa_ref, wp_ref, wo_ref, *rest):
        datt_ref, dmixs_ref, dgt_ref, da_ref, dp_ref = rest[-5:]
        dm = _dot_nt(dh2_ref[...].astype(BF16), wo_ref[...])
        da = (dm * _sigmoid(gt_ref[:, 0:D].astype(F32))).astype(BF16)
        dp = (dm * _sigmoid(gt_ref[:, D:2 * D].astype(F32))).astype(BF16)
        da_ref[...] = da
        dp_ref[...] = dp
        dgt_ref[:, 0:D] = (dm * gf_ref[:, 0:D].astype(F32)).astype(BF16)
        dgt_ref[:, D:2 * D] = (dm * gf_ref[:, D:2 * D].astype(F32)).astype(BF16)
        datt_ref[...] = _dot_nt(da, wa_ref[...]).astype(BF16)
        dmixs_ref[...] = _dot_nt(dp, wp_ref[...])

    tile = lambda w: pl.BlockSpec((TM, w), lambda i: (i, 0))
    return pl.pallas_call(
        body, name="merge_bwd", grid=(T // TM,),
        in_specs=[tile(D), tile(2 * D), tile(2 * D), _const_spec((AW, D)), _const_spec((PW, D)),
                  _const_spec((D, D))] + token_spec,
        out_specs=[tile(AW), tile(PW), tile(2 * D), tile(D), tile(D)],
        out_shape=[SDS((T, AW), BF16), SDS((T, PW), F32), SDS((T, 2 * D), BF16), SDS((T, D), BF16),
                   SDS((T, D), BF16)],
        compiler_params=_cp(("arbitrary",)),
    )(dh2, gt, gf, wattn, wpool, wout, *token_arg)


def adamw(updates, place, name, after=None):
    tile_bytes = 2 * 1024 * 1024 // len(updates)
    jobs = []
    for w, g, m, v in updates:
        R, C = w.shape
        tr = R
        if R * C * 4 > tile_bytes:
            tr = next(cand for cand in (512, 256, 128, 64, 32, 16, 8) if R % cand == 0 and cand * C * 4 <= tile_bytes)
        blk = ((tr, C), lambda ls, p: (ls, 0))
        jobs.append((R // tr, [(a, *blk) for a in (w, g, m, v)], [((R, C), F32, *blk)] * 4,
                     lambda ins, outs: _adamw_tile(*ins, *outs)))
    return _run_jobs(jobs, place, name, after)


GROUP_UP1, GROUP_MIX, GROUP_FFN2 = 0, 1, 2


def local_fwd_bwd(x, target, S, comm):
    w_up1 = comm.weights(GROUP_UP1, None)["ffn1_w_up"]
    ab1, hid1 = ffn_up(x, S["ffn1_norm"], w_up1, "ffn1_up", after=comm.started())
    W = comm.weights(GROUP_MIX, hid1)
    h1 = ffn_down(x, hid1, W["ffn1_w_down"], "ffn1_down", after=comm.started())
    u, q, kv, z, gt = mix_in_fwd(h1, S["mix_norm"], W["w_in"])
    att, lse = attn_fwd(q, kv, S["sinks"])
    pooled, mixs = pool_fwd(z, S["pool_w_mix"], S["pool_scale"], after=comm.prefetch(GROUP_FFN2, att))
    h2, merged, gf = merge_fwd(att, mixs, gt, h1, W["w_attn_up"], W["w_pool_up"], W["w_out"])
    W = comm.weights(GROUP_FFN2, h2)
    dh3, ab2, hid2, loss, g_final = ffn_fwd_loss(h2, S["ffn2_norm"], W["ffn2_w_up"], W["ffn2_w_down"], target,
                                                 S["final_norm"], "ffn2_fwd")

    G = {"final_norm": g_final}
    dh2, dup2, n2, G["ffn2_norm"] = ffn_bwd_x(dh3, h2, ab2, S["ffn2_norm"], W["ffn2_w_up"], W["ffn2_w_down"],
                                              "ffn2_bwd_x")
    ffn2 = ("ffn2_w_down", "ffn2_w_up")
    token = comm.grads({"ffn2_w_down": matmul_tn(hid2, dh3, "ffn2_dw_down", tm=1408, tn=512, tt=4096, b_scale=0.5),
                        "ffn2_w_up": matmul_tn(n2, dup2, "ffn2_dw_up", tm=D, tn=512, tt=4096)})

    W = comm.weights(GROUP_MIX, None)
    datt, dmixs, dgt, da, dp = merge_bwd(dh2, gt, gf, W["w_attn_up"], W["w_pool_up"], W["w_out"], after=token)
    token = comm.advance(ffn2, datt)
    g_mix = {"w_out": matmul_tn(merged, dh2, "dw_out", tm=D, tn=D, after=token),
             "w_attn_up": matmul_tn(att, da, "dw_attn_up", tm=AW, tn=D),
             "w_pool_up": matmul_tn(mixs, dp, "dw_pool_up", tm=PW, tn=D)}
    dz, G["pool_w_mix"], G["pool_scale"] = pool_bwd(dmixs, pooled, S["pool_w_mix"], S["pool_scale"], after=token)
    dq, dkv, G["sinks"] = attn_bwd(q, kv, datt, att, lse, S["sinks"], after=token)
    dh1, G["mix_norm"] = mix_in_bwd(dq, dkv, dz, dgt, dh2, h1, S["mix_norm"], W["w_in"])
    g_mix["w_in"] = jnp.concatenate([
        matmul_tn(dq, u, "dw_in_q", tm=AW, tn=D),
        matmul_tn(dkv, u, "dw_in_kv", tm=2 * KVW, tn=D),
        matmul_tn(dz, u, "dw_in_z", tm=PW, tn=D),
        matmul_tn(dgt, u, "dw_in_g", tm=D, tn=512, tt=4096),
    ], axis=0)
    mix = tuple(g_mix)
    token = comm.grads(g_mix)

    g_dn1 = matmul_tn(hid1, dh1, "ffn1_dw_down", tm=1408, tn=512, tt=4096, b_scale=0.5, after=token)
    token = comm.advance(mix, g_dn1)
    token2 = comm.grads({"ffn1_w_down": g_dn1})
    dx, dup1, n1, G["ffn1_norm"], *outs = ffn_bwd_x(dh1, x, ab1, S["ffn1_norm"], w_up1, W["ffn1_w_down"], "ffn1_bwd_x",
                                                   updates=comm.reduce(ffn2, token2), after=(token, token2))
    comm.updated(ffn2, outs)
    token = comm.advance(("ffn1_w_down",), dx)
    token2 = comm.small_start(G, loss, dx)
    g_up1 = matmul_tn(n1, dup1, "ffn1_dw_up", tm=D, tn=512, tt=4096, after=(token, token2))
    token = comm.grads({"ffn1_w_up": g_up1})
    updates = comm.reduce(("ffn1_w_down",), token)
    token = comm.advance(("ffn1_w_up",), [u[1] for u in updates])
    done = comm.update(("ffn1_w_down",), updates, token)
    done = comm.finish(mix, done)
    done = comm.small_finish(done)
    comm.finish(("ffn1_w_up",), done)
    return loss, dx


HBM_SPEC = pl.BlockSpec(memory_space=pltpu.HBM)


def _me():
    return lax.axis_index("x"), lax.axis_index("y"), lax.axis_index("c")


def _peer_chip(x, y, k):
    return x ^ (k >> 1), y ^ (k & 1)


def _piece_half(ref, rowlike, j, h):
    if rowlike:
        return ref.at[j, h]
    ns = ref.shape[-1] // N_CHIPS
    return ref.at[h, :, pl.ds(pl.multiple_of(j * ns, 128), ns)]


def _piece(ref, rowlike, j):
    if rowlike:
        return ref.at[j]
    ns = ref.shape[-1] // N_CHIPS
    return ref.at[:, :, pl.ds(pl.multiple_of(j * ns, 128), ns)]


def _full_shape(shard_view, rowlike):
    _, kh, ns = shard_view.shape
    return (N_CHIPS, 2, kh, ns) if rowlike else (2, kh, N_CHIPS * ns)


def _remote(src, dst, send_sem, recv_sem, dev):
    return pltpu.make_async_remote_copy(src, dst, send_sem, recv_sem, device_id=dev, device_id_type=MESH)


SEM_SPEC = pl.BlockSpec(memory_space=pltpu.SEMAPHORE)
SPLIT_PARAMS = pltpu.CompilerParams(has_side_effects=pltpu.SideEffectType.DATAFLOW_SIDE_EFFECTING)


def _gather_plan(rowlikes):
    def plan(s_refs, f_refs, a, k, x, y, c):
        if k == 0:
            own = _piece(f_refs[a], rowlikes[a], 2 * x + y)
            return s_refs[a], own, own, (x, y, 1 - c)
        px, py = _peer_chip(x, y, k)
        return (s_refs[a].at[c], _piece_half(f_refs[a], rowlikes[a], 2 * x + y, c),
                _piece_half(f_refs[a], rowlikes[a], 2 * px + py, c), (px, py, c))
    return plan


def _all_to_all_plan(rowlikes):
    def plan(q_refs, r_refs, a, k, x, y, c):
        px, py = _peer_chip(x, y, k)
        if rowlikes[a]:
            src = q_refs[a].at[2 * px + py]
        else:
            ns = q_refs[a].shape[-1] // N_CHIPS
            src = q_refs[a].at[:, pl.ds(pl.multiple_of((2 * px + py) * ns, 128), ns)]
        return src, r_refs[a].at[k - 1], r_refs[a].at[k - 1], (px, py, c)
    return plan


def _swap_plan(rowlikes):
    def plan(g_refs, got_refs, a, k, x, y, c):
        src = g_refs[a].at[:, 1 - c] if rowlikes[a] else g_refs[a].at[1 - c]
        return src, got_refs[a], got_refs[a], (x, y, 1 - c)
    return plan


def _everyone_plan(s_refs, slot_refs, a, k, x, y, c):
    px, py, pc = x ^ (k >> 2), y ^ ((k >> 1) & 1), c ^ (k & 1)
    return s_refs[a], slot_refs[a].at[4 * x + 2 * y + c], slot_refs[a].at[4 * px + 2 * py + pc], (px, py, pc)


def _forward_plan(rowlikes):
    def plan(s_refs, f_refs, a, k, x, y, c):
        px, py = _peer_chip(x, y, k)
        mine = _piece_half(f_refs[a], rowlikes[a], 2 * px + py, c)
        return mine, mine, _piece_half(f_refs[a], rowlikes[a], 2 * px + py, 1 - c), (x, y, 1 - c)
    return plan


CHIPS, SIBLING, EVERYONE, GATHERS = (1, 2, 3), (1,), tuple(range(1, 8)), (0, 1, 2, 3)


def _as_list(after):
    return [] if after is None else list(after) if isinstance(after, (list, tuple)) else [after]


def exchange_start(srcs, land_shapes, plan, after, name, peers=CHIPS):
    n = len(srcs)
    lands = [l if hasattr(l, "dtype") else lax.empty(l, s.dtype) for l, s in zip(land_shapes, srcs)]

    behind = _as_list(after)

    def body(*refs):
        s_refs, l_refs = refs[:n], refs[n:2 * n]
        send_sems, recv_sems = refs[2 * n + len(behind)], refs[2 * n + len(behind) + 1]
        token = refs[-1]
        x, y, c = _me()
        for a in range(n):
            for i, k in enumerate(peers):
                src, dst, _, peer = plan(s_refs, l_refs, a, k, x, y, c)
                sem = len(peers) * a + i
                _remote(src, dst, send_sems.at[sem], recv_sems.at[sem], peer).start()
        token[...] = jnp.zeros_like(token)

    n_sems = len(peers) * n
    outs = pl.pallas_call(
        body, name=name, in_specs=[HBM_SPEC] * (2 * n) + [pl.BlockSpec(memory_space=pl.ANY)] * len(behind),
        out_specs=[SEM_SPEC, SEM_SPEC] + [HBM_SPEC] * (2 * n) + [pl.BlockSpec(memory_space=pltpu.VMEM)],
        out_shape=[pltpu.SemaphoreType.DMA((n_sems,)), pltpu.SemaphoreType.DMA((n_sems,))]
        + [pltpu.HBM(a.shape, a.dtype) for a in (*srcs, *lands)] + [SDS((8, 128), F32)],
        input_output_aliases={i: 2 + i for i in range(2 * n)},
        compiler_params=SPLIT_PARAMS,
    )(*[pltpu.with_memory_space_constraint(a, pltpu.HBM) for a in (*srcs, *lands)], *behind)
    return {"sems": outs[:2], "srcs": outs[2:2 + n], "lands": outs[2 + n:2 + 2 * n], "token": outs[-1]}


def exchange_wait(state, plan, after, name, peers=CHIPS):
    n = len(state["srcs"])
    behind = _as_list(after)

    def body(*refs):
        s_refs, l_refs = refs[:n], refs[n:2 * n]
        send_sems, recv_sems = refs[2 * n], refs[2 * n + 1]
        x, y, c = _me()
        for a in range(n):
            for i, k in enumerate(peers):
                src, _, landing, peer = plan(s_refs, l_refs, a, k, x, y, c)
                sem = len(peers) * a + i
                cp = _remote(src, landing, send_sems.at[sem], recv_sems.at[sem], peer)
                cp.wait_send()
                cp.wait_recv()

    bufs = (*state["srcs"], *state["lands"])
    outs = pl.pallas_call(
        body, name=name,
        in_specs=[HBM_SPEC] * (2 * n) + [SEM_SPEC, SEM_SPEC] + [pl.BlockSpec(memory_space=pl.ANY)] * len(behind),
        out_specs=[HBM_SPEC] * (2 * n),
        out_shape=[pltpu.HBM(a.shape, a.dtype) for a in bufs],
        input_output_aliases={i: i for i in range(2 * n)},
        compiler_params=SPLIT_PARAMS,
    )(*bufs, *state["sems"], *behind)
    return outs[:n], outs[n:]


def gather_finish(fulls, rowlikes, name):
    n = len(fulls)

    def body(*refs):
        f_refs = refs[n:2 * n]
        send_sems, recv_sems = refs[2 * n:]
        x, y, c = _me()
        sib = (x, y, 1 - c)
        sends = []
        for a in range(n):
            for k in CHIPS:
                px, py = _peer_chip(x, y, k)
                slot = _piece_half(f_refs[a], rowlikes[a], 2 * px + py, c)
                cp = _remote(slot, slot, send_sems.at[a, k - 1], recv_sems.at[a, k - 1], sib)
                cp.start()
                sends.append(cp)
        for a in range(n):
            for k in CHIPS:
                px, py = _peer_chip(x, y, k)
                slot = _piece_half(f_refs[a], rowlikes[a], 2 * px + py, 1 - c)
                _remote(slot, slot, send_sems.at[a, k - 1], recv_sems.at[a, k - 1], sib).wait_recv()
        for cp in sends:
            cp.wait_send()

    return pl.pallas_call(
        body, name=name, in_specs=[HBM_SPEC] * n, out_specs=[HBM_SPEC] * n,
        out_shape=[SDS(f.shape, f.dtype) for f in fulls],
        input_output_aliases={a: a for a in range(n)},
        scratch_shapes=[pltpu.SemaphoreType.DMA((n, len(CHIPS))), pltpu.SemaphoreType.DMA((n, len(CHIPS)))],
    )(*fulls)


def _half_buffer_shape(gview, rowlike):
    return (N_CHIPS,) + gview.shape[2:] if rowlike else gview.shape[1:]


def _row_tiles(kh, ns):
    return 1 if kh * ns <= 256 * 1024 else 2


def _run_jobs(jobs, place, name, after=None):
    n_steps = max(job[0] for job in jobs)

    def spec(block, index, steps):
        return pl.BlockSpec(block, lambda s, p: index(jnp.minimum(s, steps - 1), p))

    in_specs = [spec(b, ix, steps) for steps, ins, _, _ in jobs for _, b, ix in ins]
    out_specs = [spec(b, ix, steps) for steps, _, outs, _ in jobs for _, _, b, ix in outs]
    token_spec, token_arg = _token_operand(after)
    n_in = len(in_specs) + len(token_spec)

    def body(place_ref, *refs):
        s = pl.program_id(0)
        i, o = 0, n_in
        for steps, ins, outs, fn in jobs:
            in_refs, out_refs = refs[i:i + len(ins)], refs[o:o + len(outs)]
            i, o = i + len(ins), o + len(outs)

            @pl.when(s < steps)
            def _(fn=fn, in_refs=in_refs, out_refs=out_refs):
                fn(in_refs, out_refs)

    return pl.pallas_call(
        body, name=name,
        grid_spec=pltpu.PrefetchScalarGridSpec(num_scalar_prefetch=1, grid=(n_steps,),
                                               in_specs=in_specs + token_spec, out_specs=out_specs),
        out_shape=[SDS(shape, dtype) for _, _, outs, _ in jobs for shape, dtype, _, _ in outs],
        compiler_params=_cp(("arbitrary",)),
    )(place, *[a for _, ins, _, _ in jobs for a, _, _ in ins], *token_arg)


def add_halves(gviews, gots, rowlikes, place, name):
    def add(ins, outs):
        outs[0][...] = (ins[0][...].astype(F32) + ins[1][...].astype(F32)).astype(BF16)

    jobs = []
    for g, got, rowlike in zip(gviews, gots, rowlikes):
        kh, ns = (g.shape[2], g.shape[3]) if rowlike else (g.shape[1], g.shape[2] // N_CHIPS)
        rt = _row_tiles(kh, ns)
        tr = kh // rt
        if rowlike:
            g_in = (g, (None, None, tr, ns), lambda ls, p, rt=rt: (ls // rt, p[0], ls % rt, 0))
            half = ((None, tr, ns), lambda ls, p, rt=rt: (ls // rt, ls % rt, 0))
        else:
            g_in = (g, (None, tr, ns), lambda ls, p, rt=rt: (p[0], ls % rt, ls // rt))
            half = ((tr, ns), lambda ls, p, rt=rt: (ls % rt, ls // rt))
        jobs.append((N_CHIPS * rt, [g_in, (got, *half)], [(got.shape, BF16, *half)], add))
    return _run_jobs(jobs, place, name)


def _slot_shape(q, rowlike):
    return (3,) + (q.shape[1:] if rowlike else (q.shape[0], q.shape[1] // N_CHIPS))


def sum_pieces(qs, recvs, rowlikes, place, name):
    def total(ins, outs):
        acc = ins[0][...].astype(F32)
        for k in range(3):
            acc = acc + ins[1][k].astype(F32)
        outs[0][...] = acc

    jobs = []
    for q, recv, rowlike in zip(qs, recvs, rowlikes):
        kh, ns = recv.shape[1], recv.shape[2]
        rt = _row_tiles(kh, ns)
        tr = kh // rt
        mine = ((None, tr, ns), lambda ls, p: (p[1], ls, 0)) if rowlike else ((tr, ns), lambda ls, p: (ls, p[1]))
        jobs.append((rt, [(q, *mine), (recv, (3, tr, ns), lambda ls, p: (0, ls, 0))],
                     [((2, kh, ns), F32, (None, tr, ns), lambda ls, p: (p[0], ls, 0))], total))
    return _run_jobs(jobs, place, name)


def join_halves(halves, name):
    n = len(halves)

    def body(*refs):
        o_refs = refs[n:2 * n]
        send_sems, recv_sems = refs[2 * n:]
        x, y, c = _me()
        sib = (x, y, 1 - c)
        sends = []
        for a in range(n):
            cp = _remote(o_refs[a].at[c], o_refs[a].at[c], send_sems.at[a], recv_sems.at[a], sib)
            cp.start()
            sends.append(cp)
        for a in range(n):
            got = o_refs[a].at[1 - c]
            _remote(got, got, send_sems.at[a], recv_sems.at[a], sib).wait_recv()
        for cp in sends:
            cp.wait_send()

    return pl.pallas_call(
        body, name=name, in_specs=[HBM_SPEC] * n, out_specs=[HBM_SPEC] * n,
        out_shape=[SDS(h.shape, h.dtype) for h in halves],
        input_output_aliases={a: a for a in range(n)},
        scratch_shapes=[pltpu.SemaphoreType.DMA((n,)), pltpu.SemaphoreType.DMA((n,))],
    )(*halves)


N_DEV = 8


def sum_devices(s, slots, me):
    R, Wd = s.shape

    def body(me_ref, s_ref, slots_ref, out_ref):
        acc = None
        for d in range(N_DEV):
            mine = me_ref[0] == d
            term = jnp.where(mine, s_ref[...], slots_ref[jnp.where(mine, d ^ 1, d)])
            acc = term if acc is None else acc + term
        out_ref[...] = acc

    vmem = pl.BlockSpec(memory_space=pltpu.VMEM)
    return pl.pallas_call(
        body, name="sum_devices", in_specs=[pl.BlockSpec(memory_space=pltpu.SMEM), vmem, vmem], out_specs=vmem,
        out_shape=SDS((R, Wd), F32),
    )(me, s, slots)


TRANSPOSED = ("w_in",)
BIG = {"ffn1_w_up": (D, 2 * FF, "col"), "ffn1_w_down": (FF, D, "row"), "w_in": (INW, D, "row"),
       "w_attn_up": (AW, D, "row"), "w_pool_up": (PW, D, "col"), "w_out": (D, D, "row"),
       "ffn2_w_up": (D, 2 * FF, "col"), "ffn2_w_down": (FF, D, "row")}
GROUPS = (("ffn1_w_up",), ("ffn1_w_down", "w_in", "w_attn_up", "w_pool_up", "w_out"), ("ffn2_w_up", "ffn2_w_down"))
SMALL = ("ffn1_norm", "mix_norm", "ffn2_norm", "final_norm", "pool_scale", "sinks", "pool_w_mix")
SMALL_W = 128


def _rowlike(name):
    return BIG[name][2] == "row"


def _half_dims(name):
    k, n, kind = BIG[name]
    return (k // N_CHIPS // 2, n) if kind == "row" else (k // 2, n // N_CHIPS)


def shard_view(name, shard):
    return shard.reshape((2,) + _half_dims(name))


def full_from_view(name, fv):
    k, n, _ = BIG[name]
    return fv.reshape(k, n)


def grad_view(name, g):
    kh, ns = _half_dims(name)
    return g.reshape(_full_shape(jax.ShapeDtypeStruct((2, kh, ns), g.dtype), _rowlike(name)))


def pack_small(d):
    parts = []
    for name in SMALL:
        a = d[name].reshape(-1)
        pad = (-a.shape[0]) % SMALL_W
        parts.append(jnp.pad(a, (0, pad)).reshape(-1, SMALL_W))
    a = jnp.concatenate(parts, axis=0)
    return jnp.pad(a, ((0, (-a.shape[0]) % 8), (0, 0)))


def unpack_small(a, like):
    out, r0 = {}, 0
    for name in SMALL:
        size = int(np.prod(like[name].shape))
        rows = -(-size // SMALL_W)
        out[name] = a[r0:r0 + rows].reshape(-1)[:size].reshape(like[name].shape)
        r0 += rows
    return out


WEIGHTS = ("ffn1_norm", "ffn1_w_up", "ffn1_w_down", "mix_norm", "w_in", "sinks", "w_attn_up", "pool_w_mix",
           "pool_scale", "w_pool_up", "w_out", "ffn2_norm", "ffn2_w_up", "ffn2_w_down", "final_norm")


def kernel(x, ffn1_norm, ffn1_w_up, ffn1_w_down, mix_norm, w_in, sinks, w_attn_up, pool_w_mix, pool_scale, w_pool_up, w_out, ffn2_norm, ffn2_w_up, ffn2_w_down, final_norm, loss_target, m_ffn1_norm, m_ffn1_w_up, m_ffn1_w_down, m_mix_norm, m_w_in, m_sinks, m_w_attn_up, m_pool_w_mix, m_pool_scale, m_w_pool_up, m_w_out, m_ffn2_norm, m_ffn2_w_up, m_ffn2_w_down, m_final_norm, v_ffn1_norm, v_ffn1_w_up, v_ffn1_w_down, v_mix_norm, v_w_in, v_sinks, v_w_attn_up, v_pool_w_mix, v_pool_scale, v_w_pool_up, v_w_out, v_ffn2_norm, v_ffn2_w_up, v_ffn2_w_down, v_final_norm):
    given = dict(locals())
    w = {n: given[n] for n in WEIGHTS}
    m = {n: given["m_" + n] for n in WEIGHTS}
    v = {n: given["v_" + n] for n in WEIGHTS}
    cx, cy, cc = _me()
    place = jnp.stack([cc, 2 * cx + cy]).astype(jnp.int32)

    def local2d(d, n):
        return d[n][0].T if n in TRANSPOSED else d[n][0]

    shards = {n: local2d(w, n) for n in BIG}
    grads, delta, new_m, new_v = {}, {}, {}, {}
    rowlikes = [[_rowlike(n) for n in names] for names in GROUPS]

    class Exchanges:
        def __init__(self):
            self.gathers, self.forwards, self.fulls, self.reductions, self.small = {}, {}, {}, {}, None

        def _behind_first(self, a):
            return a + self.gathers[0]["token"][0, 0]

        def _start_gather(self, group, after):
            sv = [self.cast[n] if group else shard_view(n, shards[n].astype(BF16)) for n in GROUPS[group]]
            rl = rowlikes[group]
            self.gathers[group] = exchange_start(sv, [_full_shape(s, r) for s, r in zip(sv, rl)], _gather_plan(rl),
                                                 after, f"gather_start_{group}", GATHERS)

        def weights(self, group, after):
            if not self.gathers:
                self._start_gather(0, None)
                self.cast = {n: shard_view(n, self._behind_first(shards[n]).astype(BF16))
                             for names in GROUPS[1:] for n in names}
                after = [self.gathers[0]["token"], *self.cast.values()]
            if group not in self.fulls:
                names, rl, state = GROUPS[group], rowlikes[group], self.gathers[group]
                if group in self.forwards:
                    _, fulls = exchange_wait(self.forwards.pop(group), _forward_plan(rl), after,
                                             f"forward_wait_{group}")
                else:
                    _, fulls = exchange_wait(state, _gather_plan(rl), state["token"] if after is None else after,
                                             f"gather_wait_{group}", GATHERS)
                    fulls = gather_finish(fulls, rl, f"gather_finish_{group}")
                self.fulls[group] = {n: full_from_view(n, f) for n, f in zip(names, fulls)}
                if group + 1 < len(GROUPS):
                    self._start_gather(group + 1, fulls[0])
            return self.fulls[group]

        def started(self):
            return self.gathers[max(self.gathers)]["token"]

        def prefetch(self, group, after):
            rl, state = rowlikes[group], self.gathers[group]
            sv, fulls = exchange_wait(state, _gather_plan(rl), after, f"gather_wait_{group}", GATHERS)
            self.forwards[group] = exchange_start(sv, fulls, _forward_plan(rl), None, f"forward_start_{group}")
            return self.forwards[group]["token"]

        def grads(self, g):
            names = tuple(g)
            rl = [_rowlike(n) for n in names]
            gv = [grad_view(n, g[n]) for n in names]
            state = exchange_start(gv, [_half_buffer_shape(a, r) for a, r in zip(gv, rl)], _swap_plan(rl), None,
                                   "swap_start_" + names[0], SIBLING)
            self.reductions[names] = state
            return state["token"]

        def advance(self, names, after):
            rl = [_rowlike(n) for n in names]
            gv, gots = exchange_wait(self.reductions[names], _swap_plan(rl), after, "swap_wait_" + names[0], SIBLING)
            qs = add_halves(gv, gots, rl, place, "add_halves_" + names[0])
            state = exchange_start(qs, [_slot_shape(q, r) for q, r in zip(qs, rl)], _all_to_all_plan(rl), None,
                                   "all_to_all_start_" + names[0])
            self.reductions[names] = state
            return state["token"]

        def reduce(self, names, after):
            rl = [_rowlike(n) for n in names]
            qs, recvs = exchange_wait(self.reductions.pop(names), _all_to_all_plan(rl), after,
                                      "all_to_all_wait_" + names[0])
            halves = sum_pieces(qs, recvs, rl, place, "sum_pieces_" + names[0])
            return [(shards[n], o.reshape(shards[n].shape), local2d(m, n), local2d(v, n))
                    for n, o in zip(names, join_halves(halves, "join_halves_" + names[0]))]

        def updated(self, names, outs):
            for i, n in enumerate(names):
                grads[n], delta[n], new_m[n], new_v[n] = outs[4 * i:4 * i + 4]
            return [new_v[n] for n in names]

        def update(self, names, updates, token=None):
            return self.updated(names, adamw(updates, place, "adamw_" + names[0], after=token))

        def finish(self, names, after):
            return self.update(names, self.reduce(names, after))

        def small_start(self, G, loss, after):
            packed = pack_small({n: G[n] for n in SMALL})
            used_rows = sum(-(-int(np.prod(small_like[n].shape)) // SMALL_W) for n in SMALL)
            assert packed.shape[0] > used_rows
            packed = packed.at[-1, 0].set(loss[0, 0])
            self.small = exchange_start([packed], [(N_DEV,) + packed.shape], _everyone_plan, after,
                                        "small_start", EVERYONE)
            return self.small["token"]

        def small_finish(self, after):
            (packed,), (slots,) = exchange_wait(self.small, _everyone_plan, after, "small_wait", EVERYONE)
            total = sum_devices(packed, slots, (4 * cx + 2 * cy + cc).astype(jnp.int32).reshape(1))
            self.loss = total[-1, 0]
            rows2d = lambda a: a.reshape(-1, a.shape[-1])
            g = unpack_small(total, small_like)
            outs = adamw([tuple(rows2d(d[n]) for d in (w, g, m, v)) for n in SMALL], place, "adamw_small")
            for i, n in enumerate(SMALL):
                grads[n], delta[n], new_m[n], new_v[n] = (o.reshape(w[n].shape) for o in outs[4 * i:4 * i + 4])
            return new_v["pool_w_mix"]

    small_like = {n: w[n] for n in SMALL}
    S = {n: w[n].reshape(1, -1) for n in ("ffn1_norm", "mix_norm", "ffn2_norm", "final_norm", "pool_scale", "sinks")}
    S["pool_w_mix"] = w["pool_w_mix"][0].astype(BF16)
    exchanges = Exchanges()
    _, dx = local_fwd_bwd(x[0], loss_target[0], S, exchanges)
    loss = exchanges.loss

    def shaped(d, n):
        return (d[n].T if n in TRANSPOSED else d[n]).reshape(w[n].shape)

    return (loss, dx[None], *[shaped(grads, n) for n in WEIGHTS], *[shaped(delta, n) for n in WEIGHTS],
            *[shaped(new_m, n) for n in WEIGHTS], *[shaped(new_v, n) for n in WEIGHTS])
```

```python
import numpy as np
import jax
import jax.numpy as jnp
from jax import lax
from jax.experimental import pallas as pl
from jax.experimental.pallas import tpu as pltpu

F32 = jnp.float32
BF16 = jnp.bfloat16
SDS = jax.ShapeDtypeStruct
MESH = pl.DeviceIdType.MESH

D = 1024
FF = 2816
NQ = 16
NKV = 2
HD = 64
GQ = NQ // NKV
AW = NQ * HD
KVW = NKV * HD
BLK = 128
PW = 512
PG = 128
POOL_WINDOWS = (2, 4, 8, 16)
HALO = 16
INW = AW + 2 * KVW + PW + 2 * D
C_KV = AW
C_Z = AW + 2 * KVW
C_G = C_Z + PW
EPS = 1e-6
FF_CHUNK = 256
FF_CHUNKS = tuple((c, FF_CHUNK) for c in range(0, FF, FF_CHUNK))
SLOPES = tuple(float(2.0 ** (-8.0 * h / NQ)) for h in range(1, NQ + 1))
SCALE = HD ** -0.5

LR, B1, B2, ADAM_EPS, WD, STEP = 0.001, 0.9, 0.999, 1e-08, 0.01, 10

VMEM_LIMIT = 56 * 1024 * 1024
N_CHIPS = 4

NT = (((1,), (1,)), ((), ()))
TN = (((0,), (0,)), ((), ()))


def _cp(sem=None, vmem=VMEM_LIMIT):
    return pltpu.CompilerParams(dimension_semantics=sem, vmem_limit_bytes=vmem)


def _const_spec(shape):
    nd = len(shape)
    return pl.BlockSpec(shape, lambda *_: (0,) * nd, pipeline_mode=pl.Buffered(1))


def _rstd(x):
    return lax.rsqrt(jnp.mean(x * x, axis=-1, keepdims=True) + EPS)


def _rms_bwd(dn, xhat, rstd, g):
    dxhat = dn * g
    return rstd * (dxhat - xhat * jnp.mean(dxhat * xhat, axis=-1, keepdims=True))


def _sigmoid(x):
    return 0.5 * jnp.tanh(0.5 * x) + 0.5


def _dot(a, b):
    return jnp.dot(a, b, preferred_element_type=F32)


def _dot_nt(a, b):
    return lax.dot_general(a, b, NT, preferred_element_type=F32)


def _dot_tn(a, b):
    return lax.dot_general(a, b, TN, preferred_element_type=F32)


def _swiglu_up(x, g, wup_ref, ab_ref, hid_ref):
    n = (x * _rstd(x) * g).astype(BF16)
    for c0, w in FF_CHUNKS:
        a = _dot(n, wup_ref[:, c0:c0 + w])
        b = _dot(n, wup_ref[:, FF + c0:FF + c0 + w])
        sig = _sigmoid(a)
        s = a * sig
        ab_ref[:, c0:c0 + w] = (b * (sig + s * (1.0 - sig))).astype(BF16)
        ab_ref[:, FF + c0:FF + c0 + w] = s.astype(BF16)
        hid_ref[:, c0:c0 + w] = (s * b).astype(BF16)


def ffn_up(h, g, wup, name, after=None):
    T = h.shape[0]
    TM = 512
    tile = lambda w: pl.BlockSpec((TM, w), lambda i: (i, 0))
    token_spec, token_arg = _token_operand(after)

    def body(h_ref, g_ref, wup_ref, *rest):
        _swiglu_up(h_ref[...], g_ref[...], wup_ref, *rest[-2:])

    return pl.pallas_call(
        body, name=name, grid=(T // TM,),
        in_specs=[tile(D), _const_spec((1, D)), _const_spec((D, 2 * FF))] + token_spec,
        out_specs=[tile(2 * FF), tile(FF)],
        out_shape=[SDS((T, 2 * FF), BF16), SDS((T, FF), BF16)],
        compiler_params=_cp(("arbitrary",)),
    )(h, g, wup, *token_arg)


def ffn_down(h, hid, wdn, name, after=None):
    T = h.shape[0]
    TM = min(1024, T)
    tile = lambda w: pl.BlockSpec((TM, w), lambda i: (i, 0))
    token_spec, token_arg = _token_operand(after)

    def body(h_ref, hid_ref, wdn_ref, *rest):
        rest[-1][...] = h_ref[...] + 0.5 * _dot(hid_ref[...], wdn_ref[...])

    return pl.pallas_call(
        body, name=name, grid=(T // TM,),
        in_specs=[tile(D), tile(FF), _const_spec((FF, D))] + token_spec,
        out_specs=tile(D), out_shape=SDS((T, D), F32),
        compiler_params=_cp(("arbitrary",)),
    )(h, hid, wdn, *token_arg)


def ffn_fwd_loss(h, g, wup, wdn, target, g_final, name):
    T = h.shape[0]
    TM = 512
    tile = lambda w: pl.BlockSpec((TM, w), lambda i: (i, 0))
    acc_spec = lambda w: pl.BlockSpec((1, w), lambda i: (0, 0))

    def body(h_ref, g_ref, wup_ref, wdn_ref, t_ref, gf_ref, dout_ref, ab_ref, hid_ref, loss_ref, dgf_ref):
        x = h_ref[...]
        _swiglu_up(x, g_ref[...], wup_ref, ab_ref, hid_ref)
        out = x + 0.5 * _dot(hid_ref[...], wdn_ref[...])
        dout_ref[...] = _loss_head(out, t_ref[...], gf_ref[...], loss_ref, dgf_ref, pl.program_id(0) == 0)

    return pl.pallas_call(
        body, name=name, grid=(T // TM,),
        in_specs=[tile(D), _const_spec((1, D)), _const_spec((D, 2 * FF)), _const_spec((FF, D)), tile(D),
                  _const_spec((1, D))],
        out_specs=[tile(D), tile(2 * FF), tile(FF), acc_spec(1), acc_spec(D)],
        out_shape=[SDS((T, D), F32), SDS((T, 2 * FF), BF16), SDS((T, FF), BF16), SDS((1, 1), F32), SDS((1, D), F32)],
        compiler_params=_cp(("arbitrary",)),
    )(h, g, wup, wdn, target, g_final)


def _loss_head(x, target, g, loss_ref, dg_ref, first):
    @pl.when(first)
    def _():
        loss_ref[...] = jnp.zeros_like(loss_ref)
        dg_ref[...] = jnp.zeros_like(dg_ref)

    rstd = _rstd(x)
    xhat = x * rstd
    err = xhat * g - target
    loss_ref[...] += 0.5 * jnp.sum(jnp.mean(err * err, axis=-1, keepdims=True), axis=0, keepdims=True)
    dy = err * (1.0 / D)
    dg_ref[...] += jnp.sum(dy * xhat, axis=0, keepdims=True)
    return _rms_bwd(dy, xhat, rstd, g)


def _adamw_tile(w_ref, g_ref, m_ref, v_ref, go_ref, d_ref, nm_ref, nv_ref):
    gv = g_ref[...]
    go_ref[...] = gv
    nm = B1 * m_ref[...] + (1.0 - B1) * gv
    nv = B2 * v_ref[...] + (1.0 - B2) * (gv * gv)
    nm_ref[...] = nm
    nv_ref[...] = nv
    d_ref[...] = -LR * ((nm / (1.0 - B1 ** STEP)) / (jnp.sqrt(nv / (1.0 - B2 ** STEP)) + ADAM_EPS) + WD * w_ref[...])


def _riders(updates, steps):
    in_specs, out_specs, out_shapes, operands, tiles = [], [], [], [], []
    for w, g, m, v in updates:
        R, C = w.shape
        n = max(d for d in range(1, steps + 1) if R % d == 0 and (R // d) % 8 == 0)
        spec = pl.BlockSpec((R // n, C), lambda i, n=n: (jnp.minimum(i, n - 1), 0))
        in_specs += [spec] * 4
        out_specs += [spec] * 4
        out_shapes += [SDS((R, C), F32)] * 4
        operands += [w, g, m, v]
        tiles.append(n)

    def run(step, in_refs, out_refs):
        for u, n in enumerate(tiles):
            @pl.when(step < n)
            def _(u=u):
                _adamw_tile(*in_refs[4 * u:4 * u + 4], *out_refs[4 * u:4 * u + 4])

    return in_specs, out_specs, out_shapes, operands, run


def ffn_bwd_x(dh, h_in, ab, g, wup, wdn, name, updates=(), after=None):
    T = dh.shape[0]
    TM = 256 if updates else 512
    SUB = 256
    r_in, r_out, r_shapes, r_args, ride = _riders(updates, T // TM)
    token_spec, token_arg = _token_operand(after)
    n_in = len(r_in) + len(token_spec)

    def body(dh_ref, h_ref, ab_ref, g_ref, wup_ref, wdn_ref, *rest):
        dhin_ref, dup_ref, n_ref, dg_ref = rest[n_in:n_in + 4]
        ride(pl.program_id(0), rest[:len(r_in)], rest[n_in + 4:])
        g = g_ref[...]
        dg = None
        for r0 in range(0, TM, SUB):
            rows = slice(r0, r0 + SUB)
            x = h_ref[rows, :]
            rstd = _rstd(x)
            xhat = x * rstd
            n_ref[rows, :] = (xhat * g).astype(BF16)
            dh = dh_ref[rows, :]
            dhh = (0.5 * dh).astype(BF16)
            for c0, w in FF_CHUNKS:
                dhid = _dot_nt(dhh, wdn_ref[c0:c0 + w, :]).astype(BF16)
                dup_ref[rows, c0:c0 + w] = dhid * ab_ref[rows, c0:c0 + w]
                dup_ref[rows, FF + c0:FF + c0 + w] = dhid * ab_ref[rows, FF + c0:FF + c0 + w]
            dn = _dot_nt(dup_ref[rows, :], wup_ref[...])
            dhin_ref[rows, :] = dh + _rms_bwd(dn, xhat, rstd, g)
            part = jnp.sum(dn * xhat, axis=0, keepdims=True)
            dg = part if dg is None else dg + part

        @pl.when(pl.program_id(0) == 0)
        def _():
            dg_ref[...] = jnp.zeros_like(dg_ref)

        dg_ref[...] += dg

    tile = lambda w: pl.BlockSpec((TM, w), lambda i: (i, 0))
    return pl.pallas_call(
        body, name=name, grid=(T // TM,),
        in_specs=[tile(D), tile(D), tile(2 * FF), _const_spec((1, D)), _const_spec((D, 2 * FF)), _const_spec((FF, D))]
        + r_in + token_spec,
        out_specs=[tile(D), tile(2 * FF), tile(D), pl.BlockSpec((1, D), lambda i: (0, 0))] + r_out,
        out_shape=[SDS((T, D), F32), SDS((T, 2 * FF), BF16), SDS((T, D), BF16), SDS((1, D), F32)] + r_shapes,
        compiler_params=_cp(("arbitrary",)),
    )(dh, h_in, ab, g, wup, wdn, *r_args, *token_arg)


TOKEN_SPEC = pl.BlockSpec((8, 128), lambda *_: (0, 0))


def _token_operand(after):
    tokens = [t for t in (after if isinstance(after, tuple) else (after,)) if t is not None]
    return [TOKEN_SPEC] * len(tokens), tokens


def matmul_tn(a, b, name, *, tm, tn, tt=2048, b_scale=None, after=None):
    T, M = a.shape
    N = b.shape[1]
    tt = min(tt, T)
    assert M % tm == 0 and N % tn == 0 and T % tt == 0
    nt = T // tt
    token_spec, token_arg = _token_operand(after)

    def body(a_ref, b_ref, *rest):
        o_ref, acc_ref = rest[-2:]
        t = pl.program_id(2)

        @pl.when(t == 0)
        def _():
            acc_ref[...] = jnp.zeros_like(acc_ref)

        bv = b_ref[...]
        if b_scale is not None:
            bv = bv * b_scale
        acc_ref[...] += _dot_tn(a_ref[...].astype(BF16), bv.astype(BF16))

        @pl.when(t == nt - 1)
        def _():
            o_ref[...] = acc_ref[...].astype(BF16)

    return pl.pallas_call(
        body, name=name, grid=(M // tm, N // tn, nt),
        in_specs=[pl.BlockSpec((tt, tm), lambda i, j, t: (t, i)), pl.BlockSpec((tt, tn), lambda i, j, t: (t, j))]
        + token_spec,
        out_specs=pl.BlockSpec((tm, tn), lambda i, j, t: (i, j)),
        out_shape=SDS((M, N), BF16),
        scratch_shapes=[pltpu.VMEM((tm, tn), F32)],
        compiler_params=_cp(("parallel", "parallel", "arbitrary")),
    )(a, b, *token_arg)


def mix_in_fwd(h1, g, win_t):
    T = h1.shape[0]
    TM = 512

    def body(h_ref, g_ref, w_ref, u_ref, q_ref, kv_ref, z_ref, gt_ref):
        x = h_ref[...]
        u = (x * _rstd(x) * g_ref[...]).astype(BF16)
        u_ref[...] = u
        for c in range(0, AW, 256):
            q_ref[:, c:c + 256] = _dot_nt(u, w_ref[c:c + 256, :]).astype(BF16)
        kv_ref[...] = _dot_nt(u, w_ref[C_KV:C_Z, :]).astype(BF16)
        for c in range(0, PW, 256):
            z_ref[:, c:c + 256] = _dot_nt(u, w_ref[C_Z + c:C_Z + c + 256, :])
        for c in range(0, 2 * D, 256):
            gt_ref[:, c:c + 256] = _dot_nt(u, w_ref[C_G + c:C_G + c + 256, :]).astype(BF16)

    tile = lambda w: pl.BlockSpec((TM, w), lambda i: (i, 0))
    return pl.pallas_call(
        body, name="mix_in_fwd", grid=(T // TM,),
        in_specs=[tile(D), _const_spec((1, D)), _const_spec((INW, D))],
        out_specs=[tile(D), tile(AW), tile(2 * KVW), tile(PW), tile(2 * D)],
        out_shape=[SDS((T, D), BF16), SDS((T, AW), BF16), SDS((T, 2 * KVW), BF16), SDS((T, PW), F32),
                   SDS((T, 2 * D), BF16)],
        compiler_params=_cp(("arbitrary",)),
    )(h1, g, win_t)


def mix_in_bwd(dq, dkv, dz, dgt, dh2, h1, g, win_t):
    T = h1.shape[0]
    TM = min(1024, T)
    SUB = 256

    def body(dq_ref, dkv_ref, dz_ref, dgt_ref, dh2_ref, h_ref, g_ref, w_ref, dh1_ref, dg_ref):
        g = g_ref[...]
        dg = None
        for r0 in range(0, TM, SUB):
            rows = slice(r0, r0 + SUB)
            du = _dot(dq_ref[rows, :], w_ref[0:AW, :])
            du += _dot(dkv_ref[rows, :], w_ref[C_KV:C_Z, :])
            du += _dot(dz_ref[rows, :], w_ref[C_Z:C_G, :])
            du += _dot(dgt_ref[rows, :], w_ref[C_G:INW, :])
            x = h_ref[rows, :]
            rstd = _rstd(x)
            xhat = x * rstd
            dh1_ref[rows, :] = dh2_ref[rows, :] + _rms_bwd(du, xhat, rstd, g)
            part = jnp.sum(du * xhat, axis=0, keepdims=True)
            dg = part if dg is None else dg + part

        @pl.when(pl.program_id(0) == 0)
        def _():
            dg_ref[...] = jnp.zeros_like(dg_ref)

        dg_ref[...] += dg

    tile = lambda w: pl.BlockSpec((TM, w), lambda i: (i, 0))
    return pl.pallas_call(
        body, name="mix_in_bwd", grid=(T // TM,),
        in_specs=[tile(AW), tile(2 * KVW), tile(PW), tile(2 * D), tile(D), tile(D), _const_spec((1, D)),
                  _const_spec((INW, D))],
        out_specs=[tile(D), pl.BlockSpec((1, D), lambda i: (0, 0))],
        out_shape=[SDS((T, D), F32), SDS((1, D), F32)],
        compiler_params=_cp(("arbitrary",)),
    )(dq, dkv, dz, dgt, dh2, h1, g, win_t)


PAIR = 2 * HD
NPAIR = GQ // 2


def _lo_lanes():
    return lax.broadcasted_iota(jnp.int32, (BLK, PAIR), 1) < HD


def _stack_heads(ref, kvh, scale=None):
    lo = _lo_lanes()
    parts = []
    for pr in range(NPAIR):
        t = ref[:, (kvh * NPAIR + pr) * PAIR:(kvh * NPAIR + pr + 1) * PAIR]
        if scale is not None:
            t = t * scale
        zero = jnp.zeros_like(t)
        parts += [jnp.where(lo, t, zero), jnp.where(lo, zero, t)]
    return jnp.concatenate(parts, axis=0)


def _kv_tiles(kvc_ref, kvp_ref, tile, kvh):
    lo = _lo_lanes()
    dup, left, right = [], [], []
    for ref in (kvp_ref, kvc_ref):
        t = ref[:, tile * PAIR:(tile + 1) * PAIR]
        r = pltpu.roll(t.astype(F32), HD, 1).astype(BF16)
        zero = jnp.zeros_like(t)
        a, b = (t, r) if kvh == 0 else (r, t)
        dup.append(jnp.where(lo, a, b))
        left.append(jnp.where(lo, a, zero))
        right.append(jnp.where(lo, zero, b))
    cat = lambda xs: jnp.concatenate(xs, axis=0)
    return cat(dup), cat(left), cat(right)


def _band_consts(first):
    row = lax.broadcasted_iota(jnp.int32, (BLK, BLK), 0)
    col = lax.broadcasted_iota(jnp.int32, (BLK, BLK), 1)
    upper = col > row
    dist = jnp.where(upper, row - col + BLK, row - col).astype(F32)
    pen = jnp.where(jnp.logical_and(upper, first), -jnp.inf, 0.0)
    return upper, dist, pen


def _split_band(upper, t):
    zero = jnp.zeros_like(t)
    return jnp.concatenate([jnp.where(upper, t, zero), jnp.where(upper, zero, t)], axis=1)


def attn_fwd(q, kv, sinks):
    T = q.shape[0]
    nb = T // BLK

    def body(sink_ref, q_ref, kvc_ref, kvp_ref, att_ref, lse_ref):
        upper, dist, pen = _band_consts(pl.program_id(0) == 0)
        scores, values = [], []
        for kvh in range(NKV):
            kdup, _, _ = _kv_tiles(kvc_ref, kvp_ref, 0, kvh)
            values.append(_kv_tiles(kvc_ref, kvp_ref, 1, kvh)[1:])
            scores.append(_dot_nt(_stack_heads(q_ref, kvh, SCALE), kdup))
        for kvh in range(NKV):
            s_all = scores[kvh]
            vleft, vright = values[kvh]
            for pr in range(NPAIR):
                outs, inv = [], []
                for side, vpad in ((0, vleft), (1, vright)):
                    g = 2 * pr + side
                    hq = kvh * GQ + g
                    sink = sink_ref[0, hq]
                    rows = slice(g * BLK, (g + 1) * BLK)
                    s = jnp.where(upper, s_all[rows, 0:BLK], s_all[rows, BLK:2 * BLK]) - SLOPES[hq] * dist + pen
                    m = jnp.maximum(jnp.max(s, axis=-1, keepdims=True), sink)
                    p = jnp.exp(s - m)
                    l = jnp.sum(p, axis=-1, keepdims=True) + jnp.exp(sink - m)
                    lse_ref[:, hq:hq + 1] = m + jnp.log(l)
                    outs.append(_dot(_split_band(upper, p.astype(BF16)), vpad))
                    inv.append(1.0 / l)
                col0 = (kvh * NPAIR + pr) * PAIR
                att_ref[:, col0:col0 + PAIR] = ((outs[0] + outs[1]) * jnp.where(_lo_lanes(), inv[0], inv[1])).astype(BF16)

    return pl.pallas_call(
        body, name="attn_fwd", grid=(nb,),
        in_specs=[pl.BlockSpec(memory_space=pltpu.SMEM),
                  pl.BlockSpec((BLK, AW), lambda i: (i, 0)),
                  pl.BlockSpec((BLK, 2 * KVW), lambda i: (i, 0)),
                  pl.BlockSpec((BLK, 2 * KVW), lambda i: (jnp.maximum(i - 1, 0), 0))],
        out_specs=[pl.BlockSpec((BLK, AW), lambda i: (i, 0)), pl.BlockSpec((BLK, NQ), lambda i: (i, 0))],
        out_shape=[SDS((T, AW), BF16), SDS((T, NQ), F32)],
        compiler_params=_cp(("arbitrary",)),
    )(sinks, q, kv, kv)


def attn_bwd(q, kv, datt, att, lse, sinks, after=None):
    T = q.shape[0]
    nb = T // BLK
    token_spec, token_arg = _token_operand(after)

    def body(sink_ref, q_ref, kvc_ref, kvp_ref, do_ref, out_ref, lse_ref, *rest):
        dq_ref, dkv_ref, dsink_ref, carry_ref = rest[-4:]
        i = pl.program_id(0)

        @pl.when(i == 0)
        def _():
            dsink_ref[...] = jnp.zeros_like(dsink_ref)
            carry_ref[...] = jnp.zeros_like(carry_ref)

        @pl.when(i < nb)
        def _():
            upper, dist, pen = _band_consts(i == 0)
            lo = _lo_lanes()
            dk_dup, dv_dup = [], []
            staged = []
            for kvh in range(NKV):
                kdup, kleft, kright = _kv_tiles(kvc_ref, kvp_ref, 0, kvh)
                vdup, _, _ = _kv_tiles(kvc_ref, kvp_ref, 1, kvh)
                qs = _stack_heads(q_ref, kvh, SCALE)
                dos = _stack_heads(do_ref, kvh)
                staged.append((kleft, kright, qs, dos, _dot_nt(qs, kdup), _dot_nt(dos, vdup)))
            deltas = []
            for pair in range(NQ // 2):
                cols = slice(pair * PAIR, (pair + 1) * PAIR)
                t = do_ref[:, cols].astype(F32) * out_ref[:, cols].astype(F32)
                deltas += [jnp.sum(jnp.where(lo, t, 0.0), axis=-1, keepdims=True),
                           jnp.sum(jnp.where(lo, 0.0, t), axis=-1, keepdims=True)]
            for kvh in range(NKV):
                kleft, kright, qs, dos, s_all, dp_all = staged[kvh]
                ds_parts, p_parts = [], []
                for pr in range(NPAIR):
                    dq = None
                    for side, kpad in ((0, kleft), (1, kright)):
                        g = 2 * pr + side
                        hq = kvh * GQ + g
                        lse_h = lse_ref[:, hq:hq + 1]
                        rows = slice(g * BLK, (g + 1) * BLK)
                        s = jnp.where(upper, s_all[rows, 0:BLK], s_all[rows, BLK:2 * BLK]) - SLOPES[hq] * dist + pen
                        p = jnp.exp(s - lse_h)
                        dp = jnp.where(upper, dp_all[rows, 0:BLK], dp_all[rows, BLK:2 * BLK])
                        delta = deltas[hq]
                        dsink_ref[:, hq:hq + 1] += -jnp.sum(jnp.exp(sink_ref[0, hq] - lse_h) * delta, axis=0,
                                                            keepdims=True)
                        ds = _split_band(upper, (p * (dp - delta)).astype(BF16))
                        ds_parts.append(ds)
                        p_parts.append(_split_band(upper, p.astype(BF16)))
                        d = _dot(ds, kpad)
                        dq = d if dq is None else dq + d
                    col0 = (kvh * NPAIR + pr) * PAIR
                    dq_ref[:, col0:col0 + PAIR] = (dq * SCALE).astype(BF16)
                dkw = _dot_tn(qs, jnp.concatenate(ds_parts, axis=0)).T
                dvw = _dot_tn(dos, jnp.concatenate(p_parts, axis=0)).T
                dk_dup.append(dkw + pltpu.roll(dkw, HD, 1))
                dv_dup.append(dvw + pltpu.roll(dvw, HD, 1))
            dk = jnp.where(jnp.concatenate([lo, lo], axis=0), dk_dup[0], dk_dup[1])
            dv = jnp.where(jnp.concatenate([lo, lo], axis=0), dv_dup[0], dv_dup[1])
            dkv_ref[:, 0:PAIR] = (carry_ref[:, 0:PAIR] + dk[0:BLK]).astype(BF16)
            dkv_ref[:, PAIR:2 * PAIR] = (carry_ref[:, PAIR:2 * PAIR] + dv[0:BLK]).astype(BF16)
            carry_ref[:, 0:PAIR] = dk[BLK:2 * BLK]
            carry_ref[:, PAIR:2 * PAIR] = dv[BLK:2 * BLK]

        @pl.when(i == nb)
        def _():
            dkv_ref[...] = carry_ref[...].astype(BF16)

    cur = lambda i: (jnp.minimum(i, nb - 1), 0)
    prev = lambda i: (jnp.maximum(jnp.minimum(i, nb - 1) - 1, 0), 0)
    return pl.pallas_call(
        body, name="attn_bwd", grid=(nb + 1,),
        in_specs=[pl.BlockSpec(memory_space=pltpu.SMEM),
                  pl.BlockSpec((BLK, AW), cur), pl.BlockSpec((BLK, 2 * KVW), cur), pl.BlockSpec((BLK, 2 * KVW), prev),
                  pl.BlockSpec((BLK, AW), cur), pl.BlockSpec((BLK, AW), cur), pl.BlockSpec((BLK, NQ), cur)] + token_spec,
        out_specs=[pl.BlockSpec((BLK, AW), cur),
                   pl.BlockSpec((BLK, 2 * KVW), lambda i: (jnp.maximum(i - 1, 0), 0)),
                   pl.BlockSpec((1, NQ), lambda i: (0, 0))],
        out_shape=[SDS((T, AW), BF16), SDS((T, 2 * KVW), BF16), SDS((1, NQ), F32)],
        scratch_shapes=[pltpu.VMEM((BLK, 2 * KVW), F32)],
        compiler_params=_cp(("arbitrary",)),
    )(sinks, q, kv, kv, datt, att, lse, *token_arg)


def _inv_counts(t0, rows):
    t = (t0 + lax.broadcasted_iota(jnp.int32, (rows, 1), 0) + 1).astype(F32)
    return [1.0 / jnp.minimum(t, float(w)) for w in POOL_WINDOWS]


def pool_fwd(z, wmix, scale, after=None):
    T = z.shape[0]
    TM = min(1024, T)
    L = TM + HALO
    token_spec, token_arg = _token_operand(after)

    def body(z_ref, halo_ref, wmix_ref, scale_ref, *rest):
        pooled_ref, mixs_ref = rest[-2:]
        i = pl.program_id(0)
        halo = jnp.where(i > 0, halo_ref[...], 0.0)
        zt = z_ref[...]
        e = jnp.concatenate([halo, zt], axis=0)
        sums = []
        s = e
        for k in (1, 2, 4, 8):
            s = s + pltpu.roll(s, k, 0)
            sums.append(s)
        inv = _inv_counts(i * TM, TM)
        for gi in range(len(POOL_WINDOWS)):
            cols = slice(gi * PG, (gi + 1) * PG)
            pooled = (sums[gi][HALO:, cols] * inv[gi] - zt[:, cols]).astype(BF16)
            pooled_ref[:, cols] = pooled
            mixs_ref[:, cols] = (_dot(pooled, wmix_ref[gi]) * scale_ref[:, cols]).astype(BF16)

    return pl.pallas_call(
        body, name="pool_fwd", grid=(T // TM,),
        in_specs=[pl.BlockSpec((TM, PW), lambda i: (i, 0)),
                  pl.BlockSpec((HALO, PW), lambda i: (jnp.maximum(i * (TM // HALO) - 1, 0), 0)),
                  _const_spec((len(POOL_WINDOWS), PG, PG)), _const_spec((1, PW))] + token_spec,
        out_specs=[pl.BlockSpec((TM, PW), lambda i: (i, 0)), pl.BlockSpec((TM, PW), lambda i: (i, 0))],
        out_shape=[SDS((T, PW), BF16), SDS((T, PW), BF16)],
        compiler_params=_cp(("arbitrary",)),
    )(z, z, wmix, scale, *token_arg)


def pool_bwd(dmixs, pooled, wmix, scale, after=None):
    T = dmixs.shape[0]
    TM = min(1024, T)
    L = TM + HALO
    nt = T // TM
    token_spec, token_arg = _token_operand(after)

    def body(dm_ref, halo_ref, pooled_ref, wmix_ref, scale_ref, *rest):
        dz_ref, dwmix_ref, dscale_ref = rest[-3:]
        i = pl.program_id(0)

        @pl.when(i == 0)
        def _():
            dwmix_ref[...] = jnp.zeros_like(dwmix_ref)
            dscale_ref[...] = jnp.zeros_like(dscale_ref)

        halo = jnp.where(i < nt - 1, halo_ref[...], 0.0)
        dm = dm_ref[...]
        e = jnp.concatenate([dm, halo], axis=0)
        inv = _inv_counts(i * TM, L)
        for gi in range(len(POOL_WINDOWS)):
            cols = slice(gi * PG, (gi + 1) * PG)
            w = wmix_ref[gi]
            dmixed = (e[:, cols] * scale_ref[:, cols]).astype(BF16)
            dpooled = _dot_nt(dmixed, w)
            pooled = pooled_ref[:, cols]
            mixed = _dot(pooled, w)
            dscale_ref[:, cols] += jnp.sum(dm[:, cols] * mixed, axis=0, keepdims=True)
            dwmix_ref[gi] += _dot_tn(pooled, dmixed[:TM])
            s = dpooled * inv[gi]
            k = 1
            while k < POOL_WINDOWS[gi]:
                s = s + pltpu.roll(s, L - k, 0)
                k *= 2
            dz_ref[:, cols] = (s[:TM] - dpooled[:TM]).astype(BF16)

    return pl.pallas_call(
        body, name="pool_bwd", grid=(nt,),
        in_specs=[pl.BlockSpec((TM, PW), lambda i: (i, 0)),
                  pl.BlockSpec((HALO, PW), lambda i: (jnp.minimum((i + 1) * (TM // HALO), T // HALO - 1), 0)),
                  pl.BlockSpec((TM, PW), lambda i: (i, 0)),
                  _const_spec((len(POOL_WINDOWS), PG, PG)), _const_spec((1, PW))] + token_spec,
        out_specs=[pl.BlockSpec((TM, PW), lambda i: (i, 0)),
                   pl.BlockSpec((len(POOL_WINDOWS), PG, PG), lambda i: (0, 0, 0)),
                   pl.BlockSpec((1, PW), lambda i: (0, 0))],
        out_shape=[SDS((T, PW), BF16), SDS((len(POOL_WINDOWS), PG, PG), F32), SDS((1, PW), F32)],
        compiler_params=_cp(("arbitrary",)),
    )(dmixs, dmixs, pooled, wmix, scale, *token_arg)


def merge_fwd(att, mixs, gt, h1, wattn, wpool, wout):
    T = h1.shape[0]
    TM = 512

    def body(att_ref, mixs_ref, gt_ref, h_ref, wa_ref, wp_ref, wo_ref, h2_ref, mg_ref, gf_ref):
        a = _dot(att_ref[...], wa_ref[...])
        p = _dot(mixs_ref[...], wp_ref[...])
        sa = _sigmoid(gt_ref[:, 0:D].astype(F32))
        sp = _sigmoid(gt_ref[:, D:2 * D].astype(F32))
        gf_ref[:, 0:D] = (a * sa * (1.0 - sa)).astype(BF16)
        gf_ref[:, D:2 * D] = (p * sp * (1.0 - sp)).astype(BF16)
        mg = (sa * a + sp * p).astype(BF16)
        mg_ref[...] = mg
        h2_ref[...] = h_ref[...] + _dot(mg, wo_ref[...])

    tile = lambda w: pl.BlockSpec((TM, w), lambda i: (i, 0))
    return pl.pallas_call(
        body, name="merge_fwd", grid=(T // TM,),
        in_specs=[tile(AW), tile(PW), tile(2 * D), tile(D), _const_spec((AW, D)), _const_spec((PW, D)),
                  _const_spec((D, D))],
        out_specs=[tile(D), tile(D), tile(2 * D)],
        out_shape=[SDS((T, D), F32), SDS((T, D), BF16), SDS((T, 2 * D), BF16)],
        compiler_params=_cp(("arbitrary",)),
    )(att, mixs, gt, h1, wattn, wpool, wout)


def merge_bwd(dh2, gt, gf, wattn, wpool, wout, after=None):
    T = dh2.shape[0]
    TM = 512
    steps = T // TM
    RING = 3
    token_spec, token_arg = _token_operand(after)
    n_tok = len(token_spec)

    def body(dh2_hbm, gt_hbm, gf_hbm, wa_ref, wp_ref, wo_ref, *rest):
        datt_ref, dmixs_ref, dgt_ref, da_ref, dp_ref = rest[n_tok:n_tok + 5]
        bufs, sems = rest[n_tok + 5:n_tok + 8], rest[n_tok + 8]
        s = pl.program_id(0)

        def fetch(step):
            slot = step % RING
            rows = pl.ds(pl.multiple_of(step * TM, TM), TM)
            return [pltpu.make_async_copy(src.at[rows], buf.at[slot], sems.at[i, slot])
                    for i, (src, buf) in enumerate(zip((dh2_hbm, gt_hbm, gf_hbm), bufs))]

        @pl.when(s == 0)
        def _():
            for first in range(min(RING - 1, steps)):
                for copy in fetch(first):
                    copy.start()

        @pl.when(s + RING - 1 < steps)
        def _():
            for copy in fetch(s + RING - 1):
                copy.start()

        for copy in fetch(s):
            copy.wait()
        dh2_ref, gt_ref, gf_ref = (buf.at[s % RING] for buf in bufs)
        dm = _dot_nt(dh2_ref[...].astype(BF16), wo_ref[...])
        da = (dm * _sigmoid(gt_ref[:, 0:D].astype(F32))).astype(BF16)
        dp = (dm * _sigmoid(gt_ref[:, D:2 * D].astype(F32))).astype(BF16)
        da_ref[...] = da
        dp_ref[...] = dp
        dgt_ref[:, 0:D] = (dm * gf_ref[:, 0:D].astype(F32)).astype(BF16)
        dgt_ref[:, D:2 * D] = (dm * gf_ref[:, D:2 * D].astype(F32)).astype(BF16)
        datt_ref[...] = _dot_nt(da, wa_ref[...]).astype(BF16)
        dmixs_ref[...] = _dot_nt(dp, wp_ref[...])

    tile = lambda w: pl.BlockSpec((TM, w), lambda i: (i, 0))
    return pl.pallas_call(
        body, name="merge_bwd", grid=(T // TM,),
        in_specs=[pl.BlockSpec(memory_space=pl.ANY)] * 3 + [_const_spec((AW, D)), _const_spec((PW, D)),
                                                            _const_spec((D, D))] + token_spec,
        out_specs=[tile(AW), tile(PW), tile(2 * D), tile(D), tile(D)],
        out_shape=[SDS((T, AW), BF16), SDS((T, PW), F32), SDS((T, 2 * D), BF16), SDS((T, D), BF16),
                   SDS((T, D), BF16)],
        scratch_shapes=[pltpu.VMEM((RING, TM, D), F32), pltpu.VMEM((RING, TM, 2 * D), BF16),
                        pltpu.VMEM((RING, TM, 2 * D), BF16), pltpu.SemaphoreType.DMA((3, RING))],
        compiler_params=_cp(("arbitrary",)),
    )(dh2, gt, gf, wattn, wpool, wout, *token_arg)


def adamw(updates, place, name, after=None):
    tile_bytes = 2 * 1024 * 1024 // len(updates)
    jobs = []
    for w, g, m, v in updates:
        R, C = w.shape
        tr = R
        if R * C * 4 > tile_bytes:
            tr = next(cand for cand in (512, 256, 128, 64, 32, 16, 8) if R % cand == 0 and cand * C * 4 <= tile_bytes)
        blk = ((tr, C), lambda ls, p: (ls, 0))
        jobs.append((R // tr, [(a, *blk) for a in (w, g, m, v)], [((R, C), F32, *blk)] * 4,
                     lambda ins, outs: _adamw_tile(*ins, *outs)))
    return _run_jobs(jobs, place, name, after)


GROUP_UP1, GROUP_MIX, GROUP_FFN2 = 0, 1, 2


def local_fwd_bwd(x, target, S, comm):
    w_up1 = comm.weights(GROUP_UP1, None)["ffn1_w_up"]
    ab1, hid1 = ffn_up(x, S["ffn1_norm"], w_up1, "ffn1_up", after=comm.started())
    W = comm.weights(GROUP_MIX, hid1)
    h1 = ffn_down(x, hid1, W["ffn1_w_down"], "ffn1_down", after=comm.started())
    u, q, kv, z, gt = mix_in_fwd(h1, S["mix_norm"], W["w_in"])
    att, lse = attn_fwd(q, kv, S["sinks"])
    pooled, mixs = pool_fwd(z, S["pool_w_mix"], S["pool_scale"], after=comm.prefetch(GROUP_FFN2, att))
    h2, merged, gf = merge_fwd(att, mixs, gt, h1, W["w_attn_up"], W["w_pool_up"], W["w_out"])
    W = comm.weights(GROUP_FFN2, h2)
    dh3, ab2, hid2, loss, g_final = ffn_fwd_loss(h2, S["ffn2_norm"], W["ffn2_w_up"], W["ffn2_w_down"], target,
                                                 S["final_norm"], "ffn2_fwd")

    G = {"final_norm": g_final}
    dh2, dup2, n2, G["ffn2_norm"] = ffn_bwd_x(dh3, h2, ab2, S["ffn2_norm"], W["ffn2_w_up"], W["ffn2_w_down"],
                                              "ffn2_bwd_x")
    ffn2 = ("ffn2_w_down", "ffn2_w_up")
    token = comm.grads({"ffn2_w_down": matmul_tn(hid2, dh3, "ffn2_dw_down", tm=1408, tn=512, tt=4096, b_scale=0.5),
                        "ffn2_w_up": matmul_tn(n2, dup2, "ffn2_dw_up", tm=D, tn=512, tt=4096)})

    W = comm.weights(GROUP_MIX, None)
    datt, dmixs, dgt, da, dp = merge_bwd(dh2, gt, gf, W["w_attn_up"], W["w_pool_up"], W["w_out"], after=token)
    token = comm.advance(ffn2, datt)
    g_mix = {"w_out": matmul_tn(merged, dh2, "dw_out", tm=D, tn=D, after=token),
             "w_attn_up": matmul_tn(att, da, "dw_attn_up", tm=AW, tn=D),
             "w_pool_up": matmul_tn(mixs, dp, "dw_pool_up", tm=PW, tn=D)}
    dz, G["pool_w_mix"], G["pool_scale"] = pool_bwd(dmixs, pooled, S["pool_w_mix"], S["pool_scale"], after=token)
    dq, dkv, G["sinks"] = attn_bwd(q, kv, datt, att, lse, S["sinks"], after=token)
    dh1, G["mix_norm"] = mix_in_bwd(dq, dkv, dz, dgt, dh2, h1, S["mix_norm"], W["w_in"])
    g_mix["w_in"] = jnp.concatenate([
        matmul_tn(dq, u, "dw_in_q", tm=AW, tn=D),
        matmul_tn(dkv, u, "dw_in_kv", tm=2 * KVW, tn=D),
        matmul_tn(dz, u, "dw_in_z", tm=PW, tn=D),
        matmul_tn(dgt, u, "dw_in_g", tm=D, tn=512, tt=4096),
    ], axis=0)
    mix = tuple(g_mix)
    token = comm.grads(g_mix)

    g_dn1 = matmul_tn(hid1, dh1, "ffn1_dw_down", tm=1408, tn=512, tt=4096, b_scale=0.5, after=token)
    token = comm.advance(mix, g_dn1)
    token2 = comm.grads({"ffn1_w_down": g_dn1})
    dx, dup1, n1, G["ffn1_norm"], *outs = ffn_bwd_x(dh1, x, ab1, S["ffn1_norm"], w_up1, W["ffn1_w_down"], "ffn1_bwd_x",
                                                   updates=comm.reduce(ffn2, token2), after=(token, token2))
    comm.updated(ffn2, outs)
    token = comm.advance(("ffn1_w_down",), dx)
    token2 = comm.small_start(G, loss, dx)
    g_up1 = matmul_tn(n1, dup1, "ffn1_dw_up", tm=D, tn=512, tt=4096, after=(token, token2))
    token = comm.grads({"ffn1_w_up": g_up1})
    updates = comm.reduce(("ffn1_w_down",), token)
    token = comm.advance(("ffn1_w_up",), [u[1] for u in updates])
    done = comm.update(("ffn1_w_down",), updates, token)
    done = comm.finish(mix, done)
    done = comm.small_finish(done)
    comm.finish(("ffn1_w_up",), done)
    return loss, dx


HBM_SPEC = pl.BlockSpec(memory_space=pltpu.HBM)


def _me():
    return lax.axis_index("x"), lax.axis_index("y"), lax.axis_index("c")


def _peer_chip(x, y, k):
    return x ^ (k >> 1), y ^ (k & 1)


def _piece_half(ref, rowlike, j, h):
    if rowlike:
        return ref.at[j, h]
    ns = ref.shape[-1] // N_CHIPS
    return ref.at[h, :, pl.ds(pl.multiple_of(j * ns, 128), ns)]


def _piece(ref, rowlike, j):
    if rowlike:
        return ref.at[j]
    ns = ref.shape[-1] // N_CHIPS
    return ref.at[:, :, pl.ds(pl.multiple_of(j * ns, 128), ns)]


def _full_shape(shard_view, rowlike):
    _, kh, ns = shard_view.shape
    return (N_CHIPS, 2, kh, ns) if rowlike else (2, kh, N_CHIPS * ns)


def _remote(src, dst, send_sem, recv_sem, dev):
    return pltpu.make_async_remote_copy(src, dst, send_sem, recv_sem, device_id=dev, device_id_type=MESH)


SEM_SPEC = pl.BlockSpec(memory_space=pltpu.SEMAPHORE)
SPLIT_PARAMS = pltpu.CompilerParams(has_side_effects=pltpu.SideEffectType.DATAFLOW_SIDE_EFFECTING)


def _gather_plan(rowlikes):
    def plan(s_refs, f_refs, a, k, x, y, c):
        if k == 0:
            own = _piece(f_refs[a], rowlikes[a], 2 * x + y)
            return s_refs[a], own, own, (x, y, 1 - c)
        px, py = _peer_chip(x, y, k)
        return (s_refs[a].at[c], _piece_half(f_refs[a], rowlikes[a], 2 * x + y, c),
                _piece_half(f_refs[a], rowlikes[a], 2 * px + py, c), (px, py, c))
    return plan


def _all_to_all_plan(rowlikes):
    def plan(q_refs, r_refs, a, k, x, y, c):
        px, py = _peer_chip(x, y, k)
        if rowlikes[a]:
            src = q_refs[a].at[2 * px + py]
        else:
            ns = q_refs[a].shape[-1] // N_CHIPS
            src = q_refs[a].at[:, pl.ds(pl.multiple_of((2 * px + py) * ns, 128), ns)]
        return src, r_refs[a].at[k - 1], r_refs[a].at[k - 1], (px, py, c)
    return plan


def _swap_plan(rowlikes):
    def plan(g_refs, got_refs, a, k, x, y, c):
        src = g_refs[a].at[:, 1 - c] if rowlikes[a] else g_refs[a].at[1 - c]
        return src, got_refs[a], got_refs[a], (x, y, 1 - c)
    return plan


def _everyone_plan(s_refs, slot_refs, a, k, x, y, c):
    px, py, pc = x ^ (k >> 2), y ^ ((k >> 1) & 1), c ^ (k & 1)
    return s_refs[a], slot_refs[a].at[4 * x + 2 * y + c], slot_refs[a].at[4 * px + 2 * py + pc], (px, py, pc)


def _forward_plan(rowlikes):
    def plan(s_refs, f_refs, a, k, x, y, c):
        px, py = _peer_chip(x, y, k)
        mine = _piece_half(f_refs[a], rowlikes[a], 2 * px + py, c)
        return mine, mine, _piece_half(f_refs[a], rowlikes[a], 2 * px + py, 1 - c), (x, y, 1 - c)
    return plan


CHIPS, SIBLING, EVERYONE, GATHERS = (1, 2, 3), (1,), tuple(range(1, 8)), (0, 1, 2, 3)


def _as_list(after):
    return [] if after is None else list(after) if isinstance(after, (list, tuple)) else [after]


def exchange_start(srcs, land_shapes, plan, after, name, peers=CHIPS):
    n = len(srcs)
    lands = [l if hasattr(l, "dtype") else lax.empty(l, s.dtype) for l, s in zip(land_shapes, srcs)]

    behind = _as_list(after)

    def body(*refs):
        s_refs, l_refs = refs[:n], refs[n:2 * n]
        send_sems, recv_sems = refs[2 * n + len(behind)], refs[2 * n + len(behind) + 1]
        token = refs[-1]
        x, y, c = _me()
        for a in range(n):
            for i, k in enumerate(peers):
                src, dst, _, peer = plan(s_refs, l_refs, a, k, x, y, c)
                sem = len(peers) * a + i
                _remote(src, dst, send_sems.at[sem], recv_sems.at[sem], peer).start()
        token[...] = jnp.zeros_like(token)

    n_sems = len(peers) * n
    outs = pl.pallas_call(
        body, name=name, in_specs=[HBM_SPEC] * (2 * n) + [pl.BlockSpec(memory_space=pl.ANY)] * len(behind),
        out_specs=[SEM_SPEC, SEM_SPEC] + [HBM_SPEC] * (2 * n) + [pl.BlockSpec(memory_space=pltpu.VMEM)],
        out_shape=[pltpu.SemaphoreType.DMA((n_sems,)), pltpu.SemaphoreType.DMA((n_sems,))]
        + [pltpu.HBM(a.shape, a.dtype) for a in (*srcs, *lands)] + [SDS((8, 128), F32)],
        input_output_aliases={i: 2 + i for i in range(2 * n)},
        compiler_params=SPLIT_PARAMS,
    )(*[pltpu.with_memory_space_constraint(a, pltpu.HBM) for a in (*srcs, *lands)], *behind)
    return {"sems": outs[:2], "srcs": outs[2:2 + n], "lands": outs[2 + n:2 + 2 * n], "token": outs[-1]}


def exchange_wait(state, plan, after, name, peers=CHIPS):
    n = len(state["srcs"])
    behind = _as_list(after)

    def body(*refs):
        s_refs, l_refs = refs[:n], refs[n:2 * n]
        send_sems, recv_sems = refs[2 * n], refs[2 * n + 1]
        x, y, c = _me()
        for a in range(n):
            for i, k in enumerate(peers):
                src, _, landing, peer = plan(s_refs, l_refs, a, k, x, y, c)
                sem = len(peers) * a + i
                cp = _remote(src, landing, send_sems.at[sem], recv_sems.at[sem], peer)
                cp.wait_send()
                cp.wait_recv()

    bufs = (*state["srcs"], *state["lands"])
    outs = pl.pallas_call(
        body, name=name,
        in_specs=[HBM_SPEC] * (2 * n) + [SEM_SPEC, SEM_SPEC] + [pl.BlockSpec(memory_space=pl.ANY)] * len(behind),
        out_specs=[HBM_SPEC] * (2 * n),
        out_shape=[pltpu.HBM(a.shape, a.dtype) for a in bufs],
        input_output_aliases={i: i for i in range(2 * n)},
        compiler_params=SPLIT_PARAMS,
    )(*bufs, *state["sems"], *behind)
    return outs[:n], outs[n:]


def gather_finish(fulls, rowlikes, name):
    n = len(fulls)

    def body(*refs):
        f_refs = refs[n:2 * n]
        send_sems, recv_sems = refs[2 * n:]
        x, y, c = _me()
        sib = (x, y, 1 - c)
        sends = []
        for a in range(n):
            for k in CHIPS:
                px, py = _peer_chip(x, y, k)
                slot = _piece_half(f_refs[a], rowlikes[a], 2 * px + py, c)
                cp = _remote(slot, slot, send_sems.at[a, k - 1], recv_sems.at[a, k - 1], sib)
                cp.start()
                sends.append(cp)
        for a in range(n):
            for k in CHIPS:
                px, py = _peer_chip(x, y, k)
                slot = _piece_half(f_refs[a], rowlikes[a], 2 * px + py, 1 - c)
                _remote(slot, slot, send_sems.at[a, k - 1], recv_sems.at[a, k - 1], sib).wait_recv()
        for cp in sends:
            cp.wait_send()

    return pl.pallas_call(
        body, name=name, in_specs=[HBM_SPEC] * n, out_specs=[HBM_SPEC] * n,
        out_shape=[SDS(f.shape, f.dtype) for f in fulls],
        input_output_aliases={a: a for a in range(n)},
        scratch_shapes=[pltpu.SemaphoreType.DMA((n, len(CHIPS))), pltpu.SemaphoreType.DMA((n, len(CHIPS)))],
    )(*fulls)


def _half_buffer_shape(gview, rowlike):
    return (N_CHIPS,) + gview.shape[2:] if rowlike else gview.shape[1:]


def _row_tiles(kh, ns):
    return 1 if kh * ns <= 256 * 1024 else 2


def _run_jobs(jobs, place, name, after=None):
    n_steps = max(job[0] for job in jobs)

    def spec(block, index, steps):
        return pl.BlockSpec(block, lambda s, p: index(jnp.minimum(s, steps - 1), p))

    in_specs = [spec(b, ix, steps) for steps, ins, _, _ in jobs for _, b, ix in ins]
    out_specs = [spec(b, ix, steps) for steps, _, outs, _ in jobs for _, _, b, ix in outs]
    token_spec, token_arg = _token_operand(after)
    n_in = len(in_specs) + len(token_spec)

    def body(place_ref, *refs):
        s = pl.program_id(0)
        i, o = 0, n_in
        for steps, ins, outs, fn in jobs:
            in_refs, out_refs = refs[i:i + len(ins)], refs[o:o + len(outs)]
            i, o = i + len(ins), o + len(outs)

            @pl.when(s < steps)
            def _(fn=fn, in_refs=in_refs, out_refs=out_refs):
                fn(in_refs, out_refs)

    return pl.pallas_call(
        body, name=name,
        grid_spec=pltpu.PrefetchScalarGridSpec(num_scalar_prefetch=1, grid=(n_steps,),
                                               in_specs=in_specs + token_spec, out_specs=out_specs),
        out_shape=[SDS(shape, dtype) for _, _, outs, _ in jobs for shape, dtype, _, _ in outs],
        compiler_params=_cp(("arbitrary",)),
    )(place, *[a for _, ins, _, _ in jobs for a, _, _ in ins], *token_arg)


def add_halves(gviews, gots, rowlikes, place, name):
    def add(ins, outs):
        outs[0][...] = (ins[0][...].astype(F32) + ins[1][...].astype(F32)).astype(BF16)

    jobs = []
    for g, got, rowlike in zip(gviews, gots, rowlikes):
        kh, ns = (g.shape[2], g.shape[3]) if rowlike else (g.shape[1], g.shape[2] // N_CHIPS)
        rt = _row_tiles(kh, ns)
        tr = kh // rt
        if rowlike:
            g_in = (g, (None, None, tr, ns), lambda ls, p, rt=rt: (ls // rt, p[0], ls % rt, 0))
            half = ((None, tr, ns), lambda ls, p, rt=rt: (ls // rt, ls % rt, 0))
        else:
            g_in = (g, (None, tr, ns), lambda ls, p, rt=rt: (p[0], ls % rt, ls // rt))
            half = ((tr, ns), lambda ls, p, rt=rt: (ls % rt, ls // rt))
        jobs.append((N_CHIPS * rt, [g_in, (got, *half)], [(got.shape, BF16, *half)], add))
    return _run_jobs(jobs, place, name)


def _slot_shape(q, rowlike):
    return (3,) + (q.shape[1:] if rowlike else (q.shape[0], q.shape[1] // N_CHIPS))


def sum_pieces(qs, recvs, rowlikes, place, name):
    def total(ins, outs):
        acc = ins[0][...].astype(F32)
        for k in range(3):
            acc = acc + ins[1][k].astype(F32)
        outs[0][...] = acc

    jobs = []
    for q, recv, rowlike in zip(qs, recvs, rowlikes):
        kh, ns = recv.shape[1], recv.shape[2]
        rt = _row_tiles(kh, ns)
        tr = kh // rt
        mine = ((None, tr, ns), lambda ls, p: (p[1], ls, 0)) if rowlike else ((tr, ns), lambda ls, p: (ls, p[1]))
        jobs.append((rt, [(q, *mine), (recv, (3, tr, ns), lambda ls, p: (0, ls, 0))],
                     [((2, kh, ns), F32, (None, tr, ns), lambda ls, p: (p[0], ls, 0))], total))
    return _run_jobs(jobs, place, name)


def join_halves(halves, name):
    n = len(halves)

    def body(*refs):
        o_refs = refs[n:2 * n]
        send_sems, recv_sems = refs[2 * n:]
        x, y, c = _me()
        sib = (x, y, 1 - c)
        sends = []
        for a in range(n):
            cp = _remote(o_refs[a].at[c], o_refs[a].at[c], send_sems.at[a], recv_sems.at[a], sib)
            cp.start()
            sends.append(cp)
        for a in range(n):
            got = o_refs[a].at[1 - c]
            _remote(got, got, send_sems.at[a], recv_sems.at[a], sib).wait_recv()
        for cp in sends:
            cp.wait_send()

    return pl.pallas_call(
        body, name=name, in_specs=[HBM_SPEC] * n, out_specs=[HBM_SPEC] * n,
        out_shape=[SDS(h.shape, h.dtype) for h in halves],
        input_output_aliases={a: a for a in range(n)},
        scratch_shapes=[pltpu.SemaphoreType.DMA((n,)), pltpu.SemaphoreType.DMA((n,))],
    )(*halves)


N_DEV = 8


def sum_devices(s, slots, me):
    R, Wd = s.shape

    def body(me_ref, s_ref, slots_ref, out_ref):
        acc = None
        for d in range(N_DEV):
            mine = me_ref[0] == d
            term = jnp.where(mine, s_ref[...], slots_ref[jnp.where(mine, d ^ 1, d)])
            acc = term if acc is None else acc + term
        out_ref[...] = acc

    vmem = pl.BlockSpec(memory_space=pltpu.VMEM)
    return pl.pallas_call(
        body, name="sum_devices", in_specs=[pl.BlockSpec(memory_space=pltpu.SMEM), vmem, vmem], out_specs=vmem,
        out_shape=SDS((R, Wd), F32),
    )(me, s, slots)


TRANSPOSED = ("w_in",)
BIG = {"ffn1_w_up": (D, 2 * FF, "col"), "ffn1_w_down": (FF, D, "row"), "w_in": (INW, D, "row"),
       "w_attn_up": (AW, D, "row"), "w_pool_up": (PW, D, "col"), "w_out": (D, D, "row"),
       "ffn2_w_up": (D, 2 * FF, "col"), "ffn2_w_down": (FF, D, "row")}
GROUPS = (("ffn1_w_up",), ("ffn1_w_down", "w_in", "w_attn_up", "w_pool_up", "w_out"), ("ffn2_w_up", "ffn2_w_down"))
SMALL = ("ffn1_norm", "mix_norm", "ffn2_norm", "final_norm", "pool_scale", "sinks", "pool_w_mix")
SMALL_W = 128


def _rowlike(name):
    return BIG[name][2] == "row"


def _half_dims(name):
    k, n, kind = BIG[name]
    return (k // N_CHIPS // 2, n) if kind == "row" else (k // 2, n // N_CHIPS)


def shard_view(name, shard):
    return shard.reshape((2,) + _half_dims(name))


def full_from_view(name, fv):
    k, n, _ = BIG[name]
    return fv.reshape(k, n)


def grad_view(name, g):
    kh, ns = _half_dims(name)
    return g.reshape(_full_shape(jax.ShapeDtypeStruct((2, kh, ns), g.dtype), _rowlike(name)))


def pack_small(d):
    parts = []
    for name in SMALL:
        a = d[name].reshape(-1)
        pad = (-a.shape[0]) % SMALL_W
        parts.append(jnp.pad(a, (0, pad)).reshape(-1, SMALL_W))
    a = jnp.concatenate(parts, axis=0)
    return jnp.pad(a, ((0, (-a.shape[0]) % 8), (0, 0)))


def unpack_small(a, like):
    out, r0 = {}, 0
    for name in SMALL:
        size = int(np.prod(like[name].shape))
        rows = -(-size // SMALL_W)
        out[name] = a[r0:r0 + rows].reshape(-1)[:size].reshape(like[name].shape)
        r0 += rows
    return out


WEIGHTS = ("ffn1_norm", "ffn1_w_up", "ffn1_w_down", "mix_norm", "w_in", "sinks", "w_attn_up", "pool_w_mix",
           "pool_scale", "w_pool_up", "w_out", "ffn2_norm", "ffn2_w_up", "ffn2_w_down", "final_norm")


def kernel(x, ffn1_norm, ffn1_w_up, ffn1_w_down, mix_norm, w_in, sinks, w_attn_up, pool_w_mix, pool_scale, w_pool_up, w_out, ffn2_norm, ffn2_w_up, ffn2_w_down, final_norm, loss_target, m_ffn1_norm, m_ffn1_w_up, m_ffn1_w_down, m_mix_norm, m_w_in, m_sinks, m_w_attn_up, m_pool_w_mix, m_pool_scale, m_w_pool_up, m_w_out, m_ffn2_norm, m_ffn2_w_up, m_ffn2_w_down, m_final_norm, v_ffn1_norm, v_ffn1_w_up, v_ffn1_w_down, v_mix_norm, v_w_in, v_sinks, v_w_attn_up, v_pool_w_mix, v_pool_scale, v_w_pool_up, v_w_out, v_ffn2_norm, v_ffn2_w_up, v_ffn2_w_down, v_final_norm):
    given = dict(locals())
    w = {n: given[n] for n in WEIGHTS}
    m = {n: given["m_" + n] for n in WEIGHTS}
    v = {n: given["v_" + n] for n in WEIGHTS}
    cx, cy, cc = _me()
    place = jnp.stack([cc, 2 * cx + cy]).astype(jnp.int32)

    def local2d(d, n):
        return d[n][0].T if n in TRANSPOSED else d[n][0]

    shards = {n: local2d(w, n) for n in BIG}
    grads, delta, new_m, new_v = {}, {}, {}, {}
    rowlikes = [[_rowlike(n) for n in names] for names in GROUPS]

    class Exchanges:
        def __init__(self):
            self.gathers, self.forwards, self.fulls, self.reductions, self.small = {}, {}, {}, {}, None

        def _behind_first(self, a):
            return a + self.gathers[0]["token"][0, 0]

        def _start_gather(self, group, after):
            sv = [self.cast[n] if group else shard_view(n, shards[n].astype(BF16)) for n in GROUPS[group]]
            rl = rowlikes[group]
            self.gathers[group] = exchange_start(sv, [_full_shape(s, r) for s, r in zip(sv, rl)], _gather_plan(rl),
                                                 after, f"gather_start_{group}", GATHERS)

        def weights(self, group, after):
            if not self.gathers:
                self._start_gather(0, None)
                self.cast = {n: shard_view(n, self._behind_first(shards[n]).astype(BF16))
                             for names in GROUPS[1:] for n in names}
                after = [self.gathers[0]["token"], *self.cast.values()]
            if group not in self.fulls:
                names, rl, state = GROUPS[group], rowlikes[group], self.gathers[group]
                if group in self.forwards:
                    _, fulls = exchange_wait(self.forwards.pop(group), _forward_plan(rl), after,
                                             f"forward_wait_{group}")
                else:
                    _, fulls = exchange_wait(state, _gather_plan(rl), state["token"] if after is None else after,
                                             f"gather_wait_{group}", GATHERS)
                    fulls = gather_finish(fulls, rl, f"gather_finish_{group}")
                self.fulls[group] = {n: full_from_view(n, f) for n, f in zip(names, fulls)}
                if group + 1 < len(GROUPS):
                    self._start_gather(group + 1, fulls[0])
            return self.fulls[group]

        def started(self):
            return self.gathers[max(self.gathers)]["token"]

        def prefetch(self, group, after):
            rl, state = rowlikes[group], self.gathers[group]
            sv, fulls = exchange_wait(state, _gather_plan(rl), after, f"gather_wait_{group}", GATHERS)
            self.forwards[group] = exchange_start(sv, fulls, _forward_plan(rl), None, f"forward_start_{group}")
            return self.forwards[group]["token"]

        def grads(self, g):
            names = tuple(g)
            rl = [_rowlike(n) for n in names]
            gv = [grad_view(n, g[n]) for n in names]
            state = exchange_start(gv, [_half_buffer_shape(a, r) for a, r in zip(gv, rl)], _swap_plan(rl), None,
                                   "swap_start_" + names[0], SIBLING)
            self.reductions[names] = state
            return state["token"]

        def advance(self, names, after):
            rl = [_rowlike(n) for n in names]
            gv, gots = exchange_wait(self.reductions[names], _swap_plan(rl), after, "swap_wait_" + names[0], SIBLING)
            qs = add_halves(gv, gots, rl, place, "add_halves_" + names[0])
            state = exchange_start(qs, [_slot_shape(q, r) for q, r in zip(qs, rl)], _all_to_all_plan(rl), None,
                                   "all_to_all_start_" + names[0])
            self.reductions[names] = state
            return state["token"]

        def reduce(self, names, after):
            rl = [_rowlike(n) for n in names]
            qs, recvs = exchange_wait(self.reductions.pop(names), _all_to_all_plan(rl), after,
                                      "all_to_all_wait_" + names[0])
            halves = sum_pieces(qs, recvs, rl, place, "sum_pieces_" + names[0])
            return [(shards[n], o.reshape(shards[n].shape), local2d(m, n), local2d(v, n))
                    for n, o in zip(names, join_halves(halves, "join_halves_" + names[0]))]

        def updated(self, names, outs):
            for i, n in enumerate(names):
                grads[n], delta[n], new_m[n], new_v[n] = outs[4 * i:4 * i + 4]
            return [new_v[n] for n in names]

        def update(self, names, updates, token=None):
            return self.updated(names, adamw(updates, place, "adamw_" + names[0], after=token))

        def finish(self, names, after):
            return self.update(names, self.reduce(names, after))

        def small_start(self, G, loss, after):
            packed = pack_small({n: G[n] for n in SMALL})
            used_rows = sum(-(-int(np.prod(small_like[n].shape)) // SMALL_W) for n in SMALL)
            assert packed.shape[0] > used_rows
            packed = packed.at[-1, 0].set(loss[0, 0])
            self.small = exchange_start([packed], [(N_DEV,) + packed.shape], _everyone_plan, after,
                                        "small_start", EVERYONE)
            return self.small["token"]

        def small_finish(self, after):
            (packed,), (slots,) = exchange_wait(self.small, _everyone_plan, after, "small_wait", EVERYONE)
            total = sum_devices(packed, slots, (4 * cx + 2 * cy + cc).astype(jnp.int32).reshape(1))
            self.loss = total[-1, 0]
            rows2d = lambda a: a.reshape(-1, a.shape[-1])
            g = unpack_small(total, small_like)
            outs = adamw([tuple(rows2d(d[n]) for d in (w, g, m, v)) for n in SMALL], place, "adamw_small")
            for i, n in enumerate(SMALL):
                grads[n], delta[n], new_m[n], new_v[n] = (o.reshape(w[n].shape) for o in outs[4 * i:4 * i + 4])
            return new_v["pool_w_mix"]

    small_like = {n: w[n] for n in SMALL}
    S = {n: w[n].reshape(1, -1) for n in ("ffn1_norm", "mix_norm", "ffn2_norm", "final_norm", "pool_scale", "sinks")}
    S["pool_w_mix"] = w["pool_w_mix"][0].astype(BF16)
    exchanges = Exchanges()
    _, dx = local_fwd_bwd(x[0], loss_target[0], S, exchanges)
    loss = exchanges.loss

    def shaped(d, n):
        return (d[n].T if n in TRANSPOSED else d[n]).reshape(w[n].shape)

    return (loss, dx[None], *[shaped(grads, n) for n in WEIGHTS], *[shaped(delta, n) for n in WEIGHTS],
            *[shaped(new_m, n) for n in WEIGHTS], *[shaped(new_v, n) for n in WEIGHTS])
```

```python
import numpy as np
import jax
import jax.numpy as jnp
from jax import lax
from jax.experimental import pallas as pl
from jax.experimental.pallas import tpu as pltpu

F32 = jnp.float32
BF16 = jnp.bfloat16
SDS = jax.ShapeDtypeStruct
MESH = pl.DeviceIdType.MESH

D = 1024
FF = 2816
NQ = 16
NKV = 2
HD = 64
GQ = NQ // NKV
AW = NQ * HD
KVW = NKV * HD
BLK = 128
PW = 512
PG = 128
POOL_WINDOWS = (2, 4, 8, 16)
HALO = 16
INW = AW + 2 * KVW + PW + 2 * D
C_KV = AW
C_Z = AW + 2 * KVW
C_G = C_Z + PW
EPS = 1e-6
FF_CHUNK = 256
FF_CHUNKS = tuple((c, FF_CHUNK) for c in range(0, FF, FF_CHUNK))
SLOPES = tuple(float(2.0 ** (-8.0 * h / NQ)) for h in range(1, NQ + 1))
SCALE = HD ** -0.5

LR, B1, B2, ADAM_EPS, WD, STEP = 0.001, 0.9, 0.999, 1e-08, 0.01, 10

VMEM_LIMIT = 56 * 1024 * 1024
N_CHIPS = 4

NT = (((1,), (1,)), ((), ()))
TN = (((0,), (0,)), ((), ()))


def _cp(sem=None, vmem=VMEM_LIMIT):
    return pltpu.CompilerParams(dimension_semantics=sem, vmem_limit_bytes=vmem)


def _const_spec(shape):
    nd = len(shape)
    return pl.BlockSpec(shape, lambda *_: (0,) * nd, pipeline_mode=pl.Buffered(1))


def _rstd(x):
    return lax.rsqrt(jnp.mean(x * x, axis=-1, keepdims=True) + EPS)


def _rms_bwd(dn, xhat, rstd, g):
    dxhat = dn * g
    return rstd * (dxhat - xhat * jnp.mean(dxhat * xhat, axis=-1, keepdims=True))


def _sigmoid(x):
    return 0.5 * jnp.tanh(0.5 * x) + 0.5


def _dot(a, b):
    return jnp.dot(a, b, preferred_element_type=F32)


def _dot_nt(a, b):
    return lax.dot_general(a, b, NT, preferred_element_type=F32)


def _dot_tn(a, b):
    return lax.dot_general(a, b, TN, preferred_element_type=F32)


def _swiglu_up(x, g, wup_ref, ab_ref, hid_ref):
    n = (x * _rstd(x) * g).astype(BF16)
    for c0, w in FF_CHUNKS:
        a = _dot(n, wup_ref[:, c0:c0 + w])
        b = _dot(n, wup_ref[:, FF + c0:FF + c0 + w])
        sig = _sigmoid(a)
        s = a * sig
        ab_ref[:, c0:c0 + w] = (b * (sig + s * (1.0 - sig))).astype(BF16)
        ab_ref[:, FF + c0:FF + c0 + w] = s.astype(BF16)
        hid_ref[:, c0:c0 + w] = (s * b).astype(BF16)


def ffn_up(h, g, wup, name, after=None):
    T = h.shape[0]
    TM = 512
    tile = lambda w: pl.BlockSpec((TM, w), lambda i: (i, 0))
    token_spec, token_arg = _token_operand(after)

    def body(h_ref, g_ref, wup_ref, *rest):
        _swiglu_up(h_ref[...], g_ref[...], wup_ref, *rest[-2:])

    return pl.pallas_call(
        body, name=name, grid=(T // TM,),
        in_specs=[tile(D), _const_spec((1, D)), _const_spec((D, 2 * FF))] + token_spec,
        out_specs=[tile(2 * FF), tile(FF)],
        out_shape=[SDS((T, 2 * FF), BF16), SDS((T, FF), BF16)],
        compiler_params=_cp(("arbitrary",)),
    )(h, g, wup, *token_arg)


def ffn_down(h, hid, wdn, name, after=None):
    T = h.shape[0]
    TM = min(1024, T)
    tile = lambda w: pl.BlockSpec((TM, w), lambda i: (i, 0))
    token_spec, token_arg = _token_operand(after)

    def body(h_ref, hid_ref, wdn_ref, *rest):
        rest[-1][...] = h_ref[...] + 0.5 * _dot(hid_ref[...], wdn_ref[...])

    return pl.pallas_call(
        body, name=name, grid=(T // TM,),
        in_specs=[tile(D), tile(FF), _const_spec((FF, D))] + token_spec,
        out_specs=tile(D), out_shape=SDS((T, D), F32),
        compiler_params=_cp(("arbitrary",)),
    )(h, hid, wdn, *token_arg)


def ffn_fwd_loss(h, g, wup, wdn, target, g_final, name):
    T = h.shape[0]
    TM = 512
    tile = lambda w: pl.BlockSpec((TM, w), lambda i: (i, 0))
    acc_spec = lambda w: pl.BlockSpec((1, w), lambda i: (0, 0))

    def body(h_ref, g_ref, wup_ref, wdn_ref, t_ref, gf_ref, dout_ref, ab_ref, hid_ref, loss_ref, dgf_ref):
        x = h_ref[...]
        _swiglu_up(x, g_ref[...], wup_ref, ab_ref, hid_ref)
        out = x + 0.5 * _dot(hid_ref[...], wdn_ref[...])
        dout_ref[...] = _loss_head(out, t_ref[...], gf_ref[...], loss_ref, dgf_ref, pl.program_id(0) == 0)

    return pl.pallas_call(
        body, name=name, grid=(T // TM,),
        in_specs=[tile(D), _const_spec((1, D)), _const_spec((D, 2 * FF)), _const_spec((FF, D)), tile(D),
                  _const_spec((1, D))],
        out_specs=[tile(D), tile(2 * FF), tile(FF), acc_spec(1), acc_spec(D)],
        out_shape=[SDS((T, D), F32), SDS((T, 2 * FF), BF16), SDS((T, FF), BF16), SDS((1, 1), F32), SDS((1, D), F32)],
        compiler_params=_cp(("arbitrary",)),
    )(h, g, wup, wdn, target, g_final)


def _loss_head(x, target, g, loss_ref, dg_ref, first):
    @pl.when(first)
    def _():
        loss_ref[...] = jnp.zeros_like(loss_ref)
        dg_ref[...] = jnp.zeros_like(dg_ref)

    rstd = _rstd(x)
    xhat = x * rstd
    err = xhat * g - target
    loss_ref[...] += 0.5 * jnp.sum(jnp.mean(err * err, axis=-1, keepdims=True), axis=0, keepdims=True)
    dy = err * (1.0 / D)
    dg_ref[...] += jnp.sum(dy * xhat, axis=0, keepdims=True)
    return _rms_bwd(dy, xhat, rstd, g)


def _adamw_tile(w_ref, g_ref, m_ref, v_ref, go_ref, d_ref, nm_ref, nv_ref):
    gv = g_ref[...]
    go_ref[...] = gv
    nm = B1 * m_ref[...] + (1.0 - B1) * gv
    nv = B2 * v_ref[...] + (1.0 - B2) * (gv * gv)
    nm_ref[...] = nm
    nv_ref[...] = nv
    d_ref[...] = -LR * ((nm / (1.0 - B1 ** STEP)) / (jnp.sqrt(nv / (1.0 - B2 ** STEP)) + ADAM_EPS) + WD * w_ref[...])


def _riders(updates, steps):
    in_specs, out_specs, out_shapes, operands, tiles = [], [], [], [], []
    for w, g, m, v in updates:
        R, C = w.shape
        n = max(d for d in range(1, steps + 1) if R % d == 0 and (R // d) % 8 == 0)
        spec = pl.BlockSpec((R // n, C), lambda i, n=n: (jnp.minimum(i, n - 1), 0))
        in_specs += [spec] * 4
        out_specs += [spec] * 4
        out_shapes += [SDS((R, C), F32)] * 4
        operands += [w, g, m, v]
        tiles.append(n)

    def run(step, in_refs, out_refs):
        for u, n in enumerate(tiles):
            @pl.when(step < n)
            def _(u=u):
                _adamw_tile(*in_refs[4 * u:4 * u + 4], *out_refs[4 * u:4 * u + 4])

    return in_specs, out_specs, out_shapes, operands, run


def ffn_bwd_x(dh, h_in, ab, g, wup, wdn, name, updates=(), after=None):
    T = dh.shape[0]
    TM = 256 if updates else 512
    SUB = 256
    r_in, r_out, r_shapes, r_args, ride = _riders(updates, T // TM)
    token_spec, token_arg = _token_operand(after)
    n_in = len(r_in) + len(token_spec)

    def body(dh_ref, h_ref, ab_ref, g_ref, wup_ref, wdn_ref, *rest):
        dhin_ref, dup_ref, n_ref, dg_ref = rest[n_in:n_in + 4]
        ride(pl.program_id(0), rest[:len(r_in)], rest[n_in + 4:])
        g = g_ref[...]
        dg = None
        for r0 in range(0, TM, SUB):
            rows = slice(r0, r0 + SUB)
            x = h_ref[rows, :]
            rstd = _rstd(x)
            xhat = x * rstd
            n_ref[rows, :] = (xhat * g).astype(BF16)
            dh = dh_ref[rows, :]
            dhh = (0.5 * dh).astype(BF16)
            for c0, w in FF_CHUNKS:
                dhid = _dot_nt(dhh, wdn_ref[c0:c0 + w, :]).astype(BF16)
                dup_ref[rows, c0:c0 + w] = dhid * ab_ref[rows, c0:c0 + w]
                dup_ref[rows, FF + c0:FF + c0 + w] = dhid * ab_ref[rows, FF + c0:FF + c0 + w]
            dn = _dot_nt(dup_ref[rows, :], wup_ref[...])
            dhin_ref[rows, :] = dh + _rms_bwd(dn, xhat, rstd, g)
            part = jnp.sum(dn * xhat, axis=0, keepdims=True)
            dg = part if dg is None else dg + part

        @pl.when(pl.program_id(0) == 0)
        def _():
            dg_ref[...] = jnp.zeros_like(dg_ref)

        dg_ref[...] += dg

    tile = lambda w: pl.BlockSpec((TM, w), lambda i: (i, 0))
    return pl.pallas_call(
        body, name=name, grid=(T // TM,),
        in_specs=[tile(D), tile(D), tile(2 * FF), _const_spec((1, D)), _const_spec((D, 2 * FF)), _const_spec((FF, D))]
        + r_in + token_spec,
        out_specs=[tile(D), tile(2 * FF), tile(D), pl.BlockSpec((1, D), lambda i: (0, 0))] + r_out,
        out_shape=[SDS((T, D), F32), SDS((T, 2 * FF), BF16), SDS((T, D), BF16), SDS((1, D), F32)] + r_shapes,
        compiler_params=_cp(("arbitrary",)),
    )(dh, h_in, ab, g, wup, wdn, *r_args, *token_arg)


TOKEN_SPEC = pl.BlockSpec((8, 128), lambda *_: (0, 0))


def _token_operand(after):
    tokens = [t for t in (after if isinstance(after, tuple) else (after,)) if t is not None]
    return [TOKEN_SPEC] * len(tokens), tokens


def matmul_tn(a, b, name, *, tm, tn, tt=2048, b_scale=None, after=None):
    T, M = a.shape
    N = b.shape[1]
    tt = min(tt, T)
    assert M % tm == 0 and N % tn == 0 and T % tt == 0
    nt = T // tt
    token_spec, token_arg = _token_operand(after)

    def body(a_ref, b_ref, *rest):
        o_ref, acc_ref = rest[-2:]
        t = pl.program_id(2)

        @pl.when(t == 0)
        def _():
            acc_ref[...] = jnp.zeros_like(acc_ref)

        bv = b_ref[...]
        if b_scale is not None:
            bv = bv * b_scale
        acc_ref[...] += _dot_tn(a_ref[...].astype(BF16), bv.astype(BF16))

        @pl.when(t == nt - 1)
        def _():
            o_ref[...] = acc_ref[...].astype(BF16)

    return pl.pallas_call(
        body, name=name, grid=(M // tm, N // tn, nt),
        in_specs=[pl.BlockSpec((tt, tm), lambda i, j, t: (t, i)), pl.BlockSpec((tt, tn), lambda i, j, t: (t, j))]
        + token_spec,
        out_specs=pl.BlockSpec((tm, tn), lambda i, j, t: (i, j)),
        out_shape=SDS((M, N), BF16),
        scratch_shapes=[pltpu.VMEM((tm, tn), F32)],
        compiler_params=_cp(("parallel", "parallel", "arbitrary")),
    )(a, b, *token_arg)


def mix_in_fwd(h1, g, win_t):
    T = h1.shape[0]
    TM = 512

    def body(h_ref, g_ref, w_ref, u_ref, q_ref, kv_ref, z_ref, gt_ref):
        x = h_ref[...]
        u = (x * _rstd(x) * g_ref[...]).astype(BF16)
        u_ref[...] = u
        for c in range(0, AW, 256):
            q_ref[:, c:c + 256] = _dot_nt(u, w_ref[c:c + 256, :]).astype(BF16)
        kv_ref[...] = _dot_nt(u, w_ref[C_KV:C_Z, :]).astype(BF16)
        for c in range(0, PW, 256):
            z_ref[:, c:c + 256] = _dot_nt(u, w_ref[C_Z + c:C_Z + c + 256, :])
        for c in range(0, 2 * D, 256):
            gt_ref[:, c:c + 256] = _dot_nt(u, w_ref[C_G + c:C_G + c + 256, :]).astype(BF16)

    tile = lambda w: pl.BlockSpec((TM, w), lambda i: (i, 0))
    return pl.pallas_call(
        body, name="mix_in_fwd", grid=(T // TM,),
        in_specs=[tile(D), _const_spec((1, D)), _const_spec((INW, D))],
        out_specs=[tile(D), tile(AW), tile(2 * KVW), tile(PW), tile(2 * D)],
        out_shape=[SDS((T, D), BF16), SDS((T, AW), BF16), SDS((T, 2 * KVW), BF16), SDS((T, PW), F32),
                   SDS((T, 2 * D), BF16)],
        compiler_params=_cp(("arbitrary",)),
    )(h1, g, win_t)


def mix_in_bwd(dq, dkv, dz, dgt, dh2, h1, g, win_t):
    T = h1.shape[0]
    TM = min(1024, T)
    SUB = 256

    def body(dq_ref, dkv_ref, dz_ref, dgt_ref, dh2_ref, h_ref, g_ref, w_ref, dh1_ref, dg_ref):
        g = g_ref[...]
        dg = None
        for r0 in range(0, TM, SUB):
            rows = slice(r0, r0 + SUB)
            du = _dot(dq_ref[rows, :], w_ref[0:AW, :])
            du += _dot(dkv_ref[rows, :], w_ref[C_KV:C_Z, :])
            du += _dot(dz_ref[rows, :], w_ref[C_Z:C_G, :])
            du += _dot(dgt_ref[rows, :], w_ref[C_G:INW, :])
            x = h_ref[rows, :]
            rstd = _rstd(x)
            xhat = x * rstd
            dh1_ref[rows, :] = dh2_ref[rows, :] + _rms_bwd(du, xhat, rstd, g)
            part = jnp.sum(du * xhat, axis=0, keepdims=True)
            dg = part if dg is None else dg + part

        @pl.when(pl.program_id(0) == 0)
        def _():
            dg_ref[...] = jnp.zeros_like(dg_ref)

        dg_ref[...] += dg

    tile = lambda w: pl.BlockSpec((TM, w), lambda i: (i, 0))
    return pl.pallas_call(
        body, name="mix_in_bwd", grid=(T // TM,),
        in_specs=[tile(AW), tile(2 * KVW), tile(PW), tile(2 * D), tile(D), tile(D), _const_spec((1, D)),
                  _const_spec((INW, D))],
        out_specs=[tile(D), pl.BlockSpec((1, D), lambda i: (0, 0))],
        out_shape=[SDS((T, D), F32), SDS((1, D), F32)],
        compiler_params=_cp(("arbitrary",)),
    )(dq, dkv, dz, dgt, dh2, h1, g, win_t)


PAIR = 2 * HD
NPAIR = GQ // 2


def _lo_lanes():
    return lax.broadcasted_iota(jnp.int32, (BLK, PAIR), 1) < HD


def _stack_heads(ref, kvh, scale=None):
    lo = _lo_lanes()
    parts = []
    for pr in range(NPAIR):
        t = ref[:, (kvh * NPAIR + pr) * PAIR:(kvh * NPAIR + pr + 1) * PAIR]
        if scale is not None:
            t = t * scale
        zero = jnp.zeros_like(t)
        parts += [jnp.where(lo, t, zero), jnp.where(lo, zero, t)]
    return jnp.concatenate(parts, axis=0)


def _kv_tiles(kvc_ref, kvp_ref, tile, kvh):
    lo = _lo_lanes()
    dup, left, right = [], [], []
    for ref in (kvp_ref, kvc_ref):
        t = ref[:, tile * PAIR:(tile + 1) * PAIR]
        r = pltpu.roll(t.astype(F32), HD, 1).astype(BF16)
        zero = jnp.zeros_like(t)
        a, b = (t, r) if kvh == 0 else (r, t)
        dup.append(jnp.where(lo, a, b))
        left.append(jnp.where(lo, a, zero))
        right.append(jnp.where(lo, zero, b))
    cat = lambda xs: jnp.concatenate(xs, axis=0)
    return cat(dup), cat(left), cat(right)


def _band_consts(first):
    row = lax.broadcasted_iota(jnp.int32, (BLK, BLK), 0)
    col = lax.broadcasted_iota(jnp.int32, (BLK, BLK), 1)
    upper = col > row
    dist = jnp.where(upper, row - col + BLK, row - col).astype(F32)
    pen = jnp.where(jnp.logical_and(upper, first), -jnp.inf, 0.0)
    return upper, dist, pen


def _split_band(upper, t):
    zero = jnp.zeros_like(t)
    return jnp.concatenate([jnp.where(upper, t, zero), jnp.where(upper, zero, t)], axis=1)


def attn_fwd(q, kv, sinks):
    T = q.shape[0]
    nb = T // BLK

    def body(sink_ref, q_ref, kvc_ref, kvp_ref, att_ref, lse_ref):
        upper, dist, pen = _band_consts(pl.program_id(0) == 0)
        scores, values = [], []
        for kvh in range(NKV):
            kdup, _, _ = _kv_tiles(kvc_ref, kvp_ref, 0, kvh)
            values.append(_kv_tiles(kvc_ref, kvp_ref, 1, kvh)[1:])
            scores.append(_dot_nt(_stack_heads(q_ref, kvh, SCALE), kdup))
        for kvh in range(NKV):
            s_all = scores[kvh]
            vleft, vright = values[kvh]
            for pr in range(NPAIR):
                outs, inv = [], []
                for side, vpad in ((0, vleft), (1, vright)):
                    g = 2 * pr + side
                    hq = kvh * GQ + g
                    sink = sink_ref[0, hq]
                    rows = slice(g * BLK, (g + 1) * BLK)
                    s = jnp.where(upper, s_all[rows, 0:BLK], s_all[rows, BLK:2 * BLK]) - SLOPES[hq] * dist + pen
                    m = jnp.maximum(jnp.max(s, axis=-1, keepdims=True), sink)
                    p = jnp.exp(s - m)
                    l = jnp.sum(p, axis=-1, keepdims=True) + jnp.exp(sink - m)
                    lse_ref[:, hq:hq + 1] = m + jnp.log(l)
                    outs.append(_dot(_split_band(upper, p.astype(BF16)), vpad))
                    inv.append(1.0 / l)
                col0 = (kvh * NPAIR + pr) * PAIR
                att_ref[:, col0:col0 + PAIR] = ((outs[0] + outs[1]) * jnp.where(_lo_lanes(), inv[0], inv[1])).astype(BF16)

    return pl.pallas_call(
        body, name="attn_fwd", grid=(nb,),
        in_specs=[pl.BlockSpec(memory_space=pltpu.SMEM),
                  pl.BlockSpec((BLK, AW), lambda i: (i, 0)),
                  pl.BlockSpec((BLK, 2 * KVW), lambda i: (i, 0)),
                  pl.BlockSpec((BLK, 2 * KVW), lambda i: (jnp.maximum(i - 1, 0), 0))],
        out_specs=[pl.BlockSpec((BLK, AW), lambda i: (i, 0)), pl.BlockSpec((BLK, NQ), lambda i: (i, 0))],
        out_shape=[SDS((T, AW), BF16), SDS((T, NQ), F32)],
        compiler_params=_cp(("arbitrary",)),
    )(sinks, q, kv, kv)


def attn_bwd(q, kv, datt, att, lse, sinks, after=None):
    T = q.shape[0]
    nb = T // BLK
    token_spec, token_arg = _token_operand(after)

    def body(sink_ref, q_ref, kvc_ref, kvp_ref, do_ref, out_ref, lse_ref, *rest):
        dq_ref, dkv_ref, dsink_ref, carry_ref = rest[-4:]
        i = pl.program_id(0)

        @pl.when(i == 0)
        def _():
            dsink_ref[...] = jnp.zeros_like(dsink_ref)
            carry_ref[...] = jnp.zeros_like(carry_ref)

        @pl.when(i < nb)
        def _():
            upper, dist, pen = _band_consts(i == 0)
            lo = _lo_lanes()
            dk_dup, dv_dup = [], []
            staged = []
            for kvh in range(NKV):
                kdup, kleft, kright = _kv_tiles(kvc_ref, kvp_ref, 0, kvh)
                vdup, _, _ = _kv_tiles(kvc_ref, kvp_ref, 1, kvh)
                qs = _stack_heads(q_ref, kvh, SCALE)
                dos = _stack_heads(do_ref, kvh)
                staged.append((kleft, kright, qs, dos, _dot_nt(qs, kdup), _dot_nt(dos, vdup)))
            deltas = []
            for pair in range(NQ // 2):
                cols = slice(pair * PAIR, (pair + 1) * PAIR)
                t = do_ref[:, cols].astype(F32) * out_ref[:, cols].astype(F32)
                deltas += [jnp.sum(jnp.where(lo, t, 0.0), axis=-1, keepdims=True),
                           jnp.sum(jnp.where(lo, 0.0, t), axis=-1, keepdims=True)]
            for kvh in range(NKV):
                kleft, kright, qs, dos, s_all, dp_all = staged[kvh]
                ds_parts, p_parts = [], []
                for pr in range(NPAIR):
                    dq = None
                    for side, kpad in ((0, kleft), (1, kright)):
                        g = 2 * pr + side
                        hq = kvh * GQ + g
                        lse_h = lse_ref[:, hq:hq + 1]
                        rows = slice(g * BLK, (g + 1) * BLK)
                        s = jnp.where(upper, s_all[rows, 0:BLK], s_all[rows, BLK:2 * BLK]) - SLOPES[hq] * dist + pen
                        p = jnp.exp(s - lse_h)
                        dp = jnp.where(upper, dp_all[rows, 0:BLK], dp_all[rows, BLK:2 * BLK])
                        delta = deltas[hq]
                        dsink_ref[:, hq:hq + 1] += -jnp.sum(jnp.exp(sink_ref[0, hq] - lse_h) * delta, axis=0,
                                                            keepdims=True)
                        ds = _split_band(upper, (p * (dp - delta)).astype(BF16))
                        ds_parts.append(ds)
                        p_parts.append(_split_band(upper, p.astype(BF16)))
                        d = _dot(ds, kpad)
                        dq = d if dq is None else dq + d
                    col0 = (kvh * NPAIR + pr) * PAIR
                    dq_ref[:, col0:col0 + PAIR] = (dq * SCALE).astype(BF16)
                dkw = _dot_tn(qs, jnp.concatenate(ds_parts, axis=0)).T
                dvw = _dot_tn(dos, jnp.concatenate(p_parts, axis=0)).T
                dk_dup.append(dkw + pltpu.roll(dkw, HD, 1))
                dv_dup.append(dvw + pltpu.roll(dvw, HD, 1))
            dk = jnp.where(jnp.concatenate([lo, lo], axis=0), dk_dup[0], dk_dup[1])
            dv = jnp.where(jnp.concatenate([lo, lo], axis=0), dv_dup[0], dv_dup[1])
            dkv_ref[:, 0:PAIR] = (carry_ref[:, 0:PAIR] + dk[0:BLK]).astype(BF16)
            dkv_ref[:, PAIR:2 * PAIR] = (carry_ref[:, PAIR:2 * PAIR] + dv[0:BLK]).astype(BF16)
            carry_ref[:, 0:PAIR] = dk[BLK:2 * BLK]
            carry_ref[:, PAIR:2 * PAIR] = dv[BLK:2 * BLK]

        @pl.when(i == nb)
        def _():
            dkv_ref[...] = carry_ref[...].astype(BF16)

    cur = lambda i: (jnp.minimum(i, nb - 1), 0)
    prev = lambda i: (jnp.maximum(jnp.minimum(i, nb - 1) - 1, 0), 0)
    return pl.pallas_call(
        body, name="attn_bwd", grid=(nb + 1,),
        in_specs=[pl.BlockSpec(memory_space=pltpu.SMEM),
                  pl.BlockSpec((BLK, AW), cur), pl.BlockSpec((BLK, 2 * KVW), cur), pl.BlockSpec((BLK, 2 * KVW), prev),
                  pl.BlockSpec((BLK, AW), cur), pl.BlockSpec((BLK, AW), cur), pl.BlockSpec((BLK, NQ), cur)] + token_spec,
        out_specs=[pl.BlockSpec((BLK, AW), cur),
                   pl.BlockSpec((BLK, 2 * KVW), lambda i: (jnp.maximum(i - 1, 0), 0)),
                   pl.BlockSpec((1, NQ), lambda i: (0, 0))],
        out_shape=[SDS((T, AW), BF16), SDS((T, 2 * KVW), BF16), SDS((1, NQ), F32)],
        scratch_shapes=[pltpu.VMEM((BLK, 2 * KVW), F32)],
        compiler_params=_cp(("arbitrary",)),
    )(sinks, q, kv, kv, datt, att, lse, *token_arg)


def _inv_counts(t0, rows):
    t = (t0 + lax.broadcasted_iota(jnp.int32, (rows, 1), 0) + 1).astype(F32)
    return [1.0 / jnp.minimum(t, float(w)) for w in POOL_WINDOWS]


def pool_fwd(z, wmix, scale, after=None):
    T = z.shape[0]
    TM = min(1024, T)
    L = TM + HALO
    token_spec, token_arg = _token_operand(after)

    def body(z_ref, halo_ref, wmix_ref, scale_ref, *rest):
        pooled_ref, mixs_ref = rest[-2:]
        i = pl.program_id(0)
        halo = jnp.where(i > 0, halo_ref[...], 0.0)
        zt = z_ref[...]
        e = jnp.concatenate([halo, zt], axis=0)
        sums = []
        s = e
        for k in (1, 2, 4, 8):
            s = s + pltpu.roll(s, k, 0)
            sums.append(s)
        inv = _inv_counts(i * TM, TM)
        for gi in range(len(POOL_WINDOWS)):
            cols = slice(gi * PG, (gi + 1) * PG)
            pooled = (sums[gi][HALO:, cols] * inv[gi] - zt[:, cols]).astype(BF16)
            pooled_ref[:, cols] = pooled
            mixs_ref[:, cols] = (_dot(pooled, wmix_ref[gi]) * scale_ref[:, cols]).astype(BF16)

    return pl.pallas_call(
        body, name="pool_fwd", grid=(T // TM,),
        in_specs=[pl.BlockSpec((TM, PW), lambda i: (i, 0)),
                  pl.BlockSpec((HALO, PW), lambda i: (jnp.maximum(i * (TM // HALO) - 1, 0), 0)),
                  _const_spec((len(POOL_WINDOWS), PG, PG)), _const_spec((1, PW))] + token_spec,
        out_specs=[pl.BlockSpec((TM, PW), lambda i: (i, 0)), pl.BlockSpec((TM, PW), lambda i: (i, 0))],
        out_shape=[SDS((T, PW), BF16), SDS((T, PW), BF16)],
        compiler_params=_cp(("arbitrary",)),
    )(z, z, wmix, scale, *token_arg)


def pool_bwd(dmixs, pooled, wmix, scale, after=None):
    T = dmixs.shape[0]
    TM = min(1024, T)
    L = TM + HALO
    nt = T // TM
    token_spec, token_arg = _token_operand(after)

    def body(dm_ref, halo_ref, pooled_ref, wmix_ref, scale_ref, *rest):
        dz_ref, dwmix_ref, dscale_ref = rest[-3:]
        i = pl.program_id(0)

        @pl.when(i == 0)
        def _():
            dwmix_ref[...] = jnp.zeros_like(dwmix_ref)
            dscale_ref[...] = jnp.zeros_like(dscale_ref)

        halo = jnp.where(i < nt - 1, halo_ref[...], 0.0)
        dm = dm_ref[...]
        e = jnp.concatenate([dm, halo], axis=0)
        inv = _inv_counts(i * TM, L)
        for gi in range(len(POOL_WINDOWS)):
            cols = slice(gi * PG, (gi + 1) * PG)
            w = wmix_ref[gi]
            dmixed = (e[:, cols] * scale_ref[:, cols]).astype(BF16)
            dpooled = _dot_nt(dmixed, w)
            pooled = pooled_ref[:, cols]
            mixed = _dot(pooled, w)
            dscale_ref[:, cols] += jnp.sum(dm[:, cols] * mixed, axis=0, keepdims=True)
            dwmix_ref[gi] += _dot_tn(pooled, dmixed[:TM])
            s = dpooled * inv[gi]
            k = 1
            while k < POOL_WINDOWS[gi]:
                s = s + pltpu.roll(s, L - k, 0)
                k *= 2
            dz_ref[:, cols] = (s[:TM] - dpooled[:TM]).astype(BF16)

    return pl.pallas_call(
        body, name="pool_bwd", grid=(nt,),
        in_specs=[pl.BlockSpec((TM, PW), lambda i: (i, 0)),
                  pl.BlockSpec((HALO, PW), lambda i: (jnp.minimum((i + 1) * (TM // HALO), T // HALO - 1), 0)),
                  pl.BlockSpec((TM, PW), lambda i: (i, 0)),
                  _const_spec((len(POOL_WINDOWS), PG, PG)), _const_spec((1, PW))] + token_spec,
        out_specs=[pl.BlockSpec((TM, PW), lambda i: (i, 0)),
                   pl.BlockSpec((len(POOL_WINDOWS), PG, PG), lambda i: (0, 0, 0)),
                   pl.BlockSpec((1, PW), lambda i: (0, 0))],
        out_shape=[SDS((T, PW), BF16), SDS((len(POOL_WINDOWS), PG, PG), F32), SDS((1, PW), F32)],
        compiler_params=_cp(("arbitrary",)),
    )(dmixs, dmixs, pooled, wmix, scale, *token_arg)


RING = 3


def _ring_tiles(srcs, bufs, sems, tm, steps):
    s = pl.program_id(0)

    def fetch(step):
        slot = step % RING
        rows = pl.ds(pl.multiple_of(step * tm, tm), tm)
        return [pltpu.make_async_copy(src.at[rows], buf.at[slot], sems.at[i, slot])
                for i, (src, buf) in enumerate(zip(srcs, bufs))]

    @pl.when(s == 0)
    def _():
        for first in range(min(RING - 1, steps)):
            for copy in fetch(first):
                copy.start()

    @pl.when(s + RING - 1 < steps)
    def _():
        for copy in fetch(s + RING - 1):
            copy.start()

    for copy in fetch(s):
        copy.wait()
    return [buf.at[s % RING] for buf in bufs]


def _ring_scratch(tiles, tm):
    return [pltpu.VMEM((RING, tm, w), dt) for w, dt in tiles] + [pltpu.SemaphoreType.DMA((len(tiles), RING))]


def merge_fwd(att, mixs, gt, h1, wattn, wpool, wout):
    T = h1.shape[0]
    TM = 512

    def body(att_hbm, mixs_hbm, gt_hbm, h_hbm, wa_ref, wp_ref, wo_ref, h2_ref, mg_ref, gf_ref, *ring):
        att_ref, mixs_ref, gt_ref, h_ref = _ring_tiles((att_hbm, mixs_hbm, gt_hbm, h_hbm), ring[:-1], ring[-1], TM,
                                                       T // TM)
        a = _dot(att_ref[...], wa_ref[...])
        p = _dot(mixs_ref[...], wp_ref[...])
        sa = _sigmoid(gt_ref[:, 0:D].astype(F32))
        sp = _sigmoid(gt_ref[:, D:2 * D].astype(F32))
        gf_ref[:, 0:D] = (a * sa * (1.0 - sa)).astype(BF16)
        gf_ref[:, D:2 * D] = (p * sp * (1.0 - sp)).astype(BF16)
        mg = (sa * a + sp * p).astype(BF16)
        mg_ref[...] = mg
        h2_ref[...] = h_ref[...] + _dot(mg, wo_ref[...])

    tile = lambda w: pl.BlockSpec((TM, w), lambda i: (i, 0))
    return pl.pallas_call(
        body, name="merge_fwd", grid=(T // TM,),
        in_specs=[pl.BlockSpec(memory_space=pl.ANY)] * 4 + [_const_spec((AW, D)), _const_spec((PW, D)),
                                                            _const_spec((D, D))],
        out_specs=[tile(D), tile(D), tile(2 * D)],
        out_shape=[SDS((T, D), F32), SDS((T, D), BF16), SDS((T, 2 * D), BF16)],
        scratch_shapes=_ring_scratch(((AW, BF16), (PW, BF16), (2 * D, BF16), (D, F32)), TM),
        compiler_params=_cp(("arbitrary",)),
    )(att, mixs, gt, h1, wattn, wpool, wout)


def merge_bwd(dh2, gt, gf, wattn, wpool, wout, after=None):
    T = dh2.shape[0]
    TM = 512
    token_spec, token_arg = _token_operand(after)
    n_tok = len(token_spec)

    def body(dh2_hbm, gt_hbm, gf_hbm, wa_ref, wp_ref, wo_ref, *rest):
        datt_ref, dmixs_ref, dgt_ref, da_ref, dp_ref = rest[n_tok:n_tok + 5]
        dh2_ref, gt_ref, gf_ref = _ring_tiles((dh2_hbm, gt_hbm, gf_hbm), rest[n_tok + 5:-1], rest[-1], TM, T // TM)
        dm = _dot_nt(dh2_ref[...].astype(BF16), wo_ref[...])
        da = (dm * _sigmoid(gt_ref[:, 0:D].astype(F32))).astype(BF16)
        dp = (dm * _sigmoid(gt_ref[:, D:2 * D].astype(F32))).astype(BF16)
        da_ref[...] = da
        dp_ref[...] = dp
        dgt_ref[:, 0:D] = (dm * gf_ref[:, 0:D].astype(F32)).astype(BF16)
        dgt_ref[:, D:2 * D] = (dm * gf_ref[:, D:2 * D].astype(F32)).astype(BF16)
        datt_ref[...] = _dot_nt(da, wa_ref[...]).astype(BF16)
        dmixs_ref[...] = _dot_nt(dp, wp_ref[...])

    tile = lambda w: pl.BlockSpec((TM, w), lambda i: (i, 0))
    return pl.pallas_call(
        body, name="merge_bwd", grid=(T // TM,),
        in_specs=[pl.BlockSpec(memory_space=pl.ANY)] * 3 + [_const_spec((AW, D)), _const_spec((PW, D)),
                                                            _const_spec((D, D))] + token_spec,
        out_specs=[tile(AW), tile(PW), tile(2 * D), tile(D), tile(D)],
        out_shape=[SDS((T, AW), BF16), SDS((T, PW), F32), SDS((T, 2 * D), BF16), SDS((T, D), BF16),
                   SDS((T, D), BF16)],
        scratch_shapes=_ring_scratch(((D, F32), (2 * D, BF16), (2 * D, BF16)), TM),
        compiler_params=_cp(("arbitrary",)),
    )(dh2, gt, gf, wattn, wpool, wout, *token_arg)


def adamw(updates, place, name, after=None):
    tile_bytes = 2 * 1024 * 1024 // len(updates)
    jobs = []
    for w, g, m, v in updates:
        R, C = w.shape
        tr = R
        if R * C * 4 > tile_bytes:
            tr = next(cand for cand in (512, 256, 128, 64, 32, 16, 8) if R % cand == 0 and cand * C * 4 <= tile_bytes)
        blk = ((tr, C), lambda ls, p: (ls, 0))
        jobs.append((R // tr, [(a, *blk) for a in (w, g, m, v)], [((R, C), F32, *blk)] * 4,
                     lambda ins, outs: _adamw_tile(*ins, *outs)))
    return _run_jobs(jobs, place, name, after)


GROUP_UP1, GROUP_MIX, GROUP_FFN2 = 0, 1, 2


def local_fwd_bwd(x, target, S, comm):
    w_up1 = comm.weights(GROUP_UP1, None)["ffn1_w_up"]
    ab1, hid1 = ffn_up(x, S["ffn1_norm"], w_up1, "ffn1_up", after=comm.started())
    W = comm.weights(GROUP_MIX, hid1)
    h1 = ffn_down(x, hid1, W["ffn1_w_down"], "ffn1_down", after=comm.started())
    u, q, kv, z, gt = mix_in_fwd(h1, S["mix_norm"], W["w_in"])
    att, lse = attn_fwd(q, kv, S["sinks"])
    pooled, mixs = pool_fwd(z, S["pool_w_mix"], S["pool_scale"], after=comm.prefetch(GROUP_FFN2, att))
    h2, merged, gf = merge_fwd(att, mixs, gt, h1, W["w_attn_up"], W["w_pool_up"], W["w_out"])
    W = comm.weights(GROUP_FFN2, h2)
    dh3, ab2, hid2, loss, g_final = ffn_fwd_loss(h2, S["ffn2_norm"], W["ffn2_w_up"], W["ffn2_w_down"], target,
                                                 S["final_norm"], "ffn2_fwd")

    G = {"final_norm": g_final}
    dh2, dup2, n2, G["ffn2_norm"] = ffn_bwd_x(dh3, h2, ab2, S["ffn2_norm"], W["ffn2_w_up"], W["ffn2_w_down"],
                                              "ffn2_bwd_x")
    ffn2 = ("ffn2_w_down", "ffn2_w_up")
    token = comm.grads({"ffn2_w_down": matmul_tn(hid2, dh3, "ffn2_dw_down", tm=1408, tn=512, tt=4096, b_scale=0.5),
                        "ffn2_w_up": matmul_tn(n2, dup2, "ffn2_dw_up", tm=D, tn=512, tt=4096)})

    W = comm.weights(GROUP_MIX, None)
    datt, dmixs, dgt, da, dp = merge_bwd(dh2, gt, gf, W["w_attn_up"], W["w_pool_up"], W["w_out"], after=token)
    token = comm.advance(ffn2, datt)
    g_mix = {"w_out": matmul_tn(merged, dh2, "dw_out", tm=D, tn=D, after=token),
             "w_attn_up": matmul_tn(att, da, "dw_attn_up", tm=AW, tn=D),
             "w_pool_up": matmul_tn(mixs, dp, "dw_pool_up", tm=PW, tn=D)}
    dz, G["pool_w_mix"], G["pool_scale"] = pool_bwd(dmixs, pooled, S["pool_w_mix"], S["pool_scale"], after=token)
    dq, dkv, G["sinks"] = attn_bwd(q, kv, datt, att, lse, S["sinks"], after=token)
    dh1, G["mix_norm"] = mix_in_bwd(dq, dkv, dz, dgt, dh2, h1, S["mix_norm"], W["w_in"])
    g_mix["w_in"] = jnp.concatenate([
        matmul_tn(dq, u, "dw_in_q", tm=AW, tn=D),
        matmul_tn(dkv, u, "dw_in_kv", tm=2 * KVW, tn=D),
        matmul_tn(dz, u, "dw_in_z", tm=PW, tn=D),
        matmul_tn(dgt, u, "dw_in_g", tm=D, tn=512, tt=4096),
    ], axis=0)
    mix = tuple(g_mix)
    token = comm.grads(g_mix)

    g_dn1 = matmul_tn(hid1, dh1, "ffn1_dw_down", tm=1408, tn=512, tt=4096, b_scale=0.5, after=token)
    token = comm.advance(mix, g_dn1)
    token2 = comm.grads({"ffn1_w_down": g_dn1})
    dx, dup1, n1, G["ffn1_norm"], *outs = ffn_bwd_x(dh1, x, ab1, S["ffn1_norm"], w_up1, W["ffn1_w_down"], "ffn1_bwd_x",
                                                   updates=comm.reduce(ffn2, token2), after=(token, token2))
    comm.updated(ffn2, outs)
    token = comm.advance(("ffn1_w_down",), dx)
    token2 = comm.small_start(G, loss, dx)
    g_up1 = matmul_tn(n1, dup1, "ffn1_dw_up", tm=D, tn=512, tt=4096, after=(token, token2))
    token = comm.grads({"ffn1_w_up": g_up1})
    updates = comm.reduce(("ffn1_w_down",), token)
    token = comm.advance(("ffn1_w_up",), [u[1] for u in updates])
    done = comm.update(("ffn1_w_down",), updates, token)
    done = comm.finish(mix, done)
    done = comm.small_finish(done)
    comm.finish(("ffn1_w_up",), done)
    return loss, dx


HBM_SPEC = pl.BlockSpec(memory_space=pltpu.HBM)


def _me():
    return lax.axis_index("x"), lax.axis_index("y"), lax.axis_index("c")


def _peer_chip(x, y, k):
    return x ^ (k >> 1), y ^ (k & 1)


def _piece_half(ref, rowlike, j, h):
    if rowlike:
        return ref.at[j, h]
    ns = ref.shape[-1] // N_CHIPS
    return ref.at[h, :, pl.ds(pl.multiple_of(j * ns, 128), ns)]


def _piece(ref, rowlike, j):
    if rowlike:
        return ref.at[j]
    ns = ref.shape[-1] // N_CHIPS
    return ref.at[:, :, pl.ds(pl.multiple_of(j * ns, 128), ns)]


def _full_shape(shard_view, rowlike):
    _, kh, ns = shard_view.shape
    return (N_CHIPS, 2, kh, ns) if rowlike else (2, kh, N_CHIPS * ns)


def _remote(src, dst, send_sem, recv_sem, dev):
    return pltpu.make_async_remote_copy(src, dst, send_sem, recv_sem, device_id=dev, device_id_type=MESH)


SEM_SPEC = pl.BlockSpec(memory_space=pltpu.SEMAPHORE)
SPLIT_PARAMS = pltpu.CompilerParams(has_side_effects=pltpu.SideEffectType.DATAFLOW_SIDE_EFFECTING)


def _gather_plan(rowlikes):
    def plan(s_refs, f_refs, a, k, x, y, c):
        if k == 0:
            own = _piece(f_refs[a], rowlikes[a], 2 * x + y)
            return s_refs[a], own, own, (x, y, 1 - c)
        px, py = _peer_chip(x, y, k)
        return (s_refs[a].at[c], _piece_half(f_refs[a], rowlikes[a], 2 * x + y, c),
                _piece_half(f_refs[a], rowlikes[a], 2 * px + py, c), (px, py, c))
    return plan


def _all_to_all_plan(rowlikes):
    def plan(q_refs, r_refs, a, k, x, y, c):
        px, py = _peer_chip(x, y, k)
        if rowlikes[a]:
            src = q_refs[a].at[2 * px + py]
        else:
            ns = q_refs[a].shape[-1] // N_CHIPS
            src = q_refs[a].at[:, pl.ds(pl.multiple_of((2 * px + py) * ns, 128), ns)]
        return src, r_refs[a].at[k - 1], r_refs[a].at[k - 1], (px, py, c)
    return plan


def _swap_plan(rowlikes):
    def plan(g_refs, got_refs, a, k, x, y, c):
        src = g_refs[a].at[:, 1 - c] if rowlikes[a] else g_refs[a].at[1 - c]
        return src, got_refs[a], got_refs[a], (x, y, 1 - c)
    return plan


def _everyone_plan(s_refs, slot_refs, a, k, x, y, c):
    px, py, pc = x ^ (k >> 2), y ^ ((k >> 1) & 1), c ^ (k & 1)
    return s_refs[a], slot_refs[a].at[4 * x + 2 * y + c], slot_refs[a].at[4 * px + 2 * py + pc], (px, py, pc)


def _forward_plan(rowlikes):
    def plan(s_refs, f_refs, a, k, x, y, c):
        px, py = _peer_chip(x, y, k)
        mine = _piece_half(f_refs[a], rowlikes[a], 2 * px + py, c)
        return mine, mine, _piece_half(f_refs[a], rowlikes[a], 2 * px + py, 1 - c), (x, y, 1 - c)
    return plan


CHIPS, SIBLING, EVERYONE, GATHERS = (1, 2, 3), (1,), tuple(range(1, 8)), (0, 1, 2, 3)


def _as_list(after):
    return [] if after is None else list(after) if isinstance(after, (list, tuple)) else [after]


def exchange_start(srcs, land_shapes, plan, after, name, peers=CHIPS):
    n = len(srcs)
    lands = [l if hasattr(l, "dtype") else lax.empty(l, s.dtype) for l, s in zip(land_shapes, srcs)]

    behind = _as_list(after)

    def body(*refs):
        s_refs, l_refs = refs[:n], refs[n:2 * n]
        send_sems, recv_sems = refs[2 * n + len(behind)], refs[2 * n + len(behind) + 1]
        token = refs[-1]
        x, y, c = _me()
        for a in range(n):
            for i, k in enumerate(peers):
                src, dst, _, peer = plan(s_refs, l_refs, a, k, x, y, c)
                sem = len(peers) * a + i
                _remote(src, dst, send_sems.at[sem], recv_sems.at[sem], peer).start()
        token[...] = jnp.zeros_like(token)

    n_sems = len(peers) * n
    outs = pl.pallas_call(
        body, name=name, in_specs=[HBM_SPEC] * (2 * n) + [pl.BlockSpec(memory_space=pl.ANY)] * len(behind),
        out_specs=[SEM_SPEC, SEM_SPEC] + [HBM_SPEC] * (2 * n) + [pl.BlockSpec(memory_space=pltpu.VMEM)],
        out_shape=[pltpu.SemaphoreType.DMA((n_sems,)), pltpu.SemaphoreType.DMA((n_sems,))]
        + [pltpu.HBM(a.shape, a.dtype) for a in (*srcs, *lands)] + [SDS((8, 128), F32)],
        input_output_aliases={i: 2 + i for i in range(2 * n)},
        compiler_params=SPLIT_PARAMS,
    )(*[pltpu.with_memory_space_constraint(a, pltpu.HBM) for a in (*srcs, *lands)], *behind)
    return {"sems": outs[:2], "srcs": outs[2:2 + n], "lands": outs[2 + n:2 + 2 * n], "token": outs[-1]}


def exchange_wait(state, plan, after, name, peers=CHIPS):
    n = len(state["srcs"])
    behind = _as_list(after)

    def body(*refs):
        s_refs, l_refs = refs[:n], refs[n:2 * n]
        send_sems, recv_sems = refs[2 * n], refs[2 * n + 1]
        x, y, c = _me()
        for a in range(n):
            for i, k in enumerate(peers):
                src, _, landing, peer = plan(s_refs, l_refs, a, k, x, y, c)
                sem = len(peers) * a + i
                cp = _remote(src, landing, send_sems.at[sem], recv_sems.at[sem], peer)
                cp.wait_send()
                cp.wait_recv()

    bufs = (*state["srcs"], *state["lands"])
    outs = pl.pallas_call(
        body, name=name,
        in_specs=[HBM_SPEC] * (2 * n) + [SEM_SPEC, SEM_SPEC] + [pl.BlockSpec(memory_space=pl.ANY)] * len(behind),
        out_specs=[HBM_SPEC] * (2 * n),
        out_shape=[pltpu.HBM(a.shape, a.dtype) for a in bufs],
        input_output_aliases={i: i for i in range(2 * n)},
        compiler_params=SPLIT_PARAMS,
    )(*bufs, *state["sems"], *behind)
    return outs[:n], outs[n:]


def gather_finish(fulls, rowlikes, name):
    n = len(fulls)

    def body(*refs):
        f_refs = refs[n:2 * n]
        send_sems, recv_sems = refs[2 * n:]
        x, y, c = _me()
        sib = (x, y, 1 - c)
        sends = []
        for a in range(n):
            for k in CHIPS:
                px, py = _peer_chip(x, y, k)
                slot = _piece_half(f_refs[a], rowlikes[a], 2 * px + py, c)
                cp = _remote(slot, slot, send_sems.at[a, k - 1], recv_sems.at[a, k - 1], sib)
                cp.start()
                sends.append(cp)
        for a in range(n):
            for k in CHIPS:
                px, py = _peer_chip(x, y, k)
                slot = _piece_half(f_refs[a], rowlikes[a], 2 * px + py, 1 - c)
                _remote(slot, slot, send_sems.at[a, k - 1], recv_sems.at[a, k - 1], sib).wait_recv()
        for cp in sends:
            cp.wait_send()

    return pl.pallas_call(
        body, name=name, in_specs=[HBM_SPEC] * n, out_specs=[HBM_SPEC] * n,
        out_shape=[SDS(f.shape, f.dtype) for f in fulls],
        input_output_aliases={a: a for a in range(n)},
        scratch_shapes=[pltpu.SemaphoreType.DMA((n, len(CHIPS))), pltpu.SemaphoreType.DMA((n, len(CHIPS)))],
    )(*fulls)


def _half_buffer_shape(gview, rowlike):
    return (N_CHIPS,) + gview.shape[2:] if rowlike else gview.shape[1:]


def _row_tiles(kh, ns):
    return 1 if kh * ns <= 256 * 1024 else 2


def _run_jobs(jobs, place, name, after=None):
    n_steps = max(job[0] for job in jobs)

    def spec(block, index, steps):
        return pl.BlockSpec(block, lambda s, p: index(jnp.minimum(s, steps - 1), p))

    in_specs = [spec(b, ix, steps) for steps, ins, _, _ in jobs for _, b, ix in ins]
    out_specs = [spec(b, ix, steps) for steps, _, outs, _ in jobs for _, _, b, ix in outs]
    token_spec, token_arg = _token_operand(after)
    n_in = len(in_specs) + len(token_spec)

    def body(place_ref, *refs):
        s = pl.program_id(0)
        i, o = 0, n_in
        for steps, ins, outs, fn in jobs:
            in_refs, out_refs = refs[i:i + len(ins)], refs[o:o + len(outs)]
            i, o = i + len(ins), o + len(outs)

            @pl.when(s < steps)
            def _(fn=fn, in_refs=in_refs, out_refs=out_refs):
                fn(in_refs, out_refs)

    return pl.pallas_call(
        body, name=name,
        grid_spec=pltpu.PrefetchScalarGridSpec(num_scalar_prefetch=1, grid=(n_steps,),
                                               in_specs=in_specs + token_spec, out_specs=out_specs),
        out_shape=[SDS(shape, dtype) for _, _, outs, _ in jobs for shape, dtype, _, _ in outs],
        compiler_params=_cp(("arbitrary",)),
    )(place, *[a for _, ins, _, _ in jobs for a, _, _ in ins], *token_arg)


def add_halves(gviews, gots, rowlikes, place, name):
    def add(ins, outs):
        outs[0][...] = (ins[0][...].astype(F32) + ins[1][...].astype(F32)).astype(BF16)

    jobs = []
    for g, got, rowlike in zip(gviews, gots, rowlikes):
        kh, ns = (g.shape[2], g.shape[3]) if rowlike else (g.shape[1], g.shape[2] // N_CHIPS)
        rt = _row_tiles(kh, ns)
        tr = kh // rt
        if rowlike:
            g_in = (g, (None, None, tr, ns), lambda ls, p, rt=rt: (ls // rt, p[0], ls % rt, 0))
            half = ((None, tr, ns), lambda ls, p, rt=rt: (ls // rt, ls % rt, 0))
        else:
            g_in = (g, (None, tr, ns), lambda ls, p, rt=rt: (p[0], ls % rt, ls // rt))
            half = ((tr, ns), lambda ls, p, rt=rt: (ls % rt, ls // rt))
        jobs.append((N_CHIPS * rt, [g_in, (got, *half)], [(got.shape, BF16, *half)], add))
    return _run_jobs(jobs, place, name)


def _slot_shape(q, rowlike):
    return (3,) + (q.shape[1:] if rowlike else (q.shape[0], q.shape[1] // N_CHIPS))


def sum_pieces(qs, recvs, rowlikes, place, name):
    def total(ins, outs):
        acc = ins[0][...].astype(F32)
        for k in range(3):
            acc = acc + ins[1][k].astype(F32)
        outs[0][...] = acc

    jobs = []
    for q, recv, rowlike in zip(qs, recvs, rowlikes):
        kh, ns = recv.shape[1], recv.shape[2]
        rt = _row_tiles(kh, ns)
        tr = kh // rt
        mine = ((None, tr, ns), lambda ls, p: (p[1], ls, 0)) if rowlike else ((tr, ns), lambda ls, p: (ls, p[1]))
        jobs.append((rt, [(q, *mine), (recv, (3, tr, ns), lambda ls, p: (0, ls, 0))],
                     [((2, kh, ns), F32, (None, tr, ns), lambda ls, p: (p[0], ls, 0))], total))
    return _run_jobs(jobs, place, name)


def join_halves(halves, name):
    n = len(halves)

    def body(*refs):
        o_refs = refs[n:2 * n]
        send_sems, recv_sems = refs[2 * n:]
        x, y, c = _me()
        sib = (x, y, 1 - c)
        sends = []
        for a in range(n):
            cp = _remote(o_refs[a].at[c], o_refs[a].at[c], send_sems.at[a], recv_sems.at[a], sib)
            cp.start()
            sends.append(cp)
        for a in range(n):
            got = o_refs[a].at[1 - c]
            _remote(got, got, send_sems.at[a], recv_sems.at[a], sib).wait_recv()
        for cp in sends:
            cp.wait_send()

    return pl.pallas_call(
        body, name=name, in_specs=[HBM_SPEC] * n, out_specs=[HBM_SPEC] * n,
        out_shape=[SDS(h.shape, h.dtype) for h in halves],
        input_output_aliases={a: a for a in range(n)},
        scratch_shapes=[pltpu.SemaphoreType.DMA((n,)), pltpu.SemaphoreType.DMA((n,))],
    )(*halves)


N_DEV = 8


def sum_devices(s, slots, me):
    R, Wd = s.shape

    def body(me_ref, s_ref, slots_ref, out_ref):
        acc = None
        for d in range(N_DEV):
            mine = me_ref[0] == d
            term = jnp.where(mine, s_ref[...], slots_ref[jnp.where(mine, d ^ 1, d)])
            acc = term if acc is None else acc + term
        out_ref[...] = acc

    vmem = pl.BlockSpec(memory_space=pltpu.VMEM)
    return pl.pallas_call(
        body, name="sum_devices", in_specs=[pl.BlockSpec(memory_space=pltpu.SMEM), vmem, vmem], out_specs=vmem,
        out_shape=SDS((R, Wd), F32),
    )(me, s, slots)


TRANSPOSED = ("w_in",)
BIG = {"ffn1_w_up": (D, 2 * FF, "col"), "ffn1_w_down": (FF, D, "row"), "w_in": (INW, D, "row"),
       "w_attn_up": (AW, D, "row"), "w_pool_up": (PW, D, "col"), "w_out": (D, D, "row"),
       "ffn2_w_up": (D, 2 * FF, "col"), "ffn2_w_down": (FF, D, "row")}
GROUPS = (("ffn1_w_up",), ("ffn1_w_down", "w_in", "w_attn_up", "w_pool_up", "w_out"), ("ffn2_w_up", "ffn2_w_down"))
SMALL = ("ffn1_norm", "mix_norm", "ffn2_norm", "final_norm", "pool_scale", "sinks", "pool_w_mix")
SMALL_W = 128


def _rowlike(name):
    return BIG[name][2] == "row"


def _half_dims(name):
    k, n, kind = BIG[name]
    return (k // N_CHIPS // 2, n) if kind == "row" else (k // 2, n // N_CHIPS)


def shard_view(name, shard):
    return shard.reshape((2,) + _half_dims(name))


def full_from_view(name, fv):
    k, n, _ = BIG[name]
    return fv.reshape(k, n)


def grad_view(name, g):
    kh, ns = _half_dims(name)
    return g.reshape(_full_shape(jax.ShapeDtypeStruct((2, kh, ns), g.dtype), _rowlike(name)))


def pack_small(d):
    parts = []
    for name in SMALL:
        a = d[name].reshape(-1)
        pad = (-a.shape[0]) % SMALL_W
        parts.append(jnp.pad(a, (0, pad)).reshape(-1, SMALL_W))
    a = jnp.concatenate(parts, axis=0)
    return jnp.pad(a, ((0, (-a.shape[0]) % 8), (0, 0)))


def unpack_small(a, like):
    out, r0 = {}, 0
    for name in SMALL:
        size = int(np.prod(like[name].shape))
        rows = -(-size // SMALL_W)
        out[name] = a[r0:r0 + rows].reshape(-1)[:size].reshape(like[name].shape)
        r0 += rows
    return out


WEIGHTS = ("ffn1_norm", "ffn1_w_up", "ffn1_w_down", "mix_norm", "w_in", "sinks", "w_attn_up", "pool_w_mix",
           "pool_scale", "w_pool_up", "w_out", "ffn2_norm", "ffn2_w_up", "ffn2_w_down", "final_norm")


def kernel(x, ffn1_norm, ffn1_w_up, ffn1_w_down, mix_norm, w_in, sinks, w_attn_up, pool_w_mix, pool_scale, w_pool_up, w_out, ffn2_norm, ffn2_w_up, ffn2_w_down, final_norm, loss_target, m_ffn1_norm, m_ffn1_w_up, m_ffn1_w_down, m_mix_norm, m_w_in, m_sinks, m_w_attn_up, m_pool_w_mix, m_pool_scale, m_w_pool_up, m_w_out, m_ffn2_norm, m_ffn2_w_up, m_ffn2_w_down, m_final_norm, v_ffn1_norm, v_ffn1_w_up, v_ffn1_w_down, v_mix_norm, v_w_in, v_sinks, v_w_attn_up, v_pool_w_mix, v_pool_scale, v_w_pool_up, v_w_out, v_ffn2_norm, v_ffn2_w_up, v_ffn2_w_down, v_final_norm):
    given = dict(locals())
    w = {n: given[n] for n in WEIGHTS}
    m = {n: given["m_" + n] for n in WEIGHTS}
    v = {n: given["v_" + n] for n in WEIGHTS}
    cx, cy, cc = _me()
    place = jnp.stack([cc, 2 * cx + cy]).astype(jnp.int32)

    def local2d(d, n):
        return d[n][0].T if n in TRANSPOSED else d[n][0]

    shards = {n: local2d(w, n) for n in BIG}
    grads, delta, new_m, new_v = {}, {}, {}, {}
    rowlikes = [[_rowlike(n) for n in names] for names in GROUPS]

    class Exchanges:
        def __init__(self):
            self.gathers, self.forwards, self.fulls, self.reductions, self.small = {}, {}, {}, {}, None

        def _behind_first(self, a):
            return a + self.gathers[0]["token"][0, 0]

        def _start_gather(self, group, after):
            sv = [self.cast[n] if group else shard_view(n, shards[n].astype(BF16)) for n in GROUPS[group]]
            rl = rowlikes[group]
            self.gathers[group] = exchange_start(sv, [_full_shape(s, r) for s, r in zip(sv, rl)], _gather_plan(rl),
                                                 after, f"gather_start_{group}", GATHERS)

        def weights(self, group, after):
            if not self.gathers:
                self._start_gather(0, None)
                self.cast = {n: shard_view(n, self._behind_first(shards[n]).astype(BF16))
                             for names in GROUPS[1:] for n in names}
                after = [self.gathers[0]["token"], *self.cast.values()]
            if group not in self.fulls:
                names, rl, state = GROUPS[group], rowlikes[group], self.gathers[group]
                if group in self.forwards:
                    _, fulls = exchange_wait(self.forwards.pop(group), _forward_plan(rl), after,
                                             f"forward_wait_{group}")
                else:
                    _, fulls = exchange_wait(state, _gather_plan(rl), state["token"] if after is None else after,
                                             f"gather_wait_{group}", GATHERS)
                    fulls = gather_finish(fulls, rl, f"gather_finish_{group}")
                self.fulls[group] = {n: full_from_view(n, f) for n, f in zip(names, fulls)}
                if group + 1 < len(GROUPS):
                    self._start_gather(group + 1, fulls[0])
            return self.fulls[group]

        def started(self):
            return self.gathers[max(self.gathers)]["token"]

        def prefetch(self, group, after):
            rl, state = rowlikes[group], self.gathers[group]
            sv, fulls = exchange_wait(state, _gather_plan(rl), after, f"gather_wait_{group}", GATHERS)
            self.forwards[group] = exchange_start(sv, fulls, _forward_plan(rl), None, f"forward_start_{group}")
            return self.forwards[group]["token"]

        def grads(self, g):
            names = tuple(g)
            rl = [_rowlike(n) for n in names]
            gv = [grad_view(n, g[n]) for n in names]
            state = exchange_start(gv, [_half_buffer_shape(a, r) for a, r in zip(gv, rl)], _swap_plan(rl), None,
                                   "swap_start_" + names[0], SIBLING)
            self.reductions[names] = state
            return state["token"]

        def advance(self, names, after):
            rl = [_rowlike(n) for n in names]
            gv, gots = exchange_wait(self.reductions[names], _swap_plan(rl), after, "swap_wait_" + names[0], SIBLING)
            qs = add_halves(gv, gots, rl, place, "add_halves_" + names[0])
            state = exchange_start(qs, [_slot_shape(q, r) for q, r in zip(qs, rl)], _all_to_all_plan(rl), None,
                                   "all_to_all_start_" + names[0])
            self.reductions[names] = state
            return state["token"]

        def reduce(self, names, after):
            rl = [_rowlike(n) for n in names]
            qs, recvs = exchange_wait(self.reductions.pop(names), _all_to_all_plan(rl), after,
                                      "all_to_all_wait_" + names[0])
            halves = sum_pieces(qs, recvs, rl, place, "sum_pieces_" + names[0])
            return [(shards[n], o.reshape(shards[n].shape), local2d(m, n), local2d(v, n))
                    for n, o in zip(names, join_halves(halves, "join_halves_" + names[0]))]

        def updated(self, names, outs):
            for i, n in enumerate(names):
                grads[n], delta[n], new_m[n], new_v[n] = outs[4 * i:4 * i + 4]
            return [new_v[n] for n in names]

        def update(self, names, updates, token=None):
            return self.updated(names, adamw(updates, place, "adamw_" + names[0], after=token))

        def finish(self, names, after):
            return self.update(names, self.reduce(names, after))

        def small_start(self, G, loss, after):
            packed = pack_small({n: G[n] for n in SMALL})
            used_rows = sum(-(-int(np.prod(small_like[n].shape)) // SMALL_W) for n in SMALL)
            assert packed.shape[0] > used_rows
            packed = packed.at[-1, 0].set(loss[0, 0])
            self.small = exchange_start([packed], [(N_DEV,) + packed.shape], _everyone_plan, after,
                                        "small_start", EVERYONE)
            return self.small["token"]

        def small_finish(self, after):
            (packed,), (slots,) = exchange_wait(self.small, _everyone_plan, after, "small_wait", EVERYONE)
            total = sum_devices(packed, slots, (4 * cx + 2 * cy + cc).astype(jnp.int32).reshape(1))
            self.loss = total[-1, 0]
            rows2d = lambda a: a.reshape(-1, a.shape[-1])
            g = unpack_small(total, small_like)
            outs = adamw([tuple(rows2d(d[n]) for d in (w, g, m, v)) for n in SMALL], place, "adamw_small")
            for i, n in enumerate(SMALL):
                grads[n], delta[n], new_m[n], new_v[n] = (o.reshape(w[n].shape) for o in outs[4 * i:4 * i + 4])
            return new_v["pool_w_mix"]

    small_like = {n: w[n] for n in SMALL}
    S = {n: w[n].reshape(1, -1) for n in ("ffn1_norm", "mix_norm", "ffn2_norm", "final_norm", "pool_scale", "sinks")}
    S["pool_w_mix"] = w["pool_w_mix"][0].astype(BF16)
    exchanges = Exchanges()
    _, dx = local_fwd_bwd(x[0], loss_target[0], S, exchanges)
    loss = exchanges.loss

    def shaped(d, n):
        return (d[n].T if n in TRANSPOSED else d[n]).reshape(w[n].shape)

    return (loss, dx[None], *[shaped(grads, n) for n in WEIGHTS], *[shaped(delta, n) for n in WEIGHTS],
            *[shaped(new_m, n) for n in WEIGHTS], *[shaped(new_v, n) for n in WEIGHTS])
```
